```python
import jax, jax.numpy as jnp
from jax import lax
import numpy as np

D_MODEL = 1024
BATCH = 8
SEQ = 4096
DEPTH = 1

HG_WIDTH = D_MODEL // 2
HG_HEAD_DIM = 128
HG_HEADS = HG_WIDTH // HG_HEAD_DIM
CONV_WIDTH = D_MODEL // 2
CONV_GROUPS = 8
CONV_K = 3
D_FF = ((-(-8 * D_MODEL // 3) + 255) // 256) * 256
CHUNK = 32
EPS = 1e-6
SPLIT_SIZES = (HG_WIDTH, HG_WIDTH, HG_WIDTH, HG_WIDTH, CONV_WIDTH, CONV_WIDTH, CONV_WIDTH, D_MODEL, D_MODEL)
N_IN = 4 * HG_WIDTH + 3 * CONV_WIDTH + 2 * D_MODEL

kernel_name = 'hybrid_hgrn2_shortconv_gated_block'


def rmsnorm(x, g):
    xf = x.astype(jnp.float32)
    y = xf * lax.rsqrt(jnp.mean(xf * xf, axis=-1, keepdims=True) + EPS)
    return (y * g.astype(jnp.float32)).astype(x.dtype)


def hgrn2_chunkwise(q, k, v, log_f):
    B, L, H, dk = q.shape
    dv = v.shape[-1]
    n = L // CHUNK

    def to_chunks(t):
        return t.reshape(B, n, CHUNK, H, t.shape[-1]).transpose(1, 0, 3, 2, 4)

    qc, kc, vc, gc = to_chunks(q), to_chunks(k), to_chunks(v), to_chunks(log_f)
    b = jnp.cumsum(gc, axis=3)
    anchor = b[:, :, :, CHUNK // 2 - 1:CHUNK // 2, :]
    q_hat = qc * jnp.exp(b - anchor)
    k_hat = kc * jnp.exp(anchor - b)
    scores = jnp.einsum('nbhid,nbhjd->nbhij', q_hat, k_hat)
    causal = jnp.tril(jnp.ones((CHUNK, CHUNK), dtype=bool))
    scores = jnp.where(causal, scores, 0.0)
    o_intra = jnp.einsum('nbhij,nbhjv->nbhiv', scores, vc)
    b_last = b[:, :, :, -1:, :]
    q_in = qc * jnp.exp(b)
    k_out = kc * jnp.exp(b_last - b)
    chunk_decay = jnp.exp(b_last[:, :, :, 0, :])

    def step(S, inp):
        q_i, k_o, v_c, dec = inp
        o = jnp.einsum('bhid,bhdv->bhiv', q_i, S)
        S = dec[..., None] * S + jnp.einsum('bhjd,bhjv->bhdv', k_o, v_c)
        return S, o

    S0 = jnp.zeros((B, H, dk, dv), jnp.float32)
    _, o_inter = lax.scan(step, S0, (q_in, k_out, vc, chunk_decay))
    o = o_intra + o_inter
    return o.transpose(1, 0, 3, 2, 4).reshape(B, L, H, dv)


def hgrn2_mixer(q_raw, f_raw, i_raw, g_raw, lb, norm_g):
    B, L, _ = q_raw.shape

    def heads(t):
        return t.reshape(B, L, HG_HEADS, HG_HEAD_DIM).astype(jnp.float32)

    q = jax.nn.silu(heads(q_raw)) * (HG_HEAD_DIM ** -0.5)
    lbh = lb.astype(jnp.float32).reshape(HG_HEADS, HG_HEAD_DIM)
    f = lbh + (1.0 - lbh) * jax.nn.sigmoid(heads(f_raw))
    o = hgrn2_chunkwise(q, 1.0 - f, heads(i_raw), jnp.log(f))
    o = rmsnorm(o, norm_g) * jax.nn.silu(heads(g_raw))
    return o.reshape(B, L, HG_WIDTH).astype(q_raw.dtype)


def short_conv_mixer(c_gate, b_gate, xb, conv_w):
    u = c_gate * xb
    rhs = conv_w.astype(u.dtype)[:, None, :]
    y = lax.conv_general_dilated(u, rhs, window_strides=(1,), padding=[(CONV_K - 1, 0)],
                                 dimension_numbers=('NWC', 'WIO', 'NWC'),
                                 feature_group_count=CONV_WIDTH)
    return b_gate * y


def swiglu(h, w_gate, w_up, w_down):
    return (jax.nn.silu(h @ w_gate) * (h @ w_up)) @ w_down


def _fwd_setup_inputs(seed: int = 0) -> dict:
    key = jax.random.key(seed)
    ks = jax.random.split(key, 14)

    def nrm(k, shape, fan):
        return jax.random.normal(k, shape, jnp.float32) * (fan ** -0.5)

    def gain(k, shape):
        return 1.0 + 0.02 * jax.random.normal(k, shape, jnp.float32)

    return {
        'x': jax.random.normal(ks[0], (BATCH, SEQ, D_MODEL), jnp.float32),
        'norm_mix_g': gain(ks[1], (DEPTH, D_MODEL)),
        'w_in': nrm(ks[2], (DEPTH, D_MODEL, N_IN), D_MODEL),
        'lower_bounds': 0.1 * jax.random.normal(ks[3], (DEPTH + 1, HG_WIDTH), jnp.float32),
        'hg_norm_g': gain(ks[4], (DEPTH, HG_HEAD_DIM)),
        'conv_w': nrm(ks[5], (DEPTH, CONV_K, CONV_WIDTH), CONV_K),
        'w_branch_a': nrm(ks[6], (DEPTH, HG_WIDTH, D_MODEL), HG_WIDTH),
        'w_branch_b': nrm(ks[7], (DEPTH, CONV_WIDTH, D_MODEL), CONV_WIDTH),
        'w_out': nrm(ks[8], (DEPTH, D_MODEL, D_MODEL), D_MODEL),
        'norm_ffn_g': gain(ks[9], (DEPTH, D_MODEL)),
        'w_ffn_gate': nrm(ks[10], (DEPTH, D_MODEL, D_FF), D_MODEL),
        'w_ffn_up': nrm(ks[11], (DEPTH, D_MODEL, D_FF), D_MODEL),
        'w_ffn_down': nrm(ks[12], (DEPTH, D_FF, D_MODEL), D_FF),
        'norm_final_g': gain(ks[13], (D_MODEL,)),
    }


def _fwd_reference(x, norm_mix_g, w_in, lower_bounds, hg_norm_g, conv_w, w_branch_a, w_branch_b,
              w_out, norm_ffn_g, w_ffn_gate, w_ffn_up, w_ffn_down, norm_final_g):
    offsets = [int(o) for o in np.cumsum(SPLIT_SIZES)[:-1]]
    lb_all = jnp.cumsum(jax.nn.softmax(lower_bounds.astype(jnp.float32), axis=0), axis=0)
    for l in range(DEPTH):
        h = rmsnorm(x, norm_mix_g[l])
        proj = h @ w_in[l]
        q_raw, f_raw, i_raw, g_raw, c_gate, b_gate, xb, gate_a, gate_b = jnp.split(proj, offsets, axis=-1)
        y_a = hgrn2_mixer(q_raw, f_raw, i_raw, g_raw, lb_all[l], hg_norm_g[l]) @ w_branch_a[l]
        y_b = short_conv_mixer(c_gate, b_gate, xb, conv_w[l]) @ w_branch_b[l]
        merged = jax.nn.sigmoid(gate_a) * y_a + jax.nn.sigmoid(gate_b) * y_b
        x = x + merged @ w_out[l]
        h2 = rmsnorm(x, norm_ffn_g[l])
        x = x + swiglu(h2, w_ffn_gate[l], w_ffn_up[l], w_ffn_down[l])
    return rmsnorm(x, norm_final_g)


import jax as _jax
import jax.numpy as _jnp

TWIN_FORMAT = 'train_step'
FWD_PARAMS = ['x', 'norm_mix_g', 'w_in', 'lower_bounds', 'hg_norm_g', 'conv_w', 'w_branch_a', 'w_branch_b', 'w_out', 'norm_ffn_g', 'w_ffn_gate', 'w_ffn_up', 'w_ffn_down', 'norm_final_g']
TWIN_WEIGHTS = ['norm_mix_g', 'w_in', 'lower_bounds', 'hg_norm_g', 'conv_w', 'w_branch_a', 'w_branch_b', 'w_out', 'norm_ffn_g', 'w_ffn_gate', 'w_ffn_up', 'w_ffn_down', 'norm_final_g']
TWIN_DIFF_INPUT = 'x'
TWIN_INPUTS = ['x', 'norm_mix_g', 'w_in', 'lower_bounds', 'hg_norm_g', 'conv_w', 'w_branch_a', 'w_branch_b', 'w_out', 'norm_ffn_g', 'w_ffn_gate', 'w_ffn_up', 'w_ffn_down', 'norm_final_g', 'loss_target', 'm_norm_mix_g', 'm_w_in', 'm_lower_bounds', 'm_hg_norm_g', 'm_conv_w', 'm_w_branch_a', 'm_w_branch_b', 'm_w_out', 'm_norm_ffn_g', 'm_w_ffn_gate', 'm_w_ffn_up', 'm_w_ffn_down', 'm_norm_final_g', 'v_norm_mix_g', 'v_w_in', 'v_lower_bounds', 'v_hg_norm_g', 'v_conv_w', 'v_w_branch_a', 'v_w_branch_b', 'v_w_out', 'v_norm_ffn_g', 'v_w_ffn_gate', 'v_w_ffn_up', 'v_w_ffn_down', 'v_norm_final_g']
TWIN_OUTPUTS = ['loss', 'grad_x', 'grad_norm_mix_g', 'grad_w_in', 'grad_lower_bounds', 'grad_hg_norm_g', 'grad_conv_w', 'grad_w_branch_a', 'grad_w_branch_b', 'grad_w_out', 'grad_norm_ffn_g', 'grad_w_ffn_gate', 'grad_w_ffn_up', 'grad_w_ffn_down', 'grad_norm_final_g', 'delta_norm_mix_g', 'delta_w_in', 'delta_lower_bounds', 'delta_hg_norm_g', 'delta_conv_w', 'delta_w_branch_a', 'delta_w_branch_b', 'delta_w_out', 'delta_norm_ffn_g', 'delta_w_ffn_gate', 'delta_w_ffn_up', 'delta_w_ffn_down', 'delta_norm_final_g', 'new_m_norm_mix_g', 'new_m_w_in', 'new_m_lower_bounds', 'new_m_hg_norm_g', 'new_m_conv_w', 'new_m_w_branch_a', 'new_m_w_branch_b', 'new_m_w_out', 'new_m_norm_ffn_g', 'new_m_w_ffn_gate', 'new_m_w_ffn_up', 'new_m_w_ffn_down', 'new_m_norm_final_g', 'new_v_norm_mix_g', 'new_v_w_in', 'new_v_lower_bounds', 'new_v_hg_norm_g', 'new_v_conv_w', 'new_v_w_branch_a', 'new_v_w_branch_b', 'new_v_w_out', 'new_v_norm_ffn_g', 'new_v_w_ffn_gate', 'new_v_w_ffn_up', 'new_v_w_ffn_down', 'new_v_norm_final_g']
TWIN_LEAF_KINDS = {'loss': 'loss', 'grad_x': 'grad_x', 'grad_norm_mix_g': 'grad_w', 'grad_w_in': 'grad_w', 'grad_lower_bounds': 'grad_w', 'grad_hg_norm_g': 'grad_w', 'grad_conv_w': 'grad_w', 'grad_w_branch_a': 'grad_w', 'grad_w_branch_b': 'grad_w', 'grad_w_out': 'grad_w', 'grad_norm_ffn_g': 'grad_w', 'grad_w_ffn_gate': 'grad_w', 'grad_w_ffn_up': 'grad_w', 'grad_w_ffn_down': 'grad_w', 'grad_norm_final_g': 'grad_w', 'delta_norm_mix_g': 'delta_w', 'delta_w_in': 'delta_w', 'delta_lower_bounds': 'delta_w', 'delta_hg_norm_g': 'delta_w', 'delta_conv_w': 'delta_w', 'delta_w_branch_a': 'delta_w', 'delta_w_branch_b': 'delta_w', 'delta_w_out': 'delta_w', 'delta_norm_ffn_g': 'delta_w', 'delta_w_ffn_gate': 'delta_w', 'delta_w_ffn_up': 'delta_w', 'delta_w_ffn_down': 'delta_w', 'delta_norm_final_g': 'delta_w', 'new_m_norm_mix_g': 'new_m', 'new_m_w_in': 'new_m', 'new_m_lower_bounds': 'new_m', 'new_m_hg_norm_g': 'new_m', 'new_m_conv_w': 'new_m', 'new_m_w_branch_a': 'new_m', 'new_m_w_branch_b': 'new_m', 'new_m_w_out': 'new_m', 'new_m_norm_ffn_g': 'new_m', 'new_m_w_ffn_gate': 'new_m', 'new_m_w_ffn_up': 'new_m', 'new_m_w_ffn_down': 'new_m', 'new_m_norm_final_g': 'new_m', 'new_v_norm_mix_g': 'new_v', 'new_v_w_in': 'new_v', 'new_v_lower_bounds': 'new_v', 'new_v_hg_norm_g': 'new_v', 'new_v_conv_w': 'new_v', 'new_v_w_branch_a': 'new_v', 'new_v_w_branch_b': 'new_v', 'new_v_w_out': 'new_v', 'new_v_norm_ffn_g': 'new_v', 'new_v_w_ffn_gate': 'new_v', 'new_v_w_ffn_up': 'new_v', 'new_v_w_ffn_down': 'new_v', 'new_v_norm_final_g': 'new_v'}


def _forward(args):
    return _fwd_reference(*[args[k] for k in FWD_PARAMS])


def _output_shape():
    def fwd():
        inp = _fwd_setup_inputs(0)
        return _fwd_reference(*[inp[k] for k in FWD_PARAMS])
    out = _jax.eval_shape(fwd)
    return out.shape, out.dtype

N_MICROBATCH = 1
ADAM_LR = 0.001
ADAM_B1 = 0.9
ADAM_B2 = 0.999
ADAM_EPS = 1e-08
ADAM_WD = 0.01
ADAM_STEP = 10
PER_EXAMPLE_BATCH_AXIS = {'x': 0, 'loss_target': 0}
SHARED_INPUTS = []
_WEIGHT_DTYPES = {'norm_mix_g': _jnp.float32, 'w_in': _jnp.float32, 'lower_bounds': _jnp.float32, 'hg_norm_g': _jnp.float32, 'conv_w': _jnp.float32, 'w_branch_a': _jnp.float32, 'w_branch_b': _jnp.float32, 'w_out': _jnp.float32, 'norm_ffn_g': _jnp.float32, 'w_ffn_gate': _jnp.float32, 'w_ffn_up': _jnp.float32, 'w_ffn_down': _jnp.float32, 'norm_final_g': _jnp.float32}
MOMENT_SCALE = {'norm_mix_g': 2.000845e-01, 'w_in': 7.642050e-02, 'lower_bounds': 7.424871e-03, 'hg_norm_g': 1.517245e-01, 'conv_w': 1.236545e-01, 'w_branch_a': 5.320222e-02, 'w_branch_b': 9.077112e-02, 'w_out': 1.052422e-01, 'norm_ffn_g': 1.194553e-01, 'w_ffn_gate': 4.958688e-02, 'w_ffn_up': 4.802350e-02, 'w_ffn_down': 7.965186e-02, 'norm_final_g': 3.200549e+01}


def _to_microbatches(a, axis):
    t = _jnp.moveaxis(a, axis, 0)
    t = t.reshape((N_MICROBATCH, t.shape[0] // N_MICROBATCH) + t.shape[1:])
    return _jnp.moveaxis(t, 1, axis + 1)


def setup_inputs(seed: int = 0) -> dict:
    inp = _fwd_setup_inputs(seed)
    key = _jax.random.fold_in(_jax.random.key(seed), 7919)
    shape, _ = _output_shape()
    out = dict(inp)
    out["loss_target"] = _jax.random.normal(_jax.random.fold_in(key, 0), shape, _jnp.float32)
    for i, name in enumerate(TWIN_WEIGHTS):
        w = inp[name].astype(_jnp.float32)
        if MOMENT_SCALE is None:
            s = _jnp.sqrt(_jnp.mean(_jnp.square(w)) + 1e-30)
        else:
            s = MOMENT_SCALE[name]
        km, kv = _jax.random.split(_jax.random.fold_in(key, i + 1))
        out[name] = w
        out["m_" + name] = s * _jax.random.normal(km, w.shape, _jnp.float32)
        out["v_" + name] = (s * s) * _jax.random.uniform(kv, w.shape, _jnp.float32, 0.5, 1.5)
    if N_MICROBATCH > 1:
        for name, axis in PER_EXAMPLE_BATCH_AXIS.items():
            out[name] = _to_microbatches(out[name], axis)
    return {'x': out['x'], 'norm_mix_g': out['norm_mix_g'], 'w_in': out['w_in'], 'lower_bounds': out['lower_bounds'], 'hg_norm_g': out['hg_norm_g'], 'conv_w': out['conv_w'], 'w_branch_a': out['w_branch_a'], 'w_branch_b': out['w_branch_b'], 'w_out': out['w_out'], 'norm_ffn_g': out['norm_ffn_g'], 'w_ffn_gate': out['w_ffn_gate'], 'w_ffn_up': out['w_ffn_up'], 'w_ffn_down': out['w_ffn_down'], 'norm_final_g': out['norm_final_g'], 'loss_target': out['loss_target'], 'm_norm_mix_g': out['m_norm_mix_g'], 'm_w_in': out['m_w_in'], 'm_lower_bounds': out['m_lower_bounds'], 'm_hg_norm_g': out['m_hg_norm_g'], 'm_conv_w': out['m_conv_w'], 'm_w_branch_a': out['m_w_branch_a'], 'm_w_branch_b': out['m_w_branch_b'], 'm_w_out': out['m_w_out'], 'm_norm_ffn_g': out['m_norm_ffn_g'], 'm_w_ffn_gate': out['m_w_ffn_gate'], 'm_w_ffn_up': out['m_w_ffn_up'], 'm_w_ffn_down': out['m_w_ffn_down'], 'm_norm_final_g': out['m_norm_final_g'], 'v_norm_mix_g': out['v_norm_mix_g'], 'v_w_in': out['v_w_in'], 'v_lower_bounds': out['v_lower_bounds'], 'v_hg_norm_g': out['v_hg_norm_g'], 'v_conv_w': out['v_conv_w'], 'v_w_branch_a': out['v_w_branch_a'], 'v_w_branch_b': out['v_w_branch_b'], 'v_w_out': out['v_w_out'], 'v_norm_ffn_g': out['v_norm_ffn_g'], 'v_w_ffn_gate': out['v_w_ffn_gate'], 'v_w_ffn_up': out['v_w_ffn_up'], 'v_w_ffn_down': out['v_w_ffn_down'], 'v_norm_final_g': out['v_norm_final_g']}


def _loss(weights, diff, rest, loss_target):
    with _jax.named_scope("forward"):
        args = {**rest, TWIN_DIFF_INPUT: diff, **{k: w.astype(_WEIGHT_DTYPES[k]) for k, w in weights.items()}}
        y = _forward(args)
    with _jax.named_scope("loss_head"):
        err = _jnp.square(y.astype(_jnp.float32) - loss_target)
        return 0.5 * _jnp.sum(_jnp.mean(err, axis=-1)) if err.ndim else 0.5 * err


def _adamw(w, g, m, v):
    m = ADAM_B1 * m + (1.0 - ADAM_B1) * g
    v = ADAM_B2 * v + (1.0 - ADAM_B2) * _jnp.square(g)
    m_hat = m / (1.0 - ADAM_B1 ** ADAM_STEP)
    v_hat = v / (1.0 - ADAM_B2 ** ADAM_STEP)
    delta = -ADAM_LR * (m_hat / (_jnp.sqrt(v_hat) + ADAM_EPS) + ADAM_WD * w)
    return delta, m, v


def reference(x, norm_mix_g, w_in, lower_bounds, hg_norm_g, conv_w, w_branch_a, w_branch_b, w_out, norm_ffn_g, w_ffn_gate, w_ffn_up, w_ffn_down, norm_final_g, loss_target, m_norm_mix_g, m_w_in, m_lower_bounds, m_hg_norm_g, m_conv_w, m_w_branch_a, m_w_branch_b, m_w_out, m_norm_ffn_g, m_w_ffn_gate, m_w_ffn_up, m_w_ffn_down, m_norm_final_g, v_norm_mix_g, v_w_in, v_lower_bounds, v_hg_norm_g, v_conv_w, v_w_branch_a, v_w_branch_b, v_w_out, v_norm_ffn_g, v_w_ffn_gate, v_w_ffn_up, v_w_ffn_down, v_norm_final_g):
    given = dict(x=x, norm_mix_g=norm_mix_g, w_in=w_in, lower_bounds=lower_bounds, hg_norm_g=hg_norm_g, conv_w=conv_w, w_branch_a=w_branch_a, w_branch_b=w_branch_b, w_out=w_out, norm_ffn_g=norm_ffn_g, w_ffn_gate=w_ffn_gate, w_ffn_up=w_ffn_up, w_ffn_down=w_ffn_down, norm_final_g=norm_final_g, loss_target=loss_target, m_norm_mix_g=m_norm_mix_g, m_w_in=m_w_in, m_lower_bounds=m_lower_bounds, m_hg_norm_g=m_hg_norm_g, m_conv_w=m_conv_w, m_w_branch_a=m_w_branch_a, m_w_branch_b=m_w_branch_b, m_w_out=m_w_out, m_norm_ffn_g=m_norm_ffn_g, m_w_ffn_gate=m_w_ffn_gate, m_w_ffn_up=m_w_ffn_up, m_w_ffn_down=m_w_ffn_down, m_norm_final_g=m_norm_final_g, v_norm_mix_g=v_norm_mix_g, v_w_in=v_w_in, v_lower_bounds=v_lower_bounds, v_hg_norm_g=v_hg_norm_g, v_conv_w=v_conv_w, v_w_branch_a=v_w_branch_a, v_w_branch_b=v_w_branch_b, v_w_out=v_w_out, v_norm_ffn_g=v_norm_ffn_g, v_w_ffn_gate=v_w_ffn_gate, v_w_ffn_up=v_w_ffn_up, v_w_ffn_down=v_w_ffn_down, v_norm_final_g=v_norm_final_g)
    weights = {n: given[n] for n in TWIN_WEIGHTS}
    shared = {n: given[n] for n in SHARED_INPUTS}
    per_example = {n: given[n] for n in ['x']}
    grad_fn = _jax.value_and_grad(_loss, argnums=(0, 1))

    def one_microbatch(ex, loss_target):
        ex = dict(ex)
        diff = ex.pop(TWIN_DIFF_INPUT)
        return grad_fn(weights, diff, {**shared, **ex}, loss_target)

    if N_MICROBATCH == 1:
        loss, (grad_w, grad_x) = one_microbatch(per_example, given["loss_target"])
    else:
        def body(carry, xs):
            loss_sum, grad_sum = carry
            l_k, (gw_k, gx_k) = one_microbatch(xs[0], xs[1])
            with _jax.named_scope("update"):
                return (loss_sum + l_k, _jax.tree.map(_jnp.add, grad_sum, gw_k)), gx_k

        init = (_jnp.zeros((), _jnp.float32), _jax.tree.map(_jnp.zeros_like, weights))
        (loss, grad_w), grad_x = _jax.lax.scan(body, init, (per_example, given["loss_target"]))
    with _jax.named_scope("update"):
        delta_w, new_m, new_v = {}, {}, {}
        for n in TWIN_WEIGHTS:
            delta_w[n], new_m[n], new_v[n] = _adamw(weights[n], grad_w[n], given["m_" + n], given["v_" + n])
    return (loss, grad_x, *[grad_w[n] for n in TWIN_WEIGHTS], *[delta_w[n] for n in TWIN_WEIGHTS],
            *[new_m[n] for n in TWIN_WEIGHTS], *[new_v[n] for n in TWIN_WEIGHTS])
```

```python
import functools

import jax
import jax.numpy as jnp
from jax import lax
from jax.experimental import pallas as pl
from jax.experimental.pallas import tpu as pltpu

F32 = jnp.float32
BF16 = jnp.bfloat16
EPS = 1e-6
CHUNK = 32
HEAD_DIM = 128
LANES = 128
N_CHIPS = 4
N_SMALL_ROWS = 16

ADAM_LR = 0.001
ADAM_B1 = 0.9
ADAM_B2 = 0.999
ADAM_EPS = 1e-08
ADAM_WD = 0.01
ADAM_STEP = 10

MESH = pl.DeviceIdType.MESH
ANY = pl.BlockSpec(memory_space=pl.ANY)
VMEM = pl.BlockSpec(memory_space=pltpu.VMEM)


def _sds(shape, dtype):
    return jax.ShapeDtypeStruct(shape, dtype)


def _params(semantics, vmem_mb):
    return pltpu.CompilerParams(dimension_semantics=semantics, vmem_limit_bytes=vmem_mb << 20)


def _nn(a, b):
    return lax.dot_general(a, b, (((1,), (0,)), ((), ())), preferred_element_type=F32)


def _nt(a, b):
    return lax.dot_general(a, b, (((1,), (1,)), ((), ())), preferred_element_type=F32)


def _tn(a, b):
    return lax.dot_general(a, b, (((0,), (0,)), ((), ())), preferred_element_type=F32)


def _sigmoid(x):
    return jax.nn.sigmoid(x)


def _rms_stats(x):
    r = lax.rsqrt(jnp.mean(x * x, axis=-1, keepdims=True) + EPS)
    return r, x * r


def _rms_bwd(dxh, xh, r):
    return r * (dxh - xh * jnp.mean(dxh * xh, axis=-1, keepdims=True))


def _fwd_proj(x, g_mix, w_in3):
    L, D = x.shape
    tn = w_in3.shape[2]
    tm = min(L, 1024)

    def body(x_ref, g_ref, w_ref, h_ref, p_ref):
        @pl.when(pl.program_id(1) == 0)
        def _():
            _, xh = _rms_stats(x_ref[...])
            h_ref[...] = (xh * g_ref[...]).astype(BF16)

        p_ref[...] = _nn(h_ref[...], w_ref[...])

    return pl.pallas_call(
        body, name="fwd_proj", grid=(L // tm, N_CHIPS),
        in_specs=[pl.BlockSpec((tm, D), lambda i, j: (i, 0)),
                  pl.BlockSpec((1, D), lambda i, j: (0, 0)),
                  pl.BlockSpec((None, D, tn), lambda i, j: (j, 0, 0))],
        out_specs=[pl.BlockSpec((tm, D), lambda i, j: (i, 0)),
                   pl.BlockSpec((tm, tn), lambda i, j: (i, j))],
        out_shape=[_sds((L, D), BF16), _sds((L, N_CHIPS * tn), F32)],
        compiler_params=_params(("parallel", "arbitrary"), 48),
    )(x, g_mix, w_in3)


def _lower_bound(lbp):
    l0, l1 = lbp[0:1, :], lbp[1:2, :]
    m = jnp.maximum(l0, l1)
    e0, e1 = jnp.exp(l0 - m), jnp.exp(l1 - m)
    return e0 / (e0 + e1)


def _seg_scan(x, r32, forward):
    n = x.shape[0]
    s = 1
    while s < CHUNK:
        if forward:
            x = x + jnp.where(r32 >= s, pltpu.roll(x, s, 0), 0.0)
        else:
            x = x + jnp.where(r32 < CHUNK - s, pltpu.roll(x, n - s, 0), 0.0)
        s *= 2
    return x


def _bcast_row(x, row):
    n, w = x.shape
    nc = n // CHUNK
    x3 = x.reshape(nc, CHUNK, w)
    return jnp.broadcast_to(x3[:, row:row + 1, :], (nc, CHUNK, w)).reshape(n, w)


def _hgrn_prep(q_raw, f_raw, lb):
    r32 = lax.broadcasted_iota(jnp.int32, f_raw.shape, 0) & (CHUNK - 1)
    sig = _sigmoid(f_raw)
    f = lb + (1.0 - lb) * sig
    b = _seg_scan(jnp.log(f), r32, True)
    a = _bcast_row(b, CHUNK // 2 - 1)
    bl = _bcast_row(b, CHUNK - 1)
    sq = _sigmoid(q_raw)
    q = q_raw * sq * (HEAD_DIM ** -0.5)
    return dict(r32=r32, sig=sig, f=f, k=1.0 - f, b=b, a=a, bl=bl, sq=sq, q=q)


def _chunk_masks(n):
    ri = lax.broadcasted_iota(jnp.int32, (n, n), 0)
    ci = lax.broadcasted_iota(jnp.int32, (n, n), 1)
    same = (ri // CHUNK) == (ci // CHUNK)
    return same & (ci <= ri), same & (ri <= ci)


def _hgrn_fwd(proj, lower_bounds, gamma, H):
    L = proj.shape[0]
    nh = H // HEAD_DIM
    TL = min(L, 256)
    nc = TL // CHUNK

    def body(q_ref, f_ref, v_ref, g_ref, lbp_ref, gam_ref, og_ref, o_ref, s_ref, st_ref):
        @pl.when(pl.program_id(0) == 0)
        def _():
            st_ref[...] = jnp.zeros_like(st_ref)

        lb = _lower_bound(lbp_ref[...])
        gam = gam_ref[...]
        mask, _ = _chunk_masks(TL)
        rowc = lax.broadcasted_iota(jnp.int32, (TL, HEAD_DIM), 0) // CHUNK
        for h in range(nh):
            hs = slice(h * HEAD_DIM, (h + 1) * HEAD_DIM)
            p = _hgrn_prep(q_ref[:, hs], f_ref[:, hs], lb[:, hs])
            v = v_ref[:, hs]
            vb = v.astype(BF16)
            vt = v.T.astype(BF16)
            q_hat = (p["q"] * jnp.exp(p["b"] - p["a"])).astype(BF16)
            k_hat = (p["k"] * jnp.exp(p["a"] - p["b"])).astype(BF16)
            q_in = (p["q"] * jnp.exp(p["b"])).astype(BF16)
            k_out = (p["k"] * jnp.exp(p["bl"] - p["b"])).astype(BF16)
            dec = jnp.exp(p["bl"])
            att = jnp.where(mask, _nt(q_hat, k_hat), 0.0).astype(BF16)
            o_intra = _nn(att, vb)
            st = st_ref[h]
            for c in range(nc):
                rs = slice(c * CHUNK, (c + 1) * CHUNK)
                stb = st.astype(BF16)
                s_ref[c, h] = stb
                o_ref[rs, hs] = o_intra[rs] + _nt(q_in[rs], stb)
                k_c = jnp.where(rowc == c, k_out, jnp.zeros_like(k_out))
                st = st * dec[c * CHUNK:c * CHUNK + 1, :] + _nn(vt, k_c)
            st_ref[h] = st
            o = o_ref[:, hs]
            _, xh = _rms_stats(o)
            gr = g_ref[:, hs]
            og_ref[:, hs] = (xh * gam * (gr * _sigmoid(gr))).astype(BF16)

    col = lambda k: pl.BlockSpec((TL, H), lambda i, k=k: (i, k))
    return pl.pallas_call(
        body, name="hgrn_fwd", grid=(L // TL,),
        in_specs=[col(0), col(1), col(2), col(3),
                  pl.BlockSpec(lower_bounds.shape, lambda i: (0, 0)),
                  pl.BlockSpec(gamma.shape, lambda i: (0, 0))],
        out_specs=[pl.BlockSpec((TL, H), lambda i: (i, 0)),
                   pl.BlockSpec((TL, H), lambda i: (i, 0)),
                   pl.BlockSpec((nc, nh, HEAD_DIM, HEAD_DIM), lambda i: (i, 0, 0, 0))],
        out_shape=[_sds((L, H), BF16), _sds((L, H), F32),
                   _sds((L // CHUNK, nh, HEAD_DIM, HEAD_DIM), BF16)],
        scratch_shapes=[pltpu.VMEM((nh, HEAD_DIM, HEAD_DIM), F32)],
        compiler_params=_params(("arbitrary",), 48),
    )(proj, proj, proj, proj, lower_bounds, gamma)


def _hgrn_bwd(proj, lower_bounds, gamma, o_pre, d_out, s_saved, H):
    L = proj.shape[0]
    nh = H // HEAD_DIM
    TL = min(L, 256)
    nc = TL // CHUNK
    nt = L // TL

    def body(q_ref, f_ref, v_ref, g_ref, lbp_ref, gam_ref, o_ref, d_ref, s_ref,
             dq_ref, df_ref, dv_ref, dg_ref, red_ref, dst_ref, dsall_ref, tmp_ref):
        @pl.when(pl.program_id(0) == 0)
        def _():
            dst_ref[...] = jnp.zeros_like(dst_ref)
            red_ref[...] = jnp.zeros_like(red_ref)

        lb = _lower_bound(lbp_ref[...])
        gam = gam_ref[...]
        mask, mask_t = _chunk_masks(TL)
        rowc = lax.broadcasted_iota(jnp.int32, (TL, HEAD_DIM), 0) // CHUNK
        for h in range(nh):
            hs = slice(h * HEAD_DIM, (h + 1) * HEAD_DIM)
            qr, gr, lbh = q_ref[:, hs], g_ref[:, hs], lb[:, hs]
            p = _hgrn_prep(qr, f_ref[:, hs], lbh)
            vb = v_ref[:, hs].astype(BF16)
            eba, eab = jnp.exp(p["b"] - p["a"]), jnp.exp(p["a"] - p["b"])
            eb, elb = jnp.exp(p["b"]), jnp.exp(p["bl"] - p["b"])
            dec = jnp.exp(p["bl"])
            q_hat, k_hat = p["q"] * eba, p["k"] * eab
            q_in, k_out = p["q"] * eb, p["k"] * elb
            q_hat_b, k_hat_b = q_hat.astype(BF16), k_hat.astype(BF16)
            q_in_b, k_out_b = q_in.astype(BF16), k_out.astype(BF16)

            o, dout = o_ref[:, hs], d_ref[:, hs]
            sg = _sigmoid(gr)
            r, xh = _rms_stats(o)
            dg_ref[:, hs] = (dout * (xh * gam) * (sg * (1.0 + gr * (1.0 - sg)))).astype(BF16)
            dn = dout * (gr * sg)
            red_ref[1:2, hs] += jnp.sum(dn * xh, axis=0, keepdims=True)
            do = _rms_bwd(dn * gam, xh, r)
            dob = do.astype(BF16)
            dot_b = do.T.astype(BF16)

            att_t = jnp.where(mask_t, _nt(k_hat_b, q_hat_b), 0.0).astype(BF16)
            dv_intra = _nn(att_t, dob)
            datt = jnp.where(mask, _nt(dob, vb), 0.0).astype(BF16)
            dqh = _nn(datt, k_hat_b)
            datt_t = jnp.where(mask_t, _nt(vb, dob), 0.0).astype(BF16)
            dkh = _nn(datt_t, q_hat_b)

            dst = dst_ref[h]
            for c in reversed(range(nc)):
                dsall_ref[c] = dst
                q_c = jnp.where(rowc == c, q_in_b, jnp.zeros_like(q_in_b))
                dst = dst * dec[c * CHUNK:c * CHUNK + 1, :] + _nn(dot_b, q_c)
            dst_ref[h] = dst
            for c in range(nc):
                rs = slice(c * CHUNK, (c + 1) * CHUNK)
                ds_c = dsall_ref[c]
                dsb = ds_c.astype(BF16)
                st_prev = s_ref[c, h]
                tmp_ref[0, rs, :] = _nt(k_out_b[rs], dsb)
                tmp_ref[1, rs, :] = _nn(vb[rs], dsb)
                tmp_ref[2, rs, :] = _nn(dob[rs], st_prev)
                ddec = jnp.sum(ds_c * st_prev.astype(F32), axis=0, keepdims=True)
                tmp_ref[3, rs, :] = jnp.broadcast_to(ddec * dec[c * CHUNK:c * CHUNK + 1, :],
                                                     (CHUNK, HEAD_DIM))
            dko, dqi = tmp_ref[1], tmp_ref[2]
            dq = dqh * eba + dqi * eb
            dk = dkh * eab + dko * elb
            tko = dko * k_out
            db = dqh * q_hat - dkh * k_hat + dqi * q_in - tko
            dlog = (_seg_scan(db, p["r32"], False)
                    + _bcast_row(_seg_scan(tko, p["r32"], True), CHUNK - 1) + tmp_ref[3])
            df = dlog / p["f"] - dk
            sig = p["sig"]
            red_ref[0:1, hs] += jnp.sum(df * (1.0 - sig), axis=0, keepdims=True)
            df_ref[:, hs] = (df * (1.0 - lbh) * sig * (1.0 - sig)).astype(BF16)
            sq = p["sq"]
            dq_ref[:, hs] = (dq * (HEAD_DIM ** -0.5) * (sq * (1.0 + qr * (1.0 - sq)))).astype(BF16)
            dv_ref[:, hs] = (dv_intra + tmp_ref[0]).astype(BF16)

    col = lambda k: pl.BlockSpec((TL, H), lambda i, k=k: (nt - 1 - i, k))
    rev = pl.BlockSpec((TL, H), lambda i: (nt - 1 - i, 0))
    return pl.pallas_call(
        body, name="hgrn_bwd", grid=(nt,),
        in_specs=[col(0), col(1), col(2), col(3),
                  pl.BlockSpec(lower_bounds.shape, lambda i: (0, 0)),
                  pl.BlockSpec(gamma.shape, lambda i: (0, 0)),
                  rev, rev,
                  pl.BlockSpec((nc, nh, HEAD_DIM, HEAD_DIM), lambda i: (nt - 1 - i, 0, 0, 0))],
        out_specs=[rev, rev, rev, rev, pl.BlockSpec((8, H), lambda i: (0, 0))],
        out_shape=[_sds((L, H), BF16)] * 4 + [_sds((8, H), F32)],
        scratch_shapes=[pltpu.VMEM((nh, HEAD_DIM, HEAD_DIM), F32),
                        pltpu.VMEM((nc, HEAD_DIM, HEAD_DIM), F32),
                        pltpu.VMEM((4, TL, HEAD_DIM), F32)],
        compiler_params=_params(("arbitrary",), 48),
    )(proj, proj, proj, proj, lower_bounds, gamma, o_pre, d_out, s_saved)


def _shift_down(u, s, row):
    return jnp.where(row >= s, pltpu.roll(u, s, 0), 0.0)


def _shift_up(u, s, row):
    n = u.shape[0]
    return jnp.where(row < n - s, pltpu.roll(u, n - s, 0), 0.0)


def _conv_specs(L, H):
    per = H // LANES
    return [pl.BlockSpec((L, LANES), lambda j, o=o: (0, o * per + j)) for o in (4, 5, 6)]


def _conv_fwd(proj, conv_w, H):
    L = proj.shape[0]

    def body(c_ref, b_ref, x_ref, w_ref, o_ref):
        row = lax.broadcasted_iota(jnp.int32, (L, LANES), 0)
        u = c_ref[...] * x_ref[...]
        w = w_ref[...]
        y = w[0:1] * _shift_down(u, 2, row) + w[1:2] * _shift_down(u, 1, row) + w[2:3] * u
        o_ref[...] = (b_ref[...] * y).astype(BF16)

    return pl.pallas_call(
        body, name="conv_fwd", grid=(H // LANES,),
        in_specs=_conv_specs(L, H) + [pl.BlockSpec((3, LANES), lambda j: (0, j))],
        out_specs=pl.BlockSpec((L, LANES), lambda j: (0, j)),
        out_shape=_sds((L, H), BF16),
        compiler_params=_params(("parallel",), 48),
    )(proj, proj, proj, conv_w)


def _conv_bwd(proj, conv_w, dcb, H):
    L = proj.shape[0]

    def body(c_ref, b_ref, x_ref, w_ref, d_ref, dc_ref, db_ref, dx_ref, dw_ref):
        row = lax.broadcasted_iota(jnp.int32, (L, LANES), 0)
        cg, xb = c_ref[...], x_ref[...]
        u = cg * xb
        u1, u2 = _shift_down(u, 1, row), _shift_down(u, 2, row)
        w = w_ref[...]
        y = w[0:1] * u2 + w[1:2] * u1 + w[2:3] * u
        d = d_ref[...]
        db_ref[...] = (d * y).astype(BF16)
        dy = d * b_ref[...]
        du = w[2:3] * dy + w[1:2] * _shift_up(dy, 1, row) + w[0:1] * _shift_up(dy, 2, row)
        dw_ref[0:1, :] = jnp.sum(dy * u2, axis=0, keepdims=True)
        dw_ref[1:2, :] = jnp.sum(dy * u1, axis=0, keepdims=True)
        dw_ref[2:3, :] = jnp.sum(dy * u, axis=0, keepdims=True)
        dc_ref[...] = (du * xb).astype(BF16)
        dx_ref[...] = (du * cg).astype(BF16)

    blk = pl.BlockSpec((L, LANES), lambda j: (0, j))
    return pl.pallas_call(
        body, name="conv_bwd", grid=(H // LANES,),
        in_specs=_conv_specs(L, H) + [pl.BlockSpec((3, LANES), lambda j: (0, j)), blk],
        out_specs=[blk, blk, blk, pl.BlockSpec((3, LANES), lambda j: (0, j))],
        out_shape=[_sds((L, H), BF16)] * 3 + [_sds((3, H), F32)],
        compiler_params=_params(("parallel",), 56),
    )(proj, proj, proj, conv_w, dcb)


def _gate_specs(tm, H):
    return [pl.BlockSpec((tm, H), lambda i, k=k: (i, k)) for k in (7, 8, 9, 10)]


def _fwd_mix(og, cb, proj, x, wa3, wb3, wout, g_ffn, H):
    L, D = x.shape
    sw = wa3.shape[2]
    tm = min(L, 256)

    def body(o_ref, cb_ref, ga0, ga1, gb0, gb1, x_ref, wa_ref, wb_ref, wo_ref, g_ref,
             ya_ref, yb_ref, m_ref, x1_ref, h2_ref):
        o, cbv = o_ref[...], cb_ref[...]
        for k in range(N_CHIPS):
            cs = slice(k * sw, (k + 1) * sw)
            gs = slice((k % 2) * sw, (k % 2 + 1) * sw)
            gar, gbr = (ga0, ga1)[k // 2], (gb0, gb1)[k // 2]
            ya, yb = _nn(o, wa_ref[k]), _nn(cbv, wb_ref[k])
            ya_ref[:, cs] = ya.astype(BF16)
            yb_ref[:, cs] = yb.astype(BF16)
            m_ref[:, cs] = (_sigmoid(gar[:, gs]) * ya + _sigmoid(gbr[:, gs]) * yb).astype(BF16)
        x1 = x_ref[...] + _nn(m_ref[...], wo_ref[...])
        x1_ref[...] = x1
        _, xh = _rms_stats(x1)
        h2_ref[...] = (xh * g_ref[...]).astype(BF16)

    row = lambda w: pl.BlockSpec((tm, w), lambda i: (i, 0))
    full = lambda a: pl.BlockSpec(a.shape, lambda i: (0,) * a.ndim)
    return pl.pallas_call(
        body, name="fwd_mix", grid=(L // tm,),
        in_specs=[row(H), row(H)] + _gate_specs(tm, H) + [row(D), full(wa3), full(wb3), full(wout),
                                                           full(g_ffn)],
        out_specs=[row(D)] * 5,
        out_shape=[_sds((L, D), BF16)] * 3 + [_sds((L, D), F32), _sds((L, D), BF16)],
        compiler_params=_params(("parallel",), 48),
    )(og, cb, proj, proj, proj, proj, x, wa3, wb3, wout, g_ffn)


def _bwd_mix(dx1b, proj, ya, yb, wa3, wb3, wout, H):
    L, D = dx1b.shape
    sw = wa3.shape[2]
    tm = min(L, 256)

    def body(dx_ref, ga0, ga1, gb0, gb1, ya_ref, yb_ref, wa_ref, wb_ref, wo_ref,
             dya_ref, dyb_ref, dga_ref, dgb_ref, do_ref, dcb_ref):
        dm = _nt(dx_ref[...], wo_ref[...])
        do = jnp.zeros((tm, H), F32)
        dcb = jnp.zeros((tm, H), F32)
        for k in range(N_CHIPS):
            cs = slice(k * sw, (k + 1) * sw)
            gs = slice((k % 2) * sw, (k % 2 + 1) * sw)
            gar, gbr = (ga0, ga1)[k // 2], (gb0, gb1)[k // 2]
            sa, sb = _sigmoid(gar[:, gs]), _sigmoid(gbr[:, gs])
            dmk = dm[:, cs]
            dga_ref[:, cs] = (dmk * ya_ref[:, cs].astype(F32) * sa * (1.0 - sa)).astype(BF16)
            dgb_ref[:, cs] = (dmk * yb_ref[:, cs].astype(F32) * sb * (1.0 - sb)).astype(BF16)
            dya, dyb = (dmk * sa).astype(BF16), (dmk * sb).astype(BF16)
            dya_ref[:, cs] = dya
            dyb_ref[:, cs] = dyb
            do = do + _nt(dya, wa_ref[k])
            dcb = dcb + _nt(dyb, wb_ref[k])
        do_ref[...] = do
        dcb_ref[...] = dcb

    row = lambda w: pl.BlockSpec((tm, w), lambda i: (i, 0))
    full = lambda a: pl.BlockSpec(a.shape, lambda i: (0,) * a.ndim)
    return pl.pallas_call(
        body, name="bwd_mix", grid=(L // tm,),
        in_specs=[row(D)] + _gate_specs(tm, H) + [row(D), row(D), full(wa3), full(wb3), full(wout)],
        out_specs=[row(D)] * 4 + [row(H)] * 2,
        out_shape=[_sds((L, D), BF16)] * 4 + [_sds((L, H), F32)] * 2,
        compiler_params=_params(("parallel",), 48),
    )(dx1b, proj, proj, proj, proj, ya, yb, wa3, wb3, wout)


def _fwd_ffn_up(h2, wg3, wu3):
    L, D = h2.shape
    fw = wg3.shape[2]
    tm = min(L, 1024)

    def body(h_ref, wg_ref, wu_ref, a_ref, b_ref, s_ref):
        h = h_ref[...]
        a, b = _nn(h, wg_ref[...]), _nn(h, wu_ref[...])
        a_ref[...] = a.astype(BF16)
        b_ref[...] = b.astype(BF16)
        s_ref[...] = (a * _sigmoid(a) * b).astype(BF16)

    wspec = pl.BlockSpec((None, D, fw), lambda i, j: (j, 0, 0))
    ospec = pl.BlockSpec((None, tm, fw), lambda i, j: (j, i, 0))
    return pl.pallas_call(
        body, name="fwd_ffn_up", grid=(L // tm, N_CHIPS),
        in_specs=[pl.BlockSpec((tm, D), lambda i, j: (i, 0)), wspec, wspec],
        out_specs=[ospec] * 3,
        out_shape=[_sds((N_CHIPS, L, fw), BF16)] * 3,
        compiler_params=_params(("parallel", "arbitrary"), 48),
    )(h2, wg3, wu3)


def _fwd_down_loss(s3, wd3, x1, target, g_final):
    L, D = x1.shape
    fw = wd3.shape[1]
    tm = min(L, 512)
    last = N_CHIPS - 1

    def body(s_ref, wd_ref, x1_ref, t_ref, g_ref, dx_ref, dxb_ref, red_ref, acc_ref):
        i, k = pl.program_id(0), pl.program_id(1)

        @pl.when(k == 0)
        def _():
            acc_ref[...] = x1_ref[...]

        acc_ref[...] += _nn(s_ref[...], wd_ref[...])

        @pl.when((i == 0) & (k == last))
        def _():
            red_ref[...] = jnp.zeros_like(red_ref)

        @pl.when(k == last)
        def _():
            g = g_ref[...]
            r, xh = _rms_stats(acc_ref[...])
            e = xh * g - t_ref[...]
            dy = e * (1.0 / D)
            dx = _rms_bwd(dy * g, xh, r)
            dx_ref[...] = dx
            dxb_ref[...] = dx.astype(BF16)
            red_ref[0:1, :] += jnp.sum(dy * xh, axis=0, keepdims=True)
            red_ref[1:2, :] += jnp.broadcast_to(0.5 * jnp.sum(e * e) * (1.0 / D), (1, D))

    row = pl.BlockSpec((tm, D), lambda i, k: (i, 0))
    return pl.pallas_call(
        body, name="fwd_down_loss", grid=(L // tm, N_CHIPS),
        in_specs=[pl.BlockSpec((None, tm, fw), lambda i, k: (k, i, 0)),
                  pl.BlockSpec((None, fw, D), lambda i, k: (k, 0, 0)),
                  row, row, pl.BlockSpec((1, D), lambda i, k: (0, 0))],
        out_specs=[row, row, pl.BlockSpec((8, D), lambda i, k: (0, 0))],
        out_shape=[_sds((L, D), F32), _sds((L, D), BF16), _sds((8, D), F32)],
        scratch_shapes=[pltpu.VMEM((tm, D), F32)],
        compiler_params=_params(("arbitrary", "arbitrary"), 48),
    )(s3, wd3, x1, target, g_final)


def _bwd_down(dx2b, wd3, a3, b3):
    L, D = dx2b.shape
    fw = wd3.shape[1]
    tm = min(L, 1024)

    def body(dx_ref, wd_ref, a_ref, b_ref, da_ref, db_ref):
        ds = _nt(dx_ref[...], wd_ref[...])
        a, b = a_ref[...].astype(F32), b_ref[...].astype(F32)
        sg = _sigmoid(a)
        da_ref[...] = (ds * b * sg * (1.0 + a * (1.0 - sg))).astype(BF16)
        db_ref[...] = (ds * a * sg).astype(BF16)

    ospec = pl.BlockSpec((None, tm, fw), lambda i, j: (j, i, 0))
    return pl.pallas_call(
        body, name="bwd_down", grid=(L // tm, N_CHIPS),
        in_specs=[pl.BlockSpec((tm, D), lambda i, j: (i, 0)),
                  pl.BlockSpec((None, fw, D), lambda i, j: (j, 0, 0)), ospec, ospec],
        out_specs=[ospec] * 2,
        out_shape=[_sds((N_CHIPS, L, fw), BF16)] * 2,
        compiler_params=_params(("parallel", "arbitrary"), 48),
    )(dx2b, wd3, a3, b3)


def _bwd_ffn_dh(da3, db3, wg3, wu3, x1, dx2, g_ffn):
    L, D = x1.shape
    fw = wg3.shape[2]
    tm = min(L, 512)
    last = N_CHIPS - 1

    def body(da_ref, db_ref, wg_ref, wu_ref, x1_ref, dx2_ref, g_ref, dx_ref, dxb_ref, red_ref,
             acc_ref):
        i, k = pl.program_id(0), pl.program_id(1)
        part = _nt(da_ref[...], wg_ref[...]) + _nt(db_ref[...], wu_ref[...])

        @pl.when(k == 0)
        def _():
            acc_ref[...] = part

        @pl.when(k > 0)
        def _():
            acc_ref[...] += part

        @pl.when((i == 0) & (k == last))
        def _():
            red_ref[...] = jnp.zeros_like(red_ref)

        @pl.when(k == last)
        def _():
            dh = acc_ref[...]
            r, xh = _rms_stats(x1_ref[...])
            red_ref[0:1, :] += jnp.sum(dh * xh, axis=0, keepdims=True)
            dx = dx2_ref[...] + _rms_bwd(dh * g_ref[...], xh, r)
            dx_ref[...] = dx
            dxb_ref[...] = dx.astype(BF16)

    row = pl.BlockSpec((tm, D), lambda i, k: (i, 0))
    aspec = pl.BlockSpec((None, tm, fw), lambda i, k: (k, i, 0))
    wspec = pl.BlockSpec((None, D, fw), lambda i, k: (k, 0, 0))
    return pl.pallas_call(
        body, name="bwd_ffn_dh", grid=(L // tm, N_CHIPS),
        in_specs=[aspec, aspec, wspec, wspec, row, row, pl.BlockSpec((1, D), lambda i, k: (0, 0))],
        out_specs=[row, row, pl.BlockSpec((8, D), lambda i, k: (0, 0))],
        out_shape=[_sds((L, D), F32), _sds((L, D), BF16), _sds((8, D), F32)],
        scratch_shapes=[pltpu.VMEM((tm, D), F32)],
        compiler_params=_params(("arbitrary", "arbitrary"), 48),
    )(da3, db3, wg3, wu3, x1, dx2, g_ffn)


def _bwd_in(dproj, w_in3, x, dx1, g_mix):
    L, D = x.shape
    tn = w_in3.shape[2]
    tm = min(L, 512)
    last = N_CHIPS - 1

    def body(dp_ref, w_ref, x_ref, dx1_ref, g_ref, dx_ref, red_ref, acc_ref):
        i, k = pl.program_id(0), pl.program_id(1)
        part = _nt(dp_ref[...], w_ref[...])

        @pl.when(k == 0)
        def _():
            acc_ref[...] = part

        @pl.when(k > 0)
        def _():
            acc_ref[...] += part

        @pl.when((i == 0) & (k == last))
        def _():
            red_ref[...] = jnp.zeros_like(red_ref)

        @pl.when(k == last)
        def _():
            dh = acc_ref[...]
            r, xh = _rms_stats(x_ref[...])
            red_ref[0:1, :] += jnp.sum(dh * xh, axis=0, keepdims=True)
            dx_ref[...] = dx1_ref[...] + _rms_bwd(dh * g_ref[...], xh, r)

    row = pl.BlockSpec((tm, D), lambda i, k: (i, 0))
    return pl.pallas_call(
        body, name="bwd_in", grid=(L // tm, N_CHIPS),
        in_specs=[pl.BlockSpec((tm, tn), lambda i, k: (i, k)),
                  pl.BlockSpec((None, D, tn), lambda i, k: (k, 0, 0)),
                  row, row, pl.BlockSpec((1, D), lambda i, k: (0, 0))],
        out_specs=[row, pl.BlockSpec((8, D), lambda i, k: (0, 0))],
        out_shape=[_sds((L, D), F32), _sds((8, D), F32)],
        scratch_shapes=[pltpu.VMEM((tm, D), F32)],
        compiler_params=_params(("arbitrary", "arbitrary"), 48),
    )(dproj, w_in3, x, dx1, g_mix)


def _mm_tn(name, a, b, a_spec, b_spec, o_block, n_out, n_k):
    def body(a_ref, b_ref, o_ref):
        part = _tn(a_ref[...], b_ref[...])

        @pl.when(pl.program_id(1) == 0)
        def _():
            o_ref[...] = part

        @pl.when(pl.program_id(1) > 0)
        def _():
            o_ref[...] += part

    return pl.pallas_call(
        body, name=name, grid=(n_out, n_k),
        in_specs=[a_spec, b_spec],
        out_specs=pl.BlockSpec((None,) + o_block, lambda j, k: (j, 0, 0)),
        out_shape=_sds((n_out,) + o_block, F32),
        compiler_params=_params(("parallel", "arbitrary"), 48),
    )(a, b)


def _dw_cols(name, a, b, n_cols):
    L, M = a.shape
    tk = min(L, 512)
    return _mm_tn(name, a, b, pl.BlockSpec((tk, M), lambda j, k: (k, 0)),
                  pl.BlockSpec((tk, n_cols), lambda j, k: (k, j)), (M, n_cols), N_CHIPS, L // tk)


def _dw_cols3(name, a, b3):
    L, M = a.shape
    fw = b3.shape[2]
    tk = min(L, 512)
    return _mm_tn(name, a, b3, pl.BlockSpec((tk, M), lambda j, k: (k, 0)),
                  pl.BlockSpec((None, tk, fw), lambda j, k: (j, k, 0)), (M, fw), N_CHIPS, L // tk)


def _dw_rows(name, a, b):
    L, M = a.shape
    N = b.shape[1]
    tk = min(L, 512)
    return _mm_tn(name, a, b, pl.BlockSpec((tk, M // N_CHIPS), lambda j, k: (k, j)),
                  pl.BlockSpec((tk, N), lambda j, k: (k, 0)), (M // N_CHIPS, N), N_CHIPS, L // tk)


def _dw_rows3(name, a3, b):
    _, L, fw = a3.shape
    N = b.shape[1]
    tk = min(L, 512)
    return _mm_tn(name, a3, b, pl.BlockSpec((None, tk, fw), lambda j, k: (j, k, 0)),
                  pl.BlockSpec((tk, N), lambda j, k: (k, 0)), (fw, N), N_CHIPS, L // tk)


def _place():
    x, y, c = lax.axis_index("x"), lax.axis_index("y"), lax.axis_index("c")
    chips = [(1 - x, y), (x, 1 - y), (1 - x, 1 - y)]
    return x, y, c, 2 * x + y, chips


def _remote(src, dst, send_sem, recv_sem, device):
    return pltpu.make_async_remote_copy(src_ref=src, dst_ref=dst, send_sem=send_sem,
                                        recv_sem=recv_sem, device_id=device, device_id_type=MESH)


def _half(ref, lead, c, r2):
    return ref.at[lead, pl.ds(pl.multiple_of(c * r2, 16), r2), :]


def _gather_weights(shards, conv_shard):
    n = len(shards)

    def body(*refs):
        ins, conv_in = refs[:n], refs[n]
        outs, conv_out = refs[n + 1:2 * n + 1], refs[2 * n + 1]
        send_sems, recv_sems, local_sems = refs[2 * n + 2:]
        x, y, c, k, chips = _place()
        me, sibling = (x, y, c), (x, y, 1 - c)
        local, sends, forwards = [], [], []
        for w in range(n):
            local.append(pltpu.make_async_copy(ins[w], outs[w].at[k], local_sems.at[w]))
        local.append(pltpu.make_async_copy(conv_in, conv_out.at[k], local_sems.at[n]))
        for cp in local:
            cp.start()
        for w in range(n):
            r2 = ins[w].shape[0] // 2
            for j, chip in enumerate(chips):
                sends.append(_remote(ins[w].at[pl.ds(pl.multiple_of(c * r2, 16), r2), :],
                                     _half(outs[w], k, c, r2),
                                     send_sems.at[w * 6 + j], recv_sems.at[w * 6 + j], (*chip, c)))
        for j, chip in enumerate(chips):
            sends.append(_remote(conv_in, conv_out.at[k], send_sems.at[n * 6 + j],
                                 recv_sems.at[n * 6 + j], (*chip, c)))
        for cp in sends:
            cp.start()
        for w in range(n):
            r2 = ins[w].shape[0] // 2
            for j, (cx, cy) in enumerate(chips):
                kj = 2 * cx + cy
                landed = _half(outs[w], kj, c, r2)
                _remote(landed, landed, send_sems.at[w * 6 + j], recv_sems.at[w * 6 + j], me).wait_recv()
                fwd = _remote(landed, landed, send_sems.at[w * 6 + 3 + j], recv_sems.at[w * 6 + 3 + j],
                              sibling)
                fwd.start()
                forwards.append(fwd)
        for j, (cx, cy) in enumerate(chips):
            got = conv_out.at[2 * cx + cy]
            _remote(got, got, send_sems.at[n * 6 + j], recv_sems.at[n * 6 + j], me).wait_recv()
        for w in range(n):
            r2 = ins[w].shape[0] // 2
            for j, (cx, cy) in enumerate(chips):
                got = _half(outs[w], 2 * cx + cy, 1 - c, r2)
                _remote(got, got, send_sems.at[w * 6 + 3 + j], recv_sems.at[w * 6 + 3 + j], me).wait_recv()
        for cp in sends + forwards:
            cp.wait_send()
        for cp in local:
            cp.wait()

    out_shape = [_sds((N_CHIPS,) + s.shape, s.dtype) for s in shards]
    out_shape.append(_sds((N_CHIPS,) + conv_shard.shape, conv_shard.dtype))
    return pl.pallas_call(
        body, name="gather_w",
        in_specs=[ANY] * (n + 1), out_specs=[ANY] * (n + 1), out_shape=out_shape,
        scratch_shapes=[pltpu.SemaphoreType.DMA((n * 6 + 3,)), pltpu.SemaphoreType.DMA((n * 6 + 3,)),
                        pltpu.SemaphoreType.DMA((n + 1,))],
    )(*shards, conv_shard)


def _rs_sibling(grads):
    n = len(grads)

    def body(*refs):
        ins, outs = refs[:n], refs[n:2 * n]
        send_sems, recv_sems = refs[2 * n:]
        x, y, c, _, _ = _place()
        copies = []
        for w in range(n):
            r2 = ins[w].shape[1] // 2
            copies.append(_remote(_half(ins[w], slice(None), 1 - c, r2), outs[w],
                                  send_sems.at[w], recv_sems.at[w], (x, y, 1 - c)))
        for cp in copies:
            cp.start()
        for cp in copies:
            cp.wait()

    return pl.pallas_call(
        body, name="rs_sibling",
        in_specs=[ANY] * n, out_specs=[ANY] * n,
        out_shape=[_sds((N_CHIPS, g.shape[1] // 2, g.shape[2]), F32) for g in grads],
        scratch_shapes=[pltpu.SemaphoreType.DMA((n,)), pltpu.SemaphoreType.DMA((n,))],
    )(*grads)


def _rs_add(name, grad3, from_sibling, c_idx):
    _, r2, cols = from_sibling.shape

    def body(c_ref, g_ref, s_ref, o_ref):
        o_ref[...] = (g_ref[...] + s_ref[...]).astype(BF16)

    return pl.pallas_call(
        body, name=name,
        grid_spec=pltpu.PrefetchScalarGridSpec(
            num_scalar_prefetch=1, grid=(N_CHIPS,),
            in_specs=[pl.BlockSpec((None, r2, cols), lambda k, c_ref: (k, c_ref[0], 0)),
                      pl.BlockSpec((None, r2, cols), lambda k, c_ref: (k, 0, 0))],
            out_specs=pl.BlockSpec((None, r2, cols), lambda k, c_ref: (k, 0, 0))),
        out_shape=_sds(from_sibling.shape, BF16),
        compiler_params=_params(("parallel",), 48),
    )(c_idx, grad3, from_sibling)


def _rs_ici(partials):
    n = len(partials)

    def body(*refs):
        ins, outs = refs[:n], refs[n:2 * n]
        send_sems, recv_sems, local_sems = refs[2 * n:]
        x, y, c, k, chips = _place()
        local, sends = [], []
        for w in range(n):
            local.append(pltpu.make_async_copy(ins[w].at[k], outs[w].at[3], local_sems.at[w]))
            for j, (cx, cy) in enumerate(chips):
                sends.append(_remote(ins[w].at[2 * cx + cy], outs[w].at[j],
                                     send_sems.at[w * 3 + j], recv_sems.at[w * 3 + j], (cx, cy, c)))
        for cp in local + sends:
            cp.start()
        for cp in sends:
            cp.wait()
        for cp in local:
            cp.wait()

    return pl.pallas_call(
        body, name="rs_ici",
        in_specs=[ANY] * n, out_specs=[ANY] * n,
        out_shape=[_sds(p.shape, BF16) for p in partials],
        scratch_shapes=[pltpu.SemaphoreType.DMA((n * 3,)), pltpu.SemaphoreType.DMA((n * 3,)),
                        pltpu.SemaphoreType.DMA((n,))],
    )(*partials)


def _rs_sum(name, parts):
    _, r2, cols = parts.shape
    tr = r2 // 2

    def body(p_ref, o_ref):
        o_ref[...] = ((p_ref[3].astype(F32) + p_ref[0].astype(F32))
                      + (p_ref[1].astype(F32) + p_ref[2].astype(F32)))

    return pl.pallas_call(
        body, name=name, grid=(2,),
        in_specs=[pl.BlockSpec((N_CHIPS, tr, cols), lambda i: (0, i, 0))],
        out_specs=pl.BlockSpec((tr, cols), lambda i: (i, 0)),
        out_shape=_sds((r2, cols), F32),
        compiler_params=_params(("parallel",), 48),
    )(parts)


def _rs_share(halves):
    n = len(halves)

    def body(*refs):
        ins, outs = refs[:n], refs[n:2 * n]
        send_sems, recv_sems, local_sems = refs[2 * n:]
        x, y, c, _, _ = _place()
        local, sends = [], []
        for w in range(n):
            r2 = ins[w].shape[0]
            mine = outs[w].at[pl.ds(pl.multiple_of(c * r2, 8), r2), :]
            local.append(pltpu.make_async_copy(ins[w], mine, local_sems.at[w]))
            sends.append(_remote(ins[w], mine, send_sems.at[w], recv_sems.at[w], (x, y, 1 - c)))
        for cp in local + sends:
            cp.start()
        for w in range(n):
            r2 = ins[w].shape[0]
            theirs = outs[w].at[pl.ds(pl.multiple_of((1 - c) * r2, 8), r2), :]
            _remote(ins[w], theirs, send_sems.at[w], recv_sems.at[w], (x, y, c)).wait_recv()
        for cp in sends:
            cp.wait_send()
        for cp in local:
            cp.wait()

    return pl.pallas_call(
        body, name="rs_share",
        in_specs=[ANY] * n, out_specs=[ANY] * n,
        out_shape=[_sds((2 * h.shape[0], h.shape[1]), F32) for h in halves],
        scratch_shapes=[pltpu.SemaphoreType.DMA((n,)), pltpu.SemaphoreType.DMA((n,)),
                        pltpu.SemaphoreType.DMA((n,))],
    )(*halves)


def _small_allreduce(block):
    rows, D = block.shape

    def body(in_ref, sum_ref, all_ref, send_sems, recv_sems):
        x, y, c, _, _ = _place()
        me = 4 * x + 2 * y + c
        all_ref[me] = in_ref[...]
        copies = []
        for m in range(1, 8):
            mx, my, mc = (m >> 2) & 1, (m >> 1) & 1, m & 1
            px, py, pc = x ^ mx, y ^ my, c ^ mc
            copies.append((_remote(in_ref, all_ref.at[me], send_sems.at[m - 1], recv_sems.at[m - 1],
                                   (px, py, pc)), 4 * px + 2 * py + pc, m))
        for cp, _, _ in copies:
            cp.start()
        for _, peer, m in copies:
            _remote(in_ref, all_ref.at[peer], send_sems.at[m - 1], recv_sems.at[m - 1],
                    (x, y, c)).wait_recv()
        for cp, _, _ in copies:
            cp.wait_send()
        total = all_ref[0]
        for d in range(1, 8):
            total = total + all_ref[d]
        sum_ref[...] = total

    return pl.pallas_call(
        body, name="small_allreduce",
        in_specs=[VMEM], out_specs=[VMEM, VMEM],
        out_shape=[_sds((rows, D), F32), _sds((8, rows, D), F32)],
        scratch_shapes=[pltpu.SemaphoreType.DMA((7,)), pltpu.SemaphoreType.DMA((7,))],
    )(block)[0]


def _adamw_math(w, g, m, v):
    m = ADAM_B1 * m + (1.0 - ADAM_B1) * g
    v = ADAM_B2 * v + (1.0 - ADAM_B2) * jnp.square(g)
    m_hat = m / (1.0 - ADAM_B1 ** ADAM_STEP)
    v_hat = v / (1.0 - ADAM_B2 ** ADAM_STEP)
    delta = -ADAM_LR * (m_hat / (jnp.sqrt(v_hat) + ADAM_EPS) + ADAM_WD * w)
    return delta, m, v


def _adamw(name, g, w, m, v):
    r, cols = g.shape
    tr = r // 4

    def body(g_ref, w_ref, m_ref, v_ref, d_ref, mo_ref, vo_ref):
        d_ref[...], mo_ref[...], vo_ref[...] = _adamw_math(w_ref[...], g_ref[...], m_ref[...], v_ref[...])

    blk = pl.BlockSpec((tr, cols), lambda i: (i, 0))
    return pl.pallas_call(
        body, name=name, grid=(r // tr,),
        in_specs=[blk] * 4, out_specs=[blk] * 3, out_shape=[_sds((r, cols), F32)] * 3,
        compiler_params=_params(("parallel",), 48),
    )(g, w, m, v)


def _small_update(gsum, wp, mp, vp, lower_bounds):
    rows, D = gsum.shape
    H = lower_bounds.shape[1]

    def body(s_ref, w_ref, m_ref, v_ref, lbp_ref, g_ref, d_ref, mo_ref, vo_ref):
        g_ref[...] = s_ref[...]
        p0 = _lower_bound(lbp_ref[...])
        dl0 = p0 * (1.0 - p0) * s_ref[4:5, 0:H]
        g_ref[4:5, 0:H] = dl0
        g_ref[5:6, 0:H] = -dl0
        d_ref[...], mo_ref[...], vo_ref[...] = _adamw_math(w_ref[...], g_ref[...], m_ref[...], v_ref[...])

    return pl.pallas_call(
        body, name="small_update",
        in_specs=[VMEM] * 5, out_specs=[VMEM] * 4, out_shape=[_sds((rows, D), F32)] * 4,
    )(gsum, wp, mp, vp, lower_bounds)


def _pack_rows(rows, D):
    padded = [jnp.pad(r.reshape(1, -1), ((0, 0), (0, D - r.size))) for r in rows]
    padded.append(jnp.zeros((N_SMALL_ROWS - len(rows), D), F32))
    return jnp.concatenate(padded, axis=0)


def kernel(x, norm_mix_g, w_in, lower_bounds, hg_norm_g, conv_w, w_branch_a, w_branch_b, w_out, norm_ffn_g, w_ffn_gate, w_ffn_up, w_ffn_down, norm_final_g, loss_target, m_norm_mix_g, m_w_in, m_lower_bounds, m_hg_norm_g, m_conv_w, m_w_branch_a, m_w_branch_b, m_w_out, m_norm_ffn_g, m_w_ffn_gate, m_w_ffn_up, m_w_ffn_down, m_norm_final_g, v_norm_mix_g, v_w_in, v_lower_bounds, v_hg_norm_g, v_conv_w, v_w_branch_a, v_w_branch_b, v_w_out, v_norm_ffn_g, v_w_ffn_gate, v_w_ffn_up, v_w_ffn_down, v_norm_final_g):
    _, L, D = x.shape
    H = D // 2
    assert lower_bounds.shape == (2, H) and hg_norm_g.shape == (1, HEAD_DIM)
    assert conv_w.shape == (1, 3, LANES) and w_in.shape[2] * N_CHIPS == 11 * H
    x2d, target = x.reshape(L, D), loss_target.reshape(L, D)
    g_final = norm_final_g.reshape(1, D)
    chip = 2 * lax.axis_index("x") + lax.axis_index("y")
    core = lax.axis_index("c")

    big = [w_in, w_branch_a, w_branch_b, w_out, w_ffn_gate, w_ffn_up, w_ffn_down]
    big_m = [m_w_in, m_w_branch_a, m_w_branch_b, m_w_out, m_w_ffn_gate, m_w_ffn_up, m_w_ffn_down]
    big_v = [v_w_in, v_w_branch_a, v_w_branch_b, v_w_out, v_w_ffn_gate, v_w_ffn_up, v_w_ffn_down]
    names = ["w_in", "w_branch_a", "w_branch_b", "w_out", "w_ffn_gate", "w_ffn_up", "w_ffn_down"]

    gathered = _gather_weights([w[0].astype(BF16) for w in big], conv_w[0])
    w_in3, wa3, wb3, wout3, wg3, wu3, wd3 = gathered[:7]
    conv_full = jnp.transpose(gathered[7], (1, 0, 2)).reshape(3, H)
    wout = wout3.reshape(D, D)

    h, proj = _fwd_proj(x2d, norm_mix_g, w_in3)
    og, o_pre, s_saved = _hgrn_fwd(proj, lower_bounds, hg_norm_g, H)
    cb = _conv_fwd(proj, conv_full, H)
    ya, yb, merged, x1, h2 = _fwd_mix(og, cb, proj, x2d, wa3, wb3, wout, norm_ffn_g, H)
    a3, b3, s3 = _fwd_ffn_up(h2, wg3, wu3)
    dx2, dx2b, red_final = _fwd_down_loss(s3, wd3, x1, target, g_final)

    da3, db3 = _bwd_down(dx2b, wd3, a3, b3)
    g_wd = _dw_rows3("dw_ffn_down", s3, dx2b)
    dx1, dx1b, red_ffn = _bwd_ffn_dh(da3, db3, wg3, wu3, x1, dx2, norm_ffn_g)
    g_wg = _dw_cols3("dw_ffn_gate", h2, da3)
    g_wu = _dw_cols3("dw_ffn_up", h2, db3)
    dya, dyb, dga, dgb, d_o, d_cb = _bwd_mix(dx1b, proj, ya, yb, wa3, wb3, wout, H)
    g_wout = _dw_rows("dw_out", merged, dx1b)
    g_wa = _dw_cols("dw_branch_a", og, dya, D // N_CHIPS)
    g_wb = _dw_cols("dw_branch_b", cb, dyb, D // N_CHIPS)
    dcg, dbg, dxb, g_conv = _conv_bwd(proj, conv_full, d_cb, H)
    dq, df, dv, dg, red_hg = _hgrn_bwd(proj, lower_bounds, hg_norm_g, o_pre, d_o, s_saved, H)
    dproj = jnp.concatenate([dq, df, dv, dg, dcg, dbg, dxb, dga, dgb], axis=1)
    grad_x, red_mix = _bwd_in(dproj, w_in3, x2d, dx1, norm_mix_g)
    g_win = _dw_cols("dw_in", h, dproj, w_in3.shape[2])

    grads3 = [g_win, g_wa, g_wb, g_wout, g_wg, g_wu, g_wd]
    c_idx = core.reshape(1).astype(jnp.int32)
    from_sib = _rs_sibling(grads3)
    partials = [_rs_add("rs_add_" + nm, g, s, c_idx) for nm, g, s in zip(names, grads3, from_sib)]
    mine = _rs_ici(partials)
    halves = [_rs_sum("rs_sum_" + nm, p) for nm, p in zip(names, mine)]
    shard_grads = _rs_share(halves)
    big_out = [_adamw("adamw_" + nm, g, w[0], m[0], v[0])
               for nm, g, w, m, v in zip(names, shard_grads, big, big_m, big_v)]

    hg_row = jnp.concatenate([red_hg[1].reshape(-1, HEAD_DIM).sum(axis=0), red_final[1, :LANES]])
    small = _pack_rows([red_mix[0], red_ffn[0], red_final[0], hg_row, red_hg[0], jnp.zeros((H,), F32),
                        g_conv[0], g_conv[1], g_conv[2]], D)
    total = _small_allreduce(small)
    conv_mine = lax.dynamic_slice(total, (6, chip * LANES), (3, LANES))
    gsum = jnp.concatenate([total[:3], jnp.pad(total[3:4, :HEAD_DIM], ((0, 0), (0, D - HEAD_DIM))),
                            total[4:6], jnp.pad(conv_mine, ((0, 0), (0, D - LANES))), total[9:]], axis=0)

    def pack(ps):
        mix, lb, hg, cw, ffn, fin = ps
        return _pack_rows([mix[0], ffn[0], fin, hg[0], lb[0], lb[1], cw[0, 0], cw[0, 1], cw[0, 2]], D)

    wp = pack((norm_mix_g, lower_bounds, hg_norm_g, conv_w, norm_ffn_g, norm_final_g))
    mp = pack((m_norm_mix_g, m_lower_bounds, m_hg_norm_g, m_conv_w, m_norm_ffn_g, m_norm_final_g))
    vp = pack((v_norm_mix_g, v_lower_bounds, v_hg_norm_g, v_conv_w, v_norm_ffn_g, v_norm_final_g))
    small_out = _small_update(gsum, wp, mp, vp, lower_bounds)

    def unpack(p, i):
        b = lambda j: (shard_grads[j] if i == 0 else big_out[j][i - 1])[None]
        return [p[0:1], b(0), p[4:6, :H], p[3:4, :HEAD_DIM], p[6:9, :LANES][None], b(1), b(2), b(3),
                p[1:2], b(4), b(5), b(6), p[2]]

    loss = total[3, HEAD_DIM]
    outs = [loss, grad_x.reshape(1, L, D)]
    for i in range(4):
        outs += unpack(small_out[i], i)
    return tuple(outs)
```

```python
import functools

import jax
import jax.numpy as jnp
from jax import lax
from jax.experimental import pallas as pl
from jax.experimental.pallas import tpu as pltpu

F32 = jnp.float32
BF16 = jnp.bfloat16
EPS = 1e-6
CHUNK = 32
HEAD_DIM = 128
LANES = 128
N_CHIPS = 4
N_SMALL_ROWS = 16

ADAM_LR = 0.001
ADAM_B1 = 0.9
ADAM_B2 = 0.999
ADAM_EPS = 1e-08
ADAM_WD = 0.01
ADAM_STEP = 10

MESH = pl.DeviceIdType.MESH
ANY = pl.BlockSpec(memory_space=pl.ANY)
VMEM = pl.BlockSpec(memory_space=pltpu.VMEM)


def _sds(shape, dtype):
    return jax.ShapeDtypeStruct(shape, dtype)


def _params(semantics, vmem_mb):
    return pltpu.CompilerParams(dimension_semantics=semantics, vmem_limit_bytes=vmem_mb << 20)


def _nn(a, b):
    return lax.dot_general(a, b, (((1,), (0,)), ((), ())), preferred_element_type=F32)


def _nt(a, b):
    return lax.dot_general(a, b, (((1,), (1,)), ((), ())), preferred_element_type=F32)


def _tn(a, b):
    return lax.dot_general(a, b, (((0,), (0,)), ((), ())), preferred_element_type=F32)


def _sigmoid(x):
    return jax.nn.sigmoid(x)


def _rms_stats(x):
    r = lax.rsqrt(jnp.mean(x * x, axis=-1, keepdims=True) + EPS)
    return r, x * r


def _rms_bwd(dxh, xh, r):
    return r * (dxh - xh * jnp.mean(dxh * xh, axis=-1, keepdims=True))


def _fwd_proj(x, g_mix, w_in3):
    L, D = x.shape
    tn = w_in3.shape[2]
    tm = min(L, 1024)

    def body(x_ref, g_ref, w_ref, h_ref, p_ref):
        @pl.when(pl.program_id(1) == 0)
        def _():
            _, xh = _rms_stats(x_ref[...])
            h_ref[...] = (xh * g_ref[...]).astype(BF16)

        p_ref[...] = _nn(h_ref[...], w_ref[...])

    return pl.pallas_call(
        body, name="fwd_proj", grid=(L // tm, N_CHIPS),
        in_specs=[pl.BlockSpec((tm, D), lambda i, j: (i, 0)),
                  pl.BlockSpec((1, D), lambda i, j: (0, 0)),
                  pl.BlockSpec((None, D, tn), lambda i, j: (j, 0, 0))],
        out_specs=[pl.BlockSpec((tm, D), lambda i, j: (i, 0)),
                   pl.BlockSpec((tm, tn), lambda i, j: (i, j))],
        out_shape=[_sds((L, D), BF16), _sds((L, N_CHIPS * tn), F32)],
        compiler_params=_params(("parallel", "arbitrary"), 48),
    )(x, g_mix, w_in3)


def _lower_bound(lbp):
    l0, l1 = lbp[0:1, :], lbp[1:2, :]
    m = jnp.maximum(l0, l1)
    e0, e1 = jnp.exp(l0 - m), jnp.exp(l1 - m)
    return e0 / (e0 + e1)


def _seg_scan(x, r32, forward):
    n = x.shape[0]
    s = 1
    while s < CHUNK:
        if forward:
            x = x + jnp.where(r32 >= s, pltpu.roll(x, s, 0), 0.0)
        else:
            x = x + jnp.where(r32 < CHUNK - s, pltpu.roll(x, n - s, 0), 0.0)
        s *= 2
    return x


def _bcast_row(x, row):
    n, w = x.shape
    nc = n // CHUNK
    x3 = x.reshape(nc, CHUNK, w)
    return jnp.broadcast_to(x3[:, row:row + 1, :], (nc, CHUNK, w)).reshape(n, w)


def _hgrn_prep(q_raw, f_raw, lb):
    r32 = lax.broadcasted_iota(jnp.int32, f_raw.shape, 0) & (CHUNK - 1)
    sig = _sigmoid(f_raw)
    f = lb + (1.0 - lb) * sig
    b = _seg_scan(jnp.log(f), r32, True)
    a = _bcast_row(b, CHUNK // 2 - 1)
    bl = _bcast_row(b, CHUNK - 1)
    sq = _sigmoid(q_raw)
    q = q_raw * sq * (HEAD_DIM ** -0.5)
    return dict(r32=r32, sig=sig, f=f, k=1.0 - f, b=b, a=a, bl=bl, sq=sq, q=q)


def _chunk_masks(n):
    ri = lax.broadcasted_iota(jnp.int32, (n, n), 0)
    ci = lax.broadcasted_iota(jnp.int32, (n, n), 1)
    same = (ri // CHUNK) == (ci // CHUNK)
    return same & (ci <= ri), same & (ri <= ci)


def _hgrn_fwd(proj, lower_bounds, gamma, H):
    L = proj.shape[0]
    nh = H // HEAD_DIM
    TL = min(L, 256)
    nc = TL // CHUNK

    def body(q_ref, f_ref, v_ref, g_ref, lbp_ref, gam_ref, og_ref, o_ref, s_ref, st_ref):
        @pl.when(pl.program_id(0) == 0)
        def _():
            st_ref[...] = jnp.zeros_like(st_ref)

        lb = _lower_bound(lbp_ref[...])
        gam = gam_ref[...]
        mask, _ = _chunk_masks(TL)
        rowc = lax.broadcasted_iota(jnp.int32, (TL, HEAD_DIM), 0) // CHUNK
        for h in range(nh):
            hs = slice(h * HEAD_DIM, (h + 1) * HEAD_DIM)
            p = _hgrn_prep(q_ref[:, hs], f_ref[:, hs], lb[:, hs])
            v = v_ref[:, hs]
            vb = v.astype(BF16)
            vt = v.T.astype(BF16)
            q_hat = (p["q"] * jnp.exp(p["b"] - p["a"])).astype(BF16)
            k_hat = (p["k"] * jnp.exp(p["a"] - p["b"])).astype(BF16)
            q_in = (p["q"] * jnp.exp(p["b"])).astype(BF16)
            k_out = (p["k"] * jnp.exp(p["bl"] - p["b"])).astype(BF16)
            dec = jnp.exp(p["bl"])
            att = jnp.where(mask, _nt(q_hat, k_hat), 0.0).astype(BF16)
            o_intra = _nn(att, vb)
            st = st_ref[h]
            for c in range(nc):
                rs = slice(c * CHUNK, (c + 1) * CHUNK)
                stb = st.astype(BF16)
                s_ref[c, h] = stb
                o_ref[rs, hs] = o_intra[rs] + _nt(q_in[rs], stb)
                k_c = jnp.where(rowc == c, k_out, jnp.zeros_like(k_out))
                st = st * dec[c * CHUNK:c * CHUNK + 1, :] + _nn(vt, k_c)
            st_ref[h] = st
            o = o_ref[:, hs]
            _, xh = _rms_stats(o)
            gr = g_ref[:, hs]
            og_ref[:, hs] = (xh * gam * (gr * _sigmoid(gr))).astype(BF16)

    col = lambda k: pl.BlockSpec((TL, H), lambda i, k=k: (i, k))
    return pl.pallas_call(
        body, name="hgrn_fwd", grid=(L // TL,),
        in_specs=[col(0), col(1), col(2), col(3),
                  pl.BlockSpec(lower_bounds.shape, lambda i: (0, 0)),
                  pl.BlockSpec(gamma.shape, lambda i: (0, 0))],
        out_specs=[pl.BlockSpec((TL, H), lambda i: (i, 0)),
                   pl.BlockSpec((TL, H), lambda i: (i, 0)),
                   pl.BlockSpec((nc, nh, HEAD_DIM, HEAD_DIM), lambda i: (i, 0, 0, 0))],
        out_shape=[_sds((L, H), BF16), _sds((L, H), F32),
                   _sds((L // CHUNK, nh, HEAD_DIM, HEAD_DIM), BF16)],
        scratch_shapes=[pltpu.VMEM((nh, HEAD_DIM, HEAD_DIM), F32)],
        compiler_params=_params(("arbitrary",), 48),
    )(proj, proj, proj, proj, lower_bounds, gamma)


def _hgrn_bwd(proj, lower_bounds, gamma, o_pre, d_out, s_saved, H):
    L = proj.shape[0]
    nh = H // HEAD_DIM
    TL = min(L, 256)
    nc = TL // CHUNK
    nt = L // TL

    def body(q_ref, f_ref, v_ref, g_ref, lbp_ref, gam_ref, o_ref, d_ref, s_ref,
             dq_ref, df_ref, dv_ref, dg_ref, red_ref, dst_ref, dsall_ref, tmp_ref):
        @pl.when(pl.program_id(0) == 0)
        def _():
            dst_ref[...] = jnp.zeros_like(dst_ref)
            red_ref[...] = jnp.zeros_like(red_ref)

        lb = _lower_bound(lbp_ref[...])
        gam = gam_ref[...]
        mask, mask_t = _chunk_masks(TL)
        rowc = lax.broadcasted_iota(jnp.int32, (TL, HEAD_DIM), 0) // CHUNK
        for h in range(nh):
            hs = slice(h * HEAD_DIM, (h + 1) * HEAD_DIM)
            qr, gr, lbh = q_ref[:, hs], g_ref[:, hs], lb[:, hs]
            p = _hgrn_prep(qr, f_ref[:, hs], lbh)
            vb = v_ref[:, hs].astype(BF16)
            eba, eab = jnp.exp(p["b"] - p["a"]), jnp.exp(p["a"] - p["b"])
            eb, elb = jnp.exp(p["b"]), jnp.exp(p["bl"] - p["b"])
            dec = jnp.exp(p["bl"])
            q_hat, k_hat = p["q"] * eba, p["k"] * eab
            q_in, k_out = p["q"] * eb, p["k"] * elb
            q_hat_b, k_hat_b = q_hat.astype(BF16), k_hat.astype(BF16)
            q_in_b, k_out_b = q_in.astype(BF16), k_out.astype(BF16)

            o, dout = o_ref[:, hs], d_ref[:, hs]
            sg = _sigmoid(gr)
            r, xh = _rms_stats(o)
            dg_ref[:, hs] = (dout * (xh * gam) * (sg * (1.0 + gr * (1.0 - sg)))).astype(BF16)
            dn = dout * (gr * sg)
            red_ref[1:2, hs] += jnp.sum(dn * xh, axis=0, keepdims=True)
            do = _rms_bwd(dn * gam, xh, r)
            dob = do.astype(BF16)
            dot_b = do.T.astype(BF16)

            att_t = jnp.where(mask_t, _nt(k_hat_b, q_hat_b), 0.0).astype(BF16)
            dv_intra = _nn(att_t, dob)
            datt = jnp.where(mask, _nt(dob, vb), 0.0).astype(BF16)
            dqh = _nn(datt, k_hat_b)
            datt_t = jnp.where(mask_t, _nt(vb, dob), 0.0).astype(BF16)
            dkh = _nn(datt_t, q_hat_b)

            dst = dst_ref[h]
            for c in reversed(range(nc)):
                dsall_ref[c] = dst
                q_c = jnp.where(rowc == c, q_in_b, jnp.zeros_like(q_in_b))
                dst = dst * dec[c * CHUNK:c * CHUNK + 1, :] + _nn(dot_b, q_c)
            dst_ref[h] = dst
            for c in range(nc):
                rs = slice(c * CHUNK, (c + 1) * CHUNK)
                ds_c = dsall_ref[c]
                dsb = ds_c.astype(BF16)
                st_prev = s_ref[c, h]
                tmp_ref[0, rs, :] = _nt(k_out_b[rs], dsb)
                tmp_ref[1, rs, :] = _nn(vb[rs], dsb)
                tmp_ref[2, rs, :] = _nn(dob[rs], st_prev)
                ddec = jnp.sum(ds_c * st_prev.astype(F32), axis=0, keepdims=True)
                tmp_ref[3, rs, :] = jnp.broadcast_to(ddec * dec[c * CHUNK:c * CHUNK + 1, :],
                                                     (CHUNK, HEAD_DIM))
            dko, dqi = tmp_ref[1], tmp_ref[2]
            dq = dqh * eba + dqi * eb
            dk = dkh * eab + dko * elb
            tko = dko * k_out
            db = dqh * q_hat - dkh * k_hat + dqi * q_in - tko
            dlog = (_seg_scan(db, p["r32"], False)
                    + _bcast_row(_seg_scan(tko, p["r32"], True), CHUNK - 1) + tmp_ref[3])
            df = dlog / p["f"] - dk
            sig = p["sig"]
            red_ref[0:1, hs] += jnp.sum(df * (1.0 - sig), axis=0, keepdims=True)
            df_ref[:, hs] = (df * (1.0 - lbh) * sig * (1.0 - sig)).astype(BF16)
            sq = p["sq"]
            dq_ref[:, hs] = (dq * (HEAD_DIM ** -0.5) * (sq * (1.0 + qr * (1.0 - sq)))).astype(BF16)
            dv_ref[:, hs] = (dv_intra + tmp_ref[0]).astype(BF16)

    col = lambda k: pl.BlockSpec((TL, H), lambda i, k=k: (nt - 1 - i, k))
    rev = pl.BlockSpec((TL, H), lambda i: (nt - 1 - i, 0))
    return pl.pallas_call(
        body, name="hgrn_bwd", grid=(nt,),
        in_specs=[col(0), col(1), col(2), col(3),
                  pl.BlockSpec(lower_bounds.shape, lambda i: (0, 0)),
                  pl.BlockSpec(gamma.shape, lambda i: (0, 0)),
                  rev, rev,
                  pl.BlockSpec((nc, nh, HEAD_DIM, HEAD_DIM), lambda i: (nt - 1 - i, 0, 0, 0))],
        out_specs=[rev, rev, rev, rev, pl.BlockSpec((8, H), lambda i: (0, 0))],
        out_shape=[_sds((L, H), BF16)] * 4 + [_sds((8, H), F32)],
        scratch_shapes=[pltpu.VMEM((nh, HEAD_DIM, HEAD_DIM), F32),
                        pltpu.VMEM((nc, HEAD_DIM, HEAD_DIM), F32),
                        pltpu.VMEM((4, TL, HEAD_DIM), F32)],
        compiler_params=_params(("arbitrary",), 48),
    )(proj, proj, proj, proj, lower_bounds, gamma, o_pre, d_out, s_saved)


def _shift_down(u, s, row):
    return jnp.where(row >= s, pltpu.roll(u, s, 0), 0.0)


def _shift_up(u, s, row):
    n = u.shape[0]
    return jnp.where(row < n - s, pltpu.roll(u, n - s, 0), 0.0)


def _conv_specs(L, H):
    per = H // LANES
    return [pl.BlockSpec((L, LANES), lambda j, o=o: (0, o * per + j)) for o in (4, 5, 6)]


def _conv_fwd(proj, conv_w, H):
    L = proj.shape[0]

    def body(c_ref, b_ref, x_ref, w_ref, o_ref):
        row = lax.broadcasted_iota(jnp.int32, (L, LANES), 0)
        u = c_ref[...] * x_ref[...]
        w = w_ref[...]
        y = w[0:1] * _shift_down(u, 2, row) + w[1:2] * _shift_down(u, 1, row) + w[2:3] * u
        o_ref[...] = (b_ref[...] * y).astype(BF16)

    return pl.pallas_call(
        body, name="conv_fwd", grid=(H // LANES,),
        in_specs=_conv_specs(L, H) + [pl.BlockSpec((3, LANES), lambda j: (0, j))],
        out_specs=pl.BlockSpec((L, LANES), lambda j: (0, j)),
        out_shape=_sds((L, H), BF16),
        compiler_params=_params(("parallel",), 48),
    )(proj, proj, proj, conv_w)


def _conv_bwd(proj, conv_w, dcb, H):
    L = proj.shape[0]

    def body(c_ref, b_ref, x_ref, w_ref, d_ref, dc_ref, db_ref, dx_ref, dw_ref):
        row = lax.broadcasted_iota(jnp.int32, (L, LANES), 0)
        cg, xb = c_ref[...], x_ref[...]
        u = cg * xb
        u1, u2 = _shift_down(u, 1, row), _shift_down(u, 2, row)
        w = w_ref[...]
        y = w[0:1] * u2 + w[1:2] * u1 + w[2:3] * u
        d = d_ref[...]
        db_ref[...] = (d * y).astype(BF16)
        dy = d * b_ref[...]
        du = w[2:3] * dy + w[1:2] * _shift_up(dy, 1, row) + w[0:1] * _shift_up(dy, 2, row)
        dw_ref[0:1, :] = jnp.sum(dy * u2, axis=0, keepdims=True)
        dw_ref[1:2, :] = jnp.sum(dy * u1, axis=0, keepdims=True)
        dw_ref[2:3, :] = jnp.sum(dy * u, axis=0, keepdims=True)
        dc_ref[...] = (du * xb).astype(BF16)
        dx_ref[...] = (du * cg).astype(BF16)

    blk = pl.BlockSpec((L, LANES), lambda j: (0, j))
    return pl.pallas_call(
        body, name="conv_bwd", grid=(H // LANES,),
        in_specs=_conv_specs(L, H) + [pl.BlockSpec((3, LANES), lambda j: (0, j)), blk],
        out_specs=[blk, blk, blk, pl.BlockSpec((3, LANES), lambda j: (0, j))],
        out_shape=[_sds((L, H), BF16)] * 3 + [_sds((3, H), F32)],
        compiler_params=_params(("parallel",), 56),
    )(proj, proj, proj, conv_w, dcb)


def _gate_specs(tm, H):
    return [pl.BlockSpec((tm, H), lambda i, k=k: (i, k)) for k in (7, 8, 9, 10)]


def _fwd_mix(og, cb, proj, x, wa3, wb3, wout, g_ffn, H):
    L, D = x.shape
    sw = wa3.shape[2]
    tm = min(L, 256)

    def body(o_ref, cb_ref, ga0, ga1, gb0, gb1, x_ref, wa_ref, wb_ref, wo_ref, g_ref,
             ya_ref, yb_ref, m_ref, x1_ref, h2_ref):
        o, cbv = o_ref[...], cb_ref[...]
        for k in range(N_CHIPS):
            cs = slice(k * sw, (k + 1) * sw)
            gs = slice((k % 2) * sw, (k % 2 + 1) * sw)
            gar, gbr = (ga0, ga1)[k // 2], (gb0, gb1)[k // 2]
            ya, yb = _nn(o, wa_ref[k]), _nn(cbv, wb_ref[k])
            ya_ref[:, cs] = ya.astype(BF16)
            yb_ref[:, cs] = yb.astype(BF16)
            m_ref[:, cs] = (_sigmoid(gar[:, gs]) * ya + _sigmoid(gbr[:, gs]) * yb).astype(BF16)
        x1 = x_ref[...] + _nn(m_ref[...], wo_ref[...])
        x1_ref[...] = x1
        _, xh = _rms_stats(x1)
        h2_ref[...] = (xh * g_ref[...]).astype(BF16)

    row = lambda w: pl.BlockSpec((tm, w), lambda i: (i, 0))
    full = lambda a: pl.BlockSpec(a.shape, lambda i: (0,) * a.ndim)
    return pl.pallas_call(
        body, name="fwd_mix", grid=(L // tm,),
        in_specs=[row(H), row(H)] + _gate_specs(tm, H) + [row(D), full(wa3), full(wb3), full(wout),
                                                           full(g_ffn)],
        out_specs=[row(D)] * 5,
        out_shape=[_sds((L, D), BF16)] * 3 + [_sds((L, D), F32), _sds((L, D), BF16)],
        compiler_params=_params(("parallel",), 48),
    )(og, cb, proj, proj, proj, proj, x, wa3, wb3, wout, g_ffn)


def _bwd_mix(dx1b, proj, ya, yb, wa3, wb3, wout, H):
    L, D = dx1b.shape
    sw = wa3.shape[2]
    tm = min(L, 256)

    def body(dx_ref, ga0, ga1, gb0, gb1, ya_ref, yb_ref, wa_ref, wb_ref, wo_ref,
             dya_ref, dyb_ref, dga_ref, dgb_ref, do_ref, dcb_ref):
        dm = _nt(dx_ref[...], wo_ref[...])
        do = jnp.zeros((tm, H), F32)
        dcb = jnp.zeros((tm, H), F32)
        for k in range(N_CHIPS):
            cs = slice(k * sw, (k + 1) * sw)
            gs = slice((k % 2) * sw, (k % 2 + 1) * sw)
            gar, gbr = (ga0, ga1)[k // 2], (gb0, gb1)[k // 2]
            sa, sb = _sigmoid(gar[:, gs]), _sigmoid(gbr[:, gs])
            dmk = dm[:, cs]
            dga_ref[:, cs] = (dmk * ya_ref[:, cs].astype(F32) * sa * (1.0 - sa)).astype(BF16)
            dgb_ref[:, cs] = (dmk * yb_ref[:, cs].astype(F32) * sb * (1.0 - sb)).astype(BF16)
            dya, dyb = (dmk * sa).astype(BF16), (dmk * sb).astype(BF16)
            dya_ref[:, cs] = dya
            dyb_ref[:, cs] = dyb
            do = do + _nt(dya, wa_ref[k])
            dcb = dcb + _nt(dyb, wb_ref[k])
        do_ref[...] = do
        dcb_ref[...] = dcb

    row = lambda w: pl.BlockSpec((tm, w), lambda i: (i, 0))
    full = lambda a: pl.BlockSpec(a.shape, lambda i: (0,) * a.ndim)
    return pl.pallas_call(
        body, name="bwd_mix", grid=(L // tm,),
        in_specs=[row(D)] + _gate_specs(tm, H) + [row(D), row(D), full(wa3), full(wb3), full(wout)],
        out_specs=[row(D)] * 4 + [row(H)] * 2,
        out_shape=[_sds((L, D), BF16)] * 4 + [_sds((L, H), F32)] * 2,
        compiler_params=_params(("parallel",), 48),
    )(dx1b, proj, proj, proj, proj, ya, yb, wa3, wb3, wout)


def _fwd_ffn_up(h2, wg3, wu3):
    L, D = h2.shape
    fw = wg3.shape[2]
    tm = min(L, 1024)

    def body(h_ref, wg_ref, wu_ref, a_ref, b_ref, s_ref):
        h = h_ref[...]
        a, b = _nn(h, wg_ref[...]), _nn(h, wu_ref[...])
        a_ref[...] = a.astype(BF16)
        b_ref[...] = b.astype(BF16)
        s_ref[...] = (a * _sigmoid(a) * b).astype(BF16)

    wspec = pl.BlockSpec((None, D, fw), lambda i, j: (j, 0, 0))
    ospec = pl.BlockSpec((None, tm, fw), lambda i, j: (j, i, 0))
    return pl.pallas_call(
        body, name="fwd_ffn_up", grid=(L // tm, N_CHIPS),
        in_specs=[pl.BlockSpec((tm, D), lambda i, j: (i, 0)), wspec, wspec],
        out_specs=[ospec] * 3,
        out_shape=[_sds((N_CHIPS, L, fw), BF16)] * 3,
        compiler_params=_params(("parallel", "arbitrary"), 48),
    )(h2, wg3, wu3)


def _fwd_down_loss(s3, wd3, x1, target, g_final):
    L, D = x1.shape
    fw = wd3.shape[1]
    tm = min(L, 512)
    last = N_CHIPS - 1

    def body(s_ref, wd_ref, x1_ref, t_ref, g_ref, dx_ref, dxb_ref, red_ref, acc_ref):
        i, k = pl.program_id(0), pl.program_id(1)

        @pl.when(k == 0)
        def _():
            acc_ref[...] = x1_ref[...]

        acc_ref[...] += _nn(s_ref[...], wd_ref[...])

        @pl.when((i == 0) & (k == last))
        def _():
            red_ref[...] = jnp.zeros_like(red_ref)

        @pl.when(k == last)
        def _():
            g = g_ref[...]
            r, xh = _rms_stats(acc_ref[...])
            e = xh * g - t_ref[...]
            dy = e * (1.0 / D)
            dx = _rms_bwd(dy * g, xh, r)
            dx_ref[...] = dx
            dxb_ref[...] = dx.astype(BF16)
            red_ref[0:1, :] += jnp.sum(dy * xh, axis=0, keepdims=True)
            red_ref[1:2, :] += jnp.broadcast_to(0.5 * jnp.sum(e * e) * (1.0 / D), (1, D))

    row = pl.BlockSpec((tm, D), lambda i, k: (i, 0))
    return pl.pallas_call(
        body, name="fwd_down_loss", grid=(L // tm, N_CHIPS),
        in_specs=[pl.BlockSpec((None, tm, fw), lambda i, k: (k, i, 0)),
                  pl.BlockSpec((None, fw, D), lambda i, k: (k, 0, 0)),
                  row, row, pl.BlockSpec((1, D), lambda i, k: (0, 0))],
        out_specs=[row, row, pl.BlockSpec((8, D), lambda i, k: (0, 0))],
        out_shape=[_sds((L, D), F32), _sds((L, D), BF16), _sds((8, D), F32)],
        scratch_shapes=[pltpu.VMEM((tm, D), F32)],
        compiler_params=_params(("arbitrary", "arbitrary"), 48),
    )(s3, wd3, x1, target, g_final)


def _bwd_down(dx2b, wd3, a3, b3):
    L, D = dx2b.shape
    fw = wd3.shape[1]
    tm = min(L, 1024)

    def body(dx_ref, wd_ref, a_ref, b_ref, da_ref, db_ref):
        ds = _nt(dx_ref[...], wd_ref[...])
        a, b = a_ref[...].astype(F32), b_ref[...].astype(F32)
        sg = _sigmoid(a)
        da_ref[...] = (ds * b * sg * (1.0 + a * (1.0 - sg))).astype(BF16)
        db_ref[...] = (ds * a * sg).astype(BF16)

    ospec = pl.BlockSpec((None, tm, fw), lambda i, j: (j, i, 0))
    return pl.pallas_call(
        body, name="bwd_down", grid=(L // tm, N_CHIPS),
        in_specs=[pl.BlockSpec((tm, D), lambda i, j: (i, 0)),
                  pl.BlockSpec((None, fw, D), lambda i, j: (j, 0, 0)), ospec, ospec],
        out_specs=[ospec] * 2,
        out_shape=[_sds((N_CHIPS, L, fw), BF16)] * 2,
        compiler_params=_params(("parallel", "arbitrary"), 48),
    )(dx2b, wd3, a3, b3)


def _bwd_ffn_dh(da3, db3, wg3, wu3, x1, dx2, g_ffn):
    L, D = x1.shape
    fw = wg3.shape[2]
    tm = min(L, 512)
    last = N_CHIPS - 1

    def body(da_ref, db_ref, wg_ref, wu_ref, x1_ref, dx2_ref, g_ref, dx_ref, dxb_ref, red_ref,
             acc_ref):
        i, k = pl.program_id(0), pl.program_id(1)
        part = _nt(da_ref[...], wg_ref[...]) + _nt(db_ref[...], wu_ref[...])

        @pl.when(k == 0)
        def _():
            acc_ref[...] = part

        @pl.when(k > 0)
        def _():
            acc_ref[...] += part

        @pl.when((i == 0) & (k == last))
        def _():
            red_ref[...] = jnp.zeros_like(red_ref)

        @pl.when(k == last)
        def _():
            dh = acc_ref[...]
            r, xh = _rms_stats(x1_ref[...])
            red_ref[0:1, :] += jnp.sum(dh * xh, axis=0, keepdims=True)
            dx = dx2_ref[...] + _rms_bwd(dh * g_ref[...], xh, r)
            dx_ref[...] = dx
            dxb_ref[...] = dx.astype(BF16)

    row = pl.BlockSpec((tm, D), lambda i, k: (i, 0))
    aspec = pl.BlockSpec((None, tm, fw), lambda i, k: (k, i, 0))
    wspec = pl.BlockSpec((None, D, fw), lambda i, k: (k, 0, 0))
    return pl.pallas_call(
        body, name="bwd_ffn_dh", grid=(L // tm, N_CHIPS),
        in_specs=[aspec, aspec, wspec, wspec, row, row, pl.BlockSpec((1, D), lambda i, k: (0, 0))],
        out_specs=[row, row, pl.BlockSpec((8, D), lambda i, k: (0, 0))],
        out_shape=[_sds((L, D), F32), _sds((L, D), BF16), _sds((8, D), F32)],
        scratch_shapes=[pltpu.VMEM((tm, D), F32)],
        compiler_params=_params(("arbitrary", "arbitrary"), 48),
    )(da3, db3, wg3, wu3, x1, dx2, g_ffn)


def _bwd_in(dproj, w_in3, x, dx1, g_mix):
    L, D = x.shape
    tn = w_in3.shape[2]
    tm = min(L, 512)
    last = N_CHIPS - 1

    def body(dp_ref, w_ref, x_ref, dx1_ref, g_ref, dx_ref, red_ref, acc_ref):
        i, k = pl.program_id(0), pl.program_id(1)
        part = _nt(dp_ref[...], w_ref[...])

        @pl.when(k == 0)
        def _():
            acc_ref[...] = part

        @pl.when(k > 0)
        def _():
            acc_ref[...] += part

        @pl.when((i == 0) & (k == last))
        def _():
            red_ref[...] = jnp.zeros_like(red_ref)

        @pl.when(k == last)
        def _():
            dh = acc_ref[...]
            r, xh = _rms_stats(x_ref[...])
            red_ref[0:1, :] += jnp.sum(dh * xh, axis=0, keepdims=True)
            dx_ref[...] = dx1_ref[...] + _rms_bwd(dh * g_ref[...], xh, r)

    row = pl.BlockSpec((tm, D), lambda i, k: (i, 0))
    return pl.pallas_call(
        body, name="bwd_in", grid=(L // tm, N_CHIPS),
        in_specs=[pl.BlockSpec((tm, tn), lambda i, k: (i, k)),
                  pl.BlockSpec((None, D, tn), lambda i, k: (k, 0, 0)),
                  row, row, pl.BlockSpec((1, D), lambda i, k: (0, 0))],
        out_specs=[row, pl.BlockSpec((8, D), lambda i, k: (0, 0))],
        out_shape=[_sds((L, D), F32), _sds((8, D), F32)],
        scratch_shapes=[pltpu.VMEM((tm, D), F32)],
        compiler_params=_params(("arbitrary", "arbitrary"), 48),
    )(dproj, w_in3, x, dx1, g_mix)


def _mm_tn(name, a, b, a_spec, b_spec, o_block, n_out, n_k):
    def body(a_ref, b_ref, o_ref):
        part = _tn(a_ref[...], b_ref[...])

        @pl.when(pl.program_id(1) == 0)
        def _():
            o_ref[...] = part

        @pl.when(pl.program_id(1) > 0)
        def _():
            o_ref[...] += part

    return pl.pallas_call(
        body, name=name, grid=(n_out, n_k),
        in_specs=[a_spec, b_spec],
        out_specs=pl.BlockSpec((None,) + o_block, lambda j, k: (j, 0, 0)),
        out_shape=_sds((n_out,) + o_block, F32),
        compiler_params=_params(("parallel", "arbitrary"), 48),
    )(a, b)


def _dw_cols(name, a, b, n_cols):
    L, M = a.shape
    tk = min(L, 512)
    return _mm_tn(name, a, b, pl.BlockSpec((tk, M), lambda j, k: (k, 0)),
                  pl.BlockSpec((tk, n_cols), lambda j, k: (k, j)), (M, n_cols), N_CHIPS, L // tk)


def _dw_cols3(name, a, b3):
    L, M = a.shape
    fw = b3.shape[2]
    tk = min(L, 512)
    return _mm_tn(name, a, b3, pl.BlockSpec((tk, M), lambda j, k: (k, 0)),
                  pl.BlockSpec((None, tk, fw), lambda j, k: (j, k, 0)), (M, fw), N_CHIPS, L // tk)


def _dw_rows(name, a, b):
    L, M = a.shape
    N = b.shape[1]
    tk = min(L, 512)
    return _mm_tn(name, a, b, pl.BlockSpec((tk, M // N_CHIPS), lambda j, k: (k, j)),
                  pl.BlockSpec((tk, N), lambda j, k: (k, 0)), (M // N_CHIPS, N), N_CHIPS, L // tk)


def _dw_rows3(name, a3, b):
    _, L, fw = a3.shape
    N = b.shape[1]
    tk = min(L, 512)
    return _mm_tn(name, a3, b, pl.BlockSpec((None, tk, fw), lambda j, k: (j, k, 0)),
                  pl.BlockSpec((tk, N), lambda j, k: (k, 0)), (fw, N), N_CHIPS, L // tk)


def _place():
    x, y, c = lax.axis_index("x"), lax.axis_index("y"), lax.axis_index("c")
    chips = [(1 - x, y), (x, 1 - y), (1 - x, 1 - y)]
    return x, y, c, 2 * x + y, chips


def _remote(src, dst, send_sem, recv_sem, device):
    return pltpu.make_async_remote_copy(src_ref=src, dst_ref=dst, send_sem=send_sem,
                                        recv_sem=recv_sem, device_id=device, device_id_type=MESH)


def _half(ref, lead, c, r2):
    return ref.at[lead, pl.ds(pl.multiple_of(c * r2, 16), r2), :]


def _cast_place(name, w, chip_idx):
    r, cols = w.shape
    tr = r // 2

    def body(k_ref, w_ref, o_ref):
        o_ref[...] = w_ref[...].astype(BF16)

    return pl.pallas_call(
        body, name=name,
        grid_spec=pltpu.PrefetchScalarGridSpec(
            num_scalar_prefetch=1, grid=(2,),
            in_specs=[pl.BlockSpec((tr, cols), lambda i, k_ref: (i, 0))],
            out_specs=pl.BlockSpec((None, tr, cols), lambda i, k_ref: (k_ref[0], i, 0))),
        out_shape=_sds((N_CHIPS, r, cols), BF16),
        compiler_params=_params(("parallel",), 48),
    )(chip_idx, w)


def _gather_weights(bufs, conv_buf):
    n = len(bufs)

    def body(*refs):
        outs, conv_out = refs[n + 1:2 * n + 1], refs[2 * n + 1]
        send_sems, recv_sems = refs[2 * n + 2:]
        x, y, c, k, chips = _place()
        me, sibling = (x, y, c), (x, y, 1 - c)
        sends, forwards = [], []
        for w in range(n):
            r2 = outs[w].shape[1] // 2
            mine = _half(outs[w], k, c, r2)
            for j, chip in enumerate(chips):
                sends.append(_remote(mine, mine, send_sems.at[w * 6 + j], recv_sems.at[w * 6 + j],
                                     (*chip, c)))
        for j, chip in enumerate(chips):
            sends.append(_remote(conv_out.at[k], conv_out.at[k], send_sems.at[n * 6 + j],
                                 recv_sems.at[n * 6 + j], (*chip, c)))
        for cp in sends:
            cp.start()
        for w in range(n):
            r2 = outs[w].shape[1] // 2
            for j, (cx, cy) in enumerate(chips):
                landed = _half(outs[w], 2 * cx + cy, c, r2)
                _remote(landed, landed, send_sems.at[w * 6 + j], recv_sems.at[w * 6 + j], me).wait_recv()
                fwd = _remote(landed, landed, send_sems.at[w * 6 + 3 + j], recv_sems.at[w * 6 + 3 + j],
                              sibling)
                fwd.start()
                forwards.append(fwd)
        for j, (cx, cy) in enumerate(chips):
            got = conv_out.at[2 * cx + cy]
            _remote(got, got, send_sems.at[n * 6 + j], recv_sems.at[n * 6 + j], me).wait_recv()
        for w in range(n):
            r2 = outs[w].shape[1] // 2
            for j, (cx, cy) in enumerate(chips):
                got = _half(outs[w], 2 * cx + cy, 1 - c, r2)
                _remote(got, got, send_sems.at[w * 6 + 3 + j], recv_sems.at[w * 6 + 3 + j], me).wait_recv()
        for cp in sends + forwards:
            cp.wait_send()

    return pl.pallas_call(
        body, name="gather_w",
        in_specs=[ANY] * (n + 1), out_specs=[ANY] * (n + 1),
        out_shape=[_sds(b.shape, b.dtype) for b in bufs] + [_sds(conv_buf.shape, conv_buf.dtype)],
        input_output_aliases={i: i for i in range(n + 1)},
        scratch_shapes=[pltpu.SemaphoreType.DMA((n * 6 + 3,)), pltpu.SemaphoreType.DMA((n * 6 + 3,))],
    )(*bufs, conv_buf)


def _rs_sibling(grads):
    n = len(grads)

    def body(*refs):
        ins, outs = refs[:n], refs[n:2 * n]
        send_sems, recv_sems = refs[2 * n:]
        x, y, c, _, _ = _place()
        copies = []
        for w in range(n):
            r2 = ins[w].shape[1] // 2
            copies.append(_remote(_half(ins[w], slice(None), 1 - c, r2), outs[w],
                                  send_sems.at[w], recv_sems.at[w], (x, y, 1 - c)))
        for cp in copies:
            cp.start()
        for cp in copies:
            cp.wait()

    return pl.pallas_call(
        body, name="rs_sibling",
        in_specs=[ANY] * n, out_specs=[ANY] * n,
        out_shape=[_sds((N_CHIPS, g.shape[1] // 2, g.shape[2]), F32) for g in grads],
        scratch_shapes=[pltpu.SemaphoreType.DMA((n,)), pltpu.SemaphoreType.DMA((n,))],
    )(*grads)


def _rs_add(name, grad3, from_sibling, c_idx):
    _, r2, cols = from_sibling.shape

    def body(c_ref, g_ref, s_ref, o_ref):
        o_ref[...] = (g_ref[...] + s_ref[...]).astype(BF16)

    return pl.pallas_call(
        body, name=name,
        grid_spec=pltpu.PrefetchScalarGridSpec(
            num_scalar_prefetch=1, grid=(N_CHIPS,),
            in_specs=[pl.BlockSpec((None, r2, cols), lambda k, c_ref: (k, c_ref[0], 0)),
                      pl.BlockSpec((None, r2, cols), lambda k, c_ref: (k, 0, 0))],
            out_specs=pl.BlockSpec((None, r2, cols), lambda k, c_ref: (k, 0, 0))),
        out_shape=_sds(from_sibling.shape, BF16),
        compiler_params=_params(("parallel",), 48),
    )(c_idx, grad3, from_sibling)


def _rs_ici(partials):
    n = len(partials)

    def body(*refs):
        ins, outs = refs[:n], refs[n:2 * n]
        send_sems, recv_sems = refs[2 * n:]
        x, y, c, k, chips = _place()
        sends = []
        for w in range(n):
            for j, (cx, cy) in enumerate(chips):
                sends.append(_remote(ins[w].at[2 * cx + cy], outs[w].at[j],
                                     send_sems.at[w * 3 + j], recv_sems.at[w * 3 + j], (cx, cy, c)))
        for cp in sends:
            cp.start()
        for cp in sends:
            cp.wait()

    return pl.pallas_call(
        body, name="rs_ici",
        in_specs=[ANY] * n, out_specs=[ANY] * n,
        out_shape=[_sds((3,) + p.shape[1:], BF16) for p in partials],
        scratch_shapes=[pltpu.SemaphoreType.DMA((n * 3,)), pltpu.SemaphoreType.DMA((n * 3,))],
    )(*partials)


def _rs_sum(name, partials, received, place_idx):
    _, r2, cols = partials.shape
    nb = 2
    tr = r2 // nb

    def body(idx_ref, p_ref, r_ref, o_ref):
        o_ref[...] = ((p_ref[...].astype(F32) + r_ref[0].astype(F32))
                      + (r_ref[1].astype(F32) + r_ref[2].astype(F32)))

    return pl.pallas_call(
        body, name=name,
        grid_spec=pltpu.PrefetchScalarGridSpec(
            num_scalar_prefetch=1, grid=(nb,),
            in_specs=[pl.BlockSpec((None, tr, cols), lambda i, idx: (idx[0], i, 0)),
                      pl.BlockSpec((3, tr, cols), lambda i, idx: (0, i, 0))],
            out_specs=pl.BlockSpec((tr, cols), lambda i, idx: (idx[1] * nb + i, 0))),
        out_shape=_sds((2 * r2, cols), F32),
        compiler_params=_params(("parallel",), 48),
    )(place_idx, partials, received)


def _rs_share(shards):
    n = len(shards)

    def body(*refs):
        outs = refs[n:2 * n]
        send_sems, recv_sems = refs[2 * n:]
        x, y, c, _, _ = _place()
        sends = []
        for w in range(n):
            r2 = outs[w].shape[0] // 2
            mine = outs[w].at[pl.ds(pl.multiple_of(c * r2, 8), r2), :]
            sends.append(_remote(mine, mine, send_sems.at[w], recv_sems.at[w], (x, y, 1 - c)))
        for cp in sends:
            cp.start()
        for w in range(n):
            r2 = outs[w].shape[0] // 2
            theirs = outs[w].at[pl.ds(pl.multiple_of((1 - c) * r2, 8), r2), :]
            _remote(theirs, theirs, send_sems.at[w], recv_sems.at[w], (x, y, c)).wait_recv()
        for cp in sends:
            cp.wait_send()

    return pl.pallas_call(
        body, name="rs_share",
        in_specs=[ANY] * n, out_specs=[ANY] * n,
        out_shape=[_sds(s.shape, F32) for s in shards],
        input_output_aliases={i: i for i in range(n)},
        scratch_shapes=[pltpu.SemaphoreType.DMA((n,)), pltpu.SemaphoreType.DMA((n,))],
    )(*shards)


def _small_allreduce(block):
    rows, D = block.shape

    def body(in_ref, sum_ref, all_ref, send_sems, recv_sems):
        x, y, c, _, _ = _place()
        me = 4 * x + 2 * y + c
        all_ref[me] = in_ref[...]
        copies = []
        for m in range(1, 8):
            mx, my, mc = (m >> 2) & 1, (m >> 1) & 1, m & 1
            px, py, pc = x ^ mx, y ^ my, c ^ mc
            copies.append((_remote(in_ref, all_ref.at[me], send_sems.at[m - 1], recv_sems.at[m - 1],
                                   (px, py, pc)), 4 * px + 2 * py + pc, m))
        for cp, _, _ in copies:
            cp.start()
        for _, peer, m in copies:
            _remote(in_ref, all_ref.at[peer], send_sems.at[m - 1], recv_sems.at[m - 1],
                    (x, y, c)).wait_recv()
        for cp, _, _ in copies:
            cp.wait_send()
        total = all_ref[0]
        for d in range(1, 8):
            total = total + all_ref[d]
        sum_ref[...] = total

    return pl.pallas_call(
        body, name="small_allreduce",
        in_specs=[VMEM], out_specs=[VMEM, VMEM],
        out_shape=[_sds((rows, D), F32), _sds((8, rows, D), F32)],
        scratch_shapes=[pltpu.SemaphoreType.DMA((7,)), pltpu.SemaphoreType.DMA((7,))],
    )(block)[0]


def _adamw_math(w, g, m, v):
    m = ADAM_B1 * m + (1.0 - ADAM_B1) * g
    v = ADAM_B2 * v + (1.0 - ADAM_B2) * jnp.square(g)
    m_hat = m / (1.0 - ADAM_B1 ** ADAM_STEP)
    v_hat = v / (1.0 - ADAM_B2 ** ADAM_STEP)
    delta = -ADAM_LR * (m_hat / (jnp.sqrt(v_hat) + ADAM_EPS) + ADAM_WD * w)
    return delta, m, v


def _adamw(name, g, w, m, v):
    r, cols = g.shape
    tr = r // 4

    def body(g_ref, w_ref, m_ref, v_ref, d_ref, mo_ref, vo_ref):
        d_ref[...], mo_ref[...], vo_ref[...] = _adamw_math(w_ref[...], g_ref[...], m_ref[...], v_ref[...])

    blk = pl.BlockSpec((tr, cols), lambda i: (i, 0))
    return pl.pallas_call(
        body, name=name, grid=(r // tr,),
        in_specs=[blk] * 4, out_specs=[blk] * 3, out_shape=[_sds((r, cols), F32)] * 3,
        compiler_params=_params(("parallel",), 48),
    )(g, w, m, v)


def _small_update(gsum, wp, mp, vp, lower_bounds):
    rows, D = gsum.shape
    H = lower_bounds.shape[1]

    def body(s_ref, w_ref, m_ref, v_ref, lbp_ref, g_ref, d_ref, mo_ref, vo_ref):
        g_ref[...] = s_ref[...]
        p0 = _lower_bound(lbp_ref[...])
        dl0 = p0 * (1.0 - p0) * s_ref[4:5, 0:H]
        g_ref[4:5, 0:H] = dl0
        g_ref[5:6, 0:H] = -dl0
        d_ref[...], mo_ref[...], vo_ref[...] = _adamw_math(w_ref[...], g_ref[...], m_ref[...], v_ref[...])

    return pl.pallas_call(
        body, name="small_update",
        in_specs=[VMEM] * 5, out_specs=[VMEM] * 4, out_shape=[_sds((rows, D), F32)] * 4,
    )(gsum, wp, mp, vp, lower_bounds)


def _pack_rows(rows, D):
    padded = [jnp.pad(r.reshape(1, -1), ((0, 0), (0, D - r.size))) for r in rows]
    padded.append(jnp.zeros((N_SMALL_ROWS - len(rows), D), F32))
    return jnp.concatenate(padded, axis=0)


def kernel(x, norm_mix_g, w_in, lower_bounds, hg_norm_g, conv_w, w_branch_a, w_branch_b, w_out, norm_ffn_g, w_ffn_gate, w_ffn_up, w_ffn_down, norm_final_g, loss_target, m_norm_mix_g, m_w_in, m_lower_bounds, m_hg_norm_g, m_conv_w, m_w_branch_a, m_w_branch_b, m_w_out, m_norm_ffn_g, m_w_ffn_gate, m_w_ffn_up, m_w_ffn_down, m_norm_final_g, v_norm_mix_g, v_w_in, v_lower_bounds, v_hg_norm_g, v_conv_w, v_w_branch_a, v_w_branch_b, v_w_out, v_norm_ffn_g, v_w_ffn_gate, v_w_ffn_up, v_w_ffn_down, v_norm_final_g):
    _, L, D = x.shape
    H = D // 2
    assert lower_bounds.shape == (2, H) and hg_norm_g.shape == (1, HEAD_DIM)
    assert conv_w.shape == (1, 3, LANES) and w_in.shape[2] * N_CHIPS == 11 * H
    x2d, target = x.reshape(L, D), loss_target.reshape(L, D)
    g_final = norm_final_g.reshape(1, D)
    chip = 2 * lax.axis_index("x") + lax.axis_index("y")
    core = lax.axis_index("c")

    big = [w_in, w_branch_a, w_branch_b, w_out, w_ffn_gate, w_ffn_up, w_ffn_down]
    big_m = [m_w_in, m_w_branch_a, m_w_branch_b, m_w_out, m_w_ffn_gate, m_w_ffn_up, m_w_ffn_down]
    big_v = [v_w_in, v_w_branch_a, v_w_branch_b, v_w_out, v_w_ffn_gate, v_w_ffn_up, v_w_ffn_down]
    names = ["w_in", "w_branch_a", "w_branch_b", "w_out", "w_ffn_gate", "w_ffn_up", "w_ffn_down"]

    chip_idx = chip.reshape(1).astype(jnp.int32)
    placed = [_cast_place("place_" + nm, w[0], chip_idx) for nm, w in zip(names, big)]
    conv_placed = lax.dynamic_update_slice(jnp.zeros((N_CHIPS, 3, LANES), F32), conv_w, (chip, 0, 0))
    gathered = _gather_weights(placed, conv_placed)
    w_in3, wa3, wb3, wout3, wg3, wu3, wd3 = gathered[:7]
    conv_full = jnp.transpose(gathered[7], (1, 0, 2)).reshape(3, H)
    wout = wout3.reshape(D, D)

    h, proj = _fwd_proj(x2d, norm_mix_g, w_in3)
    og, o_pre, s_saved = _hgrn_fwd(proj, lower_bounds, hg_norm_g, H)
    cb = _conv_fwd(proj, conv_full, H)
    ya, yb, merged, x1, h2 = _fwd_mix(og, cb, proj, x2d, wa3, wb3, wout, norm_ffn_g, H)
    a3, b3, s3 = _fwd_ffn_up(h2, wg3, wu3)
    dx2, dx2b, red_final = _fwd_down_loss(s3, wd3, x1, target, g_final)

    da3, db3 = _bwd_down(dx2b, wd3, a3, b3)
    g_wd = _dw_rows3("dw_ffn_down", s3, dx2b)
    dx1, dx1b, red_ffn = _bwd_ffn_dh(da3, db3, wg3, wu3, x1, dx2, norm_ffn_g)
    g_wg = _dw_cols3("dw_ffn_gate", h2, da3)
    g_wu = _dw_cols3("dw_ffn_up", h2, db3)
    dya, dyb, dga, dgb, d_o, d_cb = _bwd_mix(dx1b, proj, ya, yb, wa3, wb3, wout, H)
    g_wout = _dw_rows("dw_out", merged, dx1b)
    g_wa = _dw_cols("dw_branch_a", og, dya, D // N_CHIPS)
    g_wb = _dw_cols("dw_branch_b", cb, dyb, D // N_CHIPS)
    dcg, dbg, dxb, g_conv = _conv_bwd(proj, conv_full, d_cb, H)
    dq, df, dv, dg, red_hg = _hgrn_bwd(proj, lower_bounds, hg_norm_g, o_pre, d_o, s_saved, H)
    dproj = jnp.concatenate([dq, df, dv, dg, dcg, dbg, dxb, dga, dgb], axis=1)
    grad_x, red_mix = _bwd_in(dproj, w_in3, x2d, dx1, norm_mix_g)
    g_win = _dw_cols("dw_in", h, dproj, w_in3.shape[2])

    grads3 = [g_win, g_wa, g_wb, g_wout, g_wg, g_wu, g_wd]
    c_idx = core.reshape(1).astype(jnp.int32)
    from_sib = _rs_sibling(grads3)
    partials = [_rs_add("rs_add_" + nm, g, s, c_idx) for nm, g, s in zip(names, grads3, from_sib)]
    received = _rs_ici(partials)
    place_idx = jnp.stack([chip, core]).astype(jnp.int32)
    halves = [_rs_sum("rs_sum_" + nm, p, r, place_idx) for nm, p, r in zip(names, partials, received)]
    shard_grads = _rs_share(halves)
    big_out = [_adamw("adamw_" + nm, g, w[0], m[0], v[0])
               for nm, g, w, m, v in zip(names, shard_grads, big, big_m, big_v)]

    hg_row = jnp.concatenate([red_hg[1].reshape(-1, HEAD_DIM).sum(axis=0), red_final[1, :LANES]])
    small = _pack_rows([red_mix[0], red_ffn[0], red_final[0], hg_row, red_hg[0], jnp.zeros((H,), F32),
                        g_conv[0], g_conv[1], g_conv[2]], D)
    total = _small_allreduce(small)
    conv_mine = lax.dynamic_slice(total, (6, chip * LANES), (3, LANES))
    gsum = jnp.concatenate([total[:3], jnp.pad(total[3:4, :HEAD_DIM], ((0, 0), (0, D - HEAD_DIM))),
                            total[4:6], jnp.pad(conv_mine, ((0, 0), (0, D - LANES))), total[9:]], axis=0)

    def pack(ps):
        mix, lb, hg, cw, ffn, fin = ps
        return _pack_rows([mix[0], ffn[0], fin, hg[0], lb[0], lb[1], cw[0, 0], cw[0, 1], cw[0, 2]], D)

    wp = pack((norm_mix_g, lower_bounds, hg_norm_g, conv_w, norm_ffn_g, norm_final_g))
    mp = pack((m_norm_mix_g, m_lower_bounds, m_hg_norm_g, m_conv_w, m_norm_ffn_g, m_norm_final_g))
    vp = pack((v_norm_mix_g, v_lower_bounds, v_hg_norm_g, v_conv_w, v_norm_ffn_g, v_norm_final_g))
    small_out = _small_update(gsum, wp, mp, vp, lower_bounds)

    def unpack(p, i):
        b = lambda j: (shard_grads[j] if i == 0 else big_out[j][i - 1])[None]
        return [p[0:1], b(0), p[4:6, :H], p[3:4, :HEAD_DIM], p[6:9, :LANES][None], b(1), b(2), b(3),
                p[1:2], b(4), b(5), b(6), p[2]]

    loss = total[3, HEAD_DIM]
    outs = [loss, grad_x.reshape(1, L, D)]
    for i in range(4):
        outs += unpack(small_out[i], i)
    return tuple(outs)
```

```python
import functools

import jax
import jax.numpy as jnp
from jax import lax
from jax.experimental import pallas as pl
from jax.experimental.pallas import tpu as pltpu

F32 = jnp.float32
BF16 = jnp.bfloat16
EPS = 1e-6
CHUNK = 32
HEAD_DIM = 128
LANES = 128
N_CHIPS = 4
N_SMALL_ROWS = 16

ADAM_LR = 0.001
ADAM_B1 = 0.9
ADAM_B2 = 0.999
ADAM_EPS = 1e-08
ADAM_WD = 0.01
ADAM_STEP = 10

MESH = pl.DeviceIdType.MESH
ANY = pl.BlockSpec(memory_space=pl.ANY)
VMEM = pl.BlockSpec(memory_space=pltpu.VMEM)


def _sds(shape, dtype):
    return jax.ShapeDtypeStruct(shape, dtype)


def _params(semantics, vmem_mb):
    return pltpu.CompilerParams(dimension_semantics=semantics, vmem_limit_bytes=vmem_mb << 20)


def _nn(a, b):
    return lax.dot_general(a, b, (((1,), (0,)), ((), ())), preferred_element_type=F32)


def _nt(a, b):
    return lax.dot_general(a, b, (((1,), (1,)), ((), ())), preferred_element_type=F32)


def _tn(a, b):
    return lax.dot_general(a, b, (((0,), (0,)), ((), ())), preferred_element_type=F32)


def _sigmoid(x):
    return jax.nn.sigmoid(x)


def _rms_stats(x):
    r = lax.rsqrt(jnp.mean(x * x, axis=-1, keepdims=True) + EPS)
    return r, x * r


def _rms_bwd(dxh, xh, r):
    return r * (dxh - xh * jnp.mean(dxh * xh, axis=-1, keepdims=True))


def _fwd_proj(x, g_mix, w_in3):
    L, D = x.shape
    tn = w_in3.shape[2]
    tm = min(L, 1024)

    def body(x_ref, g_ref, w_ref, h_ref, p_ref):
        @pl.when(pl.program_id(1) == 0)
        def _():
            _, xh = _rms_stats(x_ref[...])
            h_ref[...] = (xh * g_ref[...]).astype(BF16)

        p_ref[...] = _nn(h_ref[...], w_ref[...])

    return pl.pallas_call(
        body, name="fwd_proj", grid=(L // tm, N_CHIPS),
        in_specs=[pl.BlockSpec((tm, D), lambda i, j: (i, 0)),
                  pl.BlockSpec((1, D), lambda i, j: (0, 0)),
                  pl.BlockSpec((None, D, tn), lambda i, j: (j, 0, 0))],
        out_specs=[pl.BlockSpec((tm, D), lambda i, j: (i, 0)),
                   pl.BlockSpec((tm, tn), lambda i, j: (i, j))],
        out_shape=[_sds((L, D), BF16), _sds((L, N_CHIPS * tn), F32)],
        compiler_params=_params(("parallel", "arbitrary"), 48),
    )(x, g_mix, w_in3)


def _lower_bound(lbp):
    l0, l1 = lbp[0:1, :], lbp[1:2, :]
    m = jnp.maximum(l0, l1)
    e0, e1 = jnp.exp(l0 - m), jnp.exp(l1 - m)
    return e0 / (e0 + e1)


def _seg_scan(x, r32, forward):
    n = x.shape[0]
    s = 1
    while s < CHUNK:
        if forward:
            x = x + jnp.where(r32 >= s, pltpu.roll(x, s, 0), 0.0)
        else:
            x = x + jnp.where(r32 < CHUNK - s, pltpu.roll(x, n - s, 0), 0.0)
        s *= 2
    return x


def _bcast_row(x, row):
    n, w = x.shape
    nc = n // CHUNK
    x3 = x.reshape(nc, CHUNK, w)
    return jnp.broadcast_to(x3[:, row:row + 1, :], (nc, CHUNK, w)).reshape(n, w)


def _hgrn_prep(q_raw, f_raw, lb):
    r32 = lax.broadcasted_iota(jnp.int32, f_raw.shape, 0) & (CHUNK - 1)
    sig = _sigmoid(f_raw)
    f = lb + (1.0 - lb) * sig
    b = _seg_scan(jnp.log(f), r32, True)
    a = _bcast_row(b, CHUNK // 2 - 1)
    bl = _bcast_row(b, CHUNK - 1)
    sq = _sigmoid(q_raw)
    q = q_raw * sq * (HEAD_DIM ** -0.5)
    return dict(r32=r32, sig=sig, f=f, k=1.0 - f, b=b, a=a, bl=bl, sq=sq, q=q)


def _chunk_masks(n):
    ri = lax.broadcasted_iota(jnp.int32, (n, n), 0)
    ci = lax.broadcasted_iota(jnp.int32, (n, n), 1)
    same = (ri // CHUNK) == (ci // CHUNK)
    return same & (ci <= ri), same & (ri <= ci)


def _hgrn_fwd(proj, lower_bounds, gamma, H):
    L = proj.shape[0]
    nh = H // HEAD_DIM
    TL = min(L, 256)
    nc = TL // CHUNK

    def body(q_ref, f_ref, v_ref, g_ref, lbp_ref, gam_ref, og_ref, o_ref, s_ref, st_ref):
        @pl.when(pl.program_id(0) == 0)
        def _():
            st_ref[...] = jnp.zeros_like(st_ref)

        lb = _lower_bound(lbp_ref[...])
        gam = gam_ref[...]
        mask, _ = _chunk_masks(TL)
        rowc = lax.broadcasted_iota(jnp.int32, (TL, HEAD_DIM), 0) // CHUNK
        for h in range(nh):
            hs = slice(h * HEAD_DIM, (h + 1) * HEAD_DIM)
            p = _hgrn_prep(q_ref[:, hs], f_ref[:, hs], lb[:, hs])
            v = v_ref[:, hs]
            vb = v.astype(BF16)
            vt = v.T.astype(BF16)
            q_hat = (p["q"] * jnp.exp(p["b"] - p["a"])).astype(BF16)
            k_hat = (p["k"] * jnp.exp(p["a"] - p["b"])).astype(BF16)
            q_in = (p["q"] * jnp.exp(p["b"])).astype(BF16)
            k_out = (p["k"] * jnp.exp(p["bl"] - p["b"])).astype(BF16)
            dec = jnp.exp(p["bl"])
            att = jnp.where(mask, _nt(q_hat, k_hat), 0.0).astype(BF16)
            o_intra = _nn(att, vb)
            st = st_ref[h]
            for c in range(nc):
                rs = slice(c * CHUNK, (c + 1) * CHUNK)
                stb = st.astype(BF16)
                s_ref[c, h] = stb
                o_ref[rs, hs] = o_intra[rs] + _nt(q_in[rs], stb)
                k_c = jnp.where(rowc == c, k_out, jnp.zeros_like(k_out))
                st = st * dec[c * CHUNK:c * CHUNK + 1, :] + _nn(vt, k_c)
            st_ref[h] = st
            o = o_ref[:, hs]
            _, xh = _rms_stats(o)
            gr = g_ref[:, hs]
            og_ref[:, hs] = (xh * gam * (gr * _sigmoid(gr))).astype(BF16)

    col = lambda k: pl.BlockSpec((TL, H), lambda i, k=k: (i, k))
    return pl.pallas_call(
        body, name="hgrn_fwd", grid=(L // TL,),
        in_specs=[col(0), col(1), col(2), col(3),
                  pl.BlockSpec(lower_bounds.shape, lambda i: (0, 0)),
                  pl.BlockSpec(gamma.shape, lambda i: (0, 0))],
        out_specs=[pl.BlockSpec((TL, H), lambda i: (i, 0)),
                   pl.BlockSpec((TL, H), lambda i: (i, 0)),
                   pl.BlockSpec((nc, nh, HEAD_DIM, HEAD_DIM), lambda i: (i, 0, 0, 0))],
        out_shape=[_sds((L, H), BF16), _sds((L, H), F32),
                   _sds((L // CHUNK, nh, HEAD_DIM, HEAD_DIM), BF16)],
        scratch_shapes=[pltpu.VMEM((nh, HEAD_DIM, HEAD_DIM), F32)],
        compiler_params=_params(("arbitrary",), 48),
    )(proj, proj, proj, proj, lower_bounds, gamma)


def _hgrn_bwd(proj, lower_bounds, gamma, o_pre, d_out, s_saved, H):
    L = proj.shape[0]
    nh = H // HEAD_DIM
    TL = min(L, 256)
    nc = TL // CHUNK
    nt = L // TL

    def body(q_ref, f_ref, v_ref, g_ref, lbp_ref, gam_ref, o_ref, d_ref, s_ref,
             dq_ref, df_ref, dv_ref, dg_ref, red_ref, dst_ref, dsall_ref, tmp_ref):
        @pl.when(pl.program_id(0) == 0)
        def _():
            dst_ref[...] = jnp.zeros_like(dst_ref)
            red_ref[...] = jnp.zeros_like(red_ref)

        lb = _lower_bound(lbp_ref[...])
        gam = gam_ref[...]
        mask, mask_t = _chunk_masks(TL)
        rowc = lax.broadcasted_iota(jnp.int32, (TL, HEAD_DIM), 0) // CHUNK
        for h in range(nh):
            hs = slice(h * HEAD_DIM, (h + 1) * HEAD_DIM)
            qr, gr, lbh = q_ref[:, hs], g_ref[:, hs], lb[:, hs]
            p = _hgrn_prep(qr, f_ref[:, hs], lbh)
            vb = v_ref[:, hs].astype(BF16)
            eba, eab = jnp.exp(p["b"] - p["a"]), jnp.exp(p["a"] - p["b"])
            eb, elb = jnp.exp(p["b"]), jnp.exp(p["bl"] - p["b"])
            dec = jnp.exp(p["bl"])
            q_hat, k_hat = p["q"] * eba, p["k"] * eab
            q_in, k_out = p["q"] * eb, p["k"] * elb
            q_hat_b, k_hat_b = q_hat.astype(BF16), k_hat.astype(BF16)
            q_in_b, k_out_b = q_in.astype(BF16), k_out.astype(BF16)

            o, dout = o_ref[:, hs], d_ref[:, hs]
            sg = _sigmoid(gr)
            r, xh = _rms_stats(o)
            dg_ref[:, hs] = (dout * (xh * gam) * (sg * (1.0 + gr * (1.0 - sg)))).astype(BF16)
            dn = dout * (gr * sg)
            red_ref[1:2, hs] += jnp.sum(dn * xh, axis=0, keepdims=True)
            do = _rms_bwd(dn * gam, xh, r)
            dob = do.astype(BF16)
            dot_b = do.T.astype(BF16)

            att_t = jnp.where(mask_t, _nt(k_hat_b, q_hat_b), 0.0).astype(BF16)
            dv_intra = _nn(att_t, dob)
            datt = jnp.where(mask, _nt(dob, vb), 0.0).astype(BF16)
            dqh = _nn(datt, k_hat_b)
            datt_t = jnp.where(mask_t, _nt(vb, dob), 0.0).astype(BF16)
            dkh = _nn(datt_t, q_hat_b)

            dst = dst_ref[h]
            for c in reversed(range(nc)):
                dsall_ref[c] = dst
                q_c = jnp.where(rowc == c, q_in_b, jnp.zeros_like(q_in_b))
                dst = dst * dec[c * CHUNK:c * CHUNK + 1, :] + _nn(dot_b, q_c)
            dst_ref[h] = dst
            for c in range(nc):
                rs = slice(c * CHUNK, (c + 1) * CHUNK)
                ds_c = dsall_ref[c]
                dsb = ds_c.astype(BF16)
                st_prev = s_ref[c, h]
                tmp_ref[0, rs, :] = _nt(k_out_b[rs], dsb)
                tmp_ref[1, rs, :] = _nn(vb[rs], dsb)
                tmp_ref[2, rs, :] = _nn(dob[rs], st_prev)
                ddec = jnp.sum(ds_c * st_prev.astype(F32), axis=0, keepdims=True)
                tmp_ref[3, rs, :] = jnp.broadcast_to(ddec * dec[c * CHUNK:c * CHUNK + 1, :],
                                                     (CHUNK, HEAD_DIM))
            dko, dqi = tmp_ref[1], tmp_ref[2]
            dq = dqh * eba + dqi * eb
            dk = dkh * eab + dko * elb
            tko = dko * k_out
            db = dqh * q_hat - dkh * k_hat + dqi * q_in - tko
            dlog = (_seg_scan(db, p["r32"], False)
                    + _bcast_row(_seg_scan(tko, p["r32"], True), CHUNK - 1) + tmp_ref[3])
            df = dlog / p["f"] - dk
            sig = p["sig"]
            red_ref[0:1, hs] += jnp.sum(df * (1.0 - sig), axis=0, keepdims=True)
            df_ref[:, hs] = (df * (1.0 - lbh) * sig * (1.0 - sig)).astype(BF16)
            sq = p["sq"]
            dq_ref[:, hs] = (dq * (HEAD_DIM ** -0.5) * (sq * (1.0 + qr * (1.0 - sq)))).astype(BF16)
            dv_ref[:, hs] = (dv_intra + tmp_ref[0]).astype(BF16)

    col = lambda k: pl.BlockSpec((TL, H), lambda i, k=k: (nt - 1 - i, k))
    rev = pl.BlockSpec((TL, H), lambda i: (nt - 1 - i, 0))
    return pl.pallas_call(
        body, name="hgrn_bwd", grid=(nt,),
        in_specs=[col(0), col(1), col(2), col(3),
                  pl.BlockSpec(lower_bounds.shape, lambda i: (0, 0)),
                  pl.BlockSpec(gamma.shape, lambda i: (0, 0)),
                  rev, rev,
                  pl.BlockSpec((nc, nh, HEAD_DIM, HEAD_DIM), lambda i: (nt - 1 - i, 0, 0, 0))],
        out_specs=[rev, rev, rev, rev, pl.BlockSpec((8, H), lambda i: (0, 0))],
        out_shape=[_sds((L, H), BF16)] * 4 + [_sds((8, H), F32)],
        scratch_shapes=[pltpu.VMEM((nh, HEAD_DIM, HEAD_DIM), F32),
                        pltpu.VMEM((nc, HEAD_DIM, HEAD_DIM), F32),
                        pltpu.VMEM((4, TL, HEAD_DIM), F32)],
        compiler_params=_params(("arbitrary",), 48),
    )(proj, proj, proj, proj, lower_bounds, gamma, o_pre, d_out, s_saved)


def _shift_down(u, s, row):
    return jnp.where(row >= s, pltpu.roll(u, s, 0), 0.0)


def _shift_up(u, s, row):
    n = u.shape[0]
    return jnp.where(row < n - s, pltpu.roll(u, n - s, 0), 0.0)


def _conv_specs(L, H):
    per = H // LANES
    return [pl.BlockSpec((L, LANES), lambda j, o=o: (0, o * per + j)) for o in (4, 5, 6)]


def _conv_fwd(proj, conv_w, H):
    L = proj.shape[0]

    def body(c_ref, b_ref, x_ref, w_ref, o_ref):
        row = lax.broadcasted_iota(jnp.int32, (L, LANES), 0)
        u = c_ref[...] * x_ref[...]
        w = w_ref[...]
        y = w[0:1] * _shift_down(u, 2, row) + w[1:2] * _shift_down(u, 1, row) + w[2:3] * u
        o_ref[...] = (b_ref[...] * y).astype(BF16)

    return pl.pallas_call(
        body, name="conv_fwd", grid=(H // LANES,),
        in_specs=_conv_specs(L, H) + [pl.BlockSpec((3, LANES), lambda j: (0, j))],
        out_specs=pl.BlockSpec((L, LANES), lambda j: (0, j)),
        out_shape=_sds((L, H), BF16),
        compiler_params=_params(("parallel",), 48),
    )(proj, proj, proj, conv_w)


def _conv_bwd(proj, conv_w, dcb, H):
    L = proj.shape[0]

    def body(c_ref, b_ref, x_ref, w_ref, d_ref, dc_ref, db_ref, dx_ref, dw_ref):
        row = lax.broadcasted_iota(jnp.int32, (L, LANES), 0)
        cg, xb = c_ref[...], x_ref[...]
        u = cg * xb
        u1, u2 = _shift_down(u, 1, row), _shift_down(u, 2, row)
        w = w_ref[...]
        y = w[0:1] * u2 + w[1:2] * u1 + w[2:3] * u
        d = d_ref[...]
        db_ref[...] = (d * y).astype(BF16)
        dy = d * b_ref[...]
        du = w[2:3] * dy + w[1:2] * _shift_up(dy, 1, row) + w[0:1] * _shift_up(dy, 2, row)
        dw_ref[0:1, :] = jnp.sum(dy * u2, axis=0, keepdims=True)
        dw_ref[1:2, :] = jnp.sum(dy * u1, axis=0, keepdims=True)
        dw_ref[2:3, :] = jnp.sum(dy * u, axis=0, keepdims=True)
        dc_ref[...] = (du * xb).astype(BF16)
        dx_ref[...] = (du * cg).astype(BF16)

    blk = pl.BlockSpec((L, LANES), lambda j: (0, j))
    return pl.pallas_call(
        body, name="conv_bwd", grid=(H // LANES,),
        in_specs=_conv_specs(L, H) + [pl.BlockSpec((3, LANES), lambda j: (0, j)), blk],
        out_specs=[blk, blk, blk, pl.BlockSpec((3, LANES), lambda j: (0, j))],
        out_shape=[_sds((L, H), BF16)] * 3 + [_sds((3, H), F32)],
        compiler_params=_params(("parallel",), 56),
    )(proj, proj, proj, conv_w, dcb)


def _gate_specs(tm, H):
    return [pl.BlockSpec((tm, H), lambda i, k=k: (i, k)) for k in (7, 8, 9, 10)]


def _fwd_mix(og, cb, proj, x, wa3, wb3, wout, g_ffn, H):
    L, D = x.shape
    sw = wa3.shape[2]
    tm = min(L, 256)

    def body(o_ref, cb_ref, ga0, ga1, gb0, gb1, x_ref, wa_ref, wb_ref, wo_ref, g_ref,
             ya_ref, yb_ref, m_ref, x1_ref, h2_ref):
        o, cbv = o_ref[...], cb_ref[...]
        for k in range(N_CHIPS):
            cs = slice(k * sw, (k + 1) * sw)
            gs = slice((k % 2) * sw, (k % 2 + 1) * sw)
            gar, gbr = (ga0, ga1)[k // 2], (gb0, gb1)[k // 2]
            ya, yb = _nn(o, wa_ref[k]), _nn(cbv, wb_ref[k])
            ya_ref[:, cs] = ya.astype(BF16)
            yb_ref[:, cs] = yb.astype(BF16)
            m_ref[:, cs] = (_sigmoid(gar[:, gs]) * ya + _sigmoid(gbr[:, gs]) * yb).astype(BF16)
        x1 = x_ref[...] + _nn(m_ref[...], wo_ref[...])
        x1_ref[...] = x1
        _, xh = _rms_stats(x1)
        h2_ref[...] = (xh * g_ref[...]).astype(BF16)

    row = lambda w: pl.BlockSpec((tm, w), lambda i: (i, 0))
    full = lambda a: pl.BlockSpec(a.shape, lambda i: (0,) * a.ndim)
    return pl.pallas_call(
        body, name="fwd_mix", grid=(L // tm,),
        in_specs=[row(H), row(H)] + _gate_specs(tm, H) + [row(D), full(wa3), full(wb3), full(wout),
                                                           full(g_ffn)],
        out_specs=[row(D)] * 5,
        out_shape=[_sds((L, D), BF16)] * 3 + [_sds((L, D), F32), _sds((L, D), BF16)],
        compiler_params=_params(("parallel",), 48),
    )(og, cb, proj, proj, proj, proj, x, wa3, wb3, wout, g_ffn)


def _bwd_mix(dx1b, proj, ya, yb, wa3, wb3, wout, H):
    L, D = dx1b.shape
    sw = wa3.shape[2]
    tm = min(L, 256)

    def body(dx_ref, ga0, ga1, gb0, gb1, ya_ref, yb_ref, wa_ref, wb_ref, wo_ref,
             dya_ref, dyb_ref, dga_ref, dgb_ref, do_ref, dcb_ref):
        dm = _nt(dx_ref[...], wo_ref[...])
        do = jnp.zeros((tm, H), F32)
        dcb = jnp.zeros((tm, H), F32)
        for k in range(N_CHIPS):
            cs = slice(k * sw, (k + 1) * sw)
            gs = slice((k % 2) * sw, (k % 2 + 1) * sw)
            gar, gbr = (ga0, ga1)[k // 2], (gb0, gb1)[k // 2]
            sa, sb = _sigmoid(gar[:, gs]), _sigmoid(gbr[:, gs])
            dmk = dm[:, cs]
            dga_ref[:, cs] = (dmk * ya_ref[:, cs].astype(F32) * sa * (1.0 - sa)).astype(BF16)
            dgb_ref[:, cs] = (dmk * yb_ref[:, cs].astype(F32) * sb * (1.0 - sb)).astype(BF16)
            dya, dyb = (dmk * sa).astype(BF16), (dmk * sb).astype(BF16)
            dya_ref[:, cs] = dya
            dyb_ref[:, cs] = dyb
            do = do + _nt(dya, wa_ref[k])
            dcb = dcb + _nt(dyb, wb_ref[k])
        do_ref[...] = do
        dcb_ref[...] = dcb

    row = lambda w: pl.BlockSpec((tm, w), lambda i: (i, 0))
    full = lambda a: pl.BlockSpec(a.shape, lambda i: (0,) * a.ndim)
    return pl.pallas_call(
        body, name="bwd_mix", grid=(L // tm,),
        in_specs=[row(D)] + _gate_specs(tm, H) + [row(D), row(D), full(wa3), full(wb3), full(wout)],
        out_specs=[row(D)] * 4 + [row(H)] * 2,
        out_shape=[_sds((L, D), BF16)] * 4 + [_sds((L, H), F32)] * 2,
        compiler_params=_params(("parallel",), 48),
    )(dx1b, proj, proj, proj, proj, ya, yb, wa3, wb3, wout)


def _fwd_ffn_up(h2, wgt, wut):
    L, D = h2.shape
    F = wgt.shape[0]
    tn = F // 2
    tm = min(L, 512)

    def body(h_ref, wg_ref, wu_ref, a_ref, b_ref, s_ref):
        h = h_ref[...]
        a, b = _nt(h, wg_ref[...]), _nt(h, wu_ref[...])
        a_ref[...] = a.astype(BF16)
        b_ref[...] = b.astype(BF16)
        s_ref[...] = (a * _sigmoid(a) * b).astype(BF16)

    wspec = pl.BlockSpec((tn, D), lambda j, i: (j, 0))
    ospec = pl.BlockSpec((tm, tn), lambda j, i: (i, j))
    return pl.pallas_call(
        body, name="fwd_ffn_up", grid=(2, L // tm),
        in_specs=[pl.BlockSpec((tm, D), lambda j, i: (i, 0)), wspec, wspec],
        out_specs=[ospec] * 3,
        out_shape=[_sds((L, F), BF16)] * 3,
        compiler_params=_params(("parallel", "parallel"), 48),
    )(h2, wgt, wut)


def _fwd_down_loss(s, wd, x1, target, g_final):
    L, D = x1.shape
    F = wd.shape[0]
    tm = min(L, 256)

    def body(s_ref, wd_ref, x1_ref, t_ref, g_ref, dx_ref, dxb_ref, red_ref):
        @pl.when(pl.program_id(0) == 0)
        def _():
            red_ref[...] = jnp.zeros_like(red_ref)

        g = g_ref[...]
        r, xh = _rms_stats(x1_ref[...] + _nn(s_ref[...], wd_ref[...]))
        e = xh * g - t_ref[...]
        dy = e * (1.0 / D)
        dx = _rms_bwd(dy * g, xh, r)
        dx_ref[...] = dx
        dxb_ref[...] = dx.astype(BF16)
        red_ref[0:1, :] += jnp.sum(dy * xh, axis=0, keepdims=True)
        red_ref[1:2, :] += jnp.broadcast_to(0.5 * jnp.sum(e * e) * (1.0 / D), (1, D))

    row = pl.BlockSpec((tm, D), lambda i: (i, 0))
    return pl.pallas_call(
        body, name="fwd_down_loss", grid=(L // tm,),
        in_specs=[pl.BlockSpec((tm, F), lambda i: (i, 0)), pl.BlockSpec((F, D), lambda i: (0, 0)),
                  row, row, pl.BlockSpec((1, D), lambda i: (0, 0))],
        out_specs=[row, row, pl.BlockSpec((8, D), lambda i: (0, 0))],
        out_shape=[_sds((L, D), F32), _sds((L, D), BF16), _sds((8, D), F32)],
        compiler_params=_params(("arbitrary",), 48),
    )(s, wd, x1, target, g_final)


def _bwd_down(dx2b, wd, a, b):
    L, D = dx2b.shape
    F = wd.shape[0]
    tn = F // 2
    tm = min(L, 512)

    def body(dx_ref, wd_ref, a_ref, b_ref, da_ref, db_ref):
        ds = _nt(dx_ref[...], wd_ref[...])
        a, b = a_ref[...].astype(F32), b_ref[...].astype(F32)
        sg = _sigmoid(a)
        da_ref[...] = (ds * b * sg * (1.0 + a * (1.0 - sg))).astype(BF16)
        db_ref[...] = (ds * a * sg).astype(BF16)

    ospec = pl.BlockSpec((tm, tn), lambda j, i: (i, j))
    return pl.pallas_call(
        body, name="bwd_down", grid=(2, L // tm),
        in_specs=[pl.BlockSpec((tm, D), lambda j, i: (i, 0)),
                  pl.BlockSpec((tn, D), lambda j, i: (j, 0)), ospec, ospec],
        out_specs=[ospec] * 2,
        out_shape=[_sds((L, F), BF16)] * 2,
        compiler_params=_params(("parallel", "parallel"), 48),
    )(dx2b, wd, a, b)


def _bwd_ffn_dh(da, db, wgt, wut, x1, dx2, g_ffn):
    L, D = x1.shape
    F = wgt.shape[0]
    tm = min(L, 256)

    def body(da_ref, db_ref, wg_ref, wu_ref, x1_ref, dx2_ref, g_ref, dx_ref, dxb_ref, red_ref):
        @pl.when(pl.program_id(0) == 0)
        def _():
            red_ref[...] = jnp.zeros_like(red_ref)

        dh = _nn(da_ref[...], wg_ref[...]) + _nn(db_ref[...], wu_ref[...])
        r, xh = _rms_stats(x1_ref[...])
        red_ref[0:1, :] += jnp.sum(dh * xh, axis=0, keepdims=True)
        dx = dx2_ref[...] + _rms_bwd(dh * g_ref[...], xh, r)
        dx_ref[...] = dx
        dxb_ref[...] = dx.astype(BF16)

    row = pl.BlockSpec((tm, D), lambda i: (i, 0))
    aspec = pl.BlockSpec((tm, F), lambda i: (i, 0))
    wspec = pl.BlockSpec((F, D), lambda i: (0, 0))
    return pl.pallas_call(
        body, name="bwd_ffn_dh", grid=(L // tm,),
        in_specs=[aspec, aspec, wspec, wspec, row, row, pl.BlockSpec((1, D), lambda i: (0, 0))],
        out_specs=[row, row, pl.BlockSpec((8, D), lambda i: (0, 0))],
        out_shape=[_sds((L, D), F32), _sds((L, D), BF16), _sds((8, D), F32)],
        compiler_params=_params(("arbitrary",), 56),
    )(da, db, wgt, wut, x1, dx2, g_ffn)


def _bwd_in(dproj, w_in3, x, dx1, g_mix):
    L, D = x.shape
    tn = w_in3.shape[2]
    tm = min(L, 512)
    last = N_CHIPS - 1

    def body(dp_ref, w_ref, x_ref, dx1_ref, g_ref, dx_ref, red_ref, acc_ref):
        i, k = pl.program_id(0), pl.program_id(1)
        part = _nt(dp_ref[...], w_ref[...])

        @pl.when(k == 0)
        def _():
            acc_ref[...] = part

        @pl.when(k > 0)
        def _():
            acc_ref[...] += part

        @pl.when((i == 0) & (k == last))
        def _():
            red_ref[...] = jnp.zeros_like(red_ref)

        @pl.when(k == last)
        def _():
            dh = acc_ref[...]
            r, xh = _rms_stats(x_ref[...])
            red_ref[0:1, :] += jnp.sum(dh * xh, axis=0, keepdims=True)
            dx_ref[...] = dx1_ref[...] + _rms_bwd(dh * g_ref[...], xh, r)

    row = pl.BlockSpec((tm, D), lambda i, k: (i, 0))
    return pl.pallas_call(
        body, name="bwd_in", grid=(L // tm, N_CHIPS),
        in_specs=[pl.BlockSpec((tm, tn), lambda i, k: (i, k)),
                  pl.BlockSpec((None, D, tn), lambda i, k: (k, 0, 0)),
                  row, row, pl.BlockSpec((1, D), lambda i, k: (0, 0))],
        out_specs=[row, pl.BlockSpec((8, D), lambda i, k: (0, 0))],
        out_shape=[_sds((L, D), F32), _sds((8, D), F32)],
        scratch_shapes=[pltpu.VMEM((tm, D), F32)],
        compiler_params=_params(("arbitrary", "arbitrary"), 48),
    )(dproj, w_in3, x, dx1, g_mix)


def _mm_tn(name, a, b, a_spec, b_spec, o_block, n_out, n_k):
    def body(a_ref, b_ref, o_ref):
        part = _tn(a_ref[...], b_ref[...])

        @pl.when(pl.program_id(1) == 0)
        def _():
            o_ref[...] = part

        @pl.when(pl.program_id(1) > 0)
        def _():
            o_ref[...] += part

    return pl.pallas_call(
        body, name=name, grid=(n_out, n_k),
        in_specs=[a_spec, b_spec],
        out_specs=pl.BlockSpec((None,) + o_block, lambda j, k: (j, 0, 0)),
        out_shape=_sds((n_out,) + o_block, F32),
        compiler_params=_params(("parallel", "arbitrary"), 56),
    )(a, b)


TK_TOKENS = 2048


def _dw_cols(name, a, b, n_cols):
    L, M = a.shape
    tk = min(L, TK_TOKENS)
    return _mm_tn(name, a, b, pl.BlockSpec((tk, M), lambda j, k: (k, 0)),
                  pl.BlockSpec((tk, n_cols), lambda j, k: (k, j)), (M, n_cols), N_CHIPS, L // tk)


def _dw_rows(name, a, b):
    L, M = a.shape
    N = b.shape[1]
    tk = min(L, TK_TOKENS)
    return _mm_tn(name, a, b, pl.BlockSpec((tk, M // N_CHIPS), lambda j, k: (k, j)),
                  pl.BlockSpec((tk, N), lambda j, k: (k, 0)), (M // N_CHIPS, N), N_CHIPS, L // tk)


def _dw_rows2(name, a, b):
    L, M = a.shape
    N = b.shape[1]
    tk = min(L, TK_TOKENS)
    return _mm_tn(name, a, b, pl.BlockSpec((tk, M // 2), lambda j, k: (k, j)),
                  pl.BlockSpec((tk, N), lambda j, k: (k, 0)), (M // 2, N), 2, L // tk)


def _place():
    x, y, c = lax.axis_index("x"), lax.axis_index("y"), lax.axis_index("c")
    chips = [(1 - x, y), (x, 1 - y), (1 - x, 1 - y)]
    return x, y, c, 2 * x + y, chips


def _remote(src, dst, send_sem, recv_sem, device):
    return pltpu.make_async_remote_copy(src_ref=src, dst_ref=dst, send_sem=send_sem,
                                        recv_sem=recv_sem, device_id=device, device_id_type=MESH)


def _half(ref, lead, c, r2):
    return ref.at[lead, pl.ds(pl.multiple_of(c * r2, 16), r2), :]


def _cast_place(name, w, chip_idx):
    r, cols = w.shape
    tr = r // 2

    def body(k_ref, w_ref, o_ref):
        o_ref[...] = w_ref[...].astype(BF16)

    return pl.pallas_call(
        body, name=name,
        grid_spec=pltpu.PrefetchScalarGridSpec(
            num_scalar_prefetch=1, grid=(2,),
            in_specs=[pl.BlockSpec((tr, cols), lambda i, k_ref: (i, 0))],
            out_specs=pl.BlockSpec((None, tr, cols), lambda i, k_ref: (k_ref[0], i, 0))),
        out_shape=_sds((N_CHIPS, r, cols), BF16),
        compiler_params=_params(("parallel",), 48),
    )(chip_idx, w)


def _gather_weights(bufs, conv_buf):
    n = len(bufs)

    def body(*refs):
        outs, conv_out = refs[n + 1:2 * n + 1], refs[2 * n + 1]
        send_sems, recv_sems = refs[2 * n + 2:]
        x, y, c, k, chips = _place()
        me, sibling = (x, y, c), (x, y, 1 - c)
        sends, forwards = [], []
        for w in range(n):
            r2 = outs[w].shape[1] // 2
            mine = _half(outs[w], k, c, r2)
            for j, chip in enumerate(chips):
                sends.append(_remote(mine, mine, send_sems.at[w * 6 + j], recv_sems.at[w * 6 + j],
                                     (*chip, c)))
        for j, chip in enumerate(chips):
            sends.append(_remote(conv_out.at[k], conv_out.at[k], send_sems.at[n * 6 + j],
                                 recv_sems.at[n * 6 + j], (*chip, c)))
        for cp in sends:
            cp.start()
        for w in range(n):
            r2 = outs[w].shape[1] // 2
            for j, (cx, cy) in enumerate(chips):
                landed = _half(outs[w], 2 * cx + cy, c, r2)
                _remote(landed, landed, send_sems.at[w * 6 + j], recv_sems.at[w * 6 + j], me).wait_recv()
                fwd = _remote(landed, landed, send_sems.at[w * 6 + 3 + j], recv_sems.at[w * 6 + 3 + j],
                              sibling)
                fwd.start()
                forwards.append(fwd)
        for j, (cx, cy) in enumerate(chips):
            got = conv_out.at[2 * cx + cy]
            _remote(got, got, send_sems.at[n * 6 + j], recv_sems.at[n * 6 + j], me).wait_recv()
        for w in range(n):
            r2 = outs[w].shape[1] // 2
            for j, (cx, cy) in enumerate(chips):
                got = _half(outs[w], 2 * cx + cy, 1 - c, r2)
                _remote(got, got, send_sems.at[w * 6 + 3 + j], recv_sems.at[w * 6 + 3 + j], me).wait_recv()
        for cp in sends + forwards:
            cp.wait_send()

    return pl.pallas_call(
        body, name="gather_w",
        in_specs=[ANY] * (n + 1), out_specs=[ANY] * (n + 1),
        out_shape=[_sds(b.shape, b.dtype) for b in bufs] + [_sds(conv_buf.shape, conv_buf.dtype)],
        input_output_aliases={i: i for i in range(n + 1)},
        scratch_shapes=[pltpu.SemaphoreType.DMA((n * 6 + 3,)), pltpu.SemaphoreType.DMA((n * 6 + 3,))],
    )(*bufs, conv_buf)


def _rs_sibling(grads):
    n = len(grads)

    def body(*refs):
        ins, outs = refs[:n], refs[n:2 * n]
        send_sems, recv_sems = refs[2 * n:]
        x, y, c, _, _ = _place()
        copies = []
        for w in range(n):
            r2 = ins[w].shape[1] // 2
            copies.append(_remote(_half(ins[w], slice(None), 1 - c, r2), outs[w],
                                  send_sems.at[w], recv_sems.at[w], (x, y, 1 - c)))
        for cp in copies:
            cp.start()
        for cp in copies:
            cp.wait()

    return pl.pallas_call(
        body, name="rs_sibling",
        in_specs=[ANY] * n, out_specs=[ANY] * n,
        out_shape=[_sds((N_CHIPS, g.shape[1] // 2, g.shape[2]), F32) for g in grads],
        scratch_shapes=[pltpu.SemaphoreType.DMA((n,)), pltpu.SemaphoreType.DMA((n,))],
    )(*grads)


def _rs_add(name, grad3, from_sibling, c_idx):
    _, r2, cols = from_sibling.shape

    def body(c_ref, g_ref, s_ref, o_ref):
        o_ref[...] = (g_ref[...] + s_ref[...]).astype(BF16)

    return pl.pallas_call(
        body, name=name,
        grid_spec=pltpu.PrefetchScalarGridSpec(
            num_scalar_prefetch=1, grid=(N_CHIPS,),
            in_specs=[pl.BlockSpec((None, r2, cols), lambda k, c_ref: (k, c_ref[0], 0)),
                      pl.BlockSpec((None, r2, cols), lambda k, c_ref: (k, 0, 0))],
            out_specs=pl.BlockSpec((None, r2, cols), lambda k, c_ref: (k, 0, 0))),
        out_shape=_sds(from_sibling.shape, BF16),
        compiler_params=_params(("parallel",), 48),
    )(c_idx, grad3, from_sibling)


def _rs_ici(partials):
    n = len(partials)

    def body(*refs):
        ins, outs = refs[:n], refs[n:2 * n]
        send_sems, recv_sems = refs[2 * n:]
        x, y, c, k, chips = _place()
        sends = []
        for w in range(n):
            for j, (cx, cy) in enumerate(chips):
                sends.append(_remote(ins[w].at[2 * cx + cy], outs[w].at[j],
                                     send_sems.at[w * 3 + j], recv_sems.at[w * 3 + j], (cx, cy, c)))
        for cp in sends:
            cp.start()
        for cp in sends:
            cp.wait()

    return pl.pallas_call(
        body, name="rs_ici",
        in_specs=[ANY] * n, out_specs=[ANY] * n,
        out_shape=[_sds((3,) + p.shape[1:], BF16) for p in partials],
        scratch_shapes=[pltpu.SemaphoreType.DMA((n * 3,)), pltpu.SemaphoreType.DMA((n * 3,))],
    )(*partials)


def _rs_sum(name, partials, received, place_idx):
    _, r2, cols = partials.shape
    nb = 2
    tr = r2 // nb

    def body(idx_ref, p_ref, r_ref, o_ref):
        o_ref[...] = ((p_ref[...].astype(F32) + r_ref[0].astype(F32))
                      + (r_ref[1].astype(F32) + r_ref[2].astype(F32)))

    return pl.pallas_call(
        body, name=name,
        grid_spec=pltpu.PrefetchScalarGridSpec(
            num_scalar_prefetch=1, grid=(nb,),
            in_specs=[pl.BlockSpec((None, tr, cols), lambda i, idx: (idx[0], i, 0)),
                      pl.BlockSpec((3, tr, cols), lambda i, idx: (0, i, 0))],
            out_specs=pl.BlockSpec((tr, cols), lambda i, idx: (idx[1] * nb + i, 0))),
        out_shape=_sds((2 * r2, cols), F32),
        compiler_params=_params(("parallel",), 48),
    )(place_idx, partials, received)


def _rs_share(shards):
    n = len(shards)

    def body(*refs):
        outs = refs[n:2 * n]
        send_sems, recv_sems = refs[2 * n:]
        x, y, c, _, _ = _place()
        sends = []
        for w in range(n):
            r2 = outs[w].shape[0] // 2
            mine = outs[w].at[pl.ds(pl.multiple_of(c * r2, 8), r2), :]
            sends.append(_remote(mine, mine, send_sems.at[w], recv_sems.at[w], (x, y, 1 - c)))
        for cp in sends:
            cp.start()
        for w in range(n):
            r2 = outs[w].shape[0] // 2
            theirs = outs[w].at[pl.ds(pl.multiple_of((1 - c) * r2, 8), r2), :]
            _remote(theirs, theirs, send_sems.at[w], recv_sems.at[w], (x, y, c)).wait_recv()
        for cp in sends:
            cp.wait_send()

    return pl.pallas_call(
        body, name="rs_share",
        in_specs=[ANY] * n, out_specs=[ANY] * n,
        out_shape=[_sds(s.shape, F32) for s in shards],
        input_output_aliases={i: i for i in range(n)},
        scratch_shapes=[pltpu.SemaphoreType.DMA((n,)), pltpu.SemaphoreType.DMA((n,))],
    )(*shards)


def _small_allreduce(red_mix, red_ffn, red_final, red_hg, g_conv):
    rows = N_SMALL_ROWS
    D = red_mix.shape[1]
    H = red_hg.shape[1]

    def body(mix_ref, ffn_ref, fin_ref, hg_ref, cv_ref, sum_ref, all_ref, in_ref, send_sems, recv_sems):
        in_ref[...] = jnp.zeros_like(in_ref)
        in_ref[0:1, :] = mix_ref[0:1, :]
        in_ref[1:2, :] = ffn_ref[0:1, :]
        in_ref[2:3, :] = fin_ref[0:1, :]
        gam = hg_ref[1:2, 0:HEAD_DIM]
        for h in range(1, H // HEAD_DIM):
            gam = gam + hg_ref[1:2, h * HEAD_DIM:(h + 1) * HEAD_DIM]
        in_ref[3:4, 0:HEAD_DIM] = gam
        in_ref[3:4, HEAD_DIM:2 * HEAD_DIM] = fin_ref[1:2, 0:HEAD_DIM]
        in_ref[4:5, 0:H] = hg_ref[0:1, :]
        in_ref[6:9, 0:H] = cv_ref[...]
        x, y, c, _, _ = _place()
        me = 4 * x + 2 * y + c
        all_ref[me] = in_ref[...]
        copies = []
        for m in range(1, 8):
            mx, my, mc = (m >> 2) & 1, (m >> 1) & 1, m & 1
            px, py, pc = x ^ mx, y ^ my, c ^ mc
            copies.append((_remote(in_ref, all_ref.at[me], send_sems.at[m - 1], recv_sems.at[m - 1],
                                   (px, py, pc)), 4 * px + 2 * py + pc, m))
        for cp, _, _ in copies:
            cp.start()
        for _, peer, m in copies:
            _remote(in_ref, all_ref.at[peer], send_sems.at[m - 1], recv_sems.at[m - 1],
                    (x, y, c)).wait_recv()
        for cp, _, _ in copies:
            cp.wait_send()
        total = all_ref[0]
        for d in range(1, 8):
            total = total + all_ref[d]
        sum_ref[...] = total

    return pl.pallas_call(
        body, name="small_allreduce",
        in_specs=[VMEM] * 5, out_specs=[VMEM, VMEM],
        out_shape=[_sds((rows, D), F32), _sds((8, rows, D), F32)],
        scratch_shapes=[pltpu.VMEM((rows, D), F32), pltpu.SemaphoreType.DMA((7,)),
                        pltpu.SemaphoreType.DMA((7,))],
    )(red_mix, red_ffn, red_final, red_hg, g_conv)[0]


def _adamw_math(w, g, m, v):
    m = ADAM_B1 * m + (1.0 - ADAM_B1) * g
    v = ADAM_B2 * v + (1.0 - ADAM_B2) * jnp.square(g)
    m_hat = m / (1.0 - ADAM_B1 ** ADAM_STEP)
    v_hat = v / (1.0 - ADAM_B2 ** ADAM_STEP)
    delta = -ADAM_LR * (m_hat / (jnp.sqrt(v_hat) + ADAM_EPS) + ADAM_WD * w)
    return delta, m, v


def _adamw(name, g, w, m, v):
    r, cols = g.shape
    tr = r // 4

    def body(g_ref, w_ref, m_ref, v_ref, d_ref, mo_ref, vo_ref):
        d_ref[...], mo_ref[...], vo_ref[...] = _adamw_math(w_ref[...], g_ref[...], m_ref[...], v_ref[...])

    blk = pl.BlockSpec((tr, cols), lambda i: (i, 0))
    return pl.pallas_call(
        body, name=name, grid=(r // tr,),
        in_specs=[blk] * 4, out_specs=[blk] * 3, out_shape=[_sds((r, cols), F32)] * 3,
        compiler_params=_params(("parallel",), 48),
    )(g, w, m, v)


def _small_update(total, chip_idx, ws, ms, vs):
    n = len(ws)
    H = ws[1].shape[1]

    def body(idx_ref, tot_ref, *refs):
        w, m, v, outs = refs[:n], refs[n:2 * n], refs[2 * n:3 * n], refs[3 * n:]
        chip = idx_ref[0]
        p0 = _lower_bound(w[1][...])
        dl0 = p0 * (1.0 - p0) * tot_ref[4:5, 0:H]
        conv = jnp.zeros((3, LANES), F32)
        for k in range(N_CHIPS):
            conv = jnp.where(chip == k, tot_ref[6:9, k * LANES:(k + 1) * LANES], conv)
        grads = [tot_ref[0:1, :], None, tot_ref[3:4, 0:HEAD_DIM], conv, tot_ref[1:2, :], tot_ref[2:3, :]]
        for p in range(n):
            g_ref, d_ref, mo_ref, vo_ref = outs[4 * p:4 * p + 4]
            if p == 1:
                for row, g in ((slice(0, 1), dl0), (slice(1, 2), -dl0)):
                    g_ref[row, :] = g
                    d_ref[row, :], mo_ref[row, :], vo_ref[row, :] = _adamw_math(
                        w[p][row, :], g, m[p][row, :], v[p][row, :])
            else:
                g_ref[...] = grads[p]
                d_ref[...], mo_ref[...], vo_ref[...] = _adamw_math(w[p][...], grads[p], m[p][...], v[p][...])
        outs[4 * n][...] = tot_ref[3:4, HEAD_DIM:2 * HEAD_DIM]

    full = lambda a: pl.BlockSpec(a.shape, lambda i, idx: (0,) * a.ndim)
    out_shape = [_sds(w.shape, F32) for w in ws for _ in range(4)] + [_sds((1, LANES), F32)]
    return pl.pallas_call(
        body, name="small_update",
        grid_spec=pltpu.PrefetchScalarGridSpec(
            num_scalar_prefetch=1, grid=(1,),
            in_specs=[full(total)] + [full(a) for a in ws + ms + vs],
            out_specs=[full(s) for s in out_shape]),
        out_shape=out_shape,
    )(chip_idx, total, *ws, *ms, *vs)


def kernel(x, norm_mix_g, w_in, lower_bounds, hg_norm_g, conv_w, w_branch_a, w_branch_b, w_out, norm_ffn_g, w_ffn_gate, w_ffn_up, w_ffn_down, norm_final_g, loss_target, m_norm_mix_g, m_w_in, m_lower_bounds, m_hg_norm_g, m_conv_w, m_w_branch_a, m_w_branch_b, m_w_out, m_norm_ffn_g, m_w_ffn_gate, m_w_ffn_up, m_w_ffn_down, m_norm_final_g, v_norm_mix_g, v_w_in, v_lower_bounds, v_hg_norm_g, v_conv_w, v_w_branch_a, v_w_branch_b, v_w_out, v_norm_ffn_g, v_w_ffn_gate, v_w_ffn_up, v_w_ffn_down, v_norm_final_g):
    _, L, D = x.shape
    H = D // 2
    assert lower_bounds.shape == (2, H) and hg_norm_g.shape == (1, HEAD_DIM)
    assert conv_w.shape == (1, 3, LANES) and w_in.shape[2] * N_CHIPS == 11 * H
    x2d, target = x.reshape(L, D), loss_target.reshape(L, D)
    g_final = norm_final_g.reshape(1, D)
    chip = 2 * lax.axis_index("x") + lax.axis_index("y")
    core = lax.axis_index("c")

    tr = lambda w: jnp.transpose(w[0])
    big = [w_in[0], w_branch_a[0], w_branch_b[0], w_out[0], tr(w_ffn_gate), tr(w_ffn_up), w_ffn_down[0]]
    big_m = [m_w_in[0], m_w_branch_a[0], m_w_branch_b[0], m_w_out[0], tr(m_w_ffn_gate), tr(m_w_ffn_up),
             m_w_ffn_down[0]]
    big_v = [v_w_in[0], v_w_branch_a[0], v_w_branch_b[0], v_w_out[0], tr(v_w_ffn_gate), tr(v_w_ffn_up),
             v_w_ffn_down[0]]
    names = ["w_in", "w_branch_a", "w_branch_b", "w_out", "w_ffn_gate", "w_ffn_up", "w_ffn_down"]

    chip_idx = chip.reshape(1).astype(jnp.int32)
    placed = [_cast_place("place_" + nm, w, chip_idx) for nm, w in zip(names, big)]
    conv_placed = lax.dynamic_update_slice(jnp.zeros((N_CHIPS, 3, LANES), F32), conv_w, (chip, 0, 0))
    gathered = _gather_weights(placed, conv_placed)
    w_in3, wa3, wb3, wout3, wgt3, wut3, wd3 = gathered[:7]
    conv_full = jnp.transpose(gathered[7], (1, 0, 2)).reshape(3, H)
    wout = wout3.reshape(D, D)
    d_ff = N_CHIPS * wd3.shape[1]
    wgt, wut, wd = wgt3.reshape(d_ff, D), wut3.reshape(d_ff, D), wd3.reshape(d_ff, D)

    h, proj = _fwd_proj(x2d, norm_mix_g, w_in3)
    og, o_pre, s_saved = _hgrn_fwd(proj, lower_bounds, hg_norm_g, H)
    cb = _conv_fwd(proj, conv_full, H)
    ya, yb, merged, x1, h2 = _fwd_mix(og, cb, proj, x2d, wa3, wb3, wout, norm_ffn_g, H)
    ffn_a, ffn_b, ffn_s = _fwd_ffn_up(h2, wgt, wut)
    dx2, dx2b, red_final = _fwd_down_loss(ffn_s, wd, x1, target, g_final)

    shards3 = lambda g: g.reshape(N_CHIPS, d_ff // N_CHIPS, D)
    da, db = _bwd_down(dx2b, wd, ffn_a, ffn_b)
    g_wd = shards3(_dw_rows2("dw_ffn_down", ffn_s, dx2b))
    dx1, dx1b, red_ffn = _bwd_ffn_dh(da, db, wgt, wut, x1, dx2, norm_ffn_g)
    g_wg = shards3(_dw_rows2("dw_ffn_gate", da, h2))
    g_wu = shards3(_dw_rows2("dw_ffn_up", db, h2))
    dya, dyb, dga, dgb, d_o, d_cb = _bwd_mix(dx1b, proj, ya, yb, wa3, wb3, wout, H)
    g_wout = _dw_rows("dw_out", merged, dx1b)
    g_wa = _dw_cols("dw_branch_a", og, dya, D // N_CHIPS)
    g_wb = _dw_cols("dw_branch_b", cb, dyb, D // N_CHIPS)
    dcg, dbg, dxb, g_conv = _conv_bwd(proj, conv_full, d_cb, H)
    dq, df, dv, dg, red_hg = _hgrn_bwd(proj, lower_bounds, hg_norm_g, o_pre, d_o, s_saved, H)
    dproj = jnp.concatenate([dq, df, dv, dg, dcg, dbg, dxb, dga, dgb], axis=1)
    grad_x, red_mix = _bwd_in(dproj, w_in3, x2d, dx1, norm_mix_g)
    g_win = _dw_cols("dw_in", h, dproj, w_in3.shape[2])

    grads3 = [g_win, g_wa, g_wb, g_wout, g_wg, g_wu, g_wd]
    c_idx = core.reshape(1).astype(jnp.int32)
    from_sib = _rs_sibling(grads3)
    partials = [_rs_add("rs_add_" + nm, g, s, c_idx) for nm, g, s in zip(names, grads3, from_sib)]
    received = _rs_ici(partials)
    place_idx = jnp.stack([chip, core]).astype(jnp.int32)
    halves = [_rs_sum("rs_sum_" + nm, p, r, place_idx) for nm, p, r in zip(names, partials, received)]
    shard_grads = _rs_share(halves)
    big_out = [_adamw("adamw_" + nm, g, w, m, v)
               for nm, g, w, m, v in zip(names, shard_grads, big, big_m, big_v)]

    total = _small_allreduce(red_mix, red_ffn, red_final, red_hg, g_conv)

    def smalls(mix, lb, hg, cw, ffn, fin):
        return [mix, lb, hg, cw[0], ffn, fin.reshape(1, D)]

    small_out = _small_update(
        total, chip_idx,
        smalls(norm_mix_g, lower_bounds, hg_norm_g, conv_w, norm_ffn_g, norm_final_g),
        smalls(m_norm_mix_g, m_lower_bounds, m_hg_norm_g, m_conv_w, m_norm_ffn_g, m_norm_final_g),
        smalls(v_norm_mix_g, v_lower_bounds, v_hg_norm_g, v_conv_w, v_norm_ffn_g, v_norm_final_g))

    def outputs(i):
        big_i = [shard_grads[j] if i == 0 else big_out[j][i - 1] for j in range(7)]
        mix, lb, hg, cw, ffn, fin = [small_out[4 * p + i] for p in range(6)]
        return [mix, big_i[0][None], lb, hg, cw[None], big_i[1][None], big_i[2][None], big_i[3][None], ffn,
                big_i[4].T[None], big_i[5].T[None], big_i[6][None], fin.reshape(D)]

    outs = [small_out[24][0, 0], grad_x.reshape(1, L, D)]
    for i in range(4):
        outs += outputs(i)
    return tuple(outs)
```

```python
import functools

import jax
import jax.numpy as jnp
from jax import lax
from jax.experimental import pallas as pl
from jax.experimental.pallas import tpu as pltpu

F32 = jnp.float32
BF16 = jnp.bfloat16
EPS = 1e-6
CHUNK = 32
HEAD_DIM = 128
LANES = 128
N_CHIPS = 4
N_SMALL_ROWS = 16

ADAM_LR = 0.001
ADAM_B1 = 0.9
ADAM_B2 = 0.999
ADAM_EPS = 1e-08
ADAM_WD = 0.01
ADAM_STEP = 10

MESH = pl.DeviceIdType.MESH
ANY = pl.BlockSpec(memory_space=pl.ANY)
VMEM = pl.BlockSpec(memory_space=pltpu.VMEM)
HBM = pl.BlockSpec(memory_space=pltpu.HBM)
SEM = pl.BlockSpec(memory_space=pltpu.SEMAPHORE)
EFFECT = pltpu.SideEffectType.DATAFLOW_SIDE_EFFECTING


def _sds(shape, dtype):
    return jax.ShapeDtypeStruct(shape, dtype)


def _params(semantics, vmem_mb):
    return pltpu.CompilerParams(dimension_semantics=semantics, vmem_limit_bytes=vmem_mb << 20)


def _nn(a, b):
    return lax.dot_general(a, b, (((1,), (0,)), ((), ())), preferred_element_type=F32)


def _nt(a, b):
    return lax.dot_general(a, b, (((1,), (1,)), ((), ())), preferred_element_type=F32)


def _tn(a, b):
    return lax.dot_general(a, b, (((0,), (0,)), ((), ())), preferred_element_type=F32)


def _sigmoid(x):
    return jax.nn.sigmoid(x)


def _rms_stats(x):
    r = lax.rsqrt(jnp.mean(x * x, axis=-1, keepdims=True) + EPS)
    return r, x * r


def _rms_bwd(dxh, xh, r):
    return r * (dxh - xh * jnp.mean(dxh * xh, axis=-1, keepdims=True))


def _fwd_proj(x, g_mix, w_in3):
    L, D = x.shape
    tn = w_in3.shape[2]
    tm = min(L, 1024)

    def body(x_ref, g_ref, w_ref, h_ref, p_ref):
        @pl.when(pl.program_id(1) == 0)
        def _():
            _, xh = _rms_stats(x_ref[...])
            h_ref[...] = (xh * g_ref[...]).astype(BF16)

        p_ref[...] = _nn(h_ref[...], w_ref[...])

    return pl.pallas_call(
        body, name="fwd_proj", grid=(L // tm, N_CHIPS),
        in_specs=[pl.BlockSpec((tm, D), lambda i, j: (i, 0)),
                  pl.BlockSpec((1, D), lambda i, j: (0, 0)),
                  pl.BlockSpec((None, D, tn), lambda i, j: (j, 0, 0))],
        out_specs=[pl.BlockSpec((tm, D), lambda i, j: (i, 0)),
                   pl.BlockSpec((tm, tn), lambda i, j: (i, j))],
        out_shape=[_sds((L, D), BF16), _sds((L, N_CHIPS * tn), F32)],
        compiler_params=_params(("parallel", "arbitrary"), 48),
    )(x, g_mix, w_in3)


def _lower_bound(lbp):
    l0, l1 = lbp[0:1, :], lbp[1:2, :]
    m = jnp.maximum(l0, l1)
    e0, e1 = jnp.exp(l0 - m), jnp.exp(l1 - m)
    return e0 / (e0 + e1)


def _seg_scan(x, r32, forward):
    n = x.shape[0]
    s = 1
    while s < CHUNK:
        if forward:
            x = x + jnp.where(r32 >= s, pltpu.roll(x, s, 0), 0.0)
        else:
            x = x + jnp.where(r32 < CHUNK - s, pltpu.roll(x, n - s, 0), 0.0)
        s *= 2
    return x


def _bcast_row(x, row):
    n, w = x.shape
    nc = n // CHUNK
    x3 = x.reshape(nc, CHUNK, w)
    return jnp.broadcast_to(x3[:, row:row + 1, :], (nc, CHUNK, w)).reshape(n, w)


def _hgrn_prep(q_raw, f_raw, lb):
    r32 = lax.broadcasted_iota(jnp.int32, f_raw.shape, 0) & (CHUNK - 1)
    sig = _sigmoid(f_raw)
    f = lb + (1.0 - lb) * sig
    b = _seg_scan(jnp.log(f), r32, True)
    a = _bcast_row(b, CHUNK // 2 - 1)
    bl = _bcast_row(b, CHUNK - 1)
    sq = _sigmoid(q_raw)
    q = q_raw * sq * (HEAD_DIM ** -0.5)
    return dict(r32=r32, sig=sig, f=f, k=1.0 - f, b=b, a=a, bl=bl, sq=sq, q=q)


def _chunk_masks(n):
    ri = lax.broadcasted_iota(jnp.int32, (n, n), 0)
    ci = lax.broadcasted_iota(jnp.int32, (n, n), 1)
    same = (ri // CHUNK) == (ci // CHUNK)
    return same & (ci <= ri), same & (ri <= ci)


def _hgrn_fwd(proj, lower_bounds, gamma, H):
    L = proj.shape[0]
    nh = H // HEAD_DIM
    TL = min(L, 256)
    nc = TL // CHUNK

    def body(q_ref, f_ref, v_ref, g_ref, lbp_ref, gam_ref, og_ref, o_ref, s_ref, st_ref):
        @pl.when(pl.program_id(0) == 0)
        def _():
            st_ref[...] = jnp.zeros_like(st_ref)

        lb = _lower_bound(lbp_ref[...])
        gam = gam_ref[...]
        mask, _ = _chunk_masks(TL)
        rowc = lax.broadcasted_iota(jnp.int32, (TL, HEAD_DIM), 0) // CHUNK
        for h in range(nh):
            hs = slice(h * HEAD_DIM, (h + 1) * HEAD_DIM)
            p = _hgrn_prep(q_ref[:, hs], f_ref[:, hs], lb[:, hs])
            v = v_ref[:, hs]
            vb = v.astype(BF16)
            vt = v.T.astype(BF16)
            q_hat = (p["q"] * jnp.exp(p["b"] - p["a"])).astype(BF16)
            k_hat = (p["k"] * jnp.exp(p["a"] - p["b"])).astype(BF16)
            q_in = (p["q"] * jnp.exp(p["b"])).astype(BF16)
            k_out = (p["k"] * jnp.exp(p["bl"] - p["b"])).astype(BF16)
            dec = jnp.exp(p["bl"])
            att = jnp.where(mask, _nt(q_hat, k_hat), 0.0).astype(BF16)
            o_intra = _nn(att, vb)
            st = st_ref[h]
            for c in range(nc):
                rs = slice(c * CHUNK, (c + 1) * CHUNK)
                stb = st.astype(BF16)
                s_ref[c, h] = stb
                o_ref[rs, hs] = o_intra[rs] + _nt(q_in[rs], stb)
                k_c = jnp.where(rowc == c, k_out, jnp.zeros_like(k_out))
                st = st * dec[c * CHUNK:c * CHUNK + 1, :] + _nn(vt, k_c)
            st_ref[h] = st
            o = o_ref[:, hs]
            _, xh = _rms_stats(o)
            gr = g_ref[:, hs]
            og_ref[:, hs] = (xh * gam * (gr * _sigmoid(gr))).astype(BF16)

    col = lambda k: pl.BlockSpec((TL, H), lambda i, k=k: (i, k))
    return pl.pallas_call(
        body, name="hgrn_fwd", grid=(L // TL,),
        in_specs=[col(0), col(1), col(2), col(3),
                  pl.BlockSpec(lower_bounds.shape, lambda i: (0, 0)),
                  pl.BlockSpec(gamma.shape, lambda i: (0, 0))],
        out_specs=[pl.BlockSpec((TL, H), lambda i: (i, 0)),
                   pl.BlockSpec((TL, H), lambda i: (i, 0)),
                   pl.BlockSpec((nc, nh, HEAD_DIM, HEAD_DIM), lambda i: (i, 0, 0, 0))],
        out_shape=[_sds((L, H), BF16), _sds((L, H), F32),
                   _sds((L // CHUNK, nh, HEAD_DIM, HEAD_DIM), BF16)],
        scratch_shapes=[pltpu.VMEM((nh, HEAD_DIM, HEAD_DIM), F32)],
        compiler_params=_params(("arbitrary",), 48),
    )(proj, proj, proj, proj, lower_bounds, gamma)


def _hgrn_bwd(proj, lower_bounds, gamma, o_pre, d_out, s_saved, H, after):
    L = proj.shape[0]
    nh = H // HEAD_DIM
    TL = min(L, 256)
    nc = TL // CHUNK
    nt = L // TL

    def body(q_ref, f_ref, v_ref, g_ref, lbp_ref, gam_ref, o_ref, d_ref, s_ref, after_ref,
             dq_ref, df_ref, dv_ref, dg_ref, red_ref, dst_ref, dsall_ref, tmp_ref):
        @pl.when(pl.program_id(0) == 0)
        def _():
            dst_ref[...] = jnp.zeros_like(dst_ref)
            red_ref[...] = jnp.zeros_like(red_ref)

        lb = _lower_bound(lbp_ref[...])
        gam = gam_ref[...]
        mask, mask_t = _chunk_masks(TL)
        rowc = lax.broadcasted_iota(jnp.int32, (TL, HEAD_DIM), 0) // CHUNK
        for h in range(nh):
            hs = slice(h * HEAD_DIM, (h + 1) * HEAD_DIM)
            qr, gr, lbh = q_ref[:, hs], g_ref[:, hs], lb[:, hs]
            p = _hgrn_prep(qr, f_ref[:, hs], lbh)
            vb = v_ref[:, hs].astype(BF16)
            eba, eab = jnp.exp(p["b"] - p["a"]), jnp.exp(p["a"] - p["b"])
            eb, elb = jnp.exp(p["b"]), jnp.exp(p["bl"] - p["b"])
            dec = jnp.exp(p["bl"])
            q_hat, k_hat = p["q"] * eba, p["k"] * eab
            q_in, k_out = p["q"] * eb, p["k"] * elb
            q_hat_b, k_hat_b = q_hat.astype(BF16), k_hat.astype(BF16)
            q_in_b, k_out_b = q_in.astype(BF16), k_out.astype(BF16)

            o, dout = o_ref[:, hs], d_ref[:, hs]
            sg = _sigmoid(gr)
            r, xh = _rms_stats(o)
            dg_ref[:, hs] = (dout * (xh * gam) * (sg * (1.0 + gr * (1.0 - sg)))).astype(BF16)
            dn = dout * (gr * sg)
            red_ref[1:2, hs] += jnp.sum(dn * xh, axis=0, keepdims=True)
            do = _rms_bwd(dn * gam, xh, r)
            dob = do.astype(BF16)
            dot_b = do.T.astype(BF16)

            att_t = jnp.where(mask_t, _nt(k_hat_b, q_hat_b), 0.0).astype(BF16)
            dv_intra = _nn(att_t, dob)
            datt = jnp.where(mask, _nt(dob, vb), 0.0).astype(BF16)
            dqh = _nn(datt, k_hat_b)
            datt_t = jnp.where(mask_t, _nt(vb, dob), 0.0).astype(BF16)
            dkh = _nn(datt_t, q_hat_b)

            dst = dst_ref[h]
            for c in reversed(range(nc)):
                dsall_ref[c] = dst
                q_c = jnp.where(rowc == c, q_in_b, jnp.zeros_like(q_in_b))
                dst = dst * dec[c * CHUNK:c * CHUNK + 1, :] + _nn(dot_b, q_c)
            dst_ref[h] = dst
            for c in range(nc):
                rs = slice(c * CHUNK, (c + 1) * CHUNK)
                ds_c = dsall_ref[c]
                dsb = ds_c.astype(BF16)
                st_prev = s_ref[c, h]
                tmp_ref[0, rs, :] = _nt(k_out_b[rs], dsb)
                tmp_ref[1, rs, :] = _nn(vb[rs], dsb)
                tmp_ref[2, rs, :] = _nn(dob[rs], st_prev)
                ddec = jnp.sum(ds_c * st_prev.astype(F32), axis=0, keepdims=True)
                tmp_ref[3, rs, :] = jnp.broadcast_to(ddec * dec[c * CHUNK:c * CHUNK + 1, :],
                                                     (CHUNK, HEAD_DIM))
            dko, dqi = tmp_ref[1], tmp_ref[2]
            dq = dqh * eba + dqi * eb
            dk = dkh * eab + dko * elb
            tko = dko * k_out
            db = dqh * q_hat - dkh * k_hat + dqi * q_in - tko
            dlog = (_seg_scan(db, p["r32"], False)
                    + _bcast_row(_seg_scan(tko, p["r32"], True), CHUNK - 1) + tmp_ref[3])
            df = dlog / p["f"] - dk
            sig = p["sig"]
            red_ref[0:1, hs] += jnp.sum(df * (1.0 - sig), axis=0, keepdims=True)
            df_ref[:, hs] = (df * (1.0 - lbh) * sig * (1.0 - sig)).astype(BF16)
            sq = p["sq"]
            dq_ref[:, hs] = (dq * (HEAD_DIM ** -0.5) * (sq * (1.0 + qr * (1.0 - sq)))).astype(BF16)
            dv_ref[:, hs] = (dv_intra + tmp_ref[0]).astype(BF16)

    col = lambda k: pl.BlockSpec((TL, H), lambda i, k=k: (nt - 1 - i, k))
    rev = pl.BlockSpec((TL, H), lambda i: (nt - 1 - i, 0))
    return pl.pallas_call(
        body, name="hgrn_bwd", grid=(nt,),
        in_specs=[col(0), col(1), col(2), col(3),
                  pl.BlockSpec(lower_bounds.shape, lambda i: (0, 0)),
                  pl.BlockSpec(gamma.shape, lambda i: (0, 0)),
                  rev, rev,
                  pl.BlockSpec((nc, nh, HEAD_DIM, HEAD_DIM), lambda i: (nt - 1 - i, 0, 0, 0)), ANY],
        out_specs=[rev, rev, rev, rev, pl.BlockSpec((8, H), lambda i: (0, 0))],
        out_shape=[_sds((L, H), BF16)] * 4 + [_sds((8, H), F32)],
        scratch_shapes=[pltpu.VMEM((nh, HEAD_DIM, HEAD_DIM), F32),
                        pltpu.VMEM((nc, HEAD_DIM, HEAD_DIM), F32),
                        pltpu.VMEM((4, TL, HEAD_DIM), F32)],
        compiler_params=_params(("arbitrary",), 48),
    )(proj, proj, proj, proj, lower_bounds, gamma, o_pre, d_out, s_saved, after)


def _shift_down(u, s, row):
    return jnp.where(row >= s, pltpu.roll(u, s, 0), 0.0)


def _shift_up(u, s, row):
    n = u.shape[0]
    return jnp.where(row < n - s, pltpu.roll(u, n - s, 0), 0.0)


def _conv_specs(L, H):
    per = H // LANES
    return [pl.BlockSpec((L, LANES), lambda j, o=o: (0, o * per + j)) for o in (4, 5, 6)]


def _conv_fwd(proj, conv_w, H):
    L = proj.shape[0]

    def body(c_ref, b_ref, x_ref, w_ref, o_ref):
        row = lax.broadcasted_iota(jnp.int32, (L, LANES), 0)
        u = c_ref[...] * x_ref[...]
        w = w_ref[...]
        y = w[0:1] * _shift_down(u, 2, row) + w[1:2] * _shift_down(u, 1, row) + w[2:3] * u
        o_ref[...] = (b_ref[...] * y).astype(BF16)

    return pl.pallas_call(
        body, name="conv_fwd", grid=(H // LANES,),
        in_specs=_conv_specs(L, H) + [pl.BlockSpec((3, LANES), lambda j: (0, j))],
        out_specs=pl.BlockSpec((L, LANES), lambda j: (0, j)),
        out_shape=_sds((L, H), BF16),
        compiler_params=_params(("parallel",), 48),
    )(proj, proj, proj, conv_w)


def _conv_bwd(proj, conv_w, dcb, H):
    L = proj.shape[0]

    def body(c_ref, b_ref, x_ref, w_ref, d_ref, dc_ref, db_ref, dx_ref, dw_ref):
        row = lax.broadcasted_iota(jnp.int32, (L, LANES), 0)
        cg, xb = c_ref[...], x_ref[...]
        u = cg * xb
        u1, u2 = _shift_down(u, 1, row), _shift_down(u, 2, row)
        w = w_ref[...]
        y = w[0:1] * u2 + w[1:2] * u1 + w[2:3] * u
        d = d_ref[...]
        db_ref[...] = (d * y).astype(BF16)
        dy = d * b_ref[...]
        du = w[2:3] * dy + w[1:2] * _shift_up(dy, 1, row) + w[0:1] * _shift_up(dy, 2, row)
        dw_ref[0:1, :] = jnp.sum(dy * u2, axis=0, keepdims=True)
        dw_ref[1:2, :] = jnp.sum(dy * u1, axis=0, keepdims=True)
        dw_ref[2:3, :] = jnp.sum(dy * u, axis=0, keepdims=True)
        dc_ref[...] = (du * xb).astype(BF16)
        dx_ref[...] = (du * cg).astype(BF16)

    blk = pl.BlockSpec((L, LANES), lambda j: (0, j))
    return pl.pallas_call(
        body, name="conv_bwd", grid=(H // LANES,),
        in_specs=_conv_specs(L, H) + [pl.BlockSpec((3, LANES), lambda j: (0, j)), blk],
        out_specs=[blk, blk, blk, pl.BlockSpec((3, LANES), lambda j: (0, j))],
        out_shape=[_sds((L, H), BF16)] * 3 + [_sds((3, H), F32)],
        compiler_params=_params(("parallel",), 56),
    )(proj, proj, proj, conv_w, dcb)


def _gate_specs(tm, H):
    return [pl.BlockSpec((tm, H), lambda i, k=k: (i, k)) for k in (7, 8, 9, 10)]


def _fwd_mix(og, cb, proj, x, wa3, wb3, wout, g_ffn, H):
    L, D = x.shape
    sw = wa3.shape[2]
    tm = min(L, 256)

    def body(o_ref, cb_ref, ga0, ga1, gb0, gb1, x_ref, wa_ref, wb_ref, wo_ref, g_ref,
             ya_ref, yb_ref, m_ref, x1_ref, h2_ref):
        o, cbv = o_ref[...], cb_ref[...]
        for k in range(N_CHIPS):
            cs = slice(k * sw, (k + 1) * sw)
            gs = slice((k % 2) * sw, (k % 2 + 1) * sw)
            gar, gbr = (ga0, ga1)[k // 2], (gb0, gb1)[k // 2]
            ya, yb = _nn(o, wa_ref[k]), _nn(cbv, wb_ref[k])
            ya_ref[:, cs] = ya.astype(BF16)
            yb_ref[:, cs] = yb.astype(BF16)
            m_ref[:, cs] = (_sigmoid(gar[:, gs]) * ya + _sigmoid(gbr[:, gs]) * yb).astype(BF16)
        x1 = x_ref[...] + _nn(m_ref[...], wo_ref[...])
        x1_ref[...] = x1
        _, xh = _rms_stats(x1)
        h2_ref[...] = (xh * g_ref[...]).astype(BF16)

    row = lambda w: pl.BlockSpec((tm, w), lambda i: (i, 0))
    full = lambda a: pl.BlockSpec(a.shape, lambda i: (0,) * a.ndim)
    return pl.pallas_call(
        body, name="fwd_mix", grid=(L // tm,),
        in_specs=[row(H), row(H)] + _gate_specs(tm, H) + [row(D), full(wa3), full(wb3), full(wout),
                                                           full(g_ffn)],
        out_specs=[row(D)] * 5,
        out_shape=[_sds((L, D), BF16)] * 3 + [_sds((L, D), F32), _sds((L, D), BF16)],
        compiler_params=_params(("parallel",), 48),
    )(og, cb, proj, proj, proj, proj, x, wa3, wb3, wout, g_ffn)


def _bwd_mix(dx1b, proj, ya, yb, wa3, wb3, wout, H):
    L, D = dx1b.shape
    sw = wa3.shape[2]
    tm = min(L, 256)

    def body(dx_ref, ga0, ga1, gb0, gb1, ya_ref, yb_ref, wa_ref, wb_ref, wo_ref,
             dya_ref, dyb_ref, dga_ref, dgb_ref, do_ref, dcb_ref):
        dm = _nt(dx_ref[...], wo_ref[...])
        do = jnp.zeros((tm, H), F32)
        dcb = jnp.zeros((tm, H), F32)
        for k in range(N_CHIPS):
            cs = slice(k * sw, (k + 1) * sw)
            gs = slice((k % 2) * sw, (k % 2 + 1) * sw)
            gar, gbr = (ga0, ga1)[k // 2], (gb0, gb1)[k // 2]
            sa, sb = _sigmoid(gar[:, gs]), _sigmoid(gbr[:, gs])
            dmk = dm[:, cs]
            dga_ref[:, cs] = (dmk * ya_ref[:, cs].astype(F32) * sa * (1.0 - sa)).astype(BF16)
            dgb_ref[:, cs] = (dmk * yb_ref[:, cs].astype(F32) * sb * (1.0 - sb)).astype(BF16)
            dya, dyb = (dmk * sa).astype(BF16), (dmk * sb).astype(BF16)
            dya_ref[:, cs] = dya
            dyb_ref[:, cs] = dyb
            do = do + _nt(dya, wa_ref[k])
            dcb = dcb + _nt(dyb, wb_ref[k])
        do_ref[...] = do
        dcb_ref[...] = dcb

    row = lambda w: pl.BlockSpec((tm, w), lambda i: (i, 0))
    full = lambda a: pl.BlockSpec(a.shape, lambda i: (0,) * a.ndim)
    return pl.pallas_call(
        body, name="bwd_mix", grid=(L // tm,),
        in_specs=[row(D)] + _gate_specs(tm, H) + [row(D), row(D), full(wa3), full(wb3), full(wout)],
        out_specs=[row(D)] * 4 + [row(H)] * 2,
        out_shape=[_sds((L, D), BF16)] * 4 + [_sds((L, H), F32)] * 2,
        compiler_params=_params(("parallel",), 48),
    )(dx1b, proj, proj, proj, proj, ya, yb, wa3, wb3, wout)


def _fwd_ffn_up(h2, wgt, wut):
    L, D = h2.shape
    F = wgt.shape[0]
    tn = F // 2
    tm = min(L, 512)

    def body(h_ref, wg_ref, wu_ref, a_ref, b_ref, s_ref):
        h = h_ref[...]
        a, b = _nt(h, wg_ref[...]), _nt(h, wu_ref[...])
        a_ref[...] = a.astype(BF16)
        b_ref[...] = b.astype(BF16)
        s_ref[...] = (a * _sigmoid(a) * b).astype(BF16)

    wspec = pl.BlockSpec((tn, D), lambda j, i: (j, 0))
    ospec = pl.BlockSpec((tm, tn), lambda j, i: (i, j))
    return pl.pallas_call(
        body, name="fwd_ffn_up", grid=(2, L // tm),
        in_specs=[pl.BlockSpec((tm, D), lambda j, i: (i, 0)), wspec, wspec],
        out_specs=[ospec] * 3,
        out_shape=[_sds((L, F), BF16)] * 3,
        compiler_params=_params(("parallel", "parallel"), 48),
    )(h2, wgt, wut)


def _fwd_down_loss(s, wd, x1, target, g_final):
    L, D = x1.shape
    F = wd.shape[0]
    tm = min(L, 256)

    def body(s_ref, wd_ref, x1_ref, t_ref, g_ref, dx_ref, dxb_ref, red_ref):
        @pl.when(pl.program_id(0) == 0)
        def _():
            red_ref[...] = jnp.zeros_like(red_ref)

        g = g_ref[...]
        r, xh = _rms_stats(x1_ref[...] + _nn(s_ref[...], wd_ref[...]))
        e = xh * g - t_ref[...]
        dy = e * (1.0 / D)
        dx = _rms_bwd(dy * g, xh, r)
        dx_ref[...] = dx
        dxb_ref[...] = dx.astype(BF16)
        red_ref[0:1, :] += jnp.sum(dy * xh, axis=0, keepdims=True)
        red_ref[1:2, :] += jnp.broadcast_to(0.5 * jnp.sum(e * e) * (1.0 / D), (1, D))

    row = pl.BlockSpec((tm, D), lambda i: (i, 0))
    return pl.pallas_call(
        body, name="fwd_down_loss", grid=(L // tm,),
        in_specs=[pl.BlockSpec((tm, F), lambda i: (i, 0)), pl.BlockSpec((F, D), lambda i: (0, 0)),
                  row, row, pl.BlockSpec((1, D), lambda i: (0, 0))],
        out_specs=[row, row, pl.BlockSpec((8, D), lambda i: (0, 0))],
        out_shape=[_sds((L, D), F32), _sds((L, D), BF16), _sds((8, D), F32)],
        compiler_params=_params(("arbitrary",), 48),
    )(s, wd, x1, target, g_final)


def _bwd_down(dx2b, wd, a, b):
    L, D = dx2b.shape
    F = wd.shape[0]
    tn = F // 2
    tm = min(L, 512)

    def body(dx_ref, wd_ref, a_ref, b_ref, da_ref, db_ref):
        ds = _nt(dx_ref[...], wd_ref[...])
        a, b = a_ref[...].astype(F32), b_ref[...].astype(F32)
        sg = _sigmoid(a)
        da_ref[...] = (ds * b * sg * (1.0 + a * (1.0 - sg))).astype(BF16)
        db_ref[...] = (ds * a * sg).astype(BF16)

    ospec = pl.BlockSpec((tm, tn), lambda j, i: (i, j))
    return pl.pallas_call(
        body, name="bwd_down", grid=(2, L // tm),
        in_specs=[pl.BlockSpec((tm, D), lambda j, i: (i, 0)),
                  pl.BlockSpec((tn, D), lambda j, i: (j, 0)), ospec, ospec],
        out_specs=[ospec] * 2,
        out_shape=[_sds((L, F), BF16)] * 2,
        compiler_params=_params(("parallel", "parallel"), 48),
    )(dx2b, wd, a, b)


def _bwd_ffn_dh(da, db, wgt, wut, x1, dx2, g_ffn):
    L, D = x1.shape
    F = wgt.shape[0]
    tm = min(L, 256)

    def body(da_ref, db_ref, wg_ref, wu_ref, x1_ref, dx2_ref, g_ref, dx_ref, dxb_ref, red_ref):
        @pl.when(pl.program_id(0) == 0)
        def _():
            red_ref[...] = jnp.zeros_like(red_ref)

        dh = _nn(da_ref[...], wg_ref[...]) + _nn(db_ref[...], wu_ref[...])
        r, xh = _rms_stats(x1_ref[...])
        red_ref[0:1, :] += jnp.sum(dh * xh, axis=0, keepdims=True)
        dx = dx2_ref[...] + _rms_bwd(dh * g_ref[...], xh, r)
        dx_ref[...] = dx
        dxb_ref[...] = dx.astype(BF16)

    row = pl.BlockSpec((tm, D), lambda i: (i, 0))
    aspec = pl.BlockSpec((tm, F), lambda i: (i, 0))
    wspec = pl.BlockSpec((F, D), lambda i: (0, 0))
    return pl.pallas_call(
        body, name="bwd_ffn_dh", grid=(L // tm,),
        in_specs=[aspec, aspec, wspec, wspec, row, row, pl.BlockSpec((1, D), lambda i: (0, 0))],
        out_specs=[row, row, pl.BlockSpec((8, D), lambda i: (0, 0))],
        out_shape=[_sds((L, D), F32), _sds((L, D), BF16), _sds((8, D), F32)],
        compiler_params=_params(("arbitrary",), 56),
    )(da, db, wgt, wut, x1, dx2, g_ffn)


def _bwd_in(dproj, w_in3, x, dx1, g_mix, after):
    L, D = x.shape
    tn = w_in3.shape[2]
    tm = min(L, 512)
    last = N_CHIPS - 1

    def body(dp_ref, w_ref, x_ref, dx1_ref, g_ref, after_ref, dx_ref, red_ref, acc_ref):
        i, k = pl.program_id(0), pl.program_id(1)
        part = _nt(dp_ref[...], w_ref[...])

        @pl.when(k == 0)
        def _():
            acc_ref[...] = part

        @pl.when(k > 0)
        def _():
            acc_ref[...] += part

        @pl.when((i == 0) & (k == last))
        def _():
            red_ref[...] = jnp.zeros_like(red_ref)

        @pl.when(k == last)
        def _():
            dh = acc_ref[...]
            r, xh = _rms_stats(x_ref[...])
            red_ref[0:1, :] += jnp.sum(dh * xh, axis=0, keepdims=True)
            dx_ref[...] = dx1_ref[...] + _rms_bwd(dh * g_ref[...], xh, r)

    row = pl.BlockSpec((tm, D), lambda i, k: (i, 0))
    return pl.pallas_call(
        body, name="bwd_in", grid=(L // tm, N_CHIPS),
        in_specs=[pl.BlockSpec((tm, tn), lambda i, k: (i, k)),
                  pl.BlockSpec((None, D, tn), lambda i, k: (k, 0, 0)),
                  row, row, pl.BlockSpec((1, D), lambda i, k: (0, 0)), ANY],
        out_specs=[row, pl.BlockSpec((8, D), lambda i, k: (0, 0))],
        out_shape=[_sds((L, D), F32), _sds((8, D), F32)],
        scratch_shapes=[pltpu.VMEM((tm, D), F32)],
        compiler_params=_params(("arbitrary", "arbitrary"), 48),
    )(dproj, w_in3, x, dx1, g_mix, after)


def _mm_tn(name, a, b, a_spec, b_spec, o_block, n_out, n_k):
    def body(a_ref, b_ref, o_ref):
        part = _tn(a_ref[...], b_ref[...])

        @pl.when(pl.program_id(1) == 0)
        def _():
            o_ref[...] = part

        @pl.when(pl.program_id(1) > 0)
        def _():
            o_ref[...] += part

    return pl.pallas_call(
        body, name=name, grid=(n_out, n_k),
        in_specs=[a_spec, b_spec],
        out_specs=pl.BlockSpec((None,) + o_block, lambda j, k: (j, 0, 0)),
        out_shape=_sds((n_out,) + o_block, F32),
        compiler_params=_params(("parallel", "arbitrary"), 56),
    )(a, b)


TK_TOKENS = 2048


def _dw_cols(name, a, b, n_cols):
    L, M = a.shape
    tk = min(L, TK_TOKENS)
    return _mm_tn(name, a, b, pl.BlockSpec((tk, M), lambda j, k: (k, 0)),
                  pl.BlockSpec((tk, n_cols), lambda j, k: (k, j)), (M, n_cols), N_CHIPS, L // tk)


def _dw_rows(name, a, b):
    L, M = a.shape
    N = b.shape[1]
    tk = min(L, TK_TOKENS)
    return _mm_tn(name, a, b, pl.BlockSpec((tk, M // N_CHIPS), lambda j, k: (k, j)),
                  pl.BlockSpec((tk, N), lambda j, k: (k, 0)), (M // N_CHIPS, N), N_CHIPS, L // tk)


def _dw_rows2(name, a, b):
    L, M = a.shape
    N = b.shape[1]
    tk = min(L, TK_TOKENS)
    return _mm_tn(name, a, b, pl.BlockSpec((tk, M // 2), lambda j, k: (k, j)),
                  pl.BlockSpec((tk, N), lambda j, k: (k, 0)), (M // 2, N), 2, L // tk)


def _place():
    x, y, c = lax.axis_index("x"), lax.axis_index("y"), lax.axis_index("c")
    chips = [(1 - x, y), (x, 1 - y), (1 - x, 1 - y)]
    return x, y, c, 2 * x + y, chips


def _remote(src, dst, send_sem, recv_sem, device):
    return pltpu.make_async_remote_copy(src_ref=src, dst_ref=dst, send_sem=send_sem,
                                        recv_sem=recv_sem, device_id=device, device_id_type=MESH)


def _half(ref, lead, c, r2):
    return ref.at[lead, pl.ds(pl.multiple_of(c * r2, 16), r2), :]


def _cast_place(name, w, chip_idx):
    r, cols = w.shape
    tr = r // 2

    def body(k_ref, w_ref, o_ref):
        o_ref[...] = w_ref[...].astype(BF16)

    return pl.pallas_call(
        body, name=name,
        grid_spec=pltpu.PrefetchScalarGridSpec(
            num_scalar_prefetch=1, grid=(2,),
            in_specs=[pl.BlockSpec((tr, cols), lambda i, k_ref: (i, 0))],
            out_specs=pl.BlockSpec((None, tr, cols), lambda i, k_ref: (k_ref[0], i, 0))),
        out_shape=_sds((N_CHIPS, r, cols), BF16),
        compiler_params=_params(("parallel",), 48),
    )(chip_idx, w)


def _in_hbm(arrays):
    return [pltpu.with_memory_space_constraint(a, pltpu.HBM) for a in arrays]


def _gather_copies(bufs, whole, send_sems, recv_sems):
    x, y, c, k, chips = _place()
    pairs = []
    for w, buf in enumerate(bufs):
        for j, (cx, cy) in enumerate(chips):
            if w in whole:
                mine, theirs = buf.at[k], buf.at[2 * cx + cy]
            else:
                r2 = buf.shape[1] // 2
                mine, theirs = _half(buf, k, c, r2), _half(buf, 2 * cx + cy, c, r2)
            sems = (send_sems.at[w * 3 + j], recv_sems.at[w * 3 + j])
            pairs.append((_remote(mine, mine, *sems, (cx, cy, c)), _remote(theirs, theirs, *sems, (x, y, c))))
    return pairs


def _gather_start(groups):
    flat = [b for bufs, _ in groups for b in bufs]
    nb, ng = len(flat), len(groups)

    def body(*refs):
        ins, sems, token = refs[:nb], refs[nb:nb + 2 * ng], refs[-1]
        pos = 0
        for g, (bufs, whole) in enumerate(groups):
            for send, _ in _gather_copies(ins[pos:pos + len(bufs)], whole, sems[2 * g], sems[2 * g + 1]):
                send.start()
            pos += len(bufs)
        token[...] = jnp.zeros_like(token)

    sem_shapes = []
    for bufs, _ in groups:
        sem_shapes += [pltpu.SemaphoreType.DMA((3 * len(bufs),))] * 2
    out = pl.pallas_call(
        body, name="gather_start",
        in_specs=[HBM] * nb, out_specs=tuple([SEM] * (2 * ng) + [HBM] * nb + [VMEM]),
        out_shape=tuple(sem_shapes + [pltpu.HBM(b.shape, b.dtype) for b in flat] + [_sds((8, LANES), F32)]),
        input_output_aliases={i: 2 * ng + i for i in range(nb)},
        compiler_params=pltpu.CompilerParams(has_side_effects=EFFECT),
    )(*_in_hbm(flat))
    sems, thru, pos = [], [], 2 * ng
    for g, (bufs, _) in enumerate(groups):
        sems.append((out[2 * g], out[2 * g + 1]))
        thru.append(list(out[pos:pos + len(bufs)]))
        pos += len(bufs)
    return sems, thru, out[-1]


def _gather_wait(name, bufs, whole, sems, after):
    nb = len(bufs)

    def body(*refs):
        ins, send_sems, recv_sems = refs[:nb], refs[nb], refs[nb + 1]
        for send, arrival in _gather_copies(ins, whole, send_sems, recv_sems):
            send.wait_send()
            arrival.wait_recv()

    return pl.pallas_call(
        body, name=name,
        in_specs=[HBM] * nb + [SEM, SEM, ANY], out_specs=[HBM] * nb,
        out_shape=[pltpu.HBM(b.shape, b.dtype) for b in bufs],
        input_output_aliases={i: i for i in range(nb)},
        compiler_params=pltpu.CompilerParams(has_side_effects=EFFECT),
    )(*bufs, sems[0], sems[1], after)


def _gather_forward(name, bufs):
    n = len(bufs)

    def body(*refs):
        outs = refs[n:2 * n]
        send_sems, recv_sems = refs[2 * n:]
        x, y, c, _, chips = _place()
        sends = []
        for w in range(n):
            r2 = outs[w].shape[1] // 2
            for j, (cx, cy) in enumerate(chips):
                landed = _half(outs[w], 2 * cx + cy, c, r2)
                sends.append(_remote(landed, landed, send_sems.at[w * 3 + j], recv_sems.at[w * 3 + j],
                                     (x, y, 1 - c)))
        for cp in sends:
            cp.start()
        for w in range(n):
            r2 = outs[w].shape[1] // 2
            for j, (cx, cy) in enumerate(chips):
                got = _half(outs[w], 2 * cx + cy, 1 - c, r2)
                _remote(got, got, send_sems.at[w * 3 + j], recv_sems.at[w * 3 + j], (x, y, c)).wait_recv()
        for cp in sends:
            cp.wait_send()

    return pl.pallas_call(
        body, name=name,
        in_specs=[ANY] * n, out_specs=[ANY] * n,
        out_shape=[_sds(b.shape, b.dtype) for b in bufs],
        input_output_aliases={i: i for i in range(n)},
        scratch_shapes=[pltpu.SemaphoreType.DMA((n * 3,)), pltpu.SemaphoreType.DMA((n * 3,))],
    )(*bufs)


def _rs_sibling(name, grads):
    n = len(grads)

    def body(*refs):
        ins, outs = refs[:n], refs[n:2 * n]
        send_sems, recv_sems = refs[2 * n:]
        x, y, c, _, _ = _place()
        copies = []
        for w in range(n):
            r2 = ins[w].shape[1] // 2
            copies.append(_remote(_half(ins[w], slice(None), 1 - c, r2), outs[w],
                                  send_sems.at[w], recv_sems.at[w], (x, y, 1 - c)))
        for cp in copies:
            cp.start()
        for cp in copies:
            cp.wait()

    return pl.pallas_call(
        body, name=name,
        in_specs=[ANY] * n, out_specs=[ANY] * n,
        out_shape=[_sds((N_CHIPS, g.shape[1] // 2, g.shape[2]), F32) for g in grads],
        scratch_shapes=[pltpu.SemaphoreType.DMA((n,)), pltpu.SemaphoreType.DMA((n,))],
    )(*grads)


def _rs_add(name, grad3, from_sibling, c_idx):
    _, r2, cols = from_sibling.shape

    def body(c_ref, g_ref, s_ref, o_ref):
        o_ref[...] = (g_ref[...] + s_ref[...]).astype(BF16)

    return pl.pallas_call(
        body, name=name,
        grid_spec=pltpu.PrefetchScalarGridSpec(
            num_scalar_prefetch=1, grid=(N_CHIPS,),
            in_specs=[pl.BlockSpec((None, r2, cols), lambda k, c_ref: (k, c_ref[0], 0)),
                      pl.BlockSpec((None, r2, cols), lambda k, c_ref: (k, 0, 0))],
            out_specs=pl.BlockSpec((None, r2, cols), lambda k, c_ref: (k, 0, 0))),
        out_shape=_sds(from_sibling.shape, BF16),
        compiler_params=_params(("parallel",), 48),
    )(c_idx, grad3, from_sibling)


def _rs_ici_copies(partials, landings, send_sems, recv_sems):
    x, y, c, _, chips = _place()
    copies = []
    for w in range(len(partials)):
        for j, (cx, cy) in enumerate(chips):
            copies.append(_remote(partials[w].at[2 * cx + cy], landings[w].at[j],
                                  send_sems.at[w * 3 + j], recv_sems.at[w * 3 + j], (cx, cy, c)))
    return copies


def _rs_ici_start(name, partials):
    n = len(partials)
    landings = [lax.empty((3,) + p.shape[1:], BF16) for p in partials]

    def body(*refs):
        ins, lands, send_sems, recv_sems, token = refs[:n], refs[n:2 * n], refs[2 * n], refs[2 * n + 1], refs[-1]
        for cp in _rs_ici_copies(ins, lands, send_sems, recv_sems):
            cp.start()
        token[...] = jnp.zeros_like(token)

    flat = list(partials) + landings
    out = pl.pallas_call(
        body, name=name,
        in_specs=[HBM] * (2 * n), out_specs=tuple([SEM, SEM] + [HBM] * (2 * n) + [VMEM]),
        out_shape=tuple([pltpu.SemaphoreType.DMA((3 * n,))] * 2 + [pltpu.HBM(a.shape, a.dtype) for a in flat]
                        + [_sds((8, LANES), F32)]),
        input_output_aliases={i: 2 + i for i in range(2 * n)},
        compiler_params=pltpu.CompilerParams(has_side_effects=EFFECT),
    )(*_in_hbm(flat))
    return (out[0], out[1]), list(out[2:2 + 2 * n]), out[-1]


def _rs_ici_wait(name, sems, in_flight, after):
    n = len(in_flight) // 2

    def body(*refs):
        ins, lands, send_sems, recv_sems = refs[:n], refs[n:2 * n], refs[2 * n], refs[2 * n + 1]
        for cp in _rs_ici_copies(ins, lands, send_sems, recv_sems):
            cp.wait_send()
            cp.wait_recv()

    out = pl.pallas_call(
        body, name=name,
        in_specs=[HBM] * (2 * n) + [SEM, SEM, ANY], out_specs=[HBM] * (2 * n),
        out_shape=[pltpu.HBM(a.shape, a.dtype) for a in in_flight],
        input_output_aliases={i: i for i in range(2 * n)},
        compiler_params=pltpu.CompilerParams(has_side_effects=EFFECT),
    )(*in_flight, sems[0], sems[1], after)
    return list(out[:n]), list(out[n:])


def _rs_sum(name, partials, received, place_idx):
    _, r2, cols = partials.shape
    nb = 2
    tr = r2 // nb

    def body(idx_ref, p_ref, r_ref, o_ref):
        o_ref[...] = ((p_ref[...].astype(F32) + r_ref[0].astype(F32))
                      + (r_ref[1].astype(F32) + r_ref[2].astype(F32)))

    return pl.pallas_call(
        body, name=name,
        grid_spec=pltpu.PrefetchScalarGridSpec(
            num_scalar_prefetch=1, grid=(nb,),
            in_specs=[pl.BlockSpec((None, tr, cols), lambda i, idx: (idx[0], i, 0)),
                      pl.BlockSpec((3, tr, cols), lambda i, idx: (0, i, 0))],
            out_specs=pl.BlockSpec((tr, cols), lambda i, idx: (idx[1] * nb + i, 0))),
        out_shape=_sds((2 * r2, cols), F32),
        compiler_params=_params(("parallel",), 48),
    )(place_idx, partials, received)


def _rs_share(name, shards):
    n = len(shards)

    def body(*refs):
        outs = refs[n:2 * n]
        send_sems, recv_sems = refs[2 * n:]
        x, y, c, _, _ = _place()
        sends = []
        for w in range(n):
            r2 = outs[w].shape[0] // 2
            mine = outs[w].at[pl.ds(pl.multiple_of(c * r2, 8), r2), :]
            sends.append(_remote(mine, mine, send_sems.at[w], recv_sems.at[w], (x, y, 1 - c)))
        for cp in sends:
            cp.start()
        for w in range(n):
            r2 = outs[w].shape[0] // 2
            theirs = outs[w].at[pl.ds(pl.multiple_of((1 - c) * r2, 8), r2), :]
            _remote(theirs, theirs, send_sems.at[w], recv_sems.at[w], (x, y, c)).wait_recv()
        for cp in sends:
            cp.wait_send()

    return pl.pallas_call(
        body, name=name,
        in_specs=[ANY] * n, out_specs=[ANY] * n,
        out_shape=[_sds(s.shape, F32) for s in shards],
        input_output_aliases={i: i for i in range(n)},
        scratch_shapes=[pltpu.SemaphoreType.DMA((n,)), pltpu.SemaphoreType.DMA((n,))],
    )(*shards)


def _small_allreduce(red_mix, red_ffn, red_final, red_hg, g_conv):
    rows = N_SMALL_ROWS
    D = red_mix.shape[1]
    H = red_hg.shape[1]

    def body(mix_ref, ffn_ref, fin_ref, hg_ref, cv_ref, sum_ref, all_ref, in_ref, send_sems, recv_sems):
        in_ref[...] = jnp.zeros_like(in_ref)
        in_ref[0:1, :] = mix_ref[0:1, :]
        in_ref[1:2, :] = ffn_ref[0:1, :]
        in_ref[2:3, :] = fin_ref[0:1, :]
        gam = hg_ref[1:2, 0:HEAD_DIM]
        for h in range(1, H // HEAD_DIM):
            gam = gam + hg_ref[1:2, h * HEAD_DIM:(h + 1) * HEAD_DIM]
        in_ref[3:4, 0:HEAD_DIM] = gam
        in_ref[3:4, HEAD_DIM:2 * HEAD_DIM] = fin_ref[1:2, 0:HEAD_DIM]
        in_ref[4:5, 0:H] = hg_ref[0:1, :]
        in_ref[6:9, 0:H] = cv_ref[...]
        x, y, c, _, _ = _place()
        me = 4 * x + 2 * y + c
        all_ref[me] = in_ref[...]
        copies = []
        for m in range(1, 8):
            mx, my, mc = (m >> 2) & 1, (m >> 1) & 1, m & 1
            px, py, pc = x ^ mx, y ^ my, c ^ mc
            copies.append((_remote(in_ref, all_ref.at[me], send_sems.at[m - 1], recv_sems.at[m - 1],
                                   (px, py, pc)), 4 * px + 2 * py + pc, m))
        for cp, _, _ in copies:
            cp.start()
        for _, peer, m in copies:
            _remote(in_ref, all_ref.at[peer], send_sems.at[m - 1], recv_sems.at[m - 1],
                    (x, y, c)).wait_recv()
        for cp, _, _ in copies:
            cp.wait_send()
        total = all_ref[0]
        for d in range(1, 8):
            total = total + all_ref[d]
        sum_ref[...] = total

    return pl.pallas_call(
        body, name="small_allreduce",
        in_specs=[VMEM] * 5, out_specs=[VMEM, VMEM],
        out_shape=[_sds((rows, D), F32), _sds((8, rows, D), F32)],
        scratch_shapes=[pltpu.VMEM((rows, D), F32), pltpu.SemaphoreType.DMA((7,)),
                        pltpu.SemaphoreType.DMA((7,))],
    )(red_mix, red_ffn, red_final, red_hg, g_conv)[0]


def _adamw_math(w, g, m, v):
    m = ADAM_B1 * m + (1.0 - ADAM_B1) * g
    v = ADAM_B2 * v + (1.0 - ADAM_B2) * jnp.square(g)
    m_hat = m / (1.0 - ADAM_B1 ** ADAM_STEP)
    v_hat = v / (1.0 - ADAM_B2 ** ADAM_STEP)
    delta = -ADAM_LR * (m_hat / (jnp.sqrt(v_hat) + ADAM_EPS) + ADAM_WD * w)
    return delta, m, v


def _adamw(name, g, w, m, v):
    r, cols = g.shape
    tr = r // 4

    def body(g_ref, w_ref, m_ref, v_ref, d_ref, mo_ref, vo_ref):
        d_ref[...], mo_ref[...], vo_ref[...] = _adamw_math(w_ref[...], g_ref[...], m_ref[...], v_ref[...])

    blk = pl.BlockSpec((tr, cols), lambda i: (i, 0))
    return pl.pallas_call(
        body, name=name, grid=(r // tr,),
        in_specs=[blk] * 4, out_specs=[blk] * 3, out_shape=[_sds((r, cols), F32)] * 3,
        compiler_params=_params(("parallel",), 48),
    )(g, w, m, v)


def _small_update(total, chip_idx, ws, ms, vs):
    n = len(ws)
    H = ws[1].shape[1]

    def body(idx_ref, tot_ref, *refs):
        w, m, v, outs = refs[:n], refs[n:2 * n], refs[2 * n:3 * n], refs[3 * n:]
        chip = idx_ref[0]
        p0 = _lower_bound(w[1][...])
        dl0 = p0 * (1.0 - p0) * tot_ref[4:5, 0:H]
        conv = jnp.zeros((3, LANES), F32)
        for k in range(N_CHIPS):
            conv = jnp.where(chip == k, tot_ref[6:9, k * LANES:(k + 1) * LANES], conv)
        grads = [tot_ref[0:1, :], None, tot_ref[3:4, 0:HEAD_DIM], conv, tot_ref[1:2, :], tot_ref[2:3, :]]
        for p in range(n):
            g_ref, d_ref, mo_ref, vo_ref = outs[4 * p:4 * p + 4]
            if p == 1:
                for row, g in ((slice(0, 1), dl0), (slice(1, 2), -dl0)):
                    g_ref[row, :] = g
                    d_ref[row, :], mo_ref[row, :], vo_ref[row, :] = _adamw_math(
                        w[p][row, :], g, m[p][row, :], v[p][row, :])
            else:
                g_ref[...] = grads[p]
                d_ref[...], mo_ref[...], vo_ref[...] = _adamw_math(w[p][...], grads[p], m[p][...], v[p][...])
        outs[4 * n][...] = tot_ref[3:4, HEAD_DIM:2 * HEAD_DIM]

    full = lambda a: pl.BlockSpec(a.shape, lambda i, idx: (0,) * a.ndim)
    out_shape = [_sds(w.shape, F32) for w in ws for _ in range(4)] + [_sds((1, LANES), F32)]
    return pl.pallas_call(
        body, name="small_update",
        grid_spec=pltpu.PrefetchScalarGridSpec(
            num_scalar_prefetch=1, grid=(1,),
            in_specs=[full(total)] + [full(a) for a in ws + ms + vs],
            out_specs=[full(s) for s in out_shape]),
        out_shape=out_shape,
    )(chip_idx, total, *ws, *ms, *vs)


def kernel(x, norm_mix_g, w_in, lower_bounds, hg_norm_g, conv_w, w_branch_a, w_branch_b, w_out, norm_ffn_g, w_ffn_gate, w_ffn_up, w_ffn_down, norm_final_g, loss_target, m_norm_mix_g, m_w_in, m_lower_bounds, m_hg_norm_g, m_conv_w, m_w_branch_a, m_w_branch_b, m_w_out, m_norm_ffn_g, m_w_ffn_gate, m_w_ffn_up, m_w_ffn_down, m_norm_final_g, v_norm_mix_g, v_w_in, v_lower_bounds, v_hg_norm_g, v_conv_w, v_w_branch_a, v_w_branch_b, v_w_out, v_norm_ffn_g, v_w_ffn_gate, v_w_ffn_up, v_w_ffn_down, v_norm_final_g):
    _, L, D = x.shape
    H = D // 2
    assert lower_bounds.shape == (2, H) and hg_norm_g.shape == (1, HEAD_DIM)
    assert conv_w.shape == (1, 3, LANES) and w_in.shape[2] * N_CHIPS == 11 * H
    x2d, target = x.reshape(L, D), loss_target.reshape(L, D)
    g_final = norm_final_g.reshape(1, D)
    chip = 2 * lax.axis_index("x") + lax.axis_index("y")
    core = lax.axis_index("c")

    tr = lambda w: jnp.transpose(w[0])
    big = [w_in[0], w_branch_a[0], w_branch_b[0], w_out[0], tr(w_ffn_gate), tr(w_ffn_up), w_ffn_down[0]]
    big_m = [m_w_in[0], m_w_branch_a[0], m_w_branch_b[0], m_w_out[0], tr(m_w_ffn_gate), tr(m_w_ffn_up),
             m_w_ffn_down[0]]
    big_v = [v_w_in[0], v_w_branch_a[0], v_w_branch_b[0], v_w_out[0], tr(v_w_ffn_gate), tr(v_w_ffn_up),
             v_w_ffn_down[0]]
    names = ["w_in", "w_branch_a", "w_branch_b", "w_out", "w_ffn_gate", "w_ffn_up", "w_ffn_down"]

    chip_idx = chip.reshape(1).astype(jnp.int32)
    placed = [_cast_place("place_" + nm, w, chip_idx) for nm, w in zip(names, big)]
    conv_placed = lax.dynamic_update_slice(jnp.zeros((N_CHIPS, 3, LANES), F32), conv_w, (chip, 0, 0))
    sems, in_flight, token = _gather_start([([placed[0], conv_placed], {1}), (placed[1:], set())])
    w_in_landed, conv_all = _gather_wait("gather_wait_in", in_flight[0], {1}, sems[0], token)
    (w_in3,) = _gather_forward("gather_fwd_in", [w_in_landed])
    conv_full = jnp.transpose(conv_all, (1, 0, 2)).reshape(3, H)

    h, proj = _fwd_proj(x2d, norm_mix_g, w_in3)
    landed = _gather_wait("gather_wait_rest", in_flight[1], set(), sems[1], h)
    wa3, wb3, wout3, wgt3, wut3, wd3 = _gather_forward("gather_fwd_rest", landed)
    wout = wout3.reshape(D, D)
    d_ff = N_CHIPS * wd3.shape[1]
    wgt, wut, wd = wgt3.reshape(d_ff, D), wut3.reshape(d_ff, D), wd3.reshape(d_ff, D)
    og, o_pre, s_saved = _hgrn_fwd(proj, lower_bounds, hg_norm_g, H)
    cb = _conv_fwd(proj, conv_full, H)
    ya, yb, merged, x1, h2 = _fwd_mix(og, cb, proj, x2d, wa3, wb3, wout, norm_ffn_g, H)
    ffn_a, ffn_b, ffn_s = _fwd_ffn_up(h2, wgt, wut)
    dx2, dx2b, red_final = _fwd_down_loss(ffn_s, wd, x1, target, g_final)

    shards3 = lambda g: g.reshape(N_CHIPS, d_ff // N_CHIPS, D)
    da, db = _bwd_down(dx2b, wd, ffn_a, ffn_b)
    g_wd = shards3(_dw_rows2("dw_ffn_down", ffn_s, dx2b))
    dx1, dx1b, red_ffn = _bwd_ffn_dh(da, db, wgt, wut, x1, dx2, norm_ffn_g)
    g_wg = shards3(_dw_rows2("dw_ffn_gate", da, h2))
    g_wu = shards3(_dw_rows2("dw_ffn_up", db, h2))
    dya, dyb, dga, dgb, d_o, d_cb = _bwd_mix(dx1b, proj, ya, yb, wa3, wb3, wout, H)
    g_wout = _dw_rows("dw_out", merged, dx1b)
    g_wa = _dw_cols("dw_branch_a", og, dya, D // N_CHIPS)
    g_wb = _dw_cols("dw_branch_b", cb, dyb, D // N_CHIPS)

    c_idx = core.reshape(1).astype(jnp.int32)
    place_idx = jnp.stack([chip, core]).astype(jnp.int32)

    def rs_begin(tag, js, grads):
        from_sib = _rs_sibling("rs_sibling_" + tag, grads)
        partials = [_rs_add("rs_add_" + names[j], g, s, c_idx) for j, g, s in zip(js, grads, from_sib)]
        return _rs_ici_start("rs_ici_start_" + tag, partials)

    def rs_end(tag, js, started, after):
        sems, in_flight, _ = started
        partials, received = _rs_ici_wait("rs_ici_wait_" + tag, sems, in_flight, after)
        halves = [_rs_sum("rs_sum_" + names[j], p, r, place_idx) for j, p, r in zip(js, partials, received)]
        return _rs_share("rs_share_" + tag, halves)

    rest = [1, 2, 3, 4, 5, 6]
    rest_started = rs_begin("rest", rest, [g_wa, g_wb, g_wout, g_wg, g_wu, g_wd])
    dcg, dbg, dxb, g_conv = _conv_bwd(proj, conv_full, d_cb, H)
    dq, df, dv, dg, red_hg = _hgrn_bwd(proj, lower_bounds, hg_norm_g, o_pre, d_o, s_saved, H,
                                       rest_started[2])
    dproj = jnp.concatenate([dq, df, dv, dg, dcg, dbg, dxb, dga, dgb], axis=1)
    g_win = _dw_cols("dw_in", h, dproj, w_in3.shape[2])
    in_started = rs_begin("in", [0], [g_win])
    grad_x, red_mix = _bwd_in(dproj, w_in3, x2d, dx1, norm_mix_g, in_started[2])
    shard_grads = (rs_end("in", [0], in_started, grad_x)
                   + rs_end("rest", rest, rest_started, in_started[2]))
    big_out = [_adamw("adamw_" + nm, g, w, m, v)
               for nm, g, w, m, v in zip(names, shard_grads, big, big_m, big_v)]

    total = _small_allreduce(red_mix, red_ffn, red_final, red_hg, g_conv)

    def smalls(mix, lb, hg, cw, ffn, fin):
        return [mix, lb, hg, cw[0], ffn, fin.reshape(1, D)]

    small_out = _small_update(
        total, chip_idx,
        smalls(norm_mix_g, lower_bounds, hg_norm_g, conv_w, norm_ffn_g, norm_final_g),
        smalls(m_norm_mix_g, m_lower_bounds, m_hg_norm_g, m_conv_w, m_norm_ffn_g, m_norm_final_g),
        smalls(v_norm_mix_g, v_lower_bounds, v_hg_norm_g, v_conv_w, v_norm_ffn_g, v_norm_final_g))

    def outputs(i):
        big_i = [shard_grads[j] if i == 0 else big_out[j][i - 1] for j in range(7)]
        mix, lb, hg, cw, ffn, fin = [small_out[4 * p + i] for p in range(6)]
        return [mix, big_i[0][None], lb, hg, cw[None], big_i[1][None], big_i[2][None], big_i[3][None], ffn,
                big_i[4].T[None], big_i[5].T[None], big_i[6][None], fin.reshape(D)]

    outs = [small_out[24][0, 0], grad_x.reshape(1, L, D)]
    for i in range(4):
        outs += outputs(i)
    return tuple(outs)
```

```python
import functools

import jax
import jax.numpy as jnp
from jax import lax
from jax.experimental import pallas as pl
from jax.experimental.pallas import tpu as pltpu

F32 = jnp.float32
BF16 = jnp.bfloat16
EPS = 1e-6
CHUNK = 32
HEAD_DIM = 128
LANES = 128
N_CHIPS = 4
N_SMALL_ROWS = 16

ADAM_LR = 0.001
ADAM_B1 = 0.9
ADAM_B2 = 0.999
ADAM_EPS = 1e-08
ADAM_WD = 0.01
ADAM_STEP = 10

MESH = pl.DeviceIdType.MESH
ANY = pl.BlockSpec(memory_space=pl.ANY)
VMEM = pl.BlockSpec(memory_space=pltpu.VMEM)
HBM = pl.BlockSpec(memory_space=pltpu.HBM)
SEM = pl.BlockSpec(memory_space=pltpu.SEMAPHORE)
EFFECT = pltpu.SideEffectType.DATAFLOW_SIDE_EFFECTING


def _sds(shape, dtype):
    return jax.ShapeDtypeStruct(shape, dtype)


def _params(semantics, vmem_mb):
    return pltpu.CompilerParams(dimension_semantics=semantics, vmem_limit_bytes=vmem_mb << 20)


def _nn(a, b):
    return lax.dot_general(a, b, (((1,), (0,)), ((), ())), preferred_element_type=F32)


def _nt(a, b):
    return lax.dot_general(a, b, (((1,), (1,)), ((), ())), preferred_element_type=F32)


def _tn(a, b):
    return lax.dot_general(a, b, (((0,), (0,)), ((), ())), preferred_element_type=F32)


def _sigmoid(x):
    return jax.nn.sigmoid(x)


def _rms_stats(x):
    r = lax.rsqrt(jnp.mean(x * x, axis=-1, keepdims=True) + EPS)
    return r, x * r


def _rms_bwd(dxh, xh, r):
    return r * (dxh - xh * jnp.mean(dxh * xh, axis=-1, keepdims=True))


def _fwd_proj(x, g_mix, w_int3):
    L, D = x.shape
    tn = w_int3.shape[1]
    tm = min(L, 1024)

    def body(x_ref, g_ref, w_ref, h_ref, p_ref):
        @pl.when(pl.program_id(1) == 0)
        def _():
            _, xh = _rms_stats(x_ref[...])
            h_ref[...] = (xh * g_ref[...]).astype(BF16)

        p_ref[...] = _nt(h_ref[...], w_ref[...])

    return pl.pallas_call(
        body, name="fwd_proj", grid=(L // tm, N_CHIPS),
        in_specs=[pl.BlockSpec((tm, D), lambda i, j: (i, 0)),
                  pl.BlockSpec((1, D), lambda i, j: (0, 0)),
                  pl.BlockSpec((None, tn, D), lambda i, j: (j, 0, 0))],
        out_specs=[pl.BlockSpec((tm, D), lambda i, j: (i, 0)),
                   pl.BlockSpec((tm, tn), lambda i, j: (i, j))],
        out_shape=[_sds((L, D), BF16), _sds((L, N_CHIPS * tn), F32)],
        compiler_params=_params(("parallel", "arbitrary"), 48),
    )(x, g_mix, w_int3)


def _lower_bound(lbp):
    l0, l1 = lbp[0:1, :], lbp[1:2, :]
    m = jnp.maximum(l0, l1)
    e0, e1 = jnp.exp(l0 - m), jnp.exp(l1 - m)
    return e0 / (e0 + e1)


def _seg_scan(x, r32, forward):
    n = x.shape[0]
    s = 1
    while s < CHUNK:
        if forward:
            x = x + jnp.where(r32 >= s, pltpu.roll(x, s, 0), 0.0)
        else:
            x = x + jnp.where(r32 < CHUNK - s, pltpu.roll(x, n - s, 0), 0.0)
        s *= 2
    return x


def _bcast_row(x, row):
    n, w = x.shape
    nc = n // CHUNK
    x3 = x.reshape(nc, CHUNK, w)
    return jnp.broadcast_to(x3[:, row:row + 1, :], (nc, CHUNK, w)).reshape(n, w)


def _hgrn_prep(q_raw, f_raw, lb):
    r32 = lax.broadcasted_iota(jnp.int32, f_raw.shape, 0) & (CHUNK - 1)
    sig = _sigmoid(f_raw)
    f = lb + (1.0 - lb) * sig
    b = _seg_scan(jnp.log(f), r32, True)
    a = _bcast_row(b, CHUNK // 2 - 1)
    bl = _bcast_row(b, CHUNK - 1)
    sq = _sigmoid(q_raw)
    q = q_raw * sq * (HEAD_DIM ** -0.5)
    return dict(r32=r32, sig=sig, f=f, k=1.0 - f, b=b, a=a, bl=bl, sq=sq, q=q)


def _chunk_masks(n):
    ri = lax.broadcasted_iota(jnp.int32, (n, n), 0)
    ci = lax.broadcasted_iota(jnp.int32, (n, n), 1)
    same = (ri // CHUNK) == (ci // CHUNK)
    return same & (ci <= ri), same & (ri <= ci)


def _hgrn_fwd(proj, lower_bounds, gamma, H):
    L = proj.shape[0]
    nh = H // HEAD_DIM
    TL = min(L, 256)
    nc = TL // CHUNK

    def body(q_ref, f_ref, v_ref, g_ref, lbp_ref, gam_ref, og_ref, o_ref, s_ref, st_ref):
        @pl.when(pl.program_id(0) == 0)
        def _():
            st_ref[...] = jnp.zeros_like(st_ref)

        lb = _lower_bound(lbp_ref[...])
        gam = gam_ref[...]
        mask, _ = _chunk_masks(TL)
        rowc = lax.broadcasted_iota(jnp.int32, (TL, HEAD_DIM), 0) // CHUNK
        for h in range(nh):
            hs = slice(h * HEAD_DIM, (h + 1) * HEAD_DIM)
            p = _hgrn_prep(q_ref[:, hs], f_ref[:, hs], lb[:, hs])
            v = v_ref[:, hs]
            vb = v.astype(BF16)
            vt = v.T.astype(BF16)
            q_hat = (p["q"] * jnp.exp(p["b"] - p["a"])).astype(BF16)
            k_hat = (p["k"] * jnp.exp(p["a"] - p["b"])).astype(BF16)
            q_in = (p["q"] * jnp.exp(p["b"])).astype(BF16)
            k_out = (p["k"] * jnp.exp(p["bl"] - p["b"])).astype(BF16)
            dec = jnp.exp(p["bl"])
            att = jnp.where(mask, _nt(q_hat, k_hat), 0.0).astype(BF16)
            o_intra = _nn(att, vb)
            st = st_ref[h]
            for c in range(nc):
                rs = slice(c * CHUNK, (c + 1) * CHUNK)
                stb = st.astype(BF16)
                s_ref[c, h] = stb
                o_ref[rs, hs] = o_intra[rs] + _nt(q_in[rs], stb)
                k_c = jnp.where(rowc == c, k_out, jnp.zeros_like(k_out))
                st = st * dec[c * CHUNK:c * CHUNK + 1, :] + _nn(vt, k_c)
            st_ref[h] = st
            o = o_ref[:, hs]
            _, xh = _rms_stats(o)
            gr = g_ref[:, hs]
            og_ref[:, hs] = (xh * gam * (gr * _sigmoid(gr))).astype(BF16)

    col = lambda k: pl.BlockSpec((TL, H), lambda i, k=k: (i, k))
    return pl.pallas_call(
        body, name="hgrn_fwd", grid=(L // TL,),
        in_specs=[col(0), col(1), col(2), col(3),
                  pl.BlockSpec(lower_bounds.shape, lambda i: (0, 0)),
                  pl.BlockSpec(gamma.shape, lambda i: (0, 0))],
        out_specs=[pl.BlockSpec((TL, H), lambda i: (i, 0)),
                   pl.BlockSpec((TL, H), lambda i: (i, 0)),
                   pl.BlockSpec((nc, nh, HEAD_DIM, HEAD_DIM), lambda i: (i, 0, 0, 0))],
        out_shape=[_sds((L, H), BF16), _sds((L, H), F32),
                   _sds((L // CHUNK, nh, HEAD_DIM, HEAD_DIM), BF16)],
        scratch_shapes=[pltpu.VMEM((nh, HEAD_DIM, HEAD_DIM), F32)],
        compiler_params=_params(("arbitrary",), 48),
    )(proj, proj, proj, proj, lower_bounds, gamma)


def _hgrn_bwd(proj, lower_bounds, gamma, o_pre, d_out, s_saved, H, after):
    L = proj.shape[0]
    nh = H // HEAD_DIM
    TL = min(L, 256)
    nc = TL // CHUNK
    nt = L // TL

    def body(q_ref, f_ref, v_ref, g_ref, lbp_ref, gam_ref, o_ref, d_ref, s_ref, after_ref,
             dq_ref, df_ref, dv_ref, dg_ref, red_ref, dst_ref, dsall_ref, tmp_ref):
        @pl.when(pl.program_id(0) == 0)
        def _():
            dst_ref[...] = jnp.zeros_like(dst_ref)
            red_ref[...] = jnp.zeros_like(red_ref)

        lb = _lower_bound(lbp_ref[...])
        gam = gam_ref[...]
        mask, mask_t = _chunk_masks(TL)
        rowc = lax.broadcasted_iota(jnp.int32, (TL, HEAD_DIM), 0) // CHUNK
        for h in range(nh):
            hs = slice(h * HEAD_DIM, (h + 1) * HEAD_DIM)
            qr, gr, lbh = q_ref[:, hs], g_ref[:, hs], lb[:, hs]
            p = _hgrn_prep(qr, f_ref[:, hs], lbh)
            vb = v_ref[:, hs].astype(BF16)
            eba, eab = jnp.exp(p["b"] - p["a"]), jnp.exp(p["a"] - p["b"])
            eb, elb = jnp.exp(p["b"]), jnp.exp(p["bl"] - p["b"])
            dec = jnp.exp(p["bl"])
            q_hat, k_hat = p["q"] * eba, p["k"] * eab
            q_in, k_out = p["q"] * eb, p["k"] * elb
            q_hat_b, k_hat_b = q_hat.astype(BF16), k_hat.astype(BF16)
            q_in_b, k_out_b = q_in.astype(BF16), k_out.astype(BF16)

            o, dout = o_ref[:, hs], d_ref[:, hs]
            sg = _sigmoid(gr)
            r, xh = _rms_stats(o)
            dg_ref[:, hs] = (dout * (xh * gam) * (sg * (1.0 + gr * (1.0 - sg)))).astype(BF16)
            dn = dout * (gr * sg)
            red_ref[1:2, hs] += jnp.sum(dn * xh, axis=0, keepdims=True)
            do = _rms_bwd(dn * gam, xh, r)
            dob = do.astype(BF16)
            dot_b = do.T.astype(BF16)

            att_t = jnp.where(mask_t, _nt(k_hat_b, q_hat_b), 0.0).astype(BF16)
            dv_intra = _nn(att_t, dob)
            datt = jnp.where(mask, _nt(dob, vb), 0.0).astype(BF16)
            dqh = _nn(datt, k_hat_b)
            datt_t = jnp.where(mask_t, _nt(vb, dob), 0.0).astype(BF16)
            dkh = _nn(datt_t, q_hat_b)

            dst = dst_ref[h]
            for c in reversed(range(nc)):
                dsall_ref[c] = dst
                q_c = jnp.where(rowc == c, q_in_b, jnp.zeros_like(q_in_b))
                dst = dst * dec[c * CHUNK:c * CHUNK + 1, :] + _nn(dot_b, q_c)
            dst_ref[h] = dst
            for c in range(nc):
                rs = slice(c * CHUNK, (c + 1) * CHUNK)
                ds_c = dsall_ref[c]
                dsb = ds_c.astype(BF16)
                st_prev = s_ref[c, h]
                tmp_ref[0, rs, :] = _nt(k_out_b[rs], dsb)
                tmp_ref[1, rs, :] = _nn(vb[rs], dsb)
                tmp_ref[2, rs, :] = _nn(dob[rs], st_prev)
                ddec = jnp.sum(ds_c * st_prev.astype(F32), axis=0, keepdims=True)
                tmp_ref[3, rs, :] = jnp.broadcast_to(ddec * dec[c * CHUNK:c * CHUNK + 1, :],
                                                     (CHUNK, HEAD_DIM))
            dko, dqi = tmp_ref[1], tmp_ref[2]
            dq = dqh * eba + dqi * eb
            dk = dkh * eab + dko * elb
            tko = dko * k_out
            db = dqh * q_hat - dkh * k_hat + dqi * q_in - tko
            dlog = (_seg_scan(db, p["r32"], False)
                    + _bcast_row(_seg_scan(tko, p["r32"], True), CHUNK - 1) + tmp_ref[3])
            df = dlog / p["f"] - dk
            sig = p["sig"]
            red_ref[0:1, hs] += jnp.sum(df * (1.0 - sig), axis=0, keepdims=True)
            df_ref[:, hs] = (df * (1.0 - lbh) * sig * (1.0 - sig)).astype(BF16)
            sq = p["sq"]
            dq_ref[:, hs] = (dq * (HEAD_DIM ** -0.5) * (sq * (1.0 + qr * (1.0 - sq)))).astype(BF16)
            dv_ref[:, hs] = (dv_intra + tmp_ref[0]).astype(BF16)

    col = lambda k: pl.BlockSpec((TL, H), lambda i, k=k: (nt - 1 - i, k))
    rev = pl.BlockSpec((TL, H), lambda i: (nt - 1 - i, 0))
    return pl.pallas_call(
        body, name="hgrn_bwd", grid=(nt,),
        in_specs=[col(0), col(1), col(2), col(3),
                  pl.BlockSpec(lower_bounds.shape, lambda i: (0, 0)),
                  pl.BlockSpec(gamma.shape, lambda i: (0, 0)),
                  rev, rev,
                  pl.BlockSpec((nc, nh, HEAD_DIM, HEAD_DIM), lambda i: (nt - 1 - i, 0, 0, 0)), ANY],
        out_specs=[rev, rev, rev, rev, pl.BlockSpec((8, H), lambda i: (0, 0))],
        out_shape=[_sds((L, H), BF16)] * 4 + [_sds((8, H), F32)],
        scratch_shapes=[pltpu.VMEM((nh, HEAD_DIM, HEAD_DIM), F32),
                        pltpu.VMEM((nc, HEAD_DIM, HEAD_DIM), F32),
                        pltpu.VMEM((4, TL, HEAD_DIM), F32)],
        compiler_params=_params(("arbitrary",), 48),
    )(proj, proj, proj, proj, lower_bounds, gamma, o_pre, d_out, s_saved, after)


def _shift_down(u, s, row):
    return jnp.where(row >= s, pltpu.roll(u, s, 0), 0.0)


def _shift_up(u, s, row):
    n = u.shape[0]
    return jnp.where(row < n - s, pltpu.roll(u, n - s, 0), 0.0)


def _conv_specs(L, H):
    per = H // LANES
    return [pl.BlockSpec((L, LANES), lambda j, o=o: (0, o * per + j)) for o in (4, 5, 6)]


def _conv_fwd(proj, conv_w, H):
    L = proj.shape[0]

    def body(c_ref, b_ref, x_ref, w_ref, o_ref):
        row = lax.broadcasted_iota(jnp.int32, (L, LANES), 0)
        u = c_ref[...] * x_ref[...]
        w = w_ref[...]
        y = w[0:1] * _shift_down(u, 2, row) + w[1:2] * _shift_down(u, 1, row) + w[2:3] * u
        o_ref[...] = (b_ref[...] * y).astype(BF16)

    return pl.pallas_call(
        body, name="conv_fwd", grid=(H // LANES,),
        in_specs=_conv_specs(L, H) + [pl.BlockSpec((3, LANES), lambda j: (0, j))],
        out_specs=pl.BlockSpec((L, LANES), lambda j: (0, j)),
        out_shape=_sds((L, H), BF16),
        compiler_params=_params(("parallel",), 48),
    )(proj, proj, proj, conv_w)


def _conv_bwd(proj, conv_w, dcb, H):
    L = proj.shape[0]

    def body(c_ref, b_ref, x_ref, w_ref, d_ref, dc_ref, db_ref, dx_ref, dw_ref):
        row = lax.broadcasted_iota(jnp.int32, (L, LANES), 0)
        cg, xb = c_ref[...], x_ref[...]
        u = cg * xb
        u1, u2 = _shift_down(u, 1, row), _shift_down(u, 2, row)
        w = w_ref[...]
        y = w[0:1] * u2 + w[1:2] * u1 + w[2:3] * u
        d = d_ref[...]
        db_ref[...] = (d * y).astype(BF16)
        dy = d * b_ref[...]
        du = w[2:3] * dy + w[1:2] * _shift_up(dy, 1, row) + w[0:1] * _shift_up(dy, 2, row)
        dw_ref[0:1, :] = jnp.sum(dy * u2, axis=0, keepdims=True)
        dw_ref[1:2, :] = jnp.sum(dy * u1, axis=0, keepdims=True)
        dw_ref[2:3, :] = jnp.sum(dy * u, axis=0, keepdims=True)
        dc_ref[...] = (du * xb).astype(BF16)
        dx_ref[...] = (du * cg).astype(BF16)

    blk = pl.BlockSpec((L, LANES), lambda j: (0, j))
    return pl.pallas_call(
        body, name="conv_bwd", grid=(H // LANES,),
        in_specs=_conv_specs(L, H) + [pl.BlockSpec((3, LANES), lambda j: (0, j)), blk],
        out_specs=[blk, blk, blk, pl.BlockSpec((3, LANES), lambda j: (0, j))],
        out_shape=[_sds((L, H), BF16)] * 3 + [_sds((3, H), F32)],
        compiler_params=_params(("parallel",), 56),
    )(proj, proj, proj, conv_w, dcb)


def _gate_specs(tm, H):
    return [pl.BlockSpec((tm, H), lambda i, k=k: (i, k)) for k in (7, 8, 9, 10)]


def _fwd_mix(og, cb, proj, x, wat, wbt, wout, g_ffn, H):
    L, D = x.shape
    tm = min(L, 512)

    def body(o_ref, cb_ref, ga0, ga1, gb0, gb1, x_ref, wa_ref, wb_ref, wo_ref, g_ref,
             ya_ref, yb_ref, m_ref, x1_ref, h2_ref):
        ya, yb = _nt(o_ref[...], wa_ref[...]), _nt(cb_ref[...], wb_ref[...])
        ya_ref[...] = ya.astype(BF16)
        yb_ref[...] = yb.astype(BF16)
        for k, (gar, gbr) in enumerate(((ga0, gb0), (ga1, gb1))):
            cs = slice(k * H, (k + 1) * H)
            m_ref[:, cs] = (_sigmoid(gar[...]) * ya[:, cs] + _sigmoid(gbr[...]) * yb[:, cs]).astype(BF16)
        x1 = x_ref[...] + _nn(m_ref[...], wo_ref[...])
        x1_ref[...] = x1
        _, xh = _rms_stats(x1)
        h2_ref[...] = (xh * g_ref[...]).astype(BF16)

    row = lambda w: pl.BlockSpec((tm, w), lambda i: (i, 0))
    full = lambda a: pl.BlockSpec(a.shape, lambda i: (0,) * a.ndim)
    return pl.pallas_call(
        body, name="fwd_mix", grid=(L // tm,),
        in_specs=[row(H), row(H)] + _gate_specs(tm, H) + [row(D), full(wat), full(wbt), full(wout),
                                                           full(g_ffn)],
        out_specs=[row(D)] * 5,
        out_shape=[_sds((L, D), BF16)] * 3 + [_sds((L, D), F32), _sds((L, D), BF16)],
        compiler_params=_params(("parallel",), 56),
    )(og, cb, proj, proj, proj, proj, x, wat, wbt, wout, g_ffn)


def _bwd_mix(dx1b, proj, ya, yb, wat, wbt, wout, H):
    L, D = dx1b.shape
    tm = min(L, 512)

    def body(dx_ref, ga0, ga1, gb0, gb1, ya_ref, yb_ref, wa_ref, wb_ref, wo_ref,
             dya_ref, dyb_ref, dga_ref, dgb_ref, do_ref, dcb_ref):
        dm = _nt(dx_ref[...], wo_ref[...])
        for k, (gar, gbr) in enumerate(((ga0, gb0), (ga1, gb1))):
            cs = slice(k * H, (k + 1) * H)
            sa, sb = _sigmoid(gar[...]), _sigmoid(gbr[...])
            dmk = dm[:, cs]
            dga_ref[:, cs] = (dmk * ya_ref[:, cs].astype(F32) * sa * (1.0 - sa)).astype(BF16)
            dgb_ref[:, cs] = (dmk * yb_ref[:, cs].astype(F32) * sb * (1.0 - sb)).astype(BF16)
            dya_ref[:, cs] = (dmk * sa).astype(BF16)
            dyb_ref[:, cs] = (dmk * sb).astype(BF16)
        do_ref[...] = _nn(dya_ref[...], wa_ref[...])
        dcb_ref[...] = _nn(dyb_ref[...], wb_ref[...])

    row = lambda w: pl.BlockSpec((tm, w), lambda i: (i, 0))
    full = lambda a: pl.BlockSpec(a.shape, lambda i: (0,) * a.ndim)
    return pl.pallas_call(
        body, name="bwd_mix", grid=(L // tm,),
        in_specs=[row(D)] + _gate_specs(tm, H) + [row(D), row(D), full(wat), full(wbt), full(wout)],
        out_specs=[row(D)] * 4 + [row(H)] * 2,
        out_shape=[_sds((L, D), BF16)] * 4 + [_sds((L, H), F32)] * 2,
        compiler_params=_params(("parallel",), 56),
    )(dx1b, proj, proj, proj, proj, ya, yb, wat, wbt, wout)


def _fwd_ffn_up(h2, wgt, wut):
    L, D = h2.shape
    F = wgt.shape[0]
    tn = F // 2
    tm = min(L, 512)

    def body(h_ref, wg_ref, wu_ref, a_ref, b_ref, s_ref):
        h = h_ref[...]
        a, b = _nt(h, wg_ref[...]), _nt(h, wu_ref[...])
        a_ref[...] = a.astype(BF16)
        b_ref[...] = b.astype(BF16)
        s_ref[...] = (a * _sigmoid(a) * b).astype(BF16)

    wspec = pl.BlockSpec((tn, D), lambda j, i: (j, 0))
    ospec = pl.BlockSpec((tm, tn), lambda j, i: (i, j))
    return pl.pallas_call(
        body, name="fwd_ffn_up", grid=(2, L // tm),
        in_specs=[pl.BlockSpec((tm, D), lambda j, i: (i, 0)), wspec, wspec],
        out_specs=[ospec] * 3,
        out_shape=[_sds((L, F), BF16)] * 3,
        compiler_params=_params(("parallel", "parallel"), 48),
    )(h2, wgt, wut)


def _fwd_down_loss(s, wd, x1, target, g_final):
    L, D = x1.shape
    F = wd.shape[0]
    tm = min(L, 256)

    def body(s_ref, wd_ref, x1_ref, t_ref, g_ref, dx_ref, dxb_ref, red_ref):
        @pl.when(pl.program_id(0) == 0)
        def _():
            red_ref[...] = jnp.zeros_like(red_ref)

        g = g_ref[...]
        r, xh = _rms_stats(x1_ref[...] + _nn(s_ref[...], wd_ref[...]))
        e = xh * g - t_ref[...]
        dy = e * (1.0 / D)
        dx = _rms_bwd(dy * g, xh, r)
        dx_ref[...] = dx
        dxb_ref[...] = dx.astype(BF16)
        red_ref[0:1, :] += jnp.sum(dy * xh, axis=0, keepdims=True)
        red_ref[1:2, :] += jnp.broadcast_to(0.5 * jnp.sum(e * e) * (1.0 / D), (1, D))

    row = pl.BlockSpec((tm, D), lambda i: (i, 0))
    return pl.pallas_call(
        body, name="fwd_down_loss", grid=(L // tm,),
        in_specs=[pl.BlockSpec((tm, F), lambda i: (i, 0)), pl.BlockSpec((F, D), lambda i: (0, 0)),
                  row, row, pl.BlockSpec((1, D), lambda i: (0, 0))],
        out_specs=[row, row, pl.BlockSpec((8, D), lambda i: (0, 0))],
        out_shape=[_sds((L, D), F32), _sds((L, D), BF16), _sds((8, D), F32)],
        compiler_params=_params(("arbitrary",), 48),
    )(s, wd, x1, target, g_final)


def _bwd_down(dx2b, wd, a, b):
    L, D = dx2b.shape
    F = wd.shape[0]
    tn = F // 2
    tm = min(L, 512)

    def body(dx_ref, wd_ref, a_ref, b_ref, da_ref, db_ref):
        ds = _nt(dx_ref[...], wd_ref[...])
        a, b = a_ref[...].astype(F32), b_ref[...].astype(F32)
        sg = _sigmoid(a)
        da_ref[...] = (ds * b * sg * (1.0 + a * (1.0 - sg))).astype(BF16)
        db_ref[...] = (ds * a * sg).astype(BF16)

    ospec = pl.BlockSpec((tm, tn), lambda j, i: (i, j))
    return pl.pallas_call(
        body, name="bwd_down", grid=(2, L // tm),
        in_specs=[pl.BlockSpec((tm, D), lambda j, i: (i, 0)),
                  pl.BlockSpec((tn, D), lambda j, i: (j, 0)), ospec, ospec],
        out_specs=[ospec] * 2,
        out_shape=[_sds((L, F), BF16)] * 2,
        compiler_params=_params(("parallel", "parallel"), 48),
    )(dx2b, wd, a, b)


def _bwd_ffn_dh(da, db, wgt, wut, x1, dx2, g_ffn):
    L, D = x1.shape
    F = wgt.shape[0]
    tm = min(L, 256)

    def body(da_ref, db_ref, wg_ref, wu_ref, x1_ref, dx2_ref, g_ref, dx_ref, dxb_ref, red_ref):
        @pl.when(pl.program_id(0) == 0)
        def _():
            red_ref[...] = jnp.zeros_like(red_ref)

        dh = _nn(da_ref[...], wg_ref[...]) + _nn(db_ref[...], wu_ref[...])
        r, xh = _rms_stats(x1_ref[...])
        red_ref[0:1, :] += jnp.sum(dh * xh, axis=0, keepdims=True)
        dx = dx2_ref[...] + _rms_bwd(dh * g_ref[...], xh, r)
        dx_ref[...] = dx
        dxb_ref[...] = dx.astype(BF16)

    row = pl.BlockSpec((tm, D), lambda i: (i, 0))
    aspec = pl.BlockSpec((tm, F), lambda i: (i, 0))
    wspec = pl.BlockSpec((F, D), lambda i: (0, 0))
    return pl.pallas_call(
        body, name="bwd_ffn_dh", grid=(L // tm,),
        in_specs=[aspec, aspec, wspec, wspec, row, row, pl.BlockSpec((1, D), lambda i: (0, 0))],
        out_specs=[row, row, pl.BlockSpec((8, D), lambda i: (0, 0))],
        out_shape=[_sds((L, D), F32), _sds((L, D), BF16), _sds((8, D), F32)],
        compiler_params=_params(("arbitrary",), 56),
    )(da, db, wgt, wut, x1, dx2, g_ffn)


def _bwd_in(dproj, w_int, x, dx1, g_mix, after):
    L, D = x.shape
    N = w_int.shape[0]
    tm = min(L, 256)

    def body(dp_ref, w_ref, x_ref, dx1_ref, g_ref, after_ref, dx_ref, red_ref):
        @pl.when(pl.program_id(0) == 0)
        def _():
            red_ref[...] = jnp.zeros_like(red_ref)

        dh = _nn(dp_ref[...], w_ref[...])
        r, xh = _rms_stats(x_ref[...])
        red_ref[0:1, :] += jnp.sum(dh * xh, axis=0, keepdims=True)
        dx_ref[...] = dx1_ref[...] + _rms_bwd(dh * g_ref[...], xh, r)

    row = pl.BlockSpec((tm, D), lambda i: (i, 0))
    return pl.pallas_call(
        body, name="bwd_in", grid=(L // tm,),
        in_specs=[pl.BlockSpec((tm, N), lambda i: (i, 0)), pl.BlockSpec((N, D), lambda i: (0, 0)),
                  row, row, pl.BlockSpec((1, D), lambda i: (0, 0)), ANY],
        out_specs=[row, pl.BlockSpec((8, D), lambda i: (0, 0))],
        out_shape=[_sds((L, D), F32), _sds((8, D), F32)],
        compiler_params=_params(("arbitrary",), 56),
    )(dproj, w_int, x, dx1, g_mix, after)


def _mm_tn(name, a, b, a_spec, b_spec, o_block, n_out, n_k):
    def body(a_ref, b_ref, o_ref):
        part = _tn(a_ref[...], b_ref[...])

        @pl.when(pl.program_id(1) == 0)
        def _():
            o_ref[...] = part

        @pl.when(pl.program_id(1) > 0)
        def _():
            o_ref[...] += part

    return pl.pallas_call(
        body, name=name, grid=(n_out, n_k),
        in_specs=[a_spec, b_spec],
        out_specs=pl.BlockSpec((None,) + o_block, lambda j, k: (j, 0, 0)),
        out_shape=_sds((n_out,) + o_block, F32),
        compiler_params=_params(("parallel", "arbitrary"), 56),
    )(a, b)


TK_TOKENS = 2048


def _dw_cols(name, a, b, n_cols):
    L, M = a.shape
    tk = min(L, TK_TOKENS)
    return _mm_tn(name, a, b, pl.BlockSpec((tk, M), lambda j, k: (k, 0)),
                  pl.BlockSpec((tk, n_cols), lambda j, k: (k, j)), (M, n_cols), N_CHIPS, L // tk)


def _dw_rows(name, a, b):
    L, M = a.shape
    N = b.shape[1]
    tk = min(L, TK_TOKENS)
    return _mm_tn(name, a, b, pl.BlockSpec((tk, M // N_CHIPS), lambda j, k: (k, j)),
                  pl.BlockSpec((tk, N), lambda j, k: (k, 0)), (M // N_CHIPS, N), N_CHIPS, L // tk)


def _dw_rows2(name, a, b):
    L, M = a.shape
    N = b.shape[1]
    tk = min(L, TK_TOKENS)
    return _mm_tn(name, a, b, pl.BlockSpec((tk, M // 2), lambda j, k: (k, j)),
                  pl.BlockSpec((tk, N), lambda j, k: (k, 0)), (M // 2, N), 2, L // tk)


def _place():
    x, y, c = lax.axis_index("x"), lax.axis_index("y"), lax.axis_index("c")
    chips = [(1 - x, y), (x, 1 - y), (1 - x, 1 - y)]
    return x, y, c, 2 * x + y, chips


def _remote(src, dst, send_sem, recv_sem, device):
    return pltpu.make_async_remote_copy(src_ref=src, dst_ref=dst, send_sem=send_sem,
                                        recv_sem=recv_sem, device_id=device, device_id_type=MESH)


def _half(ref, lead, c, r2):
    return ref.at[lead, pl.ds(pl.multiple_of(c * r2, 16), r2), :]


def _cast_place(name, w, chip_idx):
    r, cols = w.shape
    tr = r // 2

    def body(k_ref, w_ref, o_ref):
        o_ref[...] = w_ref[...].astype(BF16)

    return pl.pallas_call(
        body, name=name,
        grid_spec=pltpu.PrefetchScalarGridSpec(
            num_scalar_prefetch=1, grid=(2,),
            in_specs=[pl.BlockSpec((tr, cols), lambda i, k_ref: (i, 0))],
            out_specs=pl.BlockSpec((None, tr, cols), lambda i, k_ref: (k_ref[0], i, 0))),
        out_shape=_sds((N_CHIPS, r, cols), BF16),
        compiler_params=_params(("parallel",), 48),
    )(chip_idx, w)


def _cast_place_t(name, w, chip_idx):
    r, cols = w.shape

    def body(k_ref, w_ref, o_ref):
        o_ref[...] = w_ref[...].T.astype(BF16)

    return pl.pallas_call(
        body, name=name,
        grid_spec=pltpu.PrefetchScalarGridSpec(
            num_scalar_prefetch=1, grid=(cols // LANES,),
            in_specs=[pl.BlockSpec((r, LANES), lambda i, k_ref: (0, i))],
            out_specs=pl.BlockSpec((None, LANES, r), lambda i, k_ref: (k_ref[0], i, 0))),
        out_shape=_sds((N_CHIPS, cols, r), BF16),
        compiler_params=_params(("parallel",), 48),
    )(chip_idx, w)


def _in_hbm(arrays):
    return [pltpu.with_memory_space_constraint(a, pltpu.HBM) for a in arrays]


def _gather_copies(bufs, whole, send_sems, recv_sems):
    x, y, c, k, chips = _place()
    pairs = []
    for w, buf in enumerate(bufs):
        for j, (cx, cy) in enumerate(chips):
            if w in whole:
                mine, theirs = buf.at[k], buf.at[2 * cx + cy]
            else:
                r2 = buf.shape[1] // 2
                mine, theirs = _half(buf, k, c, r2), _half(buf, 2 * cx + cy, c, r2)
            sems = (send_sems.at[w * 3 + j], recv_sems.at[w * 3 + j])
            pairs.append((_remote(mine, mine, *sems, (cx, cy, c)), _remote(theirs, theirs, *sems, (x, y, c))))
    return pairs


def _gather_start(groups):
    flat = [b for bufs, _ in groups for b in bufs]
    nb, ng = len(flat), len(groups)

    def body(*refs):
        ins, sems, token = refs[:nb], refs[nb:nb + 2 * ng], refs[-1]
        pos = 0
        for g, (bufs, whole) in enumerate(groups):
            for send, _ in _gather_copies(ins[pos:pos + len(bufs)], whole, sems[2 * g], sems[2 * g + 1]):
                send.start()
            pos += len(bufs)
        token[...] = jnp.zeros_like(token)

    sem_shapes = []
    for bufs, _ in groups:
        sem_shapes += [pltpu.SemaphoreType.DMA((3 * len(bufs),))] * 2
    out = pl.pallas_call(
        body, name="gather_start",
        in_specs=[HBM] * nb, out_specs=tuple([SEM] * (2 * ng) + [HBM] * nb + [VMEM]),
        out_shape=tuple(sem_shapes + [pltpu.HBM(b.shape, b.dtype) for b in flat] + [_sds((8, LANES), F32)]),
        input_output_aliases={i: 2 * ng + i for i in range(nb)},
        compiler_params=pltpu.CompilerParams(has_side_effects=EFFECT),
    )(*_in_hbm(flat))
    sems, thru, pos = [], [], 2 * ng
    for g, (bufs, _) in enumerate(groups):
        sems.append((out[2 * g], out[2 * g + 1]))
        thru.append(list(out[pos:pos + len(bufs)]))
        pos += len(bufs)
    return sems, thru, out[-1]


def _gather_wait(name, bufs, whole, sems, after):
    nb = len(bufs)

    def body(*refs):
        ins, send_sems, recv_sems = refs[:nb], refs[nb], refs[nb + 1]
        for send, arrival in _gather_copies(ins, whole, send_sems, recv_sems):
            send.wait_send()
            arrival.wait_recv()

    return pl.pallas_call(
        body, name=name,
        in_specs=[HBM] * nb + [SEM, SEM, ANY], out_specs=[HBM] * nb,
        out_shape=[pltpu.HBM(b.shape, b.dtype) for b in bufs],
        input_output_aliases={i: i for i in range(nb)},
        compiler_params=pltpu.CompilerParams(has_side_effects=EFFECT),
    )(*bufs, sems[0], sems[1], after)


def _gather_forward(name, bufs):
    n = len(bufs)

    def body(*refs):
        outs = refs[n:2 * n]
        send_sems, recv_sems = refs[2 * n:]
        x, y, c, _, chips = _place()
        sends = []
        for w in range(n):
            r2 = outs[w].shape[1] // 2
            for j, (cx, cy) in enumerate(chips):
                landed = _half(outs[w], 2 * cx + cy, c, r2)
                sends.append(_remote(landed, landed, send_sems.at[w * 3 + j], recv_sems.at[w * 3 + j],
                                     (x, y, 1 - c)))
        for cp in sends:
            cp.start()
        for w in range(n):
            r2 = outs[w].shape[1] // 2
            for j, (cx, cy) in enumerate(chips):
                got = _half(outs[w], 2 * cx + cy, 1 - c, r2)
                _remote(got, got, send_sems.at[w * 3 + j], recv_sems.at[w * 3 + j], (x, y, c)).wait_recv()
        for cp in sends:
            cp.wait_send()

    return pl.pallas_call(
        body, name=name,
        in_specs=[ANY] * n, out_specs=[ANY] * n,
        out_shape=[_sds(b.shape, b.dtype) for b in bufs],
        input_output_aliases={i: i for i in range(n)},
        scratch_shapes=[pltpu.SemaphoreType.DMA((n * 3,)), pltpu.SemaphoreType.DMA((n * 3,))],
    )(*bufs)


def _rs_sibling(name, grads):
    n = len(grads)

    def body(*refs):
        ins, outs = refs[:n], refs[n:2 * n]
        send_sems, recv_sems = refs[2 * n:]
        x, y, c, _, _ = _place()
        copies = []
        for w in range(n):
            r2 = ins[w].shape[1] // 2
            copies.append(_remote(_half(ins[w], slice(None), 1 - c, r2), outs[w],
                                  send_sems.at[w], recv_sems.at[w], (x, y, 1 - c)))
        for cp in copies:
            cp.start()
        for cp in copies:
            cp.wait()

    return pl.pallas_call(
        body, name=name,
        in_specs=[ANY] * n, out_specs=[ANY] * n,
        out_shape=[_sds((N_CHIPS, g.shape[1] // 2, g.shape[2]), F32) for g in grads],
        scratch_shapes=[pltpu.SemaphoreType.DMA((n,)), pltpu.SemaphoreType.DMA((n,))],
    )(*grads)


def _rs_add(name, grad3, from_sibling, c_idx):
    _, r2, cols = from_sibling.shape

    def body(c_ref, g_ref, s_ref, o_ref):
        o_ref[...] = (g_ref[...] + s_ref[...]).astype(BF16)

    return pl.pallas_call(
        body, name=name,
        grid_spec=pltpu.PrefetchScalarGridSpec(
            num_scalar_prefetch=1, grid=(N_CHIPS,),
            in_specs=[pl.BlockSpec((None, r2, cols), lambda k, c_ref: (k, c_ref[0], 0)),
                      pl.BlockSpec((None, r2, cols), lambda k, c_ref: (k, 0, 0))],
            out_specs=pl.BlockSpec((None, r2, cols), lambda k, c_ref: (k, 0, 0))),
        out_shape=_sds(from_sibling.shape, BF16),
        compiler_params=_params(("parallel",), 48),
    )(c_idx, grad3, from_sibling)


def _rs_ici_copies(partials, landings, send_sems, recv_sems):
    x, y, c, _, chips = _place()
    copies = []
    for w in range(len(partials)):
        for j, (cx, cy) in enumerate(chips):
            copies.append(_remote(partials[w].at[2 * cx + cy], landings[w].at[j],
                                  send_sems.at[w * 3 + j], recv_sems.at[w * 3 + j], (cx, cy, c)))
    return copies


def _rs_ici_start(name, partials):
    n = len(partials)
    landings = [lax.empty((3,) + p.shape[1:], BF16) for p in partials]

    def body(*refs):
        ins, lands, send_sems, recv_sems, token = refs[:n], refs[n:2 * n], refs[2 * n], refs[2 * n + 1], refs[-1]
        for cp in _rs_ici_copies(ins, lands, send_sems, recv_sems):
            cp.start()
        token[...] = jnp.zeros_like(token)

    flat = list(partials) + landings
    out = pl.pallas_call(
        body, name=name,
        in_specs=[HBM] * (2 * n), out_specs=tuple([SEM, SEM] + [HBM] * (2 * n) + [VMEM]),
        out_shape=tuple([pltpu.SemaphoreType.DMA((3 * n,))] * 2 + [pltpu.HBM(a.shape, a.dtype) for a in flat]
                        + [_sds((8, LANES), F32)]),
        input_output_aliases={i: 2 + i for i in range(2 * n)},
        compiler_params=pltpu.CompilerParams(has_side_effects=EFFECT),
    )(*_in_hbm(flat))
    return (out[0], out[1]), list(out[2:2 + 2 * n]), out[-1]


def _rs_ici_wait(name, sems, in_flight, after):
    n = len(in_flight) // 2

    def body(*refs):
        ins, lands, send_sems, recv_sems = refs[:n], refs[n:2 * n], refs[2 * n], refs[2 * n + 1]
        for cp in _rs_ici_copies(ins, lands, send_sems, recv_sems):
            cp.wait_send()
            cp.wait_recv()

    out = pl.pallas_call(
        body, name=name,
        in_specs=[HBM] * (2 * n) + [SEM, SEM, ANY], out_specs=[HBM] * (2 * n),
        out_shape=[pltpu.HBM(a.shape, a.dtype) for a in in_flight],
        input_output_aliases={i: i for i in range(2 * n)},
        compiler_params=pltpu.CompilerParams(has_side_effects=EFFECT),
    )(*in_flight, sems[0], sems[1], after)
    return list(out[:n]), list(out[n:])


def _rs_sum(name, partials, received, place_idx):
    _, r2, cols = partials.shape
    nb = 2
    tr = r2 // nb

    def body(idx_ref, p_ref, r_ref, o_ref):
        o_ref[...] = ((p_ref[...].astype(F32) + r_ref[0].astype(F32))
                      + (r_ref[1].astype(F32) + r_ref[2].astype(F32)))

    return pl.pallas_call(
        body, name=name,
        grid_spec=pltpu.PrefetchScalarGridSpec(
            num_scalar_prefetch=1, grid=(nb,),
            in_specs=[pl.BlockSpec((None, tr, cols), lambda i, idx: (idx[0], i, 0)),
                      pl.BlockSpec((3, tr, cols), lambda i, idx: (0, i, 0))],
            out_specs=pl.BlockSpec((tr, cols), lambda i, idx: (idx[1] * nb + i, 0))),
        out_shape=_sds((2 * r2, cols), F32),
        compiler_params=_params(("parallel",), 48),
    )(place_idx, partials, received)


def _rs_share(name, shards):
    n = len(shards)

    def body(*refs):
        outs = refs[n:2 * n]
        send_sems, recv_sems = refs[2 * n:]
        x, y, c, _, _ = _place()
        sends = []
        for w in range(n):
            r2 = outs[w].shape[0] // 2
            mine = outs[w].at[pl.ds(pl.multiple_of(c * r2, 8), r2), :]
            sends.append(_remote(mine, mine, send_sems.at[w], recv_sems.at[w], (x, y, 1 - c)))
        for cp in sends:
            cp.start()
        for w in range(n):
            r2 = outs[w].shape[0] // 2
            theirs = outs[w].at[pl.ds(pl.multiple_of((1 - c) * r2, 8), r2), :]
            _remote(theirs, theirs, send_sems.at[w], recv_sems.at[w], (x, y, c)).wait_recv()
        for cp in sends:
            cp.wait_send()

    return pl.pallas_call(
        body, name=name,
        in_specs=[ANY] * n, out_specs=[ANY] * n,
        out_shape=[_sds(s.shape, F32) for s in shards],
        input_output_aliases={i: i for i in range(n)},
        scratch_shapes=[pltpu.SemaphoreType.DMA((n,)), pltpu.SemaphoreType.DMA((n,))],
    )(*shards)


def _small_allreduce(red_mix, red_ffn, red_final, red_hg, g_conv):
    rows = N_SMALL_ROWS
    D = red_mix.shape[1]
    H = red_hg.shape[1]

    def body(mix_ref, ffn_ref, fin_ref, hg_ref, cv_ref, sum_ref, all_ref, in_ref, send_sems, recv_sems):
        in_ref[...] = jnp.zeros_like(in_ref)
        in_ref[0:1, :] = mix_ref[0:1, :]
        in_ref[1:2, :] = ffn_ref[0:1, :]
        in_ref[2:3, :] = fin_ref[0:1, :]
        gam = hg_ref[1:2, 0:HEAD_DIM]
        for h in range(1, H // HEAD_DIM):
            gam = gam + hg_ref[1:2, h * HEAD_DIM:(h + 1) * HEAD_DIM]
        in_ref[3:4, 0:HEAD_DIM] = gam
        in_ref[3:4, HEAD_DIM:2 * HEAD_DIM] = fin_ref[1:2, 0:HEAD_DIM]
        in_ref[4:5, 0:H] = hg_ref[0:1, :]
        in_ref[6:9, 0:H] = cv_ref[...]
        x, y, c, _, _ = _place()
        me = 4 * x + 2 * y + c
        all_ref[me] = in_ref[...]
        copies = []
        for m in range(1, 8):
            mx, my, mc = (m >> 2) & 1, (m >> 1) & 1, m & 1
            px, py, pc = x ^ mx, y ^ my, c ^ mc
            copies.append((_remote(in_ref, all_ref.at[me], send_sems.at[m - 1], recv_sems.at[m - 1],
                                   (px, py, pc)), 4 * px + 2 * py + pc, m))
        for cp, _, _ in copies:
            cp.start()
        for _, peer, m in copies:
            _remote(in_ref, all_ref.at[peer], send_sems.at[m - 1], recv_sems.at[m - 1],
                    (x, y, c)).wait_recv()
        for cp, _, _ in copies:
            cp.wait_send()
        total = all_ref[0]
        for d in range(1, 8):
            total = total + all_ref[d]
        sum_ref[...] = total

    return pl.pallas_call(
        body, name="small_allreduce",
        in_specs=[VMEM] * 5, out_specs=[VMEM, VMEM],
        out_shape=[_sds((rows, D), F32), _sds((8, rows, D), F32)],
        scratch_shapes=[pltpu.VMEM((rows, D), F32), pltpu.SemaphoreType.DMA((7,)),
                        pltpu.SemaphoreType.DMA((7,))],
    )(red_mix, red_ffn, red_final, red_hg, g_conv)[0]


def _adamw_math(w, g, m, v):
    m = ADAM_B1 * m + (1.0 - ADAM_B1) * g
    v = ADAM_B2 * v + (1.0 - ADAM_B2) * jnp.square(g)
    m_hat = m / (1.0 - ADAM_B1 ** ADAM_STEP)
    v_hat = v / (1.0 - ADAM_B2 ** ADAM_STEP)
    delta = -ADAM_LR * (m_hat / (jnp.sqrt(v_hat) + ADAM_EPS) + ADAM_WD * w)
    return delta, m, v


def _adamw(name, g, w, m, v):
    r, cols = g.shape
    tr = r // 4

    def body(g_ref, w_ref, m_ref, v_ref, d_ref, mo_ref, vo_ref):
        d_ref[...], mo_ref[...], vo_ref[...] = _adamw_math(w_ref[...], g_ref[...], m_ref[...], v_ref[...])

    blk = pl.BlockSpec((tr, cols), lambda i: (i, 0))
    return pl.pallas_call(
        body, name=name, grid=(r // tr,),
        in_specs=[blk] * 4, out_specs=[blk] * 3, out_shape=[_sds((r, cols), F32)] * 3,
        compiler_params=_params(("parallel",), 48),
    )(g, w, m, v)


def _small_update(total, chip_idx, ws, ms, vs):
    n = len(ws)
    H = ws[1].shape[1]

    def body(idx_ref, tot_ref, *refs):
        w, m, v, outs = refs[:n], refs[n:2 * n], refs[2 * n:3 * n], refs[3 * n:]
        chip = idx_ref[0]
        p0 = _lower_bound(w[1][...])
        dl0 = p0 * (1.0 - p0) * tot_ref[4:5, 0:H]
        conv = jnp.zeros((3, LANES), F32)
        for k in range(N_CHIPS):
            conv = jnp.where(chip == k, tot_ref[6:9, k * LANES:(k + 1) * LANES], conv)
        grads = [tot_ref[0:1, :], None, tot_ref[3:4, 0:HEAD_DIM], conv, tot_ref[1:2, :], tot_ref[2:3, :]]
        for p in range(n):
            g_ref, d_ref, mo_ref, vo_ref = outs[4 * p:4 * p + 4]
            if p == 1:
                for row, g in ((slice(0, 1), dl0), (slice(1, 2), -dl0)):
                    g_ref[row, :] = g
                    d_ref[row, :], mo_ref[row, :], vo_ref[row, :] = _adamw_math(
                        w[p][row, :], g, m[p][row, :], v[p][row, :])
            else:
                g_ref[...] = grads[p]
                d_ref[...], mo_ref[...], vo_ref[...] = _adamw_math(w[p][...], grads[p], m[p][...], v[p][...])
        outs[4 * n][...] = tot_ref[3:4, HEAD_DIM:2 * HEAD_DIM]

    full = lambda a: pl.BlockSpec(a.shape, lambda i, idx: (0,) * a.ndim)
    out_shape = [_sds(w.shape, F32) for w in ws for _ in range(4)] + [_sds((1, LANES), F32)]
    return pl.pallas_call(
        body, name="small_update",
        grid_spec=pltpu.PrefetchScalarGridSpec(
            num_scalar_prefetch=1, grid=(1,),
            in_specs=[full(total)] + [full(a) for a in ws + ms + vs],
            out_specs=[full(s) for s in out_shape]),
        out_shape=out_shape,
    )(chip_idx, total, *ws, *ms, *vs)


def kernel(x, norm_mix_g, w_in, lower_bounds, hg_norm_g, conv_w, w_branch_a, w_branch_b, w_out, norm_ffn_g, w_ffn_gate, w_ffn_up, w_ffn_down, norm_final_g, loss_target, m_norm_mix_g, m_w_in, m_lower_bounds, m_hg_norm_g, m_conv_w, m_w_branch_a, m_w_branch_b, m_w_out, m_norm_ffn_g, m_w_ffn_gate, m_w_ffn_up, m_w_ffn_down, m_norm_final_g, v_norm_mix_g, v_w_in, v_lower_bounds, v_hg_norm_g, v_conv_w, v_w_branch_a, v_w_branch_b, v_w_out, v_norm_ffn_g, v_w_ffn_gate, v_w_ffn_up, v_w_ffn_down, v_norm_final_g):
    _, L, D = x.shape
    H = D // 2
    assert lower_bounds.shape == (2, H) and hg_norm_g.shape == (1, HEAD_DIM)
    assert conv_w.shape == (1, 3, LANES) and w_in.shape[2] * N_CHIPS == 11 * H
    x2d, target = x.reshape(L, D), loss_target.reshape(L, D)
    g_final = norm_final_g.reshape(1, D)
    chip = 2 * lax.axis_index("x") + lax.axis_index("y")
    core = lax.axis_index("c")

    tr = lambda w: jnp.transpose(w[0])
    big = [w_in[0], w_branch_a[0], w_branch_b[0], w_out[0], tr(w_ffn_gate), tr(w_ffn_up), w_ffn_down[0]]
    big_m = [m_w_in[0], m_w_branch_a[0], m_w_branch_b[0], m_w_out[0], tr(m_w_ffn_gate), tr(m_w_ffn_up),
             m_w_ffn_down[0]]
    big_v = [v_w_in[0], v_w_branch_a[0], v_w_branch_b[0], v_w_out[0], tr(v_w_ffn_gate), tr(v_w_ffn_up),
             v_w_ffn_down[0]]
    names = ["w_in", "w_branch_a", "w_branch_b", "w_out", "w_ffn_gate", "w_ffn_up", "w_ffn_down"]

    chip_idx = chip.reshape(1).astype(jnp.int32)
    placed = [(_cast_place_t if j < 3 else _cast_place)("place_" + nm, w, chip_idx)
              for j, (nm, w) in enumerate(zip(names, big))]
    conv_placed = lax.dynamic_update_slice(jnp.zeros((N_CHIPS, 3, LANES), F32), conv_w, (chip, 0, 0))
    sems, in_flight, token = _gather_start([([placed[0], conv_placed], {1}), (placed[1:4], set()),
                                            (placed[4:], set())])
    w_in_landed, conv_all = _gather_wait("gather_wait_in", in_flight[0], {1}, sems[0], token)
    (w_int3,) = _gather_forward("gather_fwd_in", [w_in_landed])
    w_int = w_int3.reshape(-1, D)
    conv_full = jnp.transpose(conv_all, (1, 0, 2)).reshape(3, H)

    h, proj = _fwd_proj(x2d, norm_mix_g, w_int3)
    cb = _conv_fwd(proj, conv_full, H)
    og, o_pre, s_saved = _hgrn_fwd(proj, lower_bounds, hg_norm_g, H)
    landed = _gather_wait("gather_wait_mix", in_flight[1], set(), sems[1], og)
    wat3, wbt3, wout3 = _gather_forward("gather_fwd_mix", landed)
    wat, wbt, wout = wat3.reshape(D, H), wbt3.reshape(D, H), wout3.reshape(D, D)
    ya, yb, merged, x1, h2 = _fwd_mix(og, cb, proj, x2d, wat, wbt, wout, norm_ffn_g, H)
    landed = _gather_wait("gather_wait_ffn", in_flight[2], set(), sems[2], h2)
    wgt3, wut3, wd3 = _gather_forward("gather_fwd_ffn", landed)
    d_ff = N_CHIPS * wd3.shape[1]
    wgt, wut, wd = wgt3.reshape(d_ff, D), wut3.reshape(d_ff, D), wd3.reshape(d_ff, D)
    ffn_a, ffn_b, ffn_s = _fwd_ffn_up(h2, wgt, wut)
    dx2, dx2b, red_final = _fwd_down_loss(ffn_s, wd, x1, target, g_final)

    shards3 = lambda g: g.reshape(N_CHIPS, d_ff // N_CHIPS, D)
    da, db = _bwd_down(dx2b, wd, ffn_a, ffn_b)
    g_wd = shards3(_dw_rows2("dw_ffn_down", ffn_s, dx2b))
    dx1, dx1b, red_ffn = _bwd_ffn_dh(da, db, wgt, wut, x1, dx2, norm_ffn_g)
    g_wg = shards3(_dw_rows2("dw_ffn_gate", da, h2))
    g_wu = shards3(_dw_rows2("dw_ffn_up", db, h2))
    dya, dyb, dga, dgb, d_o, d_cb = _bwd_mix(dx1b, proj, ya, yb, wat, wbt, wout, H)
    g_wout = _dw_rows("dw_out", merged, dx1b)
    g_wa = _dw_cols("dw_branch_a", og, dya, D // N_CHIPS)
    g_wb = _dw_cols("dw_branch_b", cb, dyb, D // N_CHIPS)

    c_idx = core.reshape(1).astype(jnp.int32)
    place_idx = jnp.stack([chip, core]).astype(jnp.int32)

    def rs_begin(tag, js, grads):
        from_sib = _rs_sibling("rs_sibling_" + tag, grads)
        partials = [_rs_add("rs_add_" + names[j], g, s, c_idx) for j, g, s in zip(js, grads, from_sib)]
        return _rs_ici_start("rs_ici_start_" + tag, partials)

    def rs_end(tag, js, started, after):
        sems, in_flight, _ = started
        partials, received = _rs_ici_wait("rs_ici_wait_" + tag, sems, in_flight, after)
        halves = [_rs_sum("rs_sum_" + names[j], p, r, place_idx) for j, p, r in zip(js, partials, received)]
        return _rs_share("rs_share_" + tag, halves)

    rest = [1, 2, 3, 4, 5, 6]
    rest_started = rs_begin("rest", rest, [g_wa, g_wb, g_wout, g_wg, g_wu, g_wd])
    dcg, dbg, dxb, g_conv = _conv_bwd(proj, conv_full, d_cb, H)
    dq, df, dv, dg, red_hg = _hgrn_bwd(proj, lower_bounds, hg_norm_g, o_pre, d_o, s_saved, H,
                                       rest_started[2])
    dproj = jnp.concatenate([dq, df, dv, dg, dcg, dbg, dxb, dga, dgb], axis=1)
    g_win = _dw_cols("dw_in", h, dproj, w_int3.shape[1])
    in_started = rs_begin("in", [0], [g_win])
    grad_x, red_mix = _bwd_in(dproj, w_int, x2d, dx1, norm_mix_g, in_started[2])
    shard_grads = (rs_end("in", [0], in_started, grad_x)
                   + rs_end("rest", rest, rest_started, in_started[2]))
    big_out = [_adamw("adamw_" + nm, g, w, m, v)
               for nm, g, w, m, v in zip(names, shard_grads, big, big_m, big_v)]

    total = _small_allreduce(red_mix, red_ffn, red_final, red_hg, g_conv)

    def smalls(mix, lb, hg, cw, ffn, fin):
        return [mix, lb, hg, cw[0], ffn, fin.reshape(1, D)]

    small_out = _small_update(
        total, chip_idx,
        smalls(norm_mix_g, lower_bounds, hg_norm_g, conv_w, norm_ffn_g, norm_final_g),
        smalls(m_norm_mix_g, m_lower_bounds, m_hg_norm_g, m_conv_w, m_norm_ffn_g, m_norm_final_g),
        smalls(v_norm_mix_g, v_lower_bounds, v_hg_norm_g, v_conv_w, v_norm_ffn_g, v_norm_final_g))

    def outputs(i):
        big_i = [shard_grads[j] if i == 0 else big_out[j][i - 1] for j in range(7)]
        mix, lb, hg, cw, ffn, fin = [small_out[4 * p + i] for p in range(6)]
        return [mix, big_i[0][None], lb, hg, cw[None], big_i[1][None], big_i[2][None], big_i[3][None], ffn,
                big_i[4].T[None], big_i[5].T[None], big_i[6][None], fin.reshape(D)]

    outs = [small_out[24][0, 0], grad_x.reshape(1, L, D)]
    for i in range(4):
        outs += outputs(i)
    return tuple(outs)
```

```python
import functools

import jax
import jax.numpy as jnp
from jax import lax
from jax.experimental import pallas as pl
from jax.experimental.pallas import tpu as pltpu

F32 = jnp.float32
BF16 = jnp.bfloat16
EPS = 1e-6
CHUNK = 32
HEAD_DIM = 128
LANES = 128
N_CHIPS = 4
N_SMALL_ROWS = 16

ADAM_LR = 0.001
ADAM_B1 = 0.9
ADAM_B2 = 0.999
ADAM_EPS = 1e-08
ADAM_WD = 0.01
ADAM_STEP = 10

MESH = pl.DeviceIdType.MESH
ANY = pl.BlockSpec(memory_space=pl.ANY)
VMEM = pl.BlockSpec(memory_space=pltpu.VMEM)
HBM = pl.BlockSpec(memory_space=pltpu.HBM)
SEM = pl.BlockSpec(memory_space=pltpu.SEMAPHORE)
EFFECT = pltpu.SideEffectType.DATAFLOW_SIDE_EFFECTING


def _sds(shape, dtype):
    return jax.ShapeDtypeStruct(shape, dtype)


def _params(semantics, vmem_mb):
    return pltpu.CompilerParams(dimension_semantics=semantics, vmem_limit_bytes=vmem_mb << 20)


def _nn(a, b):
    return lax.dot_general(a, b, (((1,), (0,)), ((), ())), preferred_element_type=F32)


def _nt(a, b):
    return lax.dot_general(a, b, (((1,), (1,)), ((), ())), preferred_element_type=F32)


def _tn(a, b):
    return lax.dot_general(a, b, (((0,), (0,)), ((), ())), preferred_element_type=F32)


def _sigmoid(x):
    return jax.nn.sigmoid(x)


def _rms_stats(x):
    r = lax.rsqrt(jnp.mean(x * x, axis=-1, keepdims=True) + EPS)
    return r, x * r


def _rms_bwd(dxh, xh, r):
    return r * (dxh - xh * jnp.mean(dxh * xh, axis=-1, keepdims=True))


def _fwd_proj(x, g_mix, w_int3):
    L, D = x.shape
    tn = w_int3.shape[1]
    tm = min(L, 1024)

    def body(x_ref, g_ref, w_ref, h_ref, p_ref):
        @pl.when(pl.program_id(1) == 0)
        def _():
            _, xh = _rms_stats(x_ref[...])
            h_ref[...] = (xh * g_ref[...]).astype(BF16)

        p_ref[...] = _nt(h_ref[...], w_ref[...])

    return pl.pallas_call(
        body, name="fwd_proj", grid=(L // tm, N_CHIPS),
        in_specs=[pl.BlockSpec((tm, D), lambda i, j: (i, 0)),
                  pl.BlockSpec((1, D), lambda i, j: (0, 0)),
                  pl.BlockSpec((None, tn, D), lambda i, j: (j, 0, 0))],
        out_specs=[pl.BlockSpec((tm, D), lambda i, j: (i, 0)),
                   pl.BlockSpec((tm, tn), lambda i, j: (i, j))],
        out_shape=[_sds((L, D), BF16), _sds((L, N_CHIPS * tn), F32)],
        compiler_params=_params(("parallel", "arbitrary"), 48),
    )(x, g_mix, w_int3)


def _lower_bound(lbp):
    l0, l1 = lbp[0:1, :], lbp[1:2, :]
    m = jnp.maximum(l0, l1)
    e0, e1 = jnp.exp(l0 - m), jnp.exp(l1 - m)
    return e0 / (e0 + e1)


def _seg_scan(x, r32, forward):
    n = x.shape[0]
    s = 1
    while s < CHUNK:
        if forward:
            x = x + jnp.where(r32 >= s, pltpu.roll(x, s, 0), 0.0)
        else:
            x = x + jnp.where(r32 < CHUNK - s, pltpu.roll(x, n - s, 0), 0.0)
        s *= 2
    return x


def _bcast_row(x, row):
    n, w = x.shape
    nc = n // CHUNK
    x3 = x.reshape(nc, CHUNK, w)
    return jnp.broadcast_to(x3[:, row:row + 1, :], (nc, CHUNK, w)).reshape(n, w)


def _hgrn_prep(q_raw, f_raw, lb):
    r32 = lax.broadcasted_iota(jnp.int32, f_raw.shape, 0) & (CHUNK - 1)
    sig = _sigmoid(f_raw)
    f = lb + (1.0 - lb) * sig
    b = _seg_scan(jnp.log(f), r32, True)
    a = _bcast_row(b, CHUNK // 2 - 1)
    bl = _bcast_row(b, CHUNK - 1)
    sq = _sigmoid(q_raw)
    q = q_raw * sq * (HEAD_DIM ** -0.5)
    return dict(r32=r32, sig=sig, f=f, k=1.0 - f, b=b, a=a, bl=bl, sq=sq, q=q)


def _chunk_masks(n):
    ri = lax.broadcasted_iota(jnp.int32, (n, n), 0)
    ci = lax.broadcasted_iota(jnp.int32, (n, n), 1)
    same = (ri // CHUNK) == (ci // CHUNK)
    return same & (ci <= ri), same & (ri <= ci)


def _hgrn_fwd(proj, lower_bounds, gamma, H):
    L = proj.shape[0]
    nh = H // HEAD_DIM
    TL = min(L, 256)
    nc = TL // CHUNK

    def body(q_ref, f_ref, v_ref, g_ref, lbp_ref, gam_ref, og_ref, o_ref, s_ref, st_ref):
        @pl.when(pl.program_id(0) == 0)
        def _():
            st_ref[...] = jnp.zeros_like(st_ref)

        lb = _lower_bound(lbp_ref[...])
        gam = gam_ref[...]
        mask, _ = _chunk_masks(TL)
        rowc = lax.broadcasted_iota(jnp.int32, (TL, HEAD_DIM), 0) // CHUNK
        for h in range(nh):
            hs = slice(h * HEAD_DIM, (h + 1) * HEAD_DIM)
            p = _hgrn_prep(q_ref[:, hs], f_ref[:, hs], lb[:, hs])
            v = v_ref[:, hs]
            vb = v.astype(BF16)
            vt = v.T.astype(BF16)
            q_hat = (p["q"] * jnp.exp(p["b"] - p["a"])).astype(BF16)
            k_hat = (p["k"] * jnp.exp(p["a"] - p["b"])).astype(BF16)
            q_in = (p["q"] * jnp.exp(p["b"])).astype(BF16)
            k_out = (p["k"] * jnp.exp(p["bl"] - p["b"])).astype(BF16)
            dec = jnp.exp(p["bl"])
            att = jnp.where(mask, _nt(q_hat, k_hat), 0.0).astype(BF16)
            o_intra = _nn(att, vb)
            st = st_ref[h]
            for c in range(nc):
                rs = slice(c * CHUNK, (c + 1) * CHUNK)
                stb = st.astype(BF16)
                s_ref[c, h] = stb
                o_ref[rs, hs] = o_intra[rs] + _nt(q_in[rs], stb)
                k_c = jnp.where(rowc == c, k_out, jnp.zeros_like(k_out))
                st = st * dec[c * CHUNK:c * CHUNK + 1, :] + _nn(vt, k_c)
            st_ref[h] = st
            o = o_ref[:, hs]
            _, xh = _rms_stats(o)
            gr = g_ref[:, hs]
            og_ref[:, hs] = (xh * gam * (gr * _sigmoid(gr))).astype(BF16)

    col = lambda k: pl.BlockSpec((TL, H), lambda i, k=k: (i, k))
    return pl.pallas_call(
        body, name="hgrn_fwd", grid=(L // TL,),
        in_specs=[col(0), col(1), col(2), col(3),
                  pl.BlockSpec(lower_bounds.shape, lambda i: (0, 0)),
                  pl.BlockSpec(gamma.shape, lambda i: (0, 0))],
        out_specs=[pl.BlockSpec((TL, H), lambda i: (i, 0)),
                   pl.BlockSpec((TL, H), lambda i: (i, 0)),
                   pl.BlockSpec((nc, nh, HEAD_DIM, HEAD_DIM), lambda i: (i, 0, 0, 0))],
        out_shape=[_sds((L, H), BF16), _sds((L, H), F32),
                   _sds((L // CHUNK, nh, HEAD_DIM, HEAD_DIM), BF16)],
        scratch_shapes=[pltpu.VMEM((nh, HEAD_DIM, HEAD_DIM), F32)],
        compiler_params=_params(("arbitrary",), 48),
    )(proj, proj, proj, proj, lower_bounds, gamma)


def _hgrn_bwd(proj, lower_bounds, gamma, o_pre, d_out, s_saved, H, after):
    L = proj.shape[0]
    nh = H // HEAD_DIM
    TL = min(L, 256)
    nc = TL // CHUNK
    nt = L // TL

    def body(q_ref, f_ref, v_ref, g_ref, lbp_ref, gam_ref, o_ref, d_ref, s_ref, after_ref,
             dq_ref, df_ref, dv_ref, dg_ref, red_ref, dst_ref, dsall_ref, tmp_ref):
        @pl.when(pl.program_id(0) == 0)
        def _():
            dst_ref[...] = jnp.zeros_like(dst_ref)
            red_ref[...] = jnp.zeros_like(red_ref)

        lb = _lower_bound(lbp_ref[...])
        gam = gam_ref[...]
        mask, mask_t = _chunk_masks(TL)
        rowc = lax.broadcasted_iota(jnp.int32, (TL, HEAD_DIM), 0) // CHUNK
        for h in range(nh):
            hs = slice(h * HEAD_DIM, (h + 1) * HEAD_DIM)
            qr, gr, lbh = q_ref[:, hs], g_ref[:, hs], lb[:, hs]
            p = _hgrn_prep(qr, f_ref[:, hs], lbh)
            vb = v_ref[:, hs].astype(BF16)
            eba, eab = jnp.exp(p["b"] - p["a"]), jnp.exp(p["a"] - p["b"])
            eb, elb = jnp.exp(p["b"]), jnp.exp(p["bl"] - p["b"])
            dec = jnp.exp(p["bl"])
            q_hat, k_hat = p["q"] * eba, p["k"] * eab
            q_in, k_out = p["q"] * eb, p["k"] * elb
            q_hat_b, k_hat_b = q_hat.astype(BF16), k_hat.astype(BF16)
            q_in_b, k_out_b = q_in.astype(BF16), k_out.astype(BF16)

            o, dout = o_ref[:, hs], d_ref[:, hs]
            sg = _sigmoid(gr)
            r, xh = _rms_stats(o)
            dg_ref[:, hs] = (dout * (xh * gam) * (sg * (1.0 + gr * (1.0 - sg)))).astype(BF16)
            dn = dout * (gr * sg)
            red_ref[1:2, hs] += jnp.sum(dn * xh, axis=0, keepdims=True)
            do = _rms_bwd(dn * gam, xh, r)
            dob = do.astype(BF16)
            dot_b = do.T.astype(BF16)

            att_t = jnp.where(mask_t, _nt(k_hat_b, q_hat_b), 0.0).astype(BF16)
            dv_intra = _nn(att_t, dob)
            datt = jnp.where(mask, _nt(dob, vb), 0.0).astype(BF16)
            dqh = _nn(datt, k_hat_b)
            datt_t = jnp.where(mask_t, _nt(vb, dob), 0.0).astype(BF16)
            dkh = _nn(datt_t, q_hat_b)

            dst = dst_ref[h]
            for c in reversed(range(nc)):
                dsall_ref[c] = dst
                q_c = jnp.where(rowc == c, q_in_b, jnp.zeros_like(q_in_b))
                dst = dst * dec[c * CHUNK:c * CHUNK + 1, :] + _nn(dot_b, q_c)
            dst_ref[h] = dst
            for c in range(nc):
                rs = slice(c * CHUNK, (c + 1) * CHUNK)
                ds_c = dsall_ref[c]
                dsb = ds_c.astype(BF16)
                st_prev = s_ref[c, h]
                tmp_ref[0, rs, :] = _nt(k_out_b[rs], dsb)
                tmp_ref[1, rs, :] = _nn(vb[rs], dsb)
                tmp_ref[2, rs, :] = _nn(dob[rs], st_prev)
                ddec = jnp.sum(ds_c * st_prev.astype(F32), axis=0, keepdims=True)
                tmp_ref[3, rs, :] = jnp.broadcast_to(ddec * dec[c * CHUNK:c * CHUNK + 1, :],
                                                     (CHUNK, HEAD_DIM))
            dko, dqi = tmp_ref[1], tmp_ref[2]
            dq = dqh * eba + dqi * eb
            dk = dkh * eab + dko * elb
            tko = dko * k_out
            db = dqh * q_hat - dkh * k_hat + dqi * q_in - tko
            dlog = (_seg_scan(db, p["r32"], False)
                    + _bcast_row(_seg_scan(tko, p["r32"], True), CHUNK - 1) + tmp_ref[3])
            df = dlog / p["f"] - dk
            sig = p["sig"]
            red_ref[0:1, hs] += jnp.sum(df * (1.0 - sig), axis=0, keepdims=True)
            df_ref[:, hs] = (df * (1.0 - lbh) * sig * (1.0 - sig)).astype(BF16)
            sq = p["sq"]
            dq_ref[:, hs] = (dq * (HEAD_DIM ** -0.5) * (sq * (1.0 + qr * (1.0 - sq)))).astype(BF16)
            dv_ref[:, hs] = (dv_intra + tmp_ref[0]).astype(BF16)

    col = lambda k: pl.BlockSpec((TL, H), lambda i, k=k: (nt - 1 - i, k))
    rev = pl.BlockSpec((TL, H), lambda i: (nt - 1 - i, 0))
    return pl.pallas_call(
        body, name="hgrn_bwd", grid=(nt,),
        in_specs=[col(0), col(1), col(2), col(3),
                  pl.BlockSpec(lower_bounds.shape, lambda i: (0, 0)),
                  pl.BlockSpec(gamma.shape, lambda i: (0, 0)),
                  rev, rev,
                  pl.BlockSpec((nc, nh, HEAD_DIM, HEAD_DIM), lambda i: (nt - 1 - i, 0, 0, 0)), ANY],
        out_specs=[rev, rev, rev, rev, pl.BlockSpec((8, H), lambda i: (0, 0))],
        out_shape=[_sds((L, H), BF16)] * 4 + [_sds((8, H), F32)],
        scratch_shapes=[pltpu.VMEM((nh, HEAD_DIM, HEAD_DIM), F32),
                        pltpu.VMEM((nc, HEAD_DIM, HEAD_DIM), F32),
                        pltpu.VMEM((4, TL, HEAD_DIM), F32)],
        compiler_params=_params(("arbitrary",), 48),
    )(proj, proj, proj, proj, lower_bounds, gamma, o_pre, d_out, s_saved, after)


def _shift_down(u, s, row):
    return jnp.where(row >= s, pltpu.roll(u, s, 0), 0.0)


def _shift_up(u, s, row):
    n = u.shape[0]
    return jnp.where(row < n - s, pltpu.roll(u, n - s, 0), 0.0)


def _conv_specs(L, H):
    per = H // LANES
    return [pl.BlockSpec((L, LANES), lambda j, o=o: (0, o * per + j)) for o in (4, 5, 6)]


def _conv_fwd(proj, conv_w, H):
    L = proj.shape[0]

    def body(c_ref, b_ref, x_ref, w_ref, o_ref):
        row = lax.broadcasted_iota(jnp.int32, (L, LANES), 0)
        u = c_ref[...] * x_ref[...]
        w = w_ref[...]
        y = w[0:1] * _shift_down(u, 2, row) + w[1:2] * _shift_down(u, 1, row) + w[2:3] * u
        o_ref[...] = (b_ref[...] * y).astype(BF16)

    return pl.pallas_call(
        body, name="conv_fwd", grid=(H // LANES,),
        in_specs=_conv_specs(L, H) + [pl.BlockSpec((3, LANES), lambda j: (0, j))],
        out_specs=pl.BlockSpec((L, LANES), lambda j: (0, j)),
        out_shape=_sds((L, H), BF16),
        compiler_params=_params(("parallel",), 48),
    )(proj, proj, proj, conv_w)


def _conv_bwd(proj, conv_w, dcb, H):
    L = proj.shape[0]

    def body(c_ref, b_ref, x_ref, w_ref, d_ref, dc_ref, db_ref, dx_ref, dw_ref):
        row = lax.broadcasted_iota(jnp.int32, (L, LANES), 0)
        cg, xb = c_ref[...], x_ref[...]
        u = cg * xb
        u1, u2 = _shift_down(u, 1, row), _shift_down(u, 2, row)
        w = w_ref[...]
        y = w[0:1] * u2 + w[1:2] * u1 + w[2:3] * u
        d = d_ref[...]
        db_ref[...] = (d * y).astype(BF16)
        dy = d * b_ref[...]
        du = w[2:3] * dy + w[1:2] * _shift_up(dy, 1, row) + w[0:1] * _shift_up(dy, 2, row)
        dw_ref[0:1, :] = jnp.sum(dy * u2, axis=0, keepdims=True)
        dw_ref[1:2, :] = jnp.sum(dy * u1, axis=0, keepdims=True)
        dw_ref[2:3, :] = jnp.sum(dy * u, axis=0, keepdims=True)
        dc_ref[...] = (du * xb).astype(BF16)
        dx_ref[...] = (du * cg).astype(BF16)

    blk = pl.BlockSpec((L, LANES), lambda j: (0, j))
    return pl.pallas_call(
        body, name="conv_bwd", grid=(H // LANES,),
        in_specs=_conv_specs(L, H) + [pl.BlockSpec((3, LANES), lambda j: (0, j)), blk],
        out_specs=[blk, blk, blk, pl.BlockSpec((3, LANES), lambda j: (0, j))],
        out_shape=[_sds((L, H), BF16)] * 3 + [_sds((3, H), F32)],
        compiler_params=_params(("parallel",), 56),
    )(proj, proj, proj, conv_w, dcb)


def _gate_specs(tm, H):
    return [pl.BlockSpec((tm, H), lambda i, k=k: (i, k)) for k in (7, 8, 9, 10)]


def _fwd_mix(og, cb, proj, x, wat, wbt, wout, g_ffn, H):
    L, D = x.shape
    tm = min(L, 512)

    def body(o_ref, cb_ref, ga0, ga1, gb0, gb1, x_ref, wa_ref, wb_ref, wo_ref, g_ref,
             ya_ref, yb_ref, m_ref, x1_ref, h2_ref):
        ya, yb = _nt(o_ref[...], wa_ref[...]), _nt(cb_ref[...], wb_ref[...])
        ya_ref[...] = ya.astype(BF16)
        yb_ref[...] = yb.astype(BF16)
        for k, (gar, gbr) in enumerate(((ga0, gb0), (ga1, gb1))):
            cs = slice(k * H, (k + 1) * H)
            m_ref[:, cs] = (_sigmoid(gar[...]) * ya[:, cs] + _sigmoid(gbr[...]) * yb[:, cs]).astype(BF16)
        x1 = x_ref[...] + _nn(m_ref[...], wo_ref[...])
        x1_ref[...] = x1
        _, xh = _rms_stats(x1)
        h2_ref[...] = (xh * g_ref[...]).astype(BF16)

    row = lambda w: pl.BlockSpec((tm, w), lambda i: (i, 0))
    full = lambda a: pl.BlockSpec(a.shape, lambda i: (0,) * a.ndim)
    return pl.pallas_call(
        body, name="fwd_mix", grid=(L // tm,),
        in_specs=[row(H), row(H)] + _gate_specs(tm, H) + [row(D), full(wat), full(wbt), full(wout),
                                                           full(g_ffn)],
        out_specs=[row(D)] * 5,
        out_shape=[_sds((L, D), BF16)] * 3 + [_sds((L, D), F32), _sds((L, D), BF16)],
        compiler_params=_params(("parallel",), 56),
    )(og, cb, proj, proj, proj, proj, x, wat, wbt, wout, g_ffn)


def _bwd_mix(dx1b, proj, ya, yb, wat, wbt, wout, H):
    L, D = dx1b.shape
    tm = min(L, 512)

    def body(dx_ref, ga0, ga1, gb0, gb1, ya_ref, yb_ref, wa_ref, wb_ref, wo_ref,
             dya_ref, dyb_ref, dga_ref, dgb_ref, do_ref, dcb_ref):
        dm = _nt(dx_ref[...], wo_ref[...])
        for k, (gar, gbr) in enumerate(((ga0, gb0), (ga1, gb1))):
            cs = slice(k * H, (k + 1) * H)
            sa, sb = _sigmoid(gar[...]), _sigmoid(gbr[...])
            dmk = dm[:, cs]
            dga_ref[:, cs] = (dmk * ya_ref[:, cs].astype(F32) * sa * (1.0 - sa)).astype(BF16)
            dgb_ref[:, cs] = (dmk * yb_ref[:, cs].astype(F32) * sb * (1.0 - sb)).astype(BF16)
            dya_ref[:, cs] = (dmk * sa).astype(BF16)
            dyb_ref[:, cs] = (dmk * sb).astype(BF16)
        do_ref[...] = _nn(dya_ref[...], wa_ref[...])
        dcb_ref[...] = _nn(dyb_ref[...], wb_ref[...])

    row = lambda w: pl.BlockSpec((tm, w), lambda i: (i, 0))
    full = lambda a: pl.BlockSpec(a.shape, lambda i: (0,) * a.ndim)
    return pl.pallas_call(
        body, name="bwd_mix", grid=(L // tm,),
        in_specs=[row(D)] + _gate_specs(tm, H) + [row(D), row(D), full(wat), full(wbt), full(wout)],
        out_specs=[row(D)] * 4 + [row(H)] * 2,
        out_shape=[_sds((L, D), BF16)] * 4 + [_sds((L, H), F32)] * 2,
        compiler_params=_params(("parallel",), 56),
    )(dx1b, proj, proj, proj, proj, ya, yb, wat, wbt, wout)


def _fwd_ffn_up(h2, wgt, wut):
    L, D = h2.shape
    F = wgt.shape[0]
    tn = F // 2
    tm = min(L, 512)

    def body(h_ref, wg_ref, wu_ref, a_ref, b_ref, s_ref):
        h = h_ref[...]
        a, b = _nt(h, wg_ref[...]), _nt(h, wu_ref[...])
        a_ref[...] = a.astype(BF16)
        b_ref[...] = b.astype(BF16)
        s_ref[...] = (a * _sigmoid(a) * b).astype(BF16)

    wspec = pl.BlockSpec((tn, D), lambda j, i: (j, 0))
    ospec = pl.BlockSpec((tm, tn), lambda j, i: (i, j))
    return pl.pallas_call(
        body, name="fwd_ffn_up", grid=(2, L // tm),
        in_specs=[pl.BlockSpec((tm, D), lambda j, i: (i, 0)), wspec, wspec],
        out_specs=[ospec] * 3,
        out_shape=[_sds((L, F), BF16)] * 3,
        compiler_params=_params(("parallel", "parallel"), 48),
    )(h2, wgt, wut)


def _fwd_down_loss(s, wd, x1, target, g_final):
    L, D = x1.shape
    F = wd.shape[0]
    tm = min(L, 256)

    def body(s_ref, wd_ref, x1_ref, t_ref, g_ref, dx_ref, dxb_ref, red_ref):
        @pl.when(pl.program_id(0) == 0)
        def _():
            red_ref[...] = jnp.zeros_like(red_ref)

        g = g_ref[...]
        r, xh = _rms_stats(x1_ref[...] + _nn(s_ref[...], wd_ref[...]))
        e = xh * g - t_ref[...]
        dy = e * (1.0 / D)
        dx = _rms_bwd(dy * g, xh, r)
        dx_ref[...] = dx
        dxb_ref[...] = dx.astype(BF16)
        red_ref[0:1, :] += jnp.sum(dy * xh, axis=0, keepdims=True)
        red_ref[1:2, :] += jnp.broadcast_to(0.5 * jnp.sum(e * e) * (1.0 / D), (1, D))

    row = pl.BlockSpec((tm, D), lambda i: (i, 0))
    return pl.pallas_call(
        body, name="fwd_down_loss", grid=(L // tm,),
        in_specs=[pl.BlockSpec((tm, F), lambda i: (i, 0)), pl.BlockSpec((F, D), lambda i: (0, 0)),
                  row, row, pl.BlockSpec((1, D), lambda i: (0, 0))],
        out_specs=[row, row, pl.BlockSpec((8, D), lambda i: (0, 0))],
        out_shape=[_sds((L, D), F32), _sds((L, D), BF16), _sds((8, D), F32)],
        compiler_params=_params(("arbitrary",), 48),
    )(s, wd, x1, target, g_final)


def _bwd_down(dx2b, wd, a, b):
    L, D = dx2b.shape
    F = wd.shape[0]
    tn = F // 2
    tm = min(L, 512)

    def body(dx_ref, wd_ref, a_ref, b_ref, da_ref, db_ref):
        ds = _nt(dx_ref[...], wd_ref[...])
        a, b = a_ref[...].astype(F32), b_ref[...].astype(F32)
        sg = _sigmoid(a)
        da_ref[...] = (ds * b * sg * (1.0 + a * (1.0 - sg))).astype(BF16)
        db_ref[...] = (ds * a * sg).astype(BF16)

    ospec = pl.BlockSpec((tm, tn), lambda j, i: (i, j))
    return pl.pallas_call(
        body, name="bwd_down", grid=(2, L // tm),
        in_specs=[pl.BlockSpec((tm, D), lambda j, i: (i, 0)),
                  pl.BlockSpec((tn, D), lambda j, i: (j, 0)), ospec, ospec],
        out_specs=[ospec] * 2,
        out_shape=[_sds((L, F), BF16)] * 2,
        compiler_params=_params(("parallel", "parallel"), 48),
    )(dx2b, wd, a, b)


def _bwd_ffn_dh(da, db, wgt, wut, x1, dx2, g_ffn):
    L, D = x1.shape
    F = wgt.shape[0]
    tm = min(L, 256)

    def body(da_ref, db_ref, wg_ref, wu_ref, x1_ref, dx2_ref, g_ref, dx_ref, dxb_ref, red_ref):
        @pl.when(pl.program_id(0) == 0)
        def _():
            red_ref[...] = jnp.zeros_like(red_ref)

        dh = _nn(da_ref[...], wg_ref[...]) + _nn(db_ref[...], wu_ref[...])
        r, xh = _rms_stats(x1_ref[...])
        red_ref[0:1, :] += jnp.sum(dh * xh, axis=0, keepdims=True)
        dx = dx2_ref[...] + _rms_bwd(dh * g_ref[...], xh, r)
        dx_ref[...] = dx
        dxb_ref[...] = dx.astype(BF16)

    row = pl.BlockSpec((tm, D), lambda i: (i, 0))
    aspec = pl.BlockSpec((tm, F), lambda i: (i, 0))
    wspec = pl.BlockSpec((F, D), lambda i: (0, 0))
    return pl.pallas_call(
        body, name="bwd_ffn_dh", grid=(L // tm,),
        in_specs=[aspec, aspec, wspec, wspec, row, row, pl.BlockSpec((1, D), lambda i: (0, 0))],
        out_specs=[row, row, pl.BlockSpec((8, D), lambda i: (0, 0))],
        out_shape=[_sds((L, D), F32), _sds((L, D), BF16), _sds((8, D), F32)],
        compiler_params=_params(("arbitrary",), 56),
    )(da, db, wgt, wut, x1, dx2, g_ffn)


def _bwd_in(dproj, w_int, x, dx1, g_mix, after):
    L, D = x.shape
    N = w_int.shape[0]
    tm = min(L, 256)

    def body(dp_ref, w_ref, x_ref, dx1_ref, g_ref, after_ref, dx_ref, red_ref):
        @pl.when(pl.program_id(0) == 0)
        def _():
            red_ref[...] = jnp.zeros_like(red_ref)

        dh = _nn(dp_ref[...], w_ref[...])
        r, xh = _rms_stats(x_ref[...])
        red_ref[0:1, :] += jnp.sum(dh * xh, axis=0, keepdims=True)
        dx_ref[...] = dx1_ref[...] + _rms_bwd(dh * g_ref[...], xh, r)

    row = pl.BlockSpec((tm, D), lambda i: (i, 0))
    return pl.pallas_call(
        body, name="bwd_in", grid=(L // tm,),
        in_specs=[pl.BlockSpec((tm, N), lambda i: (i, 0)), pl.BlockSpec((N, D), lambda i: (0, 0)),
                  row, row, pl.BlockSpec((1, D), lambda i: (0, 0)), ANY],
        out_specs=[row, pl.BlockSpec((8, D), lambda i: (0, 0))],
        out_shape=[_sds((L, D), F32), _sds((8, D), F32)],
        compiler_params=_params(("arbitrary",), 56),
    )(dproj, w_int, x, dx1, g_mix, after)


def _mm_tn(name, a, b, a_spec, b_spec, o_block, n_out, n_k):
    def body(a_ref, b_ref, o_ref):
        part = _tn(a_ref[...], b_ref[...])

        @pl.when(pl.program_id(1) == 0)
        def _():
            o_ref[...] = part

        @pl.when(pl.program_id(1) > 0)
        def _():
            o_ref[...] += part

    return pl.pallas_call(
        body, name=name, grid=(n_out, n_k),
        in_specs=[a_spec, b_spec],
        out_specs=pl.BlockSpec((None,) + o_block, lambda j, k: (j, 0, 0)),
        out_shape=_sds((n_out,) + o_block, F32),
        compiler_params=_params(("parallel", "arbitrary"), 56),
    )(a, b)


TK_TOKENS = 2048


def _dw_cols(name, a, b, n_cols):
    L, M = a.shape
    tk = min(L, TK_TOKENS)
    return _mm_tn(name, a, b, pl.BlockSpec((tk, M), lambda j, k: (k, 0)),
                  pl.BlockSpec((tk, n_cols), lambda j, k: (k, j)), (M, n_cols), N_CHIPS, L // tk)


def _dw_rows(name, a, b):
    L, M = a.shape
    N = b.shape[1]
    tk = min(L, TK_TOKENS)
    return _mm_tn(name, a, b, pl.BlockSpec((tk, M // N_CHIPS), lambda j, k: (k, j)),
                  pl.BlockSpec((tk, N), lambda j, k: (k, 0)), (M // N_CHIPS, N), N_CHIPS, L // tk)


def _dw_rows2(name, a, b):
    L, M = a.shape
    N = b.shape[1]
    tk = min(L, TK_TOKENS)
    return _mm_tn(name, a, b, pl.BlockSpec((tk, M // 2), lambda j, k: (k, j)),
                  pl.BlockSpec((tk, N), lambda j, k: (k, 0)), (M // 2, N), 2, L // tk)


def _place():
    x, y, c = lax.axis_index("x"), lax.axis_index("y"), lax.axis_index("c")
    chips = [(1 - x, y), (x, 1 - y), (1 - x, 1 - y)]
    return x, y, c, 2 * x + y, chips


def _remote(src, dst, send_sem, recv_sem, device):
    return pltpu.make_async_remote_copy(src_ref=src, dst_ref=dst, send_sem=send_sem,
                                        recv_sem=recv_sem, device_id=device, device_id_type=MESH)


def _half(ref, lead, c, r2):
    return ref.at[lead, pl.ds(pl.multiple_of(c * r2, 16), r2), :]


def _cast_place(name, w, chip_idx):
    r, cols = w.shape
    tr = r // 2

    def body(k_ref, w_ref, o_ref):
        o_ref[...] = w_ref[...].astype(BF16)

    return pl.pallas_call(
        body, name=name,
        grid_spec=pltpu.PrefetchScalarGridSpec(
            num_scalar_prefetch=1, grid=(2,),
            in_specs=[pl.BlockSpec((tr, cols), lambda i, k_ref: (i, 0))],
            out_specs=pl.BlockSpec((None, tr, cols), lambda i, k_ref: (k_ref[0], i, 0))),
        out_shape=_sds((N_CHIPS, r, cols), BF16),
        compiler_params=_params(("parallel",), 48),
    )(chip_idx, w)


def _cast_place_t(name, w, chip_idx):
    r, cols = w.shape

    def body(k_ref, w_ref, o_ref):
        o_ref[...] = w_ref[...].T.astype(BF16)

    return pl.pallas_call(
        body, name=name,
        grid_spec=pltpu.PrefetchScalarGridSpec(
            num_scalar_prefetch=1, grid=(cols // LANES,),
            in_specs=[pl.BlockSpec((r, LANES), lambda i, k_ref: (0, i))],
            out_specs=pl.BlockSpec((None, LANES, r), lambda i, k_ref: (k_ref[0], i, 0))),
        out_shape=_sds((N_CHIPS, cols, r), BF16),
        compiler_params=_params(("parallel",), 48),
    )(chip_idx, w)


def _in_hbm(arrays):
    return [pltpu.with_memory_space_constraint(a, pltpu.HBM) for a in arrays]


def _gather_copies(bufs, whole, send_sems, recv_sems):
    x, y, c, k, chips = _place()
    pairs = []
    for w, buf in enumerate(bufs):
        for j, (cx, cy) in enumerate(chips):
            if w in whole:
                mine, theirs = buf.at[k], buf.at[2 * cx + cy]
            else:
                r2 = buf.shape[1] // 2
                mine, theirs = _half(buf, k, c, r2), _half(buf, 2 * cx + cy, c, r2)
            sems = (send_sems.at[w * 3 + j], recv_sems.at[w * 3 + j])
            pairs.append((_remote(mine, mine, *sems, (cx, cy, c)), _remote(theirs, theirs, *sems, (x, y, c))))
    return pairs


def _gather_start(groups):
    flat = [b for bufs, _ in groups for b in bufs]
    nb, ng = len(flat), len(groups)

    def body(*refs):
        ins, sems, token = refs[:nb], refs[nb:nb + 2 * ng], refs[-1]
        pos = 0
        for g, (bufs, whole) in enumerate(groups):
            for send, _ in _gather_copies(ins[pos:pos + len(bufs)], whole, sems[2 * g], sems[2 * g + 1]):
                send.start()
            pos += len(bufs)
        token[...] = jnp.zeros_like(token)

    sem_shapes = []
    for bufs, _ in groups:
        sem_shapes += [pltpu.SemaphoreType.DMA((3 * len(bufs),))] * 2
    out = pl.pallas_call(
        body, name="gather_start",
        in_specs=[HBM] * nb, out_specs=tuple([SEM] * (2 * ng) + [HBM] * nb + [VMEM]),
        out_shape=tuple(sem_shapes + [pltpu.HBM(b.shape, b.dtype) for b in flat] + [_sds((8, LANES), F32)]),
        input_output_aliases={i: 2 * ng + i for i in range(nb)},
        compiler_params=pltpu.CompilerParams(has_side_effects=EFFECT),
    )(*_in_hbm(flat))
    sems, thru, pos = [], [], 2 * ng
    for g, (bufs, _) in enumerate(groups):
        sems.append((out[2 * g], out[2 * g + 1]))
        thru.append(list(out[pos:pos + len(bufs)]))
        pos += len(bufs)
    return sems, thru, out[-1]


def _gather_wait(name, bufs, whole, sems, after):
    nb = len(bufs)

    def body(*refs):
        ins, send_sems, recv_sems = refs[:nb], refs[nb], refs[nb + 1]
        for send, arrival in _gather_copies(ins, whole, send_sems, recv_sems):
            send.wait_send()
            arrival.wait_recv()

    return pl.pallas_call(
        body, name=name,
        in_specs=[HBM] * nb + [SEM, SEM, ANY], out_specs=[HBM] * nb,
        out_shape=[pltpu.HBM(b.shape, b.dtype) for b in bufs],
        input_output_aliases={i: i for i in range(nb)},
        compiler_params=pltpu.CompilerParams(has_side_effects=EFFECT),
    )(*bufs, sems[0], sems[1], after)


def _gather_forward(name, bufs):
    n = len(bufs)

    def body(*refs):
        outs = refs[n:2 * n]
        send_sems, recv_sems = refs[2 * n:]
        x, y, c, _, chips = _place()
        sends = []
        for w in range(n):
            r2 = outs[w].shape[1] // 2
            for j, (cx, cy) in enumerate(chips):
                landed = _half(outs[w], 2 * cx + cy, c, r2)
                sends.append(_remote(landed, landed, send_sems.at[w * 3 + j], recv_sems.at[w * 3 + j],
                                     (x, y, 1 - c)))
        for cp in sends:
            cp.start()
        for w in range(n):
            r2 = outs[w].shape[1] // 2
            for j, (cx, cy) in enumerate(chips):
                got = _half(outs[w], 2 * cx + cy, 1 - c, r2)
                _remote(got, got, send_sems.at[w * 3 + j], recv_sems.at[w * 3 + j], (x, y, c)).wait_recv()
        for cp in sends:
            cp.wait_send()

    return pl.pallas_call(
        body, name=name,
        in_specs=[ANY] * n, out_specs=[ANY] * n,
        out_shape=[_sds(b.shape, b.dtype) for b in bufs],
        input_output_aliases={i: i for i in range(n)},
        scratch_shapes=[pltpu.SemaphoreType.DMA((n * 3,)), pltpu.SemaphoreType.DMA((n * 3,))],
    )(*bufs)


def _rs_sibling(name, grads):
    n = len(grads)

    def body(*refs):
        ins, outs = refs[:n], refs[n:2 * n]
        send_sems, recv_sems = refs[2 * n:]
        x, y, c, _, _ = _place()
        copies = []
        for w in range(n):
            r2 = ins[w].shape[1] // 2
            copies.append(_remote(_half(ins[w], slice(None), 1 - c, r2), outs[w],
                                  send_sems.at[w], recv_sems.at[w], (x, y, 1 - c)))
        for cp in copies:
            cp.start()
        for cp in copies:
            cp.wait()

    return pl.pallas_call(
        body, name=name,
        in_specs=[ANY] * n, out_specs=[ANY] * n,
        out_shape=[_sds((N_CHIPS, g.shape[1] // 2, g.shape[2]), F32) for g in grads],
        scratch_shapes=[pltpu.SemaphoreType.DMA((n,)), pltpu.SemaphoreType.DMA((n,))],
    )(*grads)


def _rs_add(name, grad3, from_sibling, c_idx):
    _, r2, cols = from_sibling.shape

    def body(c_ref, g_ref, s_ref, o_ref):
        o_ref[...] = (g_ref[...] + s_ref[...]).astype(BF16)

    return pl.pallas_call(
        body, name=name,
        grid_spec=pltpu.PrefetchScalarGridSpec(
            num_scalar_prefetch=1, grid=(N_CHIPS,),
            in_specs=[pl.BlockSpec((None, r2, cols), lambda k, c_ref: (k, c_ref[0], 0)),
                      pl.BlockSpec((None, r2, cols), lambda k, c_ref: (k, 0, 0))],
            out_specs=pl.BlockSpec((None, r2, cols), lambda k, c_ref: (k, 0, 0))),
        out_shape=_sds(from_sibling.shape, BF16),
        compiler_params=_params(("parallel",), 48),
    )(c_idx, grad3, from_sibling)


def _split_start(name, arrays, n_sems, pairs_fn):
    n = len(arrays)

    def body(*refs):
        for send, _ in pairs_fn(refs[:n], refs[n], refs[n + 1]):
            send.start()
        refs[-1][...] = jnp.zeros_like(refs[-1])

    out = pl.pallas_call(
        body, name=name,
        in_specs=[HBM] * n, out_specs=tuple([SEM, SEM] + [HBM] * n + [VMEM]),
        out_shape=tuple([pltpu.SemaphoreType.DMA((n_sems,))] * 2 + [pltpu.HBM(a.shape, a.dtype) for a in arrays]
                        + [_sds((8, LANES), F32)]),
        input_output_aliases={i: 2 + i for i in range(n)},
        compiler_params=pltpu.CompilerParams(has_side_effects=EFFECT),
    )(*_in_hbm(arrays))
    return (out[0], out[1]), list(out[2:2 + n]), out[-1]


def _split_wait(name, sems, arrays, pairs_fn, after):
    n = len(arrays)

    def body(*refs):
        for send, arrival in pairs_fn(refs[:n], refs[n], refs[n + 1]):
            send.wait_send()
            arrival.wait_recv()

    return list(pl.pallas_call(
        body, name=name,
        in_specs=[HBM] * n + [SEM, SEM, ANY], out_specs=[HBM] * n,
        out_shape=[pltpu.HBM(a.shape, a.dtype) for a in arrays],
        input_output_aliases={i: i for i in range(n)},
        compiler_params=pltpu.CompilerParams(has_side_effects=EFFECT),
    )(*arrays, sems[0], sems[1], after))


def _forward_pairs(bufs, send_sems, recv_sems):
    x, y, c, _, chips = _place()
    pairs = []
    for w, buf in enumerate(bufs):
        r2 = buf.shape[1] // 2
        for j, (cx, cy) in enumerate(chips):
            landed, theirs = _half(buf, 2 * cx + cy, c, r2), _half(buf, 2 * cx + cy, 1 - c, r2)
            sems = (send_sems.at[w * 3 + j], recv_sems.at[w * 3 + j])
            pairs.append((_remote(landed, landed, *sems, (x, y, 1 - c)), _remote(theirs, theirs, *sems, (x, y, c))))
    return pairs


def _sibling_pairs(arrays, send_sems, recv_sems):
    x, y, c, _, _ = _place()
    n = len(arrays) // 2
    pairs = []
    for w in range(n):
        r2 = arrays[w].shape[1] // 2
        cp = _remote(_half(arrays[w], slice(None), 1 - c, r2), arrays[n + w], send_sems.at[w], recv_sems.at[w],
                     (x, y, 1 - c))
        pairs.append((cp, cp))
    return pairs


def _ici_pairs(arrays, send_sems, recv_sems):
    x, y, c, _, chips = _place()
    n = len(arrays) // 2
    pairs = []
    for w in range(n):
        for j, (cx, cy) in enumerate(chips):
            cp = _remote(arrays[w].at[2 * cx + cy], arrays[n + w].at[j],
                         send_sems.at[w * 3 + j], recv_sems.at[w * 3 + j], (cx, cy, c))
            pairs.append((cp, cp))
    return pairs


def _rs_sum(name, partials, received, place_idx):
    _, r2, cols = partials.shape
    nb = 2
    tr = r2 // nb

    def body(idx_ref, p_ref, r_ref, o_ref):
        o_ref[...] = ((p_ref[...].astype(F32) + r_ref[0].astype(F32))
                      + (r_ref[1].astype(F32) + r_ref[2].astype(F32)))

    return pl.pallas_call(
        body, name=name,
        grid_spec=pltpu.PrefetchScalarGridSpec(
            num_scalar_prefetch=1, grid=(nb,),
            in_specs=[pl.BlockSpec((None, tr, cols), lambda i, idx: (idx[0], i, 0)),
                      pl.BlockSpec((3, tr, cols), lambda i, idx: (0, i, 0))],
            out_specs=pl.BlockSpec((tr, cols), lambda i, idx: (idx[1] * nb + i, 0))),
        out_shape=_sds((2 * r2, cols), F32),
        compiler_params=_params(("parallel",), 48),
    )(place_idx, partials, received)


def _rs_share(name, shards):
    n = len(shards)

    def body(*refs):
        outs = refs[n:2 * n]
        send_sems, recv_sems = refs[2 * n:]
        x, y, c, _, _ = _place()
        sends = []
        for w in range(n):
            r2 = outs[w].shape[0] // 2
            mine = outs[w].at[pl.ds(pl.multiple_of(c * r2, 8), r2), :]
            sends.append(_remote(mine, mine, send_sems.at[w], recv_sems.at[w], (x, y, 1 - c)))
        for cp in sends:
            cp.start()
        for w in range(n):
            r2 = outs[w].shape[0] // 2
            theirs = outs[w].at[pl.ds(pl.multiple_of((1 - c) * r2, 8), r2), :]
            _remote(theirs, theirs, send_sems.at[w], recv_sems.at[w], (x, y, c)).wait_recv()
        for cp in sends:
            cp.wait_send()

    return pl.pallas_call(
        body, name=name,
        in_specs=[ANY] * n, out_specs=[ANY] * n,
        out_shape=[_sds(s.shape, F32) for s in shards],
        input_output_aliases={i: i for i in range(n)},
        scratch_shapes=[pltpu.SemaphoreType.DMA((n,)), pltpu.SemaphoreType.DMA((n,))],
    )(*shards)


def _small_allreduce(red_mix, red_ffn, red_final, red_hg, g_conv):
    rows = N_SMALL_ROWS
    D = red_mix.shape[1]
    H = red_hg.shape[1]

    def body(mix_ref, ffn_ref, fin_ref, hg_ref, cv_ref, sum_ref, all_ref, in_ref, send_sems, recv_sems):
        in_ref[...] = jnp.zeros_like(in_ref)
        in_ref[0:1, :] = mix_ref[0:1, :]
        in_ref[1:2, :] = ffn_ref[0:1, :]
        in_ref[2:3, :] = fin_ref[0:1, :]
        gam = hg_ref[1:2, 0:HEAD_DIM]
        for h in range(1, H // HEAD_DIM):
            gam = gam + hg_ref[1:2, h * HEAD_DIM:(h + 1) * HEAD_DIM]
        in_ref[3:4, 0:HEAD_DIM] = gam
        in_ref[3:4, HEAD_DIM:2 * HEAD_DIM] = fin_ref[1:2, 0:HEAD_DIM]
        in_ref[4:5, 0:H] = hg_ref[0:1, :]
        in_ref[6:9, 0:H] = cv_ref[...]
        x, y, c, _, _ = _place()
        me = 4 * x + 2 * y + c
        all_ref[me] = in_ref[...]
        copies = []
        for m in range(1, 8):
            mx, my, mc = (m >> 2) & 1, (m >> 1) & 1, m & 1
            px, py, pc = x ^ mx, y ^ my, c ^ mc
            copies.append((_remote(in_ref, all_ref.at[me], send_sems.at[m - 1], recv_sems.at[m - 1],
                                   (px, py, pc)), 4 * px + 2 * py + pc, m))
        for cp, _, _ in copies:
            cp.start()
        for _, peer, m in copies:
            _remote(in_ref, all_ref.at[peer], send_sems.at[m - 1], recv_sems.at[m - 1],
                    (x, y, c)).wait_recv()
        for cp, _, _ in copies:
            cp.wait_send()
        total = all_ref[0]
        for d in range(1, 8):
            total = total + all_ref[d]
        sum_ref[...] = total

    return pl.pallas_call(
        body, name="small_allreduce",
        in_specs=[VMEM] * 5, out_specs=[VMEM, VMEM],
        out_shape=[_sds((rows, D), F32), _sds((8, rows, D), F32)],
        scratch_shapes=[pltpu.VMEM((rows, D), F32), pltpu.SemaphoreType.DMA((7,)),
                        pltpu.SemaphoreType.DMA((7,))],
    )(red_mix, red_ffn, red_final, red_hg, g_conv)[0]


def _adamw_math(w, g, m, v):
    m = ADAM_B1 * m + (1.0 - ADAM_B1) * g
    v = ADAM_B2 * v + (1.0 - ADAM_B2) * jnp.square(g)
    m_hat = m / (1.0 - ADAM_B1 ** ADAM_STEP)
    v_hat = v / (1.0 - ADAM_B2 ** ADAM_STEP)
    delta = -ADAM_LR * (m_hat / (jnp.sqrt(v_hat) + ADAM_EPS) + ADAM_WD * w)
    return delta, m, v


def _adamw(name, g, w, m, v):
    r, cols = g.shape
    tr = r // 4

    def body(g_ref, w_ref, m_ref, v_ref, d_ref, mo_ref, vo_ref):
        d_ref[...], mo_ref[...], vo_ref[...] = _adamw_math(w_ref[...], g_ref[...], m_ref[...], v_ref[...])

    blk = pl.BlockSpec((tr, cols), lambda i: (i, 0))
    return pl.pallas_call(
        body, name=name, grid=(r // tr,),
        in_specs=[blk] * 4, out_specs=[blk] * 3, out_shape=[_sds((r, cols), F32)] * 3,
        compiler_params=_params(("parallel",), 48),
    )(g, w, m, v)


def _small_update(total, chip_idx, ws, ms, vs):
    n = len(ws)
    H = ws[1].shape[1]

    def body(idx_ref, tot_ref, *refs):
        w, m, v, outs = refs[:n], refs[n:2 * n], refs[2 * n:3 * n], refs[3 * n:]
        chip = idx_ref[0]
        p0 = _lower_bound(w[1][...])
        dl0 = p0 * (1.0 - p0) * tot_ref[4:5, 0:H]
        conv = jnp.zeros((3, LANES), F32)
        for k in range(N_CHIPS):
            conv = jnp.where(chip == k, tot_ref[6:9, k * LANES:(k + 1) * LANES], conv)
        grads = [tot_ref[0:1, :], None, tot_ref[3:4, 0:HEAD_DIM], conv, tot_ref[1:2, :], tot_ref[2:3, :]]
        for p in range(n):
            g_ref, d_ref, mo_ref, vo_ref = outs[4 * p:4 * p + 4]
            if p == 1:
                for row, g in ((slice(0, 1), dl0), (slice(1, 2), -dl0)):
                    g_ref[row, :] = g
                    d_ref[row, :], mo_ref[row, :], vo_ref[row, :] = _adamw_math(
                        w[p][row, :], g, m[p][row, :], v[p][row, :])
            else:
                g_ref[...] = grads[p]
                d_ref[...], mo_ref[...], vo_ref[...] = _adamw_math(w[p][...], grads[p], m[p][...], v[p][...])
        outs[4 * n][...] = tot_ref[3:4, HEAD_DIM:2 * HEAD_DIM]

    full = lambda a: pl.BlockSpec(a.shape, lambda i, idx: (0,) * a.ndim)
    out_shape = [_sds(w.shape, F32) for w in ws for _ in range(4)] + [_sds((1, LANES), F32)]
    return pl.pallas_call(
        body, name="small_update",
        grid_spec=pltpu.PrefetchScalarGridSpec(
            num_scalar_prefetch=1, grid=(1,),
            in_specs=[full(total)] + [full(a) for a in ws + ms + vs],
            out_specs=[full(s) for s in out_shape]),
        out_shape=out_shape,
    )(chip_idx, total, *ws, *ms, *vs)


def kernel(x, norm_mix_g, w_in, lower_bounds, hg_norm_g, conv_w, w_branch_a, w_branch_b, w_out, norm_ffn_g, w_ffn_gate, w_ffn_up, w_ffn_down, norm_final_g, loss_target, m_norm_mix_g, m_w_in, m_lower_bounds, m_hg_norm_g, m_conv_w, m_w_branch_a, m_w_branch_b, m_w_out, m_norm_ffn_g, m_w_ffn_gate, m_w_ffn_up, m_w_ffn_down, m_norm_final_g, v_norm_mix_g, v_w_in, v_lower_bounds, v_hg_norm_g, v_conv_w, v_w_branch_a, v_w_branch_b, v_w_out, v_norm_ffn_g, v_w_ffn_gate, v_w_ffn_up, v_w_ffn_down, v_norm_final_g):
    _, L, D = x.shape
    H = D // 2
    assert lower_bounds.shape == (2, H) and hg_norm_g.shape == (1, HEAD_DIM)
    assert conv_w.shape == (1, 3, LANES) and w_in.shape[2] * N_CHIPS == 11 * H
    x2d, target = x.reshape(L, D), loss_target.reshape(L, D)
    g_final = norm_final_g.reshape(1, D)
    chip = 2 * lax.axis_index("x") + lax.axis_index("y")
    core = lax.axis_index("c")

    tr = lambda w: jnp.transpose(w[0])
    big = [w_in[0], w_branch_a[0], w_branch_b[0], w_out[0], tr(w_ffn_gate), tr(w_ffn_up), w_ffn_down[0]]
    big_m = [m_w_in[0], m_w_branch_a[0], m_w_branch_b[0], m_w_out[0], tr(m_w_ffn_gate), tr(m_w_ffn_up),
             m_w_ffn_down[0]]
    big_v = [v_w_in[0], v_w_branch_a[0], v_w_branch_b[0], v_w_out[0], tr(v_w_ffn_gate), tr(v_w_ffn_up),
             v_w_ffn_down[0]]
    names = ["w_in", "w_branch_a", "w_branch_b", "w_out", "w_ffn_gate", "w_ffn_up", "w_ffn_down"]

    chip_idx = chip.reshape(1).astype(jnp.int32)
    placed = [(_cast_place_t if j < 3 else _cast_place)("place_" + nm, w, chip_idx)
              for j, (nm, w) in enumerate(zip(names, big))]
    conv_placed = lax.dynamic_update_slice(jnp.zeros((N_CHIPS, 3, LANES), F32), conv_w, (chip, 0, 0))
    sems, in_flight, token = _gather_start([([placed[0], conv_placed], {1}), (placed[1:4], set()),
                                            (placed[4:], set())])
    w_in_landed, conv_all = _gather_wait("gather_wait_in", in_flight[0], {1}, sems[0], token)
    (w_int3,) = _gather_forward("gather_fwd_in", [w_in_landed])
    w_int = w_int3.reshape(-1, D)
    conv_full = jnp.transpose(conv_all, (1, 0, 2)).reshape(3, H)

    def behind(value, token):
        return lax.optimization_barrier((value, token))[0]

    h, proj = _fwd_proj(x2d, norm_mix_g, w_int3)
    og, o_pre, s_saved = _hgrn_fwd(proj, lower_bounds, hg_norm_g, H)
    landed = _gather_wait("gather_wait_mix", in_flight[1], set(), sems[1], og)
    fwd_sems, landed, token = _split_start("gather_fwd_mix_start", landed, 9, _forward_pairs)
    cb = _conv_fwd(behind(proj, token), conv_full, H)
    wat3, wbt3, wout3 = _split_wait("gather_fwd_mix_wait", fwd_sems, landed, _forward_pairs, cb)
    wat, wbt, wout = wat3.reshape(D, H), wbt3.reshape(D, H), wout3.reshape(D, D)
    landed = _gather_wait("gather_wait_ffn", in_flight[2], set(), sems[2], cb)
    fwd_sems, landed, token = _split_start("gather_fwd_ffn_start", landed, 9, _forward_pairs)
    ya, yb, merged, x1, h2 = _fwd_mix(behind(og, token), cb, proj, x2d, wat, wbt, wout, norm_ffn_g, H)
    wgt3, wut3, wd3 = _split_wait("gather_fwd_ffn_wait", fwd_sems, landed, _forward_pairs, h2)
    d_ff = N_CHIPS * wd3.shape[1]
    wgt, wut, wd = wgt3.reshape(d_ff, D), wut3.reshape(d_ff, D), wd3.reshape(d_ff, D)
    ffn_a, ffn_b, ffn_s = _fwd_ffn_up(h2, wgt, wut)
    dx2, dx2b, red_final = _fwd_down_loss(ffn_s, wd, x1, target, g_final)

    c_idx = core.reshape(1).astype(jnp.int32)
    place_idx = jnp.stack([chip, core]).astype(jnp.int32)

    def sibling_start(tag, grads):
        bufs = [lax.empty((N_CHIPS, g.shape[1] // 2, g.shape[2]), F32) for g in grads]
        return _split_start("rs_sibling_start_" + tag, list(grads) + bufs, len(grads), _sibling_pairs)

    def ici_start(tag, js, grads, from_sibling):
        partials = [_rs_add("rs_add_" + names[j], g, s, c_idx) for j, g, s in zip(js, grads, from_sibling)]
        landings = [lax.empty((3,) + p.shape[1:], BF16) for p in partials]
        return _split_start("rs_ici_start_" + tag, partials + landings, 3 * len(js), _ici_pairs)

    def ici_start_behind(tag, js, started, after):
        n = len(js)
        arrays = _split_wait("rs_sibling_wait_" + tag, started[0], started[1], _sibling_pairs, after)
        return ici_start(tag, js, arrays[:n], arrays[n:])

    def rs_end(tag, js, started, after):
        n = len(js)
        arrays = _split_wait("rs_ici_wait_" + tag, started[0], started[1], _ici_pairs, after)
        halves = [_rs_sum("rs_sum_" + names[j], p, r, place_idx)
                  for j, p, r in zip(js, arrays[:n], arrays[n:])]
        return _rs_share("rs_share_" + tag, halves)

    shards3 = lambda g: g.reshape(N_CHIPS, d_ff // N_CHIPS, D)
    da, db = _bwd_down(dx2b, wd, ffn_a, ffn_b)
    g_wd = shards3(_dw_rows2("dw_ffn_down", ffn_s, dx2b))
    g_wg = shards3(_dw_rows2("dw_ffn_gate", da, h2))
    g_wu = shards3(_dw_rows2("dw_ffn_up", db, h2))
    ffn_sibling = sibling_start("ffn", [g_wg, g_wu, g_wd])
    dx1, dx1b, red_ffn = _bwd_ffn_dh(behind(da, ffn_sibling[2]), db, wgt, wut, x1, dx2, norm_ffn_g)
    ffn_ici = ici_start_behind("ffn", [4, 5, 6], ffn_sibling, dx1b)
    dya, dyb, dga, dgb, d_o, d_cb = _bwd_mix(behind(dx1b, ffn_ici[2]), proj, ya, yb, wat, wbt, wout, H)
    g_wout = _dw_rows("dw_out", merged, dx1b)
    g_wa = _dw_cols("dw_branch_a", og, dya, D // N_CHIPS)
    g_wb = _dw_cols("dw_branch_b", cb, dyb, D // N_CHIPS)
    mix_sibling = sibling_start("mix", [g_wa, g_wb, g_wout])
    dq, df, dv, dg, red_hg = _hgrn_bwd(proj, lower_bounds, hg_norm_g, o_pre, d_o, s_saved, H, mix_sibling[2])
    mix_ici = ici_start_behind("mix", [1, 2, 3], mix_sibling, dq)
    dcg, dbg, dxb, g_conv = _conv_bwd(proj, conv_full, behind(d_cb, mix_ici[2]), H)
    dproj = jnp.concatenate([dq, df, dv, dg, dcg, dbg, dxb, dga, dgb], axis=1)
    g_win = _dw_cols("dw_in", h, dproj, w_int3.shape[1])
    in_ici = ici_start("in", [0], [g_win], _rs_sibling("rs_sibling_in", [g_win]))
    grad_x, red_mix = _bwd_in(dproj, w_int, x2d, dx1, norm_mix_g, in_ici[2])
    shard_grads = (rs_end("in", [0], in_ici, grad_x) + rs_end("mix", [1, 2, 3], mix_ici, in_ici[2])
                   + rs_end("ffn", [4, 5, 6], ffn_ici, in_ici[2]))
    big_out = [_adamw("adamw_" + nm, g, w, m, v)
               for nm, g, w, m, v in zip(names, shard_grads, big, big_m, big_v)]

    total = _small_allreduce(red_mix, red_ffn, red_final, red_hg, g_conv)

    def smalls(mix, lb, hg, cw, ffn, fin):
        return [mix, lb, hg, cw[0], ffn, fin.reshape(1, D)]

    small_out = _small_update(
        total, chip_idx,
        smalls(norm_mix_g, lower_bounds, hg_norm_g, conv_w, norm_ffn_g, norm_final_g),
        smalls(m_norm_mix_g, m_lower_bounds, m_hg_norm_g, m_conv_w, m_norm_ffn_g, m_norm_final_g),
        smalls(v_norm_mix_g, v_lower_bounds, v_hg_norm_g, v_conv_w, v_norm_ffn_g, v_norm_final_g))

    def outputs(i):
        big_i = [shard_grads[j] if i == 0 else big_out[j][i - 1] for j in range(7)]
        mix, lb, hg, cw, ffn, fin = [small_out[4 * p + i] for p in range(6)]
        return [mix, big_i[0][None], lb, hg, cw[None], big_i[1][None], big_i[2][None], big_i[3][None], ffn,
                big_i[4].T[None], big_i[5].T[None], big_i[6][None], fin.reshape(D)]

    outs = [small_out[24][0, 0], grad_x.reshape(1, L, D)]
    for i in range(4):
        outs += outputs(i)
    return tuple(outs)
```

```python
import functools

import jax
import jax.numpy as jnp
from jax import lax
from jax.experimental import pallas as pl
from jax.experimental.pallas import tpu as pltpu

F32 = jnp.float32
BF16 = jnp.bfloat16
EPS = 1e-6
CHUNK = 32
HEAD_DIM = 128
LANES = 128
N_CHIPS = 4
N_SMALL_ROWS = 16

ADAM_LR = 0.001
ADAM_B1 = 0.9
ADAM_B2 = 0.999
ADAM_EPS = 1e-08
ADAM_WD = 0.01
ADAM_STEP = 10

MESH = pl.DeviceIdType.MESH
ANY = pl.BlockSpec(memory_space=pl.ANY)
VMEM = pl.BlockSpec(memory_space=pltpu.VMEM)
HBM = pl.BlockSpec(memory_space=pltpu.HBM)
SEM = pl.BlockSpec(memory_space=pltpu.SEMAPHORE)
EFFECT = pltpu.SideEffectType.DATAFLOW_SIDE_EFFECTING


def _sds(shape, dtype):
    return jax.ShapeDtypeStruct(shape, dtype)


def _params(semantics, vmem_mb):
    return pltpu.CompilerParams(dimension_semantics=semantics, vmem_limit_bytes=vmem_mb << 20)


def _nn(a, b):
    return lax.dot_general(a, b, (((1,), (0,)), ((), ())), preferred_element_type=F32)


def _nt(a, b):
    return lax.dot_general(a, b, (((1,), (1,)), ((), ())), preferred_element_type=F32)


def _tn(a, b):
    return lax.dot_general(a, b, (((0,), (0,)), ((), ())), preferred_element_type=F32)


def _sigmoid(x):
    return jax.nn.sigmoid(x)


def _rms_stats(x):
    r = lax.rsqrt(jnp.mean(x * x, axis=-1, keepdims=True) + EPS)
    return r, x * r


def _rms_bwd(dxh, xh, r):
    return r * (dxh - xh * jnp.mean(dxh * xh, axis=-1, keepdims=True))


def _fwd_proj(x, g_mix, w_int3):
    L, D = x.shape
    tn = w_int3.shape[1]
    tm = min(L, 1024)

    def body(x_ref, g_ref, w_ref, h_ref, p_ref):
        @pl.when(pl.program_id(1) == 0)
        def _():
            _, xh = _rms_stats(x_ref[...])
            h_ref[...] = (xh * g_ref[...]).astype(BF16)

        p_ref[...] = _nt(h_ref[...], w_ref[...])

    return pl.pallas_call(
        body, name="fwd_proj", grid=(L // tm, N_CHIPS),
        in_specs=[pl.BlockSpec((tm, D), lambda i, j: (i, 0)),
                  pl.BlockSpec((1, D), lambda i, j: (0, 0)),
                  pl.BlockSpec((None, tn, D), lambda i, j: (j, 0, 0))],
        out_specs=[pl.BlockSpec((tm, D), lambda i, j: (i, 0)),
                   pl.BlockSpec((tm, tn), lambda i, j: (i, j))],
        out_shape=[_sds((L, D), BF16), _sds((L, N_CHIPS * tn), F32)],
        compiler_params=_params(("parallel", "arbitrary"), 48),
    )(x, g_mix, w_int3)


def _lower_bound(lbp):
    l0, l1 = lbp[0:1, :], lbp[1:2, :]
    m = jnp.maximum(l0, l1)
    e0, e1 = jnp.exp(l0 - m), jnp.exp(l1 - m)
    return e0 / (e0 + e1)


def _seg_scan(x, r32, forward):
    n = x.shape[0]
    s = 1
    while s < CHUNK:
        if forward:
            x = x + jnp.where(r32 >= s, pltpu.roll(x, s, 0), 0.0)
        else:
            x = x + jnp.where(r32 < CHUNK - s, pltpu.roll(x, n - s, 0), 0.0)
        s *= 2
    return x


def _bcast_row(x, row):
    n, w = x.shape
    nc = n // CHUNK
    x3 = x.reshape(nc, CHUNK, w)
    return jnp.broadcast_to(x3[:, row:row + 1, :], (nc, CHUNK, w)).reshape(n, w)


def _hgrn_prep(q_raw, f_raw, lb):
    r32 = lax.broadcasted_iota(jnp.int32, f_raw.shape, 0) & (CHUNK - 1)
    sig = _sigmoid(f_raw)
    f = lb + (1.0 - lb) * sig
    b = _seg_scan(jnp.log(f), r32, True)
    a = _bcast_row(b, CHUNK // 2 - 1)
    bl = _bcast_row(b, CHUNK - 1)
    sq = _sigmoid(q_raw)
    q = q_raw * sq * (HEAD_DIM ** -0.5)
    return dict(r32=r32, sig=sig, f=f, k=1.0 - f, b=b, a=a, bl=bl, sq=sq, q=q)


def _chunk_masks(n):
    ri = lax.broadcasted_iota(jnp.int32, (n, n), 0)
    ci = lax.broadcasted_iota(jnp.int32, (n, n), 1)
    same = (ri // CHUNK) == (ci // CHUNK)
    return same & (ci <= ri), same & (ri <= ci)


def _hgrn_fwd(proj, lower_bounds, gamma, H):
    L = proj.shape[0]
    nh = H // HEAD_DIM
    TL = min(L, 256)
    nc = TL // CHUNK

    def body(q_ref, f_ref, v_ref, g_ref, lbp_ref, gam_ref, og_ref, o_ref, s_ref, st_ref):
        @pl.when(pl.program_id(0) == 0)
        def _():
            st_ref[...] = jnp.zeros_like(st_ref)

        lb = _lower_bound(lbp_ref[...])
        gam = gam_ref[...]
        mask, _ = _chunk_masks(TL)
        rowc = lax.broadcasted_iota(jnp.int32, (TL, HEAD_DIM), 0) // CHUNK
        for h in range(nh):
            hs = slice(h * HEAD_DIM, (h + 1) * HEAD_DIM)
            p = _hgrn_prep(q_ref[:, hs], f_ref[:, hs], lb[:, hs])
            v = v_ref[:, hs]
            vb = v.astype(BF16)
            vt = v.T.astype(BF16)
            q_hat = (p["q"] * jnp.exp(p["b"] - p["a"])).astype(BF16)
            k_hat = (p["k"] * jnp.exp(p["a"] - p["b"])).astype(BF16)
            q_in = (p["q"] * jnp.exp(p["b"])).astype(BF16)
            k_out = (p["k"] * jnp.exp(p["bl"] - p["b"])).astype(BF16)
            dec = jnp.exp(p["bl"])
            att = jnp.where(mask, _nt(q_hat, k_hat), 0.0).astype(BF16)
            o_intra = _nn(att, vb)
            st = st_ref[h]
            for c in range(nc):
                rs = slice(c * CHUNK, (c + 1) * CHUNK)
                stb = st.astype(BF16)
                s_ref[c, h] = stb
                o_ref[rs, hs] = o_intra[rs] + _nt(q_in[rs], stb)
                k_c = jnp.where(rowc == c, k_out, jnp.zeros_like(k_out))
                st = st * dec[c * CHUNK:c * CHUNK + 1, :] + _nn(vt, k_c)
            st_ref[h] = st
            o = o_ref[:, hs]
            _, xh = _rms_stats(o)
            gr = g_ref[:, hs]
            og_ref[:, hs] = (xh * gam * (gr * _sigmoid(gr))).astype(BF16)

    col = lambda k: pl.BlockSpec((TL, H), lambda i, k=k: (i, k))
    return pl.pallas_call(
        body, name="hgrn_fwd", grid=(L // TL,),
        in_specs=[col(0), col(1), col(2), col(3),
                  pl.BlockSpec(lower_bounds.shape, lambda i: (0, 0)),
                  pl.BlockSpec(gamma.shape, lambda i: (0, 0))],
        out_specs=[pl.BlockSpec((TL, H), lambda i: (i, 0)),
                   pl.BlockSpec((TL, H), lambda i: (i, 0)),
                   pl.BlockSpec((nc, nh, HEAD_DIM, HEAD_DIM), lambda i: (i, 0, 0, 0))],
        out_shape=[_sds((L, H), BF16), _sds((L, H), F32),
                   _sds((L // CHUNK, nh, HEAD_DIM, HEAD_DIM), BF16)],
        scratch_shapes=[pltpu.VMEM((nh, HEAD_DIM, HEAD_DIM), F32)],
        compiler_params=_params(("arbitrary",), 48),
    )(proj, proj, proj, proj, lower_bounds, gamma)


def _hgrn_bwd(proj, lower_bounds, gamma, o_pre, d_out, s_saved, H, after):
    L = proj.shape[0]
    nh = H // HEAD_DIM
    TL = min(L, 256)
    nc = TL // CHUNK
    nt = L // TL

    def body(q_ref, f_ref, v_ref, g_ref, lbp_ref, gam_ref, o_ref, d_ref, s_ref, after_ref,
             dq_ref, df_ref, dv_ref, dg_ref, red_ref, dst_ref, dsall_ref, tmp_ref):
        @pl.when(pl.program_id(0) == 0)
        def _():
            dst_ref[...] = jnp.zeros_like(dst_ref)
            red_ref[...] = jnp.zeros_like(red_ref)

        lb = _lower_bound(lbp_ref[...])
        gam = gam_ref[...]
        mask, mask_t = _chunk_masks(TL)
        rowc = lax.broadcasted_iota(jnp.int32, (TL, HEAD_DIM), 0) // CHUNK
        for h in range(nh):
            hs = slice(h * HEAD_DIM, (h + 1) * HEAD_DIM)
            qr, gr, lbh = q_ref[:, hs], g_ref[:, hs], lb[:, hs]
            p = _hgrn_prep(qr, f_ref[:, hs], lbh)
            vb = v_ref[:, hs].astype(BF16)
            eba, eab = jnp.exp(p["b"] - p["a"]), jnp.exp(p["a"] - p["b"])
            eb, elb = jnp.exp(p["b"]), jnp.exp(p["bl"] - p["b"])
            dec = jnp.exp(p["bl"])
            q_hat, k_hat = p["q"] * eba, p["k"] * eab
            q_in, k_out = p["q"] * eb, p["k"] * elb
            q_hat_b, k_hat_b = q_hat.astype(BF16), k_hat.astype(BF16)
            q_in_b, k_out_b = q_in.astype(BF16), k_out.astype(BF16)

            o, dout = o_ref[:, hs], d_ref[:, hs]
            sg = _sigmoid(gr)
            r, xh = _rms_stats(o)
            dg_ref[:, hs] = (dout * (xh * gam) * (sg * (1.0 + gr * (1.0 - sg)))).astype(BF16)
            dn = dout * (gr * sg)
            red_ref[1:2, hs] += jnp.sum(dn * xh, axis=0, keepdims=True)
            do = _rms_bwd(dn * gam, xh, r)
            dob = do.astype(BF16)
            dot_b = do.T.astype(BF16)

            att_t = jnp.where(mask_t, _nt(k_hat_b, q_hat_b), 0.0).astype(BF16)
            dv_intra = _nn(att_t, dob)
            datt = jnp.where(mask, _nt(dob, vb), 0.0).astype(BF16)
            dqh = _nn(datt, k_hat_b)
            datt_t = jnp.where(mask_t, _nt(vb, dob), 0.0).astype(BF16)
            dkh = _nn(datt_t, q_hat_b)

            dst = dst_ref[h]
            for c in reversed(range(nc)):
                dsall_ref[c] = dst
                q_c = jnp.where(rowc == c, q_in_b, jnp.zeros_like(q_in_b))
                dst = dst * dec[c * CHUNK:c * CHUNK + 1, :] + _nn(dot_b, q_c)
            dst_ref[h] = dst
            for c in range(nc):
                rs = slice(c * CHUNK, (c + 1) * CHUNK)
                ds_c = dsall_ref[c]
                dsb = ds_c.astype(BF16)
                st_prev = s_ref[c, h]
                tmp_ref[0, rs, :] = _nt(k_out_b[rs], dsb)
                tmp_ref[1, rs, :] = _nn(vb[rs], dsb)
                tmp_ref[2, rs, :] = _nn(dob[rs], st_prev)
                ddec = jnp.sum(ds_c * st_prev.astype(F32), axis=0, keepdims=True)
                tmp_ref[3, rs, :] = jnp.broadcast_to(ddec * dec[c * CHUNK:c * CHUNK + 1, :],
                                                     (CHUNK, HEAD_DIM))
            dko, dqi = tmp_ref[1], tmp_ref[2]
            dq = dqh * eba + dqi * eb
            dk = dkh * eab + dko * elb
            tko = dko * k_out
            db = dqh * q_hat - dkh * k_hat + dqi * q_in - tko
            dlog = (_seg_scan(db, p["r32"], False)
                    + _bcast_row(_seg_scan(tko, p["r32"], True), CHUNK - 1) + tmp_ref[3])
            df = dlog / p["f"] - dk
            sig = p["sig"]
            red_ref[0:1, hs] += jnp.sum(df * (1.0 - sig), axis=0, keepdims=True)
            df_ref[:, hs] = (df * (1.0 - lbh) * sig * (1.0 - sig)).astype(BF16)
            sq = p["sq"]
            dq_ref[:, hs] = (dq * (HEAD_DIM ** -0.5) * (sq * (1.0 + qr * (1.0 - sq)))).astype(BF16)
            dv_ref[:, hs] = (dv_intra + tmp_ref[0]).astype(BF16)

    col = lambda k: pl.BlockSpec((TL, H), lambda i, k=k: (nt - 1 - i, k))
    rev = pl.BlockSpec((TL, H), lambda i: (nt - 1 - i, 0))
    return pl.pallas_call(
        body, name="hgrn_bwd", grid=(nt,),
        in_specs=[col(0), col(1), col(2), col(3),
                  pl.BlockSpec(lower_bounds.shape, lambda i: (0, 0)),
                  pl.BlockSpec(gamma.shape, lambda i: (0, 0)),
                  rev, rev,
                  pl.BlockSpec((nc, nh, HEAD_DIM, HEAD_DIM), lambda i: (nt - 1 - i, 0, 0, 0)), ANY],
        out_specs=[rev, rev, rev, rev, pl.BlockSpec((8, H), lambda i: (0, 0))],
        out_shape=[_sds((L, H), BF16)] * 4 + [_sds((8, H), F32)],
        scratch_shapes=[pltpu.VMEM((nh, HEAD_DIM, HEAD_DIM), F32),
                        pltpu.VMEM((nc, HEAD_DIM, HEAD_DIM), F32),
                        pltpu.VMEM((4, TL, HEAD_DIM), F32)],
        compiler_params=_params(("arbitrary",), 48),
    )(proj, proj, proj, proj, lower_bounds, gamma, o_pre, d_out, s_saved, after)


def _shift_down(u, s, row):
    return jnp.where(row >= s, pltpu.roll(u, s, 0), 0.0)


def _shift_up(u, s, row):
    n = u.shape[0]
    return jnp.where(row < n - s, pltpu.roll(u, n - s, 0), 0.0)


def _conv_specs(L, H):
    per = H // LANES
    return [pl.BlockSpec((L, LANES), lambda j, o=o: (0, o * per + j)) for o in (4, 5, 6)]


def _conv_fwd(proj, conv_w, H, after):
    L = proj.shape[0]

    def body(c_ref, b_ref, x_ref, w_ref, after_ref, o_ref):
        row = lax.broadcasted_iota(jnp.int32, (L, LANES), 0)
        u = c_ref[...] * x_ref[...]
        w = w_ref[...]
        y = w[0:1] * _shift_down(u, 2, row) + w[1:2] * _shift_down(u, 1, row) + w[2:3] * u
        o_ref[...] = (b_ref[...] * y).astype(BF16)

    return pl.pallas_call(
        body, name="conv_fwd", grid=(H // LANES,),
        in_specs=_conv_specs(L, H) + [pl.BlockSpec((3, LANES), lambda j: (0, j)), ANY],
        out_specs=pl.BlockSpec((L, LANES), lambda j: (0, j)),
        out_shape=_sds((L, H), BF16),
        compiler_params=_params(("parallel",), 48),
    )(proj, proj, proj, conv_w, after)


def _conv_bwd(proj, conv_w, dcb, H, after):
    L = proj.shape[0]

    def body(c_ref, b_ref, x_ref, w_ref, d_ref, after_ref, dc_ref, db_ref, dx_ref, dw_ref):
        row = lax.broadcasted_iota(jnp.int32, (L, LANES), 0)
        cg, xb = c_ref[...], x_ref[...]
        u = cg * xb
        u1, u2 = _shift_down(u, 1, row), _shift_down(u, 2, row)
        w = w_ref[...]
        y = w[0:1] * u2 + w[1:2] * u1 + w[2:3] * u
        d = d_ref[...]
        db_ref[...] = (d * y).astype(BF16)
        dy = d * b_ref[...]
        du = w[2:3] * dy + w[1:2] * _shift_up(dy, 1, row) + w[0:1] * _shift_up(dy, 2, row)
        dw_ref[0:1, :] = jnp.sum(dy * u2, axis=0, keepdims=True)
        dw_ref[1:2, :] = jnp.sum(dy * u1, axis=0, keepdims=True)
        dw_ref[2:3, :] = jnp.sum(dy * u, axis=0, keepdims=True)
        dc_ref[...] = (du * xb).astype(BF16)
        dx_ref[...] = (du * cg).astype(BF16)

    blk = pl.BlockSpec((L, LANES), lambda j: (0, j))
    return pl.pallas_call(
        body, name="conv_bwd", grid=(H // LANES,),
        in_specs=_conv_specs(L, H) + [pl.BlockSpec((3, LANES), lambda j: (0, j)), blk, ANY],
        out_specs=[blk, blk, blk, pl.BlockSpec((3, LANES), lambda j: (0, j))],
        out_shape=[_sds((L, H), BF16)] * 3 + [_sds((3, H), F32)],
        compiler_params=_params(("parallel",), 56),
    )(proj, proj, proj, conv_w, dcb, after)


def _gate_specs(tm, H):
    return [pl.BlockSpec((tm, H), lambda i, k=k: (i, k)) for k in (7, 8, 9, 10)]


def _fwd_mix(og, cb, proj, x, wat, wbt, wout, g_ffn, H, after):
    L, D = x.shape
    tm = min(L, 512)

    def body(o_ref, cb_ref, ga0, ga1, gb0, gb1, x_ref, wa_ref, wb_ref, wo_ref, g_ref, after_ref,
             ya_ref, yb_ref, m_ref, x1_ref, h2_ref):
        ya, yb = _nt(o_ref[...], wa_ref[...]), _nt(cb_ref[...], wb_ref[...])
        ya_ref[...] = ya.astype(BF16)
        yb_ref[...] = yb.astype(BF16)
        for k, (gar, gbr) in enumerate(((ga0, gb0), (ga1, gb1))):
            cs = slice(k * H, (k + 1) * H)
            m_ref[:, cs] = (_sigmoid(gar[...]) * ya[:, cs] + _sigmoid(gbr[...]) * yb[:, cs]).astype(BF16)
        x1 = x_ref[...] + _nn(m_ref[...], wo_ref[...])
        x1_ref[...] = x1
        _, xh = _rms_stats(x1)
        h2_ref[...] = (xh * g_ref[...]).astype(BF16)

    row = lambda w: pl.BlockSpec((tm, w), lambda i: (i, 0))
    full = lambda a: pl.BlockSpec(a.shape, lambda i: (0,) * a.ndim)
    return pl.pallas_call(
        body, name="fwd_mix", grid=(L // tm,),
        in_specs=[row(H), row(H)] + _gate_specs(tm, H) + [row(D), full(wat), full(wbt), full(wout),
                                                           full(g_ffn), ANY],
        out_specs=[row(D)] * 5,
        out_shape=[_sds((L, D), BF16)] * 3 + [_sds((L, D), F32), _sds((L, D), BF16)],
        compiler_params=_params(("parallel",), 56),
    )(og, cb, proj, proj, proj, proj, x, wat, wbt, wout, g_ffn, after)


def _bwd_mix(dx1b, proj, ya, yb, wat, wbt, wout, H, after):
    L, D = dx1b.shape
    tm = min(L, 512)

    def body(dx_ref, ga0, ga1, gb0, gb1, ya_ref, yb_ref, wa_ref, wb_ref, wo_ref, after_ref,
             dya_ref, dyb_ref, dga_ref, dgb_ref, do_ref, dcb_ref):
        dm = _nt(dx_ref[...], wo_ref[...])
        for k, (gar, gbr) in enumerate(((ga0, gb0), (ga1, gb1))):
            cs = slice(k * H, (k + 1) * H)
            sa, sb = _sigmoid(gar[...]), _sigmoid(gbr[...])
            dmk = dm[:, cs]
            dga_ref[:, cs] = (dmk * ya_ref[:, cs].astype(F32) * sa * (1.0 - sa)).astype(BF16)
            dgb_ref[:, cs] = (dmk * yb_ref[:, cs].astype(F32) * sb * (1.0 - sb)).astype(BF16)
            dya_ref[:, cs] = (dmk * sa).astype(BF16)
            dyb_ref[:, cs] = (dmk * sb).astype(BF16)
        do_ref[...] = _nn(dya_ref[...], wa_ref[...])
        dcb_ref[...] = _nn(dyb_ref[...], wb_ref[...])

    row = lambda w: pl.BlockSpec((tm, w), lambda i: (i, 0))
    full = lambda a: pl.BlockSpec(a.shape, lambda i: (0,) * a.ndim)
    return pl.pallas_call(
        body, name="bwd_mix", grid=(L // tm,),
        in_specs=[row(D)] + _gate_specs(tm, H) + [row(D), row(D), full(wat), full(wbt), full(wout), ANY],
        out_specs=[row(D)] * 4 + [row(H)] * 2,
        out_shape=[_sds((L, D), BF16)] * 4 + [_sds((L, H), F32)] * 2,
        compiler_params=_params(("parallel",), 56),
    )(dx1b, proj, proj, proj, proj, ya, yb, wat, wbt, wout, after)


def _fwd_ffn_up(h2, wgt, wut):
    L, D = h2.shape
    F = wgt.shape[0]
    tn = F // 2
    tm = min(L, 512)

    def body(h_ref, wg_ref, wu_ref, a_ref, b_ref, s_ref):
        h = h_ref[...]
        a, b = _nt(h, wg_ref[...]), _nt(h, wu_ref[...])
        a_ref[...] = a.astype(BF16)
        b_ref[...] = b.astype(BF16)
        s_ref[...] = (a * _sigmoid(a) * b).astype(BF16)

    wspec = pl.BlockSpec((tn, D), lambda j, i: (j, 0))
    ospec = pl.BlockSpec((tm, tn), lambda j, i: (i, j))
    return pl.pallas_call(
        body, name="fwd_ffn_up", grid=(2, L // tm),
        in_specs=[pl.BlockSpec((tm, D), lambda j, i: (i, 0)), wspec, wspec],
        out_specs=[ospec] * 3,
        out_shape=[_sds((L, F), BF16)] * 3,
        compiler_params=_params(("parallel", "parallel"), 48),
    )(h2, wgt, wut)


def _fwd_down_loss(s, wd, x1, target, g_final):
    L, D = x1.shape
    F = wd.shape[0]
    tm = min(L, 256)

    def body(s_ref, wd_ref, x1_ref, t_ref, g_ref, dx_ref, dxb_ref, red_ref):
        @pl.when(pl.program_id(0) == 0)
        def _():
            red_ref[...] = jnp.zeros_like(red_ref)

        g = g_ref[...]
        r, xh = _rms_stats(x1_ref[...] + _nn(s_ref[...], wd_ref[...]))
        e = xh * g - t_ref[...]
        dy = e * (1.0 / D)
        dx = _rms_bwd(dy * g, xh, r)
        dx_ref[...] = dx
        dxb_ref[...] = dx.astype(BF16)
        red_ref[0:1, :] += jnp.sum(dy * xh, axis=0, keepdims=True)
        red_ref[1:2, :] += jnp.broadcast_to(0.5 * jnp.sum(e * e) * (1.0 / D), (1, D))

    row = pl.BlockSpec((tm, D), lambda i: (i, 0))
    return pl.pallas_call(
        body, name="fwd_down_loss", grid=(L // tm,),
        in_specs=[pl.BlockSpec((tm, F), lambda i: (i, 0)), pl.BlockSpec((F, D), lambda i: (0, 0)),
                  row, row, pl.BlockSpec((1, D), lambda i: (0, 0))],
        out_specs=[row, row, pl.BlockSpec((8, D), lambda i: (0, 0))],
        out_shape=[_sds((L, D), F32), _sds((L, D), BF16), _sds((8, D), F32)],
        compiler_params=_params(("arbitrary",), 48),
    )(s, wd, x1, target, g_final)


def _bwd_down(dx2b, wd, a, b):
    L, D = dx2b.shape
    F = wd.shape[0]
    tn = F // 2
    tm = min(L, 512)

    def body(dx_ref, wd_ref, a_ref, b_ref, da_ref, db_ref):
        ds = _nt(dx_ref[...], wd_ref[...])
        a, b = a_ref[...].astype(F32), b_ref[...].astype(F32)
        sg = _sigmoid(a)
        da_ref[...] = (ds * b * sg * (1.0 + a * (1.0 - sg))).astype(BF16)
        db_ref[...] = (ds * a * sg).astype(BF16)

    ospec = pl.BlockSpec((tm, tn), lambda j, i: (i, j))
    return pl.pallas_call(
        body, name="bwd_down", grid=(2, L // tm),
        in_specs=[pl.BlockSpec((tm, D), lambda j, i: (i, 0)),
                  pl.BlockSpec((tn, D), lambda j, i: (j, 0)), ospec, ospec],
        out_specs=[ospec] * 2,
        out_shape=[_sds((L, F), BF16)] * 2,
        compiler_params=_params(("parallel", "parallel"), 48),
    )(dx2b, wd, a, b)


def _bwd_ffn_dh(da, db, wgt, wut, x1, dx2, g_ffn, after):
    L, D = x1.shape
    F = wgt.shape[0]
    tm = min(L, 256)

    def body(da_ref, db_ref, wg_ref, wu_ref, x1_ref, dx2_ref, g_ref, after_ref, dx_ref, dxb_ref, red_ref):
        @pl.when(pl.program_id(0) == 0)
        def _():
            red_ref[...] = jnp.zeros_like(red_ref)

        dh = _nn(da_ref[...], wg_ref[...]) + _nn(db_ref[...], wu_ref[...])
        r, xh = _rms_stats(x1_ref[...])
        red_ref[0:1, :] += jnp.sum(dh * xh, axis=0, keepdims=True)
        dx = dx2_ref[...] + _rms_bwd(dh * g_ref[...], xh, r)
        dx_ref[...] = dx
        dxb_ref[...] = dx.astype(BF16)

    row = pl.BlockSpec((tm, D), lambda i: (i, 0))
    aspec = pl.BlockSpec((tm, F), lambda i: (i, 0))
    wspec = pl.BlockSpec((F, D), lambda i: (0, 0))
    return pl.pallas_call(
        body, name="bwd_ffn_dh", grid=(L // tm,),
        in_specs=[aspec, aspec, wspec, wspec, row, row, pl.BlockSpec((1, D), lambda i: (0, 0)), ANY],
        out_specs=[row, row, pl.BlockSpec((8, D), lambda i: (0, 0))],
        out_shape=[_sds((L, D), F32), _sds((L, D), BF16), _sds((8, D), F32)],
        compiler_params=_params(("arbitrary",), 56),
    )(da, db, wgt, wut, x1, dx2, g_ffn, after)


def _bwd_in(dproj, w_int, x, dx1, g_mix, after):
    L, D = x.shape
    N = w_int.shape[0]
    tm = min(L, 256)

    def body(dp_ref, w_ref, x_ref, dx1_ref, g_ref, after_ref, dx_ref, red_ref):
        @pl.when(pl.program_id(0) == 0)
        def _():
            red_ref[...] = jnp.zeros_like(red_ref)

        dh = _nn(dp_ref[...], w_ref[...])
        r, xh = _rms_stats(x_ref[...])
        red_ref[0:1, :] += jnp.sum(dh * xh, axis=0, keepdims=True)
        dx_ref[...] = dx1_ref[...] + _rms_bwd(dh * g_ref[...], xh, r)

    row = pl.BlockSpec((tm, D), lambda i: (i, 0))
    return pl.pallas_call(
        body, name="bwd_in", grid=(L // tm,),
        in_specs=[pl.BlockSpec((tm, N), lambda i: (i, 0)), pl.BlockSpec((N, D), lambda i: (0, 0)),
                  row, row, pl.BlockSpec((1, D), lambda i: (0, 0)), ANY],
        out_specs=[row, pl.BlockSpec((8, D), lambda i: (0, 0))],
        out_shape=[_sds((L, D), F32), _sds((8, D), F32)],
        compiler_params=_params(("arbitrary",), 56),
    )(dproj, w_int, x, dx1, g_mix, after)


def _mm_tn(name, a, b, a_spec, b_spec, o_block, n_out, n_k):
    def body(a_ref, b_ref, o_ref):
        part = _tn(a_ref[...], b_ref[...])

        @pl.when(pl.program_id(1) == 0)
        def _():
            o_ref[...] = part

        @pl.when(pl.program_id(1) > 0)
        def _():
            o_ref[...] += part

    return pl.pallas_call(
        body, name=name, grid=(n_out, n_k),
        in_specs=[a_spec, b_spec],
        out_specs=pl.BlockSpec((None,) + o_block, lambda j, k: (j, 0, 0)),
        out_shape=_sds((n_out,) + o_block, F32),
        compiler_params=_params(("parallel", "arbitrary"), 56),
    )(a, b)


TK_TOKENS = 2048


def _dw_cols(name, a, b, n_cols):
    L, M = a.shape
    tk = min(L, TK_TOKENS)
    return _mm_tn(name, a, b, pl.BlockSpec((tk, M), lambda j, k: (k, 0)),
                  pl.BlockSpec((tk, n_cols), lambda j, k: (k, j)), (M, n_cols), N_CHIPS, L // tk)


def _dw_rows(name, a, b):
    L, M = a.shape
    N = b.shape[1]
    tk = min(L, TK_TOKENS)
    return _mm_tn(name, a, b, pl.BlockSpec((tk, M // N_CHIPS), lambda j, k: (k, j)),
                  pl.BlockSpec((tk, N), lambda j, k: (k, 0)), (M // N_CHIPS, N), N_CHIPS, L // tk)


def _dw_rows2(name, a, b):
    L, M = a.shape
    N = b.shape[1]
    tk = min(L, TK_TOKENS)
    return _mm_tn(name, a, b, pl.BlockSpec((tk, M // 2), lambda j, k: (k, j)),
                  pl.BlockSpec((tk, N), lambda j, k: (k, 0)), (M // 2, N), 2, L // tk)


def _place():
    x, y, c = lax.axis_index("x"), lax.axis_index("y"), lax.axis_index("c")
    chips = [(1 - x, y), (x, 1 - y), (1 - x, 1 - y)]
    return x, y, c, 2 * x + y, chips


def _remote(src, dst, send_sem, recv_sem, device):
    return pltpu.make_async_remote_copy(src_ref=src, dst_ref=dst, send_sem=send_sem,
                                        recv_sem=recv_sem, device_id=device, device_id_type=MESH)


def _half(ref, lead, c, r2):
    return ref.at[lead, pl.ds(pl.multiple_of(c * r2, 16), r2), :]


def _cast_place(name, w, chip_idx):
    r, cols = w.shape
    tr = r // 2

    def body(k_ref, w_ref, o_ref):
        o_ref[...] = w_ref[...].astype(BF16)

    return pl.pallas_call(
        body, name=name,
        grid_spec=pltpu.PrefetchScalarGridSpec(
            num_scalar_prefetch=1, grid=(2,),
            in_specs=[pl.BlockSpec((tr, cols), lambda i, k_ref: (i, 0))],
            out_specs=pl.BlockSpec((None, tr, cols), lambda i, k_ref: (k_ref[0], i, 0))),
        out_shape=_sds((N_CHIPS, r, cols), BF16),
        compiler_params=_params(("parallel",), 48),
    )(chip_idx, w)


def _cast_place_t(name, w, chip_idx):
    r, cols = w.shape

    def body(k_ref, w_ref, o_ref):
        o_ref[...] = w_ref[...].T.astype(BF16)

    return pl.pallas_call(
        body, name=name,
        grid_spec=pltpu.PrefetchScalarGridSpec(
            num_scalar_prefetch=1, grid=(cols // LANES,),
            in_specs=[pl.BlockSpec((r, LANES), lambda i, k_ref: (0, i))],
            out_specs=pl.BlockSpec((None, LANES, r), lambda i, k_ref: (k_ref[0], i, 0))),
        out_shape=_sds((N_CHIPS, cols, r), BF16),
        compiler_params=_params(("parallel",), 48),
    )(chip_idx, w)


def _in_hbm(arrays):
    return [pltpu.with_memory_space_constraint(a, pltpu.HBM) for a in arrays]


def _gather_copies(bufs, whole, send_sems, recv_sems):
    x, y, c, k, chips = _place()
    pairs = []
    for w, buf in enumerate(bufs):
        for j, (cx, cy) in enumerate(chips):
            if w in whole:
                mine, theirs = buf.at[k], buf.at[2 * cx + cy]
            else:
                r2 = buf.shape[1] // 2
                mine, theirs = _half(buf, k, c, r2), _half(buf, 2 * cx + cy, c, r2)
            sems = (send_sems.at[w * 3 + j], recv_sems.at[w * 3 + j])
            pairs.append((_remote(mine, mine, *sems, (cx, cy, c)), _remote(theirs, theirs, *sems, (x, y, c))))
    return pairs


def _gather_start(groups):
    flat = [b for bufs, _ in groups for b in bufs]
    nb, ng = len(flat), len(groups)

    def body(*refs):
        ins, sems, token = refs[:nb], refs[nb:nb + 2 * ng], refs[-1]
        pos = 0
        for g, (bufs, whole) in enumerate(groups):
            for send, _ in _gather_copies(ins[pos:pos + len(bufs)], whole, sems[2 * g], sems[2 * g + 1]):
                send.start()
            pos += len(bufs)
        token[...] = jnp.zeros_like(token)

    sem_shapes = []
    for bufs, _ in groups:
        sem_shapes += [pltpu.SemaphoreType.DMA((3 * len(bufs),))] * 2
    out = pl.pallas_call(
        body, name="gather_start",
        in_specs=[HBM] * nb, out_specs=tuple([SEM] * (2 * ng) + [HBM] * nb + [VMEM]),
        out_shape=tuple(sem_shapes + [pltpu.HBM(b.shape, b.dtype) for b in flat] + [_sds((8, LANES), F32)]),
        input_output_aliases={i: 2 * ng + i for i in range(nb)},
        compiler_params=pltpu.CompilerParams(has_side_effects=EFFECT),
    )(*_in_hbm(flat))
    sems, thru, pos = [], [], 2 * ng
    for g, (bufs, _) in enumerate(groups):
        sems.append((out[2 * g], out[2 * g + 1]))
        thru.append(list(out[pos:pos + len(bufs)]))
        pos += len(bufs)
    return sems, thru, out[-1]


def _gather_wait(name, bufs, whole, sems, after):
    nb = len(bufs)

    def body(*refs):
        ins, send_sems, recv_sems = refs[:nb], refs[nb], refs[nb + 1]
        for send, arrival in _gather_copies(ins, whole, send_sems, recv_sems):
            send.wait_send()
            arrival.wait_recv()

    return pl.pallas_call(
        body, name=name,
        in_specs=[HBM] * nb + [SEM, SEM, ANY], out_specs=[HBM] * nb,
        out_shape=[pltpu.HBM(b.shape, b.dtype) for b in bufs],
        input_output_aliases={i: i for i in range(nb)},
        compiler_params=pltpu.CompilerParams(has_side_effects=EFFECT),
    )(*bufs, sems[0], sems[1], after)


def _gather_forward(name, bufs):
    n = len(bufs)

    def body(*refs):
        outs = refs[n:2 * n]
        send_sems, recv_sems = refs[2 * n:]
        x, y, c, _, chips = _place()
        sends = []
        for w in range(n):
            r2 = outs[w].shape[1] // 2
            for j, (cx, cy) in enumerate(chips):
                landed = _half(outs[w], 2 * cx + cy, c, r2)
                sends.append(_remote(landed, landed, send_sems.at[w * 3 + j], recv_sems.at[w * 3 + j],
                                     (x, y, 1 - c)))
        for cp in sends:
            cp.start()
        for w in range(n):
            r2 = outs[w].shape[1] // 2
            for j, (cx, cy) in enumerate(chips):
                got = _half(outs[w], 2 * cx + cy, 1 - c, r2)
                _remote(got, got, send_sems.at[w * 3 + j], recv_sems.at[w * 3 + j], (x, y, c)).wait_recv()
        for cp in sends:
            cp.wait_send()

    return pl.pallas_call(
        body, name=name,
        in_specs=[ANY] * n, out_specs=[ANY] * n,
        out_shape=[_sds(b.shape, b.dtype) for b in bufs],
        input_output_aliases={i: i for i in range(n)},
        scratch_shapes=[pltpu.SemaphoreType.DMA((n * 3,)), pltpu.SemaphoreType.DMA((n * 3,))],
    )(*bufs)


def _rs_sibling(name, grads):
    n = len(grads)

    def body(*refs):
        ins, outs = refs[:n], refs[n:2 * n]
        send_sems, recv_sems = refs[2 * n:]
        x, y, c, _, _ = _place()
        copies = []
        for w in range(n):
            r2 = ins[w].shape[1] // 2
            copies.append(_remote(_half(ins[w], slice(None), 1 - c, r2), outs[w],
                                  send_sems.at[w], recv_sems.at[w], (x, y, 1 - c)))
        for cp in copies:
            cp.start()
        for cp in copies:
            cp.wait()

    return pl.pallas_call(
        body, name=name,
        in_specs=[ANY] * n, out_specs=[ANY] * n,
        out_shape=[_sds((N_CHIPS, g.shape[1] // 2, g.shape[2]), F32) for g in grads],
        scratch_shapes=[pltpu.SemaphoreType.DMA((n,)), pltpu.SemaphoreType.DMA((n,))],
    )(*grads)


def _rs_add(name, grad3, from_sibling, c_idx):
    _, r2, cols = from_sibling.shape

    def body(c_ref, g_ref, s_ref, o_ref):
        o_ref[...] = (g_ref[...] + s_ref[...]).astype(BF16)

    return pl.pallas_call(
        body, name=name,
        grid_spec=pltpu.PrefetchScalarGridSpec(
            num_scalar_prefetch=1, grid=(N_CHIPS,),
            in_specs=[pl.BlockSpec((None, r2, cols), lambda k, c_ref: (k, c_ref[0], 0)),
                      pl.BlockSpec((None, r2, cols), lambda k, c_ref: (k, 0, 0))],
            out_specs=pl.BlockSpec((None, r2, cols), lambda k, c_ref: (k, 0, 0))),
        out_shape=_sds(from_sibling.shape, BF16),
        compiler_params=_params(("parallel",), 48),
    )(c_idx, grad3, from_sibling)


def _split_start(name, arrays, n_sems, pairs_fn):
    n = len(arrays)

    def body(*refs):
        for send, _ in pairs_fn(refs[:n], refs[n], refs[n + 1]):
            send.start()
        refs[-1][...] = jnp.zeros_like(refs[-1])

    out = pl.pallas_call(
        body, name=name,
        in_specs=[HBM] * n, out_specs=tuple([SEM, SEM] + [HBM] * n + [VMEM]),
        out_shape=tuple([pltpu.SemaphoreType.DMA((n_sems,))] * 2 + [pltpu.HBM(a.shape, a.dtype) for a in arrays]
                        + [_sds((8, LANES), F32)]),
        input_output_aliases={i: 2 + i for i in range(n)},
        compiler_params=pltpu.CompilerParams(has_side_effects=EFFECT),
    )(*_in_hbm(arrays))
    return (out[0], out[1]), list(out[2:2 + n]), out[-1]


def _split_wait(name, sems, arrays, pairs_fn, after):
    n = len(arrays)

    def body(*refs):
        for send, arrival in pairs_fn(refs[:n], refs[n], refs[n + 1]):
            send.wait_send()
            arrival.wait_recv()

    return list(pl.pallas_call(
        body, name=name,
        in_specs=[HBM] * n + [SEM, SEM, ANY], out_specs=[HBM] * n,
        out_shape=[pltpu.HBM(a.shape, a.dtype) for a in arrays],
        input_output_aliases={i: i for i in range(n)},
        compiler_params=pltpu.CompilerParams(has_side_effects=EFFECT),
    )(*arrays, sems[0], sems[1], after))


def _forward_pairs(bufs, send_sems, recv_sems):
    x, y, c, _, chips = _place()
    pairs = []
    for w, buf in enumerate(bufs):
        r2 = buf.shape[1] // 2
        for j, (cx, cy) in enumerate(chips):
            landed, theirs = _half(buf, 2 * cx + cy, c, r2), _half(buf, 2 * cx + cy, 1 - c, r2)
            sems = (send_sems.at[w * 3 + j], recv_sems.at[w * 3 + j])
            pairs.append((_remote(landed, landed, *sems, (x, y, 1 - c)), _remote(theirs, theirs, *sems, (x, y, c))))
    return pairs


def _sibling_pairs(arrays, send_sems, recv_sems):
    x, y, c, _, _ = _place()
    n = len(arrays) // 2
    pairs = []
    for w in range(n):
        r2 = arrays[w].shape[1] // 2
        cp = _remote(_half(arrays[w], slice(None), 1 - c, r2), arrays[n + w], send_sems.at[w], recv_sems.at[w],
                     (x, y, 1 - c))
        pairs.append((cp, cp))
    return pairs


def _ici_pairs(arrays, send_sems, recv_sems):
    x, y, c, _, chips = _place()
    n = len(arrays) // 2
    pairs = []
    for w in range(n):
        for j, (cx, cy) in enumerate(chips):
            cp = _remote(arrays[w].at[2 * cx + cy], arrays[n + w].at[j],
                         send_sems.at[w * 3 + j], recv_sems.at[w * 3 + j], (cx, cy, c))
            pairs.append((cp, cp))
    return pairs


def _rs_sum(name, partials, received, place_idx):
    _, r2, cols = partials.shape
    nb = 2
    tr = r2 // nb

    def body(idx_ref, p_ref, r_ref, o_ref):
        o_ref[...] = ((p_ref[...].astype(F32) + r_ref[0].astype(F32))
                      + (r_ref[1].astype(F32) + r_ref[2].astype(F32)))

    return pl.pallas_call(
        body, name=name,
        grid_spec=pltpu.PrefetchScalarGridSpec(
            num_scalar_prefetch=1, grid=(nb,),
            in_specs=[pl.BlockSpec((None, tr, cols), lambda i, idx: (idx[0], i, 0)),
                      pl.BlockSpec((3, tr, cols), lambda i, idx: (0, i, 0))],
            out_specs=pl.BlockSpec((tr, cols), lambda i, idx: (idx[1] * nb + i, 0))),
        out_shape=_sds((2 * r2, cols), F32),
        compiler_params=_params(("parallel",), 48),
    )(place_idx, partials, received)


def _rs_share(name, shards):
    n = len(shards)

    def body(*refs):
        outs = refs[n:2 * n]
        send_sems, recv_sems = refs[2 * n:]
        x, y, c, _, _ = _place()
        sends = []
        for w in range(n):
            r2 = outs[w].shape[0] // 2
            mine = outs[w].at[pl.ds(pl.multiple_of(c * r2, 8), r2), :]
            sends.append(_remote(mine, mine, send_sems.at[w], recv_sems.at[w], (x, y, 1 - c)))
        for cp in sends:
            cp.start()
        for w in range(n):
            r2 = outs[w].shape[0] // 2
            theirs = outs[w].at[pl.ds(pl.multiple_of((1 - c) * r2, 8), r2), :]
            _remote(theirs, theirs, send_sems.at[w], recv_sems.at[w], (x, y, c)).wait_recv()
        for cp in sends:
            cp.wait_send()

    return pl.pallas_call(
        body, name=name,
        in_specs=[ANY] * n, out_specs=[ANY] * n,
        out_shape=[_sds(s.shape, F32) for s in shards],
        input_output_aliases={i: i for i in range(n)},
        scratch_shapes=[pltpu.SemaphoreType.DMA((n,)), pltpu.SemaphoreType.DMA((n,))],
    )(*shards)


def _small_allreduce(red_mix, red_ffn, red_final, red_hg, g_conv):
    rows = N_SMALL_ROWS
    D = red_mix.shape[1]
    H = red_hg.shape[1]

    def body(mix_ref, ffn_ref, fin_ref, hg_ref, cv_ref, sum_ref, all_ref, in_ref, send_sems, recv_sems):
        in_ref[...] = jnp.zeros_like(in_ref)
        in_ref[0:1, :] = mix_ref[0:1, :]
        in_ref[1:2, :] = ffn_ref[0:1, :]
        in_ref[2:3, :] = fin_ref[0:1, :]
        gam = hg_ref[1:2, 0:HEAD_DIM]
        for h in range(1, H // HEAD_DIM):
            gam = gam + hg_ref[1:2, h * HEAD_DIM:(h + 1) * HEAD_DIM]
        in_ref[3:4, 0:HEAD_DIM] = gam
        in_ref[3:4, HEAD_DIM:2 * HEAD_DIM] = fin_ref[1:2, 0:HEAD_DIM]
        in_ref[4:5, 0:H] = hg_ref[0:1, :]
        in_ref[6:9, 0:H] = cv_ref[...]
        x, y, c, _, _ = _place()
        me = 4 * x + 2 * y + c
        all_ref[me] = in_ref[...]
        copies = []
        for m in range(1, 8):
            mx, my, mc = (m >> 2) & 1, (m >> 1) & 1, m & 1
            px, py, pc = x ^ mx, y ^ my, c ^ mc
            copies.append((_remote(in_ref, all_ref.at[me], send_sems.at[m - 1], recv_sems.at[m - 1],
                                   (px, py, pc)), 4 * px + 2 * py + pc, m))
        for cp, _, _ in copies:
            cp.start()
        for _, peer, m in copies:
            _remote(in_ref, all_ref.at[peer], send_sems.at[m - 1], recv_sems.at[m - 1],
                    (x, y, c)).wait_recv()
        for cp, _, _ in copies:
            cp.wait_send()
        total = all_ref[0]
        for d in range(1, 8):
            total = total + all_ref[d]
        sum_ref[...] = total

    return pl.pallas_call(
        body, name="small_allreduce",
        in_specs=[VMEM] * 5, out_specs=[VMEM, VMEM],
        out_shape=[_sds((rows, D), F32), _sds((8, rows, D), F32)],
        scratch_shapes=[pltpu.VMEM((rows, D), F32), pltpu.SemaphoreType.DMA((7,)),
                        pltpu.SemaphoreType.DMA((7,))],
    )(red_mix, red_ffn, red_final, red_hg, g_conv)[0]


def _adamw_math(w, g, m, v):
    m = ADAM_B1 * m + (1.0 - ADAM_B1) * g
    v = ADAM_B2 * v + (1.0 - ADAM_B2) * jnp.square(g)
    m_hat = m / (1.0 - ADAM_B1 ** ADAM_STEP)
    v_hat = v / (1.0 - ADAM_B2 ** ADAM_STEP)
    delta = -ADAM_LR * (m_hat / (jnp.sqrt(v_hat) + ADAM_EPS) + ADAM_WD * w)
    return delta, m, v


def _adamw(name, g, w, m, v):
    r, cols = g.shape
    tr = r // 4

    def body(g_ref, w_ref, m_ref, v_ref, d_ref, mo_ref, vo_ref):
        d_ref[...], mo_ref[...], vo_ref[...] = _adamw_math(w_ref[...], g_ref[...], m_ref[...], v_ref[...])

    blk = pl.BlockSpec((tr, cols), lambda i: (i, 0))
    return pl.pallas_call(
        body, name=name, grid=(r // tr,),
        in_specs=[blk] * 4, out_specs=[blk] * 3, out_shape=[_sds((r, cols), F32)] * 3,
        compiler_params=_params(("parallel",), 48),
    )(g, w, m, v)


def _small_update(total, chip_idx, ws, ms, vs):
    n = len(ws)
    H = ws[1].shape[1]

    def body(idx_ref, tot_ref, *refs):
        w, m, v, outs = refs[:n], refs[n:2 * n], refs[2 * n:3 * n], refs[3 * n:]
        chip = idx_ref[0]
        p0 = _lower_bound(w[1][...])
        dl0 = p0 * (1.0 - p0) * tot_ref[4:5, 0:H]
        conv = jnp.zeros((3, LANES), F32)
        for k in range(N_CHIPS):
            conv = jnp.where(chip == k, tot_ref[6:9, k * LANES:(k + 1) * LANES], conv)
        grads = [tot_ref[0:1, :], None, tot_ref[3:4, 0:HEAD_DIM], conv, tot_ref[1:2, :], tot_ref[2:3, :]]
        for p in range(n):
            g_ref, d_ref, mo_ref, vo_ref = outs[4 * p:4 * p + 4]
            if p == 1:
                for row, g in ((slice(0, 1), dl0), (slice(1, 2), -dl0)):
                    g_ref[row, :] = g
                    d_ref[row, :], mo_ref[row, :], vo_ref[row, :] = _adamw_math(
                        w[p][row, :], g, m[p][row, :], v[p][row, :])
            else:
                g_ref[...] = grads[p]
                d_ref[...], mo_ref[...], vo_ref[...] = _adamw_math(w[p][...], grads[p], m[p][...], v[p][...])
        outs[4 * n][...] = tot_ref[3:4, HEAD_DIM:2 * HEAD_DIM]

    full = lambda a: pl.BlockSpec(a.shape, lambda i, idx: (0,) * a.ndim)
    out_shape = [_sds(w.shape, F32) for w in ws for _ in range(4)] + [_sds((1, LANES), F32)]
    return pl.pallas_call(
        body, name="small_update",
        grid_spec=pltpu.PrefetchScalarGridSpec(
            num_scalar_prefetch=1, grid=(1,),
            in_specs=[full(total)] + [full(a) for a in ws + ms + vs],
            out_specs=[full(s) for s in out_shape]),
        out_shape=out_shape,
    )(chip_idx, total, *ws, *ms, *vs)


def kernel(x, norm_mix_g, w_in, lower_bounds, hg_norm_g, conv_w, w_branch_a, w_branch_b, w_out, norm_ffn_g, w_ffn_gate, w_ffn_up, w_ffn_down, norm_final_g, loss_target, m_norm_mix_g, m_w_in, m_lower_bounds, m_hg_norm_g, m_conv_w, m_w_branch_a, m_w_branch_b, m_w_out, m_norm_ffn_g, m_w_ffn_gate, m_w_ffn_up, m_w_ffn_down, m_norm_final_g, v_norm_mix_g, v_w_in, v_lower_bounds, v_hg_norm_g, v_conv_w, v_w_branch_a, v_w_branch_b, v_w_out, v_norm_ffn_g, v_w_ffn_gate, v_w_ffn_up, v_w_ffn_down, v_norm_final_g):
    _, L, D = x.shape
    H = D // 2
    assert lower_bounds.shape == (2, H) and hg_norm_g.shape == (1, HEAD_DIM)
    assert conv_w.shape == (1, 3, LANES) and w_in.shape[2] * N_CHIPS == 11 * H
    x2d, target = x.reshape(L, D), loss_target.reshape(L, D)
    g_final = norm_final_g.reshape(1, D)
    chip = 2 * lax.axis_index("x") + lax.axis_index("y")
    core = lax.axis_index("c")

    tr = lambda w: jnp.transpose(w[0])
    big = [w_in[0], w_branch_a[0], w_branch_b[0], w_out[0], tr(w_ffn_gate), tr(w_ffn_up), w_ffn_down[0]]
    big_m = [m_w_in[0], m_w_branch_a[0], m_w_branch_b[0], m_w_out[0], tr(m_w_ffn_gate), tr(m_w_ffn_up),
             m_w_ffn_down[0]]
    big_v = [v_w_in[0], v_w_branch_a[0], v_w_branch_b[0], v_w_out[0], tr(v_w_ffn_gate), tr(v_w_ffn_up),
             v_w_ffn_down[0]]
    names = ["w_in", "w_branch_a", "w_branch_b", "w_out", "w_ffn_gate", "w_ffn_up", "w_ffn_down"]

    chip_idx = chip.reshape(1).astype(jnp.int32)
    placed = [(_cast_place_t if j < 3 else _cast_place)("place_" + nm, w, chip_idx)
              for j, (nm, w) in enumerate(zip(names, big))]
    conv_placed = lax.dynamic_update_slice(jnp.zeros((N_CHIPS, 3, LANES), F32), conv_w, (chip, 0, 0))
    sems, in_flight, token = _gather_start([([placed[0], conv_placed], {1}), (placed[1:4], set()),
                                            (placed[4:], set())])
    w_in_landed, conv_all = _gather_wait("gather_wait_in", in_flight[0], {1}, sems[0], token)
    (w_int3,) = _gather_forward("gather_fwd_in", [w_in_landed])
    w_int = w_int3.reshape(-1, D)
    conv_full = jnp.transpose(conv_all, (1, 0, 2)).reshape(3, H)

    h, proj = _fwd_proj(x2d, norm_mix_g, w_int3)
    og, o_pre, s_saved = _hgrn_fwd(proj, lower_bounds, hg_norm_g, H)
    landed = _gather_wait("gather_wait_mix", in_flight[1], set(), sems[1], og)
    fwd_sems, landed, token = _split_start("gather_fwd_mix_start", landed, 9, _forward_pairs)
    cb = _conv_fwd(proj, conv_full, H, token)
    wat3, wbt3, wout3 = _split_wait("gather_fwd_mix_wait", fwd_sems, landed, _forward_pairs, cb)
    wat, wbt, wout = wat3.reshape(D, H), wbt3.reshape(D, H), wout3.reshape(D, D)
    landed = _gather_wait("gather_wait_ffn", in_flight[2], set(), sems[2], cb)
    fwd_sems, landed, token = _split_start("gather_fwd_ffn_start", landed, 9, _forward_pairs)
    ya, yb, merged, x1, h2 = _fwd_mix(og, cb, proj, x2d, wat, wbt, wout, norm_ffn_g, H, token)
    wgt3, wut3, wd3 = _split_wait("gather_fwd_ffn_wait", fwd_sems, landed, _forward_pairs, h2)
    d_ff = N_CHIPS * wd3.shape[1]
    wgt, wut, wd = wgt3.reshape(d_ff, D), wut3.reshape(d_ff, D), wd3.reshape(d_ff, D)
    ffn_a, ffn_b, ffn_s = _fwd_ffn_up(h2, wgt, wut)
    dx2, dx2b, red_final = _fwd_down_loss(ffn_s, wd, x1, target, g_final)

    c_idx = core.reshape(1).astype(jnp.int32)
    place_idx = jnp.stack([chip, core]).astype(jnp.int32)

    def sibling_start(tag, grads):
        bufs = [lax.empty((N_CHIPS, g.shape[1] // 2, g.shape[2]), F32) for g in grads]
        return _split_start("rs_sibling_start_" + tag, list(grads) + bufs, len(grads), _sibling_pairs)

    def ici_start(tag, js, grads, from_sibling):
        partials = [_rs_add("rs_add_" + names[j], g, s, c_idx) for j, g, s in zip(js, grads, from_sibling)]
        landings = [lax.empty((3,) + p.shape[1:], BF16) for p in partials]
        return _split_start("rs_ici_start_" + tag, partials + landings, 3 * len(js), _ici_pairs)

    def ici_start_behind(tag, js, started, after):
        n = len(js)
        arrays = _split_wait("rs_sibling_wait_" + tag, started[0], started[1], _sibling_pairs, after)
        return ici_start(tag, js, arrays[:n], arrays[n:])

    def rs_end(tag, js, started, after):
        n = len(js)
        arrays = _split_wait("rs_ici_wait_" + tag, started[0], started[1], _ici_pairs, after)
        halves = [_rs_sum("rs_sum_" + names[j], p, r, place_idx)
                  for j, p, r in zip(js, arrays[:n], arrays[n:])]
        return list(_rs_share("rs_share_" + tag, halves))

    shards3 = lambda g: g.reshape(N_CHIPS, d_ff // N_CHIPS, D)
    da, db = _bwd_down(dx2b, wd, ffn_a, ffn_b)
    g_wd = shards3(_dw_rows2("dw_ffn_down", ffn_s, dx2b))
    g_wg = shards3(_dw_rows2("dw_ffn_gate", da, h2))
    g_wu = shards3(_dw_rows2("dw_ffn_up", db, h2))
    ffn_sibling = sibling_start("ffn", [g_wg, g_wu, g_wd])
    dx1, dx1b, red_ffn = _bwd_ffn_dh(da, db, wgt, wut, x1, dx2, norm_ffn_g, ffn_sibling[2])
    ffn_ici = ici_start_behind("ffn", [4, 5, 6], ffn_sibling, dx1b)
    dya, dyb, dga, dgb, d_o, d_cb = _bwd_mix(dx1b, proj, ya, yb, wat, wbt, wout, H, ffn_ici[2])
    g_wout = _dw_rows("dw_out", merged, dx1b)
    g_wa = _dw_cols("dw_branch_a", og, dya, D // N_CHIPS)
    g_wb = _dw_cols("dw_branch_b", cb, dyb, D // N_CHIPS)
    mix_sibling = sibling_start("mix", [g_wa, g_wb, g_wout])
    dq, df, dv, dg, red_hg = _hgrn_bwd(proj, lower_bounds, hg_norm_g, o_pre, d_o, s_saved, H, mix_sibling[2])
    mix_ici = ici_start_behind("mix", [1, 2, 3], mix_sibling, dq)
    dcg, dbg, dxb, g_conv = _conv_bwd(proj, conv_full, d_cb, H, mix_ici[2])
    dproj = jnp.concatenate([dq, df, dv, dg, dcg, dbg, dxb, dga, dgb], axis=1)
    g_win = _dw_cols("dw_in", h, dproj, w_int3.shape[1])
    adamw = lambda j, g: _adamw("adamw_" + names[j], g, big[j], big_m[j], big_v[j])
    in_sibling = sibling_start("in", [g_win])
    shard_grads = [None] + rs_end("mix", [1, 2, 3], mix_ici, in_sibling[2]) + rs_end(
        "ffn", [4, 5, 6], ffn_ici, in_sibling[2])
    big_out = [None] + [adamw(j, shard_grads[j]) for j in range(1, 7)]
    in_ici = ici_start_behind("in", [0], in_sibling, big_out[6][0])
    grad_x, red_mix = _bwd_in(dproj, w_int, x2d, dx1, norm_mix_g, in_ici[2])
    shard_grads[0] = rs_end("in", [0], in_ici, grad_x)[0]
    big_out[0] = adamw(0, shard_grads[0])

    total = _small_allreduce(red_mix, red_ffn, red_final, red_hg, g_conv)

    def smalls(mix, lb, hg, cw, ffn, fin):
        return [mix, lb, hg, cw[0], ffn, fin.reshape(1, D)]

    small_out = _small_update(
        total, chip_idx,
        smalls(norm_mix_g, lower_bounds, hg_norm_g, conv_w, norm_ffn_g, norm_final_g),
        smalls(m_norm_mix_g, m_lower_bounds, m_hg_norm_g, m_conv_w, m_norm_ffn_g, m_norm_final_g),
        smalls(v_norm_mix_g, v_lower_bounds, v_hg_norm_g, v_conv_w, v_norm_ffn_g, v_norm_final_g))

    def outputs(i):
        big_i = [shard_grads[j] if i == 0 else big_out[j][i - 1] for j in range(7)]
        mix, lb, hg, cw, ffn, fin = [small_out[4 * p + i] for p in range(6)]
        return [mix, big_i[0][None], lb, hg, cw[None], big_i[1][None], big_i[2][None], big_i[3][None], ffn,
                big_i[4].T[None], big_i[5].T[None], big_i[6][None], fin.reshape(D)]

    outs = [small_out[24][0, 0], grad_x.reshape(1, L, D)]
    for i in range(4):
        outs += outputs(i)
    return tuple(outs)
```

```python
import functools

import jax
import jax.numpy as jnp
from jax import lax
from jax.experimental import pallas as pl
from jax.experimental.pallas import tpu as pltpu

F32 = jnp.float32
BF16 = jnp.bfloat16
EPS = 1e-6
CHUNK = 32
HEAD_DIM = 128
LANES = 128
N_CHIPS = 4
N_SMALL_ROWS = 16

ADAM_LR = 0.001
ADAM_B1 = 0.9
ADAM_B2 = 0.999
ADAM_EPS = 1e-08
ADAM_WD = 0.01
ADAM_STEP = 10

MESH = pl.DeviceIdType.MESH
ANY = pl.BlockSpec(memory_space=pl.ANY)
VMEM = pl.BlockSpec(memory_space=pltpu.VMEM)
HBM = pl.BlockSpec(memory_space=pltpu.HBM)
SEM = pl.BlockSpec(memory_space=pltpu.SEMAPHORE)
EFFECT = pltpu.SideEffectType.DATAFLOW_SIDE_EFFECTING


def _sds(shape, dtype):
    return jax.ShapeDtypeStruct(shape, dtype)


def _pallas_call(body, pin=True, **kwargs):
    if not pin:
        return pl.pallas_call(body, **kwargs)
    in_hbm = lambda s: pltpu.HBM(s.shape, s.dtype) if isinstance(s, jax.ShapeDtypeStruct) else s
    kwargs["out_shape"] = jax.tree.map(in_hbm, kwargs["out_shape"])
    call = pl.pallas_call(body, **kwargs)

    def run(*args):
        return call(*[pltpu.with_memory_space_constraint(a, pltpu.HBM) if a.dtype in (F32, BF16) else a
                      for a in args])

    return run


def _params(semantics, vmem_mb):
    return pltpu.CompilerParams(dimension_semantics=semantics, vmem_limit_bytes=vmem_mb << 20)


def _nn(a, b):
    return lax.dot_general(a, b, (((1,), (0,)), ((), ())), preferred_element_type=F32)


def _nt(a, b):
    return lax.dot_general(a, b, (((1,), (1,)), ((), ())), preferred_element_type=F32)


def _tn(a, b):
    return lax.dot_general(a, b, (((0,), (0,)), ((), ())), preferred_element_type=F32)


def _sigmoid(x):
    return jax.nn.sigmoid(x)


def _rms_stats(x):
    r = lax.rsqrt(jnp.mean(x * x, axis=-1, keepdims=True) + EPS)
    return r, x * r


def _rms_bwd(dxh, xh, r):
    return r * (dxh - xh * jnp.mean(dxh * xh, axis=-1, keepdims=True))


def _fwd_proj(x, g_mix, w_int3):
    L, D = x.shape
    tn = w_int3.shape[1]
    tm = min(L, 1024)

    def body(x_ref, g_ref, w_ref, h_ref, p_ref):
        @pl.when(pl.program_id(1) == 0)
        def _():
            _, xh = _rms_stats(x_ref[...])
            h_ref[...] = (xh * g_ref[...]).astype(BF16)

        p_ref[...] = _nt(h_ref[...], w_ref[...])

    return _pallas_call(
        body, name="fwd_proj", grid=(L // tm, N_CHIPS),
        in_specs=[pl.BlockSpec((tm, D), lambda i, j: (i, 0)),
                  pl.BlockSpec((1, D), lambda i, j: (0, 0)),
                  pl.BlockSpec((None, tn, D), lambda i, j: (j, 0, 0))],
        out_specs=[pl.BlockSpec((tm, D), lambda i, j: (i, 0)),
                   pl.BlockSpec((tm, tn), lambda i, j: (i, j))],
        out_shape=[_sds((L, D), BF16), _sds((L, N_CHIPS * tn), F32)],
        compiler_params=_params(("parallel", "arbitrary"), 48),
    )(x, g_mix, w_int3)


def _lower_bound(lbp):
    l0, l1 = lbp[0:1, :], lbp[1:2, :]
    m = jnp.maximum(l0, l1)
    e0, e1 = jnp.exp(l0 - m), jnp.exp(l1 - m)
    return e0 / (e0 + e1)


def _seg_scan(x, r32, forward):
    n = x.shape[0]
    s = 1
    while s < CHUNK:
        if forward:
            x = x + jnp.where(r32 >= s, pltpu.roll(x, s, 0), 0.0)
        else:
            x = x + jnp.where(r32 < CHUNK - s, pltpu.roll(x, n - s, 0), 0.0)
        s *= 2
    return x


def _bcast_row(x, row):
    n, w = x.shape
    nc = n // CHUNK
    x3 = x.reshape(nc, CHUNK, w)
    return jnp.broadcast_to(x3[:, row:row + 1, :], (nc, CHUNK, w)).reshape(n, w)


def _hgrn_prep(q_raw, f_raw, lb):
    r32 = lax.broadcasted_iota(jnp.int32, f_raw.shape, 0) & (CHUNK - 1)
    sig = _sigmoid(f_raw)
    f = lb + (1.0 - lb) * sig
    b = _seg_scan(jnp.log(f), r32, True)
    a = _bcast_row(b, CHUNK // 2 - 1)
    bl = _bcast_row(b, CHUNK - 1)
    sq = _sigmoid(q_raw)
    q = q_raw * sq * (HEAD_DIM ** -0.5)
    return dict(r32=r32, sig=sig, f=f, k=1.0 - f, b=b, a=a, bl=bl, sq=sq, q=q)


def _chunk_masks(n):
    ri = lax.broadcasted_iota(jnp.int32, (n, n), 0)
    ci = lax.broadcasted_iota(jnp.int32, (n, n), 1)
    same = (ri // CHUNK) == (ci // CHUNK)
    return same & (ci <= ri), same & (ri <= ci)


def _hgrn_fwd(proj, lower_bounds, gamma, H):
    L = proj.shape[0]
    nh = H // HEAD_DIM
    TL = min(L, 256)
    nc = TL // CHUNK

    def body(q_ref, f_ref, v_ref, g_ref, lbp_ref, gam_ref, og_ref, o_ref, s_ref, st_ref):
        @pl.when(pl.program_id(0) == 0)
        def _():
            st_ref[...] = jnp.zeros_like(st_ref)

        lb = _lower_bound(lbp_ref[...])
        gam = gam_ref[...]
        mask, _ = _chunk_masks(TL)
        rowc = lax.broadcasted_iota(jnp.int32, (TL, HEAD_DIM), 0) // CHUNK
        for h in range(nh):
            hs = slice(h * HEAD_DIM, (h + 1) * HEAD_DIM)
            p = _hgrn_prep(q_ref[:, hs], f_ref[:, hs], lb[:, hs])
            v = v_ref[:, hs]
            vb = v.astype(BF16)
            vt = v.T.astype(BF16)
            q_hat = (p["q"] * jnp.exp(p["b"] - p["a"])).astype(BF16)
            k_hat = (p["k"] * jnp.exp(p["a"] - p["b"])).astype(BF16)
            q_in = (p["q"] * jnp.exp(p["b"])).astype(BF16)
            k_out = (p["k"] * jnp.exp(p["bl"] - p["b"])).astype(BF16)
            dec = jnp.exp(p["bl"])
            att = jnp.where(mask, _nt(q_hat, k_hat), 0.0).astype(BF16)
            o_intra = _nn(att, vb)
            st = st_ref[h]
            for c in range(nc):
                rs = slice(c * CHUNK, (c + 1) * CHUNK)
                stb = st.astype(BF16)
                s_ref[c, h] = stb
                o_ref[rs, hs] = o_intra[rs] + _nt(q_in[rs], stb)
                k_c = jnp.where(rowc == c, k_out, jnp.zeros_like(k_out))
                st = st * dec[c * CHUNK:c * CHUNK + 1, :] + _nn(vt, k_c)
            st_ref[h] = st
            o = o_ref[:, hs]
            _, xh = _rms_stats(o)
            gr = g_ref[:, hs]
            og_ref[:, hs] = (xh * gam * (gr * _sigmoid(gr))).astype(BF16)

    col = lambda k: pl.BlockSpec((TL, H), lambda i, k=k: (i, k))
    return _pallas_call(
        body, name="hgrn_fwd", grid=(L // TL,),
        in_specs=[col(0), col(1), col(2), col(3),
                  pl.BlockSpec(lower_bounds.shape, lambda i: (0, 0)),
                  pl.BlockSpec(gamma.shape, lambda i: (0, 0))],
        out_specs=[pl.BlockSpec((TL, H), lambda i: (i, 0)),
                   pl.BlockSpec((TL, H), lambda i: (i, 0)),
                   pl.BlockSpec((nc, nh, HEAD_DIM, HEAD_DIM), lambda i: (i, 0, 0, 0))],
        out_shape=[_sds((L, H), BF16), _sds((L, H), F32),
                   _sds((L // CHUNK, nh, HEAD_DIM, HEAD_DIM), BF16)],
        scratch_shapes=[pltpu.VMEM((nh, HEAD_DIM, HEAD_DIM), F32)],
        compiler_params=_params(("arbitrary",), 48),
    )(proj, proj, proj, proj, lower_bounds, gamma)


def _hgrn_bwd(proj, lower_bounds, gamma, o_pre, d_out, s_saved, H, after):
    L = proj.shape[0]
    nh = H // HEAD_DIM
    TL = min(L, 256)
    nc = TL // CHUNK
    nt = L // TL

    def body(q_ref, f_ref, v_ref, g_ref, lbp_ref, gam_ref, o_ref, d_ref, s_ref, after_ref,
             dq_ref, df_ref, dv_ref, dg_ref, red_ref, dst_ref, dsall_ref, tmp_ref):
        @pl.when(pl.program_id(0) == 0)
        def _():
            dst_ref[...] = jnp.zeros_like(dst_ref)
            red_ref[...] = jnp.zeros_like(red_ref)

        lb = _lower_bound(lbp_ref[...])
        gam = gam_ref[...]
        mask, mask_t = _chunk_masks(TL)
        rowc = lax.broadcasted_iota(jnp.int32, (TL, HEAD_DIM), 0) // CHUNK
        for h in range(nh):
            hs = slice(h * HEAD_DIM, (h + 1) * HEAD_DIM)
            qr, gr, lbh = q_ref[:, hs], g_ref[:, hs], lb[:, hs]
            p = _hgrn_prep(qr, f_ref[:, hs], lbh)
            vb = v_ref[:, hs].astype(BF16)
            eba, eab = jnp.exp(p["b"] - p["a"]), jnp.exp(p["a"] - p["b"])
            eb, elb = jnp.exp(p["b"]), jnp.exp(p["bl"] - p["b"])
            dec = jnp.exp(p["bl"])
            q_hat, k_hat = p["q"] * eba, p["k"] * eab
            q_in, k_out = p["q"] * eb, p["k"] * elb
            q_hat_b, k_hat_b = q_hat.astype(BF16), k_hat.astype(BF16)
            q_in_b, k_out_b = q_in.astype(BF16), k_out.astype(BF16)

            o, dout = o_ref[:, hs], d_ref[:, hs]
            sg = _sigmoid(gr)
            r, xh = _rms_stats(o)
            dg_ref[:, hs] = (dout * (xh * gam) * (sg * (1.0 + gr * (1.0 - sg)))).astype(BF16)
            dn = dout * (gr * sg)
            red_ref[1:2, hs] += jnp.sum(dn * xh, axis=0, keepdims=True)
            do = _rms_bwd(dn * gam, xh, r)
            dob = do.astype(BF16)
            dot_b = do.T.astype(BF16)

            att_t = jnp.where(mask_t, _nt(k_hat_b, q_hat_b), 0.0).astype(BF16)
            dv_intra = _nn(att_t, dob)
            datt = jnp.where(mask, _nt(dob, vb), 0.0).astype(BF16)
            dqh = _nn(datt, k_hat_b)
            datt_t = jnp.where(mask_t, _nt(vb, dob), 0.0).astype(BF16)
            dkh = _nn(datt_t, q_hat_b)

            dst = dst_ref[h]
            for c in reversed(range(nc)):
                dsall_ref[c] = dst
                q_c = jnp.where(rowc == c, q_in_b, jnp.zeros_like(q_in_b))
                dst = dst * dec[c * CHUNK:c * CHUNK + 1, :] + _nn(dot_b, q_c)
            dst_ref[h] = dst
            for c in range(nc):
                rs = slice(c * CHUNK, (c + 1) * CHUNK)
                ds_c = dsall_ref[c]
                dsb = ds_c.astype(BF16)
                st_prev = s_ref[c, h]
                tmp_ref[0, rs, :] = _nt(k_out_b[rs], dsb)
                tmp_ref[1, rs, :] = _nn(vb[rs], dsb)
                tmp_ref[2, rs, :] = _nn(dob[rs], st_prev)
                ddec = jnp.sum(ds_c * st_prev.astype(F32), axis=0, keepdims=True)
                tmp_ref[3, rs, :] = jnp.broadcast_to(ddec * dec[c * CHUNK:c * CHUNK + 1, :],
                                                     (CHUNK, HEAD_DIM))
            dko, dqi = tmp_ref[1], tmp_ref[2]
            dq = dqh * eba + dqi * eb
            dk = dkh * eab + dko * elb
            tko = dko * k_out
            db = dqh * q_hat - dkh * k_hat + dqi * q_in - tko
            dlog = (_seg_scan(db, p["r32"], False)
                    + _bcast_row(_seg_scan(tko, p["r32"], True), CHUNK - 1) + tmp_ref[3])
            df = dlog / p["f"] - dk
            sig = p["sig"]
            red_ref[0:1, hs] += jnp.sum(df * (1.0 - sig), axis=0, keepdims=True)
            df_ref[:, hs] = (df * (1.0 - lbh) * sig * (1.0 - sig)).astype(BF16)
            sq = p["sq"]
            dq_ref[:, hs] = (dq * (HEAD_DIM ** -0.5) * (sq * (1.0 + qr * (1.0 - sq)))).astype(BF16)
            dv_ref[:, hs] = (dv_intra + tmp_ref[0]).astype(BF16)

    col = lambda k: pl.BlockSpec((TL, H), lambda i, k=k: (nt - 1 - i, k))
    rev = pl.BlockSpec((TL, H), lambda i: (nt - 1 - i, 0))
    return _pallas_call(
        body, name="hgrn_bwd", grid=(nt,),
        in_specs=[col(0), col(1), col(2), col(3),
                  pl.BlockSpec(lower_bounds.shape, lambda i: (0, 0)),
                  pl.BlockSpec(gamma.shape, lambda i: (0, 0)),
                  rev, rev,
                  pl.BlockSpec((nc, nh, HEAD_DIM, HEAD_DIM), lambda i: (nt - 1 - i, 0, 0, 0)), ANY],
        out_specs=[rev, rev, rev, rev, pl.BlockSpec((8, H), lambda i: (0, 0))],
        out_shape=[_sds((L, H), BF16)] * 4 + [_sds((8, H), F32)],
        scratch_shapes=[pltpu.VMEM((nh, HEAD_DIM, HEAD_DIM), F32),
                        pltpu.VMEM((nc, HEAD_DIM, HEAD_DIM), F32),
                        pltpu.VMEM((4, TL, HEAD_DIM), F32)],
        compiler_params=_params(("arbitrary",), 48),
    )(proj, proj, proj, proj, lower_bounds, gamma, o_pre, d_out, s_saved, after)


def _shift_down(u, s, row):
    return jnp.where(row >= s, pltpu.roll(u, s, 0), 0.0)


def _shift_up(u, s, row):
    n = u.shape[0]
    return jnp.where(row < n - s, pltpu.roll(u, n - s, 0), 0.0)


def _conv_specs(L, H):
    per = H // LANES
    return [pl.BlockSpec((L, LANES), lambda j, o=o: (0, o * per + j)) for o in (4, 5, 6)]


def _conv_fwd(proj, conv_w, H, after):
    L = proj.shape[0]

    def body(c_ref, b_ref, x_ref, w_ref, after_ref, o_ref):
        row = lax.broadcasted_iota(jnp.int32, (L, LANES), 0)
        u = c_ref[...] * x_ref[...]
        w = w_ref[...]
        y = w[0:1] * _shift_down(u, 2, row) + w[1:2] * _shift_down(u, 1, row) + w[2:3] * u
        o_ref[...] = (b_ref[...] * y).astype(BF16)

    return _pallas_call(
        body, name="conv_fwd", grid=(H // LANES,),
        in_specs=_conv_specs(L, H) + [pl.BlockSpec((3, LANES), lambda j: (0, j)), ANY],
        out_specs=pl.BlockSpec((L, LANES), lambda j: (0, j)),
        out_shape=_sds((L, H), BF16),
        compiler_params=_params(("parallel",), 48),
    )(proj, proj, proj, conv_w, after)


def _conv_bwd(proj, conv_w, dcb, H, after):
    L = proj.shape[0]

    def body(c_ref, b_ref, x_ref, w_ref, d_ref, after_ref, dc_ref, db_ref, dx_ref, dw_ref):
        row = lax.broadcasted_iota(jnp.int32, (L, LANES), 0)
        cg, xb = c_ref[...], x_ref[...]
        u = cg * xb
        u1, u2 = _shift_down(u, 1, row), _shift_down(u, 2, row)
        w = w_ref[...]
        y = w[0:1] * u2 + w[1:2] * u1 + w[2:3] * u
        d = d_ref[...]
        db_ref[...] = (d * y).astype(BF16)
        dy = d * b_ref[...]
        du = w[2:3] * dy + w[1:2] * _shift_up(dy, 1, row) + w[0:1] * _shift_up(dy, 2, row)
        dw_ref[0:1, :] = jnp.sum(dy * u2, axis=0, keepdims=True)
        dw_ref[1:2, :] = jnp.sum(dy * u1, axis=0, keepdims=True)
        dw_ref[2:3, :] = jnp.sum(dy * u, axis=0, keepdims=True)
        dc_ref[...] = (du * xb).astype(BF16)
        dx_ref[...] = (du * cg).astype(BF16)

    blk = pl.BlockSpec((L, LANES), lambda j: (0, j))
    return _pallas_call(
        body, name="conv_bwd", grid=(H // LANES,),
        in_specs=_conv_specs(L, H) + [pl.BlockSpec((3, LANES), lambda j: (0, j)), blk, ANY],
        out_specs=[blk, blk, blk, pl.BlockSpec((3, LANES), lambda j: (0, j))],
        out_shape=[_sds((L, H), BF16)] * 3 + [_sds((3, H), F32)],
        compiler_params=_params(("parallel",), 56),
    )(proj, proj, proj, conv_w, dcb, after)


def _gate_specs(tm, H):
    return [pl.BlockSpec((tm, H), lambda i, k=k: (i, k)) for k in (7, 8, 9, 10)]


def _fwd_mix(og, cb, proj, x, wat, wbt, wout, g_ffn, H, after):
    L, D = x.shape
    tm = min(L, 512)

    def body(o_ref, cb_ref, ga0, ga1, gb0, gb1, x_ref, wa_ref, wb_ref, wo_ref, g_ref, after_ref,
             ya_ref, yb_ref, m_ref, x1_ref, h2_ref):
        ya, yb = _nt(o_ref[...], wa_ref[...]), _nt(cb_ref[...], wb_ref[...])
        ya_ref[...] = ya.astype(BF16)
        yb_ref[...] = yb.astype(BF16)
        for k, (gar, gbr) in enumerate(((ga0, gb0), (ga1, gb1))):
            cs = slice(k * H, (k + 1) * H)
            m_ref[:, cs] = (_sigmoid(gar[...]) * ya[:, cs] + _sigmoid(gbr[...]) * yb[:, cs]).astype(BF16)
        x1 = x_ref[...] + _nn(m_ref[...], wo_ref[...])
        x1_ref[...] = x1
        _, xh = _rms_stats(x1)
        h2_ref[...] = (xh * g_ref[...]).astype(BF16)

    row = lambda w: pl.BlockSpec((tm, w), lambda i: (i, 0))
    full = lambda a: pl.BlockSpec(a.shape, lambda i: (0,) * a.ndim)
    return _pallas_call(
        body, name="fwd_mix", grid=(L // tm,),
        in_specs=[row(H), row(H)] + _gate_specs(tm, H) + [row(D), full(wat), full(wbt), full(wout),
                                                           full(g_ffn), ANY],
        out_specs=[row(D)] * 5,
        out_shape=[_sds((L, D), BF16)] * 3 + [_sds((L, D), F32), _sds((L, D), BF16)],
        compiler_params=_params(("parallel",), 56),
    )(og, cb, proj, proj, proj, proj, x, wat, wbt, wout, g_ffn, after)


def _bwd_mix(dx1b, proj, ya, yb, wat, wbt, wout, H, after):
    L, D = dx1b.shape
    tm = min(L, 512)

    def body(dx_ref, ga0, ga1, gb0, gb1, ya_ref, yb_ref, wa_ref, wb_ref, wo_ref, after_ref,
             dya_ref, dyb_ref, dga_ref, dgb_ref, do_ref, dcb_ref):
        dm = _nt(dx_ref[...], wo_ref[...])
        for k, (gar, gbr) in enumerate(((ga0, gb0), (ga1, gb1))):
            cs = slice(k * H, (k + 1) * H)
            sa, sb = _sigmoid(gar[...]), _sigmoid(gbr[...])
            dmk = dm[:, cs]
            dga_ref[:, cs] = (dmk * ya_ref[:, cs].astype(F32) * sa * (1.0 - sa)).astype(BF16)
            dgb_ref[:, cs] = (dmk * yb_ref[:, cs].astype(F32) * sb * (1.0 - sb)).astype(BF16)
            dya_ref[:, cs] = (dmk * sa).astype(BF16)
            dyb_ref[:, cs] = (dmk * sb).astype(BF16)
        do_ref[...] = _nn(dya_ref[...], wa_ref[...])
        dcb_ref[...] = _nn(dyb_ref[...], wb_ref[...])

    row = lambda w: pl.BlockSpec((tm, w), lambda i: (i, 0))
    full = lambda a: pl.BlockSpec(a.shape, lambda i: (0,) * a.ndim)
    return _pallas_call(
        body, name="bwd_mix", grid=(L // tm,),
        in_specs=[row(D)] + _gate_specs(tm, H) + [row(D), row(D), full(wat), full(wbt), full(wout), ANY],
        out_specs=[row(D)] * 4 + [row(H)] * 2,
        out_shape=[_sds((L, D), BF16)] * 4 + [_sds((L, H), F32)] * 2,
        compiler_params=_params(("parallel",), 56),
    )(dx1b, proj, proj, proj, proj, ya, yb, wat, wbt, wout, after)


def _fwd_ffn_up(h2, wgt, wut):
    L, D = h2.shape
    F = wgt.shape[0]
    tn = F // 2
    tm = min(L, 512)

    def body(h_ref, wg_ref, wu_ref, a_ref, b_ref, s_ref):
        h = h_ref[...]
        a, b = _nt(h, wg_ref[...]), _nt(h, wu_ref[...])
        a_ref[...] = a.astype(BF16)
        b_ref[...] = b.astype(BF16)
        s_ref[...] = (a * _sigmoid(a) * b).astype(BF16)

    wspec = pl.BlockSpec((tn, D), lambda j, i: (j, 0))
    ospec = pl.BlockSpec((tm, tn), lambda j, i: (i, j))
    return _pallas_call(
        body, name="fwd_ffn_up", grid=(2, L // tm),
        in_specs=[pl.BlockSpec((tm, D), lambda j, i: (i, 0)), wspec, wspec],
        out_specs=[ospec] * 3,
        out_shape=[_sds((L, F), BF16)] * 3,
        compiler_params=_params(("parallel", "parallel"), 48),
    )(h2, wgt, wut)


def _fwd_down_loss(s, wd, x1, target, g_final):
    L, D = x1.shape
    F = wd.shape[0]
    tm = min(L, 256)

    def body(s_ref, wd_ref, x1_ref, t_ref, g_ref, dx_ref, dxb_ref, red_ref):
        @pl.when(pl.program_id(0) == 0)
        def _():
            red_ref[...] = jnp.zeros_like(red_ref)

        g = g_ref[...]
        r, xh = _rms_stats(x1_ref[...] + _nn(s_ref[...], wd_ref[...]))
        e = xh * g - t_ref[...]
        dy = e * (1.0 / D)
        dx = _rms_bwd(dy * g, xh, r)
        dx_ref[...] = dx
        dxb_ref[...] = dx.astype(BF16)
        red_ref[0:1, :] += jnp.sum(dy * xh, axis=0, keepdims=True)
        red_ref[1:2, :] += jnp.broadcast_to(0.5 * jnp.sum(e * e) * (1.0 / D), (1, D))

    row = pl.BlockSpec((tm, D), lambda i: (i, 0))
    return _pallas_call(
        body, name="fwd_down_loss", grid=(L // tm,),
        in_specs=[pl.BlockSpec((tm, F), lambda i: (i, 0)), pl.BlockSpec((F, D), lambda i: (0, 0)),
                  row, row, pl.BlockSpec((1, D), lambda i: (0, 0))],
        out_specs=[row, row, pl.BlockSpec((8, D), lambda i: (0, 0))],
        out_shape=[_sds((L, D), F32), _sds((L, D), BF16), _sds((8, D), F32)],
        compiler_params=_params(("arbitrary",), 48),
    )(s, wd, x1, target, g_final)


def _bwd_down(dx2b, wd, a, b):
    L, D = dx2b.shape
    F = wd.shape[0]
    tn = F // 2
    tm = min(L, 512)

    def body(dx_ref, wd_ref, a_ref, b_ref, da_ref, db_ref):
        ds = _nt(dx_ref[...], wd_ref[...])
        a, b = a_ref[...].astype(F32), b_ref[...].astype(F32)
        sg = _sigmoid(a)
        da_ref[...] = (ds * b * sg * (1.0 + a * (1.0 - sg))).astype(BF16)
        db_ref[...] = (ds * a * sg).astype(BF16)

    ospec = pl.BlockSpec((tm, tn), lambda j, i: (i, j))
    return _pallas_call(
        body, name="bwd_down", grid=(2, L // tm),
        in_specs=[pl.BlockSpec((tm, D), lambda j, i: (i, 0)),
                  pl.BlockSpec((tn, D), lambda j, i: (j, 0)), ospec, ospec],
        out_specs=[ospec] * 2,
        out_shape=[_sds((L, F), BF16)] * 2,
        compiler_params=_params(("parallel", "parallel"), 48),
    )(dx2b, wd, a, b)


def _bwd_ffn_dh(da, db, wgt, wut, x1, dx2, g_ffn, after):
    L, D = x1.shape
    F = wgt.shape[0]
    tm = min(L, 256)

    def body(da_ref, db_ref, wg_ref, wu_ref, x1_ref, dx2_ref, g_ref, after_ref, dx_ref, dxb_ref, red_ref):
        @pl.when(pl.program_id(0) == 0)
        def _():
            red_ref[...] = jnp.zeros_like(red_ref)

        dh = _nn(da_ref[...], wg_ref[...]) + _nn(db_ref[...], wu_ref[...])
        r, xh = _rms_stats(x1_ref[...])
        red_ref[0:1, :] += jnp.sum(dh * xh, axis=0, keepdims=True)
        dx = dx2_ref[...] + _rms_bwd(dh * g_ref[...], xh, r)
        dx_ref[...] = dx
        dxb_ref[...] = dx.astype(BF16)

    row = pl.BlockSpec((tm, D), lambda i: (i, 0))
    aspec = pl.BlockSpec((tm, F), lambda i: (i, 0))
    wspec = pl.BlockSpec((F, D), lambda i: (0, 0))
    return _pallas_call(
        body, name="bwd_ffn_dh", grid=(L // tm,),
        in_specs=[aspec, aspec, wspec, wspec, row, row, pl.BlockSpec((1, D), lambda i: (0, 0)), ANY],
        out_specs=[row, row, pl.BlockSpec((8, D), lambda i: (0, 0))],
        out_shape=[_sds((L, D), F32), _sds((L, D), BF16), _sds((8, D), F32)],
        compiler_params=_params(("arbitrary",), 56),
    )(da, db, wgt, wut, x1, dx2, g_ffn, after)


def _bwd_in(dproj, w_int, x, dx1, g_mix, after):
    L, D = x.shape
    N = w_int.shape[0]
    tm = min(L, 256)

    def body(dp_ref, w_ref, x_ref, dx1_ref, g_ref, after_ref, dx_ref, red_ref):
        @pl.when(pl.program_id(0) == 0)
        def _():
            red_ref[...] = jnp.zeros_like(red_ref)

        dh = _nn(dp_ref[...], w_ref[...])
        r, xh = _rms_stats(x_ref[...])
        red_ref[0:1, :] += jnp.sum(dh * xh, axis=0, keepdims=True)
        dx_ref[...] = dx1_ref[...] + _rms_bwd(dh * g_ref[...], xh, r)

    row = pl.BlockSpec((tm, D), lambda i: (i, 0))
    return _pallas_call(
        body, name="bwd_in", grid=(L // tm,),
        in_specs=[pl.BlockSpec((tm, N), lambda i: (i, 0)), pl.BlockSpec((N, D), lambda i: (0, 0)),
                  row, row, pl.BlockSpec((1, D), lambda i: (0, 0)), ANY],
        out_specs=[row, pl.BlockSpec((8, D), lambda i: (0, 0))],
        out_shape=[_sds((L, D), F32), _sds((8, D), F32)],
        compiler_params=_params(("arbitrary",), 56),
    )(dproj, w_int, x, dx1, g_mix, after)


def _mm_tn(name, a, b, a_spec, b_spec, o_block, n_out, n_k):
    def body(a_ref, b_ref, o_ref):
        part = _tn(a_ref[...], b_ref[...])

        @pl.when(pl.program_id(1) == 0)
        def _():
            o_ref[...] = part

        @pl.when(pl.program_id(1) > 0)
        def _():
            o_ref[...] += part

    return _pallas_call(
        body, name=name, grid=(n_out, n_k),
        in_specs=[a_spec, b_spec],
        out_specs=pl.BlockSpec((None,) + o_block, lambda j, k: (j, 0, 0)),
        out_shape=_sds((n_out,) + o_block, F32),
        compiler_params=_params(("parallel", "arbitrary"), 56),
    )(a, b)


TK_TOKENS = 2048


def _dw_cols(name, a, b, n_cols):
    L, M = a.shape
    tk = min(L, TK_TOKENS)
    return _mm_tn(name, a, b, pl.BlockSpec((tk, M), lambda j, k: (k, 0)),
                  pl.BlockSpec((tk, n_cols), lambda j, k: (k, j)), (M, n_cols), N_CHIPS, L // tk)


def _dw_rows(name, a, b):
    L, M = a.shape
    N = b.shape[1]
    tk = min(L, TK_TOKENS)
    return _mm_tn(name, a, b, pl.BlockSpec((tk, M // N_CHIPS), lambda j, k: (k, j)),
                  pl.BlockSpec((tk, N), lambda j, k: (k, 0)), (M // N_CHIPS, N), N_CHIPS, L // tk)


def _dw_rows2(name, a, b):
    L, M = a.shape
    N = b.shape[1]
    tk = min(L, TK_TOKENS)
    return _mm_tn(name, a, b, pl.BlockSpec((tk, M // 2), lambda j, k: (k, j)),
                  pl.BlockSpec((tk, N), lambda j, k: (k, 0)), (M // 2, N), 2, L // tk)


def _place():
    x, y, c = lax.axis_index("x"), lax.axis_index("y"), lax.axis_index("c")
    chips = [(1 - x, y), (x, 1 - y), (1 - x, 1 - y)]
    return x, y, c, 2 * x + y, chips


def _remote(src, dst, send_sem, recv_sem, device):
    return pltpu.make_async_remote_copy(src_ref=src, dst_ref=dst, send_sem=send_sem,
                                        recv_sem=recv_sem, device_id=device, device_id_type=MESH)


def _half(ref, lead, c, r2):
    return ref.at[lead, pl.ds(pl.multiple_of(c * r2, 16), r2), :]


def _cast_place(name, w, chip_idx):
    r, cols = w.shape
    tr = r // 2

    def body(k_ref, w_ref, o_ref):
        o_ref[...] = w_ref[...].astype(BF16)

    return _pallas_call(
        body, name=name,
        grid_spec=pltpu.PrefetchScalarGridSpec(
            num_scalar_prefetch=1, grid=(2,),
            in_specs=[pl.BlockSpec((tr, cols), lambda i, k_ref: (i, 0))],
            out_specs=pl.BlockSpec((None, tr, cols), lambda i, k_ref: (k_ref[0], i, 0))),
        out_shape=_sds((N_CHIPS, r, cols), BF16),
        compiler_params=_params(("parallel",), 48),
    )(chip_idx, w)


def _cast_place_t(name, w, chip_idx):
    r, cols = w.shape

    def body(k_ref, w_ref, o_ref):
        o_ref[...] = w_ref[...].T.astype(BF16)

    return _pallas_call(
        body, name=name,
        grid_spec=pltpu.PrefetchScalarGridSpec(
            num_scalar_prefetch=1, grid=(cols // LANES,),
            in_specs=[pl.BlockSpec((r, LANES), lambda i, k_ref: (0, i))],
            out_specs=pl.BlockSpec((None, LANES, r), lambda i, k_ref: (k_ref[0], i, 0))),
        out_shape=_sds((N_CHIPS, cols, r), BF16),
        compiler_params=_params(("parallel",), 48),
    )(chip_idx, w)


def _gather_copies(bufs, whole, send_sems, recv_sems):
    x, y, c, k, chips = _place()
    pairs = []
    for w, buf in enumerate(bufs):
        for j, (cx, cy) in enumerate(chips):
            if w in whole:
                mine, theirs = buf.at[k], buf.at[2 * cx + cy]
            else:
                r2 = buf.shape[1] // 2
                mine, theirs = _half(buf, k, c, r2), _half(buf, 2 * cx + cy, c, r2)
            sems = (send_sems.at[w * 3 + j], recv_sems.at[w * 3 + j])
            pairs.append((_remote(mine, mine, *sems, (cx, cy, c)), _remote(theirs, theirs, *sems, (x, y, c))))
    return pairs


def _gather_start(groups):
    flat = [b for bufs, _ in groups for b in bufs]
    nb, ng = len(flat), len(groups)

    def body(*refs):
        ins, sems, token = refs[:nb], refs[nb:nb + 2 * ng], refs[-1]
        pos = 0
        for g, (bufs, whole) in enumerate(groups):
            for send, _ in _gather_copies(ins[pos:pos + len(bufs)], whole, sems[2 * g], sems[2 * g + 1]):
                send.start()
            pos += len(bufs)
        token[...] = jnp.zeros_like(token)

    sem_shapes = []
    for bufs, _ in groups:
        sem_shapes += [pltpu.SemaphoreType.DMA((3 * len(bufs),))] * 2
    out = _pallas_call(
        body, name="gather_start",
        in_specs=[HBM] * nb, out_specs=tuple([SEM] * (2 * ng) + [HBM] * nb + [VMEM]),
        out_shape=tuple(sem_shapes + [pltpu.HBM(b.shape, b.dtype) for b in flat] + [_sds((8, LANES), F32)]),
        input_output_aliases={i: 2 * ng + i for i in range(nb)},
        compiler_params=pltpu.CompilerParams(has_side_effects=EFFECT),
    )(*flat)
    sems, thru, pos = [], [], 2 * ng
    for g, (bufs, _) in enumerate(groups):
        sems.append((out[2 * g], out[2 * g + 1]))
        thru.append(list(out[pos:pos + len(bufs)]))
        pos += len(bufs)
    return sems, thru, out[-1]


def _gather_wait(name, bufs, whole, sems, after):
    nb = len(bufs)

    def body(*refs):
        ins, send_sems, recv_sems = refs[:nb], refs[nb], refs[nb + 1]
        for send, arrival in _gather_copies(ins, whole, send_sems, recv_sems):
            send.wait_send()
            arrival.wait_recv()

    return _pallas_call(
        body, name=name,
        in_specs=[HBM] * nb + [SEM, SEM, ANY], out_specs=[HBM] * nb,
        out_shape=[pltpu.HBM(b.shape, b.dtype) for b in bufs],
        input_output_aliases={i: i for i in range(nb)},
        compiler_params=pltpu.CompilerParams(has_side_effects=EFFECT),
    )(*bufs, sems[0], sems[1], after)


def _gather_forward(name, bufs):
    n = len(bufs)

    def body(*refs):
        outs = refs[n:2 * n]
        send_sems, recv_sems = refs[2 * n:]
        x, y, c, _, chips = _place()
        sends = []
        for w in range(n):
            r2 = outs[w].shape[1] // 2
            for j, (cx, cy) in enumerate(chips):
                landed = _half(outs[w], 2 * cx + cy, c, r2)
                sends.append(_remote(landed, landed, send_sems.at[w * 3 + j], recv_sems.at[w * 3 + j],
                                     (x, y, 1 - c)))
        for cp in sends:
            cp.start()
        for w in range(n):
            r2 = outs[w].shape[1] // 2
            for j, (cx, cy) in enumerate(chips):
                got = _half(outs[w], 2 * cx + cy, 1 - c, r2)
                _remote(got, got, send_sems.at[w * 3 + j], recv_sems.at[w * 3 + j], (x, y, c)).wait_recv()
        for cp in sends:
            cp.wait_send()

    return _pallas_call(
        body, name=name,
        in_specs=[ANY] * n, out_specs=[ANY] * n,
        out_shape=[_sds(b.shape, b.dtype) for b in bufs],
        input_output_aliases={i: i for i in range(n)},
        scratch_shapes=[pltpu.SemaphoreType.DMA((n * 3,)), pltpu.SemaphoreType.DMA((n * 3,))],
    )(*bufs)


def _rs_sibling(name, grads):
    n = len(grads)

    def body(*refs):
        ins, outs = refs[:n], refs[n:2 * n]
        send_sems, recv_sems = refs[2 * n:]
        x, y, c, _, _ = _place()
        copies = []
        for w in range(n):
            r2 = ins[w].shape[1] // 2
            copies.append(_remote(_half(ins[w], slice(None), 1 - c, r2), outs[w],
                                  send_sems.at[w], recv_sems.at[w], (x, y, 1 - c)))
        for cp in copies:
            cp.start()
        for cp in copies:
            cp.wait()

    return _pallas_call(
        body, name=name,
        in_specs=[ANY] * n, out_specs=[ANY] * n,
        out_shape=[_sds((N_CHIPS, g.shape[1] // 2, g.shape[2]), F32) for g in grads],
        scratch_shapes=[pltpu.SemaphoreType.DMA((n,)), pltpu.SemaphoreType.DMA((n,))],
    )(*grads)


def _rs_add(name, grad3, from_sibling, c_idx):
    _, r2, cols = from_sibling.shape

    def body(c_ref, g_ref, s_ref, o_ref):
        o_ref[...] = (g_ref[...] + s_ref[...]).astype(BF16)

    return _pallas_call(
        body, name=name,
        grid_spec=pltpu.PrefetchScalarGridSpec(
            num_scalar_prefetch=1, grid=(N_CHIPS,),
            in_specs=[pl.BlockSpec((None, r2, cols), lambda k, c_ref: (k, c_ref[0], 0)),
                      pl.BlockSpec((None, r2, cols), lambda k, c_ref: (k, 0, 0))],
            out_specs=pl.BlockSpec((None, r2, cols), lambda k, c_ref: (k, 0, 0))),
        out_shape=_sds(from_sibling.shape, BF16),
        compiler_params=_params(("parallel",), 48),
    )(c_idx, grad3, from_sibling)


def _split_start(name, arrays, n_sems, pairs_fn):
    n = len(arrays)

    def body(*refs):
        for send, _ in pairs_fn(refs[:n], refs[n], refs[n + 1]):
            send.start()
        refs[-1][...] = jnp.zeros_like(refs[-1])

    out = _pallas_call(
        body, name=name,
        in_specs=[HBM] * n, out_specs=tuple([SEM, SEM] + [HBM] * n + [VMEM]),
        out_shape=tuple([pltpu.SemaphoreType.DMA((n_sems,))] * 2 + [pltpu.HBM(a.shape, a.dtype) for a in arrays]
                        + [_sds((8, LANES), F32)]),
        input_output_aliases={i: 2 + i for i in range(n)},
        compiler_params=pltpu.CompilerParams(has_side_effects=EFFECT),
    )(*arrays)
    return (out[0], out[1]), list(out[2:2 + n]), out[-1]


def _split_wait(name, sems, arrays, pairs_fn, after):
    n = len(arrays)

    def body(*refs):
        for send, arrival in pairs_fn(refs[:n], refs[n], refs[n + 1]):
            send.wait_send()
            arrival.wait_recv()

    return list(_pallas_call(
        body, name=name,
        in_specs=[HBM] * n + [SEM, SEM, ANY], out_specs=[HBM] * n,
        out_shape=[pltpu.HBM(a.shape, a.dtype) for a in arrays],
        input_output_aliases={i: i for i in range(n)},
        compiler_params=pltpu.CompilerParams(has_side_effects=EFFECT),
    )(*arrays, sems[0], sems[1], after))


def _forward_pairs(bufs, send_sems, recv_sems):
    x, y, c, _, chips = _place()
    pairs = []
    for w, buf in enumerate(bufs):
        r2 = buf.shape[1] // 2
        for j, (cx, cy) in enumerate(chips):
            landed, theirs = _half(buf, 2 * cx + cy, c, r2), _half(buf, 2 * cx + cy, 1 - c, r2)
            sems = (send_sems.at[w * 3 + j], recv_sems.at[w * 3 + j])
            pairs.append((_remote(landed, landed, *sems, (x, y, 1 - c)), _remote(theirs, theirs, *sems, (x, y, c))))
    return pairs


def _sibling_pairs(arrays, send_sems, recv_sems):
    x, y, c, _, _ = _place()
    n = len(arrays) // 2
    pairs = []
    for w in range(n):
        r2 = arrays[w].shape[1] // 2
        cp = _remote(_half(arrays[w], slice(None), 1 - c, r2), arrays[n + w], send_sems.at[w], recv_sems.at[w],
                     (x, y, 1 - c))
        pairs.append((cp, cp))
    return pairs


def _ici_pairs(arrays, send_sems, recv_sems):
    x, y, c, _, chips = _place()
    n = len(arrays) // 2
    pairs = []
    for w in range(n):
        for j, (cx, cy) in enumerate(chips):
            cp = _remote(arrays[w].at[2 * cx + cy], arrays[n + w].at[j],
                         send_sems.at[w * 3 + j], recv_sems.at[w * 3 + j], (cx, cy, c))
            pairs.append((cp, cp))
    return pairs


def _rs_sum(name, partials, received, place_idx):
    _, r2, cols = partials.shape
    nb = 2
    tr = r2 // nb

    def body(idx_ref, p_ref, r_ref, o_ref):
        o_ref[...] = ((p_ref[...].astype(F32) + r_ref[0].astype(F32))
                      + (r_ref[1].astype(F32) + r_ref[2].astype(F32)))

    return _pallas_call(
        body, name=name,
        grid_spec=pltpu.PrefetchScalarGridSpec(
            num_scalar_prefetch=1, grid=(nb,),
            in_specs=[pl.BlockSpec((None, tr, cols), lambda i, idx: (idx[0], i, 0)),
                      pl.BlockSpec((3, tr, cols), lambda i, idx: (0, i, 0))],
            out_specs=pl.BlockSpec((tr, cols), lambda i, idx: (idx[1] * nb + i, 0))),
        out_shape=_sds((2 * r2, cols), F32),
        compiler_params=_params(("parallel",), 48),
    )(place_idx, partials, received)


def _rs_share(name, shards):
    n = len(shards)

    def body(*refs):
        outs = refs[n:2 * n]
        send_sems, recv_sems = refs[2 * n:]
        x, y, c, _, _ = _place()
        sends = []
        for w in range(n):
            r2 = outs[w].shape[0] // 2
            mine = outs[w].at[pl.ds(pl.multiple_of(c * r2, 8), r2), :]
            sends.append(_remote(mine, mine, send_sems.at[w], recv_sems.at[w], (x, y, 1 - c)))
        for cp in sends:
            cp.start()
        for w in range(n):
            r2 = outs[w].shape[0] // 2
            theirs = outs[w].at[pl.ds(pl.multiple_of((1 - c) * r2, 8), r2), :]
            _remote(theirs, theirs, send_sems.at[w], recv_sems.at[w], (x, y, c)).wait_recv()
        for cp in sends:
            cp.wait_send()

    return _pallas_call(
        body, name=name,
        in_specs=[ANY] * n, out_specs=[ANY] * n,
        out_shape=[_sds(s.shape, F32) for s in shards],
        input_output_aliases={i: i for i in range(n)},
        scratch_shapes=[pltpu.SemaphoreType.DMA((n,)), pltpu.SemaphoreType.DMA((n,))],
    )(*shards)


def _small_allreduce(red_mix, red_ffn, red_final, red_hg, g_conv):
    rows = N_SMALL_ROWS
    D = red_mix.shape[1]
    H = red_hg.shape[1]

    def body(mix_ref, ffn_ref, fin_ref, hg_ref, cv_ref, sum_ref, all_ref, in_ref, send_sems, recv_sems):
        in_ref[...] = jnp.zeros_like(in_ref)
        in_ref[0:1, :] = mix_ref[0:1, :]
        in_ref[1:2, :] = ffn_ref[0:1, :]
        in_ref[2:3, :] = fin_ref[0:1, :]
        gam = hg_ref[1:2, 0:HEAD_DIM]
        for h in range(1, H // HEAD_DIM):
            gam = gam + hg_ref[1:2, h * HEAD_DIM:(h + 1) * HEAD_DIM]
        in_ref[3:4, 0:HEAD_DIM] = gam
        in_ref[3:4, HEAD_DIM:2 * HEAD_DIM] = fin_ref[1:2, 0:HEAD_DIM]
        in_ref[4:5, 0:H] = hg_ref[0:1, :]
        in_ref[6:9, 0:H] = cv_ref[...]
        x, y, c, _, _ = _place()
        me = 4 * x + 2 * y + c
        all_ref[me] = in_ref[...]
        copies = []
        for m in range(1, 8):
            mx, my, mc = (m >> 2) & 1, (m >> 1) & 1, m & 1
            px, py, pc = x ^ mx, y ^ my, c ^ mc
            copies.append((_remote(in_ref, all_ref.at[me], send_sems.at[m - 1], recv_sems.at[m - 1],
                                   (px, py, pc)), 4 * px + 2 * py + pc, m))
        for cp, _, _ in copies:
            cp.start()
        for _, peer, m in copies:
            _remote(in_ref, all_ref.at[peer], send_sems.at[m - 1], recv_sems.at[m - 1],
                    (x, y, c)).wait_recv()
        for cp, _, _ in copies:
            cp.wait_send()
        total = all_ref[0]
        for d in range(1, 8):
            total = total + all_ref[d]
        sum_ref[...] = total

    return _pallas_call(
        body, name="small_allreduce", pin=False,
        in_specs=[VMEM] * 5, out_specs=[VMEM, VMEM],
        out_shape=[_sds((rows, D), F32), _sds((8, rows, D), F32)],
        scratch_shapes=[pltpu.VMEM((rows, D), F32), pltpu.SemaphoreType.DMA((7,)),
                        pltpu.SemaphoreType.DMA((7,))],
    )(red_mix, red_ffn, red_final, red_hg, g_conv)[0]


def _adamw_math(w, g, m, v):
    m = ADAM_B1 * m + (1.0 - ADAM_B1) * g
    v = ADAM_B2 * v + (1.0 - ADAM_B2) * jnp.square(g)
    m_hat = m / (1.0 - ADAM_B1 ** ADAM_STEP)
    v_hat = v / (1.0 - ADAM_B2 ** ADAM_STEP)
    delta = -ADAM_LR * (m_hat / (jnp.sqrt(v_hat) + ADAM_EPS) + ADAM_WD * w)
    return delta, m, v


def _adamw(name, g, w, m, v):
    r, cols = g.shape
    tr = r // 4

    def body(g_ref, w_ref, m_ref, v_ref, d_ref, mo_ref, vo_ref):
        d_ref[...], mo_ref[...], vo_ref[...] = _adamw_math(w_ref[...], g_ref[...], m_ref[...], v_ref[...])

    blk = pl.BlockSpec((tr, cols), lambda i: (i, 0))
    return _pallas_call(
        body, name=name, grid=(r // tr,),
        in_specs=[blk] * 4, out_specs=[blk] * 3, out_shape=[_sds((r, cols), F32)] * 3,
        compiler_params=_params(("parallel",), 48),
    )(g, w, m, v)


def _small_update(total, chip_idx, ws, ms, vs):
    n = len(ws)
    H = ws[1].shape[1]

    def body(idx_ref, tot_ref, *refs):
        w, m, v, outs = refs[:n], refs[n:2 * n], refs[2 * n:3 * n], refs[3 * n:]
        chip = idx_ref[0]
        p0 = _lower_bound(w[1][...])
        dl0 = p0 * (1.0 - p0) * tot_ref[4:5, 0:H]
        conv = jnp.zeros((3, LANES), F32)
        for k in range(N_CHIPS):
            conv = jnp.where(chip == k, tot_ref[6:9, k * LANES:(k + 1) * LANES], conv)
        grads = [tot_ref[0:1, :], None, tot_ref[3:4, 0:HEAD_DIM], conv, tot_ref[1:2, :], tot_ref[2:3, :]]
        for p in range(n):
            g_ref, d_ref, mo_ref, vo_ref = outs[4 * p:4 * p + 4]
            if p == 1:
                for row, g in ((slice(0, 1), dl0), (slice(1, 2), -dl0)):
                    g_ref[row, :] = g
                    d_ref[row, :], mo_ref[row, :], vo_ref[row, :] = _adamw_math(
                        w[p][row, :], g, m[p][row, :], v[p][row, :])
            else:
                g_ref[...] = grads[p]
                d_ref[...], mo_ref[...], vo_ref[...] = _adamw_math(w[p][...], grads[p], m[p][...], v[p][...])
        outs[4 * n][...] = tot_ref[3:4, HEAD_DIM:2 * HEAD_DIM]

    full = lambda a: pl.BlockSpec(a.shape, lambda i, idx: (0,) * a.ndim)
    out_shape = [_sds(w.shape, F32) for w in ws for _ in range(4)] + [_sds((1, LANES), F32)]
    return _pallas_call(
        body, name="small_update",
        grid_spec=pltpu.PrefetchScalarGridSpec(
            num_scalar_prefetch=1, grid=(1,),
            in_specs=[full(total)] + [full(a) for a in ws + ms + vs],
            out_specs=[full(s) for s in out_shape]),
        out_shape=out_shape,
    )(chip_idx, total, *ws, *ms, *vs)


def kernel(x, norm_mix_g, w_in, lower_bounds, hg_norm_g, conv_w, w_branch_a, w_branch_b, w_out, norm_ffn_g, w_ffn_gate, w_ffn_up, w_ffn_down, norm_final_g, loss_target, m_norm_mix_g, m_w_in, m_lower_bounds, m_hg_norm_g, m_conv_w, m_w_branch_a, m_w_branch_b, m_w_out, m_norm_ffn_g, m_w_ffn_gate, m_w_ffn_up, m_w_ffn_down, m_norm_final_g, v_norm_mix_g, v_w_in, v_lower_bounds, v_hg_norm_g, v_conv_w, v_w_branch_a, v_w_branch_b, v_w_out, v_norm_ffn_g, v_w_ffn_gate, v_w_ffn_up, v_w_ffn_down, v_norm_final_g):
    _, L, D = x.shape
    H = D // 2
    assert lower_bounds.shape == (2, H) and hg_norm_g.shape == (1, HEAD_DIM)
    assert conv_w.shape == (1, 3, LANES) and w_in.shape[2] * N_CHIPS == 11 * H
    x2d, target = x.reshape(L, D), loss_target.reshape(L, D)
    g_final = norm_final_g.reshape(1, D)
    chip = 2 * lax.axis_index("x") + lax.axis_index("y")
    core = lax.axis_index("c")

    tr = lambda w: jnp.transpose(w[0])
    big = [w_in[0], w_branch_a[0], w_branch_b[0], w_out[0], tr(w_ffn_gate), tr(w_ffn_up), w_ffn_down[0]]
    big_m = [m_w_in[0], m_w_branch_a[0], m_w_branch_b[0], m_w_out[0], tr(m_w_ffn_gate), tr(m_w_ffn_up),
             m_w_ffn_down[0]]
    big_v = [v_w_in[0], v_w_branch_a[0], v_w_branch_b[0], v_w_out[0], tr(v_w_ffn_gate), tr(v_w_ffn_up),
             v_w_ffn_down[0]]
    names = ["w_in", "w_branch_a", "w_branch_b", "w_out", "w_ffn_gate", "w_ffn_up", "w_ffn_down"]

    chip_idx = chip.reshape(1).astype(jnp.int32)
    placed = [(_cast_place_t if j < 3 else _cast_place)("place_" + nm, w, chip_idx)
              for j, (nm, w) in enumerate(zip(names, big))]
    conv_placed = lax.dynamic_update_slice(jnp.zeros((N_CHIPS, 3, LANES), F32), conv_w, (chip, 0, 0))
    sems, in_flight, token = _gather_start([([placed[0], conv_placed], {1}), (placed[1:4], set()),
                                            (placed[4:], set())])
    w_in_landed, conv_all = _gather_wait("gather_wait_in", in_flight[0], {1}, sems[0], token)
    (w_int3,) = _gather_forward("gather_fwd_in", [w_in_landed])
    w_int = w_int3.reshape(-1, D)
    conv_full = jnp.transpose(conv_all, (1, 0, 2)).reshape(3, H)

    h, proj = _fwd_proj(x2d, norm_mix_g, w_int3)
    og, o_pre, s_saved = _hgrn_fwd(proj, lower_bounds, hg_norm_g, H)
    landed = _gather_wait("gather_wait_mix", in_flight[1], set(), sems[1], og)
    fwd_sems, landed, token = _split_start("gather_fwd_mix_start", landed, 9, _forward_pairs)
    cb = _conv_fwd(proj, conv_full, H, token)
    wat3, wbt3, wout3 = _split_wait("gather_fwd_mix_wait", fwd_sems, landed, _forward_pairs, cb)
    wat, wbt, wout = wat3.reshape(D, H), wbt3.reshape(D, H), wout3.reshape(D, D)
    landed = _gather_wait("gather_wait_ffn", in_flight[2], set(), sems[2], cb)
    fwd_sems, landed, token = _split_start("gather_fwd_ffn_start", landed, 9, _forward_pairs)
    ya, yb, merged, x1, h2 = _fwd_mix(og, cb, proj, x2d, wat, wbt, wout, norm_ffn_g, H, token)
    wgt3, wut3, wd3 = _split_wait("gather_fwd_ffn_wait", fwd_sems, landed, _forward_pairs, h2)
    d_ff = N_CHIPS * wd3.shape[1]
    wgt, wut, wd = wgt3.reshape(d_ff, D), wut3.reshape(d_ff, D), wd3.reshape(d_ff, D)
    ffn_a, ffn_b, ffn_s = _fwd_ffn_up(h2, wgt, wut)
    dx2, dx2b, red_final = _fwd_down_loss(ffn_s, wd, x1, target, g_final)

    c_idx = core.reshape(1).astype(jnp.int32)
    place_idx = jnp.stack([chip, core]).astype(jnp.int32)

    def sibling_start(tag, grads):
        bufs = [lax.empty((N_CHIPS, g.shape[1] // 2, g.shape[2]), F32) for g in grads]
        return _split_start("rs_sibling_start_" + tag, list(grads) + bufs, len(grads), _sibling_pairs)

    def ici_start(tag, js, grads, from_sibling):
        partials = [_rs_add("rs_add_" + names[j], g, s, c_idx) for j, g, s in zip(js, grads, from_sibling)]
        landings = [lax.empty((3,) + p.shape[1:], BF16) for p in partials]
        return _split_start("rs_ici_start_" + tag, partials + landings, 3 * len(js), _ici_pairs)

    def ici_start_behind(tag, js, started, after):
        n = len(js)
        arrays = _split_wait("rs_sibling_wait_" + tag, started[0], started[1], _sibling_pairs, after)
        return ici_start(tag, js, arrays[:n], arrays[n:])

    def rs_end(tag, js, started, after):
        n = len(js)
        arrays = _split_wait("rs_ici_wait_" + tag, started[0], started[1], _ici_pairs, after)
        halves = [_rs_sum("rs_sum_" + names[j], p, r, place_idx)
                  for j, p, r in zip(js, arrays[:n], arrays[n:])]
        return list(_rs_share("rs_share_" + tag, halves))

    shards3 = lambda g: g.reshape(N_CHIPS, d_ff // N_CHIPS, D)
    da, db = _bwd_down(dx2b, wd, ffn_a, ffn_b)
    g_wd = shards3(_dw_rows2("dw_ffn_down", ffn_s, dx2b))
    g_wg = shards3(_dw_rows2("dw_ffn_gate", da, h2))
    g_wu = shards3(_dw_rows2("dw_ffn_up", db, h2))
    ffn_sibling = sibling_start("ffn", [g_wg, g_wu, g_wd])
    dx1, dx1b, red_ffn = _bwd_ffn_dh(da, db, wgt, wut, x1, dx2, norm_ffn_g, ffn_sibling[2])
    ffn_ici = ici_start_behind("ffn", [4, 5, 6], ffn_sibling, dx1b)
    dya, dyb, dga, dgb, d_o, d_cb = _bwd_mix(dx1b, proj, ya, yb, wat, wbt, wout, H, ffn_ici[2])
    g_wout = _dw_rows("dw_out", merged, dx1b)
    g_wa = _dw_cols("dw_branch_a", og, dya, D // N_CHIPS)
    g_wb = _dw_cols("dw_branch_b", cb, dyb, D // N_CHIPS)
    mix_sibling = sibling_start("mix", [g_wa, g_wb, g_wout])
    dq, df, dv, dg, red_hg = _hgrn_bwd(proj, lower_bounds, hg_norm_g, o_pre, d_o, s_saved, H, mix_sibling[2])
    mix_ici = ici_start_behind("mix", [1, 2, 3], mix_sibling, dq)
    dcg, dbg, dxb, g_conv = _conv_bwd(proj, conv_full, d_cb, H, mix_ici[2])
    dproj = jnp.concatenate([dq, df, dv, dg, dcg, dbg, dxb, dga, dgb], axis=1)
    g_win = _dw_cols("dw_in", h, dproj, w_int3.shape[1])
    adamw = lambda j, g: _adamw("adamw_" + names[j], g, big[j], big_m[j], big_v[j])
    in_sibling = sibling_start("in", [g_win])
    shard_grads = [None] + rs_end("mix", [1, 2, 3], mix_ici, in_sibling[2]) + rs_end(
        "ffn", [4, 5, 6], ffn_ici, in_sibling[2])
    big_out = [None] + [adamw(j, shard_grads[j]) for j in range(1, 7)]
    in_ici = ici_start_behind("in", [0], in_sibling, big_out[6][0])
    grad_x, red_mix = _bwd_in(dproj, w_int, x2d, dx1, norm_mix_g, in_ici[2])
    shard_grads[0] = rs_end("in", [0], in_ici, grad_x)[0]
    big_out[0] = adamw(0, shard_grads[0])

    total = _small_allreduce(red_mix, red_ffn, red_final, red_hg, g_conv)

    def smalls(mix, lb, hg, cw, ffn, fin):
        return [mix, lb, hg, cw[0], ffn, fin.reshape(1, D)]

    small_out = _small_update(
        total, chip_idx,
        smalls(norm_mix_g, lower_bounds, hg_norm_g, conv_w, norm_ffn_g, norm_final_g),
        smalls(m_norm_mix_g, m_lower_bounds, m_hg_norm_g, m_conv_w, m_norm_ffn_g, m_norm_final_g),
        smalls(v_norm_mix_g, v_lower_bounds, v_hg_norm_g, v_conv_w, v_norm_ffn_g, v_norm_final_g))

    def outputs(i):
        big_i = [shard_grads[j] if i == 0 else big_out[j][i - 1] for j in range(7)]
        mix, lb, hg, cw, ffn, fin = [small_out[4 * p + i] for p in range(6)]
        return [mix, big_i[0][None], lb, hg, cw[None], big_i[1][None], big_i[2][None], big_i[3][None], ffn,
                big_i[4].T[None], big_i[5].T[None], big_i[6][None], fin.reshape(D)]

    outs = [small_out[24][0, 0], grad_x.reshape(1, L, D)]
    for i in range(4):
        outs += outputs(i)
    return tuple(outs)
```

```python
import functools

import jax
import jax.numpy as jnp
from jax import lax
from jax.experimental import pallas as pl
from jax.experimental.pallas import tpu as pltpu

F32 = jnp.float32
BF16 = jnp.bfloat16
EPS = 1e-6
CHUNK = 32
HEAD_DIM = 128
LANES = 128
N_CHIPS = 4
N_SMALL_ROWS = 16

ADAM_LR = 0.001
ADAM_B1 = 0.9
ADAM_B2 = 0.999
ADAM_EPS = 1e-08
ADAM_WD = 0.01
ADAM_STEP = 10

MESH = pl.DeviceIdType.MESH
ANY = pl.BlockSpec(memory_space=pl.ANY)
VMEM = pl.BlockSpec(memory_space=pltpu.VMEM)
HBM = pl.BlockSpec(memory_space=pltpu.HBM)
SEM = pl.BlockSpec(memory_space=pltpu.SEMAPHORE)
EFFECT = pltpu.SideEffectType.DATAFLOW_SIDE_EFFECTING


def _sds(shape, dtype):
    return jax.ShapeDtypeStruct(shape, dtype)


def _pallas_call(body, pin=True, **kwargs):
    if not pin:
        return pl.pallas_call(body, **kwargs)
    in_hbm = lambda s: pltpu.HBM(s.shape, s.dtype) if isinstance(s, jax.ShapeDtypeStruct) else s
    kwargs["out_shape"] = jax.tree.map(in_hbm, kwargs["out_shape"])
    call = pl.pallas_call(body, **kwargs)

    def run(*args):
        return call(*[pltpu.with_memory_space_constraint(a, pltpu.HBM) if a.dtype in (F32, BF16) else a
                      for a in args])

    return run


def _params(semantics, vmem_mb):
    return pltpu.CompilerParams(dimension_semantics=semantics, vmem_limit_bytes=vmem_mb << 20)


def _nn(a, b):
    return lax.dot_general(a, b, (((1,), (0,)), ((), ())), preferred_element_type=F32)


def _nt(a, b):
    return lax.dot_general(a, b, (((1,), (1,)), ((), ())), preferred_element_type=F32)


def _tn(a, b):
    return lax.dot_general(a, b, (((0,), (0,)), ((), ())), preferred_element_type=F32)


def _sigmoid(x):
    return jax.nn.sigmoid(x)


def _rms_stats(x):
    r = lax.rsqrt(jnp.mean(x * x, axis=-1, keepdims=True) + EPS)
    return r, x * r


def _rms_bwd(dxh, xh, r):
    return r * (dxh - xh * jnp.mean(dxh * xh, axis=-1, keepdims=True))


def _fwd_proj(x, g_mix, w_int3):
    L, D = x.shape
    tn = w_int3.shape[1]
    tm = min(L, 1024)

    def body(x_ref, g_ref, w_ref, h_ref, p_ref):
        @pl.when(pl.program_id(1) == 0)
        def _():
            _, xh = _rms_stats(x_ref[...])
            h_ref[...] = (xh * g_ref[...]).astype(BF16)

        p_ref[...] = _nt(h_ref[...], w_ref[...])

    return _pallas_call(
        body, name="fwd_proj", grid=(L // tm, N_CHIPS),
        in_specs=[pl.BlockSpec((tm, D), lambda i, j: (i, 0)),
                  pl.BlockSpec((1, D), lambda i, j: (0, 0)),
                  pl.BlockSpec((None, tn, D), lambda i, j: (j, 0, 0))],
        out_specs=[pl.BlockSpec((tm, D), lambda i, j: (i, 0)),
                   pl.BlockSpec((tm, tn), lambda i, j: (i, j))],
        out_shape=[_sds((L, D), BF16), _sds((L, N_CHIPS * tn), F32)],
        compiler_params=_params(("parallel", "arbitrary"), 48),
    )(x, g_mix, w_int3)


def _lower_bound(lbp):
    l0, l1 = lbp[0:1, :], lbp[1:2, :]
    m = jnp.maximum(l0, l1)
    e0, e1 = jnp.exp(l0 - m), jnp.exp(l1 - m)
    return e0 / (e0 + e1)


def _seg_scan(x, r32, forward):
    n = x.shape[0]
    s = 1
    while s < CHUNK:
        if forward:
            x = x + jnp.where(r32 >= s, pltpu.roll(x, s, 0), 0.0)
        else:
            x = x + jnp.where(r32 < CHUNK - s, pltpu.roll(x, n - s, 0), 0.0)
        s *= 2
    return x


def _bcast_row(x, row):
    n, w = x.shape
    nc = n // CHUNK
    x3 = x.reshape(nc, CHUNK, w)
    return jnp.broadcast_to(x3[:, row:row + 1, :], (nc, CHUNK, w)).reshape(n, w)


def _hgrn_prep(q_raw, f_raw, lb):
    r32 = lax.broadcasted_iota(jnp.int32, f_raw.shape, 0) & (CHUNK - 1)
    sig = _sigmoid(f_raw)
    f = lb + (1.0 - lb) * sig
    b = _seg_scan(jnp.log(f), r32, True)
    a = _bcast_row(b, CHUNK // 2 - 1)
    bl = _bcast_row(b, CHUNK - 1)
    sq = _sigmoid(q_raw)
    q = q_raw * sq * (HEAD_DIM ** -0.5)
    return dict(r32=r32, sig=sig, f=f, k=1.0 - f, b=b, a=a, bl=bl, sq=sq, q=q)


def _chunk_masks(n):
    ri = lax.broadcasted_iota(jnp.int32, (n, n), 0)
    ci = lax.broadcasted_iota(jnp.int32, (n, n), 1)
    same = (ri // CHUNK) == (ci // CHUNK)
    return same & (ci <= ri), same & (ri <= ci)


def _hgrn_fwd(proj, lower_bounds, gamma, H):
    L = proj.shape[0]
    nh = H // HEAD_DIM
    TL = min(L, 256)
    nc = TL // CHUNK

    def body(q_ref, f_ref, v_ref, g_ref, lbp_ref, gam_ref, og_ref, o_ref, s_ref, st_ref):
        @pl.when(pl.program_id(0) == 0)
        def _():
            st_ref[...] = jnp.zeros_like(st_ref)

        lb = _lower_bound(lbp_ref[...])
        gam = gam_ref[...]
        mask, _ = _chunk_masks(TL)
        rowc = lax.broadcasted_iota(jnp.int32, (TL, HEAD_DIM), 0) // CHUNK
        for h in range(nh):
            hs = slice(h * HEAD_DIM, (h + 1) * HEAD_DIM)
            p = _hgrn_prep(q_ref[:, hs], f_ref[:, hs], lb[:, hs])
            v = v_ref[:, hs]
            vb = v.astype(BF16)
            vt = v.T.astype(BF16)
            q_hat = (p["q"] * jnp.exp(p["b"] - p["a"])).astype(BF16)
            k_hat = (p["k"] * jnp.exp(p["a"] - p["b"])).astype(BF16)
            q_in = (p["q"] * jnp.exp(p["b"])).astype(BF16)
            k_out = (p["k"] * jnp.exp(p["bl"] - p["b"])).astype(BF16)
            dec = jnp.exp(p["bl"])
            att = jnp.where(mask, _nt(q_hat, k_hat), 0.0).astype(BF16)
            o_intra = _nn(att, vb)
            st = st_ref[h]
            for c in range(nc):
                rs = slice(c * CHUNK, (c + 1) * CHUNK)
                stb = st.astype(BF16)
                s_ref[c, h] = stb
                o_ref[rs, hs] = o_intra[rs] + _nt(q_in[rs], stb)
                k_c = jnp.where(rowc == c, k_out, jnp.zeros_like(k_out))
                st = st * dec[c * CHUNK:c * CHUNK + 1, :] + _nn(vt, k_c)
            st_ref[h] = st
            o = o_ref[:, hs]
            _, xh = _rms_stats(o)
            gr = g_ref[:, hs]
            og_ref[:, hs] = (xh * gam * (gr * _sigmoid(gr))).astype(BF16)

    col = lambda k: pl.BlockSpec((TL, H), lambda i, k=k: (i, k))
    return _pallas_call(
        body, name="hgrn_fwd", grid=(L // TL,),
        in_specs=[col(0), col(1), col(2), col(3),
                  pl.BlockSpec(lower_bounds.shape, lambda i: (0, 0)),
                  pl.BlockSpec(gamma.shape, lambda i: (0, 0))],
        out_specs=[pl.BlockSpec((TL, H), lambda i: (i, 0)),
                   pl.BlockSpec((TL, H), lambda i: (i, 0)),
                   pl.BlockSpec((nc, nh, HEAD_DIM, HEAD_DIM), lambda i: (i, 0, 0, 0))],
        out_shape=[_sds((L, H), BF16), _sds((L, H), F32),
                   _sds((L // CHUNK, nh, HEAD_DIM, HEAD_DIM), BF16)],
        scratch_shapes=[pltpu.VMEM((nh, HEAD_DIM, HEAD_DIM), F32)],
        compiler_params=_params(("arbitrary",), 48),
    )(proj, proj, proj, proj, lower_bounds, gamma)


def _hgrn_bwd(proj, lower_bounds, gamma, o_pre, d_out, s_saved, H, after):
    L = proj.shape[0]
    nh = H // HEAD_DIM
    TL = min(L, 256)
    nc = TL // CHUNK
    nt = L // TL

    def body(q_ref, f_ref, v_ref, g_ref, lbp_ref, gam_ref, o_ref, d_ref, s_ref, after_ref,
             dq_ref, df_ref, dv_ref, dg_ref, red_ref, dst_ref, dsall_ref, tmp_ref):
        @pl.when(pl.program_id(0) == 0)
        def _():
            dst_ref[...] = jnp.zeros_like(dst_ref)
            red_ref[...] = jnp.zeros_like(red_ref)

        lb = _lower_bound(lbp_ref[...])
        gam = gam_ref[...]
        mask, mask_t = _chunk_masks(TL)
        rowc = lax.broadcasted_iota(jnp.int32, (TL, HEAD_DIM), 0) // CHUNK
        for h in range(nh):
            hs = slice(h * HEAD_DIM, (h + 1) * HEAD_DIM)
            qr, gr, lbh = q_ref[:, hs], g_ref[:, hs], lb[:, hs]
            p = _hgrn_prep(qr, f_ref[:, hs], lbh)
            vb = v_ref[:, hs].astype(BF16)
            eba, eab = jnp.exp(p["b"] - p["a"]), jnp.exp(p["a"] - p["b"])
            eb, elb = jnp.exp(p["b"]), jnp.exp(p["bl"] - p["b"])
            dec = jnp.exp(p["bl"])
            q_hat, k_hat = p["q"] * eba, p["k"] * eab
            q_in, k_out = p["q"] * eb, p["k"] * elb
            q_hat_b, k_hat_b = q_hat.astype(BF16), k_hat.astype(BF16)
            q_in_b, k_out_b = q_in.astype(BF16), k_out.astype(BF16)

            o, dout = o_ref[:, hs], d_ref[:, hs]
            sg = _sigmoid(gr)
            r, xh = _rms_stats(o)
            dg_ref[:, hs] = (dout * (xh * gam) * (sg * (1.0 + gr * (1.0 - sg)))).astype(BF16)
            dn = dout * (gr * sg)
            red_ref[1:2, hs] += jnp.sum(dn * xh, axis=0, keepdims=True)
            do = _rms_bwd(dn * gam, xh, r)
            dob = do.astype(BF16)
            dot_b = do.T.astype(BF16)

            att_t = jnp.where(mask_t, _nt(k_hat_b, q_hat_b), 0.0).astype(BF16)
            dv_intra = _nn(att_t, dob)
            datt = jnp.where(mask, _nt(dob, vb), 0.0).astype(BF16)
            dqh = _nn(datt, k_hat_b)
            datt_t = jnp.where(mask_t, _nt(vb, dob), 0.0).astype(BF16)
            dkh = _nn(datt_t, q_hat_b)

            dst = dst_ref[h]
            for c in reversed(range(nc)):
                dsall_ref[c] = dst
                q_c = jnp.where(rowc == c, q_in_b, jnp.zeros_like(q_in_b))
                dst = dst * dec[c * CHUNK:c * CHUNK + 1, :] + _nn(dot_b, q_c)
            dst_ref[h] = dst
            for c in range(nc):
                rs = slice(c * CHUNK, (c + 1) * CHUNK)
                ds_c = dsall_ref[c]
                dsb = ds_c.astype(BF16)
                st_prev = s_ref[c, h]
                tmp_ref[0, rs, :] = _nt(k_out_b[rs], dsb)
                tmp_ref[1, rs, :] = _nn(vb[rs], dsb)
                tmp_ref[2, rs, :] = _nn(dob[rs], st_prev)
                ddec = jnp.sum(ds_c * st_prev.astype(F32), axis=0, keepdims=True)
                tmp_ref[3, rs, :] = jnp.broadcast_to(ddec * dec[c * CHUNK:c * CHUNK + 1, :],
                                                     (CHUNK, HEAD_DIM))
            dko, dqi = tmp_ref[1], tmp_ref[2]
            dq = dqh * eba + dqi * eb
            dk = dkh * eab + dko * elb
            tko = dko * k_out
            db = dqh * q_hat - dkh * k_hat + dqi * q_in - tko
            dlog = (_seg_scan(db, p["r32"], False)
                    + _bcast_row(_seg_scan(tko, p["r32"], True), CHUNK - 1) + tmp_ref[3])
            df = dlog / p["f"] - dk
            sig = p["sig"]
            red_ref[0:1, hs] += jnp.sum(df * (1.0 - sig), axis=0, keepdims=True)
            df_ref[:, hs] = (df * (1.0 - lbh) * sig * (1.0 - sig)).astype(BF16)
            sq = p["sq"]
            dq_ref[:, hs] = (dq * (HEAD_DIM ** -0.5) * (sq * (1.0 + qr * (1.0 - sq)))).astype(BF16)
            dv_ref[:, hs] = (dv_intra + tmp_ref[0]).astype(BF16)

    col = lambda k: pl.BlockSpec((TL, H), lambda i, k=k: (nt - 1 - i, k))
    rev = pl.BlockSpec((TL, H), lambda i: (nt - 1 - i, 0))
    return _pallas_call(
        body, name="hgrn_bwd", grid=(nt,),
        in_specs=[col(0), col(1), col(2), col(3),
                  pl.BlockSpec(lower_bounds.shape, lambda i: (0, 0)),
                  pl.BlockSpec(gamma.shape, lambda i: (0, 0)),
                  rev, rev,
                  pl.BlockSpec((nc, nh, HEAD_DIM, HEAD_DIM), lambda i: (nt - 1 - i, 0, 0, 0)), ANY],
        out_specs=[rev, rev, rev, rev, pl.BlockSpec((8, H), lambda i: (0, 0))],
        out_shape=[_sds((L, H), BF16)] * 4 + [_sds((8, H), F32)],
        scratch_shapes=[pltpu.VMEM((nh, HEAD_DIM, HEAD_DIM), F32),
                        pltpu.VMEM((nc, HEAD_DIM, HEAD_DIM), F32),
                        pltpu.VMEM((4, TL, HEAD_DIM), F32)],
        compiler_params=_params(("arbitrary",), 48),
    )(proj, proj, proj, proj, lower_bounds, gamma, o_pre, d_out, s_saved, after)


def _shift_down(u, s, row):
    return jnp.where(row >= s, pltpu.roll(u, s, 0), 0.0)


def _shift_up(u, s, row):
    n = u.shape[0]
    return jnp.where(row < n - s, pltpu.roll(u, n - s, 0), 0.0)


def _conv_specs(L, H):
    per = H // LANES
    return [pl.BlockSpec((L, LANES), lambda j, o=o: (0, o * per + j)) for o in (4, 5, 6)]


def _conv_fwd(proj, conv_w, H, after):
    L = proj.shape[0]

    def body(c_ref, b_ref, x_ref, w_ref, after_ref, o_ref):
        row = lax.broadcasted_iota(jnp.int32, (L, LANES), 0)
        u = c_ref[...] * x_ref[...]
        w = w_ref[...]
        y = w[0:1] * _shift_down(u, 2, row) + w[1:2] * _shift_down(u, 1, row) + w[2:3] * u
        o_ref[...] = (b_ref[...] * y).astype(BF16)

    return _pallas_call(
        body, name="conv_fwd", grid=(H // LANES,),
        in_specs=_conv_specs(L, H) + [pl.BlockSpec((3, LANES), lambda j: (0, j)), ANY],
        out_specs=pl.BlockSpec((L, LANES), lambda j: (0, j)),
        out_shape=_sds((L, H), BF16),
        compiler_params=_params(("parallel",), 48),
    )(proj, proj, proj, conv_w, after)


def _conv_bwd(proj, conv_w, dcb, H, after):
    L = proj.shape[0]

    def body(c_ref, b_ref, x_ref, w_ref, d_ref, after_ref, dc_ref, db_ref, dx_ref, dw_ref):
        row = lax.broadcasted_iota(jnp.int32, (L, LANES), 0)
        cg, xb = c_ref[...], x_ref[...]
        u = cg * xb
        u1, u2 = _shift_down(u, 1, row), _shift_down(u, 2, row)
        w = w_ref[...]
        y = w[0:1] * u2 + w[1:2] * u1 + w[2:3] * u
        d = d_ref[...]
        db_ref[...] = (d * y).astype(BF16)
        dy = d * b_ref[...]
        du = w[2:3] * dy + w[1:2] * _shift_up(dy, 1, row) + w[0:1] * _shift_up(dy, 2, row)
        dw_ref[0:1, :] = jnp.sum(dy * u2, axis=0, keepdims=True)
        dw_ref[1:2, :] = jnp.sum(dy * u1, axis=0, keepdims=True)
        dw_ref[2:3, :] = jnp.sum(dy * u, axis=0, keepdims=True)
        dc_ref[...] = (du * xb).astype(BF16)
        dx_ref[...] = (du * cg).astype(BF16)

    blk = pl.BlockSpec((L, LANES), lambda j: (0, j))
    return _pallas_call(
        body, name="conv_bwd", grid=(H // LANES,),
        in_specs=_conv_specs(L, H) + [pl.BlockSpec((3, LANES), lambda j: (0, j)), blk, ANY],
        out_specs=[blk, blk, blk, pl.BlockSpec((3, LANES), lambda j: (0, j))],
        out_shape=[_sds((L, H), BF16)] * 3 + [_sds((3, H), F32)],
        compiler_params=_params(("parallel",), 56),
    )(proj, proj, proj, conv_w, dcb, after)


def _gate_specs(tm, H):
    return [pl.BlockSpec((tm, H), lambda i, k=k: (i, k)) for k in (7, 8, 9, 10)]


def _fwd_mix(og, cb, proj, x, wat, wbt, wout, g_ffn, H, after):
    L, D = x.shape
    tm = min(L, 512)

    def body(o_ref, cb_ref, ga0, ga1, gb0, gb1, x_ref, wa_ref, wb_ref, wo_ref, g_ref, after_ref,
             ya_ref, yb_ref, m_ref, x1_ref, h2_ref):
        ya, yb = _nt(o_ref[...], wa_ref[...]), _nt(cb_ref[...], wb_ref[...])
        ya_ref[...] = ya.astype(BF16)
        yb_ref[...] = yb.astype(BF16)
        for k, (gar, gbr) in enumerate(((ga0, gb0), (ga1, gb1))):
            cs = slice(k * H, (k + 1) * H)
            m_ref[:, cs] = (_sigmoid(gar[...]) * ya[:, cs] + _sigmoid(gbr[...]) * yb[:, cs]).astype(BF16)
        x1 = x_ref[...] + _nn(m_ref[...], wo_ref[...])
        x1_ref[...] = x1
        _, xh = _rms_stats(x1)
        h2_ref[...] = (xh * g_ref[...]).astype(BF16)

    row = lambda w: pl.BlockSpec((tm, w), lambda i: (i, 0))
    full = lambda a: pl.BlockSpec(a.shape, lambda i: (0,) * a.ndim)
    return _pallas_call(
        body, name="fwd_mix", grid=(L // tm,),
        in_specs=[row(H), row(H)] + _gate_specs(tm, H) + [row(D), full(wat), full(wbt), full(wout),
                                                           full(g_ffn), ANY],
        out_specs=[row(D)] * 5,
        out_shape=[_sds((L, D), BF16)] * 3 + [_sds((L, D), F32), _sds((L, D), BF16)],
        compiler_params=_params(("parallel",), 56),
    )(og, cb, proj, proj, proj, proj, x, wat, wbt, wout, g_ffn, after)


def _bwd_mix(dx1b, proj, ya, yb, wat, wbt, wout, H, after):
    L, D = dx1b.shape
    tm = min(L, 512)

    def body(dx_ref, ga0, ga1, gb0, gb1, ya_ref, yb_ref, wa_ref, wb_ref, wo_ref, after_ref,
             dya_ref, dyb_ref, dga_ref, dgb_ref, do_ref, dcb_ref):
        dm = _nt(dx_ref[...], wo_ref[...])
        for k, (gar, gbr) in enumerate(((ga0, gb0), (ga1, gb1))):
            cs = slice(k * H, (k + 1) * H)
            sa, sb = _sigmoid(gar[...]), _sigmoid(gbr[...])
            dmk = dm[:, cs]
            dga_ref[:, cs] = (dmk * ya_ref[:, cs].astype(F32) * sa * (1.0 - sa)).astype(BF16)
            dgb_ref[:, cs] = (dmk * yb_ref[:, cs].astype(F32) * sb * (1.0 - sb)).astype(BF16)
            dya_ref[:, cs] = (dmk * sa).astype(BF16)
            dyb_ref[:, cs] = (dmk * sb).astype(BF16)
        do_ref[...] = _nn(dya_ref[...], wa_ref[...])
        dcb_ref[...] = _nn(dyb_ref[...], wb_ref[...])

    row = lambda w: pl.BlockSpec((tm, w), lambda i: (i, 0))
    full = lambda a: pl.BlockSpec(a.shape, lambda i: (0,) * a.ndim)
    return _pallas_call(
        body, name="bwd_mix", grid=(L // tm,),
        in_specs=[row(D)] + _gate_specs(tm, H) + [row(D), row(D), full(wat), full(wbt), full(wout), ANY],
        out_specs=[row(D)] * 4 + [row(H)] * 2,
        out_shape=[_sds((L, D), BF16)] * 4 + [_sds((L, H), F32)] * 2,
        compiler_params=_params(("parallel",), 56),
    )(dx1b, proj, proj, proj, proj, ya, yb, wat, wbt, wout, after)


def _fwd_ffn_up(h2, wgt, wut):
    L, D = h2.shape
    F = wgt.shape[0]
    tn = F // 2
    tm = min(L, 512)

    def body(h_ref, wg_ref, wu_ref, a_ref, b_ref, s_ref):
        h = h_ref[...]
        a, b = _nt(h, wg_ref[...]), _nt(h, wu_ref[...])
        a_ref[...] = a.astype(BF16)
        b_ref[...] = b.astype(BF16)
        s_ref[...] = (a * _sigmoid(a) * b).astype(BF16)

    wspec = pl.BlockSpec((tn, D), lambda j, i: (j, 0))
    ospec = pl.BlockSpec((tm, tn), lambda j, i: (i, j))
    return _pallas_call(
        body, name="fwd_ffn_up", grid=(2, L // tm),
        in_specs=[pl.BlockSpec((tm, D), lambda j, i: (i, 0)), wspec, wspec],
        out_specs=[ospec] * 3,
        out_shape=[_sds((L, F), BF16)] * 3,
        compiler_params=_params(("parallel", "parallel"), 48),
    )(h2, wgt, wut)


def _fwd_down_loss(s, wd, x1, target, g_final):
    L, D = x1.shape
    F = wd.shape[0]
    tm = min(L, 256)

    def body(s_ref, wd_ref, x1_ref, t_ref, g_ref, dx_ref, dxb_ref, red_ref):
        @pl.when(pl.program_id(0) == 0)
        def _():
            red_ref[...] = jnp.zeros_like(red_ref)

        g = g_ref[...]
        r, xh = _rms_stats(x1_ref[...] + _nn(s_ref[...], wd_ref[...]))
        e = xh * g - t_ref[...]
        dy = e * (1.0 / D)
        dx = _rms_bwd(dy * g, xh, r)
        dx_ref[...] = dx
        dxb_ref[...] = dx.astype(BF16)
        red_ref[0:1, :] += jnp.sum(dy * xh, axis=0, keepdims=True)
        red_ref[1:2, :] += jnp.broadcast_to(0.5 * jnp.sum(e * e) * (1.0 / D), (1, D))

    row = pl.BlockSpec((tm, D), lambda i: (i, 0))
    return _pallas_call(
        body, name="fwd_down_loss", grid=(L // tm,),
        in_specs=[pl.BlockSpec((tm, F), lambda i: (i, 0)), pl.BlockSpec((F, D), lambda i: (0, 0)),
                  row, row, pl.BlockSpec((1, D), lambda i: (0, 0))],
        out_specs=[row, row, pl.BlockSpec((8, D), lambda i: (0, 0))],
        out_shape=[_sds((L, D), F32), _sds((L, D), BF16), _sds((8, D), F32)],
        compiler_params=_params(("arbitrary",), 48),
    )(s, wd, x1, target, g_final)


def _bwd_down(dx2b, wd, a, b):
    L, D = dx2b.shape
    F = wd.shape[0]
    tn = F // 2
    tm = min(L, 512)

    def body(dx_ref, wd_ref, a_ref, b_ref, da_ref, db_ref):
        ds = _nt(dx_ref[...], wd_ref[...])
        a, b = a_ref[...].astype(F32), b_ref[...].astype(F32)
        sg = _sigmoid(a)
        da_ref[...] = (ds * b * sg * (1.0 + a * (1.0 - sg))).astype(BF16)
        db_ref[...] = (ds * a * sg).astype(BF16)

    ospec = pl.BlockSpec((tm, tn), lambda j, i: (i, j))
    return _pallas_call(
        body, name="bwd_down", grid=(2, L // tm),
        in_specs=[pl.BlockSpec((tm, D), lambda j, i: (i, 0)),
                  pl.BlockSpec((tn, D), lambda j, i: (j, 0)), ospec, ospec],
        out_specs=[ospec] * 2,
        out_shape=[_sds((L, F), BF16)] * 2,
        compiler_params=_params(("parallel", "parallel"), 48),
    )(dx2b, wd, a, b)


def _bwd_ffn_dh(da, db, wgt, wut, x1, dx2, g_ffn, after):
    L, D = x1.shape
    F = wgt.shape[0]
    tm = min(L, 256)

    def body(da_ref, db_ref, wg_ref, wu_ref, x1_ref, dx2_ref, g_ref, after_ref, dx_ref, dxb_ref, red_ref):
        @pl.when(pl.program_id(0) == 0)
        def _():
            red_ref[...] = jnp.zeros_like(red_ref)

        dh = _nn(da_ref[...], wg_ref[...]) + _nn(db_ref[...], wu_ref[...])
        r, xh = _rms_stats(x1_ref[...])
        red_ref[0:1, :] += jnp.sum(dh * xh, axis=0, keepdims=True)
        dx = dx2_ref[...] + _rms_bwd(dh * g_ref[...], xh, r)
        dx_ref[...] = dx
        dxb_ref[...] = dx.astype(BF16)

    row = pl.BlockSpec((tm, D), lambda i: (i, 0))
    aspec = pl.BlockSpec((tm, F), lambda i: (i, 0))
    wspec = pl.BlockSpec((F, D), lambda i: (0, 0))
    return _pallas_call(
        body, name="bwd_ffn_dh", grid=(L // tm,),
        in_specs=[aspec, aspec, wspec, wspec, row, row, pl.BlockSpec((1, D), lambda i: (0, 0)), ANY],
        out_specs=[row, row, pl.BlockSpec((8, D), lambda i: (0, 0))],
        out_shape=[_sds((L, D), F32), _sds((L, D), BF16), _sds((8, D), F32)],
        compiler_params=_params(("arbitrary",), 56),
    )(da, db, wgt, wut, x1, dx2, g_ffn, after)


def _bwd_in(dproj, w_int, x, dx1, g_mix, after):
    L, D = x.shape
    N = w_int.shape[0]
    tm = min(L, 256)

    def body(dp_ref, w_ref, x_ref, dx1_ref, g_ref, after_ref, dx_ref, red_ref):
        @pl.when(pl.program_id(0) == 0)
        def _():
            red_ref[...] = jnp.zeros_like(red_ref)

        dh = _nn(dp_ref[...], w_ref[...])
        r, xh = _rms_stats(x_ref[...])
        red_ref[0:1, :] += jnp.sum(dh * xh, axis=0, keepdims=True)
        dx_ref[...] = dx1_ref[...] + _rms_bwd(dh * g_ref[...], xh, r)

    row = pl.BlockSpec((tm, D), lambda i: (i, 0))
    return _pallas_call(
        body, name="bwd_in", grid=(L // tm,),
        in_specs=[pl.BlockSpec((tm, N), lambda i: (i, 0)), pl.BlockSpec((N, D), lambda i: (0, 0)),
                  row, row, pl.BlockSpec((1, D), lambda i: (0, 0)), ANY],
        out_specs=[row, pl.BlockSpec((8, D), lambda i: (0, 0))],
        out_shape=[_sds((L, D), F32), _sds((8, D), F32)],
        compiler_params=_params(("arbitrary",), 56),
    )(dproj, w_int, x, dx1, g_mix, after)


def _mm_tn(name, a, b, a_spec, b_spec, o_block, n_out, n_k):
    def body(a_ref, b_ref, o_ref):
        part = _tn(a_ref[...], b_ref[...])

        @pl.when(pl.program_id(1) == 0)
        def _():
            o_ref[...] = part

        @pl.when(pl.program_id(1) > 0)
        def _():
            o_ref[...] += part

    return _pallas_call(
        body, name=name, grid=(n_out, n_k),
        in_specs=[a_spec, b_spec],
        out_specs=pl.BlockSpec((None,) + o_block, lambda j, k: (j, 0, 0)),
        out_shape=_sds((n_out,) + o_block, F32),
        compiler_params=_params(("parallel", "arbitrary"), 56),
    )(a, b)


TK_TOKENS = 2048


def _dw_cols(name, a, b, n_cols):
    L, M = a.shape
    tk = min(L, TK_TOKENS)
    return _mm_tn(name, a, b, pl.BlockSpec((tk, M), lambda j, k: (k, 0)),
                  pl.BlockSpec((tk, n_cols), lambda j, k: (k, j)), (M, n_cols), N_CHIPS, L // tk)


def _dw_rows(name, a, b):
    L, M = a.shape
    N = b.shape[1]
    tk = min(L, TK_TOKENS)
    return _mm_tn(name, a, b, pl.BlockSpec((tk, M // N_CHIPS), lambda j, k: (k, j)),
                  pl.BlockSpec((tk, N), lambda j, k: (k, 0)), (M // N_CHIPS, N), N_CHIPS, L // tk)


def _dw_rows2(name, a, b):
    L, M = a.shape
    N = b.shape[1]
    tk = min(L, TK_TOKENS)
    return _mm_tn(name, a, b, pl.BlockSpec((tk, M // 2), lambda j, k: (k, j)),
                  pl.BlockSpec((tk, N), lambda j, k: (k, 0)), (M // 2, N), 2, L // tk)


def _place():
    x, y, c = lax.axis_index("x"), lax.axis_index("y"), lax.axis_index("c")
    chips = [(1 - x, y), (x, 1 - y), (1 - x, 1 - y)]
    return x, y, c, 2 * x + y, chips


def _remote(src, dst, send_sem, recv_sem, device):
    return pltpu.make_async_remote_copy(src_ref=src, dst_ref=dst, send_sem=send_sem,
                                        recv_sem=recv_sem, device_id=device, device_id_type=MESH)


def _half(ref, lead, c, r2):
    return ref.at[lead, pl.ds(pl.multiple_of(c * r2, 16), r2), :]


def _cast_place(name, w, chip_idx):
    r, cols = w.shape
    tr = r // 2

    def body(k_ref, w_ref, o_ref):
        o_ref[...] = w_ref[...].astype(BF16)

    return _pallas_call(
        body, name=name,
        grid_spec=pltpu.PrefetchScalarGridSpec(
            num_scalar_prefetch=1, grid=(2,),
            in_specs=[pl.BlockSpec((tr, cols), lambda i, k_ref: (i, 0))],
            out_specs=pl.BlockSpec((None, tr, cols), lambda i, k_ref: (k_ref[0], i, 0))),
        out_shape=_sds((N_CHIPS, r, cols), BF16),
        compiler_params=_params(("parallel",), 48),
    )(chip_idx, w)


def _cast_place_t(name, w, chip_idx):
    r, cols = w.shape

    def body(k_ref, w_ref, o_ref):
        o_ref[...] = w_ref[...].T.astype(BF16)

    return _pallas_call(
        body, name=name,
        grid_spec=pltpu.PrefetchScalarGridSpec(
            num_scalar_prefetch=1, grid=(cols // LANES,),
            in_specs=[pl.BlockSpec((r, LANES), lambda i, k_ref: (0, i))],
            out_specs=pl.BlockSpec((None, LANES, r), lambda i, k_ref: (k_ref[0], i, 0))),
        out_shape=_sds((N_CHIPS, cols, r), BF16),
        compiler_params=_params(("parallel",), 48),
    )(chip_idx, w)


def _gather_copies(bufs, whole, send_sems, recv_sems):
    x, y, c, k, chips = _place()
    pairs = []
    for w, buf in enumerate(bufs):
        for j, (cx, cy) in enumerate(chips):
            if w in whole:
                mine, theirs = buf.at[k], buf.at[2 * cx + cy]
            else:
                r2 = buf.shape[1] // 2
                mine, theirs = _half(buf, k, c, r2), _half(buf, 2 * cx + cy, c, r2)
            sems = (send_sems.at[w * 3 + j], recv_sems.at[w * 3 + j])
            pairs.append((_remote(mine, mine, *sems, (cx, cy, c)), _remote(theirs, theirs, *sems, (x, y, c))))
    return pairs


def _gather_start(groups):
    flat = [b for bufs, _ in groups for b in bufs]
    nb, ng = len(flat), len(groups)

    def body(*refs):
        ins, sems, token = refs[:nb], refs[nb:nb + 2 * ng], refs[-1]
        pos = 0
        for g, (bufs, whole) in enumerate(groups):
            for send, _ in _gather_copies(ins[pos:pos + len(bufs)], whole, sems[2 * g], sems[2 * g + 1]):
                send.start()
            pos += len(bufs)
        token[...] = jnp.zeros_like(token)

    sem_shapes = []
    for bufs, _ in groups:
        sem_shapes += [pltpu.SemaphoreType.DMA((3 * len(bufs),))] * 2
    out = _pallas_call(
        body, name="gather_start",
        in_specs=[HBM] * nb, out_specs=tuple([SEM] * (2 * ng) + [HBM] * nb + [VMEM]),
        out_shape=tuple(sem_shapes + [pltpu.HBM(b.shape, b.dtype) for b in flat] + [_sds((8, LANES), F32)]),
        input_output_aliases={i: 2 * ng + i for i in range(nb)},
        compiler_params=pltpu.CompilerParams(has_side_effects=EFFECT),
    )(*flat)
    sems, thru, pos = [], [], 2 * ng
    for g, (bufs, _) in enumerate(groups):
        sems.append((out[2 * g], out[2 * g + 1]))
        thru.append(list(out[pos:pos + len(bufs)]))
        pos += len(bufs)
    return sems, thru, out[-1]


def _gather_wait(name, bufs, whole, sems, after):
    nb = len(bufs)

    def body(*refs):
        ins, send_sems, recv_sems = refs[:nb], refs[nb], refs[nb + 1]
        for send, arrival in _gather_copies(ins, whole, send_sems, recv_sems):
            send.wait_send()
            arrival.wait_recv()

    return _pallas_call(
        body, name=name,
        in_specs=[HBM] * nb + [SEM, SEM, ANY], out_specs=[HBM] * nb,
        out_shape=[pltpu.HBM(b.shape, b.dtype) for b in bufs],
        input_output_aliases={i: i for i in range(nb)},
        compiler_params=pltpu.CompilerParams(has_side_effects=EFFECT),
    )(*bufs, sems[0], sems[1], after)


def _gather_forward(name, bufs):
    n = len(bufs)

    def body(*refs):
        outs = refs[n:2 * n]
        send_sems, recv_sems = refs[2 * n:]
        x, y, c, _, chips = _place()
        sends = []
        for w in range(n):
            r2 = outs[w].shape[1] // 2
            for j, (cx, cy) in enumerate(chips):
                landed = _half(outs[w], 2 * cx + cy, c, r2)
                sends.append(_remote(landed, landed, send_sems.at[w * 3 + j], recv_sems.at[w * 3 + j],
                                     (x, y, 1 - c)))
        for cp in sends:
            cp.start()
        for w in range(n):
            r2 = outs[w].shape[1] // 2
            for j, (cx, cy) in enumerate(chips):
                got = _half(outs[w], 2 * cx + cy, 1 - c, r2)
                _remote(got, got, send_sems.at[w * 3 + j], recv_sems.at[w * 3 + j], (x, y, c)).wait_recv()
        for cp in sends:
            cp.wait_send()

    return _pallas_call(
        body, name=name,
        in_specs=[ANY] * n, out_specs=[ANY] * n,
        out_shape=[_sds(b.shape, b.dtype) for b in bufs],
        input_output_aliases={i: i for i in range(n)},
        scratch_shapes=[pltpu.SemaphoreType.DMA((n * 3,)), pltpu.SemaphoreType.DMA((n * 3,))],
    )(*bufs)


def _rs_sibling(name, grads):
    n = len(grads)

    def body(*refs):
        ins, outs = refs[:n], refs[n:2 * n]
        send_sems, recv_sems = refs[2 * n:]
        x, y, c, _, _ = _place()
        copies = []
        for w in range(n):
            r2 = ins[w].shape[1] // 2
            copies.append(_remote(_half(ins[w], slice(None), 1 - c, r2), outs[w],
                                  send_sems.at[w], recv_sems.at[w], (x, y, 1 - c)))
        for cp in copies:
            cp.start()
        for cp in copies:
            cp.wait()

    return _pallas_call(
        body, name=name,
        in_specs=[ANY] * n, out_specs=[ANY] * n,
        out_shape=[_sds((N_CHIPS, g.shape[1] // 2, g.shape[2]), F32) for g in grads],
        scratch_shapes=[pltpu.SemaphoreType.DMA((n,)), pltpu.SemaphoreType.DMA((n,))],
    )(*grads)


def _rs_add(name, grad3, from_sibling, c_idx):
    _, r2, cols = from_sibling.shape

    def body(c_ref, g_ref, s_ref, o_ref):
        o_ref[...] = (g_ref[...] + s_ref[...]).astype(BF16)

    return _pallas_call(
        body, name=name,
        grid_spec=pltpu.PrefetchScalarGridSpec(
            num_scalar_prefetch=1, grid=(N_CHIPS,),
            in_specs=[pl.BlockSpec((None, r2, cols), lambda k, c_ref: (k, c_ref[0], 0)),
                      pl.BlockSpec((None, r2, cols), lambda k, c_ref: (k, 0, 0))],
            out_specs=pl.BlockSpec((None, r2, cols), lambda k, c_ref: (k, 0, 0))),
        out_shape=_sds(from_sibling.shape, BF16),
        compiler_params=_params(("parallel",), 48),
    )(c_idx, grad3, from_sibling)


def _split_start(name, arrays, n_sems, pairs_fn):
    n = len(arrays)

    def body(*refs):
        for send, _ in pairs_fn(refs[:n], refs[n], refs[n + 1]):
            send.start()
        refs[-1][...] = jnp.zeros_like(refs[-1])

    out = _pallas_call(
        body, name=name,
        in_specs=[HBM] * n, out_specs=tuple([SEM, SEM] + [HBM] * n + [VMEM]),
        out_shape=tuple([pltpu.SemaphoreType.DMA((n_sems,))] * 2 + [pltpu.HBM(a.shape, a.dtype) for a in arrays]
                        + [_sds((8, LANES), F32)]),
        input_output_aliases={i: 2 + i for i in range(n)},
        compiler_params=pltpu.CompilerParams(has_side_effects=EFFECT),
    )(*arrays)
    return (out[0], out[1]), list(out[2:2 + n]), out[-1]


def _split_wait(name, sems, arrays, pairs_fn, after):
    n = len(arrays)

    def body(*refs):
        for send, arrival in pairs_fn(refs[:n], refs[n], refs[n + 1]):
            send.wait_send()
            arrival.wait_recv()

    return list(_pallas_call(
        body, name=name,
        in_specs=[HBM] * n + [SEM, SEM, ANY], out_specs=[HBM] * n,
        out_shape=[pltpu.HBM(a.shape, a.dtype) for a in arrays],
        input_output_aliases={i: i for i in range(n)},
        compiler_params=pltpu.CompilerParams(has_side_effects=EFFECT),
    )(*arrays, sems[0], sems[1], after))


def _forward_pairs(bufs, send_sems, recv_sems):
    x, y, c, _, chips = _place()
    pairs = []
    for w, buf in enumerate(bufs):
        r2 = buf.shape[1] // 2
        for j, (cx, cy) in enumerate(chips):
            landed, theirs = _half(buf, 2 * cx + cy, c, r2), _half(buf, 2 * cx + cy, 1 - c, r2)
            sems = (send_sems.at[w * 3 + j], recv_sems.at[w * 3 + j])
            pairs.append((_remote(landed, landed, *sems, (x, y, 1 - c)), _remote(theirs, theirs, *sems, (x, y, c))))
    return pairs


def _sibling_pairs(arrays, send_sems, recv_sems):
    x, y, c, _, _ = _place()
    n = len(arrays) // 2
    pairs = []
    for w in range(n):
        r2 = arrays[w].shape[1] // 2
        cp = _remote(_half(arrays[w], slice(None), 1 - c, r2), arrays[n + w], send_sems.at[w], recv_sems.at[w],
                     (x, y, 1 - c))
        pairs.append((cp, cp))
    return pairs


def _ici_pairs(arrays, send_sems, recv_sems):
    x, y, c, _, chips = _place()
    n = len(arrays) // 2
    pairs = []
    for w in range(n):
        for j, (cx, cy) in enumerate(chips):
            cp = _remote(arrays[w].at[2 * cx + cy], arrays[n + w].at[j],
                         send_sems.at[w * 3 + j], recv_sems.at[w * 3 + j], (cx, cy, c))
            pairs.append((cp, cp))
    return pairs


def _rs_update(name, partials, received, ws, ms, vs, place_idx):
    n = len(ws)
    _, r2, cols = partials[0].shape
    nb = 2
    tr = r2 // nb

    def body(idx_ref, *refs):
        ins, outs = refs[:5 * n], refs[5 * n:]
        for j in range(n):
            p_ref, r_ref, w_ref, m_ref, v_ref = ins[5 * j:5 * j + 5]
            g_ref, d_ref, mo_ref, vo_ref = outs[4 * j:4 * j + 4]
            g = ((p_ref[...].astype(F32) + r_ref[0].astype(F32))
                 + (r_ref[1].astype(F32) + r_ref[2].astype(F32)))
            g_ref[...] = g
            d_ref[...], mo_ref[...], vo_ref[...] = _adamw_math(w_ref[...], g, m_ref[...], v_ref[...])

    mine = pl.BlockSpec((tr, cols), lambda i, idx: (idx[1] * nb + i, 0))
    operands, in_specs = [], []
    for j in range(n):
        operands += [partials[j], received[j], ws[j], ms[j], vs[j]]
        in_specs += [pl.BlockSpec((None, tr, cols), lambda i, idx: (idx[0], i, 0)),
                     pl.BlockSpec((3, tr, cols), lambda i, idx: (0, i, 0)), mine, mine, mine]
    return _pallas_call(
        body, name=name,
        grid_spec=pltpu.PrefetchScalarGridSpec(num_scalar_prefetch=1, grid=(nb,), in_specs=in_specs,
                                               out_specs=[mine] * (4 * n)),
        out_shape=[_sds((2 * r2, cols), F32)] * (4 * n),
        compiler_params=_params(("parallel",), 56),
    )(place_idx, *operands)


def _share_pairs(arrays, send_sems, recv_sems):
    x, y, c, _, _ = _place()
    pairs = []
    for w, arr in enumerate(arrays):
        r2 = arr.shape[0] // 2
        mine = arr.at[pl.ds(pl.multiple_of(c * r2, 8), r2), :]
        theirs = arr.at[pl.ds(pl.multiple_of((1 - c) * r2, 8), r2), :]
        sems = (send_sems.at[w], recv_sems.at[w])
        pairs.append((_remote(mine, mine, *sems, (x, y, 1 - c)), _remote(theirs, theirs, *sems, (x, y, c))))
    return pairs


def _rs_share(name, shards):
    n = len(shards)

    def body(*refs):
        outs = refs[n:2 * n]
        send_sems, recv_sems = refs[2 * n:]
        x, y, c, _, _ = _place()
        sends = []
        for w in range(n):
            r2 = outs[w].shape[0] // 2
            mine = outs[w].at[pl.ds(pl.multiple_of(c * r2, 8), r2), :]
            sends.append(_remote(mine, mine, send_sems.at[w], recv_sems.at[w], (x, y, 1 - c)))
        for cp in sends:
            cp.start()
        for w in range(n):
            r2 = outs[w].shape[0] // 2
            theirs = outs[w].at[pl.ds(pl.multiple_of((1 - c) * r2, 8), r2), :]
            _remote(theirs, theirs, send_sems.at[w], recv_sems.at[w], (x, y, c)).wait_recv()
        for cp in sends:
            cp.wait_send()

    return _pallas_call(
        body, name=name,
        in_specs=[ANY] * n, out_specs=[ANY] * n,
        out_shape=[_sds(s.shape, F32) for s in shards],
        input_output_aliases={i: i for i in range(n)},
        scratch_shapes=[pltpu.SemaphoreType.DMA((n,)), pltpu.SemaphoreType.DMA((n,))],
    )(*shards)


def _small_allreduce(red_mix, red_ffn, red_final, red_hg, g_conv):
    rows = N_SMALL_ROWS
    D = red_mix.shape[1]
    H = red_hg.shape[1]

    def body(mix_ref, ffn_ref, fin_ref, hg_ref, cv_ref, sum_ref, all_ref, in_ref, send_sems, recv_sems):
        in_ref[...] = jnp.zeros_like(in_ref)
        in_ref[0:1, :] = mix_ref[0:1, :]
        in_ref[1:2, :] = ffn_ref[0:1, :]
        in_ref[2:3, :] = fin_ref[0:1, :]
        gam = hg_ref[1:2, 0:HEAD_DIM]
        for h in range(1, H // HEAD_DIM):
            gam = gam + hg_ref[1:2, h * HEAD_DIM:(h + 1) * HEAD_DIM]
        in_ref[3:4, 0:HEAD_DIM] = gam
        in_ref[3:4, HEAD_DIM:2 * HEAD_DIM] = fin_ref[1:2, 0:HEAD_DIM]
        in_ref[4:5, 0:H] = hg_ref[0:1, :]
        in_ref[6:9, 0:H] = cv_ref[...]
        x, y, c, _, _ = _place()
        me = 4 * x + 2 * y + c
        all_ref[me] = in_ref[...]
        copies = []
        for m in range(1, 8):
            mx, my, mc = (m >> 2) & 1, (m >> 1) & 1, m & 1
            px, py, pc = x ^ mx, y ^ my, c ^ mc
            copies.append((_remote(in_ref, all_ref.at[me], send_sems.at[m - 1], recv_sems.at[m - 1],
                                   (px, py, pc)), 4 * px + 2 * py + pc, m))
        for cp, _, _ in copies:
            cp.start()
        for _, peer, m in copies:
            _remote(in_ref, all_ref.at[peer], send_sems.at[m - 1], recv_sems.at[m - 1],
                    (x, y, c)).wait_recv()
        for cp, _, _ in copies:
            cp.wait_send()
        total = all_ref[0]
        for d in range(1, 8):
            total = total + all_ref[d]
        sum_ref[...] = total

    return _pallas_call(
        body, name="small_allreduce", pin=False,
        in_specs=[VMEM] * 5, out_specs=[VMEM, VMEM],
        out_shape=[_sds((rows, D), F32), _sds((8, rows, D), F32)],
        scratch_shapes=[pltpu.VMEM((rows, D), F32), pltpu.SemaphoreType.DMA((7,)),
                        pltpu.SemaphoreType.DMA((7,))],
    )(red_mix, red_ffn, red_final, red_hg, g_conv)[0]


def _adamw_math(w, g, m, v):
    m = ADAM_B1 * m + (1.0 - ADAM_B1) * g
    v = ADAM_B2 * v + (1.0 - ADAM_B2) * jnp.square(g)
    m_hat = m / (1.0 - ADAM_B1 ** ADAM_STEP)
    v_hat = v / (1.0 - ADAM_B2 ** ADAM_STEP)
    delta = -ADAM_LR * (m_hat / (jnp.sqrt(v_hat) + ADAM_EPS) + ADAM_WD * w)
    return delta, m, v


def _small_update(total, chip_idx, ws, ms, vs):
    n = len(ws)
    H = ws[1].shape[1]

    def body(idx_ref, tot_ref, *refs):
        w, m, v, outs = refs[:n], refs[n:2 * n], refs[2 * n:3 * n], refs[3 * n:]
        chip = idx_ref[0]
        p0 = _lower_bound(w[1][...])
        dl0 = p0 * (1.0 - p0) * tot_ref[4:5, 0:H]
        conv = jnp.zeros((3, LANES), F32)
        for k in range(N_CHIPS):
            conv = jnp.where(chip == k, tot_ref[6:9, k * LANES:(k + 1) * LANES], conv)
        grads = [tot_ref[0:1, :], None, tot_ref[3:4, 0:HEAD_DIM], conv, tot_ref[1:2, :], tot_ref[2:3, :]]
        for p in range(n):
            g_ref, d_ref, mo_ref, vo_ref = outs[4 * p:4 * p + 4]
            if p == 1:
                for row, g in ((slice(0, 1), dl0), (slice(1, 2), -dl0)):
                    g_ref[row, :] = g
                    d_ref[row, :], mo_ref[row, :], vo_ref[row, :] = _adamw_math(
                        w[p][row, :], g, m[p][row, :], v[p][row, :])
            else:
                g_ref[...] = grads[p]
                d_ref[...], mo_ref[...], vo_ref[...] = _adamw_math(w[p][...], grads[p], m[p][...], v[p][...])
        outs[4 * n][...] = tot_ref[3:4, HEAD_DIM:2 * HEAD_DIM]

    full = lambda a: pl.BlockSpec(a.shape, lambda i, idx: (0,) * a.ndim)
    out_shape = [_sds(w.shape, F32) for w in ws for _ in range(4)] + [_sds((1, LANES), F32)]
    return _pallas_call(
        body, name="small_update",
        grid_spec=pltpu.PrefetchScalarGridSpec(
            num_scalar_prefetch=1, grid=(1,),
            in_specs=[full(total)] + [full(a) for a in ws + ms + vs],
            out_specs=[full(s) for s in out_shape]),
        out_shape=out_shape,
    )(chip_idx, total, *ws, *ms, *vs)


def kernel(x, norm_mix_g, w_in, lower_bounds, hg_norm_g, conv_w, w_branch_a, w_branch_b, w_out, norm_ffn_g, w_ffn_gate, w_ffn_up, w_ffn_down, norm_final_g, loss_target, m_norm_mix_g, m_w_in, m_lower_bounds, m_hg_norm_g, m_conv_w, m_w_branch_a, m_w_branch_b, m_w_out, m_norm_ffn_g, m_w_ffn_gate, m_w_ffn_up, m_w_ffn_down, m_norm_final_g, v_norm_mix_g, v_w_in, v_lower_bounds, v_hg_norm_g, v_conv_w, v_w_branch_a, v_w_branch_b, v_w_out, v_norm_ffn_g, v_w_ffn_gate, v_w_ffn_up, v_w_ffn_down, v_norm_final_g):
    _, L, D = x.shape
    H = D // 2
    assert lower_bounds.shape == (2, H) and hg_norm_g.shape == (1, HEAD_DIM)
    assert conv_w.shape == (1, 3, LANES) and w_in.shape[2] * N_CHIPS == 11 * H
    x2d, target = x.reshape(L, D), loss_target.reshape(L, D)
    g_final = norm_final_g.reshape(1, D)
    chip = 2 * lax.axis_index("x") + lax.axis_index("y")
    core = lax.axis_index("c")

    tr = lambda w: jnp.transpose(w[0])
    big = [w_in[0], w_branch_a[0], w_branch_b[0], w_out[0], tr(w_ffn_gate), tr(w_ffn_up), w_ffn_down[0]]
    big_m = [m_w_in[0], m_w_branch_a[0], m_w_branch_b[0], m_w_out[0], tr(m_w_ffn_gate), tr(m_w_ffn_up),
             m_w_ffn_down[0]]
    big_v = [v_w_in[0], v_w_branch_a[0], v_w_branch_b[0], v_w_out[0], tr(v_w_ffn_gate), tr(v_w_ffn_up),
             v_w_ffn_down[0]]
    names = ["w_in", "w_branch_a", "w_branch_b", "w_out", "w_ffn_gate", "w_ffn_up", "w_ffn_down"]

    chip_idx = chip.reshape(1).astype(jnp.int32)
    placed = [(_cast_place_t if j < 3 else _cast_place)("place_" + nm, w, chip_idx)
              for j, (nm, w) in enumerate(zip(names, big))]
    conv_placed = lax.dynamic_update_slice(jnp.zeros((N_CHIPS, 3, LANES), F32), conv_w, (chip, 0, 0))
    sems, in_flight, token = _gather_start([([placed[0], conv_placed], {1}), (placed[1:4], set()),
                                            (placed[4:], set())])
    w_in_landed, conv_all = _gather_wait("gather_wait_in", in_flight[0], {1}, sems[0], token)
    (w_int3,) = _gather_forward("gather_fwd_in", [w_in_landed])
    w_int = w_int3.reshape(-1, D)
    conv_full = jnp.transpose(conv_all, (1, 0, 2)).reshape(3, H)

    h, proj = _fwd_proj(x2d, norm_mix_g, w_int3)
    og, o_pre, s_saved = _hgrn_fwd(proj, lower_bounds, hg_norm_g, H)
    landed = _gather_wait("gather_wait_mix", in_flight[1], set(), sems[1], og)
    fwd_sems, landed, token = _split_start("gather_fwd_mix_start", landed, 9, _forward_pairs)
    cb = _conv_fwd(proj, conv_full, H, token)
    wat3, wbt3, wout3 = _split_wait("gather_fwd_mix_wait", fwd_sems, landed, _forward_pairs, cb)
    wat, wbt, wout = wat3.reshape(D, H), wbt3.reshape(D, H), wout3.reshape(D, D)
    landed = _gather_wait("gather_wait_ffn", in_flight[2], set(), sems[2], cb)
    fwd_sems, landed, token = _split_start("gather_fwd_ffn_start", landed, 9, _forward_pairs)
    ya, yb, merged, x1, h2 = _fwd_mix(og, cb, proj, x2d, wat, wbt, wout, norm_ffn_g, H, token)
    wgt3, wut3, wd3 = _split_wait("gather_fwd_ffn_wait", fwd_sems, landed, _forward_pairs, h2)
    d_ff = N_CHIPS * wd3.shape[1]
    wgt, wut, wd = wgt3.reshape(d_ff, D), wut3.reshape(d_ff, D), wd3.reshape(d_ff, D)
    ffn_a, ffn_b, ffn_s = _fwd_ffn_up(h2, wgt, wut)
    dx2, dx2b, red_final = _fwd_down_loss(ffn_s, wd, x1, target, g_final)

    c_idx = core.reshape(1).astype(jnp.int32)
    place_idx = jnp.stack([chip, core]).astype(jnp.int32)

    def sibling_start(tag, grads):
        bufs = [lax.empty((N_CHIPS, g.shape[1] // 2, g.shape[2]), F32) for g in grads]
        return _split_start("rs_sibling_start_" + tag, list(grads) + bufs, len(grads), _sibling_pairs)

    def ici_start(tag, js, grads, from_sibling):
        partials = [_rs_add("rs_add_" + names[j], g, s, c_idx) for j, g, s in zip(js, grads, from_sibling)]
        landings = [lax.empty((3,) + p.shape[1:], BF16) for p in partials]
        return _split_start("rs_ici_start_" + tag, partials + landings, 3 * len(js), _ici_pairs)

    def ici_start_behind(tag, js, started, after):
        n = len(js)
        arrays = _split_wait("rs_sibling_wait_" + tag, started[0], started[1], _sibling_pairs, after)
        return ici_start(tag, js, arrays[:n], arrays[n:])

    def rs_update(tag, js, started, after):
        n = len(js)
        arrays = _split_wait("rs_ici_wait_" + tag, started[0], started[1], _ici_pairs, after)
        by_shape = {}
        for pos in range(n):
            by_shape.setdefault(arrays[pos].shape, []).append(pos)
        results = [None] * n
        for same in by_shape.values():
            pick = lambda xs: [xs[p] for p in same]
            res = _rs_update("rs_update_" + names[js[same[0]]], pick(arrays[:n]), pick(arrays[n:]),
                             *[[src[js[p]] for p in same] for src in (big, big_m, big_v)], place_idx)
            for q, p in enumerate(same):
                results[p] = list(res[4 * q:4 * q + 4])
        return [a for r in results for a in r]

    shards3 = lambda g: g.reshape(N_CHIPS, d_ff // N_CHIPS, D)
    da, db = _bwd_down(dx2b, wd, ffn_a, ffn_b)
    g_wd = shards3(_dw_rows2("dw_ffn_down", ffn_s, dx2b))
    g_wg = shards3(_dw_rows2("dw_ffn_gate", da, h2))
    g_wu = shards3(_dw_rows2("dw_ffn_up", db, h2))
    ffn_sibling = sibling_start("ffn", [g_wg, g_wu, g_wd])
    dx1, dx1b, red_ffn = _bwd_ffn_dh(da, db, wgt, wut, x1, dx2, norm_ffn_g, ffn_sibling[2])
    ffn_ici = ici_start_behind("ffn", [4, 5, 6], ffn_sibling, dx1b)
    dya, dyb, dga, dgb, d_o, d_cb = _bwd_mix(dx1b, proj, ya, yb, wat, wbt, wout, H, ffn_ici[2])
    g_wout = _dw_rows("dw_out", merged, dx1b)
    g_wa = _dw_cols("dw_branch_a", og, dya, D // N_CHIPS)
    g_wb = _dw_cols("dw_branch_b", cb, dyb, D // N_CHIPS)
    mix_sibling = sibling_start("mix", [g_wa, g_wb, g_wout])
    dq, df, dv, dg, red_hg = _hgrn_bwd(proj, lower_bounds, hg_norm_g, o_pre, d_o, s_saved, H, mix_sibling[2])
    mix_ici = ici_start_behind("mix", [1, 2, 3], mix_sibling, dq)
    dcg, dbg, dxb, g_conv = _conv_bwd(proj, conv_full, d_cb, H, mix_ici[2])
    dproj = jnp.concatenate([dq, df, dv, dg, dcg, dbg, dxb, dga, dgb], axis=1)
    g_win = _dw_cols("dw_in", h, dproj, w_int3.shape[1])
    in_sibling = sibling_start("in", [g_win])
    halves = (rs_update("mix", [1, 2, 3], mix_ici, in_sibling[2])
              + rs_update("ffn", [4, 5, 6], ffn_ici, in_sibling[2]))
    share = _split_start("rs_share_start_rest", halves, len(halves), _share_pairs)
    in_ici = ici_start_behind("in", [0], in_sibling, share[2])
    grad_x, red_mix = _bwd_in(dproj, w_int, x2d, dx1, norm_mix_g, in_ici[2])
    rest_out = _split_wait("rs_share_wait_rest", share[0], share[1], _share_pairs, grad_x)
    in_out = list(_rs_share("rs_share_in", rs_update("in", [0], in_ici, grad_x)))
    big_out = [in_out] + [rest_out[4 * q:4 * q + 4] for q in range(6)]

    total = _small_allreduce(red_mix, red_ffn, red_final, red_hg, g_conv)

    def smalls(mix, lb, hg, cw, ffn, fin):
        return [mix, lb, hg, cw[0], ffn, fin.reshape(1, D)]

    small_out = _small_update(
        total, chip_idx,
        smalls(norm_mix_g, lower_bounds, hg_norm_g, conv_w, norm_ffn_g, norm_final_g),
        smalls(m_norm_mix_g, m_lower_bounds, m_hg_norm_g, m_conv_w, m_norm_ffn_g, m_norm_final_g),
        smalls(v_norm_mix_g, v_lower_bounds, v_hg_norm_g, v_conv_w, v_norm_ffn_g, v_norm_final_g))

    def outputs(i):
        big_i = [big_out[j][i] for j in range(7)]
        mix, lb, hg, cw, ffn, fin = [small_out[4 * p + i] for p in range(6)]
        return [mix, big_i[0][None], lb, hg, cw[None], big_i[1][None], big_i[2][None], big_i[3][None], ffn,
                big_i[4].T[None], big_i[5].T[None], big_i[6][None], fin.reshape(D)]

    outs = [small_out[24][0, 0], grad_x.reshape(1, L, D)]
    for i in range(4):
        outs += outputs(i)
    return tuple(outs)
```

```python
import functools

import jax
import jax.numpy as jnp
from jax import lax
from jax.experimental import pallas as pl
from jax.experimental.pallas import tpu as pltpu

F32 = jnp.float32
BF16 = jnp.bfloat16
EPS = 1e-6
CHUNK = 32
HEAD_DIM = 128
LANES = 128
N_CHIPS = 4
N_SMALL_ROWS = 16

ADAM_LR = 0.001
ADAM_B1 = 0.9
ADAM_B2 = 0.999
ADAM_EPS = 1e-08
ADAM_WD = 0.01
ADAM_STEP = 10

MESH = pl.DeviceIdType.MESH
ANY = pl.BlockSpec(memory_space=pl.ANY)
VMEM = pl.BlockSpec(memory_space=pltpu.VMEM)
HBM = pl.BlockSpec(memory_space=pltpu.HBM)
SEM = pl.BlockSpec(memory_space=pltpu.SEMAPHORE)
EFFECT = pltpu.SideEffectType.DATAFLOW_SIDE_EFFECTING


def _sds(shape, dtype):
    return jax.ShapeDtypeStruct(shape, dtype)


def _pallas_call(body, pin=True, **kwargs):
    if not pin:
        return pl.pallas_call(body, **kwargs)
    in_hbm = lambda s: pltpu.HBM(s.shape, s.dtype) if isinstance(s, jax.ShapeDtypeStruct) else s
    kwargs["out_shape"] = jax.tree.map(in_hbm, kwargs["out_shape"])
    call = pl.pallas_call(body, **kwargs)

    def run(*args):
        return call(*[pltpu.with_memory_space_constraint(a, pltpu.HBM) if a.dtype in (F32, BF16) else a
                      for a in args])

    return run


def _params(semantics, vmem_mb):
    return pltpu.CompilerParams(dimension_semantics=semantics, vmem_limit_bytes=vmem_mb << 20)


def _nn(a, b):
    return lax.dot_general(a, b, (((1,), (0,)), ((), ())), preferred_element_type=F32)


def _nt(a, b):
    return lax.dot_general(a, b, (((1,), (1,)), ((), ())), preferred_element_type=F32)


def _tn(a, b):
    return lax.dot_general(a, b, (((0,), (0,)), ((), ())), preferred_element_type=F32)


def _sigmoid(x):
    return jax.nn.sigmoid(x)


def _rms_stats(x):
    r = lax.rsqrt(jnp.mean(x * x, axis=-1, keepdims=True) + EPS)
    return r, x * r


def _rms_bwd(dxh, xh, r):
    return r * (dxh - xh * jnp.mean(dxh * xh, axis=-1, keepdims=True))


def _fwd_proj_first(x, g_mix, w_int3, block):
    L, D = x.shape
    tn = w_int3.shape[1]
    tm = min(L, 1024)

    def body(blk_ref, x_ref, g_ref, w_ref, h_ref, p_ref):
        _, xh = _rms_stats(x_ref[...])
        h = (xh * g_ref[...]).astype(BF16)
        h_ref[...] = h
        p_ref[...] = _nt(h, w_ref[...])

    return _pallas_call(
        body, name="fwd_proj_own",
        grid_spec=pltpu.PrefetchScalarGridSpec(
            num_scalar_prefetch=1, grid=(L // tm,),
            in_specs=[pl.BlockSpec((tm, D), lambda i, blk: (i, 0)),
                      pl.BlockSpec((1, D), lambda i, blk: (0, 0)),
                      pl.BlockSpec((None, tn, D), lambda i, blk: (blk[0], 0, 0))],
            out_specs=[pl.BlockSpec((tm, D), lambda i, blk: (i, 0)),
                       pl.BlockSpec((tm, tn), lambda i, blk: (i, blk[0]))]),
        out_shape=[_sds((L, D), BF16), _sds((L, N_CHIPS * tn), F32)],
        compiler_params=_params(("parallel",), 48),
    )(block, x, g_mix, w_int3)


def _fwd_proj_more(name, h, w_int3, proj, blocks):
    L, D = h.shape
    tn = w_int3.shape[1]
    tm = min(L, 1024)

    def body(blk_ref, h_ref, w_ref, proj_ref, p_ref):
        p_ref[...] = _nt(h_ref[...], w_ref[...])

    return _pallas_call(
        body, name=name,
        grid_spec=pltpu.PrefetchScalarGridSpec(
            num_scalar_prefetch=1, grid=(L // tm, blocks.shape[0]),
            in_specs=[pl.BlockSpec((tm, D), lambda i, j, blk: (i, 0)),
                      pl.BlockSpec((None, tn, D), lambda i, j, blk: (blk[j], 0, 0)), ANY],
            out_specs=pl.BlockSpec((tm, tn), lambda i, j, blk: (i, blk[j]))),
        out_shape=_sds(proj.shape, proj.dtype),
        input_output_aliases={3: 0},
        compiler_params=_params(("parallel", "arbitrary"), 48),
    )(blocks, h, w_int3, proj)


def _lower_bound(lbp):
    l0, l1 = lbp[0:1, :], lbp[1:2, :]
    m = jnp.maximum(l0, l1)
    e0, e1 = jnp.exp(l0 - m), jnp.exp(l1 - m)
    return e0 / (e0 + e1)


def _seg_scan(x, r32, forward):
    n = x.shape[0]
    s = 1
    while s < CHUNK:
        if forward:
            x = x + jnp.where(r32 >= s, pltpu.roll(x, s, 0), 0.0)
        else:
            x = x + jnp.where(r32 < CHUNK - s, pltpu.roll(x, n - s, 0), 0.0)
        s *= 2
    return x


def _bcast_row(x, row):
    n, w = x.shape
    nc = n // CHUNK
    x3 = x.reshape(nc, CHUNK, w)
    return jnp.broadcast_to(x3[:, row:row + 1, :], (nc, CHUNK, w)).reshape(n, w)


def _hgrn_prep(q_raw, f_raw, lb):
    r32 = lax.broadcasted_iota(jnp.int32, f_raw.shape, 0) & (CHUNK - 1)
    sig = _sigmoid(f_raw)
    f = lb + (1.0 - lb) * sig
    b = _seg_scan(jnp.log(f), r32, True)
    a = _bcast_row(b, CHUNK // 2 - 1)
    bl = _bcast_row(b, CHUNK - 1)
    sq = _sigmoid(q_raw)
    q = q_raw * sq * (HEAD_DIM ** -0.5)
    return dict(r32=r32, sig=sig, f=f, k=1.0 - f, b=b, a=a, bl=bl, sq=sq, q=q)


def _chunk_masks(n):
    ri = lax.broadcasted_iota(jnp.int32, (n, n), 0)
    ci = lax.broadcasted_iota(jnp.int32, (n, n), 1)
    same = (ri // CHUNK) == (ci // CHUNK)
    return same & (ci <= ri), same & (ri <= ci)


def _hgrn_fwd(proj, lower_bounds, gamma, H):
    L = proj.shape[0]
    nh = H // HEAD_DIM
    TL = min(L, 256)
    nc = TL // CHUNK

    def body(q_ref, f_ref, v_ref, g_ref, lbp_ref, gam_ref, og_ref, o_ref, s_ref, st_ref):
        @pl.when(pl.program_id(0) == 0)
        def _():
            st_ref[...] = jnp.zeros_like(st_ref)

        lb = _lower_bound(lbp_ref[...])
        gam = gam_ref[...]
        mask, _ = _chunk_masks(TL)
        rowc = lax.broadcasted_iota(jnp.int32, (TL, HEAD_DIM), 0) // CHUNK
        for h in range(nh):
            hs = slice(h * HEAD_DIM, (h + 1) * HEAD_DIM)
            p = _hgrn_prep(q_ref[:, hs], f_ref[:, hs], lb[:, hs])
            v = v_ref[:, hs]
            vb = v.astype(BF16)
            vt = v.T.astype(BF16)
            q_hat = (p["q"] * jnp.exp(p["b"] - p["a"])).astype(BF16)
            k_hat = (p["k"] * jnp.exp(p["a"] - p["b"])).astype(BF16)
            q_in = (p["q"] * jnp.exp(p["b"])).astype(BF16)
            k_out = (p["k"] * jnp.exp(p["bl"] - p["b"])).astype(BF16)
            dec = jnp.exp(p["bl"])
            att = jnp.where(mask, _nt(q_hat, k_hat), 0.0).astype(BF16)
            o_intra = _nn(att, vb)
            st = st_ref[h]
            for c in range(nc):
                rs = slice(c * CHUNK, (c + 1) * CHUNK)
                stb = st.astype(BF16)
                s_ref[c, h] = stb
                o_ref[rs, hs] = o_intra[rs] + _nt(q_in[rs], stb)
                k_c = jnp.where(rowc == c, k_out, jnp.zeros_like(k_out))
                st = st * dec[c * CHUNK:c * CHUNK + 1, :] + _nn(vt, k_c)
            st_ref[h] = st
            o = o_ref[:, hs]
            _, xh = _rms_stats(o)
            gr = g_ref[:, hs]
            og_ref[:, hs] = (xh * gam * (gr * _sigmoid(gr))).astype(BF16)

    col = lambda k: pl.BlockSpec((TL, H), lambda i, k=k: (i, k))
    return _pallas_call(
        body, name="hgrn_fwd", grid=(L // TL,),
        in_specs=[col(0), col(1), col(2), col(3),
                  pl.BlockSpec(lower_bounds.shape, lambda i: (0, 0)),
                  pl.BlockSpec(gamma.shape, lambda i: (0, 0))],
        out_specs=[pl.BlockSpec((TL, H), lambda i: (i, 0)),
                   pl.BlockSpec((TL, H), lambda i: (i, 0)),
                   pl.BlockSpec((nc, nh, HEAD_DIM, HEAD_DIM), lambda i: (i, 0, 0, 0))],
        out_shape=[_sds((L, H), BF16), _sds((L, H), F32),
                   _sds((L // CHUNK, nh, HEAD_DIM, HEAD_DIM), BF16)],
        scratch_shapes=[pltpu.VMEM((nh, HEAD_DIM, HEAD_DIM), F32)],
        compiler_params=_params(("arbitrary",), 48),
    )(proj, proj, proj, proj, lower_bounds, gamma)


def _hgrn_bwd(proj, lower_bounds, gamma, o_pre, d_out, s_saved, H, after):
    L = proj.shape[0]
    nh = H // HEAD_DIM
    TL = min(L, 256)
    nc = TL // CHUNK
    nt = L // TL

    def body(q_ref, f_ref, v_ref, g_ref, lbp_ref, gam_ref, o_ref, d_ref, s_ref, after_ref,
             dq_ref, df_ref, dv_ref, dg_ref, red_ref, dst_ref, dsall_ref, tmp_ref):
        @pl.when(pl.program_id(0) == 0)
        def _():
            dst_ref[...] = jnp.zeros_like(dst_ref)
            red_ref[...] = jnp.zeros_like(red_ref)

        lb = _lower_bound(lbp_ref[...])
        gam = gam_ref[...]
        mask, mask_t = _chunk_masks(TL)
        rowc = lax.broadcasted_iota(jnp.int32, (TL, HEAD_DIM), 0) // CHUNK
        for h in range(nh):
            hs = slice(h * HEAD_DIM, (h + 1) * HEAD_DIM)
            qr, gr, lbh = q_ref[:, hs], g_ref[:, hs], lb[:, hs]
            p = _hgrn_prep(qr, f_ref[:, hs], lbh)
            vb = v_ref[:, hs].astype(BF16)
            eba, eab = jnp.exp(p["b"] - p["a"]), jnp.exp(p["a"] - p["b"])
            eb, elb = jnp.exp(p["b"]), jnp.exp(p["bl"] - p["b"])
            dec = jnp.exp(p["bl"])
            q_hat, k_hat = p["q"] * eba, p["k"] * eab
            q_in, k_out = p["q"] * eb, p["k"] * elb
            q_hat_b, k_hat_b = q_hat.astype(BF16), k_hat.astype(BF16)
            q_in_b, k_out_b = q_in.astype(BF16), k_out.astype(BF16)

            o, dout = o_ref[:, hs], d_ref[:, hs]
            sg = _sigmoid(gr)
            r, xh = _rms_stats(o)
            dg_ref[:, hs] = (dout * (xh * gam) * (sg * (1.0 + gr * (1.0 - sg)))).astype(BF16)
            dn = dout * (gr * sg)
            red_ref[1:2, hs] += jnp.sum(dn * xh, axis=0, keepdims=True)
            do = _rms_bwd(dn * gam, xh, r)
            dob = do.astype(BF16)
            dot_b = do.T.astype(BF16)

            att_t = jnp.where(mask_t, _nt(k_hat_b, q_hat_b), 0.0).astype(BF16)
            dv_intra = _nn(att_t, dob)
            datt = jnp.where(mask, _nt(dob, vb), 0.0).astype(BF16)
            dqh = _nn(datt, k_hat_b)
            datt_t = jnp.where(mask_t, _nt(vb, dob), 0.0).astype(BF16)
            dkh = _nn(datt_t, q_hat_b)

            dst = dst_ref[h]
            for c in reversed(range(nc)):
                dsall_ref[c] = dst
                q_c = jnp.where(rowc == c, q_in_b, jnp.zeros_like(q_in_b))
                dst = dst * dec[c * CHUNK:c * CHUNK + 1, :] + _nn(dot_b, q_c)
            dst_ref[h] = dst
            for c in range(nc):
                rs = slice(c * CHUNK, (c + 1) * CHUNK)
                ds_c = dsall_ref[c]
                dsb = ds_c.astype(BF16)
                st_prev = s_ref[c, h]
                tmp_ref[0, rs, :] = _nt(k_out_b[rs], dsb)
                tmp_ref[1, rs, :] = _nn(vb[rs], dsb)
                tmp_ref[2, rs, :] = _nn(dob[rs], st_prev)
                ddec = jnp.sum(ds_c * st_prev.astype(F32), axis=0, keepdims=True)
                tmp_ref[3, rs, :] = jnp.broadcast_to(ddec * dec[c * CHUNK:c * CHUNK + 1, :],
                                                     (CHUNK, HEAD_DIM))
            dko, dqi = tmp_ref[1], tmp_ref[2]
            dq = dqh * eba + dqi * eb
            dk = dkh * eab + dko * elb
            tko = dko * k_out
            db = dqh * q_hat - dkh * k_hat + dqi * q_in - tko
            dlog = (_seg_scan(db, p["r32"], False)
                    + _bcast_row(_seg_scan(tko, p["r32"], True), CHUNK - 1) + tmp_ref[3])
            df = dlog / p["f"] - dk
            sig = p["sig"]
            red_ref[0:1, hs] += jnp.sum(df * (1.0 - sig), axis=0, keepdims=True)
            df_ref[:, hs] = (df * (1.0 - lbh) * sig * (1.0 - sig)).astype(BF16)
            sq = p["sq"]
            dq_ref[:, hs] = (dq * (HEAD_DIM ** -0.5) * (sq * (1.0 + qr * (1.0 - sq)))).astype(BF16)
            dv_ref[:, hs] = (dv_intra + tmp_ref[0]).astype(BF16)

    col = lambda k: pl.BlockSpec((TL, H), lambda i, k=k: (nt - 1 - i, k))
    rev = pl.BlockSpec((TL, H), lambda i: (nt - 1 - i, 0))
    return _pallas_call(
        body, name="hgrn_bwd", grid=(nt,),
        in_specs=[col(0), col(1), col(2), col(3),
                  pl.BlockSpec(lower_bounds.shape, lambda i: (0, 0)),
                  pl.BlockSpec(gamma.shape, lambda i: (0, 0)),
                  rev, rev,
                  pl.BlockSpec((nc, nh, HEAD_DIM, HEAD_DIM), lambda i: (nt - 1 - i, 0, 0, 0)), ANY],
        out_specs=[rev, rev, rev, rev, pl.BlockSpec((8, H), lambda i: (0, 0))],
        out_shape=[_sds((L, H), BF16)] * 4 + [_sds((8, H), F32)],
        scratch_shapes=[pltpu.VMEM((nh, HEAD_DIM, HEAD_DIM), F32),
                        pltpu.VMEM((nc, HEAD_DIM, HEAD_DIM), F32),
                        pltpu.VMEM((4, TL, HEAD_DIM), F32)],
        compiler_params=_params(("arbitrary",), 48),
    )(proj, proj, proj, proj, lower_bounds, gamma, o_pre, d_out, s_saved, after)


def _shift_down(u, s, row):
    return jnp.where(row >= s, pltpu.roll(u, s, 0), 0.0)


def _shift_up(u, s, row):
    n = u.shape[0]
    return jnp.where(row < n - s, pltpu.roll(u, n - s, 0), 0.0)


def _conv_specs(L, H):
    per = H // LANES
    return [pl.BlockSpec((L, LANES), lambda j, o=o: (0, o * per + j)) for o in (4, 5, 6)]


def _conv_fwd(proj, conv_w, H, after):
    L = proj.shape[0]

    def body(c_ref, b_ref, x_ref, w_ref, after_ref, o_ref):
        row = lax.broadcasted_iota(jnp.int32, (L, LANES), 0)
        u = c_ref[...] * x_ref[...]
        w = w_ref[...]
        y = w[0:1] * _shift_down(u, 2, row) + w[1:2] * _shift_down(u, 1, row) + w[2:3] * u
        o_ref[...] = (b_ref[...] * y).astype(BF16)

    return _pallas_call(
        body, name="conv_fwd", grid=(H // LANES,),
        in_specs=_conv_specs(L, H) + [pl.BlockSpec((3, LANES), lambda j: (0, j)), ANY],
        out_specs=pl.BlockSpec((L, LANES), lambda j: (0, j)),
        out_shape=_sds((L, H), BF16),
        compiler_params=_params(("parallel",), 48),
    )(proj, proj, proj, conv_w, after)


def _conv_bwd(proj, conv_w, dcb, H, after):
    L = proj.shape[0]

    def body(c_ref, b_ref, x_ref, w_ref, d_ref, after_ref, dc_ref, db_ref, dx_ref, dw_ref):
        row = lax.broadcasted_iota(jnp.int32, (L, LANES), 0)
        cg, xb = c_ref[...], x_ref[...]
        u = cg * xb
        u1, u2 = _shift_down(u, 1, row), _shift_down(u, 2, row)
        w = w_ref[...]
        y = w[0:1] * u2 + w[1:2] * u1 + w[2:3] * u
        d = d_ref[...]
        db_ref[...] = (d * y).astype(BF16)
        dy = d * b_ref[...]
        du = w[2:3] * dy + w[1:2] * _shift_up(dy, 1, row) + w[0:1] * _shift_up(dy, 2, row)
        dw_ref[0:1, :] = jnp.sum(dy * u2, axis=0, keepdims=True)
        dw_ref[1:2, :] = jnp.sum(dy * u1, axis=0, keepdims=True)
        dw_ref[2:3, :] = jnp.sum(dy * u, axis=0, keepdims=True)
        dc_ref[...] = (du * xb).astype(BF16)
        dx_ref[...] = (du * cg).astype(BF16)

    blk = pl.BlockSpec((L, LANES), lambda j: (0, j))
    return _pallas_call(
        body, name="conv_bwd", grid=(H // LANES,),
        in_specs=_conv_specs(L, H) + [pl.BlockSpec((3, LANES), lambda j: (0, j)), blk, ANY],
        out_specs=[blk, blk, blk, pl.BlockSpec((3, LANES), lambda j: (0, j))],
        out_shape=[_sds((L, H), BF16)] * 3 + [_sds((3, H), F32)],
        compiler_params=_params(("parallel",), 56),
    )(proj, proj, proj, conv_w, dcb, after)


def _gate_specs(tm, H):
    return [pl.BlockSpec((tm, H), lambda i, k=k: (i, k)) for k in (7, 8, 9, 10)]


def _fwd_mix(og, cb, proj, x, wat, wbt, wout, g_ffn, H, after):
    L, D = x.shape
    tm = min(L, 512)

    def body(o_ref, cb_ref, ga0, ga1, gb0, gb1, x_ref, wa_ref, wb_ref, wo_ref, g_ref, after_ref,
             ya_ref, yb_ref, m_ref, x1_ref, h2_ref):
        ya, yb = _nt(o_ref[...], wa_ref[...]), _nt(cb_ref[...], wb_ref[...])
        ya_ref[...] = ya.astype(BF16)
        yb_ref[...] = yb.astype(BF16)
        for k, (gar, gbr) in enumerate(((ga0, gb0), (ga1, gb1))):
            cs = slice(k * H, (k + 1) * H)
            m_ref[:, cs] = (_sigmoid(gar[...]) * ya[:, cs] + _sigmoid(gbr[...]) * yb[:, cs]).astype(BF16)
        x1 = x_ref[...] + _nn(m_ref[...], wo_ref[...])
        x1_ref[...] = x1
        _, xh = _rms_stats(x1)
        h2_ref[...] = (xh * g_ref[...]).astype(BF16)

    row = lambda w: pl.BlockSpec((tm, w), lambda i: (i, 0))
    full = lambda a: pl.BlockSpec(a.shape, lambda i: (0,) * a.ndim)
    return _pallas_call(
        body, name="fwd_mix", grid=(L // tm,),
        in_specs=[row(H), row(H)] + _gate_specs(tm, H) + [row(D), full(wat), full(wbt), full(wout),
                                                           full(g_ffn), ANY],
        out_specs=[row(D)] * 5,
        out_shape=[_sds((L, D), BF16)] * 3 + [_sds((L, D), F32), _sds((L, D), BF16)],
        compiler_params=_params(("parallel",), 56),
    )(og, cb, proj, proj, proj, proj, x, wat, wbt, wout, g_ffn, after)


def _bwd_mix(dx1b, proj, ya, yb, wat, wbt, wout, H, after):
    L, D = dx1b.shape
    tm = min(L, 512)

    def body(dx_ref, ga0, ga1, gb0, gb1, ya_ref, yb_ref, wa_ref, wb_ref, wo_ref, after_ref,
             dya_ref, dyb_ref, dga_ref, dgb_ref, do_ref, dcb_ref):
        dm = _nt(dx_ref[...], wo_ref[...])
        for k, (gar, gbr) in enumerate(((ga0, gb0), (ga1, gb1))):
            cs = slice(k * H, (k + 1) * H)
            sa, sb = _sigmoid(gar[...]), _sigmoid(gbr[...])
            dmk = dm[:, cs]
            dga_ref[:, cs] = (dmk * ya_ref[:, cs].astype(F32) * sa * (1.0 - sa)).astype(BF16)
            dgb_ref[:, cs] = (dmk * yb_ref[:, cs].astype(F32) * sb * (1.0 - sb)).astype(BF16)
            dya_ref[:, cs] = (dmk * sa).astype(BF16)
            dyb_ref[:, cs] = (dmk * sb).astype(BF16)
        do_ref[...] = _nn(dya_ref[...], wa_ref[...])
        dcb_ref[...] = _nn(dyb_ref[...], wb_ref[...])

    row = lambda w: pl.BlockSpec((tm, w), lambda i: (i, 0))
    full = lambda a: pl.BlockSpec(a.shape, lambda i: (0,) * a.ndim)
    return _pallas_call(
        body, name="bwd_mix", grid=(L // tm,),
        in_specs=[row(D)] + _gate_specs(tm, H) + [row(D), row(D), full(wat), full(wbt), full(wout), ANY],
        out_specs=[row(D)] * 4 + [row(H)] * 2,
        out_shape=[_sds((L, D), BF16)] * 4 + [_sds((L, H), F32)] * 2,
        compiler_params=_params(("parallel",), 56),
    )(dx1b, proj, proj, proj, proj, ya, yb, wat, wbt, wout, after)


def _fwd_ffn_up(h2, wgt, wut):
    L, D = h2.shape
    F = wgt.shape[0]
    tn = F // 2
    tm = min(L, 512)

    def body(h_ref, wg_ref, wu_ref, a_ref, b_ref, s_ref):
        h = h_ref[...]
        a, b = _nt(h, wg_ref[...]), _nt(h, wu_ref[...])
        a_ref[...] = a.astype(BF16)
        b_ref[...] = b.astype(BF16)
        s_ref[...] = (a * _sigmoid(a) * b).astype(BF16)

    wspec = pl.BlockSpec((tn, D), lambda j, i: (j, 0))
    ospec = pl.BlockSpec((tm, tn), lambda j, i: (i, j))
    return _pallas_call(
        body, name="fwd_ffn_up", grid=(2, L // tm),
        in_specs=[pl.BlockSpec((tm, D), lambda j, i: (i, 0)), wspec, wspec],
        out_specs=[ospec] * 3,
        out_shape=[_sds((L, F), BF16)] * 3,
        compiler_params=_params(("parallel", "parallel"), 48),
    )(h2, wgt, wut)


def _fwd_down_loss(s, wd, x1, target, g_final):
    L, D = x1.shape
    F = wd.shape[0]
    tm = min(L, 256)

    def body(s_ref, wd_ref, x1_ref, t_ref, g_ref, dx_ref, dxb_ref, red_ref):
        @pl.when(pl.program_id(0) == 0)
        def _():
            red_ref[...] = jnp.zeros_like(red_ref)

        g = g_ref[...]
        r, xh = _rms_stats(x1_ref[...] + _nn(s_ref[...], wd_ref[...]))
        e = xh * g - t_ref[...]
        dy = e * (1.0 / D)
        dx = _rms_bwd(dy * g, xh, r)
        dx_ref[...] = dx
        dxb_ref[...] = dx.astype(BF16)
        red_ref[0:1, :] += jnp.sum(dy * xh, axis=0, keepdims=True)
        red_ref[1:2, :] += jnp.broadcast_to(0.5 * jnp.sum(e * e) * (1.0 / D), (1, D))

    row = pl.BlockSpec((tm, D), lambda i: (i, 0))
    return _pallas_call(
        body, name="fwd_down_loss", grid=(L // tm,),
        in_specs=[pl.BlockSpec((tm, F), lambda i: (i, 0)), pl.BlockSpec((F, D), lambda i: (0, 0)),
                  row, row, pl.BlockSpec((1, D), lambda i: (0, 0))],
        out_specs=[row, row, pl.BlockSpec((8, D), lambda i: (0, 0))],
        out_shape=[_sds((L, D), F32), _sds((L, D), BF16), _sds((8, D), F32)],
        compiler_params=_params(("arbitrary",), 48),
    )(s, wd, x1, target, g_final)


def _bwd_down(dx2b, wd, a, b):
    L, D = dx2b.shape
    F = wd.shape[0]
    tn = F // 2
    tm = min(L, 512)

    def body(dx_ref, wd_ref, a_ref, b_ref, da_ref, db_ref):
        ds = _nt(dx_ref[...], wd_ref[...])
        a, b = a_ref[...].astype(F32), b_ref[...].astype(F32)
        sg = _sigmoid(a)
        da_ref[...] = (ds * b * sg * (1.0 + a * (1.0 - sg))).astype(BF16)
        db_ref[...] = (ds * a * sg).astype(BF16)

    ospec = pl.BlockSpec((tm, tn), lambda j, i: (i, j))
    return _pallas_call(
        body, name="bwd_down", grid=(2, L // tm),
        in_specs=[pl.BlockSpec((tm, D), lambda j, i: (i, 0)),
                  pl.BlockSpec((tn, D), lambda j, i: (j, 0)), ospec, ospec],
        out_specs=[ospec] * 2,
        out_shape=[_sds((L, F), BF16)] * 2,
        compiler_params=_params(("parallel", "parallel"), 48),
    )(dx2b, wd, a, b)


def _bwd_ffn_dh(da, db, wgt, wut, x1, dx2, g_ffn, after):
    L, D = x1.shape
    F = wgt.shape[0]
    tm = min(L, 256)

    def body(da_ref, db_ref, wg_ref, wu_ref, x1_ref, dx2_ref, g_ref, after_ref, dx_ref, dxb_ref, red_ref):
        @pl.when(pl.program_id(0) == 0)
        def _():
            red_ref[...] = jnp.zeros_like(red_ref)

        dh = _nn(da_ref[...], wg_ref[...]) + _nn(db_ref[...], wu_ref[...])
        r, xh = _rms_stats(x1_ref[...])
        red_ref[0:1, :] += jnp.sum(dh * xh, axis=0, keepdims=True)
        dx = dx2_ref[...] + _rms_bwd(dh * g_ref[...], xh, r)
        dx_ref[...] = dx
        dxb_ref[...] = dx.astype(BF16)

    row = pl.BlockSpec((tm, D), lambda i: (i, 0))
    aspec = pl.BlockSpec((tm, F), lambda i: (i, 0))
    wspec = pl.BlockSpec((F, D), lambda i: (0, 0))
    return _pallas_call(
        body, name="bwd_ffn_dh", grid=(L // tm,),
        in_specs=[aspec, aspec, wspec, wspec, row, row, pl.BlockSpec((1, D), lambda i: (0, 0)), ANY],
        out_specs=[row, row, pl.BlockSpec((8, D), lambda i: (0, 0))],
        out_shape=[_sds((L, D), F32), _sds((L, D), BF16), _sds((8, D), F32)],
        compiler_params=_params(("arbitrary",), 56),
    )(da, db, wgt, wut, x1, dx2, g_ffn, after)


def _bwd_in(dproj, w_int, x, dx1, g_mix, after):
    L, D = x.shape
    N = w_int.shape[0]
    tm = min(L, 256)

    def body(dp_ref, w_ref, x_ref, dx1_ref, g_ref, after_ref, dx_ref, red_ref):
        @pl.when(pl.program_id(0) == 0)
        def _():
            red_ref[...] = jnp.zeros_like(red_ref)

        dh = _nn(dp_ref[...], w_ref[...])
        r, xh = _rms_stats(x_ref[...])
        red_ref[0:1, :] += jnp.sum(dh * xh, axis=0, keepdims=True)
        dx_ref[...] = dx1_ref[...] + _rms_bwd(dh * g_ref[...], xh, r)

    row = pl.BlockSpec((tm, D), lambda i: (i, 0))
    return _pallas_call(
        body, name="bwd_in", grid=(L // tm,),
        in_specs=[pl.BlockSpec((tm, N), lambda i: (i, 0)), pl.BlockSpec((N, D), lambda i: (0, 0)),
                  row, row, pl.BlockSpec((1, D), lambda i: (0, 0)), ANY],
        out_specs=[row, pl.BlockSpec((8, D), lambda i: (0, 0))],
        out_shape=[_sds((L, D), F32), _sds((8, D), F32)],
        compiler_params=_params(("arbitrary",), 56),
    )(dproj, w_int, x, dx1, g_mix, after)


def _mm_tn(name, a, b, a_spec, b_spec, o_block, n_out, n_k):
    def body(a_ref, b_ref, o_ref):
        part = _tn(a_ref[...], b_ref[...])

        @pl.when(pl.program_id(1) == 0)
        def _():
            o_ref[...] = part

        @pl.when(pl.program_id(1) > 0)
        def _():
            o_ref[...] += part

    return _pallas_call(
        body, name=name, grid=(n_out, n_k),
        in_specs=[a_spec, b_spec],
        out_specs=pl.BlockSpec((None,) + o_block, lambda j, k: (j, 0, 0)),
        out_shape=_sds((n_out,) + o_block, F32),
        compiler_params=_params(("parallel", "arbitrary"), 56),
    )(a, b)


TK_TOKENS = 2048


def _dw_cols(name, a, b, n_cols):
    L, M = a.shape
    tk = min(L, TK_TOKENS)
    return _mm_tn(name, a, b, pl.BlockSpec((tk, M), lambda j, k: (k, 0)),
                  pl.BlockSpec((tk, n_cols), lambda j, k: (k, j)), (M, n_cols), N_CHIPS, L // tk)


def _dw_rows(name, a, b):
    L, M = a.shape
    N = b.shape[1]
    tk = min(L, TK_TOKENS)
    return _mm_tn(name, a, b, pl.BlockSpec((tk, M // N_CHIPS), lambda j, k: (k, j)),
                  pl.BlockSpec((tk, N), lambda j, k: (k, 0)), (M // N_CHIPS, N), N_CHIPS, L // tk)


def _dw_rows2(name, a, b):
    L, M = a.shape
    N = b.shape[1]
    tk = min(L, TK_TOKENS)
    return _mm_tn(name, a, b, pl.BlockSpec((tk, M // 2), lambda j, k: (k, j)),
                  pl.BlockSpec((tk, N), lambda j, k: (k, 0)), (M // 2, N), 2, L // tk)


def _place():
    x, y, c = lax.axis_index("x"), lax.axis_index("y"), lax.axis_index("c")
    chips = [(1 - x, y), (x, 1 - y), (1 - x, 1 - y)]
    return x, y, c, 2 * x + y, chips


def _remote(src, dst, send_sem, recv_sem, device):
    return pltpu.make_async_remote_copy(src_ref=src, dst_ref=dst, send_sem=send_sem,
                                        recv_sem=recv_sem, device_id=device, device_id_type=MESH)


def _half(ref, lead, c, r2):
    return ref.at[lead, pl.ds(pl.multiple_of(c * r2, 16), r2), :]


def _cast_place(name, w, chip_idx):
    r, cols = w.shape
    tr = r // 2

    def body(k_ref, w_ref, o_ref):
        o_ref[...] = w_ref[...].astype(BF16)

    return _pallas_call(
        body, name=name,
        grid_spec=pltpu.PrefetchScalarGridSpec(
            num_scalar_prefetch=1, grid=(2,),
            in_specs=[pl.BlockSpec((tr, cols), lambda i, k_ref: (i, 0))],
            out_specs=pl.BlockSpec((None, tr, cols), lambda i, k_ref: (k_ref[0], i, 0))),
        out_shape=_sds((N_CHIPS, r, cols), BF16),
        compiler_params=_params(("parallel",), 48),
    )(chip_idx, w)


def _cast_place_t(name, w, chip_idx):
    r, cols = w.shape

    def body(k_ref, w_ref, o_ref):
        o_ref[...] = w_ref[...].T.astype(BF16)

    return _pallas_call(
        body, name=name,
        grid_spec=pltpu.PrefetchScalarGridSpec(
            num_scalar_prefetch=1, grid=(cols // LANES,),
            in_specs=[pl.BlockSpec((r, LANES), lambda i, k_ref: (0, i))],
            out_specs=pl.BlockSpec((None, LANES, r), lambda i, k_ref: (k_ref[0], i, 0))),
        out_shape=_sds((N_CHIPS, cols, r), BF16),
        compiler_params=_params(("parallel",), 48),
    )(chip_idx, w)


def _gather_copies(bufs, whole, send_sems, recv_sems, select=None):
    x, y, c, k, chips = _place()
    pairs = []
    for w, buf in enumerate(bufs):
        for j, (cx, cy) in enumerate(chips):
            if select is not None and not select(w, j):
                continue
            if w in whole:
                mine, theirs = buf.at[k], buf.at[2 * cx + cy]
            else:
                r2 = buf.shape[1] // 2
                mine, theirs = _half(buf, k, c, r2), _half(buf, 2 * cx + cy, c, r2)
            sems = (send_sems.at[w * 3 + j], recv_sems.at[w * 3 + j])
            pairs.append((_remote(mine, mine, *sems, (cx, cy, c)), _remote(theirs, theirs, *sems, (x, y, c))))
    return pairs


def _gather_start(groups):
    flat = [b for bufs, _ in groups for b in bufs]
    nb, ng = len(flat), len(groups)

    def body(*refs):
        ins, sems, token = refs[:nb], refs[nb:nb + 2 * ng], refs[-1]
        pos = 0
        for g, (bufs, whole) in enumerate(groups):
            for send, _ in _gather_copies(ins[pos:pos + len(bufs)], whole, sems[2 * g], sems[2 * g + 1]):
                send.start()
            pos += len(bufs)
        token[...] = jnp.zeros_like(token)

    sem_shapes = []
    for bufs, _ in groups:
        sem_shapes += [pltpu.SemaphoreType.DMA((3 * len(bufs),))] * 2
    out = _pallas_call(
        body, name="gather_start",
        in_specs=[HBM] * nb, out_specs=tuple([SEM] * (2 * ng) + [HBM] * nb + [VMEM]),
        out_shape=tuple(sem_shapes + [pltpu.HBM(b.shape, b.dtype) for b in flat] + [_sds((8, LANES), F32)]),
        input_output_aliases={i: 2 * ng + i for i in range(nb)},
        compiler_params=pltpu.CompilerParams(has_side_effects=EFFECT),
    )(*flat)
    sems, thru, pos = [], [], 2 * ng
    for g, (bufs, _) in enumerate(groups):
        sems.append((out[2 * g], out[2 * g + 1]))
        thru.append(list(out[pos:pos + len(bufs)]))
        pos += len(bufs)
    return sems, thru, out[-1]


def _gather_wait(name, bufs, whole, sems, after, select=None):
    nb = len(bufs)

    def body(*refs):
        ins, send_sems, recv_sems = refs[:nb], refs[nb], refs[nb + 1]
        for send, arrival in _gather_copies(ins, whole, send_sems, recv_sems, select):
            send.wait_send()
            arrival.wait_recv()

    return _pallas_call(
        body, name=name,
        in_specs=[HBM] * nb + [SEM, SEM, ANY], out_specs=[HBM] * nb,
        out_shape=[pltpu.HBM(b.shape, b.dtype) for b in bufs],
        input_output_aliases={i: i for i in range(nb)},
        compiler_params=pltpu.CompilerParams(has_side_effects=EFFECT),
    )(*bufs, sems[0], sems[1], after)


def _gather_forward(name, bufs, sources=(0, 1, 2)):
    n = len(bufs)

    def body(*refs):
        outs = refs[n:2 * n]
        send_sems, recv_sems = refs[2 * n:]
        x, y, c, _, chips = _place()
        sends = []
        for w in range(n):
            r2 = outs[w].shape[1] // 2
            for j in sources:
                landed = _half(outs[w], 2 * chips[j][0] + chips[j][1], c, r2)
                sends.append(_remote(landed, landed, send_sems.at[w * 3 + j], recv_sems.at[w * 3 + j],
                                     (x, y, 1 - c)))
        for cp in sends:
            cp.start()
        for w in range(n):
            r2 = outs[w].shape[1] // 2
            for j in sources:
                got = _half(outs[w], 2 * chips[j][0] + chips[j][1], 1 - c, r2)
                _remote(got, got, send_sems.at[w * 3 + j], recv_sems.at[w * 3 + j], (x, y, c)).wait_recv()
        for cp in sends:
            cp.wait_send()

    return _pallas_call(
        body, name=name,
        in_specs=[ANY] * n, out_specs=[ANY] * n,
        out_shape=[_sds(b.shape, b.dtype) for b in bufs],
        input_output_aliases={i: i for i in range(n)},
        scratch_shapes=[pltpu.SemaphoreType.DMA((n * 3,)), pltpu.SemaphoreType.DMA((n * 3,))],
    )(*bufs)


def _rs_sibling(name, grads):
    n = len(grads)

    def body(*refs):
        ins, outs = refs[:n], refs[n:2 * n]
        send_sems, recv_sems = refs[2 * n:]
        x, y, c, _, _ = _place()
        copies = []
        for w in range(n):
            r2 = ins[w].shape[1] // 2
            copies.append(_remote(_half(ins[w], slice(None), 1 - c, r2), outs[w],
                                  send_sems.at[w], recv_sems.at[w], (x, y, 1 - c)))
        for cp in copies:
            cp.start()
        for cp in copies:
            cp.wait()

    return _pallas_call(
        body, name=name,
        in_specs=[ANY] * n, out_specs=[ANY] * n,
        out_shape=[_sds((N_CHIPS, g.shape[1] // 2, g.shape[2]), F32) for g in grads],
        scratch_shapes=[pltpu.SemaphoreType.DMA((n,)), pltpu.SemaphoreType.DMA((n,))],
    )(*grads)


def _rs_add(name, grad3, from_sibling, c_idx):
    _, r2, cols = from_sibling.shape

    def body(c_ref, g_ref, s_ref, o_ref):
        o_ref[...] = (g_ref[...] + s_ref[...]).astype(BF16)

    return _pallas_call(
        body, name=name,
        grid_spec=pltpu.PrefetchScalarGridSpec(
            num_scalar_prefetch=1, grid=(N_CHIPS,),
            in_specs=[pl.BlockSpec((None, r2, cols), lambda k, c_ref: (k, c_ref[0], 0)),
                      pl.BlockSpec((None, r2, cols), lambda k, c_ref: (k, 0, 0))],
            out_specs=pl.BlockSpec((None, r2, cols), lambda k, c_ref: (k, 0, 0))),
        out_shape=_sds(from_sibling.shape, BF16),
        compiler_params=_params(("parallel",), 48),
    )(c_idx, grad3, from_sibling)


def _split_start(name, arrays, n_sems, pairs_fn):
    n = len(arrays)

    def body(*refs):
        for send, _ in pairs_fn(refs[:n], refs[n], refs[n + 1]):
            send.start()
        refs[-1][...] = jnp.zeros_like(refs[-1])

    out = _pallas_call(
        body, name=name,
        in_specs=[HBM] * n, out_specs=tuple([SEM, SEM] + [HBM] * n + [VMEM]),
        out_shape=tuple([pltpu.SemaphoreType.DMA((n_sems,))] * 2 + [pltpu.HBM(a.shape, a.dtype) for a in arrays]
                        + [_sds((8, LANES), F32)]),
        input_output_aliases={i: 2 + i for i in range(n)},
        compiler_params=pltpu.CompilerParams(has_side_effects=EFFECT),
    )(*arrays)
    return (out[0], out[1]), list(out[2:2 + n]), out[-1]


def _split_wait(name, sems, arrays, pairs_fn, after):
    n = len(arrays)

    def body(*refs):
        for send, arrival in pairs_fn(refs[:n], refs[n], refs[n + 1]):
            send.wait_send()
            arrival.wait_recv()

    return list(_pallas_call(
        body, name=name,
        in_specs=[HBM] * n + [SEM, SEM, ANY], out_specs=[HBM] * n,
        out_shape=[pltpu.HBM(a.shape, a.dtype) for a in arrays],
        input_output_aliases={i: i for i in range(n)},
        compiler_params=pltpu.CompilerParams(has_side_effects=EFFECT),
    )(*arrays, sems[0], sems[1], after))


def _forward_pairs(bufs, send_sems, recv_sems):
    x, y, c, _, chips = _place()
    pairs = []
    for w, buf in enumerate(bufs):
        r2 = buf.shape[1] // 2
        for j, (cx, cy) in enumerate(chips):
            landed, theirs = _half(buf, 2 * cx + cy, c, r2), _half(buf, 2 * cx + cy, 1 - c, r2)
            sems = (send_sems.at[w * 3 + j], recv_sems.at[w * 3 + j])
            pairs.append((_remote(landed, landed, *sems, (x, y, 1 - c)), _remote(theirs, theirs, *sems, (x, y, c))))
    return pairs


def _sibling_pairs(arrays, send_sems, recv_sems):
    x, y, c, _, _ = _place()
    n = len(arrays) // 2
    pairs = []
    for w in range(n):
        r2 = arrays[w].shape[1] // 2
        cp = _remote(_half(arrays[w], slice(None), 1 - c, r2), arrays[n + w], send_sems.at[w], recv_sems.at[w],
                     (x, y, 1 - c))
        pairs.append((cp, cp))
    return pairs


def _ici_pairs(arrays, send_sems, recv_sems):
    x, y, c, _, chips = _place()
    n = len(arrays) // 2
    pairs = []
    for w in range(n):
        for j, (cx, cy) in enumerate(chips):
            cp = _remote(arrays[w].at[2 * cx + cy], arrays[n + w].at[j],
                         send_sems.at[w * 3 + j], recv_sems.at[w * 3 + j], (cx, cy, c))
            pairs.append((cp, cp))
    return pairs


def _rs_sum(name, partials, received, place_idx):
    _, r2, cols = partials.shape
    nb = 2
    tr = r2 // nb

    def body(idx_ref, p_ref, r_ref, o_ref):
        o_ref[...] = ((p_ref[...].astype(F32) + r_ref[0].astype(F32))
                      + (r_ref[1].astype(F32) + r_ref[2].astype(F32)))

    return _pallas_call(
        body, name=name,
        grid_spec=pltpu.PrefetchScalarGridSpec(
            num_scalar_prefetch=1, grid=(nb,),
            in_specs=[pl.BlockSpec((None, tr, cols), lambda i, idx: (idx[0], i, 0)),
                      pl.BlockSpec((3, tr, cols), lambda i, idx: (0, i, 0))],
            out_specs=pl.BlockSpec((tr, cols), lambda i, idx: (idx[1] * nb + i, 0))),
        out_shape=_sds((2 * r2, cols), F32),
        compiler_params=_params(("parallel",), 48),
    )(place_idx, partials, received)


def _rs_share(name, shards):
    n = len(shards)

    def body(*refs):
        outs = refs[n:2 * n]
        send_sems, recv_sems = refs[2 * n:]
        x, y, c, _, _ = _place()
        sends = []
        for w in range(n):
            r2 = outs[w].shape[0] // 2
            mine = outs[w].at[pl.ds(pl.multiple_of(c * r2, 8), r2), :]
            sends.append(_remote(mine, mine, send_sems.at[w], recv_sems.at[w], (x, y, 1 - c)))
        for cp in sends:
            cp.start()
        for w in range(n):
            r2 = outs[w].shape[0] // 2
            theirs = outs[w].at[pl.ds(pl.multiple_of((1 - c) * r2, 8), r2), :]
            _remote(theirs, theirs, send_sems.at[w], recv_sems.at[w], (x, y, c)).wait_recv()
        for cp in sends:
            cp.wait_send()

    return _pallas_call(
        body, name=name,
        in_specs=[ANY] * n, out_specs=[ANY] * n,
        out_shape=[_sds(s.shape, F32) for s in shards],
        input_output_aliases={i: i for i in range(n)},
        scratch_shapes=[pltpu.SemaphoreType.DMA((n,)), pltpu.SemaphoreType.DMA((n,))],
    )(*shards)


def _small_allreduce(red_mix, red_ffn, red_final, red_hg, g_conv):
    rows = N_SMALL_ROWS
    D = red_mix.shape[1]
    H = red_hg.shape[1]

    def body(mix_ref, ffn_ref, fin_ref, hg_ref, cv_ref, sum_ref, all_ref, in_ref, send_sems, recv_sems):
        in_ref[...] = jnp.zeros_like(in_ref)
        in_ref[0:1, :] = mix_ref[0:1, :]
        in_ref[1:2, :] = ffn_ref[0:1, :]
        in_ref[2:3, :] = fin_ref[0:1, :]
        gam = hg_ref[1:2, 0:HEAD_DIM]
        for h in range(1, H // HEAD_DIM):
            gam = gam + hg_ref[1:2, h * HEAD_DIM:(h + 1) * HEAD_DIM]
        in_ref[3:4, 0:HEAD_DIM] = gam
        in_ref[3:4, HEAD_DIM:2 * HEAD_DIM] = fin_ref[1:2, 0:HEAD_DIM]
        in_ref[4:5, 0:H] = hg_ref[0:1, :]
        in_ref[6:9, 0:H] = cv_ref[...]
        x, y, c, _, _ = _place()
        me = 4 * x + 2 * y + c
        all_ref[me] = in_ref[...]
        copies = []
        for m in range(1, 8):
            mx, my, mc = (m >> 2) & 1, (m >> 1) & 1, m & 1
            px, py, pc = x ^ mx, y ^ my, c ^ mc
            copies.append((_remote(in_ref, all_ref.at[me], send_sems.at[m - 1], recv_sems.at[m - 1],
                                   (px, py, pc)), 4 * px + 2 * py + pc, m))
        for cp, _, _ in copies:
            cp.start()
        for _, peer, m in copies:
            _remote(in_ref, all_ref.at[peer], send_sems.at[m - 1], recv_sems.at[m - 1],
                    (x, y, c)).wait_recv()
        for cp, _, _ in copies:
            cp.wait_send()
        total = all_ref[0]
        for d in range(1, 8):
            total = total + all_ref[d]
        sum_ref[...] = total

    return _pallas_call(
        body, name="small_allreduce", pin=False,
        in_specs=[VMEM] * 5, out_specs=[VMEM, VMEM],
        out_shape=[_sds((rows, D), F32), _sds((8, rows, D), F32)],
        scratch_shapes=[pltpu.VMEM((rows, D), F32), pltpu.SemaphoreType.DMA((7,)),
                        pltpu.SemaphoreType.DMA((7,))],
    )(red_mix, red_ffn, red_final, red_hg, g_conv)[0]


def _adamw_math(w, g, m, v):
    m = ADAM_B1 * m + (1.0 - ADAM_B1) * g
    v = ADAM_B2 * v + (1.0 - ADAM_B2) * jnp.square(g)
    m_hat = m / (1.0 - ADAM_B1 ** ADAM_STEP)
    v_hat = v / (1.0 - ADAM_B2 ** ADAM_STEP)
    delta = -ADAM_LR * (m_hat / (jnp.sqrt(v_hat) + ADAM_EPS) + ADAM_WD * w)
    return delta, m, v


def _adamw(name, g, w, m, v):
    r, cols = g.shape
    tr = r // 4

    def body(g_ref, w_ref, m_ref, v_ref, go_ref, d_ref, mo_ref, vo_ref):
        g = g_ref[...]
        go_ref[...] = g
        d_ref[...], mo_ref[...], vo_ref[...] = _adamw_math(w_ref[...], g, m_ref[...], v_ref[...])

    blk = pl.BlockSpec((tr, cols), lambda i: (i, 0))
    return _pallas_call(
        body, name=name, grid=(r // tr,),
        in_specs=[blk] * 4, out_specs=[blk] * 4, out_shape=[_sds((r, cols), F32)] * 4,
        compiler_params=_params(("parallel",), 48),
    )(g, w, m, v)


def _small_update(total, chip_idx, ws, ms, vs):
    n = len(ws)
    H = ws[1].shape[1]

    def body(idx_ref, tot_ref, *refs):
        w, m, v, outs = refs[:n], refs[n:2 * n], refs[2 * n:3 * n], refs[3 * n:]
        chip = idx_ref[0]
        p0 = _lower_bound(w[1][...])
        dl0 = p0 * (1.0 - p0) * tot_ref[4:5, 0:H]
        conv = jnp.zeros((3, LANES), F32)
        for k in range(N_CHIPS):
            conv = jnp.where(chip == k, tot_ref[6:9, k * LANES:(k + 1) * LANES], conv)
        grads = [tot_ref[0:1, :], None, tot_ref[3:4, 0:HEAD_DIM], conv, tot_ref[1:2, :], tot_ref[2:3, :]]
        for p in range(n):
            g_ref, d_ref, mo_ref, vo_ref = outs[4 * p:4 * p + 4]
            if p == 1:
                for row, g in ((slice(0, 1), dl0), (slice(1, 2), -dl0)):
                    g_ref[row, :] = g
                    d_ref[row, :], mo_ref[row, :], vo_ref[row, :] = _adamw_math(
                        w[p][row, :], g, m[p][row, :], v[p][row, :])
            else:
                g_ref[...] = grads[p]
                d_ref[...], mo_ref[...], vo_ref[...] = _adamw_math(w[p][...], grads[p], m[p][...], v[p][...])
        outs[4 * n][...] = tot_ref[3:4, HEAD_DIM:2 * HEAD_DIM]

    full = lambda a: pl.BlockSpec(a.shape, lambda i, idx: (0,) * a.ndim)
    out_shape = [_sds(w.shape, F32) for w in ws for _ in range(4)] + [_sds((1, LANES), F32)]
    return _pallas_call(
        body, name="small_update",
        grid_spec=pltpu.PrefetchScalarGridSpec(
            num_scalar_prefetch=1, grid=(1,),
            in_specs=[full(total)] + [full(a) for a in ws + ms + vs],
            out_specs=[full(s) for s in out_shape]),
        out_shape=out_shape,
    )(chip_idx, total, *ws, *ms, *vs)


def kernel(x, norm_mix_g, w_in, lower_bounds, hg_norm_g, conv_w, w_branch_a, w_branch_b, w_out, norm_ffn_g, w_ffn_gate, w_ffn_up, w_ffn_down, norm_final_g, loss_target, m_norm_mix_g, m_w_in, m_lower_bounds, m_hg_norm_g, m_conv_w, m_w_branch_a, m_w_branch_b, m_w_out, m_norm_ffn_g, m_w_ffn_gate, m_w_ffn_up, m_w_ffn_down, m_norm_final_g, v_norm_mix_g, v_w_in, v_lower_bounds, v_hg_norm_g, v_conv_w, v_w_branch_a, v_w_branch_b, v_w_out, v_norm_ffn_g, v_w_ffn_gate, v_w_ffn_up, v_w_ffn_down, v_norm_final_g):
    _, L, D = x.shape
    H = D // 2
    assert lower_bounds.shape == (2, H) and hg_norm_g.shape == (1, HEAD_DIM)
    assert conv_w.shape == (1, 3, LANES) and w_in.shape[2] * N_CHIPS == 11 * H
    x2d, target = x.reshape(L, D), loss_target.reshape(L, D)
    g_final = norm_final_g.reshape(1, D)
    chip = 2 * lax.axis_index("x") + lax.axis_index("y")
    core = lax.axis_index("c")

    tr = lambda w: jnp.transpose(w[0])
    big = [w_in[0], w_branch_a[0], w_branch_b[0], w_out[0], tr(w_ffn_gate), tr(w_ffn_up), w_ffn_down[0]]
    big_m = [m_w_in[0], m_w_branch_a[0], m_w_branch_b[0], m_w_out[0], tr(m_w_ffn_gate), tr(m_w_ffn_up),
             m_w_ffn_down[0]]
    big_v = [v_w_in[0], v_w_branch_a[0], v_w_branch_b[0], v_w_out[0], tr(v_w_ffn_gate), tr(v_w_ffn_up),
             v_w_ffn_down[0]]
    names = ["w_in", "w_branch_a", "w_branch_b", "w_out", "w_ffn_gate", "w_ffn_up", "w_ffn_down"]

    chip_idx = chip.reshape(1).astype(jnp.int32)
    placed = [(_cast_place_t if j < 3 else _cast_place)("place_" + nm, w, chip_idx)
              for j, (nm, w) in enumerate(zip(names, big))]
    conv_placed = lax.dynamic_update_slice(jnp.zeros((N_CHIPS, 3, LANES), F32), conv_w, (chip, 0, 0))
    sems, in_flight, token = _gather_start([([placed[0], conv_placed], {1}), (placed[1:4], set()),
                                            (placed[4:], set())])

    x_i, y_i = lax.axis_index("x"), lax.axis_index("y")
    blocks = lambda *ks: jnp.stack(ks).astype(jnp.int32)
    w_in_buf, conv_buf = in_flight[0]
    h, proj = _fwd_proj_first(x2d, norm_mix_g, w_in_buf, blocks(chip))
    w_in_buf, conv_buf = _gather_wait("gather_wait_in_near", [w_in_buf, conv_buf], {1}, sems[0], h,
                                      lambda w, j: w == 0 and j < 2)
    (w_in_buf,) = _gather_forward("gather_fwd_in_near", [w_in_buf], (0, 1))
    proj = _fwd_proj_more("fwd_proj_near", h, w_in_buf, proj,
                          blocks(2 * (1 - x_i) + y_i, 2 * x_i + (1 - y_i)))
    w_in_buf, conv_all = _gather_wait("gather_wait_in_far", [w_in_buf, conv_buf], {1}, sems[0], proj,
                                      lambda w, j: w == 1 or j == 2)
    (w_int3,) = _gather_forward("gather_fwd_in_far", [w_in_buf], (2,))
    proj = _fwd_proj_more("fwd_proj_far", h, w_int3, proj, blocks(2 * (1 - x_i) + (1 - y_i)))
    w_int = w_int3.reshape(-1, D)
    conv_full = jnp.transpose(conv_all, (1, 0, 2)).reshape(3, H)
    og, o_pre, s_saved = _hgrn_fwd(proj, lower_bounds, hg_norm_g, H)
    landed = _gather_wait("gather_wait_mix", in_flight[1], set(), sems[1], og)
    fwd_sems, landed, token = _split_start("gather_fwd_mix_start", landed, 9, _forward_pairs)
    cb = _conv_fwd(proj, conv_full, H, token)
    wat3, wbt3, wout3 = _split_wait("gather_fwd_mix_wait", fwd_sems, landed, _forward_pairs, cb)
    wat, wbt, wout = wat3.reshape(D, H), wbt3.reshape(D, H), wout3.reshape(D, D)
    landed = _gather_wait("gather_wait_ffn", in_flight[2], set(), sems[2], cb)
    fwd_sems, landed, token = _split_start("gather_fwd_ffn_start", landed, 9, _forward_pairs)
    ya, yb, merged, x1, h2 = _fwd_mix(og, cb, proj, x2d, wat, wbt, wout, norm_ffn_g, H, token)
    wgt3, wut3, wd3 = _split_wait("gather_fwd_ffn_wait", fwd_sems, landed, _forward_pairs, h2)
    d_ff = N_CHIPS * wd3.shape[1]
    wgt, wut, wd = wgt3.reshape(d_ff, D), wut3.reshape(d_ff, D), wd3.reshape(d_ff, D)
    ffn_a, ffn_b, ffn_s = _fwd_ffn_up(h2, wgt, wut)
    dx2, dx2b, red_final = _fwd_down_loss(ffn_s, wd, x1, target, g_final)

    c_idx = core.reshape(1).astype(jnp.int32)
    place_idx = jnp.stack([chip, core]).astype(jnp.int32)

    def sibling_start(tag, grads):
        bufs = [lax.empty((N_CHIPS, g.shape[1] // 2, g.shape[2]), F32) for g in grads]
        return _split_start("rs_sibling_start_" + tag, list(grads) + bufs, len(grads), _sibling_pairs)

    def ici_start(tag, js, grads, from_sibling):
        partials = [_rs_add("rs_add_" + names[j], g, s, c_idx) for j, g, s in zip(js, grads, from_sibling)]
        landings = [lax.empty((3,) + p.shape[1:], BF16) for p in partials]
        return _split_start("rs_ici_start_" + tag, partials + landings, 3 * len(js), _ici_pairs)

    def ici_start_behind(tag, js, started, after):
        n = len(js)
        arrays = _split_wait("rs_sibling_wait_" + tag, started[0], started[1], _sibling_pairs, after)
        return ici_start(tag, js, arrays[:n], arrays[n:])

    def rs_end(tag, js, started, after):
        n = len(js)
        arrays = _split_wait("rs_ici_wait_" + tag, started[0], started[1], _ici_pairs, after)
        halves = [_rs_sum("rs_sum_" + names[j], p, r, place_idx)
                  for j, p, r in zip(js, arrays[:n], arrays[n:])]
        grads = _rs_share("rs_share_" + tag, halves)
        return [_adamw("adamw_" + names[j], g, big[j], big_m[j], big_v[j]) for j, g in zip(js, grads)]

    shards3 = lambda g: g.reshape(N_CHIPS, d_ff // N_CHIPS, D)
    da, db = _bwd_down(dx2b, wd, ffn_a, ffn_b)
    g_wd = shards3(_dw_rows2("dw_ffn_down", ffn_s, dx2b))
    g_wg = shards3(_dw_rows2("dw_ffn_gate", da, h2))
    g_wu = shards3(_dw_rows2("dw_ffn_up", db, h2))
    ffn_sibling = sibling_start("ffn", [g_wg, g_wu, g_wd])
    dx1, dx1b, red_ffn = _bwd_ffn_dh(da, db, wgt, wut, x1, dx2, norm_ffn_g, ffn_sibling[2])
    ffn_ici = ici_start_behind("ffn", [4, 5, 6], ffn_sibling, dx1b)
    dya, dyb, dga, dgb, d_o, d_cb = _bwd_mix(dx1b, proj, ya, yb, wat, wbt, wout, H, ffn_ici[2])
    g_wout = _dw_rows("dw_out", merged, dx1b)
    g_wa = _dw_cols("dw_branch_a", og, dya, D // N_CHIPS)
    g_wb = _dw_cols("dw_branch_b", cb, dyb, D // N_CHIPS)
    mix_sibling = sibling_start("mix", [g_wa, g_wb, g_wout])
    dq, df, dv, dg, red_hg = _hgrn_bwd(proj, lower_bounds, hg_norm_g, o_pre, d_o, s_saved, H, mix_sibling[2])
    mix_ici = ici_start_behind("mix", [1, 2, 3], mix_sibling, dq)
    dcg, dbg, dxb, g_conv = _conv_bwd(proj, conv_full, d_cb, H, mix_ici[2])
    dproj = jnp.concatenate([dq, df, dv, dg, dcg, dbg, dxb, dga, dgb], axis=1)
    g_win = _dw_cols("dw_in", h, dproj, w_int3.shape[1])
    in_sibling = sibling_start("in", [g_win])
    big_out = [None] + rs_end("mix", [1, 2, 3], mix_ici, in_sibling[2]) + rs_end(
        "ffn", [4, 5, 6], ffn_ici, in_sibling[2])
    in_ici = ici_start_behind("in", [0], in_sibling, big_out[6][0])
    grad_x, red_mix = _bwd_in(dproj, w_int, x2d, dx1, norm_mix_g, in_ici[2])
    big_out[0] = rs_end("in", [0], in_ici, grad_x)[0]

    total = _small_allreduce(red_mix, red_ffn, red_final, red_hg, g_conv)

    def smalls(mix, lb, hg, cw, ffn, fin):
        return [mix, lb, hg, cw[0], ffn, fin.reshape(1, D)]

    small_out = _small_update(
        total, chip_idx,
        smalls(norm_mix_g, lower_bounds, hg_norm_g, conv_w, norm_ffn_g, norm_final_g),
        smalls(m_norm_mix_g, m_lower_bounds, m_hg_norm_g, m_conv_w, m_norm_ffn_g, m_norm_final_g),
        smalls(v_norm_mix_g, v_lower_bounds, v_hg_norm_g, v_conv_w, v_norm_ffn_g, v_norm_final_g))

    def outputs(i):
        big_i = [big_out[j][i] for j in range(7)]
        mix, lb, hg, cw, ffn, fin = [small_out[4 * p + i] for p in range(6)]
        return [mix, big_i[0][None], lb, hg, cw[None], big_i[1][None], big_i[2][None], big_i[3][None], ffn,
                big_i[4].T[None], big_i[5].T[None], big_i[6][None], fin.reshape(D)]

    outs = [small_out[24][0, 0], grad_x.reshape(1, L, D)]
    for i in range(4):
        outs += outputs(i)
    return tuple(outs)
```

```python
import functools

import jax
import jax.numpy as jnp
from jax import lax
from jax.experimental import pallas as pl
from jax.experimental.pallas import tpu as pltpu

F32 = jnp.float32
BF16 = jnp.bfloat16
EPS = 1e-6
CHUNK = 32
HEAD_DIM = 128
LANES = 128
N_CHIPS = 4
N_SMALL_ROWS = 16

ADAM_LR = 0.001
ADAM_B1 = 0.9
ADAM_B2 = 0.999
ADAM_EPS = 1e-08
ADAM_WD = 0.01
ADAM_STEP = 10

MESH = pl.DeviceIdType.MESH
ANY = pl.BlockSpec(memory_space=pl.ANY)
VMEM = pl.BlockSpec(memory_space=pltpu.VMEM)
HBM = pl.BlockSpec(memory_space=pltpu.HBM)
SEM = pl.BlockSpec(memory_space=pltpu.SEMAPHORE)
EFFECT = pltpu.SideEffectType.DATAFLOW_SIDE_EFFECTING


def _sds(shape, dtype):
    return jax.ShapeDtypeStruct(shape, dtype)


def _pallas_call(body, pin=True, **kwargs):
    if not pin:
        return pl.pallas_call(body, **kwargs)
    in_hbm = lambda s: pltpu.HBM(s.shape, s.dtype) if isinstance(s, jax.ShapeDtypeStruct) else s
    kwargs["out_shape"] = jax.tree.map(in_hbm, kwargs["out_shape"])
    call = pl.pallas_call(body, **kwargs)

    def run(*args):
        return call(*[pltpu.with_memory_space_constraint(a, pltpu.HBM) if a.dtype in (F32, BF16) else a
                      for a in args])

    return run


def _params(semantics, vmem_mb):
    return pltpu.CompilerParams(dimension_semantics=semantics, vmem_limit_bytes=vmem_mb << 20)


def _nn(a, b):
    return lax.dot_general(a, b, (((1,), (0,)), ((), ())), preferred_element_type=F32)


def _nt(a, b):
    return lax.dot_general(a, b, (((1,), (1,)), ((), ())), preferred_element_type=F32)


def _tn(a, b):
    return lax.dot_general(a, b, (((0,), (0,)), ((), ())), preferred_element_type=F32)


def _sigmoid(x):
    return jax.nn.sigmoid(x)


def _rms_stats(x):
    r = lax.rsqrt(jnp.mean(x * x, axis=-1, keepdims=True) + EPS)
    return r, x * r


def _rms_bwd(dxh, xh, r):
    return r * (dxh - xh * jnp.mean(dxh * xh, axis=-1, keepdims=True))


def _fwd_proj_first(x, g_mix, w_int3, block):
    L, D = x.shape
    tn = w_int3.shape[1]
    tm = min(L, 1024)

    def body(blk_ref, x_ref, g_ref, w_ref, h_ref, p_ref):
        _, xh = _rms_stats(x_ref[...])
        h = (xh * g_ref[...]).astype(BF16)
        h_ref[...] = h
        p_ref[...] = _nt(h, w_ref[...])

    return _pallas_call(
        body, name="fwd_proj_own",
        grid_spec=pltpu.PrefetchScalarGridSpec(
            num_scalar_prefetch=1, grid=(L // tm,),
            in_specs=[pl.BlockSpec((tm, D), lambda i, blk: (i, 0)),
                      pl.BlockSpec((1, D), lambda i, blk: (0, 0)),
                      pl.BlockSpec((None, tn, D), lambda i, blk: (blk[0], 0, 0))],
            out_specs=[pl.BlockSpec((tm, D), lambda i, blk: (i, 0)),
                       pl.BlockSpec((tm, tn), lambda i, blk: (i, blk[0]))]),
        out_shape=[_sds((L, D), BF16), _sds((L, N_CHIPS * tn), F32)],
        compiler_params=_params(("parallel",), 48),
    )(block, x, g_mix, w_int3)


def _fwd_proj_more(name, h, w_int3, proj, blocks):
    L, D = h.shape
    tn = w_int3.shape[1]
    tm = min(L, 1024)

    def body(blk_ref, h_ref, w_ref, proj_ref, p_ref):
        p_ref[...] = _nt(h_ref[...], w_ref[...])

    return _pallas_call(
        body, name=name,
        grid_spec=pltpu.PrefetchScalarGridSpec(
            num_scalar_prefetch=1, grid=(L // tm, blocks.shape[0]),
            in_specs=[pl.BlockSpec((tm, D), lambda i, j, blk: (i, 0)),
                      pl.BlockSpec((None, tn, D), lambda i, j, blk: (blk[j], 0, 0)), ANY],
            out_specs=pl.BlockSpec((tm, tn), lambda i, j, blk: (i, blk[j]))),
        out_shape=_sds(proj.shape, proj.dtype),
        input_output_aliases={3: 0},
        compiler_params=_params(("parallel", "arbitrary"), 48),
    )(blocks, h, w_int3, proj)


def _lower_bound(lbp):
    l0, l1 = lbp[0:1, :], lbp[1:2, :]
    m = jnp.maximum(l0, l1)
    e0, e1 = jnp.exp(l0 - m), jnp.exp(l1 - m)
    return e0 / (e0 + e1)


def _seg_scan(x, r32, forward):
    n = x.shape[0]
    s = 1
    while s < CHUNK:
        if forward:
            x = x + jnp.where(r32 >= s, pltpu.roll(x, s, 0), 0.0)
        else:
            x = x + jnp.where(r32 < CHUNK - s, pltpu.roll(x, n - s, 0), 0.0)
        s *= 2
    return x


def _bcast_row(x, row):
    n, w = x.shape
    nc = n // CHUNK
    x3 = x.reshape(nc, CHUNK, w)
    return jnp.broadcast_to(x3[:, row:row + 1, :], (nc, CHUNK, w)).reshape(n, w)


def _hgrn_prep(q_raw, f_raw, lb):
    r32 = lax.broadcasted_iota(jnp.int32, f_raw.shape, 0) & (CHUNK - 1)
    sig = _sigmoid(f_raw)
    f = lb + (1.0 - lb) * sig
    b = _seg_scan(jnp.log(f), r32, True)
    a = _bcast_row(b, CHUNK // 2 - 1)
    bl = _bcast_row(b, CHUNK - 1)
    sq = _sigmoid(q_raw)
    q = q_raw * sq * (HEAD_DIM ** -0.5)
    return dict(r32=r32, sig=sig, f=f, k=1.0 - f, b=b, a=a, bl=bl, sq=sq, q=q)


def _chunk_masks(n):
    ri = lax.broadcasted_iota(jnp.int32, (n, n), 0)
    ci = lax.broadcasted_iota(jnp.int32, (n, n), 1)
    same = (ri // CHUNK) == (ci // CHUNK)
    return same & (ci <= ri), same & (ri <= ci)


def _hgrn_fwd(proj, lower_bounds, gamma, H):
    L = proj.shape[0]
    nh = H // HEAD_DIM
    TL = min(L, 256)
    nc = TL // CHUNK

    def body(q_ref, f_ref, v_ref, g_ref, lbp_ref, gam_ref, og_ref, o_ref, s_ref, st_ref):
        @pl.when(pl.program_id(0) == 0)
        def _():
            st_ref[...] = jnp.zeros_like(st_ref)

        lb = _lower_bound(lbp_ref[...])
        gam = gam_ref[...]
        mask, _ = _chunk_masks(TL)
        rowc = lax.broadcasted_iota(jnp.int32, (TL, HEAD_DIM), 0) // CHUNK
        for h in range(nh):
            hs = slice(h * HEAD_DIM, (h + 1) * HEAD_DIM)
            p = _hgrn_prep(q_ref[:, hs], f_ref[:, hs], lb[:, hs])
            v = v_ref[:, hs]
            vb = v.astype(BF16)
            vt = v.T.astype(BF16)
            q_hat = (p["q"] * jnp.exp(p["b"] - p["a"])).astype(BF16)
            k_hat = (p["k"] * jnp.exp(p["a"] - p["b"])).astype(BF16)
            q_in = (p["q"] * jnp.exp(p["b"])).astype(BF16)
            k_out = (p["k"] * jnp.exp(p["bl"] - p["b"])).astype(BF16)
            dec = jnp.exp(p["bl"])
            att = jnp.where(mask, _nt(q_hat, k_hat), 0.0).astype(BF16)
            o_intra = _nn(att, vb)
            st = st_ref[h]
            for c in range(nc):
                rs = slice(c * CHUNK, (c + 1) * CHUNK)
                stb = st.astype(BF16)
                s_ref[c, h] = stb
                o_ref[rs, hs] = o_intra[rs] + _nt(q_in[rs], stb)
                k_c = jnp.where(rowc == c, k_out, jnp.zeros_like(k_out))
                st = st * dec[c * CHUNK:c * CHUNK + 1, :] + _nn(vt, k_c)
            st_ref[h] = st
            o = o_ref[:, hs]
            _, xh = _rms_stats(o)
            gr = g_ref[:, hs]
            og_ref[:, hs] = (xh * gam * (gr * _sigmoid(gr))).astype(BF16)

    col = lambda k: pl.BlockSpec((TL, H), lambda i, k=k: (i, k))
    return _pallas_call(
        body, name="hgrn_fwd", grid=(L // TL,),
        in_specs=[col(0), col(1), col(2), col(3),
                  pl.BlockSpec(lower_bounds.shape, lambda i: (0, 0)),
                  pl.BlockSpec(gamma.shape, lambda i: (0, 0))],
        out_specs=[pl.BlockSpec((TL, H), lambda i: (i, 0)),
                   pl.BlockSpec((TL, H), lambda i: (i, 0)),
                   pl.BlockSpec((nc, nh, HEAD_DIM, HEAD_DIM), lambda i: (i, 0, 0, 0))],
        out_shape=[_sds((L, H), BF16), _sds((L, H), F32),
                   _sds((L // CHUNK, nh, HEAD_DIM, HEAD_DIM), BF16)],
        scratch_shapes=[pltpu.VMEM((nh, HEAD_DIM, HEAD_DIM), F32)],
        compiler_params=_params(("arbitrary",), 48),
    )(proj, proj, proj, proj, lower_bounds, gamma)


def _hgrn_bwd(proj, lower_bounds, gamma, o_pre, d_out, s_saved, H, after):
    L = proj.shape[0]
    nh = H // HEAD_DIM
    TL = min(L, 256)
    nc = TL // CHUNK
    nt = L // TL

    def body(q_ref, f_ref, v_ref, g_ref, lbp_ref, gam_ref, o_ref, d_ref, s_ref, after_ref,
             dq_ref, df_ref, dv_ref, dg_ref, red_ref, dst_ref, dsall_ref, tmp_ref):
        @pl.when(pl.program_id(0) == 0)
        def _():
            dst_ref[...] = jnp.zeros_like(dst_ref)
            red_ref[...] = jnp.zeros_like(red_ref)

        lb = _lower_bound(lbp_ref[...])
        gam = gam_ref[...]
        mask, mask_t = _chunk_masks(TL)
        rowc = lax.broadcasted_iota(jnp.int32, (TL, HEAD_DIM), 0) // CHUNK
        for h in range(nh):
            hs = slice(h * HEAD_DIM, (h + 1) * HEAD_DIM)
            qr, gr, lbh = q_ref[:, hs], g_ref[:, hs], lb[:, hs]
            p = _hgrn_prep(qr, f_ref[:, hs], lbh)
            vb = v_ref[:, hs].astype(BF16)
            eba, eab = jnp.exp(p["b"] - p["a"]), jnp.exp(p["a"] - p["b"])
            eb, elb = jnp.exp(p["b"]), jnp.exp(p["bl"] - p["b"])
            dec = jnp.exp(p["bl"])
            q_hat, k_hat = p["q"] * eba, p["k"] * eab
            q_in, k_out = p["q"] * eb, p["k"] * elb
            q_hat_b, k_hat_b = q_hat.astype(BF16), k_hat.astype(BF16)
            q_in_b, k_out_b = q_in.astype(BF16), k_out.astype(BF16)

            o, dout = o_ref[:, hs], d_ref[:, hs]
            sg = _sigmoid(gr)
            r, xh = _rms_stats(o)
            dg_ref[:, hs] = (dout * (xh * gam) * (sg * (1.0 + gr * (1.0 - sg)))).astype(BF16)
            dn = dout * (gr * sg)
            red_ref[1:2, hs] += jnp.sum(dn * xh, axis=0, keepdims=True)
            do = _rms_bwd(dn * gam, xh, r)
            dob = do.astype(BF16)
            dot_b = do.T.astype(BF16)

            att_t = jnp.where(mask_t, _nt(k_hat_b, q_hat_b), 0.0).astype(BF16)
            dv_intra = _nn(att_t, dob)
            datt = jnp.where(mask, _nt(dob, vb), 0.0).astype(BF16)
            dqh = _nn(datt, k_hat_b)
            datt_t = jnp.where(mask_t, _nt(vb, dob), 0.0).astype(BF16)
            dkh = _nn(datt_t, q_hat_b)

            dst = dst_ref[h]
            for c in reversed(range(nc)):
                dsall_ref[c] = dst
                q_c = jnp.where(rowc == c, q_in_b, jnp.zeros_like(q_in_b))
                dst = dst * dec[c * CHUNK:c * CHUNK + 1, :] + _nn(dot_b, q_c)
            dst_ref[h] = dst
            for c in range(nc):
                rs = slice(c * CHUNK, (c + 1) * CHUNK)
                ds_c = dsall_ref[c]
                dsb = ds_c.astype(BF16)
                st_prev = s_ref[c, h]
                tmp_ref[0, rs, :] = _nt(k_out_b[rs], dsb)
                tmp_ref[1, rs, :] = _nn(vb[rs], dsb)
                tmp_ref[2, rs, :] = _nn(dob[rs], st_prev)
                ddec = jnp.sum(ds_c * st_prev.astype(F32), axis=0, keepdims=True)
                tmp_ref[3, rs, :] = jnp.broadcast_to(ddec * dec[c * CHUNK:c * CHUNK + 1, :],
                                                     (CHUNK, HEAD_DIM))
            dko, dqi = tmp_ref[1], tmp_ref[2]
            dq = dqh * eba + dqi * eb
            dk = dkh * eab + dko * elb
            tko = dko * k_out
            db = dqh * q_hat - dkh * k_hat + dqi * q_in - tko
            dlog = (_seg_scan(db, p["r32"], False)
                    + _bcast_row(_seg_scan(tko, p["r32"], True), CHUNK - 1) + tmp_ref[3])
            df = dlog / p["f"] - dk
            sig = p["sig"]
            red_ref[0:1, hs] += jnp.sum(df * (1.0 - sig), axis=0, keepdims=True)
            df_ref[:, hs] = (df * (1.0 - lbh) * sig * (1.0 - sig)).astype(BF16)
            sq = p["sq"]
            dq_ref[:, hs] = (dq * (HEAD_DIM ** -0.5) * (sq * (1.0 + qr * (1.0 - sq)))).astype(BF16)
            dv_ref[:, hs] = (dv_intra + tmp_ref[0]).astype(BF16)

    col = lambda k: pl.BlockSpec((TL, H), lambda i, k=k: (nt - 1 - i, k))
    rev = pl.BlockSpec((TL, H), lambda i: (nt - 1 - i, 0))
    return _pallas_call(
        body, name="hgrn_bwd", grid=(nt,),
        in_specs=[col(0), col(1), col(2), col(3),
                  pl.BlockSpec(lower_bounds.shape, lambda i: (0, 0)),
                  pl.BlockSpec(gamma.shape, lambda i: (0, 0)),
                  rev, rev,
                  pl.BlockSpec((nc, nh, HEAD_DIM, HEAD_DIM), lambda i: (nt - 1 - i, 0, 0, 0)), ANY],
        out_specs=[rev, rev, rev, rev, pl.BlockSpec((8, H), lambda i: (0, 0))],
        out_shape=[_sds((L, H), BF16)] * 4 + [_sds((8, H), F32)],
        scratch_shapes=[pltpu.VMEM((nh, HEAD_DIM, HEAD_DIM), F32),
                        pltpu.VMEM((nc, HEAD_DIM, HEAD_DIM), F32),
                        pltpu.VMEM((4, TL, HEAD_DIM), F32)],
        compiler_params=_params(("arbitrary",), 48),
    )(proj, proj, proj, proj, lower_bounds, gamma, o_pre, d_out, s_saved, after)


def _shift_down(u, s, row):
    return jnp.where(row >= s, pltpu.roll(u, s, 0), 0.0)


def _shift_up(u, s, row):
    n = u.shape[0]
    return jnp.where(row < n - s, pltpu.roll(u, n - s, 0), 0.0)


def _conv_specs(L, H):
    per = H // LANES
    return [pl.BlockSpec((L, LANES), lambda j, o=o: (0, o * per + j)) for o in (4, 5, 6)]


def _conv_fwd(proj, conv_w, H, after):
    L = proj.shape[0]

    def body(c_ref, b_ref, x_ref, w_ref, after_ref, o_ref):
        row = lax.broadcasted_iota(jnp.int32, (L, LANES), 0)
        u = c_ref[...] * x_ref[...]
        w = w_ref[...]
        y = w[0:1] * _shift_down(u, 2, row) + w[1:2] * _shift_down(u, 1, row) + w[2:3] * u
        o_ref[...] = (b_ref[...] * y).astype(BF16)

    return _pallas_call(
        body, name="conv_fwd", grid=(H // LANES,),
        in_specs=_conv_specs(L, H) + [pl.BlockSpec((3, LANES), lambda j: (0, j)), ANY],
        out_specs=pl.BlockSpec((L, LANES), lambda j: (0, j)),
        out_shape=_sds((L, H), BF16),
        compiler_params=_params(("parallel",), 48),
    )(proj, proj, proj, conv_w, after)


def _conv_bwd(proj, conv_w, dcb, H, after):
    L = proj.shape[0]

    def body(c_ref, b_ref, x_ref, w_ref, d_ref, after_ref, dc_ref, db_ref, dx_ref, dw_ref):
        row = lax.broadcasted_iota(jnp.int32, (L, LANES), 0)
        cg, xb = c_ref[...], x_ref[...]
        u = cg * xb
        u1, u2 = _shift_down(u, 1, row), _shift_down(u, 2, row)
        w = w_ref[...]
        y = w[0:1] * u2 + w[1:2] * u1 + w[2:3] * u
        d = d_ref[...]
        db_ref[...] = (d * y).astype(BF16)
        dy = d * b_ref[...]
        du = w[2:3] * dy + w[1:2] * _shift_up(dy, 1, row) + w[0:1] * _shift_up(dy, 2, row)
        dw_ref[0:1, :] = jnp.sum(dy * u2, axis=0, keepdims=True)
        dw_ref[1:2, :] = jnp.sum(dy * u1, axis=0, keepdims=True)
        dw_ref[2:3, :] = jnp.sum(dy * u, axis=0, keepdims=True)
        dc_ref[...] = (du * xb).astype(BF16)
        dx_ref[...] = (du * cg).astype(BF16)

    blk = pl.BlockSpec((L, LANES), lambda j: (0, j))
    return _pallas_call(
        body, name="conv_bwd", grid=(H // LANES,),
        in_specs=_conv_specs(L, H) + [pl.BlockSpec((3, LANES), lambda j: (0, j)), blk, ANY],
        out_specs=[blk, blk, blk, pl.BlockSpec((3, LANES), lambda j: (0, j))],
        out_shape=[_sds((L, H), BF16)] * 3 + [_sds((3, H), F32)],
        compiler_params=_params(("parallel",), 56),
    )(proj, proj, proj, conv_w, dcb, after)


def _gate_specs(tm, H):
    return [pl.BlockSpec((tm, H), lambda i, k=k: (i, k)) for k in (7, 8, 9, 10)]


def _fwd_mix(og, cb, proj, x, wat, wbt, wout, g_ffn, H, after):
    L, D = x.shape
    tm = min(L, 512)

    def body(o_ref, cb_ref, ga0, ga1, gb0, gb1, x_ref, wa_ref, wb_ref, wo_ref, g_ref, after_ref,
             sa_ref, sb_ref, ta_ref, tb_ref, m_ref, x1_ref, h2_ref):
        ya, yb = _nt(o_ref[...], wa_ref[...]), _nt(cb_ref[...], wb_ref[...])
        for k, (gar, gbr) in enumerate(((ga0, gb0), (ga1, gb1))):
            cs = slice(k * H, (k + 1) * H)
            sa, sb = _sigmoid(gar[...]), _sigmoid(gbr[...])
            ma, mb = sa * ya[:, cs], sb * yb[:, cs]
            m_ref[:, cs] = (ma + mb).astype(BF16)
            sa_ref[:, cs] = sa.astype(BF16)
            sb_ref[:, cs] = sb.astype(BF16)
            ta_ref[:, cs] = (ma * (1.0 - sa)).astype(BF16)
            tb_ref[:, cs] = (mb * (1.0 - sb)).astype(BF16)
        x1 = x_ref[...] + _nn(m_ref[...], wo_ref[...])
        x1_ref[...] = x1
        _, xh = _rms_stats(x1)
        h2_ref[...] = (xh * g_ref[...]).astype(BF16)

    row = lambda w: pl.BlockSpec((tm, w), lambda i: (i, 0))
    full = lambda a: pl.BlockSpec(a.shape, lambda i: (0,) * a.ndim)
    return _pallas_call(
        body, name="fwd_mix", grid=(L // tm,),
        in_specs=[row(H), row(H)] + _gate_specs(tm, H) + [row(D), full(wat), full(wbt), full(wout),
                                                           full(g_ffn), ANY],
        out_specs=[row(D)] * 7,
        out_shape=[_sds((L, D), BF16)] * 5 + [_sds((L, D), F32), _sds((L, D), BF16)],
        compiler_params=_params(("parallel",), 56),
    )(og, cb, proj, proj, proj, proj, x, wat, wbt, wout, g_ffn, after)


def _bwd_mix(dx1b, sig_a, sig_b, dm_dga, dm_dgb, wat, wbt, wout, H, after):
    L, D = dx1b.shape
    tm = min(L, 512)

    def body(dx_ref, sa_ref, sb_ref, ta_ref, tb_ref, wa_ref, wb_ref, wo_ref, after_ref,
             dya_ref, dyb_ref, dga_ref, dgb_ref, do_ref, dcb_ref):
        dm = _nt(dx_ref[...], wo_ref[...])
        dga_ref[...] = (dm * ta_ref[...].astype(F32)).astype(BF16)
        dgb_ref[...] = (dm * tb_ref[...].astype(F32)).astype(BF16)
        dya_ref[...] = (dm * sa_ref[...].astype(F32)).astype(BF16)
        dyb_ref[...] = (dm * sb_ref[...].astype(F32)).astype(BF16)
        do_ref[...] = _nn(dya_ref[...], wa_ref[...])
        dcb_ref[...] = _nn(dyb_ref[...], wb_ref[...])

    row = lambda w: pl.BlockSpec((tm, w), lambda i: (i, 0))
    full = lambda a: pl.BlockSpec(a.shape, lambda i: (0,) * a.ndim)
    return _pallas_call(
        body, name="bwd_mix", grid=(L // tm,),
        in_specs=[row(D)] * 5 + [full(wat), full(wbt), full(wout), ANY],
        out_specs=[row(D)] * 4 + [row(H)] * 2,
        out_shape=[_sds((L, D), BF16)] * 4 + [_sds((L, H), F32)] * 2,
        compiler_params=_params(("parallel",), 56),
    )(dx1b, sig_a, sig_b, dm_dga, dm_dgb, wat, wbt, wout, after)


def _fwd_ffn_up(h2, wgt, wut):
    L, D = h2.shape
    F = wgt.shape[0]
    tn = F // 2
    tm = min(L, 512)

    def body(h_ref, wg_ref, wu_ref, sa_ref, sb_ref, s_ref):
        h = h_ref[...]
        a, b = _nt(h, wg_ref[...]), _nt(h, wu_ref[...])
        sg = _sigmoid(a)
        silu = a * sg
        sa_ref[...] = (b * sg * (1.0 + a * (1.0 - sg))).astype(BF16)
        sb_ref[...] = silu.astype(BF16)
        s_ref[...] = (silu * b).astype(BF16)

    wspec = pl.BlockSpec((tn, D), lambda j, i: (j, 0))
    ospec = pl.BlockSpec((tm, tn), lambda j, i: (i, j))
    return _pallas_call(
        body, name="fwd_ffn_up", grid=(2, L // tm),
        in_specs=[pl.BlockSpec((tm, D), lambda j, i: (i, 0)), wspec, wspec],
        out_specs=[ospec] * 3,
        out_shape=[_sds((L, F), BF16)] * 3,
        compiler_params=_params(("parallel", "parallel"), 48),
    )(h2, wgt, wut)


def _fwd_down_loss(s, wd, x1, target, g_final):
    L, D = x1.shape
    F = wd.shape[0]
    tm = min(L, 256)

    def body(s_ref, wd_ref, x1_ref, t_ref, g_ref, dx_ref, dxb_ref, red_ref):
        @pl.when(pl.program_id(0) == 0)
        def _():
            red_ref[...] = jnp.zeros_like(red_ref)

        g = g_ref[...]
        r, xh = _rms_stats(x1_ref[...] + _nn(s_ref[...], wd_ref[...]))
        e = xh * g - t_ref[...]
        dy = e * (1.0 / D)
        dx = _rms_bwd(dy * g, xh, r)
        dx_ref[...] = dx
        dxb_ref[...] = dx.astype(BF16)
        red_ref[0:1, :] += jnp.sum(dy * xh, axis=0, keepdims=True)
        red_ref[1:2, :] += jnp.broadcast_to(0.5 * jnp.sum(e * e) * (1.0 / D), (1, D))

    row = pl.BlockSpec((tm, D), lambda i: (i, 0))
    return _pallas_call(
        body, name="fwd_down_loss", grid=(L // tm,),
        in_specs=[pl.BlockSpec((tm, F), lambda i: (i, 0)), pl.BlockSpec((F, D), lambda i: (0, 0)),
                  row, row, pl.BlockSpec((1, D), lambda i: (0, 0))],
        out_specs=[row, row, pl.BlockSpec((8, D), lambda i: (0, 0))],
        out_shape=[_sds((L, D), F32), _sds((L, D), BF16), _sds((8, D), F32)],
        compiler_params=_params(("arbitrary",), 48),
    )(s, wd, x1, target, g_final)


def _bwd_down(dx2b, wd, s_a, s_b):
    L, D = dx2b.shape
    F = wd.shape[0]
    tn = F // 2
    tm = min(L, 512)

    def body(dx_ref, wd_ref, sa_ref, sb_ref, da_ref, db_ref):
        ds = _nt(dx_ref[...], wd_ref[...])
        da_ref[...] = (ds * sa_ref[...].astype(F32)).astype(BF16)
        db_ref[...] = (ds * sb_ref[...].astype(F32)).astype(BF16)

    ospec = pl.BlockSpec((tm, tn), lambda j, i: (i, j))
    return _pallas_call(
        body, name="bwd_down", grid=(2, L // tm),
        in_specs=[pl.BlockSpec((tm, D), lambda j, i: (i, 0)),
                  pl.BlockSpec((tn, D), lambda j, i: (j, 0)), ospec, ospec],
        out_specs=[ospec] * 2,
        out_shape=[_sds((L, F), BF16)] * 2,
        compiler_params=_params(("parallel", "parallel"), 48),
    )(dx2b, wd, s_a, s_b)


def _bwd_ffn_dh(da, db, wgt, wut, x1, dx2, g_ffn, after):
    L, D = x1.shape
    F = wgt.shape[0]
    tm = min(L, 256)

    def body(da_ref, db_ref, wg_ref, wu_ref, x1_ref, dx2_ref, g_ref, after_ref, dx_ref, dxb_ref, red_ref):
        @pl.when(pl.program_id(0) == 0)
        def _():
            red_ref[...] = jnp.zeros_like(red_ref)

        dh = _nn(da_ref[...], wg_ref[...]) + _nn(db_ref[...], wu_ref[...])
        r, xh = _rms_stats(x1_ref[...])
        red_ref[0:1, :] += jnp.sum(dh * xh, axis=0, keepdims=True)
        dx = dx2_ref[...] + _rms_bwd(dh * g_ref[...], xh, r)
        dx_ref[...] = dx
        dxb_ref[...] = dx.astype(BF16)

    row = pl.BlockSpec((tm, D), lambda i: (i, 0))
    aspec = pl.BlockSpec((tm, F), lambda i: (i, 0))
    wspec = pl.BlockSpec((F, D), lambda i: (0, 0))
    return _pallas_call(
        body, name="bwd_ffn_dh", grid=(L // tm,),
        in_specs=[aspec, aspec, wspec, wspec, row, row, pl.BlockSpec((1, D), lambda i: (0, 0)), ANY],
        out_specs=[row, row, pl.BlockSpec((8, D), lambda i: (0, 0))],
        out_shape=[_sds((L, D), F32), _sds((L, D), BF16), _sds((8, D), F32)],
        compiler_params=_params(("arbitrary",), 56),
    )(da, db, wgt, wut, x1, dx2, g_ffn, after)


def _bwd_in(dproj, w_int, x, dx1, g_mix, after):
    L, D = x.shape
    N = w_int.shape[0]
    tm = min(L, 256)

    def body(dp_ref, w_ref, x_ref, dx1_ref, g_ref, after_ref, dx_ref, red_ref):
        @pl.when(pl.program_id(0) == 0)
        def _():
            red_ref[...] = jnp.zeros_like(red_ref)

        dh = _nn(dp_ref[...], w_ref[...])
        r, xh = _rms_stats(x_ref[...])
        red_ref[0:1, :] += jnp.sum(dh * xh, axis=0, keepdims=True)
        dx_ref[...] = dx1_ref[...] + _rms_bwd(dh * g_ref[...], xh, r)

    row = pl.BlockSpec((tm, D), lambda i: (i, 0))
    return _pallas_call(
        body, name="bwd_in", grid=(L // tm,),
        in_specs=[pl.BlockSpec((tm, N), lambda i: (i, 0)), pl.BlockSpec((N, D), lambda i: (0, 0)),
                  row, row, pl.BlockSpec((1, D), lambda i: (0, 0)), ANY],
        out_specs=[row, pl.BlockSpec((8, D), lambda i: (0, 0))],
        out_shape=[_sds((L, D), F32), _sds((8, D), F32)],
        compiler_params=_params(("arbitrary",), 56),
    )(dproj, w_int, x, dx1, g_mix, after)


def _mm_tn(name, a, b, a_spec, b_spec, o_block, n_out, n_k):
    def body(a_ref, b_ref, o_ref):
        part = _tn(a_ref[...], b_ref[...])

        @pl.when(pl.program_id(1) == 0)
        def _():
            o_ref[...] = part

        @pl.when(pl.program_id(1) > 0)
        def _():
            o_ref[...] += part

    return _pallas_call(
        body, name=name, grid=(n_out, n_k),
        in_specs=[a_spec, b_spec],
        out_specs=pl.BlockSpec((None,) + o_block, lambda j, k: (j, 0, 0)),
        out_shape=_sds((n_out,) + o_block, F32),
        compiler_params=_params(("parallel", "arbitrary"), 56),
    )(a, b)


TK_TOKENS = 2048


def _dw_cols(name, a, b, n_cols):
    L, M = a.shape
    tk = min(L, TK_TOKENS)
    return _mm_tn(name, a, b, pl.BlockSpec((tk, M), lambda j, k: (k, 0)),
                  pl.BlockSpec((tk, n_cols), lambda j, k: (k, j)), (M, n_cols), N_CHIPS, L // tk)


def _dw_rows(name, a, b):
    L, M = a.shape
    N = b.shape[1]
    tk = min(L, TK_TOKENS)
    return _mm_tn(name, a, b, pl.BlockSpec((tk, M // N_CHIPS), lambda j, k: (k, j)),
                  pl.BlockSpec((tk, N), lambda j, k: (k, 0)), (M // N_CHIPS, N), N_CHIPS, L // tk)


def _dw_rows2(name, a, b):
    L, M = a.shape
    N = b.shape[1]
    tk = min(L, TK_TOKENS)
    return _mm_tn(name, a, b, pl.BlockSpec((tk, M // 2), lambda j, k: (k, j)),
                  pl.BlockSpec((tk, N), lambda j, k: (k, 0)), (M // 2, N), 2, L // tk)


def _place():
    x, y, c = lax.axis_index("x"), lax.axis_index("y"), lax.axis_index("c")
    chips = [(1 - x, y), (x, 1 - y), (1 - x, 1 - y)]
    return x, y, c, 2 * x + y, chips


def _remote(src, dst, send_sem, recv_sem, device):
    return pltpu.make_async_remote_copy(src_ref=src, dst_ref=dst, send_sem=send_sem,
                                        recv_sem=recv_sem, device_id=device, device_id_type=MESH)


def _half(ref, lead, c, r2):
    return ref.at[lead, pl.ds(pl.multiple_of(c * r2, 16), r2), :]


def _cast_place(name, w, chip_idx):
    r, cols = w.shape
    tr = r // 2

    def body(k_ref, w_ref, o_ref):
        o_ref[...] = w_ref[...].astype(BF16)

    return _pallas_call(
        body, name=name,
        grid_spec=pltpu.PrefetchScalarGridSpec(
            num_scalar_prefetch=1, grid=(2,),
            in_specs=[pl.BlockSpec((tr, cols), lambda i, k_ref: (i, 0))],
            out_specs=pl.BlockSpec((None, tr, cols), lambda i, k_ref: (k_ref[0], i, 0))),
        out_shape=_sds((N_CHIPS, r, cols), BF16),
        compiler_params=_params(("parallel",), 48),
    )(chip_idx, w)


def _cast_place_t(name, w, chip_idx):
    r, cols = w.shape

    def body(k_ref, w_ref, o_ref):
        o_ref[...] = w_ref[...].T.astype(BF16)

    return _pallas_call(
        body, name=name,
        grid_spec=pltpu.PrefetchScalarGridSpec(
            num_scalar_prefetch=1, grid=(cols // LANES,),
            in_specs=[pl.BlockSpec((r, LANES), lambda i, k_ref: (0, i))],
            out_specs=pl.BlockSpec((None, LANES, r), lambda i, k_ref: (k_ref[0], i, 0))),
        out_shape=_sds((N_CHIPS, cols, r), BF16),
        compiler_params=_params(("parallel",), 48),
    )(chip_idx, w)


def _gather_copies(bufs, whole, send_sems, recv_sems, select=None):
    x, y, c, k, chips = _place()
    pairs = []
    for w, buf in enumerate(bufs):
        for j, (cx, cy) in enumerate(chips):
            if select is not None and not select(w, j):
                continue
            if w in whole:
                mine, theirs = buf.at[k], buf.at[2 * cx + cy]
            else:
                r2 = buf.shape[1] // 2
                mine, theirs = _half(buf, k, c, r2), _half(buf, 2 * cx + cy, c, r2)
            sems = (send_sems.at[w * 3 + j], recv_sems.at[w * 3 + j])
            pairs.append((_remote(mine, mine, *sems, (cx, cy, c)), _remote(theirs, theirs, *sems, (x, y, c))))
    return pairs


def _gather_start(groups):
    flat = [b for bufs, _ in groups for b in bufs]
    nb, ng = len(flat), len(groups)

    def body(*refs):
        ins, sems, token = refs[:nb], refs[nb:nb + 2 * ng], refs[-1]
        pos = 0
        for g, (bufs, whole) in enumerate(groups):
            for send, _ in _gather_copies(ins[pos:pos + len(bufs)], whole, sems[2 * g], sems[2 * g + 1]):
                send.start()
            pos += len(bufs)
        token[...] = jnp.zeros_like(token)

    sem_shapes = []
    for bufs, _ in groups:
        sem_shapes += [pltpu.SemaphoreType.DMA((3 * len(bufs),))] * 2
    out = _pallas_call(
        body, name="gather_start",
        in_specs=[HBM] * nb, out_specs=tuple([SEM] * (2 * ng) + [HBM] * nb + [VMEM]),
        out_shape=tuple(sem_shapes + [pltpu.HBM(b.shape, b.dtype) for b in flat] + [_sds((8, LANES), F32)]),
        input_output_aliases={i: 2 * ng + i for i in range(nb)},
        compiler_params=pltpu.CompilerParams(has_side_effects=EFFECT),
    )(*flat)
    sems, thru, pos = [], [], 2 * ng
    for g, (bufs, _) in enumerate(groups):
        sems.append((out[2 * g], out[2 * g + 1]))
        thru.append(list(out[pos:pos + len(bufs)]))
        pos += len(bufs)
    return sems, thru, out[-1]


def _gather_wait(name, bufs, whole, sems, after, select=None):
    nb = len(bufs)

    def body(*refs):
        ins, send_sems, recv_sems = refs[:nb], refs[nb], refs[nb + 1]
        for send, arrival in _gather_copies(ins, whole, send_sems, recv_sems, select):
            send.wait_send()
            arrival.wait_recv()

    return _pallas_call(
        body, name=name,
        in_specs=[HBM] * nb + [SEM, SEM, ANY], out_specs=[HBM] * nb,
        out_shape=[pltpu.HBM(b.shape, b.dtype) for b in bufs],
        input_output_aliases={i: i for i in range(nb)},
        compiler_params=pltpu.CompilerParams(has_side_effects=EFFECT),
    )(*bufs, sems[0], sems[1], after)


def _gather_forward(name, bufs, sources=(0, 1, 2)):
    n = len(bufs)

    def body(*refs):
        outs = refs[n:2 * n]
        send_sems, recv_sems = refs[2 * n:]
        x, y, c, _, chips = _place()
        sends = []
        for w in range(n):
            r2 = outs[w].shape[1] // 2
            for j in sources:
                landed = _half(outs[w], 2 * chips[j][0] + chips[j][1], c, r2)
                sends.append(_remote(landed, landed, send_sems.at[w * 3 + j], recv_sems.at[w * 3 + j],
                                     (x, y, 1 - c)))
        for cp in sends:
            cp.start()
        for w in range(n):
            r2 = outs[w].shape[1] // 2
            for j in sources:
                got = _half(outs[w], 2 * chips[j][0] + chips[j][1], 1 - c, r2)
                _remote(got, got, send_sems.at[w * 3 + j], recv_sems.at[w * 3 + j], (x, y, c)).wait_recv()
        for cp in sends:
            cp.wait_send()

    return _pallas_call(
        body, name=name,
        in_specs=[ANY] * n, out_specs=[ANY] * n,
        out_shape=[_sds(b.shape, b.dtype) for b in bufs],
        input_output_aliases={i: i for i in range(n)},
        scratch_shapes=[pltpu.SemaphoreType.DMA((n * 3,)), pltpu.SemaphoreType.DMA((n * 3,))],
    )(*bufs)


def _rs_sibling(name, grads):
    n = len(grads)

    def body(*refs):
        ins, outs = refs[:n], refs[n:2 * n]
        send_sems, recv_sems = refs[2 * n:]
        x, y, c, _, _ = _place()
        copies = []
        for w in range(n):
            r2 = ins[w].shape[1] // 2
            copies.append(_remote(_half(ins[w], slice(None), 1 - c, r2), outs[w],
                                  send_sems.at[w], recv_sems.at[w], (x, y, 1 - c)))
        for cp in copies:
            cp.start()
        for cp in copies:
            cp.wait()

    return _pallas_call(
        body, name=name,
        in_specs=[ANY] * n, out_specs=[ANY] * n,
        out_shape=[_sds((N_CHIPS, g.shape[1] // 2, g.shape[2]), F32) for g in grads],
        scratch_shapes=[pltpu.SemaphoreType.DMA((n,)), pltpu.SemaphoreType.DMA((n,))],
    )(*grads)


def _rs_add(name, grad3, from_sibling, c_idx):
    _, r2, cols = from_sibling.shape

    def body(c_ref, g_ref, s_ref, o_ref):
        o_ref[...] = (g_ref[...] + s_ref[...]).astype(BF16)

    return _pallas_call(
        body, name=name,
        grid_spec=pltpu.PrefetchScalarGridSpec(
            num_scalar_prefetch=1, grid=(N_CHIPS,),
            in_specs=[pl.BlockSpec((None, r2, cols), lambda k, c_ref: (k, c_ref[0], 0)),
                      pl.BlockSpec((None, r2, cols), lambda k, c_ref: (k, 0, 0))],
            out_specs=pl.BlockSpec((None, r2, cols), lambda k, c_ref: (k, 0, 0))),
        out_shape=_sds(from_sibling.shape, BF16),
        compiler_params=_params(("parallel",), 48),
    )(c_idx, grad3, from_sibling)


def _split_start(name, arrays, n_sems, pairs_fn):
    n = len(arrays)

    def body(*refs):
        for send, _ in pairs_fn(refs[:n], refs[n], refs[n + 1]):
            send.start()
        refs[-1][...] = jnp.zeros_like(refs[-1])

    out = _pallas_call(
        body, name=name,
        in_specs=[HBM] * n, out_specs=tuple([SEM, SEM] + [HBM] * n + [VMEM]),
        out_shape=tuple([pltpu.SemaphoreType.DMA((n_sems,))] * 2 + [pltpu.HBM(a.shape, a.dtype) for a in arrays]
                        + [_sds((8, LANES), F32)]),
        input_output_aliases={i: 2 + i for i in range(n)},
        compiler_params=pltpu.CompilerParams(has_side_effects=EFFECT),
    )(*arrays)
    return (out[0], out[1]), list(out[2:2 + n]), out[-1]


def _split_wait(name, sems, arrays, pairs_fn, after):
    n = len(arrays)

    def body(*refs):
        for send, arrival in pairs_fn(refs[:n], refs[n], refs[n + 1]):
            send.wait_send()
            arrival.wait_recv()

    return list(_pallas_call(
        body, name=name,
        in_specs=[HBM] * n + [SEM, SEM, ANY], out_specs=[HBM] * n,
        out_shape=[pltpu.HBM(a.shape, a.dtype) for a in arrays],
        input_output_aliases={i: i for i in range(n)},
        compiler_params=pltpu.CompilerParams(has_side_effects=EFFECT),
    )(*arrays, sems[0], sems[1], after))


def _forward_pairs(bufs, send_sems, recv_sems):
    x, y, c, _, chips = _place()
    pairs = []
    for w, buf in enumerate(bufs):
        r2 = buf.shape[1] // 2
        for j, (cx, cy) in enumerate(chips):
            landed, theirs = _half(buf, 2 * cx + cy, c, r2), _half(buf, 2 * cx + cy, 1 - c, r2)
            sems = (send_sems.at[w * 3 + j], recv_sems.at[w * 3 + j])
            pairs.append((_remote(landed, landed, *sems, (x, y, 1 - c)), _remote(theirs, theirs, *sems, (x, y, c))))
    return pairs


def _sibling_pairs(arrays, send_sems, recv_sems):
    x, y, c, _, _ = _place()
    n = len(arrays) // 2
    pairs = []
    for w in range(n):
        r2 = arrays[w].shape[1] // 2
        cp = _remote(_half(arrays[w], slice(None), 1 - c, r2), arrays[n + w], send_sems.at[w], recv_sems.at[w],
                     (x, y, 1 - c))
        pairs.append((cp, cp))
    return pairs


def _ici_pairs(arrays, send_sems, recv_sems):
    x, y, c, _, chips = _place()
    n = len(arrays) // 2
    pairs = []
    for w in range(n):
        for j, (cx, cy) in enumerate(chips):
            cp = _remote(arrays[w].at[2 * cx + cy], arrays[n + w].at[j],
                         send_sems.at[w * 3 + j], recv_sems.at[w * 3 + j], (cx, cy, c))
            pairs.append((cp, cp))
    return pairs


def _rs_sum(name, partials, received, place_idx):
    _, r2, cols = partials.shape
    nb = 2
    tr = r2 // nb

    def body(idx_ref, p_ref, r_ref, o_ref):
        o_ref[...] = ((p_ref[...].astype(F32) + r_ref[0].astype(F32))
                      + (r_ref[1].astype(F32) + r_ref[2].astype(F32)))

    return _pallas_call(
        body, name=name,
        grid_spec=pltpu.PrefetchScalarGridSpec(
            num_scalar_prefetch=1, grid=(nb,),
            in_specs=[pl.BlockSpec((None, tr, cols), lambda i, idx: (idx[0], i, 0)),
                      pl.BlockSpec((3, tr, cols), lambda i, idx: (0, i, 0))],
            out_specs=pl.BlockSpec((tr, cols), lambda i, idx: (idx[1] * nb + i, 0))),
        out_shape=_sds((2 * r2, cols), F32),
        compiler_params=_params(("parallel",), 48),
    )(place_idx, partials, received)


def _rs_share(name, shards):
    n = len(shards)

    def body(*refs):
        outs = refs[n:2 * n]
        send_sems, recv_sems = refs[2 * n:]
        x, y, c, _, _ = _place()
        sends = []
        for w in range(n):
            r2 = outs[w].shape[0] // 2
            mine = outs[w].at[pl.ds(pl.multiple_of(c * r2, 8), r2), :]
            sends.append(_remote(mine, mine, send_sems.at[w], recv_sems.at[w], (x, y, 1 - c)))
        for cp in sends:
            cp.start()
        for w in range(n):
            r2 = outs[w].shape[0] // 2
            theirs = outs[w].at[pl.ds(pl.multiple_of((1 - c) * r2, 8), r2), :]
            _remote(theirs, theirs, send_sems.at[w], recv_sems.at[w], (x, y, c)).wait_recv()
        for cp in sends:
            cp.wait_send()

    return _pallas_call(
        body, name=name,
        in_specs=[ANY] * n, out_specs=[ANY] * n,
        out_shape=[_sds(s.shape, F32) for s in shards],
        input_output_aliases={i: i for i in range(n)},
        scratch_shapes=[pltpu.SemaphoreType.DMA((n,)), pltpu.SemaphoreType.DMA((n,))],
    )(*shards)


def _small_allreduce(red_mix, red_ffn, red_final, red_hg, g_conv):
    rows = N_SMALL_ROWS
    D = red_mix.shape[1]
    H = red_hg.shape[1]

    def body(mix_ref, ffn_ref, fin_ref, hg_ref, cv_ref, sum_ref, all_ref, in_ref, send_sems, recv_sems):
        in_ref[...] = jnp.zeros_like(in_ref)
        in_ref[0:1, :] = mix_ref[0:1, :]
        in_ref[1:2, :] = ffn_ref[0:1, :]
        in_ref[2:3, :] = fin_ref[0:1, :]
        gam = hg_ref[1:2, 0:HEAD_DIM]
        for h in range(1, H // HEAD_DIM):
            gam = gam + hg_ref[1:2, h * HEAD_DIM:(h + 1) * HEAD_DIM]
        in_ref[3:4, 0:HEAD_DIM] = gam
        in_ref[3:4, HEAD_DIM:2 * HEAD_DIM] = fin_ref[1:2, 0:HEAD_DIM]
        in_ref[4:5, 0:H] = hg_ref[0:1, :]
        in_ref[6:9, 0:H] = cv_ref[...]
        x, y, c, _, _ = _place()
        me = 4 * x + 2 * y + c
        all_ref[me] = in_ref[...]
        copies = []
        for m in range(1, 8):
            mx, my, mc = (m >> 2) & 1, (m >> 1) & 1, m & 1
            px, py, pc = x ^ mx, y ^ my, c ^ mc
            copies.append((_remote(in_ref, all_ref.at[me], send_sems.at[m - 1], recv_sems.at[m - 1],
                                   (px, py, pc)), 4 * px + 2 * py + pc, m))
        for cp, _, _ in copies:
            cp.start()
        for _, peer, m in copies:
            _remote(in_ref, all_ref.at[peer], send_sems.at[m - 1], recv_sems.at[m - 1],
                    (x, y, c)).wait_recv()
        for cp, _, _ in copies:
            cp.wait_send()
        total = all_ref[0]
        for d in range(1, 8):
            total = total + all_ref[d]
        sum_ref[...] = total

    return _pallas_call(
        body, name="small_allreduce", pin=False,
        in_specs=[VMEM] * 5, out_specs=[VMEM, VMEM],
        out_shape=[_sds((rows, D), F32), _sds((8, rows, D), F32)],
        scratch_shapes=[pltpu.VMEM((rows, D), F32), pltpu.SemaphoreType.DMA((7,)),
                        pltpu.SemaphoreType.DMA((7,))],
    )(red_mix, red_ffn, red_final, red_hg, g_conv)[0]


def _adamw_math(w, g, m, v):
    m = ADAM_B1 * m + (1.0 - ADAM_B1) * g
    v = ADAM_B2 * v + (1.0 - ADAM_B2) * jnp.square(g)
    m_hat = m / (1.0 - ADAM_B1 ** ADAM_STEP)
    v_hat = v / (1.0 - ADAM_B2 ** ADAM_STEP)
    delta = -ADAM_LR * (m_hat / (jnp.sqrt(v_hat) + ADAM_EPS) + ADAM_WD * w)
    return delta, m, v


def _adamw(name, g, w, m, v):
    r, cols = g.shape
    tr = r // 4

    def body(g_ref, w_ref, m_ref, v_ref, go_ref, d_ref, mo_ref, vo_ref):
        g = g_ref[...]
        go_ref[...] = g
        d_ref[...], mo_ref[...], vo_ref[...] = _adamw_math(w_ref[...], g, m_ref[...], v_ref[...])

    blk = pl.BlockSpec((tr, cols), lambda i: (i, 0))
    return _pallas_call(
        body, name=name, grid=(r // tr,),
        in_specs=[blk] * 4, out_specs=[blk] * 4, out_shape=[_sds((r, cols), F32)] * 4,
        compiler_params=_params(("parallel",), 48),
    )(g, w, m, v)


def _small_update(total, chip_idx, ws, ms, vs):
    n = len(ws)
    H = ws[1].shape[1]

    def body(idx_ref, tot_ref, *refs):
        w, m, v, outs = refs[:n], refs[n:2 * n], refs[2 * n:3 * n], refs[3 * n:]
        chip = idx_ref[0]
        p0 = _lower_bound(w[1][...])
        dl0 = p0 * (1.0 - p0) * tot_ref[4:5, 0:H]
        conv = jnp.zeros((3, LANES), F32)
        for k in range(N_CHIPS):
            conv = jnp.where(chip == k, tot_ref[6:9, k * LANES:(k + 1) * LANES], conv)
        grads = [tot_ref[0:1, :], None, tot_ref[3:4, 0:HEAD_DIM], conv, tot_ref[1:2, :], tot_ref[2:3, :]]
        for p in range(n):
            g_ref, d_ref, mo_ref, vo_ref = outs[4 * p:4 * p + 4]
            if p == 1:
                for row, g in ((slice(0, 1), dl0), (slice(1, 2), -dl0)):
                    g_ref[row, :] = g
                    d_ref[row, :], mo_ref[row, :], vo_ref[row, :] = _adamw_math(
                        w[p][row, :], g, m[p][row, :], v[p][row, :])
            else:
                g_ref[...] = grads[p]
                d_ref[...], mo_ref[...], vo_ref[...] = _adamw_math(w[p][...], grads[p], m[p][...], v[p][...])
        outs[4 * n][...] = tot_ref[3:4, HEAD_DIM:2 * HEAD_DIM]

    full = lambda a: pl.BlockSpec(a.shape, lambda i, idx: (0,) * a.ndim)
    out_shape = [_sds(w.shape, F32) for w in ws for _ in range(4)] + [_sds((1, LANES), F32)]
    return _pallas_call(
        body, name="small_update",
        grid_spec=pltpu.PrefetchScalarGridSpec(
            num_scalar_prefetch=1, grid=(1,),
            in_specs=[full(total)] + [full(a) for a in ws + ms + vs],
            out_specs=[full(s) for s in out_shape]),
        out_shape=out_shape,
    )(chip_idx, total, *ws, *ms, *vs)


def kernel(x, norm_mix_g, w_in, lower_bounds, hg_norm_g, conv_w, w_branch_a, w_branch_b, w_out, norm_ffn_g, w_ffn_gate, w_ffn_up, w_ffn_down, norm_final_g, loss_target, m_norm_mix_g, m_w_in, m_lower_bounds, m_hg_norm_g, m_conv_w, m_w_branch_a, m_w_branch_b, m_w_out, m_norm_ffn_g, m_w_ffn_gate, m_w_ffn_up, m_w_ffn_down, m_norm_final_g, v_norm_mix_g, v_w_in, v_lower_bounds, v_hg_norm_g, v_conv_w, v_w_branch_a, v_w_branch_b, v_w_out, v_norm_ffn_g, v_w_ffn_gate, v_w_ffn_up, v_w_ffn_down, v_norm_final_g):
    _, L, D = x.shape
    H = D // 2
    assert lower_bounds.shape == (2, H) and hg_norm_g.shape == (1, HEAD_DIM)
    assert conv_w.shape == (1, 3, LANES) and w_in.shape[2] * N_CHIPS == 11 * H
    x2d, target = x.reshape(L, D), loss_target.reshape(L, D)
    g_final = norm_final_g.reshape(1, D)
    chip = 2 * lax.axis_index("x") + lax.axis_index("y")
    core = lax.axis_index("c")

    tr = lambda w: jnp.transpose(w[0])
    big = [w_in[0], w_branch_a[0], w_branch_b[0], w_out[0], tr(w_ffn_gate), tr(w_ffn_up), w_ffn_down[0]]
    big_m = [m_w_in[0], m_w_branch_a[0], m_w_branch_b[0], m_w_out[0], tr(m_w_ffn_gate), tr(m_w_ffn_up),
             m_w_ffn_down[0]]
    big_v = [v_w_in[0], v_w_branch_a[0], v_w_branch_b[0], v_w_out[0], tr(v_w_ffn_gate), tr(v_w_ffn_up),
             v_w_ffn_down[0]]
    names = ["w_in", "w_branch_a", "w_branch_b", "w_out", "w_ffn_gate", "w_ffn_up", "w_ffn_down"]

    chip_idx = chip.reshape(1).astype(jnp.int32)
    placed = [(_cast_place_t if j < 3 else _cast_place)("place_" + nm, w, chip_idx)
              for j, (nm, w) in enumerate(zip(names, big))]
    conv_placed = lax.dynamic_update_slice(jnp.zeros((N_CHIPS, 3, LANES), F32), conv_w, (chip, 0, 0))
    sems, in_flight, token = _gather_start([([placed[0], conv_placed], {1}), (placed[1:4], set()),
                                            (placed[4:], set())])

    x_i, y_i = lax.axis_index("x"), lax.axis_index("y")
    blocks = lambda *ks: jnp.stack(ks).astype(jnp.int32)
    w_in_buf, conv_buf = in_flight[0]
    h, proj = _fwd_proj_first(x2d, norm_mix_g, w_in_buf, blocks(chip))
    w_in_buf, conv_buf = _gather_wait("gather_wait_in_near", [w_in_buf, conv_buf], {1}, sems[0], h,
                                      lambda w, j: w == 0 and j < 2)
    (w_in_buf,) = _gather_forward("gather_fwd_in_near", [w_in_buf], (0, 1))
    proj = _fwd_proj_more("fwd_proj_near", h, w_in_buf, proj,
                          blocks(2 * (1 - x_i) + y_i, 2 * x_i + (1 - y_i)))
    w_in_buf, conv_all = _gather_wait("gather_wait_in_far", [w_in_buf, conv_buf], {1}, sems[0], proj,
                                      lambda w, j: w == 1 or j == 2)
    (w_int3,) = _gather_forward("gather_fwd_in_far", [w_in_buf], (2,))
    proj = _fwd_proj_more("fwd_proj_far", h, w_int3, proj, blocks(2 * (1 - x_i) + (1 - y_i)))
    w_int = w_int3.reshape(-1, D)
    conv_full = jnp.transpose(conv_all, (1, 0, 2)).reshape(3, H)
    og, o_pre, s_saved = _hgrn_fwd(proj, lower_bounds, hg_norm_g, H)
    landed = _gather_wait("gather_wait_mix", in_flight[1], set(), sems[1], og)
    fwd_sems, landed, token = _split_start("gather_fwd_mix_start", landed, 9, _forward_pairs)
    cb = _conv_fwd(proj, conv_full, H, token)
    wat3, wbt3, wout3 = _split_wait("gather_fwd_mix_wait", fwd_sems, landed, _forward_pairs, cb)
    wat, wbt, wout = wat3.reshape(D, H), wbt3.reshape(D, H), wout3.reshape(D, D)
    landed = _gather_wait("gather_wait_ffn", in_flight[2], set(), sems[2], cb)
    fwd_sems, landed, token = _split_start("gather_fwd_ffn_start", landed, 9, _forward_pairs)
    sig_a, sig_b, dm_dga, dm_dgb, merged, x1, h2 = _fwd_mix(og, cb, proj, x2d, wat, wbt, wout, norm_ffn_g,
                                                              H, token)
    wgt3, wut3, wd3 = _split_wait("gather_fwd_ffn_wait", fwd_sems, landed, _forward_pairs, h2)
    d_ff = N_CHIPS * wd3.shape[1]
    wgt, wut, wd = wgt3.reshape(d_ff, D), wut3.reshape(d_ff, D), wd3.reshape(d_ff, D)
    ffn_ds_da, ffn_ds_db, ffn_s = _fwd_ffn_up(h2, wgt, wut)
    dx2, dx2b, red_final = _fwd_down_loss(ffn_s, wd, x1, target, g_final)

    c_idx = core.reshape(1).astype(jnp.int32)
    place_idx = jnp.stack([chip, core]).astype(jnp.int32)

    def sibling_start(tag, grads):
        bufs = [lax.empty((N_CHIPS, g.shape[1] // 2, g.shape[2]), F32) for g in grads]
        return _split_start("rs_sibling_start_" + tag, list(grads) + bufs, len(grads), _sibling_pairs)

    def ici_start(tag, js, grads, from_sibling):
        partials = [_rs_add("rs_add_" + names[j], g, s, c_idx) for j, g, s in zip(js, grads, from_sibling)]
        landings = [lax.empty((3,) + p.shape[1:], BF16) for p in partials]
        return _split_start("rs_ici_start_" + tag, partials + landings, 3 * len(js), _ici_pairs)

    def ici_start_behind(tag, js, started, after):
        n = len(js)
        arrays = _split_wait("rs_sibling_wait_" + tag, started[0], started[1], _sibling_pairs, after)
        return ici_start(tag, js, arrays[:n], arrays[n:])

    def rs_end(tag, js, started, after):
        n = len(js)
        arrays = _split_wait("rs_ici_wait_" + tag, started[0], started[1], _ici_pairs, after)
        halves = [_rs_sum("rs_sum_" + names[j], p, r, place_idx)
                  for j, p, r in zip(js, arrays[:n], arrays[n:])]
        grads = _rs_share("rs_share_" + tag, halves)
        return [_adamw("adamw_" + names[j], g, big[j], big_m[j], big_v[j]) for j, g in zip(js, grads)]

    shards3 = lambda g: g.reshape(N_CHIPS, d_ff // N_CHIPS, D)
    da, db = _bwd_down(dx2b, wd, ffn_ds_da, ffn_ds_db)
    g_wd = shards3(_dw_rows2("dw_ffn_down", ffn_s, dx2b))
    g_wg = shards3(_dw_rows2("dw_ffn_gate", da, h2))
    g_wu = shards3(_dw_rows2("dw_ffn_up", db, h2))
    ffn_sibling = sibling_start("ffn", [g_wg, g_wu, g_wd])
    dx1, dx1b, red_ffn = _bwd_ffn_dh(da, db, wgt, wut, x1, dx2, norm_ffn_g, ffn_sibling[2])
    ffn_ici = ici_start_behind("ffn", [4, 5, 6], ffn_sibling, dx1b)
    dya, dyb, dga, dgb, d_o, d_cb = _bwd_mix(dx1b, sig_a, sig_b, dm_dga, dm_dgb, wat, wbt, wout, H,
                                             ffn_ici[2])
    g_wout = _dw_rows("dw_out", merged, dx1b)
    g_wa = _dw_cols("dw_branch_a", og, dya, D // N_CHIPS)
    g_wb = _dw_cols("dw_branch_b", cb, dyb, D // N_CHIPS)
    mix_sibling = sibling_start("mix", [g_wa, g_wb, g_wout])
    dq, df, dv, dg, red_hg = _hgrn_bwd(proj, lower_bounds, hg_norm_g, o_pre, d_o, s_saved, H, mix_sibling[2])
    mix_ici = ici_start_behind("mix", [1, 2, 3], mix_sibling, dq)
    dcg, dbg, dxb, g_conv = _conv_bwd(proj, conv_full, d_cb, H, mix_ici[2])
    dproj = jnp.concatenate([dq, df, dv, dg, dcg, dbg, dxb, dga, dgb], axis=1)
    g_win = _dw_cols("dw_in", h, dproj, w_int3.shape[1])
    in_sibling = sibling_start("in", [g_win])
    big_out = [None] + rs_end("mix", [1, 2, 3], mix_ici, in_sibling[2]) + rs_end(
        "ffn", [4, 5, 6], ffn_ici, in_sibling[2])
    in_ici = ici_start_behind("in", [0], in_sibling, big_out[6][0])
    grad_x, red_mix = _bwd_in(dproj, w_int, x2d, dx1, norm_mix_g, in_ici[2])
    big_out[0] = rs_end("in", [0], in_ici, grad_x)[0]

    total = _small_allreduce(red_mix, red_ffn, red_final, red_hg, g_conv)

    def smalls(mix, lb, hg, cw, ffn, fin):
        return [mix, lb, hg, cw[0], ffn, fin.reshape(1, D)]

    small_out = _small_update(
        total, chip_idx,
        smalls(norm_mix_g, lower_bounds, hg_norm_g, conv_w, norm_ffn_g, norm_final_g),
        smalls(m_norm_mix_g, m_lower_bounds, m_hg_norm_g, m_conv_w, m_norm_ffn_g, m_norm_final_g),
        smalls(v_norm_mix_g, v_lower_bounds, v_hg_norm_g, v_conv_w, v_norm_ffn_g, v_norm_final_g))

    def outputs(i):
        big_i = [big_out[j][i] for j in range(7)]
        mix, lb, hg, cw, ffn, fin = [small_out[4 * p + i] for p in range(6)]
        return [mix, big_i[0][None], lb, hg, cw[None], big_i[1][None], big_i[2][None], big_i[3][None], ffn,
                big_i[4].T[None], big_i[5].T[None], big_i[6][None], fin.reshape(D)]

    outs = [small_out[24][0, 0], grad_x.reshape(1, L, D)]
    for i in range(4):
        outs += outputs(i)
    return tuple(outs)
```

```python
import functools

import jax
import jax.numpy as jnp
from jax import lax
from jax.experimental import pallas as pl
from jax.experimental.pallas import tpu as pltpu

F32 = jnp.float32
BF16 = jnp.bfloat16
EPS = 1e-6
CHUNK = 32
HEAD_DIM = 128
LANES = 128
N_CHIPS = 4
N_SMALL_ROWS = 16

ADAM_LR = 0.001
ADAM_B1 = 0.9
ADAM_B2 = 0.999
ADAM_EPS = 1e-08
ADAM_WD = 0.01
ADAM_STEP = 10

MESH = pl.DeviceIdType.MESH
ANY = pl.BlockSpec(memory_space=pl.ANY)
VMEM = pl.BlockSpec(memory_space=pltpu.VMEM)
HBM = pl.BlockSpec(memory_space=pltpu.HBM)
SEM = pl.BlockSpec(memory_space=pltpu.SEMAPHORE)
EFFECT = pltpu.SideEffectType.DATAFLOW_SIDE_EFFECTING


def _sds(shape, dtype):
    return jax.ShapeDtypeStruct(shape, dtype)


def _pallas_call(body, pin=True, **kwargs):
    if not pin:
        return pl.pallas_call(body, **kwargs)
    in_hbm = lambda s: pltpu.HBM(s.shape, s.dtype) if isinstance(s, jax.ShapeDtypeStruct) else s
    kwargs["out_shape"] = jax.tree.map(in_hbm, kwargs["out_shape"])
    call = pl.pallas_call(body, **kwargs)

    def run(*args):
        return call(*[pltpu.with_memory_space_constraint(a, pltpu.HBM) if a.dtype in (F32, BF16) else a
                      for a in args])

    return run


def _params(semantics, vmem_mb):
    return pltpu.CompilerParams(dimension_semantics=semantics, vmem_limit_bytes=vmem_mb << 20)


def _nn(a, b):
    return lax.dot_general(a, b, (((1,), (0,)), ((), ())), preferred_element_type=F32)


def _nt(a, b):
    return lax.dot_general(a, b, (((1,), (1,)), ((), ())), preferred_element_type=F32)


def _tn(a, b):
    return lax.dot_general(a, b, (((0,), (0,)), ((), ())), preferred_element_type=F32)


MXU_COLS = 256


def _col_blocks(n):
    return [slice(c, min(c + MXU_COLS, n)) for c in range(0, n, MXU_COLS)]


def _sigmoid(x):
    return jax.nn.sigmoid(x)


def _rms_stats(x):
    r = lax.rsqrt(jnp.mean(x * x, axis=-1, keepdims=True) + EPS)
    return r, x * r


def _rms_bwd(dxh, xh, r):
    return r * (dxh - xh * jnp.mean(dxh * xh, axis=-1, keepdims=True))


def _fwd_proj_first(x, g_mix, w_int3, block):
    L, D = x.shape
    tn = w_int3.shape[1]
    tm = min(L, 1024)

    def body(blk_ref, x_ref, g_ref, w_ref, h_ref, p_ref):
        _, xh = _rms_stats(x_ref[...])
        h = (xh * g_ref[...]).astype(BF16)
        h_ref[...] = h
        p_ref[...] = _nt(h, w_ref[...])

    return _pallas_call(
        body, name="fwd_proj_own",
        grid_spec=pltpu.PrefetchScalarGridSpec(
            num_scalar_prefetch=1, grid=(L // tm,),
            in_specs=[pl.BlockSpec((tm, D), lambda i, blk: (i, 0)),
                      pl.BlockSpec((1, D), lambda i, blk: (0, 0)),
                      pl.BlockSpec((None, tn, D), lambda i, blk: (blk[0], 0, 0))],
            out_specs=[pl.BlockSpec((tm, D), lambda i, blk: (i, 0)),
                       pl.BlockSpec((tm, tn), lambda i, blk: (i, blk[0]))]),
        out_shape=[_sds((L, D), BF16), _sds((L, N_CHIPS * tn), F32)],
        compiler_params=_params(("parallel",), 48),
    )(block, x, g_mix, w_int3)


def _fwd_proj_more(name, h, w_int3, proj, blocks):
    L, D = h.shape
    tn = w_int3.shape[1]
    tm = min(L, 1024)

    def body(blk_ref, h_ref, w_ref, proj_ref, p_ref):
        p_ref[...] = _nt(h_ref[...], w_ref[...])

    return _pallas_call(
        body, name=name,
        grid_spec=pltpu.PrefetchScalarGridSpec(
            num_scalar_prefetch=1, grid=(L // tm, blocks.shape[0]),
            in_specs=[pl.BlockSpec((tm, D), lambda i, j, blk: (i, 0)),
                      pl.BlockSpec((None, tn, D), lambda i, j, blk: (blk[j], 0, 0)), ANY],
            out_specs=pl.BlockSpec((tm, tn), lambda i, j, blk: (i, blk[j]))),
        out_shape=_sds(proj.shape, proj.dtype),
        input_output_aliases={3: 0},
        compiler_params=_params(("parallel", "arbitrary"), 48),
    )(blocks, h, w_int3, proj)


def _lower_bound(lbp):
    l0, l1 = lbp[0:1, :], lbp[1:2, :]
    m = jnp.maximum(l0, l1)
    e0, e1 = jnp.exp(l0 - m), jnp.exp(l1 - m)
    return e0 / (e0 + e1)


def _seg_scan(x, r32, forward):
    n = x.shape[0]
    s = 1
    while s < CHUNK:
        if forward:
            x = x + jnp.where(r32 >= s, pltpu.roll(x, s, 0), 0.0)
        else:
            x = x + jnp.where(r32 < CHUNK - s, pltpu.roll(x, n - s, 0), 0.0)
        s *= 2
    return x


def _bcast_row(x, row):
    n, w = x.shape
    nc = n // CHUNK
    x3 = x.reshape(nc, CHUNK, w)
    return jnp.broadcast_to(x3[:, row:row + 1, :], (nc, CHUNK, w)).reshape(n, w)


def _hgrn_prep(q_raw, f_raw, lb):
    r32 = lax.broadcasted_iota(jnp.int32, f_raw.shape, 0) & (CHUNK - 1)
    sig = _sigmoid(f_raw)
    f = lb + (1.0 - lb) * sig
    b = _seg_scan(jnp.log(f), r32, True)
    a = _bcast_row(b, CHUNK // 2 - 1)
    bl = _bcast_row(b, CHUNK - 1)
    sq = _sigmoid(q_raw)
    q = q_raw * sq * (HEAD_DIM ** -0.5)
    return dict(r32=r32, sig=sig, f=f, k=1.0 - f, b=b, a=a, bl=bl, sq=sq, q=q)


def _chunk_masks(n):
    ri = lax.broadcasted_iota(jnp.int32, (n, n), 0)
    ci = lax.broadcasted_iota(jnp.int32, (n, n), 1)
    same = (ri // CHUNK) == (ci // CHUNK)
    return same & (ci <= ri), same & (ri <= ci)


def _hgrn_fwd(proj, lower_bounds, gamma, H):
    L = proj.shape[0]
    nh = H // HEAD_DIM
    TL = min(L, 256)
    nc = TL // CHUNK

    def body(q_ref, f_ref, v_ref, g_ref, lbp_ref, gam_ref, og_ref, o_ref, s_ref, st_ref):
        @pl.when(pl.program_id(0) == 0)
        def _():
            st_ref[...] = jnp.zeros_like(st_ref)

        lb = _lower_bound(lbp_ref[...])
        gam = gam_ref[...]
        mask, _ = _chunk_masks(TL)
        rowc = lax.broadcasted_iota(jnp.int32, (TL, HEAD_DIM), 0) // CHUNK
        for h in range(nh):
            hs = slice(h * HEAD_DIM, (h + 1) * HEAD_DIM)
            p = _hgrn_prep(q_ref[:, hs], f_ref[:, hs], lb[:, hs])
            v = v_ref[:, hs]
            vb = v.astype(BF16)
            vt = v.T.astype(BF16)
            q_hat = (p["q"] * jnp.exp(p["b"] - p["a"])).astype(BF16)
            k_hat = (p["k"] * jnp.exp(p["a"] - p["b"])).astype(BF16)
            q_in = (p["q"] * jnp.exp(p["b"])).astype(BF16)
            k_out = (p["k"] * jnp.exp(p["bl"] - p["b"])).astype(BF16)
            dec = jnp.exp(p["bl"])
            att = jnp.where(mask, _nt(q_hat, k_hat), 0.0).astype(BF16)
            o_intra = _nn(att, vb)
            st = st_ref[h]
            for c in range(nc):
                rs = slice(c * CHUNK, (c + 1) * CHUNK)
                stb = st.astype(BF16)
                s_ref[c, h] = stb
                o_ref[rs, hs] = o_intra[rs] + _nt(q_in[rs], stb)
                k_c = jnp.where(rowc == c, k_out, jnp.zeros_like(k_out))
                st = st * dec[c * CHUNK:c * CHUNK + 1, :] + _nn(vt, k_c)
            st_ref[h] = st
            o = o_ref[:, hs]
            _, xh = _rms_stats(o)
            gr = g_ref[:, hs]
            og_ref[:, hs] = (xh * gam * (gr * _sigmoid(gr))).astype(BF16)

    col = lambda k: pl.BlockSpec((TL, H), lambda i, k=k: (i, k))
    return _pallas_call(
        body, name="hgrn_fwd", grid=(L // TL,),
        in_specs=[col(0), col(1), col(2), col(3),
                  pl.BlockSpec(lower_bounds.shape, lambda i: (0, 0)),
                  pl.BlockSpec(gamma.shape, lambda i: (0, 0))],
        out_specs=[pl.BlockSpec((TL, H), lambda i: (i, 0)),
                   pl.BlockSpec((TL, H), lambda i: (i, 0)),
                   pl.BlockSpec((nc, nh, HEAD_DIM, HEAD_DIM), lambda i: (i, 0, 0, 0))],
        out_shape=[_sds((L, H), BF16), _sds((L, H), F32),
                   _sds((L // CHUNK, nh, HEAD_DIM, HEAD_DIM), BF16)],
        scratch_shapes=[pltpu.VMEM((nh, HEAD_DIM, HEAD_DIM), F32)],
        compiler_params=_params(("arbitrary",), 48),
    )(proj, proj, proj, proj, lower_bounds, gamma)


def _hgrn_bwd(proj, lower_bounds, gamma, o_pre, d_out, s_saved, H, after):
    L = proj.shape[0]
    nh = H // HEAD_DIM
    TL = min(L, 256)
    nc = TL // CHUNK
    nt = L // TL

    def body(q_ref, f_ref, v_ref, g_ref, lbp_ref, gam_ref, o_ref, d_ref, s_ref, after_ref,
             dq_ref, df_ref, dv_ref, dg_ref, red_ref, dst_ref, dsall_ref, tmp_ref):
        @pl.when(pl.program_id(0) == 0)
        def _():
            dst_ref[...] = jnp.zeros_like(dst_ref)
            red_ref[...] = jnp.zeros_like(red_ref)

        lb = _lower_bound(lbp_ref[...])
        gam = gam_ref[...]
        mask, mask_t = _chunk_masks(TL)
        rowc = lax.broadcasted_iota(jnp.int32, (TL, HEAD_DIM), 0) // CHUNK
        for h in range(nh):
            hs = slice(h * HEAD_DIM, (h + 1) * HEAD_DIM)
            qr, gr, lbh = q_ref[:, hs], g_ref[:, hs], lb[:, hs]
            p = _hgrn_prep(qr, f_ref[:, hs], lbh)
            vb = v_ref[:, hs].astype(BF16)
            eba, eab = jnp.exp(p["b"] - p["a"]), jnp.exp(p["a"] - p["b"])
            eb, elb = jnp.exp(p["b"]), jnp.exp(p["bl"] - p["b"])
            dec = jnp.exp(p["bl"])
            q_hat, k_hat = p["q"] * eba, p["k"] * eab
            q_in, k_out = p["q"] * eb, p["k"] * elb
            q_hat_b, k_hat_b = q_hat.astype(BF16), k_hat.astype(BF16)
            q_in_b, k_out_b = q_in.astype(BF16), k_out.astype(BF16)

            o, dout = o_ref[:, hs], d_ref[:, hs]
            sg = _sigmoid(gr)
            r, xh = _rms_stats(o)
            dg_ref[:, hs] = (dout * (xh * gam) * (sg * (1.0 + gr * (1.0 - sg)))).astype(BF16)
            dn = dout * (gr * sg)
            red_ref[1:2, hs] += jnp.sum(dn * xh, axis=0, keepdims=True)
            do = _rms_bwd(dn * gam, xh, r)
            dob = do.astype(BF16)
            dot_b = do.T.astype(BF16)

            att_t = jnp.where(mask_t, _nt(k_hat_b, q_hat_b), 0.0).astype(BF16)
            dv_intra = _nn(att_t, dob)
            datt = jnp.where(mask, _nt(dob, vb), 0.0).astype(BF16)
            dqh = _nn(datt, k_hat_b)
            datt_t = jnp.where(mask_t, _nt(vb, dob), 0.0).astype(BF16)
            dkh = _nn(datt_t, q_hat_b)

            dst = dst_ref[h]
            for c in reversed(range(nc)):
                dsall_ref[c] = dst
                q_c = jnp.where(rowc == c, q_in_b, jnp.zeros_like(q_in_b))
                dst = dst * dec[c * CHUNK:c * CHUNK + 1, :] + _nn(dot_b, q_c)
            dst_ref[h] = dst
            for c in range(nc):
                rs = slice(c * CHUNK, (c + 1) * CHUNK)
                ds_c = dsall_ref[c]
                dsb = ds_c.astype(BF16)
                st_prev = s_ref[c, h]
                tmp_ref[0, rs, :] = _nt(k_out_b[rs], dsb)
                tmp_ref[1, rs, :] = _nn(vb[rs], dsb)
                tmp_ref[2, rs, :] = _nn(dob[rs], st_prev)
                ddec = jnp.sum(ds_c * st_prev.astype(F32), axis=0, keepdims=True)
                tmp_ref[3, rs, :] = jnp.broadcast_to(ddec * dec[c * CHUNK:c * CHUNK + 1, :],
                                                     (CHUNK, HEAD_DIM))
            dko, dqi = tmp_ref[1], tmp_ref[2]
            dq = dqh * eba + dqi * eb
            dk = dkh * eab + dko * elb
            tko = dko * k_out
            db = dqh * q_hat - dkh * k_hat + dqi * q_in - tko
            dlog = (_seg_scan(db, p["r32"], False)
                    + _bcast_row(_seg_scan(tko, p["r32"], True), CHUNK - 1) + tmp_ref[3])
            df = dlog / p["f"] - dk
            sig = p["sig"]
            red_ref[0:1, hs] += jnp.sum(df * (1.0 - sig), axis=0, keepdims=True)
            df_ref[:, hs] = (df * (1.0 - lbh) * sig * (1.0 - sig)).astype(BF16)
            sq = p["sq"]
            dq_ref[:, hs] = (dq * (HEAD_DIM ** -0.5) * (sq * (1.0 + qr * (1.0 - sq)))).astype(BF16)
            dv_ref[:, hs] = (dv_intra + tmp_ref[0]).astype(BF16)

    col = lambda k: pl.BlockSpec((TL, H), lambda i, k=k: (nt - 1 - i, k))
    rev = pl.BlockSpec((TL, H), lambda i: (nt - 1 - i, 0))
    return _pallas_call(
        body, name="hgrn_bwd", grid=(nt,),
        in_specs=[col(0), col(1), col(2), col(3),
                  pl.BlockSpec(lower_bounds.shape, lambda i: (0, 0)),
                  pl.BlockSpec(gamma.shape, lambda i: (0, 0)),
                  rev, rev,
                  pl.BlockSpec((nc, nh, HEAD_DIM, HEAD_DIM), lambda i: (nt - 1 - i, 0, 0, 0)), ANY],
        out_specs=[rev, rev, rev, rev, pl.BlockSpec((8, H), lambda i: (0, 0))],
        out_shape=[_sds((L, H), BF16)] * 4 + [_sds((8, H), F32)],
        scratch_shapes=[pltpu.VMEM((nh, HEAD_DIM, HEAD_DIM), F32),
                        pltpu.VMEM((nc, HEAD_DIM, HEAD_DIM), F32),
                        pltpu.VMEM((4, TL, HEAD_DIM), F32)],
        compiler_params=_params(("arbitrary",), 48),
    )(proj, proj, proj, proj, lower_bounds, gamma, o_pre, d_out, s_saved, after)


def _shift_down(u, s, row):
    return jnp.where(row >= s, pltpu.roll(u, s, 0), 0.0)


def _shift_up(u, s, row):
    n = u.shape[0]
    return jnp.where(row < n - s, pltpu.roll(u, n - s, 0), 0.0)


def _conv_specs(L, H):
    per = H // LANES
    return [pl.BlockSpec((L, LANES), lambda j, o=o: (0, o * per + j)) for o in (4, 5, 6)]


def _conv_fwd(proj, conv_w, H, after):
    L = proj.shape[0]

    def body(c_ref, b_ref, x_ref, w_ref, after_ref, o_ref):
        row = lax.broadcasted_iota(jnp.int32, (L, LANES), 0)
        u = c_ref[...] * x_ref[...]
        w = w_ref[...]
        y = w[0:1] * _shift_down(u, 2, row) + w[1:2] * _shift_down(u, 1, row) + w[2:3] * u
        o_ref[...] = (b_ref[...] * y).astype(BF16)

    return _pallas_call(
        body, name="conv_fwd", grid=(H // LANES,),
        in_specs=_conv_specs(L, H) + [pl.BlockSpec((3, LANES), lambda j: (0, j)), ANY],
        out_specs=pl.BlockSpec((L, LANES), lambda j: (0, j)),
        out_shape=_sds((L, H), BF16),
        compiler_params=_params(("parallel",), 48),
    )(proj, proj, proj, conv_w, after)


def _conv_bwd(proj, conv_w, dcb, H, after):
    L = proj.shape[0]

    def body(c_ref, b_ref, x_ref, w_ref, d_ref, after_ref, dc_ref, db_ref, dx_ref, dw_ref):
        row = lax.broadcasted_iota(jnp.int32, (L, LANES), 0)
        cg, xb = c_ref[...], x_ref[...]
        u = cg * xb
        u1, u2 = _shift_down(u, 1, row), _shift_down(u, 2, row)
        w = w_ref[...]
        y = w[0:1] * u2 + w[1:2] * u1 + w[2:3] * u
        d = d_ref[...]
        db_ref[...] = (d * y).astype(BF16)
        dy = d * b_ref[...]
        du = w[2:3] * dy + w[1:2] * _shift_up(dy, 1, row) + w[0:1] * _shift_up(dy, 2, row)
        dw_ref[0:1, :] = jnp.sum(dy * u2, axis=0, keepdims=True)
        dw_ref[1:2, :] = jnp.sum(dy * u1, axis=0, keepdims=True)
        dw_ref[2:3, :] = jnp.sum(dy * u, axis=0, keepdims=True)
        dc_ref[...] = (du * xb).astype(BF16)
        dx_ref[...] = (du * cg).astype(BF16)

    blk = pl.BlockSpec((L, LANES), lambda j: (0, j))
    return _pallas_call(
        body, name="conv_bwd", grid=(H // LANES,),
        in_specs=_conv_specs(L, H) + [pl.BlockSpec((3, LANES), lambda j: (0, j)), blk, ANY],
        out_specs=[blk, blk, blk, pl.BlockSpec((3, LANES), lambda j: (0, j))],
        out_shape=[_sds((L, H), BF16)] * 3 + [_sds((3, H), F32)],
        compiler_params=_params(("parallel",), 56),
    )(proj, proj, proj, conv_w, dcb, after)


def _gate_specs(tm, H):
    return [pl.BlockSpec((tm, H), lambda i, k=k: (i, k)) for k in (7, 8, 9, 10)]


def _fwd_mix(og, cb, proj, x, wat, wbt, wout, g_ffn, H, after):
    L, D = x.shape
    tm = min(L, 512)

    def body(o_ref, cb_ref, ga0, ga1, gb0, gb1, x_ref, wa_ref, wb_ref, wo_ref, g_ref, after_ref,
             sa_ref, sb_ref, ta_ref, tb_ref, m_ref, x1_ref, h2_ref):
        ya, yb = _nt(o_ref[...], wa_ref[...]), _nt(cb_ref[...], wb_ref[...])
        for k, (gar, gbr) in enumerate(((ga0, gb0), (ga1, gb1))):
            cs = slice(k * H, (k + 1) * H)
            sa, sb = _sigmoid(gar[...]), _sigmoid(gbr[...])
            ma, mb = sa * ya[:, cs], sb * yb[:, cs]
            m_ref[:, cs] = (ma + mb).astype(BF16)
            sa_ref[:, cs] = sa.astype(BF16)
            sb_ref[:, cs] = sb.astype(BF16)
            ta_ref[:, cs] = (ma * (1.0 - sa)).astype(BF16)
            tb_ref[:, cs] = (mb * (1.0 - sb)).astype(BF16)
        x1 = x_ref[...] + _nn(m_ref[...], wo_ref[...])
        x1_ref[...] = x1
        _, xh = _rms_stats(x1)
        h2_ref[...] = (xh * g_ref[...]).astype(BF16)

    row = lambda w: pl.BlockSpec((tm, w), lambda i: (i, 0))
    full = lambda a: pl.BlockSpec(a.shape, lambda i: (0,) * a.ndim)
    return _pallas_call(
        body, name="fwd_mix", grid=(L // tm,),
        in_specs=[row(H), row(H)] + _gate_specs(tm, H) + [row(D), full(wat), full(wbt), full(wout),
                                                           full(g_ffn), ANY],
        out_specs=[row(D)] * 7,
        out_shape=[_sds((L, D), BF16)] * 5 + [_sds((L, D), F32), _sds((L, D), BF16)],
        compiler_params=_params(("parallel",), 56),
    )(og, cb, proj, proj, proj, proj, x, wat, wbt, wout, g_ffn, after)


def _bwd_mix(dx1b, sig_a, sig_b, dm_dga, dm_dgb, wat, wbt, wout, H, after):
    L, D = dx1b.shape
    tm = min(L, 512)

    def body(dx_ref, sa_ref, sb_ref, ta_ref, tb_ref, wa_ref, wb_ref, wo_ref, after_ref,
             dya_ref, dyb_ref, dga_ref, dgb_ref, do_ref, dcb_ref):
        dm = _nt(dx_ref[...], wo_ref[...])
        dga_ref[...] = (dm * ta_ref[...].astype(F32)).astype(BF16)
        dgb_ref[...] = (dm * tb_ref[...].astype(F32)).astype(BF16)
        dya_ref[...] = (dm * sa_ref[...].astype(F32)).astype(BF16)
        dyb_ref[...] = (dm * sb_ref[...].astype(F32)).astype(BF16)
        do_ref[...] = _nn(dya_ref[...], wa_ref[...])
        dcb_ref[...] = _nn(dyb_ref[...], wb_ref[...])

    row = lambda w: pl.BlockSpec((tm, w), lambda i: (i, 0))
    full = lambda a: pl.BlockSpec(a.shape, lambda i: (0,) * a.ndim)
    return _pallas_call(
        body, name="bwd_mix", grid=(L // tm,),
        in_specs=[row(D)] * 5 + [full(wat), full(wbt), full(wout), ANY],
        out_specs=[row(D)] * 4 + [row(H)] * 2,
        out_shape=[_sds((L, D), BF16)] * 4 + [_sds((L, H), F32)] * 2,
        compiler_params=_params(("parallel",), 56),
    )(dx1b, sig_a, sig_b, dm_dga, dm_dgb, wat, wbt, wout, after)


def _fwd_ffn_up(h2, wgt, wut):
    L, D = h2.shape
    F = wgt.shape[0]
    tn = F // 2
    tm = min(L, 512)

    def body(h_ref, wg_ref, wu_ref, sa_ref, sb_ref, s_ref):
        h = h_ref[...]
        for cs in _col_blocks(tn):
            a, b = _nt(h, wg_ref[cs, :]), _nt(h, wu_ref[cs, :])
            sg = _sigmoid(a)
            silu = a * sg
            sa_ref[:, cs] = (b * sg * (1.0 + a * (1.0 - sg))).astype(BF16)
            sb_ref[:, cs] = silu.astype(BF16)
            s_ref[:, cs] = (silu * b).astype(BF16)

    wspec = pl.BlockSpec((tn, D), lambda j, i: (j, 0))
    ospec = pl.BlockSpec((tm, tn), lambda j, i: (i, j))
    return _pallas_call(
        body, name="fwd_ffn_up", grid=(2, L // tm),
        in_specs=[pl.BlockSpec((tm, D), lambda j, i: (i, 0)), wspec, wspec],
        out_specs=[ospec] * 3,
        out_shape=[_sds((L, F), BF16)] * 3,
        compiler_params=_params(("parallel", "parallel"), 48),
    )(h2, wgt, wut)


def _fwd_down_loss(s, wd, x1, target, g_final):
    L, D = x1.shape
    F = wd.shape[0]
    tm = min(L, 256)

    def body(s_ref, wd_ref, x1_ref, t_ref, g_ref, dx_ref, dxb_ref, red_ref):
        @pl.when(pl.program_id(0) == 0)
        def _():
            red_ref[...] = jnp.zeros_like(red_ref)

        g = g_ref[...]
        r, xh = _rms_stats(x1_ref[...] + _nn(s_ref[...], wd_ref[...]))
        e = xh * g - t_ref[...]
        dy = e * (1.0 / D)
        dx = _rms_bwd(dy * g, xh, r)
        dx_ref[...] = dx
        dxb_ref[...] = dx.astype(BF16)
        red_ref[0:1, :] += jnp.sum(dy * xh, axis=0, keepdims=True)
        red_ref[1:2, :] += jnp.broadcast_to(0.5 * jnp.sum(e * e) * (1.0 / D), (1, D))

    row = pl.BlockSpec((tm, D), lambda i: (i, 0))
    return _pallas_call(
        body, name="fwd_down_loss", grid=(L // tm,),
        in_specs=[pl.BlockSpec((tm, F), lambda i: (i, 0)), pl.BlockSpec((F, D), lambda i: (0, 0)),
                  row, row, pl.BlockSpec((1, D), lambda i: (0, 0))],
        out_specs=[row, row, pl.BlockSpec((8, D), lambda i: (0, 0))],
        out_shape=[_sds((L, D), F32), _sds((L, D), BF16), _sds((8, D), F32)],
        compiler_params=_params(("arbitrary",), 48),
    )(s, wd, x1, target, g_final)


def _bwd_down(dx2b, wd, s_a, s_b):
    L, D = dx2b.shape
    F = wd.shape[0]
    tn = F // 2
    tm = min(L, 512)

    def body(dx_ref, wd_ref, sa_ref, sb_ref, da_ref, db_ref):
        dx = dx_ref[...]
        for cs in _col_blocks(tn):
            ds = _nt(dx, wd_ref[cs, :])
            da_ref[:, cs] = (ds * sa_ref[:, cs].astype(F32)).astype(BF16)
            db_ref[:, cs] = (ds * sb_ref[:, cs].astype(F32)).astype(BF16)

    ospec = pl.BlockSpec((tm, tn), lambda j, i: (i, j))
    return _pallas_call(
        body, name="bwd_down", grid=(2, L // tm),
        in_specs=[pl.BlockSpec((tm, D), lambda j, i: (i, 0)),
                  pl.BlockSpec((tn, D), lambda j, i: (j, 0)), ospec, ospec],
        out_specs=[ospec] * 2,
        out_shape=[_sds((L, F), BF16)] * 2,
        compiler_params=_params(("parallel", "parallel"), 48),
    )(dx2b, wd, s_a, s_b)


def _bwd_ffn_dh(da, db, wgt, wut, x1, dx2, g_ffn, after):
    L, D = x1.shape
    F = wgt.shape[0]
    tm = min(L, 256)

    def body(da_ref, db_ref, wg_ref, wu_ref, x1_ref, dx2_ref, g_ref, after_ref, dx_ref, dxb_ref, red_ref):
        @pl.when(pl.program_id(0) == 0)
        def _():
            red_ref[...] = jnp.zeros_like(red_ref)

        dh = _nn(da_ref[...], wg_ref[...]) + _nn(db_ref[...], wu_ref[...])
        r, xh = _rms_stats(x1_ref[...])
        red_ref[0:1, :] += jnp.sum(dh * xh, axis=0, keepdims=True)
        dx = dx2_ref[...] + _rms_bwd(dh * g_ref[...], xh, r)
        dx_ref[...] = dx
        dxb_ref[...] = dx.astype(BF16)

    row = pl.BlockSpec((tm, D), lambda i: (i, 0))
    aspec = pl.BlockSpec((tm, F), lambda i: (i, 0))
    wspec = pl.BlockSpec((F, D), lambda i: (0, 0))
    return _pallas_call(
        body, name="bwd_ffn_dh", grid=(L // tm,),
        in_specs=[aspec, aspec, wspec, wspec, row, row, pl.BlockSpec((1, D), lambda i: (0, 0)), ANY],
        out_specs=[row, row, pl.BlockSpec((8, D), lambda i: (0, 0))],
        out_shape=[_sds((L, D), F32), _sds((L, D), BF16), _sds((8, D), F32)],
        compiler_params=_params(("arbitrary",), 56),
    )(da, db, wgt, wut, x1, dx2, g_ffn, after)


def _bwd_in(dproj, w_int, x, dx1, g_mix, after):
    L, D = x.shape
    N = w_int.shape[0]
    tm = min(L, 256)

    def body(dp_ref, w_ref, x_ref, dx1_ref, g_ref, after_ref, dx_ref, red_ref):
        @pl.when(pl.program_id(0) == 0)
        def _():
            red_ref[...] = jnp.zeros_like(red_ref)

        dh = _nn(dp_ref[...], w_ref[...])
        r, xh = _rms_stats(x_ref[...])
        red_ref[0:1, :] += jnp.sum(dh * xh, axis=0, keepdims=True)
        dx_ref[...] = dx1_ref[...] + _rms_bwd(dh * g_ref[...], xh, r)

    row = pl.BlockSpec((tm, D), lambda i: (i, 0))
    return _pallas_call(
        body, name="bwd_in", grid=(L // tm,),
        in_specs=[pl.BlockSpec((tm, N), lambda i: (i, 0)), pl.BlockSpec((N, D), lambda i: (0, 0)),
                  row, row, pl.BlockSpec((1, D), lambda i: (0, 0)), ANY],
        out_specs=[row, pl.BlockSpec((8, D), lambda i: (0, 0))],
        out_shape=[_sds((L, D), F32), _sds((8, D), F32)],
        compiler_params=_params(("arbitrary",), 56),
    )(dproj, w_int, x, dx1, g_mix, after)


def _mm_tn(name, a, b, a_spec, b_spec, o_block, n_out, n_k):
    def body(a_ref, b_ref, o_ref):
        part = _tn(a_ref[...], b_ref[...])

        @pl.when(pl.program_id(1) == 0)
        def _():
            o_ref[...] = part

        @pl.when(pl.program_id(1) > 0)
        def _():
            o_ref[...] += part

    return _pallas_call(
        body, name=name, grid=(n_out, n_k),
        in_specs=[a_spec, b_spec],
        out_specs=pl.BlockSpec((None,) + o_block, lambda j, k: (j, 0, 0)),
        out_shape=_sds((n_out,) + o_block, F32),
        compiler_params=_params(("parallel", "arbitrary"), 56),
    )(a, b)


TK_TOKENS = 2048


def _dw_cols(name, a, b, n_cols):
    L, M = a.shape
    tk = min(L, TK_TOKENS)
    return _mm_tn(name, a, b, pl.BlockSpec((tk, M), lambda j, k: (k, 0)),
                  pl.BlockSpec((tk, n_cols), lambda j, k: (k, j)), (M, n_cols), N_CHIPS, L // tk)


def _dw_rows(name, a, b):
    L, M = a.shape
    N = b.shape[1]
    tk = min(L, TK_TOKENS)
    return _mm_tn(name, a, b, pl.BlockSpec((tk, M // N_CHIPS), lambda j, k: (k, j)),
                  pl.BlockSpec((tk, N), lambda j, k: (k, 0)), (M // N_CHIPS, N), N_CHIPS, L // tk)


def _dw_rows2(name, a, b):
    L, M = a.shape
    N = b.shape[1]
    tk = min(L, TK_TOKENS)
    return _mm_tn(name, a, b, pl.BlockSpec((tk, M // 2), lambda j, k: (k, j)),
                  pl.BlockSpec((tk, N), lambda j, k: (k, 0)), (M // 2, N), 2, L // tk)


def _place():
    x, y, c = lax.axis_index("x"), lax.axis_index("y"), lax.axis_index("c")
    chips = [(1 - x, y), (x, 1 - y), (1 - x, 1 - y)]
    return x, y, c, 2 * x + y, chips


def _remote(src, dst, send_sem, recv_sem, device):
    return pltpu.make_async_remote_copy(src_ref=src, dst_ref=dst, send_sem=send_sem,
                                        recv_sem=recv_sem, device_id=device, device_id_type=MESH)


def _half(ref, lead, c, r2):
    return ref.at[lead, pl.ds(pl.multiple_of(c * r2, 16), r2), :]


def _cast_place(name, w, chip_idx):
    r, cols = w.shape
    tr = r // 2

    def body(k_ref, w_ref, o_ref):
        o_ref[...] = w_ref[...].astype(BF16)

    return _pallas_call(
        body, name=name,
        grid_spec=pltpu.PrefetchScalarGridSpec(
            num_scalar_prefetch=1, grid=(2,),
            in_specs=[pl.BlockSpec((tr, cols), lambda i, k_ref: (i, 0))],
            out_specs=pl.BlockSpec((None, tr, cols), lambda i, k_ref: (k_ref[0], i, 0))),
        out_shape=_sds((N_CHIPS, r, cols), BF16),
        compiler_params=_params(("parallel",), 48),
    )(chip_idx, w)


def _cast_place_t(name, w, chip_idx):
    r, cols = w.shape

    def body(k_ref, w_ref, o_ref):
        o_ref[...] = w_ref[...].T.astype(BF16)

    return _pallas_call(
        body, name=name,
        grid_spec=pltpu.PrefetchScalarGridSpec(
            num_scalar_prefetch=1, grid=(cols // LANES,),
            in_specs=[pl.BlockSpec((r, LANES), lambda i, k_ref: (0, i))],
            out_specs=pl.BlockSpec((None, LANES, r), lambda i, k_ref: (k_ref[0], i, 0))),
        out_shape=_sds((N_CHIPS, cols, r), BF16),
        compiler_params=_params(("parallel",), 48),
    )(chip_idx, w)


def _gather_copies(bufs, whole, send_sems, recv_sems, select=None):
    x, y, c, k, chips = _place()
    pairs = []
    for w, buf in enumerate(bufs):
        for j, (cx, cy) in enumerate(chips):
            if select is not None and not select(w, j):
                continue
            if w in whole:
                mine, theirs = buf.at[k], buf.at[2 * cx + cy]
            else:
                r2 = buf.shape[1] // 2
                mine, theirs = _half(buf, k, c, r2), _half(buf, 2 * cx + cy, c, r2)
            sems = (send_sems.at[w * 3 + j], recv_sems.at[w * 3 + j])
            pairs.append((_remote(mine, mine, *sems, (cx, cy, c)), _remote(theirs, theirs, *sems, (x, y, c))))
    return pairs


def _gather_start(groups):
    flat = [b for bufs, _ in groups for b in bufs]
    nb, ng = len(flat), len(groups)

    def body(*refs):
        ins, sems, token = refs[:nb], refs[nb:nb + 2 * ng], refs[-1]
        pos = 0
        for g, (bufs, whole) in enumerate(groups):
            for send, _ in _gather_copies(ins[pos:pos + len(bufs)], whole, sems[2 * g], sems[2 * g + 1]):
                send.start()
            pos += len(bufs)
        token[...] = jnp.zeros_like(token)

    sem_shapes = []
    for bufs, _ in groups:
        sem_shapes += [pltpu.SemaphoreType.DMA((3 * len(bufs),))] * 2
    out = _pallas_call(
        body, name="gather_start",
        in_specs=[HBM] * nb, out_specs=tuple([SEM] * (2 * ng) + [HBM] * nb + [VMEM]),
        out_shape=tuple(sem_shapes + [pltpu.HBM(b.shape, b.dtype) for b in flat] + [_sds((8, LANES), F32)]),
        input_output_aliases={i: 2 * ng + i for i in range(nb)},
        compiler_params=pltpu.CompilerParams(has_side_effects=EFFECT),
    )(*flat)
    sems, thru, pos = [], [], 2 * ng
    for g, (bufs, _) in enumerate(groups):
        sems.append((out[2 * g], out[2 * g + 1]))
        thru.append(list(out[pos:pos + len(bufs)]))
        pos += len(bufs)
    return sems, thru, out[-1]


def _gather_wait(name, bufs, whole, sems, after, select=None):
    nb = len(bufs)

    def body(*refs):
        ins, send_sems, recv_sems = refs[:nb], refs[nb], refs[nb + 1]
        for send, arrival in _gather_copies(ins, whole, send_sems, recv_sems, select):
            send.wait_send()
            arrival.wait_recv()

    return _pallas_call(
        body, name=name,
        in_specs=[HBM] * nb + [SEM, SEM, ANY], out_specs=[HBM] * nb,
        out_shape=[pltpu.HBM(b.shape, b.dtype) for b in bufs],
        input_output_aliases={i: i for i in range(nb)},
        compiler_params=pltpu.CompilerParams(has_side_effects=EFFECT),
    )(*bufs, sems[0], sems[1], after)


def _gather_forward(name, bufs, sources=(0, 1, 2)):
    n = len(bufs)

    def body(*refs):
        outs = refs[n:2 * n]
        send_sems, recv_sems = refs[2 * n:]
        x, y, c, _, chips = _place()
        sends = []
        for w in range(n):
            r2 = outs[w].shape[1] // 2
            for j in sources:
                landed = _half(outs[w], 2 * chips[j][0] + chips[j][1], c, r2)
                sends.append(_remote(landed, landed, send_sems.at[w * 3 + j], recv_sems.at[w * 3 + j],
                                     (x, y, 1 - c)))
        for cp in sends:
            cp.start()
        for w in range(n):
            r2 = outs[w].shape[1] // 2
            for j in sources:
                got = _half(outs[w], 2 * chips[j][0] + chips[j][1], 1 - c, r2)
                _remote(got, got, send_sems.at[w * 3 + j], recv_sems.at[w * 3 + j], (x, y, c)).wait_recv()
        for cp in sends:
            cp.wait_send()

    return _pallas_call(
        body, name=name,
        in_specs=[ANY] * n, out_specs=[ANY] * n,
        out_shape=[_sds(b.shape, b.dtype) for b in bufs],
        input_output_aliases={i: i for i in range(n)},
        scratch_shapes=[pltpu.SemaphoreType.DMA((n * 3,)), pltpu.SemaphoreType.DMA((n * 3,))],
    )(*bufs)


def _rs_sibling(name, grads):
    n = len(grads)

    def body(*refs):
        ins, outs = refs[:n], refs[n:2 * n]
        send_sems, recv_sems = refs[2 * n:]
        x, y, c, _, _ = _place()
        copies = []
        for w in range(n):
            r2 = ins[w].shape[1] // 2
            copies.append(_remote(_half(ins[w], slice(None), 1 - c, r2), outs[w],
                                  send_sems.at[w], recv_sems.at[w], (x, y, 1 - c)))
        for cp in copies:
            cp.start()
        for cp in copies:
            cp.wait()

    return _pallas_call(
        body, name=name,
        in_specs=[ANY] * n, out_specs=[ANY] * n,
        out_shape=[_sds((N_CHIPS, g.shape[1] // 2, g.shape[2]), F32) for g in grads],
        scratch_shapes=[pltpu.SemaphoreType.DMA((n,)), pltpu.SemaphoreType.DMA((n,))],
    )(*grads)


def _rs_add(name, grad3, from_sibling, c_idx):
    _, r2, cols = from_sibling.shape

    def body(c_ref, g_ref, s_ref, o_ref):
        o_ref[...] = (g_ref[...] + s_ref[...]).astype(BF16)

    return _pallas_call(
        body, name=name,
        grid_spec=pltpu.PrefetchScalarGridSpec(
            num_scalar_prefetch=1, grid=(N_CHIPS,),
            in_specs=[pl.BlockSpec((None, r2, cols), lambda k, c_ref: (k, c_ref[0], 0)),
                      pl.BlockSpec((None, r2, cols), lambda k, c_ref: (k, 0, 0))],
            out_specs=pl.BlockSpec((None, r2, cols), lambda k, c_ref: (k, 0, 0))),
        out_shape=_sds(from_sibling.shape, BF16),
        compiler_params=_params(("parallel",), 48),
    )(c_idx, grad3, from_sibling)


def _split_start(name, arrays, n_sems, pairs_fn):
    n = len(arrays)

    def body(*refs):
        for send, _ in pairs_fn(refs[:n], refs[n], refs[n + 1]):
            send.start()
        refs[-1][...] = jnp.zeros_like(refs[-1])

    out = _pallas_call(
        body, name=name,
        in_specs=[HBM] * n, out_specs=tuple([SEM, SEM] + [HBM] * n + [VMEM]),
        out_shape=tuple([pltpu.SemaphoreType.DMA((n_sems,))] * 2 + [pltpu.HBM(a.shape, a.dtype) for a in arrays]
                        + [_sds((8, LANES), F32)]),
        input_output_aliases={i: 2 + i for i in range(n)},
        compiler_params=pltpu.CompilerParams(has_side_effects=EFFECT),
    )(*arrays)
    return (out[0], out[1]), list(out[2:2 + n]), out[-1]


def _split_wait(name, sems, arrays, pairs_fn, after):
    n = len(arrays)

    def body(*refs):
        for send, arrival in pairs_fn(refs[:n], refs[n], refs[n + 1]):
            send.wait_send()
            arrival.wait_recv()

    return list(_pallas_call(
        body, name=name,
        in_specs=[HBM] * n + [SEM, SEM, ANY], out_specs=[HBM] * n,
        out_shape=[pltpu.HBM(a.shape, a.dtype) for a in arrays],
        input_output_aliases={i: i for i in range(n)},
        compiler_params=pltpu.CompilerParams(has_side_effects=EFFECT),
    )(*arrays, sems[0], sems[1], after))


def _forward_pairs(bufs, send_sems, recv_sems):
    x, y, c, _, chips = _place()
    pairs = []
    for w, buf in enumerate(bufs):
        r2 = buf.shape[1] // 2
        for j, (cx, cy) in enumerate(chips):
            landed, theirs = _half(buf, 2 * cx + cy, c, r2), _half(buf, 2 * cx + cy, 1 - c, r2)
            sems = (send_sems.at[w * 3 + j], recv_sems.at[w * 3 + j])
            pairs.append((_remote(landed, landed, *sems, (x, y, 1 - c)), _remote(theirs, theirs, *sems, (x, y, c))))
    return pairs


def _sibling_pairs(arrays, send_sems, recv_sems):
    x, y, c, _, _ = _place()
    n = len(arrays) // 2
    pairs = []
    for w in range(n):
        r2 = arrays[w].shape[1] // 2
        cp = _remote(_half(arrays[w], slice(None), 1 - c, r2), arrays[n + w], send_sems.at[w], recv_sems.at[w],
                     (x, y, 1 - c))
        pairs.append((cp, cp))
    return pairs


def _ici_pairs(arrays, send_sems, recv_sems):
    x, y, c, _, chips = _place()
    n = len(arrays) // 2
    pairs = []
    for w in range(n):
        for j, (cx, cy) in enumerate(chips):
            cp = _remote(arrays[w].at[2 * cx + cy], arrays[n + w].at[j],
                         send_sems.at[w * 3 + j], recv_sems.at[w * 3 + j], (cx, cy, c))
            pairs.append((cp, cp))
    return pairs


def _rs_sum(name, partials, received, place_idx):
    _, r2, cols = partials.shape
    nb = 2
    tr = r2 // nb

    def body(idx_ref, p_ref, r_ref, o_ref):
        o_ref[...] = ((p_ref[...].astype(F32) + r_ref[0].astype(F32))
                      + (r_ref[1].astype(F32) + r_ref[2].astype(F32)))

    return _pallas_call(
        body, name=name,
        grid_spec=pltpu.PrefetchScalarGridSpec(
            num_scalar_prefetch=1, grid=(nb,),
            in_specs=[pl.BlockSpec((None, tr, cols), lambda i, idx: (idx[0], i, 0)),
                      pl.BlockSpec((3, tr, cols), lambda i, idx: (0, i, 0))],
            out_specs=pl.BlockSpec((tr, cols), lambda i, idx: (idx[1] * nb + i, 0))),
        out_shape=_sds((2 * r2, cols), F32),
        compiler_params=_params(("parallel",), 48),
    )(place_idx, partials, received)


def _rs_share(name, shards):
    n = len(shards)

    def body(*refs):
        outs = refs[n:2 * n]
        send_sems, recv_sems = refs[2 * n:]
        x, y, c, _, _ = _place()
        sends = []
        for w in range(n):
            r2 = outs[w].shape[0] // 2
            mine = outs[w].at[pl.ds(pl.multiple_of(c * r2, 8), r2), :]
            sends.append(_remote(mine, mine, send_sems.at[w], recv_sems.at[w], (x, y, 1 - c)))
        for cp in sends:
            cp.start()
        for w in range(n):
            r2 = outs[w].shape[0] // 2
            theirs = outs[w].at[pl.ds(pl.multiple_of((1 - c) * r2, 8), r2), :]
            _remote(theirs, theirs, send_sems.at[w], recv_sems.at[w], (x, y, c)).wait_recv()
        for cp in sends:
            cp.wait_send()

    return _pallas_call(
        body, name=name,
        in_specs=[ANY] * n, out_specs=[ANY] * n,
        out_shape=[_sds(s.shape, F32) for s in shards],
        input_output_aliases={i: i for i in range(n)},
        scratch_shapes=[pltpu.SemaphoreType.DMA((n,)), pltpu.SemaphoreType.DMA((n,))],
    )(*shards)


def _small_allreduce(red_mix, red_ffn, red_final, red_hg, g_conv):
    rows = N_SMALL_ROWS
    D = red_mix.shape[1]
    H = red_hg.shape[1]

    def body(mix_ref, ffn_ref, fin_ref, hg_ref, cv_ref, sum_ref, all_ref, in_ref, send_sems, recv_sems):
        in_ref[...] = jnp.zeros_like(in_ref)
        in_ref[0:1, :] = mix_ref[0:1, :]
        in_ref[1:2, :] = ffn_ref[0:1, :]
        in_ref[2:3, :] = fin_ref[0:1, :]
        gam = hg_ref[1:2, 0:HEAD_DIM]
        for h in range(1, H // HEAD_DIM):
            gam = gam + hg_ref[1:2, h * HEAD_DIM:(h + 1) * HEAD_DIM]
        in_ref[3:4, 0:HEAD_DIM] = gam
        in_ref[3:4, HEAD_DIM:2 * HEAD_DIM] = fin_ref[1:2, 0:HEAD_DIM]
        in_ref[4:5, 0:H] = hg_ref[0:1, :]
        in_ref[6:9, 0:H] = cv_ref[...]
        x, y, c, _, _ = _place()
        me = 4 * x + 2 * y + c
        all_ref[me] = in_ref[...]
        copies = []
        for m in range(1, 8):
            mx, my, mc = (m >> 2) & 1, (m >> 1) & 1, m & 1
            px, py, pc = x ^ mx, y ^ my, c ^ mc
            copies.append((_remote(in_ref, all_ref.at[me], send_sems.at[m - 1], recv_sems.at[m - 1],
                                   (px, py, pc)), 4 * px + 2 * py + pc, m))
        for cp, _, _ in copies:
            cp.start()
        for _, peer, m in copies:
            _remote(in_ref, all_ref.at[peer], send_sems.at[m - 1], recv_sems.at[m - 1],
                    (x, y, c)).wait_recv()
        for cp, _, _ in copies:
            cp.wait_send()
        total = all_ref[0]
        for d in range(1, 8):
            total = total + all_ref[d]
        sum_ref[...] = total

    return _pallas_call(
        body, name="small_allreduce", pin=False,
        in_specs=[VMEM] * 5, out_specs=[VMEM, VMEM],
        out_shape=[_sds((rows, D), F32), _sds((8, rows, D), F32)],
        scratch_shapes=[pltpu.VMEM((rows, D), F32), pltpu.SemaphoreType.DMA((7,)),
                        pltpu.SemaphoreType.DMA((7,))],
    )(red_mix, red_ffn, red_final, red_hg, g_conv)[0]


def _adamw_math(w, g, m, v):
    m = ADAM_B1 * m + (1.0 - ADAM_B1) * g
    v = ADAM_B2 * v + (1.0 - ADAM_B2) * jnp.square(g)
    m_hat = m / (1.0 - ADAM_B1 ** ADAM_STEP)
    v_hat = v / (1.0 - ADAM_B2 ** ADAM_STEP)
    delta = -ADAM_LR * (m_hat / (jnp.sqrt(v_hat) + ADAM_EPS) + ADAM_WD * w)
    return delta, m, v


def _adamw(name, g, w, m, v):
    r, cols = g.shape
    tr = r // 4

    def body(g_ref, w_ref, m_ref, v_ref, go_ref, d_ref, mo_ref, vo_ref):
        g = g_ref[...]
        go_ref[...] = g
        d_ref[...], mo_ref[...], vo_ref[...] = _adamw_math(w_ref[...], g, m_ref[...], v_ref[...])

    blk = pl.BlockSpec((tr, cols), lambda i: (i, 0))
    return _pallas_call(
        body, name=name, grid=(r // tr,),
        in_specs=[blk] * 4, out_specs=[blk] * 4, out_shape=[_sds((r, cols), F32)] * 4,
        compiler_params=_params(("parallel",), 48),
    )(g, w, m, v)


def _small_update(total, chip_idx, ws, ms, vs):
    n = len(ws)
    H = ws[1].shape[1]

    def body(idx_ref, tot_ref, *refs):
        w, m, v, outs = refs[:n], refs[n:2 * n], refs[2 * n:3 * n], refs[3 * n:]
        chip = idx_ref[0]
        p0 = _lower_bound(w[1][...])
        dl0 = p0 * (1.0 - p0) * tot_ref[4:5, 0:H]
        conv = jnp.zeros((3, LANES), F32)
        for k in range(N_CHIPS):
            conv = jnp.where(chip == k, tot_ref[6:9, k * LANES:(k + 1) * LANES], conv)
        grads = [tot_ref[0:1, :], None, tot_ref[3:4, 0:HEAD_DIM], conv, tot_ref[1:2, :], tot_ref[2:3, :]]
        for p in range(n):
            g_ref, d_ref, mo_ref, vo_ref = outs[4 * p:4 * p + 4]
            if p == 1:
                for row, g in ((slice(0, 1), dl0), (slice(1, 2), -dl0)):
                    g_ref[row, :] = g
                    d_ref[row, :], mo_ref[row, :], vo_ref[row, :] = _adamw_math(
                        w[p][row, :], g, m[p][row, :], v[p][row, :])
            else:
                g_ref[...] = grads[p]
                d_ref[...], mo_ref[...], vo_ref[...] = _adamw_math(w[p][...], grads[p], m[p][...], v[p][...])
        outs[4 * n][...] = tot_ref[3:4, HEAD_DIM:2 * HEAD_DIM]

    full = lambda a: pl.BlockSpec(a.shape, lambda i, idx: (0,) * a.ndim)
    out_shape = [_sds(w.shape, F32) for w in ws for _ in range(4)] + [_sds((1, LANES), F32)]
    return _pallas_call(
        body, name="small_update",
        grid_spec=pltpu.PrefetchScalarGridSpec(
            num_scalar_prefetch=1, grid=(1,),
            in_specs=[full(total)] + [full(a) for a in ws + ms + vs],
            out_specs=[full(s) for s in out_shape]),
        out_shape=out_shape,
    )(chip_idx, total, *ws, *ms, *vs)


def kernel(x, norm_mix_g, w_in, lower_bounds, hg_norm_g, conv_w, w_branch_a, w_branch_b, w_out, norm_ffn_g, w_ffn_gate, w_ffn_up, w_ffn_down, norm_final_g, loss_target, m_norm_mix_g, m_w_in, m_lower_bounds, m_hg_norm_g, m_conv_w, m_w_branch_a, m_w_branch_b, m_w_out, m_norm_ffn_g, m_w_ffn_gate, m_w_ffn_up, m_w_ffn_down, m_norm_final_g, v_norm_mix_g, v_w_in, v_lower_bounds, v_hg_norm_g, v_conv_w, v_w_branch_a, v_w_branch_b, v_w_out, v_norm_ffn_g, v_w_ffn_gate, v_w_ffn_up, v_w_ffn_down, v_norm_final_g):
    _, L, D = x.shape
    H = D // 2
    assert lower_bounds.shape == (2, H) and hg_norm_g.shape == (1, HEAD_DIM)
    assert conv_w.shape == (1, 3, LANES) and w_in.shape[2] * N_CHIPS == 11 * H
    x2d, target = x.reshape(L, D), loss_target.reshape(L, D)
    g_final = norm_final_g.reshape(1, D)
    chip = 2 * lax.axis_index("x") + lax.axis_index("y")
    core = lax.axis_index("c")

    tr = lambda w: jnp.transpose(w[0])
    big = [w_in[0], w_branch_a[0], w_branch_b[0], w_out[0], tr(w_ffn_gate), tr(w_ffn_up), w_ffn_down[0]]
    big_m = [m_w_in[0], m_w_branch_a[0], m_w_branch_b[0], m_w_out[0], tr(m_w_ffn_gate), tr(m_w_ffn_up),
             m_w_ffn_down[0]]
    big_v = [v_w_in[0], v_w_branch_a[0], v_w_branch_b[0], v_w_out[0], tr(v_w_ffn_gate), tr(v_w_ffn_up),
             v_w_ffn_down[0]]
    names = ["w_in", "w_branch_a", "w_branch_b", "w_out", "w_ffn_gate", "w_ffn_up", "w_ffn_down"]

    chip_idx = chip.reshape(1).astype(jnp.int32)
    placed = [(_cast_place_t if j < 3 else _cast_place)("place_" + nm, w, chip_idx)
              for j, (nm, w) in enumerate(zip(names, big))]
    conv_placed = lax.dynamic_update_slice(jnp.zeros((N_CHIPS, 3, LANES), F32), conv_w, (chip, 0, 0))
    sems, in_flight, token = _gather_start([([placed[0], conv_placed], {1}), (placed[1:4], set()),
                                            (placed[4:], set())])

    x_i, y_i = lax.axis_index("x"), lax.axis_index("y")
    blocks = lambda *ks: jnp.stack(ks).astype(jnp.int32)
    w_in_buf, conv_buf = in_flight[0]
    h, proj = _fwd_proj_first(x2d, norm_mix_g, w_in_buf, blocks(chip))
    w_in_buf, conv_buf = _gather_wait("gather_wait_in_near", [w_in_buf, conv_buf], {1}, sems[0], h,
                                      lambda w, j: w == 0 and j < 2)
    (w_in_buf,) = _gather_forward("gather_fwd_in_near", [w_in_buf], (0, 1))
    proj = _fwd_proj_more("fwd_proj_near", h, w_in_buf, proj,
                          blocks(2 * (1 - x_i) + y_i, 2 * x_i + (1 - y_i)))
    w_in_buf, conv_all = _gather_wait("gather_wait_in_far", [w_in_buf, conv_buf], {1}, sems[0], proj,
                                      lambda w, j: w == 1 or j == 2)
    (w_int3,) = _gather_forward("gather_fwd_in_far", [w_in_buf], (2,))
    proj = _fwd_proj_more("fwd_proj_far", h, w_int3, proj, blocks(2 * (1 - x_i) + (1 - y_i)))
    w_int = w_int3.reshape(-1, D)
    conv_full = jnp.transpose(conv_all, (1, 0, 2)).reshape(3, H)
    og, o_pre, s_saved = _hgrn_fwd(proj, lower_bounds, hg_norm_g, H)
    landed = _gather_wait("gather_wait_mix", in_flight[1], set(), sems[1], og)
    fwd_sems, landed, token = _split_start("gather_fwd_mix_start", landed, 9, _forward_pairs)
    cb = _conv_fwd(proj, conv_full, H, token)
    wat3, wbt3, wout3 = _split_wait("gather_fwd_mix_wait", fwd_sems, landed, _forward_pairs, cb)
    wat, wbt, wout = wat3.reshape(D, H), wbt3.reshape(D, H), wout3.reshape(D, D)
    landed = _gather_wait("gather_wait_ffn", in_flight[2], set(), sems[2], cb)
    fwd_sems, landed, token = _split_start("gather_fwd_ffn_start", landed, 9, _forward_pairs)
    sig_a, sig_b, dm_dga, dm_dgb, merged, x1, h2 = _fwd_mix(og, cb, proj, x2d, wat, wbt, wout, norm_ffn_g,
                                                              H, token)
    wgt3, wut3, wd3 = _split_wait("gather_fwd_ffn_wait", fwd_sems, landed, _forward_pairs, h2)
    d_ff = N_CHIPS * wd3.shape[1]
    wgt, wut, wd = wgt3.reshape(d_ff, D), wut3.reshape(d_ff, D), wd3.reshape(d_ff, D)
    ffn_ds_da, ffn_ds_db, ffn_s = _fwd_ffn_up(h2, wgt, wut)
    dx2, dx2b, red_final = _fwd_down_loss(ffn_s, wd, x1, target, g_final)

    c_idx = core.reshape(1).astype(jnp.int32)
    place_idx = jnp.stack([chip, core]).astype(jnp.int32)

    def sibling_start(tag, grads):
        bufs = [lax.empty((N_CHIPS, g.shape[1] // 2, g.shape[2]), F32) for g in grads]
        return _split_start("rs_sibling_start_" + tag, list(grads) + bufs, len(grads), _sibling_pairs)

    def ici_start(tag, js, grads, from_sibling):
        partials = [_rs_add("rs_add_" + names[j], g, s, c_idx) for j, g, s in zip(js, grads, from_sibling)]
        landings = [lax.empty((3,) + p.shape[1:], BF16) for p in partials]
        return _split_start("rs_ici_start_" + tag, partials + landings, 3 * len(js), _ici_pairs)

    def ici_start_behind(tag, js, started, after):
        n = len(js)
        arrays = _split_wait("rs_sibling_wait_" + tag, started[0], started[1], _sibling_pairs, after)
        return ici_start(tag, js, arrays[:n], arrays[n:])

    def rs_end(tag, js, started, after):
        n = len(js)
        arrays = _split_wait("rs_ici_wait_" + tag, started[0], started[1], _ici_pairs, after)
        halves = [_rs_sum("rs_sum_" + names[j], p, r, place_idx)
                  for j, p, r in zip(js, arrays[:n], arrays[n:])]
        grads = _rs_share("rs_share_" + tag, halves)
        return [_adamw("adamw_" + names[j], g, big[j], big_m[j], big_v[j]) for j, g in zip(js, grads)]

    shards3 = lambda g: g.reshape(N_CHIPS, d_ff // N_CHIPS, D)
    da, db = _bwd_down(dx2b, wd, ffn_ds_da, ffn_ds_db)
    g_wd = shards3(_dw_rows2("dw_ffn_down", ffn_s, dx2b))
    g_wg = shards3(_dw_rows2("dw_ffn_gate", da, h2))
    g_wu = shards3(_dw_rows2("dw_ffn_up", db, h2))
    ffn_sibling = sibling_start("ffn", [g_wg, g_wu, g_wd])
    dx1, dx1b, red_ffn = _bwd_ffn_dh(da, db, wgt, wut, x1, dx2, norm_ffn_g, ffn_sibling[2])
    ffn_ici = ici_start_behind("ffn", [4, 5, 6], ffn_sibling, dx1b)
    dya, dyb, dga, dgb, d_o, d_cb = _bwd_mix(dx1b, sig_a, sig_b, dm_dga, dm_dgb, wat, wbt, wout, H,
                                             ffn_ici[2])
    g_wout = _dw_rows("dw_out", merged, dx1b)
    g_wa = _dw_cols("dw_branch_a", og, dya, D // N_CHIPS)
    g_wb = _dw_cols("dw_branch_b", cb, dyb, D // N_CHIPS)
    mix_sibling = sibling_start("mix", [g_wa, g_wb, g_wout])
    dq, df, dv, dg, red_hg = _hgrn_bwd(proj, lower_bounds, hg_norm_g, o_pre, d_o, s_saved, H, mix_sibling[2])
    mix_ici = ici_start_behind("mix", [1, 2, 3], mix_sibling, dq)
    dcg, dbg, dxb, g_conv = _conv_bwd(proj, conv_full, d_cb, H, mix_ici[2])
    dproj = jnp.concatenate([dq, df, dv, dg, dcg, dbg, dxb, dga, dgb], axis=1)
    g_win = _dw_cols("dw_in", h, dproj, w_int3.shape[1])
    in_sibling = sibling_start("in", [g_win])
    big_out = [None] + rs_end("mix", [1, 2, 3], mix_ici, in_sibling[2]) + rs_end(
        "ffn", [4, 5, 6], ffn_ici, in_sibling[2])
    in_ici = ici_start_behind("in", [0], in_sibling, big_out[6][0])
    grad_x, red_mix = _bwd_in(dproj, w_int, x2d, dx1, norm_mix_g, in_ici[2])
    big_out[0] = rs_end("in", [0], in_ici, grad_x)[0]

    total = _small_allreduce(red_mix, red_ffn, red_final, red_hg, g_conv)

    def smalls(mix, lb, hg, cw, ffn, fin):
        return [mix, lb, hg, cw[0], ffn, fin.reshape(1, D)]

    small_out = _small_update(
        total, chip_idx,
        smalls(norm_mix_g, lower_bounds, hg_norm_g, conv_w, norm_ffn_g, norm_final_g),
        smalls(m_norm_mix_g, m_lower_bounds, m_hg_norm_g, m_conv_w, m_norm_ffn_g, m_norm_final_g),
        smalls(v_norm_mix_g, v_lower_bounds, v_hg_norm_g, v_conv_w, v_norm_ffn_g, v_norm_final_g))

    def outputs(i):
        big_i = [big_out[j][i] for j in range(7)]
        mix, lb, hg, cw, ffn, fin = [small_out[4 * p + i] for p in range(6)]
        return [mix, big_i[0][None], lb, hg, cw[None], big_i[1][None], big_i[2][None], big_i[3][None], ffn,
                big_i[4].T[None], big_i[5].T[None], big_i[6][None], fin.reshape(D)]

    outs = [small_out[24][0, 0], grad_x.reshape(1, L, D)]
    for i in range(4):
        outs += outputs(i)
    return tuple(outs)
```

```python
import functools

import jax
import jax.numpy as jnp
from jax import lax
from jax.experimental import pallas as pl
from jax.experimental.pallas import tpu as pltpu

F32 = jnp.float32
BF16 = jnp.bfloat16
EPS = 1e-6
CHUNK = 32
HEAD_DIM = 128
LANES = 128
N_CHIPS = 4
N_SMALL_ROWS = 16

ADAM_LR = 0.001
ADAM_B1 = 0.9
ADAM_B2 = 0.999
ADAM_EPS = 1e-08
ADAM_WD = 0.01
ADAM_STEP = 10

MESH = pl.DeviceIdType.MESH
ANY = pl.BlockSpec(memory_space=pl.ANY)
VMEM = pl.BlockSpec(memory_space=pltpu.VMEM)
HBM = pl.BlockSpec(memory_space=pltpu.HBM)
SEM = pl.BlockSpec(memory_space=pltpu.SEMAPHORE)
EFFECT = pltpu.SideEffectType.DATAFLOW_SIDE_EFFECTING


def _sds(shape, dtype):
    return jax.ShapeDtypeStruct(shape, dtype)


def _pallas_call(body, pin=True, **kwargs):
    if not pin:
        return pl.pallas_call(body, **kwargs)
    in_hbm = lambda s: pltpu.HBM(s.shape, s.dtype) if isinstance(s, jax.ShapeDtypeStruct) else s
    kwargs["out_shape"] = jax.tree.map(in_hbm, kwargs["out_shape"])
    call = pl.pallas_call(body, **kwargs)

    def run(*args):
        return call(*[pltpu.with_memory_space_constraint(a, pltpu.HBM) if a.dtype in (F32, BF16) else a
                      for a in args])

    return run


def _params(semantics, vmem_mb):
    return pltpu.CompilerParams(dimension_semantics=semantics, vmem_limit_bytes=vmem_mb << 20)


def _nn(a, b):
    return lax.dot_general(a, b, (((1,), (0,)), ((), ())), preferred_element_type=F32)


def _nt(a, b):
    return lax.dot_general(a, b, (((1,), (1,)), ((), ())), preferred_element_type=F32)


def _tn(a, b):
    return lax.dot_general(a, b, (((0,), (0,)), ((), ())), preferred_element_type=F32)


def _sigmoid(x):
    return jax.nn.sigmoid(x)


def _rms_stats(x):
    r = lax.rsqrt(jnp.mean(x * x, axis=-1, keepdims=True) + EPS)
    return r, x * r


def _rms_bwd(dxh, xh, r):
    return r * (dxh - xh * jnp.mean(dxh * xh, axis=-1, keepdims=True))


def _fwd_proj_first(x, g_mix, w_int3, block):
    L, D = x.shape
    tn = w_int3.shape[1]
    tm = min(L, 1024)

    def body(blk_ref, x_ref, g_ref, w_ref, h_ref, p_ref):
        _, xh = _rms_stats(x_ref[...])
        h = (xh * g_ref[...]).astype(BF16)
        h_ref[...] = h
        p_ref[...] = _nt(h, w_ref[...])

    return _pallas_call(
        body, name="fwd_proj_own",
        grid_spec=pltpu.PrefetchScalarGridSpec(
            num_scalar_prefetch=1, grid=(L // tm,),
            in_specs=[pl.BlockSpec((tm, D), lambda i, blk: (i, 0)),
                      pl.BlockSpec((1, D), lambda i, blk: (0, 0)),
                      pl.BlockSpec((None, tn, D), lambda i, blk: (blk[0], 0, 0))],
            out_specs=[pl.BlockSpec((tm, D), lambda i, blk: (i, 0)),
                       pl.BlockSpec((tm, tn), lambda i, blk: (i, blk[0]))]),
        out_shape=[_sds((L, D), BF16), _sds((L, N_CHIPS * tn), F32)],
        compiler_params=_params(("parallel",), 48),
    )(block, x, g_mix, w_int3)


def _fwd_proj_more(name, h, w_int3, proj, blocks):
    L, D = h.shape
    tn = w_int3.shape[1]
    tm = min(L, 1024)

    def body(blk_ref, h_ref, w_ref, proj_ref, p_ref):
        p_ref[...] = _nt(h_ref[...], w_ref[...])

    return _pallas_call(
        body, name=name,
        grid_spec=pltpu.PrefetchScalarGridSpec(
            num_scalar_prefetch=1, grid=(L // tm, blocks.shape[0]),
            in_specs=[pl.BlockSpec((tm, D), lambda i, j, blk: (i, 0)),
                      pl.BlockSpec((None, tn, D), lambda i, j, blk: (blk[j], 0, 0)), ANY],
            out_specs=pl.BlockSpec((tm, tn), lambda i, j, blk: (i, blk[j]))),
        out_shape=_sds(proj.shape, proj.dtype),
        input_output_aliases={3: 0},
        compiler_params=_params(("parallel", "arbitrary"), 48),
    )(blocks, h, w_int3, proj)


def _lower_bound(lbp):
    l0, l1 = lbp[0:1, :], lbp[1:2, :]
    m = jnp.maximum(l0, l1)
    e0, e1 = jnp.exp(l0 - m), jnp.exp(l1 - m)
    return e0 / (e0 + e1)


def _seg_scan(x, r32, forward):
    n = x.shape[0]
    s = 1
    while s < CHUNK:
        if forward:
            x = x + jnp.where(r32 >= s, pltpu.roll(x, s, 0), 0.0)
        else:
            x = x + jnp.where(r32 < CHUNK - s, pltpu.roll(x, n - s, 0), 0.0)
        s *= 2
    return x


def _bcast_row(x, row):
    n, w = x.shape
    nc = n // CHUNK
    x3 = x.reshape(nc, CHUNK, w)
    return jnp.broadcast_to(x3[:, row:row + 1, :], (nc, CHUNK, w)).reshape(n, w)


def _hgrn_prep(q_raw, f_raw, lb):
    r32 = lax.broadcasted_iota(jnp.int32, f_raw.shape, 0) & (CHUNK - 1)
    sig = _sigmoid(f_raw)
    f = lb + (1.0 - lb) * sig
    b = _seg_scan(jnp.log(f), r32, True)
    a = _bcast_row(b, CHUNK // 2 - 1)
    bl = _bcast_row(b, CHUNK - 1)
    sq = _sigmoid(q_raw)
    q = q_raw * sq * (HEAD_DIM ** -0.5)
    return dict(r32=r32, sig=sig, f=f, k=1.0 - f, b=b, a=a, bl=bl, sq=sq, q=q)


def _chunk_masks(n):
    ri = lax.broadcasted_iota(jnp.int32, (n, n), 0)
    ci = lax.broadcasted_iota(jnp.int32, (n, n), 1)
    same = (ri // CHUNK) == (ci // CHUNK)
    return same & (ci <= ri), same & (ri <= ci)


def _hgrn_fwd(proj, lower_bounds, gamma, H):
    L = proj.shape[0]
    nh = H // HEAD_DIM
    TL = min(L, 256)
    nc = TL // CHUNK

    def body(q_ref, f_ref, v_ref, g_ref, lbp_ref, gam_ref, og_ref, o_ref, s_ref, st_ref):
        @pl.when(pl.program_id(0) == 0)
        def _():
            st_ref[...] = jnp.zeros_like(st_ref)

        lb = _lower_bound(lbp_ref[...])
        gam = gam_ref[...]
        mask, _ = _chunk_masks(TL)
        rowc = lax.broadcasted_iota(jnp.int32, (TL, HEAD_DIM), 0) // CHUNK
        for h in range(nh):
            hs = slice(h * HEAD_DIM, (h + 1) * HEAD_DIM)
            p = _hgrn_prep(q_ref[:, hs], f_ref[:, hs], lb[:, hs])
            v = v_ref[:, hs]
            vb = v.astype(BF16)
            vt = v.T.astype(BF16)
            q_hat = (p["q"] * jnp.exp(p["b"] - p["a"])).astype(BF16)
            k_hat = (p["k"] * jnp.exp(p["a"] - p["b"])).astype(BF16)
            q_in = (p["q"] * jnp.exp(p["b"])).astype(BF16)
            k_out = (p["k"] * jnp.exp(p["bl"] - p["b"])).astype(BF16)
            dec = jnp.exp(p["bl"])
            att = jnp.where(mask, _nt(q_hat, k_hat), 0.0).astype(BF16)
            o_intra = _nn(att, vb)
            st = st_ref[h]
            for c in range(nc):
                rs = slice(c * CHUNK, (c + 1) * CHUNK)
                stb = st.astype(BF16)
                s_ref[c, h] = stb
                o_ref[rs, hs] = o_intra[rs] + _nt(q_in[rs], stb)
                k_c = jnp.where(rowc == c, k_out, jnp.zeros_like(k_out))
                st = st * dec[c * CHUNK:c * CHUNK + 1, :] + _nn(vt, k_c)
            st_ref[h] = st
            o = o_ref[:, hs]
            _, xh = _rms_stats(o)
            gr = g_ref[:, hs]
            og_ref[:, hs] = (xh * gam * (gr * _sigmoid(gr))).astype(BF16)

    col = lambda k: pl.BlockSpec((TL, H), lambda i, k=k: (i, k))
    return _pallas_call(
        body, name="hgrn_fwd", grid=(L // TL,),
        in_specs=[col(0), col(1), col(2), col(3),
                  pl.BlockSpec(lower_bounds.shape, lambda i: (0, 0)),
                  pl.BlockSpec(gamma.shape, lambda i: (0, 0))],
        out_specs=[pl.BlockSpec((TL, H), lambda i: (i, 0)),
                   pl.BlockSpec((TL, H), lambda i: (i, 0)),
                   pl.BlockSpec((nc, nh, HEAD_DIM, HEAD_DIM), lambda i: (i, 0, 0, 0))],
        out_shape=[_sds((L, H), BF16), _sds((L, H), F32),
                   _sds((L // CHUNK, nh, HEAD_DIM, HEAD_DIM), BF16)],
        scratch_shapes=[pltpu.VMEM((nh, HEAD_DIM, HEAD_DIM), F32)],
        compiler_params=_params(("arbitrary",), 48),
    )(proj, proj, proj, proj, lower_bounds, gamma)


def _hgrn_bwd(proj, lower_bounds, gamma, o_pre, d_out, s_saved, H, after):
    L = proj.shape[0]
    nh = H // HEAD_DIM
    TL = min(L, 256)
    nc = TL // CHUNK
    nt = L // TL

    def body(q_ref, f_ref, v_ref, g_ref, lbp_ref, gam_ref, o_ref, d_ref, s_ref, after_ref,
             dq_ref, df_ref, dv_ref, dg_ref, red_ref, dst_ref, dsall_ref, tmp_ref):
        @pl.when(pl.program_id(0) == 0)
        def _():
            dst_ref[...] = jnp.zeros_like(dst_ref)
            red_ref[...] = jnp.zeros_like(red_ref)

        lb = _lower_bound(lbp_ref[...])
        gam = gam_ref[...]
        mask, mask_t = _chunk_masks(TL)
        rowc = lax.broadcasted_iota(jnp.int32, (TL, HEAD_DIM), 0) // CHUNK
        for h in range(nh):
            hs = slice(h * HEAD_DIM, (h + 1) * HEAD_DIM)
            qr, gr, lbh = q_ref[:, hs], g_ref[:, hs], lb[:, hs]
            p = _hgrn_prep(qr, f_ref[:, hs], lbh)
            vb = v_ref[:, hs].astype(BF16)
            eba, eab = jnp.exp(p["b"] - p["a"]), jnp.exp(p["a"] - p["b"])
            eb, elb = jnp.exp(p["b"]), jnp.exp(p["bl"] - p["b"])
            dec = jnp.exp(p["bl"])
            q_hat, k_hat = p["q"] * eba, p["k"] * eab
            q_in, k_out = p["q"] * eb, p["k"] * elb
            q_hat_b, k_hat_b = q_hat.astype(BF16), k_hat.astype(BF16)
            q_in_b, k_out_b = q_in.astype(BF16), k_out.astype(BF16)

            o, dout = o_ref[:, hs], d_ref[:, hs]
            sg = _sigmoid(gr)
            r, xh = _rms_stats(o)
            dg_ref[:, hs] = (dout * (xh * gam) * (sg * (1.0 + gr * (1.0 - sg)))).astype(BF16)
            dn = dout * (gr * sg)
            red_ref[1:2, hs] += jnp.sum(dn * xh, axis=0, keepdims=True)
            do = _rms_bwd(dn * gam, xh, r)
            dob = do.astype(BF16)
            dot_b = do.T.astype(BF16)

            att_t = jnp.where(mask_t, _nt(k_hat_b, q_hat_b), 0.0).astype(BF16)
            dv_intra = _nn(att_t, dob)
            datt = jnp.where(mask, _nt(dob, vb), 0.0).astype(BF16)
            dqh = _nn(datt, k_hat_b)
            datt_t = jnp.where(mask_t, _nt(vb, dob), 0.0).astype(BF16)
            dkh = _nn(datt_t, q_hat_b)

            dst = dst_ref[h]
            for c in reversed(range(nc)):
                dsall_ref[c] = dst
                q_c = jnp.where(rowc == c, q_in_b, jnp.zeros_like(q_in_b))
                dst = dst * dec[c * CHUNK:c * CHUNK + 1, :] + _nn(dot_b, q_c)
            dst_ref[h] = dst
            for c in range(nc):
                rs = slice(c * CHUNK, (c + 1) * CHUNK)
                ds_c = dsall_ref[c]
                dsb = ds_c.astype(BF16)
                st_prev = s_ref[c, h]
                tmp_ref[0, rs, :] = _nt(k_out_b[rs], dsb)
                tmp_ref[1, rs, :] = _nn(vb[rs], dsb)
                tmp_ref[2, rs, :] = _nn(dob[rs], st_prev)
                ddec = jnp.sum(ds_c * st_prev.astype(F32), axis=0, keepdims=True)
                tmp_ref[3, rs, :] = jnp.broadcast_to(ddec * dec[c * CHUNK:c * CHUNK + 1, :],
                                                     (CHUNK, HEAD_DIM))
            dko, dqi = tmp_ref[1], tmp_ref[2]
            dq = dqh * eba + dqi * eb
            dk = dkh * eab + dko * elb
            tko = dko * k_out
            db = dqh * q_hat - dkh * k_hat + dqi * q_in - tko
            dlog = (_seg_scan(db, p["r32"], False)
                    + _bcast_row(_seg_scan(tko, p["r32"], True), CHUNK - 1) + tmp_ref[3])
            df = dlog / p["f"] - dk
            sig = p["sig"]
            red_ref[0:1, hs] += jnp.sum(df * (1.0 - sig), axis=0, keepdims=True)
            df_ref[:, hs] = (df * (1.0 - lbh) * sig * (1.0 - sig)).astype(BF16)
            sq = p["sq"]
            dq_ref[:, hs] = (dq * (HEAD_DIM ** -0.5) * (sq * (1.0 + qr * (1.0 - sq)))).astype(BF16)
            dv_ref[:, hs] = (dv_intra + tmp_ref[0]).astype(BF16)

    col = lambda k: pl.BlockSpec((TL, H), lambda i, k=k: (nt - 1 - i, k))
    rev = pl.BlockSpec((TL, H), lambda i: (nt - 1 - i, 0))
    return _pallas_call(
        body, name="hgrn_bwd", grid=(nt,),
        in_specs=[col(0), col(1), col(2), col(3),
                  pl.BlockSpec(lower_bounds.shape, lambda i: (0, 0)),
                  pl.BlockSpec(gamma.shape, lambda i: (0, 0)),
                  rev, rev,
                  pl.BlockSpec((nc, nh, HEAD_DIM, HEAD_DIM), lambda i: (nt - 1 - i, 0, 0, 0)), ANY],
        out_specs=[rev, rev, rev, rev, pl.BlockSpec((8, H), lambda i: (0, 0))],
        out_shape=[_sds((L, H), BF16)] * 4 + [_sds((8, H), F32)],
        scratch_shapes=[pltpu.VMEM((nh, HEAD_DIM, HEAD_DIM), F32),
                        pltpu.VMEM((nc, HEAD_DIM, HEAD_DIM), F32),
                        pltpu.VMEM((4, TL, HEAD_DIM), F32)],
        compiler_params=_params(("arbitrary",), 48),
    )(proj, proj, proj, proj, lower_bounds, gamma, o_pre, d_out, s_saved, after)


def _shift_down(u, s, row):
    return jnp.where(row >= s, pltpu.roll(u, s, 0), 0.0)


def _shift_up(u, s, row):
    n = u.shape[0]
    return jnp.where(row < n - s, pltpu.roll(u, n - s, 0), 0.0)


def _conv_specs(L, H):
    per = H // LANES
    return [pl.BlockSpec((L, LANES), lambda j, o=o: (0, o * per + j)) for o in (4, 5, 6)]


def _conv_fwd(proj, conv_w, H, after):
    L = proj.shape[0]

    def body(c_ref, b_ref, x_ref, w_ref, after_ref, o_ref):
        row = lax.broadcasted_iota(jnp.int32, (L, LANES), 0)
        u = c_ref[...] * x_ref[...]
        w = w_ref[...]
        y = w[0:1] * _shift_down(u, 2, row) + w[1:2] * _shift_down(u, 1, row) + w[2:3] * u
        o_ref[...] = (b_ref[...] * y).astype(BF16)

    return _pallas_call(
        body, name="conv_fwd", grid=(H // LANES,),
        in_specs=_conv_specs(L, H) + [pl.BlockSpec((3, LANES), lambda j: (0, j)), ANY],
        out_specs=pl.BlockSpec((L, LANES), lambda j: (0, j)),
        out_shape=_sds((L, H), BF16),
        compiler_params=_params(("parallel",), 48),
    )(proj, proj, proj, conv_w, after)


def _conv_bwd(proj, conv_w, dcb, H, after):
    L = proj.shape[0]

    def body(c_ref, b_ref, x_ref, w_ref, d_ref, after_ref, dc_ref, db_ref, dx_ref, dw_ref):
        row = lax.broadcasted_iota(jnp.int32, (L, LANES), 0)
        cg, xb = c_ref[...], x_ref[...]
        u = cg * xb
        u1, u2 = _shift_down(u, 1, row), _shift_down(u, 2, row)
        w = w_ref[...]
        y = w[0:1] * u2 + w[1:2] * u1 + w[2:3] * u
        d = d_ref[...]
        db_ref[...] = (d * y).astype(BF16)
        dy = d * b_ref[...]
        du = w[2:3] * dy + w[1:2] * _shift_up(dy, 1, row) + w[0:1] * _shift_up(dy, 2, row)
        dw_ref[0:1, :] = jnp.sum(dy * u2, axis=0, keepdims=True)
        dw_ref[1:2, :] = jnp.sum(dy * u1, axis=0, keepdims=True)
        dw_ref[2:3, :] = jnp.sum(dy * u, axis=0, keepdims=True)
        dc_ref[...] = (du * xb).astype(BF16)
        dx_ref[...] = (du * cg).astype(BF16)

    blk = pl.BlockSpec((L, LANES), lambda j: (0, j))
    return _pallas_call(
        body, name="conv_bwd", grid=(H // LANES,),
        in_specs=_conv_specs(L, H) + [pl.BlockSpec((3, LANES), lambda j: (0, j)), blk, ANY],
        out_specs=[blk, blk, blk, pl.BlockSpec((3, LANES), lambda j: (0, j))],
        out_shape=[_sds((L, H), BF16)] * 3 + [_sds((3, H), F32)],
        compiler_params=_params(("parallel",), 56),
    )(proj, proj, proj, conv_w, dcb, after)


def _gate_specs(tm, H):
    return [pl.BlockSpec((tm, H), lambda i, k=k: (i, k)) for k in (7, 8, 9, 10)]


def _fwd_mix(og, cb, proj, x, wat, wbt, wout, g_ffn, H, after):
    L, D = x.shape
    tm = min(L, 512)

    def body(o_ref, cb_ref, ga0, ga1, gb0, gb1, x_ref, wa_ref, wb_ref, wo_ref, g_ref, after_ref,
             sa_ref, sb_ref, ta_ref, tb_ref, m_ref, x1_ref, h2_ref):
        ya, yb = _nt(o_ref[...], wa_ref[...]), _nt(cb_ref[...], wb_ref[...])
        for k, (gar, gbr) in enumerate(((ga0, gb0), (ga1, gb1))):
            cs = slice(k * H, (k + 1) * H)
            sa, sb = _sigmoid(gar[...]), _sigmoid(gbr[...])
            ma, mb = sa * ya[:, cs], sb * yb[:, cs]
            m_ref[:, cs] = (ma + mb).astype(BF16)
            sa_ref[:, cs] = sa.astype(BF16)
            sb_ref[:, cs] = sb.astype(BF16)
            ta_ref[:, cs] = (ma * (1.0 - sa)).astype(BF16)
            tb_ref[:, cs] = (mb * (1.0 - sb)).astype(BF16)
        x1 = x_ref[...] + _nn(m_ref[...], wo_ref[...])
        x1_ref[...] = x1
        _, xh = _rms_stats(x1)
        h2_ref[...] = (xh * g_ref[...]).astype(BF16)

    row = lambda w: pl.BlockSpec((tm, w), lambda i: (i, 0))
    full = lambda a: pl.BlockSpec(a.shape, lambda i: (0,) * a.ndim)
    return _pallas_call(
        body, name="fwd_mix", grid=(L // tm,),
        in_specs=[row(H), row(H)] + _gate_specs(tm, H) + [row(D), full(wat), full(wbt), full(wout),
                                                           full(g_ffn), ANY],
        out_specs=[row(D)] * 7,
        out_shape=[_sds((L, D), BF16)] * 5 + [_sds((L, D), F32), _sds((L, D), BF16)],
        compiler_params=_params(("parallel",), 56),
    )(og, cb, proj, proj, proj, proj, x, wat, wbt, wout, g_ffn, after)


def _bwd_mix(dx1b, sig_a, sig_b, dm_dga, dm_dgb, wat, wbt, wout, H, after):
    L, D = dx1b.shape
    tm = min(L, 512)

    def body(dx_ref, sa_ref, sb_ref, ta_ref, tb_ref, wa_ref, wb_ref, wo_ref, after_ref,
             dya_ref, dyb_ref, dga_ref, dgb_ref, do_ref, dcb_ref):
        dm = _nt(dx_ref[...], wo_ref[...])
        dga_ref[...] = (dm * ta_ref[...].astype(F32)).astype(BF16)
        dgb_ref[...] = (dm * tb_ref[...].astype(F32)).astype(BF16)
        dya_ref[...] = (dm * sa_ref[...].astype(F32)).astype(BF16)
        dyb_ref[...] = (dm * sb_ref[...].astype(F32)).astype(BF16)
        do_ref[...] = _nn(dya_ref[...], wa_ref[...])
        dcb_ref[...] = _nn(dyb_ref[...], wb_ref[...])

    row = lambda w: pl.BlockSpec((tm, w), lambda i: (i, 0))
    full = lambda a: pl.BlockSpec(a.shape, lambda i: (0,) * a.ndim)
    return _pallas_call(
        body, name="bwd_mix", grid=(L // tm,),
        in_specs=[row(D)] * 5 + [full(wat), full(wbt), full(wout), ANY],
        out_specs=[row(D)] * 4 + [row(H)] * 2,
        out_shape=[_sds((L, D), BF16)] * 4 + [_sds((L, H), F32)] * 2,
        compiler_params=_params(("parallel",), 56),
    )(dx1b, sig_a, sig_b, dm_dga, dm_dgb, wat, wbt, wout, after)


def _fwd_ffn_up(h2, wgt, wut):
    L, D = h2.shape
    F = wgt.shape[0]
    tn = F // 2
    tm = min(L, 512)

    def body(h_ref, wg_ref, wu_ref, sa_ref, sb_ref, s_ref):
        h = h_ref[...]
        a, b = _nt(h, wg_ref[...]), _nt(h, wu_ref[...])
        sg = _sigmoid(a)
        silu = a * sg
        sa_ref[...] = (b * sg * (1.0 + a * (1.0 - sg))).astype(BF16)
        sb_ref[...] = silu.astype(BF16)
        s_ref[...] = (silu * b).astype(BF16)

    wspec = pl.BlockSpec((tn, D), lambda j, i: (j, 0))
    ospec = pl.BlockSpec((tm, tn), lambda j, i: (i, j))
    return _pallas_call(
        body, name="fwd_ffn_up", grid=(2, L // tm),
        in_specs=[pl.BlockSpec((tm, D), lambda j, i: (i, 0)), wspec, wspec],
        out_specs=[ospec] * 3,
        out_shape=[_sds((L, F), BF16)] * 3,
        compiler_params=_params(("parallel", "parallel"), 48),
    )(h2, wgt, wut)


def _fwd_down_loss(s, wd, x1, target, g_final):
    L, D = x1.shape
    F = wd.shape[0]
    tm = min(L, 256)

    def body(s_ref, wd_ref, x1_ref, t_ref, g_ref, dx_ref, dxb_ref, red_ref):
        @pl.when(pl.program_id(0) == 0)
        def _():
            red_ref[...] = jnp.zeros_like(red_ref)

        g = g_ref[...]
        r, xh = _rms_stats(x1_ref[...] + _nn(s_ref[...], wd_ref[...]))
        e = xh * g - t_ref[...]
        dy = e * (1.0 / D)
        dx = _rms_bwd(dy * g, xh, r)
        dx_ref[...] = dx
        dxb_ref[...] = dx.astype(BF16)
        red_ref[0:1, :] += jnp.sum(dy * xh, axis=0, keepdims=True)
        red_ref[1:2, :] += jnp.broadcast_to(0.5 * jnp.sum(e * e) * (1.0 / D), (1, D))

    row = pl.BlockSpec((tm, D), lambda i: (i, 0))
    return _pallas_call(
        body, name="fwd_down_loss", grid=(L // tm,),
        in_specs=[pl.BlockSpec((tm, F), lambda i: (i, 0)), pl.BlockSpec((F, D), lambda i: (0, 0)),
                  row, row, pl.BlockSpec((1, D), lambda i: (0, 0))],
        out_specs=[row, row, pl.BlockSpec((8, D), lambda i: (0, 0))],
        out_shape=[_sds((L, D), F32), _sds((L, D), BF16), _sds((8, D), F32)],
        compiler_params=_params(("arbitrary",), 48),
    )(s, wd, x1, target, g_final)


def _bwd_down(dx2b, wd, s_a, s_b):
    L, D = dx2b.shape
    F = wd.shape[0]
    tn = F // 2
    tm = min(L, 512)

    def body(dx_ref, wd_ref, sa_ref, sb_ref, da_ref, db_ref):
        ds = _nt(dx_ref[...], wd_ref[...])
        da_ref[...] = (ds * sa_ref[...].astype(F32)).astype(BF16)
        db_ref[...] = (ds * sb_ref[...].astype(F32)).astype(BF16)

    ospec = pl.BlockSpec((tm, tn), lambda j, i: (i, j))
    return _pallas_call(
        body, name="bwd_down", grid=(2, L // tm),
        in_specs=[pl.BlockSpec((tm, D), lambda j, i: (i, 0)),
                  pl.BlockSpec((tn, D), lambda j, i: (j, 0)), ospec, ospec],
        out_specs=[ospec] * 2,
        out_shape=[_sds((L, F), BF16)] * 2,
        compiler_params=_params(("parallel", "parallel"), 48),
    )(dx2b, wd, s_a, s_b)


def _bwd_ffn_dh(da, db, wgt, wut, x1, dx2, g_ffn, after):
    L, D = x1.shape
    F = wgt.shape[0]
    tm = min(L, 256)

    def body(da_ref, db_ref, wg_ref, wu_ref, x1_ref, dx2_ref, g_ref, after_ref, dx_ref, dxb_ref, red_ref):
        @pl.when(pl.program_id(0) == 0)
        def _():
            red_ref[...] = jnp.zeros_like(red_ref)

        dh = _nn(da_ref[...], wg_ref[...]) + _nn(db_ref[...], wu_ref[...])
        r, xh = _rms_stats(x1_ref[...])
        red_ref[0:1, :] += jnp.sum(dh * xh, axis=0, keepdims=True)
        dx = dx2_ref[...] + _rms_bwd(dh * g_ref[...], xh, r)
        dx_ref[...] = dx
        dxb_ref[...] = dx.astype(BF16)

    row = pl.BlockSpec((tm, D), lambda i: (i, 0))
    aspec = pl.BlockSpec((tm, F), lambda i: (i, 0))
    wspec = pl.BlockSpec((F, D), lambda i: (0, 0))
    return _pallas_call(
        body, name="bwd_ffn_dh", grid=(L // tm,),
        in_specs=[aspec, aspec, wspec, wspec, row, row, pl.BlockSpec((1, D), lambda i: (0, 0)), ANY],
        out_specs=[row, row, pl.BlockSpec((8, D), lambda i: (0, 0))],
        out_shape=[_sds((L, D), F32), _sds((L, D), BF16), _sds((8, D), F32)],
        compiler_params=_params(("arbitrary",), 56),
    )(da, db, wgt, wut, x1, dx2, g_ffn, after)


def _bwd_in(dproj, w_int, x, dx1, g_mix, after):
    L, D = x.shape
    N = w_int.shape[0]
    tm = min(L, 256)

    def body(dp_ref, w_ref, x_ref, dx1_ref, g_ref, after_ref, dx_ref, red_ref):
        @pl.when(pl.program_id(0) == 0)
        def _():
            red_ref[...] = jnp.zeros_like(red_ref)

        dh = _nn(dp_ref[...], w_ref[...])
        r, xh = _rms_stats(x_ref[...])
        red_ref[0:1, :] += jnp.sum(dh * xh, axis=0, keepdims=True)
        dx_ref[...] = dx1_ref[...] + _rms_bwd(dh * g_ref[...], xh, r)

    row = pl.BlockSpec((tm, D), lambda i: (i, 0))
    return _pallas_call(
        body, name="bwd_in", grid=(L // tm,),
        in_specs=[pl.BlockSpec((tm, N), lambda i: (i, 0)), pl.BlockSpec((N, D), lambda i: (0, 0)),
                  row, row, pl.BlockSpec((1, D), lambda i: (0, 0)), ANY],
        out_specs=[row, pl.BlockSpec((8, D), lambda i: (0, 0))],
        out_shape=[_sds((L, D), F32), _sds((8, D), F32)],
        compiler_params=_params(("arbitrary",), 56),
    )(dproj, w_int, x, dx1, g_mix, after)


def _mm_tn(name, a, b, a_spec, b_spec, o_block, n_out, n_k):
    def body(a_ref, b_ref, o_ref):
        part = _tn(a_ref[...], b_ref[...])

        @pl.when(pl.program_id(1) == 0)
        def _():
            o_ref[...] = part

        @pl.when(pl.program_id(1) > 0)
        def _():
            o_ref[...] += part

    return _pallas_call(
        body, name=name, grid=(n_out, n_k),
        in_specs=[a_spec, b_spec],
        out_specs=pl.BlockSpec((None,) + o_block, lambda j, k: (j, 0, 0)),
        out_shape=_sds((n_out,) + o_block, F32),
        compiler_params=_params(("parallel", "arbitrary"), 56),
    )(a, b)


TK_TOKENS = 2048


def _dw_cols(name, a, b, n_cols):
    L, M = a.shape
    tk = min(L, TK_TOKENS)
    return _mm_tn(name, a, b, pl.BlockSpec((tk, M), lambda j, k: (k, 0)),
                  pl.BlockSpec((tk, n_cols), lambda j, k: (k, j)), (M, n_cols), N_CHIPS, L // tk)


def _dw_rows(name, a, b):
    L, M = a.shape
    N = b.shape[1]
    tk = min(L, TK_TOKENS)
    return _mm_tn(name, a, b, pl.BlockSpec((tk, M // N_CHIPS), lambda j, k: (k, j)),
                  pl.BlockSpec((tk, N), lambda j, k: (k, 0)), (M // N_CHIPS, N), N_CHIPS, L // tk)


def _dw_rows2(name, a, b):
    L, M = a.shape
    N = b.shape[1]
    tk = min(L, TK_TOKENS)
    return _mm_tn(name, a, b, pl.BlockSpec((tk, M // 2), lambda j, k: (k, j)),
                  pl.BlockSpec((tk, N), lambda j, k: (k, 0)), (M // 2, N), 2, L // tk)


def _place():
    x, y, c = lax.axis_index("x"), lax.axis_index("y"), lax.axis_index("c")
    chips = [(1 - x, y), (x, 1 - y), (1 - x, 1 - y)]
    return x, y, c, 2 * x + y, chips


def _remote(src, dst, send_sem, recv_sem, device):
    return pltpu.make_async_remote_copy(src_ref=src, dst_ref=dst, send_sem=send_sem,
                                        recv_sem=recv_sem, device_id=device, device_id_type=MESH)


def _half(ref, lead, c, r2):
    return ref.at[lead, pl.ds(pl.multiple_of(c * r2, 16), r2), :]


def _cast_place(name, w, chip_idx):
    r, cols = w.shape
    tr = r // 2

    def body(k_ref, w_ref, o_ref):
        o_ref[...] = w_ref[...].astype(BF16)

    return _pallas_call(
        body, name=name,
        grid_spec=pltpu.PrefetchScalarGridSpec(
            num_scalar_prefetch=1, grid=(2,),
            in_specs=[pl.BlockSpec((tr, cols), lambda i, k_ref: (i, 0))],
            out_specs=pl.BlockSpec((None, tr, cols), lambda i, k_ref: (k_ref[0], i, 0))),
        out_shape=_sds((N_CHIPS, r, cols), BF16),
        compiler_params=_params(("parallel",), 48),
    )(chip_idx, w)


def _cast_place_t(name, w, chip_idx):
    r, cols = w.shape

    def body(k_ref, w_ref, o_ref):
        o_ref[...] = w_ref[...].T.astype(BF16)

    return _pallas_call(
        body, name=name,
        grid_spec=pltpu.PrefetchScalarGridSpec(
            num_scalar_prefetch=1, grid=(cols // LANES,),
            in_specs=[pl.BlockSpec((r, LANES), lambda i, k_ref: (0, i))],
            out_specs=pl.BlockSpec((None, LANES, r), lambda i, k_ref: (k_ref[0], i, 0))),
        out_shape=_sds((N_CHIPS, cols, r), BF16),
        compiler_params=_params(("parallel",), 48),
    )(chip_idx, w)


def _gather_copies(bufs, whole, send_sems, recv_sems, select=None):
    x, y, c, k, chips = _place()
    pairs = []
    for w, buf in enumerate(bufs):
        for j, (cx, cy) in enumerate(chips):
            if select is not None and not select(w, j):
                continue
            if w in whole:
                mine, theirs = buf.at[k], buf.at[2 * cx + cy]
            else:
                r2 = buf.shape[1] // 2
                mine, theirs = _half(buf, k, c, r2), _half(buf, 2 * cx + cy, c, r2)
            sems = (send_sems.at[w * 3 + j], recv_sems.at[w * 3 + j])
            pairs.append((_remote(mine, mine, *sems, (cx, cy, c)), _remote(theirs, theirs, *sems, (x, y, c))))
    return pairs


def _gather_start(name, groups, after):
    flat = [b for bufs, _, _ in groups for b in bufs]
    nb, ng = len(flat), len(groups)

    def body(*refs):
        ins, sems, token = refs[:nb], refs[nb + 1:nb + 1 + 2 * ng], refs[-1]
        pos = 0
        for g, (bufs, whole, select) in enumerate(groups):
            for send, _ in _gather_copies(ins[pos:pos + len(bufs)], whole, sems[2 * g], sems[2 * g + 1], select):
                send.start()
            pos += len(bufs)
        token[...] = jnp.zeros_like(token)

    sem_shapes = []
    for bufs, _, _ in groups:
        sem_shapes += [pltpu.SemaphoreType.DMA((3 * len(bufs),))] * 2
    out = _pallas_call(
        body, name=name,
        in_specs=[HBM] * nb + [ANY], out_specs=tuple([SEM] * (2 * ng) + [HBM] * nb + [VMEM]),
        out_shape=tuple(sem_shapes + [pltpu.HBM(b.shape, b.dtype) for b in flat] + [_sds((8, LANES), F32)]),
        input_output_aliases={i: 2 * ng + i for i in range(nb)},
        compiler_params=pltpu.CompilerParams(has_side_effects=EFFECT),
    )(*flat, after)
    sems, thru, pos = [], [], 2 * ng
    for g, (bufs, _, _) in enumerate(groups):
        sems.append((out[2 * g], out[2 * g + 1]))
        thru.append(list(out[pos:pos + len(bufs)]))
        pos += len(bufs)
    return sems, thru, out[-1]


def _gather_wait(name, bufs, whole, sems, after, select=None):
    nb = len(bufs)

    def body(*refs):
        ins, send_sems, recv_sems = refs[:nb], refs[nb], refs[nb + 1]
        for send, arrival in _gather_copies(ins, whole, send_sems, recv_sems, select):
            send.wait_send()
            arrival.wait_recv()

    return _pallas_call(
        body, name=name,
        in_specs=[HBM] * nb + [SEM, SEM, ANY], out_specs=[HBM] * nb,
        out_shape=[pltpu.HBM(b.shape, b.dtype) for b in bufs],
        input_output_aliases={i: i for i in range(nb)},
        compiler_params=pltpu.CompilerParams(has_side_effects=EFFECT),
    )(*bufs, sems[0], sems[1], after)


def _gather_forward(name, bufs, sources=(0, 1, 2)):
    n = len(bufs)

    def body(*refs):
        outs = refs[n:2 * n]
        send_sems, recv_sems = refs[2 * n:]
        x, y, c, _, chips = _place()
        sends = []
        for w in range(n):
            r2 = outs[w].shape[1] // 2
            for j in sources:
                landed = _half(outs[w], 2 * chips[j][0] + chips[j][1], c, r2)
                sends.append(_remote(landed, landed, send_sems.at[w * 3 + j], recv_sems.at[w * 3 + j],
                                     (x, y, 1 - c)))
        for cp in sends:
            cp.start()
        for w in range(n):
            r2 = outs[w].shape[1] // 2
            for j in sources:
                got = _half(outs[w], 2 * chips[j][0] + chips[j][1], 1 - c, r2)
                _remote(got, got, send_sems.at[w * 3 + j], recv_sems.at[w * 3 + j], (x, y, c)).wait_recv()
        for cp in sends:
            cp.wait_send()

    return _pallas_call(
        body, name=name,
        in_specs=[ANY] * n, out_specs=[ANY] * n,
        out_shape=[_sds(b.shape, b.dtype) for b in bufs],
        input_output_aliases={i: i for i in range(n)},
        scratch_shapes=[pltpu.SemaphoreType.DMA((n * 3,)), pltpu.SemaphoreType.DMA((n * 3,))],
    )(*bufs)


def _rs_sibling(name, grads):
    n = len(grads)

    def body(*refs):
        ins, outs = refs[:n], refs[n:2 * n]
        send_sems, recv_sems = refs[2 * n:]
        x, y, c, _, _ = _place()
        copies = []
        for w in range(n):
            r2 = ins[w].shape[1] // 2
            copies.append(_remote(_half(ins[w], slice(None), 1 - c, r2), outs[w],
                                  send_sems.at[w], recv_sems.at[w], (x, y, 1 - c)))
        for cp in copies:
            cp.start()
        for cp in copies:
            cp.wait()

    return _pallas_call(
        body, name=name,
        in_specs=[ANY] * n, out_specs=[ANY] * n,
        out_shape=[_sds((N_CHIPS, g.shape[1] // 2, g.shape[2]), F32) for g in grads],
        scratch_shapes=[pltpu.SemaphoreType.DMA((n,)), pltpu.SemaphoreType.DMA((n,))],
    )(*grads)


def _rs_add(name, grad3, from_sibling, c_idx):
    _, r2, cols = from_sibling.shape

    def body(c_ref, g_ref, s_ref, o_ref):
        o_ref[...] = (g_ref[...] + s_ref[...]).astype(BF16)

    return _pallas_call(
        body, name=name,
        grid_spec=pltpu.PrefetchScalarGridSpec(
            num_scalar_prefetch=1, grid=(N_CHIPS,),
            in_specs=[pl.BlockSpec((None, r2, cols), lambda k, c_ref: (k, c_ref[0], 0)),
                      pl.BlockSpec((None, r2, cols), lambda k, c_ref: (k, 0, 0))],
            out_specs=pl.BlockSpec((None, r2, cols), lambda k, c_ref: (k, 0, 0))),
        out_shape=_sds(from_sibling.shape, BF16),
        compiler_params=_params(("parallel",), 48),
    )(c_idx, grad3, from_sibling)


def _split_start(name, arrays, n_sems, pairs_fn):
    n = len(arrays)

    def body(*refs):
        for send, _ in pairs_fn(refs[:n], refs[n], refs[n + 1]):
            send.start()
        refs[-1][...] = jnp.zeros_like(refs[-1])

    out = _pallas_call(
        body, name=name,
        in_specs=[HBM] * n, out_specs=tuple([SEM, SEM] + [HBM] * n + [VMEM]),
        out_shape=tuple([pltpu.SemaphoreType.DMA((n_sems,))] * 2 + [pltpu.HBM(a.shape, a.dtype) for a in arrays]
                        + [_sds((8, LANES), F32)]),
        input_output_aliases={i: 2 + i for i in range(n)},
        compiler_params=pltpu.CompilerParams(has_side_effects=EFFECT),
    )(*arrays)
    return (out[0], out[1]), list(out[2:2 + n]), out[-1]


def _split_wait(name, sems, arrays, pairs_fn, after):
    n = len(arrays)

    def body(*refs):
        for send, arrival in pairs_fn(refs[:n], refs[n], refs[n + 1]):
            send.wait_send()
            arrival.wait_recv()

    return list(_pallas_call(
        body, name=name,
        in_specs=[HBM] * n + [SEM, SEM, ANY], out_specs=[HBM] * n,
        out_shape=[pltpu.HBM(a.shape, a.dtype) for a in arrays],
        input_output_aliases={i: i for i in range(n)},
        compiler_params=pltpu.CompilerParams(has_side_effects=EFFECT),
    )(*arrays, sems[0], sems[1], after))


def _forward_pairs(bufs, send_sems, recv_sems):
    x, y, c, _, chips = _place()
    pairs = []
    for w, buf in enumerate(bufs):
        r2 = buf.shape[1] // 2
        for j, (cx, cy) in enumerate(chips):
            landed, theirs = _half(buf, 2 * cx + cy, c, r2), _half(buf, 2 * cx + cy, 1 - c, r2)
            sems = (send_sems.at[w * 3 + j], recv_sems.at[w * 3 + j])
            pairs.append((_remote(landed, landed, *sems, (x, y, 1 - c)), _remote(theirs, theirs, *sems, (x, y, c))))
    return pairs


def _sibling_pairs(arrays, send_sems, recv_sems):
    x, y, c, _, _ = _place()
    n = len(arrays) // 2
    pairs = []
    for w in range(n):
        r2 = arrays[w].shape[1] // 2
        cp = _remote(_half(arrays[w], slice(None), 1 - c, r2), arrays[n + w], send_sems.at[w], recv_sems.at[w],
                     (x, y, 1 - c))
        pairs.append((cp, cp))
    return pairs


def _ici_pairs(arrays, send_sems, recv_sems):
    x, y, c, _, chips = _place()
    n = len(arrays) // 2
    pairs = []
    for w in range(n):
        for j, (cx, cy) in enumerate(chips):
            cp = _remote(arrays[w].at[2 * cx + cy], arrays[n + w].at[j],
                         send_sems.at[w * 3 + j], recv_sems.at[w * 3 + j], (cx, cy, c))
            pairs.append((cp, cp))
    return pairs


def _rs_sum(name, partials, received, place_idx):
    _, r2, cols = partials.shape
    nb = 2
    tr = r2 // nb

    def body(idx_ref, p_ref, r_ref, o_ref):
        o_ref[...] = ((p_ref[...].astype(F32) + r_ref[0].astype(F32))
                      + (r_ref[1].astype(F32) + r_ref[2].astype(F32)))

    return _pallas_call(
        body, name=name,
        grid_spec=pltpu.PrefetchScalarGridSpec(
            num_scalar_prefetch=1, grid=(nb,),
            in_specs=[pl.BlockSpec((None, tr, cols), lambda i, idx: (idx[0], i, 0)),
                      pl.BlockSpec((3, tr, cols), lambda i, idx: (0, i, 0))],
            out_specs=pl.BlockSpec((tr, cols), lambda i, idx: (idx[1] * nb + i, 0))),
        out_shape=_sds((2 * r2, cols), F32),
        compiler_params=_params(("parallel",), 48),
    )(place_idx, partials, received)


def _rs_share(name, shards):
    n = len(shards)

    def body(*refs):
        outs = refs[n:2 * n]
        send_sems, recv_sems = refs[2 * n:]
        x, y, c, _, _ = _place()
        sends = []
        for w in range(n):
            r2 = outs[w].shape[0] // 2
            mine = outs[w].at[pl.ds(pl.multiple_of(c * r2, 8), r2), :]
            sends.append(_remote(mine, mine, send_sems.at[w], recv_sems.at[w], (x, y, 1 - c)))
        for cp in sends:
            cp.start()
        for w in range(n):
            r2 = outs[w].shape[0] // 2
            theirs = outs[w].at[pl.ds(pl.multiple_of((1 - c) * r2, 8), r2), :]
            _remote(theirs, theirs, send_sems.at[w], recv_sems.at[w], (x, y, c)).wait_recv()
        for cp in sends:
            cp.wait_send()

    return _pallas_call(
        body, name=name,
        in_specs=[ANY] * n, out_specs=[ANY] * n,
        out_shape=[_sds(s.shape, F32) for s in shards],
        input_output_aliases={i: i for i in range(n)},
        scratch_shapes=[pltpu.SemaphoreType.DMA((n,)), pltpu.SemaphoreType.DMA((n,))],
    )(*shards)


def _small_allreduce(red_mix, red_ffn, red_final, red_hg, g_conv):
    rows = N_SMALL_ROWS
    D = red_mix.shape[1]
    H = red_hg.shape[1]

    def body(mix_ref, ffn_ref, fin_ref, hg_ref, cv_ref, sum_ref, all_ref, in_ref, send_sems, recv_sems):
        in_ref[...] = jnp.zeros_like(in_ref)
        in_ref[0:1, :] = mix_ref[0:1, :]
        in_ref[1:2, :] = ffn_ref[0:1, :]
        in_ref[2:3, :] = fin_ref[0:1, :]
        gam = hg_ref[1:2, 0:HEAD_DIM]
        for h in range(1, H // HEAD_DIM):
            gam = gam + hg_ref[1:2, h * HEAD_DIM:(h + 1) * HEAD_DIM]
        in_ref[3:4, 0:HEAD_DIM] = gam
        in_ref[3:4, HEAD_DIM:2 * HEAD_DIM] = fin_ref[1:2, 0:HEAD_DIM]
        in_ref[4:5, 0:H] = hg_ref[0:1, :]
        in_ref[6:9, 0:H] = cv_ref[...]
        x, y, c, _, _ = _place()
        me = 4 * x + 2 * y + c
        all_ref[me] = in_ref[...]
        copies = []
        for m in range(1, 8):
            mx, my, mc = (m >> 2) & 1, (m >> 1) & 1, m & 1
            px, py, pc = x ^ mx, y ^ my, c ^ mc
            copies.append((_remote(in_ref, all_ref.at[me], send_sems.at[m - 1], recv_sems.at[m - 1],
                                   (px, py, pc)), 4 * px + 2 * py + pc, m))
        for cp, _, _ in copies:
            cp.start()
        for _, peer, m in copies:
            _remote(in_ref, all_ref.at[peer], send_sems.at[m - 1], recv_sems.at[m - 1],
                    (x, y, c)).wait_recv()
        for cp, _, _ in copies:
            cp.wait_send()
        total = all_ref[0]
        for d in range(1, 8):
            total = total + all_ref[d]
        sum_ref[...] = total

    return _pallas_call(
        body, name="small_allreduce", pin=False,
        in_specs=[VMEM] * 5, out_specs=[VMEM, VMEM],
        out_shape=[_sds((rows, D), F32), _sds((8, rows, D), F32)],
        scratch_shapes=[pltpu.VMEM((rows, D), F32), pltpu.SemaphoreType.DMA((7,)),
                        pltpu.SemaphoreType.DMA((7,))],
    )(red_mix, red_ffn, red_final, red_hg, g_conv)[0]


def _adamw_math(w, g, m, v):
    m = ADAM_B1 * m + (1.0 - ADAM_B1) * g
    v = ADAM_B2 * v + (1.0 - ADAM_B2) * jnp.square(g)
    m_hat = m / (1.0 - ADAM_B1 ** ADAM_STEP)
    v_hat = v / (1.0 - ADAM_B2 ** ADAM_STEP)
    delta = -ADAM_LR * (m_hat / (jnp.sqrt(v_hat) + ADAM_EPS) + ADAM_WD * w)
    return delta, m, v


def _adamw(name, g, w, m, v):
    r, cols = g.shape
    tr = r // 4

    def body(g_ref, w_ref, m_ref, v_ref, go_ref, d_ref, mo_ref, vo_ref):
        g = g_ref[...]
        go_ref[...] = g
        d_ref[...], mo_ref[...], vo_ref[...] = _adamw_math(w_ref[...], g, m_ref[...], v_ref[...])

    blk = pl.BlockSpec((tr, cols), lambda i: (i, 0))
    return _pallas_call(
        body, name=name, grid=(r // tr,),
        in_specs=[blk] * 4, out_specs=[blk] * 4, out_shape=[_sds((r, cols), F32)] * 4,
        compiler_params=_params(("parallel",), 48),
    )(g, w, m, v)


def _small_update(total, chip_idx, ws, ms, vs):
    n = len(ws)
    H = ws[1].shape[1]

    def body(idx_ref, tot_ref, *refs):
        w, m, v, outs = refs[:n], refs[n:2 * n], refs[2 * n:3 * n], refs[3 * n:]
        chip = idx_ref[0]
        p0 = _lower_bound(w[1][...])
        dl0 = p0 * (1.0 - p0) * tot_ref[4:5, 0:H]
        conv = jnp.zeros((3, LANES), F32)
        for k in range(N_CHIPS):
            conv = jnp.where(chip == k, tot_ref[6:9, k * LANES:(k + 1) * LANES], conv)
        grads = [tot_ref[0:1, :], None, tot_ref[3:4, 0:HEAD_DIM], conv, tot_ref[1:2, :], tot_ref[2:3, :]]
        for p in range(n):
            g_ref, d_ref, mo_ref, vo_ref = outs[4 * p:4 * p + 4]
            if p == 1:
                for row, g in ((slice(0, 1), dl0), (slice(1, 2), -dl0)):
                    g_ref[row, :] = g
                    d_ref[row, :], mo_ref[row, :], vo_ref[row, :] = _adamw_math(
                        w[p][row, :], g, m[p][row, :], v[p][row, :])
            else:
                g_ref[...] = grads[p]
                d_ref[...], mo_ref[...], vo_ref[...] = _adamw_math(w[p][...], grads[p], m[p][...], v[p][...])
        outs[4 * n][...] = tot_ref[3:4, HEAD_DIM:2 * HEAD_DIM]

    full = lambda a: pl.BlockSpec(a.shape, lambda i, idx: (0,) * a.ndim)
    out_shape = [_sds(w.shape, F32) for w in ws for _ in range(4)] + [_sds((1, LANES), F32)]
    return _pallas_call(
        body, name="small_update",
        grid_spec=pltpu.PrefetchScalarGridSpec(
            num_scalar_prefetch=1, grid=(1,),
            in_specs=[full(total)] + [full(a) for a in ws + ms + vs],
            out_specs=[full(s) for s in out_shape]),
        out_shape=out_shape,
    )(chip_idx, total, *ws, *ms, *vs)


def kernel(x, norm_mix_g, w_in, lower_bounds, hg_norm_g, conv_w, w_branch_a, w_branch_b, w_out, norm_ffn_g, w_ffn_gate, w_ffn_up, w_ffn_down, norm_final_g, loss_target, m_norm_mix_g, m_w_in, m_lower_bounds, m_hg_norm_g, m_conv_w, m_w_branch_a, m_w_branch_b, m_w_out, m_norm_ffn_g, m_w_ffn_gate, m_w_ffn_up, m_w_ffn_down, m_norm_final_g, v_norm_mix_g, v_w_in, v_lower_bounds, v_hg_norm_g, v_conv_w, v_w_branch_a, v_w_branch_b, v_w_out, v_norm_ffn_g, v_w_ffn_gate, v_w_ffn_up, v_w_ffn_down, v_norm_final_g):
    _, L, D = x.shape
    H = D // 2
    assert lower_bounds.shape == (2, H) and hg_norm_g.shape == (1, HEAD_DIM)
    assert conv_w.shape == (1, 3, LANES) and w_in.shape[2] * N_CHIPS == 11 * H
    x2d, target = x.reshape(L, D), loss_target.reshape(L, D)
    g_final = norm_final_g.reshape(1, D)
    chip = 2 * lax.axis_index("x") + lax.axis_index("y")
    core = lax.axis_index("c")

    tr = lambda w: jnp.transpose(w[0])
    big = [w_in[0], w_branch_a[0], w_branch_b[0], w_out[0], tr(w_ffn_gate), tr(w_ffn_up), w_ffn_down[0]]
    big_m = [m_w_in[0], m_w_branch_a[0], m_w_branch_b[0], m_w_out[0], tr(m_w_ffn_gate), tr(m_w_ffn_up),
             m_w_ffn_down[0]]
    big_v = [v_w_in[0], v_w_branch_a[0], v_w_branch_b[0], v_w_out[0], tr(v_w_ffn_gate), tr(v_w_ffn_up),
             v_w_ffn_down[0]]
    names = ["w_in", "w_branch_a", "w_branch_b", "w_out", "w_ffn_gate", "w_ffn_up", "w_ffn_down"]

    chip_idx = chip.reshape(1).astype(jnp.int32)
    placed = [(_cast_place_t if j < 3 else _cast_place)("place_" + nm, w, chip_idx)
              for j, (nm, w) in enumerate(zip(names, big))]
    conv_placed = lax.dynamic_update_slice(jnp.zeros((N_CHIPS, 3, LANES), F32), conv_w, (chip, 0, 0))
    x_i, y_i = lax.axis_index("x"), lax.axis_index("y")
    blocks = lambda *ks: jnp.stack(ks).astype(jnp.int32)
    near = lambda w, j: j < 2
    far = lambda w, j: w == 1 or j == 2
    near_sems, in_flight, _ = _gather_start("gather_start_near", [([placed[0]], set(), near)], chip_idx)
    w_in_buf = in_flight[0][0]
    h, proj = _fwd_proj_first(x2d, norm_mix_g, w_in_buf, blocks(chip))
    sems, in_flight, _ = _gather_start(
        "gather_start_rest", [([w_in_buf, conv_placed], {1}, far), (placed[1:4], set(), None),
                              (placed[4:], set(), None)], h)
    w_in_buf, conv_buf = in_flight[0]
    (w_in_buf,) = _gather_wait("gather_wait_in_near", [w_in_buf], set(), near_sems[0], h, near)
    (w_in_buf,) = _gather_forward("gather_fwd_in_near", [w_in_buf], (0, 1))
    proj = _fwd_proj_more("fwd_proj_near", h, w_in_buf, proj,
                          blocks(2 * (1 - x_i) + y_i, 2 * x_i + (1 - y_i)))
    w_in_buf, conv_all = _gather_wait("gather_wait_in_far", [w_in_buf, conv_buf], {1}, sems[0], proj, far)
    (w_int3,) = _gather_forward("gather_fwd_in_far", [w_in_buf], (2,))
    proj = _fwd_proj_more("fwd_proj_far", h, w_int3, proj, blocks(2 * (1 - x_i) + (1 - y_i)))
    w_int = w_int3.reshape(-1, D)
    conv_full = jnp.transpose(conv_all, (1, 0, 2)).reshape(3, H)
    og, o_pre, s_saved = _hgrn_fwd(proj, lower_bounds, hg_norm_g, H)
    landed = _gather_wait("gather_wait_mix", in_flight[1], set(), sems[1], og)
    fwd_sems, landed, token = _split_start("gather_fwd_mix_start", landed, 9, _forward_pairs)
    cb = _conv_fwd(proj, conv_full, H, token)
    wat3, wbt3, wout3 = _split_wait("gather_fwd_mix_wait", fwd_sems, landed, _forward_pairs, cb)
    wat, wbt, wout = wat3.reshape(D, H), wbt3.reshape(D, H), wout3.reshape(D, D)
    landed = _gather_wait("gather_wait_ffn", in_flight[2], set(), sems[2], cb)
    fwd_sems, landed, token = _split_start("gather_fwd_ffn_start", landed, 9, _forward_pairs)
    sig_a, sig_b, dm_dga, dm_dgb, merged, x1, h2 = _fwd_mix(og, cb, proj, x2d, wat, wbt, wout, norm_ffn_g,
                                                              H, token)
    wgt3, wut3, wd3 = _split_wait("gather_fwd_ffn_wait", fwd_sems, landed, _forward_pairs, h2)
    d_ff = N_CHIPS * wd3.shape[1]
    wgt, wut, wd = wgt3.reshape(d_ff, D), wut3.reshape(d_ff, D), wd3.reshape(d_ff, D)
    ffn_ds_da, ffn_ds_db, ffn_s = _fwd_ffn_up(h2, wgt, wut)
    dx2, dx2b, red_final = _fwd_down_loss(ffn_s, wd, x1, target, g_final)

    c_idx = core.reshape(1).astype(jnp.int32)
    place_idx = jnp.stack([chip, core]).astype(jnp.int32)

    def sibling_start(tag, grads):
        bufs = [lax.empty((N_CHIPS, g.shape[1] // 2, g.shape[2]), F32) for g in grads]
        return _split_start("rs_sibling_start_" + tag, list(grads) + bufs, len(grads), _sibling_pairs)

    def ici_start(tag, js, grads, from_sibling):
        partials = [_rs_add("rs_add_" + names[j], g, s, c_idx) for j, g, s in zip(js, grads, from_sibling)]
        landings = [lax.empty((3,) + p.shape[1:], BF16) for p in partials]
        return _split_start("rs_ici_start_" + tag, partials + landings, 3 * len(js), _ici_pairs)

    def ici_start_behind(tag, js, started, after):
        n = len(js)
        arrays = _split_wait("rs_sibling_wait_" + tag, started[0], started[1], _sibling_pairs, after)
        return ici_start(tag, js, arrays[:n], arrays[n:])

    def rs_end(tag, js, started, after):
        n = len(js)
        arrays = _split_wait("rs_ici_wait_" + tag, started[0], started[1], _ici_pairs, after)
        halves = [_rs_sum("rs_sum_" + names[j], p, r, place_idx)
                  for j, p, r in zip(js, arrays[:n], arrays[n:])]
        grads = _rs_share("rs_share_" + tag, halves)
        return [_adamw("adamw_" + names[j], g, big[j], big_m[j], big_v[j]) for j, g in zip(js, grads)]

    shards3 = lambda g: g.reshape(N_CHIPS, d_ff // N_CHIPS, D)
    da, db = _bwd_down(dx2b, wd, ffn_ds_da, ffn_ds_db)
    g_wd = shards3(_dw_rows2("dw_ffn_down", ffn_s, dx2b))
    g_wg = shards3(_dw_rows2("dw_ffn_gate", da, h2))
    g_wu = shards3(_dw_rows2("dw_ffn_up", db, h2))
    ffn_sibling = sibling_start("ffn", [g_wg, g_wu, g_wd])
    dx1, dx1b, red_ffn = _bwd_ffn_dh(da, db, wgt, wut, x1, dx2, norm_ffn_g, ffn_sibling[2])
    ffn_ici = ici_start_behind("ffn", [4, 5, 6], ffn_sibling, dx1b)
    dya, dyb, dga, dgb, d_o, d_cb = _bwd_mix(dx1b, sig_a, sig_b, dm_dga, dm_dgb, wat, wbt, wout, H,
                                             ffn_ici[2])
    g_wout = _dw_rows("dw_out", merged, dx1b)
    g_wa = _dw_cols("dw_branch_a", og, dya, D // N_CHIPS)
    g_wb = _dw_cols("dw_branch_b", cb, dyb, D // N_CHIPS)
    mix_sibling = sibling_start("mix", [g_wa, g_wb, g_wout])
    dq, df, dv, dg, red_hg = _hgrn_bwd(proj, lower_bounds, hg_norm_g, o_pre, d_o, s_saved, H, mix_sibling[2])
    mix_ici = ici_start_behind("mix", [1, 2, 3], mix_sibling, dq)
    dcg, dbg, dxb, g_conv = _conv_bwd(proj, conv_full, d_cb, H, mix_ici[2])
    dproj = jnp.concatenate([dq, df, dv, dg, dcg, dbg, dxb, dga, dgb], axis=1)
    g_win = _dw_cols("dw_in", h, dproj, w_int3.shape[1])
    in_sibling = sibling_start("in", [g_win])
    big_out = [None] + rs_end("mix", [1, 2, 3], mix_ici, in_sibling[2]) + rs_end(
        "ffn", [4, 5, 6], ffn_ici, in_sibling[2])
    in_ici = ici_start_behind("in", [0], in_sibling, big_out[6][0])
    grad_x, red_mix = _bwd_in(dproj, w_int, x2d, dx1, norm_mix_g, in_ici[2])
    big_out[0] = rs_end("in", [0], in_ici, grad_x)[0]

    total = _small_allreduce(red_mix, red_ffn, red_final, red_hg, g_conv)

    def smalls(mix, lb, hg, cw, ffn, fin):
        return [mix, lb, hg, cw[0], ffn, fin.reshape(1, D)]

    small_out = _small_update(
        total, chip_idx,
        smalls(norm_mix_g, lower_bounds, hg_norm_g, conv_w, norm_ffn_g, norm_final_g),
        smalls(m_norm_mix_g, m_lower_bounds, m_hg_norm_g, m_conv_w, m_norm_ffn_g, m_norm_final_g),
        smalls(v_norm_mix_g, v_lower_bounds, v_hg_norm_g, v_conv_w, v_norm_ffn_g, v_norm_final_g))

    def outputs(i):
        big_i = [big_out[j][i] for j in range(7)]
        mix, lb, hg, cw, ffn, fin = [small_out[4 * p + i] for p in range(6)]
        return [mix, big_i[0][None], lb, hg, cw[None], big_i[1][None], big_i[2][None], big_i[3][None], ffn,
                big_i[4].T[None], big_i[5].T[None], big_i[6][None], fin.reshape(D)]

    outs = [small_out[24][0, 0], grad_x.reshape(1, L, D)]
    for i in range(4):
        outs += outputs(i)
    return tuple(outs)
```

```python
import functools

import jax
import jax.numpy as jnp
from jax import lax
from jax.experimental import pallas as pl
from jax.experimental.pallas import tpu as pltpu

F32 = jnp.float32
BF16 = jnp.bfloat16
EPS = 1e-6
CHUNK = 32
HEAD_DIM = 128
LANES = 128
N_CHIPS = 4
N_SMALL_ROWS = 16

ADAM_LR = 0.001
ADAM_B1 = 0.9
ADAM_B2 = 0.999
ADAM_EPS = 1e-08
ADAM_WD = 0.01
ADAM_STEP = 10

MESH = pl.DeviceIdType.MESH
ANY = pl.BlockSpec(memory_space=pl.ANY)
VMEM = pl.BlockSpec(memory_space=pltpu.VMEM)
HBM = pl.BlockSpec(memory_space=pltpu.HBM)
SEM = pl.BlockSpec(memory_space=pltpu.SEMAPHORE)
EFFECT = pltpu.SideEffectType.DATAFLOW_SIDE_EFFECTING


def _sds(shape, dtype):
    return jax.ShapeDtypeStruct(shape, dtype)


def _pallas_call(body, pin=True, **kwargs):
    if not pin:
        return pl.pallas_call(body, **kwargs)
    in_hbm = lambda s: pltpu.HBM(s.shape, s.dtype) if isinstance(s, jax.ShapeDtypeStruct) else s
    kwargs["out_shape"] = jax.tree.map(in_hbm, kwargs["out_shape"])
    call = pl.pallas_call(body, **kwargs)

    def run(*args):
        return call(*[pltpu.with_memory_space_constraint(a, pltpu.HBM) if a.dtype in (F32, BF16) else a
                      for a in args])

    return run


def _params(semantics, vmem_mb):
    return pltpu.CompilerParams(dimension_semantics=semantics, vmem_limit_bytes=vmem_mb << 20)


def _nn(a, b):
    return lax.dot_general(a, b, (((1,), (0,)), ((), ())), preferred_element_type=F32)


def _nt(a, b):
    return lax.dot_general(a, b, (((1,), (1,)), ((), ())), preferred_element_type=F32)


def _tn(a, b):
    return lax.dot_general(a, b, (((0,), (0,)), ((), ())), preferred_element_type=F32)


def _sigmoid(x):
    return jax.nn.sigmoid(x)


def _rms_stats(x):
    r = lax.rsqrt(jnp.mean(x * x, axis=-1, keepdims=True) + EPS)
    return r, x * r


def _rms_bwd(dxh, xh, r):
    return r * (dxh - xh * jnp.mean(dxh * xh, axis=-1, keepdims=True))


def _fwd_proj_first(x, g_mix, w_int3, block):
    L, D = x.shape
    tn = w_int3.shape[1]
    tm = min(L, 1024)

    def body(blk_ref, x_ref, g_ref, w_ref, h_ref, p_ref):
        _, xh = _rms_stats(x_ref[...])
        h = (xh * g_ref[...]).astype(BF16)
        h_ref[...] = h
        p_ref[...] = _nt(h, w_ref[...])

    return _pallas_call(
        body, name="fwd_proj_own",
        grid_spec=pltpu.PrefetchScalarGridSpec(
            num_scalar_prefetch=1, grid=(L // tm,),
            in_specs=[pl.BlockSpec((tm, D), lambda i, blk: (i, 0)),
                      pl.BlockSpec((1, D), lambda i, blk: (0, 0)),
                      pl.BlockSpec((None, tn, D), lambda i, blk: (blk[0], 0, 0))],
            out_specs=[pl.BlockSpec((tm, D), lambda i, blk: (i, 0)),
                       pl.BlockSpec((tm, tn), lambda i, blk: (i, blk[0]))]),
        out_shape=[_sds((L, D), BF16), _sds((L, N_CHIPS * tn), F32)],
        compiler_params=_params(("parallel",), 48),
    )(block, x, g_mix, w_int3)


def _fwd_proj_more(name, h, w_int3, proj, blocks):
    L, D = h.shape
    tn = w_int3.shape[1]
    tm = min(L, 1024)

    def body(blk_ref, h_ref, w_ref, proj_ref, p_ref):
        p_ref[...] = _nt(h_ref[...], w_ref[...])

    return _pallas_call(
        body, name=name,
        grid_spec=pltpu.PrefetchScalarGridSpec(
            num_scalar_prefetch=1, grid=(L // tm, blocks.shape[0]),
            in_specs=[pl.BlockSpec((tm, D), lambda i, j, blk: (i, 0)),
                      pl.BlockSpec((None, tn, D), lambda i, j, blk: (blk[j], 0, 0)), ANY],
            out_specs=pl.BlockSpec((tm, tn), lambda i, j, blk: (i, blk[j]))),
        out_shape=_sds(proj.shape, proj.dtype),
        input_output_aliases={3: 0},
        compiler_params=_params(("parallel", "arbitrary"), 48),
    )(blocks, h, w_int3, proj)


def _lower_bound(lbp):
    l0, l1 = lbp[0:1, :], lbp[1:2, :]
    m = jnp.maximum(l0, l1)
    e0, e1 = jnp.exp(l0 - m), jnp.exp(l1 - m)
    return e0 / (e0 + e1)


def _seg_scan(x, r32, forward):
    n = x.shape[0]
    s = 1
    while s < CHUNK:
        if forward:
            x = x + jnp.where(r32 >= s, pltpu.roll(x, s, 0), 0.0)
        else:
            x = x + jnp.where(r32 < CHUNK - s, pltpu.roll(x, n - s, 0), 0.0)
        s *= 2
    return x


def _bcast_row(x, row):
    n, w = x.shape
    nc = n // CHUNK
    x3 = x.reshape(nc, CHUNK, w)
    return jnp.broadcast_to(x3[:, row:row + 1, :], (nc, CHUNK, w)).reshape(n, w)


def _hgrn_prep(q_raw, f_raw, lb):
    r32 = lax.broadcasted_iota(jnp.int32, f_raw.shape, 0) & (CHUNK - 1)
    sig = _sigmoid(f_raw)
    f = lb + (1.0 - lb) * sig
    b = _seg_scan(jnp.log(f), r32, True)
    a = _bcast_row(b, CHUNK // 2 - 1)
    bl = _bcast_row(b, CHUNK - 1)
    sq = _sigmoid(q_raw)
    q = q_raw * sq * (HEAD_DIM ** -0.5)
    return dict(r32=r32, sig=sig, f=f, k=1.0 - f, b=b, a=a, bl=bl, sq=sq, q=q)


def _chunk_masks(n):
    ri = lax.broadcasted_iota(jnp.int32, (n, n), 0)
    ci = lax.broadcasted_iota(jnp.int32, (n, n), 1)
    same = (ri // CHUNK) == (ci // CHUNK)
    return same & (ci <= ri), same & (ri <= ci)


def _hgrn_fwd(proj, lower_bounds, gamma, H):
    L = proj.shape[0]
    nh = H // HEAD_DIM
    TL = min(L, 256)
    nc = TL // CHUNK

    def body(q_ref, f_ref, v_ref, g_ref, lbp_ref, gam_ref, og_ref, o_ref, s_ref, st_ref):
        @pl.when(pl.program_id(0) == 0)
        def _():
            st_ref[...] = jnp.zeros_like(st_ref)

        lb = _lower_bound(lbp_ref[...])
        gam = gam_ref[...]
        mask, _ = _chunk_masks(TL)
        rowc = lax.broadcasted_iota(jnp.int32, (TL, HEAD_DIM), 0) // CHUNK
        for h in range(nh):
            hs = slice(h * HEAD_DIM, (h + 1) * HEAD_DIM)
            p = _hgrn_prep(q_ref[:, hs], f_ref[:, hs], lb[:, hs])
            v = v_ref[:, hs]
            vb = v.astype(BF16)
            vt = v.T.astype(BF16)
            q_hat = (p["q"] * jnp.exp(p["b"] - p["a"])).astype(BF16)
            k_hat = (p["k"] * jnp.exp(p["a"] - p["b"])).astype(BF16)
            q_in = (p["q"] * jnp.exp(p["b"])).astype(BF16)
            k_out = (p["k"] * jnp.exp(p["bl"] - p["b"])).astype(BF16)
            dec = jnp.exp(p["bl"])
            att = jnp.where(mask, _nt(q_hat, k_hat), 0.0).astype(BF16)
            o_intra = _nn(att, vb)
            st = st_ref[h]
            for c in range(nc):
                rs = slice(c * CHUNK, (c + 1) * CHUNK)
                stb = st.astype(BF16)
                s_ref[c, h] = stb
                o_ref[rs, hs] = o_intra[rs] + _nt(q_in[rs], stb)
                k_c = jnp.where(rowc == c, k_out, jnp.zeros_like(k_out))
                st = st * dec[c * CHUNK:c * CHUNK + 1, :] + _nn(vt, k_c)
            st_ref[h] = st
            o = o_ref[:, hs]
            _, xh = _rms_stats(o)
            gr = g_ref[:, hs]
            og_ref[:, hs] = (xh * gam * (gr * _sigmoid(gr))).astype(BF16)

    col = lambda k: pl.BlockSpec((TL, H), lambda i, k=k: (i, k))
    return _pallas_call(
        body, name="hgrn_fwd", grid=(L // TL,),
        in_specs=[col(0), col(1), col(2), col(3),
                  pl.BlockSpec(lower_bounds.shape, lambda i: (0, 0)),
                  pl.BlockSpec(gamma.shape, lambda i: (0, 0))],
        out_specs=[pl.BlockSpec((TL, H), lambda i: (i, 0)),
                   pl.BlockSpec((TL, H), lambda i: (i, 0)),
                   pl.BlockSpec((nc, nh, HEAD_DIM, HEAD_DIM), lambda i: (i, 0, 0, 0))],
        out_shape=[_sds((L, H), BF16), _sds((L, H), F32),
                   _sds((L // CHUNK, nh, HEAD_DIM, HEAD_DIM), BF16)],
        scratch_shapes=[pltpu.VMEM((nh, HEAD_DIM, HEAD_DIM), F32)],
        compiler_params=_params(("arbitrary",), 48),
    )(proj, proj, proj, proj, lower_bounds, gamma)


def _hgrn_bwd(proj, lower_bounds, gamma, o_pre, d_out, s_saved, H, after):
    L = proj.shape[0]
    nh = H // HEAD_DIM
    TL = min(L, 256)
    nc = TL // CHUNK
    nt = L // TL

    def body(q_ref, f_ref, v_ref, g_ref, lbp_ref, gam_ref, o_ref, d_ref, s_ref, after_ref,
             dq_ref, df_ref, dv_ref, dg_ref, red_ref, dst_ref, dsall_ref, tmp_ref):
        @pl.when(pl.program_id(0) == 0)
        def _():
            dst_ref[...] = jnp.zeros_like(dst_ref)
            red_ref[...] = jnp.zeros_like(red_ref)

        lb = _lower_bound(lbp_ref[...])
        gam = gam_ref[...]
        mask, mask_t = _chunk_masks(TL)
        rowc = lax.broadcasted_iota(jnp.int32, (TL, HEAD_DIM), 0) // CHUNK
        for h in range(nh):
            hs = slice(h * HEAD_DIM, (h + 1) * HEAD_DIM)
            qr, gr, lbh = q_ref[:, hs], g_ref[:, hs], lb[:, hs]
            p = _hgrn_prep(qr, f_ref[:, hs], lbh)
            vb = v_ref[:, hs].astype(BF16)
            eba, eab = jnp.exp(p["b"] - p["a"]), jnp.exp(p["a"] - p["b"])
            eb, elb = jnp.exp(p["b"]), jnp.exp(p["bl"] - p["b"])
            dec = jnp.exp(p["bl"])
            q_hat, k_hat = p["q"] * eba, p["k"] * eab
            q_in, k_out = p["q"] * eb, p["k"] * elb
            q_hat_b, k_hat_b = q_hat.astype(BF16), k_hat.astype(BF16)
            q_in_b, k_out_b = q_in.astype(BF16), k_out.astype(BF16)

            o, dout = o_ref[:, hs], d_ref[:, hs]
            sg = _sigmoid(gr)
            r, xh = _rms_stats(o)
            dg_ref[:, hs] = (dout * (xh * gam) * (sg * (1.0 + gr * (1.0 - sg)))).astype(BF16)
            dn = dout * (gr * sg)
            red_ref[1:2, hs] += jnp.sum(dn * xh, axis=0, keepdims=True)
            do = _rms_bwd(dn * gam, xh, r)
            dob = do.astype(BF16)
            dot_b = do.T.astype(BF16)

            att_t = jnp.where(mask_t, _nt(k_hat_b, q_hat_b), 0.0).astype(BF16)
            dv_intra = _nn(att_t, dob)
            datt = jnp.where(mask, _nt(dob, vb), 0.0).astype(BF16)
            dqh = _nn(datt, k_hat_b)
            datt_t = jnp.where(mask_t, _nt(vb, dob), 0.0).astype(BF16)
            dkh = _nn(datt_t, q_hat_b)

            dst = dst_ref[h]
            for c in reversed(range(nc)):
                dsall_ref[c] = dst
                q_c = jnp.where(rowc == c, q_in_b, jnp.zeros_like(q_in_b))
                dst = dst * dec[c * CHUNK:c * CHUNK + 1, :] + _nn(dot_b, q_c)
            dst_ref[h] = dst
            for c in range(nc):
                rs = slice(c * CHUNK, (c + 1) * CHUNK)
                ds_c = dsall_ref[c]
                dsb = ds_c.astype(BF16)
                st_prev = s_ref[c, h]
                tmp_ref[0, rs, :] = _nt(k_out_b[rs], dsb)
                tmp_ref[1, rs, :] = _nn(vb[rs], dsb)
                tmp_ref[2, rs, :] = _nn(dob[rs], st_prev)
                ddec = jnp.sum(ds_c * st_prev.astype(F32), axis=0, keepdims=True)
                tmp_ref[3, rs, :] = jnp.broadcast_to(ddec * dec[c * CHUNK:c * CHUNK + 1, :],
                                                     (CHUNK, HEAD_DIM))
            dko, dqi = tmp_ref[1], tmp_ref[2]
            dq = dqh * eba + dqi * eb
            dk = dkh * eab + dko * elb
            tko = dko * k_out
            db = dqh * q_hat - dkh * k_hat + dqi * q_in - tko
            dlog = (_seg_scan(db, p["r32"], False)
                    + _bcast_row(_seg_scan(tko, p["r32"], True), CHUNK - 1) + tmp_ref[3])
            df = dlog / p["f"] - dk
            sig = p["sig"]
            red_ref[0:1, hs] += jnp.sum(df * (1.0 - sig), axis=0, keepdims=True)
            df_ref[:, hs] = (df * (1.0 - lbh) * sig * (1.0 - sig)).astype(BF16)
            sq = p["sq"]
            dq_ref[:, hs] = (dq * (HEAD_DIM ** -0.5) * (sq * (1.0 + qr * (1.0 - sq)))).astype(BF16)
            dv_ref[:, hs] = (dv_intra + tmp_ref[0]).astype(BF16)

    col = lambda k: pl.BlockSpec((TL, H), lambda i, k=k: (nt - 1 - i, k))
    rev = pl.BlockSpec((TL, H), lambda i: (nt - 1 - i, 0))
    return _pallas_call(
        body, name="hgrn_bwd", grid=(nt,),
        in_specs=[col(0), col(1), col(2), col(3),
                  pl.BlockSpec(lower_bounds.shape, lambda i: (0, 0)),
                  pl.BlockSpec(gamma.shape, lambda i: (0, 0)),
                  rev, rev,
                  pl.BlockSpec((nc, nh, HEAD_DIM, HEAD_DIM), lambda i: (nt - 1 - i, 0, 0, 0)), ANY],
        out_specs=[rev, rev, rev, rev, pl.BlockSpec((8, H), lambda i: (0, 0))],
        out_shape=[_sds((L, H), BF16)] * 4 + [_sds((8, H), F32)],
        scratch_shapes=[pltpu.VMEM((nh, HEAD_DIM, HEAD_DIM), F32),
                        pltpu.VMEM((nc, HEAD_DIM, HEAD_DIM), F32),
                        pltpu.VMEM((4, TL, HEAD_DIM), F32)],
        compiler_params=_params(("arbitrary",), 48),
    )(proj, proj, proj, proj, lower_bounds, gamma, o_pre, d_out, s_saved, after)


def _shift_down(u, s, row):
    return jnp.where(row >= s, pltpu.roll(u, s, 0), 0.0)


def _shift_up(u, s, row):
    n = u.shape[0]
    return jnp.where(row < n - s, pltpu.roll(u, n - s, 0), 0.0)


def _conv_specs(L, H):
    per = H // LANES
    return [pl.BlockSpec((L, LANES), lambda j, o=o: (0, o * per + j)) for o in (4, 5, 6)]


def _conv_fwd(proj, conv_w, H, after):
    L = proj.shape[0]

    def body(c_ref, b_ref, x_ref, w_ref, after_ref, o_ref):
        row = lax.broadcasted_iota(jnp.int32, (L, LANES), 0)
        u = c_ref[...] * x_ref[...]
        w = w_ref[...]
        y = w[0:1] * _shift_down(u, 2, row) + w[1:2] * _shift_down(u, 1, row) + w[2:3] * u
        o_ref[...] = (b_ref[...] * y).astype(BF16)

    return _pallas_call(
        body, name="conv_fwd", grid=(H // LANES,),
        in_specs=_conv_specs(L, H) + [pl.BlockSpec((3, LANES), lambda j: (0, j)), ANY],
        out_specs=pl.BlockSpec((L, LANES), lambda j: (0, j)),
        out_shape=_sds((L, H), BF16),
        compiler_params=_params(("parallel",), 48),
    )(proj, proj, proj, conv_w, after)


def _conv_bwd(proj, conv_w, dcb, H, after):
    L = proj.shape[0]

    def body(c_ref, b_ref, x_ref, w_ref, d_ref, after_ref, dc_ref, db_ref, dx_ref, dw_ref):
        row = lax.broadcasted_iota(jnp.int32, (L, LANES), 0)
        cg, xb = c_ref[...], x_ref[...]
        u = cg * xb
        u1, u2 = _shift_down(u, 1, row), _shift_down(u, 2, row)
        w = w_ref[...]
        y = w[0:1] * u2 + w[1:2] * u1 + w[2:3] * u
        d = d_ref[...]
        db_ref[...] = (d * y).astype(BF16)
        dy = d * b_ref[...]
        du = w[2:3] * dy + w[1:2] * _shift_up(dy, 1, row) + w[0:1] * _shift_up(dy, 2, row)
        dw_ref[0:1, :] = jnp.sum(dy * u2, axis=0, keepdims=True)
        dw_ref[1:2, :] = jnp.sum(dy * u1, axis=0, keepdims=True)
        dw_ref[2:3, :] = jnp.sum(dy * u, axis=0, keepdims=True)
        dc_ref[...] = (du * xb).astype(BF16)
        dx_ref[...] = (du * cg).astype(BF16)

    blk = pl.BlockSpec((L, LANES), lambda j: (0, j))
    return _pallas_call(
        body, name="conv_bwd", grid=(H // LANES,),
        in_specs=_conv_specs(L, H) + [pl.BlockSpec((3, LANES), lambda j: (0, j)), blk, ANY],
        out_specs=[blk, blk, blk, pl.BlockSpec((3, LANES), lambda j: (0, j))],
        out_shape=[_sds((L, H), BF16)] * 3 + [_sds((3, H), F32)],
        compiler_params=_params(("parallel",), 56),
    )(proj, proj, proj, conv_w, dcb, after)


def _gate_specs(tm, H):
    return [pl.BlockSpec((tm, H), lambda i, k=k: (i, k)) for k in (7, 8, 9, 10)]


def _fwd_mix(og, cb, proj, x, wat, wbt, wout, g_ffn, H, after):
    L, D = x.shape
    tm = min(L, 512)

    def body(o_ref, cb_ref, ga0, ga1, gb0, gb1, x_ref, wa_ref, wb_ref, wo_ref, g_ref, after_ref,
             sa_ref, sb_ref, ta_ref, tb_ref, m_ref, x1_ref, h2_ref):
        ya, yb = _nt(o_ref[...], wa_ref[...]), _nt(cb_ref[...], wb_ref[...])
        for k, (gar, gbr) in enumerate(((ga0, gb0), (ga1, gb1))):
            cs = slice(k * H, (k + 1) * H)
            sa, sb = _sigmoid(gar[...]), _sigmoid(gbr[...])
            ma, mb = sa * ya[:, cs], sb * yb[:, cs]
            m_ref[:, cs] = (ma + mb).astype(BF16)
            sa_ref[:, cs] = sa.astype(BF16)
            sb_ref[:, cs] = sb.astype(BF16)
            ta_ref[:, cs] = (ma * (1.0 - sa)).astype(BF16)
            tb_ref[:, cs] = (mb * (1.0 - sb)).astype(BF16)
        x1 = x_ref[...] + _nn(m_ref[...], wo_ref[...])
        x1_ref[...] = x1
        _, xh = _rms_stats(x1)
        h2_ref[...] = (xh * g_ref[...]).astype(BF16)

    row = lambda w: pl.BlockSpec((tm, w), lambda i: (i, 0))
    full = lambda a: pl.BlockSpec(a.shape, lambda i: (0,) * a.ndim)
    return _pallas_call(
        body, name="fwd_mix", grid=(L // tm,),
        in_specs=[row(H), row(H)] + _gate_specs(tm, H) + [row(D), full(wat), full(wbt), full(wout),
                                                           full(g_ffn), ANY],
        out_specs=[row(D)] * 7,
        out_shape=[_sds((L, D), BF16)] * 5 + [_sds((L, D), F32), _sds((L, D), BF16)],
        compiler_params=_params(("parallel",), 56),
    )(og, cb, proj, proj, proj, proj, x, wat, wbt, wout, g_ffn, after)


def _bwd_mix(dx1b, sig_a, sig_b, dm_dga, dm_dgb, wat, wbt, wout, H, after):
    L, D = dx1b.shape
    tm = min(L, 512)

    def body(dx_ref, sa_ref, sb_ref, ta_ref, tb_ref, wa_ref, wb_ref, wo_ref, after_ref,
             dya_ref, dyb_ref, dga_ref, dgb_ref, do_ref, dcb_ref):
        dm = _nt(dx_ref[...], wo_ref[...])
        dga_ref[...] = (dm * ta_ref[...].astype(F32)).astype(BF16)
        dgb_ref[...] = (dm * tb_ref[...].astype(F32)).astype(BF16)
        dya_ref[...] = (dm * sa_ref[...].astype(F32)).astype(BF16)
        dyb_ref[...] = (dm * sb_ref[...].astype(F32)).astype(BF16)
        do_ref[...] = _nn(dya_ref[...], wa_ref[...])
        dcb_ref[...] = _nn(dyb_ref[...], wb_ref[...])

    row = lambda w: pl.BlockSpec((tm, w), lambda i: (i, 0))
    full = lambda a: pl.BlockSpec(a.shape, lambda i: (0,) * a.ndim)
    return _pallas_call(
        body, name="bwd_mix", grid=(L // tm,),
        in_specs=[row(D)] * 5 + [full(wat), full(wbt), full(wout), ANY],
        out_specs=[row(D)] * 4 + [row(H)] * 2,
        out_shape=[_sds((L, D), BF16)] * 4 + [_sds((L, H), F32)] * 2,
        compiler_params=_params(("parallel",), 56),
    )(dx1b, sig_a, sig_b, dm_dga, dm_dgb, wat, wbt, wout, after)


def _fwd_ffn_up(h2, wgt, wut):
    L, D = h2.shape
    F = wgt.shape[0]
    tn = F // 2
    tm = min(L, 512)

    def body(h_ref, wg_ref, wu_ref, sa_ref, sb_ref, s_ref):
        h = h_ref[...]
        a, b = _nt(h, wg_ref[...]), _nt(h, wu_ref[...])
        sg = _sigmoid(a)
        silu = a * sg
        sa_ref[...] = (b * sg * (1.0 + a * (1.0 - sg))).astype(BF16)
        sb_ref[...] = silu.astype(BF16)
        s_ref[...] = (silu * b).astype(BF16)

    wspec = pl.BlockSpec((tn, D), lambda j, i: (j, 0))
    ospec = pl.BlockSpec((tm, tn), lambda j, i: (i, j))
    return _pallas_call(
        body, name="fwd_ffn_up", grid=(2, L // tm),
        in_specs=[pl.BlockSpec((tm, D), lambda j, i: (i, 0)), wspec, wspec],
        out_specs=[ospec] * 3,
        out_shape=[_sds((L, F), BF16)] * 3,
        compiler_params=_params(("parallel", "parallel"), 48),
    )(h2, wgt, wut)


def _fwd_down_loss(s, wd, x1, target, g_final):
    L, D = x1.shape
    F = wd.shape[0]
    tm = min(L, 256)

    def body(s_ref, wd_ref, x1_ref, t_ref, g_ref, dx_ref, dxb_ref, red_ref):
        @pl.when(pl.program_id(0) == 0)
        def _():
            red_ref[...] = jnp.zeros_like(red_ref)

        g = g_ref[...]
        r, xh = _rms_stats(x1_ref[...] + _nn(s_ref[...], wd_ref[...]))
        e = xh * g - t_ref[...]
        dy = e * (1.0 / D)
        dx = _rms_bwd(dy * g, xh, r)
        dx_ref[...] = dx
        dxb_ref[...] = dx.astype(BF16)
        red_ref[0:1, :] += jnp.sum(dy * xh, axis=0, keepdims=True)
        red_ref[1:2, :] += jnp.broadcast_to(0.5 * jnp.sum(e * e) * (1.0 / D), (1, D))

    row = pl.BlockSpec((tm, D), lambda i: (i, 0))
    return _pallas_call(
        body, name="fwd_down_loss", grid=(L // tm,),
        in_specs=[pl.BlockSpec((tm, F), lambda i: (i, 0)), pl.BlockSpec((F, D), lambda i: (0, 0)),
                  row, row, pl.BlockSpec((1, D), lambda i: (0, 0))],
        out_specs=[row, row, pl.BlockSpec((8, D), lambda i: (0, 0))],
        out_shape=[_sds((L, D), F32), _sds((L, D), BF16), _sds((8, D), F32)],
        compiler_params=_params(("arbitrary",), 48),
    )(s, wd, x1, target, g_final)


def _bwd_down(dx2b, wd, s_a, s_b):
    L, D = dx2b.shape
    F = wd.shape[0]
    tn = F // 2
    tm = min(L, 512)

    def body(dx_ref, wd_ref, sa_ref, sb_ref, da_ref, db_ref):
        ds = _nt(dx_ref[...], wd_ref[...])
        da_ref[...] = (ds * sa_ref[...].astype(F32)).astype(BF16)
        db_ref[...] = (ds * sb_ref[...].astype(F32)).astype(BF16)

    ospec = pl.BlockSpec((tm, tn), lambda j, i: (i, j))
    return _pallas_call(
        body, name="bwd_down", grid=(2, L // tm),
        in_specs=[pl.BlockSpec((tm, D), lambda j, i: (i, 0)),
                  pl.BlockSpec((tn, D), lambda j, i: (j, 0)), ospec, ospec],
        out_specs=[ospec] * 2,
        out_shape=[_sds((L, F), BF16)] * 2,
        compiler_params=_params(("parallel", "parallel"), 48),
    )(dx2b, wd, s_a, s_b)


def _bwd_ffn_dh(da, db, wgt, wut, x1, dx2, g_ffn, after):
    L, D = x1.shape
    F = wgt.shape[0]
    tm = min(L, 256)

    def body(da_ref, db_ref, wg_ref, wu_ref, x1_ref, dx2_ref, g_ref, after_ref, dx_ref, dxb_ref, red_ref):
        @pl.when(pl.program_id(0) == 0)
        def _():
            red_ref[...] = jnp.zeros_like(red_ref)

        dh = _nn(da_ref[...], wg_ref[...]) + _nn(db_ref[...], wu_ref[...])
        r, xh = _rms_stats(x1_ref[...])
        red_ref[0:1, :] += jnp.sum(dh * xh, axis=0, keepdims=True)
        dx = dx2_ref[...] + _rms_bwd(dh * g_ref[...], xh, r)
        dx_ref[...] = dx
        dxb_ref[...] = dx.astype(BF16)

    row = pl.BlockSpec((tm, D), lambda i: (i, 0))
    aspec = pl.BlockSpec((tm, F), lambda i: (i, 0))
    wspec = pl.BlockSpec((F, D), lambda i: (0, 0))
    return _pallas_call(
        body, name="bwd_ffn_dh", grid=(L // tm,),
        in_specs=[aspec, aspec, wspec, wspec, row, row, pl.BlockSpec((1, D), lambda i: (0, 0)), ANY],
        out_specs=[row, row, pl.BlockSpec((8, D), lambda i: (0, 0))],
        out_shape=[_sds((L, D), F32), _sds((L, D), BF16), _sds((8, D), F32)],
        compiler_params=_params(("arbitrary",), 56),
    )(da, db, wgt, wut, x1, dx2, g_ffn, after)


def _bwd_in(dproj, w_int, x, dx1, g_mix, after):
    L, D = x.shape
    N = w_int.shape[0]
    tm = min(L, 256)

    def body(dp_ref, w_ref, x_ref, dx1_ref, g_ref, after_ref, dx_ref, red_ref):
        @pl.when(pl.program_id(0) == 0)
        def _():
            red_ref[...] = jnp.zeros_like(red_ref)

        dh = _nn(dp_ref[...], w_ref[...])
        r, xh = _rms_stats(x_ref[...])
        red_ref[0:1, :] += jnp.sum(dh * xh, axis=0, keepdims=True)
        dx_ref[...] = dx1_ref[...] + _rms_bwd(dh * g_ref[...], xh, r)

    row = pl.BlockSpec((tm, D), lambda i: (i, 0))
    return _pallas_call(
        body, name="bwd_in", grid=(L // tm,),
        in_specs=[pl.BlockSpec((tm, N), lambda i: (i, 0)), pl.BlockSpec((N, D), lambda i: (0, 0)),
                  row, row, pl.BlockSpec((1, D), lambda i: (0, 0)), ANY],
        out_specs=[row, pl.BlockSpec((8, D), lambda i: (0, 0))],
        out_shape=[_sds((L, D), F32), _sds((8, D), F32)],
        compiler_params=_params(("arbitrary",), 56),
    )(dproj, w_int, x, dx1, g_mix, after)


def _mm_tn(name, a, b, a_spec, b_spec, o_block, n_out, n_k):
    def body(a_ref, b_ref, o_ref):
        part = _tn(a_ref[...], b_ref[...])

        @pl.when(pl.program_id(1) == 0)
        def _():
            o_ref[...] = part

        @pl.when(pl.program_id(1) > 0)
        def _():
            o_ref[...] += part

    return _pallas_call(
        body, name=name, grid=(n_out, n_k),
        in_specs=[a_spec, b_spec],
        out_specs=pl.BlockSpec((None,) + o_block, lambda j, k: (j, 0, 0)),
        out_shape=_sds((n_out,) + o_block, F32),
        compiler_params=_params(("parallel", "arbitrary"), 56),
    )(a, b)


TK_TOKENS = 2048


def _dw_cols(name, a, b, n_cols):
    L, M = a.shape
    tk = min(L, TK_TOKENS)
    return _mm_tn(name, a, b, pl.BlockSpec((tk, M), lambda j, k: (k, 0)),
                  pl.BlockSpec((tk, n_cols), lambda j, k: (k, j)), (M, n_cols), N_CHIPS, L // tk)


def _dw_rows(name, a, b):
    L, M = a.shape
    N = b.shape[1]
    tk = min(L, TK_TOKENS)
    return _mm_tn(name, a, b, pl.BlockSpec((tk, M // N_CHIPS), lambda j, k: (k, j)),
                  pl.BlockSpec((tk, N), lambda j, k: (k, 0)), (M // N_CHIPS, N), N_CHIPS, L // tk)


def _dw_rows2(name, a, b):
    L, M = a.shape
    N = b.shape[1]
    tk = min(L, TK_TOKENS)
    return _mm_tn(name, a, b, pl.BlockSpec((tk, M // 2), lambda j, k: (k, j)),
                  pl.BlockSpec((tk, N), lambda j, k: (k, 0)), (M // 2, N), 2, L // tk)


def _place():
    x, y, c = lax.axis_index("x"), lax.axis_index("y"), lax.axis_index("c")
    chips = [(1 - x, y), (x, 1 - y), (1 - x, 1 - y)]
    return x, y, c, 2 * x + y, chips


def _remote(src, dst, send_sem, recv_sem, device):
    return pltpu.make_async_remote_copy(src_ref=src, dst_ref=dst, send_sem=send_sem,
                                        recv_sem=recv_sem, device_id=device, device_id_type=MESH)


def _half(ref, lead, c, r2):
    return ref.at[lead, pl.ds(pl.multiple_of(c * r2, 16), r2), :]


def _cast_place(name, w, chip_idx):
    r, cols = w.shape
    tr = r // 2

    def body(k_ref, w_ref, o_ref):
        o_ref[...] = w_ref[...].astype(BF16)

    return _pallas_call(
        body, name=name,
        grid_spec=pltpu.PrefetchScalarGridSpec(
            num_scalar_prefetch=1, grid=(2,),
            in_specs=[pl.BlockSpec((tr, cols), lambda i, k_ref: (i, 0))],
            out_specs=pl.BlockSpec((None, tr, cols), lambda i, k_ref: (k_ref[0], i, 0))),
        out_shape=_sds((N_CHIPS, r, cols), BF16),
        compiler_params=_params(("parallel",), 48),
    )(chip_idx, w)


def _cast_place_t(name, w, chip_idx):
    r, cols = w.shape

    def body(k_ref, w_ref, o_ref):
        o_ref[...] = w_ref[...].T.astype(BF16)

    return _pallas_call(
        body, name=name,
        grid_spec=pltpu.PrefetchScalarGridSpec(
            num_scalar_prefetch=1, grid=(cols // LANES,),
            in_specs=[pl.BlockSpec((r, LANES), lambda i, k_ref: (0, i))],
            out_specs=pl.BlockSpec((None, LANES, r), lambda i, k_ref: (k_ref[0], i, 0))),
        out_shape=_sds((N_CHIPS, cols, r), BF16),
        compiler_params=_params(("parallel",), 48),
    )(chip_idx, w)


def _gather_copies(bufs, whole, send_sems, recv_sems, select=None):
    x, y, c, k, chips = _place()
    pairs = []
    for w, buf in enumerate(bufs):
        for j, (cx, cy) in enumerate(chips):
            if select is not None and not select(w, j):
                continue
            if w in whole:
                mine, theirs = buf.at[k], buf.at[2 * cx + cy]
            else:
                r2 = buf.shape[1] // 2
                mine, theirs = _half(buf, k, c, r2), _half(buf, 2 * cx + cy, c, r2)
            sems = (send_sems.at[w * 3 + j], recv_sems.at[w * 3 + j])
            pairs.append((_remote(mine, mine, *sems, (cx, cy, c)), _remote(theirs, theirs, *sems, (x, y, c))))
    return pairs


def _gather_start(name, groups, after):
    flat = [b for bufs, _, _ in groups for b in bufs]
    nb, ng = len(flat), len(groups)

    def body(*refs):
        ins, sems, token = refs[:nb], refs[nb + 1:nb + 1 + 2 * ng], refs[-1]
        pos = 0
        for g, (bufs, whole, select) in enumerate(groups):
            for send, _ in _gather_copies(ins[pos:pos + len(bufs)], whole, sems[2 * g], sems[2 * g + 1], select):
                send.start()
            pos += len(bufs)
        token[...] = jnp.zeros_like(token)

    sem_shapes = []
    for bufs, _, _ in groups:
        sem_shapes += [pltpu.SemaphoreType.DMA((3 * len(bufs),))] * 2
    out = _pallas_call(
        body, name=name,
        in_specs=[HBM] * nb + [ANY], out_specs=tuple([SEM] * (2 * ng) + [HBM] * nb + [VMEM]),
        out_shape=tuple(sem_shapes + [pltpu.HBM(b.shape, b.dtype) for b in flat] + [_sds((8, LANES), F32)]),
        input_output_aliases={i: 2 * ng + i for i in range(nb)},
        compiler_params=pltpu.CompilerParams(has_side_effects=EFFECT),
    )(*flat, after)
    sems, thru, pos = [], [], 2 * ng
    for g, (bufs, _, _) in enumerate(groups):
        sems.append((out[2 * g], out[2 * g + 1]))
        thru.append(list(out[pos:pos + len(bufs)]))
        pos += len(bufs)
    return sems, thru, out[-1]


def _gather_wait(name, bufs, whole, sems, after, select=None):
    nb = len(bufs)

    def body(*refs):
        ins, send_sems, recv_sems = refs[:nb], refs[nb], refs[nb + 1]
        for send, arrival in _gather_copies(ins, whole, send_sems, recv_sems, select):
            send.wait_send()
            arrival.wait_recv()

    return _pallas_call(
        body, name=name,
        in_specs=[HBM] * nb + [SEM, SEM, ANY], out_specs=[HBM] * nb,
        out_shape=[pltpu.HBM(b.shape, b.dtype) for b in bufs],
        input_output_aliases={i: i for i in range(nb)},
        compiler_params=pltpu.CompilerParams(has_side_effects=EFFECT),
    )(*bufs, sems[0], sems[1], after)


def _gather_forward(name, bufs, sources=(0, 1, 2)):
    n = len(bufs)

    def body(*refs):
        outs = refs[n:2 * n]
        send_sems, recv_sems = refs[2 * n:]
        x, y, c, _, chips = _place()
        sends = []
        for w in range(n):
            r2 = outs[w].shape[1] // 2
            for j in sources:
                landed = _half(outs[w], 2 * chips[j][0] + chips[j][1], c, r2)
                sends.append(_remote(landed, landed, send_sems.at[w * 3 + j], recv_sems.at[w * 3 + j],
                                     (x, y, 1 - c)))
        for cp in sends:
            cp.start()
        for w in range(n):
            r2 = outs[w].shape[1] // 2
            for j in sources:
                got = _half(outs[w], 2 * chips[j][0] + chips[j][1], 1 - c, r2)
                _remote(got, got, send_sems.at[w * 3 + j], recv_sems.at[w * 3 + j], (x, y, c)).wait_recv()
        for cp in sends:
            cp.wait_send()

    return _pallas_call(
        body, name=name,
        in_specs=[ANY] * n, out_specs=[ANY] * n,
        out_shape=[_sds(b.shape, b.dtype) for b in bufs],
        input_output_aliases={i: i for i in range(n)},
        scratch_shapes=[pltpu.SemaphoreType.DMA((n * 3,)), pltpu.SemaphoreType.DMA((n * 3,))],
    )(*bufs)


def _rs_sibling(name, grads):
    n = len(grads)

    def body(*refs):
        ins, outs = refs[:n], refs[n:2 * n]
        send_sems, recv_sems = refs[2 * n:]
        x, y, c, _, _ = _place()
        copies = []
        for w in range(n):
            r2 = ins[w].shape[1] // 2
            copies.append(_remote(_half(ins[w], slice(None), 1 - c, r2), outs[w],
                                  send_sems.at[w], recv_sems.at[w], (x, y, 1 - c)))
        for cp in copies:
            cp.start()
        for cp in copies:
            cp.wait()

    return _pallas_call(
        body, name=name,
        in_specs=[ANY] * n, out_specs=[ANY] * n,
        out_shape=[_sds((N_CHIPS, g.shape[1] // 2, g.shape[2]), F32) for g in grads],
        scratch_shapes=[pltpu.SemaphoreType.DMA((n,)), pltpu.SemaphoreType.DMA((n,))],
    )(*grads)


def _rs_add(name, grad3, from_sibling, c_idx):
    _, r2, cols = from_sibling.shape

    def body(c_ref, g_ref, s_ref, o_ref):
        o_ref[...] = (g_ref[...] + s_ref[...]).astype(BF16)

    return _pallas_call(
        body, name=name,
        grid_spec=pltpu.PrefetchScalarGridSpec(
            num_scalar_prefetch=1, grid=(N_CHIPS,),
            in_specs=[pl.BlockSpec((None, r2, cols), lambda k, c_ref: (k, c_ref[0], 0)),
                      pl.BlockSpec((None, r2, cols), lambda k, c_ref: (k, 0, 0))],
            out_specs=pl.BlockSpec((None, r2, cols), lambda k, c_ref: (k, 0, 0))),
        out_shape=_sds(from_sibling.shape, BF16),
        compiler_params=_params(("parallel",), 48),
    )(c_idx, grad3, from_sibling)


def _split_start(name, arrays, n_sems, pairs_fn):
    n = len(arrays)

    def body(*refs):
        for send, _ in pairs_fn(refs[:n], refs[n], refs[n + 1]):
            send.start()
        refs[-1][...] = jnp.zeros_like(refs[-1])

    out = _pallas_call(
        body, name=name,
        in_specs=[HBM] * n, out_specs=tuple([SEM, SEM] + [HBM] * n + [VMEM]),
        out_shape=tuple([pltpu.SemaphoreType.DMA((n_sems,))] * 2 + [pltpu.HBM(a.shape, a.dtype) for a in arrays]
                        + [_sds((8, LANES), F32)]),
        input_output_aliases={i: 2 + i for i in range(n)},
        compiler_params=pltpu.CompilerParams(has_side_effects=EFFECT),
    )(*arrays)
    return (out[0], out[1]), list(out[2:2 + n]), out[-1]


def _split_wait(name, sems, arrays, pairs_fn, after):
    n = len(arrays)

    def body(*refs):
        for send, arrival in pairs_fn(refs[:n], refs[n], refs[n + 1]):
            send.wait_send()
            arrival.wait_recv()

    return list(_pallas_call(
        body, name=name,
        in_specs=[HBM] * n + [SEM, SEM, ANY], out_specs=[HBM] * n,
        out_shape=[pltpu.HBM(a.shape, a.dtype) for a in arrays],
        input_output_aliases={i: i for i in range(n)},
        compiler_params=pltpu.CompilerParams(has_side_effects=EFFECT),
    )(*arrays, sems[0], sems[1], after))


def _forward_pairs(bufs, send_sems, recv_sems):
    x, y, c, _, chips = _place()
    pairs = []
    for w, buf in enumerate(bufs):
        r2 = buf.shape[1] // 2
        for j, (cx, cy) in enumerate(chips):
            landed, theirs = _half(buf, 2 * cx + cy, c, r2), _half(buf, 2 * cx + cy, 1 - c, r2)
            sems = (send_sems.at[w * 3 + j], recv_sems.at[w * 3 + j])
            pairs.append((_remote(landed, landed, *sems, (x, y, 1 - c)), _remote(theirs, theirs, *sems, (x, y, c))))
    return pairs


def _sibling_pairs(arrays, send_sems, recv_sems):
    x, y, c, _, _ = _place()
    n = len(arrays) // 2
    pairs = []
    for w in range(n):
        r2 = arrays[w].shape[1] // 2
        cp = _remote(_half(arrays[w], slice(None), 1 - c, r2), arrays[n + w], send_sems.at[w], recv_sems.at[w],
                     (x, y, 1 - c))
        pairs.append((cp, cp))
    return pairs


def _ici_pairs(arrays, send_sems, recv_sems):
    x, y, c, _, chips = _place()
    n = len(arrays) // 2
    pairs = []
    for w in range(n):
        for j, (cx, cy) in enumerate(chips):
            cp = _remote(arrays[w].at[2 * cx + cy], arrays[n + w].at[j],
                         send_sems.at[w * 3 + j], recv_sems.at[w * 3 + j], (cx, cy, c))
            pairs.append((cp, cp))
    return pairs


def _rs_sum(name, partials, received, place_idx):
    _, r2, cols = partials.shape
    nb = 2
    tr = r2 // nb

    def body(idx_ref, p_ref, r_ref, o_ref):
        o_ref[...] = ((p_ref[...].astype(F32) + r_ref[0].astype(F32))
                      + (r_ref[1].astype(F32) + r_ref[2].astype(F32)))

    return _pallas_call(
        body, name=name,
        grid_spec=pltpu.PrefetchScalarGridSpec(
            num_scalar_prefetch=1, grid=(nb,),
            in_specs=[pl.BlockSpec((None, tr, cols), lambda i, idx: (idx[0], i, 0)),
                      pl.BlockSpec((3, tr, cols), lambda i, idx: (0, i, 0))],
            out_specs=pl.BlockSpec((tr, cols), lambda i, idx: (idx[1] * nb + i, 0))),
        out_shape=_sds((2 * r2, cols), F32),
        compiler_params=_params(("parallel",), 48),
    )(place_idx, partials, received)


def _share_pairs(arrays, send_sems, recv_sems):
    x, y, c, _, _ = _place()
    pairs = []
    for w, arr in enumerate(arrays):
        r2 = arr.shape[0] // 2
        mine = arr.at[pl.ds(pl.multiple_of(c * r2, 8), r2), :]
        theirs = arr.at[pl.ds(pl.multiple_of((1 - c) * r2, 8), r2), :]
        sems = (send_sems.at[w], recv_sems.at[w])
        pairs.append((_remote(mine, mine, *sems, (x, y, 1 - c)), _remote(theirs, theirs, *sems, (x, y, c))))
    return pairs


def _small_allreduce(red_mix, red_ffn, red_final, red_hg, g_conv, after):
    rows = N_SMALL_ROWS
    D = red_mix.shape[1]
    H = red_hg.shape[1]

    def body(mix_ref, ffn_ref, fin_ref, hg_ref, cv_ref, after_ref, sum_ref, all_ref, in_ref, send_sems,
             recv_sems):
        in_ref[...] = jnp.zeros_like(in_ref)
        in_ref[0:1, :] = mix_ref[0:1, :]
        in_ref[1:2, :] = ffn_ref[0:1, :]
        in_ref[2:3, :] = fin_ref[0:1, :]
        gam = hg_ref[1:2, 0:HEAD_DIM]
        for h in range(1, H // HEAD_DIM):
            gam = gam + hg_ref[1:2, h * HEAD_DIM:(h + 1) * HEAD_DIM]
        in_ref[3:4, 0:HEAD_DIM] = gam
        in_ref[3:4, HEAD_DIM:2 * HEAD_DIM] = fin_ref[1:2, 0:HEAD_DIM]
        in_ref[4:5, 0:H] = hg_ref[0:1, :]
        in_ref[6:9, 0:H] = cv_ref[...]
        x, y, c, _, _ = _place()
        me = 4 * x + 2 * y + c
        all_ref[me] = in_ref[...]
        copies = []
        for m in range(1, 8):
            mx, my, mc = (m >> 2) & 1, (m >> 1) & 1, m & 1
            px, py, pc = x ^ mx, y ^ my, c ^ mc
            copies.append((_remote(in_ref, all_ref.at[me], send_sems.at[m - 1], recv_sems.at[m - 1],
                                   (px, py, pc)), 4 * px + 2 * py + pc, m))
        for cp, _, _ in copies:
            cp.start()
        for _, peer, m in copies:
            _remote(in_ref, all_ref.at[peer], send_sems.at[m - 1], recv_sems.at[m - 1],
                    (x, y, c)).wait_recv()
        for cp, _, _ in copies:
            cp.wait_send()
        total = all_ref[0]
        for d in range(1, 8):
            total = total + all_ref[d]
        sum_ref[...] = total

    return _pallas_call(
        body, name="small_allreduce", pin=False,
        in_specs=[VMEM] * 5 + [ANY], out_specs=[VMEM, VMEM],
        out_shape=[_sds((rows, D), F32), _sds((8, rows, D), F32)],
        scratch_shapes=[pltpu.VMEM((rows, D), F32), pltpu.SemaphoreType.DMA((7,)),
                        pltpu.SemaphoreType.DMA((7,))],
    )(red_mix, red_ffn, red_final, red_hg, g_conv, after)[0]


def _adamw_math(w, g, m, v):
    m = ADAM_B1 * m + (1.0 - ADAM_B1) * g
    v = ADAM_B2 * v + (1.0 - ADAM_B2) * jnp.square(g)
    m_hat = m / (1.0 - ADAM_B1 ** ADAM_STEP)
    v_hat = v / (1.0 - ADAM_B2 ** ADAM_STEP)
    delta = -ADAM_LR * (m_hat / (jnp.sqrt(v_hat) + ADAM_EPS) + ADAM_WD * w)
    return delta, m, v


def _adamw(name, g, w, m, v):
    r, cols = g.shape
    tr = r // 4

    def body(g_ref, w_ref, m_ref, v_ref, go_ref, d_ref, mo_ref, vo_ref):
        g = g_ref[...]
        go_ref[...] = g
        d_ref[...], mo_ref[...], vo_ref[...] = _adamw_math(w_ref[...], g, m_ref[...], v_ref[...])

    blk = pl.BlockSpec((tr, cols), lambda i: (i, 0))
    return _pallas_call(
        body, name=name, grid=(r // tr,),
        in_specs=[blk] * 4, out_specs=[blk] * 4, out_shape=[_sds((r, cols), F32)] * 4,
        compiler_params=_params(("parallel",), 48),
    )(g, w, m, v)


def _small_update(total, chip_idx, ws, ms, vs):
    n = len(ws)
    H = ws[1].shape[1]

    def body(idx_ref, tot_ref, *refs):
        w, m, v, outs = refs[:n], refs[n:2 * n], refs[2 * n:3 * n], refs[3 * n:]
        chip = idx_ref[0]
        p0 = _lower_bound(w[1][...])
        dl0 = p0 * (1.0 - p0) * tot_ref[4:5, 0:H]
        conv = jnp.zeros((3, LANES), F32)
        for k in range(N_CHIPS):
            conv = jnp.where(chip == k, tot_ref[6:9, k * LANES:(k + 1) * LANES], conv)
        grads = [tot_ref[0:1, :], None, tot_ref[3:4, 0:HEAD_DIM], conv, tot_ref[1:2, :], tot_ref[2:3, :]]
        for p in range(n):
            g_ref, d_ref, mo_ref, vo_ref = outs[4 * p:4 * p + 4]
            if p == 1:
                for row, g in ((slice(0, 1), dl0), (slice(1, 2), -dl0)):
                    g_ref[row, :] = g
                    d_ref[row, :], mo_ref[row, :], vo_ref[row, :] = _adamw_math(
                        w[p][row, :], g, m[p][row, :], v[p][row, :])
            else:
                g_ref[...] = grads[p]
                d_ref[...], mo_ref[...], vo_ref[...] = _adamw_math(w[p][...], grads[p], m[p][...], v[p][...])
        outs[4 * n][...] = tot_ref[3:4, HEAD_DIM:2 * HEAD_DIM]

    full = lambda a: pl.BlockSpec(a.shape, lambda i, idx: (0,) * a.ndim)
    out_shape = [_sds(w.shape, F32) for w in ws for _ in range(4)] + [_sds((1, LANES), F32)]
    return _pallas_call(
        body, name="small_update",
        grid_spec=pltpu.PrefetchScalarGridSpec(
            num_scalar_prefetch=1, grid=(1,),
            in_specs=[full(total)] + [full(a) for a in ws + ms + vs],
            out_specs=[full(s) for s in out_shape]),
        out_shape=out_shape,
    )(chip_idx, total, *ws, *ms, *vs)


def kernel(x, norm_mix_g, w_in, lower_bounds, hg_norm_g, conv_w, w_branch_a, w_branch_b, w_out, norm_ffn_g, w_ffn_gate, w_ffn_up, w_ffn_down, norm_final_g, loss_target, m_norm_mix_g, m_w_in, m_lower_bounds, m_hg_norm_g, m_conv_w, m_w_branch_a, m_w_branch_b, m_w_out, m_norm_ffn_g, m_w_ffn_gate, m_w_ffn_up, m_w_ffn_down, m_norm_final_g, v_norm_mix_g, v_w_in, v_lower_bounds, v_hg_norm_g, v_conv_w, v_w_branch_a, v_w_branch_b, v_w_out, v_norm_ffn_g, v_w_ffn_gate, v_w_ffn_up, v_w_ffn_down, v_norm_final_g):
    _, L, D = x.shape
    H = D // 2
    assert lower_bounds.shape == (2, H) and hg_norm_g.shape == (1, HEAD_DIM)
    assert conv_w.shape == (1, 3, LANES) and w_in.shape[2] * N_CHIPS == 11 * H
    x2d, target = x.reshape(L, D), loss_target.reshape(L, D)
    g_final = norm_final_g.reshape(1, D)
    chip = 2 * lax.axis_index("x") + lax.axis_index("y")
    core = lax.axis_index("c")

    tr = lambda w: jnp.transpose(w[0])
    big = [w_in[0], w_branch_a[0], w_branch_b[0], w_out[0], tr(w_ffn_gate), tr(w_ffn_up), w_ffn_down[0]]
    big_m = [m_w_in[0], m_w_branch_a[0], m_w_branch_b[0], m_w_out[0], tr(m_w_ffn_gate), tr(m_w_ffn_up),
             m_w_ffn_down[0]]
    big_v = [v_w_in[0], v_w_branch_a[0], v_w_branch_b[0], v_w_out[0], tr(v_w_ffn_gate), tr(v_w_ffn_up),
             v_w_ffn_down[0]]
    names = ["w_in", "w_branch_a", "w_branch_b", "w_out", "w_ffn_gate", "w_ffn_up", "w_ffn_down"]

    chip_idx = chip.reshape(1).astype(jnp.int32)
    placed = [(_cast_place_t if j < 3 else _cast_place)("place_" + nm, w, chip_idx)
              for j, (nm, w) in enumerate(zip(names, big))]
    conv_placed = lax.dynamic_update_slice(jnp.zeros((N_CHIPS, 3, LANES), F32), conv_w, (chip, 0, 0))
    x_i, y_i = lax.axis_index("x"), lax.axis_index("y")
    blocks = lambda *ks: jnp.stack(ks).astype(jnp.int32)
    near = lambda w, j: j < 2
    far = lambda w, j: w == 1 or j == 2
    near_sems, in_flight, _ = _gather_start("gather_start_near", [([placed[0]], set(), near)], chip_idx)
    w_in_buf = in_flight[0][0]
    h, proj = _fwd_proj_first(x2d, norm_mix_g, w_in_buf, blocks(chip))
    sems, in_flight, _ = _gather_start(
        "gather_start_rest", [([w_in_buf, conv_placed], {1}, far), (placed[1:4], set(), None),
                              (placed[4:], set(), None)], h)
    w_in_buf, conv_buf = in_flight[0]
    (w_in_buf,) = _gather_wait("gather_wait_in_near", [w_in_buf], set(), near_sems[0], h, near)
    (w_in_buf,) = _gather_forward("gather_fwd_in_near", [w_in_buf], (0, 1))
    proj = _fwd_proj_more("fwd_proj_near", h, w_in_buf, proj,
                          blocks(2 * (1 - x_i) + y_i, 2 * x_i + (1 - y_i)))
    w_in_buf, conv_all = _gather_wait("gather_wait_in_far", [w_in_buf, conv_buf], {1}, sems[0], proj, far)
    (w_int3,) = _gather_forward("gather_fwd_in_far", [w_in_buf], (2,))
    proj = _fwd_proj_more("fwd_proj_far", h, w_int3, proj, blocks(2 * (1 - x_i) + (1 - y_i)))
    w_int = w_int3.reshape(-1, D)
    conv_full = jnp.transpose(conv_all, (1, 0, 2)).reshape(3, H)
    og, o_pre, s_saved = _hgrn_fwd(proj, lower_bounds, hg_norm_g, H)
    landed = _gather_wait("gather_wait_mix", in_flight[1], set(), sems[1], og)
    fwd_sems, landed, token = _split_start("gather_fwd_mix_start", landed, 9, _forward_pairs)
    cb = _conv_fwd(proj, conv_full, H, token)
    wat3, wbt3, wout3 = _split_wait("gather_fwd_mix_wait", fwd_sems, landed, _forward_pairs, cb)
    wat, wbt, wout = wat3.reshape(D, H), wbt3.reshape(D, H), wout3.reshape(D, D)
    landed = _gather_wait("gather_wait_ffn", in_flight[2], set(), sems[2], cb)
    fwd_sems, landed, token = _split_start("gather_fwd_ffn_start", landed, 9, _forward_pairs)
    sig_a, sig_b, dm_dga, dm_dgb, merged, x1, h2 = _fwd_mix(og, cb, proj, x2d, wat, wbt, wout, norm_ffn_g,
                                                              H, token)
    wgt3, wut3, wd3 = _split_wait("gather_fwd_ffn_wait", fwd_sems, landed, _forward_pairs, h2)
    d_ff = N_CHIPS * wd3.shape[1]
    wgt, wut, wd = wgt3.reshape(d_ff, D), wut3.reshape(d_ff, D), wd3.reshape(d_ff, D)
    ffn_ds_da, ffn_ds_db, ffn_s = _fwd_ffn_up(h2, wgt, wut)
    dx2, dx2b, red_final = _fwd_down_loss(ffn_s, wd, x1, target, g_final)

    c_idx = core.reshape(1).astype(jnp.int32)
    place_idx = jnp.stack([chip, core]).astype(jnp.int32)

    def sibling_start(tag, grads):
        bufs = [lax.empty((N_CHIPS, g.shape[1] // 2, g.shape[2]), F32) for g in grads]
        return _split_start("rs_sibling_start_" + tag, list(grads) + bufs, len(grads), _sibling_pairs)

    def ici_start(tag, js, grads, from_sibling):
        partials = [_rs_add("rs_add_" + names[j], g, s, c_idx) for j, g, s in zip(js, grads, from_sibling)]
        landings = [lax.empty((3,) + p.shape[1:], BF16) for p in partials]
        return _split_start("rs_ici_start_" + tag, partials + landings, 3 * len(js), _ici_pairs)

    def ici_start_behind(tag, js, started, after):
        n = len(js)
        arrays = _split_wait("rs_sibling_wait_" + tag, started[0], started[1], _sibling_pairs, after)
        return ici_start(tag, js, arrays[:n], arrays[n:])

    def sums(tag, js, started, after):
        n = len(js)
        arrays = _split_wait("rs_ici_wait_" + tag, started[0], started[1], _ici_pairs, after)
        return [_rs_sum("rs_sum_" + names[j], p, r, place_idx) for j, p, r in zip(js, arrays[:n], arrays[n:])]

    adamw = lambda j, g: _adamw("adamw_" + names[j], g, big[j], big_m[j], big_v[j])

    shards3 = lambda g: g.reshape(N_CHIPS, d_ff // N_CHIPS, D)
    da, db = _bwd_down(dx2b, wd, ffn_ds_da, ffn_ds_db)
    g_wd = shards3(_dw_rows2("dw_ffn_down", ffn_s, dx2b))
    g_wg = shards3(_dw_rows2("dw_ffn_gate", da, h2))
    g_wu = shards3(_dw_rows2("dw_ffn_up", db, h2))
    ffn_sibling = sibling_start("ffn", [g_wg, g_wu, g_wd])
    dx1, dx1b, red_ffn = _bwd_ffn_dh(da, db, wgt, wut, x1, dx2, norm_ffn_g, ffn_sibling[2])
    ffn_ici = ici_start_behind("ffn", [4, 5, 6], ffn_sibling, dx1b)
    dya, dyb, dga, dgb, d_o, d_cb = _bwd_mix(dx1b, sig_a, sig_b, dm_dga, dm_dgb, wat, wbt, wout, H,
                                             ffn_ici[2])
    g_wout = _dw_rows("dw_out", merged, dx1b)
    g_wa = _dw_cols("dw_branch_a", og, dya, D // N_CHIPS)
    g_wb = _dw_cols("dw_branch_b", cb, dyb, D // N_CHIPS)
    mix_sibling = sibling_start("mix", [g_wa, g_wb, g_wout])
    dq, df, dv, dg, red_hg = _hgrn_bwd(proj, lower_bounds, hg_norm_g, o_pre, d_o, s_saved, H, mix_sibling[2])
    mix_ici = ici_start_behind("mix", [1, 2, 3], mix_sibling, dq)
    dcg, dbg, dxb, g_conv = _conv_bwd(proj, conv_full, d_cb, H, mix_ici[2])
    dproj = jnp.concatenate([dq, df, dv, dg, dcg, dbg, dxb, dga, dgb], axis=1)
    g_win = _dw_cols("dw_in", h, dproj, w_int3.shape[1])
    in_sibling = sibling_start("in", [g_win])
    halves = sums("mix", [1, 2, 3], mix_ici, in_sibling[2]) + sums("ffn", [4, 5, 6], ffn_ici, in_sibling[2])
    rest_share = _split_start("rs_share_start_rest", halves, len(halves), _share_pairs)
    in_ici = ici_start_behind("in", [0], in_sibling, rest_share[2])
    grad_x, red_mix = _bwd_in(dproj, w_int, x2d, dx1, norm_mix_g, in_ici[2])
    in_share = _split_start("rs_share_start_in", sums("in", [0], in_ici, grad_x), 1, _share_pairs)
    total = _small_allreduce(red_mix, red_ffn, red_final, red_hg, g_conv, in_share[2])
    rest_grads = _split_wait("rs_share_wait_rest", rest_share[0], rest_share[1], _share_pairs, total)
    big_out = [None] + [adamw(j, g) for j, g in zip(range(1, 7), rest_grads)]
    in_grad = _split_wait("rs_share_wait_in", in_share[0], in_share[1], _share_pairs, big_out[6][0])
    big_out[0] = adamw(0, in_grad[0])

    def smalls(mix, lb, hg, cw, ffn, fin):
        return [mix, lb, hg, cw[0], ffn, fin.reshape(1, D)]

    small_out = _small_update(
        total, chip_idx,
        smalls(norm_mix_g, lower_bounds, hg_norm_g, conv_w, norm_ffn_g, norm_final_g),
        smalls(m_norm_mix_g, m_lower_bounds, m_hg_norm_g, m_conv_w, m_norm_ffn_g, m_norm_final_g),
        smalls(v_norm_mix_g, v_lower_bounds, v_hg_norm_g, v_conv_w, v_norm_ffn_g, v_norm_final_g))

    def outputs(i):
        big_i = [big_out[j][i] for j in range(7)]
        mix, lb, hg, cw, ffn, fin = [small_out[4 * p + i] for p in range(6)]
        return [mix, big_i[0][None], lb, hg, cw[None], big_i[1][None], big_i[2][None], big_i[3][None], ffn,
                big_i[4].T[None], big_i[5].T[None], big_i[6][None], fin.reshape(D)]

    outs = [small_out[24][0, 0], grad_x.reshape(1, L, D)]
    for i in range(4):
        outs += outputs(i)
    return tuple(outs)
```

```python
import functools

import jax
import jax.numpy as jnp
from jax import lax
from jax.experimental import pallas as pl
from jax.experimental.pallas import tpu as pltpu

F32 = jnp.float32
BF16 = jnp.bfloat16
EPS = 1e-6
CHUNK = 32
HEAD_DIM = 128
LANES = 128
N_CHIPS = 4
N_SMALL_ROWS = 16

ADAM_LR = 0.001
ADAM_B1 = 0.9
ADAM_B2 = 0.999
ADAM_EPS = 1e-08
ADAM_WD = 0.01
ADAM_STEP = 10

MESH = pl.DeviceIdType.MESH
ANY = pl.BlockSpec(memory_space=pl.ANY)
VMEM = pl.BlockSpec(memory_space=pltpu.VMEM)
HBM = pl.BlockSpec(memory_space=pltpu.HBM)
SEM = pl.BlockSpec(memory_space=pltpu.SEMAPHORE)
EFFECT = pltpu.SideEffectType.DATAFLOW_SIDE_EFFECTING


def _sds(shape, dtype):
    return jax.ShapeDtypeStruct(shape, dtype)


def _pallas_call(body, pin=True, **kwargs):
    if not pin:
        return pl.pallas_call(body, **kwargs)
    in_hbm = lambda s: pltpu.HBM(s.shape, s.dtype) if isinstance(s, jax.ShapeDtypeStruct) else s
    kwargs["out_shape"] = jax.tree.map(in_hbm, kwargs["out_shape"])
    call = pl.pallas_call(body, **kwargs)

    def run(*args):
        return call(*[pltpu.with_memory_space_constraint(a, pltpu.HBM) if a.dtype in (F32, BF16) else a
                      for a in args])

    return run


def _params(semantics, vmem_mb):
    return pltpu.CompilerParams(dimension_semantics=semantics, vmem_limit_bytes=vmem_mb << 20)


def _nn(a, b):
    return lax.dot_general(a, b, (((1,), (0,)), ((), ())), preferred_element_type=F32)


def _nt(a, b):
    return lax.dot_general(a, b, (((1,), (1,)), ((), ())), preferred_element_type=F32)


def _tn(a, b):
    return lax.dot_general(a, b, (((0,), (0,)), ((), ())), preferred_element_type=F32)


def _sigmoid(x):
    return jax.nn.sigmoid(x)


def _rms_stats(x):
    r = lax.rsqrt(jnp.mean(x * x, axis=-1, keepdims=True) + EPS)
    return r, x * r


def _rms_bwd(dxh, xh, r):
    return r * (dxh - xh * jnp.mean(dxh * xh, axis=-1, keepdims=True))


def _fwd_proj_first(x, g_mix, w_int3, block):
    L, D = x.shape
    tn = w_int3.shape[1]
    tm = min(L, 1024)

    def body(blk_ref, x_ref, g_ref, w_ref, h_ref, p_ref):
        _, xh = _rms_stats(x_ref[...])
        h = (xh * g_ref[...]).astype(BF16)
        h_ref[...] = h
        p_ref[...] = _nt(h, w_ref[...])

    return _pallas_call(
        body, name="fwd_proj_own",
        grid_spec=pltpu.PrefetchScalarGridSpec(
            num_scalar_prefetch=1, grid=(L // tm,),
            in_specs=[pl.BlockSpec((tm, D), lambda i, blk: (i, 0)),
                      pl.BlockSpec((1, D), lambda i, blk: (0, 0)),
                      pl.BlockSpec((None, tn, D), lambda i, blk: (blk[0], 0, 0))],
            out_specs=[pl.BlockSpec((tm, D), lambda i, blk: (i, 0)),
                       pl.BlockSpec((tm, tn), lambda i, blk: (i, blk[0]))]),
        out_shape=[_sds((L, D), BF16), _sds((L, N_CHIPS * tn), F32)],
        compiler_params=_params(("parallel",), 48),
    )(block, x, g_mix, w_int3)


def _fwd_proj_more(name, h, w_int3, proj, blocks):
    L, D = h.shape
    tn = w_int3.shape[1]
    tm = min(L, 1024)

    def body(blk_ref, h_ref, w_ref, proj_ref, p_ref):
        p_ref[...] = _nt(h_ref[...], w_ref[...])

    return _pallas_call(
        body, name=name,
        grid_spec=pltpu.PrefetchScalarGridSpec(
            num_scalar_prefetch=1, grid=(L // tm, blocks.shape[0]),
            in_specs=[pl.BlockSpec((tm, D), lambda i, j, blk: (i, 0)),
                      pl.BlockSpec((None, tn, D), lambda i, j, blk: (blk[j], 0, 0)), ANY],
            out_specs=pl.BlockSpec((tm, tn), lambda i, j, blk: (i, blk[j]))),
        out_shape=_sds(proj.shape, proj.dtype),
        input_output_aliases={3: 0},
        compiler_params=_params(("parallel", "arbitrary"), 48),
    )(blocks, h, w_int3, proj)


def _lower_bound(lbp):
    l0, l1 = lbp[0:1, :], lbp[1:2, :]
    m = jnp.maximum(l0, l1)
    e0, e1 = jnp.exp(l0 - m), jnp.exp(l1 - m)
    return e0 / (e0 + e1)


def _seg_scan(x, r32, forward):
    n = x.shape[0]
    s = 1
    while s < CHUNK:
        if forward:
            x = x + jnp.where(r32 >= s, pltpu.roll(x, s, 0), 0.0)
        else:
            x = x + jnp.where(r32 < CHUNK - s, pltpu.roll(x, n - s, 0), 0.0)
        s *= 2
    return x


def _bcast_row(x, row):
    n, w = x.shape
    nc = n // CHUNK
    x3 = x.reshape(nc, CHUNK, w)
    return jnp.broadcast_to(x3[:, row:row + 1, :], (nc, CHUNK, w)).reshape(n, w)


def _hgrn_prep(q_raw, f_raw, lb):
    r32 = lax.broadcasted_iota(jnp.int32, f_raw.shape, 0) & (CHUNK - 1)
    sig = _sigmoid(f_raw)
    f = lb + (1.0 - lb) * sig
    b = _seg_scan(jnp.log(f), r32, True)
    a = _bcast_row(b, CHUNK // 2 - 1)
    bl = _bcast_row(b, CHUNK - 1)
    sq = _sigmoid(q_raw)
    q = q_raw * sq * (HEAD_DIM ** -0.5)
    return dict(r32=r32, sig=sig, f=f, k=1.0 - f, b=b, a=a, bl=bl, sq=sq, q=q)


def _chunk_masks(n):
    ri = lax.broadcasted_iota(jnp.int32, (n, n), 0)
    ci = lax.broadcasted_iota(jnp.int32, (n, n), 1)
    same = (ri // CHUNK) == (ci // CHUNK)
    return same & (ci <= ri), same & (ri <= ci)


def _hgrn_fwd(proj, lower_bounds, gamma, H):
    L = proj.shape[0]
    nh = H // HEAD_DIM
    TL = min(L, 256)
    nc = TL // CHUNK

    def body(q_ref, f_ref, v_ref, g_ref, lbp_ref, gam_ref, og_ref, o_ref, s_ref, st_ref):
        @pl.when(pl.program_id(0) == 0)
        def _():
            st_ref[...] = jnp.zeros_like(st_ref)

        lb = _lower_bound(lbp_ref[...])
        gam = gam_ref[...]
        mask, _ = _chunk_masks(TL)
        rowc = lax.broadcasted_iota(jnp.int32, (TL, HEAD_DIM), 0) // CHUNK
        for h in range(nh):
            hs = slice(h * HEAD_DIM, (h + 1) * HEAD_DIM)
            p = _hgrn_prep(q_ref[:, hs], f_ref[:, hs], lb[:, hs])
            v = v_ref[:, hs]
            vb = v.astype(BF16)
            vt = v.T.astype(BF16)
            q_hat = (p["q"] * jnp.exp(p["b"] - p["a"])).astype(BF16)
            k_hat = (p["k"] * jnp.exp(p["a"] - p["b"])).astype(BF16)
            q_in = (p["q"] * jnp.exp(p["b"])).astype(BF16)
            k_out = (p["k"] * jnp.exp(p["bl"] - p["b"])).astype(BF16)
            dec = jnp.exp(p["bl"])
            att = jnp.where(mask, _nt(q_hat, k_hat), 0.0).astype(BF16)
            o_intra = _nn(att, vb)
            st = st_ref[h]
            for c in range(nc):
                rs = slice(c * CHUNK, (c + 1) * CHUNK)
                stb = st.astype(BF16)
                s_ref[c, h] = stb
                o_ref[rs, hs] = o_intra[rs] + _nt(q_in[rs], stb)
                k_c = jnp.where(rowc == c, k_out, jnp.zeros_like(k_out))
                st = st * dec[c * CHUNK:c * CHUNK + 1, :] + _nn(vt, k_c)
            st_ref[h] = st
            o = o_ref[:, hs]
            _, xh = _rms_stats(o)
            gr = g_ref[:, hs]
            og_ref[:, hs] = (xh * gam * (gr * _sigmoid(gr))).astype(BF16)

    col = lambda k: pl.BlockSpec((TL, H), lambda i, k=k: (i, k))
    return _pallas_call(
        body, name="hgrn_fwd", grid=(L // TL,),
        in_specs=[col(0), col(1), col(2), col(3),
                  pl.BlockSpec(lower_bounds.shape, lambda i: (0, 0)),
                  pl.BlockSpec(gamma.shape, lambda i: (0, 0))],
        out_specs=[pl.BlockSpec((TL, H), lambda i: (i, 0)),
                   pl.BlockSpec((TL, H), lambda i: (i, 0)),
                   pl.BlockSpec((nc, nh, HEAD_DIM, HEAD_DIM), lambda i: (i, 0, 0, 0))],
        out_shape=[_sds((L, H), BF16), _sds((L, H), F32),
                   _sds((L // CHUNK, nh, HEAD_DIM, HEAD_DIM), BF16)],
        scratch_shapes=[pltpu.VMEM((nh, HEAD_DIM, HEAD_DIM), F32)],
        compiler_params=_params(("arbitrary",), 48),
    )(proj, proj, proj, proj, lower_bounds, gamma)


def _hgrn_bwd(proj, lower_bounds, gamma, o_pre, d_out, s_saved, H, after):
    L = proj.shape[0]
    nh = H // HEAD_DIM
    TL = min(L, 256)
    nc = TL // CHUNK
    nt = L // TL

    def body(q_ref, f_ref, v_ref, g_ref, lbp_ref, gam_ref, o_ref, d_ref, s_ref, after_ref,
             dp_ref, red_ref, dst_ref, dsall_ref, tmp_ref):
        @pl.when(pl.program_id(0) == 0)
        def _():
            dst_ref[...] = jnp.zeros_like(dst_ref)
            red_ref[...] = jnp.zeros_like(red_ref)

        lb = _lower_bound(lbp_ref[...])
        gam = gam_ref[...]
        mask, mask_t = _chunk_masks(TL)
        rowc = lax.broadcasted_iota(jnp.int32, (TL, HEAD_DIM), 0) // CHUNK
        for h in range(nh):
            hs = slice(h * HEAD_DIM, (h + 1) * HEAD_DIM)
            qr, gr, lbh = q_ref[:, hs], g_ref[:, hs], lb[:, hs]
            p = _hgrn_prep(qr, f_ref[:, hs], lbh)
            vb = v_ref[:, hs].astype(BF16)
            eba, eab = jnp.exp(p["b"] - p["a"]), jnp.exp(p["a"] - p["b"])
            eb, elb = jnp.exp(p["b"]), jnp.exp(p["bl"] - p["b"])
            dec = jnp.exp(p["bl"])
            q_hat, k_hat = p["q"] * eba, p["k"] * eab
            q_in, k_out = p["q"] * eb, p["k"] * elb
            q_hat_b, k_hat_b = q_hat.astype(BF16), k_hat.astype(BF16)
            q_in_b, k_out_b = q_in.astype(BF16), k_out.astype(BF16)

            o, dout = o_ref[:, hs], d_ref[:, hs]
            sg = _sigmoid(gr)
            r, xh = _rms_stats(o)
            hq, hf, hv, hg = [slice(k * H + h * HEAD_DIM, k * H + (h + 1) * HEAD_DIM) for k in range(4)]
            dp_ref[:, hg] = (dout * (xh * gam) * (sg * (1.0 + gr * (1.0 - sg)))).astype(BF16)
            dn = dout * (gr * sg)
            red_ref[1:2, hs] += jnp.sum(dn * xh, axis=0, keepdims=True)
            do = _rms_bwd(dn * gam, xh, r)
            dob = do.astype(BF16)
            dot_b = do.T.astype(BF16)

            att_t = jnp.where(mask_t, _nt(k_hat_b, q_hat_b), 0.0).astype(BF16)
            dv_intra = _nn(att_t, dob)
            datt = jnp.where(mask, _nt(dob, vb), 0.0).astype(BF16)
            dqh = _nn(datt, k_hat_b)
            datt_t = jnp.where(mask_t, _nt(vb, dob), 0.0).astype(BF16)
            dkh = _nn(datt_t, q_hat_b)

            dst = dst_ref[h]
            for c in reversed(range(nc)):
                dsall_ref[c] = dst
                q_c = jnp.where(rowc == c, q_in_b, jnp.zeros_like(q_in_b))
                dst = dst * dec[c * CHUNK:c * CHUNK + 1, :] + _nn(dot_b, q_c)
            dst_ref[h] = dst
            for c in range(nc):
                rs = slice(c * CHUNK, (c + 1) * CHUNK)
                ds_c = dsall_ref[c]
                dsb = ds_c.astype(BF16)
                st_prev = s_ref[c, h]
                tmp_ref[0, rs, :] = _nt(k_out_b[rs], dsb)
                tmp_ref[1, rs, :] = _nn(vb[rs], dsb)
                tmp_ref[2, rs, :] = _nn(dob[rs], st_prev)
                ddec = jnp.sum(ds_c * st_prev.astype(F32), axis=0, keepdims=True)
                tmp_ref[3, rs, :] = jnp.broadcast_to(ddec * dec[c * CHUNK:c * CHUNK + 1, :],
                                                     (CHUNK, HEAD_DIM))
            dko, dqi = tmp_ref[1], tmp_ref[2]
            dq = dqh * eba + dqi * eb
            dk = dkh * eab + dko * elb
            tko = dko * k_out
            db = dqh * q_hat - dkh * k_hat + dqi * q_in - tko
            dlog = (_seg_scan(db, p["r32"], False)
                    + _bcast_row(_seg_scan(tko, p["r32"], True), CHUNK - 1) + tmp_ref[3])
            df = dlog / p["f"] - dk
            sig = p["sig"]
            red_ref[0:1, hs] += jnp.sum(df * (1.0 - sig), axis=0, keepdims=True)
            dp_ref[:, hf] = (df * (1.0 - lbh) * sig * (1.0 - sig)).astype(BF16)
            sq = p["sq"]
            dp_ref[:, hq] = (dq * (HEAD_DIM ** -0.5) * (sq * (1.0 + qr * (1.0 - sq)))).astype(BF16)
            dp_ref[:, hv] = (dv_intra + tmp_ref[0]).astype(BF16)

    col = lambda k: pl.BlockSpec((TL, H), lambda i, k=k: (nt - 1 - i, k))
    rev = pl.BlockSpec((TL, H), lambda i: (nt - 1 - i, 0))
    return _pallas_call(
        body, name="hgrn_bwd", grid=(nt,),
        in_specs=[col(0), col(1), col(2), col(3),
                  pl.BlockSpec(lower_bounds.shape, lambda i: (0, 0)),
                  pl.BlockSpec(gamma.shape, lambda i: (0, 0)),
                  rev, rev,
                  pl.BlockSpec((nc, nh, HEAD_DIM, HEAD_DIM), lambda i: (nt - 1 - i, 0, 0, 0)), ANY],
        out_specs=[pl.BlockSpec((TL, 4 * H), lambda i: (nt - 1 - i, 0)), pl.BlockSpec((8, H), lambda i: (0, 0))],
        out_shape=[_sds((L, 4 * H), BF16), _sds((8, H), F32)],
        scratch_shapes=[pltpu.VMEM((nh, HEAD_DIM, HEAD_DIM), F32),
                        pltpu.VMEM((nc, HEAD_DIM, HEAD_DIM), F32),
                        pltpu.VMEM((4, TL, HEAD_DIM), F32)],
        compiler_params=_params(("arbitrary",), 48),
    )(proj, proj, proj, proj, lower_bounds, gamma, o_pre, d_out, s_saved, after)


def _shift_down(u, s, row):
    return jnp.where(row >= s, pltpu.roll(u, s, 0), 0.0)


def _shift_up(u, s, row):
    n = u.shape[0]
    return jnp.where(row < n - s, pltpu.roll(u, n - s, 0), 0.0)


def _conv_specs(L, H):
    per = H // LANES
    return [pl.BlockSpec((L, LANES), lambda j, o=o: (0, o * per + j)) for o in (4, 5, 6)]


def _conv_fwd(proj, conv_w, H, after):
    L = proj.shape[0]

    def body(c_ref, b_ref, x_ref, w_ref, after_ref, o_ref):
        row = lax.broadcasted_iota(jnp.int32, (L, LANES), 0)
        u = c_ref[...] * x_ref[...]
        w = w_ref[...]
        y = w[0:1] * _shift_down(u, 2, row) + w[1:2] * _shift_down(u, 1, row) + w[2:3] * u
        o_ref[...] = (b_ref[...] * y).astype(BF16)

    return _pallas_call(
        body, name="conv_fwd", grid=(H // LANES,),
        in_specs=_conv_specs(L, H) + [pl.BlockSpec((3, LANES), lambda j: (0, j)), ANY],
        out_specs=pl.BlockSpec((L, LANES), lambda j: (0, j)),
        out_shape=_sds((L, H), BF16),
        compiler_params=_params(("parallel",), 48),
    )(proj, proj, proj, conv_w, after)


def _conv_bwd(proj, conv_w, dcb, H, after):
    L = proj.shape[0]

    def body(c_ref, b_ref, x_ref, w_ref, d_ref, after_ref, dc_ref, db_ref, dx_ref, dw_ref):
        row = lax.broadcasted_iota(jnp.int32, (L, LANES), 0)
        cg, xb = c_ref[...], x_ref[...]
        u = cg * xb
        u1, u2 = _shift_down(u, 1, row), _shift_down(u, 2, row)
        w = w_ref[...]
        y = w[0:1] * u2 + w[1:2] * u1 + w[2:3] * u
        d = d_ref[...]
        db_ref[...] = (d * y).astype(BF16)
        dy = d * b_ref[...]
        du = w[2:3] * dy + w[1:2] * _shift_up(dy, 1, row) + w[0:1] * _shift_up(dy, 2, row)
        dw_ref[0:1, :] = jnp.sum(dy * u2, axis=0, keepdims=True)
        dw_ref[1:2, :] = jnp.sum(dy * u1, axis=0, keepdims=True)
        dw_ref[2:3, :] = jnp.sum(dy * u, axis=0, keepdims=True)
        dc_ref[...] = (du * xb).astype(BF16)
        dx_ref[...] = (du * cg).astype(BF16)

    blk = pl.BlockSpec((L, LANES), lambda j: (0, j))
    return _pallas_call(
        body, name="conv_bwd", grid=(H // LANES,),
        in_specs=_conv_specs(L, H) + [pl.BlockSpec((3, LANES), lambda j: (0, j)), blk, ANY],
        out_specs=[blk, blk, blk, pl.BlockSpec((3, LANES), lambda j: (0, j))],
        out_shape=[_sds((L, H), BF16)] * 3 + [_sds((3, H), F32)],
        compiler_params=_params(("parallel",), 56),
    )(proj, proj, proj, conv_w, dcb, after)


def _gate_specs(tm, H):
    return [pl.BlockSpec((tm, H), lambda i, k=k: (i, k)) for k in (7, 8, 9, 10)]


def _fwd_mix(og, cb, proj, x, wat, wbt, wout, g_ffn, H, after):
    L, D = x.shape
    tm = min(L, 512)

    def body(o_ref, cb_ref, ga0, ga1, gb0, gb1, x_ref, wa_ref, wb_ref, wo_ref, g_ref, after_ref,
             sa_ref, sb_ref, ta_ref, tb_ref, m_ref, x1_ref, h2_ref):
        ya, yb = _nt(o_ref[...], wa_ref[...]), _nt(cb_ref[...], wb_ref[...])
        for k, (gar, gbr) in enumerate(((ga0, gb0), (ga1, gb1))):
            cs = slice(k * H, (k + 1) * H)
            sa, sb = _sigmoid(gar[...]), _sigmoid(gbr[...])
            ma, mb = sa * ya[:, cs], sb * yb[:, cs]
            m_ref[:, cs] = (ma + mb).astype(BF16)
            sa_ref[:, cs] = sa.astype(BF16)
            sb_ref[:, cs] = sb.astype(BF16)
            ta_ref[:, cs] = (ma * (1.0 - sa)).astype(BF16)
            tb_ref[:, cs] = (mb * (1.0 - sb)).astype(BF16)
        x1 = x_ref[...] + _nn(m_ref[...], wo_ref[...])
        x1_ref[...] = x1
        _, xh = _rms_stats(x1)
        h2_ref[...] = (xh * g_ref[...]).astype(BF16)

    row = lambda w: pl.BlockSpec((tm, w), lambda i: (i, 0))
    full = lambda a: pl.BlockSpec(a.shape, lambda i: (0,) * a.ndim)
    return _pallas_call(
        body, name="fwd_mix", grid=(L // tm,),
        in_specs=[row(H), row(H)] + _gate_specs(tm, H) + [row(D), full(wat), full(wbt), full(wout),
                                                           full(g_ffn), ANY],
        out_specs=[row(D)] * 7,
        out_shape=[_sds((L, D), BF16)] * 5 + [_sds((L, D), F32), _sds((L, D), BF16)],
        compiler_params=_params(("parallel",), 56),
    )(og, cb, proj, proj, proj, proj, x, wat, wbt, wout, g_ffn, after)


def _bwd_mix(dx1b, sig_a, sig_b, dm_dga, dm_dgb, wat, wbt, wout, H, after):
    L, D = dx1b.shape
    tm = min(L, 512)

    def body(dx_ref, sa_ref, sb_ref, ta_ref, tb_ref, wa_ref, wb_ref, wo_ref, after_ref,
             dya_ref, dyb_ref, dgate_ref, do_ref, dcb_ref):
        dm = _nt(dx_ref[...], wo_ref[...])
        dgate_ref[:, 0:D] = (dm * ta_ref[...].astype(F32)).astype(BF16)
        dgate_ref[:, D:2 * D] = (dm * tb_ref[...].astype(F32)).astype(BF16)
        dya_ref[...] = (dm * sa_ref[...].astype(F32)).astype(BF16)
        dyb_ref[...] = (dm * sb_ref[...].astype(F32)).astype(BF16)
        do_ref[...] = _nn(dya_ref[...], wa_ref[...])
        dcb_ref[...] = _nn(dyb_ref[...], wb_ref[...])

    row = lambda w: pl.BlockSpec((tm, w), lambda i: (i, 0))
    full = lambda a: pl.BlockSpec(a.shape, lambda i: (0,) * a.ndim)
    return _pallas_call(
        body, name="bwd_mix", grid=(L // tm,),
        in_specs=[row(D)] * 5 + [full(wat), full(wbt), full(wout), ANY],
        out_specs=[row(D), row(D), row(2 * D), row(H), row(H)],
        out_shape=[_sds((L, D), BF16)] * 2 + [_sds((L, 2 * D), BF16)] + [_sds((L, H), F32)] * 2,
        compiler_params=_params(("parallel",), 56),
    )(dx1b, sig_a, sig_b, dm_dga, dm_dgb, wat, wbt, wout, after)


def _fwd_ffn_up(h2, wgt, wut):
    L, D = h2.shape
    F = wgt.shape[0]
    tn = F // 2
    tm = min(L, 512)

    def body(h_ref, wg_ref, wu_ref, sa_ref, sb_ref, s_ref):
        h = h_ref[...]
        a, b = _nt(h, wg_ref[...]), _nt(h, wu_ref[...])
        sg = _sigmoid(a)
        silu = a * sg
        sa_ref[...] = (b * sg * (1.0 + a * (1.0 - sg))).astype(BF16)
        sb_ref[...] = silu.astype(BF16)
        s_ref[...] = (silu * b).astype(BF16)

    wspec = pl.BlockSpec((tn, D), lambda j, i: (j, 0))
    ospec = pl.BlockSpec((tm, tn), lambda j, i: (i, j))
    return _pallas_call(
        body, name="fwd_ffn_up", grid=(2, L // tm),
        in_specs=[pl.BlockSpec((tm, D), lambda j, i: (i, 0)), wspec, wspec],
        out_specs=[ospec] * 3,
        out_shape=[_sds((L, F), BF16)] * 3,
        compiler_params=_params(("parallel", "parallel"), 48),
    )(h2, wgt, wut)


def _fwd_down_loss(s, wd, x1, target, g_final):
    L, D = x1.shape
    F = wd.shape[0]
    tm = min(L, 256)

    def body(s_ref, wd_ref, x1_ref, t_ref, g_ref, dx_ref, dxb_ref, red_ref):
        @pl.when(pl.program_id(0) == 0)
        def _():
            red_ref[...] = jnp.zeros_like(red_ref)

        g = g_ref[...]
        r, xh = _rms_stats(x1_ref[...] + _nn(s_ref[...], wd_ref[...]))
        e = xh * g - t_ref[...]
        dy = e * (1.0 / D)
        dx = _rms_bwd(dy * g, xh, r)
        dx_ref[...] = dx
        dxb_ref[...] = dx.astype(BF16)
        red_ref[0:1, :] += jnp.sum(dy * xh, axis=0, keepdims=True)
        red_ref[1:2, :] += jnp.broadcast_to(0.5 * jnp.sum(e * e) * (1.0 / D), (1, D))

    row = pl.BlockSpec((tm, D), lambda i: (i, 0))
    return _pallas_call(
        body, name="fwd_down_loss", grid=(L // tm,),
        in_specs=[pl.BlockSpec((tm, F), lambda i: (i, 0)), pl.BlockSpec((F, D), lambda i: (0, 0)),
                  row, row, pl.BlockSpec((1, D), lambda i: (0, 0))],
        out_specs=[row, row, pl.BlockSpec((8, D), lambda i: (0, 0))],
        out_shape=[_sds((L, D), F32), _sds((L, D), BF16), _sds((8, D), F32)],
        compiler_params=_params(("arbitrary",), 48),
    )(s, wd, x1, target, g_final)


def _bwd_down(dx2b, wd, s_a, s_b):
    L, D = dx2b.shape
    F = wd.shape[0]
    tn = F // 2
    tm = min(L, 512)

    def body(dx_ref, wd_ref, sa_ref, sb_ref, da_ref, db_ref):
        ds = _nt(dx_ref[...], wd_ref[...])
        da_ref[...] = (ds * sa_ref[...].astype(F32)).astype(BF16)
        db_ref[...] = (ds * sb_ref[...].astype(F32)).astype(BF16)

    ospec = pl.BlockSpec((tm, tn), lambda j, i: (i, j))
    return _pallas_call(
        body, name="bwd_down", grid=(2, L // tm),
        in_specs=[pl.BlockSpec((tm, D), lambda j, i: (i, 0)),
                  pl.BlockSpec((tn, D), lambda j, i: (j, 0)), ospec, ospec],
        out_specs=[ospec] * 2,
        out_shape=[_sds((L, F), BF16)] * 2,
        compiler_params=_params(("parallel", "parallel"), 48),
    )(dx2b, wd, s_a, s_b)


def _bwd_ffn_dh(da, db, wgt, wut, x1, dx2, g_ffn, after):
    L, D = x1.shape
    F = wgt.shape[0]
    tm = min(L, 256)

    def body(da_ref, db_ref, wg_ref, wu_ref, x1_ref, dx2_ref, g_ref, after_ref, dx_ref, dxb_ref, red_ref):
        @pl.when(pl.program_id(0) == 0)
        def _():
            red_ref[...] = jnp.zeros_like(red_ref)

        dh = _nn(da_ref[...], wg_ref[...]) + _nn(db_ref[...], wu_ref[...])
        r, xh = _rms_stats(x1_ref[...])
        red_ref[0:1, :] += jnp.sum(dh * xh, axis=0, keepdims=True)
        dx = dx2_ref[...] + _rms_bwd(dh * g_ref[...], xh, r)
        dx_ref[...] = dx
        dxb_ref[...] = dx.astype(BF16)

    row = pl.BlockSpec((tm, D), lambda i: (i, 0))
    aspec = pl.BlockSpec((tm, F), lambda i: (i, 0))
    wspec = pl.BlockSpec((F, D), lambda i: (0, 0))
    return _pallas_call(
        body, name="bwd_ffn_dh", grid=(L // tm,),
        in_specs=[aspec, aspec, wspec, wspec, row, row, pl.BlockSpec((1, D), lambda i: (0, 0)), ANY],
        out_specs=[row, row, pl.BlockSpec((8, D), lambda i: (0, 0))],
        out_shape=[_sds((L, D), F32), _sds((L, D), BF16), _sds((8, D), F32)],
        compiler_params=_params(("arbitrary",), 56),
    )(da, db, wgt, wut, x1, dx2, g_ffn, after)


def _piece_offsets(pieces):
    offsets, total = [], 0
    for p in pieces:
        offsets.append(total)
        total += p.shape[1]
    return offsets, total


def _bwd_in(pieces, w_int, x, dx1, g_mix, after):
    L, D = x.shape
    N = w_int.shape[0]
    tm = min(L, 256)
    n = len(pieces)
    offsets, total = _piece_offsets(pieces)
    assert total == N

    def body(*refs):
        piece_refs = refs[:n]
        w_ref, x_ref, dx1_ref, g_ref, after_ref, dx_ref, red_ref, dp_ref = refs[n:]

        @pl.when(pl.program_id(0) == 0)
        def _():
            red_ref[...] = jnp.zeros_like(red_ref)

        for p_ref, off in zip(piece_refs, offsets):
            dp_ref[:, off:off + p_ref.shape[1]] = p_ref[...]
        dh = _nn(dp_ref[...], w_ref[...])
        r, xh = _rms_stats(x_ref[...])
        red_ref[0:1, :] += jnp.sum(dh * xh, axis=0, keepdims=True)
        dx_ref[...] = dx1_ref[...] + _rms_bwd(dh * g_ref[...], xh, r)

    row = pl.BlockSpec((tm, D), lambda i: (i, 0))
    return _pallas_call(
        body, name="bwd_in", grid=(L // tm,),
        in_specs=[pl.BlockSpec((tm, p.shape[1]), lambda i: (i, 0)) for p in pieces]
        + [pl.BlockSpec((N, D), lambda i: (0, 0)), row, row, pl.BlockSpec((1, D), lambda i: (0, 0)), ANY],
        out_specs=[row, pl.BlockSpec((8, D), lambda i: (0, 0))],
        out_shape=[_sds((L, D), F32), _sds((8, D), F32)],
        scratch_shapes=[pltpu.VMEM((tm, N), BF16)],
        compiler_params=_params(("arbitrary",), 56),
    )(*pieces, w_int, x, dx1, g_mix, after)


def _dw_in(h, pieces, n_cols):
    L, D = h.shape
    tk = min(L, TK_TOKENS // 2)
    n = len(pieces)
    offsets, total = _piece_offsets(pieces)
    assert total == N_CHIPS * n_cols
    plan = []
    for j in range(N_CHIPS):
        lo, hi = j * n_cols, (j + 1) * n_cols
        segments = []
        for p, off in enumerate(offsets):
            a, b = max(lo, off), min(hi, off + pieces[p].shape[1])
            if a < b:
                segments.append((p, a - off, b - a, a - lo))
        plan.append(segments)

    def body(*refs):
        h_ref, piece_refs, o_ref, b_ref = refs[0], refs[1:1 + n], refs[1 + n], refs[2 + n]
        j, k = pl.program_id(0), pl.program_id(1)
        for jj in range(N_CHIPS):
            @pl.when(j == jj)
            def _(jj=jj):
                for p, start, width, at in plan[jj]:
                    b_ref[:, at:at + width] = piece_refs[p][:, start:start + width]

        part = _tn(h_ref[...], b_ref[...])

        @pl.when(k == 0)
        def _():
            o_ref[...] = part

        @pl.when(k > 0)
        def _():
            o_ref[...] += part

    def piece_spec(p):
        used = [j for j in range(N_CHIPS) if any(seg[0] == p for seg in plan[j])]

        def index(j, k):
            in_use = functools.reduce(jnp.logical_or, [j == u for u in used])
            return (jnp.where(in_use, k, 0), 0)

        return pl.BlockSpec((tk, pieces[p].shape[1]), index)

    return _pallas_call(
        body, name="dw_in", grid=(N_CHIPS, L // tk),
        in_specs=[pl.BlockSpec((tk, D), lambda j, k: (k, 0))] + [piece_spec(p) for p in range(n)],
        out_specs=pl.BlockSpec((None, D, n_cols), lambda j, k: (j, 0, 0)),
        out_shape=_sds((N_CHIPS, D, n_cols), F32),
        scratch_shapes=[pltpu.VMEM((tk, n_cols), BF16)],
        compiler_params=_params(("parallel", "arbitrary"), 56),
    )(h, *pieces)


def _mm_tn(name, a, b, a_spec, b_spec, o_block, n_out, n_k):
    def body(a_ref, b_ref, o_ref):
        part = _tn(a_ref[...], b_ref[...])

        @pl.when(pl.program_id(1) == 0)
        def _():
            o_ref[...] = part

        @pl.when(pl.program_id(1) > 0)
        def _():
            o_ref[...] += part

    return _pallas_call(
        body, name=name, grid=(n_out, n_k),
        in_specs=[a_spec, b_spec],
        out_specs=pl.BlockSpec((None,) + o_block, lambda j, k: (j, 0, 0)),
        out_shape=_sds((n_out,) + o_block, F32),
        compiler_params=_params(("parallel", "arbitrary"), 56),
    )(a, b)


TK_TOKENS = 2048


def _dw_cols(name, a, b, n_cols):
    L, M = a.shape
    tk = min(L, TK_TOKENS)
    return _mm_tn(name, a, b, pl.BlockSpec((tk, M), lambda j, k: (k, 0)),
                  pl.BlockSpec((tk, n_cols), lambda j, k: (k, j)), (M, n_cols), N_CHIPS, L // tk)


def _dw_rows(name, a, b):
    L, M = a.shape
    N = b.shape[1]
    tk = min(L, TK_TOKENS)
    return _mm_tn(name, a, b, pl.BlockSpec((tk, M // N_CHIPS), lambda j, k: (k, j)),
                  pl.BlockSpec((tk, N), lambda j, k: (k, 0)), (M // N_CHIPS, N), N_CHIPS, L // tk)


def _dw_rows2(name, a, b):
    L, M = a.shape
    N = b.shape[1]
    tk = min(L, TK_TOKENS)
    return _mm_tn(name, a, b, pl.BlockSpec((tk, M // 2), lambda j, k: (k, j)),
                  pl.BlockSpec((tk, N), lambda j, k: (k, 0)), (M // 2, N), 2, L // tk)


def _place():
    x, y, c = lax.axis_index("x"), lax.axis_index("y"), lax.axis_index("c")
    chips = [(1 - x, y), (x, 1 - y), (1 - x, 1 - y)]
    return x, y, c, 2 * x + y, chips


def _remote(src, dst, send_sem, recv_sem, device):
    return pltpu.make_async_remote_copy(src_ref=src, dst_ref=dst, send_sem=send_sem,
                                        recv_sem=recv_sem, device_id=device, device_id_type=MESH)


def _half(ref, lead, c, r2):
    return ref.at[lead, pl.ds(pl.multiple_of(c * r2, 16), r2), :]


def _cast_place(name, w, chip_idx):
    r, cols = w.shape
    tr = r // 2

    def body(k_ref, w_ref, o_ref):
        o_ref[...] = w_ref[...].astype(BF16)

    return _pallas_call(
        body, name=name,
        grid_spec=pltpu.PrefetchScalarGridSpec(
            num_scalar_prefetch=1, grid=(2,),
            in_specs=[pl.BlockSpec((tr, cols), lambda i, k_ref: (i, 0))],
            out_specs=pl.BlockSpec((None, tr, cols), lambda i, k_ref: (k_ref[0], i, 0))),
        out_shape=_sds((N_CHIPS, r, cols), BF16),
        compiler_params=_params(("parallel",), 48),
    )(chip_idx, w)


def _cast_place_t(name, w, chip_idx):
    r, cols = w.shape

    def body(k_ref, w_ref, o_ref):
        o_ref[...] = w_ref[...].T.astype(BF16)

    return _pallas_call(
        body, name=name,
        grid_spec=pltpu.PrefetchScalarGridSpec(
            num_scalar_prefetch=1, grid=(cols // LANES,),
            in_specs=[pl.BlockSpec((r, LANES), lambda i, k_ref: (0, i))],
            out_specs=pl.BlockSpec((None, LANES, r), lambda i, k_ref: (k_ref[0], i, 0))),
        out_shape=_sds((N_CHIPS, cols, r), BF16),
        compiler_params=_params(("parallel",), 48),
    )(chip_idx, w)


def _gather_copies(bufs, whole, send_sems, recv_sems, select=None):
    x, y, c, k, chips = _place()
    pairs = []
    for w, buf in enumerate(bufs):
        for j, (cx, cy) in enumerate(chips):
            if select is not None and not select(w, j):
                continue
            if w in whole:
                mine, theirs = buf.at[k], buf.at[2 * cx + cy]
            else:
                r2 = buf.shape[1] // 2
                mine, theirs = _half(buf, k, c, r2), _half(buf, 2 * cx + cy, c, r2)
            sems = (send_sems.at[w * 3 + j], recv_sems.at[w * 3 + j])
            pairs.append((_remote(mine, mine, *sems, (cx, cy, c)), _remote(theirs, theirs, *sems, (x, y, c))))
    return pairs


def _gather_start(name, groups, after):
    flat = [b for bufs, _, _ in groups for b in bufs]
    nb, ng = len(flat), len(groups)

    def body(*refs):
        ins, sems, token = refs[:nb], refs[nb + 1:nb + 1 + 2 * ng], refs[-1]
        pos = 0
        for g, (bufs, whole, select) in enumerate(groups):
            for send, _ in _gather_copies(ins[pos:pos + len(bufs)], whole, sems[2 * g], sems[2 * g + 1], select):
                send.start()
            pos += len(bufs)
        token[...] = jnp.zeros_like(token)

    sem_shapes = []
    for bufs, _, _ in groups:
        sem_shapes += [pltpu.SemaphoreType.DMA((3 * len(bufs),))] * 2
    out = _pallas_call(
        body, name=name,
        in_specs=[HBM] * nb + [ANY], out_specs=tuple([SEM] * (2 * ng) + [HBM] * nb + [VMEM]),
        out_shape=tuple(sem_shapes + [pltpu.HBM(b.shape, b.dtype) for b in flat] + [_sds((8, LANES), F32)]),
        input_output_aliases={i: 2 * ng + i for i in range(nb)},
        compiler_params=pltpu.CompilerParams(has_side_effects=EFFECT),
    )(*flat, after)
    sems, thru, pos = [], [], 2 * ng
    for g, (bufs, _, _) in enumerate(groups):
        sems.append((out[2 * g], out[2 * g + 1]))
        thru.append(list(out[pos:pos + len(bufs)]))
        pos += len(bufs)
    return sems, thru, out[-1]


def _gather_wait(name, bufs, whole, sems, after, select=None):
    nb = len(bufs)

    def body(*refs):
        ins, send_sems, recv_sems = refs[:nb], refs[nb], refs[nb + 1]
        for send, arrival in _gather_copies(ins, whole, send_sems, recv_sems, select):
            send.wait_send()
            arrival.wait_recv()

    return _pallas_call(
        body, name=name,
        in_specs=[HBM] * nb + [SEM, SEM, ANY], out_specs=[HBM] * nb,
        out_shape=[pltpu.HBM(b.shape, b.dtype) for b in bufs],
        input_output_aliases={i: i for i in range(nb)},
        compiler_params=pltpu.CompilerParams(has_side_effects=EFFECT),
    )(*bufs, sems[0], sems[1], after)


def _gather_forward(name, bufs, sources=(0, 1, 2)):
    n = len(bufs)

    def body(*refs):
        outs = refs[n:2 * n]
        send_sems, recv_sems = refs[2 * n:]
        x, y, c, _, chips = _place()
        sends = []
        for w in range(n):
            r2 = outs[w].shape[1] // 2
            for j in sources:
                landed = _half(outs[w], 2 * chips[j][0] + chips[j][1], c, r2)
                sends.append(_remote(landed, landed, send_sems.at[w * 3 + j], recv_sems.at[w * 3 + j],
                                     (x, y, 1 - c)))
        for cp in sends:
            cp.start()
        for w in range(n):
            r2 = outs[w].shape[1] // 2
            for j in sources:
                got = _half(outs[w], 2 * chips[j][0] + chips[j][1], 1 - c, r2)
                _remote(got, got, send_sems.at[w * 3 + j], recv_sems.at[w * 3 + j], (x, y, c)).wait_recv()
        for cp in sends:
            cp.wait_send()

    return _pallas_call(
        body, name=name,
        in_specs=[ANY] * n, out_specs=[ANY] * n,
        out_shape=[_sds(b.shape, b.dtype) for b in bufs],
        input_output_aliases={i: i for i in range(n)},
        scratch_shapes=[pltpu.SemaphoreType.DMA((n * 3,)), pltpu.SemaphoreType.DMA((n * 3,))],
    )(*bufs)


def _rs_sibling(name, grads):
    n = len(grads)

    def body(*refs):
        ins, outs = refs[:n], refs[n:2 * n]
        send_sems, recv_sems = refs[2 * n:]
        x, y, c, _, _ = _place()
        copies = []
        for w in range(n):
            r2 = ins[w].shape[1] // 2
            copies.append(_remote(_half(ins[w], slice(None), 1 - c, r2), outs[w],
                                  send_sems.at[w], recv_sems.at[w], (x, y, 1 - c)))
        for cp in copies:
            cp.start()
        for cp in copies:
            cp.wait()

    return _pallas_call(
        body, name=name,
        in_specs=[ANY] * n, out_specs=[ANY] * n,
        out_shape=[_sds((N_CHIPS, g.shape[1] // 2, g.shape[2]), F32) for g in grads],
        scratch_shapes=[pltpu.SemaphoreType.DMA((n,)), pltpu.SemaphoreType.DMA((n,))],
    )(*grads)


def _rs_add(name, grad3, from_sibling, c_idx):
    _, r2, cols = from_sibling.shape

    def body(c_ref, g_ref, s_ref, o_ref):
        o_ref[...] = (g_ref[...] + s_ref[...]).astype(BF16)

    return _pallas_call(
        body, name=name,
        grid_spec=pltpu.PrefetchScalarGridSpec(
            num_scalar_prefetch=1, grid=(N_CHIPS,),
            in_specs=[pl.BlockSpec((None, r2, cols), lambda k, c_ref: (k, c_ref[0], 0)),
                      pl.BlockSpec((None, r2, cols), lambda k, c_ref: (k, 0, 0))],
            out_specs=pl.BlockSpec((None, r2, cols), lambda k, c_ref: (k, 0, 0))),
        out_shape=_sds(from_sibling.shape, BF16),
        compiler_params=_params(("parallel",), 48),
    )(c_idx, grad3, from_sibling)


def _split_start(name, arrays, n_sems, pairs_fn):
    n = len(arrays)

    def body(*refs):
        for send, _ in pairs_fn(refs[:n], refs[n], refs[n + 1]):
            send.start()
        refs[-1][...] = jnp.zeros_like(refs[-1])

    out = _pallas_call(
        body, name=name,
        in_specs=[HBM] * n, out_specs=tuple([SEM, SEM] + [HBM] * n + [VMEM]),
        out_shape=tuple([pltpu.SemaphoreType.DMA((n_sems,))] * 2 + [pltpu.HBM(a.shape, a.dtype) for a in arrays]
                        + [_sds((8, LANES), F32)]),
        input_output_aliases={i: 2 + i for i in range(n)},
        compiler_params=pltpu.CompilerParams(has_side_effects=EFFECT),
    )(*arrays)
    return (out[0], out[1]), list(out[2:2 + n]), out[-1]


def _split_wait(name, sems, arrays, pairs_fn, after):
    n = len(arrays)

    def body(*refs):
        for send, arrival in pairs_fn(refs[:n], refs[n], refs[n + 1]):
            send.wait_send()
            arrival.wait_recv()

    return list(_pallas_call(
        body, name=name,
        in_specs=[HBM] * n + [SEM, SEM, ANY], out_specs=[HBM] * n,
        out_shape=[pltpu.HBM(a.shape, a.dtype) for a in arrays],
        input_output_aliases={i: i for i in range(n)},
        compiler_params=pltpu.CompilerParams(has_side_effects=EFFECT),
    )(*arrays, sems[0], sems[1], after))


def _forward_pairs(bufs, send_sems, recv_sems):
    x, y, c, _, chips = _place()
    pairs = []
    for w, buf in enumerate(bufs):
        r2 = buf.shape[1] // 2
        for j, (cx, cy) in enumerate(chips):
            landed, theirs = _half(buf, 2 * cx + cy, c, r2), _half(buf, 2 * cx + cy, 1 - c, r2)
            sems = (send_sems.at[w * 3 + j], recv_sems.at[w * 3 + j])
            pairs.append((_remote(landed, landed, *sems, (x, y, 1 - c)), _remote(theirs, theirs, *sems, (x, y, c))))
    return pairs


def _sibling_pairs(arrays, send_sems, recv_sems):
    x, y, c, _, _ = _place()
    n = len(arrays) // 2
    pairs = []
    for w in range(n):
        r2 = arrays[w].shape[1] // 2
        cp = _remote(_half(arrays[w], slice(None), 1 - c, r2), arrays[n + w], send_sems.at[w], recv_sems.at[w],
                     (x, y, 1 - c))
        pairs.append((cp, cp))
    return pairs


def _ici_pairs(arrays, send_sems, recv_sems):
    x, y, c, _, chips = _place()
    n = len(arrays) // 2
    pairs = []
    for w in range(n):
        for j, (cx, cy) in enumerate(chips):
            cp = _remote(arrays[w].at[2 * cx + cy], arrays[n + w].at[j],
                         send_sems.at[w * 3 + j], recv_sems.at[w * 3 + j], (cx, cy, c))
            pairs.append((cp, cp))
    return pairs


def _rs_sum(name, partials, received, place_idx):
    _, r2, cols = partials.shape
    nb = 2
    tr = r2 // nb

    def body(idx_ref, p_ref, r_ref, o_ref):
        o_ref[...] = ((p_ref[...].astype(F32) + r_ref[0].astype(F32))
                      + (r_ref[1].astype(F32) + r_ref[2].astype(F32)))

    return _pallas_call(
        body, name=name,
        grid_spec=pltpu.PrefetchScalarGridSpec(
            num_scalar_prefetch=1, grid=(nb,),
            in_specs=[pl.BlockSpec((None, tr, cols), lambda i, idx: (idx[0], i, 0)),
                      pl.BlockSpec((3, tr, cols), lambda i, idx: (0, i, 0))],
            out_specs=pl.BlockSpec((tr, cols), lambda i, idx: (idx[1] * nb + i, 0))),
        out_shape=_sds((2 * r2, cols), F32),
        compiler_params=_params(("parallel",), 48),
    )(place_idx, partials, received)


def _share_pairs(arrays, send_sems, recv_sems):
    x, y, c, _, _ = _place()
    pairs = []
    for w, arr in enumerate(arrays):
        r2 = arr.shape[0] // 2
        mine = arr.at[pl.ds(pl.multiple_of(c * r2, 8), r2), :]
        theirs = arr.at[pl.ds(pl.multiple_of((1 - c) * r2, 8), r2), :]
        sems = (send_sems.at[w], recv_sems.at[w])
        pairs.append((_remote(mine, mine, *sems, (x, y, 1 - c)), _remote(theirs, theirs, *sems, (x, y, c))))
    return pairs


def _small_allreduce(red_mix, red_ffn, red_final, red_hg, g_conv, after):
    rows = N_SMALL_ROWS
    D = red_mix.shape[1]
    H = red_hg.shape[1]

    def body(mix_ref, ffn_ref, fin_ref, hg_ref, cv_ref, after_ref, sum_ref, all_ref, in_ref, send_sems,
             recv_sems):
        in_ref[...] = jnp.zeros_like(in_ref)
        in_ref[0:1, :] = mix_ref[0:1, :]
        in_ref[1:2, :] = ffn_ref[0:1, :]
        in_ref[2:3, :] = fin_ref[0:1, :]
        gam = hg_ref[1:2, 0:HEAD_DIM]
        for h in range(1, H // HEAD_DIM):
            gam = gam + hg_ref[1:2, h * HEAD_DIM:(h + 1) * HEAD_DIM]
        in_ref[3:4, 0:HEAD_DIM] = gam
        in_ref[3:4, HEAD_DIM:2 * HEAD_DIM] = fin_ref[1:2, 0:HEAD_DIM]
        in_ref[4:5, 0:H] = hg_ref[0:1, :]
        in_ref[6:9, 0:H] = cv_ref[...]
        x, y, c, _, _ = _place()
        me = 4 * x + 2 * y + c
        all_ref[me] = in_ref[...]
        copies = []
        for m in range(1, 8):
            mx, my, mc = (m >> 2) & 1, (m >> 1) & 1, m & 1
            px, py, pc = x ^ mx, y ^ my, c ^ mc
            copies.append((_remote(in_ref, all_ref.at[me], send_sems.at[m - 1], recv_sems.at[m - 1],
                                   (px, py, pc)), 4 * px + 2 * py + pc, m))
        for cp, _, _ in copies:
            cp.start()
        for _, peer, m in copies:
            _remote(in_ref, all_ref.at[peer], send_sems.at[m - 1], recv_sems.at[m - 1],
                    (x, y, c)).wait_recv()
        for cp, _, _ in copies:
            cp.wait_send()
        total = all_ref[0]
        for d in range(1, 8):
            total = total + all_ref[d]
        sum_ref[...] = total

    return _pallas_call(
        body, name="small_allreduce", pin=False,
        in_specs=[VMEM] * 5 + [ANY], out_specs=[VMEM, VMEM],
        out_shape=[_sds((rows, D), F32), _sds((8, rows, D), F32)],
        scratch_shapes=[pltpu.VMEM((rows, D), F32), pltpu.SemaphoreType.DMA((7,)),
                        pltpu.SemaphoreType.DMA((7,))],
    )(red_mix, red_ffn, red_final, red_hg, g_conv, after)[0]


def _adamw_math(w, g, m, v):
    m = ADAM_B1 * m + (1.0 - ADAM_B1) * g
    v = ADAM_B2 * v + (1.0 - ADAM_B2) * jnp.square(g)
    m_hat = m / (1.0 - ADAM_B1 ** ADAM_STEP)
    v_hat = v / (1.0 - ADAM_B2 ** ADAM_STEP)
    delta = -ADAM_LR * (m_hat / (jnp.sqrt(v_hat) + ADAM_EPS) + ADAM_WD * w)
    return delta, m, v


def _adamw(name, g, w, m, v):
    r, cols = g.shape
    tr = r // 4

    def body(g_ref, w_ref, m_ref, v_ref, go_ref, d_ref, mo_ref, vo_ref):
        g = g_ref[...]
        go_ref[...] = g
        d_ref[...], mo_ref[...], vo_ref[...] = _adamw_math(w_ref[...], g, m_ref[...], v_ref[...])

    blk = pl.BlockSpec((tr, cols), lambda i: (i, 0))
    return _pallas_call(
        body, name=name, grid=(r // tr,),
        in_specs=[blk] * 4, out_specs=[blk] * 4, out_shape=[_sds((r, cols), F32)] * 4,
        compiler_params=_params(("parallel",), 48),
    )(g, w, m, v)


def _small_update(total, chip_idx, ws, ms, vs):
    n = len(ws)
    H = ws[1].shape[1]

    def body(idx_ref, tot_ref, *refs):
        w, m, v, outs = refs[:n], refs[n:2 * n], refs[2 * n:3 * n], refs[3 * n:]
        chip = idx_ref[0]
        p0 = _lower_bound(w[1][...])
        dl0 = p0 * (1.0 - p0) * tot_ref[4:5, 0:H]
        conv = jnp.zeros((3, LANES), F32)
        for k in range(N_CHIPS):
            conv = jnp.where(chip == k, tot_ref[6:9, k * LANES:(k + 1) * LANES], conv)
        grads = [tot_ref[0:1, :], None, tot_ref[3:4, 0:HEAD_DIM], conv, tot_ref[1:2, :], tot_ref[2:3, :]]
        for p in range(n):
            g_ref, d_ref, mo_ref, vo_ref = outs[4 * p:4 * p + 4]
            if p == 1:
                for row, g in ((slice(0, 1), dl0), (slice(1, 2), -dl0)):
                    g_ref[row, :] = g
                    d_ref[row, :], mo_ref[row, :], vo_ref[row, :] = _adamw_math(
                        w[p][row, :], g, m[p][row, :], v[p][row, :])
            else:
                g_ref[...] = grads[p]
                d_ref[...], mo_ref[...], vo_ref[...] = _adamw_math(w[p][...], grads[p], m[p][...], v[p][...])
        outs[4 * n][...] = tot_ref[3:4, HEAD_DIM:2 * HEAD_DIM]

    full = lambda a: pl.BlockSpec(a.shape, lambda i, idx: (0,) * a.ndim)
    out_shape = [_sds(w.shape, F32) for w in ws for _ in range(4)] + [_sds((1, LANES), F32)]
    return _pallas_call(
        body, name="small_update",
        grid_spec=pltpu.PrefetchScalarGridSpec(
            num_scalar_prefetch=1, grid=(1,),
            in_specs=[full(total)] + [full(a) for a in ws + ms + vs],
            out_specs=[full(s) for s in out_shape]),
        out_shape=out_shape,
    )(chip_idx, total, *ws, *ms, *vs)


def kernel(x, norm_mix_g, w_in, lower_bounds, hg_norm_g, conv_w, w_branch_a, w_branch_b, w_out, norm_ffn_g, w_ffn_gate, w_ffn_up, w_ffn_down, norm_final_g, loss_target, m_norm_mix_g, m_w_in, m_lower_bounds, m_hg_norm_g, m_conv_w, m_w_branch_a, m_w_branch_b, m_w_out, m_norm_ffn_g, m_w_ffn_gate, m_w_ffn_up, m_w_ffn_down, m_norm_final_g, v_norm_mix_g, v_w_in, v_lower_bounds, v_hg_norm_g, v_conv_w, v_w_branch_a, v_w_branch_b, v_w_out, v_norm_ffn_g, v_w_ffn_gate, v_w_ffn_up, v_w_ffn_down, v_norm_final_g):
    _, L, D = x.shape
    H = D // 2
    assert lower_bounds.shape == (2, H) and hg_norm_g.shape == (1, HEAD_DIM)
    assert conv_w.shape == (1, 3, LANES) and w_in.shape[2] * N_CHIPS == 11 * H
    x2d, target = x.reshape(L, D), loss_target.reshape(L, D)
    g_final = norm_final_g.reshape(1, D)
    chip = 2 * lax.axis_index("x") + lax.axis_index("y")
    core = lax.axis_index("c")

    tr = lambda w: jnp.transpose(w[0])
    big = [w_in[0], w_branch_a[0], w_branch_b[0], w_out[0], tr(w_ffn_gate), tr(w_ffn_up), w_ffn_down[0]]
    big_m = [m_w_in[0], m_w_branch_a[0], m_w_branch_b[0], m_w_out[0], tr(m_w_ffn_gate), tr(m_w_ffn_up),
             m_w_ffn_down[0]]
    big_v = [v_w_in[0], v_w_branch_a[0], v_w_branch_b[0], v_w_out[0], tr(v_w_ffn_gate), tr(v_w_ffn_up),
             v_w_ffn_down[0]]
    names = ["w_in", "w_branch_a", "w_branch_b", "w_out", "w_ffn_gate", "w_ffn_up", "w_ffn_down"]

    chip_idx = chip.reshape(1).astype(jnp.int32)
    placed = [(_cast_place_t if j < 3 else _cast_place)("place_" + nm, w, chip_idx)
              for j, (nm, w) in enumerate(zip(names, big))]
    conv_placed = lax.dynamic_update_slice(jnp.zeros((N_CHIPS, 3, LANES), F32), conv_w, (chip, 0, 0))
    x_i, y_i = lax.axis_index("x"), lax.axis_index("y")
    blocks = lambda *ks: jnp.stack(ks).astype(jnp.int32)
    near = lambda w, j: j < 2
    far = lambda w, j: w == 1 or j == 2
    near_sems, in_flight, _ = _gather_start("gather_start_near", [([placed[0]], set(), near)], chip_idx)
    w_in_buf = in_flight[0][0]
    h, proj = _fwd_proj_first(x2d, norm_mix_g, w_in_buf, blocks(chip))
    sems, in_flight, _ = _gather_start(
        "gather_start_rest", [([w_in_buf, conv_placed], {1}, far), (placed[1:4], set(), None),
                              (placed[4:], set(), None)], h)
    w_in_buf, conv_buf = in_flight[0]
    (w_in_buf,) = _gather_wait("gather_wait_in_near", [w_in_buf], set(), near_sems[0], h, near)
    (w_in_buf,) = _gather_forward("gather_fwd_in_near", [w_in_buf], (0, 1))
    proj = _fwd_proj_more("fwd_proj_near", h, w_in_buf, proj,
                          blocks(2 * (1 - x_i) + y_i, 2 * x_i + (1 - y_i)))
    w_in_buf, conv_all = _gather_wait("gather_wait_in_far", [w_in_buf, conv_buf], {1}, sems[0], proj, far)
    (w_int3,) = _gather_forward("gather_fwd_in_far", [w_in_buf], (2,))
    proj = _fwd_proj_more("fwd_proj_far", h, w_int3, proj, blocks(2 * (1 - x_i) + (1 - y_i)))
    w_int = w_int3.reshape(-1, D)
    conv_full = jnp.transpose(conv_all, (1, 0, 2)).reshape(3, H)
    og, o_pre, s_saved = _hgrn_fwd(proj, lower_bounds, hg_norm_g, H)
    landed = _gather_wait("gather_wait_mix", in_flight[1], set(), sems[1], og)
    fwd_sems, landed, token = _split_start("gather_fwd_mix_start", landed, 9, _forward_pairs)
    cb = _conv_fwd(proj, conv_full, H, token)
    wat3, wbt3, wout3 = _split_wait("gather_fwd_mix_wait", fwd_sems, landed, _forward_pairs, cb)
    wat, wbt, wout = wat3.reshape(D, H), wbt3.reshape(D, H), wout3.reshape(D, D)
    landed = _gather_wait("gather_wait_ffn", in_flight[2], set(), sems[2], cb)
    fwd_sems, landed, token = _split_start("gather_fwd_ffn_start", landed, 9, _forward_pairs)
    sig_a, sig_b, dm_dga, dm_dgb, merged, x1, h2 = _fwd_mix(og, cb, proj, x2d, wat, wbt, wout, norm_ffn_g,
                                                              H, token)
    wgt3, wut3, wd3 = _split_wait("gather_fwd_ffn_wait", fwd_sems, landed, _forward_pairs, h2)
    d_ff = N_CHIPS * wd3.shape[1]
    wgt, wut, wd = wgt3.reshape(d_ff, D), wut3.reshape(d_ff, D), wd3.reshape(d_ff, D)
    ffn_ds_da, ffn_ds_db, ffn_s = _fwd_ffn_up(h2, wgt, wut)
    dx2, dx2b, red_final = _fwd_down_loss(ffn_s, wd, x1, target, g_final)

    c_idx = core.reshape(1).astype(jnp.int32)
    place_idx = jnp.stack([chip, core]).astype(jnp.int32)

    def sibling_start(tag, grads):
        bufs = [lax.empty((N_CHIPS, g.shape[1] // 2, g.shape[2]), F32) for g in grads]
        return _split_start("rs_sibling_start_" + tag, list(grads) + bufs, len(grads), _sibling_pairs)

    def ici_start(tag, js, grads, from_sibling):
        partials = [_rs_add("rs_add_" + names[j], g, s, c_idx) for j, g, s in zip(js, grads, from_sibling)]
        landings = [lax.empty((3,) + p.shape[1:], BF16) for p in partials]
        return _split_start("rs_ici_start_" + tag, partials + landings, 3 * len(js), _ici_pairs)

    def ici_start_behind(tag, js, started, after):
        n = len(js)
        arrays = _split_wait("rs_sibling_wait_" + tag, started[0], started[1], _sibling_pairs, after)
        return ici_start(tag, js, arrays[:n], arrays[n:])

    def sums(tag, js, started, after):
        n = len(js)
        arrays = _split_wait("rs_ici_wait_" + tag, started[0], started[1], _ici_pairs, after)
        return [_rs_sum("rs_sum_" + names[j], p, r, place_idx) for j, p, r in zip(js, arrays[:n], arrays[n:])]

    adamw = lambda j, g: _adamw("adamw_" + names[j], g, big[j], big_m[j], big_v[j])

    shards3 = lambda g: g.reshape(N_CHIPS, d_ff // N_CHIPS, D)
    da, db = _bwd_down(dx2b, wd, ffn_ds_da, ffn_ds_db)
    g_wd = shards3(_dw_rows2("dw_ffn_down", ffn_s, dx2b))
    g_wg = shards3(_dw_rows2("dw_ffn_gate", da, h2))
    g_wu = shards3(_dw_rows2("dw_ffn_up", db, h2))
    ffn_sibling = sibling_start("ffn", [g_wg, g_wu, g_wd])
    dx1, dx1b, red_ffn = _bwd_ffn_dh(da, db, wgt, wut, x1, dx2, norm_ffn_g, ffn_sibling[2])
    ffn_ici = ici_start_behind("ffn", [4, 5, 6], ffn_sibling, dx1b)
    dya, dyb, d_gates, d_o, d_cb = _bwd_mix(dx1b, sig_a, sig_b, dm_dga, dm_dgb, wat, wbt, wout, H, ffn_ici[2])
    g_wout = _dw_rows("dw_out", merged, dx1b)
    g_wa = _dw_cols("dw_branch_a", og, dya, D // N_CHIPS)
    g_wb = _dw_cols("dw_branch_b", cb, dyb, D // N_CHIPS)
    mix_sibling = sibling_start("mix", [g_wa, g_wb, g_wout])
    d_hgrn, red_hg = _hgrn_bwd(proj, lower_bounds, hg_norm_g, o_pre, d_o, s_saved, H, mix_sibling[2])
    mix_ici = ici_start_behind("mix", [1, 2, 3], mix_sibling, d_hgrn)
    dcg, dbg, dxb, g_conv = _conv_bwd(proj, conv_full, d_cb, H, mix_ici[2])
    dproj = [d_hgrn, dcg, dbg, dxb, d_gates]
    g_win = _dw_in(h, dproj, w_int3.shape[1])
    in_sibling = sibling_start("in", [g_win])
    halves = sums("mix", [1, 2, 3], mix_ici, in_sibling[2]) + sums("ffn", [4, 5, 6], ffn_ici, in_sibling[2])
    rest_share = _split_start("rs_share_start_rest", halves, len(halves), _share_pairs)
    in_ici = ici_start_behind("in", [0], in_sibling, rest_share[2])
    grad_x, red_mix = _bwd_in(dproj, w_int, x2d, dx1, norm_mix_g, in_ici[2])
    in_share = _split_start("rs_share_start_in", sums("in", [0], in_ici, grad_x), 1, _share_pairs)
    total = _small_allreduce(red_mix, red_ffn, red_final, red_hg, g_conv, in_share[2])
    rest_grads = _split_wait("rs_share_wait_rest", rest_share[0], rest_share[1], _share_pairs, total)
    big_out = [None] + [adamw(j, g) for j, g in zip(range(1, 7), rest_grads)]
    in_grad = _split_wait("rs_share_wait_in", in_share[0], in_share[1], _share_pairs, big_out[6][0])
    big_out[0] = adamw(0, in_grad[0])

    def smalls(mix, lb, hg, cw, ffn, fin):
        return [mix, lb, hg, cw[0], ffn, fin.reshape(1, D)]

    small_out = _small_update(
        total, chip_idx,
        smalls(norm_mix_g, lower_bounds, hg_norm_g, conv_w, norm_ffn_g, norm_final_g),
        smalls(m_norm_mix_g, m_lower_bounds, m_hg_norm_g, m_conv_w, m_norm_ffn_g, m_norm_final_g),
        smalls(v_norm_mix_g, v_lower_bounds, v_hg_norm_g, v_conv_w, v_norm_ffn_g, v_norm_final_g))

    def outputs(i):
        big_i = [big_out[j][i] for j in range(7)]
        mix, lb, hg, cw, ffn, fin = [small_out[4 * p + i] for p in range(6)]
        return [mix, big_i[0][None], lb, hg, cw[None], big_i[1][None], big_i[2][None], big_i[3][None], ffn,
                big_i[4].T[None], big_i[5].T[None], big_i[6][None], fin.reshape(D)]

    outs = [small_out[24][0, 0], grad_x.reshape(1, L, D)]
    for i in range(4):
        outs += outputs(i)
    return tuple(outs)
```

```python
import functools

import jax
import jax.numpy as jnp
from jax import lax
from jax.experimental import pallas as pl
from jax.experimental.pallas import tpu as pltpu

F32 = jnp.float32
BF16 = jnp.bfloat16
EPS = 1e-6
CHUNK = 32
HEAD_DIM = 128
LANES = 128
N_CHIPS = 4
N_SMALL_ROWS = 16

ADAM_LR = 0.001
ADAM_B1 = 0.9
ADAM_B2 = 0.999
ADAM_EPS = 1e-08
ADAM_WD = 0.01
ADAM_STEP = 10

MESH = pl.DeviceIdType.MESH
ANY = pl.BlockSpec(memory_space=pl.ANY)
VMEM = pl.BlockSpec(memory_space=pltpu.VMEM)
HBM = pl.BlockSpec(memory_space=pltpu.HBM)
SEM = pl.BlockSpec(memory_space=pltpu.SEMAPHORE)
EFFECT = pltpu.SideEffectType.DATAFLOW_SIDE_EFFECTING


def _sds(shape, dtype):
    return jax.ShapeDtypeStruct(shape, dtype)


def _pallas_call(body, pin=True, **kwargs):
    if not pin:
        return pl.pallas_call(body, **kwargs)
    in_hbm = lambda s: pltpu.HBM(s.shape, s.dtype) if isinstance(s, jax.ShapeDtypeStruct) else s
    kwargs["out_shape"] = jax.tree.map(in_hbm, kwargs["out_shape"])
    call = pl.pallas_call(body, **kwargs)

    def run(*args):
        return call(*[pltpu.with_memory_space_constraint(a, pltpu.HBM) if a.dtype in (F32, BF16) else a
                      for a in args])

    return run


def _params(semantics, vmem_mb):
    return pltpu.CompilerParams(dimension_semantics=semantics, vmem_limit_bytes=vmem_mb << 20)


def _nn(a, b):
    return lax.dot_general(a, b, (((1,), (0,)), ((), ())), preferred_element_type=F32)


def _nt(a, b):
    return lax.dot_general(a, b, (((1,), (1,)), ((), ())), preferred_element_type=F32)


def _tn(a, b):
    return lax.dot_general(a, b, (((0,), (0,)), ((), ())), preferred_element_type=F32)


def _sigmoid(x):
    return jax.nn.sigmoid(x)


def _rms_stats(x):
    r = lax.rsqrt(jnp.mean(x * x, axis=-1, keepdims=True) + EPS)
    return r, x * r


def _rms_bwd(dxh, xh, r):
    return r * (dxh - xh * jnp.mean(dxh * xh, axis=-1, keepdims=True))


def _fwd_proj_first(x, g_mix, w_int3, block):
    L, D = x.shape
    tn = w_int3.shape[1]
    tm = min(L, 1024)

    def body(blk_ref, x_ref, g_ref, w_ref, h_ref, p_ref):
        _, xh = _rms_stats(x_ref[...])
        h = (xh * g_ref[...]).astype(BF16)
        h_ref[...] = h
        p_ref[...] = _nt(h, w_ref[...])

    return _pallas_call(
        body, name="fwd_proj_own",
        grid_spec=pltpu.PrefetchScalarGridSpec(
            num_scalar_prefetch=1, grid=(L // tm,),
            in_specs=[pl.BlockSpec((tm, D), lambda i, blk: (i, 0)),
                      pl.BlockSpec((1, D), lambda i, blk: (0, 0)),
                      pl.BlockSpec((None, tn, D), lambda i, blk: (blk[0], 0, 0))],
            out_specs=[pl.BlockSpec((tm, D), lambda i, blk: (i, 0)),
                       pl.BlockSpec((tm, tn), lambda i, blk: (i, blk[0]))]),
        out_shape=[_sds((L, D), BF16), _sds((L, N_CHIPS * tn), F32)],
        compiler_params=_params(("parallel",), 48),
    )(block, x, g_mix, w_int3)


def _fwd_proj_more(name, h, w_int3, proj, blocks):
    L, D = h.shape
    tn = w_int3.shape[1]
    tm = min(L, 1024)

    def body(blk_ref, h_ref, w_ref, proj_ref, p_ref):
        p_ref[...] = _nt(h_ref[...], w_ref[...])

    return _pallas_call(
        body, name=name,
        grid_spec=pltpu.PrefetchScalarGridSpec(
            num_scalar_prefetch=1, grid=(L // tm, blocks.shape[0]),
            in_specs=[pl.BlockSpec((tm, D), lambda i, j, blk: (i, 0)),
                      pl.BlockSpec((None, tn, D), lambda i, j, blk: (blk[j], 0, 0)), ANY],
            out_specs=pl.BlockSpec((tm, tn), lambda i, j, blk: (i, blk[j]))),
        out_shape=_sds(proj.shape, proj.dtype),
        input_output_aliases={3: 0},
        compiler_params=_params(("parallel", "arbitrary"), 48),
    )(blocks, h, w_int3, proj)


def _lower_bound(lbp):
    l0, l1 = lbp[0:1, :], lbp[1:2, :]
    m = jnp.maximum(l0, l1)
    e0, e1 = jnp.exp(l0 - m), jnp.exp(l1 - m)
    return e0 / (e0 + e1)


def _chunk_masks(n):
    ri = lax.broadcasted_iota(jnp.int32, (n, n), 0)
    ci = lax.broadcasted_iota(jnp.int32, (n, n), 1)
    same = (ri // CHUNK) == (ci // CHUNK)
    return same & (ci <= ri), same & (ri <= ci)


def _chunk_sum_matrices(n):
    ri = lax.broadcasted_iota(jnp.int32, (n, n), 0)
    ci = lax.broadcasted_iota(jnp.int32, (n, n), 1)
    same = (ri // CHUNK) == (ci // CHUNK)
    ones = lambda m: jnp.where(m, 1.0, 0.0).astype(BF16)
    return dict(prefix=ones(same & (ci <= ri)), suffix=ones(same & (ci >= ri)),
                anchor=ones(same & ((ci % CHUNK) < CHUNK // 2)), total=ones(same))


def _chunk_sums(matrices, x):
    hi = x.astype(BF16)
    lo = (x - hi.astype(F32)).astype(BF16)
    return [_nn(m, hi) + _nn(m, lo) for m in matrices]


def _hgrn_prep(q_raw, f_raw, lb, sums):
    sig = _sigmoid(f_raw)
    f = lb + (1.0 - lb) * sig
    b, a, bl = _chunk_sums([sums["prefix"], sums["anchor"], sums["total"]], jnp.log(f))
    sq = _sigmoid(q_raw)
    q = q_raw * sq * (HEAD_DIM ** -0.5)
    return dict(sig=sig, f=f, k=1.0 - f, b=b, a=a, bl=bl, sq=sq, q=q)


def _hgrn_fwd(proj, lower_bounds, gamma, H):
    L = proj.shape[0]
    nh = H // HEAD_DIM
    TL = min(L, 256)
    nc = TL // CHUNK

    def body(q_ref, f_ref, v_ref, g_ref, lbp_ref, gam_ref, og_ref, o_ref, s_ref, st_ref):
        @pl.when(pl.program_id(0) == 0)
        def _():
            st_ref[...] = jnp.zeros_like(st_ref)

        lb = _lower_bound(lbp_ref[...])
        gam = gam_ref[...]
        mask, _ = _chunk_masks(TL)
        sums = _chunk_sum_matrices(TL)
        rowc = lax.broadcasted_iota(jnp.int32, (TL, HEAD_DIM), 0) // CHUNK
        for h in range(nh):
            hs = slice(h * HEAD_DIM, (h + 1) * HEAD_DIM)
            p = _hgrn_prep(q_ref[:, hs], f_ref[:, hs], lb[:, hs], sums)
            v = v_ref[:, hs]
            vb = v.astype(BF16)
            vt = v.T.astype(BF16)
            q_hat = (p["q"] * jnp.exp(p["b"] - p["a"])).astype(BF16)
            k_hat = (p["k"] * jnp.exp(p["a"] - p["b"])).astype(BF16)
            q_in = (p["q"] * jnp.exp(p["b"])).astype(BF16)
            k_out = (p["k"] * jnp.exp(p["bl"] - p["b"])).astype(BF16)
            dec = jnp.exp(p["bl"])
            att = jnp.where(mask, _nt(q_hat, k_hat), 0.0).astype(BF16)
            o_intra = _nn(att, vb)
            st = st_ref[h]
            for c in range(nc):
                rs = slice(c * CHUNK, (c + 1) * CHUNK)
                stb = st.astype(BF16)
                s_ref[c, h] = stb
                o_ref[rs, hs] = o_intra[rs] + _nt(q_in[rs], stb)
                k_c = jnp.where(rowc == c, k_out, jnp.zeros_like(k_out))
                st = st * dec[c * CHUNK:c * CHUNK + 1, :] + _nn(vt, k_c)
            st_ref[h] = st
            o = o_ref[:, hs]
            _, xh = _rms_stats(o)
            gr = g_ref[:, hs]
            og_ref[:, hs] = (xh * gam * (gr * _sigmoid(gr))).astype(BF16)

    col = lambda k: pl.BlockSpec((TL, H), lambda i, k=k: (i, k))
    return _pallas_call(
        body, name="hgrn_fwd", grid=(L // TL,),
        in_specs=[col(0), col(1), col(2), col(3),
                  pl.BlockSpec(lower_bounds.shape, lambda i: (0, 0)),
                  pl.BlockSpec(gamma.shape, lambda i: (0, 0))],
        out_specs=[pl.BlockSpec((TL, H), lambda i: (i, 0)),
                   pl.BlockSpec((TL, H), lambda i: (i, 0)),
                   pl.BlockSpec((nc, nh, HEAD_DIM, HEAD_DIM), lambda i: (i, 0, 0, 0))],
        out_shape=[_sds((L, H), BF16), _sds((L, H), F32),
                   _sds((L // CHUNK, nh, HEAD_DIM, HEAD_DIM), BF16)],
        scratch_shapes=[pltpu.VMEM((nh, HEAD_DIM, HEAD_DIM), F32)],
        compiler_params=_params(("arbitrary",), 48),
    )(proj, proj, proj, proj, lower_bounds, gamma)


def _hgrn_bwd(proj, lower_bounds, gamma, o_pre, d_out, s_saved, H, after):
    L = proj.shape[0]
    nh = H // HEAD_DIM
    TL = min(L, 256)
    nc = TL // CHUNK
    nt = L // TL

    def body(q_ref, f_ref, v_ref, g_ref, lbp_ref, gam_ref, o_ref, d_ref, s_ref, after_ref,
             dp_ref, red_ref, dst_ref, dsall_ref, tmp_ref):
        @pl.when(pl.program_id(0) == 0)
        def _():
            dst_ref[...] = jnp.zeros_like(dst_ref)
            red_ref[...] = jnp.zeros_like(red_ref)

        lb = _lower_bound(lbp_ref[...])
        gam = gam_ref[...]
        mask, mask_t = _chunk_masks(TL)
        sums = _chunk_sum_matrices(TL)
        rowc = lax.broadcasted_iota(jnp.int32, (TL, HEAD_DIM), 0) // CHUNK
        for h in range(nh):
            hs = slice(h * HEAD_DIM, (h + 1) * HEAD_DIM)
            qr, gr, lbh = q_ref[:, hs], g_ref[:, hs], lb[:, hs]
            p = _hgrn_prep(qr, f_ref[:, hs], lbh, sums)
            vb = v_ref[:, hs].astype(BF16)
            eba, eab = jnp.exp(p["b"] - p["a"]), jnp.exp(p["a"] - p["b"])
            eb, elb = jnp.exp(p["b"]), jnp.exp(p["bl"] - p["b"])
            dec = jnp.exp(p["bl"])
            q_hat, k_hat = p["q"] * eba, p["k"] * eab
            q_in, k_out = p["q"] * eb, p["k"] * elb
            q_hat_b, k_hat_b = q_hat.astype(BF16), k_hat.astype(BF16)
            q_in_b, k_out_b = q_in.astype(BF16), k_out.astype(BF16)

            o, dout = o_ref[:, hs], d_ref[:, hs]
            sg = _sigmoid(gr)
            r, xh = _rms_stats(o)
            hq, hf, hv, hg = [slice(k * H + h * HEAD_DIM, k * H + (h + 1) * HEAD_DIM) for k in range(4)]
            dp_ref[:, hg] = (dout * (xh * gam) * (sg * (1.0 + gr * (1.0 - sg)))).astype(BF16)
            dn = dout * (gr * sg)
            red_ref[1:2, hs] += jnp.sum(dn * xh, axis=0, keepdims=True)
            do = _rms_bwd(dn * gam, xh, r)
            dob = do.astype(BF16)
            dot_b = do.T.astype(BF16)

            att_t = jnp.where(mask_t, _nt(k_hat_b, q_hat_b), 0.0).astype(BF16)
            dv_intra = _nn(att_t, dob)
            datt = jnp.where(mask, _nt(dob, vb), 0.0).astype(BF16)
            dqh = _nn(datt, k_hat_b)
            datt_t = jnp.where(mask_t, _nt(vb, dob), 0.0).astype(BF16)
            dkh = _nn(datt_t, q_hat_b)

            dst = dst_ref[h]
            for c in reversed(range(nc)):
                dsall_ref[c] = dst
                q_c = jnp.where(rowc == c, q_in_b, jnp.zeros_like(q_in_b))
                dst = dst * dec[c * CHUNK:c * CHUNK + 1, :] + _nn(dot_b, q_c)
            dst_ref[h] = dst
            for c in range(nc):
                rs = slice(c * CHUNK, (c + 1) * CHUNK)
                ds_c = dsall_ref[c]
                dsb = ds_c.astype(BF16)
                st_prev = s_ref[c, h]
                tmp_ref[0, rs, :] = _nt(k_out_b[rs], dsb)
                tmp_ref[1, rs, :] = _nn(vb[rs], dsb)
                tmp_ref[2, rs, :] = _nn(dob[rs], st_prev)
                ddec = jnp.sum(ds_c * st_prev.astype(F32), axis=0, keepdims=True)
                tmp_ref[3, rs, :] = jnp.broadcast_to(ddec * dec[c * CHUNK:c * CHUNK + 1, :],
                                                     (CHUNK, HEAD_DIM))
            dko, dqi = tmp_ref[1], tmp_ref[2]
            dq = dqh * eba + dqi * eb
            dk = dkh * eab + dko * elb
            tko = dko * k_out
            db = dqh * q_hat - dkh * k_hat + dqi * q_in - tko
            dlog = (_chunk_sums([sums["suffix"]], db)[0] + _chunk_sums([sums["total"]], tko)[0]
                    + tmp_ref[3])
            df = dlog / p["f"] - dk
            sig = p["sig"]
            red_ref[0:1, hs] += jnp.sum(df * (1.0 - sig), axis=0, keepdims=True)
            dp_ref[:, hf] = (df * (1.0 - lbh) * sig * (1.0 - sig)).astype(BF16)
            sq = p["sq"]
            dp_ref[:, hq] = (dq * (HEAD_DIM ** -0.5) * (sq * (1.0 + qr * (1.0 - sq)))).astype(BF16)
            dp_ref[:, hv] = (dv_intra + tmp_ref[0]).astype(BF16)

    col = lambda k: pl.BlockSpec((TL, H), lambda i, k=k: (nt - 1 - i, k))
    rev = pl.BlockSpec((TL, H), lambda i: (nt - 1 - i, 0))
    return _pallas_call(
        body, name="hgrn_bwd", grid=(nt,),
        in_specs=[col(0), col(1), col(2), col(3),
                  pl.BlockSpec(lower_bounds.shape, lambda i: (0, 0)),
                  pl.BlockSpec(gamma.shape, lambda i: (0, 0)),
                  rev, rev,
                  pl.BlockSpec((nc, nh, HEAD_DIM, HEAD_DIM), lambda i: (nt - 1 - i, 0, 0, 0)), ANY],
        out_specs=[pl.BlockSpec((TL, 4 * H), lambda i: (nt - 1 - i, 0)), pl.BlockSpec((8, H), lambda i: (0, 0))],
        out_shape=[_sds((L, 4 * H), BF16), _sds((8, H), F32)],
        scratch_shapes=[pltpu.VMEM((nh, HEAD_DIM, HEAD_DIM), F32),
                        pltpu.VMEM((nc, HEAD_DIM, HEAD_DIM), F32),
                        pltpu.VMEM((4, TL, HEAD_DIM), F32)],
        compiler_params=_params(("arbitrary",), 48),
    )(proj, proj, proj, proj, lower_bounds, gamma, o_pre, d_out, s_saved, after)


def _shift_down(u, s, row):
    return jnp.where(row >= s, pltpu.roll(u, s, 0), 0.0)


def _shift_up(u, s, row):
    n = u.shape[0]
    return jnp.where(row < n - s, pltpu.roll(u, n - s, 0), 0.0)


def _conv_specs(L, H):
    per = H // LANES
    return [pl.BlockSpec((L, LANES), lambda j, o=o: (0, o * per + j)) for o in (4, 5, 6)]


def _conv_fwd(proj, conv_w, H, after):
    L = proj.shape[0]

    def body(c_ref, b_ref, x_ref, w_ref, after_ref, o_ref):
        row = lax.broadcasted_iota(jnp.int32, (L, LANES), 0)
        u = c_ref[...] * x_ref[...]
        w = w_ref[...]
        y = w[0:1] * _shift_down(u, 2, row) + w[1:2] * _shift_down(u, 1, row) + w[2:3] * u
        o_ref[...] = (b_ref[...] * y).astype(BF16)

    return _pallas_call(
        body, name="conv_fwd", grid=(H // LANES,),
        in_specs=_conv_specs(L, H) + [pl.BlockSpec((3, LANES), lambda j: (0, j)), ANY],
        out_specs=pl.BlockSpec((L, LANES), lambda j: (0, j)),
        out_shape=_sds((L, H), BF16),
        compiler_params=_params(("parallel",), 48),
    )(proj, proj, proj, conv_w, after)


def _conv_bwd(proj, conv_w, dcb, H, after):
    L = proj.shape[0]

    def body(c_ref, b_ref, x_ref, w_ref, d_ref, after_ref, dc_ref, db_ref, dx_ref, dw_ref):
        row = lax.broadcasted_iota(jnp.int32, (L, LANES), 0)
        cg, xb = c_ref[...], x_ref[...]
        u = cg * xb
        u1, u2 = _shift_down(u, 1, row), _shift_down(u, 2, row)
        w = w_ref[...]
        y = w[0:1] * u2 + w[1:2] * u1 + w[2:3] * u
        d = d_ref[...]
        db_ref[...] = (d * y).astype(BF16)
        dy = d * b_ref[...]
        du = w[2:3] * dy + w[1:2] * _shift_up(dy, 1, row) + w[0:1] * _shift_up(dy, 2, row)
        dw_ref[0:1, :] = jnp.sum(dy * u2, axis=0, keepdims=True)
        dw_ref[1:2, :] = jnp.sum(dy * u1, axis=0, keepdims=True)
        dw_ref[2:3, :] = jnp.sum(dy * u, axis=0, keepdims=True)
        dc_ref[...] = (du * xb).astype(BF16)
        dx_ref[...] = (du * cg).astype(BF16)

    blk = pl.BlockSpec((L, LANES), lambda j: (0, j))
    return _pallas_call(
        body, name="conv_bwd", grid=(H // LANES,),
        in_specs=_conv_specs(L, H) + [pl.BlockSpec((3, LANES), lambda j: (0, j)), blk, ANY],
        out_specs=[blk, blk, blk, pl.BlockSpec((3, LANES), lambda j: (0, j))],
        out_shape=[_sds((L, H), BF16)] * 3 + [_sds((3, H), F32)],
        compiler_params=_params(("parallel",), 56),
    )(proj, proj, proj, conv_w, dcb, after)


def _gate_specs(tm, H):
    return [pl.BlockSpec((tm, H), lambda i, k=k: (i, k)) for k in (7, 8, 9, 10)]


def _fwd_mix(og, cb, proj, x, wat, wbt, wout, g_ffn, H, after):
    L, D = x.shape
    tm = min(L, 512)

    def body(o_ref, cb_ref, ga0, ga1, gb0, gb1, x_ref, wa_ref, wb_ref, wo_ref, g_ref, after_ref,
             sa_ref, sb_ref, ta_ref, tb_ref, m_ref, x1_ref, h2_ref):
        ya, yb = _nt(o_ref[...], wa_ref[...]), _nt(cb_ref[...], wb_ref[...])
        for k, (gar, gbr) in enumerate(((ga0, gb0), (ga1, gb1))):
            cs = slice(k * H, (k + 1) * H)
            sa, sb = _sigmoid(gar[...]), _sigmoid(gbr[...])
            ma, mb = sa * ya[:, cs], sb * yb[:, cs]
            m_ref[:, cs] = (ma + mb).astype(BF16)
            sa_ref[:, cs] = sa.astype(BF16)
            sb_ref[:, cs] = sb.astype(BF16)
            ta_ref[:, cs] = (ma * (1.0 - sa)).astype(BF16)
            tb_ref[:, cs] = (mb * (1.0 - sb)).astype(BF16)
        x1 = x_ref[...] + _nn(m_ref[...], wo_ref[...])
        x1_ref[...] = x1
        _, xh = _rms_stats(x1)
        h2_ref[...] = (xh * g_ref[...]).astype(BF16)

    row = lambda w: pl.BlockSpec((tm, w), lambda i: (i, 0))
    full = lambda a: pl.BlockSpec(a.shape, lambda i: (0,) * a.ndim)
    return _pallas_call(
        body, name="fwd_mix", grid=(L // tm,),
        in_specs=[row(H), row(H)] + _gate_specs(tm, H) + [row(D), full(wat), full(wbt), full(wout),
                                                           full(g_ffn), ANY],
        out_specs=[row(D)] * 7,
        out_shape=[_sds((L, D), BF16)] * 5 + [_sds((L, D), F32), _sds((L, D), BF16)],
        compiler_params=_params(("parallel",), 56),
    )(og, cb, proj, proj, proj, proj, x, wat, wbt, wout, g_ffn, after)


def _bwd_mix(dx1b, sig_a, sig_b, dm_dga, dm_dgb, wat, wbt, wout, H, after):
    L, D = dx1b.shape
    tm = min(L, 512)

    def body(dx_ref, sa_ref, sb_ref, ta_ref, tb_ref, wa_ref, wb_ref, wo_ref, after_ref,
             dya_ref, dyb_ref, dgate_ref, do_ref, dcb_ref):
        dm = _nt(dx_ref[...], wo_ref[...])
        dgate_ref[:, 0:D] = (dm * ta_ref[...].astype(F32)).astype(BF16)
        dgate_ref[:, D:2 * D] = (dm * tb_ref[...].astype(F32)).astype(BF16)
        dya_ref[...] = (dm * sa_ref[...].astype(F32)).astype(BF16)
        dyb_ref[...] = (dm * sb_ref[...].astype(F32)).astype(BF16)
        do_ref[...] = _nn(dya_ref[...], wa_ref[...])
        dcb_ref[...] = _nn(dyb_ref[...], wb_ref[...])

    row = lambda w: pl.BlockSpec((tm, w), lambda i: (i, 0))
    full = lambda a: pl.BlockSpec(a.shape, lambda i: (0,) * a.ndim)
    return _pallas_call(
        body, name="bwd_mix", grid=(L // tm,),
        in_specs=[row(D)] * 5 + [full(wat), full(wbt), full(wout), ANY],
        out_specs=[row(D), row(D), row(2 * D), row(H), row(H)],
        out_shape=[_sds((L, D), BF16)] * 2 + [_sds((L, 2 * D), BF16)] + [_sds((L, H), F32)] * 2,
        compiler_params=_params(("parallel",), 56),
    )(dx1b, sig_a, sig_b, dm_dga, dm_dgb, wat, wbt, wout, after)


def _fwd_ffn_up(h2, wgt, wut):
    L, D = h2.shape
    F = wgt.shape[0]
    tn = F // 2
    tm = min(L, 512)

    def body(h_ref, wg_ref, wu_ref, sa_ref, sb_ref, s_ref):
        h = h_ref[...]
        a, b = _nt(h, wg_ref[...]), _nt(h, wu_ref[...])
        sg = _sigmoid(a)
        silu = a * sg
        sa_ref[...] = (b * sg * (1.0 + a * (1.0 - sg))).astype(BF16)
        sb_ref[...] = silu.astype(BF16)
        s_ref[...] = (silu * b).astype(BF16)

    wspec = pl.BlockSpec((tn, D), lambda j, i: (j, 0))
    ospec = pl.BlockSpec((tm, tn), lambda j, i: (i, j))
    return _pallas_call(
        body, name="fwd_ffn_up", grid=(2, L // tm),
        in_specs=[pl.BlockSpec((tm, D), lambda j, i: (i, 0)), wspec, wspec],
        out_specs=[ospec] * 3,
        out_shape=[_sds((L, F), BF16)] * 3,
        compiler_params=_params(("parallel", "parallel"), 48),
    )(h2, wgt, wut)


def _fwd_down_loss(s, wd, x1, target, g_final):
    L, D = x1.shape
    F = wd.shape[0]
    tm = min(L, 256)

    def body(s_ref, wd_ref, x1_ref, t_ref, g_ref, dx_ref, dxb_ref, red_ref):
        @pl.when(pl.program_id(0) == 0)
        def _():
            red_ref[...] = jnp.zeros_like(red_ref)

        g = g_ref[...]
        r, xh = _rms_stats(x1_ref[...] + _nn(s_ref[...], wd_ref[...]))
        e = xh * g - t_ref[...]
        dy = e * (1.0 / D)
        dx = _rms_bwd(dy * g, xh, r)
        dx_ref[...] = dx
        dxb_ref[...] = dx.astype(BF16)
        red_ref[0:1, :] += jnp.sum(dy * xh, axis=0, keepdims=True)
        red_ref[1:2, :] += jnp.broadcast_to(0.5 * jnp.sum(e * e) * (1.0 / D), (1, D))

    row = pl.BlockSpec((tm, D), lambda i: (i, 0))
    return _pallas_call(
        body, name="fwd_down_loss", grid=(L // tm,),
        in_specs=[pl.BlockSpec((tm, F), lambda i: (i, 0)), pl.BlockSpec((F, D), lambda i: (0, 0)),
                  row, row, pl.BlockSpec((1, D), lambda i: (0, 0))],
        out_specs=[row, row, pl.BlockSpec((8, D), lambda i: (0, 0))],
        out_shape=[_sds((L, D), F32), _sds((L, D), BF16), _sds((8, D), F32)],
        compiler_params=_params(("arbitrary",), 48),
    )(s, wd, x1, target, g_final)


def _bwd_down(dx2b, wd, s_a, s_b):
    L, D = dx2b.shape
    F = wd.shape[0]
    tn = F // 2
    tm = min(L, 512)

    def body(dx_ref, wd_ref, sa_ref, sb_ref, da_ref, db_ref):
        ds = _nt(dx_ref[...], wd_ref[...])
        da_ref[...] = (ds * sa_ref[...].astype(F32)).astype(BF16)
        db_ref[...] = (ds * sb_ref[...].astype(F32)).astype(BF16)

    ospec = pl.BlockSpec((tm, tn), lambda j, i: (i, j))
    return _pallas_call(
        body, name="bwd_down", grid=(2, L // tm),
        in_specs=[pl.BlockSpec((tm, D), lambda j, i: (i, 0)),
                  pl.BlockSpec((tn, D), lambda j, i: (j, 0)), ospec, ospec],
        out_specs=[ospec] * 2,
        out_shape=[_sds((L, F), BF16)] * 2,
        compiler_params=_params(("parallel", "parallel"), 48),
    )(dx2b, wd, s_a, s_b)


def _bwd_ffn_dh(da, db, wgt, wut, x1, dx2, g_ffn, after):
    L, D = x1.shape
    F = wgt.shape[0]
    tm = min(L, 256)

    def body(da_ref, db_ref, wg_ref, wu_ref, x1_ref, dx2_ref, g_ref, after_ref, dx_ref, dxb_ref, red_ref):
        @pl.when(pl.program_id(0) == 0)
        def _():
            red_ref[...] = jnp.zeros_like(red_ref)

        dh = _nn(da_ref[...], wg_ref[...]) + _nn(db_ref[...], wu_ref[...])
        r, xh = _rms_stats(x1_ref[...])
        red_ref[0:1, :] += jnp.sum(dh * xh, axis=0, keepdims=True)
        dx = dx2_ref[...] + _rms_bwd(dh * g_ref[...], xh, r)
        dx_ref[...] = dx
        dxb_ref[...] = dx.astype(BF16)

    row = pl.BlockSpec((tm, D), lambda i: (i, 0))
    aspec = pl.BlockSpec((tm, F), lambda i: (i, 0))
    wspec = pl.BlockSpec((F, D), lambda i: (0, 0))
    return _pallas_call(
        body, name="bwd_ffn_dh", grid=(L // tm,),
        in_specs=[aspec, aspec, wspec, wspec, row, row, pl.BlockSpec((1, D), lambda i: (0, 0)), ANY],
        out_specs=[row, row, pl.BlockSpec((8, D), lambda i: (0, 0))],
        out_shape=[_sds((L, D), F32), _sds((L, D), BF16), _sds((8, D), F32)],
        compiler_params=_params(("arbitrary",), 56),
    )(da, db, wgt, wut, x1, dx2, g_ffn, after)


def _piece_offsets(pieces):
    offsets, total = [], 0
    for p in pieces:
        offsets.append(total)
        total += p.shape[1]
    return offsets, total


def _bwd_in(pieces, w_int, x, dx1, g_mix, after):
    L, D = x.shape
    N = w_int.shape[0]
    tm = min(L, 256)
    n = len(pieces)
    offsets, total = _piece_offsets(pieces)
    assert total == N

    def body(*refs):
        piece_refs = refs[:n]
        w_ref, x_ref, dx1_ref, g_ref, after_ref, dx_ref, red_ref, dp_ref = refs[n:]

        @pl.when(pl.program_id(0) == 0)
        def _():
            red_ref[...] = jnp.zeros_like(red_ref)

        for p_ref, off in zip(piece_refs, offsets):
            dp_ref[:, off:off + p_ref.shape[1]] = p_ref[...]
        dh = _nn(dp_ref[...], w_ref[...])
        r, xh = _rms_stats(x_ref[...])
        red_ref[0:1, :] += jnp.sum(dh * xh, axis=0, keepdims=True)
        dx_ref[...] = dx1_ref[...] + _rms_bwd(dh * g_ref[...], xh, r)

    row = pl.BlockSpec((tm, D), lambda i: (i, 0))
    return _pallas_call(
        body, name="bwd_in", grid=(L // tm,),
        in_specs=[pl.BlockSpec((tm, p.shape[1]), lambda i: (i, 0)) for p in pieces]
        + [pl.BlockSpec((N, D), lambda i: (0, 0)), row, row, pl.BlockSpec((1, D), lambda i: (0, 0)), ANY],
        out_specs=[row, pl.BlockSpec((8, D), lambda i: (0, 0))],
        out_shape=[_sds((L, D), F32), _sds((8, D), F32)],
        scratch_shapes=[pltpu.VMEM((tm, N), BF16)],
        compiler_params=_params(("arbitrary",), 56),
    )(*pieces, w_int, x, dx1, g_mix, after)


def _dw_in(h, pieces, n_cols):
    L, D = h.shape
    tk = min(L, TK_TOKENS // 2)
    n = len(pieces)
    offsets, total = _piece_offsets(pieces)
    assert total == N_CHIPS * n_cols
    plan = []
    for j in range(N_CHIPS):
        lo, hi = j * n_cols, (j + 1) * n_cols
        segments = []
        for p, off in enumerate(offsets):
            a, b = max(lo, off), min(hi, off + pieces[p].shape[1])
            if a < b:
                segments.append((p, a - off, b - a, a - lo))
        plan.append(segments)

    def body(*refs):
        h_ref, piece_refs, o_ref, b_ref = refs[0], refs[1:1 + n], refs[1 + n], refs[2 + n]
        j, k = pl.program_id(0), pl.program_id(1)
        for jj in range(N_CHIPS):
            @pl.when(j == jj)
            def _(jj=jj):
                for p, start, width, at in plan[jj]:
                    b_ref[:, at:at + width] = piece_refs[p][:, start:start + width]

        part = _tn(h_ref[...], b_ref[...])

        @pl.when(k == 0)
        def _():
            o_ref[...] = part

        @pl.when(k > 0)
        def _():
            o_ref[...] += part

    def piece_spec(p):
        used = [j for j in range(N_CHIPS) if any(seg[0] == p for seg in plan[j])]

        def index(j, k):
            in_use = functools.reduce(jnp.logical_or, [j == u for u in used])
            return (jnp.where(in_use, k, 0), 0)

        return pl.BlockSpec((tk, pieces[p].shape[1]), index)

    return _pallas_call(
        body, name="dw_in", grid=(N_CHIPS, L // tk),
        in_specs=[pl.BlockSpec((tk, D), lambda j, k: (k, 0))] + [piece_spec(p) for p in range(n)],
        out_specs=pl.BlockSpec((None, D, n_cols), lambda j, k: (j, 0, 0)),
        out_shape=_sds((N_CHIPS, D, n_cols), F32),
        scratch_shapes=[pltpu.VMEM((tk, n_cols), BF16)],
        compiler_params=_params(("parallel", "arbitrary"), 56),
    )(h, *pieces)


def _mm_tn(name, a, b, a_spec, b_spec, o_block, n_out, n_k):
    def body(a_ref, b_ref, o_ref):
        part = _tn(a_ref[...], b_ref[...])

        @pl.when(pl.program_id(1) == 0)
        def _():
            o_ref[...] = part

        @pl.when(pl.program_id(1) > 0)
        def _():
            o_ref[...] += part

    return _pallas_call(
        body, name=name, grid=(n_out, n_k),
        in_specs=[a_spec, b_spec],
        out_specs=pl.BlockSpec((None,) + o_block, lambda j, k: (j, 0, 0)),
        out_shape=_sds((n_out,) + o_block, F32),
        compiler_params=_params(("parallel", "arbitrary"), 56),
    )(a, b)


TK_TOKENS = 2048


def _dw_cols(name, a, b, n_cols):
    L, M = a.shape
    tk = min(L, TK_TOKENS)
    return _mm_tn(name, a, b, pl.BlockSpec((tk, M), lambda j, k: (k, 0)),
                  pl.BlockSpec((tk, n_cols), lambda j, k: (k, j)), (M, n_cols), N_CHIPS, L // tk)


def _dw_rows(name, a, b):
    L, M = a.shape
    N = b.shape[1]
    tk = min(L, TK_TOKENS)
    return _mm_tn(name, a, b, pl.BlockSpec((tk, M // N_CHIPS), lambda j, k: (k, j)),
                  pl.BlockSpec((tk, N), lambda j, k: (k, 0)), (M // N_CHIPS, N), N_CHIPS, L // tk)


def _dw_rows2(name, a, b):
    L, M = a.shape
    N = b.shape[1]
    tk = min(L, TK_TOKENS)
    return _mm_tn(name, a, b, pl.BlockSpec((tk, M // 2), lambda j, k: (k, j)),
                  pl.BlockSpec((tk, N), lambda j, k: (k, 0)), (M // 2, N), 2, L // tk)


def _place():
    x, y, c = lax.axis_index("x"), lax.axis_index("y"), lax.axis_index("c")
    chips = [(1 - x, y), (x, 1 - y), (1 - x, 1 - y)]
    return x, y, c, 2 * x + y, chips


def _remote(src, dst, send_sem, recv_sem, device):
    return pltpu.make_async_remote_copy(src_ref=src, dst_ref=dst, send_sem=send_sem,
                                        recv_sem=recv_sem, device_id=device, device_id_type=MESH)


def _half(ref, lead, c, r2):
    return ref.at[lead, pl.ds(pl.multiple_of(c * r2, 16), r2), :]


def _cast_place(name, w, chip_idx):
    r, cols = w.shape
    tr = r // 2

    def body(k_ref, w_ref, o_ref):
        o_ref[...] = w_ref[...].astype(BF16)

    return _pallas_call(
        body, name=name,
        grid_spec=pltpu.PrefetchScalarGridSpec(
            num_scalar_prefetch=1, grid=(2,),
            in_specs=[pl.BlockSpec((tr, cols), lambda i, k_ref: (i, 0))],
            out_specs=pl.BlockSpec((None, tr, cols), lambda i, k_ref: (k_ref[0], i, 0))),
        out_shape=_sds((N_CHIPS, r, cols), BF16),
        compiler_params=_params(("parallel",), 48),
    )(chip_idx, w)


def _cast_place_t(name, w, chip_idx):
    r, cols = w.shape

    def body(k_ref, w_ref, o_ref):
        o_ref[...] = w_ref[...].T.astype(BF16)

    return _pallas_call(
        body, name=name,
        grid_spec=pltpu.PrefetchScalarGridSpec(
            num_scalar_prefetch=1, grid=(cols // LANES,),
            in_specs=[pl.BlockSpec((r, LANES), lambda i, k_ref: (0, i))],
            out_specs=pl.BlockSpec((None, LANES, r), lambda i, k_ref: (k_ref[0], i, 0))),
        out_shape=_sds((N_CHIPS, cols, r), BF16),
        compiler_params=_params(("parallel",), 48),
    )(chip_idx, w)


def _gather_copies(bufs, whole, send_sems, recv_sems, select=None):
    x, y, c, k, chips = _place()
    pairs = []
    for w, buf in enumerate(bufs):
        for j, (cx, cy) in enumerate(chips):
            if select is not None and not select(w, j):
                continue
            if w in whole:
                mine, theirs = buf.at[k], buf.at[2 * cx + cy]
            else:
                r2 = buf.shape[1] // 2
                mine, theirs = _half(buf, k, c, r2), _half(buf, 2 * cx + cy, c, r2)
            sems = (send_sems.at[w * 3 + j], recv_sems.at[w * 3 + j])
            pairs.append((_remote(mine, mine, *sems, (cx, cy, c)), _remote(theirs, theirs, *sems, (x, y, c))))
    return pairs


def _gather_start(name, groups, after):
    flat = [b for bufs, _, _ in groups for b in bufs]
    nb, ng = len(flat), len(groups)

    def body(*refs):
        ins, sems, token = refs[:nb], refs[nb + 1:nb + 1 + 2 * ng], refs[-1]
        pos = 0
        for g, (bufs, whole, select) in enumerate(groups):
            for send, _ in _gather_copies(ins[pos:pos + len(bufs)], whole, sems[2 * g], sems[2 * g + 1], select):
                send.start()
            pos += len(bufs)
        token[...] = jnp.zeros_like(token)

    sem_shapes = []
    for bufs, _, _ in groups:
        sem_shapes += [pltpu.SemaphoreType.DMA((3 * len(bufs),))] * 2
    out = _pallas_call(
        body, name=name,
        in_specs=[HBM] * nb + [ANY], out_specs=tuple([SEM] * (2 * ng) + [HBM] * nb + [VMEM]),
        out_shape=tuple(sem_shapes + [pltpu.HBM(b.shape, b.dtype) for b in flat] + [_sds((8, LANES), F32)]),
        input_output_aliases={i: 2 * ng + i for i in range(nb)},
        compiler_params=pltpu.CompilerParams(has_side_effects=EFFECT),
    )(*flat, after)
    sems, thru, pos = [], [], 2 * ng
    for g, (bufs, _, _) in enumerate(groups):
        sems.append((out[2 * g], out[2 * g + 1]))
        thru.append(list(out[pos:pos + len(bufs)]))
        pos += len(bufs)
    return sems, thru, out[-1]


def _gather_wait(name, bufs, whole, sems, after, select=None):
    nb = len(bufs)

    def body(*refs):
        ins, send_sems, recv_sems = refs[:nb], refs[nb], refs[nb + 1]
        for send, arrival in _gather_copies(ins, whole, send_sems, recv_sems, select):
            send.wait_send()
            arrival.wait_recv()

    return _pallas_call(
        body, name=name,
        in_specs=[HBM] * nb + [SEM, SEM, ANY], out_specs=[HBM] * nb,
        out_shape=[pltpu.HBM(b.shape, b.dtype) for b in bufs],
        input_output_aliases={i: i for i in range(nb)},
        compiler_params=pltpu.CompilerParams(has_side_effects=EFFECT),
    )(*bufs, sems[0], sems[1], after)


def _gather_forward(name, bufs, sources=(0, 1, 2)):
    n = len(bufs)

    def body(*refs):
        outs = refs[n:2 * n]
        send_sems, recv_sems = refs[2 * n:]
        x, y, c, _, chips = _place()
        sends = []
        for w in range(n):
            r2 = outs[w].shape[1] // 2
            for j in sources:
                landed = _half(outs[w], 2 * chips[j][0] + chips[j][1], c, r2)
                sends.append(_remote(landed, landed, send_sems.at[w * 3 + j], recv_sems.at[w * 3 + j],
                                     (x, y, 1 - c)))
        for cp in sends:
            cp.start()
        for w in range(n):
            r2 = outs[w].shape[1] // 2
            for j in sources:
                got = _half(outs[w], 2 * chips[j][0] + chips[j][1], 1 - c, r2)
                _remote(got, got, send_sems.at[w * 3 + j], recv_sems.at[w * 3 + j], (x, y, c)).wait_recv()
        for cp in sends:
            cp.wait_send()

    return _pallas_call(
        body, name=name,
        in_specs=[ANY] * n, out_specs=[ANY] * n,
        out_shape=[_sds(b.shape, b.dtype) for b in bufs],
        input_output_aliases={i: i for i in range(n)},
        scratch_shapes=[pltpu.SemaphoreType.DMA((n * 3,)), pltpu.SemaphoreType.DMA((n * 3,))],
    )(*bufs)


def _rs_sibling(name, grads):
    n = len(grads)

    def body(*refs):
        ins, outs = refs[:n], refs[n:2 * n]
        send_sems, recv_sems = refs[2 * n:]
        x, y, c, _, _ = _place()
        copies = []
        for w in range(n):
            r2 = ins[w].shape[1] // 2
            copies.append(_remote(_half(ins[w], slice(None), 1 - c, r2), outs[w],
                                  send_sems.at[w], recv_sems.at[w], (x, y, 1 - c)))
        for cp in copies:
            cp.start()
        for cp in copies:
            cp.wait()

    return _pallas_call(
        body, name=name,
        in_specs=[ANY] * n, out_specs=[ANY] * n,
        out_shape=[_sds((N_CHIPS, g.shape[1] // 2, g.shape[2]), F32) for g in grads],
        scratch_shapes=[pltpu.SemaphoreType.DMA((n,)), pltpu.SemaphoreType.DMA((n,))],
    )(*grads)


def _rs_add(name, grad3, from_sibling, c_idx):
    _, r2, cols = from_sibling.shape

    def body(c_ref, g_ref, s_ref, o_ref):
        o_ref[...] = (g_ref[...] + s_ref[...]).astype(BF16)

    return _pallas_call(
        body, name=name,
        grid_spec=pltpu.PrefetchScalarGridSpec(
            num_scalar_prefetch=1, grid=(N_CHIPS,),
            in_specs=[pl.BlockSpec((None, r2, cols), lambda k, c_ref: (k, c_ref[0], 0)),
                      pl.BlockSpec((None, r2, cols), lambda k, c_ref: (k, 0, 0))],
            out_specs=pl.BlockSpec((None, r2, cols), lambda k, c_ref: (k, 0, 0))),
        out_shape=_sds(from_sibling.shape, BF16),
        compiler_params=_params(("parallel",), 48),
    )(c_idx, grad3, from_sibling)


def _split_start(name, arrays, n_sems, pairs_fn):
    n = len(arrays)

    def body(*refs):
        for send, _ in pairs_fn(refs[:n], refs[n], refs[n + 1]):
            send.start()
        refs[-1][...] = jnp.zeros_like(refs[-1])

    out = _pallas_call(
        body, name=name,
        in_specs=[HBM] * n, out_specs=tuple([SEM, SEM] + [HBM] * n + [VMEM]),
        out_shape=tuple([pltpu.SemaphoreType.DMA((n_sems,))] * 2 + [pltpu.HBM(a.shape, a.dtype) for a in arrays]
                        + [_sds((8, LANES), F32)]),
        input_output_aliases={i: 2 + i for i in range(n)},
        compiler_params=pltpu.CompilerParams(has_side_effects=EFFECT),
    )(*arrays)
    return (out[0], out[1]), list(out[2:2 + n]), out[-1]


def _split_wait(name, sems, arrays, pairs_fn, after):
    n = len(arrays)

    def body(*refs):
        for send, arrival in pairs_fn(refs[:n], refs[n], refs[n + 1]):
            send.wait_send()
            arrival.wait_recv()

    return list(_pallas_call(
        body, name=name,
        in_specs=[HBM] * n + [SEM, SEM, ANY], out_specs=[HBM] * n,
        out_shape=[pltpu.HBM(a.shape, a.dtype) for a in arrays],
        input_output_aliases={i: i for i in range(n)},
        compiler_params=pltpu.CompilerParams(has_side_effects=EFFECT),
    )(*arrays, sems[0], sems[1], after))


def _forward_pairs(bufs, send_sems, recv_sems):
    x, y, c, _, chips = _place()
    pairs = []
    for w, buf in enumerate(bufs):
        r2 = buf.shape[1] // 2
        for j, (cx, cy) in enumerate(chips):
            landed, theirs = _half(buf, 2 * cx + cy, c, r2), _half(buf, 2 * cx + cy, 1 - c, r2)
            sems = (send_sems.at[w * 3 + j], recv_sems.at[w * 3 + j])
            pairs.append((_remote(landed, landed, *sems, (x, y, 1 - c)), _remote(theirs, theirs, *sems, (x, y, c))))
    return pairs


def _sibling_pairs(arrays, send_sems, recv_sems):
    x, y, c, _, _ = _place()
    n = len(arrays) // 2
    pairs = []
    for w in range(n):
        r2 = arrays[w].shape[1] // 2
        cp = _remote(_half(arrays[w], slice(None), 1 - c, r2), arrays[n + w], send_sems.at[w], recv_sems.at[w],
                     (x, y, 1 - c))
        pairs.append((cp, cp))
    return pairs


def _ici_pairs(arrays, send_sems, recv_sems):
    x, y, c, _, chips = _place()
    n = len(arrays) // 2
    pairs = []
    for w in range(n):
        for j, (cx, cy) in enumerate(chips):
            cp = _remote(arrays[w].at[2 * cx + cy], arrays[n + w].at[j],
                         send_sems.at[w * 3 + j], recv_sems.at[w * 3 + j], (cx, cy, c))
            pairs.append((cp, cp))
    return pairs


def _rs_sum(name, partials, received, place_idx):
    _, r2, cols = partials.shape
    nb = 2
    tr = r2 // nb

    def body(idx_ref, p_ref, r_ref, o_ref):
        o_ref[...] = ((p_ref[...].astype(F32) + r_ref[0].astype(F32))
                      + (r_ref[1].astype(F32) + r_ref[2].astype(F32)))

    return _pallas_call(
        body, name=name,
        grid_spec=pltpu.PrefetchScalarGridSpec(
            num_scalar_prefetch=1, grid=(nb,),
            in_specs=[pl.BlockSpec((None, tr, cols), lambda i, idx: (idx[0], i, 0)),
                      pl.BlockSpec((3, tr, cols), lambda i, idx: (0, i, 0))],
            out_specs=pl.BlockSpec((tr, cols), lambda i, idx: (idx[1] * nb + i, 0))),
        out_shape=_sds((2 * r2, cols), F32),
        compiler_params=_params(("parallel",), 48),
    )(place_idx, partials, received)


def _share_pairs(arrays, send_sems, recv_sems):
    x, y, c, _, _ = _place()
    pairs = []
    for w, arr in enumerate(arrays):
        r2 = arr.shape[0] // 2
        mine = arr.at[pl.ds(pl.multiple_of(c * r2, 8), r2), :]
        theirs = arr.at[pl.ds(pl.multiple_of((1 - c) * r2, 8), r2), :]
        sems = (send_sems.at[w], recv_sems.at[w])
        pairs.append((_remote(mine, mine, *sems, (x, y, 1 - c)), _remote(theirs, theirs, *sems, (x, y, c))))
    return pairs


def _small_allreduce(red_mix, red_ffn, red_final, red_hg, g_conv, after):
    rows = N_SMALL_ROWS
    D = red_mix.shape[1]
    H = red_hg.shape[1]

    def body(mix_ref, ffn_ref, fin_ref, hg_ref, cv_ref, after_ref, sum_ref, all_ref, in_ref, send_sems,
             recv_sems):
        in_ref[...] = jnp.zeros_like(in_ref)
        in_ref[0:1, :] = mix_ref[0:1, :]
        in_ref[1:2, :] = ffn_ref[0:1, :]
        in_ref[2:3, :] = fin_ref[0:1, :]
        gam = hg_ref[1:2, 0:HEAD_DIM]
        for h in range(1, H // HEAD_DIM):
            gam = gam + hg_ref[1:2, h * HEAD_DIM:(h + 1) * HEAD_DIM]
        in_ref[3:4, 0:HEAD_DIM] = gam
        in_ref[3:4, HEAD_DIM:2 * HEAD_DIM] = fin_ref[1:2, 0:HEAD_DIM]
        in_ref[4:5, 0:H] = hg_ref[0:1, :]
        in_ref[6:9, 0:H] = cv_ref[...]
        x, y, c, _, _ = _place()
        me = 4 * x + 2 * y + c
        all_ref[me] = in_ref[...]
        copies = []
        for m in range(1, 8):
            mx, my, mc = (m >> 2) & 1, (m >> 1) & 1, m & 1
            px, py, pc = x ^ mx, y ^ my, c ^ mc
            copies.append((_remote(in_ref, all_ref.at[me], send_sems.at[m - 1], recv_sems.at[m - 1],
                                   (px, py, pc)), 4 * px + 2 * py + pc, m))
        for cp, _, _ in copies:
            cp.start()
        for _, peer, m in copies:
            _remote(in_ref, all_ref.at[peer], send_sems.at[m - 1], recv_sems.at[m - 1],
                    (x, y, c)).wait_recv()
        for cp, _, _ in copies:
            cp.wait_send()
        total = all_ref[0]
        for d in range(1, 8):
            total = total + all_ref[d]
        sum_ref[...] = total

    return _pallas_call(
        body, name="small_allreduce", pin=False,
        in_specs=[VMEM] * 5 + [ANY], out_specs=[VMEM, VMEM],
        out_shape=[_sds((rows, D), F32), _sds((8, rows, D), F32)],
        scratch_shapes=[pltpu.VMEM((rows, D), F32), pltpu.SemaphoreType.DMA((7,)),
                        pltpu.SemaphoreType.DMA((7,))],
    )(red_mix, red_ffn, red_final, red_hg, g_conv, after)[0]


def _adamw_math(w, g, m, v):
    m = ADAM_B1 * m + (1.0 - ADAM_B1) * g
    v = ADAM_B2 * v + (1.0 - ADAM_B2) * jnp.square(g)
    m_hat = m / (1.0 - ADAM_B1 ** ADAM_STEP)
    v_hat = v / (1.0 - ADAM_B2 ** ADAM_STEP)
    delta = -ADAM_LR * (m_hat / (jnp.sqrt(v_hat) + ADAM_EPS) + ADAM_WD * w)
    return delta, m, v


def _adamw(name, g, w, m, v):
    r, cols = g.shape
    tr = r // 4

    def body(g_ref, w_ref, m_ref, v_ref, go_ref, d_ref, mo_ref, vo_ref):
        g = g_ref[...]
        go_ref[...] = g
        d_ref[...], mo_ref[...], vo_ref[...] = _adamw_math(w_ref[...], g, m_ref[...], v_ref[...])

    blk = pl.BlockSpec((tr, cols), lambda i: (i, 0))
    return _pallas_call(
        body, name=name, grid=(r // tr,),
        in_specs=[blk] * 4, out_specs=[blk] * 4, out_shape=[_sds((r, cols), F32)] * 4,
        compiler_params=_params(("parallel",), 48),
    )(g, w, m, v)


def _small_update(total, chip_idx, ws, ms, vs):
    n = len(ws)
    H = ws[1].shape[1]

    def body(idx_ref, tot_ref, *refs):
        w, m, v, outs = refs[:n], refs[n:2 * n], refs[2 * n:3 * n], refs[3 * n:]
        chip = idx_ref[0]
        p0 = _lower_bound(w[1][...])
        dl0 = p0 * (1.0 - p0) * tot_ref[4:5, 0:H]
        conv = jnp.zeros((3, LANES), F32)
        for k in range(N_CHIPS):
            conv = jnp.where(chip == k, tot_ref[6:9, k * LANES:(k + 1) * LANES], conv)
        grads = [tot_ref[0:1, :], None, tot_ref[3:4, 0:HEAD_DIM], conv, tot_ref[1:2, :], tot_ref[2:3, :]]
        for p in range(n):
            g_ref, d_ref, mo_ref, vo_ref = outs[4 * p:4 * p + 4]
            if p == 1:
                for row, g in ((slice(0, 1), dl0), (slice(1, 2), -dl0)):
                    g_ref[row, :] = g
                    d_ref[row, :], mo_ref[row, :], vo_ref[row, :] = _adamw_math(
                        w[p][row, :], g, m[p][row, :], v[p][row, :])
            else:
                g_ref[...] = grads[p]
                d_ref[...], mo_ref[...], vo_ref[...] = _adamw_math(w[p][...], grads[p], m[p][...], v[p][...])
        outs[4 * n][...] = tot_ref[3:4, HEAD_DIM:2 * HEAD_DIM]

    full = lambda a: pl.BlockSpec(a.shape, lambda i, idx: (0,) * a.ndim)
    out_shape = [_sds(w.shape, F32) for w in ws for _ in range(4)] + [_sds((1, LANES), F32)]
    return _pallas_call(
        body, name="small_update",
        grid_spec=pltpu.PrefetchScalarGridSpec(
            num_scalar_prefetch=1, grid=(1,),
            in_specs=[full(total)] + [full(a) for a in ws + ms + vs],
            out_specs=[full(s) for s in out_shape]),
        out_shape=out_shape,
    )(chip_idx, total, *ws, *ms, *vs)


def kernel(x, norm_mix_g, w_in, lower_bounds, hg_norm_g, conv_w, w_branch_a, w_branch_b, w_out, norm_ffn_g, w_ffn_gate, w_ffn_up, w_ffn_down, norm_final_g, loss_target, m_norm_mix_g, m_w_in, m_lower_bounds, m_hg_norm_g, m_conv_w, m_w_branch_a, m_w_branch_b, m_w_out, m_norm_ffn_g, m_w_ffn_gate, m_w_ffn_up, m_w_ffn_down, m_norm_final_g, v_norm_mix_g, v_w_in, v_lower_bounds, v_hg_norm_g, v_conv_w, v_w_branch_a, v_w_branch_b, v_w_out, v_norm_ffn_g, v_w_ffn_gate, v_w_ffn_up, v_w_ffn_down, v_norm_final_g):
    _, L, D = x.shape
    H = D // 2
    assert lower_bounds.shape == (2, H) and hg_norm_g.shape == (1, HEAD_DIM)
    assert conv_w.shape == (1, 3, LANES) and w_in.shape[2] * N_CHIPS == 11 * H
    x2d, target = x.reshape(L, D), loss_target.reshape(L, D)
    g_final = norm_final_g.reshape(1, D)
    chip = 2 * lax.axis_index("x") + lax.axis_index("y")
    core = lax.axis_index("c")

    tr = lambda w: jnp.transpose(w[0])
    big = [w_in[0], w_branch_a[0], w_branch_b[0], w_out[0], tr(w_ffn_gate), tr(w_ffn_up), w_ffn_down[0]]
    big_m = [m_w_in[0], m_w_branch_a[0], m_w_branch_b[0], m_w_out[0], tr(m_w_ffn_gate), tr(m_w_ffn_up),
             m_w_ffn_down[0]]
    big_v = [v_w_in[0], v_w_branch_a[0], v_w_branch_b[0], v_w_out[0], tr(v_w_ffn_gate), tr(v_w_ffn_up),
             v_w_ffn_down[0]]
    names = ["w_in", "w_branch_a", "w_branch_b", "w_out", "w_ffn_gate", "w_ffn_up", "w_ffn_down"]

    chip_idx = chip.reshape(1).astype(jnp.int32)
    placed = [(_cast_place_t if j < 3 else _cast_place)("place_" + nm, w, chip_idx)
              for j, (nm, w) in enumerate(zip(names, big))]
    conv_placed = lax.dynamic_update_slice(jnp.zeros((N_CHIPS, 3, LANES), F32), conv_w, (chip, 0, 0))
    x_i, y_i = lax.axis_index("x"), lax.axis_index("y")
    blocks = lambda *ks: jnp.stack(ks).astype(jnp.int32)
    near = lambda w, j: j < 2
    far = lambda w, j: w == 1 or j == 2
    near_sems, in_flight, _ = _gather_start("gather_start_near", [([placed[0]], set(), near)], chip_idx)
    w_in_buf = in_flight[0][0]
    h, proj = _fwd_proj_first(x2d, norm_mix_g, w_in_buf, blocks(chip))
    sems, in_flight, _ = _gather_start(
        "gather_start_rest", [([w_in_buf, conv_placed], {1}, far), (placed[1:4], set(), None),
                              (placed[4:], set(), None)], h)
    w_in_buf, conv_buf = in_flight[0]
    (w_in_buf,) = _gather_wait("gather_wait_in_near", [w_in_buf], set(), near_sems[0], h, near)
    (w_in_buf,) = _gather_forward("gather_fwd_in_near", [w_in_buf], (0, 1))
    proj = _fwd_proj_more("fwd_proj_near", h, w_in_buf, proj,
                          blocks(2 * (1 - x_i) + y_i, 2 * x_i + (1 - y_i)))
    w_in_buf, conv_all = _gather_wait("gather_wait_in_far", [w_in_buf, conv_buf], {1}, sems[0], proj, far)
    (w_int3,) = _gather_forward("gather_fwd_in_far", [w_in_buf], (2,))
    proj = _fwd_proj_more("fwd_proj_far", h, w_int3, proj, blocks(2 * (1 - x_i) + (1 - y_i)))
    w_int = w_int3.reshape(-1, D)
    conv_full = jnp.transpose(conv_all, (1, 0, 2)).reshape(3, H)
    og, o_pre, s_saved = _hgrn_fwd(proj, lower_bounds, hg_norm_g, H)
    landed = _gather_wait("gather_wait_mix", in_flight[1], set(), sems[1], og)
    fwd_sems, landed, token = _split_start("gather_fwd_mix_start", landed, 9, _forward_pairs)
    cb = _conv_fwd(proj, conv_full, H, token)
    wat3, wbt3, wout3 = _split_wait("gather_fwd_mix_wait", fwd_sems, landed, _forward_pairs, cb)
    wat, wbt, wout = wat3.reshape(D, H), wbt3.reshape(D, H), wout3.reshape(D, D)
    landed = _gather_wait("gather_wait_ffn", in_flight[2], set(), sems[2], cb)
    fwd_sems, landed, token = _split_start("gather_fwd_ffn_start", landed, 9, _forward_pairs)
    sig_a, sig_b, dm_dga, dm_dgb, merged, x1, h2 = _fwd_mix(og, cb, proj, x2d, wat, wbt, wout, norm_ffn_g,
                                                              H, token)
    wgt3, wut3, wd3 = _split_wait("gather_fwd_ffn_wait", fwd_sems, landed, _forward_pairs, h2)
    d_ff = N_CHIPS * wd3.shape[1]
    wgt, wut, wd = wgt3.reshape(d_ff, D), wut3.reshape(d_ff, D), wd3.reshape(d_ff, D)
    ffn_ds_da, ffn_ds_db, ffn_s = _fwd_ffn_up(h2, wgt, wut)
    dx2, dx2b, red_final = _fwd_down_loss(ffn_s, wd, x1, target, g_final)

    c_idx = core.reshape(1).astype(jnp.int32)
    place_idx = jnp.stack([chip, core]).astype(jnp.int32)

    def sibling_start(tag, grads):
        bufs = [lax.empty((N_CHIPS, g.shape[1] // 2, g.shape[2]), F32) for g in grads]
        return _split_start("rs_sibling_start_" + tag, list(grads) + bufs, len(grads), _sibling_pairs)

    def ici_start(tag, js, grads, from_sibling):
        partials = [_rs_add("rs_add_" + names[j], g, s, c_idx) for j, g, s in zip(js, grads, from_sibling)]
        landings = [lax.empty((3,) + p.shape[1:], BF16) for p in partials]
        return _split_start("rs_ici_start_" + tag, partials + landings, 3 * len(js), _ici_pairs)

    def ici_start_behind(tag, js, started, after):
        n = len(js)
        arrays = _split_wait("rs_sibling_wait_" + tag, started[0], started[1], _sibling_pairs, after)
        return ici_start(tag, js, arrays[:n], arrays[n:])

    def sums(tag, js, started, after):
        n = len(js)
        arrays = _split_wait("rs_ici_wait_" + tag, started[0], started[1], _ici_pairs, after)
        return [_rs_sum("rs_sum_" + names[j], p, r, place_idx) for j, p, r in zip(js, arrays[:n], arrays[n:])]

    adamw = lambda j, g: _adamw("adamw_" + names[j], g, big[j], big_m[j], big_v[j])

    shards3 = lambda g: g.reshape(N_CHIPS, d_ff // N_CHIPS, D)
    da, db = _bwd_down(dx2b, wd, ffn_ds_da, ffn_ds_db)
    g_wd = shards3(_dw_rows2("dw_ffn_down", ffn_s, dx2b))
    g_wg = shards3(_dw_rows2("dw_ffn_gate", da, h2))
    g_wu = shards3(_dw_rows2("dw_ffn_up", db, h2))
    ffn_sibling = sibling_start("ffn", [g_wg, g_wu, g_wd])
    dx1, dx1b, red_ffn = _bwd_ffn_dh(da, db, wgt, wut, x1, dx2, norm_ffn_g, ffn_sibling[2])
    ffn_ici = ici_start_behind("ffn", [4, 5, 6], ffn_sibling, dx1b)
    dya, dyb, d_gates, d_o, d_cb = _bwd_mix(dx1b, sig_a, sig_b, dm_dga, dm_dgb, wat, wbt, wout, H, ffn_ici[2])
    g_wout = _dw_rows("dw_out", merged, dx1b)
    g_wa = _dw_cols("dw_branch_a", og, dya, D // N_CHIPS)
    g_wb = _dw_cols("dw_branch_b", cb, dyb, D // N_CHIPS)
    mix_sibling = sibling_start("mix", [g_wa, g_wb, g_wout])
    d_hgrn, red_hg = _hgrn_bwd(proj, lower_bounds, hg_norm_g, o_pre, d_o, s_saved, H, mix_sibling[2])
    mix_ici = ici_start_behind("mix", [1, 2, 3], mix_sibling, d_hgrn)
    dcg, dbg, dxb, g_conv = _conv_bwd(proj, conv_full, d_cb, H, mix_ici[2])
    dproj = [d_hgrn, dcg, dbg, dxb, d_gates]
    g_win = _dw_in(h, dproj, w_int3.shape[1])
    in_sibling = sibling_start("in", [g_win])
    halves = sums("mix", [1, 2, 3], mix_ici, in_sibling[2]) + sums("ffn", [4, 5, 6], ffn_ici, in_sibling[2])
    rest_share = _split_start("rs_share_start_rest", halves, len(halves), _share_pairs)
    in_ici = ici_start_behind("in", [0], in_sibling, rest_share[2])
    grad_x, red_mix = _bwd_in(dproj, w_int, x2d, dx1, norm_mix_g, in_ici[2])
    in_share = _split_start("rs_share_start_in", sums("in", [0], in_ici, grad_x), 1, _share_pairs)
    total = _small_allreduce(red_mix, red_ffn, red_final, red_hg, g_conv, in_share[2])
    rest_grads = _split_wait("rs_share_wait_rest", rest_share[0], rest_share[1], _share_pairs, total)
    big_out = [None] + [adamw(j, g) for j, g in zip(range(1, 7), rest_grads)]
    in_grad = _split_wait("rs_share_wait_in", in_share[0], in_share[1], _share_pairs, big_out[6][0])
    big_out[0] = adamw(0, in_grad[0])

    def smalls(mix, lb, hg, cw, ffn, fin):
        return [mix, lb, hg, cw[0], ffn, fin.reshape(1, D)]

    small_out = _small_update(
        total, chip_idx,
        smalls(norm_mix_g, lower_bounds, hg_norm_g, conv_w, norm_ffn_g, norm_final_g),
        smalls(m_norm_mix_g, m_lower_bounds, m_hg_norm_g, m_conv_w, m_norm_ffn_g, m_norm_final_g),
        smalls(v_norm_mix_g, v_lower_bounds, v_hg_norm_g, v_conv_w, v_norm_ffn_g, v_norm_final_g))

    def outputs(i):
        big_i = [big_out[j][i] for j in range(7)]
        mix, lb, hg, cw, ffn, fin = [small_out[4 * p + i] for p in range(6)]
        return [mix, big_i[0][None], lb, hg, cw[None], big_i[1][None], big_i[2][None], big_i[3][None], ffn,
                big_i[4].T[None], big_i[5].T[None], big_i[6][None], fin.reshape(D)]

    outs = [small_out[24][0, 0], grad_x.reshape(1, L, D)]
    for i in range(4):
        outs += outputs(i)
    return tuple(outs)
```

```python
import functools

import jax
import jax.numpy as jnp
from jax import lax
from jax.experimental import pallas as pl
from jax.experimental.pallas import tpu as pltpu

F32 = jnp.float32
BF16 = jnp.bfloat16
EPS = 1e-6
CHUNK = 32
HEAD_DIM = 128
LANES = 128
N_CHIPS = 4
N_SMALL_ROWS = 16

ADAM_LR = 0.001
ADAM_B1 = 0.9
ADAM_B2 = 0.999
ADAM_EPS = 1e-08
ADAM_WD = 0.01
ADAM_STEP = 10

MESH = pl.DeviceIdType.MESH
ANY = pl.BlockSpec(memory_space=pl.ANY)
VMEM = pl.BlockSpec(memory_space=pltpu.VMEM)
HBM = pl.BlockSpec(memory_space=pltpu.HBM)
SEM = pl.BlockSpec(memory_space=pltpu.SEMAPHORE)
EFFECT = pltpu.SideEffectType.DATAFLOW_SIDE_EFFECTING


def _sds(shape, dtype):
    return jax.ShapeDtypeStruct(shape, dtype)


def _pallas_call(body, pin=True, **kwargs):
    if not pin:
        return pl.pallas_call(body, **kwargs)
    in_hbm = lambda s: pltpu.HBM(s.shape, s.dtype) if isinstance(s, jax.ShapeDtypeStruct) else s
    kwargs["out_shape"] = jax.tree.map(in_hbm, kwargs["out_shape"])
    call = pl.pallas_call(body, **kwargs)

    def run(*args):
        return call(*[pltpu.with_memory_space_constraint(a, pltpu.HBM) if a.dtype in (F32, BF16) else a
                      for a in args])

    return run


def _params(semantics, vmem_mb):
    return pltpu.CompilerParams(dimension_semantics=semantics, vmem_limit_bytes=vmem_mb << 20)


def _nn(a, b):
    return lax.dot_general(a, b, (((1,), (0,)), ((), ())), preferred_element_type=F32)


def _nt(a, b):
    return lax.dot_general(a, b, (((1,), (1,)), ((), ())), preferred_element_type=F32)


def _tn(a, b):
    return lax.dot_general(a, b, (((0,), (0,)), ((), ())), preferred_element_type=F32)


def _sigmoid(x):
    return jax.nn.sigmoid(x)


def _rms_stats(x):
    r = lax.rsqrt(jnp.mean(x * x, axis=-1, keepdims=True) + EPS)
    return r, x * r


def _rms_bwd(dxh, xh, r):
    return r * (dxh - xh * jnp.mean(dxh * xh, axis=-1, keepdims=True))


def _fwd_proj_first(x, g_mix, w_int3, block):
    L, D = x.shape
    tn = w_int3.shape[1]
    tm = min(L, 1024)

    def body(blk_ref, x_ref, g_ref, w_ref, h_ref, p_ref):
        _, xh = _rms_stats(x_ref[...])
        h = (xh * g_ref[...]).astype(BF16)
        h_ref[...] = h
        p_ref[...] = _nt(h, w_ref[...])

    return _pallas_call(
        body, name="fwd_proj_own",
        grid_spec=pltpu.PrefetchScalarGridSpec(
            num_scalar_prefetch=1, grid=(L // tm,),
            in_specs=[pl.BlockSpec((tm, D), lambda i, blk: (i, 0)),
                      pl.BlockSpec((1, D), lambda i, blk: (0, 0)),
                      pl.BlockSpec((None, tn, D), lambda i, blk: (blk[0], 0, 0))],
            out_specs=[pl.BlockSpec((tm, D), lambda i, blk: (i, 0)),
                       pl.BlockSpec((tm, tn), lambda i, blk: (i, blk[0]))]),
        out_shape=[_sds((L, D), BF16), _sds((L, N_CHIPS * tn), F32)],
        compiler_params=_params(("parallel",), 48),
    )(block, x, g_mix, w_int3)


def _fwd_proj_more(name, h, w_int3, proj, blocks):
    L, D = h.shape
    tn = w_int3.shape[1]
    tm = min(L, 1024)

    def body(blk_ref, h_ref, w_ref, proj_ref, p_ref):
        p_ref[...] = _nt(h_ref[...], w_ref[...])

    return _pallas_call(
        body, name=name,
        grid_spec=pltpu.PrefetchScalarGridSpec(
            num_scalar_prefetch=1, grid=(L // tm, blocks.shape[0]),
            in_specs=[pl.BlockSpec((tm, D), lambda i, j, blk: (i, 0)),
                      pl.BlockSpec((None, tn, D), lambda i, j, blk: (blk[j], 0, 0)), ANY],
            out_specs=pl.BlockSpec((tm, tn), lambda i, j, blk: (i, blk[j]))),
        out_shape=_sds(proj.shape, proj.dtype),
        input_output_aliases={3: 0},
        compiler_params=_params(("parallel", "arbitrary"), 48),
    )(blocks, h, w_int3, proj)


def _lower_bound(lbp):
    l0, l1 = lbp[0:1, :], lbp[1:2, :]
    m = jnp.maximum(l0, l1)
    e0, e1 = jnp.exp(l0 - m), jnp.exp(l1 - m)
    return e0 / (e0 + e1)


def _seg_scan(x, r32, forward):
    n = x.shape[0]
    s = 1
    while s < CHUNK:
        if forward:
            x = x + jnp.where(r32 >= s, pltpu.roll(x, s, 0), 0.0)
        else:
            x = x + jnp.where(r32 < CHUNK - s, pltpu.roll(x, n - s, 0), 0.0)
        s *= 2
    return x


def _bcast_row(x, row):
    n, w = x.shape
    nc = n // CHUNK
    x3 = x.reshape(nc, CHUNK, w)
    return jnp.broadcast_to(x3[:, row:row + 1, :], (nc, CHUNK, w)).reshape(n, w)


def _hgrn_prep(q_raw, f_raw, lb):
    r32 = lax.broadcasted_iota(jnp.int32, f_raw.shape, 0) & (CHUNK - 1)
    sig = _sigmoid(f_raw)
    f = lb + (1.0 - lb) * sig
    b = _seg_scan(jnp.log(f), r32, True)
    a = _bcast_row(b, CHUNK // 2 - 1)
    bl = _bcast_row(b, CHUNK - 1)
    sq = _sigmoid(q_raw)
    q = q_raw * sq * (HEAD_DIM ** -0.5)
    return dict(r32=r32, sig=sig, f=f, k=1.0 - f, b=b, a=a, bl=bl, sq=sq, q=q)


def _chunk_masks(n):
    ri = lax.broadcasted_iota(jnp.int32, (n, n), 0)
    ci = lax.broadcasted_iota(jnp.int32, (n, n), 1)
    same = (ri // CHUNK) == (ci // CHUNK)
    return same & (ci <= ri), same & (ri <= ci)


def _hgrn_fwd(proj, lower_bounds, gamma, H):
    L = proj.shape[0]
    nh = H // HEAD_DIM
    TL = min(L, 256)
    nc = TL // CHUNK

    def body(q_ref, f_ref, v_ref, g_ref, lbp_ref, gam_ref, og_ref, o_ref, s_ref, st_ref):
        @pl.when(pl.program_id(0) == 0)
        def _():
            st_ref[...] = jnp.zeros_like(st_ref)

        lb = _lower_bound(lbp_ref[...])
        gam = gam_ref[...]
        mask, _ = _chunk_masks(TL)
        rowc = lax.broadcasted_iota(jnp.int32, (TL, HEAD_DIM), 0) // CHUNK
        for h in range(nh):
            hs = slice(h * HEAD_DIM, (h + 1) * HEAD_DIM)
            p = _hgrn_prep(q_ref[:, hs], f_ref[:, hs], lb[:, hs])
            v = v_ref[:, hs]
            vb = v.astype(BF16)
            vt = v.T.astype(BF16)
            q_hat = (p["q"] * jnp.exp(p["b"] - p["a"])).astype(BF16)
            k_hat = (p["k"] * jnp.exp(p["a"] - p["b"])).astype(BF16)
            q_in = (p["q"] * jnp.exp(p["b"])).astype(BF16)
            k_out = (p["k"] * jnp.exp(p["bl"] - p["b"])).astype(BF16)
            dec = jnp.exp(p["bl"])
            att = jnp.where(mask, _nt(q_hat, k_hat), 0.0).astype(BF16)
            o_intra = _nn(att, vb)
            st = st_ref[h]
            for c in range(nc):
                rs = slice(c * CHUNK, (c + 1) * CHUNK)
                stb = st.astype(BF16)
                s_ref[c, h] = stb
                o_ref[rs, hs] = o_intra[rs] + _nt(q_in[rs], stb)
                k_c = jnp.where(rowc == c, k_out, jnp.zeros_like(k_out))
                st = st * dec[c * CHUNK:c * CHUNK + 1, :] + _nn(vt, k_c)
            st_ref[h] = st
            o = o_ref[:, hs]
            _, xh = _rms_stats(o)
            gr = g_ref[:, hs]
            og_ref[:, hs] = (xh * gam * (gr * _sigmoid(gr))).astype(BF16)

    col = lambda k: pl.BlockSpec((TL, H), lambda i, k=k: (i, k))
    return _pallas_call(
        body, name="hgrn_fwd", grid=(L // TL,),
        in_specs=[col(0), col(1), col(2), col(3),
                  pl.BlockSpec(lower_bounds.shape, lambda i: (0, 0)),
                  pl.BlockSpec(gamma.shape, lambda i: (0, 0))],
        out_specs=[pl.BlockSpec((TL, H), lambda i: (i, 0)),
                   pl.BlockSpec((TL, H), lambda i: (i, 0)),
                   pl.BlockSpec((nc, nh, HEAD_DIM, HEAD_DIM), lambda i: (i, 0, 0, 0))],
        out_shape=[_sds((L, H), BF16), _sds((L, H), F32),
                   _sds((L // CHUNK, nh, HEAD_DIM, HEAD_DIM), BF16)],
        scratch_shapes=[pltpu.VMEM((nh, HEAD_DIM, HEAD_DIM), F32)],
        compiler_params=_params(("arbitrary",), 48),
    )(proj, proj, proj, proj, lower_bounds, gamma)


def _hgrn_bwd(proj, lower_bounds, gamma, o_pre, d_out, s_saved, H, after):
    L = proj.shape[0]
    nh = H // HEAD_DIM
    TL = min(L, 256)
    nc = TL // CHUNK
    nt = L // TL

    def body(q_ref, f_ref, v_ref, g_ref, lbp_ref, gam_ref, o_ref, d_ref, s_ref, after_ref,
             dp_ref, red_ref, dst_ref, dsall_ref, tmp_ref):
        @pl.when(pl.program_id(0) == 0)
        def _():
            dst_ref[...] = jnp.zeros_like(dst_ref)
            red_ref[...] = jnp.zeros_like(red_ref)

        lb = _lower_bound(lbp_ref[...])
        gam = gam_ref[...]
        mask, mask_t = _chunk_masks(TL)
        rowc = lax.broadcasted_iota(jnp.int32, (TL, HEAD_DIM), 0) // CHUNK
        for h in range(nh):
            hs = slice(h * HEAD_DIM, (h + 1) * HEAD_DIM)
            qr, gr, lbh = q_ref[:, hs], g_ref[:, hs], lb[:, hs]
            p = _hgrn_prep(qr, f_ref[:, hs], lbh)
            vb = v_ref[:, hs].astype(BF16)
            eba, eab = jnp.exp(p["b"] - p["a"]), jnp.exp(p["a"] - p["b"])
            eb, elb = jnp.exp(p["b"]), jnp.exp(p["bl"] - p["b"])
            dec = jnp.exp(p["bl"])
            q_hat, k_hat = p["q"] * eba, p["k"] * eab
            q_in, k_out = p["q"] * eb, p["k"] * elb
            q_hat_b, k_hat_b = q_hat.astype(BF16), k_hat.astype(BF16)
            q_in_b, k_out_b = q_in.astype(BF16), k_out.astype(BF16)

            o, dout = o_ref[:, hs], d_ref[:, hs]
            sg = _sigmoid(gr)
            r, xh = _rms_stats(o)
            hq, hf, hv, hg = [slice(k * H + h * HEAD_DIM, k * H + (h + 1) * HEAD_DIM) for k in range(4)]
            dp_ref[:, hg] = (dout * (xh * gam) * (sg * (1.0 + gr * (1.0 - sg)))).astype(BF16)
            dn = dout * (gr * sg)
            red_ref[1:2, hs] += jnp.sum(dn * xh, axis=0, keepdims=True)
            do = _rms_bwd(dn * gam, xh, r)
            dob = do.astype(BF16)
            dot_b = do.T.astype(BF16)

            att_t = jnp.where(mask_t, _nt(k_hat_b, q_hat_b), 0.0).astype(BF16)
            dv_intra = _nn(att_t, dob)
            datt = jnp.where(mask, _nt(dob, vb), 0.0).astype(BF16)
            dqh = _nn(datt, k_hat_b)
            datt_t = jnp.where(mask_t, _nt(vb, dob), 0.0).astype(BF16)
            dkh = _nn(datt_t, q_hat_b)

            dst = dst_ref[h]
            for c in reversed(range(nc)):
                dsall_ref[c] = dst
                q_c = jnp.where(rowc == c, q_in_b, jnp.zeros_like(q_in_b))
                dst = dst * dec[c * CHUNK:c * CHUNK + 1, :] + _nn(dot_b, q_c)
            dst_ref[h] = dst
            for c in range(nc):
                rs = slice(c * CHUNK, (c + 1) * CHUNK)
                ds_c = dsall_ref[c]
                dsb = ds_c.astype(BF16)
                st_prev = s_ref[c, h]
                tmp_ref[0, rs, :] = _nt(k_out_b[rs], dsb)
                tmp_ref[1, rs, :] = _nn(vb[rs], dsb)
                tmp_ref[2, rs, :] = _nn(dob[rs], st_prev)
                ddec = jnp.sum(ds_c * st_prev.astype(F32), axis=0, keepdims=True)
                tmp_ref[3, rs, :] = jnp.broadcast_to(ddec * dec[c * CHUNK:c * CHUNK + 1, :],
                                                     (CHUNK, HEAD_DIM))
            dko, dqi = tmp_ref[1], tmp_ref[2]
            dq = dqh * eba + dqi * eb
            dk = dkh * eab + dko * elb
            tko = dko * k_out
            db = dqh * q_hat - dkh * k_hat + dqi * q_in - tko
            dlog = (_seg_scan(db, p["r32"], False)
                    + _bcast_row(_seg_scan(tko, p["r32"], True), CHUNK - 1) + tmp_ref[3])
            df = dlog / p["f"] - dk
            sig = p["sig"]
            red_ref[0:1, hs] += jnp.sum(df * (1.0 - sig), axis=0, keepdims=True)
            dp_ref[:, hf] = (df * (1.0 - lbh) * sig * (1.0 - sig)).astype(BF16)
            sq = p["sq"]
            dp_ref[:, hq] = (dq * (HEAD_DIM ** -0.5) * (sq * (1.0 + qr * (1.0 - sq)))).astype(BF16)
            dp_ref[:, hv] = (dv_intra + tmp_ref[0]).astype(BF16)

    col = lambda k: pl.BlockSpec((TL, H), lambda i, k=k: (nt - 1 - i, k))
    rev = pl.BlockSpec((TL, H), lambda i: (nt - 1 - i, 0))
    return _pallas_call(
        body, name="hgrn_bwd", grid=(nt,),
        in_specs=[col(0), col(1), col(2), col(3),
                  pl.BlockSpec(lower_bounds.shape, lambda i: (0, 0)),
                  pl.BlockSpec(gamma.shape, lambda i: (0, 0)),
                  rev, rev,
                  pl.BlockSpec((nc, nh, HEAD_DIM, HEAD_DIM), lambda i: (nt - 1 - i, 0, 0, 0)), ANY],
        out_specs=[pl.BlockSpec((TL, 4 * H), lambda i: (nt - 1 - i, 0)), pl.BlockSpec((8, H), lambda i: (0, 0))],
        out_shape=[_sds((L, 4 * H), BF16), _sds((8, H), F32)],
        scratch_shapes=[pltpu.VMEM((nh, HEAD_DIM, HEAD_DIM), F32),
                        pltpu.VMEM((nc, HEAD_DIM, HEAD_DIM), F32),
                        pltpu.VMEM((4, TL, HEAD_DIM), F32)],
        compiler_params=_params(("arbitrary",), 48),
    )(proj, proj, proj, proj, lower_bounds, gamma, o_pre, d_out, s_saved, after)


def _shift_down(u, s, row):
    return jnp.where(row >= s, pltpu.roll(u, s, 0), 0.0)


def _shift_up(u, s, row):
    n = u.shape[0]
    return jnp.where(row < n - s, pltpu.roll(u, n - s, 0), 0.0)


def _conv_specs(L, H):
    per = H // LANES
    return [pl.BlockSpec((L, LANES), lambda j, o=o: (0, o * per + j)) for o in (4, 5, 6)]


def _conv_fwd(proj, conv_w, H, after):
    L = proj.shape[0]

    def body(c_ref, b_ref, x_ref, w_ref, after_ref, o_ref):
        row = lax.broadcasted_iota(jnp.int32, (L, LANES), 0)
        u = c_ref[...] * x_ref[...]
        w = w_ref[...]
        y = w[0:1] * _shift_down(u, 2, row) + w[1:2] * _shift_down(u, 1, row) + w[2:3] * u
        o_ref[...] = (b_ref[...] * y).astype(BF16)

    return _pallas_call(
        body, name="conv_fwd", grid=(H // LANES,),
        in_specs=_conv_specs(L, H) + [pl.BlockSpec((3, LANES), lambda j: (0, j)), ANY],
        out_specs=pl.BlockSpec((L, LANES), lambda j: (0, j)),
        out_shape=_sds((L, H), BF16),
        compiler_params=_params(("parallel",), 48),
    )(proj, proj, proj, conv_w, after)


def _conv_bwd(proj, conv_w, dcb, H, after):
    L = proj.shape[0]

    def body(c_ref, b_ref, x_ref, w_ref, d_ref, after_ref, dc_ref, db_ref, dx_ref, dw_ref):
        row = lax.broadcasted_iota(jnp.int32, (L, LANES), 0)
        cg, xb = c_ref[...], x_ref[...]
        u = cg * xb
        u1, u2 = _shift_down(u, 1, row), _shift_down(u, 2, row)
        w = w_ref[...]
        y = w[0:1] * u2 + w[1:2] * u1 + w[2:3] * u
        d = d_ref[...]
        db_ref[...] = (d * y).astype(BF16)
        dy = d * b_ref[...]
        du = w[2:3] * dy + w[1:2] * _shift_up(dy, 1, row) + w[0:1] * _shift_up(dy, 2, row)
        dw_ref[0:1, :] = jnp.sum(dy * u2, axis=0, keepdims=True)
        dw_ref[1:2, :] = jnp.sum(dy * u1, axis=0, keepdims=True)
        dw_ref[2:3, :] = jnp.sum(dy * u, axis=0, keepdims=True)
        dc_ref[...] = (du * xb).astype(BF16)
        dx_ref[...] = (du * cg).astype(BF16)

    blk = pl.BlockSpec((L, LANES), lambda j: (0, j))
    return _pallas_call(
        body, name="conv_bwd", grid=(H // LANES,),
        in_specs=_conv_specs(L, H) + [pl.BlockSpec((3, LANES), lambda j: (0, j)), blk, ANY],
        out_specs=[blk, blk, blk, pl.BlockSpec((3, LANES), lambda j: (0, j))],
        out_shape=[_sds((L, H), BF16)] * 3 + [_sds((3, H), F32)],
        compiler_params=_params(("parallel",), 56),
    )(proj, proj, proj, conv_w, dcb, after)


def _gate_specs(tm, H):
    return [pl.BlockSpec((tm, H), lambda i, k=k: (i, k)) for k in (7, 8, 9, 10)]


def _fwd_mix(og, cb, proj, x, wat, wbt, wout, g_ffn, H, after):
    L, D = x.shape
    tm = min(L, 512)

    def body(o_ref, cb_ref, ga0, ga1, gb0, gb1, x_ref, wa_ref, wb_ref, wo_ref, g_ref, after_ref,
             sa_ref, sb_ref, ta_ref, tb_ref, m_ref, x1_ref, h2_ref):
        ya, yb = _nt(o_ref[...], wa_ref[...]), _nt(cb_ref[...], wb_ref[...])
        for k, (gar, gbr) in enumerate(((ga0, gb0), (ga1, gb1))):
            cs = slice(k * H, (k + 1) * H)
            sa, sb = _sigmoid(gar[...]), _sigmoid(gbr[...])
            ma, mb = sa * ya[:, cs], sb * yb[:, cs]
            m_ref[:, cs] = (ma + mb).astype(BF16)
            sa_ref[:, cs] = sa.astype(BF16)
            sb_ref[:, cs] = sb.astype(BF16)
            ta_ref[:, cs] = (ma * (1.0 - sa)).astype(BF16)
            tb_ref[:, cs] = (mb * (1.0 - sb)).astype(BF16)
        x1 = x_ref[...] + _nn(m_ref[...], wo_ref[...])
        x1_ref[...] = x1
        _, xh = _rms_stats(x1)
        h2_ref[...] = (xh * g_ref[...]).astype(BF16)

    row = lambda w: pl.BlockSpec((tm, w), lambda i: (i, 0))
    full = lambda a: pl.BlockSpec(a.shape, lambda i: (0,) * a.ndim)
    return _pallas_call(
        body, name="fwd_mix", grid=(L // tm,),
        in_specs=[row(H), row(H)] + _gate_specs(tm, H) + [row(D), full(wat), full(wbt), full(wout),
                                                           full(g_ffn), ANY],
        out_specs=[row(D)] * 7,
        out_shape=[_sds((L, D), BF16)] * 5 + [_sds((L, D), F32), _sds((L, D), BF16)],
        compiler_params=_params(("parallel",), 56),
    )(og, cb, proj, proj, proj, proj, x, wat, wbt, wout, g_ffn, after)


def _bwd_mix(dx1b, sig_a, sig_b, dm_dga, dm_dgb, wat, wbt, wout, H, after):
    L, D = dx1b.shape
    tm = min(L, 512)

    def body(dx_ref, sa_ref, sb_ref, ta_ref, tb_ref, wa_ref, wb_ref, wo_ref, after_ref,
             dya_ref, dyb_ref, dgate_ref, do_ref, dcb_ref):
        dm = _nt(dx_ref[...], wo_ref[...])
        dgate_ref[:, 0:D] = (dm * ta_ref[...].astype(F32)).astype(BF16)
        dgate_ref[:, D:2 * D] = (dm * tb_ref[...].astype(F32)).astype(BF16)
        dya_ref[...] = (dm * sa_ref[...].astype(F32)).astype(BF16)
        dyb_ref[...] = (dm * sb_ref[...].astype(F32)).astype(BF16)
        do_ref[...] = _nn(dya_ref[...], wa_ref[...])
        dcb_ref[...] = _nn(dyb_ref[...], wb_ref[...])

    row = lambda w: pl.BlockSpec((tm, w), lambda i: (i, 0))
    full = lambda a: pl.BlockSpec(a.shape, lambda i: (0,) * a.ndim)
    return _pallas_call(
        body, name="bwd_mix", grid=(L // tm,),
        in_specs=[row(D)] * 5 + [full(wat), full(wbt), full(wout), ANY],
        out_specs=[row(D), row(D), row(2 * D), row(H), row(H)],
        out_shape=[_sds((L, D), BF16)] * 2 + [_sds((L, 2 * D), BF16)] + [_sds((L, H), F32)] * 2,
        compiler_params=_params(("parallel",), 56),
    )(dx1b, sig_a, sig_b, dm_dga, dm_dgb, wat, wbt, wout, after)


def _fwd_ffn_up(h2, wgt, wut):
    L, D = h2.shape
    F = wgt.shape[0]
    tn = F // 2
    tm = min(L, 512)

    def body(h_ref, wg_ref, wu_ref, sa_ref, sb_ref, s_ref):
        h = h_ref[...]
        a, b = _nt(h, wg_ref[...]), _nt(h, wu_ref[...])
        sg = _sigmoid(a)
        silu = a * sg
        sa_ref[...] = (b * sg * (1.0 + a * (1.0 - sg))).astype(BF16)
        sb_ref[...] = silu.astype(BF16)
        s_ref[...] = (silu * b).astype(BF16)

    wspec = pl.BlockSpec((tn, D), lambda j, i: (j, 0))
    ospec = pl.BlockSpec((tm, tn), lambda j, i: (i, j))
    return _pallas_call(
        body, name="fwd_ffn_up", grid=(2, L // tm),
        in_specs=[pl.BlockSpec((tm, D), lambda j, i: (i, 0)), wspec, wspec],
        out_specs=[ospec] * 3,
        out_shape=[_sds((L, F), BF16)] * 3,
        compiler_params=_params(("parallel", "parallel"), 48),
    )(h2, wgt, wut)


def _fwd_down_loss(s, wd, x1, target, g_final):
    L, D = x1.shape
    F = wd.shape[0]
    tm = min(L, 256)

    def body(s_ref, wd_ref, x1_ref, t_ref, g_ref, dx_ref, dxb_ref, red_ref):
        @pl.when(pl.program_id(0) == 0)
        def _():
            red_ref[...] = jnp.zeros_like(red_ref)

        g = g_ref[...]
        r, xh = _rms_stats(x1_ref[...] + _nn(s_ref[...], wd_ref[...]))
        e = xh * g - t_ref[...]
        dy = e * (1.0 / D)
        dx = _rms_bwd(dy * g, xh, r)
        dx_ref[...] = dx
        dxb_ref[...] = dx.astype(BF16)
        red_ref[0:1, :] += jnp.sum(dy * xh, axis=0, keepdims=True)
        red_ref[1:2, :] += jnp.broadcast_to(0.5 * jnp.sum(e * e) * (1.0 / D), (1, D))

    row = pl.BlockSpec((tm, D), lambda i: (i, 0))
    return _pallas_call(
        body, name="fwd_down_loss", grid=(L // tm,),
        in_specs=[pl.BlockSpec((tm, F), lambda i: (i, 0)), pl.BlockSpec((F, D), lambda i: (0, 0)),
                  row, row, pl.BlockSpec((1, D), lambda i: (0, 0))],
        out_specs=[row, row, pl.BlockSpec((8, D), lambda i: (0, 0))],
        out_shape=[_sds((L, D), F32), _sds((L, D), BF16), _sds((8, D), F32)],
        compiler_params=_params(("arbitrary",), 48),
    )(s, wd, x1, target, g_final)


def _bwd_down(dx2b, wd, s_a, s_b):
    L, D = dx2b.shape
    F = wd.shape[0]
    tn = F // 2
    tm = min(L, 512)

    def body(dx_ref, wd_ref, sa_ref, sb_ref, da_ref, db_ref):
        ds = _nt(dx_ref[...], wd_ref[...])
        da_ref[...] = (ds * sa_ref[...].astype(F32)).astype(BF16)
        db_ref[...] = (ds * sb_ref[...].astype(F32)).astype(BF16)

    ospec = pl.BlockSpec((tm, tn), lambda j, i: (i, j))
    return _pallas_call(
        body, name="bwd_down", grid=(2, L // tm),
        in_specs=[pl.BlockSpec((tm, D), lambda j, i: (i, 0)),
                  pl.BlockSpec((tn, D), lambda j, i: (j, 0)), ospec, ospec],
        out_specs=[ospec] * 2,
        out_shape=[_sds((L, F), BF16)] * 2,
        compiler_params=_params(("parallel", "parallel"), 48),
    )(dx2b, wd, s_a, s_b)


def _bwd_ffn_dh(da, db, wgt, wut, x1, dx2, g_ffn, after):
    L, D = x1.shape
    F = wgt.shape[0]
    tm = min(L, 256)

    def body(da_ref, db_ref, wg_ref, wu_ref, x1_ref, dx2_ref, g_ref, after_ref, dx_ref, dxb_ref, red_ref):
        @pl.when(pl.program_id(0) == 0)
        def _():
            red_ref[...] = jnp.zeros_like(red_ref)

        dh = _nn(da_ref[...], wg_ref[...]) + _nn(db_ref[...], wu_ref[...])
        r, xh = _rms_stats(x1_ref[...])
        red_ref[0:1, :] += jnp.sum(dh * xh, axis=0, keepdims=True)
        dx = dx2_ref[...] + _rms_bwd(dh * g_ref[...], xh, r)
        dx_ref[...] = dx
        dxb_ref[...] = dx.astype(BF16)

    row = pl.BlockSpec((tm, D), lambda i: (i, 0))
    aspec = pl.BlockSpec((tm, F), lambda i: (i, 0))
    wspec = pl.BlockSpec((F, D), lambda i: (0, 0))
    return _pallas_call(
        body, name="bwd_ffn_dh", grid=(L // tm,),
        in_specs=[aspec, aspec, wspec, wspec, row, row, pl.BlockSpec((1, D), lambda i: (0, 0)), ANY],
        out_specs=[row, row, pl.BlockSpec((8, D), lambda i: (0, 0))],
        out_shape=[_sds((L, D), F32), _sds((L, D), BF16), _sds((8, D), F32)],
        compiler_params=_params(("arbitrary",), 56),
    )(da, db, wgt, wut, x1, dx2, g_ffn, after)


def _piece_offsets(pieces):
    offsets, total = [], 0
    for p in pieces:
        offsets.append(total)
        total += p.shape[1]
    return offsets, total


def _bwd_in(pieces, w_int, x, dx1, g_mix, after):
    L, D = x.shape
    N = w_int.shape[0]
    tm = min(L, 256)
    n = len(pieces)
    offsets, total = _piece_offsets(pieces)
    assert total == N

    def body(*refs):
        piece_refs = refs[:n]
        w_ref, x_ref, dx1_ref, g_ref, after_ref, dx_ref, red_ref, dp_ref = refs[n:]

        @pl.when(pl.program_id(0) == 0)
        def _():
            red_ref[...] = jnp.zeros_like(red_ref)

        for p_ref, off in zip(piece_refs, offsets):
            dp_ref[:, off:off + p_ref.shape[1]] = p_ref[...]
        dh = _nn(dp_ref[...], w_ref[...])
        r, xh = _rms_stats(x_ref[...])
        red_ref[0:1, :] += jnp.sum(dh * xh, axis=0, keepdims=True)
        dx_ref[...] = dx1_ref[...] + _rms_bwd(dh * g_ref[...], xh, r)

    row = pl.BlockSpec((tm, D), lambda i: (i, 0))
    return _pallas_call(
        body, name="bwd_in", grid=(L // tm,),
        in_specs=[pl.BlockSpec((tm, p.shape[1]), lambda i: (i, 0)) for p in pieces]
        + [pl.BlockSpec((N, D), lambda i: (0, 0)), row, row, pl.BlockSpec((1, D), lambda i: (0, 0)), ANY],
        out_specs=[row, pl.BlockSpec((8, D), lambda i: (0, 0))],
        out_shape=[_sds((L, D), F32), _sds((8, D), F32)],
        scratch_shapes=[pltpu.VMEM((tm, N), BF16)],
        compiler_params=_params(("arbitrary",), 56),
    )(*pieces, w_int, x, dx1, g_mix, after)


def _dw_in(h, pieces, n_cols):
    L, D = h.shape
    tk = min(L, TK_TOKENS // 2)
    n = len(pieces)
    offsets, total = _piece_offsets(pieces)
    assert total == N_CHIPS * n_cols
    plan = []
    for j in range(N_CHIPS):
        lo, hi = j * n_cols, (j + 1) * n_cols
        segments = []
        for p, off in enumerate(offsets):
            a, b = max(lo, off), min(hi, off + pieces[p].shape[1])
            if a < b:
                segments.append((p, a - off, b - a, a - lo))
        plan.append(segments)

    def body(*refs):
        h_ref, piece_refs, o_ref, b_ref = refs[0], refs[1:1 + n], refs[1 + n], refs[2 + n]
        j, k = pl.program_id(0), pl.program_id(1)
        for jj in range(N_CHIPS):
            @pl.when(j == jj)
            def _(jj=jj):
                for p, start, width, at in plan[jj]:
                    b_ref[:, at:at + width] = piece_refs[p][:, start:start + width]

        part = _tn(h_ref[...], b_ref[...])

        @pl.when(k == 0)
        def _():
            o_ref[...] = part

        @pl.when(k > 0)
        def _():
            o_ref[...] += part

    def piece_spec(p):
        used = [j for j in range(N_CHIPS) if any(seg[0] == p for seg in plan[j])]

        def index(j, k):
            in_use = functools.reduce(jnp.logical_or, [j == u for u in used])
            return (jnp.where(in_use, k, 0), 0)

        return pl.BlockSpec((tk, pieces[p].shape[1]), index)

    return _pallas_call(
        body, name="dw_in", grid=(N_CHIPS, L // tk),
        in_specs=[pl.BlockSpec((tk, D), lambda j, k: (k, 0))] + [piece_spec(p) for p in range(n)],
        out_specs=pl.BlockSpec((None, D, n_cols), lambda j, k: (j, 0, 0)),
        out_shape=_sds((N_CHIPS, D, n_cols), F32),
        scratch_shapes=[pltpu.VMEM((tk, n_cols), BF16)],
        compiler_params=_params(("parallel", "arbitrary"), 56),
    )(h, *pieces)


def _mm_tn(name, a, b, a_spec, b_spec, o_block, n_out, n_k):
    def body(a_ref, b_ref, o_ref):
        part = _tn(a_ref[...], b_ref[...])

        @pl.when(pl.program_id(1) == 0)
        def _():
            o_ref[...] = part

        @pl.when(pl.program_id(1) > 0)
        def _():
            o_ref[...] += part

    return _pallas_call(
        body, name=name, grid=(n_out, n_k),
        in_specs=[a_spec, b_spec],
        out_specs=pl.BlockSpec((None,) + o_block, lambda j, k: (j, 0, 0)),
        out_shape=_sds((n_out,) + o_block, F32),
        compiler_params=_params(("parallel", "arbitrary"), 56),
    )(a, b)


TK_TOKENS = 2048


def _dw_cols(name, a, b, n_cols):
    L, M = a.shape
    tk = min(L, TK_TOKENS)
    return _mm_tn(name, a, b, pl.BlockSpec((tk, M), lambda j, k: (k, 0)),
                  pl.BlockSpec((tk, n_cols), lambda j, k: (k, j)), (M, n_cols), N_CHIPS, L // tk)


def _dw_rows(name, a, b):
    L, M = a.shape
    N = b.shape[1]
    tk = min(L, TK_TOKENS)
    return _mm_tn(name, a, b, pl.BlockSpec((tk, M // N_CHIPS), lambda j, k: (k, j)),
                  pl.BlockSpec((tk, N), lambda j, k: (k, 0)), (M // N_CHIPS, N), N_CHIPS, L // tk)


def _dw_rows2(name, a, b):
    L, M = a.shape
    N = b.shape[1]
    tk = min(L, TK_TOKENS)
    return _mm_tn(name, a, b, pl.BlockSpec((tk, M // 2), lambda j, k: (k, j)),
                  pl.BlockSpec((tk, N), lambda j, k: (k, 0)), (M // 2, N), 2, L // tk)


def _place():
    x, y, c = lax.axis_index("x"), lax.axis_index("y"), lax.axis_index("c")
    chips = [(1 - x, y), (x, 1 - y), (1 - x, 1 - y)]
    return x, y, c, 2 * x + y, chips


def _remote(src, dst, send_sem, recv_sem, device):
    return pltpu.make_async_remote_copy(src_ref=src, dst_ref=dst, send_sem=send_sem,
                                        recv_sem=recv_sem, device_id=device, device_id_type=MESH)


def _half(ref, lead, c, r2):
    return ref.at[lead, pl.ds(pl.multiple_of(c * r2, 16), r2), :]


def _cast_place(name, ws, chip_idx):
    n = len(ws)
    r, cols = ws[0].shape
    tr = r // 2

    def body(k_ref, *refs):
        for w_ref, o_ref in zip(refs[:n], refs[n:]):
            o_ref[...] = w_ref[...].astype(BF16)

    return _pallas_call(
        body, name=name,
        grid_spec=pltpu.PrefetchScalarGridSpec(
            num_scalar_prefetch=1, grid=(2,),
            in_specs=[pl.BlockSpec((tr, cols), lambda i, k_ref: (i, 0))] * n,
            out_specs=[pl.BlockSpec((None, tr, cols), lambda i, k_ref: (k_ref[0], i, 0))] * n),
        out_shape=[_sds((N_CHIPS, r, cols), BF16)] * n,
        compiler_params=_params(("parallel",), 48),
    )(chip_idx, *ws)


def _cast_place_t(name, ws, chip_idx):
    n = len(ws)
    r, cols = ws[0].shape

    def body(k_ref, *refs):
        for w_ref, o_ref in zip(refs[:n], refs[n:]):
            o_ref[...] = w_ref[...].T.astype(BF16)

    return _pallas_call(
        body, name=name,
        grid_spec=pltpu.PrefetchScalarGridSpec(
            num_scalar_prefetch=1, grid=(cols // LANES,),
            in_specs=[pl.BlockSpec((r, LANES), lambda i, k_ref: (0, i))] * n,
            out_specs=[pl.BlockSpec((None, LANES, r), lambda i, k_ref: (k_ref[0], i, 0))] * n),
        out_shape=[_sds((N_CHIPS, cols, r), BF16)] * n,
        compiler_params=_params(("parallel",), 48),
    )(chip_idx, *ws)


def _gather_copies(bufs, whole, send_sems, recv_sems, select=None):
    x, y, c, k, chips = _place()
    pairs = []
    for w, buf in enumerate(bufs):
        for j, (cx, cy) in enumerate(chips):
            if select is not None and not select(w, j):
                continue
            if w in whole:
                mine, theirs = buf.at[k], buf.at[2 * cx + cy]
            else:
                r2 = buf.shape[1] // 2
                mine, theirs = _half(buf, k, c, r2), _half(buf, 2 * cx + cy, c, r2)
            sems = (send_sems.at[w * 3 + j], recv_sems.at[w * 3 + j])
            pairs.append((_remote(mine, mine, *sems, (cx, cy, c)), _remote(theirs, theirs, *sems, (x, y, c))))
    return pairs


def _gather_start(name, groups, after):
    flat = [b for bufs, _, _ in groups for b in bufs]
    nb, ng = len(flat), len(groups)

    def body(*refs):
        ins, sems, token = refs[:nb], refs[nb + 1:nb + 1 + 2 * ng], refs[-1]
        pos = 0
        for g, (bufs, whole, select) in enumerate(groups):
            for send, _ in _gather_copies(ins[pos:pos + len(bufs)], whole, sems[2 * g], sems[2 * g + 1], select):
                send.start()
            pos += len(bufs)
        token[...] = jnp.zeros_like(token)

    sem_shapes = []
    for bufs, _, _ in groups:
        sem_shapes += [pltpu.SemaphoreType.DMA((3 * len(bufs),))] * 2
    out = _pallas_call(
        body, name=name,
        in_specs=[HBM] * nb + [ANY], out_specs=tuple([SEM] * (2 * ng) + [HBM] * nb + [VMEM]),
        out_shape=tuple(sem_shapes + [pltpu.HBM(b.shape, b.dtype) for b in flat] + [_sds((8, LANES), F32)]),
        input_output_aliases={i: 2 * ng + i for i in range(nb)},
        compiler_params=pltpu.CompilerParams(has_side_effects=EFFECT),
    )(*flat, after)
    sems, thru, pos = [], [], 2 * ng
    for g, (bufs, _, _) in enumerate(groups):
        sems.append((out[2 * g], out[2 * g + 1]))
        thru.append(list(out[pos:pos + len(bufs)]))
        pos += len(bufs)
    return sems, thru, out[-1]


def _gather_wait(name, bufs, whole, sems, after, select=None):
    nb = len(bufs)

    def body(*refs):
        ins, send_sems, recv_sems = refs[:nb], refs[nb], refs[nb + 1]
        for send, arrival in _gather_copies(ins, whole, send_sems, recv_sems, select):
            send.wait_send()
            arrival.wait_recv()

    return _pallas_call(
        body, name=name,
        in_specs=[HBM] * nb + [SEM, SEM, ANY], out_specs=[HBM] * nb,
        out_shape=[pltpu.HBM(b.shape, b.dtype) for b in bufs],
        input_output_aliases={i: i for i in range(nb)},
        compiler_params=pltpu.CompilerParams(has_side_effects=EFFECT),
    )(*bufs, sems[0], sems[1], after)


def _gather_forward(name, bufs, sources=(0, 1, 2)):
    n = len(bufs)

    def body(*refs):
        outs = refs[n:2 * n]
        send_sems, recv_sems = refs[2 * n:]
        x, y, c, _, chips = _place()
        sends = []
        for w in range(n):
            r2 = outs[w].shape[1] // 2
            for j in sources:
                landed = _half(outs[w], 2 * chips[j][0] + chips[j][1], c, r2)
                sends.append(_remote(landed, landed, send_sems.at[w * 3 + j], recv_sems.at[w * 3 + j],
                                     (x, y, 1 - c)))
        for cp in sends:
            cp.start()
        for w in range(n):
            r2 = outs[w].shape[1] // 2
            for j in sources:
                got = _half(outs[w], 2 * chips[j][0] + chips[j][1], 1 - c, r2)
                _remote(got, got, send_sems.at[w * 3 + j], recv_sems.at[w * 3 + j], (x, y, c)).wait_recv()
        for cp in sends:
            cp.wait_send()

    return _pallas_call(
        body, name=name,
        in_specs=[ANY] * n, out_specs=[ANY] * n,
        out_shape=[_sds(b.shape, b.dtype) for b in bufs],
        input_output_aliases={i: i for i in range(n)},
        scratch_shapes=[pltpu.SemaphoreType.DMA((n * 3,)), pltpu.SemaphoreType.DMA((n * 3,))],
    )(*bufs)


def _rs_sibling(name, grads):
    n = len(grads)

    def body(*refs):
        ins, outs = refs[:n], refs[n:2 * n]
        send_sems, recv_sems = refs[2 * n:]
        x, y, c, _, _ = _place()
        copies = []
        for w in range(n):
            r2 = ins[w].shape[1] // 2
            copies.append(_remote(_half(ins[w], slice(None), 1 - c, r2), outs[w],
                                  send_sems.at[w], recv_sems.at[w], (x, y, 1 - c)))
        for cp in copies:
            cp.start()
        for cp in copies:
            cp.wait()

    return _pallas_call(
        body, name=name,
        in_specs=[ANY] * n, out_specs=[ANY] * n,
        out_shape=[_sds((N_CHIPS, g.shape[1] // 2, g.shape[2]), F32) for g in grads],
        scratch_shapes=[pltpu.SemaphoreType.DMA((n,)), pltpu.SemaphoreType.DMA((n,))],
    )(*grads)


def _rs_add(name, grads3, from_sibling, c_idx):
    n = len(grads3)
    _, r2, cols = from_sibling[0].shape

    def body(c_ref, *refs):
        for g_ref, s_ref, o_ref in zip(refs[:n], refs[n:2 * n], refs[2 * n:]):
            o_ref[...] = (g_ref[...] + s_ref[...]).astype(BF16)

    whole = pl.BlockSpec((None, r2, cols), lambda k, c_ref: (k, 0, 0))
    return _pallas_call(
        body, name=name,
        grid_spec=pltpu.PrefetchScalarGridSpec(
            num_scalar_prefetch=1, grid=(N_CHIPS,),
            in_specs=[pl.BlockSpec((None, r2, cols), lambda k, c_ref: (k, c_ref[0], 0))] * n + [whole] * n,
            out_specs=[whole] * n),
        out_shape=[_sds(from_sibling[0].shape, BF16)] * n,
        compiler_params=_params(("parallel",), 48),
    )(c_idx, *grads3, *from_sibling)


def _split_start(name, arrays, n_sems, pairs_fn):
    n = len(arrays)

    def body(*refs):
        for send, _ in pairs_fn(refs[:n], refs[n], refs[n + 1]):
            send.start()
        refs[-1][...] = jnp.zeros_like(refs[-1])

    out = _pallas_call(
        body, name=name,
        in_specs=[HBM] * n, out_specs=tuple([SEM, SEM] + [HBM] * n + [VMEM]),
        out_shape=tuple([pltpu.SemaphoreType.DMA((n_sems,))] * 2 + [pltpu.HBM(a.shape, a.dtype) for a in arrays]
                        + [_sds((8, LANES), F32)]),
        input_output_aliases={i: 2 + i for i in range(n)},
        compiler_params=pltpu.CompilerParams(has_side_effects=EFFECT),
    )(*arrays)
    return (out[0], out[1]), list(out[2:2 + n]), out[-1]


def _split_wait(name, sems, arrays, pairs_fn, after):
    n = len(arrays)

    def body(*refs):
        for send, arrival in pairs_fn(refs[:n], refs[n], refs[n + 1]):
            send.wait_send()
            arrival.wait_recv()

    return list(_pallas_call(
        body, name=name,
        in_specs=[HBM] * n + [SEM, SEM, ANY], out_specs=[HBM] * n,
        out_shape=[pltpu.HBM(a.shape, a.dtype) for a in arrays],
        input_output_aliases={i: i for i in range(n)},
        compiler_params=pltpu.CompilerParams(has_side_effects=EFFECT),
    )(*arrays, sems[0], sems[1], after))


def _forward_pairs(bufs, send_sems, recv_sems):
    x, y, c, _, chips = _place()
    pairs = []
    for w, buf in enumerate(bufs):
        r2 = buf.shape[1] // 2
        for j, (cx, cy) in enumerate(chips):
            landed, theirs = _half(buf, 2 * cx + cy, c, r2), _half(buf, 2 * cx + cy, 1 - c, r2)
            sems = (send_sems.at[w * 3 + j], recv_sems.at[w * 3 + j])
            pairs.append((_remote(landed, landed, *sems, (x, y, 1 - c)), _remote(theirs, theirs, *sems, (x, y, c))))
    return pairs


def _sibling_pairs(arrays, send_sems, recv_sems):
    x, y, c, _, _ = _place()
    n = len(arrays) // 2
    pairs = []
    for w in range(n):
        r2 = arrays[w].shape[1] // 2
        cp = _remote(_half(arrays[w], slice(None), 1 - c, r2), arrays[n + w], send_sems.at[w], recv_sems.at[w],
                     (x, y, 1 - c))
        pairs.append((cp, cp))
    return pairs


def _ici_pairs(arrays, send_sems, recv_sems):
    x, y, c, _, chips = _place()
    n = len(arrays) // 2
    pairs = []
    for w in range(n):
        for j, (cx, cy) in enumerate(chips):
            cp = _remote(arrays[w].at[2 * cx + cy], arrays[n + w].at[j],
                         send_sems.at[w * 3 + j], recv_sems.at[w * 3 + j], (cx, cy, c))
            pairs.append((cp, cp))
    return pairs


def _rs_sum(name, partials, received, place_idx):
    n = len(partials)
    _, r2, cols = partials[0].shape
    nb = 2
    tr = r2 // nb

    def body(idx_ref, *refs):
        for p_ref, r_ref, o_ref in zip(refs[:n], refs[n:2 * n], refs[2 * n:]):
            o_ref[...] = ((p_ref[...].astype(F32) + r_ref[0].astype(F32))
                          + (r_ref[1].astype(F32) + r_ref[2].astype(F32)))

    return _pallas_call(
        body, name=name,
        grid_spec=pltpu.PrefetchScalarGridSpec(
            num_scalar_prefetch=1, grid=(nb,),
            in_specs=[pl.BlockSpec((None, tr, cols), lambda i, idx: (idx[0], i, 0))] * n
            + [pl.BlockSpec((3, tr, cols), lambda i, idx: (0, i, 0))] * n,
            out_specs=[pl.BlockSpec((tr, cols), lambda i, idx: (idx[1] * nb + i, 0))] * n),
        out_shape=[_sds((2 * r2, cols), F32)] * n,
        compiler_params=_params(("parallel",), 48),
    )(place_idx, *partials, *received)


def _share_pairs(arrays, send_sems, recv_sems):
    x, y, c, _, _ = _place()
    pairs = []
    for w, arr in enumerate(arrays):
        r2 = arr.shape[0] // 2
        mine = arr.at[pl.ds(pl.multiple_of(c * r2, 8), r2), :]
        theirs = arr.at[pl.ds(pl.multiple_of((1 - c) * r2, 8), r2), :]
        sems = (send_sems.at[w], recv_sems.at[w])
        pairs.append((_remote(mine, mine, *sems, (x, y, 1 - c)), _remote(theirs, theirs, *sems, (x, y, c))))
    return pairs


def _small_allreduce(red_mix, red_ffn, red_final, red_hg, g_conv, after):
    rows = N_SMALL_ROWS
    D = red_mix.shape[1]
    H = red_hg.shape[1]

    def body(mix_ref, ffn_ref, fin_ref, hg_ref, cv_ref, after_ref, sum_ref, all_ref, in_ref, send_sems,
             recv_sems):
        in_ref[...] = jnp.zeros_like(in_ref)
        in_ref[0:1, :] = mix_ref[0:1, :]
        in_ref[1:2, :] = ffn_ref[0:1, :]
        in_ref[2:3, :] = fin_ref[0:1, :]
        gam = hg_ref[1:2, 0:HEAD_DIM]
        for h in range(1, H // HEAD_DIM):
            gam = gam + hg_ref[1:2, h * HEAD_DIM:(h + 1) * HEAD_DIM]
        in_ref[3:4, 0:HEAD_DIM] = gam
        in_ref[3:4, HEAD_DIM:2 * HEAD_DIM] = fin_ref[1:2, 0:HEAD_DIM]
        in_ref[4:5, 0:H] = hg_ref[0:1, :]
        in_ref[6:9, 0:H] = cv_ref[...]
        x, y, c, _, _ = _place()
        me = 4 * x + 2 * y + c
        all_ref[me] = in_ref[...]
        copies = []
        for m in range(1, 8):
            mx, my, mc = (m >> 2) & 1, (m >> 1) & 1, m & 1
            px, py, pc = x ^ mx, y ^ my, c ^ mc
            copies.append((_remote(in_ref, all_ref.at[me], send_sems.at[m - 1], recv_sems.at[m - 1],
                                   (px, py, pc)), 4 * px + 2 * py + pc, m))
        for cp, _, _ in copies:
            cp.start()
        for _, peer, m in copies:
            _remote(in_ref, all_ref.at[peer], send_sems.at[m - 1], recv_sems.at[m - 1],
                    (x, y, c)).wait_recv()
        for cp, _, _ in copies:
            cp.wait_send()
        total = all_ref[0]
        for d in range(1, 8):
            total = total + all_ref[d]
        sum_ref[...] = total

    return _pallas_call(
        body, name="small_allreduce", pin=False,
        in_specs=[VMEM] * 5 + [ANY], out_specs=[VMEM, VMEM],
        out_shape=[_sds((rows, D), F32), _sds((8, rows, D), F32)],
        scratch_shapes=[pltpu.VMEM((rows, D), F32), pltpu.SemaphoreType.DMA((7,)),
                        pltpu.SemaphoreType.DMA((7,))],
    )(red_mix, red_ffn, red_final, red_hg, g_conv, after)[0]


def _adamw_math(w, g, m, v):
    m = ADAM_B1 * m + (1.0 - ADAM_B1) * g
    v = ADAM_B2 * v + (1.0 - ADAM_B2) * jnp.square(g)
    m_hat = m / (1.0 - ADAM_B1 ** ADAM_STEP)
    v_hat = v / (1.0 - ADAM_B2 ** ADAM_STEP)
    delta = -ADAM_LR * (m_hat / (jnp.sqrt(v_hat) + ADAM_EPS) + ADAM_WD * w)
    return delta, m, v


def _adamw(name, gs, ws, ms, vs):
    n = len(gs)
    r, cols = gs[0].shape
    tr = r // 4

    def body(*refs):
        ins, outs = refs[:4 * n], refs[4 * n:]
        for j in range(n):
            g_ref, w_ref, m_ref, v_ref = ins[j], ins[n + j], ins[2 * n + j], ins[3 * n + j]
            go_ref, d_ref, mo_ref, vo_ref = outs[4 * j:4 * j + 4]
            g = g_ref[...]
            go_ref[...] = g
            d_ref[...], mo_ref[...], vo_ref[...] = _adamw_math(w_ref[...], g, m_ref[...], v_ref[...])

    blk = pl.BlockSpec((tr, cols), lambda i: (i, 0))
    out = _pallas_call(
        body, name=name, grid=(r // tr,),
        in_specs=[blk] * (4 * n), out_specs=[blk] * (4 * n), out_shape=[_sds((r, cols), F32)] * (4 * n),
        compiler_params=_params(("parallel",), 56),
    )(*gs, *ws, *ms, *vs)
    return [list(out[4 * j:4 * j + 4]) for j in range(n)]


def _small_update(total, chip_idx, ws, ms, vs):
    n = len(ws)
    H = ws[1].shape[1]

    def body(idx_ref, tot_ref, *refs):
        w, m, v, outs = refs[:n], refs[n:2 * n], refs[2 * n:3 * n], refs[3 * n:]
        chip = idx_ref[0]
        p0 = _lower_bound(w[1][...])
        dl0 = p0 * (1.0 - p0) * tot_ref[4:5, 0:H]
        conv = jnp.zeros((3, LANES), F32)
        for k in range(N_CHIPS):
            conv = jnp.where(chip == k, tot_ref[6:9, k * LANES:(k + 1) * LANES], conv)
        grads = [tot_ref[0:1, :], None, tot_ref[3:4, 0:HEAD_DIM], conv, tot_ref[1:2, :], tot_ref[2:3, :]]
        for p in range(n):
            g_ref, d_ref, mo_ref, vo_ref = outs[4 * p:4 * p + 4]
            if p == 1:
                for row, g in ((slice(0, 1), dl0), (slice(1, 2), -dl0)):
                    g_ref[row, :] = g
                    d_ref[row, :], mo_ref[row, :], vo_ref[row, :] = _adamw_math(
                        w[p][row, :], g, m[p][row, :], v[p][row, :])
            else:
                g_ref[...] = grads[p]
                d_ref[...], mo_ref[...], vo_ref[...] = _adamw_math(w[p][...], grads[p], m[p][...], v[p][...])
        outs[4 * n][...] = tot_ref[3:4, HEAD_DIM:2 * HEAD_DIM]

    full = lambda a: pl.BlockSpec(a.shape, lambda i, idx: (0,) * a.ndim)
    out_shape = [_sds(w.shape, F32) for w in ws for _ in range(4)] + [_sds((1, LANES), F32)]
    return _pallas_call(
        body, name="small_update",
        grid_spec=pltpu.PrefetchScalarGridSpec(
            num_scalar_prefetch=1, grid=(1,),
            in_specs=[full(total)] + [full(a) for a in ws + ms + vs],
            out_specs=[full(s) for s in out_shape]),
        out_shape=out_shape,
    )(chip_idx, total, *ws, *ms, *vs)


def kernel(x, norm_mix_g, w_in, lower_bounds, hg_norm_g, conv_w, w_branch_a, w_branch_b, w_out, norm_ffn_g, w_ffn_gate, w_ffn_up, w_ffn_down, norm_final_g, loss_target, m_norm_mix_g, m_w_in, m_lower_bounds, m_hg_norm_g, m_conv_w, m_w_branch_a, m_w_branch_b, m_w_out, m_norm_ffn_g, m_w_ffn_gate, m_w_ffn_up, m_w_ffn_down, m_norm_final_g, v_norm_mix_g, v_w_in, v_lower_bounds, v_hg_norm_g, v_conv_w, v_w_branch_a, v_w_branch_b, v_w_out, v_norm_ffn_g, v_w_ffn_gate, v_w_ffn_up, v_w_ffn_down, v_norm_final_g):
    _, L, D = x.shape
    H = D // 2
    assert lower_bounds.shape == (2, H) and hg_norm_g.shape == (1, HEAD_DIM)
    assert conv_w.shape == (1, 3, LANES) and w_in.shape[2] * N_CHIPS == 11 * H
    x2d, target = x.reshape(L, D), loss_target.reshape(L, D)
    g_final = norm_final_g.reshape(1, D)
    chip = 2 * lax.axis_index("x") + lax.axis_index("y")
    core = lax.axis_index("c")

    tr = lambda w: jnp.transpose(w[0])
    big = [w_in[0], w_branch_a[0], w_branch_b[0], w_out[0], tr(w_ffn_gate), tr(w_ffn_up), w_ffn_down[0]]
    big_m = [m_w_in[0], m_w_branch_a[0], m_w_branch_b[0], m_w_out[0], tr(m_w_ffn_gate), tr(m_w_ffn_up),
             m_w_ffn_down[0]]
    big_v = [v_w_in[0], v_w_branch_a[0], v_w_branch_b[0], v_w_out[0], tr(v_w_ffn_gate), tr(v_w_ffn_up),
             v_w_ffn_down[0]]
    names = ["w_in", "w_branch_a", "w_branch_b", "w_out", "w_ffn_gate", "w_ffn_up", "w_ffn_down"]

    chip_idx = chip.reshape(1).astype(jnp.int32)
    def per_shape(fn, tag, js, *lists):
        groups = {}
        for pos, a in enumerate(lists[0]):
            groups.setdefault(a.shape, []).append(pos)
        results = [None] * len(js)
        for same in groups.values():
            out = fn(tag + names[js[same[0]]], *[[xs[p] for p in same] for xs in lists])
            for q, p in enumerate(same):
                results[p] = out[q]
        return results

    place_t = lambda name, ws: _cast_place_t(name, ws, chip_idx)
    place = lambda name, ws: _cast_place(name, ws, chip_idx)
    placed = per_shape(place_t, "place_", [0, 1, 2], big[:3]) + per_shape(place, "place_", [3, 4, 5, 6], big[3:])
    conv_placed = lax.dynamic_update_slice(jnp.zeros((N_CHIPS, 3, LANES), F32), conv_w, (chip, 0, 0))
    x_i, y_i = lax.axis_index("x"), lax.axis_index("y")
    blocks = lambda *ks: jnp.stack(ks).astype(jnp.int32)
    near = lambda w, j: j < 2
    far = lambda w, j: w == 1 or j == 2
    near_sems, in_flight, _ = _gather_start("gather_start_near", [([placed[0]], set(), near)], chip_idx)
    w_in_buf = in_flight[0][0]
    h, proj = _fwd_proj_first(x2d, norm_mix_g, w_in_buf, blocks(chip))
    sems, in_flight, _ = _gather_start(
        "gather_start_rest", [([w_in_buf, conv_placed], {1}, far), (placed[1:4], set(), None),
                              (placed[4:], set(), None)], h)
    w_in_buf, conv_buf = in_flight[0]
    (w_in_buf,) = _gather_wait("gather_wait_in_near", [w_in_buf], set(), near_sems[0], h, near)
    (w_in_buf,) = _gather_forward("gather_fwd_in_near", [w_in_buf], (0, 1))
    proj = _fwd_proj_more("fwd_proj_near", h, w_in_buf, proj,
                          blocks(2 * (1 - x_i) + y_i, 2 * x_i + (1 - y_i)))
    w_in_buf, conv_all = _gather_wait("gather_wait_in_far", [w_in_buf, conv_buf], {1}, sems[0], proj, far)
    (w_int3,) = _gather_forward("gather_fwd_in_far", [w_in_buf], (2,))
    proj = _fwd_proj_more("fwd_proj_far", h, w_int3, proj, blocks(2 * (1 - x_i) + (1 - y_i)))
    w_int = w_int3.reshape(-1, D)
    conv_full = jnp.transpose(conv_all, (1, 0, 2)).reshape(3, H)
    og, o_pre, s_saved = _hgrn_fwd(proj, lower_bounds, hg_norm_g, H)
    landed = _gather_wait("gather_wait_mix", in_flight[1], set(), sems[1], og)
    fwd_sems, landed, token = _split_start("gather_fwd_mix_start", landed, 9, _forward_pairs)
    cb = _conv_fwd(proj, conv_full, H, token)
    wat3, wbt3, wout3 = _split_wait("gather_fwd_mix_wait", fwd_sems, landed, _forward_pairs, cb)
    wat, wbt, wout = wat3.reshape(D, H), wbt3.reshape(D, H), wout3.reshape(D, D)
    landed = _gather_wait("gather_wait_ffn", in_flight[2], set(), sems[2], cb)
    fwd_sems, landed, token = _split_start("gather_fwd_ffn_start", landed, 9, _forward_pairs)
    sig_a, sig_b, dm_dga, dm_dgb, merged, x1, h2 = _fwd_mix(og, cb, proj, x2d, wat, wbt, wout, norm_ffn_g,
                                                              H, token)
    wgt3, wut3, wd3 = _split_wait("gather_fwd_ffn_wait", fwd_sems, landed, _forward_pairs, h2)
    d_ff = N_CHIPS * wd3.shape[1]
    wgt, wut, wd = wgt3.reshape(d_ff, D), wut3.reshape(d_ff, D), wd3.reshape(d_ff, D)
    ffn_ds_da, ffn_ds_db, ffn_s = _fwd_ffn_up(h2, wgt, wut)
    dx2, dx2b, red_final = _fwd_down_loss(ffn_s, wd, x1, target, g_final)

    c_idx = core.reshape(1).astype(jnp.int32)
    place_idx = jnp.stack([chip, core]).astype(jnp.int32)

    def sibling_start(tag, grads):
        bufs = [lax.empty((N_CHIPS, g.shape[1] // 2, g.shape[2]), F32) for g in grads]
        return _split_start("rs_sibling_start_" + tag, list(grads) + bufs, len(grads), _sibling_pairs)

    def ici_start(tag, js, grads, from_sibling):
        partials = per_shape(lambda name, g, s: _rs_add(name, g, s, c_idx), "rs_add_", js, grads, from_sibling)
        landings = [lax.empty((3,) + p.shape[1:], BF16) for p in partials]
        return _split_start("rs_ici_start_" + tag, partials + landings, 3 * len(js), _ici_pairs)

    def ici_start_behind(tag, js, started, after):
        n = len(js)
        arrays = _split_wait("rs_sibling_wait_" + tag, started[0], started[1], _sibling_pairs, after)
        return ici_start(tag, js, arrays[:n], arrays[n:])

    def sums(tag, js, started, after):
        n = len(js)
        arrays = _split_wait("rs_ici_wait_" + tag, started[0], started[1], _ici_pairs, after)
        return per_shape(lambda name, p, r: _rs_sum(name, p, r, place_idx), "rs_sum_", js, arrays[:n], arrays[n:])

    def adamw(js, grads):
        return per_shape(_adamw, "adamw_", js, grads, *[[src[j] for j in js] for src in (big, big_m, big_v)])

    shards3 = lambda g: g.reshape(N_CHIPS, d_ff // N_CHIPS, D)
    da, db = _bwd_down(dx2b, wd, ffn_ds_da, ffn_ds_db)
    g_wd = shards3(_dw_rows2("dw_ffn_down", ffn_s, dx2b))
    g_wg = shards3(_dw_rows2("dw_ffn_gate", da, h2))
    g_wu = shards3(_dw_rows2("dw_ffn_up", db, h2))
    ffn_sibling = sibling_start("ffn", [g_wg, g_wu, g_wd])
    dx1, dx1b, red_ffn = _bwd_ffn_dh(da, db, wgt, wut, x1, dx2, norm_ffn_g, ffn_sibling[2])
    ffn_ici = ici_start_behind("ffn", [4, 5, 6], ffn_sibling, dx1b)
    dya, dyb, d_gates, d_o, d_cb = _bwd_mix(dx1b, sig_a, sig_b, dm_dga, dm_dgb, wat, wbt, wout, H, ffn_ici[2])
    g_wout = _dw_rows("dw_out", merged, dx1b)
    g_wa = _dw_cols("dw_branch_a", og, dya, D // N_CHIPS)
    g_wb = _dw_cols("dw_branch_b", cb, dyb, D // N_CHIPS)
    mix_sibling = sibling_start("mix", [g_wa, g_wb, g_wout])
    d_hgrn, red_hg = _hgrn_bwd(proj, lower_bounds, hg_norm_g, o_pre, d_o, s_saved, H, mix_sibling[2])
    mix_ici = ici_start_behind("mix", [1, 2, 3], mix_sibling, d_hgrn)
    dcg, dbg, dxb, g_conv = _conv_bwd(proj, conv_full, d_cb, H, mix_ici[2])
    dproj = [d_hgrn, dcg, dbg, dxb, d_gates]
    g_win = _dw_in(h, dproj, w_int3.shape[1])
    in_sibling = sibling_start("in", [g_win])
    halves = sums("mix", [1, 2, 3], mix_ici, in_sibling[2]) + sums("ffn", [4, 5, 6], ffn_ici, in_sibling[2])
    rest_share = _split_start("rs_share_start_rest", halves, len(halves), _share_pairs)
    in_ici = ici_start_behind("in", [0], in_sibling, rest_share[2])
    grad_x, red_mix = _bwd_in(dproj, w_int, x2d, dx1, norm_mix_g, in_ici[2])
    in_share = _split_start("rs_share_start_in", sums("in", [0], in_ici, grad_x), 1, _share_pairs)
    total = _small_allreduce(red_mix, red_ffn, red_final, red_hg, g_conv, in_share[2])
    rest_grads = _split_wait("rs_share_wait_rest", rest_share[0], rest_share[1], _share_pairs, total)
    big_out = [None] + adamw([1, 2, 3, 4, 5, 6], rest_grads)
    in_grad = _split_wait("rs_share_wait_in", in_share[0], in_share[1], _share_pairs, big_out[6][0])
    big_out[0] = adamw([0], in_grad)[0]

    def smalls(mix, lb, hg, cw, ffn, fin):
        return [mix, lb, hg, cw[0], ffn, fin.reshape(1, D)]

    small_out = _small_update(
        total, chip_idx,
        smalls(norm_mix_g, lower_bounds, hg_norm_g, conv_w, norm_ffn_g, norm_final_g),
        smalls(m_norm_mix_g, m_lower_bounds, m_hg_norm_g, m_conv_w, m_norm_ffn_g, m_norm_final_g),
        smalls(v_norm_mix_g, v_lower_bounds, v_hg_norm_g, v_conv_w, v_norm_ffn_g, v_norm_final_g))

    def outputs(i):
        big_i = [big_out[j][i] for j in range(7)]
        mix, lb, hg, cw, ffn, fin = [small_out[4 * p + i] for p in range(6)]
        return [mix, big_i[0][None], lb, hg, cw[None], big_i[1][None], big_i[2][None], big_i[3][None], ffn,
                big_i[4].T[None], big_i[5].T[None], big_i[6][None], fin.reshape(D)]

    outs = [small_out[24][0, 0], grad_x.reshape(1, L, D)]
    for i in range(4):
        outs += outputs(i)
    return tuple(outs)
```

```python
import functools

import jax
import jax.numpy as jnp
from jax import lax
from jax.experimental import pallas as pl
from jax.experimental.pallas import tpu as pltpu

F32 = jnp.float32
BF16 = jnp.bfloat16
EPS = 1e-6
CHUNK = 32
HEAD_DIM = 128
LANES = 128
N_CHIPS = 4
N_SMALL_ROWS = 16

ADAM_LR = 0.001
ADAM_B1 = 0.9
ADAM_B2 = 0.999
ADAM_EPS = 1e-08
ADAM_WD = 0.01
ADAM_STEP = 10

MESH = pl.DeviceIdType.MESH
ANY = pl.BlockSpec(memory_space=pl.ANY)
VMEM = pl.BlockSpec(memory_space=pltpu.VMEM)
HBM = pl.BlockSpec(memory_space=pltpu.HBM)
SEM = pl.BlockSpec(memory_space=pltpu.SEMAPHORE)
EFFECT = pltpu.SideEffectType.DATAFLOW_SIDE_EFFECTING


def _sds(shape, dtype):
    return jax.ShapeDtypeStruct(shape, dtype)


def _pallas_call(body, pin=True, **kwargs):
    if not pin:
        return pl.pallas_call(body, **kwargs)
    in_hbm = lambda s: pltpu.HBM(s.shape, s.dtype) if isinstance(s, jax.ShapeDtypeStruct) else s
    kwargs["out_shape"] = jax.tree.map(in_hbm, kwargs["out_shape"])
    call = pl.pallas_call(body, **kwargs)

    def run(*args):
        return call(*[pltpu.with_memory_space_constraint(a, pltpu.HBM) if a.dtype in (F32, BF16) else a
                      for a in args])

    return run


def _params(semantics, vmem_mb):
    return pltpu.CompilerParams(dimension_semantics=semantics, vmem_limit_bytes=vmem_mb << 20)


def _nn(a, b):
    return lax.dot_general(a, b, (((1,), (0,)), ((), ())), preferred_element_type=F32)


def _nt(a, b):
    return lax.dot_general(a, b, (((1,), (1,)), ((), ())), preferred_element_type=F32)


def _tn(a, b):
    return lax.dot_general(a, b, (((0,), (0,)), ((), ())), preferred_element_type=F32)


def _sigmoid(x):
    return jax.nn.sigmoid(x)


def _rms_stats(x):
    r = lax.rsqrt(jnp.mean(x * x, axis=-1, keepdims=True) + EPS)
    return r, x * r


def _rms_bwd(dxh, xh, r):
    return r * (dxh - xh * jnp.mean(dxh * xh, axis=-1, keepdims=True))


def _fwd_proj_first(x, g_mix, w_int3, block):
    L, D = x.shape
    tn = w_int3.shape[1]
    tm = min(L, 1024)

    def body(blk_ref, x_ref, g_ref, w_ref, h_ref, p_ref):
        _, xh = _rms_stats(x_ref[...])
        h = (xh * g_ref[...]).astype(BF16)
        h_ref[...] = h
        p_ref[...] = _nt(h, w_ref[...])

    return _pallas_call(
        body, name="fwd_proj_own",
        grid_spec=pltpu.PrefetchScalarGridSpec(
            num_scalar_prefetch=1, grid=(L // tm,),
            in_specs=[pl.BlockSpec((tm, D), lambda i, blk: (i, 0)),
                      pl.BlockSpec((1, D), lambda i, blk: (0, 0)),
                      pl.BlockSpec((None, tn, D), lambda i, blk: (blk[0], 0, 0))],
            out_specs=[pl.BlockSpec((tm, D), lambda i, blk: (i, 0)),
                       pl.BlockSpec((tm, tn), lambda i, blk: (i, blk[0]))]),
        out_shape=[_sds((L, D), BF16), _sds((L, N_CHIPS * tn), F32)],
        compiler_params=_params(("parallel",), 48),
    )(block, x, g_mix, w_int3)


def _fwd_proj_more(name, h, w_int3, proj, blocks):
    L, D = h.shape
    tn = w_int3.shape[1]
    tm = min(L, 1024)

    def body(blk_ref, h_ref, w_ref, proj_ref, p_ref):
        p_ref[...] = _nt(h_ref[...], w_ref[...])

    return _pallas_call(
        body, name=name,
        grid_spec=pltpu.PrefetchScalarGridSpec(
            num_scalar_prefetch=1, grid=(L // tm, blocks.shape[0]),
            in_specs=[pl.BlockSpec((tm, D), lambda i, j, blk: (i, 0)),
                      pl.BlockSpec((None, tn, D), lambda i, j, blk: (blk[j], 0, 0)), ANY],
            out_specs=pl.BlockSpec((tm, tn), lambda i, j, blk: (i, blk[j]))),
        out_shape=_sds(proj.shape, proj.dtype),
        input_output_aliases={3: 0},
        compiler_params=_params(("parallel", "arbitrary"), 48),
    )(blocks, h, w_int3, proj)


def _lower_bound(lbp):
    l0, l1 = lbp[0:1, :], lbp[1:2, :]
    m = jnp.maximum(l0, l1)
    e0, e1 = jnp.exp(l0 - m), jnp.exp(l1 - m)
    return e0 / (e0 + e1)


def _seg_scan(x, r32, forward):
    n = x.shape[0]
    s = 1
    while s < CHUNK:
        if forward:
            x = x + jnp.where(r32 >= s, pltpu.roll(x, s, 0), 0.0)
        else:
            x = x + jnp.where(r32 < CHUNK - s, pltpu.roll(x, n - s, 0), 0.0)
        s *= 2
    return x


def _bcast_row(x, row):
    n, w = x.shape
    nc = n // CHUNK
    x3 = x.reshape(nc, CHUNK, w)
    return jnp.broadcast_to(x3[:, row:row + 1, :], (nc, CHUNK, w)).reshape(n, w)


def _hgrn_prep(q_raw, f_raw, lb):
    r32 = lax.broadcasted_iota(jnp.int32, f_raw.shape, 0) & (CHUNK - 1)
    sig = _sigmoid(f_raw)
    f = lb + (1.0 - lb) * sig
    b = _seg_scan(jnp.log(f), r32, True)
    a = _bcast_row(b, CHUNK // 2 - 1)
    bl = _bcast_row(b, CHUNK - 1)
    sq = _sigmoid(q_raw)
    q = q_raw * sq * (HEAD_DIM ** -0.5)
    return dict(r32=r32, sig=sig, f=f, k=1.0 - f, b=b, a=a, bl=bl, sq=sq, q=q)


def _chunk_masks(n):
    ri = lax.broadcasted_iota(jnp.int32, (n, n), 0)
    ci = lax.broadcasted_iota(jnp.int32, (n, n), 1)
    same = (ri // CHUNK) == (ci // CHUNK)
    return same & (ci <= ri), same & (ri <= ci)


def _hgrn_fwd(proj, lower_bounds, gamma, H):
    L = proj.shape[0]
    nh = H // HEAD_DIM
    TL = min(L, 256)
    nc = TL // CHUNK

    def body(q_ref, f_ref, v_ref, g_ref, lbp_ref, gam_ref, og_ref, o_ref, s_ref, st_ref):
        @pl.when(pl.program_id(0) == 0)
        def _():
            st_ref[...] = jnp.zeros_like(st_ref)

        lb = _lower_bound(lbp_ref[...])
        gam = gam_ref[...]
        mask, _ = _chunk_masks(TL)
        rowc = lax.broadcasted_iota(jnp.int32, (TL, HEAD_DIM), 0) // CHUNK
        for h in range(nh):
            hs = slice(h * HEAD_DIM, (h + 1) * HEAD_DIM)
            p = _hgrn_prep(q_ref[:, hs], f_ref[:, hs], lb[:, hs])
            v = v_ref[:, hs]
            vb = v.astype(BF16)
            vt = v.T.astype(BF16)
            q_hat = (p["q"] * jnp.exp(p["b"] - p["a"])).astype(BF16)
            k_hat = (p["k"] * jnp.exp(p["a"] - p["b"])).astype(BF16)
            q_in = (p["q"] * jnp.exp(p["b"])).astype(BF16)
            k_out = (p["k"] * jnp.exp(p["bl"] - p["b"])).astype(BF16)
            dec = jnp.exp(p["bl"])
            att = jnp.where(mask, _nt(q_hat, k_hat), 0.0).astype(BF16)
            o_intra = _nn(att, vb)
            st = st_ref[h]
            for c in range(nc):
                rs = slice(c * CHUNK, (c + 1) * CHUNK)
                stb = st.astype(BF16)
                s_ref[c, h] = stb
                o_ref[rs, hs] = o_intra[rs] + _nt(q_in[rs], stb)
                k_c = jnp.where(rowc == c, k_out, jnp.zeros_like(k_out))
                st = st * dec[c * CHUNK:c * CHUNK + 1, :] + _nn(vt, k_c)
            st_ref[h] = st
            o = o_ref[:, hs]
            _, xh = _rms_stats(o)
            gr = g_ref[:, hs]
            og_ref[:, hs] = (xh * gam * (gr * _sigmoid(gr))).astype(BF16)

    col = lambda k: pl.BlockSpec((TL, H), lambda i, k=k: (i, k))
    return _pallas_call(
        body, name="hgrn_fwd", grid=(L // TL,),
        in_specs=[col(0), col(1), col(2), col(3),
                  pl.BlockSpec(lower_bounds.shape, lambda i: (0, 0)),
                  pl.BlockSpec(gamma.shape, lambda i: (0, 0))],
        out_specs=[pl.BlockSpec((TL, H), lambda i: (i, 0)),
                   pl.BlockSpec((TL, H), lambda i: (i, 0)),
                   pl.BlockSpec((nc, nh, HEAD_DIM, HEAD_DIM), lambda i: (i, 0, 0, 0))],
        out_shape=[_sds((L, H), BF16), _sds((L, H), F32),
                   _sds((L // CHUNK, nh, HEAD_DIM, HEAD_DIM), BF16)],
        scratch_shapes=[pltpu.VMEM((nh, HEAD_DIM, HEAD_DIM), F32)],
        compiler_params=_params(("arbitrary",), 48),
    )(proj, proj, proj, proj, lower_bounds, gamma)


def _hgrn_bwd(proj, lower_bounds, gamma, o_pre, d_out, s_saved, H, after):
    L = proj.shape[0]
    nh = H // HEAD_DIM
    TL = min(L, 256)
    nc = TL // CHUNK
    nt = L // TL

    def body(q_ref, f_ref, v_ref, g_ref, lbp_ref, gam_ref, o_ref, d_ref, s_ref, after_ref,
             dp_ref, red_ref, dst_ref, dsall_ref, tmp_ref):
        @pl.when(pl.program_id(0) == 0)
        def _():
            dst_ref[...] = jnp.zeros_like(dst_ref)
            red_ref[...] = jnp.zeros_like(red_ref)

        lb = _lower_bound(lbp_ref[...])
        gam = gam_ref[...]
        mask, mask_t = _chunk_masks(TL)
        rowc = lax.broadcasted_iota(jnp.int32, (TL, HEAD_DIM), 0) // CHUNK
        for h in range(nh):
            hs = slice(h * HEAD_DIM, (h + 1) * HEAD_DIM)
            qr, gr, lbh = q_ref[:, hs], g_ref[:, hs], lb[:, hs]
            p = _hgrn_prep(qr, f_ref[:, hs], lbh)
            vb = v_ref[:, hs].astype(BF16)
            eba, eab = jnp.exp(p["b"] - p["a"]), jnp.exp(p["a"] - p["b"])
            eb, elb = jnp.exp(p["b"]), jnp.exp(p["bl"] - p["b"])
            dec = jnp.exp(p["bl"])
            q_hat, k_hat = p["q"] * eba, p["k"] * eab
            q_in, k_out = p["q"] * eb, p["k"] * elb
            q_hat_b, k_hat_b = q_hat.astype(BF16), k_hat.astype(BF16)
            q_in_b, k_out_b = q_in.astype(BF16), k_out.astype(BF16)

            o, dout = o_ref[:, hs], d_ref[:, hs]
            sg = _sigmoid(gr)
            r, xh = _rms_stats(o)
            hq, hf, hv, hg = [slice(k * H + h * HEAD_DIM, k * H + (h + 1) * HEAD_DIM) for k in range(4)]
            dp_ref[:, hg] = (dout * (xh * gam) * (sg * (1.0 + gr * (1.0 - sg)))).astype(BF16)
            dn = dout * (gr * sg)
            red_ref[1:2, hs] += jnp.sum(dn * xh, axis=0, keepdims=True)
            do = _rms_bwd(dn * gam, xh, r)
            dob = do.astype(BF16)
            dot_b = do.T.astype(BF16)

            att_t = jnp.where(mask_t, _nt(k_hat_b, q_hat_b), 0.0).astype(BF16)
            dv_intra = _nn(att_t, dob)
            datt = jnp.where(mask, _nt(dob, vb), 0.0).astype(BF16)
            dqh = _nn(datt, k_hat_b)
            datt_t = jnp.where(mask_t, _nt(vb, dob), 0.0).astype(BF16)
            dkh = _nn(datt_t, q_hat_b)

            dst = dst_ref[h]
            for c in reversed(range(nc)):
                dsall_ref[c] = dst
                q_c = jnp.where(rowc == c, q_in_b, jnp.zeros_like(q_in_b))
                dst = dst * dec[c * CHUNK:c * CHUNK + 1, :] + _nn(dot_b, q_c)
            dst_ref[h] = dst
            for c in range(nc):
                rs = slice(c * CHUNK, (c + 1) * CHUNK)
                ds_c = dsall_ref[c]
                dsb = ds_c.astype(BF16)
                st_prev = s_ref[c, h]
                tmp_ref[0, rs, :] = _nt(k_out_b[rs], dsb)
                tmp_ref[1, rs, :] = _nn(vb[rs], dsb)
                tmp_ref[2, rs, :] = _nn(dob[rs], st_prev)
                ddec = jnp.sum(ds_c * st_prev.astype(F32), axis=0, keepdims=True)
                tmp_ref[3, rs, :] = jnp.broadcast_to(ddec * dec[c * CHUNK:c * CHUNK + 1, :],
                                                     (CHUNK, HEAD_DIM))
            dko, dqi = tmp_ref[1], tmp_ref[2]
            dq = dqh * eba + dqi * eb
            dk = dkh * eab + dko * elb
            tko = dko * k_out
            db = dqh * q_hat - dkh * k_hat + dqi * q_in - tko
            dlog = (_seg_scan(db, p["r32"], False)
                    + _bcast_row(_seg_scan(tko, p["r32"], True), CHUNK - 1) + tmp_ref[3])
            df = dlog / p["f"] - dk
            sig = p["sig"]
            red_ref[0:1, hs] += jnp.sum(df * (1.0 - sig), axis=0, keepdims=True)
            dp_ref[:, hf] = (df * (1.0 - lbh) * sig * (1.0 - sig)).astype(BF16)
            sq = p["sq"]
            dp_ref[:, hq] = (dq * (HEAD_DIM ** -0.5) * (sq * (1.0 + qr * (1.0 - sq)))).astype(BF16)
            dp_ref[:, hv] = (dv_intra + tmp_ref[0]).astype(BF16)

    col = lambda k: pl.BlockSpec((TL, H), lambda i, k=k: (nt - 1 - i, k))
    rev = pl.BlockSpec((TL, H), lambda i: (nt - 1 - i, 0))
    return _pallas_call(
        body, name="hgrn_bwd", grid=(nt,),
        in_specs=[col(0), col(1), col(2), col(3),
                  pl.BlockSpec(lower_bounds.shape, lambda i: (0, 0)),
                  pl.BlockSpec(gamma.shape, lambda i: (0, 0)),
                  rev, rev,
                  pl.BlockSpec((nc, nh, HEAD_DIM, HEAD_DIM), lambda i: (nt - 1 - i, 0, 0, 0)), ANY],
        out_specs=[pl.BlockSpec((TL, 4 * H), lambda i: (nt - 1 - i, 0)), pl.BlockSpec((8, H), lambda i: (0, 0))],
        out_shape=[_sds((L, 4 * H), BF16), _sds((8, H), F32)],
        scratch_shapes=[pltpu.VMEM((nh, HEAD_DIM, HEAD_DIM), F32),
                        pltpu.VMEM((nc, HEAD_DIM, HEAD_DIM), F32),
                        pltpu.VMEM((4, TL, HEAD_DIM), F32)],
        compiler_params=_params(("arbitrary",), 48),
    )(proj, proj, proj, proj, lower_bounds, gamma, o_pre, d_out, s_saved, after)


def _shift_down(u, s, row):
    return jnp.where(row >= s, pltpu.roll(u, s, 0), 0.0)


def _shift_up(u, s, row):
    n = u.shape[0]
    return jnp.where(row < n - s, pltpu.roll(u, n - s, 0), 0.0)


def _conv_specs(L, H):
    per = H // LANES
    return [pl.BlockSpec((L, LANES), lambda j, o=o: (0, o * per + j)) for o in (4, 5, 6)]


def _conv_fwd(proj, conv_w, H, after):
    L = proj.shape[0]

    def body(c_ref, b_ref, x_ref, w_ref, after_ref, o_ref):
        row = lax.broadcasted_iota(jnp.int32, (L, LANES), 0)
        u = c_ref[...] * x_ref[...]
        w = w_ref[...]
        y = w[0:1] * _shift_down(u, 2, row) + w[1:2] * _shift_down(u, 1, row) + w[2:3] * u
        o_ref[...] = (b_ref[...] * y).astype(BF16)

    return _pallas_call(
        body, name="conv_fwd", grid=(H // LANES,),
        in_specs=_conv_specs(L, H) + [pl.BlockSpec((3, LANES), lambda j: (0, j)), ANY],
        out_specs=pl.BlockSpec((L, LANES), lambda j: (0, j)),
        out_shape=_sds((L, H), BF16),
        compiler_params=_params(("parallel",), 48),
    )(proj, proj, proj, conv_w, after)


def _conv_bwd(proj, conv_w, dcb, H, after):
    L = proj.shape[0]

    def body(c_ref, b_ref, x_ref, w_ref, d_ref, after_ref, dc_ref, db_ref, dx_ref, dw_ref):
        row = lax.broadcasted_iota(jnp.int32, (L, LANES), 0)
        cg, xb = c_ref[...], x_ref[...]
        u = cg * xb
        u1, u2 = _shift_down(u, 1, row), _shift_down(u, 2, row)
        w = w_ref[...]
        y = w[0:1] * u2 + w[1:2] * u1 + w[2:3] * u
        d = d_ref[...]
        db_ref[...] = (d * y).astype(BF16)
        dy = d * b_ref[...]
        du = w[2:3] * dy + w[1:2] * _shift_up(dy, 1, row) + w[0:1] * _shift_up(dy, 2, row)
        dw_ref[0:1, :] = jnp.sum(dy * u2, axis=0, keepdims=True)
        dw_ref[1:2, :] = jnp.sum(dy * u1, axis=0, keepdims=True)
        dw_ref[2:3, :] = jnp.sum(dy * u, axis=0, keepdims=True)
        dc_ref[...] = (du * xb).astype(BF16)
        dx_ref[...] = (du * cg).astype(BF16)

    blk = pl.BlockSpec((L, LANES), lambda j: (0, j))
    return _pallas_call(
        body, name="conv_bwd", grid=(H // LANES,),
        in_specs=_conv_specs(L, H) + [pl.BlockSpec((3, LANES), lambda j: (0, j)), blk, ANY],
        out_specs=[blk, blk, blk, pl.BlockSpec((3, LANES), lambda j: (0, j))],
        out_shape=[_sds((L, H), BF16)] * 3 + [_sds((3, H), F32)],
        compiler_params=_params(("parallel",), 56),
    )(proj, proj, proj, conv_w, dcb, after)


def _gate_specs(tm, H):
    return [pl.BlockSpec((tm, H), lambda i, k=k: (i, k)) for k in (7, 8, 9, 10)]


def _fwd_mix(og, cb, proj, x, wat, wbt, wout, g_ffn, H, after):
    L, D = x.shape
    tm = min(L, 512)

    def body(o_ref, cb_ref, ga0, ga1, gb0, gb1, x_ref, wa_ref, wb_ref, wo_ref, g_ref, after_ref,
             sa_ref, sb_ref, ta_ref, tb_ref, m_ref, x1_ref, h2_ref):
        ya, yb = _nt(o_ref[...], wa_ref[...]), _nt(cb_ref[...], wb_ref[...])
        for k, (gar, gbr) in enumerate(((ga0, gb0), (ga1, gb1))):
            cs = slice(k * H, (k + 1) * H)
            sa, sb = _sigmoid(gar[...]), _sigmoid(gbr[...])
            ma, mb = sa * ya[:, cs], sb * yb[:, cs]
            m_ref[:, cs] = (ma + mb).astype(BF16)
            sa_ref[:, cs] = sa.astype(BF16)
            sb_ref[:, cs] = sb.astype(BF16)
            ta_ref[:, cs] = (ma * (1.0 - sa)).astype(BF16)
            tb_ref[:, cs] = (mb * (1.0 - sb)).astype(BF16)
        x1 = x_ref[...] + _nn(m_ref[...], wo_ref[...])
        x1_ref[...] = x1
        _, xh = _rms_stats(x1)
        h2_ref[...] = (xh * g_ref[...]).astype(BF16)

    row = lambda w: pl.BlockSpec((tm, w), lambda i: (i, 0))
    full = lambda a: pl.BlockSpec(a.shape, lambda i: (0,) * a.ndim)
    return _pallas_call(
        body, name="fwd_mix", grid=(L // tm,),
        in_specs=[row(H), row(H)] + _gate_specs(tm, H) + [row(D), full(wat), full(wbt), full(wout),
                                                           full(g_ffn), ANY],
        out_specs=[row(D)] * 7,
        out_shape=[_sds((L, D), BF16)] * 5 + [_sds((L, D), F32), _sds((L, D), BF16)],
        compiler_params=_params(("parallel",), 56),
    )(og, cb, proj, proj, proj, proj, x, wat, wbt, wout, g_ffn, after)


def _bwd_mix(dx1b, sig_a, sig_b, dm_dga, dm_dgb, wat, wbt, wout, H, after):
    L, D = dx1b.shape
    tm = min(L, 512)

    def body(dx_ref, sa_ref, sb_ref, ta_ref, tb_ref, wa_ref, wb_ref, wo_ref, after_ref,
             dya_ref, dyb_ref, dgate_ref, do_ref, dcb_ref):
        dm = _nt(dx_ref[...], wo_ref[...])
        dgate_ref[:, 0:D] = (dm * ta_ref[...].astype(F32)).astype(BF16)
        dgate_ref[:, D:2 * D] = (dm * tb_ref[...].astype(F32)).astype(BF16)
        dya_ref[...] = (dm * sa_ref[...].astype(F32)).astype(BF16)
        dyb_ref[...] = (dm * sb_ref[...].astype(F32)).astype(BF16)
        do_ref[...] = _nn(dya_ref[...], wa_ref[...])
        dcb_ref[...] = _nn(dyb_ref[...], wb_ref[...])

    row = lambda w: pl.BlockSpec((tm, w), lambda i: (i, 0))
    full = lambda a: pl.BlockSpec(a.shape, lambda i: (0,) * a.ndim)
    return _pallas_call(
        body, name="bwd_mix", grid=(L // tm,),
        in_specs=[row(D)] * 5 + [full(wat), full(wbt), full(wout), ANY],
        out_specs=[row(D), row(D), row(2 * D), row(H), row(H)],
        out_shape=[_sds((L, D), BF16)] * 2 + [_sds((L, 2 * D), BF16)] + [_sds((L, H), F32)] * 2,
        compiler_params=_params(("parallel",), 56),
    )(dx1b, sig_a, sig_b, dm_dga, dm_dgb, wat, wbt, wout, after)


def _fwd_ffn_up(h2, wgt, wut):
    L, D = h2.shape
    F = wgt.shape[0]
    tn = F // 2
    tm = min(L, 512)

    def body(h_ref, wg_ref, wu_ref, sa_ref, sb_ref, s_ref):
        h = h_ref[...]
        a, b = _nt(h, wg_ref[...]), _nt(h, wu_ref[...])
        sg = _sigmoid(a)
        silu = a * sg
        sa_ref[...] = (b * sg * (1.0 + a * (1.0 - sg))).astype(BF16)
        sb_ref[...] = silu.astype(BF16)
        s_ref[...] = (silu * b).astype(BF16)

    wspec = pl.BlockSpec((tn, D), lambda j, i: (j, 0))
    ospec = pl.BlockSpec((tm, tn), lambda j, i: (i, j))
    return _pallas_call(
        body, name="fwd_ffn_up", grid=(2, L // tm),
        in_specs=[pl.BlockSpec((tm, D), lambda j, i: (i, 0)), wspec, wspec],
        out_specs=[ospec] * 3,
        out_shape=[_sds((L, F), BF16)] * 3,
        compiler_params=_params(("parallel", "parallel"), 48),
    )(h2, wgt, wut)


def _fwd_down_loss(s, wd, x1, target, g_final):
    L, D = x1.shape
    F = wd.shape[0]
    tm = min(L, 256)

    def body(s_ref, wd_ref, x1_ref, t_ref, g_ref, dx_ref, dxb_ref, red_ref):
        @pl.when(pl.program_id(0) == 0)
        def _():
            red_ref[...] = jnp.zeros_like(red_ref)

        g = g_ref[...]
        r, xh = _rms_stats(x1_ref[...] + _nn(s_ref[...], wd_ref[...]))
        e = xh * g - t_ref[...]
        dy = e * (1.0 / D)
        dx = _rms_bwd(dy * g, xh, r)
        dx_ref[...] = dx
        dxb_ref[...] = dx.astype(BF16)
        red_ref[0:1, :] += jnp.sum(dy * xh, axis=0, keepdims=True)
        red_ref[1:2, :] += jnp.broadcast_to(0.5 * jnp.sum(e * e) * (1.0 / D), (1, D))

    row = pl.BlockSpec((tm, D), lambda i: (i, 0))
    return _pallas_call(
        body, name="fwd_down_loss", grid=(L // tm,),
        in_specs=[pl.BlockSpec((tm, F), lambda i: (i, 0)), pl.BlockSpec((F, D), lambda i: (0, 0)),
                  row, row, pl.BlockSpec((1, D), lambda i: (0, 0))],
        out_specs=[row, row, pl.BlockSpec((8, D), lambda i: (0, 0))],
        out_shape=[_sds((L, D), F32), _sds((L, D), BF16), _sds((8, D), F32)],
        compiler_params=_params(("arbitrary",), 48),
    )(s, wd, x1, target, g_final)


def _bwd_down(dx2b, wd, s_a, s_b):
    L, D = dx2b.shape
    F = wd.shape[0]
    tn = F // 2
    tm = min(L, 512)

    def body(dx_ref, wd_ref, sa_ref, sb_ref, da_ref, db_ref):
        ds = _nt(dx_ref[...], wd_ref[...])
        da_ref[...] = (ds * sa_ref[...].astype(F32)).astype(BF16)
        db_ref[...] = (ds * sb_ref[...].astype(F32)).astype(BF16)

    ospec = pl.BlockSpec((tm, tn), lambda j, i: (i, j))
    return _pallas_call(
        body, name="bwd_down", grid=(2, L // tm),
        in_specs=[pl.BlockSpec((tm, D), lambda j, i: (i, 0)),
                  pl.BlockSpec((tn, D), lambda j, i: (j, 0)), ospec, ospec],
        out_specs=[ospec] * 2,
        out_shape=[_sds((L, F), BF16)] * 2,
        compiler_params=_params(("parallel", "parallel"), 48),
    )(dx2b, wd, s_a, s_b)


def _bwd_ffn_dh(da, db, wgt, wut, x1, dx2, g_ffn, after):
    L, D = x1.shape
    F = wgt.shape[0]
    tm = min(L, 256)

    def body(da_ref, db_ref, wg_ref, wu_ref, x1_ref, dx2_ref, g_ref, after_ref, dx_ref, dxb_ref, red_ref):
        @pl.when(pl.program_id(0) == 0)
        def _():
            red_ref[...] = jnp.zeros_like(red_ref)

        dh = _nn(da_ref[...], wg_ref[...]) + _nn(db_ref[...], wu_ref[...])
        r, xh = _rms_stats(x1_ref[...])
        red_ref[0:1, :] += jnp.sum(dh * xh, axis=0, keepdims=True)
        dx = dx2_ref[...] + _rms_bwd(dh * g_ref[...], xh, r)
        dx_ref[...] = dx
        dxb_ref[...] = dx.astype(BF16)

    row = pl.BlockSpec((tm, D), lambda i: (i, 0))
    aspec = pl.BlockSpec((tm, F), lambda i: (i, 0))
    wspec = pl.BlockSpec((F, D), lambda i: (0, 0))
    return _pallas_call(
        body, name="bwd_ffn_dh", grid=(L // tm,),
        in_specs=[aspec, aspec, wspec, wspec, row, row, pl.BlockSpec((1, D), lambda i: (0, 0)), ANY],
        out_specs=[row, row, pl.BlockSpec((8, D), lambda i: (0, 0))],
        out_shape=[_sds((L, D), F32), _sds((L, D), BF16), _sds((8, D), F32)],
        compiler_params=_params(("arbitrary",), 56),
    )(da, db, wgt, wut, x1, dx2, g_ffn, after)


def _piece_offsets(pieces):
    offsets, total = [], 0
    for p in pieces:
        offsets.append(total)
        total += p.shape[1]
    return offsets, total


def _bwd_in(pieces, w_int, x, dx1, g_mix, after):
    L, D = x.shape
    N = w_int.shape[0]
    tm = min(L, 256)
    n = len(pieces)
    offsets, total = _piece_offsets(pieces)
    assert total == N

    def body(*refs):
        piece_refs = refs[:n]
        w_ref, x_ref, dx1_ref, g_ref, after_ref, dx_ref, red_ref, dp_ref = refs[n:]

        @pl.when(pl.program_id(0) == 0)
        def _():
            red_ref[...] = jnp.zeros_like(red_ref)

        for p_ref, off in zip(piece_refs, offsets):
            dp_ref[:, off:off + p_ref.shape[1]] = p_ref[...]
        dh = _nn(dp_ref[...], w_ref[...])
        r, xh = _rms_stats(x_ref[...])
        red_ref[0:1, :] += jnp.sum(dh * xh, axis=0, keepdims=True)
        dx_ref[...] = dx1_ref[...] + _rms_bwd(dh * g_ref[...], xh, r)

    row = pl.BlockSpec((tm, D), lambda i: (i, 0))
    return _pallas_call(
        body, name="bwd_in", grid=(L // tm,),
        in_specs=[pl.BlockSpec((tm, p.shape[1]), lambda i: (i, 0)) for p in pieces]
        + [pl.BlockSpec((N, D), lambda i: (0, 0)), row, row, pl.BlockSpec((1, D), lambda i: (0, 0)), ANY],
        out_specs=[row, pl.BlockSpec((8, D), lambda i: (0, 0))],
        out_shape=[_sds((L, D), F32), _sds((8, D), F32)],
        scratch_shapes=[pltpu.VMEM((tm, N), BF16)],
        compiler_params=_params(("arbitrary",), 56),
    )(*pieces, w_int, x, dx1, g_mix, after)


def _dw_in(h, pieces, n_cols):
    L, D = h.shape
    tk = min(L, TK_TOKENS // 2)
    n = len(pieces)
    offsets, total = _piece_offsets(pieces)
    assert total == N_CHIPS * n_cols
    plan = []
    for j in range(N_CHIPS):
        lo, hi = j * n_cols, (j + 1) * n_cols
        segments = []
        for p, off in enumerate(offsets):
            a, b = max(lo, off), min(hi, off + pieces[p].shape[1])
            if a < b:
                segments.append((p, a - off, b - a, a - lo))
        plan.append(segments)

    def body(*refs):
        h_ref, piece_refs, o_ref, b_ref = refs[0], refs[1:1 + n], refs[1 + n], refs[2 + n]
        j, k = pl.program_id(0), pl.program_id(1)
        for jj in range(N_CHIPS):
            @pl.when(j == jj)
            def _(jj=jj):
                for p, start, width, at in plan[jj]:
                    b_ref[:, at:at + width] = piece_refs[p][:, start:start + width]

        part = _tn(h_ref[...], b_ref[...])

        @pl.when(k == 0)
        def _():
            o_ref[...] = part

        @pl.when(k > 0)
        def _():
            o_ref[...] += part

    def piece_spec(p):
        used = [j for j in range(N_CHIPS) if any(seg[0] == p for seg in plan[j])]

        def index(j, k):
            in_use = functools.reduce(jnp.logical_or, [j == u for u in used])
            return (jnp.where(in_use, k, 0), 0)

        return pl.BlockSpec((tk, pieces[p].shape[1]), index)

    return _pallas_call(
        body, name="dw_in", grid=(N_CHIPS, L // tk),
        in_specs=[pl.BlockSpec((tk, D), lambda j, k: (k, 0))] + [piece_spec(p) for p in range(n)],
        out_specs=pl.BlockSpec((None, D, n_cols), lambda j, k: (j, 0, 0)),
        out_shape=_sds((N_CHIPS, D, n_cols), F32),
        scratch_shapes=[pltpu.VMEM((tk, n_cols), BF16)],
        compiler_params=_params(("parallel", "arbitrary"), 56),
    )(h, *pieces)


def _mm_tn(name, a, b, a_spec, b_spec, o_block, n_out, n_k):
    def body(a_ref, b_ref, o_ref):
        part = _tn(a_ref[...], b_ref[...])

        @pl.when(pl.program_id(1) == 0)
        def _():
            o_ref[...] = part

        @pl.when(pl.program_id(1) > 0)
        def _():
            o_ref[...] += part

    return _pallas_call(
        body, name=name, grid=(n_out, n_k),
        in_specs=[a_spec, b_spec],
        out_specs=pl.BlockSpec((None,) + o_block, lambda j, k: (j, 0, 0)),
        out_shape=_sds((n_out,) + o_block, F32),
        compiler_params=_params(("parallel", "arbitrary"), 56),
    )(a, b)


TK_TOKENS = 2048


def _dw_cols(name, a, b, n_cols):
    L, M = a.shape
    tk = min(L, TK_TOKENS)
    return _mm_tn(name, a, b, pl.BlockSpec((tk, M), lambda j, k: (k, 0)),
                  pl.BlockSpec((tk, n_cols), lambda j, k: (k, j)), (M, n_cols), N_CHIPS, L // tk)


def _dw_rows(name, a, b):
    L, M = a.shape
    N = b.shape[1]
    tk = min(L, TK_TOKENS)
    return _mm_tn(name, a, b, pl.BlockSpec((tk, M // N_CHIPS), lambda j, k: (k, j)),
                  pl.BlockSpec((tk, N), lambda j, k: (k, 0)), (M // N_CHIPS, N), N_CHIPS, L // tk)


def _dw_rows2(name, a, b):
    L, M = a.shape
    N = b.shape[1]
    tk = min(L, TK_TOKENS)
    return _mm_tn(name, a, b, pl.BlockSpec((tk, M // 2), lambda j, k: (k, j)),
                  pl.BlockSpec((tk, N), lambda j, k: (k, 0)), (M // 2, N), 2, L // tk)


def _place():
    x, y, c = lax.axis_index("x"), lax.axis_index("y"), lax.axis_index("c")
    chips = [(1 - x, y), (x, 1 - y), (1 - x, 1 - y)]
    return x, y, c, 2 * x + y, chips


def _remote(src, dst, send_sem, recv_sem, device):
    return pltpu.make_async_remote_copy(src_ref=src, dst_ref=dst, send_sem=send_sem,
                                        recv_sem=recv_sem, device_id=device, device_id_type=MESH)


def _half(ref, lead, c, r2):
    return ref.at[lead, pl.ds(pl.multiple_of(c * r2, 16), r2), :]


def _cast_place(name, ws, chip_idx):
    n = len(ws)

    def body(k_ref, *refs):
        for w_ref, o_ref in zip(refs[:n], refs[n:]):
            o_ref[...] = w_ref[...].astype(BF16)

    return _pallas_call(
        body, name=name,
        grid_spec=pltpu.PrefetchScalarGridSpec(
            num_scalar_prefetch=1, grid=(2,),
            in_specs=[pl.BlockSpec((w.shape[0] // 2, w.shape[1]), lambda i, k_ref: (i, 0)) for w in ws],
            out_specs=[pl.BlockSpec((None, w.shape[0] // 2, w.shape[1]), lambda i, k_ref: (k_ref[0], i, 0))
                       for w in ws]),
        out_shape=[_sds((N_CHIPS,) + w.shape, BF16) for w in ws],
        compiler_params=_params(("parallel",), 48),
    )(chip_idx, *ws)


def _cast_place_t(name, ws, chip_idx):
    n = len(ws)
    r, cols = ws[0].shape

    def body(k_ref, *refs):
        for w_ref, o_ref in zip(refs[:n], refs[n:]):
            o_ref[...] = w_ref[...].T.astype(BF16)

    return _pallas_call(
        body, name=name,
        grid_spec=pltpu.PrefetchScalarGridSpec(
            num_scalar_prefetch=1, grid=(cols // LANES,),
            in_specs=[pl.BlockSpec((r, LANES), lambda i, k_ref: (0, i))] * n,
            out_specs=[pl.BlockSpec((None, LANES, r), lambda i, k_ref: (k_ref[0], i, 0))] * n),
        out_shape=[_sds((N_CHIPS, cols, r), BF16)] * n,
        compiler_params=_params(("parallel",), 48),
    )(chip_idx, *ws)


def _gather_copies(bufs, whole, send_sems, recv_sems, select=None):
    x, y, c, k, chips = _place()
    pairs = []
    for w, buf in enumerate(bufs):
        for j, (cx, cy) in enumerate(chips):
            if select is not None and not select(w, j):
                continue
            if w in whole:
                mine, theirs = buf.at[k], buf.at[2 * cx + cy]
            else:
                r2 = buf.shape[1] // 2
                mine, theirs = _half(buf, k, c, r2), _half(buf, 2 * cx + cy, c, r2)
            sems = (send_sems.at[w * 3 + j], recv_sems.at[w * 3 + j])
            pairs.append((_remote(mine, mine, *sems, (cx, cy, c)), _remote(theirs, theirs, *sems, (x, y, c))))
    return pairs


def _gather_start(name, groups, after):
    flat = [b for bufs, _, _ in groups for b in bufs]
    nb, ng = len(flat), len(groups)

    def body(*refs):
        ins, sems, token = refs[:nb], refs[nb + 1:nb + 1 + 2 * ng], refs[-1]
        pos = 0
        for g, (bufs, whole, select) in enumerate(groups):
            for send, _ in _gather_copies(ins[pos:pos + len(bufs)], whole, sems[2 * g], sems[2 * g + 1], select):
                send.start()
            pos += len(bufs)
        token[...] = jnp.zeros_like(token)

    sem_shapes = []
    for bufs, _, _ in groups:
        sem_shapes += [pltpu.SemaphoreType.DMA((3 * len(bufs),))] * 2
    out = _pallas_call(
        body, name=name,
        in_specs=[HBM] * nb + [ANY], out_specs=tuple([SEM] * (2 * ng) + [HBM] * nb + [VMEM]),
        out_shape=tuple(sem_shapes + [pltpu.HBM(b.shape, b.dtype) for b in flat] + [_sds((8, LANES), F32)]),
        input_output_aliases={i: 2 * ng + i for i in range(nb)},
        compiler_params=pltpu.CompilerParams(has_side_effects=EFFECT),
    )(*flat, after)
    sems, thru, pos = [], [], 2 * ng
    for g, (bufs, _, _) in enumerate(groups):
        sems.append((out[2 * g], out[2 * g + 1]))
        thru.append(list(out[pos:pos + len(bufs)]))
        pos += len(bufs)
    return sems, thru, out[-1]


def _gather_wait(name, bufs, whole, sems, after, select=None):
    nb = len(bufs)

    def body(*refs):
        ins, send_sems, recv_sems = refs[:nb], refs[nb], refs[nb + 1]
        for send, arrival in _gather_copies(ins, whole, send_sems, recv_sems, select):
            send.wait_send()
            arrival.wait_recv()

    return _pallas_call(
        body, name=name,
        in_specs=[HBM] * nb + [SEM, SEM, ANY], out_specs=[HBM] * nb,
        out_shape=[pltpu.HBM(b.shape, b.dtype) for b in bufs],
        input_output_aliases={i: i for i in range(nb)},
        compiler_params=pltpu.CompilerParams(has_side_effects=EFFECT),
    )(*bufs, sems[0], sems[1], after)


def _gather_forward(name, bufs, sources=(0, 1, 2)):
    n = len(bufs)

    def body(*refs):
        outs = refs[n:2 * n]
        send_sems, recv_sems = refs[2 * n:]
        x, y, c, _, chips = _place()
        sends = []
        for w in range(n):
            r2 = outs[w].shape[1] // 2
            for j in sources:
                landed = _half(outs[w], 2 * chips[j][0] + chips[j][1], c, r2)
                sends.append(_remote(landed, landed, send_sems.at[w * 3 + j], recv_sems.at[w * 3 + j],
                                     (x, y, 1 - c)))
        for cp in sends:
            cp.start()
        for w in range(n):
            r2 = outs[w].shape[1] // 2
            for j in sources:
                got = _half(outs[w], 2 * chips[j][0] + chips[j][1], 1 - c, r2)
                _remote(got, got, send_sems.at[w * 3 + j], recv_sems.at[w * 3 + j], (x, y, c)).wait_recv()
        for cp in sends:
            cp.wait_send()

    return _pallas_call(
        body, name=name,
        in_specs=[ANY] * n, out_specs=[ANY] * n,
        out_shape=[_sds(b.shape, b.dtype) for b in bufs],
        input_output_aliases={i: i for i in range(n)},
        scratch_shapes=[pltpu.SemaphoreType.DMA((n * 3,)), pltpu.SemaphoreType.DMA((n * 3,))],
    )(*bufs)


def _rs_sibling(name, grads):
    n = len(grads)

    def body(*refs):
        ins, outs = refs[:n], refs[n:2 * n]
        send_sems, recv_sems = refs[2 * n:]
        x, y, c, _, _ = _place()
        copies = []
        for w in range(n):
            r2 = ins[w].shape[1] // 2
            copies.append(_remote(_half(ins[w], slice(None), 1 - c, r2), outs[w],
                                  send_sems.at[w], recv_sems.at[w], (x, y, 1 - c)))
        for cp in copies:
            cp.start()
        for cp in copies:
            cp.wait()

    return _pallas_call(
        body, name=name,
        in_specs=[ANY] * n, out_specs=[ANY] * n,
        out_shape=[_sds((N_CHIPS, g.shape[1] // 2, g.shape[2]), F32) for g in grads],
        scratch_shapes=[pltpu.SemaphoreType.DMA((n,)), pltpu.SemaphoreType.DMA((n,))],
    )(*grads)


def _rs_add(name, grads3, from_sibling, c_idx):
    n = len(grads3)

    def body(c_ref, *refs):
        for g_ref, s_ref, o_ref in zip(refs[:n], refs[n:2 * n], refs[2 * n:]):
            o_ref[...] = (g_ref[...] + s_ref[...]).astype(BF16)

    mine = [pl.BlockSpec((None,) + s.shape[1:], lambda k, c_ref: (k, c_ref[0], 0)) for s in from_sibling]
    whole = [pl.BlockSpec((None,) + s.shape[1:], lambda k, c_ref: (k, 0, 0)) for s in from_sibling]
    return _pallas_call(
        body, name=name,
        grid_spec=pltpu.PrefetchScalarGridSpec(num_scalar_prefetch=1, grid=(N_CHIPS,), in_specs=mine + whole,
                                               out_specs=whole),
        out_shape=[_sds(s.shape, BF16) for s in from_sibling],
        compiler_params=_params(("parallel",), 48),
    )(c_idx, *grads3, *from_sibling)


def _split_start(name, arrays, n_sems, pairs_fn):
    n = len(arrays)

    def body(*refs):
        for send, _ in pairs_fn(refs[:n], refs[n], refs[n + 1]):
            send.start()
        refs[-1][...] = jnp.zeros_like(refs[-1])

    out = _pallas_call(
        body, name=name,
        in_specs=[HBM] * n, out_specs=tuple([SEM, SEM] + [HBM] * n + [VMEM]),
        out_shape=tuple([pltpu.SemaphoreType.DMA((n_sems,))] * 2 + [pltpu.HBM(a.shape, a.dtype) for a in arrays]
                        + [_sds((8, LANES), F32)]),
        input_output_aliases={i: 2 + i for i in range(n)},
        compiler_params=pltpu.CompilerParams(has_side_effects=EFFECT),
    )(*arrays)
    return (out[0], out[1]), list(out[2:2 + n]), out[-1]


def _split_wait(name, sems, arrays, pairs_fn, after):
    n = len(arrays)

    def body(*refs):
        for send, arrival in pairs_fn(refs[:n], refs[n], refs[n + 1]):
            send.wait_send()
            arrival.wait_recv()

    return list(_pallas_call(
        body, name=name,
        in_specs=[HBM] * n + [SEM, SEM, ANY], out_specs=[HBM] * n,
        out_shape=[pltpu.HBM(a.shape, a.dtype) for a in arrays],
        input_output_aliases={i: i for i in range(n)},
        compiler_params=pltpu.CompilerParams(has_side_effects=EFFECT),
    )(*arrays, sems[0], sems[1], after))


def _forward_pairs(bufs, send_sems, recv_sems):
    x, y, c, _, chips = _place()
    pairs = []
    for w, buf in enumerate(bufs):
        r2 = buf.shape[1] // 2
        for j, (cx, cy) in enumerate(chips):
            landed, theirs = _half(buf, 2 * cx + cy, c, r2), _half(buf, 2 * cx + cy, 1 - c, r2)
            sems = (send_sems.at[w * 3 + j], recv_sems.at[w * 3 + j])
            pairs.append((_remote(landed, landed, *sems, (x, y, 1 - c)), _remote(theirs, theirs, *sems, (x, y, c))))
    return pairs


def _sibling_pairs(arrays, send_sems, recv_sems):
    x, y, c, _, _ = _place()
    n = len(arrays) // 2
    pairs = []
    for w in range(n):
        r2 = arrays[w].shape[1] // 2
        cp = _remote(_half(arrays[w], slice(None), 1 - c, r2), arrays[n + w], send_sems.at[w], recv_sems.at[w],
                     (x, y, 1 - c))
        pairs.append((cp, cp))
    return pairs


def _ici_pairs(arrays, send_sems, recv_sems):
    x, y, c, _, chips = _place()
    n = len(arrays) // 2
    pairs = []
    for w in range(n):
        for j, (cx, cy) in enumerate(chips):
            cp = _remote(arrays[w].at[2 * cx + cy], arrays[n + w].at[j],
                         send_sems.at[w * 3 + j], recv_sems.at[w * 3 + j], (cx, cy, c))
            pairs.append((cp, cp))
    return pairs


def _rs_sum(name, partials, received, place_idx):
    n = len(partials)
    nb = 2
    blocks = [(p.shape[1] // nb, p.shape[2]) for p in partials]

    def body(idx_ref, *refs):
        for p_ref, r_ref, o_ref in zip(refs[:n], refs[n:2 * n], refs[2 * n:]):
            o_ref[...] = ((p_ref[...].astype(F32) + r_ref[0].astype(F32))
                          + (r_ref[1].astype(F32) + r_ref[2].astype(F32)))

    return _pallas_call(
        body, name=name,
        grid_spec=pltpu.PrefetchScalarGridSpec(
            num_scalar_prefetch=1, grid=(nb,),
            in_specs=[pl.BlockSpec((None,) + b, lambda i, idx: (idx[0], i, 0)) for b in blocks]
            + [pl.BlockSpec((3,) + b, lambda i, idx: (0, i, 0)) for b in blocks],
            out_specs=[pl.BlockSpec(b, lambda i, idx: (idx[1] * nb + i, 0)) for b in blocks]),
        out_shape=[_sds((2 * p.shape[1], p.shape[2]), F32) for p in partials],
        compiler_params=_params(("parallel",), 48),
    )(place_idx, *partials, *received)


def _share_pairs(arrays, send_sems, recv_sems):
    x, y, c, _, _ = _place()
    pairs = []
    for w, arr in enumerate(arrays):
        r2 = arr.shape[0] // 2
        mine = arr.at[pl.ds(pl.multiple_of(c * r2, 8), r2), :]
        theirs = arr.at[pl.ds(pl.multiple_of((1 - c) * r2, 8), r2), :]
        sems = (send_sems.at[w], recv_sems.at[w])
        pairs.append((_remote(mine, mine, *sems, (x, y, 1 - c)), _remote(theirs, theirs, *sems, (x, y, c))))
    return pairs


def _small_allreduce(red_mix, red_ffn, red_final, red_hg, g_conv, after):
    rows = N_SMALL_ROWS
    D = red_mix.shape[1]
    H = red_hg.shape[1]

    def body(mix_ref, ffn_ref, fin_ref, hg_ref, cv_ref, after_ref, sum_ref, all_ref, in_ref, send_sems,
             recv_sems):
        in_ref[...] = jnp.zeros_like(in_ref)
        in_ref[0:1, :] = mix_ref[0:1, :]
        in_ref[1:2, :] = ffn_ref[0:1, :]
        in_ref[2:3, :] = fin_ref[0:1, :]
        gam = hg_ref[1:2, 0:HEAD_DIM]
        for h in range(1, H // HEAD_DIM):
            gam = gam + hg_ref[1:2, h * HEAD_DIM:(h + 1) * HEAD_DIM]
        in_ref[3:4, 0:HEAD_DIM] = gam
        in_ref[3:4, HEAD_DIM:2 * HEAD_DIM] = fin_ref[1:2, 0:HEAD_DIM]
        in_ref[4:5, 0:H] = hg_ref[0:1, :]
        in_ref[6:9, 0:H] = cv_ref[...]
        x, y, c, _, _ = _place()
        me = 4 * x + 2 * y + c
        all_ref[me] = in_ref[...]
        copies = []
        for m in range(1, 8):
            mx, my, mc = (m >> 2) & 1, (m >> 1) & 1, m & 1
            px, py, pc = x ^ mx, y ^ my, c ^ mc
            copies.append((_remote(in_ref, all_ref.at[me], send_sems.at[m - 1], recv_sems.at[m - 1],
                                   (px, py, pc)), 4 * px + 2 * py + pc, m))
        for cp, _, _ in copies:
            cp.start()
        for _, peer, m in copies:
            _remote(in_ref, all_ref.at[peer], send_sems.at[m - 1], recv_sems.at[m - 1],
                    (x, y, c)).wait_recv()
        for cp, _, _ in copies:
            cp.wait_send()
        total = all_ref[0]
        for d in range(1, 8):
            total = total + all_ref[d]
        sum_ref[...] = total

    return _pallas_call(
        body, name="small_allreduce", pin=False,
        in_specs=[VMEM] * 5 + [ANY], out_specs=[VMEM, VMEM],
        out_shape=[_sds((rows, D), F32), _sds((8, rows, D), F32)],
        scratch_shapes=[pltpu.VMEM((rows, D), F32), pltpu.SemaphoreType.DMA((7,)),
                        pltpu.SemaphoreType.DMA((7,))],
    )(red_mix, red_ffn, red_final, red_hg, g_conv, after)[0]


def _adamw_math(w, g, m, v):
    m = ADAM_B1 * m + (1.0 - ADAM_B1) * g
    v = ADAM_B2 * v + (1.0 - ADAM_B2) * jnp.square(g)
    m_hat = m / (1.0 - ADAM_B1 ** ADAM_STEP)
    v_hat = v / (1.0 - ADAM_B2 ** ADAM_STEP)
    delta = -ADAM_LR * (m_hat / (jnp.sqrt(v_hat) + ADAM_EPS) + ADAM_WD * w)
    return delta, m, v


def _adamw(name, gs, ws, ms, vs):
    n = len(gs)
    nb = 4

    def body(*refs):
        ins, outs = refs[:4 * n], refs[4 * n:]
        for j in range(n):
            g_ref, w_ref, m_ref, v_ref = ins[j], ins[n + j], ins[2 * n + j], ins[3 * n + j]
            go_ref, d_ref, mo_ref, vo_ref = outs[4 * j:4 * j + 4]
            g = g_ref[...]
            go_ref[...] = g
            d_ref[...], mo_ref[...], vo_ref[...] = _adamw_math(w_ref[...], g, m_ref[...], v_ref[...])

    blk = [pl.BlockSpec((g.shape[0] // nb, g.shape[1]), lambda i: (i, 0)) for g in gs]
    out = _pallas_call(
        body, name=name, grid=(nb,),
        in_specs=blk * 4, out_specs=[b for b in blk for _ in range(4)],
        out_shape=[_sds(g.shape, F32) for g in gs for _ in range(4)],
        compiler_params=_params(("parallel",), 56),
    )(*gs, *ws, *ms, *vs)
    return [list(out[4 * j:4 * j + 4]) for j in range(n)]


def _small_update(total, chip_idx, ws, ms, vs):
    n = len(ws)
    H = ws[1].shape[1]

    def body(idx_ref, tot_ref, *refs):
        w, m, v, outs = refs[:n], refs[n:2 * n], refs[2 * n:3 * n], refs[3 * n:]
        chip = idx_ref[0]
        p0 = _lower_bound(w[1][...])
        dl0 = p0 * (1.0 - p0) * tot_ref[4:5, 0:H]
        conv = jnp.zeros((3, LANES), F32)
        for k in range(N_CHIPS):
            conv = jnp.where(chip == k, tot_ref[6:9, k * LANES:(k + 1) * LANES], conv)
        grads = [tot_ref[0:1, :], None, tot_ref[3:4, 0:HEAD_DIM], conv, tot_ref[1:2, :], tot_ref[2:3, :]]
        for p in range(n):
            g_ref, d_ref, mo_ref, vo_ref = outs[4 * p:4 * p + 4]
            if p == 1:
                for row, g in ((slice(0, 1), dl0), (slice(1, 2), -dl0)):
                    g_ref[row, :] = g
                    d_ref[row, :], mo_ref[row, :], vo_ref[row, :] = _adamw_math(
                        w[p][row, :], g, m[p][row, :], v[p][row, :])
            else:
                g_ref[...] = grads[p]
                d_ref[...], mo_ref[...], vo_ref[...] = _adamw_math(w[p][...], grads[p], m[p][...], v[p][...])
        outs[4 * n][...] = tot_ref[3:4, HEAD_DIM:2 * HEAD_DIM]

    full = lambda a: pl.BlockSpec(a.shape, lambda i, idx: (0,) * a.ndim)
    out_shape = [_sds(w.shape, F32) for w in ws for _ in range(4)] + [_sds((1, LANES), F32)]
    return _pallas_call(
        body, name="small_update",
        grid_spec=pltpu.PrefetchScalarGridSpec(
            num_scalar_prefetch=1, grid=(1,),
            in_specs=[full(total)] + [full(a) for a in ws + ms + vs],
            out_specs=[full(s) for s in out_shape]),
        out_shape=out_shape,
    )(chip_idx, total, *ws, *ms, *vs)


def kernel(x, norm_mix_g, w_in, lower_bounds, hg_norm_g, conv_w, w_branch_a, w_branch_b, w_out, norm_ffn_g, w_ffn_gate, w_ffn_up, w_ffn_down, norm_final_g, loss_target, m_norm_mix_g, m_w_in, m_lower_bounds, m_hg_norm_g, m_conv_w, m_w_branch_a, m_w_branch_b, m_w_out, m_norm_ffn_g, m_w_ffn_gate, m_w_ffn_up, m_w_ffn_down, m_norm_final_g, v_norm_mix_g, v_w_in, v_lower_bounds, v_hg_norm_g, v_conv_w, v_w_branch_a, v_w_branch_b, v_w_out, v_norm_ffn_g, v_w_ffn_gate, v_w_ffn_up, v_w_ffn_down, v_norm_final_g):
    _, L, D = x.shape
    H = D // 2
    assert lower_bounds.shape == (2, H) and hg_norm_g.shape == (1, HEAD_DIM)
    assert conv_w.shape == (1, 3, LANES) and w_in.shape[2] * N_CHIPS == 11 * H
    x2d, target = x.reshape(L, D), loss_target.reshape(L, D)
    g_final = norm_final_g.reshape(1, D)
    chip = 2 * lax.axis_index("x") + lax.axis_index("y")
    core = lax.axis_index("c")

    tr = lambda w: jnp.transpose(w[0])
    big = [w_in[0], w_branch_a[0], w_branch_b[0], w_out[0], tr(w_ffn_gate), tr(w_ffn_up), w_ffn_down[0]]
    big_m = [m_w_in[0], m_w_branch_a[0], m_w_branch_b[0], m_w_out[0], tr(m_w_ffn_gate), tr(m_w_ffn_up),
             m_w_ffn_down[0]]
    big_v = [v_w_in[0], v_w_branch_a[0], v_w_branch_b[0], v_w_out[0], tr(v_w_ffn_gate), tr(v_w_ffn_up),
             v_w_ffn_down[0]]
    names = ["w_in", "w_branch_a", "w_branch_b", "w_out", "w_ffn_gate", "w_ffn_up", "w_ffn_down"]

    chip_idx = chip.reshape(1).astype(jnp.int32)
    def per_shape(fn, tag, js, *lists):
        groups = {}
        for pos, a in enumerate(lists[0]):
            groups.setdefault(a.shape, []).append(pos)
        results = [None] * len(js)
        for same in groups.values():
            out = fn(tag + names[js[same[0]]], *[[xs[p] for p in same] for xs in lists])
            for q, p in enumerate(same):
                results[p] = out[q]
        return results

    place_t = lambda name, ws: _cast_place_t(name, ws, chip_idx)
    placed = per_shape(place_t, "place_", [0, 1, 2], big[:3]) + list(_cast_place("place_rest", big[3:], chip_idx))
    conv_placed = lax.dynamic_update_slice(jnp.zeros((N_CHIPS, 3, LANES), F32), conv_w, (chip, 0, 0))
    x_i, y_i = lax.axis_index("x"), lax.axis_index("y")
    blocks = lambda *ks: jnp.stack(ks).astype(jnp.int32)
    near = lambda w, j: j < 2
    far = lambda w, j: w == 1 or j == 2
    near_sems, in_flight, _ = _gather_start("gather_start_near", [([placed[0]], set(), near)], chip_idx)
    w_in_buf = in_flight[0][0]
    h, proj = _fwd_proj_first(x2d, norm_mix_g, w_in_buf, blocks(chip))
    sems, in_flight, _ = _gather_start(
        "gather_start_rest", [([w_in_buf, conv_placed], {1}, far), (placed[1:4], set(), None),
                              (placed[4:], set(), None)], h)
    w_in_buf, conv_buf = in_flight[0]
    (w_in_buf,) = _gather_wait("gather_wait_in_near", [w_in_buf], set(), near_sems[0], h, near)
    (w_in_buf,) = _gather_forward("gather_fwd_in_near", [w_in_buf], (0, 1))
    proj = _fwd_proj_more("fwd_proj_near", h, w_in_buf, proj,
                          blocks(2 * (1 - x_i) + y_i, 2 * x_i + (1 - y_i)))
    w_in_buf, conv_all = _gather_wait("gather_wait_in_far", [w_in_buf, conv_buf], {1}, sems[0], proj, far)
    (w_int3,) = _gather_forward("gather_fwd_in_far", [w_in_buf], (2,))
    proj = _fwd_proj_more("fwd_proj_far", h, w_int3, proj, blocks(2 * (1 - x_i) + (1 - y_i)))
    w_int = w_int3.reshape(-1, D)
    conv_full = jnp.transpose(conv_all, (1, 0, 2)).reshape(3, H)
    og, o_pre, s_saved = _hgrn_fwd(proj, lower_bounds, hg_norm_g, H)
    landed = _gather_wait("gather_wait_mix", in_flight[1], set(), sems[1], og)
    fwd_sems, landed, token = _split_start("gather_fwd_mix_start", landed, 9, _forward_pairs)
    cb = _conv_fwd(proj, conv_full, H, token)
    wat3, wbt3, wout3 = _split_wait("gather_fwd_mix_wait", fwd_sems, landed, _forward_pairs, cb)
    wat, wbt, wout = wat3.reshape(D, H), wbt3.reshape(D, H), wout3.reshape(D, D)
    landed = _gather_wait("gather_wait_ffn", in_flight[2], set(), sems[2], cb)
    fwd_sems, landed, token = _split_start("gather_fwd_ffn_start", landed, 9, _forward_pairs)
    sig_a, sig_b, dm_dga, dm_dgb, merged, x1, h2 = _fwd_mix(og, cb, proj, x2d, wat, wbt, wout, norm_ffn_g,
                                                              H, token)
    wgt3, wut3, wd3 = _split_wait("gather_fwd_ffn_wait", fwd_sems, landed, _forward_pairs, h2)
    d_ff = N_CHIPS * wd3.shape[1]
    wgt, wut, wd = wgt3.reshape(d_ff, D), wut3.reshape(d_ff, D), wd3.reshape(d_ff, D)
    ffn_ds_da, ffn_ds_db, ffn_s = _fwd_ffn_up(h2, wgt, wut)
    dx2, dx2b, red_final = _fwd_down_loss(ffn_s, wd, x1, target, g_final)

    c_idx = core.reshape(1).astype(jnp.int32)
    place_idx = jnp.stack([chip, core]).astype(jnp.int32)

    def sibling_start(tag, grads):
        bufs = [lax.empty((N_CHIPS, g.shape[1] // 2, g.shape[2]), F32) for g in grads]
        return _split_start("rs_sibling_start_" + tag, list(grads) + bufs, len(grads), _sibling_pairs)

    def ici_start(tag, js, grads, from_sibling):
        partials = list(_rs_add("rs_add_" + tag, grads, from_sibling, c_idx))
        landings = [lax.empty((3,) + p.shape[1:], BF16) for p in partials]
        return _split_start("rs_ici_start_" + tag, partials + landings, 3 * len(js), _ici_pairs)

    def ici_start_behind(tag, js, started, after):
        n = len(js)
        arrays = _split_wait("rs_sibling_wait_" + tag, started[0], started[1], _sibling_pairs, after)
        return ici_start(tag, js, arrays[:n], arrays[n:])

    def sums(tag, started, after):
        partials, received = [], []
        for group, group_js, start in started:
            arrays = _split_wait("rs_ici_wait_" + group, start[0], start[1], _ici_pairs, after)
            partials += arrays[:len(group_js)]
            received += arrays[len(group_js):]
        return list(_rs_sum("rs_sum_" + tag, partials, received, place_idx))

    def adamw(tag, js, grads):
        return _adamw("adamw_" + tag, grads, *[[src[j] for j in js] for src in (big, big_m, big_v)])

    shards3 = lambda g: g.reshape(N_CHIPS, d_ff // N_CHIPS, D)
    da, db = _bwd_down(dx2b, wd, ffn_ds_da, ffn_ds_db)
    g_wd = shards3(_dw_rows2("dw_ffn_down", ffn_s, dx2b))
    g_wg = shards3(_dw_rows2("dw_ffn_gate", da, h2))
    g_wu = shards3(_dw_rows2("dw_ffn_up", db, h2))
    ffn_sibling = sibling_start("ffn", [g_wg, g_wu, g_wd])
    dx1, dx1b, red_ffn = _bwd_ffn_dh(da, db, wgt, wut, x1, dx2, norm_ffn_g, ffn_sibling[2])
    ffn_ici = ici_start_behind("ffn", [4, 5, 6], ffn_sibling, dx1b)
    dya, dyb, d_gates, d_o, d_cb = _bwd_mix(dx1b, sig_a, sig_b, dm_dga, dm_dgb, wat, wbt, wout, H, ffn_ici[2])
    g_wout = _dw_rows("dw_out", merged, dx1b)
    g_wa = _dw_cols("dw_branch_a", og, dya, D // N_CHIPS)
    g_wb = _dw_cols("dw_branch_b", cb, dyb, D // N_CHIPS)
    mix_sibling = sibling_start("mix", [g_wa, g_wb, g_wout])
    d_hgrn, red_hg = _hgrn_bwd(proj, lower_bounds, hg_norm_g, o_pre, d_o, s_saved, H, mix_sibling[2])
    mix_ici = ici_start_behind("mix", [1, 2, 3], mix_sibling, d_hgrn)
    dcg, dbg, dxb, g_conv = _conv_bwd(proj, conv_full, d_cb, H, mix_ici[2])
    dproj = [d_hgrn, dcg, dbg, dxb, d_gates]
    g_win = _dw_in(h, dproj, w_int3.shape[1])
    in_sibling = sibling_start("in", [g_win])
    halves = sums("rest", [("mix", [1, 2, 3], mix_ici), ("ffn", [4, 5, 6], ffn_ici)], in_sibling[2])
    rest_share = _split_start("rs_share_start_rest", halves, len(halves), _share_pairs)
    in_ici = ici_start_behind("in", [0], in_sibling, rest_share[2])
    grad_x, red_mix = _bwd_in(dproj, w_int, x2d, dx1, norm_mix_g, in_ici[2])
    in_share = _split_start("rs_share_start_in", sums("in", [("in", [0], in_ici)], grad_x), 1, _share_pairs)
    total = _small_allreduce(red_mix, red_ffn, red_final, red_hg, g_conv, in_share[2])
    rest_grads = _split_wait("rs_share_wait_rest", rest_share[0], rest_share[1], _share_pairs, total)
    big_out = [None] + adamw("rest", [1, 2, 3, 4, 5, 6], rest_grads)
    in_grad = _split_wait("rs_share_wait_in", in_share[0], in_share[1], _share_pairs, big_out[6][0])
    big_out[0] = adamw("in", [0], in_grad)[0]

    def smalls(mix, lb, hg, cw, ffn, fin):
        return [mix, lb, hg, cw[0], ffn, fin.reshape(1, D)]

    small_out = _small_update(
        total, chip_idx,
        smalls(norm_mix_g, lower_bounds, hg_norm_g, conv_w, norm_ffn_g, norm_final_g),
        smalls(m_norm_mix_g, m_lower_bounds, m_hg_norm_g, m_conv_w, m_norm_ffn_g, m_norm_final_g),
        smalls(v_norm_mix_g, v_lower_bounds, v_hg_norm_g, v_conv_w, v_norm_ffn_g, v_norm_final_g))

    def outputs(i):
        big_i = [big_out[j][i] for j in range(7)]
        mix, lb, hg, cw, ffn, fin = [small_out[4 * p + i] for p in range(6)]
        return [mix, big_i[0][None], lb, hg, cw[None], big_i[1][None], big_i[2][None], big_i[3][None], ffn,
                big_i[4].T[None], big_i[5].T[None], big_i[6][None], fin.reshape(D)]

    outs = [small_out[24][0, 0], grad_x.reshape(1, L, D)]
    for i in range(4):
        outs += outputs(i)
    return tuple(outs)
```

```python
import functools

import jax
import jax.numpy as jnp
from jax import lax
from jax.experimental import pallas as pl
from jax.experimental.pallas import tpu as pltpu

F32 = jnp.float32
BF16 = jnp.bfloat16
EPS = 1e-6
CHUNK = 32
HEAD_DIM = 128
LANES = 128
N_CHIPS = 4
N_SMALL_ROWS = 16

ADAM_LR = 0.001
ADAM_B1 = 0.9
ADAM_B2 = 0.999
ADAM_EPS = 1e-08
ADAM_WD = 0.01
ADAM_STEP = 10

MESH = pl.DeviceIdType.MESH
ANY = pl.BlockSpec(memory_space=pl.ANY)
VMEM = pl.BlockSpec(memory_space=pltpu.VMEM)
HBM = pl.BlockSpec(memory_space=pltpu.HBM)
SEM = pl.BlockSpec(memory_space=pltpu.SEMAPHORE)
EFFECT = pltpu.SideEffectType.DATAFLOW_SIDE_EFFECTING


def _sds(shape, dtype):
    return jax.ShapeDtypeStruct(shape, dtype)


def _pallas_call(body, pin=True, **kwargs):
    if not pin:
        return pl.pallas_call(body, **kwargs)
    in_hbm = lambda s: pltpu.HBM(s.shape, s.dtype) if isinstance(s, jax.ShapeDtypeStruct) else s
    kwargs["out_shape"] = jax.tree.map(in_hbm, kwargs["out_shape"])
    call = pl.pallas_call(body, **kwargs)

    def run(*args):
        return call(*[pltpu.with_memory_space_constraint(a, pltpu.HBM) if a.dtype in (F32, BF16) else a
                      for a in args])

    return run


def _params(semantics, vmem_mb):
    return pltpu.CompilerParams(dimension_semantics=semantics, vmem_limit_bytes=vmem_mb << 20)


def _nn(a, b):
    return lax.dot_general(a, b, (((1,), (0,)), ((), ())), preferred_element_type=F32)


def _nt(a, b):
    return lax.dot_general(a, b, (((1,), (1,)), ((), ())), preferred_element_type=F32)


def _tn(a, b):
    return lax.dot_general(a, b, (((0,), (0,)), ((), ())), preferred_element_type=F32)


def _sigmoid(x):
    return jax.nn.sigmoid(x)


def _rms_stats(x):
    r = lax.rsqrt(jnp.mean(x * x, axis=-1, keepdims=True) + EPS)
    return r, x * r


def _rms_bwd(dxh, xh, r):
    return r * (dxh - xh * jnp.mean(dxh * xh, axis=-1, keepdims=True))


def _fwd_proj_first(x, g_mix, w_int3, block):
    L, D = x.shape
    tn = w_int3.shape[1]
    tm = min(L, 1024)

    def body(blk_ref, x_ref, g_ref, w_ref, h_ref, p_ref):
        _, xh = _rms_stats(x_ref[...])
        h = (xh * g_ref[...]).astype(BF16)
        h_ref[...] = h
        p_ref[...] = _nt(h, w_ref[...])

    return _pallas_call(
        body, name="fwd_proj_own",
        grid_spec=pltpu.PrefetchScalarGridSpec(
            num_scalar_prefetch=1, grid=(L // tm,),
            in_specs=[pl.BlockSpec((tm, D), lambda i, blk: (i, 0)),
                      pl.BlockSpec((1, D), lambda i, blk: (0, 0)),
                      pl.BlockSpec((None, tn, D), lambda i, blk: (blk[0], 0, 0))],
            out_specs=[pl.BlockSpec((tm, D), lambda i, blk: (i, 0)),
                       pl.BlockSpec((tm, tn), lambda i, blk: (i, blk[0]))]),
        out_shape=[_sds((L, D), BF16), _sds((L, N_CHIPS * tn), F32)],
        compiler_params=_params(("parallel",), 48),
    )(block, x, g_mix, w_int3)


def _fwd_proj_more(name, h, w_int3, proj, blocks):
    L, D = h.shape
    tn = w_int3.shape[1]
    tm = min(L, 1024)

    def body(blk_ref, h_ref, w_ref, proj_ref, p_ref):
        p_ref[...] = _nt(h_ref[...], w_ref[...])

    return _pallas_call(
        body, name=name,
        grid_spec=pltpu.PrefetchScalarGridSpec(
            num_scalar_prefetch=1, grid=(L // tm, blocks.shape[0]),
            in_specs=[pl.BlockSpec((tm, D), lambda i, j, blk: (i, 0)),
                      pl.BlockSpec((None, tn, D), lambda i, j, blk: (blk[j], 0, 0)), ANY],
            out_specs=pl.BlockSpec((tm, tn), lambda i, j, blk: (i, blk[j]))),
        out_shape=_sds(proj.shape, proj.dtype),
        input_output_aliases={3: 0},
        compiler_params=_params(("parallel", "arbitrary"), 48),
    )(blocks, h, w_int3, proj)


def _lower_bound(lbp):
    l0, l1 = lbp[0:1, :], lbp[1:2, :]
    m = jnp.maximum(l0, l1)
    e0, e1 = jnp.exp(l0 - m), jnp.exp(l1 - m)
    return e0 / (e0 + e1)


def _seg_scan(x, r32, forward):
    n = x.shape[0]
    s = 1
    while s < CHUNK:
        if forward:
            x = x + jnp.where(r32 >= s, pltpu.roll(x, s, 0), 0.0)
        else:
            x = x + jnp.where(r32 < CHUNK - s, pltpu.roll(x, n - s, 0), 0.0)
        s *= 2
    return x


def _bcast_row(x, row):
    n, w = x.shape
    nc = n // CHUNK
    x3 = x.reshape(nc, CHUNK, w)
    return jnp.broadcast_to(x3[:, row:row + 1, :], (nc, CHUNK, w)).reshape(n, w)


def _hgrn_prep(q_raw, f_raw, lb):
    r32 = lax.broadcasted_iota(jnp.int32, f_raw.shape, 0) & (CHUNK - 1)
    sig = _sigmoid(f_raw)
    f = lb + (1.0 - lb) * sig
    b = _seg_scan(jnp.log(f), r32, True)
    a = _bcast_row(b, CHUNK // 2 - 1)
    bl = _bcast_row(b, CHUNK - 1)
    sq = _sigmoid(q_raw)
    q = q_raw * sq * (HEAD_DIM ** -0.5)
    return dict(r32=r32, sig=sig, f=f, k=1.0 - f, b=b, a=a, bl=bl, sq=sq, q=q)


def _chunk_masks(n):
    ri = lax.broadcasted_iota(jnp.int32, (n, n), 0)
    ci = lax.broadcasted_iota(jnp.int32, (n, n), 1)
    same = (ri // CHUNK) == (ci // CHUNK)
    return same & (ci <= ri), same & (ri <= ci)


def _hgrn_fwd(proj, lower_bounds, gamma, H):
    L = proj.shape[0]
    nh = H // HEAD_DIM
    TL = min(L, 256)
    nc = TL // CHUNK

    def body(q_ref, f_ref, v_ref, g_ref, lbp_ref, gam_ref, og_ref, o_ref, s_ref, st_ref):
        @pl.when(pl.program_id(0) == 0)
        def _():
            st_ref[...] = jnp.zeros_like(st_ref)

        lb = _lower_bound(lbp_ref[...])
        gam = gam_ref[...]
        mask, _ = _chunk_masks(TL)
        rowc = lax.broadcasted_iota(jnp.int32, (TL, HEAD_DIM), 0) // CHUNK
        for h in range(nh):
            hs = slice(h * HEAD_DIM, (h + 1) * HEAD_DIM)
            p = _hgrn_prep(q_ref[:, hs], f_ref[:, hs], lb[:, hs])
            v = v_ref[:, hs]
            vb = v.astype(BF16)
            vt = v.T.astype(BF16)
            q_hat = (p["q"] * jnp.exp(p["b"] - p["a"])).astype(BF16)
            k_hat = (p["k"] * jnp.exp(p["a"] - p["b"])).astype(BF16)
            q_in = (p["q"] * jnp.exp(p["b"])).astype(BF16)
            k_out = (p["k"] * jnp.exp(p["bl"] - p["b"])).astype(BF16)
            dec = jnp.exp(p["bl"])
            att = jnp.where(mask, _nt(q_hat, k_hat), 0.0).astype(BF16)
            o_intra = _nn(att, vb)
            st = st_ref[h]
            for c in range(nc):
                rs = slice(c * CHUNK, (c + 1) * CHUNK)
                stb = st.astype(BF16)
                s_ref[c, h] = stb
                o_ref[rs, hs] = o_intra[rs] + _nt(q_in[rs], stb)
                k_c = jnp.where(rowc == c, k_out, jnp.zeros_like(k_out))
                st = st * dec[c * CHUNK:c * CHUNK + 1, :] + _nn(vt, k_c)
            st_ref[h] = st
            o = o_ref[:, hs]
            _, xh = _rms_stats(o)
            gr = g_ref[:, hs]
            og_ref[:, hs] = (xh * gam * (gr * _sigmoid(gr))).astype(BF16)

    col = lambda k: pl.BlockSpec((TL, H), lambda i, k=k: (i, k))
    return _pallas_call(
        body, name="hgrn_fwd", grid=(L // TL,),
        in_specs=[col(0), col(1), col(2), col(3),
                  pl.BlockSpec(lower_bounds.shape, lambda i: (0, 0)),
                  pl.BlockSpec(gamma.shape, lambda i: (0, 0))],
        out_specs=[pl.BlockSpec((TL, H), lambda i: (i, 0)),
                   pl.BlockSpec((TL, H), lambda i: (i, 0)),
                   pl.BlockSpec((nc, nh, HEAD_DIM, HEAD_DIM), lambda i: (i, 0, 0, 0))],
        out_shape=[_sds((L, H), BF16), _sds((L, H), F32),
                   _sds((L // CHUNK, nh, HEAD_DIM, HEAD_DIM), BF16)],
        scratch_shapes=[pltpu.VMEM((nh, HEAD_DIM, HEAD_DIM), F32)],
        compiler_params=_params(("arbitrary",), 48),
    )(proj, proj, proj, proj, lower_bounds, gamma)


def _hgrn_bwd(proj, lower_bounds, gamma, o_pre, d_out, s_saved, H, after):
    L = proj.shape[0]
    nh = H // HEAD_DIM
    TL = min(L, 256)
    nc = TL // CHUNK
    nt = L // TL

    def body(q_ref, f_ref, v_ref, g_ref, lbp_ref, gam_ref, o_ref, d_ref, s_ref, after_ref,
             dp_ref, red_ref, dst_ref, dsall_ref, tmp_ref):
        @pl.when(pl.program_id(0) == 0)
        def _():
            dst_ref[...] = jnp.zeros_like(dst_ref)
            red_ref[...] = jnp.zeros_like(red_ref)

        lb = _lower_bound(lbp_ref[...])
        gam = gam_ref[...]
        mask, mask_t = _chunk_masks(TL)
        rowc = lax.broadcasted_iota(jnp.int32, (TL, HEAD_DIM), 0) // CHUNK
        for h in range(nh):
            hs = slice(h * HEAD_DIM, (h + 1) * HEAD_DIM)
            qr, gr, lbh = q_ref[:, hs], g_ref[:, hs], lb[:, hs]
            p = _hgrn_prep(qr, f_ref[:, hs], lbh)
            vb = v_ref[:, hs].astype(BF16)
            eba, eab = jnp.exp(p["b"] - p["a"]), jnp.exp(p["a"] - p["b"])
            eb, elb = jnp.exp(p["b"]), jnp.exp(p["bl"] - p["b"])
            dec = jnp.exp(p["bl"])
            q_hat, k_hat = p["q"] * eba, p["k"] * eab
            q_in, k_out = p["q"] * eb, p["k"] * elb
            q_hat_b, k_hat_b = q_hat.astype(BF16), k_hat.astype(BF16)
            q_in_b, k_out_b = q_in.astype(BF16), k_out.astype(BF16)

            o, dout = o_ref[:, hs], d_ref[:, hs]
            sg = _sigmoid(gr)
            r, xh = _rms_stats(o)
            hq, hf, hv, hg = [slice(k * H + h * HEAD_DIM, k * H + (h + 1) * HEAD_DIM) for k in range(4)]
            dp_ref[:, hg] = (dout * (xh * gam) * (sg * (1.0 + gr * (1.0 - sg)))).astype(BF16)
            dn = dout * (gr * sg)
            red_ref[1:2, hs] += jnp.sum(dn * xh, axis=0, keepdims=True)
            do = _rms_bwd(dn * gam, xh, r)
            dob = do.astype(BF16)
            dot_b = do.T.astype(BF16)

            att_t = jnp.where(mask_t, _nt(k_hat_b, q_hat_b), 0.0).astype(BF16)
            dv_intra = _nn(att_t, dob)
            datt = jnp.where(mask, _nt(dob, vb), 0.0).astype(BF16)
            dqh = _nn(datt, k_hat_b)
            datt_t = jnp.where(mask_t, _nt(vb, dob), 0.0).astype(BF16)
            dkh = _nn(datt_t, q_hat_b)

            dst = dst_ref[h]
            for c in reversed(range(nc)):
                dsall_ref[c] = dst
                q_c = jnp.where(rowc == c, q_in_b, jnp.zeros_like(q_in_b))
                dst = dst * dec[c * CHUNK:c * CHUNK + 1, :] + _nn(dot_b, q_c)
            dst_ref[h] = dst
            for c in range(nc):
                rs = slice(c * CHUNK, (c + 1) * CHUNK)
                ds_c = dsall_ref[c]
                dsb = ds_c.astype(BF16)
                st_prev = s_ref[c, h]
                tmp_ref[0, rs, :] = _nt(k_out_b[rs], dsb)
                tmp_ref[1, rs, :] = _nn(vb[rs], dsb)
                tmp_ref[2, rs, :] = _nn(dob[rs], st_prev)
                ddec = jnp.sum(ds_c * st_prev.astype(F32), axis=0, keepdims=True)
                tmp_ref[3, rs, :] = jnp.broadcast_to(ddec * dec[c * CHUNK:c * CHUNK + 1, :],
                                                     (CHUNK, HEAD_DIM))
            dko, dqi = tmp_ref[1], tmp_ref[2]
            dq = dqh * eba + dqi * eb
            dk = dkh * eab + dko * elb
            tko = dko * k_out
            db = dqh * q_hat - dkh * k_hat + dqi * q_in - tko
            dlog = (_seg_scan(db, p["r32"], False)
                    + _bcast_row(_seg_scan(tko, p["r32"], True), CHUNK - 1) + tmp_ref[3])
            df = dlog / p["f"] - dk
            sig = p["sig"]
            red_ref[0:1, hs] += jnp.sum(df * (1.0 - sig), axis=0, keepdims=True)
            dp_ref[:, hf] = (df * (1.0 - lbh) * sig * (1.0 - sig)).astype(BF16)
            sq = p["sq"]
            dp_ref[:, hq] = (dq * (HEAD_DIM ** -0.5) * (sq * (1.0 + qr * (1.0 - sq)))).astype(BF16)
            dp_ref[:, hv] = (dv_intra + tmp_ref[0]).astype(BF16)

    col = lambda k: pl.BlockSpec((TL, H), lambda i, k=k: (nt - 1 - i, k))
    rev = pl.BlockSpec((TL, H), lambda i: (nt - 1 - i, 0))
    return _pallas_call(
        body, name="hgrn_bwd", grid=(nt,),
        in_specs=[col(0), col(1), col(2), col(3),
                  pl.BlockSpec(lower_bounds.shape, lambda i: (0, 0)),
                  pl.BlockSpec(gamma.shape, lambda i: (0, 0)),
                  rev, rev,
                  pl.BlockSpec((nc, nh, HEAD_DIM, HEAD_DIM), lambda i: (nt - 1 - i, 0, 0, 0)), ANY],
        out_specs=[pl.BlockSpec((TL, 4 * H), lambda i: (nt - 1 - i, 0)), pl.BlockSpec((8, H), lambda i: (0, 0))],
        out_shape=[_sds((L, 4 * H), BF16), _sds((8, H), F32)],
        scratch_shapes=[pltpu.VMEM((nh, HEAD_DIM, HEAD_DIM), F32),
                        pltpu.VMEM((nc, HEAD_DIM, HEAD_DIM), F32),
                        pltpu.VMEM((4, TL, HEAD_DIM), F32)],
        compiler_params=_params(("arbitrary",), 48),
    )(proj, proj, proj, proj, lower_bounds, gamma, o_pre, d_out, s_saved, after)


def _shift_down(u, s, row):
    return jnp.where(row >= s, pltpu.roll(u, s, 0), 0.0)


def _shift_up(u, s, row):
    n = u.shape[0]
    return jnp.where(row < n - s, pltpu.roll(u, n - s, 0), 0.0)


def _conv_specs(L, H):
    per = H // LANES
    return [pl.BlockSpec((L, LANES), lambda j, o=o: (0, o * per + j)) for o in (4, 5, 6)]


def _conv_fwd(proj, conv_w, H, after):
    L = proj.shape[0]

    def body(c_ref, b_ref, x_ref, w_ref, after_ref, o_ref):
        row = lax.broadcasted_iota(jnp.int32, (L, LANES), 0)
        u = c_ref[...] * x_ref[...]
        w = w_ref[...]
        y = w[0:1] * _shift_down(u, 2, row) + w[1:2] * _shift_down(u, 1, row) + w[2:3] * u
        o_ref[...] = (b_ref[...] * y).astype(BF16)

    return _pallas_call(
        body, name="conv_fwd", grid=(H // LANES,),
        in_specs=_conv_specs(L, H) + [pl.BlockSpec((3, LANES), lambda j: (0, j)), ANY],
        out_specs=pl.BlockSpec((L, LANES), lambda j: (0, j)),
        out_shape=_sds((L, H), BF16),
        compiler_params=_params(("parallel",), 48),
    )(proj, proj, proj, conv_w, after)


def _conv_bwd(proj, conv_w, dcb, H, after):
    L = proj.shape[0]

    def body(c_ref, b_ref, x_ref, w_ref, d_ref, after_ref, dc_ref, db_ref, dx_ref, dw_ref):
        row = lax.broadcasted_iota(jnp.int32, (L, LANES), 0)
        cg, xb = c_ref[...], x_ref[...]
        u = cg * xb
        u1, u2 = _shift_down(u, 1, row), _shift_down(u, 2, row)
        w = w_ref[...]
        y = w[0:1] * u2 + w[1:2] * u1 + w[2:3] * u
        d = d_ref[...]
        db_ref[...] = (d * y).astype(BF16)
        dy = d * b_ref[...]
        du = w[2:3] * dy + w[1:2] * _shift_up(dy, 1, row) + w[0:1] * _shift_up(dy, 2, row)
        dw_ref[0:1, :] = jnp.sum(dy * u2, axis=0, keepdims=True)
        dw_ref[1:2, :] = jnp.sum(dy * u1, axis=0, keepdims=True)
        dw_ref[2:3, :] = jnp.sum(dy * u, axis=0, keepdims=True)
        dc_ref[...] = (du * xb).astype(BF16)
        dx_ref[...] = (du * cg).astype(BF16)

    blk = pl.BlockSpec((L, LANES), lambda j: (0, j))
    return _pallas_call(
        body, name="conv_bwd", grid=(H // LANES,),
        in_specs=_conv_specs(L, H) + [pl.BlockSpec((3, LANES), lambda j: (0, j)), blk, ANY],
        out_specs=[blk, blk, blk, pl.BlockSpec((3, LANES), lambda j: (0, j))],
        out_shape=[_sds((L, H), BF16)] * 3 + [_sds((3, H), F32)],
        compiler_params=_params(("parallel",), 56),
    )(proj, proj, proj, conv_w, dcb, after)


def _gate_specs(tm, H):
    return [pl.BlockSpec((tm, H), lambda i, k=k: (i, k)) for k in (7, 8, 9, 10)]


def _fwd_mix(og, cb, proj, x, wat, wbt, wout, g_ffn, H, after):
    L, D = x.shape
    tm = min(L, 512)

    def body(o_ref, cb_ref, ga0, ga1, gb0, gb1, x_ref, wa_ref, wb_ref, wo_ref, g_ref, after_ref,
             sa_ref, sb_ref, ta_ref, tb_ref, m_ref, x1_ref, h2_ref):
        ya, yb = _nt(o_ref[...], wa_ref[...]), _nt(cb_ref[...], wb_ref[...])
        for k, (gar, gbr) in enumerate(((ga0, gb0), (ga1, gb1))):
            cs = slice(k * H, (k + 1) * H)
            sa, sb = _sigmoid(gar[...]), _sigmoid(gbr[...])
            ma, mb = sa * ya[:, cs], sb * yb[:, cs]
            m_ref[:, cs] = (ma + mb).astype(BF16)
            sa_ref[:, cs] = sa.astype(BF16)
            sb_ref[:, cs] = sb.astype(BF16)
            ta_ref[:, cs] = (ma * (1.0 - sa)).astype(BF16)
            tb_ref[:, cs] = (mb * (1.0 - sb)).astype(BF16)
        x1 = x_ref[...] + _nn(m_ref[...], wo_ref[...])
        x1_ref[...] = x1
        _, xh = _rms_stats(x1)
        h2_ref[...] = (xh * g_ref[...]).astype(BF16)

    row = lambda w: pl.BlockSpec((tm, w), lambda i: (i, 0))
    full = lambda a: pl.BlockSpec(a.shape, lambda i: (0,) * a.ndim)
    return _pallas_call(
        body, name="fwd_mix", grid=(L // tm,),
        in_specs=[row(H), row(H)] + _gate_specs(tm, H) + [row(D), full(wat), full(wbt), full(wout),
                                                           full(g_ffn), ANY],
        out_specs=[row(D)] * 7,
        out_shape=[_sds((L, D), BF16)] * 5 + [_sds((L, D), F32), _sds((L, D), BF16)],
        compiler_params=_params(("parallel",), 56),
    )(og, cb, proj, proj, proj, proj, x, wat, wbt, wout, g_ffn, after)


def _bwd_mix(dx1b, sig_a, sig_b, dm_dga, dm_dgb, wat, wbt, wout, H, after):
    L, D = dx1b.shape
    tm = min(L, 512)

    def body(dx_ref, sa_ref, sb_ref, ta_ref, tb_ref, wa_ref, wb_ref, wo_ref, after_ref,
             dya_ref, dyb_ref, dgate_ref, do_ref, dcb_ref):
        dm = _nt(dx_ref[...], wo_ref[...])
        dgate_ref[:, 0:D] = (dm * ta_ref[...].astype(F32)).astype(BF16)
        dgate_ref[:, D:2 * D] = (dm * tb_ref[...].astype(F32)).astype(BF16)
        dya_ref[...] = (dm * sa_ref[...].astype(F32)).astype(BF16)
        dyb_ref[...] = (dm * sb_ref[...].astype(F32)).astype(BF16)
        do_ref[...] = _nn(dya_ref[...], wa_ref[...])
        dcb_ref[...] = _nn(dyb_ref[...], wb_ref[...])

    row = lambda w: pl.BlockSpec((tm, w), lambda i: (i, 0))
    full = lambda a: pl.BlockSpec(a.shape, lambda i: (0,) * a.ndim)
    return _pallas_call(
        body, name="bwd_mix", grid=(L // tm,),
        in_specs=[row(D)] * 5 + [full(wat), full(wbt), full(wout), ANY],
        out_specs=[row(D), row(D), row(2 * D), row(H), row(H)],
        out_shape=[_sds((L, D), BF16)] * 2 + [_sds((L, 2 * D), BF16)] + [_sds((L, H), F32)] * 2,
        compiler_params=_params(("parallel",), 56),
    )(dx1b, sig_a, sig_b, dm_dga, dm_dgb, wat, wbt, wout, after)


def _fwd_ffn_up(h2, wgt, wut):
    L, D = h2.shape
    F = wgt.shape[0]
    tn = F // 2
    tm = min(L, 512)

    def body(h_ref, wg_ref, wu_ref, sa_ref, sb_ref, s_ref):
        h = h_ref[...]
        a, b = _nt(h, wg_ref[...]), _nt(h, wu_ref[...])
        sg = _sigmoid(a)
        silu = a * sg
        sa_ref[...] = (b * sg * (1.0 + a * (1.0 - sg))).astype(BF16)
        sb_ref[...] = silu.astype(BF16)
        s_ref[...] = (silu * b).astype(BF16)

    wspec = pl.BlockSpec((tn, D), lambda j, i: (j, 0))
    ospec = pl.BlockSpec((tm, tn), lambda j, i: (i, j))
    return _pallas_call(
        body, name="fwd_ffn_up", grid=(2, L // tm),
        in_specs=[pl.BlockSpec((tm, D), lambda j, i: (i, 0)), wspec, wspec],
        out_specs=[ospec] * 3,
        out_shape=[_sds((L, F), BF16)] * 3,
        compiler_params=_params(("parallel", "parallel"), 48),
    )(h2, wgt, wut)


def _fwd_down_loss(s, wd, x1, target, g_final):
    L, D = x1.shape
    F = wd.shape[0]
    tm = min(L, 512)

    def body(s_ref, wd_ref, x1_ref, t_ref, g_ref, dx_ref, dxb_ref, red_ref):
        @pl.when(pl.program_id(0) == 0)
        def _():
            red_ref[...] = jnp.zeros_like(red_ref)

        g = g_ref[...]
        r, xh = _rms_stats(x1_ref[...] + _nn(s_ref[...], wd_ref[...]))
        e = xh * g - t_ref[...]
        dy = e * (1.0 / D)
        dx = _rms_bwd(dy * g, xh, r)
        dx_ref[...] = dx
        dxb_ref[...] = dx.astype(BF16)
        red_ref[0:1, :] += jnp.sum(dy * xh, axis=0, keepdims=True)
        red_ref[1:2, :] += jnp.broadcast_to(0.5 * jnp.sum(e * e) * (1.0 / D), (1, D))

    row = pl.BlockSpec((tm, D), lambda i: (i, 0))
    return _pallas_call(
        body, name="fwd_down_loss", grid=(L // tm,),
        in_specs=[pl.BlockSpec((tm, F), lambda i: (i, 0)), pl.BlockSpec((F, D), lambda i: (0, 0)),
                  row, row, pl.BlockSpec((1, D), lambda i: (0, 0))],
        out_specs=[row, row, pl.BlockSpec((8, D), lambda i: (0, 0))],
        out_shape=[_sds((L, D), F32), _sds((L, D), BF16), _sds((8, D), F32)],
        compiler_params=_params(("arbitrary",), 56),
    )(s, wd, x1, target, g_final)


def _bwd_down(dx2b, wd, s_a, s_b):
    L, D = dx2b.shape
    F = wd.shape[0]
    tn = F // 2
    tm = min(L, 512)

    def body(dx_ref, wd_ref, sa_ref, sb_ref, da_ref, db_ref):
        ds = _nt(dx_ref[...], wd_ref[...])
        da_ref[...] = (ds * sa_ref[...].astype(F32)).astype(BF16)
        db_ref[...] = (ds * sb_ref[...].astype(F32)).astype(BF16)

    ospec = pl.BlockSpec((tm, tn), lambda j, i: (i, j))
    return _pallas_call(
        body, name="bwd_down", grid=(2, L // tm),
        in_specs=[pl.BlockSpec((tm, D), lambda j, i: (i, 0)),
                  pl.BlockSpec((tn, D), lambda j, i: (j, 0)), ospec, ospec],
        out_specs=[ospec] * 2,
        out_shape=[_sds((L, F), BF16)] * 2,
        compiler_params=_params(("parallel", "parallel"), 48),
    )(dx2b, wd, s_a, s_b)


def _bwd_ffn_dh(da, db, wgt, wut, x1, dx2, g_ffn, after):
    L, D = x1.shape
    F = wgt.shape[0]
    tm = min(L, 256)

    def body(da_ref, db_ref, wg_ref, wu_ref, x1_ref, dx2_ref, g_ref, after_ref, dx_ref, dxb_ref, red_ref):
        @pl.when(pl.program_id(0) == 0)
        def _():
            red_ref[...] = jnp.zeros_like(red_ref)

        dh = _nn(da_ref[...], wg_ref[...]) + _nn(db_ref[...], wu_ref[...])
        r, xh = _rms_stats(x1_ref[...])
        red_ref[0:1, :] += jnp.sum(dh * xh, axis=0, keepdims=True)
        dx = dx2_ref[...] + _rms_bwd(dh * g_ref[...], xh, r)
        dx_ref[...] = dx
        dxb_ref[...] = dx.astype(BF16)

    row = pl.BlockSpec((tm, D), lambda i: (i, 0))
    aspec = pl.BlockSpec((tm, F), lambda i: (i, 0))
    wspec = pl.BlockSpec((F, D), lambda i: (0, 0))
    return _pallas_call(
        body, name="bwd_ffn_dh", grid=(L // tm,),
        in_specs=[aspec, aspec, wspec, wspec, row, row, pl.BlockSpec((1, D), lambda i: (0, 0)), ANY],
        out_specs=[row, row, pl.BlockSpec((8, D), lambda i: (0, 0))],
        out_shape=[_sds((L, D), F32), _sds((L, D), BF16), _sds((8, D), F32)],
        compiler_params=_params(("arbitrary",), 56),
    )(da, db, wgt, wut, x1, dx2, g_ffn, after)


def _piece_offsets(pieces):
    offsets, total = [], 0
    for p in pieces:
        offsets.append(total)
        total += p.shape[1]
    return offsets, total


def _bwd_in(pieces, w_int, x, dx1, g_mix, after):
    L, D = x.shape
    N = w_int.shape[0]
    tm = min(L, 256)
    n = len(pieces)
    offsets, total = _piece_offsets(pieces)
    assert total == N

    def body(*refs):
        piece_refs = refs[:n]
        w_ref, x_ref, dx1_ref, g_ref, after_ref, dx_ref, red_ref, dp_ref = refs[n:]

        @pl.when(pl.program_id(0) == 0)
        def _():
            red_ref[...] = jnp.zeros_like(red_ref)

        for p_ref, off in zip(piece_refs, offsets):
            dp_ref[:, off:off + p_ref.shape[1]] = p_ref[...]
        dh = _nn(dp_ref[...], w_ref[...])
        r, xh = _rms_stats(x_ref[...])
        red_ref[0:1, :] += jnp.sum(dh * xh, axis=0, keepdims=True)
        dx_ref[...] = dx1_ref[...] + _rms_bwd(dh * g_ref[...], xh, r)

    row = pl.BlockSpec((tm, D), lambda i: (i, 0))
    return _pallas_call(
        body, name="bwd_in", grid=(L // tm,),
        in_specs=[pl.BlockSpec((tm, p.shape[1]), lambda i: (i, 0)) for p in pieces]
        + [pl.BlockSpec((N, D), lambda i: (0, 0)), row, row, pl.BlockSpec((1, D), lambda i: (0, 0)), ANY],
        out_specs=[row, pl.BlockSpec((8, D), lambda i: (0, 0))],
        out_shape=[_sds((L, D), F32), _sds((8, D), F32)],
        scratch_shapes=[pltpu.VMEM((tm, N), BF16)],
        compiler_params=_params(("arbitrary",), 56),
    )(*pieces, w_int, x, dx1, g_mix, after)


def _dw_in(h, pieces, n_cols):
    L, D = h.shape
    tk = min(L, TK_TOKENS // 2)
    n = len(pieces)
    offsets, total = _piece_offsets(pieces)
    assert total == N_CHIPS * n_cols
    plan = []
    for j in range(N_CHIPS):
        lo, hi = j * n_cols, (j + 1) * n_cols
        segments = []
        for p, off in enumerate(offsets):
            a, b = max(lo, off), min(hi, off + pieces[p].shape[1])
            if a < b:
                segments.append((p, a - off, b - a, a - lo))
        plan.append(segments)

    def body(*refs):
        h_ref, piece_refs, o_ref, b_ref = refs[0], refs[1:1 + n], refs[1 + n], refs[2 + n]
        j, k = pl.program_id(0), pl.program_id(1)
        for jj in range(N_CHIPS):
            @pl.when(j == jj)
            def _(jj=jj):
                for p, start, width, at in plan[jj]:
                    b_ref[:, at:at + width] = piece_refs[p][:, start:start + width]

        part = _tn(h_ref[...], b_ref[...])

        @pl.when(k == 0)
        def _():
            o_ref[...] = part

        @pl.when(k > 0)
        def _():
            o_ref[...] += part

    def piece_spec(p):
        used = [j for j in range(N_CHIPS) if any(seg[0] == p for seg in plan[j])]

        def index(j, k):
            in_use = functools.reduce(jnp.logical_or, [j == u for u in used])
            return (jnp.where(in_use, k, 0), 0)

        return pl.BlockSpec((tk, pieces[p].shape[1]), index)

    return _pallas_call(
        body, name="dw_in", grid=(N_CHIPS, L // tk),
        in_specs=[pl.BlockSpec((tk, D), lambda j, k: (k, 0))] + [piece_spec(p) for p in range(n)],
        out_specs=pl.BlockSpec((None, D, n_cols), lambda j, k: (j, 0, 0)),
        out_shape=_sds((N_CHIPS, D, n_cols), F32),
        scratch_shapes=[pltpu.VMEM((tk, n_cols), BF16)],
        compiler_params=_params(("parallel", "arbitrary"), 56),
    )(h, *pieces)


def _mm_tn(name, a, b, a_spec, b_spec, o_block, n_out, n_k):
    def body(a_ref, b_ref, o_ref):
        part = _tn(a_ref[...], b_ref[...])

        @pl.when(pl.program_id(1) == 0)
        def _():
            o_ref[...] = part

        @pl.when(pl.program_id(1) > 0)
        def _():
            o_ref[...] += part

    return _pallas_call(
        body, name=name, grid=(n_out, n_k),
        in_specs=[a_spec, b_spec],
        out_specs=pl.BlockSpec((None,) + o_block, lambda j, k: (j, 0, 0)),
        out_shape=_sds((n_out,) + o_block, F32),
        compiler_params=_params(("parallel", "arbitrary"), 56),
    )(a, b)


TK_TOKENS = 2048


def _dw_cols(name, a, b, n_cols):
    L, M = a.shape
    tk = min(L, TK_TOKENS)
    return _mm_tn(name, a, b, pl.BlockSpec((tk, M), lambda j, k: (k, 0)),
                  pl.BlockSpec((tk, n_cols), lambda j, k: (k, j)), (M, n_cols), N_CHIPS, L // tk)


def _dw_rows(name, a, b):
    L, M = a.shape
    N = b.shape[1]
    tk = min(L, TK_TOKENS)
    return _mm_tn(name, a, b, pl.BlockSpec((tk, M // N_CHIPS), lambda j, k: (k, j)),
                  pl.BlockSpec((tk, N), lambda j, k: (k, 0)), (M // N_CHIPS, N), N_CHIPS, L // tk)


def _dw_rows2(name, a, b):
    L, M = a.shape
    N = b.shape[1]
    tk = min(L, TK_TOKENS)
    return _mm_tn(name, a, b, pl.BlockSpec((tk, M // 2), lambda j, k: (k, j)),
                  pl.BlockSpec((tk, N), lambda j, k: (k, 0)), (M // 2, N), 2, L // tk)


def _place():
    x, y, c = lax.axis_index("x"), lax.axis_index("y"), lax.axis_index("c")
    chips = [(1 - x, y), (x, 1 - y), (1 - x, 1 - y)]
    return x, y, c, 2 * x + y, chips


def _remote(src, dst, send_sem, recv_sem, device):
    return pltpu.make_async_remote_copy(src_ref=src, dst_ref=dst, send_sem=send_sem,
                                        recv_sem=recv_sem, device_id=device, device_id_type=MESH)


def _half(ref, lead, c, r2):
    return ref.at[lead, pl.ds(pl.multiple_of(c * r2, 16), r2), :]


def _cast_place(name, ws, chip_idx):
    n = len(ws)

    def body(k_ref, *refs):
        for w_ref, o_ref in zip(refs[:n], refs[n:]):
            o_ref[...] = w_ref[...].astype(BF16)

    return _pallas_call(
        body, name=name,
        grid_spec=pltpu.PrefetchScalarGridSpec(
            num_scalar_prefetch=1, grid=(2,),
            in_specs=[pl.BlockSpec((w.shape[0] // 2, w.shape[1]), lambda i, k_ref: (i, 0)) for w in ws],
            out_specs=[pl.BlockSpec((None, w.shape[0] // 2, w.shape[1]), lambda i, k_ref: (k_ref[0], i, 0))
                       for w in ws]),
        out_shape=[_sds((N_CHIPS,) + w.shape, BF16) for w in ws],
        compiler_params=_params(("parallel",), 48),
    )(chip_idx, *ws)


def _cast_place_t(name, ws, chip_idx):
    n = len(ws)
    r, cols = ws[0].shape

    def body(k_ref, *refs):
        for w_ref, o_ref in zip(refs[:n], refs[n:]):
            o_ref[...] = w_ref[...].T.astype(BF16)

    return _pallas_call(
        body, name=name,
        grid_spec=pltpu.PrefetchScalarGridSpec(
            num_scalar_prefetch=1, grid=(cols // LANES,),
            in_specs=[pl.BlockSpec((r, LANES), lambda i, k_ref: (0, i))] * n,
            out_specs=[pl.BlockSpec((None, LANES, r), lambda i, k_ref: (k_ref[0], i, 0))] * n),
        out_shape=[_sds((N_CHIPS, cols, r), BF16)] * n,
        compiler_params=_params(("parallel",), 48),
    )(chip_idx, *ws)


def _gather_copies(bufs, whole, send_sems, recv_sems, select=None):
    x, y, c, k, chips = _place()
    pairs = []
    for w, buf in enumerate(bufs):
        for j, (cx, cy) in enumerate(chips):
            if select is not None and not select(w, j):
                continue
            if w in whole:
                mine, theirs = buf.at[k], buf.at[2 * cx + cy]
            else:
                r2 = buf.shape[1] // 2
                mine, theirs = _half(buf, k, c, r2), _half(buf, 2 * cx + cy, c, r2)
            sems = (send_sems.at[w * 3 + j], recv_sems.at[w * 3 + j])
            pairs.append((_remote(mine, mine, *sems, (cx, cy, c)), _remote(theirs, theirs, *sems, (x, y, c))))
    return pairs


def _gather_start(name, groups, after):
    flat = [b for bufs, _, _ in groups for b in bufs]
    nb, ng = len(flat), len(groups)

    def body(*refs):
        ins, sems, token = refs[:nb], refs[nb + 1:nb + 1 + 2 * ng], refs[-1]
        pos = 0
        for g, (bufs, whole, select) in enumerate(groups):
            for send, _ in _gather_copies(ins[pos:pos + len(bufs)], whole, sems[2 * g], sems[2 * g + 1], select):
                send.start()
            pos += len(bufs)
        token[...] = jnp.zeros_like(token)

    sem_shapes = []
    for bufs, _, _ in groups:
        sem_shapes += [pltpu.SemaphoreType.DMA((3 * len(bufs),))] * 2
    out = _pallas_call(
        body, name=name,
        in_specs=[HBM] * nb + [ANY], out_specs=tuple([SEM] * (2 * ng) + [HBM] * nb + [VMEM]),
        out_shape=tuple(sem_shapes + [pltpu.HBM(b.shape, b.dtype) for b in flat] + [_sds((8, LANES), F32)]),
        input_output_aliases={i: 2 * ng + i for i in range(nb)},
        compiler_params=pltpu.CompilerParams(has_side_effects=EFFECT),
    )(*flat, after)
    sems, thru, pos = [], [], 2 * ng
    for g, (bufs, _, _) in enumerate(groups):
        sems.append((out[2 * g], out[2 * g + 1]))
        thru.append(list(out[pos:pos + len(bufs)]))
        pos += len(bufs)
    return sems, thru, out[-1]


def _gather_wait(name, bufs, whole, sems, after, select=None):
    nb = len(bufs)

    def body(*refs):
        ins, send_sems, recv_sems = refs[:nb], refs[nb], refs[nb + 1]
        for send, arrival in _gather_copies(ins, whole, send_sems, recv_sems, select):
            send.wait_send()
            arrival.wait_recv()

    return _pallas_call(
        body, name=name,
        in_specs=[HBM] * nb + [SEM, SEM, ANY], out_specs=[HBM] * nb,
        out_shape=[pltpu.HBM(b.shape, b.dtype) for b in bufs],
        input_output_aliases={i: i for i in range(nb)},
        compiler_params=pltpu.CompilerParams(has_side_effects=EFFECT),
    )(*bufs, sems[0], sems[1], after)


def _gather_forward(name, bufs, sources=(0, 1, 2)):
    n = len(bufs)

    def body(*refs):
        outs = refs[n:2 * n]
        send_sems, recv_sems = refs[2 * n:]
        x, y, c, _, chips = _place()
        sends = []
        for w in range(n):
            r2 = outs[w].shape[1] // 2
            for j in sources:
                landed = _half(outs[w], 2 * chips[j][0] + chips[j][1], c, r2)
                sends.append(_remote(landed, landed, send_sems.at[w * 3 + j], recv_sems.at[w * 3 + j],
                                     (x, y, 1 - c)))
        for cp in sends:
            cp.start()
        for w in range(n):
            r2 = outs[w].shape[1] // 2
            for j in sources:
                got = _half(outs[w], 2 * chips[j][0] + chips[j][1], 1 - c, r2)
                _remote(got, got, send_sems.at[w * 3 + j], recv_sems.at[w * 3 + j], (x, y, c)).wait_recv()
        for cp in sends:
            cp.wait_send()

    return _pallas_call(
        body, name=name,
        in_specs=[ANY] * n, out_specs=[ANY] * n,
        out_shape=[_sds(b.shape, b.dtype) for b in bufs],
        input_output_aliases={i: i for i in range(n)},
        scratch_shapes=[pltpu.SemaphoreType.DMA((n * 3,)), pltpu.SemaphoreType.DMA((n * 3,))],
    )(*bufs)


def _rs_add(name, grads3, from_sibling, c_idx):
    n = len(grads3)

    def body(c_ref, *refs):
        for g_ref, s_ref, o_ref in zip(refs[:n], refs[n:2 * n], refs[2 * n:]):
            o_ref[...] = (g_ref[...] + s_ref[...]).astype(BF16)

    mine = [pl.BlockSpec((None,) + s.shape[1:], lambda k, c_ref: (k, c_ref[0], 0)) for s in from_sibling]
    whole = [pl.BlockSpec((None,) + s.shape[1:], lambda k, c_ref: (k, 0, 0)) for s in from_sibling]
    return _pallas_call(
        body, name=name,
        grid_spec=pltpu.PrefetchScalarGridSpec(num_scalar_prefetch=1, grid=(N_CHIPS,), in_specs=mine + whole,
                                               out_specs=whole),
        out_shape=[_sds(s.shape, BF16) for s in from_sibling],
        compiler_params=_params(("parallel",), 48),
    )(c_idx, *grads3, *from_sibling)


def _split_start(name, arrays, n_sems, pairs_fn):
    n = len(arrays)

    def body(*refs):
        for send, _ in pairs_fn(refs[:n], refs[n], refs[n + 1]):
            send.start()
        refs[-1][...] = jnp.zeros_like(refs[-1])

    out = _pallas_call(
        body, name=name,
        in_specs=[HBM] * n, out_specs=tuple([SEM, SEM] + [HBM] * n + [VMEM]),
        out_shape=tuple([pltpu.SemaphoreType.DMA((n_sems,))] * 2 + [pltpu.HBM(a.shape, a.dtype) for a in arrays]
                        + [_sds((8, LANES), F32)]),
        input_output_aliases={i: 2 + i for i in range(n)},
        compiler_params=pltpu.CompilerParams(has_side_effects=EFFECT),
    )(*arrays)
    return (out[0], out[1]), list(out[2:2 + n]), out[-1]


def _split_wait(name, sems, arrays, pairs_fn, after):
    n = len(arrays)

    def body(*refs):
        for send, arrival in pairs_fn(refs[:n], refs[n], refs[n + 1]):
            send.wait_send()
            arrival.wait_recv()

    return list(_pallas_call(
        body, name=name,
        in_specs=[HBM] * n + [SEM, SEM, ANY], out_specs=[HBM] * n,
        out_shape=[pltpu.HBM(a.shape, a.dtype) for a in arrays],
        input_output_aliases={i: i for i in range(n)},
        compiler_params=pltpu.CompilerParams(has_side_effects=EFFECT),
    )(*arrays, sems[0], sems[1], after))


def _forward_pairs(bufs, send_sems, recv_sems):
    x, y, c, _, chips = _place()
    pairs = []
    for w, buf in enumerate(bufs):
        r2 = buf.shape[1] // 2
        for j, (cx, cy) in enumerate(chips):
            landed, theirs = _half(buf, 2 * cx + cy, c, r2), _half(buf, 2 * cx + cy, 1 - c, r2)
            sems = (send_sems.at[w * 3 + j], recv_sems.at[w * 3 + j])
            pairs.append((_remote(landed, landed, *sems, (x, y, 1 - c)), _remote(theirs, theirs, *sems, (x, y, c))))
    return pairs


def _sibling_pairs(arrays, send_sems, recv_sems):
    x, y, c, _, _ = _place()
    n = len(arrays) // 2
    pairs = []
    for w in range(n):
        r2 = arrays[w].shape[1] // 2
        cp = _remote(_half(arrays[w], slice(None), 1 - c, r2), arrays[n + w], send_sems.at[w], recv_sems.at[w],
                     (x, y, 1 - c))
        pairs.append((cp, cp))
    return pairs


def _ici_pairs(arrays, send_sems, recv_sems):
    x, y, c, _, chips = _place()
    n = len(arrays) // 2
    pairs = []
    for w in range(n):
        for j, (cx, cy) in enumerate(chips):
            cp = _remote(arrays[w].at[2 * cx + cy], arrays[n + w].at[j],
                         send_sems.at[w * 3 + j], recv_sems.at[w * 3 + j], (cx, cy, c))
            pairs.append((cp, cp))
    return pairs


def _rs_sum(name, partials, received, place_idx):
    n = len(partials)
    nb = 2
    blocks = [(p.shape[1] // nb, p.shape[2]) for p in partials]

    def body(idx_ref, *refs):
        for p_ref, r_ref, o_ref in zip(refs[:n], refs[n:2 * n], refs[2 * n:]):
            o_ref[...] = ((p_ref[...].astype(F32) + r_ref[0].astype(F32))
                          + (r_ref[1].astype(F32) + r_ref[2].astype(F32)))

    return _pallas_call(
        body, name=name,
        grid_spec=pltpu.PrefetchScalarGridSpec(
            num_scalar_prefetch=1, grid=(nb,),
            in_specs=[pl.BlockSpec((None,) + b, lambda i, idx: (idx[0], i, 0)) for b in blocks]
            + [pl.BlockSpec((3,) + b, lambda i, idx: (0, i, 0)) for b in blocks],
            out_specs=[pl.BlockSpec(b, lambda i, idx: (idx[1] * nb + i, 0)) for b in blocks]),
        out_shape=[_sds((2 * p.shape[1], p.shape[2]), F32) for p in partials],
        compiler_params=_params(("parallel",), 48),
    )(place_idx, *partials, *received)


def _share_pairs(arrays, send_sems, recv_sems):
    x, y, c, _, _ = _place()
    pairs = []
    for w, arr in enumerate(arrays):
        r2 = arr.shape[0] // 2
        mine = arr.at[pl.ds(pl.multiple_of(c * r2, 8), r2), :]
        theirs = arr.at[pl.ds(pl.multiple_of((1 - c) * r2, 8), r2), :]
        sems = (send_sems.at[w], recv_sems.at[w])
        pairs.append((_remote(mine, mine, *sems, (x, y, 1 - c)), _remote(theirs, theirs, *sems, (x, y, c))))
    return pairs


def _small_allreduce(red_mix, red_ffn, red_final, red_hg, g_conv, after):
    rows = N_SMALL_ROWS
    D = red_mix.shape[1]
    H = red_hg.shape[1]

    def body(mix_ref, ffn_ref, fin_ref, hg_ref, cv_ref, after_ref, sum_ref, all_ref, in_ref, send_sems,
             recv_sems):
        in_ref[...] = jnp.zeros_like(in_ref)
        in_ref[0:1, :] = mix_ref[0:1, :]
        in_ref[1:2, :] = ffn_ref[0:1, :]
        in_ref[2:3, :] = fin_ref[0:1, :]
        gam = hg_ref[1:2, 0:HEAD_DIM]
        for h in range(1, H // HEAD_DIM):
            gam = gam + hg_ref[1:2, h * HEAD_DIM:(h + 1) * HEAD_DIM]
        in_ref[3:4, 0:HEAD_DIM] = gam
        in_ref[3:4, HEAD_DIM:2 * HEAD_DIM] = fin_ref[1:2, 0:HEAD_DIM]
        in_ref[4:5, 0:H] = hg_ref[0:1, :]
        in_ref[6:9, 0:H] = cv_ref[...]
        x, y, c, _, _ = _place()
        me = 4 * x + 2 * y + c
        all_ref[me] = in_ref[...]
        copies = []
        for m in range(1, 8):
            mx, my, mc = (m >> 2) & 1, (m >> 1) & 1, m & 1
            px, py, pc = x ^ mx, y ^ my, c ^ mc
            copies.append((_remote(in_ref, all_ref.at[me], send_sems.at[m - 1], recv_sems.at[m - 1],
                                   (px, py, pc)), 4 * px + 2 * py + pc, m))
        for cp, _, _ in copies:
            cp.start()
        for _, peer, m in copies:
            _remote(in_ref, all_ref.at[peer], send_sems.at[m - 1], recv_sems.at[m - 1],
                    (x, y, c)).wait_recv()
        for cp, _, _ in copies:
            cp.wait_send()
        total = all_ref[0]
        for d in range(1, 8):
            total = total + all_ref[d]
        sum_ref[...] = total

    return _pallas_call(
        body, name="small_allreduce", pin=False,
        in_specs=[VMEM] * 5 + [ANY], out_specs=[VMEM, VMEM],
        out_shape=[_sds((rows, D), F32), _sds((8, rows, D), F32)],
        scratch_shapes=[pltpu.VMEM((rows, D), F32), pltpu.SemaphoreType.DMA((7,)),
                        pltpu.SemaphoreType.DMA((7,))],
    )(red_mix, red_ffn, red_final, red_hg, g_conv, after)[0]


def _adamw_math(w, g, m, v):
    m = ADAM_B1 * m + (1.0 - ADAM_B1) * g
    v = ADAM_B2 * v + (1.0 - ADAM_B2) * jnp.square(g)
    m_hat = m / (1.0 - ADAM_B1 ** ADAM_STEP)
    v_hat = v / (1.0 - ADAM_B2 ** ADAM_STEP)
    delta = -ADAM_LR * (m_hat / (jnp.sqrt(v_hat) + ADAM_EPS) + ADAM_WD * w)
    return delta, m, v


def _adamw(name, gs, ws, ms, vs):
    n = len(gs)
    nb = 4

    def body(*refs):
        ins, outs = refs[:4 * n], refs[4 * n:]
        for j in range(n):
            g_ref, w_ref, m_ref, v_ref = ins[j], ins[n + j], ins[2 * n + j], ins[3 * n + j]
            go_ref, d_ref, mo_ref, vo_ref = outs[4 * j:4 * j + 4]
            g = g_ref[...]
            go_ref[...] = g
            d_ref[...], mo_ref[...], vo_ref[...] = _adamw_math(w_ref[...], g, m_ref[...], v_ref[...])

    blk = [pl.BlockSpec((g.shape[0] // nb, g.shape[1]), lambda i: (i, 0)) for g in gs]
    out = _pallas_call(
        body, name=name, grid=(nb,),
        in_specs=blk * 4, out_specs=[b for b in blk for _ in range(4)],
        out_shape=[_sds(g.shape, F32) for g in gs for _ in range(4)],
        compiler_params=_params(("parallel",), 56),
    )(*gs, *ws, *ms, *vs)
    return [list(out[4 * j:4 * j + 4]) for j in range(n)]


def _small_update(total, chip_idx, ws, ms, vs):
    n = len(ws)
    H = ws[1].shape[1]

    def body(idx_ref, tot_ref, *refs):
        w, m, v, outs = refs[:n], refs[n:2 * n], refs[2 * n:3 * n], refs[3 * n:]
        chip = idx_ref[0]
        p0 = _lower_bound(w[1][...])
        dl0 = p0 * (1.0 - p0) * tot_ref[4:5, 0:H]
        conv = jnp.zeros((3, LANES), F32)
        for k in range(N_CHIPS):
            conv = jnp.where(chip == k, tot_ref[6:9, k * LANES:(k + 1) * LANES], conv)
        grads = [tot_ref[0:1, :], None, tot_ref[3:4, 0:HEAD_DIM], conv, tot_ref[1:2, :], tot_ref[2:3, :]]
        for p in range(n):
            g_ref, d_ref, mo_ref, vo_ref = outs[4 * p:4 * p + 4]
            if p == 1:
                for row, g in ((slice(0, 1), dl0), (slice(1, 2), -dl0)):
                    g_ref[row, :] = g
                    d_ref[row, :], mo_ref[row, :], vo_ref[row, :] = _adamw_math(
                        w[p][row, :], g, m[p][row, :], v[p][row, :])
            else:
                g_ref[...] = grads[p]
                d_ref[...], mo_ref[...], vo_ref[...] = _adamw_math(w[p][...], grads[p], m[p][...], v[p][...])
        outs[4 * n][...] = tot_ref[3:4, HEAD_DIM:2 * HEAD_DIM]

    full = lambda a: pl.BlockSpec(a.shape, lambda i, idx: (0,) * a.ndim)
    out_shape = [_sds(w.shape, F32) for w in ws for _ in range(4)] + [_sds((1, LANES), F32)]
    return _pallas_call(
        body, name="small_update",
        grid_spec=pltpu.PrefetchScalarGridSpec(
            num_scalar_prefetch=1, grid=(1,),
            in_specs=[full(total)] + [full(a) for a in ws + ms + vs],
            out_specs=[full(s) for s in out_shape]),
        out_shape=out_shape,
    )(chip_idx, total, *ws, *ms, *vs)


def kernel(x, norm_mix_g, w_in, lower_bounds, hg_norm_g, conv_w, w_branch_a, w_branch_b, w_out, norm_ffn_g, w_ffn_gate, w_ffn_up, w_ffn_down, norm_final_g, loss_target, m_norm_mix_g, m_w_in, m_lower_bounds, m_hg_norm_g, m_conv_w, m_w_branch_a, m_w_branch_b, m_w_out, m_norm_ffn_g, m_w_ffn_gate, m_w_ffn_up, m_w_ffn_down, m_norm_final_g, v_norm_mix_g, v_w_in, v_lower_bounds, v_hg_norm_g, v_conv_w, v_w_branch_a, v_w_branch_b, v_w_out, v_norm_ffn_g, v_w_ffn_gate, v_w_ffn_up, v_w_ffn_down, v_norm_final_g):
    _, L, D = x.shape
    H = D // 2
    assert lower_bounds.shape == (2, H) and hg_norm_g.shape == (1, HEAD_DIM)
    assert conv_w.shape == (1, 3, LANES) and w_in.shape[2] * N_CHIPS == 11 * H
    x2d, target = x.reshape(L, D), loss_target.reshape(L, D)
    g_final = norm_final_g.reshape(1, D)
    chip = 2 * lax.axis_index("x") + lax.axis_index("y")
    core = lax.axis_index("c")

    tr = lambda w: jnp.transpose(w[0])
    big = [w_in[0], w_branch_a[0], w_branch_b[0], w_out[0], tr(w_ffn_gate), tr(w_ffn_up), w_ffn_down[0]]
    big_m = [m_w_in[0], m_w_branch_a[0], m_w_branch_b[0], m_w_out[0], tr(m_w_ffn_gate), tr(m_w_ffn_up),
             m_w_ffn_down[0]]
    big_v = [v_w_in[0], v_w_branch_a[0], v_w_branch_b[0], v_w_out[0], tr(v_w_ffn_gate), tr(v_w_ffn_up),
             v_w_ffn_down[0]]
    names = ["w_in", "w_branch_a", "w_branch_b", "w_out", "w_ffn_gate", "w_ffn_up", "w_ffn_down"]

    chip_idx = chip.reshape(1).astype(jnp.int32)
    def per_shape(fn, tag, js, *lists):
        groups = {}
        for pos, a in enumerate(lists[0]):
            groups.setdefault(a.shape, []).append(pos)
        results = [None] * len(js)
        for same in groups.values():
            out = fn(tag + names[js[same[0]]], *[[xs[p] for p in same] for xs in lists])
            for q, p in enumerate(same):
                results[p] = out[q]
        return results

    place_t = lambda name, ws: _cast_place_t(name, ws, chip_idx)
    placed = per_shape(place_t, "place_", [0, 1, 2], big[:3]) + list(_cast_place("place_rest", big[3:], chip_idx))
    conv_placed = lax.dynamic_update_slice(jnp.zeros((N_CHIPS, 3, LANES), F32), conv_w, (chip, 0, 0))
    x_i, y_i = lax.axis_index("x"), lax.axis_index("y")
    blocks = lambda *ks: jnp.stack(ks).astype(jnp.int32)
    near = lambda w, j: j < 2
    far = lambda w, j: w == 1 or j == 2
    near_sems, in_flight, _ = _gather_start("gather_start_near", [([placed[0]], set(), near)], chip_idx)
    w_in_buf = in_flight[0][0]
    h, proj = _fwd_proj_first(x2d, norm_mix_g, w_in_buf, blocks(chip))
    sems, in_flight, _ = _gather_start(
        "gather_start_rest", [([w_in_buf, conv_placed], {1}, far), (placed[1:4], set(), None),
                              (placed[4:], set(), None)], h)
    w_in_buf, conv_buf = in_flight[0]
    (w_in_buf,) = _gather_wait("gather_wait_in_near", [w_in_buf], set(), near_sems[0], h, near)
    (w_in_buf,) = _gather_forward("gather_fwd_in_near", [w_in_buf], (0, 1))
    proj = _fwd_proj_more("fwd_proj_near", h, w_in_buf, proj,
                          blocks(2 * (1 - x_i) + y_i, 2 * x_i + (1 - y_i)))
    w_in_buf, conv_all = _gather_wait("gather_wait_in_far", [w_in_buf, conv_buf], {1}, sems[0], proj, far)
    (w_int3,) = _gather_forward("gather_fwd_in_far", [w_in_buf], (2,))
    proj = _fwd_proj_more("fwd_proj_far", h, w_int3, proj, blocks(2 * (1 - x_i) + (1 - y_i)))
    w_int = w_int3.reshape(-1, D)
    conv_full = jnp.transpose(conv_all, (1, 0, 2)).reshape(3, H)
    og, o_pre, s_saved = _hgrn_fwd(proj, lower_bounds, hg_norm_g, H)
    landed = _gather_wait("gather_wait_mix", in_flight[1], set(), sems[1], og)
    fwd_sems, landed, token = _split_start("gather_fwd_mix_start", landed, 9, _forward_pairs)
    cb = _conv_fwd(proj, conv_full, H, token)
    wat3, wbt3, wout3 = _split_wait("gather_fwd_mix_wait", fwd_sems, landed, _forward_pairs, cb)
    wat, wbt, wout = wat3.reshape(D, H), wbt3.reshape(D, H), wout3.reshape(D, D)
    landed = _gather_wait("gather_wait_ffn", in_flight[2], set(), sems[2], cb)
    fwd_sems, landed, token = _split_start("gather_fwd_ffn_start", landed, 9, _forward_pairs)
    sig_a, sig_b, dm_dga, dm_dgb, merged, x1, h2 = _fwd_mix(og, cb, proj, x2d, wat, wbt, wout, norm_ffn_g,
                                                              H, token)
    wgt3, wut3, wd3 = _split_wait("gather_fwd_ffn_wait", fwd_sems, landed, _forward_pairs, h2)
    d_ff = N_CHIPS * wd3.shape[1]
    wgt, wut, wd = wgt3.reshape(d_ff, D), wut3.reshape(d_ff, D), wd3.reshape(d_ff, D)
    ffn_ds_da, ffn_ds_db, ffn_s = _fwd_ffn_up(h2, wgt, wut)
    dx2, dx2b, red_final = _fwd_down_loss(ffn_s, wd, x1, target, g_final)

    c_idx = core.reshape(1).astype(jnp.int32)
    place_idx = jnp.stack([chip, core]).astype(jnp.int32)

    def sibling_start(tag, grads):
        bufs = [lax.empty((N_CHIPS, g.shape[1] // 2, g.shape[2]), F32) for g in grads]
        return _split_start("rs_sibling_start_" + tag, list(grads) + bufs, len(grads), _sibling_pairs)

    def ici_start(tag, js, grads, from_sibling):
        partials = list(_rs_add("rs_add_" + tag, grads, from_sibling, c_idx))
        landings = [lax.empty((3,) + p.shape[1:], BF16) for p in partials]
        return _split_start("rs_ici_start_" + tag, partials + landings, 3 * len(js), _ici_pairs)

    def ici_start_behind(tag, js, started, after):
        n = len(js)
        arrays = _split_wait("rs_sibling_wait_" + tag, started[0], started[1], _sibling_pairs, after)
        return ici_start(tag, js, arrays[:n], arrays[n:])

    def sums(tag, started, after):
        partials, received = [], []
        for group, group_js, start in started:
            arrays = _split_wait("rs_ici_wait_" + group, start[0], start[1], _ici_pairs, after)
            partials += arrays[:len(group_js)]
            received += arrays[len(group_js):]
        return list(_rs_sum("rs_sum_" + tag, partials, received, place_idx))

    def adamw(tag, js, grads):
        return _adamw("adamw_" + tag, grads, *[[src[j] for j in js] for src in (big, big_m, big_v)])

    shards3 = lambda g: g.reshape(N_CHIPS, d_ff // N_CHIPS, D)
    da, db = _bwd_down(dx2b, wd, ffn_ds_da, ffn_ds_db)
    g_wd = shards3(_dw_rows2("dw_ffn_down", ffn_s, dx2b))
    g_wg = shards3(_dw_rows2("dw_ffn_gate", da, h2))
    g_wu = shards3(_dw_rows2("dw_ffn_up", db, h2))
    ffn_sibling = sibling_start("ffn", [g_wg, g_wu, g_wd])
    dx1, dx1b, red_ffn = _bwd_ffn_dh(da, db, wgt, wut, x1, dx2, norm_ffn_g, ffn_sibling[2])
    ffn_ici = ici_start_behind("ffn", [4, 5, 6], ffn_sibling, dx1b)
    dya, dyb, d_gates, d_o, d_cb = _bwd_mix(dx1b, sig_a, sig_b, dm_dga, dm_dgb, wat, wbt, wout, H, ffn_ici[2])
    g_wout = _dw_rows("dw_out", merged, dx1b)
    g_wa = _dw_cols("dw_branch_a", og, dya, D // N_CHIPS)
    g_wb = _dw_cols("dw_branch_b", cb, dyb, D // N_CHIPS)
    mix_sibling = sibling_start("mix", [g_wa, g_wb, g_wout])
    d_hgrn, red_hg = _hgrn_bwd(proj, lower_bounds, hg_norm_g, o_pre, d_o, s_saved, H, mix_sibling[2])
    mix_ici = ici_start_behind("mix", [1, 2, 3], mix_sibling, d_hgrn)
    dcg, dbg, dxb, g_conv = _conv_bwd(proj, conv_full, d_cb, H, mix_ici[2])
    dproj = [d_hgrn, dcg, dbg, dxb, d_gates]
    g_win = _dw_in(h, dproj, w_int3.shape[1])
    in_sibling = sibling_start("in", [g_win])
    halves = sums("rest", [("mix", [1, 2, 3], mix_ici), ("ffn", [4, 5, 6], ffn_ici)], in_sibling[2])
    rest_share = _split_start("rs_share_start_rest", halves, len(halves), _share_pairs)
    in_ici = ici_start_behind("in", [0], in_sibling, rest_share[2])
    grad_x, red_mix = _bwd_in(dproj, w_int, x2d, dx1, norm_mix_g, in_ici[2])
    in_share = _split_start("rs_share_start_in", sums("in", [("in", [0], in_ici)], grad_x), 1, _share_pairs)
    total = _small_allreduce(red_mix, red_ffn, red_final, red_hg, g_conv, in_share[2])
    rest_grads = _split_wait("rs_share_wait_rest", rest_share[0], rest_share[1], _share_pairs, total)
    big_out = [None] + adamw("rest", [1, 2, 3, 4, 5, 6], rest_grads)
    in_grad = _split_wait("rs_share_wait_in", in_share[0], in_share[1], _share_pairs, big_out[6][0])
    big_out[0] = adamw("in", [0], in_grad)[0]

    def smalls(mix, lb, hg, cw, ffn, fin):
        return [mix, lb, hg, cw[0], ffn, fin.reshape(1, D)]

    small_out = _small_update(
        total, chip_idx,
        smalls(norm_mix_g, lower_bounds, hg_norm_g, conv_w, norm_ffn_g, norm_final_g),
        smalls(m_norm_mix_g, m_lower_bounds, m_hg_norm_g, m_conv_w, m_norm_ffn_g, m_norm_final_g),
        smalls(v_norm_mix_g, v_lower_bounds, v_hg_norm_g, v_conv_w, v_norm_ffn_g, v_norm_final_g))

    def outputs(i):
        big_i = [big_out[j][i] for j in range(7)]
        mix, lb, hg, cw, ffn, fin = [small_out[4 * p + i] for p in range(6)]
        return [mix, big_i[0][None], lb, hg, cw[None], big_i[1][None], big_i[2][None], big_i[3][None], ffn,
                big_i[4].T[None], big_i[5].T[None], big_i[6][None], fin.reshape(D)]

    outs = [small_out[24][0, 0], grad_x.reshape(1, L, D)]
    for i in range(4):
        outs += outputs(i)
    return tuple(outs)
```

```python
import functools

import jax
import jax.numpy as jnp
from jax import lax
from jax.experimental import pallas as pl
from jax.experimental.pallas import tpu as pltpu

F32 = jnp.float32
BF16 = jnp.bfloat16
EPS = 1e-6
CHUNK = 32
HEAD_DIM = 128
LANES = 128
N_CHIPS = 4
N_SMALL_ROWS = 16

ADAM_LR = 0.001
ADAM_B1 = 0.9
ADAM_B2 = 0.999
ADAM_EPS = 1e-08
ADAM_WD = 0.01
ADAM_STEP = 10

MESH = pl.DeviceIdType.MESH
ANY = pl.BlockSpec(memory_space=pl.ANY)
VMEM = pl.BlockSpec(memory_space=pltpu.VMEM)
HBM = pl.BlockSpec(memory_space=pltpu.HBM)
SEM = pl.BlockSpec(memory_space=pltpu.SEMAPHORE)
EFFECT = pltpu.SideEffectType.DATAFLOW_SIDE_EFFECTING


def _sds(shape, dtype):
    return jax.ShapeDtypeStruct(shape, dtype)


def _pallas_call(body, pin=True, **kwargs):
    if not pin:
        return pl.pallas_call(body, **kwargs)
    in_hbm = lambda s: pltpu.HBM(s.shape, s.dtype) if isinstance(s, jax.ShapeDtypeStruct) else s
    kwargs["out_shape"] = jax.tree.map(in_hbm, kwargs["out_shape"])
    call = pl.pallas_call(body, **kwargs)

    def run(*args):
        return call(*[pltpu.with_memory_space_constraint(a, pltpu.HBM) if a.dtype in (F32, BF16) else a
                      for a in args])

    return run


def _params(semantics, vmem_mb):
    return pltpu.CompilerParams(dimension_semantics=semantics, vmem_limit_bytes=vmem_mb << 20)


def _nn(a, b):
    return lax.dot_general(a, b, (((1,), (0,)), ((), ())), preferred_element_type=F32)


def _nt(a, b):
    return lax.dot_general(a, b, (((1,), (1,)), ((), ())), preferred_element_type=F32)


def _tn(a, b):
    return lax.dot_general(a, b, (((0,), (0,)), ((), ())), preferred_element_type=F32)


def _sigmoid(x):
    return jax.nn.sigmoid(x)


def _rms_stats(x):
    r = lax.rsqrt(jnp.mean(x * x, axis=-1, keepdims=True) + EPS)
    return r, x * r


def _rms_bwd(dxh, xh, r):
    return r * (dxh - xh * jnp.mean(dxh * xh, axis=-1, keepdims=True))


def _fwd_proj_first(x, g_mix, w_int3, block):
    L, D = x.shape
    tn = w_int3.shape[1]
    tm = min(L, 1024)

    def body(blk_ref, x_ref, g_ref, w_ref, h_ref, p_ref):
        _, xh = _rms_stats(x_ref[...])
        h = (xh * g_ref[...]).astype(BF16)
        h_ref[...] = h
        p_ref[...] = _nt(h, w_ref[...])

    return _pallas_call(
        body, name="fwd_proj_own",
        grid_spec=pltpu.PrefetchScalarGridSpec(
            num_scalar_prefetch=1, grid=(L // tm,),
            in_specs=[pl.BlockSpec((tm, D), lambda i, blk: (i, 0)),
                      pl.BlockSpec((1, D), lambda i, blk: (0, 0)),
                      pl.BlockSpec((None, tn, D), lambda i, blk: (blk[0], 0, 0))],
            out_specs=[pl.BlockSpec((tm, D), lambda i, blk: (i, 0)),
                       pl.BlockSpec((tm, tn), lambda i, blk: (i, blk[0]))]),
        out_shape=[_sds((L, D), BF16), _sds((L, N_CHIPS * tn), F32)],
        compiler_params=_params(("parallel",), 48),
    )(block, x, g_mix, w_int3)


def _fwd_proj_more(name, h, w_int3, proj, blocks):
    L, D = h.shape
    tn = w_int3.shape[1]
    tm = min(L, 1024)

    def body(blk_ref, h_ref, w_ref, proj_ref, p_ref):
        p_ref[...] = _nt(h_ref[...], w_ref[...])

    return _pallas_call(
        body, name=name,
        grid_spec=pltpu.PrefetchScalarGridSpec(
            num_scalar_prefetch=1, grid=(L // tm, blocks.shape[0]),
            in_specs=[pl.BlockSpec((tm, D), lambda i, j, blk: (i, 0)),
                      pl.BlockSpec((None, tn, D), lambda i, j, blk: (blk[j], 0, 0)), ANY],
            out_specs=pl.BlockSpec((tm, tn), lambda i, j, blk: (i, blk[j]))),
        out_shape=_sds(proj.shape, proj.dtype),
        input_output_aliases={3: 0},
        compiler_params=_params(("parallel", "arbitrary"), 48),
    )(blocks, h, w_int3, proj)


def _lower_bound(lbp):
    l0, l1 = lbp[0:1, :], lbp[1:2, :]
    m = jnp.maximum(l0, l1)
    e0, e1 = jnp.exp(l0 - m), jnp.exp(l1 - m)
    return e0 / (e0 + e1)


def _seg_scan(x, r32, forward):
    n = x.shape[0]
    s = 1
    while s < CHUNK:
        if forward:
            x = x + jnp.where(r32 >= s, pltpu.roll(x, s, 0), 0.0)
        else:
            x = x + jnp.where(r32 < CHUNK - s, pltpu.roll(x, n - s, 0), 0.0)
        s *= 2
    return x


def _bcast_row(x, row):
    n, w = x.shape
    nc = n // CHUNK
    x3 = x.reshape(nc, CHUNK, w)
    return jnp.broadcast_to(x3[:, row:row + 1, :], (nc, CHUNK, w)).reshape(n, w)


def _chunk_total(x):
    n, w = x.shape
    nc = n // CHUNK
    total = jnp.sum(x.reshape(nc, CHUNK, w), axis=1, keepdims=True)
    return jnp.broadcast_to(total, (nc, CHUNK, w)).reshape(n, w)


def _hgrn_prep(q_raw, f_raw, lb):
    r32 = lax.broadcasted_iota(jnp.int32, f_raw.shape, 0) & (CHUNK - 1)
    sig = _sigmoid(f_raw)
    f = lb + (1.0 - lb) * sig
    b = _seg_scan(jnp.log(f), r32, True)
    a = _bcast_row(b, CHUNK // 2 - 1)
    bl = _bcast_row(b, CHUNK - 1)
    sq = _sigmoid(q_raw)
    q = q_raw * sq * (HEAD_DIM ** -0.5)
    return dict(r32=r32, sig=sig, f=f, k=1.0 - f, b=b, a=a, bl=bl, sq=sq, q=q)


def _chunk_masks(n):
    ri = lax.broadcasted_iota(jnp.int32, (n, n), 0)
    ci = lax.broadcasted_iota(jnp.int32, (n, n), 1)
    same = (ri // CHUNK) == (ci // CHUNK)
    return same & (ci <= ri), same & (ri <= ci)


def _hgrn_fwd(proj, lower_bounds, gamma, H):
    L = proj.shape[0]
    nh = H // HEAD_DIM
    TL = min(L, 256)
    nc = TL // CHUNK

    def body(q_ref, f_ref, v_ref, g_ref, lbp_ref, gam_ref, og_ref, o_ref, s_ref, st_ref):
        @pl.when(pl.program_id(0) == 0)
        def _():
            st_ref[...] = jnp.zeros_like(st_ref)

        lb = _lower_bound(lbp_ref[...])
        gam = gam_ref[...]
        mask, _ = _chunk_masks(TL)
        rowc = lax.broadcasted_iota(jnp.int32, (TL, HEAD_DIM), 0) // CHUNK
        for h in range(nh):
            hs = slice(h * HEAD_DIM, (h + 1) * HEAD_DIM)
            p = _hgrn_prep(q_ref[:, hs], f_ref[:, hs], lb[:, hs])
            v = v_ref[:, hs]
            vb = v.astype(BF16)
            vt = v.T.astype(BF16)
            q_hat = (p["q"] * jnp.exp(p["b"] - p["a"])).astype(BF16)
            k_hat = (p["k"] * jnp.exp(p["a"] - p["b"])).astype(BF16)
            q_in = (p["q"] * jnp.exp(p["b"])).astype(BF16)
            k_out = (p["k"] * jnp.exp(p["bl"] - p["b"])).astype(BF16)
            dec = jnp.exp(p["bl"])
            att = jnp.where(mask, _nt(q_hat, k_hat), 0.0).astype(BF16)
            o_intra = _nn(att, vb)
            st = st_ref[h]
            for c in range(nc):
                rs = slice(c * CHUNK, (c + 1) * CHUNK)
                stb = st.astype(BF16)
                s_ref[c, h] = stb
                o_ref[rs, hs] = o_intra[rs] + _nt(q_in[rs], stb)
                k_c = jnp.where(rowc == c, k_out, jnp.zeros_like(k_out))
                st = st * dec[c * CHUNK:c * CHUNK + 1, :] + _nn(vt, k_c)
            st_ref[h] = st
            o = o_ref[:, hs]
            _, xh = _rms_stats(o)
            gr = g_ref[:, hs]
            og_ref[:, hs] = (xh * gam * (gr * _sigmoid(gr))).astype(BF16)

    col = lambda k: pl.BlockSpec((TL, H), lambda i, k=k: (i, k))
    return _pallas_call(
        body, name="hgrn_fwd", grid=(L // TL,),
        in_specs=[col(0), col(1), col(2), col(3),
                  pl.BlockSpec(lower_bounds.shape, lambda i: (0, 0)),
                  pl.BlockSpec(gamma.shape, lambda i: (0, 0))],
        out_specs=[pl.BlockSpec((TL, H), lambda i: (i, 0)),
                   pl.BlockSpec((TL, H), lambda i: (i, 0)),
                   pl.BlockSpec((nc, nh, HEAD_DIM, HEAD_DIM), lambda i: (i, 0, 0, 0))],
        out_shape=[_sds((L, H), BF16), _sds((L, H), F32),
                   _sds((L // CHUNK, nh, HEAD_DIM, HEAD_DIM), BF16)],
        scratch_shapes=[pltpu.VMEM((nh, HEAD_DIM, HEAD_DIM), F32)],
        compiler_params=_params(("arbitrary",), 48),
    )(proj, proj, proj, proj, lower_bounds, gamma)


def _hgrn_bwd(proj, lower_bounds, gamma, o_pre, d_out, s_saved, H, after):
    L = proj.shape[0]
    nh = H // HEAD_DIM
    TL = min(L, 256)
    nc = TL // CHUNK
    nt = L // TL

    def body(q_ref, f_ref, v_ref, g_ref, lbp_ref, gam_ref, o_ref, d_ref, s_ref, after_ref,
             dp_ref, red_ref, dst_ref, dsall_ref, tmp_ref):
        @pl.when(pl.program_id(0) == 0)
        def _():
            dst_ref[...] = jnp.zeros_like(dst_ref)
            red_ref[...] = jnp.zeros_like(red_ref)

        lb = _lower_bound(lbp_ref[...])
        gam = gam_ref[...]
        mask, mask_t = _chunk_masks(TL)
        rowc = lax.broadcasted_iota(jnp.int32, (TL, HEAD_DIM), 0) // CHUNK
        for h in range(nh):
            hs = slice(h * HEAD_DIM, (h + 1) * HEAD_DIM)
            qr, gr, lbh = q_ref[:, hs], g_ref[:, hs], lb[:, hs]
            p = _hgrn_prep(qr, f_ref[:, hs], lbh)
            vb = v_ref[:, hs].astype(BF16)
            eba, eab = jnp.exp(p["b"] - p["a"]), jnp.exp(p["a"] - p["b"])
            eb, elb = jnp.exp(p["b"]), jnp.exp(p["bl"] - p["b"])
            dec = jnp.exp(p["bl"])
            q_hat, k_hat = p["q"] * eba, p["k"] * eab
            q_in, k_out = p["q"] * eb, p["k"] * elb
            q_hat_b, k_hat_b = q_hat.astype(BF16), k_hat.astype(BF16)
            q_in_b, k_out_b = q_in.astype(BF16), k_out.astype(BF16)

            o, dout = o_ref[:, hs], d_ref[:, hs]
            sg = _sigmoid(gr)
            r, xh = _rms_stats(o)
            hq, hf, hv, hg = [slice(k * H + h * HEAD_DIM, k * H + (h + 1) * HEAD_DIM) for k in range(4)]
            dp_ref[:, hg] = (dout * (xh * gam) * (sg * (1.0 + gr * (1.0 - sg)))).astype(BF16)
            dn = dout * (gr * sg)
            red_ref[1:2, hs] += jnp.sum(dn * xh, axis=0, keepdims=True)
            do = _rms_bwd(dn * gam, xh, r)
            dob = do.astype(BF16)
            dot_b = do.T.astype(BF16)

            att_t = jnp.where(mask_t, _nt(k_hat_b, q_hat_b), 0.0).astype(BF16)
            dv_intra = _nn(att_t, dob)
            datt = jnp.where(mask, _nt(dob, vb), 0.0).astype(BF16)
            dqh = _nn(datt, k_hat_b)
            datt_t = jnp.where(mask_t, _nt(vb, dob), 0.0).astype(BF16)
            dkh = _nn(datt_t, q_hat_b)

            dst = dst_ref[h]
            for c in reversed(range(nc)):
                dsall_ref[c] = dst
                q_c = jnp.where(rowc == c, q_in_b, jnp.zeros_like(q_in_b))
                dst = dst * dec[c * CHUNK:c * CHUNK + 1, :] + _nn(dot_b, q_c)
            dst_ref[h] = dst
            for c in range(nc):
                rs = slice(c * CHUNK, (c + 1) * CHUNK)
                ds_c = dsall_ref[c]
                dsb = ds_c.astype(BF16)
                st_prev = s_ref[c, h]
                tmp_ref[0, rs, :] = _nt(k_out_b[rs], dsb)
                tmp_ref[1, rs, :] = _nn(vb[rs], dsb)
                tmp_ref[2, rs, :] = _nn(dob[rs], st_prev)
                ddec = jnp.sum(ds_c * st_prev.astype(F32), axis=0, keepdims=True)
                tmp_ref[3, rs, :] = jnp.broadcast_to(ddec * dec[c * CHUNK:c * CHUNK + 1, :],
                                                     (CHUNK, HEAD_DIM))
            dko, dqi = tmp_ref[1], tmp_ref[2]
            dq = dqh * eba + dqi * eb
            dk = dkh * eab + dko * elb
            tko = dko * k_out
            db = dqh * q_hat - dkh * k_hat + dqi * q_in - tko
            dlog = _seg_scan(db, p["r32"], False) + _chunk_total(tko) + tmp_ref[3]
            df = dlog / p["f"] - dk
            sig = p["sig"]
            red_ref[0:1, hs] += jnp.sum(df * (1.0 - sig), axis=0, keepdims=True)
            dp_ref[:, hf] = (df * (1.0 - lbh) * sig * (1.0 - sig)).astype(BF16)
            sq = p["sq"]
            dp_ref[:, hq] = (dq * (HEAD_DIM ** -0.5) * (sq * (1.0 + qr * (1.0 - sq)))).astype(BF16)
            dp_ref[:, hv] = (dv_intra + tmp_ref[0]).astype(BF16)

    col = lambda k: pl.BlockSpec((TL, H), lambda i, k=k: (nt - 1 - i, k))
    rev = pl.BlockSpec((TL, H), lambda i: (nt - 1 - i, 0))
    return _pallas_call(
        body, name="hgrn_bwd", grid=(nt,),
        in_specs=[col(0), col(1), col(2), col(3),
                  pl.BlockSpec(lower_bounds.shape, lambda i: (0, 0)),
                  pl.BlockSpec(gamma.shape, lambda i: (0, 0)),
                  rev, rev,
                  pl.BlockSpec((nc, nh, HEAD_DIM, HEAD_DIM), lambda i: (nt - 1 - i, 0, 0, 0)), ANY],
        out_specs=[pl.BlockSpec((TL, 4 * H), lambda i: (nt - 1 - i, 0)), pl.BlockSpec((8, H), lambda i: (0, 0))],
        out_shape=[_sds((L, 4 * H), BF16), _sds((8, H), F32)],
        scratch_shapes=[pltpu.VMEM((nh, HEAD_DIM, HEAD_DIM), F32),
                        pltpu.VMEM((nc, HEAD_DIM, HEAD_DIM), F32),
                        pltpu.VMEM((4, TL, HEAD_DIM), F32)],
        compiler_params=_params(("arbitrary",), 48),
    )(proj, proj, proj, proj, lower_bounds, gamma, o_pre, d_out, s_saved, after)


def _shift_down(u, s, row):
    return jnp.where(row >= s, pltpu.roll(u, s, 0), 0.0)


def _shift_up(u, s, row):
    n = u.shape[0]
    return jnp.where(row < n - s, pltpu.roll(u, n - s, 0), 0.0)


def _conv_specs(L, H):
    per = H // LANES
    return [pl.BlockSpec((L, LANES), lambda j, o=o: (0, o * per + j)) for o in (4, 5, 6)]


def _conv_fwd(proj, conv_w, H, after):
    L = proj.shape[0]

    def body(c_ref, b_ref, x_ref, w_ref, after_ref, o_ref):
        row = lax.broadcasted_iota(jnp.int32, (L, LANES), 0)
        u = c_ref[...] * x_ref[...]
        w = w_ref[...]
        y = w[0:1] * _shift_down(u, 2, row) + w[1:2] * _shift_down(u, 1, row) + w[2:3] * u
        o_ref[...] = (b_ref[...] * y).astype(BF16)

    return _pallas_call(
        body, name="conv_fwd", grid=(H // LANES,),
        in_specs=_conv_specs(L, H) + [pl.BlockSpec((3, LANES), lambda j: (0, j)), ANY],
        out_specs=pl.BlockSpec((L, LANES), lambda j: (0, j)),
        out_shape=_sds((L, H), BF16),
        compiler_params=_params(("parallel",), 48),
    )(proj, proj, proj, conv_w, after)


def _conv_bwd(proj, conv_w, dcb, H, after):
    L = proj.shape[0]

    def body(c_ref, b_ref, x_ref, w_ref, d_ref, after_ref, dc_ref, db_ref, dx_ref, dw_ref):
        row = lax.broadcasted_iota(jnp.int32, (L, LANES), 0)
        cg, xb = c_ref[...], x_ref[...]
        u = cg * xb
        u1, u2 = _shift_down(u, 1, row), _shift_down(u, 2, row)
        w = w_ref[...]
        y = w[0:1] * u2 + w[1:2] * u1 + w[2:3] * u
        d = d_ref[...]
        db_ref[...] = (d * y).astype(BF16)
        dy = d * b_ref[...]
        du = w[2:3] * dy + w[1:2] * _shift_up(dy, 1, row) + w[0:1] * _shift_up(dy, 2, row)
        dw_ref[0:1, :] = jnp.sum(dy * u2, axis=0, keepdims=True)
        dw_ref[1:2, :] = jnp.sum(dy * u1, axis=0, keepdims=True)
        dw_ref[2:3, :] = jnp.sum(dy * u, axis=0, keepdims=True)
        dc_ref[...] = (du * xb).astype(BF16)
        dx_ref[...] = (du * cg).astype(BF16)

    blk = pl.BlockSpec((L, LANES), lambda j: (0, j))
    return _pallas_call(
        body, name="conv_bwd", grid=(H // LANES,),
        in_specs=_conv_specs(L, H) + [pl.BlockSpec((3, LANES), lambda j: (0, j)), blk, ANY],
        out_specs=[blk, blk, blk, pl.BlockSpec((3, LANES), lambda j: (0, j))],
        out_shape=[_sds((L, H), BF16)] * 3 + [_sds((3, H), F32)],
        compiler_params=_params(("parallel",), 56),
    )(proj, proj, proj, conv_w, dcb, after)


def _gate_specs(tm, H):
    return [pl.BlockSpec((tm, H), lambda i, k=k: (i, k)) for k in (7, 8, 9, 10)]


def _fwd_mix(og, cb, proj, x, wat, wbt, wout, g_ffn, H, after):
    L, D = x.shape
    tm = min(L, 512)

    def body(o_ref, cb_ref, ga0, ga1, gb0, gb1, x_ref, wa_ref, wb_ref, wo_ref, g_ref, after_ref,
             sa_ref, sb_ref, ta_ref, tb_ref, m_ref, x1_ref, h2_ref):
        ya, yb = _nt(o_ref[...], wa_ref[...]), _nt(cb_ref[...], wb_ref[...])
        for k, (gar, gbr) in enumerate(((ga0, gb0), (ga1, gb1))):
            cs = slice(k * H, (k + 1) * H)
            sa, sb = _sigmoid(gar[...]), _sigmoid(gbr[...])
            ma, mb = sa * ya[:, cs], sb * yb[:, cs]
            m_ref[:, cs] = (ma + mb).astype(BF16)
            sa_ref[:, cs] = sa.astype(BF16)
            sb_ref[:, cs] = sb.astype(BF16)
            ta_ref[:, cs] = (ma * (1.0 - sa)).astype(BF16)
            tb_ref[:, cs] = (mb * (1.0 - sb)).astype(BF16)
        x1 = x_ref[...] + _nn(m_ref[...], wo_ref[...])
        x1_ref[...] = x1
        _, xh = _rms_stats(x1)
        h2_ref[...] = (xh * g_ref[...]).astype(BF16)

    row = lambda w: pl.BlockSpec((tm, w), lambda i: (i, 0))
    full = lambda a: pl.BlockSpec(a.shape, lambda i: (0,) * a.ndim)
    return _pallas_call(
        body, name="fwd_mix", grid=(L // tm,),
        in_specs=[row(H), row(H)] + _gate_specs(tm, H) + [row(D), full(wat), full(wbt), full(wout),
                                                           full(g_ffn), ANY],
        out_specs=[row(D)] * 7,
        out_shape=[_sds((L, D), BF16)] * 5 + [_sds((L, D), F32), _sds((L, D), BF16)],
        compiler_params=_params(("parallel",), 56),
    )(og, cb, proj, proj, proj, proj, x, wat, wbt, wout, g_ffn, after)


def _bwd_mix(dx1b, sig_a, sig_b, dm_dga, dm_dgb, wat, wbt, wout, H, after):
    L, D = dx1b.shape
    tm = min(L, 512)

    def body(dx_ref, sa_ref, sb_ref, ta_ref, tb_ref, wa_ref, wb_ref, wo_ref, after_ref,
             dya_ref, dyb_ref, dgate_ref, do_ref, dcb_ref):
        dm = _nt(dx_ref[...], wo_ref[...])
        dgate_ref[:, 0:D] = (dm * ta_ref[...].astype(F32)).astype(BF16)
        dgate_ref[:, D:2 * D] = (dm * tb_ref[...].astype(F32)).astype(BF16)
        dya_ref[...] = (dm * sa_ref[...].astype(F32)).astype(BF16)
        dyb_ref[...] = (dm * sb_ref[...].astype(F32)).astype(BF16)
        do_ref[...] = _nn(dya_ref[...], wa_ref[...])
        dcb_ref[...] = _nn(dyb_ref[...], wb_ref[...])

    row = lambda w: pl.BlockSpec((tm, w), lambda i: (i, 0))
    full = lambda a: pl.BlockSpec(a.shape, lambda i: (0,) * a.ndim)
    return _pallas_call(
        body, name="bwd_mix", grid=(L // tm,),
        in_specs=[row(D)] * 5 + [full(wat), full(wbt), full(wout), ANY],
        out_specs=[row(D), row(D), row(2 * D), row(H), row(H)],
        out_shape=[_sds((L, D), BF16)] * 2 + [_sds((L, 2 * D), BF16)] + [_sds((L, H), F32)] * 2,
        compiler_params=_params(("parallel",), 56),
    )(dx1b, sig_a, sig_b, dm_dga, dm_dgb, wat, wbt, wout, after)


def _fwd_ffn_up(h2, wgt, wut):
    L, D = h2.shape
    F = wgt.shape[0]
    tn = F // 2
    tm = min(L, 512)

    def body(h_ref, wg_ref, wu_ref, sa_ref, sb_ref, s_ref):
        h = h_ref[...]
        a, b = _nt(h, wg_ref[...]), _nt(h, wu_ref[...])
        sg = _sigmoid(a)
        silu = a * sg
        sa_ref[...] = (b * sg * (1.0 + a * (1.0 - sg))).astype(BF16)
        sb_ref[...] = silu.astype(BF16)
        s_ref[...] = (silu * b).astype(BF16)

    wspec = pl.BlockSpec((tn, D), lambda j, i: (j, 0))
    ospec = pl.BlockSpec((tm, tn), lambda j, i: (i, j))
    return _pallas_call(
        body, name="fwd_ffn_up", grid=(2, L // tm),
        in_specs=[pl.BlockSpec((tm, D), lambda j, i: (i, 0)), wspec, wspec],
        out_specs=[ospec] * 3,
        out_shape=[_sds((L, F), BF16)] * 3,
        compiler_params=_params(("parallel", "parallel"), 48),
    )(h2, wgt, wut)


def _fwd_down_loss(s, wd, x1, target, g_final):
    L, D = x1.shape
    F = wd.shape[0]
    tm = min(L, 512)

    def body(s_ref, wd_ref, x1_ref, t_ref, g_ref, dx_ref, dxb_ref, red_ref):
        @pl.when(pl.program_id(0) == 0)
        def _():
            red_ref[...] = jnp.zeros_like(red_ref)

        g = g_ref[...]
        r, xh = _rms_stats(x1_ref[...] + _nn(s_ref[...], wd_ref[...]))
        e = xh * g - t_ref[...]
        dy = e * (1.0 / D)
        dx = _rms_bwd(dy * g, xh, r)
        dx_ref[...] = dx
        dxb_ref[...] = dx.astype(BF16)
        red_ref[0:1, :] += jnp.sum(dy * xh, axis=0, keepdims=True)
        red_ref[1:2, :] += jnp.broadcast_to(0.5 * jnp.sum(e * e) * (1.0 / D), (1, D))

    row = pl.BlockSpec((tm, D), lambda i: (i, 0))
    return _pallas_call(
        body, name="fwd_down_loss", grid=(L // tm,),
        in_specs=[pl.BlockSpec((tm, F), lambda i: (i, 0)), pl.BlockSpec((F, D), lambda i: (0, 0)),
                  row, row, pl.BlockSpec((1, D), lambda i: (0, 0))],
        out_specs=[row, row, pl.BlockSpec((8, D), lambda i: (0, 0))],
        out_shape=[_sds((L, D), F32), _sds((L, D), BF16), _sds((8, D), F32)],
        compiler_params=_params(("arbitrary",), 56),
    )(s, wd, x1, target, g_final)


def _bwd_down(dx2b, wd, s_a, s_b):
    L, D = dx2b.shape
    F = wd.shape[0]
    tn = F // 2
    tm = min(L, 512)

    def body(dx_ref, wd_ref, sa_ref, sb_ref, da_ref, db_ref):
        ds = _nt(dx_ref[...], wd_ref[...])
        da_ref[...] = (ds * sa_ref[...].astype(F32)).astype(BF16)
        db_ref[...] = (ds * sb_ref[...].astype(F32)).astype(BF16)

    ospec = pl.BlockSpec((tm, tn), lambda j, i: (i, j))
    return _pallas_call(
        body, name="bwd_down", grid=(2, L // tm),
        in_specs=[pl.BlockSpec((tm, D), lambda j, i: (i, 0)),
                  pl.BlockSpec((tn, D), lambda j, i: (j, 0)), ospec, ospec],
        out_specs=[ospec] * 2,
        out_shape=[_sds((L, F), BF16)] * 2,
        compiler_params=_params(("parallel", "parallel"), 48),
    )(dx2b, wd, s_a, s_b)


def _bwd_ffn_dh(da, db, wgt, wut, x1, dx2, g_ffn, after):
    L, D = x1.shape
    F = wgt.shape[0]
    tm = min(L, 256)

    def body(da_ref, db_ref, wg_ref, wu_ref, x1_ref, dx2_ref, g_ref, after_ref, dx_ref, dxb_ref, red_ref):
        @pl.when(pl.program_id(0) == 0)
        def _():
            red_ref[...] = jnp.zeros_like(red_ref)

        dh = _nn(da_ref[...], wg_ref[...]) + _nn(db_ref[...], wu_ref[...])
        r, xh = _rms_stats(x1_ref[...])
        red_ref[0:1, :] += jnp.sum(dh * xh, axis=0, keepdims=True)
        dx = dx2_ref[...] + _rms_bwd(dh * g_ref[...], xh, r)
        dx_ref[...] = dx
        dxb_ref[...] = dx.astype(BF16)

    row = pl.BlockSpec((tm, D), lambda i: (i, 0))
    aspec = pl.BlockSpec((tm, F), lambda i: (i, 0))
    wspec = pl.BlockSpec((F, D), lambda i: (0, 0))
    return _pallas_call(
        body, name="bwd_ffn_dh", grid=(L // tm,),
        in_specs=[aspec, aspec, wspec, wspec, row, row, pl.BlockSpec((1, D), lambda i: (0, 0)), ANY],
        out_specs=[row, row, pl.BlockSpec((8, D), lambda i: (0, 0))],
        out_shape=[_sds((L, D), F32), _sds((L, D), BF16), _sds((8, D), F32)],
        compiler_params=_params(("arbitrary",), 56),
    )(da, db, wgt, wut, x1, dx2, g_ffn, after)


def _piece_offsets(pieces):
    offsets, total = [], 0
    for p in pieces:
        offsets.append(total)
        total += p.shape[1]
    return offsets, total


def _bwd_in(pieces, w_int, x, dx1, g_mix, after):
    L, D = x.shape
    N = w_int.shape[0]
    tm = min(L, 256)
    n = len(pieces)
    offsets, total = _piece_offsets(pieces)
    assert total == N

    def body(*refs):
        piece_refs = refs[:n]
        w_ref, x_ref, dx1_ref, g_ref, after_ref, dx_ref, red_ref, dp_ref = refs[n:]

        @pl.when(pl.program_id(0) == 0)
        def _():
            red_ref[...] = jnp.zeros_like(red_ref)

        for p_ref, off in zip(piece_refs, offsets):
            dp_ref[:, off:off + p_ref.shape[1]] = p_ref[...]
        dh = _nn(dp_ref[...], w_ref[...])
        r, xh = _rms_stats(x_ref[...])
        red_ref[0:1, :] += jnp.sum(dh * xh, axis=0, keepdims=True)
        dx_ref[...] = dx1_ref[...] + _rms_bwd(dh * g_ref[...], xh, r)

    row = pl.BlockSpec((tm, D), lambda i: (i, 0))
    return _pallas_call(
        body, name="bwd_in", grid=(L // tm,),
        in_specs=[pl.BlockSpec((tm, p.shape[1]), lambda i: (i, 0)) for p in pieces]
        + [pl.BlockSpec((N, D), lambda i: (0, 0)), row, row, pl.BlockSpec((1, D), lambda i: (0, 0)), ANY],
        out_specs=[row, pl.BlockSpec((8, D), lambda i: (0, 0))],
        out_shape=[_sds((L, D), F32), _sds((8, D), F32)],
        scratch_shapes=[pltpu.VMEM((tm, N), BF16)],
        compiler_params=_params(("arbitrary",), 56),
    )(*pieces, w_int, x, dx1, g_mix, after)


def _dw_in(h, pieces, n_cols):
    L, D = h.shape
    tk = min(L, TK_TOKENS // 2)
    n = len(pieces)
    offsets, total = _piece_offsets(pieces)
    assert total == N_CHIPS * n_cols
    plan = []
    for j in range(N_CHIPS):
        lo, hi = j * n_cols, (j + 1) * n_cols
        segments = []
        for p, off in enumerate(offsets):
            a, b = max(lo, off), min(hi, off + pieces[p].shape[1])
            if a < b:
                segments.append((p, a - off, b - a, a - lo))
        plan.append(segments)

    def body(*refs):
        h_ref, piece_refs, o_ref, b_ref = refs[0], refs[1:1 + n], refs[1 + n], refs[2 + n]
        j, k = pl.program_id(0), pl.program_id(1)
        for jj in range(N_CHIPS):
            @pl.when(j == jj)
            def _(jj=jj):
                for p, start, width, at in plan[jj]:
                    b_ref[:, at:at + width] = piece_refs[p][:, start:start + width]

        part = _tn(h_ref[...], b_ref[...])

        @pl.when(k == 0)
        def _():
            o_ref[...] = part

        @pl.when(k > 0)
        def _():
            o_ref[...] += part

    def piece_spec(p):
        used = [j for j in range(N_CHIPS) if any(seg[0] == p for seg in plan[j])]

        def index(j, k):
            in_use = functools.reduce(jnp.logical_or, [j == u for u in used])
            return (jnp.where(in_use, k, 0), 0)

        return pl.BlockSpec((tk, pieces[p].shape[1]), index)

    return _pallas_call(
        body, name="dw_in", grid=(N_CHIPS, L // tk),
        in_specs=[pl.BlockSpec((tk, D), lambda j, k: (k, 0))] + [piece_spec(p) for p in range(n)],
        out_specs=pl.BlockSpec((None, D, n_cols), lambda j, k: (j, 0, 0)),
        out_shape=_sds((N_CHIPS, D, n_cols), F32),
        scratch_shapes=[pltpu.VMEM((tk, n_cols), BF16)],
        compiler_params=_params(("parallel", "arbitrary"), 56),
    )(h, *pieces)


def _mm_tn(name, a, b, a_spec, b_spec, o_block, n_out, n_k):
    def body(a_ref, b_ref, o_ref):
        part = _tn(a_ref[...], b_ref[...])

        @pl.when(pl.program_id(1) == 0)
        def _():
            o_ref[...] = part

        @pl.when(pl.program_id(1) > 0)
        def _():
            o_ref[...] += part

    return _pallas_call(
        body, name=name, grid=(n_out, n_k),
        in_specs=[a_spec, b_spec],
        out_specs=pl.BlockSpec((None,) + o_block, lambda j, k: (j, 0, 0)),
        out_shape=_sds((n_out,) + o_block, F32),
        compiler_params=_params(("parallel", "arbitrary"), 56),
    )(a, b)


TK_TOKENS = 2048


def _dw_cols(name, a, b, n_cols):
    L, M = a.shape
    tk = min(L, TK_TOKENS)
    return _mm_tn(name, a, b, pl.BlockSpec((tk, M), lambda j, k: (k, 0)),
                  pl.BlockSpec((tk, n_cols), lambda j, k: (k, j)), (M, n_cols), N_CHIPS, L // tk)


def _dw_rows(name, a, b):
    L, M = a.shape
    N = b.shape[1]
    tk = min(L, TK_TOKENS)
    return _mm_tn(name, a, b, pl.BlockSpec((tk, M // N_CHIPS), lambda j, k: (k, j)),
                  pl.BlockSpec((tk, N), lambda j, k: (k, 0)), (M // N_CHIPS, N), N_CHIPS, L // tk)


def _dw_rows2(name, a, b):
    L, M = a.shape
    N = b.shape[1]
    tk = min(L, TK_TOKENS)
    return _mm_tn(name, a, b, pl.BlockSpec((tk, M // 2), lambda j, k: (k, j)),
                  pl.BlockSpec((tk, N), lambda j, k: (k, 0)), (M // 2, N), 2, L // tk)


def _place():
    x, y, c = lax.axis_index("x"), lax.axis_index("y"), lax.axis_index("c")
    chips = [(1 - x, y), (x, 1 - y), (1 - x, 1 - y)]
    return x, y, c, 2 * x + y, chips


def _remote(src, dst, send_sem, recv_sem, device):
    return pltpu.make_async_remote_copy(src_ref=src, dst_ref=dst, send_sem=send_sem,
                                        recv_sem=recv_sem, device_id=device, device_id_type=MESH)


def _half(ref, lead, c, r2):
    return ref.at[lead, pl.ds(pl.multiple_of(c * r2, 16), r2), :]


def _cast_place(name, ws, chip_idx):
    n = len(ws)

    def body(k_ref, *refs):
        for w_ref, o_ref in zip(refs[:n], refs[n:]):
            o_ref[...] = w_ref[...].astype(BF16)

    return _pallas_call(
        body, name=name,
        grid_spec=pltpu.PrefetchScalarGridSpec(
            num_scalar_prefetch=1, grid=(2,),
            in_specs=[pl.BlockSpec((w.shape[0] // 2, w.shape[1]), lambda i, k_ref: (i, 0)) for w in ws],
            out_specs=[pl.BlockSpec((None, w.shape[0] // 2, w.shape[1]), lambda i, k_ref: (k_ref[0], i, 0))
                       for w in ws]),
        out_shape=[_sds((N_CHIPS,) + w.shape, BF16) for w in ws],
        compiler_params=_params(("parallel",), 48),
    )(chip_idx, *ws)


def _cast_place_t(name, ws, chip_idx):
    n = len(ws)
    r, cols = ws[0].shape

    def body(k_ref, *refs):
        for w_ref, o_ref in zip(refs[:n], refs[n:]):
            o_ref[...] = w_ref[...].T.astype(BF16)

    return _pallas_call(
        body, name=name,
        grid_spec=pltpu.PrefetchScalarGridSpec(
            num_scalar_prefetch=1, grid=(cols // LANES,),
            in_specs=[pl.BlockSpec((r, LANES), lambda i, k_ref: (0, i))] * n,
            out_specs=[pl.BlockSpec((None, LANES, r), lambda i, k_ref: (k_ref[0], i, 0))] * n),
        out_shape=[_sds((N_CHIPS, cols, r), BF16)] * n,
        compiler_params=_params(("parallel",), 48),
    )(chip_idx, *ws)


def _gather_copies(bufs, whole, send_sems, recv_sems, select=None):
    x, y, c, k, chips = _place()
    pairs = []
    for w, buf in enumerate(bufs):
        for j, (cx, cy) in enumerate(chips):
            if select is not None and not select(w, j):
                continue
            if w in whole:
                mine, theirs = buf.at[k], buf.at[2 * cx + cy]
            else:
                r2 = buf.shape[1] // 2
                mine, theirs = _half(buf, k, c, r2), _half(buf, 2 * cx + cy, c, r2)
            sems = (send_sems.at[w * 3 + j], recv_sems.at[w * 3 + j])
            pairs.append((_remote(mine, mine, *sems, (cx, cy, c)), _remote(theirs, theirs, *sems, (x, y, c))))
    return pairs


def _gather_start(name, groups, after):
    flat = [b for bufs, _, _ in groups for b in bufs]
    nb, ng = len(flat), len(groups)

    def body(*refs):
        ins, sems, token = refs[:nb], refs[nb + 1:nb + 1 + 2 * ng], refs[-1]
        pos = 0
        for g, (bufs, whole, select) in enumerate(groups):
            for send, _ in _gather_copies(ins[pos:pos + len(bufs)], whole, sems[2 * g], sems[2 * g + 1], select):
                send.start()
            pos += len(bufs)
        token[...] = jnp.zeros_like(token)

    sem_shapes = []
    for bufs, _, _ in groups:
        sem_shapes += [pltpu.SemaphoreType.DMA((3 * len(bufs),))] * 2
    out = _pallas_call(
        body, name=name,
        in_specs=[HBM] * nb + [ANY], out_specs=tuple([SEM] * (2 * ng) + [HBM] * nb + [VMEM]),
        out_shape=tuple(sem_shapes + [pltpu.HBM(b.shape, b.dtype) for b in flat] + [_sds((8, LANES), F32)]),
        input_output_aliases={i: 2 * ng + i for i in range(nb)},
        compiler_params=pltpu.CompilerParams(has_side_effects=EFFECT),
    )(*flat, after)
    sems, thru, pos = [], [], 2 * ng
    for g, (bufs, _, _) in enumerate(groups):
        sems.append((out[2 * g], out[2 * g + 1]))
        thru.append(list(out[pos:pos + len(bufs)]))
        pos += len(bufs)
    return sems, thru, out[-1]


def _gather_wait(name, bufs, whole, sems, after, select=None):
    nb = len(bufs)

    def body(*refs):
        ins, send_sems, recv_sems = refs[:nb], refs[nb], refs[nb + 1]
        for send, arrival in _gather_copies(ins, whole, send_sems, recv_sems, select):
            send.wait_send()
            arrival.wait_recv()

    return _pallas_call(
        body, name=name,
        in_specs=[HBM] * nb + [SEM, SEM, ANY], out_specs=[HBM] * nb,
        out_shape=[pltpu.HBM(b.shape, b.dtype) for b in bufs],
        input_output_aliases={i: i for i in range(nb)},
        compiler_params=pltpu.CompilerParams(has_side_effects=EFFECT),
    )(*bufs, sems[0], sems[1], after)


def _gather_forward(name, bufs, sources=(0, 1, 2)):
    n = len(bufs)

    def body(*refs):
        outs = refs[n:2 * n]
        send_sems, recv_sems = refs[2 * n:]
        x, y, c, _, chips = _place()
        sends = []
        for w in range(n):
            r2 = outs[w].shape[1] // 2
            for j in sources:
                landed = _half(outs[w], 2 * chips[j][0] + chips[j][1], c, r2)
                sends.append(_remote(landed, landed, send_sems.at[w * 3 + j], recv_sems.at[w * 3 + j],
                                     (x, y, 1 - c)))
        for cp in sends:
            cp.start()
        for w in range(n):
            r2 = outs[w].shape[1] // 2
            for j in sources:
                got = _half(outs[w], 2 * chips[j][0] + chips[j][1], 1 - c, r2)
                _remote(got, got, send_sems.at[w * 3 + j], recv_sems.at[w * 3 + j], (x, y, c)).wait_recv()
        for cp in sends:
            cp.wait_send()

    return _pallas_call(
        body, name=name,
        in_specs=[ANY] * n, out_specs=[ANY] * n,
        out_shape=[_sds(b.shape, b.dtype) for b in bufs],
        input_output_aliases={i: i for i in range(n)},
        scratch_shapes=[pltpu.SemaphoreType.DMA((n * 3,)), pltpu.SemaphoreType.DMA((n * 3,))],
    )(*bufs)


def _rs_add(name, grads3, from_sibling, c_idx):
    n = len(grads3)

    def body(c_ref, *refs):
        for g_ref, s_ref, o_ref in zip(refs[:n], refs[n:2 * n], refs[2 * n:]):
            o_ref[...] = (g_ref[...] + s_ref[...]).astype(BF16)

    mine = [pl.BlockSpec((None,) + s.shape[1:], lambda k, c_ref: (k, c_ref[0], 0)) for s in from_sibling]
    whole = [pl.BlockSpec((None,) + s.shape[1:], lambda k, c_ref: (k, 0, 0)) for s in from_sibling]
    return _pallas_call(
        body, name=name,
        grid_spec=pltpu.PrefetchScalarGridSpec(num_scalar_prefetch=1, grid=(N_CHIPS,), in_specs=mine + whole,
                                               out_specs=whole),
        out_shape=[_sds(s.shape, BF16) for s in from_sibling],
        compiler_params=_params(("parallel",), 48),
    )(c_idx, *grads3, *from_sibling)


def _split_start(name, arrays, n_sems, pairs_fn):
    n = len(arrays)

    def body(*refs):
        for send, _ in pairs_fn(refs[:n], refs[n], refs[n + 1]):
            send.start()
        refs[-1][...] = jnp.zeros_like(refs[-1])

    out = _pallas_call(
        body, name=name,
        in_specs=[HBM] * n, out_specs=tuple([SEM, SEM] + [HBM] * n + [VMEM]),
        out_shape=tuple([pltpu.SemaphoreType.DMA((n_sems,))] * 2 + [pltpu.HBM(a.shape, a.dtype) for a in arrays]
                        + [_sds((8, LANES), F32)]),
        input_output_aliases={i: 2 + i for i in range(n)},
        compiler_params=pltpu.CompilerParams(has_side_effects=EFFECT),
    )(*arrays)
    return (out[0], out[1]), list(out[2:2 + n]), out[-1]


def _split_wait(name, sems, arrays, pairs_fn, after):
    n = len(arrays)

    def body(*refs):
        for send, arrival in pairs_fn(refs[:n], refs[n], refs[n + 1]):
            send.wait_send()
            arrival.wait_recv()

    return list(_pallas_call(
        body, name=name,
        in_specs=[HBM] * n + [SEM, SEM, ANY], out_specs=[HBM] * n,
        out_shape=[pltpu.HBM(a.shape, a.dtype) for a in arrays],
        input_output_aliases={i: i for i in range(n)},
        compiler_params=pltpu.CompilerParams(has_side_effects=EFFECT),
    )(*arrays, sems[0], sems[1], after))


def _forward_pairs(bufs, send_sems, recv_sems):
    x, y, c, _, chips = _place()
    pairs = []
    for w, buf in enumerate(bufs):
        r2 = buf.shape[1] // 2
        for j, (cx, cy) in enumerate(chips):
            landed, theirs = _half(buf, 2 * cx + cy, c, r2), _half(buf, 2 * cx + cy, 1 - c, r2)
            sems = (send_sems.at[w * 3 + j], recv_sems.at[w * 3 + j])
            pairs.append((_remote(landed, landed, *sems, (x, y, 1 - c)), _remote(theirs, theirs, *sems, (x, y, c))))
    return pairs


def _sibling_pairs(arrays, send_sems, recv_sems):
    x, y, c, _, _ = _place()
    n = len(arrays) // 2
    pairs = []
    for w in range(n):
        r2 = arrays[w].shape[1] // 2
        cp = _remote(_half(arrays[w], slice(None), 1 - c, r2), arrays[n + w], send_sems.at[w], recv_sems.at[w],
                     (x, y, 1 - c))
        pairs.append((cp, cp))
    return pairs


def _ici_pairs(arrays, send_sems, recv_sems):
    x, y, c, _, chips = _place()
    n = len(arrays) // 2
    pairs = []
    for w in range(n):
        for j, (cx, cy) in enumerate(chips):
            cp = _remote(arrays[w].at[2 * cx + cy], arrays[n + w].at[j],
                         send_sems.at[w * 3 + j], recv_sems.at[w * 3 + j], (cx, cy, c))
            pairs.append((cp, cp))
    return pairs


def _rs_sum(name, partials, received, place_idx):
    n = len(partials)
    nb = 2
    blocks = [(p.shape[1] // nb, p.shape[2]) for p in partials]

    def body(idx_ref, *refs):
        for p_ref, r_ref, o_ref in zip(refs[:n], refs[n:2 * n], refs[2 * n:]):
            o_ref[...] = ((p_ref[...].astype(F32) + r_ref[0].astype(F32))
                          + (r_ref[1].astype(F32) + r_ref[2].astype(F32)))

    return _pallas_call(
        body, name=name,
        grid_spec=pltpu.PrefetchScalarGridSpec(
            num_scalar_prefetch=1, grid=(nb,),
            in_specs=[pl.BlockSpec((None,) + b, lambda i, idx: (idx[0], i, 0)) for b in blocks]
            + [pl.BlockSpec((3,) + b, lambda i, idx: (0, i, 0)) for b in blocks],
            out_specs=[pl.BlockSpec(b, lambda i, idx: (idx[1] * nb + i, 0)) for b in blocks]),
        out_shape=[_sds((2 * p.shape[1], p.shape[2]), F32) for p in partials],
        compiler_params=_params(("parallel",), 48),
    )(place_idx, *partials, *received)


def _share_pairs(arrays, send_sems, recv_sems):
    x, y, c, _, _ = _place()
    pairs = []
    for w, arr in enumerate(arrays):
        r2 = arr.shape[0] // 2
        mine = arr.at[pl.ds(pl.multiple_of(c * r2, 8), r2), :]
        theirs = arr.at[pl.ds(pl.multiple_of((1 - c) * r2, 8), r2), :]
        sems = (send_sems.at[w], recv_sems.at[w])
        pairs.append((_remote(mine, mine, *sems, (x, y, 1 - c)), _remote(theirs, theirs, *sems, (x, y, c))))
    return pairs


def _small_pack(red_mix, red_ffn, red_final, red_hg, g_conv):
    D = red_mix.shape[1]
    H = red_hg.shape[1]

    def body(mix_ref, ffn_ref, fin_ref, hg_ref, cv_ref, in_ref):
        in_ref[...] = jnp.zeros_like(in_ref)
        in_ref[0:1, :] = mix_ref[0:1, :]
        in_ref[1:2, :] = ffn_ref[0:1, :]
        in_ref[2:3, :] = fin_ref[0:1, :]
        gam = hg_ref[1:2, 0:HEAD_DIM]
        for h in range(1, H // HEAD_DIM):
            gam = gam + hg_ref[1:2, h * HEAD_DIM:(h + 1) * HEAD_DIM]
        in_ref[3:4, 0:HEAD_DIM] = gam
        in_ref[3:4, HEAD_DIM:2 * HEAD_DIM] = fin_ref[1:2, 0:HEAD_DIM]
        in_ref[4:5, 0:H] = hg_ref[0:1, :]
        in_ref[6:9, 0:H] = cv_ref[...]

    return _pallas_call(
        body, name="small_pack", pin=False,
        in_specs=[VMEM] * 5, out_specs=VMEM, out_shape=_sds((N_SMALL_ROWS, D), F32),
    )(red_mix, red_ffn, red_final, red_hg, g_conv)


def _small_pairs(arrays, send_sems, recv_sems):
    block, gathered = arrays
    x, y, c, _, _ = _place()
    me = 4 * x + 2 * y + c
    pairs = []
    for m in range(1, 8):
        px, py, pc = x ^ ((m >> 2) & 1), y ^ ((m >> 1) & 1), c ^ (m & 1)
        sems = (send_sems.at[m - 1], recv_sems.at[m - 1])
        pairs.append((_remote(block, gathered.at[me], *sems, (px, py, pc)),
                      _remote(block, gathered.at[4 * px + 2 * py + pc], *sems, (x, y, c))))
    return pairs


def _adamw_math(w, g, m, v):
    m = ADAM_B1 * m + (1.0 - ADAM_B1) * g
    v = ADAM_B2 * v + (1.0 - ADAM_B2) * jnp.square(g)
    m_hat = m / (1.0 - ADAM_B1 ** ADAM_STEP)
    v_hat = v / (1.0 - ADAM_B2 ** ADAM_STEP)
    delta = -ADAM_LR * (m_hat / (jnp.sqrt(v_hat) + ADAM_EPS) + ADAM_WD * w)
    return delta, m, v


def _adamw(name, gs, ws, ms, vs):
    n = len(gs)
    nb = 4

    def body(*refs):
        ins, outs = refs[:4 * n], refs[4 * n:]
        for j in range(n):
            g_ref, w_ref, m_ref, v_ref = ins[j], ins[n + j], ins[2 * n + j], ins[3 * n + j]
            go_ref, d_ref, mo_ref, vo_ref = outs[4 * j:4 * j + 4]
            g = g_ref[...]
            go_ref[...] = g
            d_ref[...], mo_ref[...], vo_ref[...] = _adamw_math(w_ref[...], g, m_ref[...], v_ref[...])

    blk = [pl.BlockSpec((g.shape[0] // nb, g.shape[1]), lambda i: (i, 0)) for g in gs]
    out = _pallas_call(
        body, name=name, grid=(nb,),
        in_specs=blk * 4, out_specs=[b for b in blk for _ in range(4)],
        out_shape=[_sds(g.shape, F32) for g in gs for _ in range(4)],
        compiler_params=_params(("parallel",), 56),
    )(*gs, *ws, *ms, *vs)
    return [list(out[4 * j:4 * j + 4]) for j in range(n)]


def _small_update(block, gathered, place_idx, ws, ms, vs):
    n = len(ws)
    H = ws[1].shape[1]

    def body(idx_ref, blk_ref, all_ref, *refs):
        w, m, v, outs, tot_ref = refs[:n], refs[n:2 * n], refs[2 * n:3 * n], refs[3 * n:-1], refs[-1]
        chip, me = idx_ref[0], idx_ref[1]
        tot = jnp.where(me == 0, blk_ref[...], all_ref[0])
        for d in range(1, 8):
            tot = tot + jnp.where(me == d, blk_ref[...], all_ref[d])
        tot_ref[...] = tot
        p0 = _lower_bound(w[1][...])
        dl0 = p0 * (1.0 - p0) * tot_ref[4:5, 0:H]
        conv = jnp.zeros((3, LANES), F32)
        for k in range(N_CHIPS):
            conv = jnp.where(chip == k, tot_ref[6:9, k * LANES:(k + 1) * LANES], conv)
        grads = [tot_ref[0:1, :], None, tot_ref[3:4, 0:HEAD_DIM], conv, tot_ref[1:2, :], tot_ref[2:3, :]]
        for p in range(n):
            g_ref, d_ref, mo_ref, vo_ref = outs[4 * p:4 * p + 4]
            if p == 1:
                for row, g in ((slice(0, 1), dl0), (slice(1, 2), -dl0)):
                    g_ref[row, :] = g
                    d_ref[row, :], mo_ref[row, :], vo_ref[row, :] = _adamw_math(
                        w[p][row, :], g, m[p][row, :], v[p][row, :])
            else:
                g_ref[...] = grads[p]
                d_ref[...], mo_ref[...], vo_ref[...] = _adamw_math(w[p][...], grads[p], m[p][...], v[p][...])
        outs[4 * n][...] = tot_ref[3:4, HEAD_DIM:2 * HEAD_DIM]

    full = lambda a: pl.BlockSpec(a.shape, lambda i, idx: (0,) * a.ndim)
    out_shape = [_sds(w.shape, F32) for w in ws for _ in range(4)] + [_sds((1, LANES), F32)]
    return _pallas_call(
        body, name="small_update",
        grid_spec=pltpu.PrefetchScalarGridSpec(
            num_scalar_prefetch=1, grid=(1,),
            in_specs=[full(block), full(gathered)] + [full(a) for a in ws + ms + vs],
            out_specs=[full(s) for s in out_shape],
            scratch_shapes=[pltpu.VMEM(block.shape, F32)]),
        out_shape=out_shape,
    )(place_idx, block, gathered, *ws, *ms, *vs)


def kernel(x, norm_mix_g, w_in, lower_bounds, hg_norm_g, conv_w, w_branch_a, w_branch_b, w_out, norm_ffn_g, w_ffn_gate, w_ffn_up, w_ffn_down, norm_final_g, loss_target, m_norm_mix_g, m_w_in, m_lower_bounds, m_hg_norm_g, m_conv_w, m_w_branch_a, m_w_branch_b, m_w_out, m_norm_ffn_g, m_w_ffn_gate, m_w_ffn_up, m_w_ffn_down, m_norm_final_g, v_norm_mix_g, v_w_in, v_lower_bounds, v_hg_norm_g, v_conv_w, v_w_branch_a, v_w_branch_b, v_w_out, v_norm_ffn_g, v_w_ffn_gate, v_w_ffn_up, v_w_ffn_down, v_norm_final_g):
    _, L, D = x.shape
    H = D // 2
    assert lower_bounds.shape == (2, H) and hg_norm_g.shape == (1, HEAD_DIM)
    assert conv_w.shape == (1, 3, LANES) and w_in.shape[2] * N_CHIPS == 11 * H
    x2d, target = x.reshape(L, D), loss_target.reshape(L, D)
    g_final = norm_final_g.reshape(1, D)
    chip = 2 * lax.axis_index("x") + lax.axis_index("y")
    core = lax.axis_index("c")

    tr = lambda w: jnp.transpose(w[0])
    big = [w_in[0], w_branch_a[0], w_branch_b[0], w_out[0], tr(w_ffn_gate), tr(w_ffn_up), w_ffn_down[0]]
    big_m = [m_w_in[0], m_w_branch_a[0], m_w_branch_b[0], m_w_out[0], tr(m_w_ffn_gate), tr(m_w_ffn_up),
             m_w_ffn_down[0]]
    big_v = [v_w_in[0], v_w_branch_a[0], v_w_branch_b[0], v_w_out[0], tr(v_w_ffn_gate), tr(v_w_ffn_up),
             v_w_ffn_down[0]]
    names = ["w_in", "w_branch_a", "w_branch_b", "w_out", "w_ffn_gate", "w_ffn_up", "w_ffn_down"]

    chip_idx = chip.reshape(1).astype(jnp.int32)
    def per_shape(fn, tag, js, *lists):
        groups = {}
        for pos, a in enumerate(lists[0]):
            groups.setdefault(a.shape, []).append(pos)
        results = [None] * len(js)
        for same in groups.values():
            out = fn(tag + names[js[same[0]]], *[[xs[p] for p in same] for xs in lists])
            for q, p in enumerate(same):
                results[p] = out[q]
        return results

    place_t = lambda name, ws: _cast_place_t(name, ws, chip_idx)
    placed = per_shape(place_t, "place_", [0, 1, 2], big[:3]) + list(_cast_place("place_rest", big[3:], chip_idx))
    conv_placed = lax.dynamic_update_slice(jnp.zeros((N_CHIPS, 3, LANES), F32), conv_w, (chip, 0, 0))
    x_i, y_i = lax.axis_index("x"), lax.axis_index("y")
    blocks = lambda *ks: jnp.stack(ks).astype(jnp.int32)
    near = lambda w, j: j < 2
    far = lambda w, j: w == 1 or j == 2
    near_sems, in_flight, _ = _gather_start("gather_start_near", [([placed[0]], set(), near)], chip_idx)
    w_in_buf = in_flight[0][0]
    h, proj = _fwd_proj_first(x2d, norm_mix_g, w_in_buf, blocks(chip))
    sems, in_flight, _ = _gather_start(
        "gather_start_rest", [([w_in_buf, conv_placed], {1}, far), (placed[1:4], set(), None),
                              (placed[4:], set(), None)], h)
    w_in_buf, conv_buf = in_flight[0]
    (w_in_buf,) = _gather_wait("gather_wait_in_near", [w_in_buf], set(), near_sems[0], h, near)
    (w_in_buf,) = _gather_forward("gather_fwd_in_near", [w_in_buf], (0, 1))
    proj = _fwd_proj_more("fwd_proj_near", h, w_in_buf, proj,
                          blocks(2 * (1 - x_i) + y_i, 2 * x_i + (1 - y_i)))
    w_in_buf, conv_all = _gather_wait("gather_wait_in_far", [w_in_buf, conv_buf], {1}, sems[0], proj, far)
    (w_int3,) = _gather_forward("gather_fwd_in_far", [w_in_buf], (2,))
    proj = _fwd_proj_more("fwd_proj_far", h, w_int3, proj, blocks(2 * (1 - x_i) + (1 - y_i)))
    w_int = w_int3.reshape(-1, D)
    conv_full = jnp.transpose(conv_all, (1, 0, 2)).reshape(3, H)
    og, o_pre, s_saved = _hgrn_fwd(proj, lower_bounds, hg_norm_g, H)
    landed = _gather_wait("gather_wait_mix", in_flight[1], set(), sems[1], og)
    fwd_sems, landed, token = _split_start("gather_fwd_mix_start", landed, 9, _forward_pairs)
    cb = _conv_fwd(proj, conv_full, H, token)
    wat3, wbt3, wout3 = _split_wait("gather_fwd_mix_wait", fwd_sems, landed, _forward_pairs, cb)
    wat, wbt, wout = wat3.reshape(D, H), wbt3.reshape(D, H), wout3.reshape(D, D)
    landed = _gather_wait("gather_wait_ffn", in_flight[2], set(), sems[2], cb)
    fwd_sems, landed, token = _split_start("gather_fwd_ffn_start", landed, 9, _forward_pairs)
    sig_a, sig_b, dm_dga, dm_dgb, merged, x1, h2 = _fwd_mix(og, cb, proj, x2d, wat, wbt, wout, norm_ffn_g,
                                                              H, token)
    wgt3, wut3, wd3 = _split_wait("gather_fwd_ffn_wait", fwd_sems, landed, _forward_pairs, h2)
    d_ff = N_CHIPS * wd3.shape[1]
    wgt, wut, wd = wgt3.reshape(d_ff, D), wut3.reshape(d_ff, D), wd3.reshape(d_ff, D)
    ffn_ds_da, ffn_ds_db, ffn_s = _fwd_ffn_up(h2, wgt, wut)
    dx2, dx2b, red_final = _fwd_down_loss(ffn_s, wd, x1, target, g_final)

    c_idx = core.reshape(1).astype(jnp.int32)
    place_idx = jnp.stack([chip, core]).astype(jnp.int32)

    def sibling_start(tag, grads):
        bufs = [lax.empty((N_CHIPS, g.shape[1] // 2, g.shape[2]), F32) for g in grads]
        return _split_start("rs_sibling_start_" + tag, list(grads) + bufs, len(grads), _sibling_pairs)

    def ici_start(tag, js, grads, from_sibling):
        partials = list(_rs_add("rs_add_" + tag, grads, from_sibling, c_idx))
        landings = [lax.empty((3,) + p.shape[1:], BF16) for p in partials]
        return _split_start("rs_ici_start_" + tag, partials + landings, 3 * len(js), _ici_pairs)

    def ici_start_behind(tag, js, started, after):
        n = len(js)
        arrays = _split_wait("rs_sibling_wait_" + tag, started[0], started[1], _sibling_pairs, after)
        return ici_start(tag, js, arrays[:n], arrays[n:])

    def sums(tag, started, after):
        partials, received = [], []
        for group, group_js, start in started:
            arrays = _split_wait("rs_ici_wait_" + group, start[0], start[1], _ici_pairs, after)
            partials += arrays[:len(group_js)]
            received += arrays[len(group_js):]
        return list(_rs_sum("rs_sum_" + tag, partials, received, place_idx))

    def adamw(tag, js, grads):
        return _adamw("adamw_" + tag, grads, *[[src[j] for j in js] for src in (big, big_m, big_v)])

    shards3 = lambda g: g.reshape(N_CHIPS, d_ff // N_CHIPS, D)
    da, db = _bwd_down(dx2b, wd, ffn_ds_da, ffn_ds_db)
    g_wd = shards3(_dw_rows2("dw_ffn_down", ffn_s, dx2b))
    g_wg = shards3(_dw_rows2("dw_ffn_gate", da, h2))
    g_wu = shards3(_dw_rows2("dw_ffn_up", db, h2))
    ffn_sibling = sibling_start("ffn", [g_wg, g_wu, g_wd])
    dx1, dx1b, red_ffn = _bwd_ffn_dh(da, db, wgt, wut, x1, dx2, norm_ffn_g, ffn_sibling[2])
    ffn_ici = ici_start_behind("ffn", [4, 5, 6], ffn_sibling, dx1b)
    dya, dyb, d_gates, d_o, d_cb = _bwd_mix(dx1b, sig_a, sig_b, dm_dga, dm_dgb, wat, wbt, wout, H, ffn_ici[2])
    g_wout = _dw_rows("dw_out", merged, dx1b)
    g_wa = _dw_cols("dw_branch_a", og, dya, D // N_CHIPS)
    g_wb = _dw_cols("dw_branch_b", cb, dyb, D // N_CHIPS)
    mix_sibling = sibling_start("mix", [g_wa, g_wb, g_wout])
    d_hgrn, red_hg = _hgrn_bwd(proj, lower_bounds, hg_norm_g, o_pre, d_o, s_saved, H, mix_sibling[2])
    mix_ici = ici_start_behind("mix", [1, 2, 3], mix_sibling, d_hgrn)
    dcg, dbg, dxb, g_conv = _conv_bwd(proj, conv_full, d_cb, H, mix_ici[2])
    dproj = [d_hgrn, dcg, dbg, dxb, d_gates]
    g_win = _dw_in(h, dproj, w_int3.shape[1])
    in_sibling = sibling_start("in", [g_win])
    halves = sums("rest", [("mix", [1, 2, 3], mix_ici), ("ffn", [4, 5, 6], ffn_ici)], in_sibling[2])
    rest_share = _split_start("rs_share_start_rest", halves, len(halves), _share_pairs)
    in_ici = ici_start_behind("in", [0], in_sibling, rest_share[2])
    grad_x, red_mix = _bwd_in(dproj, w_int, x2d, dx1, norm_mix_g, in_ici[2])
    in_share = _split_start("rs_share_start_in", sums("in", [("in", [0], in_ici)], grad_x), 1, _share_pairs)
    small_block = _small_pack(red_mix, red_ffn, red_final, red_hg, g_conv)
    small = _split_start("small_gather_start", [small_block, lax.empty((8,) + small_block.shape, F32)], 7,
                         _small_pairs)
    rest_grads = _split_wait("rs_share_wait_rest", rest_share[0], rest_share[1], _share_pairs, small[2])
    big_out = [None] + adamw("rest", [1, 2, 3, 4, 5, 6], rest_grads)
    in_grad = _split_wait("rs_share_wait_in", in_share[0], in_share[1], _share_pairs, big_out[6][0])
    big_out[0] = adamw("in", [0], in_grad)[0]
    small_block, small_all = _split_wait("small_gather_wait", small[0], small[1], _small_pairs, big_out[0][0])

    def smalls(mix, lb, hg, cw, ffn, fin):
        return [mix, lb, hg, cw[0], ffn, fin.reshape(1, D)]

    small_out = _small_update(
        small_block, small_all, jnp.stack([chip, 4 * x_i + 2 * y_i + core]).astype(jnp.int32),
        smalls(norm_mix_g, lower_bounds, hg_norm_g, conv_w, norm_ffn_g, norm_final_g),
        smalls(m_norm_mix_g, m_lower_bounds, m_hg_norm_g, m_conv_w, m_norm_ffn_g, m_norm_final_g),
        smalls(v_norm_mix_g, v_lower_bounds, v_hg_norm_g, v_conv_w, v_norm_ffn_g, v_norm_final_g))

    def outputs(i):
        big_i = [big_out[j][i] for j in range(7)]
        mix, lb, hg, cw, ffn, fin = [small_out[4 * p + i] for p in range(6)]
        return [mix, big_i[0][None], lb, hg, cw[None], big_i[1][None], big_i[2][None], big_i[3][None], ffn,
                big_i[4].T[None], big_i[5].T[None], big_i[6][None], fin.reshape(D)]

    outs = [small_out[24][0, 0], grad_x.reshape(1, L, D)]
    for i in range(4):
        outs += outputs(i)
    return tuple(outs)
```

```python
import functools

import jax
import jax.numpy as jnp
from jax import lax
from jax.experimental import pallas as pl
from jax.experimental.pallas import tpu as pltpu

F32 = jnp.float32
BF16 = jnp.bfloat16
EPS = 1e-6
CHUNK = 32
HEAD_DIM = 128
LANES = 128
N_CHIPS = 4
N_SMALL_ROWS = 16
DPROJ_BLOCKS = 12
DPROJ_BLOCK_OF = (0, 1, 2, 3, 8, 9, 10, 4, 5, 6, 7)

ADAM_LR = 0.001
ADAM_B1 = 0.9
ADAM_B2 = 0.999
ADAM_EPS = 1e-08
ADAM_WD = 0.01
ADAM_STEP = 10

MESH = pl.DeviceIdType.MESH
ANY = pl.BlockSpec(memory_space=pl.ANY)
VMEM = pl.BlockSpec(memory_space=pltpu.VMEM)
HBM = pl.BlockSpec(memory_space=pltpu.HBM)
SEM = pl.BlockSpec(memory_space=pltpu.SEMAPHORE)
EFFECT = pltpu.SideEffectType.DATAFLOW_SIDE_EFFECTING


def _sds(shape, dtype):
    return jax.ShapeDtypeStruct(shape, dtype)


def _pallas_call(body, pin=True, **kwargs):
    if not pin:
        return pl.pallas_call(body, **kwargs)
    in_hbm = lambda s: pltpu.HBM(s.shape, s.dtype) if isinstance(s, jax.ShapeDtypeStruct) else s
    kwargs["out_shape"] = jax.tree.map(in_hbm, kwargs["out_shape"])
    call = pl.pallas_call(body, **kwargs)

    def run(*args):
        return call(*[pltpu.with_memory_space_constraint(a, pltpu.HBM) if a.dtype in (F32, BF16) else a
                      for a in args])

    return run


def _params(semantics, vmem_mb):
    return pltpu.CompilerParams(dimension_semantics=semantics, vmem_limit_bytes=vmem_mb << 20)


def _nn(a, b):
    return lax.dot_general(a, b, (((1,), (0,)), ((), ())), preferred_element_type=F32)


def _nt(a, b):
    return lax.dot_general(a, b, (((1,), (1,)), ((), ())), preferred_element_type=F32)


def _tn(a, b):
    return lax.dot_general(a, b, (((0,), (0,)), ((), ())), preferred_element_type=F32)


def _sigmoid(x):
    return jax.nn.sigmoid(x)


def _rms_stats(x):
    r = lax.rsqrt(jnp.mean(x * x, axis=-1, keepdims=True) + EPS)
    return r, x * r


def _rms_bwd(dxh, xh, r):
    return r * (dxh - xh * jnp.mean(dxh * xh, axis=-1, keepdims=True))


def _fwd_proj_first(x, g_mix, w_int3, block):
    L, D = x.shape
    tn = w_int3.shape[1]
    tm = min(L, 1024)

    def body(blk_ref, x_ref, g_ref, w_ref, h_ref, p_ref):
        _, xh = _rms_stats(x_ref[...])
        h = (xh * g_ref[...]).astype(BF16)
        h_ref[...] = h
        p_ref[...] = _nt(h, w_ref[...])

    return _pallas_call(
        body, name="fwd_proj_own",
        grid_spec=pltpu.PrefetchScalarGridSpec(
            num_scalar_prefetch=1, grid=(L // tm,),
            in_specs=[pl.BlockSpec((tm, D), lambda i, blk: (i, 0)),
                      pl.BlockSpec((1, D), lambda i, blk: (0, 0)),
                      pl.BlockSpec((None, tn, D), lambda i, blk: (blk[0], 0, 0))],
            out_specs=[pl.BlockSpec((tm, D), lambda i, blk: (i, 0)),
                       pl.BlockSpec((tm, tn), lambda i, blk: (i, blk[0]))]),
        out_shape=[_sds((L, D), BF16), _sds((L, N_CHIPS * tn), F32)],
        compiler_params=_params(("parallel",), 48),
    )(block, x, g_mix, w_int3)


def _fwd_proj_more(name, h, w_int3, proj, blocks):
    L, D = h.shape
    tn = w_int3.shape[1]
    tm = min(L, 1024)

    def body(blk_ref, h_ref, w_ref, proj_ref, p_ref):
        p_ref[...] = _nt(h_ref[...], w_ref[...])

    return _pallas_call(
        body, name=name,
        grid_spec=pltpu.PrefetchScalarGridSpec(
            num_scalar_prefetch=1, grid=(L // tm, blocks.shape[0]),
            in_specs=[pl.BlockSpec((tm, D), lambda i, j, blk: (i, 0)),
                      pl.BlockSpec((None, tn, D), lambda i, j, blk: (blk[j], 0, 0)), ANY],
            out_specs=pl.BlockSpec((tm, tn), lambda i, j, blk: (i, blk[j]))),
        out_shape=_sds(proj.shape, proj.dtype),
        input_output_aliases={3: 0},
        compiler_params=_params(("parallel", "arbitrary"), 48),
    )(blocks, h, w_int3, proj)


def _lower_bound(lbp):
    l0, l1 = lbp[0:1, :], lbp[1:2, :]
    m = jnp.maximum(l0, l1)
    e0, e1 = jnp.exp(l0 - m), jnp.exp(l1 - m)
    return e0 / (e0 + e1)


def _seg_scan(x, r32, forward):
    n = x.shape[0]
    s = 1
    while s < CHUNK:
        if forward:
            x = x + jnp.where(r32 >= s, pltpu.roll(x, s, 0), 0.0)
        else:
            x = x + jnp.where(r32 < CHUNK - s, pltpu.roll(x, n - s, 0), 0.0)
        s *= 2
    return x


def _bcast_row(x, row):
    n, w = x.shape
    nc = n // CHUNK
    x3 = x.reshape(nc, CHUNK, w)
    return jnp.broadcast_to(x3[:, row:row + 1, :], (nc, CHUNK, w)).reshape(n, w)


def _chunk_total(x):
    n, w = x.shape
    nc = n // CHUNK
    total = jnp.sum(x.reshape(nc, CHUNK, w), axis=1, keepdims=True)
    return jnp.broadcast_to(total, (nc, CHUNK, w)).reshape(n, w)


def _hgrn_prep(q_raw, f_raw, lb):
    r32 = lax.broadcasted_iota(jnp.int32, f_raw.shape, 0) & (CHUNK - 1)
    sig = _sigmoid(f_raw)
    f = lb + (1.0 - lb) * sig
    b = _seg_scan(jnp.log(f), r32, True)
    a = _bcast_row(b, CHUNK // 2 - 1)
    bl = _bcast_row(b, CHUNK - 1)
    sq = _sigmoid(q_raw)
    q = q_raw * sq * (HEAD_DIM ** -0.5)
    return dict(r32=r32, sig=sig, f=f, k=1.0 - f, b=b, a=a, bl=bl, sq=sq, q=q)


def _chunk_masks(n):
    ri = lax.broadcasted_iota(jnp.int32, (n, n), 0)
    ci = lax.broadcasted_iota(jnp.int32, (n, n), 1)
    same = (ri // CHUNK) == (ci // CHUNK)
    return same & (ci <= ri), same & (ri <= ci)


def _hgrn_fwd(proj, lower_bounds, gamma, H):
    L = proj.shape[0]
    nh = H // HEAD_DIM
    TL = min(L, 256)
    nc = TL // CHUNK

    def body(q_ref, f_ref, v_ref, g_ref, lbp_ref, gam_ref, og_ref, o_ref, s_ref, st_ref):
        @pl.when(pl.program_id(0) == 0)
        def _():
            st_ref[...] = jnp.zeros_like(st_ref)

        lb = _lower_bound(lbp_ref[...])
        gam = gam_ref[...]
        mask, _ = _chunk_masks(TL)
        rowc = lax.broadcasted_iota(jnp.int32, (TL, HEAD_DIM), 0) // CHUNK
        for h in range(nh):
            hs = slice(h * HEAD_DIM, (h + 1) * HEAD_DIM)
            p = _hgrn_prep(q_ref[:, hs], f_ref[:, hs], lb[:, hs])
            v = v_ref[:, hs]
            vb = v.astype(BF16)
            vt = v.T.astype(BF16)
            q_hat = (p["q"] * jnp.exp(p["b"] - p["a"])).astype(BF16)
            k_hat = (p["k"] * jnp.exp(p["a"] - p["b"])).astype(BF16)
            q_in = (p["q"] * jnp.exp(p["b"])).astype(BF16)
            k_out = (p["k"] * jnp.exp(p["bl"] - p["b"])).astype(BF16)
            dec = jnp.exp(p["bl"])
            att = jnp.where(mask, _nt(q_hat, k_hat), 0.0).astype(BF16)
            o_intra = _nn(att, vb)
            st = st_ref[h]
            for c in range(nc):
                rs = slice(c * CHUNK, (c + 1) * CHUNK)
                stb = st.astype(BF16)
                s_ref[c, h] = stb
                o_ref[rs, hs] = o_intra[rs] + _nt(q_in[rs], stb)
                k_c = jnp.where(rowc == c, k_out, jnp.zeros_like(k_out))
                st = st * dec[c * CHUNK:c * CHUNK + 1, :] + _nn(vt, k_c)
            st_ref[h] = st
            o = o_ref[:, hs]
            _, xh = _rms_stats(o)
            gr = g_ref[:, hs]
            og_ref[:, hs] = (xh * gam * (gr * _sigmoid(gr))).astype(BF16)

    col = lambda k: pl.BlockSpec((TL, H), lambda i, k=k: (i, k))
    return _pallas_call(
        body, name="hgrn_fwd", grid=(L // TL,),
        in_specs=[col(0), col(1), col(2), col(3),
                  pl.BlockSpec(lower_bounds.shape, lambda i: (0, 0)),
                  pl.BlockSpec(gamma.shape, lambda i: (0, 0))],
        out_specs=[pl.BlockSpec((TL, H), lambda i: (i, 0)),
                   pl.BlockSpec((TL, H), lambda i: (i, 0)),
                   pl.BlockSpec((nc, nh, HEAD_DIM, HEAD_DIM), lambda i: (i, 0, 0, 0))],
        out_shape=[_sds((L, H), BF16), _sds((L, H), F32),
                   _sds((L // CHUNK, nh, HEAD_DIM, HEAD_DIM), BF16)],
        scratch_shapes=[pltpu.VMEM((nh, HEAD_DIM, HEAD_DIM), F32)],
        compiler_params=_params(("arbitrary",), 48),
    )(proj, proj, proj, proj, lower_bounds, gamma)


def _hgrn_bwd(proj, lower_bounds, gamma, o_pre, d_out, s_saved, H, after, dproj):
    L = proj.shape[0]
    nh = H // HEAD_DIM
    TL = min(L, 256)
    nc = TL // CHUNK
    nt = L // TL

    def body(q_ref, f_ref, v_ref, g_ref, lbp_ref, gam_ref, o_ref, d_ref, s_ref, after_ref, dproj_ref,
             dp_ref, red_ref, dst_ref, dsall_ref, tmp_ref):
        @pl.when(pl.program_id(0) == 0)
        def _():
            dst_ref[...] = jnp.zeros_like(dst_ref)
            red_ref[...] = jnp.zeros_like(red_ref)

        lb = _lower_bound(lbp_ref[...])
        gam = gam_ref[...]
        mask, mask_t = _chunk_masks(TL)
        rowc = lax.broadcasted_iota(jnp.int32, (TL, HEAD_DIM), 0) // CHUNK
        for h in range(nh):
            hs = slice(h * HEAD_DIM, (h + 1) * HEAD_DIM)
            qr, gr, lbh = q_ref[:, hs], g_ref[:, hs], lb[:, hs]
            p = _hgrn_prep(qr, f_ref[:, hs], lbh)
            vb = v_ref[:, hs].astype(BF16)
            eba, eab = jnp.exp(p["b"] - p["a"]), jnp.exp(p["a"] - p["b"])
            eb, elb = jnp.exp(p["b"]), jnp.exp(p["bl"] - p["b"])
            dec = jnp.exp(p["bl"])
            q_hat, k_hat = p["q"] * eba, p["k"] * eab
            q_in, k_out = p["q"] * eb, p["k"] * elb
            q_hat_b, k_hat_b = q_hat.astype(BF16), k_hat.astype(BF16)
            q_in_b, k_out_b = q_in.astype(BF16), k_out.astype(BF16)

            o, dout = o_ref[:, hs], d_ref[:, hs]
            sg = _sigmoid(gr)
            r, xh = _rms_stats(o)
            dp_ref[3, :, hs] = (dout * (xh * gam) * (sg * (1.0 + gr * (1.0 - sg)))).astype(BF16)
            dn = dout * (gr * sg)
            red_ref[1:2, hs] += jnp.sum(dn * xh, axis=0, keepdims=True)
            do = _rms_bwd(dn * gam, xh, r)
            dob = do.astype(BF16)
            dot_b = do.T.astype(BF16)

            att_t = jnp.where(mask_t, _nt(k_hat_b, q_hat_b), 0.0).astype(BF16)
            dv_intra = _nn(att_t, dob)
            datt = jnp.where(mask, _nt(dob, vb), 0.0).astype(BF16)
            dqh = _nn(datt, k_hat_b)
            datt_t = jnp.where(mask_t, _nt(vb, dob), 0.0).astype(BF16)
            dkh = _nn(datt_t, q_hat_b)

            dst = dst_ref[h]
            for c in reversed(range(nc)):
                dsall_ref[c] = dst
                q_c = jnp.where(rowc == c, q_in_b, jnp.zeros_like(q_in_b))
                dst = dst * dec[c * CHUNK:c * CHUNK + 1, :] + _nn(dot_b, q_c)
            dst_ref[h] = dst
            for c in range(nc):
                rs = slice(c * CHUNK, (c + 1) * CHUNK)
                ds_c = dsall_ref[c]
                dsb = ds_c.astype(BF16)
                st_prev = s_ref[c, h]
                tmp_ref[0, rs, :] = _nt(k_out_b[rs], dsb)
                tmp_ref[1, rs, :] = _nn(vb[rs], dsb)
                tmp_ref[2, rs, :] = _nn(dob[rs], st_prev)
                ddec = jnp.sum(ds_c * st_prev.astype(F32), axis=0, keepdims=True)
                tmp_ref[3, rs, :] = jnp.broadcast_to(ddec * dec[c * CHUNK:c * CHUNK + 1, :],
                                                     (CHUNK, HEAD_DIM))
            dko, dqi = tmp_ref[1], tmp_ref[2]
            dq = dqh * eba + dqi * eb
            dk = dkh * eab + dko * elb
            tko = dko * k_out
            db = dqh * q_hat - dkh * k_hat + dqi * q_in - tko
            dlog = _seg_scan(db, p["r32"], False) + _chunk_total(tko) + tmp_ref[3]
            df = dlog / p["f"] - dk
            sig = p["sig"]
            red_ref[0:1, hs] += jnp.sum(df * (1.0 - sig), axis=0, keepdims=True)
            dp_ref[1, :, hs] = (df * (1.0 - lbh) * sig * (1.0 - sig)).astype(BF16)
            sq = p["sq"]
            dp_ref[0, :, hs] = (dq * (HEAD_DIM ** -0.5) * (sq * (1.0 + qr * (1.0 - sq)))).astype(BF16)
            dp_ref[2, :, hs] = (dv_intra + tmp_ref[0]).astype(BF16)

    col = lambda k: pl.BlockSpec((TL, H), lambda i, k=k: (nt - 1 - i, k))
    rev = pl.BlockSpec((TL, H), lambda i: (nt - 1 - i, 0))
    return _pallas_call(
        body, name="hgrn_bwd", grid=(nt,),
        in_specs=[col(0), col(1), col(2), col(3),
                  pl.BlockSpec(lower_bounds.shape, lambda i: (0, 0)),
                  pl.BlockSpec(gamma.shape, lambda i: (0, 0)),
                  rev, rev,
                  pl.BlockSpec((nc, nh, HEAD_DIM, HEAD_DIM), lambda i: (nt - 1 - i, 0, 0, 0)), ANY, ANY],
        out_specs=[pl.BlockSpec((4, TL, H), lambda i: (0, nt - 1 - i, 0)), pl.BlockSpec((8, H), lambda i: (0, 0))],
        out_shape=[_sds(dproj.shape, BF16), _sds((8, H), F32)],
        input_output_aliases={10: 0},
        scratch_shapes=[pltpu.VMEM((nh, HEAD_DIM, HEAD_DIM), F32),
                        pltpu.VMEM((nc, HEAD_DIM, HEAD_DIM), F32),
                        pltpu.VMEM((4, TL, HEAD_DIM), F32)],
        compiler_params=_params(("arbitrary",), 48),
    )(proj, proj, proj, proj, lower_bounds, gamma, o_pre, d_out, s_saved, after, dproj)


def _shift_down(u, s, row):
    return jnp.where(row >= s, pltpu.roll(u, s, 0), 0.0)


def _shift_up(u, s, row):
    n = u.shape[0]
    return jnp.where(row < n - s, pltpu.roll(u, n - s, 0), 0.0)


def _conv_specs(L, H):
    per = H // LANES
    return [pl.BlockSpec((L, LANES), lambda j, o=o: (0, o * per + j)) for o in (4, 5, 6)]


def _conv_fwd(proj, conv_w, H, after):
    L = proj.shape[0]

    def body(c_ref, b_ref, x_ref, w_ref, after_ref, o_ref):
        row = lax.broadcasted_iota(jnp.int32, (L, LANES), 0)
        u = c_ref[...] * x_ref[...]
        w = w_ref[...]
        y = w[0:1] * _shift_down(u, 2, row) + w[1:2] * _shift_down(u, 1, row) + w[2:3] * u
        o_ref[...] = (b_ref[...] * y).astype(BF16)

    return _pallas_call(
        body, name="conv_fwd", grid=(H // LANES,),
        in_specs=_conv_specs(L, H) + [pl.BlockSpec((3, LANES), lambda j: (0, j)), ANY],
        out_specs=pl.BlockSpec((L, LANES), lambda j: (0, j)),
        out_shape=_sds((L, H), BF16),
        compiler_params=_params(("parallel",), 48),
    )(proj, proj, proj, conv_w, after)


def _conv_bwd(proj, conv_w, dcb, H, after, dproj):
    L = proj.shape[0]

    def body(c_ref, b_ref, x_ref, w_ref, d_ref, after_ref, dproj_ref, dp_ref, dw_ref):
        row = lax.broadcasted_iota(jnp.int32, (L, LANES), 0)
        cg, xb = c_ref[...], x_ref[...]
        u = cg * xb
        u1, u2 = _shift_down(u, 1, row), _shift_down(u, 2, row)
        w = w_ref[...]
        y = w[0:1] * u2 + w[1:2] * u1 + w[2:3] * u
        d = d_ref[...]
        dp_ref[1] = (d * y).astype(BF16)
        dy = d * b_ref[...]
        du = w[2:3] * dy + w[1:2] * _shift_up(dy, 1, row) + w[0:1] * _shift_up(dy, 2, row)
        dw_ref[0:1, :] = jnp.sum(dy * u2, axis=0, keepdims=True)
        dw_ref[1:2, :] = jnp.sum(dy * u1, axis=0, keepdims=True)
        dw_ref[2:3, :] = jnp.sum(dy * u, axis=0, keepdims=True)
        dp_ref[0] = (du * xb).astype(BF16)
        dp_ref[2] = (du * cg).astype(BF16)
        dp_ref[3] = jnp.zeros((L, LANES), BF16)

    blk = pl.BlockSpec((L, LANES), lambda j: (0, j))
    return _pallas_call(
        body, name="conv_bwd", grid=(H // LANES,),
        in_specs=_conv_specs(L, H) + [pl.BlockSpec((3, LANES), lambda j: (0, j)), blk, ANY, ANY],
        out_specs=[pl.BlockSpec((4, L, LANES), lambda j: (2, 0, j)), pl.BlockSpec((3, LANES), lambda j: (0, j))],
        out_shape=[_sds(dproj.shape, BF16), _sds((3, H), F32)],
        input_output_aliases={6: 0},
        compiler_params=_params(("parallel",), 56),
    )(proj, proj, proj, conv_w, dcb, after, dproj)


def _gate_specs(tm, H):
    return [pl.BlockSpec((tm, H), lambda i, k=k: (i, k)) for k in (7, 8, 9, 10)]


def _fwd_mix(og, cb, proj, x, wat, wbt, wout, g_ffn, H, after):
    L, D = x.shape
    tm = min(L, 512)

    def body(o_ref, cb_ref, ga0, ga1, gb0, gb1, x_ref, wa_ref, wb_ref, wo_ref, g_ref, after_ref,
             sa_ref, sb_ref, ta_ref, tb_ref, m_ref, x1_ref, h2_ref):
        ya, yb = _nt(o_ref[...], wa_ref[...]), _nt(cb_ref[...], wb_ref[...])
        for k, (gar, gbr) in enumerate(((ga0, gb0), (ga1, gb1))):
            cs = slice(k * H, (k + 1) * H)
            sa, sb = _sigmoid(gar[...]), _sigmoid(gbr[...])
            ma, mb = sa * ya[:, cs], sb * yb[:, cs]
            m_ref[:, cs] = (ma + mb).astype(BF16)
            sa_ref[:, cs] = sa.astype(BF16)
            sb_ref[:, cs] = sb.astype(BF16)
            ta_ref[:, cs] = (ma * (1.0 - sa)).astype(BF16)
            tb_ref[:, cs] = (mb * (1.0 - sb)).astype(BF16)
        x1 = x_ref[...] + _nn(m_ref[...], wo_ref[...])
        x1_ref[...] = x1
        _, xh = _rms_stats(x1)
        h2_ref[...] = (xh * g_ref[...]).astype(BF16)

    row = lambda w: pl.BlockSpec((tm, w), lambda i: (i, 0))
    full = lambda a: pl.BlockSpec(a.shape, lambda i: (0,) * a.ndim)
    return _pallas_call(
        body, name="fwd_mix", grid=(L // tm,),
        in_specs=[row(H), row(H)] + _gate_specs(tm, H) + [row(D), full(wat), full(wbt), full(wout),
                                                           full(g_ffn), ANY],
        out_specs=[row(D)] * 7,
        out_shape=[_sds((L, D), BF16)] * 5 + [_sds((L, D), F32), _sds((L, D), BF16)],
        compiler_params=_params(("parallel",), 56),
    )(og, cb, proj, proj, proj, proj, x, wat, wbt, wout, g_ffn, after)


def _bwd_mix(dx1b, sig_a, sig_b, dm_dga, dm_dgb, wat, wbt, wout, H, after):
    L, D = dx1b.shape
    tm = min(L, 512)

    def body(dx_ref, sa_ref, sb_ref, ta_ref, tb_ref, wa_ref, wb_ref, wo_ref, after_ref,
             dya_ref, dyb_ref, dgate_ref, do_ref, dcb_ref):
        dm = _nt(dx_ref[...], wo_ref[...])
        dga = (dm * ta_ref[...].astype(F32)).astype(BF16)
        dgb = (dm * tb_ref[...].astype(F32)).astype(BF16)
        for q, part in enumerate((dga[:, 0:H], dga[:, H:D], dgb[:, 0:H], dgb[:, H:D])):
            dgate_ref[q] = part
        dya_ref[...] = (dm * sa_ref[...].astype(F32)).astype(BF16)
        dyb_ref[...] = (dm * sb_ref[...].astype(F32)).astype(BF16)
        do_ref[...] = _nn(dya_ref[...], wa_ref[...])
        dcb_ref[...] = _nn(dyb_ref[...], wb_ref[...])

    row = lambda w: pl.BlockSpec((tm, w), lambda i: (i, 0))
    full = lambda a: pl.BlockSpec(a.shape, lambda i: (0,) * a.ndim)
    return _pallas_call(
        body, name="bwd_mix", grid=(L // tm,),
        in_specs=[row(D)] * 5 + [full(wat), full(wbt), full(wout), ANY],
        out_specs=[row(D), row(D), pl.BlockSpec((4, tm, H), lambda i: (1, i, 0)), row(H), row(H)],
        out_shape=[_sds((L, D), BF16)] * 2 + [_sds((DPROJ_BLOCKS, L, H), BF16)] + [_sds((L, H), F32)] * 2,
        compiler_params=_params(("parallel",), 56),
    )(dx1b, sig_a, sig_b, dm_dga, dm_dgb, wat, wbt, wout, after)


def _fwd_ffn_up(h2, wgt, wut):
    L, D = h2.shape
    F = wgt.shape[0]
    tn = F // 2
    tm = min(L, 512)

    def body(h_ref, wg_ref, wu_ref, sa_ref, sb_ref, s_ref):
        h = h_ref[...]
        a, b = _nt(h, wg_ref[...]), _nt(h, wu_ref[...])
        sg = _sigmoid(a)
        silu = a * sg
        sa_ref[...] = (b * sg * (1.0 + a * (1.0 - sg))).astype(BF16)
        sb_ref[...] = silu.astype(BF16)
        s_ref[...] = (silu * b).astype(BF16)

    wspec = pl.BlockSpec((tn, D), lambda j, i: (j, 0))
    ospec = pl.BlockSpec((tm, tn), lambda j, i: (i, j))
    return _pallas_call(
        body, name="fwd_ffn_up", grid=(2, L // tm),
        in_specs=[pl.BlockSpec((tm, D), lambda j, i: (i, 0)), wspec, wspec],
        out_specs=[ospec] * 3,
        out_shape=[_sds((L, F), BF16)] * 3,
        compiler_params=_params(("parallel", "parallel"), 48),
    )(h2, wgt, wut)


def _fwd_down_loss(s, wd, x1, target, g_final):
    L, D = x1.shape
    F = wd.shape[0]
    tm = min(L, 512)

    def body(s_ref, wd_ref, x1_ref, t_ref, g_ref, dx_ref, dxb_ref, red_ref):
        @pl.when(pl.program_id(0) == 0)
        def _():
            red_ref[...] = jnp.zeros_like(red_ref)

        g = g_ref[...]
        r, xh = _rms_stats(x1_ref[...] + _nn(s_ref[...], wd_ref[...]))
        e = xh * g - t_ref[...]
        dy = e * (1.0 / D)
        dx = _rms_bwd(dy * g, xh, r)
        dx_ref[...] = dx
        dxb_ref[...] = dx.astype(BF16)
        red_ref[0:1, :] += jnp.sum(dy * xh, axis=0, keepdims=True)
        red_ref[1:2, :] += jnp.broadcast_to(0.5 * jnp.sum(e * e) * (1.0 / D), (1, D))

    row = pl.BlockSpec((tm, D), lambda i: (i, 0))
    return _pallas_call(
        body, name="fwd_down_loss", grid=(L // tm,),
        in_specs=[pl.BlockSpec((tm, F), lambda i: (i, 0)), pl.BlockSpec((F, D), lambda i: (0, 0)),
                  row, row, pl.BlockSpec((1, D), lambda i: (0, 0))],
        out_specs=[row, row, pl.BlockSpec((8, D), lambda i: (0, 0))],
        out_shape=[_sds((L, D), F32), _sds((L, D), BF16), _sds((8, D), F32)],
        compiler_params=_params(("arbitrary",), 56),
    )(s, wd, x1, target, g_final)


def _bwd_down(dx2b, wd, s_a, s_b):
    L, D = dx2b.shape
    F = wd.shape[0]
    tn = F // 2
    tm = min(L, 512)

    def body(dx_ref, wd_ref, sa_ref, sb_ref, da_ref, db_ref):
        ds = _nt(dx_ref[...], wd_ref[...])
        da_ref[...] = (ds * sa_ref[...].astype(F32)).astype(BF16)
        db_ref[...] = (ds * sb_ref[...].astype(F32)).astype(BF16)

    ospec = pl.BlockSpec((tm, tn), lambda j, i: (i, j))
    return _pallas_call(
        body, name="bwd_down", grid=(2, L // tm),
        in_specs=[pl.BlockSpec((tm, D), lambda j, i: (i, 0)),
                  pl.BlockSpec((tn, D), lambda j, i: (j, 0)), ospec, ospec],
        out_specs=[ospec] * 2,
        out_shape=[_sds((L, F), BF16)] * 2,
        compiler_params=_params(("parallel", "parallel"), 48),
    )(dx2b, wd, s_a, s_b)


def _bwd_ffn_dh(da, db, wgt, wut, x1, dx2, g_ffn, after):
    L, D = x1.shape
    F = wgt.shape[0]
    tm = min(L, 256)

    def body(da_ref, db_ref, wg_ref, wu_ref, x1_ref, dx2_ref, g_ref, after_ref, dx_ref, dxb_ref, red_ref):
        @pl.when(pl.program_id(0) == 0)
        def _():
            red_ref[...] = jnp.zeros_like(red_ref)

        dh = _nn(da_ref[...], wg_ref[...]) + _nn(db_ref[...], wu_ref[...])
        r, xh = _rms_stats(x1_ref[...])
        red_ref[0:1, :] += jnp.sum(dh * xh, axis=0, keepdims=True)
        dx = dx2_ref[...] + _rms_bwd(dh * g_ref[...], xh, r)
        dx_ref[...] = dx
        dxb_ref[...] = dx.astype(BF16)

    row = pl.BlockSpec((tm, D), lambda i: (i, 0))
    aspec = pl.BlockSpec((tm, F), lambda i: (i, 0))
    wspec = pl.BlockSpec((F, D), lambda i: (0, 0))
    return _pallas_call(
        body, name="bwd_ffn_dh", grid=(L // tm,),
        in_specs=[aspec, aspec, wspec, wspec, row, row, pl.BlockSpec((1, D), lambda i: (0, 0)), ANY],
        out_specs=[row, row, pl.BlockSpec((8, D), lambda i: (0, 0))],
        out_shape=[_sds((L, D), F32), _sds((L, D), BF16), _sds((8, D), F32)],
        compiler_params=_params(("arbitrary",), 56),
    )(da, db, wgt, wut, x1, dx2, g_ffn, after)


def _bwd_in(dproj, w_int, x, dx1, g_mix, after):
    L, D = x.shape
    N = w_int.shape[0]
    H = dproj.shape[2]
    tm = min(L, 256)
    assert N == len(DPROJ_BLOCK_OF) * H

    def body(blocks_ref, w_ref, x_ref, dx1_ref, g_ref, after_ref, dx_ref, red_ref, dp_ref):
        @pl.when(pl.program_id(0) == 0)
        def _():
            red_ref[...] = jnp.zeros_like(red_ref)

        for t, block in enumerate(DPROJ_BLOCK_OF):
            dp_ref[:, t * H:(t + 1) * H] = blocks_ref[block]
        dh = _nn(dp_ref[...], w_ref[...])
        r, xh = _rms_stats(x_ref[...])
        red_ref[0:1, :] += jnp.sum(dh * xh, axis=0, keepdims=True)
        dx_ref[...] = dx1_ref[...] + _rms_bwd(dh * g_ref[...], xh, r)

    row = pl.BlockSpec((tm, D), lambda i: (i, 0))
    return _pallas_call(
        body, name="bwd_in", grid=(L // tm,),
        in_specs=[pl.BlockSpec((DPROJ_BLOCKS, tm, H), lambda i: (0, i, 0)), pl.BlockSpec((N, D), lambda i: (0, 0)),
                  row, row, pl.BlockSpec((1, D), lambda i: (0, 0)), ANY],
        out_specs=[row, pl.BlockSpec((8, D), lambda i: (0, 0))],
        out_shape=[_sds((L, D), F32), _sds((8, D), F32)],
        scratch_shapes=[pltpu.VMEM((tm, N), BF16)],
        compiler_params=_params(("arbitrary",), 56),
    )(dproj, w_int, x, dx1, g_mix, after)


def _dw_in(h, dproj, n_cols):
    L, D = h.shape
    H = dproj.shape[2]
    tk = min(L, TK_TOKENS)
    first = [(j * n_cols) // H for j in range(N_CHIPS)]
    last = [((j + 1) * n_cols - 1) // H for j in range(N_CHIPS)]
    slots = max(b - a for a, b in zip(first, last)) + 1
    plan = []
    for j in range(N_CHIPS):
        lo, hi = j * n_cols, (j + 1) * n_cols
        segments = []
        for s in range(last[j] - first[j] + 1):
            a, b = max(lo, (first[j] + s) * H), min(hi, (first[j] + s + 1) * H)
            segments.append((s, a - (first[j] + s) * H, b - a, a - lo))
        plan.append(segments)

    def body(*refs):
        h_ref, slot_refs, o_ref, b_ref = refs[0], refs[1:1 + slots], refs[1 + slots], refs[2 + slots]
        j, k = pl.program_id(0), pl.program_id(1)
        for jj in range(N_CHIPS):
            @pl.when(j == jj)
            def _(jj=jj):
                for s, start, width, at in plan[jj]:
                    b_ref[:, at:at + width] = slot_refs[s][:, start:start + width]

        part = _tn(h_ref[...], b_ref[...])

        @pl.when(k == 0)
        def _():
            o_ref[...] = part

        @pl.when(k > 0)
        def _():
            o_ref[...] += part

    def slot_spec(s):
        blocks = [DPROJ_BLOCK_OF[min(first[j] + s, last[j])] for j in range(N_CHIPS)]

        def index(j, k):
            block = blocks[0]
            for jj in range(1, N_CHIPS):
                block = jnp.where(j == jj, blocks[jj], block)
            return (block, k, 0)

        return pl.BlockSpec((None, tk, H), index)

    return _pallas_call(
        body, name="dw_in", grid=(N_CHIPS, L // tk),
        in_specs=[pl.BlockSpec((tk, D), lambda j, k: (k, 0))] + [slot_spec(s) for s in range(slots)],
        out_specs=pl.BlockSpec((None, D, n_cols), lambda j, k: (j, 0, 0)),
        out_shape=_sds((N_CHIPS, D, n_cols), F32),
        scratch_shapes=[pltpu.VMEM((tk, n_cols), BF16)],
        compiler_params=_params(("parallel", "arbitrary"), 56),
    )(h, *([dproj] * slots))


def _mm_tn(name, a, b, a_spec, b_spec, o_block, n_out, n_k):
    def body(a_ref, b_ref, o_ref):
        part = _tn(a_ref[...], b_ref[...])

        @pl.when(pl.program_id(1) == 0)
        def _():
            o_ref[...] = part

        @pl.when(pl.program_id(1) > 0)
        def _():
            o_ref[...] += part

    return _pallas_call(
        body, name=name, grid=(n_out, n_k),
        in_specs=[a_spec, b_spec],
        out_specs=pl.BlockSpec((None,) + o_block, lambda j, k: (j, 0, 0)),
        out_shape=_sds((n_out,) + o_block, F32),
        compiler_params=_params(("parallel", "arbitrary"), 56),
    )(a, b)


TK_TOKENS = 2048


def _dw_cols(name, a, b, n_cols):
    L, M = a.shape
    tk = min(L, TK_TOKENS)
    return _mm_tn(name, a, b, pl.BlockSpec((tk, M), lambda j, k: (k, 0)),
                  pl.BlockSpec((tk, n_cols), lambda j, k: (k, j)), (M, n_cols), N_CHIPS, L // tk)


def _dw_rows(name, a, b):
    L, M = a.shape
    N = b.shape[1]
    tk = min(L, TK_TOKENS)
    return _mm_tn(name, a, b, pl.BlockSpec((tk, M // N_CHIPS), lambda j, k: (k, j)),
                  pl.BlockSpec((tk, N), lambda j, k: (k, 0)), (M // N_CHIPS, N), N_CHIPS, L // tk)


def _dw_rows2(name, a, b):
    L, M = a.shape
    N = b.shape[1]
    tk = min(L, TK_TOKENS)
    return _mm_tn(name, a, b, pl.BlockSpec((tk, M // 2), lambda j, k: (k, j)),
                  pl.BlockSpec((tk, N), lambda j, k: (k, 0)), (M // 2, N), 2, L // tk)


def _place():
    x, y, c = lax.axis_index("x"), lax.axis_index("y"), lax.axis_index("c")
    chips = [(1 - x, y), (x, 1 - y), (1 - x, 1 - y)]
    return x, y, c, 2 * x + y, chips


def _remote(src, dst, send_sem, recv_sem, device):
    return pltpu.make_async_remote_copy(src_ref=src, dst_ref=dst, send_sem=send_sem,
                                        recv_sem=recv_sem, device_id=device, device_id_type=MESH)


def _half(ref, lead, c, r2):
    return ref.at[lead, pl.ds(pl.multiple_of(c * r2, 16), r2), :]


def _cast_place(name, ws, chip_idx):
    n = len(ws)

    def body(k_ref, *refs):
        for w_ref, o_ref in zip(refs[:n], refs[n:]):
            o_ref[...] = w_ref[...].astype(BF16)

    return _pallas_call(
        body, name=name,
        grid_spec=pltpu.PrefetchScalarGridSpec(
            num_scalar_prefetch=1, grid=(2,),
            in_specs=[pl.BlockSpec((w.shape[0] // 2, w.shape[1]), lambda i, k_ref: (i, 0)) for w in ws],
            out_specs=[pl.BlockSpec((None, w.shape[0] // 2, w.shape[1]), lambda i, k_ref: (k_ref[0], i, 0))
                       for w in ws]),
        out_shape=[_sds((N_CHIPS,) + w.shape, BF16) for w in ws],
        compiler_params=_params(("parallel",), 48),
    )(chip_idx, *ws)


def _cast_place_t(name, ws, chip_idx):
    n = len(ws)
    r, cols = ws[0].shape

    def body(k_ref, *refs):
        for w_ref, o_ref in zip(refs[:n], refs[n:]):
            o_ref[...] = w_ref[...].T.astype(BF16)

    return _pallas_call(
        body, name=name,
        grid_spec=pltpu.PrefetchScalarGridSpec(
            num_scalar_prefetch=1, grid=(cols // LANES,),
            in_specs=[pl.BlockSpec((r, LANES), lambda i, k_ref: (0, i))] * n,
            out_specs=[pl.BlockSpec((None, LANES, r), lambda i, k_ref: (k_ref[0], i, 0))] * n),
        out_shape=[_sds((N_CHIPS, cols, r), BF16)] * n,
        compiler_params=_params(("parallel",), 48),
    )(chip_idx, *ws)


def _gather_copies(bufs, whole, send_sems, recv_sems, select=None):
    x, y, c, k, chips = _place()
    pairs = []
    for w, buf in enumerate(bufs):
        for j, (cx, cy) in enumerate(chips):
            if select is not None and not select(w, j):
                continue
            if w in whole:
                mine, theirs = buf.at[k], buf.at[2 * cx + cy]
            else:
                r2 = buf.shape[1] // 2
                mine, theirs = _half(buf, k, c, r2), _half(buf, 2 * cx + cy, c, r2)
            sems = (send_sems.at[w * 3 + j], recv_sems.at[w * 3 + j])
            pairs.append((_remote(mine, mine, *sems, (cx, cy, c)), _remote(theirs, theirs, *sems, (x, y, c))))
    return pairs


def _gather_start(name, groups, after):
    flat = [b for bufs, _, _ in groups for b in bufs]
    nb, ng = len(flat), len(groups)

    def body(*refs):
        ins, sems, token = refs[:nb], refs[nb + 1:nb + 1 + 2 * ng], refs[-1]
        pos = 0
        for g, (bufs, whole, select) in enumerate(groups):
            for send, _ in _gather_copies(ins[pos:pos + len(bufs)], whole, sems[2 * g], sems[2 * g + 1], select):
                send.start()
            pos += len(bufs)
        token[...] = jnp.zeros_like(token)

    sem_shapes = []
    for bufs, _, _ in groups:
        sem_shapes += [pltpu.SemaphoreType.DMA((3 * len(bufs),))] * 2
    out = _pallas_call(
        body, name=name,
        in_specs=[HBM] * nb + [ANY], out_specs=tuple([SEM] * (2 * ng) + [HBM] * nb + [VMEM]),
        out_shape=tuple(sem_shapes + [pltpu.HBM(b.shape, b.dtype) for b in flat] + [_sds((8, LANES), F32)]),
        input_output_aliases={i: 2 * ng + i for i in range(nb)},
        compiler_params=pltpu.CompilerParams(has_side_effects=EFFECT),
    )(*flat, after)
    sems, thru, pos = [], [], 2 * ng
    for g, (bufs, _, _) in enumerate(groups):
        sems.append((out[2 * g], out[2 * g + 1]))
        thru.append(list(out[pos:pos + len(bufs)]))
        pos += len(bufs)
    return sems, thru, out[-1]


def _gather_wait(name, bufs, whole, sems, after, select=None):
    nb = len(bufs)

    def body(*refs):
        ins, send_sems, recv_sems = refs[:nb], refs[nb], refs[nb + 1]
        for send, arrival in _gather_copies(ins, whole, send_sems, recv_sems, select):
            send.wait_send()
            arrival.wait_recv()

    return _pallas_call(
        body, name=name,
        in_specs=[HBM] * nb + [SEM, SEM, ANY], out_specs=[HBM] * nb,
        out_shape=[pltpu.HBM(b.shape, b.dtype) for b in bufs],
        input_output_aliases={i: i for i in range(nb)},
        compiler_params=pltpu.CompilerParams(has_side_effects=EFFECT),
    )(*bufs, sems[0], sems[1], after)


def _gather_forward(name, bufs, sources=(0, 1, 2)):
    n = len(bufs)

    def body(*refs):
        outs = refs[n:2 * n]
        send_sems, recv_sems = refs[2 * n:]
        x, y, c, _, chips = _place()
        sends = []
        for w in range(n):
            r2 = outs[w].shape[1] // 2
            for j in sources:
                landed = _half(outs[w], 2 * chips[j][0] + chips[j][1], c, r2)
                sends.append(_remote(landed, landed, send_sems.at[w * 3 + j], recv_sems.at[w * 3 + j],
                                     (x, y, 1 - c)))
        for cp in sends:
            cp.start()
        for w in range(n):
            r2 = outs[w].shape[1] // 2
            for j in sources:
                got = _half(outs[w], 2 * chips[j][0] + chips[j][1], 1 - c, r2)
                _remote(got, got, send_sems.at[w * 3 + j], recv_sems.at[w * 3 + j], (x, y, c)).wait_recv()
        for cp in sends:
            cp.wait_send()

    return _pallas_call(
        body, name=name,
        in_specs=[ANY] * n, out_specs=[ANY] * n,
        out_shape=[_sds(b.shape, b.dtype) for b in bufs],
        input_output_aliases={i: i for i in range(n)},
        scratch_shapes=[pltpu.SemaphoreType.DMA((n * 3,)), pltpu.SemaphoreType.DMA((n * 3,))],
    )(*bufs)


def _rs_add(name, grads3, from_sibling, c_idx):
    n = len(grads3)

    def body(c_ref, *refs):
        for g_ref, s_ref, o_ref in zip(refs[:n], refs[n:2 * n], refs[2 * n:]):
            o_ref[...] = (g_ref[...] + s_ref[...]).astype(BF16)

    mine = [pl.BlockSpec((None,) + s.shape[1:], lambda k, c_ref: (k, c_ref[0], 0)) for s in from_sibling]
    whole = [pl.BlockSpec((None,) + s.shape[1:], lambda k, c_ref: (k, 0, 0)) for s in from_sibling]
    return _pallas_call(
        body, name=name,
        grid_spec=pltpu.PrefetchScalarGridSpec(num_scalar_prefetch=1, grid=(N_CHIPS,), in_specs=mine + whole,
                                               out_specs=whole),
        out_shape=[_sds(s.shape, BF16) for s in from_sibling],
        compiler_params=_params(("parallel",), 48),
    )(c_idx, *grads3, *from_sibling)


def _split_start(name, arrays, n_sems, pairs_fn):
    n = len(arrays)

    def body(*refs):
        for send, _ in pairs_fn(refs[:n], refs[n], refs[n + 1]):
            send.start()
        refs[-1][...] = jnp.zeros_like(refs[-1])

    out = _pallas_call(
        body, name=name,
        in_specs=[HBM] * n, out_specs=tuple([SEM, SEM] + [HBM] * n + [VMEM]),
        out_shape=tuple([pltpu.SemaphoreType.DMA((n_sems,))] * 2 + [pltpu.HBM(a.shape, a.dtype) for a in arrays]
                        + [_sds((8, LANES), F32)]),
        input_output_aliases={i: 2 + i for i in range(n)},
        compiler_params=pltpu.CompilerParams(has_side_effects=EFFECT),
    )(*arrays)
    return (out[0], out[1]), list(out[2:2 + n]), out[-1]


def _split_wait(name, sems, arrays, pairs_fn, after):
    n = len(arrays)

    def body(*refs):
        for send, arrival in pairs_fn(refs[:n], refs[n], refs[n + 1]):
            send.wait_send()
            arrival.wait_recv()

    return list(_pallas_call(
        body, name=name,
        in_specs=[HBM] * n + [SEM, SEM, ANY], out_specs=[HBM] * n,
        out_shape=[pltpu.HBM(a.shape, a.dtype) for a in arrays],
        input_output_aliases={i: i for i in range(n)},
        compiler_params=pltpu.CompilerParams(has_side_effects=EFFECT),
    )(*arrays, sems[0], sems[1], after))


def _forward_pairs(bufs, send_sems, recv_sems):
    x, y, c, _, chips = _place()
    pairs = []
    for w, buf in enumerate(bufs):
        r2 = buf.shape[1] // 2
        for j, (cx, cy) in enumerate(chips):
            landed, theirs = _half(buf, 2 * cx + cy, c, r2), _half(buf, 2 * cx + cy, 1 - c, r2)
            sems = (send_sems.at[w * 3 + j], recv_sems.at[w * 3 + j])
            pairs.append((_remote(landed, landed, *sems, (x, y, 1 - c)), _remote(theirs, theirs, *sems, (x, y, c))))
    return pairs


def _sibling_pairs(arrays, send_sems, recv_sems):
    x, y, c, _, _ = _place()
    n = len(arrays) // 2
    pairs = []
    for w in range(n):
        r2 = arrays[w].shape[1] // 2
        cp = _remote(_half(arrays[w], slice(None), 1 - c, r2), arrays[n + w], send_sems.at[w], recv_sems.at[w],
                     (x, y, 1 - c))
        pairs.append((cp, cp))
    return pairs


def _ici_pairs(arrays, send_sems, recv_sems):
    x, y, c, _, chips = _place()
    n = len(arrays) // 2
    pairs = []
    for w in range(n):
        for j, (cx, cy) in enumerate(chips):
            cp = _remote(arrays[w].at[2 * cx + cy], arrays[n + w].at[j],
                         send_sems.at[w * 3 + j], recv_sems.at[w * 3 + j], (cx, cy, c))
            pairs.append((cp, cp))
    return pairs


def _rs_sum(name, partials, received, place_idx):
    n = len(partials)
    nb = 2
    blocks = [(p.shape[1] // nb, p.shape[2]) for p in partials]

    def body(idx_ref, *refs):
        for p_ref, r_ref, o_ref in zip(refs[:n], refs[n:2 * n], refs[2 * n:]):
            o_ref[...] = ((p_ref[...].astype(F32) + r_ref[0].astype(F32))
                          + (r_ref[1].astype(F32) + r_ref[2].astype(F32)))

    return _pallas_call(
        body, name=name,
        grid_spec=pltpu.PrefetchScalarGridSpec(
            num_scalar_prefetch=1, grid=(nb,),
            in_specs=[pl.BlockSpec((None,) + b, lambda i, idx: (idx[0], i, 0)) for b in blocks]
            + [pl.BlockSpec((3,) + b, lambda i, idx: (0, i, 0)) for b in blocks],
            out_specs=[pl.BlockSpec(b, lambda i, idx: (idx[1] * nb + i, 0)) for b in blocks]),
        out_shape=[_sds((2 * p.shape[1], p.shape[2]), F32) for p in partials],
        compiler_params=_params(("parallel",), 48),
    )(place_idx, *partials, *received)


def _share_pairs(arrays, send_sems, recv_sems):
    x, y, c, _, _ = _place()
    pairs = []
    for w, arr in enumerate(arrays):
        r2 = arr.shape[0] // 2
        mine = arr.at[pl.ds(pl.multiple_of(c * r2, 8), r2), :]
        theirs = arr.at[pl.ds(pl.multiple_of((1 - c) * r2, 8), r2), :]
        sems = (send_sems.at[w], recv_sems.at[w])
        pairs.append((_remote(mine, mine, *sems, (x, y, 1 - c)), _remote(theirs, theirs, *sems, (x, y, c))))
    return pairs


def _small_pack(red_mix, red_ffn, red_final, red_hg, g_conv):
    D = red_mix.shape[1]
    H = red_hg.shape[1]

    def body(mix_ref, ffn_ref, fin_ref, hg_ref, cv_ref, in_ref):
        in_ref[...] = jnp.zeros_like(in_ref)
        in_ref[0:1, :] = mix_ref[0:1, :]
        in_ref[1:2, :] = ffn_ref[0:1, :]
        in_ref[2:3, :] = fin_ref[0:1, :]
        gam = hg_ref[1:2, 0:HEAD_DIM]
        for h in range(1, H // HEAD_DIM):
            gam = gam + hg_ref[1:2, h * HEAD_DIM:(h + 1) * HEAD_DIM]
        in_ref[3:4, 0:HEAD_DIM] = gam
        in_ref[3:4, HEAD_DIM:2 * HEAD_DIM] = fin_ref[1:2, 0:HEAD_DIM]
        in_ref[4:5, 0:H] = hg_ref[0:1, :]
        in_ref[6:9, 0:H] = cv_ref[...]

    return _pallas_call(
        body, name="small_pack", pin=False,
        in_specs=[VMEM] * 5, out_specs=VMEM, out_shape=_sds((N_SMALL_ROWS, D), F32),
    )(red_mix, red_ffn, red_final, red_hg, g_conv)


def _small_pairs(arrays, send_sems, recv_sems):
    block, gathered = arrays
    x, y, c, _, _ = _place()
    me = 4 * x + 2 * y + c
    pairs = []
    for m in range(1, 8):
        px, py, pc = x ^ ((m >> 2) & 1), y ^ ((m >> 1) & 1), c ^ (m & 1)
        sems = (send_sems.at[m - 1], recv_sems.at[m - 1])
        pairs.append((_remote(block, gathered.at[me], *sems, (px, py, pc)),
                      _remote(block, gathered.at[4 * px + 2 * py + pc], *sems, (x, y, c))))
    return pairs


def _adamw_math(w, g, m, v):
    m = ADAM_B1 * m + (1.0 - ADAM_B1) * g
    v = ADAM_B2 * v + (1.0 - ADAM_B2) * jnp.square(g)
    m_hat = m / (1.0 - ADAM_B1 ** ADAM_STEP)
    v_hat = v / (1.0 - ADAM_B2 ** ADAM_STEP)
    delta = -ADAM_LR * (m_hat / (jnp.sqrt(v_hat) + ADAM_EPS) + ADAM_WD * w)
    return delta, m, v


def _adamw(name, gs, ws, ms, vs):
    n = len(gs)
    nb = 4

    def body(*refs):
        ins, outs = refs[:4 * n], refs[4 * n:]
        for j in range(n):
            g_ref, w_ref, m_ref, v_ref = ins[j], ins[n + j], ins[2 * n + j], ins[3 * n + j]
            go_ref, d_ref, mo_ref, vo_ref = outs[4 * j:4 * j + 4]
            g = g_ref[...]
            go_ref[...] = g
            d_ref[...], mo_ref[...], vo_ref[...] = _adamw_math(w_ref[...], g, m_ref[...], v_ref[...])

    blk = [pl.BlockSpec((g.shape[0] // nb, g.shape[1]), lambda i: (i, 0)) for g in gs]
    out = _pallas_call(
        body, name=name, grid=(nb,),
        in_specs=blk * 4, out_specs=[b for b in blk for _ in range(4)],
        out_shape=[_sds(g.shape, F32) for g in gs for _ in range(4)],
        compiler_params=_params(("parallel",), 56),
    )(*gs, *ws, *ms, *vs)
    return [list(out[4 * j:4 * j + 4]) for j in range(n)]


def _small_update(block, gathered, place_idx, ws, ms, vs):
    n = len(ws)
    H = ws[1].shape[1]

    def body(idx_ref, blk_ref, all_ref, *refs):
        w, m, v, outs, tot_ref = refs[:n], refs[n:2 * n], refs[2 * n:3 * n], refs[3 * n:-1], refs[-1]
        chip, me = idx_ref[0], idx_ref[1]
        tot = jnp.where(me == 0, blk_ref[...], all_ref[0])
        for d in range(1, 8):
            tot = tot + jnp.where(me == d, blk_ref[...], all_ref[d])
        tot_ref[...] = tot
        p0 = _lower_bound(w[1][...])
        dl0 = p0 * (1.0 - p0) * tot_ref[4:5, 0:H]
        conv = jnp.zeros((3, LANES), F32)
        for k in range(N_CHIPS):
            conv = jnp.where(chip == k, tot_ref[6:9, k * LANES:(k + 1) * LANES], conv)
        grads = [tot_ref[0:1, :], None, tot_ref[3:4, 0:HEAD_DIM], conv, tot_ref[1:2, :], tot_ref[2:3, :]]
        for p in range(n):
            g_ref, d_ref, mo_ref, vo_ref = outs[4 * p:4 * p + 4]
            if p == 1:
                for row, g in ((slice(0, 1), dl0), (slice(1, 2), -dl0)):
                    g_ref[row, :] = g
                    d_ref[row, :], mo_ref[row, :], vo_ref[row, :] = _adamw_math(
                        w[p][row, :], g, m[p][row, :], v[p][row, :])
            else:
                g_ref[...] = grads[p]
                d_ref[...], mo_ref[...], vo_ref[...] = _adamw_math(w[p][...], grads[p], m[p][...], v[p][...])
        outs[4 * n][...] = tot_ref[3:4, HEAD_DIM:2 * HEAD_DIM]

    full = lambda a: pl.BlockSpec(a.shape, lambda i, idx: (0,) * a.ndim)
    out_shape = [_sds(w.shape, F32) for w in ws for _ in range(4)] + [_sds((1, LANES), F32)]
    return _pallas_call(
        body, name="small_update",
        grid_spec=pltpu.PrefetchScalarGridSpec(
            num_scalar_prefetch=1, grid=(1,),
            in_specs=[full(block), full(gathered)] + [full(a) for a in ws + ms + vs],
            out_specs=[full(s) for s in out_shape],
            scratch_shapes=[pltpu.VMEM(block.shape, F32)]),
        out_shape=out_shape,
    )(place_idx, block, gathered, *ws, *ms, *vs)


def kernel(x, norm_mix_g, w_in, lower_bounds, hg_norm_g, conv_w, w_branch_a, w_branch_b, w_out, norm_ffn_g, w_ffn_gate, w_ffn_up, w_ffn_down, norm_final_g, loss_target, m_norm_mix_g, m_w_in, m_lower_bounds, m_hg_norm_g, m_conv_w, m_w_branch_a, m_w_branch_b, m_w_out, m_norm_ffn_g, m_w_ffn_gate, m_w_ffn_up, m_w_ffn_down, m_norm_final_g, v_norm_mix_g, v_w_in, v_lower_bounds, v_hg_norm_g, v_conv_w, v_w_branch_a, v_w_branch_b, v_w_out, v_norm_ffn_g, v_w_ffn_gate, v_w_ffn_up, v_w_ffn_down, v_norm_final_g):
    _, L, D = x.shape
    H = D // 2
    assert lower_bounds.shape == (2, H) and hg_norm_g.shape == (1, HEAD_DIM)
    assert conv_w.shape == (1, 3, LANES) and w_in.shape[2] * N_CHIPS == 11 * H
    x2d, target = x.reshape(L, D), loss_target.reshape(L, D)
    g_final = norm_final_g.reshape(1, D)
    chip = 2 * lax.axis_index("x") + lax.axis_index("y")
    core = lax.axis_index("c")

    tr = lambda w: jnp.transpose(w[0])
    big = [w_in[0], w_branch_a[0], w_branch_b[0], w_out[0], tr(w_ffn_gate), tr(w_ffn_up), w_ffn_down[0]]
    big_m = [m_w_in[0], m_w_branch_a[0], m_w_branch_b[0], m_w_out[0], tr(m_w_ffn_gate), tr(m_w_ffn_up),
             m_w_ffn_down[0]]
    big_v = [v_w_in[0], v_w_branch_a[0], v_w_branch_b[0], v_w_out[0], tr(v_w_ffn_gate), tr(v_w_ffn_up),
             v_w_ffn_down[0]]
    names = ["w_in", "w_branch_a", "w_branch_b", "w_out", "w_ffn_gate", "w_ffn_up", "w_ffn_down"]

    chip_idx = chip.reshape(1).astype(jnp.int32)
    def per_shape(fn, tag, js, *lists):
        groups = {}
        for pos, a in enumerate(lists[0]):
            groups.setdefault(a.shape, []).append(pos)
        results = [None] * len(js)
        for same in groups.values():
            out = fn(tag + names[js[same[0]]], *[[xs[p] for p in same] for xs in lists])
            for q, p in enumerate(same):
                results[p] = out[q]
        return results

    place_t = lambda name, ws: _cast_place_t(name, ws, chip_idx)
    placed = per_shape(place_t, "place_", [0, 1, 2], big[:3]) + list(_cast_place("place_rest", big[3:], chip_idx))
    conv_placed = lax.dynamic_update_slice(jnp.zeros((N_CHIPS, 3, LANES), F32), conv_w, (chip, 0, 0))
    x_i, y_i = lax.axis_index("x"), lax.axis_index("y")
    blocks = lambda *ks: jnp.stack(ks).astype(jnp.int32)
    near = lambda w, j: j < 2
    far = lambda w, j: w == 1 or j == 2
    near_sems, in_flight, _ = _gather_start("gather_start_near", [([placed[0]], set(), near)], chip_idx)
    w_in_buf = in_flight[0][0]
    h, proj = _fwd_proj_first(x2d, norm_mix_g, w_in_buf, blocks(chip))
    sems, in_flight, _ = _gather_start(
        "gather_start_rest", [([w_in_buf, conv_placed], {1}, far), (placed[1:4], set(), None),
                              (placed[4:], set(), None)], h)
    w_in_buf, conv_buf = in_flight[0]
    (w_in_buf,) = _gather_wait("gather_wait_in_near", [w_in_buf], set(), near_sems[0], h, near)
    (w_in_buf,) = _gather_forward("gather_fwd_in_near", [w_in_buf], (0, 1))
    proj = _fwd_proj_more("fwd_proj_near", h, w_in_buf, proj,
                          blocks(2 * (1 - x_i) + y_i, 2 * x_i + (1 - y_i)))
    w_in_buf, conv_all = _gather_wait("gather_wait_in_far", [w_in_buf, conv_buf], {1}, sems[0], proj, far)
    (w_int3,) = _gather_forward("gather_fwd_in_far", [w_in_buf], (2,))
    proj = _fwd_proj_more("fwd_proj_far", h, w_int3, proj, blocks(2 * (1 - x_i) + (1 - y_i)))
    w_int = w_int3.reshape(-1, D)
    conv_full = jnp.transpose(conv_all, (1, 0, 2)).reshape(3, H)
    og, o_pre, s_saved = _hgrn_fwd(proj, lower_bounds, hg_norm_g, H)
    landed = _gather_wait("gather_wait_mix", in_flight[1], set(), sems[1], og)
    fwd_sems, landed, token = _split_start("gather_fwd_mix_start", landed, 9, _forward_pairs)
    cb = _conv_fwd(proj, conv_full, H, token)
    wat3, wbt3, wout3 = _split_wait("gather_fwd_mix_wait", fwd_sems, landed, _forward_pairs, cb)
    wat, wbt, wout = wat3.reshape(D, H), wbt3.reshape(D, H), wout3.reshape(D, D)
    landed = _gather_wait("gather_wait_ffn", in_flight[2], set(), sems[2], cb)
    fwd_sems, landed, token = _split_start("gather_fwd_ffn_start", landed, 9, _forward_pairs)
    sig_a, sig_b, dm_dga, dm_dgb, merged, x1, h2 = _fwd_mix(og, cb, proj, x2d, wat, wbt, wout, norm_ffn_g,
                                                              H, token)
    wgt3, wut3, wd3 = _split_wait("gather_fwd_ffn_wait", fwd_sems, landed, _forward_pairs, h2)
    d_ff = N_CHIPS * wd3.shape[1]
    wgt, wut, wd = wgt3.reshape(d_ff, D), wut3.reshape(d_ff, D), wd3.reshape(d_ff, D)
    ffn_ds_da, ffn_ds_db, ffn_s = _fwd_ffn_up(h2, wgt, wut)
    dx2, dx2b, red_final = _fwd_down_loss(ffn_s, wd, x1, target, g_final)

    c_idx = core.reshape(1).astype(jnp.int32)
    place_idx = jnp.stack([chip, core]).astype(jnp.int32)

    def sibling_start(tag, grads):
        bufs = [lax.empty((N_CHIPS, g.shape[1] // 2, g.shape[2]), F32) for g in grads]
        return _split_start("rs_sibling_start_" + tag, list(grads) + bufs, len(grads), _sibling_pairs)

    def ici_start(tag, js, grads, from_sibling):
        partials = list(_rs_add("rs_add_" + tag, grads, from_sibling, c_idx))
        landings = [lax.empty((3,) + p.shape[1:], BF16) for p in partials]
        return _split_start("rs_ici_start_" + tag, partials + landings, 3 * len(js), _ici_pairs)

    def ici_start_behind(tag, js, started, after):
        n = len(js)
        arrays = _split_wait("rs_sibling_wait_" + tag, started[0], started[1], _sibling_pairs, after)
        return ici_start(tag, js, arrays[:n], arrays[n:])

    def sums(tag, started, after):
        partials, received = [], []
        for group, group_js, start in started:
            arrays = _split_wait("rs_ici_wait_" + group, start[0], start[1], _ici_pairs, after)
            partials += arrays[:len(group_js)]
            received += arrays[len(group_js):]
        return list(_rs_sum("rs_sum_" + tag, partials, received, place_idx))

    def adamw(tag, js, grads):
        return _adamw("adamw_" + tag, grads, *[[src[j] for j in js] for src in (big, big_m, big_v)])

    shards3 = lambda g: g.reshape(N_CHIPS, d_ff // N_CHIPS, D)
    da, db = _bwd_down(dx2b, wd, ffn_ds_da, ffn_ds_db)
    g_wd = shards3(_dw_rows2("dw_ffn_down", ffn_s, dx2b))
    g_wg = shards3(_dw_rows2("dw_ffn_gate", da, h2))
    g_wu = shards3(_dw_rows2("dw_ffn_up", db, h2))
    ffn_sibling = sibling_start("ffn", [g_wg, g_wu, g_wd])
    dx1, dx1b, red_ffn = _bwd_ffn_dh(da, db, wgt, wut, x1, dx2, norm_ffn_g, ffn_sibling[2])
    ffn_ici = ici_start_behind("ffn", [4, 5, 6], ffn_sibling, dx1b)
    dya, dyb, dproj, d_o, d_cb = _bwd_mix(dx1b, sig_a, sig_b, dm_dga, dm_dgb, wat, wbt, wout, H, ffn_ici[2])
    g_wout = _dw_rows("dw_out", merged, dx1b)
    g_wa = _dw_cols("dw_branch_a", og, dya, D // N_CHIPS)
    g_wb = _dw_cols("dw_branch_b", cb, dyb, D // N_CHIPS)
    mix_sibling = sibling_start("mix", [g_wa, g_wb, g_wout])
    dproj, red_hg = _hgrn_bwd(proj, lower_bounds, hg_norm_g, o_pre, d_o, s_saved, H, mix_sibling[2], dproj)
    mix_ici = ici_start_behind("mix", [1, 2, 3], mix_sibling, red_hg)
    dproj, g_conv = _conv_bwd(proj, conv_full, d_cb, H, mix_ici[2], dproj)
    g_win = _dw_in(h, dproj, w_int3.shape[1])
    in_sibling = sibling_start("in", [g_win])
    halves = sums("rest", [("mix", [1, 2, 3], mix_ici), ("ffn", [4, 5, 6], ffn_ici)], in_sibling[2])
    rest_share = _split_start("rs_share_start_rest", halves, len(halves), _share_pairs)
    in_ici = ici_start_behind("in", [0], in_sibling, rest_share[2])
    grad_x, red_mix = _bwd_in(dproj, w_int, x2d, dx1, norm_mix_g, in_ici[2])
    in_share = _split_start("rs_share_start_in", sums("in", [("in", [0], in_ici)], grad_x), 1, _share_pairs)
    small_block = _small_pack(red_mix, red_ffn, red_final, red_hg, g_conv)
    small = _split_start("small_gather_start", [small_block, lax.empty((8,) + small_block.shape, F32)], 7,
                         _small_pairs)
    rest_grads = _split_wait("rs_share_wait_rest", rest_share[0], rest_share[1], _share_pairs, small[2])
    big_out = [None] + adamw("rest", [1, 2, 3, 4, 5, 6], rest_grads)
    in_grad = _split_wait("rs_share_wait_in", in_share[0], in_share[1], _share_pairs, big_out[6][0])
    big_out[0] = adamw("in", [0], in_grad)[0]
    small_block, small_all = _split_wait("small_gather_wait", small[0], small[1], _small_pairs, big_out[0][0])

    def smalls(mix, lb, hg, cw, ffn, fin):
        return [mix, lb, hg, cw[0], ffn, fin.reshape(1, D)]

    small_out = _small_update(
        small_block, small_all, jnp.stack([chip, 4 * x_i + 2 * y_i + core]).astype(jnp.int32),
        smalls(norm_mix_g, lower_bounds, hg_norm_g, conv_w, norm_ffn_g, norm_final_g),
        smalls(m_norm_mix_g, m_lower_bounds, m_hg_norm_g, m_conv_w, m_norm_ffn_g, m_norm_final_g),
        smalls(v_norm_mix_g, v_lower_bounds, v_hg_norm_g, v_conv_w, v_norm_ffn_g, v_norm_final_g))

    def outputs(i):
        big_i = [big_out[j][i] for j in range(7)]
        mix, lb, hg, cw, ffn, fin = [small_out[4 * p + i] for p in range(6)]
        return [mix, big_i[0][None], lb, hg, cw[None], big_i[1][None], big_i[2][None], big_i[3][None], ffn,
                big_i[4].T[None], big_i[5].T[None], big_i[6][None], fin.reshape(D)]

    outs = [small_out[24][0, 0], grad_x.reshape(1, L, D)]
    for i in range(4):
        outs += outputs(i)
    return tuple(outs)
```

```python
import functools

import jax
import jax.numpy as jnp
from jax import lax
from jax.experimental import pallas as pl
from jax.experimental.pallas import tpu as pltpu

F32 = jnp.float32
BF16 = jnp.bfloat16
EPS = 1e-6
CHUNK = 32
HEAD_DIM = 128
LANES = 128
N_CHIPS = 4
N_SMALL_ROWS = 16
DPROJ_BLOCKS = 12
DPROJ_BLOCK_OF = (0, 1, 2, 3, 8, 9, 10, 4, 5, 6, 7)

ADAM_LR = 0.001
ADAM_B1 = 0.9
ADAM_B2 = 0.999
ADAM_EPS = 1e-08
ADAM_WD = 0.01
ADAM_STEP = 10

MESH = pl.DeviceIdType.MESH
ANY = pl.BlockSpec(memory_space=pl.ANY)
VMEM = pl.BlockSpec(memory_space=pltpu.VMEM)
HBM = pl.BlockSpec(memory_space=pltpu.HBM)
SEM = pl.BlockSpec(memory_space=pltpu.SEMAPHORE)
EFFECT = pltpu.SideEffectType.DATAFLOW_SIDE_EFFECTING


def _sds(shape, dtype):
    return jax.ShapeDtypeStruct(shape, dtype)


def _pallas_call(body, pin=True, **kwargs):
    if not pin:
        return pl.pallas_call(body, **kwargs)
    in_hbm = lambda s: pltpu.HBM(s.shape, s.dtype) if isinstance(s, jax.ShapeDtypeStruct) else s
    kwargs["out_shape"] = jax.tree.map(in_hbm, kwargs["out_shape"])
    call = pl.pallas_call(body, **kwargs)

    def run(*args):
        return call(*[pltpu.with_memory_space_constraint(a, pltpu.HBM) if a.dtype in (F32, BF16) else a
                      for a in args])

    return run


def _params(semantics, vmem_mb):
    return pltpu.CompilerParams(dimension_semantics=semantics, vmem_limit_bytes=vmem_mb << 20)


def _nn(a, b):
    return lax.dot_general(a, b, (((1,), (0,)), ((), ())), preferred_element_type=F32)


def _nt(a, b):
    return lax.dot_general(a, b, (((1,), (1,)), ((), ())), preferred_element_type=F32)


def _tn(a, b):
    return lax.dot_general(a, b, (((0,), (0,)), ((), ())), preferred_element_type=F32)


def _sigmoid(x):
    return jax.nn.sigmoid(x)


def _rms_stats(x):
    r = lax.rsqrt(jnp.mean(x * x, axis=-1, keepdims=True) + EPS)
    return r, x * r


def _rms_bwd(dxh, xh, r):
    return r * (dxh - xh * jnp.mean(dxh * xh, axis=-1, keepdims=True))


def _fwd_proj_first(x, g_mix, w_int3, block):
    L, D = x.shape
    tn = w_int3.shape[1]
    tm = min(L, 1024)

    def body(blk_ref, x_ref, g_ref, w_ref, h_ref, p_ref):
        _, xh = _rms_stats(x_ref[...])
        h = (xh * g_ref[...]).astype(BF16)
        h_ref[...] = h
        p_ref[...] = _nt(h, w_ref[...])

    return _pallas_call(
        body, name="fwd_proj_own",
        grid_spec=pltpu.PrefetchScalarGridSpec(
            num_scalar_prefetch=1, grid=(L // tm,),
            in_specs=[pl.BlockSpec((tm, D), lambda i, blk: (i, 0)),
                      pl.BlockSpec((1, D), lambda i, blk: (0, 0)),
                      pl.BlockSpec((None, tn, D), lambda i, blk: (blk[0], 0, 0))],
            out_specs=[pl.BlockSpec((tm, D), lambda i, blk: (i, 0)),
                       pl.BlockSpec((tm, tn), lambda i, blk: (i, blk[0]))]),
        out_shape=[_sds((L, D), BF16), _sds((L, N_CHIPS * tn), F32)],
        compiler_params=_params(("parallel",), 48),
    )(block, x, g_mix, w_int3)


def _fwd_proj_more(name, h, w_int3, proj, blocks):
    L, D = h.shape
    tn = w_int3.shape[1]
    tm = min(L, 1024)

    def body(blk_ref, h_ref, w_ref, proj_ref, p_ref):
        p_ref[...] = _nt(h_ref[...], w_ref[...])

    return _pallas_call(
        body, name=name,
        grid_spec=pltpu.PrefetchScalarGridSpec(
            num_scalar_prefetch=1, grid=(L // tm, blocks.shape[0]),
            in_specs=[pl.BlockSpec((tm, D), lambda i, j, blk: (i, 0)),
                      pl.BlockSpec((None, tn, D), lambda i, j, blk: (blk[j], 0, 0)), ANY],
            out_specs=pl.BlockSpec((tm, tn), lambda i, j, blk: (i, blk[j]))),
        out_shape=_sds(proj.shape, proj.dtype),
        input_output_aliases={3: 0},
        compiler_params=_params(("parallel", "arbitrary"), 48),
    )(blocks, h, w_int3, proj)


def _lower_bound(lbp):
    l0, l1 = lbp[0:1, :], lbp[1:2, :]
    m = jnp.maximum(l0, l1)
    e0, e1 = jnp.exp(l0 - m), jnp.exp(l1 - m)
    return e0 / (e0 + e1)


def _seg_scan(x, r32, forward):
    n = x.shape[0]
    s = 1
    while s < CHUNK:
        if forward:
            x = x + jnp.where(r32 >= s, pltpu.roll(x, s, 0), 0.0)
        else:
            x = x + jnp.where(r32 < CHUNK - s, pltpu.roll(x, n - s, 0), 0.0)
        s *= 2
    return x


def _bcast_row(x, row):
    n, w = x.shape
    nc = n // CHUNK
    x3 = x.reshape(nc, CHUNK, w)
    return jnp.broadcast_to(x3[:, row:row + 1, :], (nc, CHUNK, w)).reshape(n, w)


def _chunk_total(x):
    n, w = x.shape
    nc = n // CHUNK
    total = jnp.sum(x.reshape(nc, CHUNK, w), axis=1, keepdims=True)
    return jnp.broadcast_to(total, (nc, CHUNK, w)).reshape(n, w)


def _hgrn_prep(q_raw, f_raw, lb):
    r32 = lax.broadcasted_iota(jnp.int32, f_raw.shape, 0) & (CHUNK - 1)
    sig = _sigmoid(f_raw)
    f = lb + (1.0 - lb) * sig
    b = _seg_scan(jnp.log(f), r32, True)
    a = _bcast_row(b, CHUNK // 2 - 1)
    bl = _bcast_row(b, CHUNK - 1)
    sq = _sigmoid(q_raw)
    q = q_raw * sq * (HEAD_DIM ** -0.5)
    return dict(r32=r32, sig=sig, f=f, k=1.0 - f, b=b, a=a, bl=bl, sq=sq, q=q)


def _chunk_masks(n):
    ri = lax.broadcasted_iota(jnp.int32, (n, n), 0)
    ci = lax.broadcasted_iota(jnp.int32, (n, n), 1)
    same = (ri // CHUNK) == (ci // CHUNK)
    return same & (ci <= ri), same & (ri <= ci)


def _hgrn_fwd(proj, lower_bounds, gamma, H):
    L = proj.shape[0]
    nh = H // HEAD_DIM
    TL = min(L, 256)
    nc = TL // CHUNK

    def body(q_ref, f_ref, v_ref, g_ref, lbp_ref, gam_ref, og_ref, o_ref, s_ref, st_ref):
        @pl.when(pl.program_id(0) == 0)
        def _():
            st_ref[...] = jnp.zeros_like(st_ref)

        lb = _lower_bound(lbp_ref[...])
        gam = gam_ref[...]
        mask, _ = _chunk_masks(TL)
        rowc = lax.broadcasted_iota(jnp.int32, (TL, HEAD_DIM), 0) // CHUNK
        for h in range(nh):
            hs = slice(h * HEAD_DIM, (h + 1) * HEAD_DIM)
            p = _hgrn_prep(q_ref[:, hs], f_ref[:, hs], lb[:, hs])
            v = v_ref[:, hs]
            vb = v.astype(BF16)
            vt = v.T.astype(BF16)
            q_hat = (p["q"] * jnp.exp(p["b"] - p["a"])).astype(BF16)
            k_hat = (p["k"] * jnp.exp(p["a"] - p["b"])).astype(BF16)
            q_in = (p["q"] * jnp.exp(p["b"])).astype(BF16)
            k_out = (p["k"] * jnp.exp(p["bl"] - p["b"])).astype(BF16)
            dec = jnp.exp(p["bl"])
            att = jnp.where(mask, _nt(q_hat, k_hat), 0.0).astype(BF16)
            o_intra = _nn(att, vb)
            st = st_ref[h]
            for c in range(nc):
                rs = slice(c * CHUNK, (c + 1) * CHUNK)
                stb = st.astype(BF16)
                s_ref[c, h] = stb
                o_ref[rs, hs] = o_intra[rs] + _nt(q_in[rs], stb)
                k_c = jnp.where(rowc == c, k_out, jnp.zeros_like(k_out))
                st = st * dec[c * CHUNK:c * CHUNK + 1, :] + _nn(vt, k_c)
            st_ref[h] = st
            o = o_ref[:, hs]
            _, xh = _rms_stats(o)
            gr = g_ref[:, hs]
            og_ref[:, hs] = (xh * gam * (gr * _sigmoid(gr))).astype(BF16)

    col = lambda k: pl.BlockSpec((TL, H), lambda i, k=k: (i, k))
    return _pallas_call(
        body, name="hgrn_fwd", grid=(L // TL,),
        in_specs=[col(0), col(1), col(2), col(3),
                  pl.BlockSpec(lower_bounds.shape, lambda i: (0, 0)),
                  pl.BlockSpec(gamma.shape, lambda i: (0, 0))],
        out_specs=[pl.BlockSpec((TL, H), lambda i: (i, 0)),
                   pl.BlockSpec((TL, H), lambda i: (i, 0)),
                   pl.BlockSpec((nc, nh, HEAD_DIM, HEAD_DIM), lambda i: (i, 0, 0, 0))],
        out_shape=[_sds((L, H), BF16), _sds((L, H), F32),
                   _sds((L // CHUNK, nh, HEAD_DIM, HEAD_DIM), BF16)],
        scratch_shapes=[pltpu.VMEM((nh, HEAD_DIM, HEAD_DIM), F32)],
        compiler_params=_params(("arbitrary",), 48),
    )(proj, proj, proj, proj, lower_bounds, gamma)


def _hgrn_bwd(proj, lower_bounds, gamma, o_pre, d_out, s_saved, H, after, dproj):
    L = proj.shape[0]
    nh = H // HEAD_DIM
    TL = min(L, 256)
    nc = TL // CHUNK
    nt = L // TL

    def body(q_ref, f_ref, v_ref, g_ref, lbp_ref, gam_ref, o_ref, d_ref, s_ref, after_ref, dproj_ref,
             dp_ref, red_ref, dst_ref, dsall_ref, tmp_ref):
        @pl.when(pl.program_id(0) == 0)
        def _():
            dst_ref[...] = jnp.zeros_like(dst_ref)
            red_ref[...] = jnp.zeros_like(red_ref)

        lb = _lower_bound(lbp_ref[...])
        gam = gam_ref[...]
        mask, mask_t = _chunk_masks(TL)
        rowc = lax.broadcasted_iota(jnp.int32, (TL, HEAD_DIM), 0) // CHUNK
        for h in range(nh):
            hs = slice(h * HEAD_DIM, (h + 1) * HEAD_DIM)
            qr, gr, lbh = q_ref[:, hs], g_ref[:, hs], lb[:, hs]
            p = _hgrn_prep(qr, f_ref[:, hs], lbh)
            vb = v_ref[:, hs].astype(BF16)
            eba, eab = jnp.exp(p["b"] - p["a"]), jnp.exp(p["a"] - p["b"])
            eb, elb = jnp.exp(p["b"]), jnp.exp(p["bl"] - p["b"])
            dec = jnp.exp(p["bl"])
            q_hat, k_hat = p["q"] * eba, p["k"] * eab
            q_in, k_out = p["q"] * eb, p["k"] * elb
            q_hat_b, k_hat_b = q_hat.astype(BF16), k_hat.astype(BF16)
            q_in_b, k_out_b = q_in.astype(BF16), k_out.astype(BF16)

            o, dout = o_ref[:, hs], d_ref[:, hs]
            sg = _sigmoid(gr)
            r, xh = _rms_stats(o)
            dp_ref[3, :, hs] = (dout * (xh * gam) * (sg * (1.0 + gr * (1.0 - sg)))).astype(BF16)
            dn = dout * (gr * sg)
            red_ref[1:2, hs] += jnp.sum(dn * xh, axis=0, keepdims=True)
            do = _rms_bwd(dn * gam, xh, r)
            dob = do.astype(BF16)
            dot_b = do.T.astype(BF16)

            att_t = jnp.where(mask_t, _nt(k_hat_b, q_hat_b), 0.0).astype(BF16)
            dv_intra = _nn(att_t, dob)
            datt = jnp.where(mask, _nt(dob, vb), 0.0).astype(BF16)
            dqh = _nn(datt, k_hat_b)
            datt_t = jnp.where(mask_t, _nt(vb, dob), 0.0).astype(BF16)
            dkh = _nn(datt_t, q_hat_b)

            dst = dst_ref[h]
            for c in reversed(range(nc)):
                dsall_ref[c] = dst
                q_c = jnp.where(rowc == c, q_in_b, jnp.zeros_like(q_in_b))
                dst = dst * dec[c * CHUNK:c * CHUNK + 1, :] + _nn(dot_b, q_c)
            dst_ref[h] = dst
            for c in range(nc):
                rs = slice(c * CHUNK, (c + 1) * CHUNK)
                ds_c = dsall_ref[c]
                dsb = ds_c.astype(BF16)
                st_prev = s_ref[c, h]
                tmp_ref[0, rs, :] = _nt(k_out_b[rs], dsb)
                tmp_ref[1, rs, :] = _nn(vb[rs], dsb)
                tmp_ref[2, rs, :] = _nn(dob[rs], st_prev)
                ddec = jnp.sum(ds_c * st_prev.astype(F32), axis=0, keepdims=True)
                tmp_ref[3, rs, :] = jnp.broadcast_to(ddec * dec[c * CHUNK:c * CHUNK + 1, :],
                                                     (CHUNK, HEAD_DIM))
            dko, dqi = tmp_ref[1], tmp_ref[2]
            dq = dqh * eba + dqi * eb
            dk = dkh * eab + dko * elb
            tko = dko * k_out
            db = dqh * q_hat - dkh * k_hat + dqi * q_in - tko
            dlog = _seg_scan(db, p["r32"], False) + _chunk_total(tko) + tmp_ref[3]
            df = dlog / p["f"] - dk
            sig = p["sig"]
            red_ref[0:1, hs] += jnp.sum(df * (1.0 - sig), axis=0, keepdims=True)
            dp_ref[1, :, hs] = (df * (1.0 - lbh) * sig * (1.0 - sig)).astype(BF16)
            sq = p["sq"]
            dp_ref[0, :, hs] = (dq * (HEAD_DIM ** -0.5) * (sq * (1.0 + qr * (1.0 - sq)))).astype(BF16)
            dp_ref[2, :, hs] = (dv_intra + tmp_ref[0]).astype(BF16)

    col = lambda k: pl.BlockSpec((TL, H), lambda i, k=k: (nt - 1 - i, k))
    rev = pl.BlockSpec((TL, H), lambda i: (nt - 1 - i, 0))
    return _pallas_call(
        body, name="hgrn_bwd", grid=(nt,),
        in_specs=[col(0), col(1), col(2), col(3),
                  pl.BlockSpec(lower_bounds.shape, lambda i: (0, 0)),
                  pl.BlockSpec(gamma.shape, lambda i: (0, 0)),
                  rev, rev,
                  pl.BlockSpec((nc, nh, HEAD_DIM, HEAD_DIM), lambda i: (nt - 1 - i, 0, 0, 0)), ANY, ANY],
        out_specs=[pl.BlockSpec((4, TL, H), lambda i: (0, nt - 1 - i, 0)), pl.BlockSpec((8, H), lambda i: (0, 0))],
        out_shape=[_sds(dproj.shape, BF16), _sds((8, H), F32)],
        input_output_aliases={10: 0},
        scratch_shapes=[pltpu.VMEM((nh, HEAD_DIM, HEAD_DIM), F32),
                        pltpu.VMEM((nc, HEAD_DIM, HEAD_DIM), F32),
                        pltpu.VMEM((4, TL, HEAD_DIM), F32)],
        compiler_params=_params(("arbitrary",), 48),
    )(proj, proj, proj, proj, lower_bounds, gamma, o_pre, d_out, s_saved, after, dproj)


def _shift_down(u, s, row):
    return jnp.where(row >= s, pltpu.roll(u, s, 0), 0.0)


def _shift_up(u, s, row):
    n = u.shape[0]
    return jnp.where(row < n - s, pltpu.roll(u, n - s, 0), 0.0)


def _conv_specs(L, H):
    per = H // LANES
    return [pl.BlockSpec((L, LANES), lambda j, o=o: (0, o * per + j)) for o in (4, 5, 6)]


def _conv_fwd(proj, conv_w, H, after):
    L = proj.shape[0]

    def body(c_ref, b_ref, x_ref, w_ref, after_ref, o_ref):
        row = lax.broadcasted_iota(jnp.int32, (L, LANES), 0)
        u = c_ref[...] * x_ref[...]
        w = w_ref[...]
        y = w[0:1] * _shift_down(u, 2, row) + w[1:2] * _shift_down(u, 1, row) + w[2:3] * u
        o_ref[...] = (b_ref[...] * y).astype(BF16)

    return _pallas_call(
        body, name="conv_fwd", grid=(H // LANES,),
        in_specs=_conv_specs(L, H) + [pl.BlockSpec((3, LANES), lambda j: (0, j)), ANY],
        out_specs=pl.BlockSpec((L, LANES), lambda j: (0, j)),
        out_shape=_sds((L, H), BF16),
        compiler_params=_params(("parallel",), 48),
    )(proj, proj, proj, conv_w, after)


def _conv_bwd(proj, conv_w, dcb, H, after, dproj):
    L = proj.shape[0]

    def body(c_ref, b_ref, x_ref, w_ref, d_ref, after_ref, dproj_ref, dp_ref, dw_ref):
        row = lax.broadcasted_iota(jnp.int32, (L, LANES), 0)
        cg, xb = c_ref[...], x_ref[...]
        u = cg * xb
        u1, u2 = _shift_down(u, 1, row), _shift_down(u, 2, row)
        w = w_ref[...]
        y = w[0:1] * u2 + w[1:2] * u1 + w[2:3] * u
        d = d_ref[...]
        dp_ref[1] = (d * y).astype(BF16)
        dy = d * b_ref[...]
        du = w[2:3] * dy + w[1:2] * _shift_up(dy, 1, row) + w[0:1] * _shift_up(dy, 2, row)
        dw_ref[0:1, :] = jnp.sum(dy * u2, axis=0, keepdims=True)
        dw_ref[1:2, :] = jnp.sum(dy * u1, axis=0, keepdims=True)
        dw_ref[2:3, :] = jnp.sum(dy * u, axis=0, keepdims=True)
        dp_ref[0] = (du * xb).astype(BF16)
        dp_ref[2] = (du * cg).astype(BF16)
        dp_ref[3] = jnp.zeros((L, LANES), BF16)

    blk = pl.BlockSpec((L, LANES), lambda j: (0, j))
    return _pallas_call(
        body, name="conv_bwd", grid=(H // LANES,),
        in_specs=_conv_specs(L, H) + [pl.BlockSpec((3, LANES), lambda j: (0, j)), blk, ANY, ANY],
        out_specs=[pl.BlockSpec((4, L, LANES), lambda j: (2, 0, j)), pl.BlockSpec((3, LANES), lambda j: (0, j))],
        out_shape=[_sds(dproj.shape, BF16), _sds((3, H), F32)],
        input_output_aliases={6: 0},
        compiler_params=_params(("parallel",), 56),
    )(proj, proj, proj, conv_w, dcb, after, dproj)


def _gate_specs(tm, H):
    return [pl.BlockSpec((tm, H), lambda i, k=k: (i, k)) for k in (7, 8, 9, 10)]


def _fwd_mix(og, cb, proj, x, wat, wbt, wout, g_ffn, H, after):
    L, D = x.shape
    tm = min(L, 512)

    def body(o_ref, cb_ref, ga0, ga1, gb0, gb1, x_ref, wa_ref, wb_ref, wo_ref, g_ref, after_ref,
             sa_ref, sb_ref, ta_ref, tb_ref, m_ref, x1_ref, h2_ref):
        ya, yb = _nt(o_ref[...], wa_ref[...]), _nt(cb_ref[...], wb_ref[...])
        for k, (gar, gbr) in enumerate(((ga0, gb0), (ga1, gb1))):
            cs = slice(k * H, (k + 1) * H)
            sa, sb = _sigmoid(gar[...]), _sigmoid(gbr[...])
            ma, mb = sa * ya[:, cs], sb * yb[:, cs]
            m_ref[:, cs] = (ma + mb).astype(BF16)
            sa_ref[:, cs] = sa.astype(BF16)
            sb_ref[:, cs] = sb.astype(BF16)
            ta_ref[:, cs] = (ma * (1.0 - sa)).astype(BF16)
            tb_ref[:, cs] = (mb * (1.0 - sb)).astype(BF16)
        x1 = x_ref[...] + _nn(m_ref[...], wo_ref[...])
        x1_ref[...] = x1
        _, xh = _rms_stats(x1)
        h2_ref[...] = (xh * g_ref[...]).astype(BF16)

    row = lambda w: pl.BlockSpec((tm, w), lambda i: (i, 0))
    full = lambda a: pl.BlockSpec(a.shape, lambda i: (0,) * a.ndim)
    return _pallas_call(
        body, name="fwd_mix", grid=(L // tm,),
        in_specs=[row(H), row(H)] + _gate_specs(tm, H) + [row(D), full(wat), full(wbt), full(wout),
                                                           full(g_ffn), ANY],
        out_specs=[row(D)] * 7,
        out_shape=[_sds((L, D), BF16)] * 5 + [_sds((L, D), F32), _sds((L, D), BF16)],
        compiler_params=_params(("parallel",), 56),
    )(og, cb, proj, proj, proj, proj, x, wat, wbt, wout, g_ffn, after)


def _bwd_mix(dx1b, sig_a, sig_b, dm_dga, dm_dgb, wat, wbt, wout, H, after):
    L, D = dx1b.shape
    tm = min(L, 512)

    def body(dx_ref, sa_ref, sb_ref, ta_ref, tb_ref, wa_ref, wb_ref, wo_ref, after_ref,
             dya_ref, dyb_ref, dgate_ref, do_ref, dcb_ref):
        dm = _nt(dx_ref[...], wo_ref[...])
        dga = (dm * ta_ref[...].astype(F32)).astype(BF16)
        dgb = (dm * tb_ref[...].astype(F32)).astype(BF16)
        for q, part in enumerate((dga[:, 0:H], dga[:, H:D], dgb[:, 0:H], dgb[:, H:D])):
            dgate_ref[q] = part
        dya_ref[...] = (dm * sa_ref[...].astype(F32)).astype(BF16)
        dyb_ref[...] = (dm * sb_ref[...].astype(F32)).astype(BF16)
        do_ref[...] = _nn(dya_ref[...], wa_ref[...])
        dcb_ref[...] = _nn(dyb_ref[...], wb_ref[...])

    row = lambda w: pl.BlockSpec((tm, w), lambda i: (i, 0))
    full = lambda a: pl.BlockSpec(a.shape, lambda i: (0,) * a.ndim)
    return _pallas_call(
        body, name="bwd_mix", grid=(L // tm,),
        in_specs=[row(D)] * 5 + [full(wat), full(wbt), full(wout), ANY],
        out_specs=[row(D), row(D), pl.BlockSpec((4, tm, H), lambda i: (1, i, 0)), row(H), row(H)],
        out_shape=[_sds((L, D), BF16)] * 2 + [_sds((DPROJ_BLOCKS, L, H), BF16)] + [_sds((L, H), F32)] * 2,
        compiler_params=_params(("parallel",), 56),
    )(dx1b, sig_a, sig_b, dm_dga, dm_dgb, wat, wbt, wout, after)


def _fwd_ffn_up(h2, wgt, wut):
    L, D = h2.shape
    F = wgt.shape[0]
    tn = F // 2
    tm = min(L, 512)

    def body(h_ref, wg_ref, wu_ref, sa_ref, sb_ref, s_ref):
        h = h_ref[...]
        a, b = _nt(h, wg_ref[...]), _nt(h, wu_ref[...])
        sg = _sigmoid(a)
        silu = a * sg
        sa_ref[...] = (b * sg * (1.0 + a * (1.0 - sg))).astype(BF16)
        sb_ref[...] = silu.astype(BF16)
        s_ref[...] = (silu * b).astype(BF16)

    wspec = pl.BlockSpec((tn, D), lambda j, i: (j, 0))
    ospec = pl.BlockSpec((tm, tn), lambda j, i: (i, j))
    return _pallas_call(
        body, name="fwd_ffn_up", grid=(2, L // tm),
        in_specs=[pl.BlockSpec((tm, D), lambda j, i: (i, 0)), wspec, wspec],
        out_specs=[ospec] * 3,
        out_shape=[_sds((L, F), BF16)] * 3,
        compiler_params=_params(("parallel", "parallel"), 48),
    )(h2, wgt, wut)


def _fwd_down_loss(s, wd, x1, target, g_final):
    L, D = x1.shape
    F = wd.shape[0]
    tm = min(L, 512)

    def body(s_ref, wd_ref, x1_ref, t_ref, g_ref, dx_ref, dxb_ref, red_ref):
        @pl.when(pl.program_id(0) == 0)
        def _():
            red_ref[...] = jnp.zeros_like(red_ref)

        g = g_ref[...]
        r, xh = _rms_stats(x1_ref[...] + _nn(s_ref[...], wd_ref[...]))
        e = xh * g - t_ref[...]
        dy = e * (1.0 / D)
        dx = _rms_bwd(dy * g, xh, r)
        dx_ref[...] = dx
        dxb_ref[...] = dx.astype(BF16)
        red_ref[0:1, :] += jnp.sum(dy * xh, axis=0, keepdims=True)
        red_ref[1:2, :] += jnp.broadcast_to(0.5 * jnp.sum(e * e) * (1.0 / D), (1, D))

    row = pl.BlockSpec((tm, D), lambda i: (i, 0))
    return _pallas_call(
        body, name="fwd_down_loss", grid=(L // tm,),
        in_specs=[pl.BlockSpec((tm, F), lambda i: (i, 0)), pl.BlockSpec((F, D), lambda i: (0, 0)),
                  row, row, pl.BlockSpec((1, D), lambda i: (0, 0))],
        out_specs=[row, row, pl.BlockSpec((8, D), lambda i: (0, 0))],
        out_shape=[_sds((L, D), F32), _sds((L, D), BF16), _sds((8, D), F32)],
        compiler_params=_params(("arbitrary",), 56),
    )(s, wd, x1, target, g_final)


def _bwd_down(dx2b, wd, s_a, s_b):
    L, D = dx2b.shape
    F = wd.shape[0]
    tn = F // 2
    tm = min(L, 512)

    def body(dx_ref, wd_ref, sa_ref, sb_ref, da_ref, db_ref):
        ds = _nt(dx_ref[...], wd_ref[...])
        da_ref[...] = (ds * sa_ref[...].astype(F32)).astype(BF16)
        db_ref[...] = (ds * sb_ref[...].astype(F32)).astype(BF16)

    ospec = pl.BlockSpec((tm, tn), lambda j, i: (i, j))
    return _pallas_call(
        body, name="bwd_down", grid=(2, L // tm),
        in_specs=[pl.BlockSpec((tm, D), lambda j, i: (i, 0)),
                  pl.BlockSpec((tn, D), lambda j, i: (j, 0)), ospec, ospec],
        out_specs=[ospec] * 2,
        out_shape=[_sds((L, F), BF16)] * 2,
        compiler_params=_params(("parallel", "parallel"), 48),
    )(dx2b, wd, s_a, s_b)


def _bwd_ffn_dh(da, db, wgt, wut, x1, dx2, g_ffn, after):
    L, D = x1.shape
    F = wgt.shape[0]
    tm = min(L, 256)

    def body(da_ref, db_ref, wg_ref, wu_ref, x1_ref, dx2_ref, g_ref, after_ref, dx_ref, dxb_ref, red_ref):
        @pl.when(pl.program_id(0) == 0)
        def _():
            red_ref[...] = jnp.zeros_like(red_ref)

        dh = _nn(da_ref[...], wg_ref[...]) + _nn(db_ref[...], wu_ref[...])
        r, xh = _rms_stats(x1_ref[...])
        red_ref[0:1, :] += jnp.sum(dh * xh, axis=0, keepdims=True)
        dx = dx2_ref[...] + _rms_bwd(dh * g_ref[...], xh, r)
        dx_ref[...] = dx
        dxb_ref[...] = dx.astype(BF16)

    row = pl.BlockSpec((tm, D), lambda i: (i, 0))
    aspec = pl.BlockSpec((tm, F), lambda i: (i, 0))
    wspec = pl.BlockSpec((F, D), lambda i: (0, 0))
    return _pallas_call(
        body, name="bwd_ffn_dh", grid=(L // tm,),
        in_specs=[aspec, aspec, wspec, wspec, row, row, pl.BlockSpec((1, D), lambda i: (0, 0)), ANY],
        out_specs=[row, row, pl.BlockSpec((8, D), lambda i: (0, 0))],
        out_shape=[_sds((L, D), F32), _sds((L, D), BF16), _sds((8, D), F32)],
        compiler_params=_params(("arbitrary",), 56),
    )(da, db, wgt, wut, x1, dx2, g_ffn, after)


def _bwd_in(dproj, w_int, x, dx1, g_mix, after):
    L, D = x.shape
    N = w_int.shape[0]
    H = dproj.shape[2]
    tm = min(L, 256)
    assert N == len(DPROJ_BLOCK_OF) * H

    def body(blocks_ref, w_ref, x_ref, dx1_ref, g_ref, after_ref, dx_ref, red_ref, dp_ref):
        @pl.when(pl.program_id(0) == 0)
        def _():
            red_ref[...] = jnp.zeros_like(red_ref)

        for t, block in enumerate(DPROJ_BLOCK_OF):
            dp_ref[:, t * H:(t + 1) * H] = blocks_ref[block]
        dh = _nn(dp_ref[...], w_ref[...])
        r, xh = _rms_stats(x_ref[...])
        red_ref[0:1, :] += jnp.sum(dh * xh, axis=0, keepdims=True)
        dx_ref[...] = dx1_ref[...] + _rms_bwd(dh * g_ref[...], xh, r)

    row = pl.BlockSpec((tm, D), lambda i: (i, 0))
    return _pallas_call(
        body, name="bwd_in", grid=(L // tm,),
        in_specs=[pl.BlockSpec((DPROJ_BLOCKS, tm, H), lambda i: (0, i, 0)), pl.BlockSpec((N, D), lambda i: (0, 0)),
                  row, row, pl.BlockSpec((1, D), lambda i: (0, 0)), ANY],
        out_specs=[row, pl.BlockSpec((8, D), lambda i: (0, 0))],
        out_shape=[_sds((L, D), F32), _sds((8, D), F32)],
        scratch_shapes=[pltpu.VMEM((tm, N), BF16)],
        compiler_params=_params(("arbitrary",), 56),
    )(dproj, w_int, x, dx1, g_mix, after)


def _dw_in(h, dproj, n_cols):
    L, D = h.shape
    H = dproj.shape[2]
    tk = min(L, TK_TOKENS)
    first = [(j * n_cols) // H for j in range(N_CHIPS)]
    last = [((j + 1) * n_cols - 1) // H for j in range(N_CHIPS)]
    slots = max(b - a for a, b in zip(first, last)) + 1
    plan = []
    for j in range(N_CHIPS):
        lo, hi = j * n_cols, (j + 1) * n_cols
        segments = []
        for s in range(last[j] - first[j] + 1):
            a, b = max(lo, (first[j] + s) * H), min(hi, (first[j] + s + 1) * H)
            segments.append((s, a - (first[j] + s) * H, b - a, a - lo))
        plan.append(segments)

    def body(*refs):
        h_ref, slot_refs, o_ref, b_ref = refs[0], refs[1:1 + slots], refs[1 + slots], refs[2 + slots]
        j, k = pl.program_id(0), pl.program_id(1)
        for jj in range(N_CHIPS):
            @pl.when(j == jj)
            def _(jj=jj):
                for s, start, width, at in plan[jj]:
                    b_ref[:, at:at + width] = slot_refs[s][:, start:start + width]

        part = _tn(h_ref[...], b_ref[...])

        @pl.when(k == 0)
        def _():
            o_ref[...] = part

        @pl.when(k > 0)
        def _():
            o_ref[...] += part

    def slot_spec(s):
        blocks = [DPROJ_BLOCK_OF[min(first[j] + s, last[j])] for j in range(N_CHIPS)]

        def index(j, k):
            block = blocks[0]
            for jj in range(1, N_CHIPS):
                block = jnp.where(j == jj, blocks[jj], block)
            return (block, k, 0)

        return pl.BlockSpec((None, tk, H), index)

    return _pallas_call(
        body, name="dw_in", grid=(N_CHIPS, L // tk),
        in_specs=[pl.BlockSpec((tk, D), lambda j, k: (k, 0))] + [slot_spec(s) for s in range(slots)],
        out_specs=pl.BlockSpec((None, D, n_cols), lambda j, k: (j, 0, 0)),
        out_shape=_sds((N_CHIPS, D, n_cols), F32),
        scratch_shapes=[pltpu.VMEM((tk, n_cols), BF16)],
        compiler_params=_params(("parallel", "arbitrary"), 56),
    )(h, *([dproj] * slots))


def _mm_tn(name, a, b, a_spec, b_spec, o_block, n_out, n_k):
    def body(a_ref, b_ref, o_ref):
        part = _tn(a_ref[...], b_ref[...])

        @pl.when(pl.program_id(1) == 0)
        def _():
            o_ref[...] = part

        @pl.when(pl.program_id(1) > 0)
        def _():
            o_ref[...] += part

    return _pallas_call(
        body, name=name, grid=(n_out, n_k),
        in_specs=[a_spec, b_spec],
        out_specs=pl.BlockSpec((None,) + o_block, lambda j, k: (j, 0, 0)),
        out_shape=_sds((n_out,) + o_block, F32),
        compiler_params=_params(("parallel", "arbitrary"), 56),
    )(a, b)


TK_TOKENS = 2048


def _dw_cols(name, a, b, n_cols):
    L, M = a.shape
    tk = min(L, TK_TOKENS)
    return _mm_tn(name, a, b, pl.BlockSpec((tk, M), lambda j, k: (k, 0)),
                  pl.BlockSpec((tk, n_cols), lambda j, k: (k, j)), (M, n_cols), N_CHIPS, L // tk)


def _dw_rows(name, a, b):
    L, M = a.shape
    N = b.shape[1]
    tk = min(L, TK_TOKENS)
    return _mm_tn(name, a, b, pl.BlockSpec((tk, M // N_CHIPS), lambda j, k: (k, j)),
                  pl.BlockSpec((tk, N), lambda j, k: (k, 0)), (M // N_CHIPS, N), N_CHIPS, L // tk)


def _dw_rows2(name, a, b):
    L, M = a.shape
    N = b.shape[1]
    tk = min(L, TK_TOKENS)
    return _mm_tn(name, a, b, pl.BlockSpec((tk, M // 2), lambda j, k: (k, j)),
                  pl.BlockSpec((tk, N), lambda j, k: (k, 0)), (M // 2, N), 2, L // tk)


def _place():
    x, y, c = lax.axis_index("x"), lax.axis_index("y"), lax.axis_index("c")
    chips = [(1 - x, y), (x, 1 - y), (1 - x, 1 - y)]
    return x, y, c, 2 * x + y, chips


def _remote(src, dst, send_sem, recv_sem, device):
    return pltpu.make_async_remote_copy(src_ref=src, dst_ref=dst, send_sem=send_sem,
                                        recv_sem=recv_sem, device_id=device, device_id_type=MESH)


def _half(ref, lead, c, r2):
    return ref.at[lead, pl.ds(pl.multiple_of(c * r2, 16), r2), :]


def _cast_place(name, ws, chip_idx):
    n = len(ws)

    def body(k_ref, *refs):
        for w_ref, o_ref in zip(refs[:n], refs[n:]):
            o_ref[...] = w_ref[...].astype(BF16)

    return _pallas_call(
        body, name=name,
        grid_spec=pltpu.PrefetchScalarGridSpec(
            num_scalar_prefetch=1, grid=(2,),
            in_specs=[pl.BlockSpec((w.shape[0] // 2, w.shape[1]), lambda i, k_ref: (i, 0)) for w in ws],
            out_specs=[pl.BlockSpec((None, w.shape[0] // 2, w.shape[1]), lambda i, k_ref: (k_ref[0], i, 0))
                       for w in ws]),
        out_shape=[_sds((N_CHIPS,) + w.shape, BF16) for w in ws],
        compiler_params=_params(("parallel",), 48),
    )(chip_idx, *ws)


def _cast_place_t(name, ws, chip_idx):
    n = len(ws)
    r, cols = ws[0].shape

    def body(k_ref, *refs):
        for w_ref, o_ref in zip(refs[:n], refs[n:]):
            o_ref[...] = w_ref[...].T.astype(BF16)

    return _pallas_call(
        body, name=name,
        grid_spec=pltpu.PrefetchScalarGridSpec(
            num_scalar_prefetch=1, grid=(cols // LANES,),
            in_specs=[pl.BlockSpec((r, LANES), lambda i, k_ref: (0, i))] * n,
            out_specs=[pl.BlockSpec((None, LANES, r), lambda i, k_ref: (k_ref[0], i, 0))] * n),
        out_shape=[_sds((N_CHIPS, cols, r), BF16)] * n,
        compiler_params=_params(("parallel",), 48),
    )(chip_idx, *ws)


def _gather_copies(bufs, whole, send_sems, recv_sems, select=None):
    x, y, c, k, chips = _place()
    pairs = []
    for w, buf in enumerate(bufs):
        for j, (cx, cy) in enumerate(chips):
            if select is not None and not select(w, j):
                continue
            if w in whole:
                mine, theirs = buf.at[k], buf.at[2 * cx + cy]
            else:
                r2 = buf.shape[1] // 2
                mine, theirs = _half(buf, k, c, r2), _half(buf, 2 * cx + cy, c, r2)
            sems = (send_sems.at[w * 3 + j], recv_sems.at[w * 3 + j])
            pairs.append((_remote(mine, mine, *sems, (cx, cy, c)), _remote(theirs, theirs, *sems, (x, y, c))))
    return pairs


def _gather_start(name, groups, after):
    flat = [b for bufs, _, _ in groups for b in bufs]
    nb, ng = len(flat), len(groups)

    def body(*refs):
        ins, sems, token = refs[:nb], refs[nb + 1:nb + 1 + 2 * ng], refs[-1]
        pos = 0
        for g, (bufs, whole, select) in enumerate(groups):
            for send, _ in _gather_copies(ins[pos:pos + len(bufs)], whole, sems[2 * g], sems[2 * g + 1], select):
                send.start()
            pos += len(bufs)
        token[...] = jnp.zeros_like(token)

    sem_shapes = []
    for bufs, _, _ in groups:
        sem_shapes += [pltpu.SemaphoreType.DMA((3 * len(bufs),))] * 2
    out = _pallas_call(
        body, name=name,
        in_specs=[HBM] * nb + [ANY], out_specs=tuple([SEM] * (2 * ng) + [HBM] * nb + [VMEM]),
        out_shape=tuple(sem_shapes + [pltpu.HBM(b.shape, b.dtype) for b in flat] + [_sds((8, LANES), F32)]),
        input_output_aliases={i: 2 * ng + i for i in range(nb)},
        compiler_params=pltpu.CompilerParams(has_side_effects=EFFECT),
    )(*flat, after)
    sems, thru, pos = [], [], 2 * ng
    for g, (bufs, _, _) in enumerate(groups):
        sems.append((out[2 * g], out[2 * g + 1]))
        thru.append(list(out[pos:pos + len(bufs)]))
        pos += len(bufs)
    return sems, thru, out[-1]


def _gather_wait(name, bufs, whole, sems, after, select=None):
    nb = len(bufs)

    def body(*refs):
        ins, send_sems, recv_sems = refs[:nb], refs[nb], refs[nb + 1]
        for send, arrival in _gather_copies(ins, whole, send_sems, recv_sems, select):
            send.wait_send()
            arrival.wait_recv()

    return _pallas_call(
        body, name=name,
        in_specs=[HBM] * nb + [SEM, SEM, ANY], out_specs=[HBM] * nb,
        out_shape=[pltpu.HBM(b.shape, b.dtype) for b in bufs],
        input_output_aliases={i: i for i in range(nb)},
        compiler_params=pltpu.CompilerParams(has_side_effects=EFFECT),
    )(*bufs, sems[0], sems[1], after)


def _gather_forward(name, bufs, sources=(0, 1, 2)):
    n = len(bufs)

    def body(*refs):
        outs = refs[n:2 * n]
        send_sems, recv_sems = refs[2 * n:]
        x, y, c, _, chips = _place()
        sends = []
        for w in range(n):
            r2 = outs[w].shape[1] // 2
            for j in sources:
                landed = _half(outs[w], 2 * chips[j][0] + chips[j][1], c, r2)
                sends.append(_remote(landed, landed, send_sems.at[w * 3 + j], recv_sems.at[w * 3 + j],
                                     (x, y, 1 - c)))
        for cp in sends:
            cp.start()
        for w in range(n):
            r2 = outs[w].shape[1] // 2
            for j in sources:
                got = _half(outs[w], 2 * chips[j][0] + chips[j][1], 1 - c, r2)
                _remote(got, got, send_sems.at[w * 3 + j], recv_sems.at[w * 3 + j], (x, y, c)).wait_recv()
        for cp in sends:
            cp.wait_send()

    return _pallas_call(
        body, name=name,
        in_specs=[ANY] * n, out_specs=[ANY] * n,
        out_shape=[_sds(b.shape, b.dtype) for b in bufs],
        input_output_aliases={i: i for i in range(n)},
        scratch_shapes=[pltpu.SemaphoreType.DMA((n * 3,)), pltpu.SemaphoreType.DMA((n * 3,))],
    )(*bufs)


def _rs_add(name, grads3, from_sibling, c_idx):
    n = len(grads3)

    def body(c_ref, *refs):
        for g_ref, s_ref, o_ref in zip(refs[:n], refs[n:2 * n], refs[2 * n:]):
            o_ref[...] = (g_ref[...] + s_ref[...]).astype(BF16)

    mine = [pl.BlockSpec((None,) + s.shape[1:], lambda k, c_ref: (k, c_ref[0], 0)) for s in from_sibling]
    whole = [pl.BlockSpec((None,) + s.shape[1:], lambda k, c_ref: (k, 0, 0)) for s in from_sibling]
    return _pallas_call(
        body, name=name,
        grid_spec=pltpu.PrefetchScalarGridSpec(num_scalar_prefetch=1, grid=(N_CHIPS,), in_specs=mine + whole,
                                               out_specs=whole),
        out_shape=[_sds(s.shape, BF16) for s in from_sibling],
        compiler_params=_params(("parallel",), 48),
    )(c_idx, *grads3, *from_sibling)


def _split_start(name, arrays, n_sems, pairs_fn):
    n = len(arrays)

    def body(*refs):
        for send, _ in pairs_fn(refs[:n], refs[n], refs[n + 1]):
            send.start()
        refs[-1][...] = jnp.zeros_like(refs[-1])

    out = _pallas_call(
        body, name=name,
        in_specs=[HBM] * n, out_specs=tuple([SEM, SEM] + [HBM] * n + [VMEM]),
        out_shape=tuple([pltpu.SemaphoreType.DMA((n_sems,))] * 2 + [pltpu.HBM(a.shape, a.dtype) for a in arrays]
                        + [_sds((8, LANES), F32)]),
        input_output_aliases={i: 2 + i for i in range(n)},
        compiler_params=pltpu.CompilerParams(has_side_effects=EFFECT),
    )(*arrays)
    return (out[0], out[1]), list(out[2:2 + n]), out[-1]


def _split_wait(name, sems, arrays, pairs_fn, after):
    n = len(arrays)

    def body(*refs):
        for send, arrival in pairs_fn(refs[:n], refs[n], refs[n + 1]):
            send.wait_send()
            arrival.wait_recv()

    return list(_pallas_call(
        body, name=name,
        in_specs=[HBM] * n + [SEM, SEM, ANY], out_specs=[HBM] * n,
        out_shape=[pltpu.HBM(a.shape, a.dtype) for a in arrays],
        input_output_aliases={i: i for i in range(n)},
        compiler_params=pltpu.CompilerParams(has_side_effects=EFFECT),
    )(*arrays, sems[0], sems[1], after))


def _forward_pairs(bufs, send_sems, recv_sems):
    x, y, c, _, chips = _place()
    pairs = []
    for w, buf in enumerate(bufs):
        r2 = buf.shape[1] // 2
        for j, (cx, cy) in enumerate(chips):
            landed, theirs = _half(buf, 2 * cx + cy, c, r2), _half(buf, 2 * cx + cy, 1 - c, r2)
            sems = (send_sems.at[w * 3 + j], recv_sems.at[w * 3 + j])
            pairs.append((_remote(landed, landed, *sems, (x, y, 1 - c)), _remote(theirs, theirs, *sems, (x, y, c))))
    return pairs


def _sibling_pairs(arrays, send_sems, recv_sems):
    x, y, c, _, _ = _place()
    n = len(arrays) // 2
    pairs = []
    for w in range(n):
        r2 = arrays[w].shape[1] // 2
        cp = _remote(_half(arrays[w], slice(None), 1 - c, r2), arrays[n + w], send_sems.at[w], recv_sems.at[w],
                     (x, y, 1 - c))
        pairs.append((cp, cp))
    return pairs


def _ici_pairs(arrays, send_sems, recv_sems):
    x, y, c, _, chips = _place()
    n = len(arrays) // 2
    pairs = []
    for w in range(n):
        for j, (cx, cy) in enumerate(chips):
            cp = _remote(arrays[w].at[2 * cx + cy], arrays[n + w].at[j],
                         send_sems.at[w * 3 + j], recv_sems.at[w * 3 + j], (cx, cy, c))
            pairs.append((cp, cp))
    return pairs


def _rs_sum(name, partials, received, place_idx):
    n = len(partials)
    nb = 2
    blocks = [(p.shape[1] // nb, p.shape[2]) for p in partials]

    def body(idx_ref, *refs):
        for p_ref, r_ref, o_ref in zip(refs[:n], refs[n:2 * n], refs[2 * n:]):
            o_ref[...] = ((p_ref[...].astype(F32) + r_ref[0].astype(F32))
                          + (r_ref[1].astype(F32) + r_ref[2].astype(F32)))

    return _pallas_call(
        body, name=name,
        grid_spec=pltpu.PrefetchScalarGridSpec(
            num_scalar_prefetch=1, grid=(nb,),
            in_specs=[pl.BlockSpec((None,) + b, lambda i, idx: (idx[0], i, 0)) for b in blocks]
            + [pl.BlockSpec((3,) + b, lambda i, idx: (0, i, 0)) for b in blocks],
            out_specs=[pl.BlockSpec(b, lambda i, idx: (idx[1] * nb + i, 0)) for b in blocks]),
        out_shape=[_sds((2 * p.shape[1], p.shape[2]), F32) for p in partials],
        compiler_params=_params(("parallel",), 48),
    )(place_idx, *partials, *received)


def _share_pairs(arrays, send_sems, recv_sems):
    x, y, c, _, _ = _place()
    pairs = []
    for w, arr in enumerate(arrays):
        r2 = arr.shape[0] // 2
        mine = arr.at[pl.ds(pl.multiple_of(c * r2, 8), r2), :]
        theirs = arr.at[pl.ds(pl.multiple_of((1 - c) * r2, 8), r2), :]
        sems = (send_sems.at[w], recv_sems.at[w])
        pairs.append((_remote(mine, mine, *sems, (x, y, 1 - c)), _remote(theirs, theirs, *sems, (x, y, c))))
    return pairs


def _small_pack(red_mix, red_ffn, red_final, red_hg, g_conv):
    D = red_mix.shape[1]
    H = red_hg.shape[1]

    def body(mix_ref, ffn_ref, fin_ref, hg_ref, cv_ref, in_ref):
        in_ref[...] = jnp.zeros_like(in_ref)
        in_ref[0:1, :] = mix_ref[0:1, :]
        in_ref[1:2, :] = ffn_ref[0:1, :]
        in_ref[2:3, :] = fin_ref[0:1, :]
        gam = hg_ref[1:2, 0:HEAD_DIM]
        for h in range(1, H // HEAD_DIM):
            gam = gam + hg_ref[1:2, h * HEAD_DIM:(h + 1) * HEAD_DIM]
        in_ref[3:4, 0:HEAD_DIM] = gam
        in_ref[3:4, HEAD_DIM:2 * HEAD_DIM] = fin_ref[1:2, 0:HEAD_DIM]
        in_ref[4:5, 0:H] = hg_ref[0:1, :]
        in_ref[6:9, 0:H] = cv_ref[...]

    return _pallas_call(
        body, name="small_pack", pin=False,
        in_specs=[VMEM] * 5, out_specs=VMEM, out_shape=_sds((N_SMALL_ROWS, D), F32),
    )(red_mix, red_ffn, red_final, red_hg, g_conv)


def _small_pairs(arrays, send_sems, recv_sems):
    block, gathered = arrays
    x, y, c, _, _ = _place()
    me = 4 * x + 2 * y + c
    pairs = []
    for m in range(1, 8):
        px, py, pc = x ^ ((m >> 2) & 1), y ^ ((m >> 1) & 1), c ^ (m & 1)
        sems = (send_sems.at[m - 1], recv_sems.at[m - 1])
        pairs.append((_remote(block, gathered.at[me], *sems, (px, py, pc)),
                      _remote(block, gathered.at[4 * px + 2 * py + pc], *sems, (x, y, c))))
    return pairs


def _adamw_math(w, g, m, v):
    m = ADAM_B1 * m + (1.0 - ADAM_B1) * g
    v = ADAM_B2 * v + (1.0 - ADAM_B2) * jnp.square(g)
    m_hat = m / (1.0 - ADAM_B1 ** ADAM_STEP)
    v_hat = v / (1.0 - ADAM_B2 ** ADAM_STEP)
    delta = -ADAM_LR * (m_hat / (jnp.sqrt(v_hat) + ADAM_EPS) + ADAM_WD * w)
    return delta, m, v


def _adamw(name, gs, ws, ms, vs):
    n = len(gs)
    nb = 8

    def body(*refs):
        ins, outs = refs[:4 * n], refs[4 * n:]
        for j in range(n):
            g_ref, w_ref, m_ref, v_ref = ins[j], ins[n + j], ins[2 * n + j], ins[3 * n + j]
            go_ref, d_ref, mo_ref, vo_ref = outs[4 * j:4 * j + 4]
            g = g_ref[...]
            go_ref[...] = g
            d_ref[...], mo_ref[...], vo_ref[...] = _adamw_math(w_ref[...], g, m_ref[...], v_ref[...])

    blk = [pl.BlockSpec((g.shape[0] // nb, g.shape[1]), lambda i: (i, 0)) for g in gs]
    out = _pallas_call(
        body, name=name, grid=(nb,),
        in_specs=blk * 4, out_specs=[b for b in blk for _ in range(4)],
        out_shape=[_sds(g.shape, F32) for g in gs for _ in range(4)],
        compiler_params=_params(("parallel",), 56),
    )(*gs, *ws, *ms, *vs)
    return [list(out[4 * j:4 * j + 4]) for j in range(n)]


def _small_update(block, gathered, place_idx, ws, ms, vs):
    n = len(ws)
    H = ws[1].shape[1]

    def body(idx_ref, blk_ref, all_ref, *refs):
        w, m, v, outs, tot_ref = refs[:n], refs[n:2 * n], refs[2 * n:3 * n], refs[3 * n:-1], refs[-1]
        chip, me = idx_ref[0], idx_ref[1]
        tot = jnp.where(me == 0, blk_ref[...], all_ref[0])
        for d in range(1, 8):
            tot = tot + jnp.where(me == d, blk_ref[...], all_ref[d])
        tot_ref[...] = tot
        p0 = _lower_bound(w[1][...])
        dl0 = p0 * (1.0 - p0) * tot_ref[4:5, 0:H]
        conv = jnp.zeros((3, LANES), F32)
        for k in range(N_CHIPS):
            conv = jnp.where(chip == k, tot_ref[6:9, k * LANES:(k + 1) * LANES], conv)
        grads = [tot_ref[0:1, :], None, tot_ref[3:4, 0:HEAD_DIM], conv, tot_ref[1:2, :], tot_ref[2:3, :]]
        for p in range(n):
            g_ref, d_ref, mo_ref, vo_ref = outs[4 * p:4 * p + 4]
            if p == 1:
                for row, g in ((slice(0, 1), dl0), (slice(1, 2), -dl0)):
                    g_ref[row, :] = g
                    d_ref[row, :], mo_ref[row, :], vo_ref[row, :] = _adamw_math(
                        w[p][row, :], g, m[p][row, :], v[p][row, :])
            else:
                g_ref[...] = grads[p]
                d_ref[...], mo_ref[...], vo_ref[...] = _adamw_math(w[p][...], grads[p], m[p][...], v[p][...])
        outs[4 * n][...] = tot_ref[3:4, HEAD_DIM:2 * HEAD_DIM]

    full = lambda a: pl.BlockSpec(a.shape, lambda i, idx: (0,) * a.ndim)
    out_shape = [_sds(w.shape, F32) for w in ws for _ in range(4)] + [_sds((1, LANES), F32)]
    return _pallas_call(
        body, name="small_update",
        grid_spec=pltpu.PrefetchScalarGridSpec(
            num_scalar_prefetch=1, grid=(1,),
            in_specs=[full(block), full(gathered)] + [full(a) for a in ws + ms + vs],
            out_specs=[full(s) for s in out_shape],
            scratch_shapes=[pltpu.VMEM(block.shape, F32)]),
        out_shape=out_shape,
    )(place_idx, block, gathered, *ws, *ms, *vs)


def kernel(x, norm_mix_g, w_in, lower_bounds, hg_norm_g, conv_w, w_branch_a, w_branch_b, w_out, norm_ffn_g, w_ffn_gate, w_ffn_up, w_ffn_down, norm_final_g, loss_target, m_norm_mix_g, m_w_in, m_lower_bounds, m_hg_norm_g, m_conv_w, m_w_branch_a, m_w_branch_b, m_w_out, m_norm_ffn_g, m_w_ffn_gate, m_w_ffn_up, m_w_ffn_down, m_norm_final_g, v_norm_mix_g, v_w_in, v_lower_bounds, v_hg_norm_g, v_conv_w, v_w_branch_a, v_w_branch_b, v_w_out, v_norm_ffn_g, v_w_ffn_gate, v_w_ffn_up, v_w_ffn_down, v_norm_final_g):
    _, L, D = x.shape
    H = D // 2
    assert lower_bounds.shape == (2, H) and hg_norm_g.shape == (1, HEAD_DIM)
    assert conv_w.shape == (1, 3, LANES) and w_in.shape[2] * N_CHIPS == 11 * H
    x2d, target = x.reshape(L, D), loss_target.reshape(L, D)
    g_final = norm_final_g.reshape(1, D)
    chip = 2 * lax.axis_index("x") + lax.axis_index("y")
    core = lax.axis_index("c")

    tr = lambda w: jnp.transpose(w[0])
    big = [w_in[0], w_branch_a[0], w_branch_b[0], w_out[0], tr(w_ffn_gate), tr(w_ffn_up), w_ffn_down[0]]
    big_m = [m_w_in[0], m_w_branch_a[0], m_w_branch_b[0], m_w_out[0], tr(m_w_ffn_gate), tr(m_w_ffn_up),
             m_w_ffn_down[0]]
    big_v = [v_w_in[0], v_w_branch_a[0], v_w_branch_b[0], v_w_out[0], tr(v_w_ffn_gate), tr(v_w_ffn_up),
             v_w_ffn_down[0]]
    names = ["w_in", "w_branch_a", "w_branch_b", "w_out", "w_ffn_gate", "w_ffn_up", "w_ffn_down"]

    chip_idx = chip.reshape(1).astype(jnp.int32)
    def per_shape(fn, tag, js, *lists):
        groups = {}
        for pos, a in enumerate(lists[0]):
            groups.setdefault(a.shape, []).append(pos)
        results = [None] * len(js)
        for same in groups.values():
            out = fn(tag + names[js[same[0]]], *[[xs[p] for p in same] for xs in lists])
            for q, p in enumerate(same):
                results[p] = out[q]
        return results

    place_t = lambda name, ws: _cast_place_t(name, ws, chip_idx)
    placed = per_shape(place_t, "place_", [0, 1, 2], big[:3]) + list(_cast_place("place_rest", big[3:], chip_idx))
    conv_placed = lax.dynamic_update_slice(jnp.zeros((N_CHIPS, 3, LANES), F32), conv_w, (chip, 0, 0))
    x_i, y_i = lax.axis_index("x"), lax.axis_index("y")
    blocks = lambda *ks: jnp.stack(ks).astype(jnp.int32)
    near = lambda w, j: j < 2
    far = lambda w, j: w == 1 or j == 2
    near_sems, in_flight, _ = _gather_start("gather_start_near", [([placed[0]], set(), near)], chip_idx)
    w_in_buf = in_flight[0][0]
    h, proj = _fwd_proj_first(x2d, norm_mix_g, w_in_buf, blocks(chip))
    sems, in_flight, _ = _gather_start(
        "gather_start_rest", [([w_in_buf, conv_placed], {1}, far), (placed[1:4], set(), None),
                              (placed[4:], set(), None)], h)
    w_in_buf, conv_buf = in_flight[0]
    (w_in_buf,) = _gather_wait("gather_wait_in_near", [w_in_buf], set(), near_sems[0], h, near)
    (w_in_buf,) = _gather_forward("gather_fwd_in_near", [w_in_buf], (0, 1))
    proj = _fwd_proj_more("fwd_proj_near", h, w_in_buf, proj,
                          blocks(2 * (1 - x_i) + y_i, 2 * x_i + (1 - y_i)))
    w_in_buf, conv_all = _gather_wait("gather_wait_in_far", [w_in_buf, conv_buf], {1}, sems[0], proj, far)
    (w_int3,) = _gather_forward("gather_fwd_in_far", [w_in_buf], (2,))
    proj = _fwd_proj_more("fwd_proj_far", h, w_int3, proj, blocks(2 * (1 - x_i) + (1 - y_i)))
    w_int = w_int3.reshape(-1, D)
    conv_full = jnp.transpose(conv_all, (1, 0, 2)).reshape(3, H)
    og, o_pre, s_saved = _hgrn_fwd(proj, lower_bounds, hg_norm_g, H)
    landed = _gather_wait("gather_wait_mix", in_flight[1], set(), sems[1], og)
    fwd_sems, landed, token = _split_start("gather_fwd_mix_start", landed, 9, _forward_pairs)
    cb = _conv_fwd(proj, conv_full, H, token)
    wat3, wbt3, wout3 = _split_wait("gather_fwd_mix_wait", fwd_sems, landed, _forward_pairs, cb)
    wat, wbt, wout = wat3.reshape(D, H), wbt3.reshape(D, H), wout3.reshape(D, D)
    landed = _gather_wait("gather_wait_ffn", in_flight[2], set(), sems[2], cb)
    fwd_sems, landed, token = _split_start("gather_fwd_ffn_start", landed, 9, _forward_pairs)
    sig_a, sig_b, dm_dga, dm_dgb, merged, x1, h2 = _fwd_mix(og, cb, proj, x2d, wat, wbt, wout, norm_ffn_g,
                                                              H, token)
    wgt3, wut3, wd3 = _split_wait("gather_fwd_ffn_wait", fwd_sems, landed, _forward_pairs, h2)
    d_ff = N_CHIPS * wd3.shape[1]
    wgt, wut, wd = wgt3.reshape(d_ff, D), wut3.reshape(d_ff, D), wd3.reshape(d_ff, D)
    ffn_ds_da, ffn_ds_db, ffn_s = _fwd_ffn_up(h2, wgt, wut)
    dx2, dx2b, red_final = _fwd_down_loss(ffn_s, wd, x1, target, g_final)

    c_idx = core.reshape(1).astype(jnp.int32)
    place_idx = jnp.stack([chip, core]).astype(jnp.int32)

    def sibling_start(tag, grads):
        bufs = [lax.empty((N_CHIPS, g.shape[1] // 2, g.shape[2]), F32) for g in grads]
        return _split_start("rs_sibling_start_" + tag, list(grads) + bufs, len(grads), _sibling_pairs)

    def ici_start(tag, js, grads, from_sibling):
        partials = list(_rs_add("rs_add_" + tag, grads, from_sibling, c_idx))
        landings = [lax.empty((3,) + p.shape[1:], BF16) for p in partials]
        return _split_start("rs_ici_start_" + tag, partials + landings, 3 * len(js), _ici_pairs)

    def ici_start_behind(tag, js, started, after):
        n = len(js)
        arrays = _split_wait("rs_sibling_wait_" + tag, started[0], started[1], _sibling_pairs, after)
        return ici_start(tag, js, arrays[:n], arrays[n:])

    def sums(tag, started, after):
        partials, received = [], []
        for group, group_js, start in started:
            arrays = _split_wait("rs_ici_wait_" + group, start[0], start[1], _ici_pairs, after)
            partials += arrays[:len(group_js)]
            received += arrays[len(group_js):]
        return list(_rs_sum("rs_sum_" + tag, partials, received, place_idx))

    def adamw(tag, js, grads):
        return _adamw("adamw_" + tag, grads, *[[src[j] for j in js] for src in (big, big_m, big_v)])

    shards3 = lambda g: g.reshape(N_CHIPS, d_ff // N_CHIPS, D)
    da, db = _bwd_down(dx2b, wd, ffn_ds_da, ffn_ds_db)
    g_wd = shards3(_dw_rows2("dw_ffn_down", ffn_s, dx2b))
    g_wg = shards3(_dw_rows2("dw_ffn_gate", da, h2))
    g_wu = shards3(_dw_rows2("dw_ffn_up", db, h2))
    ffn_sibling = sibling_start("ffn", [g_wg, g_wu, g_wd])
    dx1, dx1b, red_ffn = _bwd_ffn_dh(da, db, wgt, wut, x1, dx2, norm_ffn_g, ffn_sibling[2])
    ffn_ici = ici_start_behind("ffn", [4, 5, 6], ffn_sibling, dx1b)
    dya, dyb, dproj, d_o, d_cb = _bwd_mix(dx1b, sig_a, sig_b, dm_dga, dm_dgb, wat, wbt, wout, H, ffn_ici[2])
    g_wout = _dw_rows("dw_out", merged, dx1b)
    g_wa = _dw_cols("dw_branch_a", og, dya, D // N_CHIPS)
    g_wb = _dw_cols("dw_branch_b", cb, dyb, D // N_CHIPS)
    mix_sibling = sibling_start("mix", [g_wa, g_wb, g_wout])
    dproj, red_hg = _hgrn_bwd(proj, lower_bounds, hg_norm_g, o_pre, d_o, s_saved, H, mix_sibling[2], dproj)
    mix_ici = ici_start_behind("mix", [1, 2, 3], mix_sibling, red_hg)
    dproj, g_conv = _conv_bwd(proj, conv_full, d_cb, H, mix_ici[2], dproj)
    g_win = _dw_in(h, dproj, w_int3.shape[1])
    in_sibling = sibling_start("in", [g_win])
    halves = sums("rest", [("mix", [1, 2, 3], mix_ici), ("ffn", [4, 5, 6], ffn_ici)], in_sibling[2])
    rest_share = _split_start("rs_share_start_rest", halves, len(halves), _share_pairs)
    in_ici = ici_start_behind("in", [0], in_sibling, rest_share[2])
    grad_x, red_mix = _bwd_in(dproj, w_int, x2d, dx1, norm_mix_g, in_ici[2])
    in_share = _split_start("rs_share_start_in", sums("in", [("in", [0], in_ici)], grad_x), 1, _share_pairs)
    small_block = _small_pack(red_mix, red_ffn, red_final, red_hg, g_conv)
    small = _split_start("small_gather_start", [small_block, lax.empty((8,) + small_block.shape, F32)], 7,
                         _small_pairs)
    rest_grads = _split_wait("rs_share_wait_rest", rest_share[0], rest_share[1], _share_pairs, small[2])
    big_out = [None] + adamw("rest", [1, 2, 3, 4, 5, 6], rest_grads)
    in_grad = _split_wait("rs_share_wait_in", in_share[0], in_share[1], _share_pairs, big_out[6][0])
    big_out[0] = adamw("in", [0], in_grad)[0]
    small_block, small_all = _split_wait("small_gather_wait", small[0], small[1], _small_pairs, big_out[0][0])

    def smalls(mix, lb, hg, cw, ffn, fin):
        return [mix, lb, hg, cw[0], ffn, fin.reshape(1, D)]

    small_out = _small_update(
        small_block, small_all, jnp.stack([chip, 4 * x_i + 2 * y_i + core]).astype(jnp.int32),
        smalls(norm_mix_g, lower_bounds, hg_norm_g, conv_w, norm_ffn_g, norm_final_g),
        smalls(m_norm_mix_g, m_lower_bounds, m_hg_norm_g, m_conv_w, m_norm_ffn_g, m_norm_final_g),
        smalls(v_norm_mix_g, v_lower_bounds, v_hg_norm_g, v_conv_w, v_norm_ffn_g, v_norm_final_g))

    def outputs(i):
        big_i = [big_out[j][i] for j in range(7)]
        mix, lb, hg, cw, ffn, fin = [small_out[4 * p + i] for p in range(6)]
        return [mix, big_i[0][None], lb, hg, cw[None], big_i[1][None], big_i[2][None], big_i[3][None], ffn,
                big_i[4].T[None], big_i[5].T[None], big_i[6][None], fin.reshape(D)]

    outs = [small_out[24][0, 0], grad_x.reshape(1, L, D)]
    for i in range(4):
        outs += outputs(i)
    return tuple(outs)
```

```python
import functools

import jax
import jax.numpy as jnp
from jax import lax
from jax.experimental import pallas as pl
from jax.experimental.pallas import tpu as pltpu

F32 = jnp.float32
BF16 = jnp.bfloat16
EPS = 1e-6
CHUNK = 32
HEAD_DIM = 128
LANES = 128
N_CHIPS = 4
N_SMALL_ROWS = 16
DPROJ_BLOCKS = 12
DPROJ_BLOCK_OF = (0, 1, 2, 3, 8, 9, 10, 4, 5, 6, 7)

ADAM_LR = 0.001
ADAM_B1 = 0.9
ADAM_B2 = 0.999
ADAM_EPS = 1e-08
ADAM_WD = 0.01
ADAM_STEP = 10

MESH = pl.DeviceIdType.MESH
ANY = pl.BlockSpec(memory_space=pl.ANY)
VMEM = pl.BlockSpec(memory_space=pltpu.VMEM)
HBM = pl.BlockSpec(memory_space=pltpu.HBM)
SEM = pl.BlockSpec(memory_space=pltpu.SEMAPHORE)
EFFECT = pltpu.SideEffectType.DATAFLOW_SIDE_EFFECTING


def _sds(shape, dtype):
    return jax.ShapeDtypeStruct(shape, dtype)


def _pallas_call(body, pin=True, **kwargs):
    if not pin:
        return pl.pallas_call(body, **kwargs)
    in_hbm = lambda s: pltpu.HBM(s.shape, s.dtype) if isinstance(s, jax.ShapeDtypeStruct) else s
    kwargs["out_shape"] = jax.tree.map(in_hbm, kwargs["out_shape"])
    call = pl.pallas_call(body, **kwargs)

    def run(*args):
        return call(*[pltpu.with_memory_space_constraint(a, pltpu.HBM) if a.dtype in (F32, BF16) else a
                      for a in args])

    return run


def _params(semantics, vmem_mb):
    return pltpu.CompilerParams(dimension_semantics=semantics, vmem_limit_bytes=vmem_mb << 20)


def _nn(a, b):
    return lax.dot_general(a, b, (((1,), (0,)), ((), ())), preferred_element_type=F32)


def _nt(a, b):
    return lax.dot_general(a, b, (((1,), (1,)), ((), ())), preferred_element_type=F32)


def _tn(a, b):
    return lax.dot_general(a, b, (((0,), (0,)), ((), ())), preferred_element_type=F32)


def _sigmoid(x):
    return jax.nn.sigmoid(x)


def _rms_stats(x):
    r = lax.rsqrt(jnp.mean(x * x, axis=-1, keepdims=True) + EPS)
    return r, x * r


def _rms_bwd(dxh, xh, r):
    return r * (dxh - xh * jnp.mean(dxh * xh, axis=-1, keepdims=True))


def _fwd_proj_first(x, g_mix, w_int3, block):
    L, D = x.shape
    tn = w_int3.shape[1]
    tm = min(L, 1024)

    def body(blk_ref, x_ref, g_ref, w_ref, h_ref, p_ref):
        _, xh = _rms_stats(x_ref[...])
        h = (xh * g_ref[...]).astype(BF16)
        h_ref[...] = h
        p_ref[...] = _nt(h, w_ref[...])

    return _pallas_call(
        body, name="fwd_proj_own",
        grid_spec=pltpu.PrefetchScalarGridSpec(
            num_scalar_prefetch=1, grid=(L // tm,),
            in_specs=[pl.BlockSpec((tm, D), lambda i, blk: (i, 0)),
                      pl.BlockSpec((1, D), lambda i, blk: (0, 0)),
                      pl.BlockSpec((None, tn, D), lambda i, blk: (blk[0], 0, 0))],
            out_specs=[pl.BlockSpec((tm, D), lambda i, blk: (i, 0)),
                       pl.BlockSpec((tm, tn), lambda i, blk: (i, blk[0]))]),
        out_shape=[_sds((L, D), BF16), _sds((L, N_CHIPS * tn), F32)],
        compiler_params=_params(("parallel",), 48),
    )(block, x, g_mix, w_int3)


def _fwd_proj_more(name, h, w_int3, proj, blocks):
    L, D = h.shape
    tn = w_int3.shape[1]
    tm = min(L, 1024)

    def body(blk_ref, h_ref, w_ref, proj_ref, p_ref):
        p_ref[...] = _nt(h_ref[...], w_ref[...])

    return _pallas_call(
        body, name=name,
        grid_spec=pltpu.PrefetchScalarGridSpec(
            num_scalar_prefetch=1, grid=(L // tm, blocks.shape[0]),
            in_specs=[pl.BlockSpec((tm, D), lambda i, j, blk: (i, 0)),
                      pl.BlockSpec((None, tn, D), lambda i, j, blk: (blk[j], 0, 0)), ANY],
            out_specs=pl.BlockSpec((tm, tn), lambda i, j, blk: (i, blk[j]))),
        out_shape=_sds(proj.shape, proj.dtype),
        input_output_aliases={3: 0},
        compiler_params=_params(("parallel", "arbitrary"), 48),
    )(blocks, h, w_int3, proj)


def _lower_bound(lbp):
    l0, l1 = lbp[0:1, :], lbp[1:2, :]
    m = jnp.maximum(l0, l1)
    e0, e1 = jnp.exp(l0 - m), jnp.exp(l1 - m)
    return e0 / (e0 + e1)


def _seg_scan(x, r32, forward):
    n = x.shape[0]
    s = 1
    while s < CHUNK:
        if forward:
            x = x + jnp.where(r32 >= s, pltpu.roll(x, s, 0), 0.0)
        else:
            x = x + jnp.where(r32 < CHUNK - s, pltpu.roll(x, n - s, 0), 0.0)
        s *= 2
    return x


def _bcast_row(x, row):
    n, w = x.shape
    nc = n // CHUNK
    x3 = x.reshape(nc, CHUNK, w)
    return jnp.broadcast_to(x3[:, row:row + 1, :], (nc, CHUNK, w)).reshape(n, w)


def _chunk_total(x):
    n, w = x.shape
    nc = n // CHUNK
    total = jnp.sum(x.reshape(nc, CHUNK, w), axis=1, keepdims=True)
    return jnp.broadcast_to(total, (nc, CHUNK, w)).reshape(n, w)


def _hgrn_prep(q_raw, f_raw, lb):
    r32 = lax.broadcasted_iota(jnp.int32, f_raw.shape, 0) & (CHUNK - 1)
    sig = _sigmoid(f_raw)
    f = lb + (1.0 - lb) * sig
    b = _seg_scan(jnp.log(f), r32, True)
    a = _bcast_row(b, CHUNK // 2 - 1)
    bl = _bcast_row(b, CHUNK - 1)
    sq = _sigmoid(q_raw)
    q = q_raw * sq * (HEAD_DIM ** -0.5)
    return dict(r32=r32, sig=sig, f=f, k=1.0 - f, b=b, a=a, bl=bl, sq=sq, q=q)


def _chunk_masks(n):
    ri = lax.broadcasted_iota(jnp.int32, (n, n), 0)
    ci = lax.broadcasted_iota(jnp.int32, (n, n), 1)
    same = (ri // CHUNK) == (ci // CHUNK)
    return same & (ci <= ri), same & (ri <= ci)


def _hgrn_fwd(proj, lower_bounds, gamma, H):
    L = proj.shape[0]
    nh = H // HEAD_DIM
    TL = min(L, 256)
    nc = TL // CHUNK

    def body(q_ref, f_ref, v_ref, g_ref, lbp_ref, gam_ref, og_ref, o_ref, s_ref, st_ref):
        @pl.when(pl.program_id(0) == 0)
        def _():
            st_ref[...] = jnp.zeros_like(st_ref)

        lb = _lower_bound(lbp_ref[...])
        gam = gam_ref[...]
        mask, _ = _chunk_masks(TL)
        rowc = lax.broadcasted_iota(jnp.int32, (TL, HEAD_DIM), 0) // CHUNK
        for h in range(nh):
            hs = slice(h * HEAD_DIM, (h + 1) * HEAD_DIM)
            p = _hgrn_prep(q_ref[:, hs], f_ref[:, hs], lb[:, hs])
            v = v_ref[:, hs]
            vb = v.astype(BF16)
            vt = v.T.astype(BF16)
            q_hat = (p["q"] * jnp.exp(p["b"] - p["a"])).astype(BF16)
            k_hat = (p["k"] * jnp.exp(p["a"] - p["b"])).astype(BF16)
            q_in = (p["q"] * jnp.exp(p["b"])).astype(BF16)
            k_out = (p["k"] * jnp.exp(p["bl"] - p["b"])).astype(BF16)
            dec = jnp.exp(p["bl"])
            att = jnp.where(mask, _nt(q_hat, k_hat), 0.0).astype(BF16)
            o_intra = _nn(att, vb)
            st = st_ref[h]
            for c in range(nc):
                rs = slice(c * CHUNK, (c + 1) * CHUNK)
                stb = st.astype(BF16)
                s_ref[c, h] = stb
                o_ref[rs, hs] = o_intra[rs] + _nt(q_in[rs], stb)
                k_c = jnp.where(rowc == c, k_out, jnp.zeros_like(k_out))
                st = st * dec[c * CHUNK:c * CHUNK + 1, :] + _nn(vt, k_c)
            st_ref[h] = st
            o = o_ref[:, hs]
            _, xh = _rms_stats(o)
            gr = g_ref[:, hs]
            og_ref[:, hs] = (xh * gam * (gr * _sigmoid(gr))).astype(BF16)

    col = lambda k: pl.BlockSpec((TL, H), lambda i, k=k: (i, k))
    return _pallas_call(
        body, name="hgrn_fwd", grid=(L // TL,),
        in_specs=[col(0), col(1), col(2), col(3),
                  pl.BlockSpec(lower_bounds.shape, lambda i: (0, 0)),
                  pl.BlockSpec(gamma.shape, lambda i: (0, 0))],
        out_specs=[pl.BlockSpec((TL, H), lambda i: (i, 0)),
                   pl.BlockSpec((TL, H), lambda i: (i, 0)),
                   pl.BlockSpec((nc, nh, HEAD_DIM, HEAD_DIM), lambda i: (i, 0, 0, 0))],
        out_shape=[_sds((L, H), BF16), _sds((L, H), F32),
                   _sds((L // CHUNK, nh, HEAD_DIM, HEAD_DIM), BF16)],
        scratch_shapes=[pltpu.VMEM((nh, HEAD_DIM, HEAD_DIM), F32)],
        compiler_params=_params(("arbitrary",), 48),
    )(proj, proj, proj, proj, lower_bounds, gamma)


def _hgrn_bwd(proj, lower_bounds, gamma, o_pre, d_out, s_saved, H, after, dproj):
    L = proj.shape[0]
    nh = H // HEAD_DIM
    TL = min(L, 256)
    nc = TL // CHUNK
    nt = L // TL

    def body(q_ref, f_ref, v_ref, g_ref, lbp_ref, gam_ref, o_ref, d_ref, s_ref, after_ref, dproj_ref,
             dp_ref, red_ref, dst_ref, dsall_ref, tmp_ref):
        @pl.when(pl.program_id(0) == 0)
        def _():
            dst_ref[...] = jnp.zeros_like(dst_ref)
            red_ref[...] = jnp.zeros_like(red_ref)

        lb = _lower_bound(lbp_ref[...])
        gam = gam_ref[...]
        mask, mask_t = _chunk_masks(TL)
        rowc = lax.broadcasted_iota(jnp.int32, (TL, HEAD_DIM), 0) // CHUNK
        for h in range(nh):
            hs = slice(h * HEAD_DIM, (h + 1) * HEAD_DIM)
            qr, gr, lbh = q_ref[:, hs], g_ref[:, hs], lb[:, hs]
            p = _hgrn_prep(qr, f_ref[:, hs], lbh)
            vb = v_ref[:, hs].astype(BF16)
            eba, eab = jnp.exp(p["b"] - p["a"]), jnp.exp(p["a"] - p["b"])
            eb, elb = jnp.exp(p["b"]), jnp.exp(p["bl"] - p["b"])
            dec = jnp.exp(p["bl"])
            q_hat, k_hat = p["q"] * eba, p["k"] * eab
            q_in, k_out = p["q"] * eb, p["k"] * elb
            q_hat_b, k_hat_b = q_hat.astype(BF16), k_hat.astype(BF16)
            q_in_b, k_out_b = q_in.astype(BF16), k_out.astype(BF16)

            o, dout = o_ref[:, hs], d_ref[:, hs]
            sg = _sigmoid(gr)
            r, xh = _rms_stats(o)
            dp_ref[3, :, hs] = (dout * (xh * gam) * (sg * (1.0 + gr * (1.0 - sg)))).astype(BF16)
            dn = dout * (gr * sg)
            red_ref[1:2, hs] += jnp.sum(dn * xh, axis=0, keepdims=True)
            do = _rms_bwd(dn * gam, xh, r)
            dob = do.astype(BF16)
            dot_b = do.T.astype(BF16)

            att_t = jnp.where(mask_t, _nt(k_hat_b, q_hat_b), 0.0).astype(BF16)
            dv_intra = _nn(att_t, dob)
            datt = jnp.where(mask, _nt(dob, vb), 0.0).astype(BF16)
            dqh = _nn(datt, k_hat_b)
            datt_t = jnp.where(mask_t, _nt(vb, dob), 0.0).astype(BF16)
            dkh = _nn(datt_t, q_hat_b)

            dst = dst_ref[h]
            for c in reversed(range(nc)):
                dsall_ref[c] = dst
                q_c = jnp.where(rowc == c, q_in_b, jnp.zeros_like(q_in_b))
                dst = dst * dec[c * CHUNK:c * CHUNK + 1, :] + _nn(dot_b, q_c)
            dst_ref[h] = dst
            for c in range(nc):
                rs = slice(c * CHUNK, (c + 1) * CHUNK)
                ds_c = dsall_ref[c]
                dsb = ds_c.astype(BF16)
                st_prev = s_ref[c, h]
                tmp_ref[0, rs, :] = _nt(k_out_b[rs], dsb)
                tmp_ref[1, rs, :] = _nn(vb[rs], dsb)
                tmp_ref[2, rs, :] = _nn(dob[rs], st_prev)
                ddec = jnp.sum(ds_c * st_prev.astype(F32), axis=0, keepdims=True)
                tmp_ref[3, rs, :] = jnp.broadcast_to(ddec * dec[c * CHUNK:c * CHUNK + 1, :],
                                                     (CHUNK, HEAD_DIM))
            dko, dqi = tmp_ref[1], tmp_ref[2]
            dq = dqh * eba + dqi * eb
            dk = dkh * eab + dko * elb
            tko = dko * k_out
            db = dqh * q_hat - dkh * k_hat + dqi * q_in - tko
            dlog = _seg_scan(db, p["r32"], False) + _chunk_total(tko) + tmp_ref[3]
            df = dlog / p["f"] - dk
            sig = p["sig"]
            red_ref[0:1, hs] += jnp.sum(df * (1.0 - sig), axis=0, keepdims=True)
            dp_ref[1, :, hs] = (df * (1.0 - lbh) * sig * (1.0 - sig)).astype(BF16)
            sq = p["sq"]
            dp_ref[0, :, hs] = (dq * (HEAD_DIM ** -0.5) * (sq * (1.0 + qr * (1.0 - sq)))).astype(BF16)
            dp_ref[2, :, hs] = (dv_intra + tmp_ref[0]).astype(BF16)

    col = lambda k: pl.BlockSpec((TL, H), lambda i, k=k: (nt - 1 - i, k))
    rev = pl.BlockSpec((TL, H), lambda i: (nt - 1 - i, 0))
    return _pallas_call(
        body, name="hgrn_bwd", grid=(nt,),
        in_specs=[col(0), col(1), col(2), col(3),
                  pl.BlockSpec(lower_bounds.shape, lambda i: (0, 0)),
                  pl.BlockSpec(gamma.shape, lambda i: (0, 0)),
                  rev, rev,
                  pl.BlockSpec((nc, nh, HEAD_DIM, HEAD_DIM), lambda i: (nt - 1 - i, 0, 0, 0)), ANY, ANY],
        out_specs=[pl.BlockSpec((4, TL, H), lambda i: (0, nt - 1 - i, 0)), pl.BlockSpec((8, H), lambda i: (0, 0))],
        out_shape=[_sds(dproj.shape, BF16), _sds((8, H), F32)],
        input_output_aliases={10: 0},
        scratch_shapes=[pltpu.VMEM((nh, HEAD_DIM, HEAD_DIM), F32),
                        pltpu.VMEM((nc, HEAD_DIM, HEAD_DIM), F32),
                        pltpu.VMEM((4, TL, HEAD_DIM), F32)],
        compiler_params=_params(("arbitrary",), 48),
    )(proj, proj, proj, proj, lower_bounds, gamma, o_pre, d_out, s_saved, after, dproj)


def _shift_down(u, s, row):
    return jnp.where(row >= s, pltpu.roll(u, s, 0), 0.0)


def _shift_up(u, s, row):
    n = u.shape[0]
    return jnp.where(row < n - s, pltpu.roll(u, n - s, 0), 0.0)


def _conv_specs(L, H):
    per = H // LANES
    return [pl.BlockSpec((L, LANES), lambda j, o=o: (0, o * per + j)) for o in (4, 5, 6)]


def _conv_fwd(proj, conv_w, H, after):
    L = proj.shape[0]

    def body(c_ref, b_ref, x_ref, w_ref, after_ref, o_ref):
        row = lax.broadcasted_iota(jnp.int32, (L, LANES), 0)
        u = c_ref[...] * x_ref[...]
        w = w_ref[...]
        y = w[0:1] * _shift_down(u, 2, row) + w[1:2] * _shift_down(u, 1, row) + w[2:3] * u
        o_ref[...] = (b_ref[...] * y).astype(BF16)

    return _pallas_call(
        body, name="conv_fwd", grid=(H // LANES,),
        in_specs=_conv_specs(L, H) + [pl.BlockSpec((3, LANES), lambda j: (0, j)), ANY],
        out_specs=pl.BlockSpec((L, LANES), lambda j: (0, j)),
        out_shape=_sds((L, H), BF16),
        compiler_params=_params(("parallel",), 48),
    )(proj, proj, proj, conv_w, after)


def _conv_bwd(proj, conv_w, dcb, H, after, dproj):
    L = proj.shape[0]

    def body(c_ref, b_ref, x_ref, w_ref, d_ref, after_ref, dproj_ref, dp_ref, dw_ref):
        row = lax.broadcasted_iota(jnp.int32, (L, LANES), 0)
        cg, xb = c_ref[...], x_ref[...]
        u = cg * xb
        u1, u2 = _shift_down(u, 1, row), _shift_down(u, 2, row)
        w = w_ref[...]
        y = w[0:1] * u2 + w[1:2] * u1 + w[2:3] * u
        d = d_ref[...]
        dp_ref[1] = (d * y).astype(BF16)
        dy = d * b_ref[...]
        du = w[2:3] * dy + w[1:2] * _shift_up(dy, 1, row) + w[0:1] * _shift_up(dy, 2, row)
        dw_ref[0:1, :] = jnp.sum(dy * u2, axis=0, keepdims=True)
        dw_ref[1:2, :] = jnp.sum(dy * u1, axis=0, keepdims=True)
        dw_ref[2:3, :] = jnp.sum(dy * u, axis=0, keepdims=True)
        dp_ref[0] = (du * xb).astype(BF16)
        dp_ref[2] = (du * cg).astype(BF16)
        dp_ref[3] = jnp.zeros((L, LANES), BF16)

    blk = pl.BlockSpec((L, LANES), lambda j: (0, j))
    return _pallas_call(
        body, name="conv_bwd", grid=(H // LANES,),
        in_specs=_conv_specs(L, H) + [pl.BlockSpec((3, LANES), lambda j: (0, j)), blk, ANY, ANY],
        out_specs=[pl.BlockSpec((4, L, LANES), lambda j: (2, 0, j)), pl.BlockSpec((3, LANES), lambda j: (0, j))],
        out_shape=[_sds(dproj.shape, BF16), _sds((3, H), F32)],
        input_output_aliases={6: 0},
        compiler_params=_params(("parallel",), 56),
    )(proj, proj, proj, conv_w, dcb, after, dproj)


def _gate_specs(tm, H):
    return [pl.BlockSpec((tm, H), lambda i, k=k: (i, k)) for k in (7, 8, 9, 10)]


def _fwd_mix(og, cb, proj, x, wat, wbt, wout, g_ffn, H, after):
    L, D = x.shape
    tm = min(L, 512)

    def body(o_ref, cb_ref, ga0, ga1, gb0, gb1, x_ref, wa_ref, wb_ref, wo_ref, g_ref, after_ref,
             sa_ref, sb_ref, ta_ref, tb_ref, m_ref, x1_ref, h2_ref):
        ya, yb = _nt(o_ref[...], wa_ref[...]), _nt(cb_ref[...], wb_ref[...])
        for k, (gar, gbr) in enumerate(((ga0, gb0), (ga1, gb1))):
            cs = slice(k * H, (k + 1) * H)
            sa, sb = _sigmoid(gar[...]), _sigmoid(gbr[...])
            ma, mb = sa * ya[:, cs], sb * yb[:, cs]
            m_ref[:, cs] = (ma + mb).astype(BF16)
            sa_ref[:, cs] = sa.astype(BF16)
            sb_ref[:, cs] = sb.astype(BF16)
            ta_ref[:, cs] = (ma * (1.0 - sa)).astype(BF16)
            tb_ref[:, cs] = (mb * (1.0 - sb)).astype(BF16)
        x1 = x_ref[...] + _nn(m_ref[...], wo_ref[...])
        x1_ref[...] = x1
        _, xh = _rms_stats(x1)
        h2_ref[...] = (xh * g_ref[...]).astype(BF16)

    row = lambda w: pl.BlockSpec((tm, w), lambda i: (i, 0))
    full = lambda a: pl.BlockSpec(a.shape, lambda i: (0,) * a.ndim)
    return _pallas_call(
        body, name="fwd_mix", grid=(L // tm,),
        in_specs=[row(H), row(H)] + _gate_specs(tm, H) + [row(D), full(wat), full(wbt), full(wout),
                                                           full(g_ffn), ANY],
        out_specs=[row(D)] * 7,
        out_shape=[_sds((L, D), BF16)] * 5 + [_sds((L, D), F32), _sds((L, D), BF16)],
        compiler_params=_params(("parallel",), 56),
    )(og, cb, proj, proj, proj, proj, x, wat, wbt, wout, g_ffn, after)


def _bwd_mix(dx1b, sig_a, sig_b, dm_dga, dm_dgb, wat, wbt, wout, H, after):
    L, D = dx1b.shape
    tm = min(L, 512)

    def body(dx_ref, sa_ref, sb_ref, ta_ref, tb_ref, wa_ref, wb_ref, wo_ref, after_ref,
             dya_ref, dyb_ref, dgate_ref, do_ref, dcb_ref):
        dm = _nt(dx_ref[...], wo_ref[...])
        dga = (dm * ta_ref[...].astype(F32)).astype(BF16)
        dgb = (dm * tb_ref[...].astype(F32)).astype(BF16)
        for q, part in enumerate((dga[:, 0:H], dga[:, H:D], dgb[:, 0:H], dgb[:, H:D])):
            dgate_ref[q] = part
        dya_ref[...] = (dm * sa_ref[...].astype(F32)).astype(BF16)
        dyb_ref[...] = (dm * sb_ref[...].astype(F32)).astype(BF16)
        do_ref[...] = _nn(dya_ref[...], wa_ref[...])
        dcb_ref[...] = _nn(dyb_ref[...], wb_ref[...])

    row = lambda w: pl.BlockSpec((tm, w), lambda i: (i, 0))
    full = lambda a: pl.BlockSpec(a.shape, lambda i: (0,) * a.ndim)
    return _pallas_call(
        body, name="bwd_mix", grid=(L // tm,),
        in_specs=[row(D)] * 5 + [full(wat), full(wbt), full(wout), ANY],
        out_specs=[row(D), row(D), pl.BlockSpec((4, tm, H), lambda i: (1, i, 0)), row(H), row(H)],
        out_shape=[_sds((L, D), BF16)] * 2 + [_sds((DPROJ_BLOCKS, L, H), BF16)] + [_sds((L, H), F32)] * 2,
        compiler_params=_params(("parallel",), 56),
    )(dx1b, sig_a, sig_b, dm_dga, dm_dgb, wat, wbt, wout, after)


def _fwd_ffn_up(h2, wgt, wut):
    L, D = h2.shape
    F = wgt.shape[0]
    tn = F // 2
    tm = min(L, 512)

    def body(h_ref, wg_ref, wu_ref, sa_ref, sb_ref, s_ref):
        h = h_ref[...]
        a, b = _nt(h, wg_ref[...]), _nt(h, wu_ref[...])
        sg = _sigmoid(a)
        silu = a * sg
        sa_ref[...] = (b * sg * (1.0 + a * (1.0 - sg))).astype(BF16)
        sb_ref[...] = silu.astype(BF16)
        s_ref[...] = (silu * b).astype(BF16)

    wspec = pl.BlockSpec((tn, D), lambda j, i: (j, 0))
    ospec = pl.BlockSpec((tm, tn), lambda j, i: (i, j))
    return _pallas_call(
        body, name="fwd_ffn_up", grid=(2, L // tm),
        in_specs=[pl.BlockSpec((tm, D), lambda j, i: (i, 0)), wspec, wspec],
        out_specs=[ospec] * 3,
        out_shape=[_sds((L, F), BF16)] * 3,
        compiler_params=_params(("parallel", "parallel"), 48),
    )(h2, wgt, wut)


def _fwd_down_loss(s, wd, x1, target, g_final):
    L, D = x1.shape
    F = wd.shape[0]
    tm = min(L, 512)

    def body(s_ref, wd_ref, x1_ref, t_ref, g_ref, dx_ref, dxb_ref, red_ref):
        @pl.when(pl.program_id(0) == 0)
        def _():
            red_ref[...] = jnp.zeros_like(red_ref)

        g = g_ref[...]
        r, xh = _rms_stats(x1_ref[...] + _nn(s_ref[...], wd_ref[...]))
        e = xh * g - t_ref[...]
        dy = e * (1.0 / D)
        dx = _rms_bwd(dy * g, xh, r)
        dx_ref[...] = dx
        dxb_ref[...] = dx.astype(BF16)
        red_ref[0:1, :] += jnp.sum(dy * xh, axis=0, keepdims=True)
        red_ref[1:2, :] += jnp.broadcast_to(0.5 * jnp.sum(e * e) * (1.0 / D), (1, D))

    row = pl.BlockSpec((tm, D), lambda i: (i, 0))
    return _pallas_call(
        body, name="fwd_down_loss", grid=(L // tm,),
        in_specs=[pl.BlockSpec((tm, F), lambda i: (i, 0)), pl.BlockSpec((F, D), lambda i: (0, 0)),
                  row, row, pl.BlockSpec((1, D), lambda i: (0, 0))],
        out_specs=[row, row, pl.BlockSpec((8, D), lambda i: (0, 0))],
        out_shape=[_sds((L, D), F32), _sds((L, D), BF16), _sds((8, D), F32)],
        compiler_params=_params(("arbitrary",), 56),
    )(s, wd, x1, target, g_final)


def _bwd_down(dx2b, wd, s_a, s_b):
    L, D = dx2b.shape
    F = wd.shape[0]
    tn = F // 2
    tm = min(L, 512)

    def body(dx_ref, wd_ref, sa_ref, sb_ref, da_ref, db_ref):
        ds = _nt(dx_ref[...], wd_ref[...])
        da_ref[...] = (ds * sa_ref[...].astype(F32)).astype(BF16)
        db_ref[...] = (ds * sb_ref[...].astype(F32)).astype(BF16)

    ospec = pl.BlockSpec((tm, tn), lambda j, i: (i, j))
    return _pallas_call(
        body, name="bwd_down", grid=(2, L // tm),
        in_specs=[pl.BlockSpec((tm, D), lambda j, i: (i, 0)),
                  pl.BlockSpec((tn, D), lambda j, i: (j, 0)), ospec, ospec],
        out_specs=[ospec] * 2,
        out_shape=[_sds((L, F), BF16)] * 2,
        compiler_params=_params(("parallel", "parallel"), 48),
    )(dx2b, wd, s_a, s_b)


def _bwd_ffn_dh(da, db, wgt, wut, x1, dx2, g_ffn, after):
    L, D = x1.shape
    F = wgt.shape[0]
    tm = min(L, 256)

    def body(da_ref, db_ref, wg_ref, wu_ref, x1_ref, dx2_ref, g_ref, after_ref, dx_ref, dxb_ref, red_ref):
        @pl.when(pl.program_id(0) == 0)
        def _():
            red_ref[...] = jnp.zeros_like(red_ref)

        dh = _nn(da_ref[...], wg_ref[...]) + _nn(db_ref[...], wu_ref[...])
        r, xh = _rms_stats(x1_ref[...])
        red_ref[0:1, :] += jnp.sum(dh * xh, axis=0, keepdims=True)
        dx = dx2_ref[...] + _rms_bwd(dh * g_ref[...], xh, r)
        dx_ref[...] = dx
        dxb_ref[...] = dx.astype(BF16)

    row = pl.BlockSpec((tm, D), lambda i: (i, 0))
    aspec = pl.BlockSpec((tm, F), lambda i: (i, 0))
    wspec = pl.BlockSpec((F, D), lambda i: (0, 0))
    return _pallas_call(
        body, name="bwd_ffn_dh", grid=(L // tm,),
        in_specs=[aspec, aspec, wspec, wspec, row, row, pl.BlockSpec((1, D), lambda i: (0, 0)), ANY],
        out_specs=[row, row, pl.BlockSpec((8, D), lambda i: (0, 0))],
        out_shape=[_sds((L, D), F32), _sds((L, D), BF16), _sds((8, D), F32)],
        compiler_params=_params(("arbitrary",), 56),
    )(da, db, wgt, wut, x1, dx2, g_ffn, after)


def _bwd_in(dproj, w_int, x, dx1, g_mix, after):
    L, D = x.shape
    N = w_int.shape[0]
    H = dproj.shape[2]
    tm = min(L, 256)
    assert N == len(DPROJ_BLOCK_OF) * H

    def body(blocks_ref, w_ref, x_ref, dx1_ref, g_ref, after_ref, dx_ref, red_ref, dp_ref):
        @pl.when(pl.program_id(0) == 0)
        def _():
            red_ref[...] = jnp.zeros_like(red_ref)

        for t, block in enumerate(DPROJ_BLOCK_OF):
            dp_ref[:, t * H:(t + 1) * H] = blocks_ref[block]
        dh = _nn(dp_ref[...], w_ref[...])
        r, xh = _rms_stats(x_ref[...])
        red_ref[0:1, :] += jnp.sum(dh * xh, axis=0, keepdims=True)
        dx_ref[...] = dx1_ref[...] + _rms_bwd(dh * g_ref[...], xh, r)

    row = pl.BlockSpec((tm, D), lambda i: (i, 0))
    return _pallas_call(
        body, name="bwd_in", grid=(L // tm,),
        in_specs=[pl.BlockSpec((DPROJ_BLOCKS, tm, H), lambda i: (0, i, 0)), pl.BlockSpec((N, D), lambda i: (0, 0)),
                  row, row, pl.BlockSpec((1, D), lambda i: (0, 0)), ANY],
        out_specs=[row, pl.BlockSpec((8, D), lambda i: (0, 0))],
        out_shape=[_sds((L, D), F32), _sds((8, D), F32)],
        scratch_shapes=[pltpu.VMEM((tm, N), BF16)],
        compiler_params=_params(("arbitrary",), 56),
    )(dproj, w_int, x, dx1, g_mix, after)


def _dw_in(h, dproj, n_cols, c_idx):
    L, D = h.shape
    H = dproj.shape[2]
    tk = min(L, TK_TOKENS)
    nk = L // tk
    r2 = D // 2
    first = [(j * n_cols) // H for j in range(N_CHIPS)]
    last = [((j + 1) * n_cols - 1) // H for j in range(N_CHIPS)]
    slots = max(b - a for a, b in zip(first, last)) + 1
    plan = []
    for j in range(N_CHIPS):
        lo, hi = j * n_cols, (j + 1) * n_cols
        segments = []
        for s in range(last[j] - first[j] + 1):
            a, b = max(lo, (first[j] + s) * H), min(hi, (first[j] + s + 1) * H)
            segments.append((s, a - (first[j] + s) * H, b - a, a - lo))
        plan.append(segments)

    def body(c_ref, *refs):
        h_ref, slot_refs = refs[0], refs[1:1 + slots]
        o_ref, sib_ref, b_ref = refs[1 + slots:]
        j, k = pl.program_id(0), pl.program_id(1)
        for jj in range(N_CHIPS):
            @pl.when(j == jj)
            def _(jj=jj):
                for s, start, width, at in plan[jj]:
                    b_ref[:, at:at + width] = slot_refs[s][:, start:start + width]

        part = _tn(h_ref[...], b_ref[...])

        @pl.when(k == 0)
        def _():
            o_ref[...] = part

        @pl.when(k > 0)
        def _():
            o_ref[...] += part

        @pl.when(k == nk - 1)
        def _():
            theirs = pl.ds(pl.multiple_of((1 - c_ref[0]) * r2, 8), r2)
            sib_ref[...] = o_ref[theirs, :].astype(BF16)

    def slot_spec(s):
        blocks = [DPROJ_BLOCK_OF[min(first[j] + s, last[j])] for j in range(N_CHIPS)]

        def index(j, k, c_ref):
            block = blocks[0]
            for jj in range(1, N_CHIPS):
                block = jnp.where(j == jj, blocks[jj], block)
            return (block, k, 0)

        return pl.BlockSpec((None, tk, H), index)

    return _pallas_call(
        body, name="dw_in",
        grid_spec=pltpu.PrefetchScalarGridSpec(
            num_scalar_prefetch=1, grid=(N_CHIPS, nk),
            in_specs=[pl.BlockSpec((tk, D), lambda j, k, c_ref: (k, 0))] + [slot_spec(s) for s in range(slots)],
            out_specs=[pl.BlockSpec((None, D, n_cols), lambda j, k, c_ref: (j, 0, 0)),
                       pl.BlockSpec((None, r2, n_cols), lambda j, k, c_ref: (j, 0, 0))],
            scratch_shapes=[pltpu.VMEM((tk, n_cols), BF16)]),
        out_shape=[_sds((N_CHIPS, D, n_cols), F32), _sds((N_CHIPS, r2, n_cols), BF16)],
        compiler_params=_params(("parallel", "arbitrary"), 56),
    )(c_idx, h, *([dproj] * slots))


def _mm_tn(name, a, b, a_spec, b_spec, o_block, n_out, n_k):
    def body(a_ref, b_ref, o_ref):
        part = _tn(a_ref[...], b_ref[...])

        @pl.when(pl.program_id(1) == 0)
        def _():
            o_ref[...] = part

        @pl.when(pl.program_id(1) > 0)
        def _():
            o_ref[...] += part

    return _pallas_call(
        body, name=name, grid=(n_out, n_k),
        in_specs=[a_spec, b_spec],
        out_specs=pl.BlockSpec((None,) + o_block, lambda j, k: (j, 0, 0)),
        out_shape=_sds((n_out,) + o_block, F32),
        compiler_params=_params(("parallel", "arbitrary"), 56),
    )(a, b)


TK_TOKENS = 2048


def _dw_cols(name, a, b, n_cols):
    L, M = a.shape
    tk = min(L, TK_TOKENS)
    return _mm_tn(name, a, b, pl.BlockSpec((tk, M), lambda j, k: (k, 0)),
                  pl.BlockSpec((tk, n_cols), lambda j, k: (k, j)), (M, n_cols), N_CHIPS, L // tk)


def _dw_rows(name, a, b):
    L, M = a.shape
    N = b.shape[1]
    tk = min(L, TK_TOKENS)
    return _mm_tn(name, a, b, pl.BlockSpec((tk, M // N_CHIPS), lambda j, k: (k, j)),
                  pl.BlockSpec((tk, N), lambda j, k: (k, 0)), (M // N_CHIPS, N), N_CHIPS, L // tk)


def _dw_rows2(name, a, b):
    L, M = a.shape
    N = b.shape[1]
    tk = min(L, TK_TOKENS)
    return _mm_tn(name, a, b, pl.BlockSpec((tk, M // 2), lambda j, k: (k, j)),
                  pl.BlockSpec((tk, N), lambda j, k: (k, 0)), (M // 2, N), 2, L // tk)


def _place():
    x, y, c = lax.axis_index("x"), lax.axis_index("y"), lax.axis_index("c")
    chips = [(1 - x, y), (x, 1 - y), (1 - x, 1 - y)]
    return x, y, c, 2 * x + y, chips


def _remote(src, dst, send_sem, recv_sem, device):
    return pltpu.make_async_remote_copy(src_ref=src, dst_ref=dst, send_sem=send_sem,
                                        recv_sem=recv_sem, device_id=device, device_id_type=MESH)


def _half(ref, lead, c, r2):
    return ref.at[lead, pl.ds(pl.multiple_of(c * r2, 16), r2), :]


def _cast_place(name, ws, chip_idx):
    n = len(ws)

    def body(k_ref, *refs):
        for w_ref, o_ref in zip(refs[:n], refs[n:]):
            o_ref[...] = w_ref[...].astype(BF16)

    return _pallas_call(
        body, name=name,
        grid_spec=pltpu.PrefetchScalarGridSpec(
            num_scalar_prefetch=1, grid=(2,),
            in_specs=[pl.BlockSpec((w.shape[0] // 2, w.shape[1]), lambda i, k_ref: (i, 0)) for w in ws],
            out_specs=[pl.BlockSpec((None, w.shape[0] // 2, w.shape[1]), lambda i, k_ref: (k_ref[0], i, 0))
                       for w in ws]),
        out_shape=[_sds((N_CHIPS,) + w.shape, BF16) for w in ws],
        compiler_params=_params(("parallel",), 48),
    )(chip_idx, *ws)


def _cast_place_t(name, ws, chip_idx):
    n = len(ws)
    r, cols = ws[0].shape

    def body(k_ref, *refs):
        for w_ref, o_ref in zip(refs[:n], refs[n:]):
            o_ref[...] = w_ref[...].T.astype(BF16)

    return _pallas_call(
        body, name=name,
        grid_spec=pltpu.PrefetchScalarGridSpec(
            num_scalar_prefetch=1, grid=(cols // LANES,),
            in_specs=[pl.BlockSpec((r, LANES), lambda i, k_ref: (0, i))] * n,
            out_specs=[pl.BlockSpec((None, LANES, r), lambda i, k_ref: (k_ref[0], i, 0))] * n),
        out_shape=[_sds((N_CHIPS, cols, r), BF16)] * n,
        compiler_params=_params(("parallel",), 48),
    )(chip_idx, *ws)


def _gather_copies(bufs, whole, send_sems, recv_sems, select=None):
    x, y, c, k, chips = _place()
    pairs = []
    for w, buf in enumerate(bufs):
        for j, (cx, cy) in enumerate(chips):
            if select is not None and not select(w, j):
                continue
            if w in whole:
                mine, theirs = buf.at[k], buf.at[2 * cx + cy]
            else:
                r2 = buf.shape[1] // 2
                mine, theirs = _half(buf, k, c, r2), _half(buf, 2 * cx + cy, c, r2)
            sems = (send_sems.at[w * 3 + j], recv_sems.at[w * 3 + j])
            pairs.append((_remote(mine, mine, *sems, (cx, cy, c)), _remote(theirs, theirs, *sems, (x, y, c))))
    return pairs


def _gather_start(name, groups, after):
    flat = [b for bufs, _, _ in groups for b in bufs]
    nb, ng = len(flat), len(groups)

    def body(*refs):
        ins, sems, token = refs[:nb], refs[nb + 1:nb + 1 + 2 * ng], refs[-1]
        pos = 0
        for g, (bufs, whole, select) in enumerate(groups):
            for send, _ in _gather_copies(ins[pos:pos + len(bufs)], whole, sems[2 * g], sems[2 * g + 1], select):
                send.start()
            pos += len(bufs)
        token[...] = jnp.zeros_like(token)

    sem_shapes = []
    for bufs, _, _ in groups:
        sem_shapes += [pltpu.SemaphoreType.DMA((3 * len(bufs),))] * 2
    out = _pallas_call(
        body, name=name,
        in_specs=[HBM] * nb + [ANY], out_specs=tuple([SEM] * (2 * ng) + [HBM] * nb + [VMEM]),
        out_shape=tuple(sem_shapes + [pltpu.HBM(b.shape, b.dtype) for b in flat] + [_sds((8, LANES), F32)]),
        input_output_aliases={i: 2 * ng + i for i in range(nb)},
        compiler_params=pltpu.CompilerParams(has_side_effects=EFFECT),
    )(*flat, after)
    sems, thru, pos = [], [], 2 * ng
    for g, (bufs, _, _) in enumerate(groups):
        sems.append((out[2 * g], out[2 * g + 1]))
        thru.append(list(out[pos:pos + len(bufs)]))
        pos += len(bufs)
    return sems, thru, out[-1]


def _gather_wait(name, bufs, whole, sems, after, select=None):
    nb = len(bufs)

    def body(*refs):
        ins, send_sems, recv_sems = refs[:nb], refs[nb], refs[nb + 1]
        for send, arrival in _gather_copies(ins, whole, send_sems, recv_sems, select):
            send.wait_send()
            arrival.wait_recv()

    return _pallas_call(
        body, name=name,
        in_specs=[HBM] * nb + [SEM, SEM, ANY], out_specs=[HBM] * nb,
        out_shape=[pltpu.HBM(b.shape, b.dtype) for b in bufs],
        input_output_aliases={i: i for i in range(nb)},
        compiler_params=pltpu.CompilerParams(has_side_effects=EFFECT),
    )(*bufs, sems[0], sems[1], after)


def _gather_forward(name, bufs, sources=(0, 1, 2)):
    n = len(bufs)

    def body(*refs):
        outs = refs[n:2 * n]
        send_sems, recv_sems = refs[2 * n:]
        x, y, c, _, chips = _place()
        sends = []
        for w in range(n):
            r2 = outs[w].shape[1] // 2
            for j in sources:
                landed = _half(outs[w], 2 * chips[j][0] + chips[j][1], c, r2)
                sends.append(_remote(landed, landed, send_sems.at[w * 3 + j], recv_sems.at[w * 3 + j],
                                     (x, y, 1 - c)))
        for cp in sends:
            cp.start()
        for w in range(n):
            r2 = outs[w].shape[1] // 2
            for j in sources:
                got = _half(outs[w], 2 * chips[j][0] + chips[j][1], 1 - c, r2)
                _remote(got, got, send_sems.at[w * 3 + j], recv_sems.at[w * 3 + j], (x, y, c)).wait_recv()
        for cp in sends:
            cp.wait_send()

    return _pallas_call(
        body, name=name,
        in_specs=[ANY] * n, out_specs=[ANY] * n,
        out_shape=[_sds(b.shape, b.dtype) for b in bufs],
        input_output_aliases={i: i for i in range(n)},
        scratch_shapes=[pltpu.SemaphoreType.DMA((n * 3,)), pltpu.SemaphoreType.DMA((n * 3,))],
    )(*bufs)


def _rs_add(name, grads3, from_sibling, c_idx):
    n = len(grads3)

    def body(c_ref, *refs):
        for g_ref, s_ref, o_ref in zip(refs[:n], refs[n:2 * n], refs[2 * n:]):
            o_ref[...] = (g_ref[...] + s_ref[...].astype(F32)).astype(BF16)

    mine =[pl.BlockSpec((None,) + s.shape[1:], lambda k, c_ref: (k, c_ref[0], 0)) for s in from_sibling]
    whole = [pl.BlockSpec((None,) + s.shape[1:], lambda k, c_ref: (k, 0, 0)) for s in from_sibling]
    return _pallas_call(
        body, name=name,
        grid_spec=pltpu.PrefetchScalarGridSpec(num_scalar_prefetch=1, grid=(N_CHIPS,), in_specs=mine + whole,
                                               out_specs=whole),
        out_shape=[_sds(s.shape, BF16) for s in from_sibling],
        compiler_params=_params(("parallel",), 48),
    )(c_idx, *grads3, *from_sibling)


def _split_start(name, arrays, n_sems, pairs_fn):
    n = len(arrays)

    def body(*refs):
        for send, _ in pairs_fn(refs[:n], refs[n], refs[n + 1]):
            send.start()
        refs[-1][...] = jnp.zeros_like(refs[-1])

    out = _pallas_call(
        body, name=name,
        in_specs=[HBM] * n, out_specs=tuple([SEM, SEM] + [HBM] * n + [VMEM]),
        out_shape=tuple([pltpu.SemaphoreType.DMA((n_sems,))] * 2 + [pltpu.HBM(a.shape, a.dtype) for a in arrays]
                        + [_sds((8, LANES), F32)]),
        input_output_aliases={i: 2 + i for i in range(n)},
        compiler_params=pltpu.CompilerParams(has_side_effects=EFFECT),
    )(*arrays)
    return (out[0], out[1]), list(out[2:2 + n]), out[-1]


def _split_wait(name, sems, arrays, pairs_fn, after):
    n = len(arrays)

    def body(*refs):
        for send, arrival in pairs_fn(refs[:n], refs[n], refs[n + 1]):
            send.wait_send()
            arrival.wait_recv()

    return list(_pallas_call(
        body, name=name,
        in_specs=[HBM] * n + [SEM, SEM, ANY], out_specs=[HBM] * n,
        out_shape=[pltpu.HBM(a.shape, a.dtype) for a in arrays],
        input_output_aliases={i: i for i in range(n)},
        compiler_params=pltpu.CompilerParams(has_side_effects=EFFECT),
    )(*arrays, sems[0], sems[1], after))


def _forward_pairs(bufs, send_sems, recv_sems):
    x, y, c, _, chips = _place()
    pairs = []
    for w, buf in enumerate(bufs):
        r2 = buf.shape[1] // 2
        for j, (cx, cy) in enumerate(chips):
            landed, theirs = _half(buf, 2 * cx + cy, c, r2), _half(buf, 2 * cx + cy, 1 - c, r2)
            sems = (send_sems.at[w * 3 + j], recv_sems.at[w * 3 + j])
            pairs.append((_remote(landed, landed, *sems, (x, y, 1 - c)), _remote(theirs, theirs, *sems, (x, y, c))))
    return pairs


def _sibling_pairs(arrays, send_sems, recv_sems):
    x, y, c, _, _ = _place()
    n = len(arrays) // 2
    pairs = []
    for w in range(n):
        r2 = arrays[w].shape[1] // 2
        cp = _remote(_half(arrays[w], slice(None), 1 - c, r2), arrays[n + w], send_sems.at[w], recv_sems.at[w],
                     (x, y, 1 - c))
        pairs.append((cp, cp))
    return pairs


def _sibling_whole_pairs(arrays, send_sems, recv_sems):
    x, y, c, _, _ = _place()
    n = len(arrays) // 2
    pairs = []
    for w in range(n):
        cp = _remote(arrays[w], arrays[n + w], send_sems.at[w], recv_sems.at[w], (x, y, 1 - c))
        pairs.append((cp, cp))
    return pairs


def _ici_pairs(arrays, send_sems, recv_sems):
    x, y, c, _, chips = _place()
    n = len(arrays) // 2
    pairs = []
    for w in range(n):
        for j, (cx, cy) in enumerate(chips):
            cp = _remote(arrays[w].at[2 * cx + cy], arrays[n + w].at[j],
                         send_sems.at[w * 3 + j], recv_sems.at[w * 3 + j], (cx, cy, c))
            pairs.append((cp, cp))
    return pairs


def _rs_sum(name, partials, received, place_idx):
    n = len(partials)
    nb = 2
    blocks = [(p.shape[1] // nb, p.shape[2]) for p in partials]

    def body(idx_ref, *refs):
        for p_ref, r_ref, o_ref in zip(refs[:n], refs[n:2 * n], refs[2 * n:]):
            o_ref[...] = ((p_ref[...].astype(F32) + r_ref[0].astype(F32))
                          + (r_ref[1].astype(F32) + r_ref[2].astype(F32)))

    return _pallas_call(
        body, name=name,
        grid_spec=pltpu.PrefetchScalarGridSpec(
            num_scalar_prefetch=1, grid=(nb,),
            in_specs=[pl.BlockSpec((None,) + b, lambda i, idx: (idx[0], i, 0)) for b in blocks]
            + [pl.BlockSpec((3,) + b, lambda i, idx: (0, i, 0)) for b in blocks],
            out_specs=[pl.BlockSpec(b, lambda i, idx: (idx[1] * nb + i, 0)) for b in blocks]),
        out_shape=[_sds((2 * p.shape[1], p.shape[2]), F32) for p in partials],
        compiler_params=_params(("parallel",), 48),
    )(place_idx, *partials, *received)


def _share_pairs(arrays, send_sems, recv_sems):
    x, y, c, _, _ = _place()
    pairs = []
    for w, arr in enumerate(arrays):
        r2 = arr.shape[0] // 2
        mine = arr.at[pl.ds(pl.multiple_of(c * r2, 8), r2), :]
        theirs = arr.at[pl.ds(pl.multiple_of((1 - c) * r2, 8), r2), :]
        sems = (send_sems.at[w], recv_sems.at[w])
        pairs.append((_remote(mine, mine, *sems, (x, y, 1 - c)), _remote(theirs, theirs, *sems, (x, y, c))))
    return pairs


def _small_pack(red_mix, red_ffn, red_final, red_hg, g_conv):
    D = red_mix.shape[1]
    H = red_hg.shape[1]

    def body(mix_ref, ffn_ref, fin_ref, hg_ref, cv_ref, in_ref):
        in_ref[...] = jnp.zeros_like(in_ref)
        in_ref[0:1, :] = mix_ref[0:1, :]
        in_ref[1:2, :] = ffn_ref[0:1, :]
        in_ref[2:3, :] = fin_ref[0:1, :]
        gam = hg_ref[1:2, 0:HEAD_DIM]
        for h in range(1, H // HEAD_DIM):
            gam = gam + hg_ref[1:2, h * HEAD_DIM:(h + 1) * HEAD_DIM]
        in_ref[3:4, 0:HEAD_DIM] = gam
        in_ref[3:4, HEAD_DIM:2 * HEAD_DIM] = fin_ref[1:2, 0:HEAD_DIM]
        in_ref[4:5, 0:H] = hg_ref[0:1, :]
        in_ref[6:9, 0:H] = cv_ref[...]

    return _pallas_call(
        body, name="small_pack", pin=False,
        in_specs=[VMEM] * 5, out_specs=VMEM, out_shape=_sds((N_SMALL_ROWS, D), F32),
    )(red_mix, red_ffn, red_final, red_hg, g_conv)


def _small_pairs(arrays, send_sems, recv_sems):
    block, gathered = arrays
    x, y, c, _, _ = _place()
    me = 4 * x + 2 * y + c
    pairs = []
    for m in range(1, 8):
        px, py, pc = x ^ ((m >> 2) & 1), y ^ ((m >> 1) & 1), c ^ (m & 1)
        sems = (send_sems.at[m - 1], recv_sems.at[m - 1])
        pairs.append((_remote(block, gathered.at[me], *sems, (px, py, pc)),
                      _remote(block, gathered.at[4 * px + 2 * py + pc], *sems, (x, y, c))))
    return pairs


def _adamw_math(w, g, m, v):
    m = ADAM_B1 * m + (1.0 - ADAM_B1) * g
    v = ADAM_B2 * v + (1.0 - ADAM_B2) * jnp.square(g)
    m_hat = m / (1.0 - ADAM_B1 ** ADAM_STEP)
    v_hat = v / (1.0 - ADAM_B2 ** ADAM_STEP)
    delta = -ADAM_LR * (m_hat / (jnp.sqrt(v_hat) + ADAM_EPS) + ADAM_WD * w)
    return delta, m, v


def _adamw(name, gs, ws, ms, vs):
    n = len(gs)
    nb = 4

    def body(*refs):
        ins, outs = refs[:4 * n], refs[4 * n:]
        for j in range(n):
            g_ref, w_ref, m_ref, v_ref = ins[j], ins[n + j], ins[2 * n + j], ins[3 * n + j]
            go_ref, d_ref, mo_ref, vo_ref = outs[4 * j:4 * j + 4]
            g = g_ref[...]
            go_ref[...] = g
            d_ref[...], mo_ref[...], vo_ref[...] = _adamw_math(w_ref[...], g, m_ref[...], v_ref[...])

    blk = [pl.BlockSpec((g.shape[0] // nb, g.shape[1]), lambda i: (i, 0)) for g in gs]
    out = _pallas_call(
        body, name=name, grid=(nb,),
        in_specs=blk * 4, out_specs=[b for b in blk for _ in range(4)],
        out_shape=[_sds(g.shape, F32) for g in gs for _ in range(4)],
        compiler_params=_params(("parallel",), 56),
    )(*gs, *ws, *ms, *vs)
    return [list(out[4 * j:4 * j + 4]) for j in range(n)]


def _small_update(block, gathered, place_idx, ws, ms, vs):
    n = len(ws)
    H = ws[1].shape[1]

    def body(idx_ref, blk_ref, all_ref, *refs):
        w, m, v, outs, tot_ref = refs[:n], refs[n:2 * n], refs[2 * n:3 * n], refs[3 * n:-1], refs[-1]
        chip, me = idx_ref[0], idx_ref[1]
        tot = jnp.where(me == 0, blk_ref[...], all_ref[0])
        for d in range(1, 8):
            tot = tot + jnp.where(me == d, blk_ref[...], all_ref[d])
        tot_ref[...] = tot
        p0 = _lower_bound(w[1][...])
        dl0 = p0 * (1.0 - p0) * tot_ref[4:5, 0:H]
        conv = jnp.zeros((3, LANES), F32)
        for k in range(N_CHIPS):
            conv = jnp.where(chip == k, tot_ref[6:9, k * LANES:(k + 1) * LANES], conv)
        grads = [tot_ref[0:1, :], None, tot_ref[3:4, 0:HEAD_DIM], conv, tot_ref[1:2, :], tot_ref[2:3, :]]
        for p in range(n):
            g_ref, d_ref, mo_ref, vo_ref = outs[4 * p:4 * p + 4]
            if p == 1:
                for row, g in ((slice(0, 1), dl0), (slice(1, 2), -dl0)):
                    g_ref[row, :] = g
                    d_ref[row, :], mo_ref[row, :], vo_ref[row, :] = _adamw_math(
                        w[p][row, :], g, m[p][row, :], v[p][row, :])
            else:
                g_ref[...] = grads[p]
                d_ref[...], mo_ref[...], vo_ref[...] = _adamw_math(w[p][...], grads[p], m[p][...], v[p][...])
        outs[4 * n][...] = tot_ref[3:4, HEAD_DIM:2 * HEAD_DIM]

    full = lambda a: pl.BlockSpec(a.shape, lambda i, idx: (0,) * a.ndim)
    out_shape = [_sds(w.shape, F32) for w in ws for _ in range(4)] + [_sds((1, LANES), F32)]
    return _pallas_call(
        body, name="small_update",
        grid_spec=pltpu.PrefetchScalarGridSpec(
            num_scalar_prefetch=1, grid=(1,),
            in_specs=[full(block), full(gathered)] + [full(a) for a in ws + ms + vs],
            out_specs=[full(s) for s in out_shape],
            scratch_shapes=[pltpu.VMEM(block.shape, F32)]),
        out_shape=out_shape,
    )(place_idx, block, gathered, *ws, *ms, *vs)


def kernel(x, norm_mix_g, w_in, lower_bounds, hg_norm_g, conv_w, w_branch_a, w_branch_b, w_out, norm_ffn_g, w_ffn_gate, w_ffn_up, w_ffn_down, norm_final_g, loss_target, m_norm_mix_g, m_w_in, m_lower_bounds, m_hg_norm_g, m_conv_w, m_w_branch_a, m_w_branch_b, m_w_out, m_norm_ffn_g, m_w_ffn_gate, m_w_ffn_up, m_w_ffn_down, m_norm_final_g, v_norm_mix_g, v_w_in, v_lower_bounds, v_hg_norm_g, v_conv_w, v_w_branch_a, v_w_branch_b, v_w_out, v_norm_ffn_g, v_w_ffn_gate, v_w_ffn_up, v_w_ffn_down, v_norm_final_g):
    _, L, D = x.shape
    H = D // 2
    assert lower_bounds.shape == (2, H) and hg_norm_g.shape == (1, HEAD_DIM)
    assert conv_w.shape == (1, 3, LANES) and w_in.shape[2] * N_CHIPS == 11 * H
    x2d, target = x.reshape(L, D), loss_target.reshape(L, D)
    g_final = norm_final_g.reshape(1, D)
    chip = 2 * lax.axis_index("x") + lax.axis_index("y")
    core = lax.axis_index("c")

    tr = lambda w: jnp.transpose(w[0])
    big = [w_in[0], w_branch_a[0], w_branch_b[0], w_out[0], tr(w_ffn_gate), tr(w_ffn_up), w_ffn_down[0]]
    big_m = [m_w_in[0], m_w_branch_a[0], m_w_branch_b[0], m_w_out[0], tr(m_w_ffn_gate), tr(m_w_ffn_up),
             m_w_ffn_down[0]]
    big_v = [v_w_in[0], v_w_branch_a[0], v_w_branch_b[0], v_w_out[0], tr(v_w_ffn_gate), tr(v_w_ffn_up),
             v_w_ffn_down[0]]
    names = ["w_in", "w_branch_a", "w_branch_b", "w_out", "w_ffn_gate", "w_ffn_up", "w_ffn_down"]

    chip_idx = chip.reshape(1).astype(jnp.int32)
    def per_shape(fn, tag, js, *lists):
        groups = {}
        for pos, a in enumerate(lists[0]):
            groups.setdefault(a.shape, []).append(pos)
        results = [None] * len(js)
        for same in groups.values():
            out = fn(tag + names[js[same[0]]], *[[xs[p] for p in same] for xs in lists])
            for q, p in enumerate(same):
                results[p] = out[q]
        return results

    place_t = lambda name, ws: _cast_place_t(name, ws, chip_idx)
    placed = per_shape(place_t, "place_", [0, 1, 2], big[:3]) + list(_cast_place("place_rest", big[3:], chip_idx))
    conv_placed = lax.dynamic_update_slice(jnp.zeros((N_CHIPS, 3, LANES), F32), conv_w, (chip, 0, 0))
    x_i, y_i = lax.axis_index("x"), lax.axis_index("y")
    blocks = lambda *ks: jnp.stack(ks).astype(jnp.int32)
    near = lambda w, j: j < 2
    far = lambda w, j: w == 1 or j == 2
    near_sems, in_flight, _ = _gather_start("gather_start_near", [([placed[0]], set(), near)], chip_idx)
    w_in_buf = in_flight[0][0]
    h, proj = _fwd_proj_first(x2d, norm_mix_g, w_in_buf, blocks(chip))
    sems, in_flight, _ = _gather_start(
        "gather_start_rest", [([w_in_buf, conv_placed], {1}, far), (placed[1:4], set(), None),
                              (placed[4:], set(), None)], h)
    w_in_buf, conv_buf = in_flight[0]
    (w_in_buf,) = _gather_wait("gather_wait_in_near", [w_in_buf], set(), near_sems[0], h, near)
    (w_in_buf,) = _gather_forward("gather_fwd_in_near", [w_in_buf], (0, 1))
    proj = _fwd_proj_more("fwd_proj_near", h, w_in_buf, proj,
                          blocks(2 * (1 - x_i) + y_i, 2 * x_i + (1 - y_i)))
    w_in_buf, conv_all = _gather_wait("gather_wait_in_far", [w_in_buf, conv_buf], {1}, sems[0], proj, far)
    (w_int3,) = _gather_forward("gather_fwd_in_far", [w_in_buf], (2,))
    proj = _fwd_proj_more("fwd_proj_far", h, w_int3, proj, blocks(2 * (1 - x_i) + (1 - y_i)))
    w_int = w_int3.reshape(-1, D)
    conv_full = jnp.transpose(conv_all, (1, 0, 2)).reshape(3, H)
    og, o_pre, s_saved = _hgrn_fwd(proj, lower_bounds, hg_norm_g, H)
    landed = _gather_wait("gather_wait_mix", in_flight[1], set(), sems[1], og)
    fwd_sems, landed, token = _split_start("gather_fwd_mix_start", landed, 9, _forward_pairs)
    cb = _conv_fwd(proj, conv_full, H, token)
    wat3, wbt3, wout3 = _split_wait("gather_fwd_mix_wait", fwd_sems, landed, _forward_pairs, cb)
    wat, wbt, wout = wat3.reshape(D, H), wbt3.reshape(D, H), wout3.reshape(D, D)
    landed = _gather_wait("gather_wait_ffn", in_flight[2], set(), sems[2], cb)
    fwd_sems, landed, token = _split_start("gather_fwd_ffn_start", landed, 9, _forward_pairs)
    sig_a, sig_b, dm_dga, dm_dgb, merged, x1, h2 = _fwd_mix(og, cb, proj, x2d, wat, wbt, wout, norm_ffn_g,
                                                              H, token)
    wgt3, wut3, wd3 = _split_wait("gather_fwd_ffn_wait", fwd_sems, landed, _forward_pairs, h2)
    d_ff = N_CHIPS * wd3.shape[1]
    wgt, wut, wd = wgt3.reshape(d_ff, D), wut3.reshape(d_ff, D), wd3.reshape(d_ff, D)
    ffn_ds_da, ffn_ds_db, ffn_s = _fwd_ffn_up(h2, wgt, wut)
    dx2, dx2b, red_final = _fwd_down_loss(ffn_s, wd, x1, target, g_final)

    c_idx = core.reshape(1).astype(jnp.int32)
    place_idx = jnp.stack([chip, core]).astype(jnp.int32)

    def sibling_start(tag, grads):
        bufs = [lax.empty((N_CHIPS, g.shape[1] // 2, g.shape[2]), F32) for g in grads]
        return _split_start("rs_sibling_start_" + tag, list(grads) + bufs, len(grads), _sibling_pairs)

    def ici_start(tag, js, grads, from_sibling):
        partials = list(_rs_add("rs_add_" + tag, grads, from_sibling, c_idx))
        landings = [lax.empty((3,) + p.shape[1:], BF16) for p in partials]
        return _split_start("rs_ici_start_" + tag, partials + landings, 3 * len(js), _ici_pairs)

    def ici_start_behind(tag, js, started, after):
        n = len(js)
        arrays = _split_wait("rs_sibling_wait_" + tag, started[0], started[1], _sibling_pairs, after)
        return ici_start(tag, js, arrays[:n], arrays[n:])

    def sums(tag, started, after):
        partials, received = [], []
        for group, group_js, start in started:
            arrays = _split_wait("rs_ici_wait_" + group, start[0], start[1], _ici_pairs, after)
            partials += arrays[:len(group_js)]
            received += arrays[len(group_js):]
        return list(_rs_sum("rs_sum_" + tag, partials, received, place_idx))

    def adamw(tag, js, grads):
        return _adamw("adamw_" + tag, grads, *[[src[j] for j in js] for src in (big, big_m, big_v)])

    shards3 = lambda g: g.reshape(N_CHIPS, d_ff // N_CHIPS, D)
    da, db = _bwd_down(dx2b, wd, ffn_ds_da, ffn_ds_db)
    g_wd = shards3(_dw_rows2("dw_ffn_down", ffn_s, dx2b))
    g_wg = shards3(_dw_rows2("dw_ffn_gate", da, h2))
    g_wu = shards3(_dw_rows2("dw_ffn_up", db, h2))
    ffn_sibling = sibling_start("ffn", [g_wg, g_wu, g_wd])
    dx1, dx1b, red_ffn = _bwd_ffn_dh(da, db, wgt, wut, x1, dx2, norm_ffn_g, ffn_sibling[2])
    ffn_ici = ici_start_behind("ffn", [4, 5, 6], ffn_sibling, dx1b)
    dya, dyb, dproj, d_o, d_cb = _bwd_mix(dx1b, sig_a, sig_b, dm_dga, dm_dgb, wat, wbt, wout, H, ffn_ici[2])
    g_wout = _dw_rows("dw_out", merged, dx1b)
    g_wa = _dw_cols("dw_branch_a", og, dya, D // N_CHIPS)
    g_wb = _dw_cols("dw_branch_b", cb, dyb, D // N_CHIPS)
    mix_sibling = sibling_start("mix", [g_wa, g_wb, g_wout])
    dproj, red_hg = _hgrn_bwd(proj, lower_bounds, hg_norm_g, o_pre, d_o, s_saved, H, mix_sibling[2], dproj)
    mix_ici = ici_start_behind("mix", [1, 2, 3], mix_sibling, red_hg)
    dproj, g_conv = _conv_bwd(proj, conv_full, d_cb, H, mix_ici[2], dproj)
    g_win, for_sibling = _dw_in(h, dproj, w_int3.shape[1], c_idx)
    in_sibling = _split_start("rs_sibling_start_in", [for_sibling, lax.empty(for_sibling.shape, BF16)], 1,
                              _sibling_whole_pairs)
    halves = sums("rest", [("mix", [1, 2, 3], mix_ici), ("ffn", [4, 5, 6], ffn_ici)], in_sibling[2])
    rest_share = _split_start("rs_share_start_rest", halves, len(halves), _share_pairs)
    from_sibling = _split_wait("rs_sibling_wait_in", in_sibling[0], in_sibling[1], _sibling_whole_pairs,
                               rest_share[2])[1]
    in_ici = ici_start("in", [0], [g_win], [from_sibling])
    grad_x, red_mix = _bwd_in(dproj, w_int, x2d, dx1, norm_mix_g, in_ici[2])
    in_share = _split_start("rs_share_start_in", sums("in", [("in", [0], in_ici)], grad_x), 1, _share_pairs)
    small_block = _small_pack(red_mix, red_ffn, red_final, red_hg, g_conv)
    small = _split_start("small_gather_start", [small_block, lax.empty((8,) + small_block.shape, F32)], 7,
                         _small_pairs)
    rest_grads = _split_wait("rs_share_wait_rest", rest_share[0], rest_share[1], _share_pairs, small[2])
    big_out = [None] + adamw("rest", [1, 2, 3, 4, 5, 6], rest_grads)
    in_grad = _split_wait("rs_share_wait_in", in_share[0], in_share[1], _share_pairs, big_out[6][0])
    big_out[0] = adamw("in", [0], in_grad)[0]
    small_block, small_all = _split_wait("small_gather_wait", small[0], small[1], _small_pairs, big_out[0][0])

    def smalls(mix, lb, hg, cw, ffn, fin):
        return [mix, lb, hg, cw[0], ffn, fin.reshape(1, D)]

    small_out = _small_update(
        small_block, small_all, jnp.stack([chip, 4 * x_i + 2 * y_i + core]).astype(jnp.int32),
        smalls(norm_mix_g, lower_bounds, hg_norm_g, conv_w, norm_ffn_g, norm_final_g),
        smalls(m_norm_mix_g, m_lower_bounds, m_hg_norm_g, m_conv_w, m_norm_ffn_g, m_norm_final_g),
        smalls(v_norm_mix_g, v_lower_bounds, v_hg_norm_g, v_conv_w, v_norm_ffn_g, v_norm_final_g))

    def outputs(i):
        big_i = [big_out[j][i] for j in range(7)]
        mix, lb, hg, cw, ffn, fin = [small_out[4 * p + i] for p in range(6)]
        return [mix, big_i[0][None], lb, hg, cw[None], big_i[1][None], big_i[2][None], big_i[3][None], ffn,
                big_i[4].T[None], big_i[5].T[None], big_i[6][None], fin.reshape(D)]

    outs = [small_out[24][0, 0], grad_x.reshape(1, L, D)]
    for i in range(4):
        outs += outputs(i)
    return tuple(outs)
```

```python
import functools

import jax
import jax.numpy as jnp
from jax import lax
from jax.experimental import pallas as pl
from jax.experimental.pallas import tpu as pltpu

F32 = jnp.float32
BF16 = jnp.bfloat16
EPS = 1e-6
CHUNK = 32
HEAD_DIM = 128
LANES = 128
N_CHIPS = 4
N_SMALL_ROWS = 16
DPROJ_BLOCKS = 12
DPROJ_BLOCK_OF = (0, 1, 2, 3, 8, 9, 10, 4, 5, 6, 7)

ADAM_LR = 0.001
ADAM_B1 = 0.9
ADAM_B2 = 0.999
ADAM_EPS = 1e-08
ADAM_WD = 0.01
ADAM_STEP = 10

MESH = pl.DeviceIdType.MESH
ANY = pl.BlockSpec(memory_space=pl.ANY)
VMEM = pl.BlockSpec(memory_space=pltpu.VMEM)
HBM = pl.BlockSpec(memory_space=pltpu.HBM)
SEM = pl.BlockSpec(memory_space=pltpu.SEMAPHORE)
EFFECT = pltpu.SideEffectType.DATAFLOW_SIDE_EFFECTING


def _sds(shape, dtype):
    return jax.ShapeDtypeStruct(shape, dtype)


def _pallas_call(body, pin=True, **kwargs):
    if not pin:
        return pl.pallas_call(body, **kwargs)
    in_hbm = lambda s: pltpu.HBM(s.shape, s.dtype) if isinstance(s, jax.ShapeDtypeStruct) else s
    kwargs["out_shape"] = jax.tree.map(in_hbm, kwargs["out_shape"])
    call = pl.pallas_call(body, **kwargs)

    def run(*args):
        return call(*[pltpu.with_memory_space_constraint(a, pltpu.HBM) if a.dtype in (F32, BF16) else a
                      for a in args])

    return run


def _params(semantics, vmem_mb):
    return pltpu.CompilerParams(dimension_semantics=semantics, vmem_limit_bytes=vmem_mb << 20)


def _nn(a, b):
    return lax.dot_general(a, b, (((1,), (0,)), ((), ())), preferred_element_type=F32)


def _nt(a, b):
    return lax.dot_general(a, b, (((1,), (1,)), ((), ())), preferred_element_type=F32)


def _tn(a, b):
    return lax.dot_general(a, b, (((0,), (0,)), ((), ())), preferred_element_type=F32)


def _sigmoid(x):
    return jax.nn.sigmoid(x)


def _rms_stats(x):
    r = lax.rsqrt(jnp.mean(x * x, axis=-1, keepdims=True) + EPS)
    return r, x * r


def _rms_bwd(dxh, xh, r):
    return r * (dxh - xh * jnp.mean(dxh * xh, axis=-1, keepdims=True))


def _fwd_proj_first(x, g_mix, w_int3, block):
    L, D = x.shape
    tn = w_int3.shape[1]
    tm = min(L, 1024)

    def body(blk_ref, x_ref, g_ref, w_ref, h_ref, p_ref):
        _, xh = _rms_stats(x_ref[...])
        h = (xh * g_ref[...]).astype(BF16)
        h_ref[...] = h
        p_ref[...] = _nt(h, w_ref[...])

    return _pallas_call(
        body, name="fwd_proj_own",
        grid_spec=pltpu.PrefetchScalarGridSpec(
            num_scalar_prefetch=1, grid=(L // tm,),
            in_specs=[pl.BlockSpec((tm, D), lambda i, blk: (i, 0)),
                      pl.BlockSpec((1, D), lambda i, blk: (0, 0)),
                      pl.BlockSpec((None, tn, D), lambda i, blk: (blk[0], 0, 0))],
            out_specs=[pl.BlockSpec((tm, D), lambda i, blk: (i, 0)),
                       pl.BlockSpec((tm, tn), lambda i, blk: (i, blk[0]))]),
        out_shape=[_sds((L, D), BF16), _sds((L, N_CHIPS * tn), F32)],
        compiler_params=_params(("parallel",), 48),
    )(block, x, g_mix, w_int3)


def _fwd_proj_more(name, h, w_int3, proj, blocks):
    L, D = h.shape
    tn = w_int3.shape[1]
    tm = min(L, 1024)

    def body(blk_ref, h_ref, w_ref, proj_ref, p_ref):
        p_ref[...] = _nt(h_ref[...], w_ref[...])

    return _pallas_call(
        body, name=name,
        grid_spec=pltpu.PrefetchScalarGridSpec(
            num_scalar_prefetch=1, grid=(L // tm, blocks.shape[0]),
            in_specs=[pl.BlockSpec((tm, D), lambda i, j, blk: (i, 0)),
                      pl.BlockSpec((None, tn, D), lambda i, j, blk: (blk[j], 0, 0)), ANY],
            out_specs=pl.BlockSpec((tm, tn), lambda i, j, blk: (i, blk[j]))),
        out_shape=_sds(proj.shape, proj.dtype),
        input_output_aliases={3: 0},
        compiler_params=_params(("parallel", "arbitrary"), 48),
    )(blocks, h, w_int3, proj)


def _lower_bound(lbp):
    l0, l1 = lbp[0:1, :], lbp[1:2, :]
    m = jnp.maximum(l0, l1)
    e0, e1 = jnp.exp(l0 - m), jnp.exp(l1 - m)
    return e0 / (e0 + e1)


def _seg_scan(x, r32, forward):
    n = x.shape[0]
    s = 1
    while s < CHUNK:
        if forward:
            x = x + jnp.where(r32 >= s, pltpu.roll(x, s, 0), 0.0)
        else:
            x = x + jnp.where(r32 < CHUNK - s, pltpu.roll(x, n - s, 0), 0.0)
        s *= 2
    return x


def _bcast_row(x, row):
    n, w = x.shape
    nc = n // CHUNK
    x3 = x.reshape(nc, CHUNK, w)
    return jnp.broadcast_to(x3[:, row:row + 1, :], (nc, CHUNK, w)).reshape(n, w)


def _chunk_total(x):
    n, w = x.shape
    nc = n // CHUNK
    total = jnp.sum(x.reshape(nc, CHUNK, w), axis=1, keepdims=True)
    return jnp.broadcast_to(total, (nc, CHUNK, w)).reshape(n, w)


def _hgrn_prep(q_raw, f_raw, lb):
    r32 = lax.broadcasted_iota(jnp.int32, f_raw.shape, 0) & (CHUNK - 1)
    sig = _sigmoid(f_raw)
    f = lb + (1.0 - lb) * sig
    b = _seg_scan(jnp.log(f), r32, True)
    a = _bcast_row(b, CHUNK // 2 - 1)
    bl = _bcast_row(b, CHUNK - 1)
    sq = _sigmoid(q_raw)
    q = q_raw * sq * (HEAD_DIM ** -0.5)
    return dict(r32=r32, sig=sig, f=f, k=1.0 - f, b=b, a=a, bl=bl, sq=sq, q=q)


def _chunk_masks(n):
    ri = lax.broadcasted_iota(jnp.int32, (n, n), 0)
    ci = lax.broadcasted_iota(jnp.int32, (n, n), 1)
    same = (ri // CHUNK) == (ci // CHUNK)
    return same & (ci <= ri), same & (ri <= ci)


def _hgrn_fwd(proj, lower_bounds, gamma, H):
    L = proj.shape[0]
    nh = H // HEAD_DIM
    TL = min(L, 256)
    nc = TL // CHUNK

    def body(q_ref, f_ref, v_ref, g_ref, lbp_ref, gam_ref, og_ref, o_ref, s_ref, st_ref):
        @pl.when(pl.program_id(0) == 0)
        def _():
            st_ref[...] = jnp.zeros_like(st_ref)

        lb = _lower_bound(lbp_ref[...])
        gam = gam_ref[...]
        mask, _ = _chunk_masks(TL)
        rowc = lax.broadcasted_iota(jnp.int32, (TL, HEAD_DIM), 0) // CHUNK
        for h in range(nh):
            hs = slice(h * HEAD_DIM, (h + 1) * HEAD_DIM)
            p = _hgrn_prep(q_ref[:, hs], f_ref[:, hs], lb[:, hs])
            v = v_ref[:, hs]
            vb = v.astype(BF16)
            vt = v.T.astype(BF16)
            q_hat = (p["q"] * jnp.exp(p["b"] - p["a"])).astype(BF16)
            k_hat = (p["k"] * jnp.exp(p["a"] - p["b"])).astype(BF16)
            q_in = (p["q"] * jnp.exp(p["b"])).astype(BF16)
            k_out = (p["k"] * jnp.exp(p["bl"] - p["b"])).astype(BF16)
            dec = jnp.exp(p["bl"])
            att = jnp.where(mask, _nt(q_hat, k_hat), 0.0).astype(BF16)
            o_intra = _nn(att, vb)
            st = st_ref[h]
            for c in range(nc):
                rs = slice(c * CHUNK, (c + 1) * CHUNK)
                stb = st.astype(BF16)
                s_ref[c, h] = stb
                o_ref[rs, hs] = o_intra[rs] + _nt(q_in[rs], stb)
                k_c = jnp.where(rowc == c, k_out, jnp.zeros_like(k_out))
                st = st * dec[c * CHUNK:c * CHUNK + 1, :] + _nn(vt, k_c)
            st_ref[h] = st
            o = o_ref[:, hs]
            _, xh = _rms_stats(o)
            gr = g_ref[:, hs]
            og_ref[:, hs] = (xh * gam * (gr * _sigmoid(gr))).astype(BF16)

    col = lambda k: pl.BlockSpec((TL, H), lambda i, k=k: (i, k))
    return _pallas_call(
        body, name="hgrn_fwd", grid=(L // TL,),
        in_specs=[col(0), col(1), col(2), col(3),
                  pl.BlockSpec(lower_bounds.shape, lambda i: (0, 0)),
                  pl.BlockSpec(gamma.shape, lambda i: (0, 0))],
        out_specs=[pl.BlockSpec((TL, H), lambda i: (i, 0)),
                   pl.BlockSpec((TL, H), lambda i: (i, 0)),
                   pl.BlockSpec((nc, nh, HEAD_DIM, HEAD_DIM), lambda i: (i, 0, 0, 0))],
        out_shape=[_sds((L, H), BF16), _sds((L, H), F32),
                   _sds((L // CHUNK, nh, HEAD_DIM, HEAD_DIM), BF16)],
        scratch_shapes=[pltpu.VMEM((nh, HEAD_DIM, HEAD_DIM), F32)],
        compiler_params=_params(("arbitrary",), 48),
    )(proj, proj, proj, proj, lower_bounds, gamma)


def _hgrn_bwd(proj, lower_bounds, gamma, o_pre, d_out, s_saved, H, after, dproj):
    L = proj.shape[0]
    nh = H // HEAD_DIM
    TL = min(L, 256)
    nc = TL // CHUNK
    nt = L // TL

    def body(q_ref, f_ref, v_ref, g_ref, lbp_ref, gam_ref, o_ref, d_ref, s_ref, after_ref, dproj_ref,
             dp_ref, red_ref, dst_ref, dsall_ref, tmp_ref):
        @pl.when(pl.program_id(0) == 0)
        def _():
            dst_ref[...] = jnp.zeros_like(dst_ref)
            red_ref[...] = jnp.zeros_like(red_ref)

        lb = _lower_bound(lbp_ref[...])
        gam = gam_ref[...]
        mask, mask_t = _chunk_masks(TL)
        rowc = lax.broadcasted_iota(jnp.int32, (TL, HEAD_DIM), 0) // CHUNK
        for h in range(nh):
            hs = slice(h * HEAD_DIM, (h + 1) * HEAD_DIM)
            qr, gr, lbh = q_ref[:, hs], g_ref[:, hs], lb[:, hs]
            p = _hgrn_prep(qr, f_ref[:, hs], lbh)
            vb = v_ref[:, hs].astype(BF16)
            eba, eab = jnp.exp(p["b"] - p["a"]), jnp.exp(p["a"] - p["b"])
            eb, elb = jnp.exp(p["b"]), jnp.exp(p["bl"] - p["b"])
            dec = jnp.exp(p["bl"])
            q_hat, k_hat = p["q"] * eba, p["k"] * eab
            q_in, k_out = p["q"] * eb, p["k"] * elb
            q_hat_b, k_hat_b = q_hat.astype(BF16), k_hat.astype(BF16)
            q_in_b, k_out_b = q_in.astype(BF16), k_out.astype(BF16)

            o, dout = o_ref[:, hs], d_ref[:, hs]
            sg = _sigmoid(gr)
            r, xh = _rms_stats(o)
            dp_ref[3, :, hs] = (dout * (xh * gam) * (sg * (1.0 + gr * (1.0 - sg)))).astype(BF16)
            dn = dout * (gr * sg)
            red_ref[1:2, hs] += jnp.sum(dn * xh, axis=0, keepdims=True)
            do = _rms_bwd(dn * gam, xh, r)
            dob = do.astype(BF16)
            dot_b = do.T.astype(BF16)

            att_t = jnp.where(mask_t, _nt(k_hat_b, q_hat_b), 0.0).astype(BF16)
            dv_intra = _nn(att_t, dob)
            datt = jnp.where(mask, _nt(dob, vb), 0.0).astype(BF16)
            dqh = _nn(datt, k_hat_b)
            datt_t = jnp.where(mask_t, _nt(vb, dob), 0.0).astype(BF16)
            dkh = _nn(datt_t, q_hat_b)

            dst = dst_ref[h]
            for c in reversed(range(nc)):
                dsall_ref[c] = dst
                q_c = jnp.where(rowc == c, q_in_b, jnp.zeros_like(q_in_b))
                dst = dst * dec[c * CHUNK:c * CHUNK + 1, :] + _nn(dot_b, q_c)
            dst_ref[h] = dst
            for c in range(nc):
                rs = slice(c * CHUNK, (c + 1) * CHUNK)
                ds_c = dsall_ref[c]
                dsb = ds_c.astype(BF16)
                st_prev = s_ref[c, h]
                tmp_ref[0, rs, :] = _nt(k_out_b[rs], dsb)
                tmp_ref[1, rs, :] = _nn(vb[rs], dsb)
                tmp_ref[2, rs, :] = _nn(dob[rs], st_prev)
                ddec = jnp.sum(ds_c * st_prev.astype(F32), axis=0, keepdims=True)
                tmp_ref[3, rs, :] = jnp.broadcast_to(ddec * dec[c * CHUNK:c * CHUNK + 1, :],
                                                     (CHUNK, HEAD_DIM))
            dko, dqi = tmp_ref[1], tmp_ref[2]
            dq = dqh * eba + dqi * eb
            dk = dkh * eab + dko * elb
            tko = dko * k_out
            db = dqh * q_hat - dkh * k_hat + dqi * q_in - tko
            dlog = _seg_scan(db, p["r32"], False) + _chunk_total(tko) + tmp_ref[3]
            df = dlog / p["f"] - dk
            sig = p["sig"]
            red_ref[0:1, hs] += jnp.sum(df * (1.0 - sig), axis=0, keepdims=True)
            dp_ref[1, :, hs] = (df * (1.0 - lbh) * sig * (1.0 - sig)).astype(BF16)
            sq = p["sq"]
            dp_ref[0, :, hs] = (dq * (HEAD_DIM ** -0.5) * (sq * (1.0 + qr * (1.0 - sq)))).astype(BF16)
            dp_ref[2, :, hs] = (dv_intra + tmp_ref[0]).astype(BF16)

    col = lambda k: pl.BlockSpec((TL, H), lambda i, k=k: (nt - 1 - i, k))
    rev = pl.BlockSpec((TL, H), lambda i: (nt - 1 - i, 0))
    return _pallas_call(
        body, name="hgrn_bwd", grid=(nt,),
        in_specs=[col(0), col(1), col(2), col(3),
                  pl.BlockSpec(lower_bounds.shape, lambda i: (0, 0)),
                  pl.BlockSpec(gamma.shape, lambda i: (0, 0)),
                  rev, rev,
                  pl.BlockSpec((nc, nh, HEAD_DIM, HEAD_DIM), lambda i: (nt - 1 - i, 0, 0, 0)), ANY, ANY],
        out_specs=[pl.BlockSpec((4, TL, H), lambda i: (0, nt - 1 - i, 0)), pl.BlockSpec((8, H), lambda i: (0, 0))],
        out_shape=[_sds(dproj.shape, BF16), _sds((8, H), F32)],
        input_output_aliases={10: 0},
        scratch_shapes=[pltpu.VMEM((nh, HEAD_DIM, HEAD_DIM), F32),
                        pltpu.VMEM((nc, HEAD_DIM, HEAD_DIM), F32),
                        pltpu.VMEM((4, TL, HEAD_DIM), F32)],
        compiler_params=_params(("arbitrary",), 48),
    )(proj, proj, proj, proj, lower_bounds, gamma, o_pre, d_out, s_saved, after, dproj)


def _shift_down(u, s, row):
    return jnp.where(row >= s, pltpu.roll(u, s, 0), 0.0)


def _shift_up(u, s, row):
    n = u.shape[0]
    return jnp.where(row < n - s, pltpu.roll(u, n - s, 0), 0.0)


def _conv_specs(L, H):
    per = H // LANES
    return [pl.BlockSpec((L, LANES), lambda j, o=o: (0, o * per + j)) for o in (4, 5, 6)]


def _conv_fwd(proj, conv_w, H, after):
    L = proj.shape[0]

    def body(c_ref, b_ref, x_ref, w_ref, after_ref, o_ref):
        row = lax.broadcasted_iota(jnp.int32, (L, LANES), 0)
        u = c_ref[...] * x_ref[...]
        w = w_ref[...]
        y = w[0:1] * _shift_down(u, 2, row) + w[1:2] * _shift_down(u, 1, row) + w[2:3] * u
        o_ref[...] = (b_ref[...] * y).astype(BF16)

    return _pallas_call(
        body, name="conv_fwd", grid=(H // LANES,),
        in_specs=_conv_specs(L, H) + [pl.BlockSpec((3, LANES), lambda j: (0, j)), ANY],
        out_specs=pl.BlockSpec((L, LANES), lambda j: (0, j)),
        out_shape=_sds((L, H), BF16),
        compiler_params=_params(("parallel",), 48),
    )(proj, proj, proj, conv_w, after)


def _conv_bwd(proj, conv_w, dcb, H, after, dproj):
    L = proj.shape[0]

    def body(c_ref, b_ref, x_ref, w_ref, d_ref, after_ref, dproj_ref, dp_ref, dw_ref):
        row = lax.broadcasted_iota(jnp.int32, (L, LANES), 0)
        cg, xb = c_ref[...], x_ref[...]
        u = cg * xb
        u1, u2 = _shift_down(u, 1, row), _shift_down(u, 2, row)
        w = w_ref[...]
        y = w[0:1] * u2 + w[1:2] * u1 + w[2:3] * u
        d = d_ref[...]
        dp_ref[1] = (d * y).astype(BF16)
        dy = d * b_ref[...]
        du = w[2:3] * dy + w[1:2] * _shift_up(dy, 1, row) + w[0:1] * _shift_up(dy, 2, row)
        dw_ref[0:1, :] = jnp.sum(dy * u2, axis=0, keepdims=True)
        dw_ref[1:2, :] = jnp.sum(dy * u1, axis=0, keepdims=True)
        dw_ref[2:3, :] = jnp.sum(dy * u, axis=0, keepdims=True)
        dp_ref[0] = (du * xb).astype(BF16)
        dp_ref[2] = (du * cg).astype(BF16)
        dp_ref[3] = jnp.zeros((L, LANES), BF16)

    blk = pl.BlockSpec((L, LANES), lambda j: (0, j))
    return _pallas_call(
        body, name="conv_bwd", grid=(H // LANES,),
        in_specs=_conv_specs(L, H) + [pl.BlockSpec((3, LANES), lambda j: (0, j)), blk, ANY, ANY],
        out_specs=[pl.BlockSpec((4, L, LANES), lambda j: (2, 0, j)), pl.BlockSpec((3, LANES), lambda j: (0, j))],
        out_shape=[_sds(dproj.shape, BF16), _sds((3, H), F32)],
        input_output_aliases={6: 0},
        compiler_params=_params(("parallel",), 56),
    )(proj, proj, proj, conv_w, dcb, after, dproj)


def _gate_specs(tm, H):
    return [pl.BlockSpec((tm, H), lambda i, k=k: (i, k)) for k in (7, 8, 9, 10)]


def _fwd_mix(og, cb, proj, x, wat, wbt, wout, g_ffn, H, after):
    L, D = x.shape
    tm = min(L, 512)

    def body(o_ref, cb_ref, ga0, ga1, gb0, gb1, x_ref, wa_ref, wb_ref, wo_ref, g_ref, after_ref,
             sa_ref, sb_ref, ta_ref, tb_ref, m_ref, x1_ref, h2_ref):
        ya, yb = _nt(o_ref[...], wa_ref[...]), _nt(cb_ref[...], wb_ref[...])
        for k, (gar, gbr) in enumerate(((ga0, gb0), (ga1, gb1))):
            cs = slice(k * H, (k + 1) * H)
            sa, sb = _sigmoid(gar[...]), _sigmoid(gbr[...])
            ma, mb = sa * ya[:, cs], sb * yb[:, cs]
            m_ref[:, cs] = (ma + mb).astype(BF16)
            sa_ref[:, cs] = sa.astype(BF16)
            sb_ref[:, cs] = sb.astype(BF16)
            ta_ref[:, cs] = (ma * (1.0 - sa)).astype(BF16)
            tb_ref[:, cs] = (mb * (1.0 - sb)).astype(BF16)
        x1 = x_ref[...] + _nn(m_ref[...], wo_ref[...])
        x1_ref[...] = x1
        _, xh = _rms_stats(x1)
        h2_ref[...] = (xh * g_ref[...]).astype(BF16)

    row = lambda w: pl.BlockSpec((tm, w), lambda i: (i, 0))
    full = lambda a: pl.BlockSpec(a.shape, lambda i: (0,) * a.ndim)
    return _pallas_call(
        body, name="fwd_mix", grid=(L // tm,),
        in_specs=[row(H), row(H)] + _gate_specs(tm, H) + [row(D), full(wat), full(wbt), full(wout),
                                                           full(g_ffn), ANY],
        out_specs=[row(D)] * 7,
        out_shape=[_sds((L, D), BF16)] * 5 + [_sds((L, D), F32), _sds((L, D), BF16)],
        compiler_params=_params(("parallel",), 56),
    )(og, cb, proj, proj, proj, proj, x, wat, wbt, wout, g_ffn, after)


def _bwd_mix(dx1b, sig_a, sig_b, dm_dga, dm_dgb, wat, wbt, wout, H, after):
    L, D = dx1b.shape
    tm = min(L, 512)

    def body(dx_ref, sa_ref, sb_ref, ta_ref, tb_ref, wa_ref, wb_ref, wo_ref, after_ref,
             dya_ref, dyb_ref, dgate_ref, do_ref, dcb_ref):
        dm = _nt(dx_ref[...], wo_ref[...])
        dga = (dm * ta_ref[...].astype(F32)).astype(BF16)
        dgb = (dm * tb_ref[...].astype(F32)).astype(BF16)
        for q, part in enumerate((dga[:, 0:H], dga[:, H:D], dgb[:, 0:H], dgb[:, H:D])):
            dgate_ref[q] = part
        dya_ref[...] = (dm * sa_ref[...].astype(F32)).astype(BF16)
        dyb_ref[...] = (dm * sb_ref[...].astype(F32)).astype(BF16)
        do_ref[...] = _nn(dya_ref[...], wa_ref[...])
        dcb_ref[...] = _nn(dyb_ref[...], wb_ref[...])

    row = lambda w: pl.BlockSpec((tm, w), lambda i: (i, 0))
    full = lambda a: pl.BlockSpec(a.shape, lambda i: (0,) * a.ndim)
    return _pallas_call(
        body, name="bwd_mix", grid=(L // tm,),
        in_specs=[row(D)] * 5 + [full(wat), full(wbt), full(wout), ANY],
        out_specs=[row(D), row(D), pl.BlockSpec((4, tm, H), lambda i: (1, i, 0)), row(H), row(H)],
        out_shape=[_sds((L, D), BF16)] * 2 + [_sds((DPROJ_BLOCKS, L, H), BF16)] + [_sds((L, H), F32)] * 2,
        compiler_params=_params(("parallel",), 56),
    )(dx1b, sig_a, sig_b, dm_dga, dm_dgb, wat, wbt, wout, after)


def _fwd_ffn_up(h2, wgt, wut):
    L, D = h2.shape
    F = wgt.shape[0]
    tn = F // 2
    tm = min(L, 512)

    def body(h_ref, wg_ref, wu_ref, sa_ref, sb_ref, s_ref):
        h = h_ref[...]
        a, b = _nt(h, wg_ref[...]), _nt(h, wu_ref[...])
        sg = _sigmoid(a)
        silu = a * sg
        sa_ref[...] = (b * sg * (1.0 + a * (1.0 - sg))).astype(BF16)
        sb_ref[...] = silu.astype(BF16)
        s_ref[...] = (silu * b).astype(BF16)

    wspec = pl.BlockSpec((tn, D), lambda j, i: (j, 0))
    ospec = pl.BlockSpec((tm, tn), lambda j, i: (i, j))
    return _pallas_call(
        body, name="fwd_ffn_up", grid=(2, L // tm),
        in_specs=[pl.BlockSpec((tm, D), lambda j, i: (i, 0)), wspec, wspec],
        out_specs=[ospec] * 3,
        out_shape=[_sds((L, F), BF16)] * 3,
        compiler_params=_params(("parallel", "parallel"), 48),
    )(h2, wgt, wut)


def _fwd_down_loss(s, wd, x1, target, g_final):
    L, D = x1.shape
    F = wd.shape[0]
    tm = min(L, 512)

    def body(s_ref, wd_ref, x1_ref, t_ref, g_ref, dx_ref, dxb_ref, red_ref):
        @pl.when(pl.program_id(0) == 0)
        def _():
            red_ref[...] = jnp.zeros_like(red_ref)

        g = g_ref[...]
        r, xh = _rms_stats(x1_ref[...] + _nn(s_ref[...], wd_ref[...]))
        e = xh * g - t_ref[...]
        dy = e * (1.0 / D)
        dx = _rms_bwd(dy * g, xh, r)
        dx_ref[...] = dx
        dxb_ref[...] = dx.astype(BF16)
        red_ref[0:1, :] += jnp.sum(dy * xh, axis=0, keepdims=True)
        red_ref[1:2, :] += jnp.broadcast_to(0.5 * jnp.sum(e * e) * (1.0 / D), (1, D))

    row = pl.BlockSpec((tm, D), lambda i: (i, 0))
    return _pallas_call(
        body, name="fwd_down_loss", grid=(L // tm,),
        in_specs=[pl.BlockSpec((tm, F), lambda i: (i, 0)), pl.BlockSpec((F, D), lambda i: (0, 0)),
                  row, row, pl.BlockSpec((1, D), lambda i: (0, 0))],
        out_specs=[row, row, pl.BlockSpec((8, D), lambda i: (0, 0))],
        out_shape=[_sds((L, D), F32), _sds((L, D), BF16), _sds((8, D), F32)],
        compiler_params=_params(("arbitrary",), 56),
    )(s, wd, x1, target, g_final)


def _bwd_down(dx2b, wd, s_a, s_b):
    L, D = dx2b.shape
    F = wd.shape[0]
    tn = F // 2
    tm = min(L, 512)

    def body(dx_ref, wd_ref, sa_ref, sb_ref, da_ref, db_ref):
        ds = _nt(dx_ref[...], wd_ref[...])
        da_ref[...] = (ds * sa_ref[...].astype(F32)).astype(BF16)
        db_ref[...] = (ds * sb_ref[...].astype(F32)).astype(BF16)

    ospec = pl.BlockSpec((tm, tn), lambda j, i: (i, j))
    return _pallas_call(
        body, name="bwd_down", grid=(2, L // tm),
        in_specs=[pl.BlockSpec((tm, D), lambda j, i: (i, 0)),
                  pl.BlockSpec((tn, D), lambda j, i: (j, 0)), ospec, ospec],
        out_specs=[ospec] * 2,
        out_shape=[_sds((L, F), BF16)] * 2,
        compiler_params=_params(("parallel", "parallel"), 48),
    )(dx2b, wd, s_a, s_b)


def _bwd_ffn_dh(da, db, wgt, wut, x1, dx2, g_ffn, after):
    L, D = x1.shape
    F = wgt.shape[0]
    tm = min(L, 256)

    def body(da_ref, db_ref, wg_ref, wu_ref, x1_ref, dx2_ref, g_ref, after_ref, dx_ref, dxb_ref, red_ref):
        @pl.when(pl.program_id(0) == 0)
        def _():
            red_ref[...] = jnp.zeros_like(red_ref)

        dh = _nn(da_ref[...], wg_ref[...]) + _nn(db_ref[...], wu_ref[...])
        r, xh = _rms_stats(x1_ref[...])
        red_ref[0:1, :] += jnp.sum(dh * xh, axis=0, keepdims=True)
        dx = dx2_ref[...] + _rms_bwd(dh * g_ref[...], xh, r)
        dx_ref[...] = dx
        dxb_ref[...] = dx.astype(BF16)

    row = pl.BlockSpec((tm, D), lambda i: (i, 0))
    aspec = pl.BlockSpec((tm, F), lambda i: (i, 0))
    wspec = pl.BlockSpec((F, D), lambda i: (0, 0))
    return _pallas_call(
        body, name="bwd_ffn_dh", grid=(L // tm,),
        in_specs=[aspec, aspec, wspec, wspec, row, row, pl.BlockSpec((1, D), lambda i: (0, 0)), ANY],
        out_specs=[row, row, pl.BlockSpec((8, D), lambda i: (0, 0))],
        out_shape=[_sds((L, D), F32), _sds((L, D), BF16), _sds((8, D), F32)],
        compiler_params=_params(("arbitrary",), 56),
    )(da, db, wgt, wut, x1, dx2, g_ffn, after)


def _bwd_in(dproj, w_int, x, dx1, g_mix, after):
    L, D = x.shape
    N = w_int.shape[0]
    H = dproj.shape[2]
    tm = min(L, 256)
    assert N == len(DPROJ_BLOCK_OF) * H

    def body(blocks_ref, w_ref, x_ref, dx1_ref, g_ref, after_ref, dx_ref, red_ref, dp_ref):
        @pl.when(pl.program_id(0) == 0)
        def _():
            red_ref[...] = jnp.zeros_like(red_ref)

        for t, block in enumerate(DPROJ_BLOCK_OF):
            dp_ref[:, t * H:(t + 1) * H] = blocks_ref[block]
        dh = _nn(dp_ref[...], w_ref[...])
        r, xh = _rms_stats(x_ref[...])
        red_ref[0:1, :] += jnp.sum(dh * xh, axis=0, keepdims=True)
        dx_ref[...] = dx1_ref[...] + _rms_bwd(dh * g_ref[...], xh, r)

    row = pl.BlockSpec((tm, D), lambda i: (i, 0))
    return _pallas_call(
        body, name="bwd_in", grid=(L // tm,),
        in_specs=[pl.BlockSpec((DPROJ_BLOCKS, tm, H), lambda i: (0, i, 0)), pl.BlockSpec((N, D), lambda i: (0, 0)),
                  row, row, pl.BlockSpec((1, D), lambda i: (0, 0)), ANY],
        out_specs=[row, pl.BlockSpec((8, D), lambda i: (0, 0))],
        out_shape=[_sds((L, D), F32), _sds((8, D), F32)],
        scratch_shapes=[pltpu.VMEM((tm, N), BF16)],
        compiler_params=_params(("arbitrary",), 56),
    )(dproj, w_int, x, dx1, g_mix, after)


def _dw_in(h, dproj, n_cols, c_idx):
    L, D = h.shape
    H = dproj.shape[2]
    tk = min(L, TK_TOKENS)
    nk = L // tk
    r2 = D // 2
    first = [(j * n_cols) // H for j in range(N_CHIPS)]
    last = [((j + 1) * n_cols - 1) // H for j in range(N_CHIPS)]
    slots = max(b - a for a, b in zip(first, last)) + 1
    plan = []
    for j in range(N_CHIPS):
        lo, hi = j * n_cols, (j + 1) * n_cols
        segments = []
        for s in range(last[j] - first[j] + 1):
            a, b = max(lo, (first[j] + s) * H), min(hi, (first[j] + s + 1) * H)
            segments.append((s, a - (first[j] + s) * H, b - a, a - lo))
        plan.append(segments)

    def body(c_ref, *refs):
        h_ref, slot_refs = refs[0], refs[1:1 + slots]
        o_ref, sib_ref, b_ref = refs[1 + slots:]
        j, k = pl.program_id(0), pl.program_id(1)
        for jj in range(N_CHIPS):
            @pl.when(j == jj)
            def _(jj=jj):
                for s, start, width, at in plan[jj]:
                    b_ref[:, at:at + width] = slot_refs[s][:, start:start + width]

        part = _tn(h_ref[...], b_ref[...])

        @pl.when(k == 0)
        def _():
            o_ref[...] = part

        @pl.when(k > 0)
        def _():
            o_ref[...] += part

        @pl.when(k == nk - 1)
        def _():
            theirs = pl.ds(pl.multiple_of((1 - c_ref[0]) * r2, 8), r2)
            sib_ref[...] = o_ref[theirs, :].astype(BF16)

    def slot_spec(s):
        blocks = [DPROJ_BLOCK_OF[min(first[j] + s, last[j])] for j in range(N_CHIPS)]

        def index(j, k, c_ref):
            block = blocks[0]
            for jj in range(1, N_CHIPS):
                block = jnp.where(j == jj, blocks[jj], block)
            return (block, k, 0)

        return pl.BlockSpec((None, tk, H), index)

    return _pallas_call(
        body, name="dw_in",
        grid_spec=pltpu.PrefetchScalarGridSpec(
            num_scalar_prefetch=1, grid=(N_CHIPS, nk),
            in_specs=[pl.BlockSpec((tk, D), lambda j, k, c_ref: (k, 0))] + [slot_spec(s) for s in range(slots)],
            out_specs=[pl.BlockSpec((None, D, n_cols), lambda j, k, c_ref: (j, 0, 0)),
                       pl.BlockSpec((None, r2, n_cols), lambda j, k, c_ref: (j, 0, 0))],
            scratch_shapes=[pltpu.VMEM((tk, n_cols), BF16)]),
        out_shape=[_sds((N_CHIPS, D, n_cols), F32), _sds((N_CHIPS, r2, n_cols), BF16)],
        compiler_params=_params(("parallel", "arbitrary"), 56),
    )(c_idx, h, *([dproj] * slots))


def _mm_tn(name, a, b, a_spec, b_spec, o_block, n_out, n_k):
    def body(a_ref, b_ref, o_ref):
        part = _tn(a_ref[...], b_ref[...])

        @pl.when(pl.program_id(1) == 0)
        def _():
            o_ref[...] = part

        @pl.when(pl.program_id(1) > 0)
        def _():
            o_ref[...] += part

    return _pallas_call(
        body, name=name, grid=(n_out, n_k),
        in_specs=[a_spec, b_spec],
        out_specs=pl.BlockSpec((None,) + o_block, lambda j, k: (j, 0, 0)),
        out_shape=_sds((n_out,) + o_block, F32),
        compiler_params=_params(("parallel", "arbitrary"), 56),
    )(a, b)


TK_TOKENS = 2048


def _dw_whole(name, pairs, by_rows):
    n = len(pairs)
    L = pairs[0][0].shape[0]
    tk = min(L, TK_TOKENS)

    def body(*refs):
        for q in range(n):
            a_ref, b_ref, o_ref = refs[2 * q], refs[2 * q + 1], refs[2 * n + q]
            part = _tn(a_ref[...], b_ref[...])
            rows, cols = o_ref.shape[1], o_ref.shape[2]
            shards = [part[j * rows:(j + 1) * rows, :] if by_rows else part[:, j * cols:(j + 1) * cols]
                      for j in range(N_CHIPS)]

            @pl.when(pl.program_id(0) == 0)
            def _(shards=shards, o_ref=o_ref):
                for j, shard in enumerate(shards):
                    o_ref[j] = shard

            @pl.when(pl.program_id(0) > 0)
            def _(shards=shards, o_ref=o_ref):
                for j, shard in enumerate(shards):
                    o_ref[j] += shard

    in_specs, out_specs, out_shape, operands = [], [], [], []
    for a, b in pairs:
        M, N = a.shape[1], b.shape[1]
        shape = (N_CHIPS, M // N_CHIPS, N) if by_rows else (N_CHIPS, M, N // N_CHIPS)
        in_specs += [pl.BlockSpec((tk, M), lambda k: (k, 0)), pl.BlockSpec((tk, N), lambda k: (k, 0))]
        out_specs.append(pl.BlockSpec(shape, lambda k: (0, 0, 0)))
        out_shape.append(_sds(shape, F32))
        operands += [a, b]
    return _pallas_call(
        body, name=name, grid=(L // tk,), in_specs=in_specs, out_specs=out_specs, out_shape=out_shape,
        compiler_params=_params(("arbitrary",), 56),
    )(*operands)


def _dw_rows2(name, a, b):
    L, M = a.shape
    N = b.shape[1]
    tk = min(L, TK_TOKENS)
    return _mm_tn(name, a, b, pl.BlockSpec((tk, M // 2), lambda j, k: (k, j)),
                  pl.BlockSpec((tk, N), lambda j, k: (k, 0)), (M // 2, N), 2, L // tk)


def _place():
    x, y, c = lax.axis_index("x"), lax.axis_index("y"), lax.axis_index("c")
    chips = [(1 - x, y), (x, 1 - y), (1 - x, 1 - y)]
    return x, y, c, 2 * x + y, chips


def _remote(src, dst, send_sem, recv_sem, device):
    return pltpu.make_async_remote_copy(src_ref=src, dst_ref=dst, send_sem=send_sem,
                                        recv_sem=recv_sem, device_id=device, device_id_type=MESH)


def _half(ref, lead, c, r2):
    return ref.at[lead, pl.ds(pl.multiple_of(c * r2, 16), r2), :]


def _cast_place(name, ws, chip_idx):
    n = len(ws)

    def body(k_ref, *refs):
        for w_ref, o_ref in zip(refs[:n], refs[n:]):
            o_ref[...] = w_ref[...].astype(BF16)

    return _pallas_call(
        body, name=name,
        grid_spec=pltpu.PrefetchScalarGridSpec(
            num_scalar_prefetch=1, grid=(2,),
            in_specs=[pl.BlockSpec((w.shape[0] // 2, w.shape[1]), lambda i, k_ref: (i, 0)) for w in ws],
            out_specs=[pl.BlockSpec((None, w.shape[0] // 2, w.shape[1]), lambda i, k_ref: (k_ref[0], i, 0))
                       for w in ws]),
        out_shape=[_sds((N_CHIPS,) + w.shape, BF16) for w in ws],
        compiler_params=_params(("parallel",), 48),
    )(chip_idx, *ws)


def _cast_place_t(name, ws, chip_idx):
    n = len(ws)
    r, cols = ws[0].shape

    def body(k_ref, *refs):
        for w_ref, o_ref in zip(refs[:n], refs[n:]):
            o_ref[...] = w_ref[...].T.astype(BF16)

    return _pallas_call(
        body, name=name,
        grid_spec=pltpu.PrefetchScalarGridSpec(
            num_scalar_prefetch=1, grid=(cols // LANES,),
            in_specs=[pl.BlockSpec((r, LANES), lambda i, k_ref: (0, i))] * n,
            out_specs=[pl.BlockSpec((None, LANES, r), lambda i, k_ref: (k_ref[0], i, 0))] * n),
        out_shape=[_sds((N_CHIPS, cols, r), BF16)] * n,
        compiler_params=_params(("parallel",), 48),
    )(chip_idx, *ws)


def _gather_copies(bufs, whole, send_sems, recv_sems, select=None):
    x, y, c, k, chips = _place()
    pairs = []
    for w, buf in enumerate(bufs):
        for j, (cx, cy) in enumerate(chips):
            if select is not None and not select(w, j):
                continue
            if w in whole:
                mine, theirs = buf.at[k], buf.at[2 * cx + cy]
            else:
                r2 = buf.shape[1] // 2
                mine, theirs = _half(buf, k, c, r2), _half(buf, 2 * cx + cy, c, r2)
            sems = (send_sems.at[w * 3 + j], recv_sems.at[w * 3 + j])
            pairs.append((_remote(mine, mine, *sems, (cx, cy, c)), _remote(theirs, theirs, *sems, (x, y, c))))
    return pairs


def _gather_start(name, groups, after):
    flat = [b for bufs, _, _ in groups for b in bufs]
    nb, ng = len(flat), len(groups)

    def body(*refs):
        ins, sems, token = refs[:nb], refs[nb + 1:nb + 1 + 2 * ng], refs[-1]
        pos = 0
        for g, (bufs, whole, select) in enumerate(groups):
            for send, _ in _gather_copies(ins[pos:pos + len(bufs)], whole, sems[2 * g], sems[2 * g + 1], select):
                send.start()
            pos += len(bufs)
        token[...] = jnp.zeros_like(token)

    sem_shapes = []
    for bufs, _, _ in groups:
        sem_shapes += [pltpu.SemaphoreType.DMA((3 * len(bufs),))] * 2
    out = _pallas_call(
        body, name=name,
        in_specs=[HBM] * nb + [ANY], out_specs=tuple([SEM] * (2 * ng) + [HBM] * nb + [VMEM]),
        out_shape=tuple(sem_shapes + [pltpu.HBM(b.shape, b.dtype) for b in flat] + [_sds((8, LANES), F32)]),
        input_output_aliases={i: 2 * ng + i for i in range(nb)},
        compiler_params=pltpu.CompilerParams(has_side_effects=EFFECT),
    )(*flat, after)
    sems, thru, pos = [], [], 2 * ng
    for g, (bufs, _, _) in enumerate(groups):
        sems.append((out[2 * g], out[2 * g + 1]))
        thru.append(list(out[pos:pos + len(bufs)]))
        pos += len(bufs)
    return sems, thru, out[-1]


def _gather_wait(name, bufs, whole, sems, after, select=None):
    nb = len(bufs)

    def body(*refs):
        ins, send_sems, recv_sems = refs[:nb], refs[nb], refs[nb + 1]
        for send, arrival in _gather_copies(ins, whole, send_sems, recv_sems, select):
            send.wait_send()
            arrival.wait_recv()

    return _pallas_call(
        body, name=name,
        in_specs=[HBM] * nb + [SEM, SEM, ANY], out_specs=[HBM] * nb,
        out_shape=[pltpu.HBM(b.shape, b.dtype) for b in bufs],
        input_output_aliases={i: i for i in range(nb)},
        compiler_params=pltpu.CompilerParams(has_side_effects=EFFECT),
    )(*bufs, sems[0], sems[1], after)


def _gather_forward(name, bufs, sources=(0, 1, 2)):
    n = len(bufs)

    def body(*refs):
        outs = refs[n:2 * n]
        send_sems, recv_sems = refs[2 * n:]
        x, y, c, _, chips = _place()
        sends = []
        for w in range(n):
            r2 = outs[w].shape[1] // 2
            for j in sources:
                landed = _half(outs[w], 2 * chips[j][0] + chips[j][1], c, r2)
                sends.append(_remote(landed, landed, send_sems.at[w * 3 + j], recv_sems.at[w * 3 + j],
                                     (x, y, 1 - c)))
        for cp in sends:
            cp.start()
        for w in range(n):
            r2 = outs[w].shape[1] // 2
            for j in sources:
                got = _half(outs[w], 2 * chips[j][0] + chips[j][1], 1 - c, r2)
                _remote(got, got, send_sems.at[w * 3 + j], recv_sems.at[w * 3 + j], (x, y, c)).wait_recv()
        for cp in sends:
            cp.wait_send()

    return _pallas_call(
        body, name=name,
        in_specs=[ANY] * n, out_specs=[ANY] * n,
        out_shape=[_sds(b.shape, b.dtype) for b in bufs],
        input_output_aliases={i: i for i in range(n)},
        scratch_shapes=[pltpu.SemaphoreType.DMA((n * 3,)), pltpu.SemaphoreType.DMA((n * 3,))],
    )(*bufs)


def _rs_add(name, grads3, from_sibling, c_idx):
    n = len(grads3)

    def body(c_ref, *refs):
        for g_ref, s_ref, o_ref in zip(refs[:n], refs[n:2 * n], refs[2 * n:]):
            o_ref[...] = (g_ref[...] + s_ref[...].astype(F32)).astype(BF16)

    mine =[pl.BlockSpec((None,) + s.shape[1:], lambda k, c_ref: (k, c_ref[0], 0)) for s in from_sibling]
    whole = [pl.BlockSpec((None,) + s.shape[1:], lambda k, c_ref: (k, 0, 0)) for s in from_sibling]
    return _pallas_call(
        body, name=name,
        grid_spec=pltpu.PrefetchScalarGridSpec(num_scalar_prefetch=1, grid=(N_CHIPS,), in_specs=mine + whole,
                                               out_specs=whole),
        out_shape=[_sds(s.shape, BF16) for s in from_sibling],
        compiler_params=_params(("parallel",), 48),
    )(c_idx, *grads3, *from_sibling)


def _split_start(name, arrays, n_sems, pairs_fn):
    n = len(arrays)

    def body(*refs):
        for send, _ in pairs_fn(refs[:n], refs[n], refs[n + 1]):
            send.start()
        refs[-1][...] = jnp.zeros_like(refs[-1])

    out = _pallas_call(
        body, name=name,
        in_specs=[HBM] * n, out_specs=tuple([SEM, SEM] + [HBM] * n + [VMEM]),
        out_shape=tuple([pltpu.SemaphoreType.DMA((n_sems,))] * 2 + [pltpu.HBM(a.shape, a.dtype) for a in arrays]
                        + [_sds((8, LANES), F32)]),
        input_output_aliases={i: 2 + i for i in range(n)},
        compiler_params=pltpu.CompilerParams(has_side_effects=EFFECT),
    )(*arrays)
    return (out[0], out[1]), list(out[2:2 + n]), out[-1]


def _split_wait(name, sems, arrays, pairs_fn, after):
    n = len(arrays)

    def body(*refs):
        for send, arrival in pairs_fn(refs[:n], refs[n], refs[n + 1]):
            send.wait_send()
            arrival.wait_recv()

    return list(_pallas_call(
        body, name=name,
        in_specs=[HBM] * n + [SEM, SEM, ANY], out_specs=[HBM] * n,
        out_shape=[pltpu.HBM(a.shape, a.dtype) for a in arrays],
        input_output_aliases={i: i for i in range(n)},
        compiler_params=pltpu.CompilerParams(has_side_effects=EFFECT),
    )(*arrays, sems[0], sems[1], after))


def _forward_pairs(bufs, send_sems, recv_sems):
    x, y, c, _, chips = _place()
    pairs = []
    for w, buf in enumerate(bufs):
        r2 = buf.shape[1] // 2
        for j, (cx, cy) in enumerate(chips):
            landed, theirs = _half(buf, 2 * cx + cy, c, r2), _half(buf, 2 * cx + cy, 1 - c, r2)
            sems = (send_sems.at[w * 3 + j], recv_sems.at[w * 3 + j])
            pairs.append((_remote(landed, landed, *sems, (x, y, 1 - c)), _remote(theirs, theirs, *sems, (x, y, c))))
    return pairs


def _sibling_pairs(arrays, send_sems, recv_sems):
    x, y, c, _, _ = _place()
    n = len(arrays) // 2
    pairs = []
    for w in range(n):
        r2 = arrays[w].shape[1] // 2
        cp = _remote(_half(arrays[w], slice(None), 1 - c, r2), arrays[n + w], send_sems.at[w], recv_sems.at[w],
                     (x, y, 1 - c))
        pairs.append((cp, cp))
    return pairs


def _sibling_whole_pairs(arrays, send_sems, recv_sems):
    x, y, c, _, _ = _place()
    n = len(arrays) // 2
    pairs = []
    for w in range(n):
        cp = _remote(arrays[w], arrays[n + w], send_sems.at[w], recv_sems.at[w], (x, y, 1 - c))
        pairs.append((cp, cp))
    return pairs


def _ici_pairs(arrays, send_sems, recv_sems):
    x, y, c, _, chips = _place()
    n = len(arrays) // 2
    pairs = []
    for w in range(n):
        for j, (cx, cy) in enumerate(chips):
            cp = _remote(arrays[w].at[2 * cx + cy], arrays[n + w].at[j],
                         send_sems.at[w * 3 + j], recv_sems.at[w * 3 + j], (cx, cy, c))
            pairs.append((cp, cp))
    return pairs


def _rs_sum(name, partials, received, place_idx):
    n = len(partials)
    nb = 2
    blocks = [(p.shape[1] // nb, p.shape[2]) for p in partials]

    def body(idx_ref, *refs):
        for p_ref, r_ref, o_ref in zip(refs[:n], refs[n:2 * n], refs[2 * n:]):
            o_ref[...] = ((p_ref[...].astype(F32) + r_ref[0].astype(F32))
                          + (r_ref[1].astype(F32) + r_ref[2].astype(F32)))

    return _pallas_call(
        body, name=name,
        grid_spec=pltpu.PrefetchScalarGridSpec(
            num_scalar_prefetch=1, grid=(nb,),
            in_specs=[pl.BlockSpec((None,) + b, lambda i, idx: (idx[0], i, 0)) for b in blocks]
            + [pl.BlockSpec((3,) + b, lambda i, idx: (0, i, 0)) for b in blocks],
            out_specs=[pl.BlockSpec(b, lambda i, idx: (idx[1] * nb + i, 0)) for b in blocks]),
        out_shape=[_sds((2 * p.shape[1], p.shape[2]), F32) for p in partials],
        compiler_params=_params(("parallel",), 48),
    )(place_idx, *partials, *received)


def _share_pairs(arrays, send_sems, recv_sems):
    x, y, c, _, _ = _place()
    pairs = []
    for w, arr in enumerate(arrays):
        r2 = arr.shape[0] // 2
        mine = arr.at[pl.ds(pl.multiple_of(c * r2, 8), r2), :]
        theirs = arr.at[pl.ds(pl.multiple_of((1 - c) * r2, 8), r2), :]
        sems = (send_sems.at[w], recv_sems.at[w])
        pairs.append((_remote(mine, mine, *sems, (x, y, 1 - c)), _remote(theirs, theirs, *sems, (x, y, c))))
    return pairs


def _small_pack(red_mix, red_ffn, red_final, red_hg, g_conv):
    D = red_mix.shape[1]
    H = red_hg.shape[1]

    def body(mix_ref, ffn_ref, fin_ref, hg_ref, cv_ref, in_ref):
        in_ref[...] = jnp.zeros_like(in_ref)
        in_ref[0:1, :] = mix_ref[0:1, :]
        in_ref[1:2, :] = ffn_ref[0:1, :]
        in_ref[2:3, :] = fin_ref[0:1, :]
        gam = hg_ref[1:2, 0:HEAD_DIM]
        for h in range(1, H // HEAD_DIM):
            gam = gam + hg_ref[1:2, h * HEAD_DIM:(h + 1) * HEAD_DIM]
        in_ref[3:4, 0:HEAD_DIM] = gam
        in_ref[3:4, HEAD_DIM:2 * HEAD_DIM] = fin_ref[1:2, 0:HEAD_DIM]
        in_ref[4:5, 0:H] = hg_ref[0:1, :]
        in_ref[6:9, 0:H] = cv_ref[...]

    return _pallas_call(
        body, name="small_pack", pin=False,
        in_specs=[VMEM] * 5, out_specs=VMEM, out_shape=_sds((N_SMALL_ROWS, D), F32),
    )(red_mix, red_ffn, red_final, red_hg, g_conv)


def _small_pairs(arrays, send_sems, recv_sems):
    block, gathered = arrays
    x, y, c, _, _ = _place()
    me = 4 * x + 2 * y + c
    pairs = []
    for m in range(1, 8):
        px, py, pc = x ^ ((m >> 2) & 1), y ^ ((m >> 1) & 1), c ^ (m & 1)
        sems = (send_sems.at[m - 1], recv_sems.at[m - 1])
        pairs.append((_remote(block, gathered.at[me], *sems, (px, py, pc)),
                      _remote(block, gathered.at[4 * px + 2 * py + pc], *sems, (x, y, c))))
    return pairs


def _adamw_math(w, g, m, v):
    m = ADAM_B1 * m + (1.0 - ADAM_B1) * g
    v = ADAM_B2 * v + (1.0 - ADAM_B2) * jnp.square(g)
    m_hat = m / (1.0 - ADAM_B1 ** ADAM_STEP)
    v_hat = v / (1.0 - ADAM_B2 ** ADAM_STEP)
    delta = -ADAM_LR * (m_hat / (jnp.sqrt(v_hat) + ADAM_EPS) + ADAM_WD * w)
    return delta, m, v


def _adamw(name, gs, ws, ms, vs):
    n = len(gs)
    nb = 4

    def body(*refs):
        ins, outs = refs[:4 * n], refs[4 * n:]
        for j in range(n):
            g_ref, w_ref, m_ref, v_ref = ins[j], ins[n + j], ins[2 * n + j], ins[3 * n + j]
            go_ref, d_ref, mo_ref, vo_ref = outs[4 * j:4 * j + 4]
            g = g_ref[...]
            go_ref[...] = g
            d_ref[...], mo_ref[...], vo_ref[...] = _adamw_math(w_ref[...], g, m_ref[...], v_ref[...])

    blk = [pl.BlockSpec((g.shape[0] // nb, g.shape[1]), lambda i: (i, 0)) for g in gs]
    out = _pallas_call(
        body, name=name, grid=(nb,),
        in_specs=blk * 4, out_specs=[b for b in blk for _ in range(4)],
        out_shape=[_sds(g.shape, F32) for g in gs for _ in range(4)],
        compiler_params=_params(("parallel",), 56),
    )(*gs, *ws, *ms, *vs)
    return [list(out[4 * j:4 * j + 4]) for j in range(n)]


def _small_update(block, gathered, place_idx, ws, ms, vs):
    n = len(ws)
    H = ws[1].shape[1]

    def body(idx_ref, blk_ref, all_ref, *refs):
        w, m, v, outs, tot_ref = refs[:n], refs[n:2 * n], refs[2 * n:3 * n], refs[3 * n:-1], refs[-1]
        chip, me = idx_ref[0], idx_ref[1]
        tot = jnp.where(me == 0, blk_ref[...], all_ref[0])
        for d in range(1, 8):
            tot = tot + jnp.where(me == d, blk_ref[...], all_ref[d])
        tot_ref[...] = tot
        p0 = _lower_bound(w[1][...])
        dl0 = p0 * (1.0 - p0) * tot_ref[4:5, 0:H]
        conv = jnp.zeros((3, LANES), F32)
        for k in range(N_CHIPS):
            conv = jnp.where(chip == k, tot_ref[6:9, k * LANES:(k + 1) * LANES], conv)
        grads = [tot_ref[0:1, :], None, tot_ref[3:4, 0:HEAD_DIM], conv, tot_ref[1:2, :], tot_ref[2:3, :]]
        for p in range(n):
            g_ref, d_ref, mo_ref, vo_ref = outs[4 * p:4 * p + 4]
            if p == 1:
                for row, g in ((slice(0, 1), dl0), (slice(1, 2), -dl0)):
                    g_ref[row, :] = g
                    d_ref[row, :], mo_ref[row, :], vo_ref[row, :] = _adamw_math(
                        w[p][row, :], g, m[p][row, :], v[p][row, :])
            else:
                g_ref[...] = grads[p]
                d_ref[...], mo_ref[...], vo_ref[...] = _adamw_math(w[p][...], grads[p], m[p][...], v[p][...])
        outs[4 * n][...] = tot_ref[3:4, HEAD_DIM:2 * HEAD_DIM]

    full = lambda a: pl.BlockSpec(a.shape, lambda i, idx: (0,) * a.ndim)
    out_shape = [_sds(w.shape, F32) for w in ws for _ in range(4)] + [_sds((1, LANES), F32)]
    return _pallas_call(
        body, name="small_update",
        grid_spec=pltpu.PrefetchScalarGridSpec(
            num_scalar_prefetch=1, grid=(1,),
            in_specs=[full(block), full(gathered)] + [full(a) for a in ws + ms + vs],
            out_specs=[full(s) for s in out_shape],
            scratch_shapes=[pltpu.VMEM(block.shape, F32)]),
        out_shape=out_shape,
    )(place_idx, block, gathered, *ws, *ms, *vs)


def kernel(x, norm_mix_g, w_in, lower_bounds, hg_norm_g, conv_w, w_branch_a, w_branch_b, w_out, norm_ffn_g, w_ffn_gate, w_ffn_up, w_ffn_down, norm_final_g, loss_target, m_norm_mix_g, m_w_in, m_lower_bounds, m_hg_norm_g, m_conv_w, m_w_branch_a, m_w_branch_b, m_w_out, m_norm_ffn_g, m_w_ffn_gate, m_w_ffn_up, m_w_ffn_down, m_norm_final_g, v_norm_mix_g, v_w_in, v_lower_bounds, v_hg_norm_g, v_conv_w, v_w_branch_a, v_w_branch_b, v_w_out, v_norm_ffn_g, v_w_ffn_gate, v_w_ffn_up, v_w_ffn_down, v_norm_final_g):
    _, L, D = x.shape
    H = D // 2
    assert lower_bounds.shape == (2, H) and hg_norm_g.shape == (1, HEAD_DIM)
    assert conv_w.shape == (1, 3, LANES) and w_in.shape[2] * N_CHIPS == 11 * H
    x2d, target = x.reshape(L, D), loss_target.reshape(L, D)
    g_final = norm_final_g.reshape(1, D)
    chip = 2 * lax.axis_index("x") + lax.axis_index("y")
    core = lax.axis_index("c")

    tr = lambda w: jnp.transpose(w[0])
    big = [w_in[0], w_branch_a[0], w_branch_b[0], w_out[0], tr(w_ffn_gate), tr(w_ffn_up), w_ffn_down[0]]
    big_m = [m_w_in[0], m_w_branch_a[0], m_w_branch_b[0], m_w_out[0], tr(m_w_ffn_gate), tr(m_w_ffn_up),
             m_w_ffn_down[0]]
    big_v = [v_w_in[0], v_w_branch_a[0], v_w_branch_b[0], v_w_out[0], tr(v_w_ffn_gate), tr(v_w_ffn_up),
             v_w_ffn_down[0]]
    names = ["w_in", "w_branch_a", "w_branch_b", "w_out", "w_ffn_gate", "w_ffn_up", "w_ffn_down"]

    chip_idx = chip.reshape(1).astype(jnp.int32)
    def per_shape(fn, tag, js, *lists):
        groups = {}
        for pos, a in enumerate(lists[0]):
            groups.setdefault(a.shape, []).append(pos)
        results = [None] * len(js)
        for same in groups.values():
            out = fn(tag + names[js[same[0]]], *[[xs[p] for p in same] for xs in lists])
            for q, p in enumerate(same):
                results[p] = out[q]
        return results

    place_t = lambda name, ws: _cast_place_t(name, ws, chip_idx)
    placed = per_shape(place_t, "place_", [0, 1, 2], big[:3]) + list(_cast_place("place_rest", big[3:], chip_idx))
    conv_placed = lax.dynamic_update_slice(jnp.zeros((N_CHIPS, 3, LANES), F32), conv_w, (chip, 0, 0))
    x_i, y_i = lax.axis_index("x"), lax.axis_index("y")
    blocks = lambda *ks: jnp.stack(ks).astype(jnp.int32)
    near = lambda w, j: j < 2
    far = lambda w, j: w == 1 or j == 2
    near_sems, in_flight, _ = _gather_start("gather_start_near", [([placed[0]], set(), near)], chip_idx)
    w_in_buf = in_flight[0][0]
    h, proj = _fwd_proj_first(x2d, norm_mix_g, w_in_buf, blocks(chip))
    sems, in_flight, _ = _gather_start(
        "gather_start_rest", [([w_in_buf, conv_placed], {1}, far), (placed[1:4], set(), None),
                              (placed[4:], set(), None)], h)
    w_in_buf, conv_buf = in_flight[0]
    (w_in_buf,) = _gather_wait("gather_wait_in_near", [w_in_buf], set(), near_sems[0], h, near)
    (w_in_buf,) = _gather_forward("gather_fwd_in_near", [w_in_buf], (0, 1))
    proj = _fwd_proj_more("fwd_proj_near", h, w_in_buf, proj,
                          blocks(2 * (1 - x_i) + y_i, 2 * x_i + (1 - y_i)))
    w_in_buf, conv_all = _gather_wait("gather_wait_in_far", [w_in_buf, conv_buf], {1}, sems[0], proj, far)
    (w_int3,) = _gather_forward("gather_fwd_in_far", [w_in_buf], (2,))
    proj = _fwd_proj_more("fwd_proj_far", h, w_int3, proj, blocks(2 * (1 - x_i) + (1 - y_i)))
    w_int = w_int3.reshape(-1, D)
    conv_full = jnp.transpose(conv_all, (1, 0, 2)).reshape(3, H)
    og, o_pre, s_saved = _hgrn_fwd(proj, lower_bounds, hg_norm_g, H)
    landed = _gather_wait("gather_wait_mix", in_flight[1], set(), sems[1], og)
    fwd_sems, landed, token = _split_start("gather_fwd_mix_start", landed, 9, _forward_pairs)
    cb = _conv_fwd(proj, conv_full, H, token)
    wat3, wbt3, wout3 = _split_wait("gather_fwd_mix_wait", fwd_sems, landed, _forward_pairs, cb)
    wat, wbt, wout = wat3.reshape(D, H), wbt3.reshape(D, H), wout3.reshape(D, D)
    landed = _gather_wait("gather_wait_ffn", in_flight[2], set(), sems[2], cb)
    fwd_sems, landed, token = _split_start("gather_fwd_ffn_start", landed, 9, _forward_pairs)
    sig_a, sig_b, dm_dga, dm_dgb, merged, x1, h2 = _fwd_mix(og, cb, proj, x2d, wat, wbt, wout, norm_ffn_g,
                                                              H, token)
    wgt3, wut3, wd3 = _split_wait("gather_fwd_ffn_wait", fwd_sems, landed, _forward_pairs, h2)
    d_ff = N_CHIPS * wd3.shape[1]
    wgt, wut, wd = wgt3.reshape(d_ff, D), wut3.reshape(d_ff, D), wd3.reshape(d_ff, D)
    ffn_ds_da, ffn_ds_db, ffn_s = _fwd_ffn_up(h2, wgt, wut)
    dx2, dx2b, red_final = _fwd_down_loss(ffn_s, wd, x1, target, g_final)

    c_idx = core.reshape(1).astype(jnp.int32)
    place_idx = jnp.stack([chip, core]).astype(jnp.int32)

    def sibling_start(tag, grads):
        bufs = [lax.empty((N_CHIPS, g.shape[1] // 2, g.shape[2]), F32) for g in grads]
        return _split_start("rs_sibling_start_" + tag, list(grads) + bufs, len(grads), _sibling_pairs)

    def ici_start(tag, js, grads, from_sibling):
        partials = list(_rs_add("rs_add_" + tag, grads, from_sibling, c_idx))
        landings = [lax.empty((3,) + p.shape[1:], BF16) for p in partials]
        return _split_start("rs_ici_start_" + tag, partials + landings, 3 * len(js), _ici_pairs)

    def ici_start_behind(tag, js, started, after):
        n = len(js)
        arrays = _split_wait("rs_sibling_wait_" + tag, started[0], started[1], _sibling_pairs, after)
        return ici_start(tag, js, arrays[:n], arrays[n:])

    def sums(tag, started, after):
        partials, received = [], []
        for group, group_js, start in started:
            arrays = _split_wait("rs_ici_wait_" + group, start[0], start[1], _ici_pairs, after)
            partials += arrays[:len(group_js)]
            received += arrays[len(group_js):]
        return list(_rs_sum("rs_sum_" + tag, partials, received, place_idx))

    def adamw(tag, js, grads):
        return _adamw("adamw_" + tag, grads, *[[src[j] for j in js] for src in (big, big_m, big_v)])

    shards3 = lambda g: g.reshape(N_CHIPS, d_ff // N_CHIPS, D)
    da, db = _bwd_down(dx2b, wd, ffn_ds_da, ffn_ds_db)
    g_wd = shards3(_dw_rows2("dw_ffn_down", ffn_s, dx2b))
    g_wg = shards3(_dw_rows2("dw_ffn_gate", da, h2))
    g_wu = shards3(_dw_rows2("dw_ffn_up", db, h2))
    ffn_sibling = sibling_start("ffn", [g_wg, g_wu, g_wd])
    dx1, dx1b, red_ffn = _bwd_ffn_dh(da, db, wgt, wut, x1, dx2, norm_ffn_g, ffn_sibling[2])
    ffn_ici = ici_start_behind("ffn", [4, 5, 6], ffn_sibling, dx1b)
    dya, dyb, dproj, d_o, d_cb = _bwd_mix(dx1b, sig_a, sig_b, dm_dga, dm_dgb, wat, wbt, wout, H, ffn_ici[2])
    (g_wout,) = _dw_whole("dw_out", [(merged, dx1b)], True)
    g_wa, g_wb = _dw_whole("dw_branch", [(og, dya), (cb, dyb)], False)
    mix_sibling = sibling_start("mix", [g_wa, g_wb, g_wout])
    dproj, red_hg = _hgrn_bwd(proj, lower_bounds, hg_norm_g, o_pre, d_o, s_saved, H, mix_sibling[2], dproj)
    mix_ici = ici_start_behind("mix", [1, 2, 3], mix_sibling, red_hg)
    dproj, g_conv = _conv_bwd(proj, conv_full, d_cb, H, mix_ici[2], dproj)
    g_win, for_sibling = _dw_in(h, dproj, w_int3.shape[1], c_idx)
    in_sibling = _split_start("rs_sibling_start_in", [for_sibling, lax.empty(for_sibling.shape, BF16)], 1,
                              _sibling_whole_pairs)
    halves = sums("rest", [("mix", [1, 2, 3], mix_ici), ("ffn", [4, 5, 6], ffn_ici)], in_sibling[2])
    rest_share = _split_start("rs_share_start_rest", halves, len(halves), _share_pairs)
    from_sibling = _split_wait("rs_sibling_wait_in", in_sibling[0], in_sibling[1], _sibling_whole_pairs,
                               rest_share[2])[1]
    in_ici = ici_start("in", [0], [g_win], [from_sibling])
    grad_x, red_mix = _bwd_in(dproj, w_int, x2d, dx1, norm_mix_g, in_ici[2])
    in_share = _split_start("rs_share_start_in", sums("in", [("in", [0], in_ici)], grad_x), 1, _share_pairs)
    small_block = _small_pack(red_mix, red_ffn, red_final, red_hg, g_conv)
    small = _split_start("small_gather_start", [small_block, lax.empty((8,) + small_block.shape, F32)], 7,
                         _small_pairs)
    rest_grads = _split_wait("rs_share_wait_rest", rest_share[0], rest_share[1], _share_pairs, small[2])
    big_out = [None] + adamw("rest", [1, 2, 3, 4, 5, 6], rest_grads)
    in_grad = _split_wait("rs_share_wait_in", in_share[0], in_share[1], _share_pairs, big_out[6][0])
    big_out[0] = adamw("in", [0], in_grad)[0]
    small_block, small_all = _split_wait("small_gather_wait", small[0], small[1], _small_pairs, big_out[0][0])

    def smalls(mix, lb, hg, cw, ffn, fin):
        return [mix, lb, hg, cw[0], ffn, fin.reshape(1, D)]

    small_out = _small_update(
        small_block, small_all, jnp.stack([chip, 4 * x_i + 2 * y_i + core]).astype(jnp.int32),
        smalls(norm_mix_g, lower_bounds, hg_norm_g, conv_w, norm_ffn_g, norm_final_g),
        smalls(m_norm_mix_g, m_lower_bounds, m_hg_norm_g, m_conv_w, m_norm_ffn_g, m_norm_final_g),
        smalls(v_norm_mix_g, v_lower_bounds, v_hg_norm_g, v_conv_w, v_norm_ffn_g, v_norm_final_g))

    def outputs(i):
        big_i = [big_out[j][i] for j in range(7)]
        mix, lb, hg, cw, ffn, fin = [small_out[4 * p + i] for p in range(6)]
        return [mix, big_i[0][None], lb, hg, cw[None], big_i[1][None], big_i[2][None], big_i[3][None], ffn,
                big_i[4].T[None], big_i[5].T[None], big_i[6][None], fin.reshape(D)]

    outs = [small_out[24][0, 0], grad_x.reshape(1, L, D)]
    for i in range(4):
        outs += outputs(i)
    return tuple(outs)
```

```python
import functools

import jax
import jax.numpy as jnp
from jax import lax
from jax.experimental import pallas as pl
from jax.experimental.pallas import tpu as pltpu

F32 = jnp.float32
BF16 = jnp.bfloat16
EPS = 1e-6
CHUNK = 32
HEAD_DIM = 128
LANES = 128
N_CHIPS = 4
N_SMALL_ROWS = 16
DPROJ_BLOCKS = 12
DPROJ_BLOCK_OF = (0, 1, 2, 3, 8, 9, 10, 4, 5, 6, 7)

ADAM_LR = 0.001
ADAM_B1 = 0.9
ADAM_B2 = 0.999
ADAM_EPS = 1e-08
ADAM_WD = 0.01
ADAM_STEP = 10

MESH = pl.DeviceIdType.MESH
ANY = pl.BlockSpec(memory_space=pl.ANY)
VMEM = pl.BlockSpec(memory_space=pltpu.VMEM)
HBM = pl.BlockSpec(memory_space=pltpu.HBM)
SEM = pl.BlockSpec(memory_space=pltpu.SEMAPHORE)
EFFECT = pltpu.SideEffectType.DATAFLOW_SIDE_EFFECTING


def _sds(shape, dtype):
    return jax.ShapeDtypeStruct(shape, dtype)


def _pallas_call(body, pin=True, **kwargs):
    if not pin:
        return pl.pallas_call(body, **kwargs)
    in_hbm = lambda s: pltpu.HBM(s.shape, s.dtype) if isinstance(s, jax.ShapeDtypeStruct) else s
    kwargs["out_shape"] = jax.tree.map(in_hbm, kwargs["out_shape"])
    call = pl.pallas_call(body, **kwargs)

    def run(*args):
        return call(*[pltpu.with_memory_space_constraint(a, pltpu.HBM) if a.dtype in (F32, BF16) else a
                      for a in args])

    return run


def _params(semantics, vmem_mb):
    return pltpu.CompilerParams(dimension_semantics=semantics, vmem_limit_bytes=vmem_mb << 20)


def _nn(a, b):
    return lax.dot_general(a, b, (((1,), (0,)), ((), ())), preferred_element_type=F32)


def _nt(a, b):
    return lax.dot_general(a, b, (((1,), (1,)), ((), ())), preferred_element_type=F32)


def _tn(a, b):
    return lax.dot_general(a, b, (((0,), (0,)), ((), ())), preferred_element_type=F32)


def _sigmoid(x):
    return jax.nn.sigmoid(x)


def _rms_stats(x):
    r = lax.rsqrt(jnp.mean(x * x, axis=-1, keepdims=True) + EPS)
    return r, x * r


def _rms_bwd(dxh, xh, r):
    return r * (dxh - xh * jnp.mean(dxh * xh, axis=-1, keepdims=True))


def _fwd_proj_first(x, g_mix, w_int3, block):
    L, D = x.shape
    tn = w_int3.shape[1]
    tm = min(L, 1024)

    def body(blk_ref, x_ref, g_ref, w_ref, h_ref, p_ref):
        _, xh = _rms_stats(x_ref[...])
        h = (xh * g_ref[...]).astype(BF16)
        h_ref[...] = h
        p_ref[...] = _nt(h, w_ref[...])

    return _pallas_call(
        body, name="fwd_proj_own",
        grid_spec=pltpu.PrefetchScalarGridSpec(
            num_scalar_prefetch=1, grid=(L // tm,),
            in_specs=[pl.BlockSpec((tm, D), lambda i, blk: (i, 0)),
                      pl.BlockSpec((1, D), lambda i, blk: (0, 0)),
                      pl.BlockSpec((None, tn, D), lambda i, blk: (blk[0], 0, 0))],
            out_specs=[pl.BlockSpec((tm, D), lambda i, blk: (i, 0)),
                       pl.BlockSpec((tm, tn), lambda i, blk: (i, blk[0]))]),
        out_shape=[_sds((L, D), BF16), _sds((L, N_CHIPS * tn), F32)],
        compiler_params=_params(("parallel",), 48),
    )(block, x, g_mix, w_int3)


def _fwd_proj_more(name, h, w_int3, proj, blocks):
    L, D = h.shape
    tn = w_int3.shape[1]
    tm = min(L, 1024)

    def body(blk_ref, h_ref, w_ref, proj_ref, p_ref):
        p_ref[...] = _nt(h_ref[...], w_ref[...])

    return _pallas_call(
        body, name=name,
        grid_spec=pltpu.PrefetchScalarGridSpec(
            num_scalar_prefetch=1, grid=(L // tm, blocks.shape[0]),
            in_specs=[pl.BlockSpec((tm, D), lambda i, j, blk: (i, 0)),
                      pl.BlockSpec((None, tn, D), lambda i, j, blk: (blk[j], 0, 0)), ANY],
            out_specs=pl.BlockSpec((tm, tn), lambda i, j, blk: (i, blk[j]))),
        out_shape=_sds(proj.shape, proj.dtype),
        input_output_aliases={3: 0},
        compiler_params=_params(("parallel", "arbitrary"), 48),
    )(blocks, h, w_int3, proj)


def _lower_bound(lbp):
    l0, l1 = lbp[0:1, :], lbp[1:2, :]
    m = jnp.maximum(l0, l1)
    e0, e1 = jnp.exp(l0 - m), jnp.exp(l1 - m)
    return e0 / (e0 + e1)


def _seg_scan(x, r32, forward):
    n = x.shape[0]
    s = 1
    while s < CHUNK:
        if forward:
            x = x + jnp.where(r32 >= s, pltpu.roll(x, s, 0), 0.0)
        else:
            x = x + jnp.where(r32 < CHUNK - s, pltpu.roll(x, n - s, 0), 0.0)
        s *= 2
    return x


def _bcast_row(x, row):
    n, w = x.shape
    nc = n // CHUNK
    x3 = x.reshape(nc, CHUNK, w)
    return jnp.broadcast_to(x3[:, row:row + 1, :], (nc, CHUNK, w)).reshape(n, w)


def _chunk_total(x):
    n, w = x.shape
    nc = n // CHUNK
    total = jnp.sum(x.reshape(nc, CHUNK, w), axis=1, keepdims=True)
    return jnp.broadcast_to(total, (nc, CHUNK, w)).reshape(n, w)


def _hgrn_prep(q_raw, f_raw, lb):
    r32 = lax.broadcasted_iota(jnp.int32, f_raw.shape, 0) & (CHUNK - 1)
    sig = _sigmoid(f_raw)
    f = lb + (1.0 - lb) * sig
    b = _seg_scan(jnp.log(f), r32, True)
    a = _bcast_row(b, CHUNK // 2 - 1)
    bl = _bcast_row(b, CHUNK - 1)
    sq = _sigmoid(q_raw)
    q = q_raw * sq * (HEAD_DIM ** -0.5)
    return dict(r32=r32, sig=sig, f=f, k=1.0 - f, b=b, a=a, bl=bl, sq=sq, q=q)


def _chunk_masks(n):
    ri = lax.broadcasted_iota(jnp.int32, (n, n), 0)
    ci = lax.broadcasted_iota(jnp.int32, (n, n), 1)
    same = (ri // CHUNK) == (ci // CHUNK)
    return same & (ci <= ri), same & (ri <= ci)


def _hgrn_fwd(proj, lower_bounds, gamma, H):
    L = proj.shape[0]
    nh = H // HEAD_DIM
    TL = min(L, 256)
    nc = TL // CHUNK

    def body(q_ref, f_ref, v_ref, g_ref, lbp_ref, gam_ref, og_ref, o_ref, s_ref, st_ref):
        @pl.when(pl.program_id(0) == 0)
        def _():
            st_ref[...] = jnp.zeros_like(st_ref)

        lb = _lower_bound(lbp_ref[...])
        gam = gam_ref[...]
        mask, _ = _chunk_masks(TL)
        rowc = lax.broadcasted_iota(jnp.int32, (TL, HEAD_DIM), 0) // CHUNK
        for h in range(nh):
            hs = slice(h * HEAD_DIM, (h + 1) * HEAD_DIM)
            p = _hgrn_prep(q_ref[:, hs], f_ref[:, hs], lb[:, hs])
            v = v_ref[:, hs]
            vb = v.astype(BF16)
            vt = v.T.astype(BF16)
            q_hat = (p["q"] * jnp.exp(p["b"] - p["a"])).astype(BF16)
            k_hat = (p["k"] * jnp.exp(p["a"] - p["b"])).astype(BF16)
            q_in = (p["q"] * jnp.exp(p["b"])).astype(BF16)
            k_out = (p["k"] * jnp.exp(p["bl"] - p["b"])).astype(BF16)
            dec = jnp.exp(p["bl"])
            att = jnp.where(mask, _nt(q_hat, k_hat), 0.0).astype(BF16)
            o_intra = _nn(att, vb)
            st = st_ref[h]
            for c in range(nc):
                rs = slice(c * CHUNK, (c + 1) * CHUNK)
                stb = st.astype(BF16)
                s_ref[c, h] = stb
                o_ref[rs, hs] = o_intra[rs] + _nt(q_in[rs], stb)
                k_c = jnp.where(rowc == c, k_out, jnp.zeros_like(k_out))
                st = st * dec[c * CHUNK:c * CHUNK + 1, :] + _nn(vt, k_c)
            st_ref[h] = st
            o = o_ref[:, hs]
            _, xh = _rms_stats(o)
            gr = g_ref[:, hs]
            og_ref[:, hs] = (xh * gam * (gr * _sigmoid(gr))).astype(BF16)

    col = lambda k: pl.BlockSpec((TL, H), lambda i, k=k: (i, k))
    return _pallas_call(
        body, name="hgrn_fwd", grid=(L // TL,),
        in_specs=[col(0), col(1), col(2), col(3),
                  pl.BlockSpec(lower_bounds.shape, lambda i: (0, 0)),
                  pl.BlockSpec(gamma.shape, lambda i: (0, 0))],
        out_specs=[pl.BlockSpec((TL, H), lambda i: (i, 0)),
                   pl.BlockSpec((TL, H), lambda i: (i, 0)),
                   pl.BlockSpec((nc, nh, HEAD_DIM, HEAD_DIM), lambda i: (i, 0, 0, 0))],
        out_shape=[_sds((L, H), BF16), _sds((L, H), F32),
                   _sds((L // CHUNK, nh, HEAD_DIM, HEAD_DIM), BF16)],
        scratch_shapes=[pltpu.VMEM((nh, HEAD_DIM, HEAD_DIM), F32)],
        compiler_params=_params(("arbitrary",), 48),
    )(proj, proj, proj, proj, lower_bounds, gamma)


def _hgrn_bwd(proj, lower_bounds, gamma, o_pre, d_out, s_saved, H, after, dproj):
    L = proj.shape[0]
    nh = H // HEAD_DIM
    TL = min(L, 256)
    nc = TL // CHUNK
    nt = L // TL

    def body(q_ref, f_ref, v_ref, g_ref, lbp_ref, gam_ref, o_ref, d_ref, s_ref, after_ref, dproj_ref,
             dp_ref, red_ref, dst_ref, dsall_ref, tmp_ref):
        @pl.when(pl.program_id(0) == 0)
        def _():
            dst_ref[...] = jnp.zeros_like(dst_ref)
            red_ref[...] = jnp.zeros_like(red_ref)

        lb = _lower_bound(lbp_ref[...])
        gam = gam_ref[...]
        mask, mask_t = _chunk_masks(TL)
        rowc = lax.broadcasted_iota(jnp.int32, (TL, HEAD_DIM), 0) // CHUNK
        for h in range(nh):
            hs = slice(h * HEAD_DIM, (h + 1) * HEAD_DIM)
            qr, gr, lbh = q_ref[:, hs], g_ref[:, hs], lb[:, hs]
            p = _hgrn_prep(qr, f_ref[:, hs], lbh)
            vb = v_ref[:, hs].astype(BF16)
            eba, eab = jnp.exp(p["b"] - p["a"]), jnp.exp(p["a"] - p["b"])
            eb, elb = jnp.exp(p["b"]), jnp.exp(p["bl"] - p["b"])
            dec = jnp.exp(p["bl"])
            q_hat, k_hat = p["q"] * eba, p["k"] * eab
            q_in, k_out = p["q"] * eb, p["k"] * elb
            q_hat_b, k_hat_b = q_hat.astype(BF16), k_hat.astype(BF16)
            q_in_b, k_out_b = q_in.astype(BF16), k_out.astype(BF16)

            o, dout = o_ref[:, hs], d_ref[:, hs]
            sg = _sigmoid(gr)
            r, xh = _rms_stats(o)
            dp_ref[3, :, hs] = (dout * (xh * gam) * (sg * (1.0 + gr * (1.0 - sg)))).astype(BF16)
            dn = dout * (gr * sg)
            red_ref[1:2, hs] += jnp.sum(dn * xh, axis=0, keepdims=True)
            do = _rms_bwd(dn * gam, xh, r)
            dob = do.astype(BF16)
            dot_b = do.T.astype(BF16)

            att_t = jnp.where(mask_t, _nt(k_hat_b, q_hat_b), 0.0).astype(BF16)
            dv_intra = _nn(att_t, dob)
            datt = jnp.where(mask, _nt(dob, vb), 0.0).astype(BF16)
            dqh = _nn(datt, k_hat_b)
            datt_t = jnp.where(mask_t, _nt(vb, dob), 0.0).astype(BF16)
            dkh = _nn(datt_t, q_hat_b)

            dst = dst_ref[h]
            for c in reversed(range(nc)):
                dsall_ref[c] = dst
                q_c = jnp.where(rowc == c, q_in_b, jnp.zeros_like(q_in_b))
                dst = dst * dec[c * CHUNK:c * CHUNK + 1, :] + _nn(dot_b, q_c)
            dst_ref[h] = dst
            for c in range(nc):
                rs = slice(c * CHUNK, (c + 1) * CHUNK)
                ds_c = dsall_ref[c]
                dsb = ds_c.astype(BF16)
                st_prev = s_ref[c, h]
                tmp_ref[0, rs, :] = _nt(k_out_b[rs], dsb)
                tmp_ref[1, rs, :] = _nn(vb[rs], dsb)
                tmp_ref[2, rs, :] = _nn(dob[rs], st_prev)
                ddec = jnp.sum(ds_c * st_prev.astype(F32), axis=0, keepdims=True)
                tmp_ref[3, rs, :] = jnp.broadcast_to(ddec * dec[c * CHUNK:c * CHUNK + 1, :],
                                                     (CHUNK, HEAD_DIM))
            dko, dqi = tmp_ref[1], tmp_ref[2]
            dq = dqh * eba + dqi * eb
            dk = dkh * eab + dko * elb
            tko = dko * k_out
            db = dqh * q_hat - dkh * k_hat + dqi * q_in - tko
            dlog = _seg_scan(db, p["r32"], False) + _chunk_total(tko) + tmp_ref[3]
            df = dlog / p["f"] - dk
            sig = p["sig"]
            red_ref[0:1, hs] += jnp.sum(df * (1.0 - sig), axis=0, keepdims=True)
            dp_ref[1, :, hs] = (df * (1.0 - lbh) * sig * (1.0 - sig)).astype(BF16)
            sq = p["sq"]
            dp_ref[0, :, hs] = (dq * (HEAD_DIM ** -0.5) * (sq * (1.0 + qr * (1.0 - sq)))).astype(BF16)
            dp_ref[2, :, hs] = (dv_intra + tmp_ref[0]).astype(BF16)

    col = lambda k: pl.BlockSpec((TL, H), lambda i, k=k: (nt - 1 - i, k))
    rev = pl.BlockSpec((TL, H), lambda i: (nt - 1 - i, 0))
    return _pallas_call(
        body, name="hgrn_bwd", grid=(nt,),
        in_specs=[col(0), col(1), col(2), col(3),
                  pl.BlockSpec(lower_bounds.shape, lambda i: (0, 0)),
                  pl.BlockSpec(gamma.shape, lambda i: (0, 0)),
                  rev, rev,
                  pl.BlockSpec((nc, nh, HEAD_DIM, HEAD_DIM), lambda i: (nt - 1 - i, 0, 0, 0)), ANY, ANY],
        out_specs=[pl.BlockSpec((4, TL, H), lambda i: (0, nt - 1 - i, 0)), pl.BlockSpec((8, H), lambda i: (0, 0))],
        out_shape=[_sds(dproj.shape, BF16), _sds((8, H), F32)],
        input_output_aliases={10: 0},
        scratch_shapes=[pltpu.VMEM((nh, HEAD_DIM, HEAD_DIM), F32),
                        pltpu.VMEM((nc, HEAD_DIM, HEAD_DIM), F32),
                        pltpu.VMEM((4, TL, HEAD_DIM), F32)],
        compiler_params=_params(("arbitrary",), 48),
    )(proj, proj, proj, proj, lower_bounds, gamma, o_pre, d_out, s_saved, after, dproj)


def _shift_down(u, s, row):
    return jnp.where(row >= s, pltpu.roll(u, s, 0), 0.0)


def _shift_up(u, s, row):
    n = u.shape[0]
    return jnp.where(row < n - s, pltpu.roll(u, n - s, 0), 0.0)


def _conv_specs(L, H):
    per = H // LANES
    return [pl.BlockSpec((L, LANES), lambda j, o=o: (0, o * per + j)) for o in (4, 5, 6)]


def _conv_fwd(proj, conv_w, H, after):
    L = proj.shape[0]

    def body(c_ref, b_ref, x_ref, w_ref, after_ref, o_ref):
        row = lax.broadcasted_iota(jnp.int32, (L, LANES), 0)
        u = c_ref[...] * x_ref[...]
        w = w_ref[...]
        y = w[0:1] * _shift_down(u, 2, row) + w[1:2] * _shift_down(u, 1, row) + w[2:3] * u
        o_ref[...] = (b_ref[...] * y).astype(BF16)

    return _pallas_call(
        body, name="conv_fwd", grid=(H // LANES,),
        in_specs=_conv_specs(L, H) + [pl.BlockSpec((3, LANES), lambda j: (0, j)), ANY],
        out_specs=pl.BlockSpec((L, LANES), lambda j: (0, j)),
        out_shape=_sds((L, H), BF16),
        compiler_params=_params(("parallel",), 48),
    )(proj, proj, proj, conv_w, after)


def _conv_bwd(proj, conv_w, dcb, H, after, dproj):
    L = proj.shape[0]

    def body(c_ref, b_ref, x_ref, w_ref, d_ref, after_ref, dproj_ref, dp_ref, dw_ref):
        row = lax.broadcasted_iota(jnp.int32, (L, LANES), 0)
        cg, xb = c_ref[...], x_ref[...]
        u = cg * xb
        u1, u2 = _shift_down(u, 1, row), _shift_down(u, 2, row)
        w = w_ref[...]
        y = w[0:1] * u2 + w[1:2] * u1 + w[2:3] * u
        d = d_ref[...]
        dp_ref[1] = (d * y).astype(BF16)
        dy = d * b_ref[...]
        du = w[2:3] * dy + w[1:2] * _shift_up(dy, 1, row) + w[0:1] * _shift_up(dy, 2, row)
        dw_ref[0:1, :] = jnp.sum(dy * u2, axis=0, keepdims=True)
        dw_ref[1:2, :] = jnp.sum(dy * u1, axis=0, keepdims=True)
        dw_ref[2:3, :] = jnp.sum(dy * u, axis=0, keepdims=True)
        dp_ref[0] = (du * xb).astype(BF16)
        dp_ref[2] = (du * cg).astype(BF16)
        dp_ref[3] = jnp.zeros((L, LANES), BF16)

    blk = pl.BlockSpec((L, LANES), lambda j: (0, j))
    return _pallas_call(
        body, name="conv_bwd", grid=(H // LANES,),
        in_specs=_conv_specs(L, H) + [pl.BlockSpec((3, LANES), lambda j: (0, j)), blk, ANY, ANY],
        out_specs=[pl.BlockSpec((4, L, LANES), lambda j: (2, 0, j)), pl.BlockSpec((3, LANES), lambda j: (0, j))],
        out_shape=[_sds(dproj.shape, BF16), _sds((3, H), F32)],
        input_output_aliases={6: 0},
        compiler_params=_params(("parallel",), 56),
    )(proj, proj, proj, conv_w, dcb, after, dproj)


def _gate_specs(tm, H):
    return [pl.BlockSpec((tm, H), lambda i, k=k: (i, k)) for k in (7, 8, 9, 10)]


def _fwd_mix(og, cb, proj, x, wat, wbt, wout, g_ffn, H, after):
    L, D = x.shape
    tm = min(L, 512)

    def body(o_ref, cb_ref, ga0, ga1, gb0, gb1, x_ref, wa_ref, wb_ref, wo_ref, g_ref, after_ref,
             sa_ref, sb_ref, ta_ref, tb_ref, m_ref, x1_ref, h2_ref):
        ya, yb = _nt(o_ref[...], wa_ref[...]), _nt(cb_ref[...], wb_ref[...])
        for k, (gar, gbr) in enumerate(((ga0, gb0), (ga1, gb1))):
            cs = slice(k * H, (k + 1) * H)
            sa, sb = _sigmoid(gar[...]), _sigmoid(gbr[...])
            ma, mb = sa * ya[:, cs], sb * yb[:, cs]
            m_ref[:, cs] = (ma + mb).astype(BF16)
            sa_ref[:, cs] = sa.astype(BF16)
            sb_ref[:, cs] = sb.astype(BF16)
            ta_ref[:, cs] = (ma * (1.0 - sa)).astype(BF16)
            tb_ref[:, cs] = (mb * (1.0 - sb)).astype(BF16)
        x1 = x_ref[...] + _nn(m_ref[...], wo_ref[...])
        x1_ref[...] = x1
        _, xh = _rms_stats(x1)
        h2_ref[...] = (xh * g_ref[...]).astype(BF16)

    row = lambda w: pl.BlockSpec((tm, w), lambda i: (i, 0))
    full = lambda a: pl.BlockSpec(a.shape, lambda i: (0,) * a.ndim)
    return _pallas_call(
        body, name="fwd_mix", grid=(L // tm,),
        in_specs=[row(H), row(H)] + _gate_specs(tm, H) + [row(D), full(wat), full(wbt), full(wout),
                                                           full(g_ffn), ANY],
        out_specs=[row(D)] * 7,
        out_shape=[_sds((L, D), BF16)] * 5 + [_sds((L, D), F32), _sds((L, D), BF16)],
        compiler_params=_params(("parallel",), 56),
    )(og, cb, proj, proj, proj, proj, x, wat, wbt, wout, g_ffn, after)


def _bwd_mix(dx1b, sig_a, sig_b, dm_dga, dm_dgb, wat, wbt, wout, H, after):
    L, D = dx1b.shape
    tm = min(L, 512)

    def body(dx_ref, sa_ref, sb_ref, ta_ref, tb_ref, wa_ref, wb_ref, wo_ref, after_ref,
             dya_ref, dyb_ref, dgate_ref, do_ref, dcb_ref):
        dm = _nt(dx_ref[...], wo_ref[...])
        dga = (dm * ta_ref[...].astype(F32)).astype(BF16)
        dgb = (dm * tb_ref[...].astype(F32)).astype(BF16)
        for q, part in enumerate((dga[:, 0:H], dga[:, H:D], dgb[:, 0:H], dgb[:, H:D])):
            dgate_ref[q] = part
        dya_ref[...] = (dm * sa_ref[...].astype(F32)).astype(BF16)
        dyb_ref[...] = (dm * sb_ref[...].astype(F32)).astype(BF16)
        do_ref[...] = _nn(dya_ref[...], wa_ref[...])
        dcb_ref[...] = _nn(dyb_ref[...], wb_ref[...])

    row = lambda w: pl.BlockSpec((tm, w), lambda i: (i, 0))
    full = lambda a: pl.BlockSpec(a.shape, lambda i: (0,) * a.ndim)
    return _pallas_call(
        body, name="bwd_mix", grid=(L // tm,),
        in_specs=[row(D)] * 5 + [full(wat), full(wbt), full(wout), ANY],
        out_specs=[row(D), row(D), pl.BlockSpec((4, tm, H), lambda i: (1, i, 0)), row(H), row(H)],
        out_shape=[_sds((L, D), BF16)] * 2 + [_sds((DPROJ_BLOCKS, L, H), BF16)] + [_sds((L, H), F32)] * 2,
        compiler_params=_params(("parallel",), 56),
    )(dx1b, sig_a, sig_b, dm_dga, dm_dgb, wat, wbt, wout, after)


def _fwd_ffn_up(h2, wgt, wut):
    L, D = h2.shape
    F = wgt.shape[0]
    tn = F // 2
    tm = min(L, 512)

    def body(h_ref, wg_ref, wu_ref, sa_ref, sb_ref, s_ref):
        h = h_ref[...]
        a, b = _nt(h, wg_ref[...]), _nt(h, wu_ref[...])
        sg = _sigmoid(a)
        silu = a * sg
        sa_ref[...] = (b * sg * (1.0 + a * (1.0 - sg))).astype(BF16)
        sb_ref[...] = silu.astype(BF16)
        s_ref[...] = (silu * b).astype(BF16)

    wspec = pl.BlockSpec((tn, D), lambda j, i: (j, 0))
    ospec = pl.BlockSpec((tm, tn), lambda j, i: (i, j))
    return _pallas_call(
        body, name="fwd_ffn_up", grid=(2, L // tm),
        in_specs=[pl.BlockSpec((tm, D), lambda j, i: (i, 0)), wspec, wspec],
        out_specs=[ospec] * 3,
        out_shape=[_sds((L, F), BF16)] * 3,
        compiler_params=_params(("parallel", "parallel"), 48),
    )(h2, wgt, wut)


def _fwd_down_loss(s, wd, x1, target, g_final):
    L, D = x1.shape
    F = wd.shape[0]
    tm = min(L, 512)

    def body(s_ref, wd_ref, x1_ref, t_ref, g_ref, dx_ref, dxb_ref, red_ref):
        @pl.when(pl.program_id(0) == 0)
        def _():
            red_ref[...] = jnp.zeros_like(red_ref)

        g = g_ref[...]
        r, xh = _rms_stats(x1_ref[...] + _nn(s_ref[...], wd_ref[...]))
        e = xh * g - t_ref[...]
        dy = e * (1.0 / D)
        dx = _rms_bwd(dy * g, xh, r)
        dx_ref[...] = dx
        dxb_ref[...] = dx.astype(BF16)
        red_ref[0:1, :] += jnp.sum(dy * xh, axis=0, keepdims=True)
        red_ref[1:2, :] += jnp.broadcast_to(0.5 * jnp.sum(e * e) * (1.0 / D), (1, D))

    row = pl.BlockSpec((tm, D), lambda i: (i, 0))
    return _pallas_call(
        body, name="fwd_down_loss", grid=(L // tm,),
        in_specs=[pl.BlockSpec((tm, F), lambda i: (i, 0)), pl.BlockSpec((F, D), lambda i: (0, 0)),
                  row, row, pl.BlockSpec((1, D), lambda i: (0, 0))],
        out_specs=[row, row, pl.BlockSpec((8, D), lambda i: (0, 0))],
        out_shape=[_sds((L, D), F32), _sds((L, D), BF16), _sds((8, D), F32)],
        compiler_params=_params(("arbitrary",), 56),
    )(s, wd, x1, target, g_final)


def _bwd_down(dx2b, wd, s_a, s_b):
    L, D = dx2b.shape
    F = wd.shape[0]
    tn = F // 2
    tm = min(L, 512)

    def body(dx_ref, wd_ref, sa_ref, sb_ref, da_ref, db_ref):
        ds = _nt(dx_ref[...], wd_ref[...])
        da_ref[...] = (ds * sa_ref[...].astype(F32)).astype(BF16)
        db_ref[...] = (ds * sb_ref[...].astype(F32)).astype(BF16)

    ospec = pl.BlockSpec((tm, tn), lambda j, i: (i, j))
    return _pallas_call(
        body, name="bwd_down", grid=(2, L // tm),
        in_specs=[pl.BlockSpec((tm, D), lambda j, i: (i, 0)),
                  pl.BlockSpec((tn, D), lambda j, i: (j, 0)), ospec, ospec],
        out_specs=[ospec] * 2,
        out_shape=[_sds((L, F), BF16)] * 2,
        compiler_params=_params(("parallel", "parallel"), 48),
    )(dx2b, wd, s_a, s_b)


def _bwd_ffn_dh(da, db, wgt, wut, x1, dx2, g_ffn, after):
    L, D = x1.shape
    F = wgt.shape[0]
    tm = min(L, 256)

    def body(da_ref, db_ref, wg_ref, wu_ref, x1_ref, dx2_ref, g_ref, after_ref, dx_ref, dxb_ref, red_ref):
        @pl.when(pl.program_id(0) == 0)
        def _():
            red_ref[...] = jnp.zeros_like(red_ref)

        dh = _nn(da_ref[...], wg_ref[...]) + _nn(db_ref[...], wu_ref[...])
        r, xh = _rms_stats(x1_ref[...])
        red_ref[0:1, :] += jnp.sum(dh * xh, axis=0, keepdims=True)
        dx = dx2_ref[...] + _rms_bwd(dh * g_ref[...], xh, r)
        dx_ref[...] = dx
        dxb_ref[...] = dx.astype(BF16)

    row = pl.BlockSpec((tm, D), lambda i: (i, 0))
    aspec = pl.BlockSpec((tm, F), lambda i: (i, 0))
    wspec = pl.BlockSpec((F, D), lambda i: (0, 0))
    return _pallas_call(
        body, name="bwd_ffn_dh", grid=(L // tm,),
        in_specs=[aspec, aspec, wspec, wspec, row, row, pl.BlockSpec((1, D), lambda i: (0, 0)), ANY],
        out_specs=[row, row, pl.BlockSpec((8, D), lambda i: (0, 0))],
        out_shape=[_sds((L, D), F32), _sds((L, D), BF16), _sds((8, D), F32)],
        compiler_params=_params(("arbitrary",), 56),
    )(da, db, wgt, wut, x1, dx2, g_ffn, after)


def _bwd_in(dproj, w_int, x, dx1, g_mix, after):
    L, D = x.shape
    N = w_int.shape[0]
    H = dproj.shape[2]
    tm = min(L, 256)
    assert N == len(DPROJ_BLOCK_OF) * H

    def body(blocks_ref, w_ref, x_ref, dx1_ref, g_ref, after_ref, dx_ref, red_ref, dp_ref):
        @pl.when(pl.program_id(0) == 0)
        def _():
            red_ref[...] = jnp.zeros_like(red_ref)

        for t, block in enumerate(DPROJ_BLOCK_OF):
            dp_ref[:, t * H:(t + 1) * H] = blocks_ref[block]
        dh = _nn(dp_ref[...], w_ref[...])
        r, xh = _rms_stats(x_ref[...])
        red_ref[0:1, :] += jnp.sum(dh * xh, axis=0, keepdims=True)
        dx_ref[...] = dx1_ref[...] + _rms_bwd(dh * g_ref[...], xh, r)

    row = pl.BlockSpec((tm, D), lambda i: (i, 0))
    return _pallas_call(
        body, name="bwd_in", grid=(L // tm,),
        in_specs=[pl.BlockSpec((DPROJ_BLOCKS, tm, H), lambda i: (0, i, 0)), pl.BlockSpec((N, D), lambda i: (0, 0)),
                  row, row, pl.BlockSpec((1, D), lambda i: (0, 0)), ANY],
        out_specs=[row, pl.BlockSpec((8, D), lambda i: (0, 0))],
        out_shape=[_sds((L, D), F32), _sds((8, D), F32)],
        scratch_shapes=[pltpu.VMEM((tm, N), BF16)],
        compiler_params=_params(("arbitrary",), 56),
    )(dproj, w_int, x, dx1, g_mix, after)


def _dw_in(h, dproj, n_cols, c_idx):
    L, D = h.shape
    H = dproj.shape[2]
    tk = min(L, TK_TOKENS)
    nk = L // tk
    r2 = D // 2
    first = [(j * n_cols) // H for j in range(N_CHIPS)]
    last = [((j + 1) * n_cols - 1) // H for j in range(N_CHIPS)]
    slots = max(b - a for a, b in zip(first, last)) + 1
    plan = []
    for j in range(N_CHIPS):
        lo, hi = j * n_cols, (j + 1) * n_cols
        segments = []
        for s in range(last[j] - first[j] + 1):
            a, b = max(lo, (first[j] + s) * H), min(hi, (first[j] + s + 1) * H)
            segments.append((s, a - (first[j] + s) * H, b - a, a - lo))
        plan.append(segments)

    def body(c_ref, *refs):
        h_ref, slot_refs = refs[0], refs[1:1 + slots]
        o_ref, sib_ref, b_ref = refs[1 + slots:]
        j, k = pl.program_id(0), pl.program_id(1)
        for jj in range(N_CHIPS):
            @pl.when(j == jj)
            def _(jj=jj):
                for s, start, width, at in plan[jj]:
                    b_ref[:, at:at + width] = slot_refs[s][:, start:start + width]

        part = _tn(h_ref[...], b_ref[...])

        @pl.when(k == 0)
        def _():
            o_ref[...] = part

        @pl.when(k > 0)
        def _():
            o_ref[...] += part

        @pl.when(k == nk - 1)
        def _():
            theirs = pl.ds(pl.multiple_of((1 - c_ref[0]) * r2, 8), r2)
            sib_ref[...] = o_ref[theirs, :].astype(BF16)

    def slot_spec(s):
        blocks = [DPROJ_BLOCK_OF[min(first[j] + s, last[j])] for j in range(N_CHIPS)]

        def index(j, k, c_ref):
            block = blocks[0]
            for jj in range(1, N_CHIPS):
                block = jnp.where(j == jj, blocks[jj], block)
            return (block, k, 0)

        return pl.BlockSpec((None, tk, H), index)

    return _pallas_call(
        body, name="dw_in",
        grid_spec=pltpu.PrefetchScalarGridSpec(
            num_scalar_prefetch=1, grid=(N_CHIPS, nk),
            in_specs=[pl.BlockSpec((tk, D), lambda j, k, c_ref: (k, 0))] + [slot_spec(s) for s in range(slots)],
            out_specs=[pl.BlockSpec((None, D, n_cols), lambda j, k, c_ref: (j, 0, 0)),
                       pl.BlockSpec((None, r2, n_cols), lambda j, k, c_ref: (j, 0, 0))],
            scratch_shapes=[pltpu.VMEM((tk, n_cols), BF16)]),
        out_shape=[_sds((N_CHIPS, D, n_cols), F32), _sds((N_CHIPS, r2, n_cols), BF16)],
        compiler_params=_params(("parallel", "arbitrary"), 56),
    )(c_idx, h, *([dproj] * slots))


def _mm_tn(name, a, b, a_spec, b_spec, o_block, n_out, n_k):
    def body(a_ref, b_ref, o_ref):
        part = _tn(a_ref[...], b_ref[...])

        @pl.when(pl.program_id(1) == 0)
        def _():
            o_ref[...] = part

        @pl.when(pl.program_id(1) > 0)
        def _():
            o_ref[...] += part

    return _pallas_call(
        body, name=name, grid=(n_out, n_k),
        in_specs=[a_spec, b_spec],
        out_specs=pl.BlockSpec((None,) + o_block, lambda j, k: (j, 0, 0)),
        out_shape=_sds((n_out,) + o_block, F32),
        compiler_params=_params(("parallel", "arbitrary"), 56),
    )(a, b)


TK_TOKENS = 2048


def _dw_whole(name, pairs, by_rows):
    n = len(pairs)
    L = pairs[0][0].shape[0]
    tk = min(L, TK_TOKENS)

    def body(*refs):
        for q in range(n):
            a_ref, b_ref, o_ref = refs[2 * q], refs[2 * q + 1], refs[2 * n + q]
            part = _tn(a_ref[...], b_ref[...])
            rows, cols = o_ref.shape[1], o_ref.shape[2]
            shards = [part[j * rows:(j + 1) * rows, :] if by_rows else part[:, j * cols:(j + 1) * cols]
                      for j in range(N_CHIPS)]

            @pl.when(pl.program_id(0) == 0)
            def _(shards=shards, o_ref=o_ref):
                for j, shard in enumerate(shards):
                    o_ref[j] = shard

            @pl.when(pl.program_id(0) > 0)
            def _(shards=shards, o_ref=o_ref):
                for j, shard in enumerate(shards):
                    o_ref[j] += shard

    in_specs, out_specs, out_shape, operands = [], [], [], []
    for a, b in pairs:
        M, N = a.shape[1], b.shape[1]
        shape = (N_CHIPS, M // N_CHIPS, N) if by_rows else (N_CHIPS, M, N // N_CHIPS)
        in_specs += [pl.BlockSpec((tk, M), lambda k: (k, 0)), pl.BlockSpec((tk, N), lambda k: (k, 0))]
        out_specs.append(pl.BlockSpec(shape, lambda k: (0, 0, 0)))
        out_shape.append(_sds(shape, F32))
        operands += [a, b]
    return _pallas_call(
        body, name=name, grid=(L // tk,), in_specs=in_specs, out_specs=out_specs, out_shape=out_shape,
        compiler_params=_params(("arbitrary",), 56),
    )(*operands)


def _dw_rows2(name, a, b):
    L, M = a.shape
    N = b.shape[1]
    tk = min(L, TK_TOKENS)
    return _mm_tn(name, a, b, pl.BlockSpec((tk, M // 2), lambda j, k: (k, j)),
                  pl.BlockSpec((tk, N), lambda j, k: (k, 0)), (M // 2, N), 2, L // tk)


def _place():
    x, y, c = lax.axis_index("x"), lax.axis_index("y"), lax.axis_index("c")
    chips = [(1 - x, y), (x, 1 - y), (1 - x, 1 - y)]
    return x, y, c, 2 * x + y, chips


def _remote(src, dst, send_sem, recv_sem, device):
    return pltpu.make_async_remote_copy(src_ref=src, dst_ref=dst, send_sem=send_sem,
                                        recv_sem=recv_sem, device_id=device, device_id_type=MESH)


def _half(ref, lead, c, r2):
    return ref.at[lead, pl.ds(pl.multiple_of(c * r2, 16), r2), :]


def _cast_place(name, ws, chip_idx, after):
    n = len(ws)

    def body(k_ref, *refs):
        for w_ref, o_ref in zip(refs[:n], refs[n + 1:]):
            o_ref[...] = w_ref[...].astype(BF16)

    return _pallas_call(
        body, name=name,
        grid_spec=pltpu.PrefetchScalarGridSpec(
            num_scalar_prefetch=1, grid=(2,),
            in_specs=[pl.BlockSpec((w.shape[0] // 2, w.shape[1]), lambda i, k_ref: (i, 0)) for w in ws] + [ANY],
            out_specs=[pl.BlockSpec((None, w.shape[0] // 2, w.shape[1]), lambda i, k_ref: (k_ref[0], i, 0))
                       for w in ws]),
        out_shape=[_sds((N_CHIPS,) + w.shape, BF16) for w in ws],
        compiler_params=_params(("parallel",), 48),
    )(chip_idx, *ws, after)


def _cast_place_t(name, ws, chip_idx, after):
    n = len(ws)
    r, cols = ws[0].shape

    def body(k_ref, *refs):
        for w_ref, o_ref in zip(refs[:n], refs[n + 1:]):
            o_ref[...] = w_ref[...].T.astype(BF16)

    return _pallas_call(
        body, name=name,
        grid_spec=pltpu.PrefetchScalarGridSpec(
            num_scalar_prefetch=1, grid=(cols // LANES,),
            in_specs=[pl.BlockSpec((r, LANES), lambda i, k_ref: (0, i))] * n + [ANY],
            out_specs=[pl.BlockSpec((None, LANES, r), lambda i, k_ref: (k_ref[0], i, 0))] * n),
        out_shape=[_sds((N_CHIPS, cols, r), BF16)] * n,
        compiler_params=_params(("parallel",), 48),
    )(chip_idx, *ws, after)


def _gather_copies(bufs, whole, send_sems, recv_sems, select=None):
    x, y, c, k, chips = _place()
    pairs = []
    for w, buf in enumerate(bufs):
        for j, (cx, cy) in enumerate(chips):
            if select is not None and not select(w, j):
                continue
            if w in whole:
                mine, theirs = buf.at[k], buf.at[2 * cx + cy]
            else:
                r2 = buf.shape[1] // 2
                mine, theirs = _half(buf, k, c, r2), _half(buf, 2 * cx + cy, c, r2)
            sems = (send_sems.at[w * 3 + j], recv_sems.at[w * 3 + j])
            pairs.append((_remote(mine, mine, *sems, (cx, cy, c)), _remote(theirs, theirs, *sems, (x, y, c))))
    return pairs


def _gather_start(name, groups, after):
    flat = [b for bufs, _, _ in groups for b in bufs]
    nb, ng = len(flat), len(groups)

    def body(*refs):
        ins, sems, token = refs[:nb], refs[nb + 1:nb + 1 + 2 * ng], refs[-1]
        pos = 0
        for g, (bufs, whole, select) in enumerate(groups):
            for send, _ in _gather_copies(ins[pos:pos + len(bufs)], whole, sems[2 * g], sems[2 * g + 1], select):
                send.start()
            pos += len(bufs)
        token[...] = jnp.zeros_like(token)

    sem_shapes = []
    for bufs, _, _ in groups:
        sem_shapes += [pltpu.SemaphoreType.DMA((3 * len(bufs),))] * 2
    out = _pallas_call(
        body, name=name,
        in_specs=[HBM] * nb + [ANY], out_specs=tuple([SEM] * (2 * ng) + [HBM] * nb + [VMEM]),
        out_shape=tuple(sem_shapes + [pltpu.HBM(b.shape, b.dtype) for b in flat] + [_sds((8, LANES), F32)]),
        input_output_aliases={i: 2 * ng + i for i in range(nb)},
        compiler_params=pltpu.CompilerParams(has_side_effects=EFFECT),
    )(*flat, after)
    sems, thru, pos = [], [], 2 * ng
    for g, (bufs, _, _) in enumerate(groups):
        sems.append((out[2 * g], out[2 * g + 1]))
        thru.append(list(out[pos:pos + len(bufs)]))
        pos += len(bufs)
    return sems, thru, out[-1]


def _gather_wait(name, bufs, whole, sems, after, select=None):
    nb = len(bufs)

    def body(*refs):
        ins, send_sems, recv_sems = refs[:nb], refs[nb], refs[nb + 1]
        for send, arrival in _gather_copies(ins, whole, send_sems, recv_sems, select):
            send.wait_send()
            arrival.wait_recv()

    return _pallas_call(
        body, name=name,
        in_specs=[HBM] * nb + [SEM, SEM, ANY], out_specs=[HBM] * nb,
        out_shape=[pltpu.HBM(b.shape, b.dtype) for b in bufs],
        input_output_aliases={i: i for i in range(nb)},
        compiler_params=pltpu.CompilerParams(has_side_effects=EFFECT),
    )(*bufs, sems[0], sems[1], after)


def _gather_forward(name, bufs, sources=(0, 1, 2)):
    n = len(bufs)

    def body(*refs):
        outs = refs[n:2 * n]
        send_sems, recv_sems = refs[2 * n:]
        x, y, c, _, chips = _place()
        sends = []
        for w in range(n):
            r2 = outs[w].shape[1] // 2
            for j in sources:
                landed = _half(outs[w], 2 * chips[j][0] + chips[j][1], c, r2)
                sends.append(_remote(landed, landed, send_sems.at[w * 3 + j], recv_sems.at[w * 3 + j],
                                     (x, y, 1 - c)))
        for cp in sends:
            cp.start()
        for w in range(n):
            r2 = outs[w].shape[1] // 2
            for j in sources:
                got = _half(outs[w], 2 * chips[j][0] + chips[j][1], 1 - c, r2)
                _remote(got, got, send_sems.at[w * 3 + j], recv_sems.at[w * 3 + j], (x, y, c)).wait_recv()
        for cp in sends:
            cp.wait_send()

    return _pallas_call(
        body, name=name,
        in_specs=[ANY] * n, out_specs=[ANY] * n,
        out_shape=[_sds(b.shape, b.dtype) for b in bufs],
        input_output_aliases={i: i for i in range(n)},
        scratch_shapes=[pltpu.SemaphoreType.DMA((n * 3,)), pltpu.SemaphoreType.DMA((n * 3,))],
    )(*bufs)


def _rs_add(name, grads3, from_sibling, c_idx):
    n = len(grads3)

    def body(c_ref, *refs):
        for g_ref, s_ref, o_ref in zip(refs[:n], refs[n:2 * n], refs[2 * n:]):
            o_ref[...] = (g_ref[...] + s_ref[...].astype(F32)).astype(BF16)

    mine =[pl.BlockSpec((None,) + s.shape[1:], lambda k, c_ref: (k, c_ref[0], 0)) for s in from_sibling]
    whole = [pl.BlockSpec((None,) + s.shape[1:], lambda k, c_ref: (k, 0, 0)) for s in from_sibling]
    return _pallas_call(
        body, name=name,
        grid_spec=pltpu.PrefetchScalarGridSpec(num_scalar_prefetch=1, grid=(N_CHIPS,), in_specs=mine + whole,
                                               out_specs=whole),
        out_shape=[_sds(s.shape, BF16) for s in from_sibling],
        compiler_params=_params(("parallel",), 48),
    )(c_idx, *grads3, *from_sibling)


def _split_start(name, arrays, n_sems, pairs_fn):
    n = len(arrays)

    def body(*refs):
        for send, _ in pairs_fn(refs[:n], refs[n], refs[n + 1]):
            send.start()
        refs[-1][...] = jnp.zeros_like(refs[-1])

    out = _pallas_call(
        body, name=name,
        in_specs=[HBM] * n, out_specs=tuple([SEM, SEM] + [HBM] * n + [VMEM]),
        out_shape=tuple([pltpu.SemaphoreType.DMA((n_sems,))] * 2 + [pltpu.HBM(a.shape, a.dtype) for a in arrays]
                        + [_sds((8, LANES), F32)]),
        input_output_aliases={i: 2 + i for i in range(n)},
        compiler_params=pltpu.CompilerParams(has_side_effects=EFFECT),
    )(*arrays)
    return (out[0], out[1]), list(out[2:2 + n]), out[-1]


def _split_wait(name, sems, arrays, pairs_fn, after):
    n = len(arrays)

    def body(*refs):
        for send, arrival in pairs_fn(refs[:n], refs[n], refs[n + 1]):
            send.wait_send()
            arrival.wait_recv()

    return list(_pallas_call(
        body, name=name,
        in_specs=[HBM] * n + [SEM, SEM, ANY], out_specs=[HBM] * n,
        out_shape=[pltpu.HBM(a.shape, a.dtype) for a in arrays],
        input_output_aliases={i: i for i in range(n)},
        compiler_params=pltpu.CompilerParams(has_side_effects=EFFECT),
    )(*arrays, sems[0], sems[1], after))


def _forward_pairs_from(sources):
    def pairs_fn(bufs, send_sems, recv_sems):
        x, y, c, _, chips = _place()
        pairs = []
        for w, buf in enumerate(bufs):
            r2 = buf.shape[1] // 2
            for j in sources:
                k = 2 * chips[j][0] + chips[j][1]
                landed, theirs = _half(buf, k, c, r2), _half(buf, k, 1 - c, r2)
                sems = (send_sems.at[w * 3 + j], recv_sems.at[w * 3 + j])
                pairs.append((_remote(landed, landed, *sems, (x, y, 1 - c)),
                              _remote(theirs, theirs, *sems, (x, y, c))))
        return pairs

    return pairs_fn


_forward_pairs = _forward_pairs_from((0, 1, 2))


def _sibling_pairs(arrays, send_sems, recv_sems):
    x, y, c, _, _ = _place()
    n = len(arrays) // 2
    pairs = []
    for w in range(n):
        r2 = arrays[w].shape[1] // 2
        cp = _remote(_half(arrays[w], slice(None), 1 - c, r2), arrays[n + w], send_sems.at[w], recv_sems.at[w],
                     (x, y, 1 - c))
        pairs.append((cp, cp))
    return pairs


def _sibling_whole_pairs(arrays, send_sems, recv_sems):
    x, y, c, _, _ = _place()
    n = len(arrays) // 2
    pairs = []
    for w in range(n):
        cp = _remote(arrays[w], arrays[n + w], send_sems.at[w], recv_sems.at[w], (x, y, 1 - c))
        pairs.append((cp, cp))
    return pairs


def _ici_pairs(arrays, send_sems, recv_sems):
    x, y, c, _, chips = _place()
    n = len(arrays) // 2
    pairs = []
    for w in range(n):
        for j, (cx, cy) in enumerate(chips):
            cp = _remote(arrays[w].at[2 * cx + cy], arrays[n + w].at[j],
                         send_sems.at[w * 3 + j], recv_sems.at[w * 3 + j], (cx, cy, c))
            pairs.append((cp, cp))
    return pairs


def _rs_sum(name, partials, received, place_idx):
    n = len(partials)
    nb = 2
    blocks = [(p.shape[1] // nb, p.shape[2]) for p in partials]

    def body(idx_ref, *refs):
        for p_ref, r_ref, o_ref in zip(refs[:n], refs[n:2 * n], refs[2 * n:]):
            o_ref[...] = ((p_ref[...].astype(F32) + r_ref[0].astype(F32))
                          + (r_ref[1].astype(F32) + r_ref[2].astype(F32)))

    return _pallas_call(
        body, name=name,
        grid_spec=pltpu.PrefetchScalarGridSpec(
            num_scalar_prefetch=1, grid=(nb,),
            in_specs=[pl.BlockSpec((None,) + b, lambda i, idx: (idx[0], i, 0)) for b in blocks]
            + [pl.BlockSpec((3,) + b, lambda i, idx: (0, i, 0)) for b in blocks],
            out_specs=[pl.BlockSpec(b, lambda i, idx: (idx[1] * nb + i, 0)) for b in blocks]),
        out_shape=[_sds((2 * p.shape[1], p.shape[2]), F32) for p in partials],
        compiler_params=_params(("parallel",), 48),
    )(place_idx, *partials, *received)


def _share_pairs(arrays, send_sems, recv_sems):
    x, y, c, _, _ = _place()
    pairs = []
    for w, arr in enumerate(arrays):
        r2 = arr.shape[0] // 2
        mine = arr.at[pl.ds(pl.multiple_of(c * r2, 8), r2), :]
        theirs = arr.at[pl.ds(pl.multiple_of((1 - c) * r2, 8), r2), :]
        sems = (send_sems.at[w], recv_sems.at[w])
        pairs.append((_remote(mine, mine, *sems, (x, y, 1 - c)), _remote(theirs, theirs, *sems, (x, y, c))))
    return pairs


def _small_pack(red_mix, red_ffn, red_final, red_hg, g_conv):
    D = red_mix.shape[1]
    H = red_hg.shape[1]

    def body(mix_ref, ffn_ref, fin_ref, hg_ref, cv_ref, in_ref):
        in_ref[...] = jnp.zeros_like(in_ref)
        in_ref[0:1, :] = mix_ref[0:1, :]
        in_ref[1:2, :] = ffn_ref[0:1, :]
        in_ref[2:3, :] = fin_ref[0:1, :]
        gam = hg_ref[1:2, 0:HEAD_DIM]
        for h in range(1, H // HEAD_DIM):
            gam = gam + hg_ref[1:2, h * HEAD_DIM:(h + 1) * HEAD_DIM]
        in_ref[3:4, 0:HEAD_DIM] = gam
        in_ref[3:4, HEAD_DIM:2 * HEAD_DIM] = fin_ref[1:2, 0:HEAD_DIM]
        in_ref[4:5, 0:H] = hg_ref[0:1, :]
        in_ref[6:9, 0:H] = cv_ref[...]

    return _pallas_call(
        body, name="small_pack", pin=False,
        in_specs=[VMEM] * 5, out_specs=VMEM, out_shape=_sds((N_SMALL_ROWS, D), F32),
    )(red_mix, red_ffn, red_final, red_hg, g_conv)


def _small_pairs(arrays, send_sems, recv_sems):
    block, gathered = arrays
    x, y, c, _, _ = _place()
    me = 4 * x + 2 * y + c
    pairs = []
    for m in range(1, 8):
        px, py, pc = x ^ ((m >> 2) & 1), y ^ ((m >> 1) & 1), c ^ (m & 1)
        sems = (send_sems.at[m - 1], recv_sems.at[m - 1])
        pairs.append((_remote(block, gathered.at[me], *sems, (px, py, pc)),
                      _remote(block, gathered.at[4 * px + 2 * py + pc], *sems, (x, y, c))))
    return pairs


def _adamw_math(w, g, m, v):
    m = ADAM_B1 * m + (1.0 - ADAM_B1) * g
    v = ADAM_B2 * v + (1.0 - ADAM_B2) * jnp.square(g)
    m_hat = m / (1.0 - ADAM_B1 ** ADAM_STEP)
    v_hat = v / (1.0 - ADAM_B2 ** ADAM_STEP)
    delta = -ADAM_LR * (m_hat / (jnp.sqrt(v_hat) + ADAM_EPS) + ADAM_WD * w)
    return delta, m, v


def _adamw(name, gs, ws, ms, vs):
    n = len(gs)
    nb = 4

    def body(*refs):
        ins, outs = refs[:4 * n], refs[4 * n:]
        for j in range(n):
            g_ref, w_ref, m_ref, v_ref = ins[j], ins[n + j], ins[2 * n + j], ins[3 * n + j]
            go_ref, d_ref, mo_ref, vo_ref = outs[4 * j:4 * j + 4]
            g = g_ref[...]
            go_ref[...] = g
            d_ref[...], mo_ref[...], vo_ref[...] = _adamw_math(w_ref[...], g, m_ref[...], v_ref[...])

    blk = [pl.BlockSpec((g.shape[0] // nb, g.shape[1]), lambda i: (i, 0)) for g in gs]
    out = _pallas_call(
        body, name=name, grid=(nb,),
        in_specs=blk * 4, out_specs=[b for b in blk for _ in range(4)],
        out_shape=[_sds(g.shape, F32) for g in gs for _ in range(4)],
        compiler_params=_params(("parallel",), 56),
    )(*gs, *ws, *ms, *vs)
    return [list(out[4 * j:4 * j + 4]) for j in range(n)]


def _small_update(block, gathered, place_idx, ws, ms, vs):
    n = len(ws)
    H = ws[1].shape[1]

    def body(idx_ref, blk_ref, all_ref, *refs):
        w, m, v, outs, tot_ref = refs[:n], refs[n:2 * n], refs[2 * n:3 * n], refs[3 * n:-1], refs[-1]
        chip, me = idx_ref[0], idx_ref[1]
        tot = jnp.where(me == 0, blk_ref[...], all_ref[0])
        for d in range(1, 8):
            tot = tot + jnp.where(me == d, blk_ref[...], all_ref[d])
        tot_ref[...] = tot
        p0 = _lower_bound(w[1][...])
        dl0 = p0 * (1.0 - p0) * tot_ref[4:5, 0:H]
        conv = jnp.zeros((3, LANES), F32)
        for k in range(N_CHIPS):
            conv = jnp.where(chip == k, tot_ref[6:9, k * LANES:(k + 1) * LANES], conv)
        grads = [tot_ref[0:1, :], None, tot_ref[3:4, 0:HEAD_DIM], conv, tot_ref[1:2, :], tot_ref[2:3, :]]
        for p in range(n):
            g_ref, d_ref, mo_ref, vo_ref = outs[4 * p:4 * p + 4]
            if p == 1:
                for row, g in ((slice(0, 1), dl0), (slice(1, 2), -dl0)):
                    g_ref[row, :] = g
                    d_ref[row, :], mo_ref[row, :], vo_ref[row, :] = _adamw_math(
                        w[p][row, :], g, m[p][row, :], v[p][row, :])
            else:
                g_ref[...] = grads[p]
                d_ref[...], mo_ref[...], vo_ref[...] = _adamw_math(w[p][...], grads[p], m[p][...], v[p][...])
        outs[4 * n][...] = tot_ref[3:4, HEAD_DIM:2 * HEAD_DIM]

    full = lambda a: pl.BlockSpec(a.shape, lambda i, idx: (0,) * a.ndim)
    out_shape = [_sds(w.shape, F32) for w in ws for _ in range(4)] + [_sds((1, LANES), F32)]
    return _pallas_call(
        body, name="small_update",
        grid_spec=pltpu.PrefetchScalarGridSpec(
            num_scalar_prefetch=1, grid=(1,),
            in_specs=[full(block), full(gathered)] + [full(a) for a in ws + ms + vs],
            out_specs=[full(s) for s in out_shape],
            scratch_shapes=[pltpu.VMEM(block.shape, F32)]),
        out_shape=out_shape,
    )(place_idx, block, gathered, *ws, *ms, *vs)


def kernel(x, norm_mix_g, w_in, lower_bounds, hg_norm_g, conv_w, w_branch_a, w_branch_b, w_out, norm_ffn_g, w_ffn_gate, w_ffn_up, w_ffn_down, norm_final_g, loss_target, m_norm_mix_g, m_w_in, m_lower_bounds, m_hg_norm_g, m_conv_w, m_w_branch_a, m_w_branch_b, m_w_out, m_norm_ffn_g, m_w_ffn_gate, m_w_ffn_up, m_w_ffn_down, m_norm_final_g, v_norm_mix_g, v_w_in, v_lower_bounds, v_hg_norm_g, v_conv_w, v_w_branch_a, v_w_branch_b, v_w_out, v_norm_ffn_g, v_w_ffn_gate, v_w_ffn_up, v_w_ffn_down, v_norm_final_g):
    _, L, D = x.shape
    H = D // 2
    assert lower_bounds.shape == (2, H) and hg_norm_g.shape == (1, HEAD_DIM)
    assert conv_w.shape == (1, 3, LANES) and w_in.shape[2] * N_CHIPS == 11 * H
    x2d, target = x.reshape(L, D), loss_target.reshape(L, D)
    g_final = norm_final_g.reshape(1, D)
    chip = 2 * lax.axis_index("x") + lax.axis_index("y")
    core = lax.axis_index("c")

    tr = lambda w: jnp.transpose(w[0])
    big = [w_in[0], w_branch_a[0], w_branch_b[0], w_out[0], tr(w_ffn_gate), tr(w_ffn_up), w_ffn_down[0]]
    big_m = [m_w_in[0], m_w_branch_a[0], m_w_branch_b[0], m_w_out[0], tr(m_w_ffn_gate), tr(m_w_ffn_up),
             m_w_ffn_down[0]]
    big_v = [v_w_in[0], v_w_branch_a[0], v_w_branch_b[0], v_w_out[0], tr(v_w_ffn_gate), tr(v_w_ffn_up),
             v_w_ffn_down[0]]
    names = ["w_in", "w_branch_a", "w_branch_b", "w_out", "w_ffn_gate", "w_ffn_up", "w_ffn_down"]

    chip_idx = chip.reshape(1).astype(jnp.int32)
    (w_in_placed,) = _cast_place_t("place_w_in", big[:1], chip_idx, chip_idx)
    conv_placed = lax.dynamic_update_slice(jnp.zeros((N_CHIPS, 3, LANES), F32), conv_w, (chip, 0, 0))
    x_i, y_i = lax.axis_index("x"), lax.axis_index("y")
    blocks = lambda *ks: jnp.stack(ks).astype(jnp.int32)
    near = lambda w, j: j < 2
    far = lambda w, j: w == 1 or j == 2
    near_sems, in_flight, _ = _gather_start("gather_start_near", [([w_in_placed], set(), near)], chip_idx)
    w_in_buf = in_flight[0][0]
    h, proj = _fwd_proj_first(x2d, norm_mix_g, w_in_buf, blocks(chip))
    (w_in_buf,) = _gather_wait("gather_wait_in_near", [w_in_buf], set(), near_sems[0], h, near)
    near_forward = _forward_pairs_from((0, 1))
    fwd_sems, (w_in_buf,), token = _split_start("gather_fwd_in_near_start", [w_in_buf], 3, near_forward)
    placed = (list(_cast_place_t("place_w_branch", big[1:3], chip_idx, token))
              + list(_cast_place("place_rest", big[3:], chip_idx, token)))
    sems, in_flight, _ = _gather_start(
        "gather_start_rest", [([w_in_buf, conv_placed], {1}, far), (placed[:3], set(), None),
                              (placed[3:], set(), None)], token)
    w_in_buf, conv_buf = in_flight[0]
    (w_in_buf,) = _split_wait("gather_fwd_in_near_wait", fwd_sems, [w_in_buf], near_forward, in_flight[2][0])
    proj = _fwd_proj_more("fwd_proj_near", h, w_in_buf, proj,
                          blocks(2 * (1 - x_i) + y_i, 2 * x_i + (1 - y_i)))
    w_in_buf, conv_all = _gather_wait("gather_wait_in_far", [w_in_buf, conv_buf], {1}, sems[0], proj, far)
    (w_int3,) = _gather_forward("gather_fwd_in_far", [w_in_buf], (2,))
    proj = _fwd_proj_more("fwd_proj_far", h, w_int3, proj, blocks(2 * (1 - x_i) + (1 - y_i)))
    w_int = w_int3.reshape(-1, D)
    conv_full = jnp.transpose(conv_all, (1, 0, 2)).reshape(3, H)
    og, o_pre, s_saved = _hgrn_fwd(proj, lower_bounds, hg_norm_g, H)
    landed = _gather_wait("gather_wait_mix", in_flight[1], set(), sems[1], og)
    fwd_sems, landed, token = _split_start("gather_fwd_mix_start", landed, 9, _forward_pairs)
    cb = _conv_fwd(proj, conv_full, H, token)
    wat3, wbt3, wout3 = _split_wait("gather_fwd_mix_wait", fwd_sems, landed, _forward_pairs, cb)
    wat, wbt, wout = wat3.reshape(D, H), wbt3.reshape(D, H), wout3.reshape(D, D)
    landed = _gather_wait("gather_wait_ffn", in_flight[2], set(), sems[2], cb)
    fwd_sems, landed, token = _split_start("gather_fwd_ffn_start", landed, 9, _forward_pairs)
    sig_a, sig_b, dm_dga, dm_dgb, merged, x1, h2 = _fwd_mix(og, cb, proj, x2d, wat, wbt, wout, norm_ffn_g,
                                                              H, token)
    wgt3, wut3, wd3 = _split_wait("gather_fwd_ffn_wait", fwd_sems, landed, _forward_pairs, h2)
    d_ff = N_CHIPS * wd3.shape[1]
    wgt, wut, wd = wgt3.reshape(d_ff, D), wut3.reshape(d_ff, D), wd3.reshape(d_ff, D)
    ffn_ds_da, ffn_ds_db, ffn_s = _fwd_ffn_up(h2, wgt, wut)
    dx2, dx2b, red_final = _fwd_down_loss(ffn_s, wd, x1, target, g_final)

    c_idx = core.reshape(1).astype(jnp.int32)
    place_idx = jnp.stack([chip, core]).astype(jnp.int32)

    def sibling_start(tag, grads):
        bufs = [lax.empty((N_CHIPS, g.shape[1] // 2, g.shape[2]), F32) for g in grads]
        return _split_start("rs_sibling_start_" + tag, list(grads) + bufs, len(grads), _sibling_pairs)

    def ici_start(tag, js, grads, from_sibling):
        partials = list(_rs_add("rs_add_" + tag, grads, from_sibling, c_idx))
        landings = [lax.empty((3,) + p.shape[1:], BF16) for p in partials]
        return _split_start("rs_ici_start_" + tag, partials + landings, 3 * len(js), _ici_pairs)

    def ici_start_behind(tag, js, started, after):
        n = len(js)
        arrays = _split_wait("rs_sibling_wait_" + tag, started[0], started[1], _sibling_pairs, after)
        return ici_start(tag, js, arrays[:n], arrays[n:])

    def sums(tag, started, after):
        partials, received = [], []
        for group, group_js, start in started:
            arrays = _split_wait("rs_ici_wait_" + group, start[0], start[1], _ici_pairs, after)
            partials += arrays[:len(group_js)]
            received += arrays[len(group_js):]
        return list(_rs_sum("rs_sum_" + tag, partials, received, place_idx))

    def adamw(tag, js, grads):
        return _adamw("adamw_" + tag, grads, *[[src[j] for j in js] for src in (big, big_m, big_v)])

    shards3 = lambda g: g.reshape(N_CHIPS, d_ff // N_CHIPS, D)
    da, db = _bwd_down(dx2b, wd, ffn_ds_da, ffn_ds_db)
    g_wd = shards3(_dw_rows2("dw_ffn_down", ffn_s, dx2b))
    g_wg = shards3(_dw_rows2("dw_ffn_gate", da, h2))
    g_wu = shards3(_dw_rows2("dw_ffn_up", db, h2))
    ffn_sibling = sibling_start("ffn", [g_wg, g_wu, g_wd])
    dx1, dx1b, red_ffn = _bwd_ffn_dh(da, db, wgt, wut, x1, dx2, norm_ffn_g, ffn_sibling[2])
    ffn_ici = ici_start_behind("ffn", [4, 5, 6], ffn_sibling, dx1b)
    dya, dyb, dproj, d_o, d_cb = _bwd_mix(dx1b, sig_a, sig_b, dm_dga, dm_dgb, wat, wbt, wout, H, ffn_ici[2])
    (g_wout,) = _dw_whole("dw_out", [(merged, dx1b)], True)
    g_wa, g_wb = _dw_whole("dw_branch", [(og, dya), (cb, dyb)], False)
    mix_sibling = sibling_start("mix", [g_wa, g_wb, g_wout])
    dproj, red_hg = _hgrn_bwd(proj, lower_bounds, hg_norm_g, o_pre, d_o, s_saved, H, mix_sibling[2], dproj)
    mix_ici = ici_start_behind("mix", [1, 2, 3], mix_sibling, red_hg)
    dproj, g_conv = _conv_bwd(proj, conv_full, d_cb, H, mix_ici[2], dproj)
    g_win, for_sibling = _dw_in(h, dproj, w_int3.shape[1], c_idx)
    in_sibling = _split_start("rs_sibling_start_in", [for_sibling, lax.empty(for_sibling.shape, BF16)], 1,
                              _sibling_whole_pairs)
    halves = sums("rest", [("mix", [1, 2, 3], mix_ici), ("ffn", [4, 5, 6], ffn_ici)], in_sibling[2])
    rest_share = _split_start("rs_share_start_rest", halves, len(halves), _share_pairs)
    from_sibling = _split_wait("rs_sibling_wait_in", in_sibling[0], in_sibling[1], _sibling_whole_pairs,
                               rest_share[2])[1]
    in_ici = ici_start("in", [0], [g_win], [from_sibling])
    grad_x, red_mix = _bwd_in(dproj, w_int, x2d, dx1, norm_mix_g, in_ici[2])
    in_share = _split_start("rs_share_start_in", sums("in", [("in", [0], in_ici)], grad_x), 1, _share_pairs)
    small_block = _small_pack(red_mix, red_ffn, red_final, red_hg, g_conv)
    small = _split_start("small_gather_start", [small_block, lax.empty((8,) + small_block.shape, F32)], 7,
                         _small_pairs)
    rest_grads = _split_wait("rs_share_wait_rest", rest_share[0], rest_share[1], _share_pairs, small[2])
    big_out = [None] + adamw("rest", [1, 2, 3, 4, 5, 6], rest_grads)
    in_grad = _split_wait("rs_share_wait_in", in_share[0], in_share[1], _share_pairs, big_out[6][0])
    big_out[0] = adamw("in", [0], in_grad)[0]
    small_block, small_all = _split_wait("small_gather_wait", small[0], small[1], _small_pairs, big_out[0][0])

    def smalls(mix, lb, hg, cw, ffn, fin):
        return [mix, lb, hg, cw[0], ffn, fin.reshape(1, D)]

    small_out = _small_update(
        small_block, small_all, jnp.stack([chip, 4 * x_i + 2 * y_i + core]).astype(jnp.int32),
        smalls(norm_mix_g, lower_bounds, hg_norm_g, conv_w, norm_ffn_g, norm_final_g),
        smalls(m_norm_mix_g, m_lower_bounds, m_hg_norm_g, m_conv_w, m_norm_ffn_g, m_norm_final_g),
        smalls(v_norm_mix_g, v_lower_bounds, v_hg_norm_g, v_conv_w, v_norm_ffn_g, v_norm_final_g))

    def outputs(i):
        big_i = [big_out[j][i] for j in range(7)]
        mix, lb, hg, cw, ffn, fin = [small_out[4 * p + i] for p in range(6)]
        return [mix, big_i[0][None], lb, hg, cw[None], big_i[1][None], big_i[2][None], big_i[3][None], ffn,
                big_i[4].T[None], big_i[5].T[None], big_i[6][None], fin.reshape(D)]

    outs = [small_out[24][0, 0], grad_x.reshape(1, L, D)]
    for i in range(4):
        outs += outputs(i)
    return tuple(outs)
```

```python
import functools

import jax
import jax.numpy as jnp
from jax import lax
from jax.experimental import pallas as pl
from jax.experimental.pallas import tpu as pltpu

F32 = jnp.float32
BF16 = jnp.bfloat16
EPS = 1e-6
CHUNK = 32
HEAD_DIM = 128
LANES = 128
N_CHIPS = 4
N_SMALL_ROWS = 16
DPROJ_BLOCKS = 12
DPROJ_BLOCK_OF = (0, 1, 2, 3, 8, 9, 10, 4, 5, 6, 7)

ADAM_LR = 0.001
ADAM_B1 = 0.9
ADAM_B2 = 0.999
ADAM_EPS = 1e-08
ADAM_WD = 0.01
ADAM_STEP = 10

MESH = pl.DeviceIdType.MESH
ANY = pl.BlockSpec(memory_space=pl.ANY)
VMEM = pl.BlockSpec(memory_space=pltpu.VMEM)
HBM = pl.BlockSpec(memory_space=pltpu.HBM)
SEM = pl.BlockSpec(memory_space=pltpu.SEMAPHORE)
EFFECT = pltpu.SideEffectType.DATAFLOW_SIDE_EFFECTING


def _sds(shape, dtype):
    return jax.ShapeDtypeStruct(shape, dtype)


def _pallas_call(body, pin=True, **kwargs):
    if not pin:
        return pl.pallas_call(body, **kwargs)
    in_hbm = lambda s: pltpu.HBM(s.shape, s.dtype) if isinstance(s, jax.ShapeDtypeStruct) else s
    kwargs["out_shape"] = jax.tree.map(in_hbm, kwargs["out_shape"])
    call = pl.pallas_call(body, **kwargs)

    def run(*args):
        return call(*[pltpu.with_memory_space_constraint(a, pltpu.HBM) if a.dtype in (F32, BF16) else a
                      for a in args])

    return run


def _params(semantics, vmem_mb):
    return pltpu.CompilerParams(dimension_semantics=semantics, vmem_limit_bytes=vmem_mb << 20)


def _nn(a, b):
    return lax.dot_general(a, b, (((1,), (0,)), ((), ())), preferred_element_type=F32)


def _nt(a, b):
    return lax.dot_general(a, b, (((1,), (1,)), ((), ())), preferred_element_type=F32)


def _tn(a, b):
    return lax.dot_general(a, b, (((0,), (0,)), ((), ())), preferred_element_type=F32)


def _sigmoid(x):
    return jax.nn.sigmoid(x)


def _rms_stats(x):
    r = lax.rsqrt(jnp.mean(x * x, axis=-1, keepdims=True) + EPS)
    return r, x * r


def _rms_bwd(dxh, xh, r):
    return r * (dxh - xh * jnp.mean(dxh * xh, axis=-1, keepdims=True))


def _fwd_proj_first(x, g_mix, w_int3, block):
    L, D = x.shape
    tn = w_int3.shape[1]
    tm = min(L, 1024)

    def body(blk_ref, x_ref, g_ref, w_ref, h_ref, p_ref):
        _, xh = _rms_stats(x_ref[...])
        h = (xh * g_ref[...]).astype(BF16)
        h_ref[...] = h
        p_ref[...] = _nt(h, w_ref[...])

    return _pallas_call(
        body, name="fwd_proj_own",
        grid_spec=pltpu.PrefetchScalarGridSpec(
            num_scalar_prefetch=1, grid=(L // tm,),
            in_specs=[pl.BlockSpec((tm, D), lambda i, blk: (i, 0)),
                      pl.BlockSpec((1, D), lambda i, blk: (0, 0)),
                      pl.BlockSpec((None, tn, D), lambda i, blk: (blk[0], 0, 0))],
            out_specs=[pl.BlockSpec((tm, D), lambda i, blk: (i, 0)),
                       pl.BlockSpec((tm, tn), lambda i, blk: (i, blk[0]))]),
        out_shape=[_sds((L, D), BF16), _sds((L, N_CHIPS * tn), F32)],
        compiler_params=_params(("parallel",), 48),
    )(block, x, g_mix, w_int3)


def _fwd_proj_more(name, h, w_int3, proj, blocks):
    L, D = h.shape
    tn = w_int3.shape[1]
    tm = min(L, 1024)

    def body(blk_ref, h_ref, w_ref, proj_ref, p_ref):
        p_ref[...] = _nt(h_ref[...], w_ref[...])

    return _pallas_call(
        body, name=name,
        grid_spec=pltpu.PrefetchScalarGridSpec(
            num_scalar_prefetch=1, grid=(L // tm, blocks.shape[0]),
            in_specs=[pl.BlockSpec((tm, D), lambda i, j, blk: (i, 0)),
                      pl.BlockSpec((None, tn, D), lambda i, j, blk: (blk[j], 0, 0)), ANY],
            out_specs=pl.BlockSpec((tm, tn), lambda i, j, blk: (i, blk[j]))),
        out_shape=_sds(proj.shape, proj.dtype),
        input_output_aliases={3: 0},
        compiler_params=_params(("parallel", "arbitrary"), 48),
    )(blocks, h, w_int3, proj)


def _lower_bound(lbp):
    l0, l1 = lbp[0:1, :], lbp[1:2, :]
    m = jnp.maximum(l0, l1)
    e0, e1 = jnp.exp(l0 - m), jnp.exp(l1 - m)
    return e0 / (e0 + e1)


def _seg_scan(x, r32, forward):
    n = x.shape[0]
    s = 1
    while s < CHUNK:
        if forward:
            x = x + jnp.where(r32 >= s, pltpu.roll(x, s, 0), 0.0)
        else:
            x = x + jnp.where(r32 < CHUNK - s, pltpu.roll(x, n - s, 0), 0.0)
        s *= 2
    return x


def _bcast_row(x, row):
    n, w = x.shape
    nc = n // CHUNK
    x3 = x.reshape(nc, CHUNK, w)
    return jnp.broadcast_to(x3[:, row:row + 1, :], (nc, CHUNK, w)).reshape(n, w)


def _chunk_total(x):
    n, w = x.shape
    nc = n // CHUNK
    total = jnp.sum(x.reshape(nc, CHUNK, w), axis=1, keepdims=True)
    return jnp.broadcast_to(total, (nc, CHUNK, w)).reshape(n, w)


def _hgrn_prep(q_raw, f_raw, lb):
    r32 = lax.broadcasted_iota(jnp.int32, f_raw.shape, 0) & (CHUNK - 1)
    sig = _sigmoid(f_raw)
    f = lb + (1.0 - lb) * sig
    b = _seg_scan(jnp.log(f), r32, True)
    a = _bcast_row(b, CHUNK // 2 - 1)
    bl = _bcast_row(b, CHUNK - 1)
    sq = _sigmoid(q_raw)
    q = q_raw * sq * (HEAD_DIM ** -0.5)
    return dict(r32=r32, sig=sig, f=f, k=1.0 - f, b=b, a=a, bl=bl, sq=sq, q=q)


def _chunk_masks(n):
    ri = lax.broadcasted_iota(jnp.int32, (n, n), 0)
    ci = lax.broadcasted_iota(jnp.int32, (n, n), 1)
    same = (ri // CHUNK) == (ci // CHUNK)
    return same & (ci <= ri), same & (ri <= ci)


def _hgrn_fwd(proj, lower_bounds, gamma, H):
    L = proj.shape[0]
    nh = H // HEAD_DIM
    TL = min(L, 512)
    nc = TL // CHUNK

    def body(q_ref, f_ref, v_ref, g_ref, lbp_ref, gam_ref, og_ref, o_ref, s_ref, st_ref):
        @pl.when(pl.program_id(0) == 0)
        def _():
            st_ref[...] = jnp.zeros_like(st_ref)

        lb = _lower_bound(lbp_ref[...])
        gam = gam_ref[...]
        mask, _ = _chunk_masks(TL)
        rowc = lax.broadcasted_iota(jnp.int32, (TL, HEAD_DIM), 0) // CHUNK
        for h in range(nh):
            hs = slice(h * HEAD_DIM, (h + 1) * HEAD_DIM)
            p = _hgrn_prep(q_ref[:, hs], f_ref[:, hs], lb[:, hs])
            v = v_ref[:, hs]
            vb = v.astype(BF16)
            vt = v.T.astype(BF16)
            q_hat = (p["q"] * jnp.exp(p["b"] - p["a"])).astype(BF16)
            k_hat = (p["k"] * jnp.exp(p["a"] - p["b"])).astype(BF16)
            q_in = (p["q"] * jnp.exp(p["b"])).astype(BF16)
            k_out = (p["k"] * jnp.exp(p["bl"] - p["b"])).astype(BF16)
            dec = jnp.exp(p["bl"])
            att = jnp.where(mask, _nt(q_hat, k_hat), 0.0).astype(BF16)
            o_intra = _nn(att, vb)
            st = st_ref[h]
            for c in range(nc):
                rs = slice(c * CHUNK, (c + 1) * CHUNK)
                stb = st.astype(BF16)
                s_ref[c, h] = stb
                o_ref[rs, hs] = o_intra[rs] + _nt(q_in[rs], stb)
                k_c = jnp.where(rowc == c, k_out, jnp.zeros_like(k_out))
                st = st * dec[c * CHUNK:c * CHUNK + 1, :] + _nn(vt, k_c)
            st_ref[h] = st
            o = o_ref[:, hs]
            _, xh = _rms_stats(o)
            gr = g_ref[:, hs]
            og_ref[:, hs] = (xh * gam * (gr * _sigmoid(gr))).astype(BF16)

    col = lambda k: pl.BlockSpec((TL, H), lambda i, k=k: (i, k))
    return _pallas_call(
        body, name="hgrn_fwd", grid=(L // TL,),
        in_specs=[col(0), col(1), col(2), col(3),
                  pl.BlockSpec(lower_bounds.shape, lambda i: (0, 0)),
                  pl.BlockSpec(gamma.shape, lambda i: (0, 0))],
        out_specs=[pl.BlockSpec((TL, H), lambda i: (i, 0)),
                   pl.BlockSpec((TL, H), lambda i: (i, 0)),
                   pl.BlockSpec((nc, nh, HEAD_DIM, HEAD_DIM), lambda i: (i, 0, 0, 0))],
        out_shape=[_sds((L, H), BF16), _sds((L, H), F32),
                   _sds((L // CHUNK, nh, HEAD_DIM, HEAD_DIM), BF16)],
        scratch_shapes=[pltpu.VMEM((nh, HEAD_DIM, HEAD_DIM), F32)],
        compiler_params=_params(("arbitrary",), 48),
    )(proj, proj, proj, proj, lower_bounds, gamma)


def _hgrn_bwd(proj, lower_bounds, gamma, o_pre, d_out, s_saved, H, after, dproj):
    L = proj.shape[0]
    nh = H // HEAD_DIM
    TL = min(L, 512)
    nc = TL // CHUNK
    nt = L // TL

    def body(q_ref, f_ref, v_ref, g_ref, lbp_ref, gam_ref, o_ref, d_ref, s_ref, after_ref, dproj_ref,
             dp_ref, red_ref, dst_ref, dsall_ref, tmp_ref):
        @pl.when(pl.program_id(0) == 0)
        def _():
            dst_ref[...] = jnp.zeros_like(dst_ref)
            red_ref[...] = jnp.zeros_like(red_ref)

        lb = _lower_bound(lbp_ref[...])
        gam = gam_ref[...]
        mask, mask_t = _chunk_masks(TL)
        rowc = lax.broadcasted_iota(jnp.int32, (TL, HEAD_DIM), 0) // CHUNK
        for h in range(nh):
            hs = slice(h * HEAD_DIM, (h + 1) * HEAD_DIM)
            qr, gr, lbh = q_ref[:, hs], g_ref[:, hs], lb[:, hs]
            p = _hgrn_prep(qr, f_ref[:, hs], lbh)
            vb = v_ref[:, hs].astype(BF16)
            eba, eab = jnp.exp(p["b"] - p["a"]), jnp.exp(p["a"] - p["b"])
            eb, elb = jnp.exp(p["b"]), jnp.exp(p["bl"] - p["b"])
            dec = jnp.exp(p["bl"])
            q_hat, k_hat = p["q"] * eba, p["k"] * eab
            q_in, k_out = p["q"] * eb, p["k"] * elb
            q_hat_b, k_hat_b = q_hat.astype(BF16), k_hat.astype(BF16)
            q_in_b, k_out_b = q_in.astype(BF16), k_out.astype(BF16)

            o, dout = o_ref[:, hs], d_ref[:, hs]
            sg = _sigmoid(gr)
            r, xh = _rms_stats(o)
            dp_ref[3, :, hs] = (dout * (xh * gam) * (sg * (1.0 + gr * (1.0 - sg)))).astype(BF16)
            dn = dout * (gr * sg)
            red_ref[1:2, hs] += jnp.sum(dn * xh, axis=0, keepdims=True)
            do = _rms_bwd(dn * gam, xh, r)
            dob = do.astype(BF16)
            dot_b = do.T.astype(BF16)

            att_t = jnp.where(mask_t, _nt(k_hat_b, q_hat_b), 0.0).astype(BF16)
            dv_intra = _nn(att_t, dob)
            datt = jnp.where(mask, _nt(dob, vb), 0.0).astype(BF16)
            dqh = _nn(datt, k_hat_b)
            datt_t = jnp.where(mask_t, _nt(vb, dob), 0.0).astype(BF16)
            dkh = _nn(datt_t, q_hat_b)

            dst = dst_ref[h]
            for c in reversed(range(nc)):
                dsall_ref[c] = dst
                q_c = jnp.where(rowc == c, q_in_b, jnp.zeros_like(q_in_b))
                dst = dst * dec[c * CHUNK:c * CHUNK + 1, :] + _nn(dot_b, q_c)
            dst_ref[h] = dst
            for c in range(nc):
                rs = slice(c * CHUNK, (c + 1) * CHUNK)
                ds_c = dsall_ref[c]
                dsb = ds_c.astype(BF16)
                st_prev = s_ref[c, h]
                tmp_ref[0, rs, :] = _nt(k_out_b[rs], dsb)
                tmp_ref[1, rs, :] = _nn(vb[rs], dsb)
                tmp_ref[2, rs, :] = _nn(dob[rs], st_prev)
                ddec = jnp.sum(ds_c * st_prev.astype(F32), axis=0, keepdims=True)
                tmp_ref[3, rs, :] = jnp.broadcast_to(ddec * dec[c * CHUNK:c * CHUNK + 1, :],
                                                     (CHUNK, HEAD_DIM))
            dko, dqi = tmp_ref[1], tmp_ref[2]
            dq = dqh * eba + dqi * eb
            dk = dkh * eab + dko * elb
            tko = dko * k_out
            db = dqh * q_hat - dkh * k_hat + dqi * q_in - tko
            dlog = _seg_scan(db, p["r32"], False) + _chunk_total(tko) + tmp_ref[3]
            df = dlog / p["f"] - dk
            sig = p["sig"]
            red_ref[0:1, hs] += jnp.sum(df * (1.0 - sig), axis=0, keepdims=True)
            dp_ref[1, :, hs] = (df * (1.0 - lbh) * sig * (1.0 - sig)).astype(BF16)
            sq = p["sq"]
            dp_ref[0, :, hs] = (dq * (HEAD_DIM ** -0.5) * (sq * (1.0 + qr * (1.0 - sq)))).astype(BF16)
            dp_ref[2, :, hs] = (dv_intra + tmp_ref[0]).astype(BF16)

    col = lambda k: pl.BlockSpec((TL, H), lambda i, k=k: (nt - 1 - i, k))
    rev = pl.BlockSpec((TL, H), lambda i: (nt - 1 - i, 0))
    return _pallas_call(
        body, name="hgrn_bwd", grid=(nt,),
        in_specs=[col(0), col(1), col(2), col(3),
                  pl.BlockSpec(lower_bounds.shape, lambda i: (0, 0)),
                  pl.BlockSpec(gamma.shape, lambda i: (0, 0)),
                  rev, rev,
                  pl.BlockSpec((nc, nh, HEAD_DIM, HEAD_DIM), lambda i: (nt - 1 - i, 0, 0, 0)), ANY, ANY],
        out_specs=[pl.BlockSpec((4, TL, H), lambda i: (0, nt - 1 - i, 0)), pl.BlockSpec((8, H), lambda i: (0, 0))],
        out_shape=[_sds(dproj.shape, BF16), _sds((8, H), F32)],
        input_output_aliases={10: 0},
        scratch_shapes=[pltpu.VMEM((nh, HEAD_DIM, HEAD_DIM), F32),
                        pltpu.VMEM((nc, HEAD_DIM, HEAD_DIM), F32),
                        pltpu.VMEM((4, TL, HEAD_DIM), F32)],
        compiler_params=_params(("arbitrary",), 48),
    )(proj, proj, proj, proj, lower_bounds, gamma, o_pre, d_out, s_saved, after, dproj)


def _shift_down(u, s, row):
    return jnp.where(row >= s, pltpu.roll(u, s, 0), 0.0)


def _shift_up(u, s, row):
    n = u.shape[0]
    return jnp.where(row < n - s, pltpu.roll(u, n - s, 0), 0.0)


def _conv_specs(L, H):
    per = H // LANES
    return [pl.BlockSpec((L, LANES), lambda j, o=o: (0, o * per + j)) for o in (4, 5, 6)]


def _conv_fwd(proj, conv_w, H, after):
    L = proj.shape[0]

    def body(c_ref, b_ref, x_ref, w_ref, after_ref, o_ref):
        row = lax.broadcasted_iota(jnp.int32, (L, LANES), 0)
        u = c_ref[...] * x_ref[...]
        w = w_ref[...]
        y = w[0:1] * _shift_down(u, 2, row) + w[1:2] * _shift_down(u, 1, row) + w[2:3] * u
        o_ref[...] = (b_ref[...] * y).astype(BF16)

    return _pallas_call(
        body, name="conv_fwd", grid=(H // LANES,),
        in_specs=_conv_specs(L, H) + [pl.BlockSpec((3, LANES), lambda j: (0, j)), ANY],
        out_specs=pl.BlockSpec((L, LANES), lambda j: (0, j)),
        out_shape=_sds((L, H), BF16),
        compiler_params=_params(("parallel",), 48),
    )(proj, proj, proj, conv_w, after)


def _conv_bwd(proj, conv_w, dcb, H, after, dproj):
    L = proj.shape[0]

    def body(c_ref, b_ref, x_ref, w_ref, d_ref, after_ref, dproj_ref, dp_ref, dw_ref):
        row = lax.broadcasted_iota(jnp.int32, (L, LANES), 0)
        cg, xb = c_ref[...], x_ref[...]
        u = cg * xb
        u1, u2 = _shift_down(u, 1, row), _shift_down(u, 2, row)
        w = w_ref[...]
        y = w[0:1] * u2 + w[1:2] * u1 + w[2:3] * u
        d = d_ref[...]
        dp_ref[1] = (d * y).astype(BF16)
        dy = d * b_ref[...]
        du = w[2:3] * dy + w[1:2] * _shift_up(dy, 1, row) + w[0:1] * _shift_up(dy, 2, row)
        dw_ref[0:1, :] = jnp.sum(dy * u2, axis=0, keepdims=True)
        dw_ref[1:2, :] = jnp.sum(dy * u1, axis=0, keepdims=True)
        dw_ref[2:3, :] = jnp.sum(dy * u, axis=0, keepdims=True)
        dp_ref[0] = (du * xb).astype(BF16)
        dp_ref[2] = (du * cg).astype(BF16)
        dp_ref[3] = jnp.zeros((L, LANES), BF16)

    blk = pl.BlockSpec((L, LANES), lambda j: (0, j))
    return _pallas_call(
        body, name="conv_bwd", grid=(H // LANES,),
        in_specs=_conv_specs(L, H) + [pl.BlockSpec((3, LANES), lambda j: (0, j)), blk, ANY, ANY],
        out_specs=[pl.BlockSpec((4, L, LANES), lambda j: (2, 0, j)), pl.BlockSpec((3, LANES), lambda j: (0, j))],
        out_shape=[_sds(dproj.shape, BF16), _sds((3, H), F32)],
        input_output_aliases={6: 0},
        compiler_params=_params(("parallel",), 56),
    )(proj, proj, proj, conv_w, dcb, after, dproj)


def _gate_specs(tm, H):
    return [pl.BlockSpec((tm, H), lambda i, k=k: (i, k)) for k in (7, 8, 9, 10)]


def _fwd_mix(og, cb, proj, x, wat, wbt, wout, g_ffn, H, after):
    L, D = x.shape
    tm = min(L, 512)

    def body(o_ref, cb_ref, ga0, ga1, gb0, gb1, x_ref, wa_ref, wb_ref, wo_ref, g_ref, after_ref,
             sa_ref, sb_ref, ta_ref, tb_ref, m_ref, x1_ref, h2_ref):
        ya, yb = _nt(o_ref[...], wa_ref[...]), _nt(cb_ref[...], wb_ref[...])
        for k, (gar, gbr) in enumerate(((ga0, gb0), (ga1, gb1))):
            cs = slice(k * H, (k + 1) * H)
            sa, sb = _sigmoid(gar[...]), _sigmoid(gbr[...])
            ma, mb = sa * ya[:, cs], sb * yb[:, cs]
            m_ref[:, cs] = (ma + mb).astype(BF16)
            sa_ref[:, cs] = sa.astype(BF16)
            sb_ref[:, cs] = sb.astype(BF16)
            ta_ref[:, cs] = (ma * (1.0 - sa)).astype(BF16)
            tb_ref[:, cs] = (mb * (1.0 - sb)).astype(BF16)
        x1 = x_ref[...] + _nn(m_ref[...], wo_ref[...])
        x1_ref[...] = x1
        _, xh = _rms_stats(x1)
        h2_ref[...] = (xh * g_ref[...]).astype(BF16)

    row = lambda w: pl.BlockSpec((tm, w), lambda i: (i, 0))
    full = lambda a: pl.BlockSpec(a.shape, lambda i: (0,) * a.ndim)
    return _pallas_call(
        body, name="fwd_mix", grid=(L // tm,),
        in_specs=[row(H), row(H)] + _gate_specs(tm, H) + [row(D), full(wat), full(wbt), full(wout),
                                                           full(g_ffn), ANY],
        out_specs=[row(D)] * 7,
        out_shape=[_sds((L, D), BF16)] * 5 + [_sds((L, D), F32), _sds((L, D), BF16)],
        compiler_params=_params(("parallel",), 56),
    )(og, cb, proj, proj, proj, proj, x, wat, wbt, wout, g_ffn, after)


def _bwd_mix(dx1b, sig_a, sig_b, dm_dga, dm_dgb, wat, wbt, wout, H, after):
    L, D = dx1b.shape
    tm = min(L, 512)

    def body(dx_ref, sa_ref, sb_ref, ta_ref, tb_ref, wa_ref, wb_ref, wo_ref, after_ref,
             dya_ref, dyb_ref, dgate_ref, do_ref, dcb_ref):
        dm = _nt(dx_ref[...], wo_ref[...])
        dga = (dm * ta_ref[...].astype(F32)).astype(BF16)
        dgb = (dm * tb_ref[...].astype(F32)).astype(BF16)
        for q, part in enumerate((dga[:, 0:H], dga[:, H:D], dgb[:, 0:H], dgb[:, H:D])):
            dgate_ref[q] = part
        dya_ref[...] = (dm * sa_ref[...].astype(F32)).astype(BF16)
        dyb_ref[...] = (dm * sb_ref[...].astype(F32)).astype(BF16)
        do_ref[...] = _nn(dya_ref[...], wa_ref[...])
        dcb_ref[...] = _nn(dyb_ref[...], wb_ref[...])

    row = lambda w: pl.BlockSpec((tm, w), lambda i: (i, 0))
    full = lambda a: pl.BlockSpec(a.shape, lambda i: (0,) * a.ndim)
    return _pallas_call(
        body, name="bwd_mix", grid=(L // tm,),
        in_specs=[row(D)] * 5 + [full(wat), full(wbt), full(wout), ANY],
        out_specs=[row(D), row(D), pl.BlockSpec((4, tm, H), lambda i: (1, i, 0)), row(H), row(H)],
        out_shape=[_sds((L, D), BF16)] * 2 + [_sds((DPROJ_BLOCKS, L, H), BF16)] + [_sds((L, H), F32)] * 2,
        compiler_params=_params(("parallel",), 56),
    )(dx1b, sig_a, sig_b, dm_dga, dm_dgb, wat, wbt, wout, after)


def _fwd_ffn_up(h2, wgt, wut):
    L, D = h2.shape
    F = wgt.shape[0]
    tn = F // 2
    tm = min(L, 512)

    def body(h_ref, wg_ref, wu_ref, sa_ref, sb_ref, s_ref):
        h = h_ref[...]
        a, b = _nt(h, wg_ref[...]), _nt(h, wu_ref[...])
        sg = _sigmoid(a)
        silu = a * sg
        sa_ref[...] = (b * sg * (1.0 + a * (1.0 - sg))).astype(BF16)
        sb_ref[...] = silu.astype(BF16)
        s_ref[...] = (silu * b).astype(BF16)

    wspec = pl.BlockSpec((tn, D), lambda j, i: (j, 0))
    ospec = pl.BlockSpec((tm, tn), lambda j, i: (i, j))
    return _pallas_call(
        body, name="fwd_ffn_up", grid=(2, L // tm),
        in_specs=[pl.BlockSpec((tm, D), lambda j, i: (i, 0)), wspec, wspec],
        out_specs=[ospec] * 3,
        out_shape=[_sds((L, F), BF16)] * 3,
        compiler_params=_params(("parallel", "parallel"), 48),
    )(h2, wgt, wut)


def _fwd_down_loss(s, wd, x1, target, g_final):
    L, D = x1.shape
    F = wd.shape[0]
    tm = min(L, 512)

    def body(s_ref, wd_ref, x1_ref, t_ref, g_ref, dx_ref, dxb_ref, red_ref):
        @pl.when(pl.program_id(0) == 0)
        def _():
            red_ref[...] = jnp.zeros_like(red_ref)

        g = g_ref[...]
        r, xh = _rms_stats(x1_ref[...] + _nn(s_ref[...], wd_ref[...]))
        e = xh * g - t_ref[...]
        dy = e * (1.0 / D)
        dx = _rms_bwd(dy * g, xh, r)
        dx_ref[...] = dx
        dxb_ref[...] = dx.astype(BF16)
        red_ref[0:1, :] += jnp.sum(dy * xh, axis=0, keepdims=True)
        red_ref[1:2, :] += jnp.broadcast_to(0.5 * jnp.sum(e * e) * (1.0 / D), (1, D))

    row = pl.BlockSpec((tm, D), lambda i: (i, 0))
    return _pallas_call(
        body, name="fwd_down_loss", grid=(L // tm,),
        in_specs=[pl.BlockSpec((tm, F), lambda i: (i, 0)), pl.BlockSpec((F, D), lambda i: (0, 0)),
                  row, row, pl.BlockSpec((1, D), lambda i: (0, 0))],
        out_specs=[row, row, pl.BlockSpec((8, D), lambda i: (0, 0))],
        out_shape=[_sds((L, D), F32), _sds((L, D), BF16), _sds((8, D), F32)],
        compiler_params=_params(("arbitrary",), 56),
    )(s, wd, x1, target, g_final)


def _bwd_down(dx2b, wd, s_a, s_b):
    L, D = dx2b.shape
    F = wd.shape[0]
    tn = F // 2
    tm = min(L, 512)

    def body(dx_ref, wd_ref, sa_ref, sb_ref, da_ref, db_ref):
        ds = _nt(dx_ref[...], wd_ref[...])
        da_ref[...] = (ds * sa_ref[...].astype(F32)).astype(BF16)
        db_ref[...] = (ds * sb_ref[...].astype(F32)).astype(BF16)

    ospec = pl.BlockSpec((tm, tn), lambda j, i: (i, j))
    return _pallas_call(
        body, name="bwd_down", grid=(2, L // tm),
        in_specs=[pl.BlockSpec((tm, D), lambda j, i: (i, 0)),
                  pl.BlockSpec((tn, D), lambda j, i: (j, 0)), ospec, ospec],
        out_specs=[ospec] * 2,
        out_shape=[_sds((L, F), BF16)] * 2,
        compiler_params=_params(("parallel", "parallel"), 48),
    )(dx2b, wd, s_a, s_b)


def _bwd_ffn_dh(da, db, wgt, wut, x1, dx2, g_ffn, after):
    L, D = x1.shape
    F = wgt.shape[0]
    tm = min(L, 256)

    def body(da_ref, db_ref, wg_ref, wu_ref, x1_ref, dx2_ref, g_ref, after_ref, dx_ref, dxb_ref, red_ref):
        @pl.when(pl.program_id(0) == 0)
        def _():
            red_ref[...] = jnp.zeros_like(red_ref)

        dh = _nn(da_ref[...], wg_ref[...]) + _nn(db_ref[...], wu_ref[...])
        r, xh = _rms_stats(x1_ref[...])
        red_ref[0:1, :] += jnp.sum(dh * xh, axis=0, keepdims=True)
        dx = dx2_ref[...] + _rms_bwd(dh * g_ref[...], xh, r)
        dx_ref[...] = dx
        dxb_ref[...] = dx.astype(BF16)

    row = pl.BlockSpec((tm, D), lambda i: (i, 0))
    aspec = pl.BlockSpec((tm, F), lambda i: (i, 0))
    wspec = pl.BlockSpec((F, D), lambda i: (0, 0))
    return _pallas_call(
        body, name="bwd_ffn_dh", grid=(L // tm,),
        in_specs=[aspec, aspec, wspec, wspec, row, row, pl.BlockSpec((1, D), lambda i: (0, 0)), ANY],
        out_specs=[row, row, pl.BlockSpec((8, D), lambda i: (0, 0))],
        out_shape=[_sds((L, D), F32), _sds((L, D), BF16), _sds((8, D), F32)],
        compiler_params=_params(("arbitrary",), 56),
    )(da, db, wgt, wut, x1, dx2, g_ffn, after)


def _bwd_in(dproj, w_int, x, dx1, g_mix, after):
    L, D = x.shape
    N = w_int.shape[0]
    H = dproj.shape[2]
    tm = min(L, 256)
    assert N == len(DPROJ_BLOCK_OF) * H

    def body(blocks_ref, w_ref, x_ref, dx1_ref, g_ref, after_ref, dx_ref, red_ref, dp_ref):
        @pl.when(pl.program_id(0) == 0)
        def _():
            red_ref[...] = jnp.zeros_like(red_ref)

        for t, block in enumerate(DPROJ_BLOCK_OF):
            dp_ref[:, t * H:(t + 1) * H] = blocks_ref[block]
        dh = _nn(dp_ref[...], w_ref[...])
        r, xh = _rms_stats(x_ref[...])
        red_ref[0:1, :] += jnp.sum(dh * xh, axis=0, keepdims=True)
        dx_ref[...] = dx1_ref[...] + _rms_bwd(dh * g_ref[...], xh, r)

    row = pl.BlockSpec((tm, D), lambda i: (i, 0))
    return _pallas_call(
        body, name="bwd_in", grid=(L // tm,),
        in_specs=[pl.BlockSpec((DPROJ_BLOCKS, tm, H), lambda i: (0, i, 0)), pl.BlockSpec((N, D), lambda i: (0, 0)),
                  row, row, pl.BlockSpec((1, D), lambda i: (0, 0)), ANY],
        out_specs=[row, pl.BlockSpec((8, D), lambda i: (0, 0))],
        out_shape=[_sds((L, D), F32), _sds((8, D), F32)],
        scratch_shapes=[pltpu.VMEM((tm, N), BF16)],
        compiler_params=_params(("arbitrary",), 56),
    )(dproj, w_int, x, dx1, g_mix, after)


def _dw_in(h, dproj, n_cols, c_idx):
    L, D = h.shape
    H = dproj.shape[2]
    tk = min(L, TK_TOKENS)
    nk = L // tk
    r2 = D // 2
    first = [(j * n_cols) // H for j in range(N_CHIPS)]
    last = [((j + 1) * n_cols - 1) // H for j in range(N_CHIPS)]
    slots = max(b - a for a, b in zip(first, last)) + 1
    plan = []
    for j in range(N_CHIPS):
        lo, hi = j * n_cols, (j + 1) * n_cols
        segments = []
        for s in range(last[j] - first[j] + 1):
            a, b = max(lo, (first[j] + s) * H), min(hi, (first[j] + s + 1) * H)
            segments.append((s, a - (first[j] + s) * H, b - a, a - lo))
        plan.append(segments)

    def body(c_ref, *refs):
        h_ref, slot_refs = refs[0], refs[1:1 + slots]
        o_ref, sib_ref, b_ref = refs[1 + slots:]
        j, k = pl.program_id(0), pl.program_id(1)
        for jj in range(N_CHIPS):
            @pl.when(j == jj)
            def _(jj=jj):
                for s, start, width, at in plan[jj]:
                    b_ref[:, at:at + width] = slot_refs[s][:, start:start + width]

        part = _tn(h_ref[...], b_ref[...])

        @pl.when(k == 0)
        def _():
            o_ref[...] = part

        @pl.when(k > 0)
        def _():
            o_ref[...] += part

        @pl.when(k == nk - 1)
        def _():
            theirs = pl.ds(pl.multiple_of((1 - c_ref[0]) * r2, 8), r2)
            sib_ref[...] = o_ref[theirs, :].astype(BF16)

    def slot_spec(s):
        blocks = [DPROJ_BLOCK_OF[min(first[j] + s, last[j])] for j in range(N_CHIPS)]

        def index(j, k, c_ref):
            block = blocks[0]
            for jj in range(1, N_CHIPS):
                block = jnp.where(j == jj, blocks[jj], block)
            return (block, k, 0)

        return pl.BlockSpec((None, tk, H), index)

    return _pallas_call(
        body, name="dw_in",
        grid_spec=pltpu.PrefetchScalarGridSpec(
            num_scalar_prefetch=1, grid=(N_CHIPS, nk),
            in_specs=[pl.BlockSpec((tk, D), lambda j, k, c_ref: (k, 0))] + [slot_spec(s) for s in range(slots)],
            out_specs=[pl.BlockSpec((None, D, n_cols), lambda j, k, c_ref: (j, 0, 0)),
                       pl.BlockSpec((None, r2, n_cols), lambda j, k, c_ref: (j, 0, 0))],
            scratch_shapes=[pltpu.VMEM((tk, n_cols), BF16)]),
        out_shape=[_sds((N_CHIPS, D, n_cols), F32), _sds((N_CHIPS, r2, n_cols), BF16)],
        compiler_params=_params(("parallel", "arbitrary"), 56),
    )(c_idx, h, *([dproj] * slots))


def _mm_tn(name, a, b, a_spec, b_spec, o_block, n_out, n_k):
    def body(a_ref, b_ref, o_ref):
        part = _tn(a_ref[...], b_ref[...])

        @pl.when(pl.program_id(1) == 0)
        def _():
            o_ref[...] = part

        @pl.when(pl.program_id(1) > 0)
        def _():
            o_ref[...] += part

    return _pallas_call(
        body, name=name, grid=(n_out, n_k),
        in_specs=[a_spec, b_spec],
        out_specs=pl.BlockSpec((None,) + o_block, lambda j, k: (j, 0, 0)),
        out_shape=_sds((n_out,) + o_block, F32),
        compiler_params=_params(("parallel", "arbitrary"), 56),
    )(a, b)


TK_TOKENS = 2048


def _dw_whole(name, pairs, by_rows):
    n = len(pairs)
    L = pairs[0][0].shape[0]
    tk = min(L, TK_TOKENS)

    def body(*refs):
        for q in range(n):
            a_ref, b_ref, o_ref = refs[2 * q], refs[2 * q + 1], refs[2 * n + q]
            part = _tn(a_ref[...], b_ref[...])
            rows, cols = o_ref.shape[1], o_ref.shape[2]
            shards = [part[j * rows:(j + 1) * rows, :] if by_rows else part[:, j * cols:(j + 1) * cols]
                      for j in range(N_CHIPS)]

            @pl.when(pl.program_id(0) == 0)
            def _(shards=shards, o_ref=o_ref):
                for j, shard in enumerate(shards):
                    o_ref[j] = shard

            @pl.when(pl.program_id(0) > 0)
            def _(shards=shards, o_ref=o_ref):
                for j, shard in enumerate(shards):
                    o_ref[j] += shard

    in_specs, out_specs, out_shape, operands = [], [], [], []
    for a, b in pairs:
        M, N = a.shape[1], b.shape[1]
        shape = (N_CHIPS, M // N_CHIPS, N) if by_rows else (N_CHIPS, M, N // N_CHIPS)
        in_specs += [pl.BlockSpec((tk, M), lambda k: (k, 0)), pl.BlockSpec((tk, N), lambda k: (k, 0))]
        out_specs.append(pl.BlockSpec(shape, lambda k: (0, 0, 0)))
        out_shape.append(_sds(shape, F32))
        operands += [a, b]
    return _pallas_call(
        body, name=name, grid=(L // tk,), in_specs=in_specs, out_specs=out_specs, out_shape=out_shape,
        compiler_params=_params(("arbitrary",), 56),
    )(*operands)


def _dw_rows2(name, a, b):
    L, M = a.shape
    N = b.shape[1]
    tk = min(L, TK_TOKENS)
    return _mm_tn(name, a, b, pl.BlockSpec((tk, M // 2), lambda j, k: (k, j)),
                  pl.BlockSpec((tk, N), lambda j, k: (k, 0)), (M // 2, N), 2, L // tk)


def _place():
    x, y, c = lax.axis_index("x"), lax.axis_index("y"), lax.axis_index("c")
    chips = [(1 - x, y), (x, 1 - y), (1 - x, 1 - y)]
    return x, y, c, 2 * x + y, chips


def _remote(src, dst, send_sem, recv_sem, device):
    return pltpu.make_async_remote_copy(src_ref=src, dst_ref=dst, send_sem=send_sem,
                                        recv_sem=recv_sem, device_id=device, device_id_type=MESH)


def _half(ref, lead, c, r2):
    return ref.at[lead, pl.ds(pl.multiple_of(c * r2, 16), r2), :]


def _cast_place(name, ws, chip_idx):
    n = len(ws)

    def body(k_ref, *refs):
        for w_ref, o_ref in zip(refs[:n], refs[n:]):
            o_ref[...] = w_ref[...].astype(BF16)

    return _pallas_call(
        body, name=name,
        grid_spec=pltpu.PrefetchScalarGridSpec(
            num_scalar_prefetch=1, grid=(2,),
            in_specs=[pl.BlockSpec((w.shape[0] // 2, w.shape[1]), lambda i, k_ref: (i, 0)) for w in ws],
            out_specs=[pl.BlockSpec((None, w.shape[0] // 2, w.shape[1]), lambda i, k_ref: (k_ref[0], i, 0))
                       for w in ws]),
        out_shape=[_sds((N_CHIPS,) + w.shape, BF16) for w in ws],
        compiler_params=_params(("parallel",), 48),
    )(chip_idx, *ws)


def _cast_place_t(name, ws, chip_idx):
    n = len(ws)
    r, cols = ws[0].shape

    def body(k_ref, *refs):
        for w_ref, o_ref in zip(refs[:n], refs[n:]):
            o_ref[...] = w_ref[...].T.astype(BF16)

    return _pallas_call(
        body, name=name,
        grid_spec=pltpu.PrefetchScalarGridSpec(
            num_scalar_prefetch=1, grid=(cols // LANES,),
            in_specs=[pl.BlockSpec((r, LANES), lambda i, k_ref: (0, i))] * n,
            out_specs=[pl.BlockSpec((None, LANES, r), lambda i, k_ref: (k_ref[0], i, 0))] * n),
        out_shape=[_sds((N_CHIPS, cols, r), BF16)] * n,
        compiler_params=_params(("parallel",), 48),
    )(chip_idx, *ws)


def _gather_copies(bufs, whole, send_sems, recv_sems, select=None):
    x, y, c, k, chips = _place()
    pairs = []
    for w, buf in enumerate(bufs):
        for j, (cx, cy) in enumerate(chips):
            if select is not None and not select(w, j):
                continue
            if w in whole:
                mine, theirs = buf.at[k], buf.at[2 * cx + cy]
            else:
                r2 = buf.shape[1] // 2
                mine, theirs = _half(buf, k, c, r2), _half(buf, 2 * cx + cy, c, r2)
            sems = (send_sems.at[w * 3 + j], recv_sems.at[w * 3 + j])
            pairs.append((_remote(mine, mine, *sems, (cx, cy, c)), _remote(theirs, theirs, *sems, (x, y, c))))
    return pairs


def _gather_start(name, groups, after):
    flat = [b for bufs, _, _ in groups for b in bufs]
    nb, ng = len(flat), len(groups)

    def body(*refs):
        ins, sems, token = refs[:nb], refs[nb + 1:nb + 1 + 2 * ng], refs[-1]
        pos = 0
        for g, (bufs, whole, select) in enumerate(groups):
            for send, _ in _gather_copies(ins[pos:pos + len(bufs)], whole, sems[2 * g], sems[2 * g + 1], select):
                send.start()
            pos += len(bufs)
        token[...] = jnp.zeros_like(token)

    sem_shapes = []
    for bufs, _, _ in groups:
        sem_shapes += [pltpu.SemaphoreType.DMA((3 * len(bufs),))] * 2
    out = _pallas_call(
        body, name=name,
        in_specs=[HBM] * nb + [ANY], out_specs=tuple([SEM] * (2 * ng) + [HBM] * nb + [VMEM]),
        out_shape=tuple(sem_shapes + [pltpu.HBM(b.shape, b.dtype) for b in flat] + [_sds((8, LANES), F32)]),
        input_output_aliases={i: 2 * ng + i for i in range(nb)},
        compiler_params=pltpu.CompilerParams(has_side_effects=EFFECT),
    )(*flat, after)
    sems, thru, pos = [], [], 2 * ng
    for g, (bufs, _, _) in enumerate(groups):
        sems.append((out[2 * g], out[2 * g + 1]))
        thru.append(list(out[pos:pos + len(bufs)]))
        pos += len(bufs)
    return sems, thru, out[-1]


def _gather_wait(name, bufs, whole, sems, after, select=None):
    nb = len(bufs)

    def body(*refs):
        ins, send_sems, recv_sems = refs[:nb], refs[nb], refs[nb + 1]
        for send, arrival in _gather_copies(ins, whole, send_sems, recv_sems, select):
            send.wait_send()
            arrival.wait_recv()

    return _pallas_call(
        body, name=name,
        in_specs=[HBM] * nb + [SEM, SEM, ANY], out_specs=[HBM] * nb,
        out_shape=[pltpu.HBM(b.shape, b.dtype) for b in bufs],
        input_output_aliases={i: i for i in range(nb)},
        compiler_params=pltpu.CompilerParams(has_side_effects=EFFECT),
    )(*bufs, sems[0], sems[1], after)


def _gather_forward(name, bufs, sources=(0, 1, 2)):
    n = len(bufs)

    def body(*refs):
        outs = refs[n:2 * n]
        send_sems, recv_sems = refs[2 * n:]
        x, y, c, _, chips = _place()
        sends = []
        for w in range(n):
            r2 = outs[w].shape[1] // 2
            for j in sources:
                landed = _half(outs[w], 2 * chips[j][0] + chips[j][1], c, r2)
                sends.append(_remote(landed, landed, send_sems.at[w * 3 + j], recv_sems.at[w * 3 + j],
                                     (x, y, 1 - c)))
        for cp in sends:
            cp.start()
        for w in range(n):
            r2 = outs[w].shape[1] // 2
            for j in sources:
                got = _half(outs[w], 2 * chips[j][0] + chips[j][1], 1 - c, r2)
                _remote(got, got, send_sems.at[w * 3 + j], recv_sems.at[w * 3 + j], (x, y, c)).wait_recv()
        for cp in sends:
            cp.wait_send()

    return _pallas_call(
        body, name=name,
        in_specs=[ANY] * n, out_specs=[ANY] * n,
        out_shape=[_sds(b.shape, b.dtype) for b in bufs],
        input_output_aliases={i: i for i in range(n)},
        scratch_shapes=[pltpu.SemaphoreType.DMA((n * 3,)), pltpu.SemaphoreType.DMA((n * 3,))],
    )(*bufs)


def _rs_add(name, grads3, from_sibling, c_idx):
    n = len(grads3)

    def body(c_ref, *refs):
        for g_ref, s_ref, o_ref in zip(refs[:n], refs[n:2 * n], refs[2 * n:]):
            o_ref[...] = (g_ref[...] + s_ref[...].astype(F32)).astype(BF16)

    mine =[pl.BlockSpec((None,) + s.shape[1:], lambda k, c_ref: (k, c_ref[0], 0)) for s in from_sibling]
    whole = [pl.BlockSpec((None,) + s.shape[1:], lambda k, c_ref: (k, 0, 0)) for s in from_sibling]
    return _pallas_call(
        body, name=name,
        grid_spec=pltpu.PrefetchScalarGridSpec(num_scalar_prefetch=1, grid=(N_CHIPS,), in_specs=mine + whole,
                                               out_specs=whole),
        out_shape=[_sds(s.shape, BF16) for s in from_sibling],
        compiler_params=_params(("parallel",), 48),
    )(c_idx, *grads3, *from_sibling)


def _split_start(name, arrays, n_sems, pairs_fn):
    n = len(arrays)

    def body(*refs):
        for send, _ in pairs_fn(refs[:n], refs[n], refs[n + 1]):
            send.start()
        refs[-1][...] = jnp.zeros_like(refs[-1])

    out = _pallas_call(
        body, name=name,
        in_specs=[HBM] * n, out_specs=tuple([SEM, SEM] + [HBM] * n + [VMEM]),
        out_shape=tuple([pltpu.SemaphoreType.DMA((n_sems,))] * 2 + [pltpu.HBM(a.shape, a.dtype) for a in arrays]
                        + [_sds((8, LANES), F32)]),
        input_output_aliases={i: 2 + i for i in range(n)},
        compiler_params=pltpu.CompilerParams(has_side_effects=EFFECT),
    )(*arrays)
    return (out[0], out[1]), list(out[2:2 + n]), out[-1]


def _split_wait(name, sems, arrays, pairs_fn, after):
    n = len(arrays)

    def body(*refs):
        for send, arrival in pairs_fn(refs[:n], refs[n], refs[n + 1]):
            send.wait_send()
            arrival.wait_recv()

    return list(_pallas_call(
        body, name=name,
        in_specs=[HBM] * n + [SEM, SEM, ANY], out_specs=[HBM] * n,
        out_shape=[pltpu.HBM(a.shape, a.dtype) for a in arrays],
        input_output_aliases={i: i for i in range(n)},
        compiler_params=pltpu.CompilerParams(has_side_effects=EFFECT),
    )(*arrays, sems[0], sems[1], after))


def _forward_pairs(bufs, send_sems, recv_sems):
    x, y, c, _, chips = _place()
    pairs = []
    for w, buf in enumerate(bufs):
        r2 = buf.shape[1] // 2
        for j, (cx, cy) in enumerate(chips):
            landed, theirs = _half(buf, 2 * cx + cy, c, r2), _half(buf, 2 * cx + cy, 1 - c, r2)
            sems = (send_sems.at[w * 3 + j], recv_sems.at[w * 3 + j])
            pairs.append((_remote(landed, landed, *sems, (x, y, 1 - c)), _remote(theirs, theirs, *sems, (x, y, c))))
    return pairs


def _sibling_pairs(arrays, send_sems, recv_sems):
    x, y, c, _, _ = _place()
    n = len(arrays) // 2
    pairs = []
    for w in range(n):
        r2 = arrays[w].shape[1] // 2
        cp = _remote(_half(arrays[w], slice(None), 1 - c, r2), arrays[n + w], send_sems.at[w], recv_sems.at[w],
                     (x, y, 1 - c))
        pairs.append((cp, cp))
    return pairs


def _sibling_whole_pairs(arrays, send_sems, recv_sems):
    x, y, c, _, _ = _place()
    n = len(arrays) // 2
    pairs = []
    for w in range(n):
        cp = _remote(arrays[w], arrays[n + w], send_sems.at[w], recv_sems.at[w], (x, y, 1 - c))
        pairs.append((cp, cp))
    return pairs


def _ici_pairs(arrays, send_sems, recv_sems):
    x, y, c, _, chips = _place()
    n = len(arrays) // 2
    pairs = []
    for w in range(n):
        for j, (cx, cy) in enumerate(chips):
            cp = _remote(arrays[w].at[2 * cx + cy], arrays[n + w].at[j],
                         send_sems.at[w * 3 + j], recv_sems.at[w * 3 + j], (cx, cy, c))
            pairs.append((cp, cp))
    return pairs


def _rs_sum(name, partials, received, place_idx):
    n = len(partials)
    nb = 2
    blocks = [(p.shape[1] // nb, p.shape[2]) for p in partials]

    def body(idx_ref, *refs):
        for p_ref, r_ref, o_ref in zip(refs[:n], refs[n:2 * n], refs[2 * n:]):
            o_ref[...] = ((p_ref[...].astype(F32) + r_ref[0].astype(F32))
                          + (r_ref[1].astype(F32) + r_ref[2].astype(F32)))

    return _pallas_call(
        body, name=name,
        grid_spec=pltpu.PrefetchScalarGridSpec(
            num_scalar_prefetch=1, grid=(nb,),
            in_specs=[pl.BlockSpec((None,) + b, lambda i, idx: (idx[0], i, 0)) for b in blocks]
            + [pl.BlockSpec((3,) + b, lambda i, idx: (0, i, 0)) for b in blocks],
            out_specs=[pl.BlockSpec(b, lambda i, idx: (idx[1] * nb + i, 0)) for b in blocks]),
        out_shape=[_sds((2 * p.shape[1], p.shape[2]), F32) for p in partials],
        compiler_params=_params(("parallel",), 48),
    )(place_idx, *partials, *received)


def _share_pairs(arrays, send_sems, recv_sems):
    x, y, c, _, _ = _place()
    pairs = []
    for w, arr in enumerate(arrays):
        r2 = arr.shape[0] // 2
        mine = arr.at[pl.ds(pl.multiple_of(c * r2, 8), r2), :]
        theirs = arr.at[pl.ds(pl.multiple_of((1 - c) * r2, 8), r2), :]
        sems = (send_sems.at[w], recv_sems.at[w])
        pairs.append((_remote(mine, mine, *sems, (x, y, 1 - c)), _remote(theirs, theirs, *sems, (x, y, c))))
    return pairs


def _small_pack(red_mix, red_ffn, red_final, red_hg, g_conv):
    D = red_mix.shape[1]
    H = red_hg.shape[1]

    def body(mix_ref, ffn_ref, fin_ref, hg_ref, cv_ref, in_ref):
        in_ref[...] = jnp.zeros_like(in_ref)
        in_ref[0:1, :] = mix_ref[0:1, :]
        in_ref[1:2, :] = ffn_ref[0:1, :]
        in_ref[2:3, :] = fin_ref[0:1, :]
        gam = hg_ref[1:2, 0:HEAD_DIM]
        for h in range(1, H // HEAD_DIM):
            gam = gam + hg_ref[1:2, h * HEAD_DIM:(h + 1) * HEAD_DIM]
        in_ref[3:4, 0:HEAD_DIM] = gam
        in_ref[3:4, HEAD_DIM:2 * HEAD_DIM] = fin_ref[1:2, 0:HEAD_DIM]
        in_ref[4:5, 0:H] = hg_ref[0:1, :]
        in_ref[6:9, 0:H] = cv_ref[...]

    return _pallas_call(
        body, name="small_pack", pin=False,
        in_specs=[VMEM] * 5, out_specs=VMEM, out_shape=_sds((N_SMALL_ROWS, D), F32),
    )(red_mix, red_ffn, red_final, red_hg, g_conv)


def _small_pairs(arrays, send_sems, recv_sems):
    block, gathered = arrays
    x, y, c, _, _ = _place()
    me = 4 * x + 2 * y + c
    pairs = []
    for m in range(1, 8):
        px, py, pc = x ^ ((m >> 2) & 1), y ^ ((m >> 1) & 1), c ^ (m & 1)
        sems = (send_sems.at[m - 1], recv_sems.at[m - 1])
        pairs.append((_remote(block, gathered.at[me], *sems, (px, py, pc)),
                      _remote(block, gathered.at[4 * px + 2 * py + pc], *sems, (x, y, c))))
    return pairs


def _adamw_math(w, g, m, v):
    m = ADAM_B1 * m + (1.0 - ADAM_B1) * g
    v = ADAM_B2 * v + (1.0 - ADAM_B2) * jnp.square(g)
    m_hat = m / (1.0 - ADAM_B1 ** ADAM_STEP)
    v_hat = v / (1.0 - ADAM_B2 ** ADAM_STEP)
    delta = -ADAM_LR * (m_hat / (jnp.sqrt(v_hat) + ADAM_EPS) + ADAM_WD * w)
    return delta, m, v


def _adamw(name, gs, ws, ms, vs):
    n = len(gs)
    nb = 4

    def body(*refs):
        ins, outs = refs[:4 * n], refs[4 * n:]
        for j in range(n):
            g_ref, w_ref, m_ref, v_ref = ins[j], ins[n + j], ins[2 * n + j], ins[3 * n + j]
            go_ref, d_ref, mo_ref, vo_ref = outs[4 * j:4 * j + 4]
            g = g_ref[...]
            go_ref[...] = g
            d_ref[...], mo_ref[...], vo_ref[...] = _adamw_math(w_ref[...], g, m_ref[...], v_ref[...])

    blk = [pl.BlockSpec((g.shape[0] // nb, g.shape[1]), lambda i: (i, 0)) for g in gs]
    out = _pallas_call(
        body, name=name, grid=(nb,),
        in_specs=blk * 4, out_specs=[b for b in blk for _ in range(4)],
        out_shape=[_sds(g.shape, F32) for g in gs for _ in range(4)],
        compiler_params=_params(("parallel",), 56),
    )(*gs, *ws, *ms, *vs)
    return [list(out[4 * j:4 * j + 4]) for j in range(n)]


def _small_update(block, gathered, place_idx, ws, ms, vs):
    n = len(ws)
    H = ws[1].shape[1]

    def body(idx_ref, blk_ref, all_ref, *refs):
        w, m, v, outs, tot_ref = refs[:n], refs[n:2 * n], refs[2 * n:3 * n], refs[3 * n:-1], refs[-1]
        chip, me = idx_ref[0], idx_ref[1]
        tot = jnp.where(me == 0, blk_ref[...], all_ref[0])
        for d in range(1, 8):
            tot = tot + jnp.where(me == d, blk_ref[...], all_ref[d])
        tot_ref[...] = tot
        p0 = _lower_bound(w[1][...])
        dl0 = p0 * (1.0 - p0) * tot_ref[4:5, 0:H]
        conv = jnp.zeros((3, LANES), F32)
        for k in range(N_CHIPS):
            conv = jnp.where(chip == k, tot_ref[6:9, k * LANES:(k + 1) * LANES], conv)
        grads = [tot_ref[0:1, :], None, tot_ref[3:4, 0:HEAD_DIM], conv, tot_ref[1:2, :], tot_ref[2:3, :]]
        for p in range(n):
            g_ref, d_ref, mo_ref, vo_ref = outs[4 * p:4 * p + 4]
            if p == 1:
                for row, g in ((slice(0, 1), dl0), (slice(1, 2), -dl0)):
                    g_ref[row, :] = g
                    d_ref[row, :], mo_ref[row, :], vo_ref[row, :] = _adamw_math(
                        w[p][row, :], g, m[p][row, :], v[p][row, :])
            else:
                g_ref[...] = grads[p]
                d_ref[...], mo_ref[...], vo_ref[...] = _adamw_math(w[p][...], grads[p], m[p][...], v[p][...])
        outs[4 * n][...] = tot_ref[3:4, HEAD_DIM:2 * HEAD_DIM]

    full = lambda a: pl.BlockSpec(a.shape, lambda i, idx: (0,) * a.ndim)
    out_shape = [_sds(w.shape, F32) for w in ws for _ in range(4)] + [_sds((1, LANES), F32)]
    return _pallas_call(
        body, name="small_update",
        grid_spec=pltpu.PrefetchScalarGridSpec(
            num_scalar_prefetch=1, grid=(1,),
            in_specs=[full(block), full(gathered)] + [full(a) for a in ws + ms + vs],
            out_specs=[full(s) for s in out_shape],
            scratch_shapes=[pltpu.VMEM(block.shape, F32)]),
        out_shape=out_shape,
    )(place_idx, block, gathered, *ws, *ms, *vs)


def kernel(x, norm_mix_g, w_in, lower_bounds, hg_norm_g, conv_w, w_branch_a, w_branch_b, w_out, norm_ffn_g, w_ffn_gate, w_ffn_up, w_ffn_down, norm_final_g, loss_target, m_norm_mix_g, m_w_in, m_lower_bounds, m_hg_norm_g, m_conv_w, m_w_branch_a, m_w_branch_b, m_w_out, m_norm_ffn_g, m_w_ffn_gate, m_w_ffn_up, m_w_ffn_down, m_norm_final_g, v_norm_mix_g, v_w_in, v_lower_bounds, v_hg_norm_g, v_conv_w, v_w_branch_a, v_w_branch_b, v_w_out, v_norm_ffn_g, v_w_ffn_gate, v_w_ffn_up, v_w_ffn_down, v_norm_final_g):
    _, L, D = x.shape
    H = D // 2
    assert lower_bounds.shape == (2, H) and hg_norm_g.shape == (1, HEAD_DIM)
    assert conv_w.shape == (1, 3, LANES) and w_in.shape[2] * N_CHIPS == 11 * H
    x2d, target = x.reshape(L, D), loss_target.reshape(L, D)
    g_final = norm_final_g.reshape(1, D)
    chip = 2 * lax.axis_index("x") + lax.axis_index("y")
    core = lax.axis_index("c")

    tr = lambda w: jnp.transpose(w[0])
    big = [w_in[0], w_branch_a[0], w_branch_b[0], w_out[0], tr(w_ffn_gate), tr(w_ffn_up), w_ffn_down[0]]
    big_m = [m_w_in[0], m_w_branch_a[0], m_w_branch_b[0], m_w_out[0], tr(m_w_ffn_gate), tr(m_w_ffn_up),
             m_w_ffn_down[0]]
    big_v = [v_w_in[0], v_w_branch_a[0], v_w_branch_b[0], v_w_out[0], tr(v_w_ffn_gate), tr(v_w_ffn_up),
             v_w_ffn_down[0]]
    names = ["w_in", "w_branch_a", "w_branch_b", "w_out", "w_ffn_gate", "w_ffn_up", "w_ffn_down"]

    chip_idx = chip.reshape(1).astype(jnp.int32)
    def per_shape(fn, tag, js, *lists):
        groups = {}
        for pos, a in enumerate(lists[0]):
            groups.setdefault(a.shape, []).append(pos)
        results = [None] * len(js)
        for same in groups.values():
            out = fn(tag + names[js[same[0]]], *[[xs[p] for p in same] for xs in lists])
            for q, p in enumerate(same):
                results[p] = out[q]
        return results

    place_t = lambda name, ws: _cast_place_t(name, ws, chip_idx)
    placed = per_shape(place_t, "place_", [0, 1, 2], big[:3]) + list(_cast_place("place_rest", big[3:], chip_idx))
    conv_placed = lax.dynamic_update_slice(jnp.zeros((N_CHIPS, 3, LANES), F32), conv_w, (chip, 0, 0))
    x_i, y_i = lax.axis_index("x"), lax.axis_index("y")
    blocks = lambda *ks: jnp.stack(ks).astype(jnp.int32)
    near = lambda w, j: j < 2
    far = lambda w, j: w == 1 or j == 2
    near_sems, in_flight, _ = _gather_start("gather_start_near", [([placed[0]], set(), near)], chip_idx)
    w_in_buf = in_flight[0][0]
    h, proj = _fwd_proj_first(x2d, norm_mix_g, w_in_buf, blocks(chip))
    sems, in_flight, _ = _gather_start(
        "gather_start_rest", [([w_in_buf, conv_placed], {1}, far), (placed[1:4], set(), None),
                              (placed[4:], set(), None)], h)
    w_in_buf, conv_buf = in_flight[0]
    (w_in_buf,) = _gather_wait("gather_wait_in_near", [w_in_buf], set(), near_sems[0], h, near)
    (w_in_buf,) = _gather_forward("gather_fwd_in_near", [w_in_buf], (0, 1))
    proj = _fwd_proj_more("fwd_proj_near", h, w_in_buf, proj,
                          blocks(2 * (1 - x_i) + y_i, 2 * x_i + (1 - y_i)))
    w_in_buf, conv_all = _gather_wait("gather_wait_in_far", [w_in_buf, conv_buf], {1}, sems[0], proj, far)
    (w_int3,) = _gather_forward("gather_fwd_in_far", [w_in_buf], (2,))
    proj = _fwd_proj_more("fwd_proj_far", h, w_int3, proj, blocks(2 * (1 - x_i) + (1 - y_i)))
    w_int = w_int3.reshape(-1, D)
    conv_full = jnp.transpose(conv_all, (1, 0, 2)).reshape(3, H)
    og, o_pre, s_saved = _hgrn_fwd(proj, lower_bounds, hg_norm_g, H)
    landed = _gather_wait("gather_wait_mix", in_flight[1], set(), sems[1], og)
    fwd_sems, landed, token = _split_start("gather_fwd_mix_start", landed, 9, _forward_pairs)
    cb = _conv_fwd(proj, conv_full, H, token)
    wat3, wbt3, wout3 = _split_wait("gather_fwd_mix_wait", fwd_sems, landed, _forward_pairs, cb)
    wat, wbt, wout = wat3.reshape(D, H), wbt3.reshape(D, H), wout3.reshape(D, D)
    landed = _gather_wait("gather_wait_ffn", in_flight[2], set(), sems[2], cb)
    fwd_sems, landed, token = _split_start("gather_fwd_ffn_start", landed, 9, _forward_pairs)
    sig_a, sig_b, dm_dga, dm_dgb, merged, x1, h2 = _fwd_mix(og, cb, proj, x2d, wat, wbt, wout, norm_ffn_g,
                                                              H, token)
    wgt3, wut3, wd3 = _split_wait("gather_fwd_ffn_wait", fwd_sems, landed, _forward_pairs, h2)
    d_ff = N_CHIPS * wd3.shape[1]
    wgt, wut, wd = wgt3.reshape(d_ff, D), wut3.reshape(d_ff, D), wd3.reshape(d_ff, D)
    ffn_ds_da, ffn_ds_db, ffn_s = _fwd_ffn_up(h2, wgt, wut)
    dx2, dx2b, red_final = _fwd_down_loss(ffn_s, wd, x1, target, g_final)

    c_idx = core.reshape(1).astype(jnp.int32)
    place_idx = jnp.stack([chip, core]).astype(jnp.int32)

    def sibling_start(tag, grads):
        bufs = [lax.empty((N_CHIPS, g.shape[1] // 2, g.shape[2]), F32) for g in grads]
        return _split_start("rs_sibling_start_" + tag, list(grads) + bufs, len(grads), _sibling_pairs)

    def ici_start(tag, js, grads, from_sibling):
        partials = list(_rs_add("rs_add_" + tag, grads, from_sibling, c_idx))
        landings = [lax.empty((3,) + p.shape[1:], BF16) for p in partials]
        return _split_start("rs_ici_start_" + tag, partials + landings, 3 * len(js), _ici_pairs)

    def ici_start_behind(tag, js, started, after):
        n = len(js)
        arrays = _split_wait("rs_sibling_wait_" + tag, started[0], started[1], _sibling_pairs, after)
        return ici_start(tag, js, arrays[:n], arrays[n:])

    def sums(tag, started, after):
        partials, received = [], []
        for group, group_js, start in started:
            arrays = _split_wait("rs_ici_wait_" + group, start[0], start[1], _ici_pairs, after)
            partials += arrays[:len(group_js)]
            received += arrays[len(group_js):]
        return list(_rs_sum("rs_sum_" + tag, partials, received, place_idx))

    def adamw(tag, js, grads):
        return _adamw("adamw_" + tag, grads, *[[src[j] for j in js] for src in (big, big_m, big_v)])

    shards3 = lambda g: g.reshape(N_CHIPS, d_ff // N_CHIPS, D)
    da, db = _bwd_down(dx2b, wd, ffn_ds_da, ffn_ds_db)
    g_wd = shards3(_dw_rows2("dw_ffn_down", ffn_s, dx2b))
    g_wg = shards3(_dw_rows2("dw_ffn_gate", da, h2))
    g_wu = shards3(_dw_rows2("dw_ffn_up", db, h2))
    ffn_sibling = sibling_start("ffn", [g_wg, g_wu, g_wd])
    dx1, dx1b, red_ffn = _bwd_ffn_dh(da, db, wgt, wut, x1, dx2, norm_ffn_g, ffn_sibling[2])
    ffn_ici = ici_start_behind("ffn", [4, 5, 6], ffn_sibling, dx1b)
    dya, dyb, dproj, d_o, d_cb = _bwd_mix(dx1b, sig_a, sig_b, dm_dga, dm_dgb, wat, wbt, wout, H, ffn_ici[2])
    (g_wout,) = _dw_whole("dw_out", [(merged, dx1b)], True)
    g_wa, g_wb = _dw_whole("dw_branch", [(og, dya), (cb, dyb)], False)
    mix_sibling = sibling_start("mix", [g_wa, g_wb, g_wout])
    dproj, red_hg = _hgrn_bwd(proj, lower_bounds, hg_norm_g, o_pre, d_o, s_saved, H, mix_sibling[2], dproj)
    mix_ici = ici_start_behind("mix", [1, 2, 3], mix_sibling, red_hg)
    dproj, g_conv = _conv_bwd(proj, conv_full, d_cb, H, mix_ici[2], dproj)
    g_win, for_sibling = _dw_in(h, dproj, w_int3.shape[1], c_idx)
    in_sibling = _split_start("rs_sibling_start_in", [for_sibling, lax.empty(for_sibling.shape, BF16)], 1,
                              _sibling_whole_pairs)
    halves = sums("rest", [("mix", [1, 2, 3], mix_ici), ("ffn", [4, 5, 6], ffn_ici)], in_sibling[2])
    rest_share = _split_start("rs_share_start_rest", halves, len(halves), _share_pairs)
    from_sibling = _split_wait("rs_sibling_wait_in", in_sibling[0], in_sibling[1], _sibling_whole_pairs,
                               rest_share[2])[1]
    in_ici = ici_start("in", [0], [g_win], [from_sibling])
    grad_x, red_mix = _bwd_in(dproj, w_int, x2d, dx1, norm_mix_g, in_ici[2])
    in_share = _split_start("rs_share_start_in", sums("in", [("in", [0], in_ici)], grad_x), 1, _share_pairs)
    small_block = _small_pack(red_mix, red_ffn, red_final, red_hg, g_conv)
    small = _split_start("small_gather_start", [small_block, lax.empty((8,) + small_block.shape, F32)], 7,
                         _small_pairs)
    rest_grads = _split_wait("rs_share_wait_rest", rest_share[0], rest_share[1], _share_pairs, small[2])
    big_out = [None] + adamw("rest", [1, 2, 3, 4, 5, 6], rest_grads)
    in_grad = _split_wait("rs_share_wait_in", in_share[0], in_share[1], _share_pairs, big_out[6][0])
    big_out[0] = adamw("in", [0], in_grad)[0]
    small_block, small_all = _split_wait("small_gather_wait", small[0], small[1], _small_pairs, big_out[0][0])

    def smalls(mix, lb, hg, cw, ffn, fin):
        return [mix, lb, hg, cw[0], ffn, fin.reshape(1, D)]

    small_out = _small_update(
        small_block, small_all, jnp.stack([chip, 4 * x_i + 2 * y_i + core]).astype(jnp.int32),
        smalls(norm_mix_g, lower_bounds, hg_norm_g, conv_w, norm_ffn_g, norm_final_g),
        smalls(m_norm_mix_g, m_lower_bounds, m_hg_norm_g, m_conv_w, m_norm_ffn_g, m_norm_final_g),
        smalls(v_norm_mix_g, v_lower_bounds, v_hg_norm_g, v_conv_w, v_norm_ffn_g, v_norm_final_g))

    def outputs(i):
        big_i = [big_out[j][i] for j in range(7)]
        mix, lb, hg, cw, ffn, fin = [small_out[4 * p + i] for p in range(6)]
        return [mix, big_i[0][None], lb, hg, cw[None], big_i[1][None], big_i[2][None], big_i[3][None], ffn,
                big_i[4].T[None], big_i[5].T[None], big_i[6][None], fin.reshape(D)]

    outs = [small_out[24][0, 0], grad_x.reshape(1, L, D)]
    for i in range(4):
        outs += outputs(i)
    return tuple(outs)
```

```python
import functools

import jax
import jax.numpy as jnp
from jax import lax
from jax.experimental import pallas as pl
from jax.experimental.pallas import tpu as pltpu

F32 = jnp.float32
BF16 = jnp.bfloat16
EPS = 1e-6
CHUNK = 32
HEAD_DIM = 128
LANES = 128
N_CHIPS = 4
N_SMALL_ROWS = 16
DPROJ_BLOCKS = 12
DPROJ_BLOCK_OF = (0, 1, 2, 3, 8, 9, 10, 4, 5, 6, 7)

ADAM_LR = 0.001
ADAM_B1 = 0.9
ADAM_B2 = 0.999
ADAM_EPS = 1e-08
ADAM_WD = 0.01
ADAM_STEP = 10

MESH = pl.DeviceIdType.MESH
ANY = pl.BlockSpec(memory_space=pl.ANY)
VMEM = pl.BlockSpec(memory_space=pltpu.VMEM)
HBM = pl.BlockSpec(memory_space=pltpu.HBM)
SEM = pl.BlockSpec(memory_space=pltpu.SEMAPHORE)
EFFECT = pltpu.SideEffectType.DATAFLOW_SIDE_EFFECTING


def _sds(shape, dtype):
    return jax.ShapeDtypeStruct(shape, dtype)


def _pallas_call(body, pin=True, **kwargs):
    if not pin:
        return pl.pallas_call(body, **kwargs)
    in_hbm = lambda s: pltpu.HBM(s.shape, s.dtype) if isinstance(s, jax.ShapeDtypeStruct) else s
    kwargs["out_shape"] = jax.tree.map(in_hbm, kwargs["out_shape"])
    call = pl.pallas_call(body, **kwargs)

    def run(*args):
        return call(*[pltpu.with_memory_space_constraint(a, pltpu.HBM) if a.dtype in (F32, BF16) else a
                      for a in args])

    return run


def _params(semantics, vmem_mb):
    return pltpu.CompilerParams(dimension_semantics=semantics, vmem_limit_bytes=vmem_mb << 20)


def _nn(a, b):
    return lax.dot_general(a, b, (((1,), (0,)), ((), ())), preferred_element_type=F32)


def _nt(a, b):
    return lax.dot_general(a, b, (((1,), (1,)), ((), ())), preferred_element_type=F32)


def _tn(a, b):
    return lax.dot_general(a, b, (((0,), (0,)), ((), ())), preferred_element_type=F32)


def _sigmoid(x):
    return jax.nn.sigmoid(x)


def _rms_stats(x):
    r = lax.rsqrt(jnp.mean(x * x, axis=-1, keepdims=True) + EPS)
    return r, x * r


def _rms_bwd(dxh, xh, r):
    return r * (dxh - xh * jnp.mean(dxh * xh, axis=-1, keepdims=True))


def _fwd_proj_first(x, g_mix, w_int3, block):
    L, D = x.shape
    tn = w_int3.shape[1]
    tm = min(L, 1024)

    def body(blk_ref, x_ref, g_ref, w_ref, h_ref, p_ref):
        _, xh = _rms_stats(x_ref[...])
        h = (xh * g_ref[...]).astype(BF16)
        h_ref[...] = h
        p_ref[...] = _nt(h, w_ref[...])

    return _pallas_call(
        body, name="fwd_proj_own",
        grid_spec=pltpu.PrefetchScalarGridSpec(
            num_scalar_prefetch=1, grid=(L // tm,),
            in_specs=[pl.BlockSpec((tm, D), lambda i, blk: (i, 0)),
                      pl.BlockSpec((1, D), lambda i, blk: (0, 0)),
                      pl.BlockSpec((None, tn, D), lambda i, blk: (blk[0], 0, 0))],
            out_specs=[pl.BlockSpec((tm, D), lambda i, blk: (i, 0)),
                       pl.BlockSpec((tm, tn), lambda i, blk: (i, blk[0]))]),
        out_shape=[_sds((L, D), BF16), _sds((L, N_CHIPS * tn), F32)],
        compiler_params=_params(("parallel",), 48),
    )(block, x, g_mix, w_int3)


def _fwd_proj_more(name, h, w_int3, proj, blocks):
    L, D = h.shape
    tn = w_int3.shape[1]
    tm = min(L, 1024)

    def body(blk_ref, h_ref, w_ref, proj_ref, p_ref):
        p_ref[...] = _nt(h_ref[...], w_ref[...])

    return _pallas_call(
        body, name=name,
        grid_spec=pltpu.PrefetchScalarGridSpec(
            num_scalar_prefetch=1, grid=(L // tm, blocks.shape[0]),
            in_specs=[pl.BlockSpec((tm, D), lambda i, j, blk: (i, 0)),
                      pl.BlockSpec((None, tn, D), lambda i, j, blk: (blk[j], 0, 0)), ANY],
            out_specs=pl.BlockSpec((tm, tn), lambda i, j, blk: (i, blk[j]))),
        out_shape=_sds(proj.shape, proj.dtype),
        input_output_aliases={3: 0},
        compiler_params=_params(("parallel", "arbitrary"), 48),
    )(blocks, h, w_int3, proj)


def _lower_bound(lbp):
    l0, l1 = lbp[0:1, :], lbp[1:2, :]
    m = jnp.maximum(l0, l1)
    e0, e1 = jnp.exp(l0 - m), jnp.exp(l1 - m)
    return e0 / (e0 + e1)


def _seg_scan(x, r32, forward):
    n = x.shape[0]
    s = 1
    while s < CHUNK:
        if forward:
            x = x + jnp.where(r32 >= s, pltpu.roll(x, s, 0), 0.0)
        else:
            x = x + jnp.where(r32 < CHUNK - s, pltpu.roll(x, n - s, 0), 0.0)
        s *= 2
    return x


def _bcast_row(x, row):
    n, w = x.shape
    nc = n // CHUNK
    x3 = x.reshape(nc, CHUNK, w)
    return jnp.broadcast_to(x3[:, row:row + 1, :], (nc, CHUNK, w)).reshape(n, w)


def _chunk_total(x):
    n, w = x.shape
    nc = n // CHUNK
    total = jnp.sum(x.reshape(nc, CHUNK, w), axis=1, keepdims=True)
    return jnp.broadcast_to(total, (nc, CHUNK, w)).reshape(n, w)


def _hgrn_prep(q_raw, f_raw, lb):
    r32 = lax.broadcasted_iota(jnp.int32, f_raw.shape, 0) & (CHUNK - 1)
    sig = _sigmoid(f_raw)
    f = lb + (1.0 - lb) * sig
    b = _seg_scan(jnp.log(f), r32, True)
    a = _bcast_row(b, CHUNK // 2 - 1)
    bl = _bcast_row(b, CHUNK - 1)
    sq = _sigmoid(q_raw)
    q = q_raw * sq * (HEAD_DIM ** -0.5)
    return dict(r32=r32, sig=sig, f=f, k=1.0 - f, b=b, a=a, bl=bl, sq=sq, q=q)


def _chunk_masks(n):
    ri = lax.broadcasted_iota(jnp.int32, (n, n), 0)
    ci = lax.broadcasted_iota(jnp.int32, (n, n), 1)
    same = (ri // CHUNK) == (ci // CHUNK)
    return same & (ci <= ri), same & (ri <= ci)


def _hgrn_fwd(proj, lower_bounds, gamma, H):
    L = proj.shape[0]
    nh = H // HEAD_DIM
    TL = min(L, 128)
    nc = TL // CHUNK

    def body(q_ref, f_ref, v_ref, g_ref, lbp_ref, gam_ref, og_ref, o_ref, s_ref, st_ref):
        @pl.when(pl.program_id(0) == 0)
        def _():
            st_ref[...] = jnp.zeros_like(st_ref)

        lb = _lower_bound(lbp_ref[...])
        gam = gam_ref[...]
        mask, _ = _chunk_masks(TL)
        rowc = lax.broadcasted_iota(jnp.int32, (TL, HEAD_DIM), 0) // CHUNK
        for h in range(nh):
            hs = slice(h * HEAD_DIM, (h + 1) * HEAD_DIM)
            p = _hgrn_prep(q_ref[:, hs], f_ref[:, hs], lb[:, hs])
            v = v_ref[:, hs]
            vb = v.astype(BF16)
            vt = v.T.astype(BF16)
            q_hat = (p["q"] * jnp.exp(p["b"] - p["a"])).astype(BF16)
            k_hat = (p["k"] * jnp.exp(p["a"] - p["b"])).astype(BF16)
            q_in = (p["q"] * jnp.exp(p["b"])).astype(BF16)
            k_out = (p["k"] * jnp.exp(p["bl"] - p["b"])).astype(BF16)
            dec = jnp.exp(p["bl"])
            att = jnp.where(mask, _nt(q_hat, k_hat), 0.0).astype(BF16)
            o_intra = _nn(att, vb)
            st = st_ref[h]
            for c in range(nc):
                rs = slice(c * CHUNK, (c + 1) * CHUNK)
                stb = st.astype(BF16)
                s_ref[c, h] = stb
                o_ref[rs, hs] = o_intra[rs] + _nt(q_in[rs], stb)
                k_c = jnp.where(rowc == c, k_out, jnp.zeros_like(k_out))
                st = st * dec[c * CHUNK:c * CHUNK + 1, :] + _nn(vt, k_c)
            st_ref[h] = st
            o = o_ref[:, hs]
            _, xh = _rms_stats(o)
            gr = g_ref[:, hs]
            og_ref[:, hs] = (xh * gam * (gr * _sigmoid(gr))).astype(BF16)

    col = lambda k: pl.BlockSpec((TL, H), lambda i, k=k: (i, k))
    return _pallas_call(
        body, name="hgrn_fwd", grid=(L // TL,),
        in_specs=[col(0), col(1), col(2), col(3),
                  pl.BlockSpec(lower_bounds.shape, lambda i: (0, 0)),
                  pl.BlockSpec(gamma.shape, lambda i: (0, 0))],
        out_specs=[pl.BlockSpec((TL, H), lambda i: (i, 0)),
                   pl.BlockSpec((TL, H), lambda i: (i, 0)),
                   pl.BlockSpec((nc, nh, HEAD_DIM, HEAD_DIM), lambda i: (i, 0, 0, 0))],
        out_shape=[_sds((L, H), BF16), _sds((L, H), F32),
                   _sds((L // CHUNK, nh, HEAD_DIM, HEAD_DIM), BF16)],
        scratch_shapes=[pltpu.VMEM((nh, HEAD_DIM, HEAD_DIM), F32)],
        compiler_params=_params(("arbitrary",), 48),
    )(proj, proj, proj, proj, lower_bounds, gamma)


def _hgrn_bwd(proj, lower_bounds, gamma, o_pre, d_out, s_saved, H, after, dproj):
    L = proj.shape[0]
    nh = H // HEAD_DIM
    TL = min(L, 128)
    nc = TL // CHUNK
    nt = L // TL

    def body(q_ref, f_ref, v_ref, g_ref, lbp_ref, gam_ref, o_ref, d_ref, s_ref, after_ref, dproj_ref,
             dp_ref, red_ref, dst_ref, dsall_ref, tmp_ref):
        @pl.when(pl.program_id(0) == 0)
        def _():
            dst_ref[...] = jnp.zeros_like(dst_ref)
            red_ref[...] = jnp.zeros_like(red_ref)

        lb = _lower_bound(lbp_ref[...])
        gam = gam_ref[...]
        mask, mask_t = _chunk_masks(TL)
        rowc = lax.broadcasted_iota(jnp.int32, (TL, HEAD_DIM), 0) // CHUNK
        for h in range(nh):
            hs = slice(h * HEAD_DIM, (h + 1) * HEAD_DIM)
            qr, gr, lbh = q_ref[:, hs], g_ref[:, hs], lb[:, hs]
            p = _hgrn_prep(qr, f_ref[:, hs], lbh)
            vb = v_ref[:, hs].astype(BF16)
            eba, eab = jnp.exp(p["b"] - p["a"]), jnp.exp(p["a"] - p["b"])
            eb, elb = jnp.exp(p["b"]), jnp.exp(p["bl"] - p["b"])
            dec = jnp.exp(p["bl"])
            q_hat, k_hat = p["q"] * eba, p["k"] * eab
            q_in, k_out = p["q"] * eb, p["k"] * elb
            q_hat_b, k_hat_b = q_hat.astype(BF16), k_hat.astype(BF16)
            q_in_b, k_out_b = q_in.astype(BF16), k_out.astype(BF16)

            o, dout = o_ref[:, hs], d_ref[:, hs]
            sg = _sigmoid(gr)
            r, xh = _rms_stats(o)
            dp_ref[3, :, hs] = (dout * (xh * gam) * (sg * (1.0 + gr * (1.0 - sg)))).astype(BF16)
            dn = dout * (gr * sg)
            red_ref[1:2, hs] += jnp.sum(dn * xh, axis=0, keepdims=True)
            do = _rms_bwd(dn * gam, xh, r)
            dob = do.astype(BF16)
            dot_b = do.T.astype(BF16)

            att_t = jnp.where(mask_t, _nt(k_hat_b, q_hat_b), 0.0).astype(BF16)
            dv_intra = _nn(att_t, dob)
            datt = jnp.where(mask, _nt(dob, vb), 0.0).astype(BF16)
            dqh = _nn(datt, k_hat_b)
            datt_t = jnp.where(mask_t, _nt(vb, dob), 0.0).astype(BF16)
            dkh = _nn(datt_t, q_hat_b)

            dst = dst_ref[h]
            for c in reversed(range(nc)):
                dsall_ref[c] = dst
                q_c = jnp.where(rowc == c, q_in_b, jnp.zeros_like(q_in_b))
                dst = dst * dec[c * CHUNK:c * CHUNK + 1, :] + _nn(dot_b, q_c)
            dst_ref[h] = dst
            for c in range(nc):
                rs = slice(c * CHUNK, (c + 1) * CHUNK)
                ds_c = dsall_ref[c]
                dsb = ds_c.astype(BF16)
                st_prev = s_ref[c, h]
                tmp_ref[0, rs, :] = _nt(k_out_b[rs], dsb)
                tmp_ref[1, rs, :] = _nn(vb[rs], dsb)
                tmp_ref[2, rs, :] = _nn(dob[rs], st_prev)
                ddec = jnp.sum(ds_c * st_prev.astype(F32), axis=0, keepdims=True)
                tmp_ref[3, rs, :] = jnp.broadcast_to(ddec * dec[c * CHUNK:c * CHUNK + 1, :],
                                                     (CHUNK, HEAD_DIM))
            dko, dqi = tmp_ref[1], tmp_ref[2]
            dq = dqh * eba + dqi * eb
            dk = dkh * eab + dko * elb
            tko = dko * k_out
            db = dqh * q_hat - dkh * k_hat + dqi * q_in - tko
            dlog = _seg_scan(db, p["r32"], False) + _chunk_total(tko) + tmp_ref[3]
            df = dlog / p["f"] - dk
            sig = p["sig"]
            red_ref[0:1, hs] += jnp.sum(df * (1.0 - sig), axis=0, keepdims=True)
            dp_ref[1, :, hs] = (df * (1.0 - lbh) * sig * (1.0 - sig)).astype(BF16)
            sq = p["sq"]
            dp_ref[0, :, hs] = (dq * (HEAD_DIM ** -0.5) * (sq * (1.0 + qr * (1.0 - sq)))).astype(BF16)
            dp_ref[2, :, hs] = (dv_intra + tmp_ref[0]).astype(BF16)

    col = lambda k: pl.BlockSpec((TL, H), lambda i, k=k: (nt - 1 - i, k))
    rev = pl.BlockSpec((TL, H), lambda i: (nt - 1 - i, 0))
    return _pallas_call(
        body, name="hgrn_bwd", grid=(nt,),
        in_specs=[col(0), col(1), col(2), col(3),
                  pl.BlockSpec(lower_bounds.shape, lambda i: (0, 0)),
                  pl.BlockSpec(gamma.shape, lambda i: (0, 0)),
                  rev, rev,
                  pl.BlockSpec((nc, nh, HEAD_DIM, HEAD_DIM), lambda i: (nt - 1 - i, 0, 0, 0)), ANY, ANY],
        out_specs=[pl.BlockSpec((4, TL, H), lambda i: (0, nt - 1 - i, 0)), pl.BlockSpec((8, H), lambda i: (0, 0))],
        out_shape=[_sds(dproj.shape, BF16), _sds((8, H), F32)],
        input_output_aliases={10: 0},
        scratch_shapes=[pltpu.VMEM((nh, HEAD_DIM, HEAD_DIM), F32),
                        pltpu.VMEM((nc, HEAD_DIM, HEAD_DIM), F32),
                        pltpu.VMEM((4, TL, HEAD_DIM), F32)],
        compiler_params=_params(("arbitrary",), 48),
    )(proj, proj, proj, proj, lower_bounds, gamma, o_pre, d_out, s_saved, after, dproj)


def _shift_down(u, s, row):
    return jnp.where(row >= s, pltpu.roll(u, s, 0), 0.0)


def _shift_up(u, s, row):
    n = u.shape[0]
    return jnp.where(row < n - s, pltpu.roll(u, n - s, 0), 0.0)


def _conv_specs(L, H):
    per = H // LANES
    return [pl.BlockSpec((L, LANES), lambda j, o=o: (0, o * per + j)) for o in (4, 5, 6)]


def _conv_fwd(proj, conv_w, H, after):
    L = proj.shape[0]

    def body(c_ref, b_ref, x_ref, w_ref, after_ref, o_ref):
        row = lax.broadcasted_iota(jnp.int32, (L, LANES), 0)
        u = c_ref[...] * x_ref[...]
        w = w_ref[...]
        y = w[0:1] * _shift_down(u, 2, row) + w[1:2] * _shift_down(u, 1, row) + w[2:3] * u
        o_ref[...] = (b_ref[...] * y).astype(BF16)

    return _pallas_call(
        body, name="conv_fwd", grid=(H // LANES,),
        in_specs=_conv_specs(L, H) + [pl.BlockSpec((3, LANES), lambda j: (0, j)), ANY],
        out_specs=pl.BlockSpec((L, LANES), lambda j: (0, j)),
        out_shape=_sds((L, H), BF16),
        compiler_params=_params(("parallel",), 48),
    )(proj, proj, proj, conv_w, after)


def _conv_bwd(proj, conv_w, dcb, H, after, dproj):
    L = proj.shape[0]

    def body(c_ref, b_ref, x_ref, w_ref, d_ref, after_ref, dproj_ref, dp_ref, dw_ref):
        row = lax.broadcasted_iota(jnp.int32, (L, LANES), 0)
        cg, xb = c_ref[...], x_ref[...]
        u = cg * xb
        u1, u2 = _shift_down(u, 1, row), _shift_down(u, 2, row)
        w = w_ref[...]
        y = w[0:1] * u2 + w[1:2] * u1 + w[2:3] * u
        d = d_ref[...]
        dp_ref[1] = (d * y).astype(BF16)
        dy = d * b_ref[...]
        du = w[2:3] * dy + w[1:2] * _shift_up(dy, 1, row) + w[0:1] * _shift_up(dy, 2, row)
        dw_ref[0:1, :] = jnp.sum(dy * u2, axis=0, keepdims=True)
        dw_ref[1:2, :] = jnp.sum(dy * u1, axis=0, keepdims=True)
        dw_ref[2:3, :] = jnp.sum(dy * u, axis=0, keepdims=True)
        dp_ref[0] = (du * xb).astype(BF16)
        dp_ref[2] = (du * cg).astype(BF16)
        dp_ref[3] = jnp.zeros((L, LANES), BF16)

    blk = pl.BlockSpec((L, LANES), lambda j: (0, j))
    return _pallas_call(
        body, name="conv_bwd", grid=(H // LANES,),
        in_specs=_conv_specs(L, H) + [pl.BlockSpec((3, LANES), lambda j: (0, j)), blk, ANY, ANY],
        out_specs=[pl.BlockSpec((4, L, LANES), lambda j: (2, 0, j)), pl.BlockSpec((3, LANES), lambda j: (0, j))],
        out_shape=[_sds(dproj.shape, BF16), _sds((3, H), F32)],
        input_output_aliases={6: 0},
        compiler_params=_params(("parallel",), 56),
    )(proj, proj, proj, conv_w, dcb, after, dproj)


def _gate_specs(tm, H):
    return [pl.BlockSpec((tm, H), lambda i, k=k: (i, k)) for k in (7, 8, 9, 10)]


def _fwd_mix(og, cb, proj, x, wat, wbt, wout, g_ffn, H, after):
    L, D = x.shape
    tm = min(L, 512)

    def body(o_ref, cb_ref, ga0, ga1, gb0, gb1, x_ref, wa_ref, wb_ref, wo_ref, g_ref, after_ref,
             sa_ref, sb_ref, ta_ref, tb_ref, m_ref, x1_ref, h2_ref):
        ya, yb = _nt(o_ref[...], wa_ref[...]), _nt(cb_ref[...], wb_ref[...])
        for k, (gar, gbr) in enumerate(((ga0, gb0), (ga1, gb1))):
            cs = slice(k * H, (k + 1) * H)
            sa, sb = _sigmoid(gar[...]), _sigmoid(gbr[...])
            ma, mb = sa * ya[:, cs], sb * yb[:, cs]
            m_ref[:, cs] = (ma + mb).astype(BF16)
            sa_ref[:, cs] = sa.astype(BF16)
            sb_ref[:, cs] = sb.astype(BF16)
            ta_ref[:, cs] = (ma * (1.0 - sa)).astype(BF16)
            tb_ref[:, cs] = (mb * (1.0 - sb)).astype(BF16)
        x1 = x_ref[...] + _nn(m_ref[...], wo_ref[...])
        x1_ref[...] = x1
        _, xh = _rms_stats(x1)
        h2_ref[...] = (xh * g_ref[...]).astype(BF16)

    row = lambda w: pl.BlockSpec((tm, w), lambda i: (i, 0))
    full = lambda a: pl.BlockSpec(a.shape, lambda i: (0,) * a.ndim)
    return _pallas_call(
        body, name="fwd_mix", grid=(L // tm,),
        in_specs=[row(H), row(H)] + _gate_specs(tm, H) + [row(D), full(wat), full(wbt), full(wout),
                                                           full(g_ffn), ANY],
        out_specs=[row(D)] * 7,
        out_shape=[_sds((L, D), BF16)] * 5 + [_sds((L, D), F32), _sds((L, D), BF16)],
        compiler_params=_params(("parallel",), 56),
    )(og, cb, proj, proj, proj, proj, x, wat, wbt, wout, g_ffn, after)


def _bwd_mix(dx1b, sig_a, sig_b, dm_dga, dm_dgb, wat, wbt, wout, H, after):
    L, D = dx1b.shape
    tm = min(L, 512)

    def body(dx_ref, sa_ref, sb_ref, ta_ref, tb_ref, wa_ref, wb_ref, wo_ref, after_ref,
             dya_ref, dyb_ref, dgate_ref, do_ref, dcb_ref):
        dm = _nt(dx_ref[...], wo_ref[...])
        dga = (dm * ta_ref[...].astype(F32)).astype(BF16)
        dgb = (dm * tb_ref[...].astype(F32)).astype(BF16)
        for q, part in enumerate((dga[:, 0:H], dga[:, H:D], dgb[:, 0:H], dgb[:, H:D])):
            dgate_ref[q] = part
        dya_ref[...] = (dm * sa_ref[...].astype(F32)).astype(BF16)
        dyb_ref[...] = (dm * sb_ref[...].astype(F32)).astype(BF16)
        do_ref[...] = _nn(dya_ref[...], wa_ref[...])
        dcb_ref[...] = _nn(dyb_ref[...], wb_ref[...])

    row = lambda w: pl.BlockSpec((tm, w), lambda i: (i, 0))
    full = lambda a: pl.BlockSpec(a.shape, lambda i: (0,) * a.ndim)
    return _pallas_call(
        body, name="bwd_mix", grid=(L // tm,),
        in_specs=[row(D)] * 5 + [full(wat), full(wbt), full(wout), ANY],
        out_specs=[row(D), row(D), pl.BlockSpec((4, tm, H), lambda i: (1, i, 0)), row(H), row(H)],
        out_shape=[_sds((L, D), BF16)] * 2 + [_sds((DPROJ_BLOCKS, L, H), BF16)] + [_sds((L, H), F32)] * 2,
        compiler_params=_params(("parallel",), 56),
    )(dx1b, sig_a, sig_b, dm_dga, dm_dgb, wat, wbt, wout, after)


def _fwd_ffn_up(h2, wgt, wut):
    L, D = h2.shape
    F = wgt.shape[0]
    tn = F // 2
    tm = min(L, 512)

    def body(h_ref, wg_ref, wu_ref, sa_ref, sb_ref, s_ref):
        h = h_ref[...]
        a, b = _nt(h, wg_ref[...]), _nt(h, wu_ref[...])
        sg = _sigmoid(a)
        silu = a * sg
        sa_ref[...] = (b * sg * (1.0 + a * (1.0 - sg))).astype(BF16)
        sb_ref[...] = silu.astype(BF16)
        s_ref[...] = (silu * b).astype(BF16)

    wspec = pl.BlockSpec((tn, D), lambda j, i: (j, 0))
    ospec = pl.BlockSpec((tm, tn), lambda j, i: (i, j))
    return _pallas_call(
        body, name="fwd_ffn_up", grid=(2, L // tm),
        in_specs=[pl.BlockSpec((tm, D), lambda j, i: (i, 0)), wspec, wspec],
        out_specs=[ospec] * 3,
        out_shape=[_sds((L, F), BF16)] * 3,
        compiler_params=_params(("parallel", "parallel"), 48),
    )(h2, wgt, wut)


def _fwd_down_loss(s, wd, x1, target, g_final):
    L, D = x1.shape
    F = wd.shape[0]
    tm = min(L, 512)

    def body(s_ref, wd_ref, x1_ref, t_ref, g_ref, dx_ref, dxb_ref, red_ref):
        @pl.when(pl.program_id(0) == 0)
        def _():
            red_ref[...] = jnp.zeros_like(red_ref)

        g = g_ref[...]
        r, xh = _rms_stats(x1_ref[...] + _nn(s_ref[...], wd_ref[...]))
        e = xh * g - t_ref[...]
        dy = e * (1.0 / D)
        dx = _rms_bwd(dy * g, xh, r)
        dx_ref[...] = dx
        dxb_ref[...] = dx.astype(BF16)
        red_ref[0:1, :] += jnp.sum(dy * xh, axis=0, keepdims=True)
        red_ref[1:2, :] += jnp.broadcast_to(0.5 * jnp.sum(e * e) * (1.0 / D), (1, D))

    row = pl.BlockSpec((tm, D), lambda i: (i, 0))
    return _pallas_call(
        body, name="fwd_down_loss", grid=(L // tm,),
        in_specs=[pl.BlockSpec((tm, F), lambda i: (i, 0)), pl.BlockSpec((F, D), lambda i: (0, 0)),
                  row, row, pl.BlockSpec((1, D), lambda i: (0, 0))],
        out_specs=[row, row, pl.BlockSpec((8, D), lambda i: (0, 0))],
        out_shape=[_sds((L, D), F32), _sds((L, D), BF16), _sds((8, D), F32)],
        compiler_params=_params(("arbitrary",), 56),
    )(s, wd, x1, target, g_final)


def _bwd_down(dx2b, wd, s_a, s_b):
    L, D = dx2b.shape
    F = wd.shape[0]
    tn = F // 2
    tm = min(L, 512)

    def body(dx_ref, wd_ref, sa_ref, sb_ref, da_ref, db_ref):
        ds = _nt(dx_ref[...], wd_ref[...])
        da_ref[...] = (ds * sa_ref[...].astype(F32)).astype(BF16)
        db_ref[...] = (ds * sb_ref[...].astype(F32)).astype(BF16)

    ospec = pl.BlockSpec((tm, tn), lambda j, i: (i, j))
    return _pallas_call(
        body, name="bwd_down", grid=(2, L // tm),
        in_specs=[pl.BlockSpec((tm, D), lambda j, i: (i, 0)),
                  pl.BlockSpec((tn, D), lambda j, i: (j, 0)), ospec, ospec],
        out_specs=[ospec] * 2,
        out_shape=[_sds((L, F), BF16)] * 2,
        compiler_params=_params(("parallel", "parallel"), 48),
    )(dx2b, wd, s_a, s_b)


def _bwd_ffn_dh(da, db, wgt, wut, x1, dx2, g_ffn, after):
    L, D = x1.shape
    F = wgt.shape[0]
    tm = min(L, 256)

    def body(da_ref, db_ref, wg_ref, wu_ref, x1_ref, dx2_ref, g_ref, after_ref, dx_ref, dxb_ref, red_ref):
        @pl.when(pl.program_id(0) == 0)
        def _():
            red_ref[...] = jnp.zeros_like(red_ref)

        dh = _nn(da_ref[...], wg_ref[...]) + _nn(db_ref[...], wu_ref[...])
        r, xh = _rms_stats(x1_ref[...])
        red_ref[0:1, :] += jnp.sum(dh * xh, axis=0, keepdims=True)
        dx = dx2_ref[...] + _rms_bwd(dh * g_ref[...], xh, r)
        dx_ref[...] = dx
        dxb_ref[...] = dx.astype(BF16)

    row = pl.BlockSpec((tm, D), lambda i: (i, 0))
    aspec = pl.BlockSpec((tm, F), lambda i: (i, 0))
    wspec = pl.BlockSpec((F, D), lambda i: (0, 0))
    return _pallas_call(
        body, name="bwd_ffn_dh", grid=(L // tm,),
        in_specs=[aspec, aspec, wspec, wspec, row, row, pl.BlockSpec((1, D), lambda i: (0, 0)), ANY],
        out_specs=[row, row, pl.BlockSpec((8, D), lambda i: (0, 0))],
        out_shape=[_sds((L, D), F32), _sds((L, D), BF16), _sds((8, D), F32)],
        compiler_params=_params(("arbitrary",), 56),
    )(da, db, wgt, wut, x1, dx2, g_ffn, after)


def _bwd_in(dproj, w_int, x, dx1, g_mix, after):
    L, D = x.shape
    N = w_int.shape[0]
    H = dproj.shape[2]
    tm = min(L, 256)
    assert N == len(DPROJ_BLOCK_OF) * H

    def body(blocks_ref, w_ref, x_ref, dx1_ref, g_ref, after_ref, dx_ref, red_ref, dp_ref):
        @pl.when(pl.program_id(0) == 0)
        def _():
            red_ref[...] = jnp.zeros_like(red_ref)

        for t, block in enumerate(DPROJ_BLOCK_OF):
            dp_ref[:, t * H:(t + 1) * H] = blocks_ref[block]
        dh = _nn(dp_ref[...], w_ref[...])
        r, xh = _rms_stats(x_ref[...])
        red_ref[0:1, :] += jnp.sum(dh * xh, axis=0, keepdims=True)
        dx_ref[...] = dx1_ref[...] + _rms_bwd(dh * g_ref[...], xh, r)

    row = pl.BlockSpec((tm, D), lambda i: (i, 0))
    return _pallas_call(
        body, name="bwd_in", grid=(L // tm,),
        in_specs=[pl.BlockSpec((DPROJ_BLOCKS, tm, H), lambda i: (0, i, 0)), pl.BlockSpec((N, D), lambda i: (0, 0)),
                  row, row, pl.BlockSpec((1, D), lambda i: (0, 0)), ANY],
        out_specs=[row, pl.BlockSpec((8, D), lambda i: (0, 0))],
        out_shape=[_sds((L, D), F32), _sds((8, D), F32)],
        scratch_shapes=[pltpu.VMEM((tm, N), BF16)],
        compiler_params=_params(("arbitrary",), 56),
    )(dproj, w_int, x, dx1, g_mix, after)


def _dw_in(h, dproj, n_cols, c_idx):
    L, D = h.shape
    H = dproj.shape[2]
    tk = min(L, TK_TOKENS)
    nk = L // tk
    r2 = D // 2
    first = [(j * n_cols) // H for j in range(N_CHIPS)]
    last = [((j + 1) * n_cols - 1) // H for j in range(N_CHIPS)]
    slots = max(b - a for a, b in zip(first, last)) + 1
    plan = []
    for j in range(N_CHIPS):
        lo, hi = j * n_cols, (j + 1) * n_cols
        segments = []
        for s in range(last[j] - first[j] + 1):
            a, b = max(lo, (first[j] + s) * H), min(hi, (first[j] + s + 1) * H)
            segments.append((s, a - (first[j] + s) * H, b - a, a - lo))
        plan.append(segments)

    def body(c_ref, *refs):
        h_ref, slot_refs = refs[0], refs[1:1 + slots]
        o_ref, sib_ref, b_ref = refs[1 + slots:]
        j, k = pl.program_id(0), pl.program_id(1)
        for jj in range(N_CHIPS):
            @pl.when(j == jj)
            def _(jj=jj):
                for s, start, width, at in plan[jj]:
                    b_ref[:, at:at + width] = slot_refs[s][:, start:start + width]

        part = _tn(h_ref[...], b_ref[...])

        @pl.when(k == 0)
        def _():
            o_ref[...] = part

        @pl.when(k > 0)
        def _():
            o_ref[...] += part

        @pl.when(k == nk - 1)
        def _():
            theirs = pl.ds(pl.multiple_of((1 - c_ref[0]) * r2, 8), r2)
            sib_ref[...] = o_ref[theirs, :].astype(BF16)

    def slot_spec(s):
        blocks = [DPROJ_BLOCK_OF[min(first[j] + s, last[j])] for j in range(N_CHIPS)]

        def index(j, k, c_ref):
            block = blocks[0]
            for jj in range(1, N_CHIPS):
                block = jnp.where(j == jj, blocks[jj], block)
            return (block, k, 0)

        return pl.BlockSpec((None, tk, H), index)

    return _pallas_call(
        body, name="dw_in",
        grid_spec=pltpu.PrefetchScalarGridSpec(
            num_scalar_prefetch=1, grid=(N_CHIPS, nk),
            in_specs=[pl.BlockSpec((tk, D), lambda j, k, c_ref: (k, 0))] + [slot_spec(s) for s in range(slots)],
            out_specs=[pl.BlockSpec((None, D, n_cols), lambda j, k, c_ref: (j, 0, 0)),
                       pl.BlockSpec((None, r2, n_cols), lambda j, k, c_ref: (j, 0, 0))],
            scratch_shapes=[pltpu.VMEM((tk, n_cols), BF16)]),
        out_shape=[_sds((N_CHIPS, D, n_cols), F32), _sds((N_CHIPS, r2, n_cols), BF16)],
        compiler_params=_params(("parallel", "arbitrary"), 56),
    )(c_idx, h, *([dproj] * slots))


def _mm_tn(name, a, b, a_spec, b_spec, o_block, n_out, n_k):
    def body(a_ref, b_ref, o_ref):
        part = _tn(a_ref[...], b_ref[...])

        @pl.when(pl.program_id(1) == 0)
        def _():
            o_ref[...] = part

        @pl.when(pl.program_id(1) > 0)
        def _():
            o_ref[...] += part

    return _pallas_call(
        body, name=name, grid=(n_out, n_k),
        in_specs=[a_spec, b_spec],
        out_specs=pl.BlockSpec((None,) + o_block, lambda j, k: (j, 0, 0)),
        out_shape=_sds((n_out,) + o_block, F32),
        compiler_params=_params(("parallel", "arbitrary"), 56),
    )(a, b)


TK_TOKENS = 2048


def _dw_whole(name, pairs, by_rows):
    n = len(pairs)
    L = pairs[0][0].shape[0]
    tk = min(L, TK_TOKENS)

    def body(*refs):
        for q in range(n):
            a_ref, b_ref, o_ref = refs[2 * q], refs[2 * q + 1], refs[2 * n + q]
            part = _tn(a_ref[...], b_ref[...])
            rows, cols = o_ref.shape[1], o_ref.shape[2]
            shards = [part[j * rows:(j + 1) * rows, :] if by_rows else part[:, j * cols:(j + 1) * cols]
                      for j in range(N_CHIPS)]

            @pl.when(pl.program_id(0) == 0)
            def _(shards=shards, o_ref=o_ref):
                for j, shard in enumerate(shards):
                    o_ref[j] = shard

            @pl.when(pl.program_id(0) > 0)
            def _(shards=shards, o_ref=o_ref):
                for j, shard in enumerate(shards):
                    o_ref[j] += shard

    in_specs, out_specs, out_shape, operands = [], [], [], []
    for a, b in pairs:
        M, N = a.shape[1], b.shape[1]
        shape = (N_CHIPS, M // N_CHIPS, N) if by_rows else (N_CHIPS, M, N // N_CHIPS)
        in_specs += [pl.BlockSpec((tk, M), lambda k: (k, 0)), pl.BlockSpec((tk, N), lambda k: (k, 0))]
        out_specs.append(pl.BlockSpec(shape, lambda k: (0, 0, 0)))
        out_shape.append(_sds(shape, F32))
        operands += [a, b]
    return _pallas_call(
        body, name=name, grid=(L // tk,), in_specs=in_specs, out_specs=out_specs, out_shape=out_shape,
        compiler_params=_params(("arbitrary",), 56),
    )(*operands)


def _dw_rows2(name, a, b):
    L, M = a.shape
    N = b.shape[1]
    tk = min(L, TK_TOKENS)
    return _mm_tn(name, a, b, pl.BlockSpec((tk, M // 2), lambda j, k: (k, j)),
                  pl.BlockSpec((tk, N), lambda j, k: (k, 0)), (M // 2, N), 2, L // tk)


def _place():
    x, y, c = lax.axis_index("x"), lax.axis_index("y"), lax.axis_index("c")
    chips = [(1 - x, y), (x, 1 - y), (1 - x, 1 - y)]
    return x, y, c, 2 * x + y, chips


def _remote(src, dst, send_sem, recv_sem, device):
    return pltpu.make_async_remote_copy(src_ref=src, dst_ref=dst, send_sem=send_sem,
                                        recv_sem=recv_sem, device_id=device, device_id_type=MESH)


def _half(ref, lead, c, r2):
    return ref.at[lead, pl.ds(pl.multiple_of(c * r2, 16), r2), :]


def _cast_place(name, ws, chip_idx):
    n = len(ws)

    def body(k_ref, *refs):
        for w_ref, o_ref in zip(refs[:n], refs[n:]):
            o_ref[...] = w_ref[...].astype(BF16)

    return _pallas_call(
        body, name=name,
        grid_spec=pltpu.PrefetchScalarGridSpec(
            num_scalar_prefetch=1, grid=(2,),
            in_specs=[pl.BlockSpec((w.shape[0] // 2, w.shape[1]), lambda i, k_ref: (i, 0)) for w in ws],
            out_specs=[pl.BlockSpec((None, w.shape[0] // 2, w.shape[1]), lambda i, k_ref: (k_ref[0], i, 0))
                       for w in ws]),
        out_shape=[_sds((N_CHIPS,) + w.shape, BF16) for w in ws],
        compiler_params=_params(("parallel",), 48),
    )(chip_idx, *ws)


def _cast_place_t(name, ws, chip_idx):
    n = len(ws)
    r, cols = ws[0].shape

    def body(k_ref, *refs):
        for w_ref, o_ref in zip(refs[:n], refs[n:]):
            o_ref[...] = w_ref[...].T.astype(BF16)

    return _pallas_call(
        body, name=name,
        grid_spec=pltpu.PrefetchScalarGridSpec(
            num_scalar_prefetch=1, grid=(cols // LANES,),
            in_specs=[pl.BlockSpec((r, LANES), lambda i, k_ref: (0, i))] * n,
            out_specs=[pl.BlockSpec((None, LANES, r), lambda i, k_ref: (k_ref[0], i, 0))] * n),
        out_shape=[_sds((N_CHIPS, cols, r), BF16)] * n,
        compiler_params=_params(("parallel",), 48),
    )(chip_idx, *ws)


def _gather_copies(bufs, whole, send_sems, recv_sems, select=None):
    x, y, c, k, chips = _place()
    pairs = []
    for w, buf in enumerate(bufs):
        for j, (cx, cy) in enumerate(chips):
            if select is not None and not select(w, j):
                continue
            if w in whole:
                mine, theirs = buf.at[k], buf.at[2 * cx + cy]
            else:
                r2 = buf.shape[1] // 2
                mine, theirs = _half(buf, k, c, r2), _half(buf, 2 * cx + cy, c, r2)
            sems = (send_sems.at[w * 3 + j], recv_sems.at[w * 3 + j])
            pairs.append((_remote(mine, mine, *sems, (cx, cy, c)), _remote(theirs, theirs, *sems, (x, y, c))))
    return pairs


def _gather_start(name, groups, after):
    flat = [b for bufs, _, _ in groups for b in bufs]
    nb, ng = len(flat), len(groups)

    def body(*refs):
        ins, sems, token = refs[:nb], refs[nb + 1:nb + 1 + 2 * ng], refs[-1]
        pos = 0
        for g, (bufs, whole, select) in enumerate(groups):
            for send, _ in _gather_copies(ins[pos:pos + len(bufs)], whole, sems[2 * g], sems[2 * g + 1], select):
                send.start()
            pos += len(bufs)
        token[...] = jnp.zeros_like(token)

    sem_shapes = []
    for bufs, _, _ in groups:
        sem_shapes += [pltpu.SemaphoreType.DMA((3 * len(bufs),))] * 2
    out = _pallas_call(
        body, name=name,
        in_specs=[HBM] * nb + [ANY], out_specs=tuple([SEM] * (2 * ng) + [HBM] * nb + [VMEM]),
        out_shape=tuple(sem_shapes + [pltpu.HBM(b.shape, b.dtype) for b in flat] + [_sds((8, LANES), F32)]),
        input_output_aliases={i: 2 * ng + i for i in range(nb)},
        compiler_params=pltpu.CompilerParams(has_side_effects=EFFECT),
    )(*flat, after)
    sems, thru, pos = [], [], 2 * ng
    for g, (bufs, _, _) in enumerate(groups):
        sems.append((out[2 * g], out[2 * g + 1]))
        thru.append(list(out[pos:pos + len(bufs)]))
        pos += len(bufs)
    return sems, thru, out[-1]


def _gather_wait(name, bufs, whole, sems, after, select=None):
    nb = len(bufs)

    def body(*refs):
        ins, send_sems, recv_sems = refs[:nb], refs[nb], refs[nb + 1]
        for send, arrival in _gather_copies(ins, whole, send_sems, recv_sems, select):
            send.wait_send()
            arrival.wait_recv()

    return _pallas_call(
        body, name=name,
        in_specs=[HBM] * nb + [SEM, SEM, ANY], out_specs=[HBM] * nb,
        out_shape=[pltpu.HBM(b.shape, b.dtype) for b in bufs],
        input_output_aliases={i: i for i in range(nb)},
        compiler_params=pltpu.CompilerParams(has_side_effects=EFFECT),
    )(*bufs, sems[0], sems[1], after)


def _gather_forward(name, bufs, sources=(0, 1, 2)):
    n = len(bufs)

    def body(*refs):
        outs = refs[n:2 * n]
        send_sems, recv_sems = refs[2 * n:]
        x, y, c, _, chips = _place()
        sends = []
        for w in range(n):
            r2 = outs[w].shape[1] // 2
            for j in sources:
                landed = _half(outs[w], 2 * chips[j][0] + chips[j][1], c, r2)
                sends.append(_remote(landed, landed, send_sems.at[w * 3 + j], recv_sems.at[w * 3 + j],
                                     (x, y, 1 - c)))
        for cp in sends:
            cp.start()
        for w in range(n):
            r2 = outs[w].shape[1] // 2
            for j in sources:
                got = _half(outs[w], 2 * chips[j][0] + chips[j][1], 1 - c, r2)
                _remote(got, got, send_sems.at[w * 3 + j], recv_sems.at[w * 3 + j], (x, y, c)).wait_recv()
        for cp in sends:
            cp.wait_send()

    return _pallas_call(
        body, name=name,
        in_specs=[ANY] * n, out_specs=[ANY] * n,
        out_shape=[_sds(b.shape, b.dtype) for b in bufs],
        input_output_aliases={i: i for i in range(n)},
        scratch_shapes=[pltpu.SemaphoreType.DMA((n * 3,)), pltpu.SemaphoreType.DMA((n * 3,))],
    )(*bufs)


def _rs_add(name, grads3, from_sibling, c_idx):
    n = len(grads3)

    def body(c_ref, *refs):
        for g_ref, s_ref, o_ref in zip(refs[:n], refs[n:2 * n], refs[2 * n:]):
            o_ref[...] = (g_ref[...] + s_ref[...].astype(F32)).astype(BF16)

    mine =[pl.BlockSpec((None,) + s.shape[1:], lambda k, c_ref: (k, c_ref[0], 0)) for s in from_sibling]
    whole = [pl.BlockSpec((None,) + s.shape[1:], lambda k, c_ref: (k, 0, 0)) for s in from_sibling]
    return _pallas_call(
        body, name=name,
        grid_spec=pltpu.PrefetchScalarGridSpec(num_scalar_prefetch=1, grid=(N_CHIPS,), in_specs=mine + whole,
                                               out_specs=whole),
        out_shape=[_sds(s.shape, BF16) for s in from_sibling],
        compiler_params=_params(("parallel",), 48),
    )(c_idx, *grads3, *from_sibling)


def _split_start(name, arrays, n_sems, pairs_fn):
    n = len(arrays)

    def body(*refs):
        for send, _ in pairs_fn(refs[:n], refs[n], refs[n + 1]):
            send.start()
        refs[-1][...] = jnp.zeros_like(refs[-1])

    out = _pallas_call(
        body, name=name,
        in_specs=[HBM] * n, out_specs=tuple([SEM, SEM] + [HBM] * n + [VMEM]),
        out_shape=tuple([pltpu.SemaphoreType.DMA((n_sems,))] * 2 + [pltpu.HBM(a.shape, a.dtype) for a in arrays]
                        + [_sds((8, LANES), F32)]),
        input_output_aliases={i: 2 + i for i in range(n)},
        compiler_params=pltpu.CompilerParams(has_side_effects=EFFECT),
    )(*arrays)
    return (out[0], out[1]), list(out[2:2 + n]), out[-1]


def _split_wait(name, sems, arrays, pairs_fn, after):
    n = len(arrays)

    def body(*refs):
        for send, arrival in pairs_fn(refs[:n], refs[n], refs[n + 1]):
            send.wait_send()
            arrival.wait_recv()

    return list(_pallas_call(
        body, name=name,
        in_specs=[HBM] * n + [SEM, SEM, ANY], out_specs=[HBM] * n,
        out_shape=[pltpu.HBM(a.shape, a.dtype) for a in arrays],
        input_output_aliases={i: i for i in range(n)},
        compiler_params=pltpu.CompilerParams(has_side_effects=EFFECT),
    )(*arrays, sems[0], sems[1], after))


def _forward_pairs(bufs, send_sems, recv_sems):
    x, y, c, _, chips = _place()
    pairs = []
    for w, buf in enumerate(bufs):
        r2 = buf.shape[1] // 2
        for j, (cx, cy) in enumerate(chips):
            landed, theirs = _half(buf, 2 * cx + cy, c, r2), _half(buf, 2 * cx + cy, 1 - c, r2)
            sems = (send_sems.at[w * 3 + j], recv_sems.at[w * 3 + j])
            pairs.append((_remote(landed, landed, *sems, (x, y, 1 - c)), _remote(theirs, theirs, *sems, (x, y, c))))
    return pairs


def _sibling_pairs(arrays, send_sems, recv_sems):
    x, y, c, _, _ = _place()
    n = len(arrays) // 2
    pairs = []
    for w in range(n):
        r2 = arrays[w].shape[1] // 2
        cp = _remote(_half(arrays[w], slice(None), 1 - c, r2), arrays[n + w], send_sems.at[w], recv_sems.at[w],
                     (x, y, 1 - c))
        pairs.append((cp, cp))
    return pairs


def _sibling_whole_pairs(arrays, send_sems, recv_sems):
    x, y, c, _, _ = _place()
    n = len(arrays) // 2
    pairs = []
    for w in range(n):
        cp = _remote(arrays[w], arrays[n + w], send_sems.at[w], recv_sems.at[w], (x, y, 1 - c))
        pairs.append((cp, cp))
    return pairs


def _ici_pairs(arrays, send_sems, recv_sems):
    x, y, c, _, chips = _place()
    n = len(arrays) // 2
    pairs = []
    for w in range(n):
        for j, (cx, cy) in enumerate(chips):
            cp = _remote(arrays[w].at[2 * cx + cy], arrays[n + w].at[j],
                         send_sems.at[w * 3 + j], recv_sems.at[w * 3 + j], (cx, cy, c))
            pairs.append((cp, cp))
    return pairs


def _rs_sum(name, partials, received, place_idx):
    n = len(partials)
    nb = 2
    blocks = [(p.shape[1] // nb, p.shape[2]) for p in partials]

    def body(idx_ref, *refs):
        for p_ref, r_ref, o_ref in zip(refs[:n], refs[n:2 * n], refs[2 * n:]):
            o_ref[...] = ((p_ref[...].astype(F32) + r_ref[0].astype(F32))
                          + (r_ref[1].astype(F32) + r_ref[2].astype(F32)))

    return _pallas_call(
        body, name=name,
        grid_spec=pltpu.PrefetchScalarGridSpec(
            num_scalar_prefetch=1, grid=(nb,),
            in_specs=[pl.BlockSpec((None,) + b, lambda i, idx: (idx[0], i, 0)) for b in blocks]
            + [pl.BlockSpec((3,) + b, lambda i, idx: (0, i, 0)) for b in blocks],
            out_specs=[pl.BlockSpec(b, lambda i, idx: (idx[1] * nb + i, 0)) for b in blocks]),
        out_shape=[_sds((2 * p.shape[1], p.shape[2]), F32) for p in partials],
        compiler_params=_params(("parallel",), 48),
    )(place_idx, *partials, *received)


def _share_pairs(arrays, send_sems, recv_sems):
    x, y, c, _, _ = _place()
    pairs = []
    for w, arr in enumerate(arrays):
        r2 = arr.shape[0] // 2
        mine = arr.at[pl.ds(pl.multiple_of(c * r2, 8), r2), :]
        theirs = arr.at[pl.ds(pl.multiple_of((1 - c) * r2, 8), r2), :]
        sems = (send_sems.at[w], recv_sems.at[w])
        pairs.append((_remote(mine, mine, *sems, (x, y, 1 - c)), _remote(theirs, theirs, *sems, (x, y, c))))
    return pairs


def _small_pack(red_mix, red_ffn, red_final, red_hg, g_conv):
    D = red_mix.shape[1]
    H = red_hg.shape[1]

    def body(mix_ref, ffn_ref, fin_ref, hg_ref, cv_ref, in_ref):
        in_ref[...] = jnp.zeros_like(in_ref)
        in_ref[0:1, :] = mix_ref[0:1, :]
        in_ref[1:2, :] = ffn_ref[0:1, :]
        in_ref[2:3, :] = fin_ref[0:1, :]
        gam = hg_ref[1:2, 0:HEAD_DIM]
        for h in range(1, H // HEAD_DIM):
            gam = gam + hg_ref[1:2, h * HEAD_DIM:(h + 1) * HEAD_DIM]
        in_ref[3:4, 0:HEAD_DIM] = gam
        in_ref[3:4, HEAD_DIM:2 * HEAD_DIM] = fin_ref[1:2, 0:HEAD_DIM]
        in_ref[4:5, 0:H] = hg_ref[0:1, :]
        in_ref[6:9, 0:H] = cv_ref[...]

    return _pallas_call(
        body, name="small_pack", pin=False,
        in_specs=[VMEM] * 5, out_specs=VMEM, out_shape=_sds((N_SMALL_ROWS, D), F32),
    )(red_mix, red_ffn, red_final, red_hg, g_conv)


def _small_pairs(arrays, send_sems, recv_sems):
    block, gathered = arrays
    x, y, c, _, _ = _place()
    me = 4 * x + 2 * y + c
    pairs = []
    for m in range(1, 8):
        px, py, pc = x ^ ((m >> 2) & 1), y ^ ((m >> 1) & 1), c ^ (m & 1)
        sems = (send_sems.at[m - 1], recv_sems.at[m - 1])
        pairs.append((_remote(block, gathered.at[me], *sems, (px, py, pc)),
                      _remote(block, gathered.at[4 * px + 2 * py + pc], *sems, (x, y, c))))
    return pairs


def _adamw_math(w, g, m, v):
    m = ADAM_B1 * m + (1.0 - ADAM_B1) * g
    v = ADAM_B2 * v + (1.0 - ADAM_B2) * jnp.square(g)
    m_hat = m / (1.0 - ADAM_B1 ** ADAM_STEP)
    v_hat = v / (1.0 - ADAM_B2 ** ADAM_STEP)
    delta = -ADAM_LR * (m_hat / (jnp.sqrt(v_hat) + ADAM_EPS) + ADAM_WD * w)
    return delta, m, v


def _adamw(name, gs, ws, ms, vs):
    n = len(gs)
    nb = 4

    def body(*refs):
        ins, outs = refs[:4 * n], refs[4 * n:]
        for j in range(n):
            g_ref, w_ref, m_ref, v_ref = ins[j], ins[n + j], ins[2 * n + j], ins[3 * n + j]
            go_ref, d_ref, mo_ref, vo_ref = outs[4 * j:4 * j + 4]
            g = g_ref[...]
            go_ref[...] = g
            d_ref[...], mo_ref[...], vo_ref[...] = _adamw_math(w_ref[...], g, m_ref[...], v_ref[...])

    blk = [pl.BlockSpec((g.shape[0] // nb, g.shape[1]), lambda i: (i, 0)) for g in gs]
    out = _pallas_call(
        body, name=name, grid=(nb,),
        in_specs=blk * 4, out_specs=[b for b in blk for _ in range(4)],
        out_shape=[_sds(g.shape, F32) for g in gs for _ in range(4)],
        compiler_params=_params(("parallel",), 56),
    )(*gs, *ws, *ms, *vs)
    return [list(out[4 * j:4 * j + 4]) for j in range(n)]


def _small_update(block, gathered, place_idx, ws, ms, vs):
    n = len(ws)
    H = ws[1].shape[1]

    def body(idx_ref, blk_ref, all_ref, *refs):
        w, m, v, outs, tot_ref = refs[:n], refs[n:2 * n], refs[2 * n:3 * n], refs[3 * n:-1], refs[-1]
        chip, me = idx_ref[0], idx_ref[1]
        tot = jnp.where(me == 0, blk_ref[...], all_ref[0])
        for d in range(1, 8):
            tot = tot + jnp.where(me == d, blk_ref[...], all_ref[d])
        tot_ref[...] = tot
        p0 = _lower_bound(w[1][...])
        dl0 = p0 * (1.0 - p0) * tot_ref[4:5, 0:H]
        conv = jnp.zeros((3, LANES), F32)
        for k in range(N_CHIPS):
            conv = jnp.where(chip == k, tot_ref[6:9, k * LANES:(k + 1) * LANES], conv)
        grads = [tot_ref[0:1, :], None, tot_ref[3:4, 0:HEAD_DIM], conv, tot_ref[1:2, :], tot_ref[2:3, :]]
        for p in range(n):
            g_ref, d_ref, mo_ref, vo_ref = outs[4 * p:4 * p + 4]
            if p == 1:
                for row, g in ((slice(0, 1), dl0), (slice(1, 2), -dl0)):
                    g_ref[row, :] = g
                    d_ref[row, :], mo_ref[row, :], vo_ref[row, :] = _adamw_math(
                        w[p][row, :], g, m[p][row, :], v[p][row, :])
            else:
                g_ref[...] = grads[p]
                d_ref[...], mo_ref[...], vo_ref[...] = _adamw_math(w[p][...], grads[p], m[p][...], v[p][...])
        outs[4 * n][...] = tot_ref[3:4, HEAD_DIM:2 * HEAD_DIM]

    full = lambda a: pl.BlockSpec(a.shape, lambda i, idx: (0,) * a.ndim)
    out_shape = [_sds(w.shape, F32) for w in ws for _ in range(4)] + [_sds((1, LANES), F32)]
    return _pallas_call(
        body, name="small_update",
        grid_spec=pltpu.PrefetchScalarGridSpec(
            num_scalar_prefetch=1, grid=(1,),
            in_specs=[full(block), full(gathered)] + [full(a) for a in ws + ms + vs],
            out_specs=[full(s) for s in out_shape],
            scratch_shapes=[pltpu.VMEM(block.shape, F32)]),
        out_shape=out_shape,
    )(place_idx, block, gathered, *ws, *ms, *vs)


def kernel(x, norm_mix_g, w_in, lower_bounds, hg_norm_g, conv_w, w_branch_a, w_branch_b, w_out, norm_ffn_g, w_ffn_gate, w_ffn_up, w_ffn_down, norm_final_g, loss_target, m_norm_mix_g, m_w_in, m_lower_bounds, m_hg_norm_g, m_conv_w, m_w_branch_a, m_w_branch_b, m_w_out, m_norm_ffn_g, m_w_ffn_gate, m_w_ffn_up, m_w_ffn_down, m_norm_final_g, v_norm_mix_g, v_w_in, v_lower_bounds, v_hg_norm_g, v_conv_w, v_w_branch_a, v_w_branch_b, v_w_out, v_norm_ffn_g, v_w_ffn_gate, v_w_ffn_up, v_w_ffn_down, v_norm_final_g):
    _, L, D = x.shape
    H = D // 2
    assert lower_bounds.shape == (2, H) and hg_norm_g.shape == (1, HEAD_DIM)
    assert conv_w.shape == (1, 3, LANES) and w_in.shape[2] * N_CHIPS == 11 * H
    x2d, target = x.reshape(L, D), loss_target.reshape(L, D)
    g_final = norm_final_g.reshape(1, D)
    chip = 2 * lax.axis_index("x") + lax.axis_index("y")
    core = lax.axis_index("c")

    tr = lambda w: jnp.transpose(w[0])
    big = [w_in[0], w_branch_a[0], w_branch_b[0], w_out[0], tr(w_ffn_gate), tr(w_ffn_up), w_ffn_down[0]]
    big_m = [m_w_in[0], m_w_branch_a[0], m_w_branch_b[0], m_w_out[0], tr(m_w_ffn_gate), tr(m_w_ffn_up),
             m_w_ffn_down[0]]
    big_v = [v_w_in[0], v_w_branch_a[0], v_w_branch_b[0], v_w_out[0], tr(v_w_ffn_gate), tr(v_w_ffn_up),
             v_w_ffn_down[0]]
    names = ["w_in", "w_branch_a", "w_branch_b", "w_out", "w_ffn_gate", "w_ffn_up", "w_ffn_down"]

    chip_idx = chip.reshape(1).astype(jnp.int32)
    def per_shape(fn, tag, js, *lists):
        groups = {}
        for pos, a in enumerate(lists[0]):
            groups.setdefault(a.shape, []).append(pos)
        results = [None] * len(js)
        for same in groups.values():
            out = fn(tag + names[js[same[0]]], *[[xs[p] for p in same] for xs in lists])
            for q, p in enumerate(same):
                results[p] = out[q]
        return results

    place_t = lambda name, ws: _cast_place_t(name, ws, chip_idx)
    placed = per_shape(place_t, "place_", [0, 1, 2], big[:3]) + list(_cast_place("place_rest", big[3:], chip_idx))
    conv_placed = lax.dynamic_update_slice(jnp.zeros((N_CHIPS, 3, LANES), F32), conv_w, (chip, 0, 0))
    x_i, y_i = lax.axis_index("x"), lax.axis_index("y")
    blocks = lambda *ks: jnp.stack(ks).astype(jnp.int32)
    near = lambda w, j: j < 2
    far = lambda w, j: w == 1 or j == 2
    near_sems, in_flight, _ = _gather_start("gather_start_near", [([placed[0]], set(), near)], chip_idx)
    w_in_buf = in_flight[0][0]
    h, proj = _fwd_proj_first(x2d, norm_mix_g, w_in_buf, blocks(chip))
    sems, in_flight, _ = _gather_start(
        "gather_start_rest", [([w_in_buf, conv_placed], {1}, far), (placed[1:4], set(), None),
                              (placed[4:], set(), None)], h)
    w_in_buf, conv_buf = in_flight[0]
    (w_in_buf,) = _gather_wait("gather_wait_in_near", [w_in_buf], set(), near_sems[0], h, near)
    (w_in_buf,) = _gather_forward("gather_fwd_in_near", [w_in_buf], (0, 1))
    proj = _fwd_proj_more("fwd_proj_near", h, w_in_buf, proj,
                          blocks(2 * (1 - x_i) + y_i, 2 * x_i + (1 - y_i)))
    w_in_buf, conv_all = _gather_wait("gather_wait_in_far", [w_in_buf, conv_buf], {1}, sems[0], proj, far)
    (w_int3,) = _gather_forward("gather_fwd_in_far", [w_in_buf], (2,))
    proj = _fwd_proj_more("fwd_proj_far", h, w_int3, proj, blocks(2 * (1 - x_i) + (1 - y_i)))
    w_int = w_int3.reshape(-1, D)
    conv_full = jnp.transpose(conv_all, (1, 0, 2)).reshape(3, H)
    og, o_pre, s_saved = _hgrn_fwd(proj, lower_bounds, hg_norm_g, H)
    landed = _gather_wait("gather_wait_mix", in_flight[1], set(), sems[1], og)
    fwd_sems, landed, token = _split_start("gather_fwd_mix_start", landed, 9, _forward_pairs)
    cb = _conv_fwd(proj, conv_full, H, token)
    wat3, wbt3, wout3 = _split_wait("gather_fwd_mix_wait", fwd_sems, landed, _forward_pairs, cb)
    wat, wbt, wout = wat3.reshape(D, H), wbt3.reshape(D, H), wout3.reshape(D, D)
    landed = _gather_wait("gather_wait_ffn", in_flight[2], set(), sems[2], cb)
    fwd_sems, landed, token = _split_start("gather_fwd_ffn_start", landed, 9, _forward_pairs)
    sig_a, sig_b, dm_dga, dm_dgb, merged, x1, h2 = _fwd_mix(og, cb, proj, x2d, wat, wbt, wout, norm_ffn_g,
                                                              H, token)
    wgt3, wut3, wd3 = _split_wait("gather_fwd_ffn_wait", fwd_sems, landed, _forward_pairs, h2)
    d_ff = N_CHIPS * wd3.shape[1]
    wgt, wut, wd = wgt3.reshape(d_ff, D), wut3.reshape(d_ff, D), wd3.reshape(d_ff, D)
    ffn_ds_da, ffn_ds_db, ffn_s = _fwd_ffn_up(h2, wgt, wut)
    dx2, dx2b, red_final = _fwd_down_loss(ffn_s, wd, x1, target, g_final)

    c_idx = core.reshape(1).astype(jnp.int32)
    place_idx = jnp.stack([chip, core]).astype(jnp.int32)

    def sibling_start(tag, grads):
        bufs = [lax.empty((N_CHIPS, g.shape[1] // 2, g.shape[2]), F32) for g in grads]
        return _split_start("rs_sibling_start_" + tag, list(grads) + bufs, len(grads), _sibling_pairs)

    def ici_start(tag, js, grads, from_sibling):
        partials = list(_rs_add("rs_add_" + tag, grads, from_sibling, c_idx))
        landings = [lax.empty((3,) + p.shape[1:], BF16) for p in partials]
        return _split_start("rs_ici_start_" + tag, partials + landings, 3 * len(js), _ici_pairs)

    def ici_start_behind(tag, js, started, after):
        n = len(js)
        arrays = _split_wait("rs_sibling_wait_" + tag, started[0], started[1], _sibling_pairs, after)
        return ici_start(tag, js, arrays[:n], arrays[n:])

    def sums(tag, started, after):
        partials, received = [], []
        for group, group_js, start in started:
            arrays = _split_wait("rs_ici_wait_" + group, start[0], start[1], _ici_pairs, after)
            partials += arrays[:len(group_js)]
            received += arrays[len(group_js):]
        return list(_rs_sum("rs_sum_" + tag, partials, received, place_idx))

    def adamw(tag, js, grads):
        return _adamw("adamw_" + tag, grads, *[[src[j] for j in js] for src in (big, big_m, big_v)])

    shards3 = lambda g: g.reshape(N_CHIPS, d_ff // N_CHIPS, D)
    da, db = _bwd_down(dx2b, wd, ffn_ds_da, ffn_ds_db)
    g_wd = shards3(_dw_rows2("dw_ffn_down", ffn_s, dx2b))
    g_wg = shards3(_dw_rows2("dw_ffn_gate", da, h2))
    g_wu = shards3(_dw_rows2("dw_ffn_up", db, h2))
    ffn_sibling = sibling_start("ffn", [g_wg, g_wu, g_wd])
    dx1, dx1b, red_ffn = _bwd_ffn_dh(da, db, wgt, wut, x1, dx2, norm_ffn_g, ffn_sibling[2])
    ffn_ici = ici_start_behind("ffn", [4, 5, 6], ffn_sibling, dx1b)
    dya, dyb, dproj, d_o, d_cb = _bwd_mix(dx1b, sig_a, sig_b, dm_dga, dm_dgb, wat, wbt, wout, H, ffn_ici[2])
    (g_wout,) = _dw_whole("dw_out", [(merged, dx1b)], True)
    g_wa, g_wb = _dw_whole("dw_branch", [(og, dya), (cb, dyb)], False)
    mix_sibling = sibling_start("mix", [g_wa, g_wb, g_wout])
    dproj, red_hg = _hgrn_bwd(proj, lower_bounds, hg_norm_g, o_pre, d_o, s_saved, H, mix_sibling[2], dproj)
    mix_ici = ici_start_behind("mix", [1, 2, 3], mix_sibling, red_hg)
    dproj, g_conv = _conv_bwd(proj, conv_full, d_cb, H, mix_ici[2], dproj)
    g_win, for_sibling = _dw_in(h, dproj, w_int3.shape[1], c_idx)
    in_sibling = _split_start("rs_sibling_start_in", [for_sibling, lax.empty(for_sibling.shape, BF16)], 1,
                              _sibling_whole_pairs)
    halves = sums("rest", [("mix", [1, 2, 3], mix_ici), ("ffn", [4, 5, 6], ffn_ici)], in_sibling[2])
    rest_share = _split_start("rs_share_start_rest", halves, len(halves), _share_pairs)
    from_sibling = _split_wait("rs_sibling_wait_in", in_sibling[0], in_sibling[1], _sibling_whole_pairs,
                               rest_share[2])[1]
    in_ici = ici_start("in", [0], [g_win], [from_sibling])
    grad_x, red_mix = _bwd_in(dproj, w_int, x2d, dx1, norm_mix_g, in_ici[2])
    in_share = _split_start("rs_share_start_in", sums("in", [("in", [0], in_ici)], grad_x), 1, _share_pairs)
    small_block = _small_pack(red_mix, red_ffn, red_final, red_hg, g_conv)
    small = _split_start("small_gather_start", [small_block, lax.empty((8,) + small_block.shape, F32)], 7,
                         _small_pairs)
    rest_grads = _split_wait("rs_share_wait_rest", rest_share[0], rest_share[1], _share_pairs, small[2])
    big_out = [None] + adamw("rest", [1, 2, 3, 4, 5, 6], rest_grads)
    in_grad = _split_wait("rs_share_wait_in", in_share[0], in_share[1], _share_pairs, big_out[6][0])
    big_out[0] = adamw("in", [0], in_grad)[0]
    small_block, small_all = _split_wait("small_gather_wait", small[0], small[1], _small_pairs, big_out[0][0])

    def smalls(mix, lb, hg, cw, ffn, fin):
        return [mix, lb, hg, cw[0], ffn, fin.reshape(1, D)]

    small_out = _small_update(
        small_block, small_all, jnp.stack([chip, 4 * x_i + 2 * y_i + core]).astype(jnp.int32),
        smalls(norm_mix_g, lower_bounds, hg_norm_g, conv_w, norm_ffn_g, norm_final_g),
        smalls(m_norm_mix_g, m_lower_bounds, m_hg_norm_g, m_conv_w, m_norm_ffn_g, m_norm_final_g),
        smalls(v_norm_mix_g, v_lower_bounds, v_hg_norm_g, v_conv_w, v_norm_ffn_g, v_norm_final_g))

    def outputs(i):
        big_i = [big_out[j][i] for j in range(7)]
        mix, lb, hg, cw, ffn, fin = [small_out[4 * p + i] for p in range(6)]
        return [mix, big_i[0][None], lb, hg, cw[None], big_i[1][None], big_i[2][None], big_i[3][None], ffn,
                big_i[4].T[None], big_i[5].T[None], big_i[6][None], fin.reshape(D)]

    outs = [small_out[24][0, 0], grad_x.reshape(1, L, D)]
    for i in range(4):
        outs += outputs(i)
    return tuple(outs)
```

```python
import jax
import jax.numpy as jnp
from jax import lax
from jax.experimental import pallas as pl
from jax.experimental.pallas import tpu as pltpu

F32 = jnp.float32
BF16 = jnp.bfloat16
EPS = 1e-6
CHUNK = 32
HEAD_DIM = 128
LANES = 128
N_CHIPS = 4
N_SMALL_ROWS = 16
DPROJ_BLOCKS = 12
DPROJ_BLOCK_OF = (0, 1, 2, 3, 8, 9, 10, 4, 5, 6, 7)

ADAM_LR = 0.001
ADAM_B1 = 0.9
ADAM_B2 = 0.999
ADAM_EPS = 1e-08
ADAM_WD = 0.01
ADAM_STEP = 10

MESH = pl.DeviceIdType.MESH
ANY = pl.BlockSpec(memory_space=pl.ANY)
VMEM = pl.BlockSpec(memory_space=pltpu.VMEM)
HBM = pl.BlockSpec(memory_space=pltpu.HBM)
SEM = pl.BlockSpec(memory_space=pltpu.SEMAPHORE)
EFFECT = pltpu.SideEffectType.DATAFLOW_SIDE_EFFECTING


def _sds(shape, dtype):
    return jax.ShapeDtypeStruct(shape, dtype)


def _pallas_call(body, pin=True, **kwargs):
    if not pin:
        return pl.pallas_call(body, **kwargs)
    in_hbm = lambda s: pltpu.HBM(s.shape, s.dtype) if isinstance(s, jax.ShapeDtypeStruct) else s
    kwargs["out_shape"] = jax.tree.map(in_hbm, kwargs["out_shape"])
    call = pl.pallas_call(body, **kwargs)

    def run(*args):
        return call(*[pltpu.with_memory_space_constraint(a, pltpu.HBM) if a.dtype in (F32, BF16) else a
                      for a in args])

    return run


def _params(semantics, vmem_mb):
    return pltpu.CompilerParams(dimension_semantics=semantics, vmem_limit_bytes=vmem_mb << 20)


def _nn(a, b):
    return lax.dot_general(a, b, (((1,), (0,)), ((), ())), preferred_element_type=F32)


def _nt(a, b):
    return lax.dot_general(a, b, (((1,), (1,)), ((), ())), preferred_element_type=F32)


def _tn(a, b):
    return lax.dot_general(a, b, (((0,), (0,)), ((), ())), preferred_element_type=F32)


def _sigmoid(x):
    return jax.nn.sigmoid(x)


def _rms_stats(x):
    r = lax.rsqrt(jnp.mean(x * x, axis=-1, keepdims=True) + EPS)
    return r, x * r


def _rms_bwd(dxh, xh, r):
    return r * (dxh - xh * jnp.mean(dxh * xh, axis=-1, keepdims=True))


def _fwd_proj_first(x, g_mix, w_int3, block, after):
    L, D = x.shape
    tn = w_int3.shape[1]
    tm = min(L, 1024)

    def body(blk_ref, x_ref, g_ref, w_ref, after_ref, h_ref, p_ref):
        _, xh = _rms_stats(x_ref[...])
        h = (xh * g_ref[...]).astype(BF16)
        h_ref[...] = h
        p_ref[...] = _nt(h, w_ref[...])

    return _pallas_call(
        body, name="fwd_proj_own",
        grid_spec=pltpu.PrefetchScalarGridSpec(
            num_scalar_prefetch=1, grid=(L // tm,),
            in_specs=[pl.BlockSpec((tm, D), lambda i, blk: (i, 0)),
                      pl.BlockSpec((1, D), lambda i, blk: (0, 0)),
                      pl.BlockSpec((None, tn, D), lambda i, blk: (blk[0], 0, 0)), ANY],
            out_specs=[pl.BlockSpec((tm, D), lambda i, blk: (i, 0)),
                       pl.BlockSpec((tm, tn), lambda i, blk: (i, blk[0]))]),
        out_shape=[_sds((L, D), BF16), _sds((L, N_CHIPS * tn), F32)],
        compiler_params=_params(("parallel",), 48),
    )(block, x, g_mix, w_int3, after)


def _fwd_proj_more(name, h, w_int3, proj, blocks):
    L, D = h.shape
    tn = w_int3.shape[1]
    tm = min(L, 1024)

    def body(blk_ref, h_ref, w_ref, proj_ref, p_ref):
        p_ref[...] = _nt(h_ref[...], w_ref[...])

    return _pallas_call(
        body, name=name,
        grid_spec=pltpu.PrefetchScalarGridSpec(
            num_scalar_prefetch=1, grid=(L // tm, blocks.shape[0]),
            in_specs=[pl.BlockSpec((tm, D), lambda i, j, blk: (i, 0)),
                      pl.BlockSpec((None, tn, D), lambda i, j, blk: (blk[j], 0, 0)), ANY],
            out_specs=pl.BlockSpec((tm, tn), lambda i, j, blk: (i, blk[j]))),
        out_shape=_sds(proj.shape, proj.dtype),
        input_output_aliases={3: 0},
        compiler_params=_params(("parallel", "arbitrary"), 48),
    )(blocks, h, w_int3, proj)


def _lower_bound(lbp):
    l0, l1 = lbp[0:1, :], lbp[1:2, :]
    m = jnp.maximum(l0, l1)
    e0, e1 = jnp.exp(l0 - m), jnp.exp(l1 - m)
    return e0 / (e0 + e1)


def _seg_scan(x, r32, forward):
    n = x.shape[0]
    s = 1
    while s < CHUNK:
        if forward:
            x = x + jnp.where(r32 >= s, pltpu.roll(x, s, 0), 0.0)
        else:
            x = x + jnp.where(r32 < CHUNK - s, pltpu.roll(x, n - s, 0), 0.0)
        s *= 2
    return x


def _bcast_row(x, row):
    n, w = x.shape
    nc = n // CHUNK
    x3 = x.reshape(nc, CHUNK, w)
    return jnp.broadcast_to(x3[:, row:row + 1, :], (nc, CHUNK, w)).reshape(n, w)


def _chunk_total(x):
    n, w = x.shape
    nc = n // CHUNK
    total = jnp.sum(x.reshape(nc, CHUNK, w), axis=1, keepdims=True)
    return jnp.broadcast_to(total, (nc, CHUNK, w)).reshape(n, w)


def _hgrn_prep(q_raw, f_raw, lb):
    r32 = lax.broadcasted_iota(jnp.int32, f_raw.shape, 0) & (CHUNK - 1)
    sig = _sigmoid(f_raw)
    f = lb + (1.0 - lb) * sig
    b = _seg_scan(jnp.log(f), r32, True)
    a = _bcast_row(b, CHUNK // 2 - 1)
    bl = _bcast_row(b, CHUNK - 1)
    sq = _sigmoid(q_raw)
    q = q_raw * sq * (HEAD_DIM ** -0.5)
    return dict(r32=r32, sig=sig, f=f, k=1.0 - f, b=b, a=a, bl=bl, sq=sq, q=q)


def _chunk_masks(n):
    ri = lax.broadcasted_iota(jnp.int32, (n, n), 0)
    ci = lax.broadcasted_iota(jnp.int32, (n, n), 1)
    same = (ri // CHUNK) == (ci // CHUNK)
    return same & (ci <= ri), same & (ri <= ci)


def _hgrn_fwd(proj, lower_bounds, gamma, H):
    L = proj.shape[0]
    nh = H // HEAD_DIM
    TL = min(L, 256)
    nc = TL // CHUNK

    def body(q_ref, f_ref, v_ref, g_ref, lbp_ref, gam_ref, og_ref, o_ref, s_ref, st_ref):
        @pl.when(pl.program_id(0) == 0)
        def _():
            st_ref[...] = jnp.zeros_like(st_ref)

        lb = _lower_bound(lbp_ref[...])
        gam = gam_ref[...]
        mask, _ = _chunk_masks(TL)
        rowc = lax.broadcasted_iota(jnp.int32, (TL, HEAD_DIM), 0) // CHUNK
        for h in range(nh):
            hs = slice(h * HEAD_DIM, (h + 1) * HEAD_DIM)
            p = _hgrn_prep(q_ref[:, hs], f_ref[:, hs], lb[:, hs])
            v = v_ref[:, hs]
            vb = v.astype(BF16)
            vt = v.T.astype(BF16)
            q_hat = (p["q"] * jnp.exp(p["b"] - p["a"])).astype(BF16)
            k_hat = (p["k"] * jnp.exp(p["a"] - p["b"])).astype(BF16)
            q_in = (p["q"] * jnp.exp(p["b"])).astype(BF16)
            k_out = (p["k"] * jnp.exp(p["bl"] - p["b"])).astype(BF16)
            dec = jnp.exp(p["bl"])
            att = jnp.where(mask, _nt(q_hat, k_hat), 0.0).astype(BF16)
            o_intra = _nn(att, vb)
            st = st_ref[h]
            for c in range(nc):
                rs = slice(c * CHUNK, (c + 1) * CHUNK)
                stb = st.astype(BF16)
                s_ref[c, h] = stb
                o_ref[rs, hs] = o_intra[rs] + _nt(q_in[rs], stb)
                k_c = jnp.where(rowc == c, k_out, jnp.zeros_like(k_out))
                st = st * dec[c * CHUNK:c * CHUNK + 1, :] + _nn(vt, k_c)
            st_ref[h] = st
            o = o_ref[:, hs]
            _, xh = _rms_stats(o)
            gr = g_ref[:, hs]
            og_ref[:, hs] = (xh * gam * (gr * _sigmoid(gr))).astype(BF16)

    col = lambda k: pl.BlockSpec((TL, H), lambda i, k=k: (i, k))
    return _pallas_call(
        body, name="hgrn_fwd", grid=(L // TL,),
        in_specs=[col(0), col(1), col(2), col(3),
                  pl.BlockSpec(lower_bounds.shape, lambda i: (0, 0)),
                  pl.BlockSpec(gamma.shape, lambda i: (0, 0))],
        out_specs=[pl.BlockSpec((TL, H), lambda i: (i, 0)),
                   pl.BlockSpec((TL, H), lambda i: (i, 0)),
                   pl.BlockSpec((nc, nh, HEAD_DIM, HEAD_DIM), lambda i: (i, 0, 0, 0))],
        out_shape=[_sds((L, H), BF16), _sds((L, H), F32),
                   _sds((L // CHUNK, nh, HEAD_DIM, HEAD_DIM), BF16)],
        scratch_shapes=[pltpu.VMEM((nh, HEAD_DIM, HEAD_DIM), F32)],
        compiler_params=_params(("arbitrary",), 48),
    )(proj, proj, proj, proj, lower_bounds, gamma)


def _hgrn_bwd(proj, lower_bounds, gamma, o_pre, d_out, s_saved, H, after, dproj):
    L = proj.shape[0]
    nh = H // HEAD_DIM
    TL = min(L, 256)
    nc = TL // CHUNK
    nt = L // TL

    def body(q_ref, f_ref, v_ref, g_ref, lbp_ref, gam_ref, o_ref, d_ref, s_ref, after_ref, dproj_ref,
             dp_ref, red_ref, dst_ref, dsall_ref, tmp_ref):
        @pl.when(pl.program_id(0) == 0)
        def _():
            dst_ref[...] = jnp.zeros_like(dst_ref)
            red_ref[...] = jnp.zeros_like(red_ref)

        lb = _lower_bound(lbp_ref[...])
        gam = gam_ref[...]
        mask, mask_t = _chunk_masks(TL)
        rowc = lax.broadcasted_iota(jnp.int32, (TL, HEAD_DIM), 0) // CHUNK
        for h in range(nh):
            hs = slice(h * HEAD_DIM, (h + 1) * HEAD_DIM)
            qr, gr, lbh = q_ref[:, hs], g_ref[:, hs], lb[:, hs]
            p = _hgrn_prep(qr, f_ref[:, hs], lbh)
            vb = v_ref[:, hs].astype(BF16)
            eba, eab = jnp.exp(p["b"] - p["a"]), jnp.exp(p["a"] - p["b"])
            eb, elb = jnp.exp(p["b"]), jnp.exp(p["bl"] - p["b"])
            dec = jnp.exp(p["bl"])
            q_hat, k_hat = p["q"] * eba, p["k"] * eab
            q_in, k_out = p["q"] * eb, p["k"] * elb
            q_hat_b, k_hat_b = q_hat.astype(BF16), k_hat.astype(BF16)
            q_in_b, k_out_b = q_in.astype(BF16), k_out.astype(BF16)

            o, dout = o_ref[:, hs], d_ref[:, hs]
            sg = _sigmoid(gr)
            r, xh = _rms_stats(o)
            dp_ref[3, :, hs] = (dout * (xh * gam) * (sg * (1.0 + gr * (1.0 - sg)))).astype(BF16)
            dn = dout * (gr * sg)
            red_ref[1:2, hs] += jnp.sum(dn * xh, axis=0, keepdims=True)
            do = _rms_bwd(dn * gam, xh, r)
            dob = do.astype(BF16)
            dot_b = do.T.astype(BF16)

            att_t = jnp.where(mask_t, _nt(k_hat_b, q_hat_b), 0.0).astype(BF16)
            dv_intra = _nn(att_t, dob)
            datt = jnp.where(mask, _nt(dob, vb), 0.0).astype(BF16)
            dqh = _nn(datt, k_hat_b)
            datt_t = jnp.where(mask_t, _nt(vb, dob), 0.0).astype(BF16)
            dkh = _nn(datt_t, q_hat_b)

            dst = dst_ref[h]
            for c in reversed(range(nc)):
                dsall_ref[c] = dst
                q_c = jnp.where(rowc == c, q_in_b, jnp.zeros_like(q_in_b))
                dst = dst * dec[c * CHUNK:c * CHUNK + 1, :] + _nn(dot_b, q_c)
            dst_ref[h] = dst
            for c in range(nc):
                rs = slice(c * CHUNK, (c + 1) * CHUNK)
                ds_c = dsall_ref[c]
                dsb = ds_c.astype(BF16)
                st_prev = s_ref[c, h]
                tmp_ref[0, rs, :] = _nt(k_out_b[rs], dsb)
                tmp_ref[1, rs, :] = _nn(vb[rs], dsb)
                tmp_ref[2, rs, :] = _nn(dob[rs], st_prev)
                ddec = jnp.sum(ds_c * st_prev.astype(F32), axis=0, keepdims=True)
                tmp_ref[3, rs, :] = jnp.broadcast_to(ddec * dec[c * CHUNK:c * CHUNK + 1, :],
                                                     (CHUNK, HEAD_DIM))
            dko, dqi = tmp_ref[1], tmp_ref[2]
            dq = dqh * eba + dqi * eb
            dk = dkh * eab + dko * elb
            tko = dko * k_out
            db = dqh * q_hat - dkh * k_hat + dqi * q_in - tko
            dlog = _seg_scan(db, p["r32"], False) + _chunk_total(tko) + tmp_ref[3]
            df = dlog / p["f"] - dk
            sig = p["sig"]
            red_ref[0:1, hs] += jnp.sum(df * (1.0 - sig), axis=0, keepdims=True)
            dp_ref[1, :, hs] = (df * (1.0 - lbh) * sig * (1.0 - sig)).astype(BF16)
            sq = p["sq"]
            dp_ref[0, :, hs] = (dq * (HEAD_DIM ** -0.5) * (sq * (1.0 + qr * (1.0 - sq)))).astype(BF16)
            dp_ref[2, :, hs] = (dv_intra + tmp_ref[0]).astype(BF16)

    col = lambda k: pl.BlockSpec((TL, H), lambda i, k=k: (nt - 1 - i, k))
    rev = pl.BlockSpec((TL, H), lambda i: (nt - 1 - i, 0))
    return _pallas_call(
        body, name="hgrn_bwd", grid=(nt,),
        in_specs=[col(0), col(1), col(2), col(3),
                  pl.BlockSpec(lower_bounds.shape, lambda i: (0, 0)),
                  pl.BlockSpec(gamma.shape, lambda i: (0, 0)),
                  rev, rev,
                  pl.BlockSpec((nc, nh, HEAD_DIM, HEAD_DIM), lambda i: (nt - 1 - i, 0, 0, 0)), ANY, ANY],
        out_specs=[pl.BlockSpec((4, TL, H), lambda i: (0, nt - 1 - i, 0)), pl.BlockSpec((8, H), lambda i: (0, 0))],
        out_shape=[_sds(dproj.shape, BF16), _sds((8, H), F32)],
        input_output_aliases={10: 0},
        scratch_shapes=[pltpu.VMEM((nh, HEAD_DIM, HEAD_DIM), F32),
                        pltpu.VMEM((nc, HEAD_DIM, HEAD_DIM), F32),
                        pltpu.VMEM((4, TL, HEAD_DIM), F32)],
        compiler_params=_params(("arbitrary",), 48),
    )(proj, proj, proj, proj, lower_bounds, gamma, o_pre, d_out, s_saved, after, dproj)


def _shift_down(u, s, row):
    return jnp.where(row >= s, pltpu.roll(u, s, 0), 0.0)


def _shift_up(u, s, row):
    n = u.shape[0]
    return jnp.where(row < n - s, pltpu.roll(u, n - s, 0), 0.0)


def _conv_specs(L, H):
    per = H // LANES
    return [pl.BlockSpec((L, LANES), lambda j, o=o: (0, o * per + j)) for o in (4, 5, 6)]


def _conv_fwd(proj, conv_w, H, after):
    L = proj.shape[0]

    def body(c_ref, b_ref, x_ref, w_ref, after_ref, o_ref):
        row = lax.broadcasted_iota(jnp.int32, (L, LANES), 0)
        u = c_ref[...] * x_ref[...]
        w = w_ref[...]
        y = w[0:1] * _shift_down(u, 2, row) + w[1:2] * _shift_down(u, 1, row) + w[2:3] * u
        o_ref[...] = (b_ref[...] * y).astype(BF16)

    return _pallas_call(
        body, name="conv_fwd", grid=(H // LANES,),
        in_specs=_conv_specs(L, H) + [pl.BlockSpec((3, LANES), lambda j: (0, j)), ANY],
        out_specs=pl.BlockSpec((L, LANES), lambda j: (0, j)),
        out_shape=_sds((L, H), BF16),
        compiler_params=_params(("parallel",), 48),
    )(proj, proj, proj, conv_w, after)


def _conv_bwd(proj, conv_w, dcb, H, after, dproj):
    L = proj.shape[0]

    def body(c_ref, b_ref, x_ref, w_ref, d_ref, after_ref, dproj_ref, dp_ref, dw_ref):
        row = lax.broadcasted_iota(jnp.int32, (L, LANES), 0)
        cg, xb = c_ref[...], x_ref[...]
        u = cg * xb
        u1, u2 = _shift_down(u, 1, row), _shift_down(u, 2, row)
        w = w_ref[...]
        y = w[0:1] * u2 + w[1:2] * u1 + w[2:3] * u
        d = d_ref[...]
        dp_ref[1] = (d * y).astype(BF16)
        dy = d * b_ref[...]
        du = w[2:3] * dy + w[1:2] * _shift_up(dy, 1, row) + w[0:1] * _shift_up(dy, 2, row)
        dw_ref[0:1, :] = jnp.sum(dy * u2, axis=0, keepdims=True)
        dw_ref[1:2, :] = jnp.sum(dy * u1, axis=0, keepdims=True)
        dw_ref[2:3, :] = jnp.sum(dy * u, axis=0, keepdims=True)
        dp_ref[0] = (du * xb).astype(BF16)
        dp_ref[2] = (du * cg).astype(BF16)
        dp_ref[3] = jnp.zeros((L, LANES), BF16)

    blk = pl.BlockSpec((L, LANES), lambda j: (0, j))
    return _pallas_call(
        body, name="conv_bwd", grid=(H // LANES,),
        in_specs=_conv_specs(L, H) + [pl.BlockSpec((3, LANES), lambda j: (0, j)), blk, ANY, ANY],
        out_specs=[pl.BlockSpec((4, L, LANES), lambda j: (2, 0, j)), pl.BlockSpec((3, LANES), lambda j: (0, j))],
        out_shape=[_sds(dproj.shape, BF16), _sds((3, H), F32)],
        input_output_aliases={6: 0},
        compiler_params=_params(("parallel",), 56),
    )(proj, proj, proj, conv_w, dcb, after, dproj)


def _gate_specs(tm, H):
    return [pl.BlockSpec((tm, H), lambda i, k=k: (i, k)) for k in (7, 8, 9, 10)]


def _fwd_mix(og, cb, proj, x, wat, wbt, wout, g_ffn, H, after):
    L, D = x.shape
    tm = min(L, 512)

    def body(o_ref, cb_ref, ga0, ga1, gb0, gb1, x_ref, wa_ref, wb_ref, wo_ref, g_ref, after_ref,
             sa_ref, sb_ref, ta_ref, tb_ref, m_ref, x1_ref, h2_ref):
        ya, yb = _nt(o_ref[...], wa_ref[...]), _nt(cb_ref[...], wb_ref[...])
        for k, (gar, gbr) in enumerate(((ga0, gb0), (ga1, gb1))):
            cs = slice(k * H, (k + 1) * H)
            sa, sb = _sigmoid(gar[...]), _sigmoid(gbr[...])
            ma, mb = sa * ya[:, cs], sb * yb[:, cs]
            m_ref[:, cs] = (ma + mb).astype(BF16)
            sa_ref[:, cs] = sa.astype(BF16)
            sb_ref[:, cs] = sb.astype(BF16)
            ta_ref[:, cs] = (ma * (1.0 - sa)).astype(BF16)
            tb_ref[:, cs] = (mb * (1.0 - sb)).astype(BF16)
        x1 = x_ref[...] + _nn(m_ref[...], wo_ref[...])
        x1_ref[...] = x1
        _, xh = _rms_stats(x1)
        h2_ref[...] = (xh * g_ref[...]).astype(BF16)

    row = lambda w: pl.BlockSpec((tm, w), lambda i: (i, 0))
    full = lambda a: pl.BlockSpec(a.shape, lambda i: (0,) * a.ndim)
    return _pallas_call(
        body, name="fwd_mix", grid=(L // tm,),
        in_specs=[row(H), row(H)] + _gate_specs(tm, H) + [row(D), full(wat), full(wbt), full(wout),
                                                           full(g_ffn), ANY],
        out_specs=[row(D)] * 7,
        out_shape=[_sds((L, D), BF16)] * 5 + [_sds((L, D), F32), _sds((L, D), BF16)],
        compiler_params=_params(("parallel",), 56),
    )(og, cb, proj, proj, proj, proj, x, wat, wbt, wout, g_ffn, after)


def _bwd_mix(dx1b, sig_a, sig_b, dm_dga, dm_dgb, wat, wbt, wout, H, after):
    L, D = dx1b.shape
    tm = min(L, 512)

    def body(dx_ref, sa_ref, sb_ref, ta_ref, tb_ref, wa_ref, wb_ref, wo_ref, after_ref,
             dya_ref, dyb_ref, dgate_ref, do_ref, dcb_ref):
        dm = _nt(dx_ref[...], wo_ref[...])
        dga = (dm * ta_ref[...].astype(F32)).astype(BF16)
        dgb = (dm * tb_ref[...].astype(F32)).astype(BF16)
        for q, part in enumerate((dga[:, 0:H], dga[:, H:D], dgb[:, 0:H], dgb[:, H:D])):
            dgate_ref[q] = part
        dya_ref[...] = (dm * sa_ref[...].astype(F32)).astype(BF16)
        dyb_ref[...] = (dm * sb_ref[...].astype(F32)).astype(BF16)
        do_ref[...] = _nn(dya_ref[...], wa_ref[...])
        dcb_ref[...] = _nn(dyb_ref[...], wb_ref[...])

    row = lambda w: pl.BlockSpec((tm, w), lambda i: (i, 0))
    full = lambda a: pl.BlockSpec(a.shape, lambda i: (0,) * a.ndim)
    return _pallas_call(
        body, name="bwd_mix", grid=(L // tm,),
        in_specs=[row(D)] * 5 + [full(wat), full(wbt), full(wout), ANY],
        out_specs=[row(D), row(D), pl.BlockSpec((4, tm, H), lambda i: (1, i, 0)), row(H), row(H)],
        out_shape=[_sds((L, D), BF16)] * 2 + [_sds((DPROJ_BLOCKS, L, H), BF16)] + [_sds((L, H), F32)] * 2,
        compiler_params=_params(("parallel",), 56),
    )(dx1b, sig_a, sig_b, dm_dga, dm_dgb, wat, wbt, wout, after)


def _fwd_ffn_up(h2, wgt, wut):
    L, D = h2.shape
    F = wgt.shape[0]
    tn = F // 2
    tm = min(L, 512)

    def body(h_ref, wg_ref, wu_ref, sa_ref, sb_ref, s_ref):
        h = h_ref[...]
        a, b = _nt(h, wg_ref[...]), _nt(h, wu_ref[...])
        sg = _sigmoid(a)
        silu = a * sg
        sa_ref[...] = (b * sg * (1.0 + a * (1.0 - sg))).astype(BF16)
        sb_ref[...] = silu.astype(BF16)
        s_ref[...] = (silu * b).astype(BF16)

    wspec = pl.BlockSpec((tn, D), lambda j, i: (j, 0))
    ospec = pl.BlockSpec((tm, tn), lambda j, i: (i, j))
    return _pallas_call(
        body, name="fwd_ffn_up", grid=(2, L // tm),
        in_specs=[pl.BlockSpec((tm, D), lambda j, i: (i, 0)), wspec, wspec],
        out_specs=[ospec] * 3,
        out_shape=[_sds((L, F), BF16)] * 3,
        compiler_params=_params(("parallel", "parallel"), 48),
    )(h2, wgt, wut)


def _fwd_down_loss(s, wd, x1, target, g_final):
    L, D = x1.shape
    F = wd.shape[0]
    tm = min(L, 512)

    def body(s_ref, wd_ref, x1_ref, t_ref, g_ref, dx_ref, dxb_ref, red_ref):
        @pl.when(pl.program_id(0) == 0)
        def _():
            red_ref[...] = jnp.zeros_like(red_ref)

        g = g_ref[...]
        r, xh = _rms_stats(x1_ref[...] + _nn(s_ref[...], wd_ref[...]))
        e = xh * g - t_ref[...]
        dy = e * (1.0 / D)
        dx = _rms_bwd(dy * g, xh, r)
        dx_ref[...] = dx
        dxb_ref[...] = dx.astype(BF16)
        red_ref[0:1, :] += jnp.sum(dy * xh, axis=0, keepdims=True)
        red_ref[1:2, :] += jnp.broadcast_to(0.5 * jnp.sum(e * e) * (1.0 / D), (1, D))

    row = pl.BlockSpec((tm, D), lambda i: (i, 0))
    return _pallas_call(
        body, name="fwd_down_loss", grid=(L // tm,),
        in_specs=[pl.BlockSpec((tm, F), lambda i: (i, 0)), pl.BlockSpec((F, D), lambda i: (0, 0)),
                  row, row, pl.BlockSpec((1, D), lambda i: (0, 0))],
        out_specs=[row, row, pl.BlockSpec((8, D), lambda i: (0, 0))],
        out_shape=[_sds((L, D), F32), _sds((L, D), BF16), _sds((8, D), F32)],
        compiler_params=_params(("arbitrary",), 56),
    )(s, wd, x1, target, g_final)


def _bwd_down(dx2b, wd, s_a, s_b):
    L, D = dx2b.shape
    F = wd.shape[0]
    tn = F // 2
    tm = min(L, 512)

    def body(dx_ref, wd_ref, sa_ref, sb_ref, da_ref, db_ref):
        ds = _nt(dx_ref[...], wd_ref[...])
        da_ref[...] = (ds * sa_ref[...].astype(F32)).astype(BF16)
        db_ref[...] = (ds * sb_ref[...].astype(F32)).astype(BF16)

    ospec = pl.BlockSpec((tm, tn), lambda j, i: (i, j))
    return _pallas_call(
        body, name="bwd_down", grid=(2, L // tm),
        in_specs=[pl.BlockSpec((tm, D), lambda j, i: (i, 0)),
                  pl.BlockSpec((tn, D), lambda j, i: (j, 0)), ospec, ospec],
        out_specs=[ospec] * 2,
        out_shape=[_sds((L, F), BF16)] * 2,
        compiler_params=_params(("parallel", "parallel"), 48),
    )(dx2b, wd, s_a, s_b)


def _bwd_ffn_dh(da, db, wgt, wut, x1, dx2, g_ffn, after):
    L, D = x1.shape
    F = wgt.shape[0]
    tm = min(L, 256)

    def body(da_ref, db_ref, wg_ref, wu_ref, x1_ref, dx2_ref, g_ref, after_ref, dx_ref, dxb_ref, red_ref):
        @pl.when(pl.program_id(0) == 0)
        def _():
            red_ref[...] = jnp.zeros_like(red_ref)

        dh = _nn(da_ref[...], wg_ref[...]) + _nn(db_ref[...], wu_ref[...])
        r, xh = _rms_stats(x1_ref[...])
        red_ref[0:1, :] += jnp.sum(dh * xh, axis=0, keepdims=True)
        dx = dx2_ref[...] + _rms_bwd(dh * g_ref[...], xh, r)
        dx_ref[...] = dx
        dxb_ref[...] = dx.astype(BF16)

    row = pl.BlockSpec((tm, D), lambda i: (i, 0))
    aspec = pl.BlockSpec((tm, F), lambda i: (i, 0))
    wspec = pl.BlockSpec((F, D), lambda i: (0, 0))
    return _pallas_call(
        body, name="bwd_ffn_dh", grid=(L // tm,),
        in_specs=[aspec, aspec, wspec, wspec, row, row, pl.BlockSpec((1, D), lambda i: (0, 0)), ANY],
        out_specs=[row, row, pl.BlockSpec((8, D), lambda i: (0, 0))],
        out_shape=[_sds((L, D), F32), _sds((L, D), BF16), _sds((8, D), F32)],
        compiler_params=_params(("arbitrary",), 56),
    )(da, db, wgt, wut, x1, dx2, g_ffn, after)


def _bwd_in(dproj, w_int, x, dx1, g_mix, after):
    L, D = x.shape
    N = w_int.shape[0]
    H = dproj.shape[2]
    tm = min(L, 256)
    assert N == len(DPROJ_BLOCK_OF) * H

    def body(blocks_ref, w_ref, x_ref, dx1_ref, g_ref, after_ref, dx_ref, red_ref, dp_ref):
        @pl.when(pl.program_id(0) == 0)
        def _():
            red_ref[...] = jnp.zeros_like(red_ref)

        for t, block in enumerate(DPROJ_BLOCK_OF):
            dp_ref[:, t * H:(t + 1) * H] = blocks_ref[block]
        dh = _nn(dp_ref[...], w_ref[...])
        r, xh = _rms_stats(x_ref[...])
        red_ref[0:1, :] += jnp.sum(dh * xh, axis=0, keepdims=True)
        dx_ref[...] = dx1_ref[...] + _rms_bwd(dh * g_ref[...], xh, r)

    row = pl.BlockSpec((tm, D), lambda i: (i, 0))
    return _pallas_call(
        body, name="bwd_in", grid=(L // tm,),
        in_specs=[pl.BlockSpec((DPROJ_BLOCKS, tm, H), lambda i: (0, i, 0)), pl.BlockSpec((N, D), lambda i: (0, 0)),
                  row, row, pl.BlockSpec((1, D), lambda i: (0, 0)), ANY],
        out_specs=[row, pl.BlockSpec((8, D), lambda i: (0, 0))],
        out_shape=[_sds((L, D), F32), _sds((8, D), F32)],
        scratch_shapes=[pltpu.VMEM((tm, N), BF16)],
        compiler_params=_params(("arbitrary",), 56),
    )(dproj, w_int, x, dx1, g_mix, after)


def _dw_in(h, dproj, n_cols, c_idx):
    L, D = h.shape
    H = dproj.shape[2]
    tk = min(L, TK_TOKENS)
    nk = L // tk
    r2 = D // 2
    first = [(j * n_cols) // H for j in range(N_CHIPS)]
    last = [((j + 1) * n_cols - 1) // H for j in range(N_CHIPS)]
    slots = max(b - a for a, b in zip(first, last)) + 1
    plan = []
    for j in range(N_CHIPS):
        lo, hi = j * n_cols, (j + 1) * n_cols
        segments = []
        for s in range(last[j] - first[j] + 1):
            a, b = max(lo, (first[j] + s) * H), min(hi, (first[j] + s + 1) * H)
            segments.append((s, a - (first[j] + s) * H, b - a, a - lo))
        plan.append(segments)

    def body(c_ref, *refs):
        h_ref, slot_refs = refs[0], refs[1:1 + slots]
        o_ref, sib_ref, b_ref = refs[1 + slots:]
        j, k = pl.program_id(0), pl.program_id(1)
        for jj in range(N_CHIPS):
            @pl.when(j == jj)
            def _(jj=jj):
                for s, start, width, at in plan[jj]:
                    b_ref[:, at:at + width] = slot_refs[s][:, start:start + width]

        part = _tn(h_ref[...], b_ref[...])

        @pl.when(k == 0)
        def _():
            o_ref[...] = part

        @pl.when(k > 0)
        def _():
            o_ref[...] += part

        @pl.when(k == nk - 1)
        def _():
            theirs = pl.ds(pl.multiple_of((1 - c_ref[0]) * r2, 8), r2)
            sib_ref[...] = o_ref[theirs, :].astype(BF16)

    def slot_spec(s):
        blocks = [DPROJ_BLOCK_OF[min(first[j] + s, last[j])] for j in range(N_CHIPS)]

        def index(j, k, c_ref):
            block = blocks[0]
            for jj in range(1, N_CHIPS):
                block = jnp.where(j == jj, blocks[jj], block)
            return (block, k, 0)

        return pl.BlockSpec((None, tk, H), index)

    return _pallas_call(
        body, name="dw_in",
        grid_spec=pltpu.PrefetchScalarGridSpec(
            num_scalar_prefetch=1, grid=(N_CHIPS, nk),
            in_specs=[pl.BlockSpec((tk, D), lambda j, k, c_ref: (k, 0))] + [slot_spec(s) for s in range(slots)],
            out_specs=[pl.BlockSpec((None, D, n_cols), lambda j, k, c_ref: (j, 0, 0)),
                       pl.BlockSpec((None, r2, n_cols), lambda j, k, c_ref: (j, 0, 0))],
            scratch_shapes=[pltpu.VMEM((tk, n_cols), BF16)]),
        out_shape=[_sds((N_CHIPS, D, n_cols), F32), _sds((N_CHIPS, r2, n_cols), BF16)],
        compiler_params=_params(("parallel", "arbitrary"), 56),
    )(c_idx, h, *([dproj] * slots))


def _mm_tn(name, a, b, a_spec, b_spec, o_block, n_out, n_k):
    def body(a_ref, b_ref, o_ref):
        part = _tn(a_ref[...], b_ref[...])

        @pl.when(pl.program_id(1) == 0)
        def _():
            o_ref[...] = part

        @pl.when(pl.program_id(1) > 0)
        def _():
            o_ref[...] += part

    return _pallas_call(
        body, name=name, grid=(n_out, n_k),
        in_specs=[a_spec, b_spec],
        out_specs=pl.BlockSpec((None,) + o_block, lambda j, k: (j, 0, 0)),
        out_shape=_sds((n_out,) + o_block, F32),
        compiler_params=_params(("parallel", "arbitrary"), 56),
    )(a, b)


TK_TOKENS = 2048


def _dw_whole(name, pairs, by_rows):
    n = len(pairs)
    L = pairs[0][0].shape[0]
    tk = min(L, TK_TOKENS)

    def body(*refs):
        for q in range(n):
            a_ref, b_ref, o_ref = refs[2 * q], refs[2 * q + 1], refs[2 * n + q]
            part = _tn(a_ref[...], b_ref[...])
            rows, cols = o_ref.shape[1], o_ref.shape[2]
            shards = [part[j * rows:(j + 1) * rows, :] if by_rows else part[:, j * cols:(j + 1) * cols]
                      for j in range(N_CHIPS)]

            @pl.when(pl.program_id(0) == 0)
            def _(shards=shards, o_ref=o_ref):
                for j, shard in enumerate(shards):
                    o_ref[j] = shard

            @pl.when(pl.program_id(0) > 0)
            def _(shards=shards, o_ref=o_ref):
                for j, shard in enumerate(shards):
                    o_ref[j] += shard

    in_specs, out_specs, out_shape, operands = [], [], [], []
    for a, b in pairs:
        M, N = a.shape[1], b.shape[1]
        shape = (N_CHIPS, M // N_CHIPS, N) if by_rows else (N_CHIPS, M, N // N_CHIPS)
        in_specs += [pl.BlockSpec((tk, M), lambda k: (k, 0)), pl.BlockSpec((tk, N), lambda k: (k, 0))]
        out_specs.append(pl.BlockSpec(shape, lambda k: (0, 0, 0)))
        out_shape.append(_sds(shape, F32))
        operands += [a, b]
    return _pallas_call(
        body, name=name, grid=(L // tk,), in_specs=in_specs, out_specs=out_specs, out_shape=out_shape,
        compiler_params=_params(("arbitrary",), 56),
    )(*operands)


def _dw_rows2(name, a, b):
    L, M = a.shape
    N = b.shape[1]
    tk = min(L, TK_TOKENS)
    return _mm_tn(name, a, b, pl.BlockSpec((tk, M // 2), lambda j, k: (k, j)),
                  pl.BlockSpec((tk, N), lambda j, k: (k, 0)), (M // 2, N), 2, L // tk)


def _place():
    x, y, c = lax.axis_index("x"), lax.axis_index("y"), lax.axis_index("c")
    chips = [(1 - x, y), (x, 1 - y), (1 - x, 1 - y)]
    return x, y, c, 2 * x + y, chips


def _remote(src, dst, send_sem, recv_sem, device):
    return pltpu.make_async_remote_copy(src_ref=src, dst_ref=dst, send_sem=send_sem,
                                        recv_sem=recv_sem, device_id=device, device_id_type=MESH)


def _half(ref, lead, c, r2):
    return ref.at[lead, pl.ds(pl.multiple_of(c * r2, 16), r2), :]


def _cast_place(name, ws, chip_idx):
    n = len(ws)

    def body(k_ref, *refs):
        for w_ref, o_ref in zip(refs[:n], refs[n:]):
            o_ref[...] = w_ref[...].astype(BF16)

    return _pallas_call(
        body, name=name,
        grid_spec=pltpu.PrefetchScalarGridSpec(
            num_scalar_prefetch=1, grid=(2,),
            in_specs=[pl.BlockSpec((w.shape[0] // 2, w.shape[1]), lambda i, k_ref: (i, 0)) for w in ws],
            out_specs=[pl.BlockSpec((None, w.shape[0] // 2, w.shape[1]), lambda i, k_ref: (k_ref[0], i, 0))
                       for w in ws]),
        out_shape=[_sds((N_CHIPS,) + w.shape, BF16) for w in ws],
        compiler_params=_params(("parallel",), 48),
    )(chip_idx, *ws)


def _cast_place_t(name, ws, chip_idx):
    n = len(ws)
    r, cols = ws[0].shape

    def body(k_ref, *refs):
        for w_ref, o_ref in zip(refs[:n], refs[n:]):
            o_ref[...] = w_ref[...].T.astype(BF16)

    return _pallas_call(
        body, name=name,
        grid_spec=pltpu.PrefetchScalarGridSpec(
            num_scalar_prefetch=1, grid=(cols // LANES,),
            in_specs=[pl.BlockSpec((r, LANES), lambda i, k_ref: (0, i))] * n,
            out_specs=[pl.BlockSpec((None, LANES, r), lambda i, k_ref: (k_ref[0], i, 0))] * n),
        out_shape=[_sds((N_CHIPS, cols, r), BF16)] * n,
        compiler_params=_params(("parallel",), 48),
    )(chip_idx, *ws)


def _gather_copies(bufs, whole, send_sems, recv_sems, select=None):
    x, y, c, k, chips = _place()
    pairs = []
    for w, buf in enumerate(bufs):
        for j, (cx, cy) in enumerate(chips):
            if select is not None and not select(w, j):
                continue
            if w in whole:
                mine, theirs = buf.at[k], buf.at[2 * cx + cy]
            else:
                r2 = buf.shape[1] // 2
                mine, theirs = _half(buf, k, c, r2), _half(buf, 2 * cx + cy, c, r2)
            sems = (send_sems.at[w * 3 + j], recv_sems.at[w * 3 + j])
            pairs.append((_remote(mine, mine, *sems, (cx, cy, c)), _remote(theirs, theirs, *sems, (x, y, c))))
    return pairs


def _gather_start(name, groups, after):
    flat = [b for bufs, _, _ in groups for b in bufs]
    nb, ng = len(flat), len(groups)

    def body(*refs):
        ins, sems, token = refs[:nb], refs[nb + 1:nb + 1 + 2 * ng], refs[-1]
        pos = 0
        for g, (bufs, whole, select) in enumerate(groups):
            for send, _ in _gather_copies(ins[pos:pos + len(bufs)], whole, sems[2 * g], sems[2 * g + 1], select):
                send.start()
            pos += len(bufs)
        token[...] = jnp.zeros_like(token)

    sem_shapes = []
    for bufs, _, _ in groups:
        sem_shapes += [pltpu.SemaphoreType.DMA((3 * len(bufs),))] * 2
    out = _pallas_call(
        body, name=name,
        in_specs=[HBM] * nb + [ANY], out_specs=tuple([SEM] * (2 * ng) + [HBM] * nb + [VMEM]),
        out_shape=tuple(sem_shapes + [pltpu.HBM(b.shape, b.dtype) for b in flat] + [_sds((8, LANES), F32)]),
        input_output_aliases={i: 2 * ng + i for i in range(nb)},
        compiler_params=pltpu.CompilerParams(has_side_effects=EFFECT),
    )(*flat, after)
    sems, thru, pos = [], [], 2 * ng
    for g, (bufs, _, _) in enumerate(groups):
        sems.append((out[2 * g], out[2 * g + 1]))
        thru.append(list(out[pos:pos + len(bufs)]))
        pos += len(bufs)
    return sems, thru, out[-1]


def _gather_wait(name, bufs, whole, sems, after, select=None):
    nb = len(bufs)

    def body(*refs):
        ins, send_sems, recv_sems = refs[:nb], refs[nb], refs[nb + 1]
        for send, arrival in _gather_copies(ins, whole, send_sems, recv_sems, select):
            send.wait_send()
            arrival.wait_recv()

    return _pallas_call(
        body, name=name,
        in_specs=[HBM] * nb + [SEM, SEM, ANY], out_specs=[HBM] * nb,
        out_shape=[pltpu.HBM(b.shape, b.dtype) for b in bufs],
        input_output_aliases={i: i for i in range(nb)},
        compiler_params=pltpu.CompilerParams(has_side_effects=EFFECT),
    )(*bufs, sems[0], sems[1], after)


def _gather_forward(name, bufs, sources=(0, 1, 2)):
    n = len(bufs)

    def body(*refs):
        outs = refs[n:2 * n]
        send_sems, recv_sems = refs[2 * n:]
        x, y, c, _, chips = _place()
        sends = []
        for w in range(n):
            r2 = outs[w].shape[1] // 2
            for j in sources:
                landed = _half(outs[w], 2 * chips[j][0] + chips[j][1], c, r2)
                sends.append(_remote(landed, landed, send_sems.at[w * 3 + j], recv_sems.at[w * 3 + j],
                                     (x, y, 1 - c)))
        for cp in sends:
            cp.start()
        for w in range(n):
            r2 = outs[w].shape[1] // 2
            for j in sources:
                got = _half(outs[w], 2 * chips[j][0] + chips[j][1], 1 - c, r2)
                _remote(got, got, send_sems.at[w * 3 + j], recv_sems.at[w * 3 + j], (x, y, c)).wait_recv()
        for cp in sends:
            cp.wait_send()

    return _pallas_call(
        body, name=name,
        in_specs=[ANY] * n, out_specs=[ANY] * n,
        out_shape=[_sds(b.shape, b.dtype) for b in bufs],
        input_output_aliases={i: i for i in range(n)},
        scratch_shapes=[pltpu.SemaphoreType.DMA((n * 3,)), pltpu.SemaphoreType.DMA((n * 3,))],
    )(*bufs)


def _rs_add(name, grads3, from_sibling, c_idx):
    n = len(grads3)

    def body(c_ref, *refs):
        for g_ref, s_ref, o_ref in zip(refs[:n], refs[n:2 * n], refs[2 * n:]):
            o_ref[...] = (g_ref[...] + s_ref[...].astype(F32)).astype(BF16)

    mine =[pl.BlockSpec((None,) + s.shape[1:], lambda k, c_ref: (k, c_ref[0], 0)) for s in from_sibling]
    whole = [pl.BlockSpec((None,) + s.shape[1:], lambda k, c_ref: (k, 0, 0)) for s in from_sibling]
    return _pallas_call(
        body, name=name,
        grid_spec=pltpu.PrefetchScalarGridSpec(num_scalar_prefetch=1, grid=(N_CHIPS,), in_specs=mine + whole,
                                               out_specs=whole),
        out_shape=[_sds(s.shape, BF16) for s in from_sibling],
        compiler_params=_params(("parallel",), 48),
    )(c_idx, *grads3, *from_sibling)


def _split_start(name, arrays, n_sems, pairs_fn):
    n = len(arrays)

    def body(*refs):
        for send, _ in pairs_fn(refs[:n], refs[n], refs[n + 1]):
            send.start()
        refs[-1][...] = jnp.zeros_like(refs[-1])

    out = _pallas_call(
        body, name=name,
        in_specs=[HBM] * n, out_specs=tuple([SEM, SEM] + [HBM] * n + [VMEM]),
        out_shape=tuple([pltpu.SemaphoreType.DMA((n_sems,))] * 2 + [pltpu.HBM(a.shape, a.dtype) for a in arrays]
                        + [_sds((8, LANES), F32)]),
        input_output_aliases={i: 2 + i for i in range(n)},
        compiler_params=pltpu.CompilerParams(has_side_effects=EFFECT),
    )(*arrays)
    return (out[0], out[1]), list(out[2:2 + n]), out[-1]


def _split_wait(name, sems, arrays, pairs_fn, after):
    n = len(arrays)

    def body(*refs):
        for send, arrival in pairs_fn(refs[:n], refs[n], refs[n + 1]):
            send.wait_send()
            arrival.wait_recv()

    return list(_pallas_call(
        body, name=name,
        in_specs=[HBM] * n + [SEM, SEM, ANY], out_specs=[HBM] * n,
        out_shape=[pltpu.HBM(a.shape, a.dtype) for a in arrays],
        input_output_aliases={i: i for i in range(n)},
        compiler_params=pltpu.CompilerParams(has_side_effects=EFFECT),
    )(*arrays, sems[0], sems[1], after))


def _forward_pairs(bufs, send_sems, recv_sems):
    x, y, c, _, chips = _place()
    pairs = []
    for w, buf in enumerate(bufs):
        r2 = buf.shape[1] // 2
        for j, (cx, cy) in enumerate(chips):
            landed, theirs = _half(buf, 2 * cx + cy, c, r2), _half(buf, 2 * cx + cy, 1 - c, r2)
            sems = (send_sems.at[w * 3 + j], recv_sems.at[w * 3 + j])
            pairs.append((_remote(landed, landed, *sems, (x, y, 1 - c)), _remote(theirs, theirs, *sems, (x, y, c))))
    return pairs


def _sibling_pairs(arrays, send_sems, recv_sems):
    x, y, c, _, _ = _place()
    n = len(arrays) // 2
    pairs = []
    for w in range(n):
        r2 = arrays[w].shape[1] // 2
        cp = _remote(_half(arrays[w], slice(None), 1 - c, r2), arrays[n + w], send_sems.at[w], recv_sems.at[w],
                     (x, y, 1 - c))
        pairs.append((cp, cp))
    return pairs


def _sibling_whole_pairs(arrays, send_sems, recv_sems):
    x, y, c, _, _ = _place()
    n = len(arrays) // 2
    pairs = []
    for w in range(n):
        cp = _remote(arrays[w], arrays[n + w], send_sems.at[w], recv_sems.at[w], (x, y, 1 - c))
        pairs.append((cp, cp))
    return pairs


def _ici_pairs(arrays, send_sems, recv_sems):
    x, y, c, _, chips = _place()
    n = len(arrays) // 2
    pairs = []
    for w in range(n):
        for j, (cx, cy) in enumerate(chips):
            cp = _remote(arrays[w].at[2 * cx + cy], arrays[n + w].at[j],
                         send_sems.at[w * 3 + j], recv_sems.at[w * 3 + j], (cx, cy, c))
            pairs.append((cp, cp))
    return pairs


def _rs_sum(name, partials, received, place_idx):
    n = len(partials)
    nb = 2
    blocks = [(p.shape[1] // nb, p.shape[2]) for p in partials]

    def body(idx_ref, *refs):
        for p_ref, r_ref, o_ref in zip(refs[:n], refs[n:2 * n], refs[2 * n:]):
            o_ref[...] = ((p_ref[...].astype(F32) + r_ref[0].astype(F32))
                          + (r_ref[1].astype(F32) + r_ref[2].astype(F32)))

    return _pallas_call(
        body, name=name,
        grid_spec=pltpu.PrefetchScalarGridSpec(
            num_scalar_prefetch=1, grid=(nb,),
            in_specs=[pl.BlockSpec((None,) + b, lambda i, idx: (idx[0], i, 0)) for b in blocks]
            + [pl.BlockSpec((3,) + b, lambda i, idx: (0, i, 0)) for b in blocks],
            out_specs=[pl.BlockSpec(b, lambda i, idx: (idx[1] * nb + i, 0)) for b in blocks]),
        out_shape=[_sds((2 * p.shape[1], p.shape[2]), F32) for p in partials],
        compiler_params=_params(("parallel",), 48),
    )(place_idx, *partials, *received)


def _share_pairs(arrays, send_sems, recv_sems):
    x, y, c, _, _ = _place()
    pairs = []
    for w, arr in enumerate(arrays):
        r2 = arr.shape[0] // 2
        mine = arr.at[pl.ds(pl.multiple_of(c * r2, 8), r2), :]
        theirs = arr.at[pl.ds(pl.multiple_of((1 - c) * r2, 8), r2), :]
        sems = (send_sems.at[w], recv_sems.at[w])
        pairs.append((_remote(mine, mine, *sems, (x, y, 1 - c)), _remote(theirs, theirs, *sems, (x, y, c))))
    return pairs


def _small_pack(red_mix, red_ffn, red_final, red_hg, g_conv):
    D = red_mix.shape[1]
    H = red_hg.shape[1]

    def body(mix_ref, ffn_ref, fin_ref, hg_ref, cv_ref, in_ref):
        in_ref[...] = jnp.zeros_like(in_ref)
        in_ref[0:1, :] = mix_ref[0:1, :]
        in_ref[1:2, :] = ffn_ref[0:1, :]
        in_ref[2:3, :] = fin_ref[0:1, :]
        gam = hg_ref[1:2, 0:HEAD_DIM]
        for h in range(1, H // HEAD_DIM):
            gam = gam + hg_ref[1:2, h * HEAD_DIM:(h + 1) * HEAD_DIM]
        in_ref[3:4, 0:HEAD_DIM] = gam
        in_ref[3:4, HEAD_DIM:2 * HEAD_DIM] = fin_ref[1:2, 0:HEAD_DIM]
        in_ref[4:5, 0:H] = hg_ref[0:1, :]
        in_ref[6:9, 0:H] = cv_ref[...]

    return _pallas_call(
        body, name="small_pack", pin=False,
        in_specs=[VMEM] * 5, out_specs=VMEM, out_shape=_sds((N_SMALL_ROWS, D), F32),
    )(red_mix, red_ffn, red_final, red_hg, g_conv)


def _small_pairs(arrays, send_sems, recv_sems):
    block, gathered = arrays
    x, y, c, _, _ = _place()
    me = 4 * x + 2 * y + c
    pairs = []
    for m in range(1, 8):
        px, py, pc = x ^ ((m >> 2) & 1), y ^ ((m >> 1) & 1), c ^ (m & 1)
        sems = (send_sems.at[m - 1], recv_sems.at[m - 1])
        pairs.append((_remote(block, gathered.at[me], *sems, (px, py, pc)),
                      _remote(block, gathered.at[4 * px + 2 * py + pc], *sems, (x, y, c))))
    return pairs


def _adamw_math(w, g, m, v):
    m = ADAM_B1 * m + (1.0 - ADAM_B1) * g
    v = ADAM_B2 * v + (1.0 - ADAM_B2) * jnp.square(g)
    m_hat = m / (1.0 - ADAM_B1 ** ADAM_STEP)
    v_hat = v / (1.0 - ADAM_B2 ** ADAM_STEP)
    delta = -ADAM_LR * (m_hat / (jnp.sqrt(v_hat) + ADAM_EPS) + ADAM_WD * w)
    return delta, m, v


def _adamw(name, gs, ws, ms, vs):
    n = len(gs)
    nb = 4

    def body(*refs):
        ins, outs = refs[:4 * n], refs[4 * n:]
        for j in range(n):
            g_ref, w_ref, m_ref, v_ref = ins[j], ins[n + j], ins[2 * n + j], ins[3 * n + j]
            go_ref, d_ref, mo_ref, vo_ref = outs[4 * j:4 * j + 4]
            g = g_ref[...]
            go_ref[...] = g
            d_ref[...], mo_ref[...], vo_ref[...] = _adamw_math(w_ref[...], g, m_ref[...], v_ref[...])

    blk = [pl.BlockSpec((g.shape[0] // nb, g.shape[1]), lambda i: (i, 0)) for g in gs]
    out = _pallas_call(
        body, name=name, grid=(nb,),
        in_specs=blk * 4, out_specs=[b for b in blk for _ in range(4)],
        out_shape=[_sds(g.shape, F32) for g in gs for _ in range(4)],
        compiler_params=_params(("parallel",), 56),
    )(*gs, *ws, *ms, *vs)
    return [list(out[4 * j:4 * j + 4]) for j in range(n)]


def _small_update(block, gathered, place_idx, ws, ms, vs):
    n = len(ws)
    H = ws[1].shape[1]

    def body(idx_ref, blk_ref, all_ref, *refs):
        w, m, v, outs, tot_ref = refs[:n], refs[n:2 * n], refs[2 * n:3 * n], refs[3 * n:-1], refs[-1]
        chip, me = idx_ref[0], idx_ref[1]
        tot = jnp.where(me == 0, blk_ref[...], all_ref[0])
        for d in range(1, 8):
            tot = tot + jnp.where(me == d, blk_ref[...], all_ref[d])
        tot_ref[...] = tot
        p0 = _lower_bound(w[1][...])
        dl0 = p0 * (1.0 - p0) * tot_ref[4:5, 0:H]
        conv = jnp.zeros((3, LANES), F32)
        for k in range(N_CHIPS):
            conv = jnp.where(chip == k, tot_ref[6:9, k * LANES:(k + 1) * LANES], conv)
        grads = [tot_ref[0:1, :], None, tot_ref[3:4, 0:HEAD_DIM], conv, tot_ref[1:2, :], tot_ref[2:3, :]]
        for p in range(n):
            g_ref, d_ref, mo_ref, vo_ref = outs[4 * p:4 * p + 4]
            if p == 1:
                for row, g in ((slice(0, 1), dl0), (slice(1, 2), -dl0)):
                    g_ref[row, :] = g
                    d_ref[row, :], mo_ref[row, :], vo_ref[row, :] = _adamw_math(
                        w[p][row, :], g, m[p][row, :], v[p][row, :])
            else:
                g_ref[...] = grads[p]
                d_ref[...], mo_ref[...], vo_ref[...] = _adamw_math(w[p][...], grads[p], m[p][...], v[p][...])
        outs[4 * n][...] = tot_ref[3:4, HEAD_DIM:2 * HEAD_DIM]

    full = lambda a: pl.BlockSpec(a.shape, lambda i, idx: (0,) * a.ndim)
    out_shape = [_sds(w.shape, F32) for w in ws for _ in range(4)] + [_sds((1, LANES), F32)]
    return _pallas_call(
        body, name="small_update",
        grid_spec=pltpu.PrefetchScalarGridSpec(
            num_scalar_prefetch=1, grid=(1,),
            in_specs=[full(block), full(gathered)] + [full(a) for a in ws + ms + vs],
            out_specs=[full(s) for s in out_shape],
            scratch_shapes=[pltpu.VMEM(block.shape, F32)]),
        out_shape=out_shape,
    )(place_idx, block, gathered, *ws, *ms, *vs)


def kernel(x, norm_mix_g, w_in, lower_bounds, hg_norm_g, conv_w, w_branch_a, w_branch_b, w_out, norm_ffn_g, w_ffn_gate, w_ffn_up, w_ffn_down, norm_final_g, loss_target, m_norm_mix_g, m_w_in, m_lower_bounds, m_hg_norm_g, m_conv_w, m_w_branch_a, m_w_branch_b, m_w_out, m_norm_ffn_g, m_w_ffn_gate, m_w_ffn_up, m_w_ffn_down, m_norm_final_g, v_norm_mix_g, v_w_in, v_lower_bounds, v_hg_norm_g, v_conv_w, v_w_branch_a, v_w_branch_b, v_w_out, v_norm_ffn_g, v_w_ffn_gate, v_w_ffn_up, v_w_ffn_down, v_norm_final_g):
    _, L, D = x.shape
    H = D // 2
    assert lower_bounds.shape == (2, H) and hg_norm_g.shape == (1, HEAD_DIM)
    assert conv_w.shape == (1, 3, LANES) and w_in.shape[2] * N_CHIPS == 11 * H
    x2d, target = x.reshape(L, D), loss_target.reshape(L, D)
    g_final = norm_final_g.reshape(1, D)
    chip = 2 * lax.axis_index("x") + lax.axis_index("y")
    core = lax.axis_index("c")

    tr = lambda w: jnp.transpose(w[0])
    big = [w_in[0], w_branch_a[0], w_branch_b[0], w_out[0], tr(w_ffn_gate), tr(w_ffn_up), w_ffn_down[0]]
    big_m = [m_w_in[0], m_w_branch_a[0], m_w_branch_b[0], m_w_out[0], tr(m_w_ffn_gate), tr(m_w_ffn_up),
             m_w_ffn_down[0]]
    big_v = [v_w_in[0], v_w_branch_a[0], v_w_branch_b[0], v_w_out[0], tr(v_w_ffn_gate), tr(v_w_ffn_up),
             v_w_ffn_down[0]]
    names = ["w_in", "w_branch_a", "w_branch_b", "w_out", "w_ffn_gate", "w_ffn_up", "w_ffn_down"]

    chip_idx = chip.reshape(1).astype(jnp.int32)
    def per_shape(fn, tag, js, *lists):
        groups = {}
        for pos, a in enumerate(lists[0]):
            groups.setdefault(a.shape, []).append(pos)
        results = [None] * len(js)
        for same in groups.values():
            out = fn(tag + names[js[same[0]]], *[[xs[p] for p in same] for xs in lists])
            for q, p in enumerate(same):
                results[p] = out[q]
        return results

    place_t = lambda name, ws: _cast_place_t(name, ws, chip_idx)
    placed = per_shape(place_t, "place_", [0, 1, 2], big[:3]) + list(_cast_place("place_rest", big[3:], chip_idx))
    conv_placed = lax.dynamic_update_slice(jnp.zeros((N_CHIPS, 3, LANES), F32), conv_w, (chip, 0, 0))
    x_i, y_i = lax.axis_index("x"), lax.axis_index("y")
    blocks = lambda *ks: jnp.stack(ks).astype(jnp.int32)
    near = lambda w, j: j < 2
    far = lambda w, j: w == 1 or j == 2
    near_sems, in_flight, _ = _gather_start("gather_start_near", [([placed[0]], set(), near)], chip_idx)
    w_in_buf = in_flight[0][0]
    h, proj = _fwd_proj_first(x2d, norm_mix_g, w_in_buf, blocks(chip), placed[-1])
    sems, in_flight, _ = _gather_start(
        "gather_start_rest", [([w_in_buf, conv_placed], {1}, far), (placed[1:4], set(), None),
                              (placed[4:], set(), None)], h)
    w_in_buf, conv_buf = in_flight[0]
    (w_in_buf,) = _gather_wait("gather_wait_in_near", [w_in_buf], set(), near_sems[0], h, near)
    (w_in_buf,) = _gather_forward("gather_fwd_in_near", [w_in_buf], (0, 1))
    proj = _fwd_proj_more("fwd_proj_near", h, w_in_buf, proj,
                          blocks(2 * (1 - x_i) + y_i, 2 * x_i + (1 - y_i)))
    w_in_buf, conv_all = _gather_wait("gather_wait_in_far", [w_in_buf, conv_buf], {1}, sems[0], proj, far)
    (w_int3,) = _gather_forward("gather_fwd_in_far", [w_in_buf], (2,))
    proj = _fwd_proj_more("fwd_proj_far", h, w_int3, proj, blocks(2 * (1 - x_i) + (1 - y_i)))
    w_int = w_int3.reshape(-1, D)
    conv_full = jnp.transpose(conv_all, (1, 0, 2)).reshape(3, H)
    og, o_pre, s_saved = _hgrn_fwd(proj, lower_bounds, hg_norm_g, H)
    landed = _gather_wait("gather_wait_mix", in_flight[1], set(), sems[1], og)
    fwd_sems, landed, token = _split_start("gather_fwd_mix_start", landed, 9, _forward_pairs)
    cb = _conv_fwd(proj, conv_full, H, token)
    wat3, wbt3, wout3 = _split_wait("gather_fwd_mix_wait", fwd_sems, landed, _forward_pairs, cb)
    wat, wbt, wout = wat3.reshape(D, H), wbt3.reshape(D, H), wout3.reshape(D, D)
    landed = _gather_wait("gather_wait_ffn", in_flight[2], set(), sems[2], cb)
    fwd_sems, landed, token = _split_start("gather_fwd_ffn_start", landed, 9, _forward_pairs)
    sig_a, sig_b, dm_dga, dm_dgb, merged, x1, h2 = _fwd_mix(og, cb, proj, x2d, wat, wbt, wout, norm_ffn_g,
                                                              H, token)
    wgt3, wut3, wd3 = _split_wait("gather_fwd_ffn_wait", fwd_sems, landed, _forward_pairs, h2)
    d_ff = N_CHIPS * wd3.shape[1]
    wgt, wut, wd = wgt3.reshape(d_ff, D), wut3.reshape(d_ff, D), wd3.reshape(d_ff, D)
    ffn_ds_da, ffn_ds_db, ffn_s = _fwd_ffn_up(h2, wgt, wut)
    dx2, dx2b, red_final = _fwd_down_loss(ffn_s, wd, x1, target, g_final)

    c_idx = core.reshape(1).astype(jnp.int32)
    place_idx = jnp.stack([chip, core]).astype(jnp.int32)

    def sibling_start(tag, grads):
        bufs = [lax.empty((N_CHIPS, g.shape[1] // 2, g.shape[2]), F32) for g in grads]
        return _split_start("rs_sibling_start_" + tag, list(grads) + bufs, len(grads), _sibling_pairs)

    def ici_start(tag, js, grads, from_sibling):
        partials = list(_rs_add("rs_add_" + tag, grads, from_sibling, c_idx))
        landings = [lax.empty((3,) + p.shape[1:], BF16) for p in partials]
        return _split_start("rs_ici_start_" + tag, partials + landings, 3 * len(js), _ici_pairs)

    def ici_start_behind(tag, js, started, after):
        n = len(js)
        arrays = _split_wait("rs_sibling_wait_" + tag, started[0], started[1], _sibling_pairs, after)
        return ici_start(tag, js, arrays[:n], arrays[n:])

    def sums(tag, started, after):
        partials, received = [], []
        for group, group_js, start in started:
            arrays = _split_wait("rs_ici_wait_" + group, start[0], start[1], _ici_pairs, after)
            partials += arrays[:len(group_js)]
            received += arrays[len(group_js):]
        return list(_rs_sum("rs_sum_" + tag, partials, received, place_idx))

    def adamw(tag, js, grads):
        return _adamw("adamw_" + tag, grads, *[[src[j] for j in js] for src in (big, big_m, big_v)])

    shards3 = lambda g: g.reshape(N_CHIPS, d_ff // N_CHIPS, D)
    da, db = _bwd_down(dx2b, wd, ffn_ds_da, ffn_ds_db)
    g_wd = shards3(_dw_rows2("dw_ffn_down", ffn_s, dx2b))
    g_wg = shards3(_dw_rows2("dw_ffn_gate", da, h2))
    g_wu = shards3(_dw_rows2("dw_ffn_up", db, h2))
    ffn_sibling = sibling_start("ffn", [g_wg, g_wu, g_wd])
    dx1, dx1b, red_ffn = _bwd_ffn_dh(da, db, wgt, wut, x1, dx2, norm_ffn_g, ffn_sibling[2])
    ffn_ici = ici_start_behind("ffn", [4, 5, 6], ffn_sibling, dx1b)
    dya, dyb, dproj, d_o, d_cb = _bwd_mix(dx1b, sig_a, sig_b, dm_dga, dm_dgb, wat, wbt, wout, H, ffn_ici[2])
    (g_wout,) = _dw_whole("dw_out", [(merged, dx1b)], True)
    g_wa, g_wb = _dw_whole("dw_branch", [(og, dya), (cb, dyb)], False)
    mix_sibling = sibling_start("mix", [g_wa, g_wb, g_wout])
    dproj, red_hg = _hgrn_bwd(proj, lower_bounds, hg_norm_g, o_pre, d_o, s_saved, H, mix_sibling[2], dproj)
    mix_ici = ici_start_behind("mix", [1, 2, 3], mix_sibling, red_hg)
    dproj, g_conv = _conv_bwd(proj, conv_full, d_cb, H, mix_ici[2], dproj)
    g_win, for_sibling = _dw_in(h, dproj, w_int3.shape[1], c_idx)
    in_sibling = _split_start("rs_sibling_start_in", [for_sibling, lax.empty(for_sibling.shape, BF16)], 1,
                              _sibling_whole_pairs)
    halves = sums("rest", [("mix", [1, 2, 3], mix_ici), ("ffn", [4, 5, 6], ffn_ici)], in_sibling[2])
    rest_share = _split_start("rs_share_start_rest", halves, len(halves), _share_pairs)
    from_sibling = _split_wait("rs_sibling_wait_in", in_sibling[0], in_sibling[1], _sibling_whole_pairs,
                               rest_share[2])[1]
    in_ici = ici_start("in", [0], [g_win], [from_sibling])
    grad_x, red_mix = _bwd_in(dproj, w_int, x2d, dx1, norm_mix_g, in_ici[2])
    in_share = _split_start("rs_share_start_in", sums("in", [("in", [0], in_ici)], grad_x), 1, _share_pairs)
    small_block = _small_pack(red_mix, red_ffn, red_final, red_hg, g_conv)
    small = _split_start("small_gather_start", [small_block, lax.empty((8,) + small_block.shape, F32)], 7,
                         _small_pairs)
    rest_grads = _split_wait("rs_share_wait_rest", rest_share[0], rest_share[1], _share_pairs, small[2])
    big_out = [None] + adamw("rest", [1, 2, 3, 4, 5, 6], rest_grads)
    in_grad = _split_wait("rs_share_wait_in", in_share[0], in_share[1], _share_pairs, big_out[6][0])
    big_out[0] = adamw("in", [0], in_grad)[0]
    small_block, small_all = _split_wait("small_gather_wait", small[0], small[1], _small_pairs, big_out[0][0])

    def smalls(mix, lb, hg, cw, ffn, fin):
        return [mix, lb, hg, cw[0], ffn, fin.reshape(1, D)]

    small_out = _small_update(
        small_block, small_all, jnp.stack([chip, 4 * x_i + 2 * y_i + core]).astype(jnp.int32),
        smalls(norm_mix_g, lower_bounds, hg_norm_g, conv_w, norm_ffn_g, norm_final_g),
        smalls(m_norm_mix_g, m_lower_bounds, m_hg_norm_g, m_conv_w, m_norm_ffn_g, m_norm_final_g),
        smalls(v_norm_mix_g, v_lower_bounds, v_hg_norm_g, v_conv_w, v_norm_ffn_g, v_norm_final_g))

    def outputs(i):
        big_i = [big_out[j][i] for j in range(7)]
        mix, lb, hg, cw, ffn, fin = [small_out[4 * p + i] for p in range(6)]
        return [mix, big_i[0][None], lb, hg, cw[None], big_i[1][None], big_i[2][None], big_i[3][None], ffn,
                big_i[4].T[None], big_i[5].T[None], big_i[6][None], fin.reshape(D)]

    outs = [small_out[24][0, 0], grad_x.reshape(1, L, D)]
    for i in range(4):
        outs += outputs(i)
    return tuple(outs)
```

```python
import jax
import jax.numpy as jnp
from jax import lax
from jax.experimental import pallas as pl
from jax.experimental.pallas import tpu as pltpu

F32 = jnp.float32
BF16 = jnp.bfloat16
EPS = 1e-6
CHUNK = 32
HEAD_DIM = 128
LANES = 128
N_CHIPS = 4
N_SMALL_ROWS = 16
DPROJ_BLOCKS = 12
DPROJ_BLOCK_OF = (0, 1, 2, 3, 8, 9, 10, 4, 5, 6, 7)

ADAM_LR = 0.001
ADAM_B1 = 0.9
ADAM_B2 = 0.999
ADAM_EPS = 1e-08
ADAM_WD = 0.01
ADAM_STEP = 10

MESH = pl.DeviceIdType.MESH
ANY = pl.BlockSpec(memory_space=pl.ANY)
VMEM = pl.BlockSpec(memory_space=pltpu.VMEM)
HBM = pl.BlockSpec(memory_space=pltpu.HBM)
SEM = pl.BlockSpec(memory_space=pltpu.SEMAPHORE)
EFFECT = pltpu.SideEffectType.DATAFLOW_SIDE_EFFECTING


def _sds(shape, dtype):
    return jax.ShapeDtypeStruct(shape, dtype)


def _pallas_call(body, pin=True, **kwargs):
    if not pin:
        return pl.pallas_call(body, **kwargs)
    in_hbm = lambda s: pltpu.HBM(s.shape, s.dtype) if isinstance(s, jax.ShapeDtypeStruct) else s
    kwargs["out_shape"] = jax.tree.map(in_hbm, kwargs["out_shape"])
    call = pl.pallas_call(body, **kwargs)

    def run(*args):
        return call(*[pltpu.with_memory_space_constraint(a, pltpu.HBM) if a.dtype in (F32, BF16) else a
                      for a in args])

    return run


def _params(semantics, vmem_mb):
    return pltpu.CompilerParams(dimension_semantics=semantics, vmem_limit_bytes=vmem_mb << 20)


def _nn(a, b):
    return lax.dot_general(a, b, (((1,), (0,)), ((), ())), preferred_element_type=F32)


def _nt(a, b):
    return lax.dot_general(a, b, (((1,), (1,)), ((), ())), preferred_element_type=F32)


def _tn(a, b):
    return lax.dot_general(a, b, (((0,), (0,)), ((), ())), preferred_element_type=F32)


def _sigmoid(x):
    return jax.nn.sigmoid(x)


def _rms_stats(x):
    r = lax.rsqrt(jnp.mean(x * x, axis=-1, keepdims=True) + EPS)
    return r, x * r


def _rms_bwd(dxh, xh, r):
    return r * (dxh - xh * jnp.mean(dxh * xh, axis=-1, keepdims=True))


def _fwd_proj_first(x, g_mix, w_int3, block, after):
    L, D = x.shape
    tn = w_int3.shape[1]
    tm = min(L, 1024)

    def body(blk_ref, x_ref, g_ref, w_ref, after_ref, h_ref, p_ref):
        _, xh = _rms_stats(x_ref[...])
        h = (xh * g_ref[...]).astype(BF16)
        h_ref[...] = h
        p_ref[...] = _nt(h, w_ref[...])

    return _pallas_call(
        body, name="fwd_proj_own",
        grid_spec=pltpu.PrefetchScalarGridSpec(
            num_scalar_prefetch=1, grid=(L // tm,),
            in_specs=[pl.BlockSpec((tm, D), lambda i, blk: (i, 0)),
                      pl.BlockSpec((1, D), lambda i, blk: (0, 0)),
                      pl.BlockSpec((None, tn, D), lambda i, blk: (blk[0], 0, 0)), ANY],
            out_specs=[pl.BlockSpec((tm, D), lambda i, blk: (i, 0)),
                       pl.BlockSpec((tm, tn), lambda i, blk: (i, blk[0]))]),
        out_shape=[_sds((L, D), BF16), _sds((L, N_CHIPS * tn), F32)],
        compiler_params=_params(("parallel",), 48),
    )(block, x, g_mix, w_int3, after)


def _fwd_proj_more(name, h, w_int3, proj, blocks):
    L, D = h.shape
    tn = w_int3.shape[1]
    tm = min(L, 1024)

    def body(blk_ref, h_ref, w_ref, proj_ref, p_ref):
        p_ref[...] = _nt(h_ref[...], w_ref[...])

    return _pallas_call(
        body, name=name,
        grid_spec=pltpu.PrefetchScalarGridSpec(
            num_scalar_prefetch=1, grid=(L // tm, blocks.shape[0]),
            in_specs=[pl.BlockSpec((tm, D), lambda i, j, blk: (i, 0)),
                      pl.BlockSpec((None, tn, D), lambda i, j, blk: (blk[j], 0, 0)), ANY],
            out_specs=pl.BlockSpec((tm, tn), lambda i, j, blk: (i, blk[j]))),
        out_shape=_sds(proj.shape, proj.dtype),
        input_output_aliases={3: 0},
        compiler_params=_params(("parallel", "arbitrary"), 48),
    )(blocks, h, w_int3, proj)


def _lower_bound(lbp):
    l0, l1 = lbp[0:1, :], lbp[1:2, :]
    m = jnp.maximum(l0, l1)
    e0, e1 = jnp.exp(l0 - m), jnp.exp(l1 - m)
    return e0 / (e0 + e1)


def _seg_scan(x, r32, forward):
    n = x.shape[0]
    s = 1
    while s < CHUNK:
        if forward:
            x = x + jnp.where(r32 >= s, pltpu.roll(x, s, 0), 0.0)
        else:
            x = x + jnp.where(r32 < CHUNK - s, pltpu.roll(x, n - s, 0), 0.0)
        s *= 2
    return x


def _bcast_row(x, row):
    n, w = x.shape
    nc = n // CHUNK
    x3 = x.reshape(nc, CHUNK, w)
    return jnp.broadcast_to(x3[:, row:row + 1, :], (nc, CHUNK, w)).reshape(n, w)


def _chunk_total(x):
    n, w = x.shape
    nc = n // CHUNK
    total = jnp.sum(x.reshape(nc, CHUNK, w), axis=1, keepdims=True)
    return jnp.broadcast_to(total, (nc, CHUNK, w)).reshape(n, w)


def _hgrn_prep(q_raw, f_raw, lb):
    r32 = lax.broadcasted_iota(jnp.int32, f_raw.shape, 0) & (CHUNK - 1)
    sig = _sigmoid(f_raw)
    f = lb + (1.0 - lb) * sig
    b = _seg_scan(jnp.log(f), r32, True)
    a = _bcast_row(b, CHUNK // 2 - 1)
    bl = _bcast_row(b, CHUNK - 1)
    sq = _sigmoid(q_raw)
    q = q_raw * sq * (HEAD_DIM ** -0.5)
    return dict(r32=r32, sig=sig, f=f, k=1.0 - f, b=b, a=a, bl=bl, sq=sq, q=q)


def _chunk_masks(n):
    ri = lax.broadcasted_iota(jnp.int32, (n, n), 0)
    ci = lax.broadcasted_iota(jnp.int32, (n, n), 1)
    same = (ri // CHUNK) == (ci // CHUNK)
    return same & (ci <= ri), same & (ri <= ci)


def _hgrn_fwd(proj, lower_bounds, gamma, H):
    L = proj.shape[0]
    nh = H // HEAD_DIM
    TL = min(L, 256)
    nc = TL // CHUNK

    def body(q_ref, f_ref, v_ref, g_ref, lbp_ref, gam_ref, og_ref, o_ref, s_ref, st_ref):
        @pl.when(pl.program_id(0) == 0)
        def _():
            st_ref[...] = jnp.zeros_like(st_ref)

        lb = _lower_bound(lbp_ref[...])
        gam = gam_ref[...]
        mask, _ = _chunk_masks(TL)
        rowc = lax.broadcasted_iota(jnp.int32, (TL, HEAD_DIM), 0) // CHUNK
        for h in range(nh):
            hs = slice(h * HEAD_DIM, (h + 1) * HEAD_DIM)
            p = _hgrn_prep(q_ref[:, hs], f_ref[:, hs], lb[:, hs])
            v = v_ref[:, hs]
            vb = v.astype(BF16)
            vt = v.T.astype(BF16)
            q_hat = (p["q"] * jnp.exp(p["b"] - p["a"])).astype(BF16)
            k_hat = (p["k"] * jnp.exp(p["a"] - p["b"])).astype(BF16)
            q_in = (p["q"] * jnp.exp(p["b"])).astype(BF16)
            k_out = (p["k"] * jnp.exp(p["bl"] - p["b"])).astype(BF16)
            dec = jnp.exp(p["bl"])
            att = jnp.where(mask, _nt(q_hat, k_hat), 0.0).astype(BF16)
            o_intra = _nn(att, vb)
            st = st_ref[h]
            for c in range(nc):
                rs = slice(c * CHUNK, (c + 1) * CHUNK)
                stb = st.astype(BF16)
                s_ref[c, h] = stb
                o_ref[rs, hs] = o_intra[rs] + _nt(q_in[rs], stb)
                k_c = jnp.where(rowc == c, k_out, jnp.zeros_like(k_out))
                st = st * dec[c * CHUNK:c * CHUNK + 1, :] + _nn(vt, k_c)
            st_ref[h] = st
            o = o_ref[:, hs]
            _, xh = _rms_stats(o)
            gr = g_ref[:, hs]
            og_ref[:, hs] = (xh * gam * (gr * _sigmoid(gr))).astype(BF16)

    col = lambda k: pl.BlockSpec((TL, H), lambda i, k=k: (i, k))
    return _pallas_call(
        body, name="hgrn_fwd", grid=(L // TL,),
        in_specs=[col(0), col(1), col(2), col(3),
                  pl.BlockSpec(lower_bounds.shape, lambda i: (0, 0)),
                  pl.BlockSpec(gamma.shape, lambda i: (0, 0))],
        out_specs=[pl.BlockSpec((TL, H), lambda i: (i, 0)),
                   pl.BlockSpec((TL, H), lambda i: (i, 0)),
                   pl.BlockSpec((nc, nh, HEAD_DIM, HEAD_DIM), lambda i: (i, 0, 0, 0))],
        out_shape=[_sds((L, H), BF16), _sds((L, H), F32),
                   _sds((L // CHUNK, nh, HEAD_DIM, HEAD_DIM), BF16)],
        scratch_shapes=[pltpu.VMEM((nh, HEAD_DIM, HEAD_DIM), F32)],
        compiler_params=_params(("arbitrary",), 48),
    )(proj, proj, proj, proj, lower_bounds, gamma)


def _hgrn_bwd(proj, lower_bounds, gamma, o_pre, d_out, s_saved, H, after, dproj):
    L = proj.shape[0]
    nh = H // HEAD_DIM
    TL = min(L, 256)
    nc = TL // CHUNK
    nt = L // TL

    def body(q_ref, f_ref, v_ref, g_ref, lbp_ref, gam_ref, o_ref, d_ref, s_ref, after_ref, dproj_ref,
             dp_ref, red_ref, dst_ref, dsall_ref, tmp_ref):
        @pl.when(pl.program_id(0) == 0)
        def _():
            dst_ref[...] = jnp.zeros_like(dst_ref)
            red_ref[...] = jnp.zeros_like(red_ref)

        lb = _lower_bound(lbp_ref[...])
        gam = gam_ref[...]
        mask, mask_t = _chunk_masks(TL)
        rowc = lax.broadcasted_iota(jnp.int32, (TL, HEAD_DIM), 0) // CHUNK
        for h in range(nh):
            hs = slice(h * HEAD_DIM, (h + 1) * HEAD_DIM)
            qr, gr, lbh = q_ref[:, hs], g_ref[:, hs], lb[:, hs]
            p = _hgrn_prep(qr, f_ref[:, hs], lbh)
            vb = v_ref[:, hs].astype(BF16)
            eba, eab = jnp.exp(p["b"] - p["a"]), jnp.exp(p["a"] - p["b"])
            eb, elb = jnp.exp(p["b"]), jnp.exp(p["bl"] - p["b"])
            dec = jnp.exp(p["bl"])
            q_hat, k_hat = p["q"] * eba, p["k"] * eab
            q_in, k_out = p["q"] * eb, p["k"] * elb
            q_hat_b, k_hat_b = q_hat.astype(BF16), k_hat.astype(BF16)
            q_in_b, k_out_b = q_in.astype(BF16), k_out.astype(BF16)

            o, dout = o_ref[:, hs], d_ref[:, hs]
            sg = _sigmoid(gr)
            r, xh = _rms_stats(o)
            dp_ref[3, :, hs] = (dout * (xh * gam) * (sg * (1.0 + gr * (1.0 - sg)))).astype(BF16)
            dn = dout * (gr * sg)
            red_ref[1:2, hs] += jnp.sum(dn * xh, axis=0, keepdims=True)
            do = _rms_bwd(dn * gam, xh, r)
            dob = do.astype(BF16)
            dot_b = do.T.astype(BF16)

            att_t = jnp.where(mask_t, _nt(k_hat_b, q_hat_b), 0.0).astype(BF16)
            dv_intra = _nn(att_t, dob)
            datt = jnp.where(mask, _nt(dob, vb), 0.0).astype(BF16)
            dqh = _nn(datt, k_hat_b)
            datt_t = jnp.where(mask_t, _nt(vb, dob), 0.0).astype(BF16)
            dkh = _nn(datt_t, q_hat_b)

            dst = dst_ref[h]
            for c in reversed(range(nc)):
                dsall_ref[c] = dst
                q_c = jnp.where(rowc == c, q_in_b, jnp.zeros_like(q_in_b))
                dst = dst * dec[c * CHUNK:c * CHUNK + 1, :] + _nn(dot_b, q_c)
            dst_ref[h] = dst
            for c in range(nc):
                rs = slice(c * CHUNK, (c + 1) * CHUNK)
                ds_c = dsall_ref[c]
                dsb = ds_c.astype(BF16)
                st_prev = s_ref[c, h]
                tmp_ref[0, rs, :] = _nt(k_out_b[rs], dsb)
                tmp_ref[1, rs, :] = _nn(vb[rs], dsb)
                tmp_ref[2, rs, :] = _nn(dob[rs], st_prev)
                ddec = jnp.sum(ds_c * st_prev.astype(F32), axis=0, keepdims=True)
                tmp_ref[3, rs, :] = jnp.broadcast_to(ddec * dec[c * CHUNK:c * CHUNK + 1, :],
                                                     (CHUNK, HEAD_DIM))
            dko, dqi = tmp_ref[1], tmp_ref[2]
            dq = dqh * eba + dqi * eb
            dk = dkh * eab + dko * elb
            tko = dko * k_out
            db = dqh * q_hat - dkh * k_hat + dqi * q_in - tko
            dlog = _seg_scan(db, p["r32"], False) + _chunk_total(tko) + tmp_ref[3]
            df = dlog / p["f"] - dk
            sig = p["sig"]
            red_ref[0:1, hs] += jnp.sum(df * (1.0 - sig), axis=0, keepdims=True)
            dp_ref[1, :, hs] = (df * (1.0 - lbh) * sig * (1.0 - sig)).astype(BF16)
            sq = p["sq"]
            dp_ref[0, :, hs] = (dq * (HEAD_DIM ** -0.5) * (sq * (1.0 + qr * (1.0 - sq)))).astype(BF16)
            dp_ref[2, :, hs] = (dv_intra + tmp_ref[0]).astype(BF16)

    col = lambda k: pl.BlockSpec((TL, H), lambda i, k=k: (nt - 1 - i, k))
    rev = pl.BlockSpec((TL, H), lambda i: (nt - 1 - i, 0))
    return _pallas_call(
        body, name="hgrn_bwd", grid=(nt,),
        in_specs=[col(0), col(1), col(2), col(3),
                  pl.BlockSpec(lower_bounds.shape, lambda i: (0, 0)),
                  pl.BlockSpec(gamma.shape, lambda i: (0, 0)),
                  rev, rev,
                  pl.BlockSpec((nc, nh, HEAD_DIM, HEAD_DIM), lambda i: (nt - 1 - i, 0, 0, 0)), ANY, ANY],
        out_specs=[pl.BlockSpec((4, TL, H), lambda i: (0, nt - 1 - i, 0)), pl.BlockSpec((8, H), lambda i: (0, 0))],
        out_shape=[_sds(dproj.shape, BF16), _sds((8, H), F32)],
        input_output_aliases={10: 0},
        scratch_shapes=[pltpu.VMEM((nh, HEAD_DIM, HEAD_DIM), F32),
                        pltpu.VMEM((nc, HEAD_DIM, HEAD_DIM), F32),
                        pltpu.VMEM((4, TL, HEAD_DIM), F32)],
        compiler_params=_params(("arbitrary",), 48),
    )(proj, proj, proj, proj, lower_bounds, gamma, o_pre, d_out, s_saved, after, dproj)


def _shift_down(u, s, row):
    return jnp.where(row >= s, pltpu.roll(u, s, 0), 0.0)


def _shift_up(u, s, row):
    n = u.shape[0]
    return jnp.where(row < n - s, pltpu.roll(u, n - s, 0), 0.0)


def _conv_specs(L, H):
    per = H // LANES
    return [pl.BlockSpec((L, LANES), lambda j, o=o: (0, o * per + j)) for o in (4, 5, 6)]


def _conv_fwd(proj, conv_w, H, after):
    L = proj.shape[0]

    def body(c_ref, b_ref, x_ref, w_ref, after_ref, o_ref):
        row = lax.broadcasted_iota(jnp.int32, (L, LANES), 0)
        u = c_ref[...] * x_ref[...]
        w = w_ref[...]
        y = w[0:1] * _shift_down(u, 2, row) + w[1:2] * _shift_down(u, 1, row) + w[2:3] * u
        o_ref[...] = (b_ref[...] * y).astype(BF16)

    return _pallas_call(
        body, name="conv_fwd", grid=(H // LANES,),
        in_specs=_conv_specs(L, H) + [pl.BlockSpec((3, LANES), lambda j: (0, j)), ANY],
        out_specs=pl.BlockSpec((L, LANES), lambda j: (0, j)),
        out_shape=_sds((L, H), BF16),
        compiler_params=_params(("parallel",), 48),
    )(proj, proj, proj, conv_w, after)


def _conv_bwd(proj, conv_w, dcb, H, after, dproj):
    L = proj.shape[0]

    def body(c_ref, b_ref, x_ref, w_ref, d_ref, after_ref, dproj_ref, dp_ref, dw_ref):
        row = lax.broadcasted_iota(jnp.int32, (L, LANES), 0)
        cg, xb = c_ref[...], x_ref[...]
        u = cg * xb
        u1, u2 = _shift_down(u, 1, row), _shift_down(u, 2, row)
        w = w_ref[...]
        y = w[0:1] * u2 + w[1:2] * u1 + w[2:3] * u
        d = d_ref[...]
        dp_ref[1] = (d * y).astype(BF16)
        dy = d * b_ref[...]
        du = w[2:3] * dy + w[1:2] * _shift_up(dy, 1, row) + w[0:1] * _shift_up(dy, 2, row)
        dw_ref[0:1, :] = jnp.sum(dy * u2, axis=0, keepdims=True)
        dw_ref[1:2, :] = jnp.sum(dy * u1, axis=0, keepdims=True)
        dw_ref[2:3, :] = jnp.sum(dy * u, axis=0, keepdims=True)
        dp_ref[0] = (du * xb).astype(BF16)
        dp_ref[2] = (du * cg).astype(BF16)
        dp_ref[3] = jnp.zeros((L, LANES), BF16)

    blk = pl.BlockSpec((L, LANES), lambda j: (0, j))
    return _pallas_call(
        body, name="conv_bwd", grid=(H // LANES,),
        in_specs=_conv_specs(L, H) + [pl.BlockSpec((3, LANES), lambda j: (0, j)), blk, ANY, ANY],
        out_specs=[pl.BlockSpec((4, L, LANES), lambda j: (2, 0, j)), pl.BlockSpec((3, LANES), lambda j: (0, j))],
        out_shape=[_sds(dproj.shape, BF16), _sds((3, H), F32)],
        input_output_aliases={6: 0},
        compiler_params=_params(("parallel",), 56),
    )(proj, proj, proj, conv_w, dcb, after, dproj)


def _gate_specs(tm, H):
    return [pl.BlockSpec((tm, H), lambda i, k=k: (i, k)) for k in (7, 8, 9, 10)]


def _fwd_mix(og, cb, proj, x, wat, wbt, wout, g_ffn, H, after):
    L, D = x.shape
    tm = min(L, 512)

    def body(o_ref, cb_ref, ga0, ga1, gb0, gb1, x_ref, wa_ref, wb_ref, wo_ref, g_ref, after_ref,
             sa_ref, sb_ref, ta_ref, tb_ref, m_ref, x1_ref, h2_ref):
        ya, yb = _nt(o_ref[...], wa_ref[...]), _nt(cb_ref[...], wb_ref[...])
        for k, (gar, gbr) in enumerate(((ga0, gb0), (ga1, gb1))):
            cs = slice(k * H, (k + 1) * H)
            sa, sb = _sigmoid(gar[...]), _sigmoid(gbr[...])
            ma, mb = sa * ya[:, cs], sb * yb[:, cs]
            m_ref[:, cs] = (ma + mb).astype(BF16)
            sa_ref[:, cs] = sa.astype(BF16)
            sb_ref[:, cs] = sb.astype(BF16)
            ta_ref[:, cs] = (ma * (1.0 - sa)).astype(BF16)
            tb_ref[:, cs] = (mb * (1.0 - sb)).astype(BF16)
        x1 = x_ref[...] + _nn(m_ref[...], wo_ref[...])
        x1_ref[...] = x1
        _, xh = _rms_stats(x1)
        h2_ref[...] = (xh * g_ref[...]).astype(BF16)

    row = lambda w: pl.BlockSpec((tm, w), lambda i: (i, 0))
    full = lambda a: pl.BlockSpec(a.shape, lambda i: (0,) * a.ndim)
    return _pallas_call(
        body, name="fwd_mix", grid=(L // tm,),
        in_specs=[row(H), row(H)] + _gate_specs(tm, H) + [row(D), full(wat), full(wbt), full(wout),
                                                           full(g_ffn), ANY],
        out_specs=[row(D)] * 7,
        out_shape=[_sds((L, D), BF16)] * 5 + [_sds((L, D), F32), _sds((L, D), BF16)],
        compiler_params=_params(("parallel",), 56),
    )(og, cb, proj, proj, proj, proj, x, wat, wbt, wout, g_ffn, after)


def _bwd_mix(dx1b, sig_a, sig_b, dm_dga, dm_dgb, wat, wbt, wout, H, after):
    L, D = dx1b.shape
    tm = min(L, 512)

    def body(dx_ref, sa_ref, sb_ref, ta_ref, tb_ref, wa_ref, wb_ref, wo_ref, after_ref,
             dya_ref, dyb_ref, dgate_ref, do_ref, dcb_ref):
        dm = _nt(dx_ref[...], wo_ref[...])
        dga = (dm * ta_ref[...].astype(F32)).astype(BF16)
        dgb = (dm * tb_ref[...].astype(F32)).astype(BF16)
        for q, part in enumerate((dga[:, 0:H], dga[:, H:D], dgb[:, 0:H], dgb[:, H:D])):
            dgate_ref[q] = part
        dya_ref[...] = (dm * sa_ref[...].astype(F32)).astype(BF16)
        dyb_ref[...] = (dm * sb_ref[...].astype(F32)).astype(BF16)
        do_ref[...] = _nn(dya_ref[...], wa_ref[...])
        dcb_ref[...] = _nn(dyb_ref[...], wb_ref[...])

    row = lambda w: pl.BlockSpec((tm, w), lambda i: (i, 0))
    full = lambda a: pl.BlockSpec(a.shape, lambda i: (0,) * a.ndim)
    return _pallas_call(
        body, name="bwd_mix", grid=(L // tm,),
        in_specs=[row(D)] * 5 + [full(wat), full(wbt), full(wout), ANY],
        out_specs=[row(D), row(D), pl.BlockSpec((4, tm, H), lambda i: (1, i, 0)), row(H), row(H)],
        out_shape=[_sds((L, D), BF16)] * 2 + [_sds((DPROJ_BLOCKS, L, H), BF16)] + [_sds((L, H), F32)] * 2,
        compiler_params=_params(("parallel",), 56),
    )(dx1b, sig_a, sig_b, dm_dga, dm_dgb, wat, wbt, wout, after)


def _fwd_ffn_up(h2, wgt, wut):
    L, D = h2.shape
    F = wgt.shape[0]
    tn = F // 2
    tm = min(L, 512)

    def body(h_ref, wg_ref, wu_ref, sa_ref, sb_ref, s_ref):
        h = h_ref[...]
        a, b = _nt(h, wg_ref[...]), _nt(h, wu_ref[...])
        sg = _sigmoid(a)
        silu = a * sg
        sa_ref[...] = (b * sg * (1.0 + a * (1.0 - sg))).astype(BF16)
        sb_ref[...] = silu.astype(BF16)
        s_ref[...] = (silu * b).astype(BF16)

    wspec = pl.BlockSpec((tn, D), lambda j, i: (j, 0))
    ospec = pl.BlockSpec((tm, tn), lambda j, i: (i, j))
    return _pallas_call(
        body, name="fwd_ffn_up", grid=(2, L // tm),
        in_specs=[pl.BlockSpec((tm, D), lambda j, i: (i, 0)), wspec, wspec],
        out_specs=[ospec] * 3,
        out_shape=[_sds((L, F), BF16)] * 3,
        compiler_params=_params(("parallel", "parallel"), 48),
    )(h2, wgt, wut)


def _fwd_down_loss(s, wd, x1, target, g_final):
    L, D = x1.shape
    F = wd.shape[0]
    tm = min(L, 512)

    def body(s_ref, wd_ref, x1_ref, t_ref, g_ref, dx_ref, dxb_ref, red_ref):
        @pl.when(pl.program_id(0) == 0)
        def _():
            red_ref[...] = jnp.zeros_like(red_ref)

        g = g_ref[...]
        r, xh = _rms_stats(x1_ref[...] + _nn(s_ref[...], wd_ref[...]))
        e = xh * g - t_ref[...]
        dy = e * (1.0 / D)
        dx = _rms_bwd(dy * g, xh, r)
        dx_ref[...] = dx
        dxb_ref[...] = dx.astype(BF16)
        red_ref[0:1, :] += jnp.sum(dy * xh, axis=0, keepdims=True)
        red_ref[1:2, :] += jnp.broadcast_to(0.5 * jnp.sum(e * e) * (1.0 / D), (1, D))

    row = pl.BlockSpec((tm, D), lambda i: (i, 0))
    return _pallas_call(
        body, name="fwd_down_loss", grid=(L // tm,),
        in_specs=[pl.BlockSpec((tm, F), lambda i: (i, 0)), pl.BlockSpec((F, D), lambda i: (0, 0)),
                  row, row, pl.BlockSpec((1, D), lambda i: (0, 0))],
        out_specs=[row, row, pl.BlockSpec((8, D), lambda i: (0, 0))],
        out_shape=[_sds((L, D), F32), _sds((L, D), BF16), _sds((8, D), F32)],
        compiler_params=_params(("arbitrary",), 56),
    )(s, wd, x1, target, g_final)


def _bwd_down(dx2b, wd, s_a, s_b):
    L, D = dx2b.shape
    F = wd.shape[0]
    tn = F // 2
    tm = min(L, 512)

    def body(dx_ref, wd_ref, sa_ref, sb_ref, da_ref, db_ref):
        ds = _nt(dx_ref[...], wd_ref[...])
        da_ref[...] = (ds * sa_ref[...].astype(F32)).astype(BF16)
        db_ref[...] = (ds * sb_ref[...].astype(F32)).astype(BF16)

    ospec = pl.BlockSpec((tm, tn), lambda j, i: (i, j))
    return _pallas_call(
        body, name="bwd_down", grid=(2, L // tm),
        in_specs=[pl.BlockSpec((tm, D), lambda j, i: (i, 0)),
                  pl.BlockSpec((tn, D), lambda j, i: (j, 0)), ospec, ospec],
        out_specs=[ospec] * 2,
        out_shape=[_sds((L, F), BF16)] * 2,
        compiler_params=_params(("parallel", "parallel"), 48),
    )(dx2b, wd, s_a, s_b)


def _bwd_ffn_dh(da, db, wgt, wut, x1, dx2, g_ffn, after):
    L, D = x1.shape
    F = wgt.shape[0]
    tm = min(L, 512)

    def body(da_ref, db_ref, wg_ref, wu_ref, x1_ref, dx2_ref, g_ref, after_ref, dx_ref, dxb_ref, red_ref):
        @pl.when(pl.program_id(0) == 0)
        def _():
            red_ref[...] = jnp.zeros_like(red_ref)

        dh = _nn(da_ref[...], wg_ref[...]) + _nn(db_ref[...], wu_ref[...])
        r, xh = _rms_stats(x1_ref[...])
        red_ref[0:1, :] += jnp.sum(dh * xh, axis=0, keepdims=True)
        dx = dx2_ref[...] + _rms_bwd(dh * g_ref[...], xh, r)
        dx_ref[...] = dx
        dxb_ref[...] = dx.astype(BF16)

    row = pl.BlockSpec((tm, D), lambda i: (i, 0))
    aspec = pl.BlockSpec((tm, F), lambda i: (i, 0))
    wspec = pl.BlockSpec((F, D), lambda i: (0, 0))
    return _pallas_call(
        body, name="bwd_ffn_dh", grid=(L // tm,),
        in_specs=[aspec, aspec, wspec, wspec, row, row, pl.BlockSpec((1, D), lambda i: (0, 0)), ANY],
        out_specs=[row, row, pl.BlockSpec((8, D), lambda i: (0, 0))],
        out_shape=[_sds((L, D), F32), _sds((L, D), BF16), _sds((8, D), F32)],
        compiler_params=_params(("arbitrary",), 60),
    )(da, db, wgt, wut, x1, dx2, g_ffn, after)


def _bwd_in(dproj, w_int, x, dx1, g_mix, after):
    L, D = x.shape
    N = w_int.shape[0]
    H = dproj.shape[2]
    tm = min(L, 256)
    assert N == len(DPROJ_BLOCK_OF) * H

    def body(blocks_ref, w_ref, x_ref, dx1_ref, g_ref, after_ref, dx_ref, red_ref, dp_ref):
        @pl.when(pl.program_id(0) == 0)
        def _():
            red_ref[...] = jnp.zeros_like(red_ref)

        for t, block in enumerate(DPROJ_BLOCK_OF):
            dp_ref[:, t * H:(t + 1) * H] = blocks_ref[block]
        dh = _nn(dp_ref[...], w_ref[...])
        r, xh = _rms_stats(x_ref[...])
        red_ref[0:1, :] += jnp.sum(dh * xh, axis=0, keepdims=True)
        dx_ref[...] = dx1_ref[...] + _rms_bwd(dh * g_ref[...], xh, r)

    row = pl.BlockSpec((tm, D), lambda i: (i, 0))
    return _pallas_call(
        body, name="bwd_in", grid=(L // tm,),
        in_specs=[pl.BlockSpec((DPROJ_BLOCKS, tm, H), lambda i: (0, i, 0)), pl.BlockSpec((N, D), lambda i: (0, 0)),
                  row, row, pl.BlockSpec((1, D), lambda i: (0, 0)), ANY],
        out_specs=[row, pl.BlockSpec((8, D), lambda i: (0, 0))],
        out_shape=[_sds((L, D), F32), _sds((8, D), F32)],
        scratch_shapes=[pltpu.VMEM((tm, N), BF16)],
        compiler_params=_params(("arbitrary",), 56),
    )(dproj, w_int, x, dx1, g_mix, after)


def _dw_in(h, dproj, n_cols, c_idx):
    L, D = h.shape
    H = dproj.shape[2]
    tk = min(L, TK_TOKENS)
    nk = L // tk
    r2 = D // 2
    first = [(j * n_cols) // H for j in range(N_CHIPS)]
    last = [((j + 1) * n_cols - 1) // H for j in range(N_CHIPS)]
    slots = max(b - a for a, b in zip(first, last)) + 1
    plan = []
    for j in range(N_CHIPS):
        lo, hi = j * n_cols, (j + 1) * n_cols
        segments = []
        for s in range(last[j] - first[j] + 1):
            a, b = max(lo, (first[j] + s) * H), min(hi, (first[j] + s + 1) * H)
            segments.append((s, a - (first[j] + s) * H, b - a, a - lo))
        plan.append(segments)

    def body(c_ref, *refs):
        h_ref, slot_refs = refs[0], refs[1:1 + slots]
        o_ref, sib_ref, b_ref = refs[1 + slots:]
        j, k = pl.program_id(0), pl.program_id(1)
        for jj in range(N_CHIPS):
            @pl.when(j == jj)
            def _(jj=jj):
                for s, start, width, at in plan[jj]:
                    b_ref[:, at:at + width] = slot_refs[s][:, start:start + width]

        part = _tn(h_ref[...], b_ref[...])

        @pl.when(k == 0)
        def _():
            o_ref[...] = part

        @pl.when(k > 0)
        def _():
            o_ref[...] += part

        @pl.when(k == nk - 1)
        def _():
            theirs = pl.ds(pl.multiple_of((1 - c_ref[0]) * r2, 8), r2)
            sib_ref[...] = o_ref[theirs, :].astype(BF16)

    def slot_spec(s):
        blocks = [DPROJ_BLOCK_OF[min(first[j] + s, last[j])] for j in range(N_CHIPS)]

        def index(j, k, c_ref):
            block = blocks[0]
            for jj in range(1, N_CHIPS):
                block = jnp.where(j == jj, blocks[jj], block)
            return (block, k, 0)

        return pl.BlockSpec((None, tk, H), index)

    return _pallas_call(
        body, name="dw_in",
        grid_spec=pltpu.PrefetchScalarGridSpec(
            num_scalar_prefetch=1, grid=(N_CHIPS, nk),
            in_specs=[pl.BlockSpec((tk, D), lambda j, k, c_ref: (k, 0))] + [slot_spec(s) for s in range(slots)],
            out_specs=[pl.BlockSpec((None, D, n_cols), lambda j, k, c_ref: (j, 0, 0)),
                       pl.BlockSpec((None, r2, n_cols), lambda j, k, c_ref: (j, 0, 0))],
            scratch_shapes=[pltpu.VMEM((tk, n_cols), BF16)]),
        out_shape=[_sds((N_CHIPS, D, n_cols), F32), _sds((N_CHIPS, r2, n_cols), BF16)],
        compiler_params=_params(("parallel", "arbitrary"), 56),
    )(c_idx, h, *([dproj] * slots))


def _mm_tn(name, a, b, a_spec, b_spec, o_block, n_out, n_k):
    def body(a_ref, b_ref, o_ref):
        part = _tn(a_ref[...], b_ref[...])

        @pl.when(pl.program_id(1) == 0)
        def _():
            o_ref[...] = part

        @pl.when(pl.program_id(1) > 0)
        def _():
            o_ref[...] += part

    return _pallas_call(
        body, name=name, grid=(n_out, n_k),
        in_specs=[a_spec, b_spec],
        out_specs=pl.BlockSpec((None,) + o_block, lambda j, k: (j, 0, 0)),
        out_shape=_sds((n_out,) + o_block, F32),
        compiler_params=_params(("parallel", "arbitrary"), 56),
    )(a, b)


TK_TOKENS = 2048


def _dw_whole(name, pairs, by_rows):
    n = len(pairs)
    L = pairs[0][0].shape[0]
    tk = min(L, TK_TOKENS)

    def body(*refs):
        for q in range(n):
            a_ref, b_ref, o_ref = refs[2 * q], refs[2 * q + 1], refs[2 * n + q]
            part = _tn(a_ref[...], b_ref[...])
            rows, cols = o_ref.shape[1], o_ref.shape[2]
            shards = [part[j * rows:(j + 1) * rows, :] if by_rows else part[:, j * cols:(j + 1) * cols]
                      for j in range(N_CHIPS)]

            @pl.when(pl.program_id(0) == 0)
            def _(shards=shards, o_ref=o_ref):
                for j, shard in enumerate(shards):
                    o_ref[j] = shard

            @pl.when(pl.program_id(0) > 0)
            def _(shards=shards, o_ref=o_ref):
                for j, shard in enumerate(shards):
                    o_ref[j] += shard

    in_specs, out_specs, out_shape, operands = [], [], [], []
    for a, b in pairs:
        M, N = a.shape[1], b.shape[1]
        shape = (N_CHIPS, M // N_CHIPS, N) if by_rows else (N_CHIPS, M, N // N_CHIPS)
        in_specs += [pl.BlockSpec((tk, M), lambda k: (k, 0)), pl.BlockSpec((tk, N), lambda k: (k, 0))]
        out_specs.append(pl.BlockSpec(shape, lambda k: (0, 0, 0)))
        out_shape.append(_sds(shape, F32))
        operands += [a, b]
    return _pallas_call(
        body, name=name, grid=(L // tk,), in_specs=in_specs, out_specs=out_specs, out_shape=out_shape,
        compiler_params=_params(("arbitrary",), 56),
    )(*operands)


def _dw_rows2(name, a, b):
    L, M = a.shape
    N = b.shape[1]
    tk = min(L, TK_TOKENS)
    return _mm_tn(name, a, b, pl.BlockSpec((tk, M // 2), lambda j, k: (k, j)),
                  pl.BlockSpec((tk, N), lambda j, k: (k, 0)), (M // 2, N), 2, L // tk)


def _place():
    x, y, c = lax.axis_index("x"), lax.axis_index("y"), lax.axis_index("c")
    chips = [(1 - x, y), (x, 1 - y), (1 - x, 1 - y)]
    return x, y, c, 2 * x + y, chips


def _remote(src, dst, send_sem, recv_sem, device):
    return pltpu.make_async_remote_copy(src_ref=src, dst_ref=dst, send_sem=send_sem,
                                        recv_sem=recv_sem, device_id=device, device_id_type=MESH)


def _half(ref, lead, c, r2):
    return ref.at[lead, pl.ds(pl.multiple_of(c * r2, 16), r2), :]


def _cast_place(name, ws, chip_idx):
    n = len(ws)

    def body(k_ref, *refs):
        for w_ref, o_ref in zip(refs[:n], refs[n:]):
            o_ref[...] = w_ref[...].astype(BF16)

    return _pallas_call(
        body, name=name,
        grid_spec=pltpu.PrefetchScalarGridSpec(
            num_scalar_prefetch=1, grid=(2,),
            in_specs=[pl.BlockSpec((w.shape[0] // 2, w.shape[1]), lambda i, k_ref: (i, 0)) for w in ws],
            out_specs=[pl.BlockSpec((None, w.shape[0] // 2, w.shape[1]), lambda i, k_ref: (k_ref[0], i, 0))
                       for w in ws]),
        out_shape=[_sds((N_CHIPS,) + w.shape, BF16) for w in ws],
        compiler_params=_params(("parallel",), 48),
    )(chip_idx, *ws)


def _cast_place_t(name, ws, chip_idx):
    n = len(ws)
    r, cols = ws[0].shape

    def body(k_ref, *refs):
        for w_ref, o_ref in zip(refs[:n], refs[n:]):
            o_ref[...] = w_ref[...].T.astype(BF16)

    return _pallas_call(
        body, name=name,
        grid_spec=pltpu.PrefetchScalarGridSpec(
            num_scalar_prefetch=1, grid=(cols // LANES,),
            in_specs=[pl.BlockSpec((r, LANES), lambda i, k_ref: (0, i))] * n,
            out_specs=[pl.BlockSpec((None, LANES, r), lambda i, k_ref: (k_ref[0], i, 0))] * n),
        out_shape=[_sds((N_CHIPS, cols, r), BF16)] * n,
        compiler_params=_params(("parallel",), 48),
    )(chip_idx, *ws)


def _gather_copies(bufs, whole, send_sems, recv_sems, select=None):
    x, y, c, k, chips = _place()
    pairs = []
    for w, buf in enumerate(bufs):
        for j, (cx, cy) in enumerate(chips):
            if select is not None and not select(w, j):
                continue
            if w in whole:
                mine, theirs = buf.at[k], buf.at[2 * cx + cy]
            else:
                r2 = buf.shape[1] // 2
                mine, theirs = _half(buf, k, c, r2), _half(buf, 2 * cx + cy, c, r2)
            sems = (send_sems.at[w * 3 + j], recv_sems.at[w * 3 + j])
            pairs.append((_remote(mine, mine, *sems, (cx, cy, c)), _remote(theirs, theirs, *sems, (x, y, c))))
    return pairs


def _gather_start(name, groups, after):
    flat = [b for bufs, _, _ in groups for b in bufs]
    nb, ng = len(flat), len(groups)

    def body(*refs):
        ins, sems, token = refs[:nb], refs[nb + 1:nb + 1 + 2 * ng], refs[-1]
        pos = 0
        for g, (bufs, whole, select) in enumerate(groups):
            for send, _ in _gather_copies(ins[pos:pos + len(bufs)], whole, sems[2 * g], sems[2 * g + 1], select):
                send.start()
            pos += len(bufs)
        token[...] = jnp.zeros_like(token)

    sem_shapes = []
    for bufs, _, _ in groups:
        sem_shapes += [pltpu.SemaphoreType.DMA((3 * len(bufs),))] * 2
    out = _pallas_call(
        body, name=name,
        in_specs=[HBM] * nb + [ANY], out_specs=tuple([SEM] * (2 * ng) + [HBM] * nb + [VMEM]),
        out_shape=tuple(sem_shapes + [pltpu.HBM(b.shape, b.dtype) for b in flat] + [_sds((8, LANES), F32)]),
        input_output_aliases={i: 2 * ng + i for i in range(nb)},
        compiler_params=pltpu.CompilerParams(has_side_effects=EFFECT),
    )(*flat, after)
    sems, thru, pos = [], [], 2 * ng
    for g, (bufs, _, _) in enumerate(groups):
        sems.append((out[2 * g], out[2 * g + 1]))
        thru.append(list(out[pos:pos + len(bufs)]))
        pos += len(bufs)
    return sems, thru, out[-1]


def _gather_wait(name, bufs, whole, sems, after, select=None):
    nb = len(bufs)

    def body(*refs):
        ins, send_sems, recv_sems = refs[:nb], refs[nb], refs[nb + 1]
        for send, arrival in _gather_copies(ins, whole, send_sems, recv_sems, select):
            send.wait_send()
            arrival.wait_recv()

    return _pallas_call(
        body, name=name,
        in_specs=[HBM] * nb + [SEM, SEM, ANY], out_specs=[HBM] * nb,
        out_shape=[pltpu.HBM(b.shape, b.dtype) for b in bufs],
        input_output_aliases={i: i for i in range(nb)},
        compiler_params=pltpu.CompilerParams(has_side_effects=EFFECT),
    )(*bufs, sems[0], sems[1], after)


def _gather_forward(name, bufs, sources=(0, 1, 2)):
    n = len(bufs)

    def body(*refs):
        outs = refs[n:2 * n]
        send_sems, recv_sems = refs[2 * n:]
        x, y, c, _, chips = _place()
        sends = []
        for w in range(n):
            r2 = outs[w].shape[1] // 2
            for j in sources:
                landed = _half(outs[w], 2 * chips[j][0] + chips[j][1], c, r2)
                sends.append(_remote(landed, landed, send_sems.at[w * 3 + j], recv_sems.at[w * 3 + j],
                                     (x, y, 1 - c)))
        for cp in sends:
            cp.start()
        for w in range(n):
            r2 = outs[w].shape[1] // 2
            for j in sources:
                got = _half(outs[w], 2 * chips[j][0] + chips[j][1], 1 - c, r2)
                _remote(got, got, send_sems.at[w * 3 + j], recv_sems.at[w * 3 + j], (x, y, c)).wait_recv()
        for cp in sends:
            cp.wait_send()

    return _pallas_call(
        body, name=name,
        in_specs=[ANY] * n, out_specs=[ANY] * n,
        out_shape=[_sds(b.shape, b.dtype) for b in bufs],
        input_output_aliases={i: i for i in range(n)},
        scratch_shapes=[pltpu.SemaphoreType.DMA((n * 3,)), pltpu.SemaphoreType.DMA((n * 3,))],
    )(*bufs)


def _rs_add(name, grads3, from_sibling, c_idx):
    n = len(grads3)

    def body(c_ref, *refs):
        for g_ref, s_ref, o_ref in zip(refs[:n], refs[n:2 * n], refs[2 * n:]):
            o_ref[...] = (g_ref[...] + s_ref[...].astype(F32)).astype(BF16)

    mine =[pl.BlockSpec((None,) + s.shape[1:], lambda k, c_ref: (k, c_ref[0], 0)) for s in from_sibling]
    whole = [pl.BlockSpec((None,) + s.shape[1:], lambda k, c_ref: (k, 0, 0)) for s in from_sibling]
    return _pallas_call(
        body, name=name,
        grid_spec=pltpu.PrefetchScalarGridSpec(num_scalar_prefetch=1, grid=(N_CHIPS,), in_specs=mine + whole,
                                               out_specs=whole),
        out_shape=[_sds(s.shape, BF16) for s in from_sibling],
        compiler_params=_params(("parallel",), 48),
    )(c_idx, *grads3, *from_sibling)


def _split_start(name, arrays, n_sems, pairs_fn):
    n = len(arrays)

    def body(*refs):
        for send, _ in pairs_fn(refs[:n], refs[n], refs[n + 1]):
            send.start()
        refs[-1][...] = jnp.zeros_like(refs[-1])

    out = _pallas_call(
        body, name=name,
        in_specs=[HBM] * n, out_specs=tuple([SEM, SEM] + [HBM] * n + [VMEM]),
        out_shape=tuple([pltpu.SemaphoreType.DMA((n_sems,))] * 2 + [pltpu.HBM(a.shape, a.dtype) for a in arrays]
                        + [_sds((8, LANES), F32)]),
        input_output_aliases={i: 2 + i for i in range(n)},
        compiler_params=pltpu.CompilerParams(has_side_effects=EFFECT),
    )(*arrays)
    return (out[0], out[1]), list(out[2:2 + n]), out[-1]


def _split_wait(name, sems, arrays, pairs_fn, after):
    n = len(arrays)

    def body(*refs):
        for send, arrival in pairs_fn(refs[:n], refs[n], refs[n + 1]):
            send.wait_send()
            arrival.wait_recv()

    return list(_pallas_call(
        body, name=name,
        in_specs=[HBM] * n + [SEM, SEM, ANY], out_specs=[HBM] * n,
        out_shape=[pltpu.HBM(a.shape, a.dtype) for a in arrays],
        input_output_aliases={i: i for i in range(n)},
        compiler_params=pltpu.CompilerParams(has_side_effects=EFFECT),
    )(*arrays, sems[0], sems[1], after))


def _forward_pairs(bufs, send_sems, recv_sems):
    x, y, c, _, chips = _place()
    pairs = []
    for w, buf in enumerate(bufs):
        r2 = buf.shape[1] // 2
        for j, (cx, cy) in enumerate(chips):
            landed, theirs = _half(buf, 2 * cx + cy, c, r2), _half(buf, 2 * cx + cy, 1 - c, r2)
            sems = (send_sems.at[w * 3 + j], recv_sems.at[w * 3 + j])
            pairs.append((_remote(landed, landed, *sems, (x, y, 1 - c)), _remote(theirs, theirs, *sems, (x, y, c))))
    return pairs


def _sibling_pairs(arrays, send_sems, recv_sems):
    x, y, c, _, _ = _place()
    n = len(arrays) // 2
    pairs = []
    for w in range(n):
        r2 = arrays[w].shape[1] // 2
        cp = _remote(_half(arrays[w], slice(None), 1 - c, r2), arrays[n + w], send_sems.at[w], recv_sems.at[w],
                     (x, y, 1 - c))
        pairs.append((cp, cp))
    return pairs


def _sibling_whole_pairs(arrays, send_sems, recv_sems):
    x, y, c, _, _ = _place()
    n = len(arrays) // 2
    pairs = []
    for w in range(n):
        cp = _remote(arrays[w], arrays[n + w], send_sems.at[w], recv_sems.at[w], (x, y, 1 - c))
        pairs.append((cp, cp))
    return pairs


def _ici_pairs(arrays, send_sems, recv_sems):
    x, y, c, _, chips = _place()
    n = len(arrays) // 2
    pairs = []
    for w in range(n):
        for j, (cx, cy) in enumerate(chips):
            cp = _remote(arrays[w].at[2 * cx + cy], arrays[n + w].at[j],
                         send_sems.at[w * 3 + j], recv_sems.at[w * 3 + j], (cx, cy, c))
            pairs.append((cp, cp))
    return pairs


def _rs_sum(name, partials, received, place_idx):
    n = len(partials)
    nb = 2
    blocks = [(p.shape[1] // nb, p.shape[2]) for p in partials]

    def body(idx_ref, *refs):
        for p_ref, r_ref, o_ref in zip(refs[:n], refs[n:2 * n], refs[2 * n:]):
            o_ref[...] = ((p_ref[...].astype(F32) + r_ref[0].astype(F32))
                          + (r_ref[1].astype(F32) + r_ref[2].astype(F32)))

    return _pallas_call(
        body, name=name,
        grid_spec=pltpu.PrefetchScalarGridSpec(
            num_scalar_prefetch=1, grid=(nb,),
            in_specs=[pl.BlockSpec((None,) + b, lambda i, idx: (idx[0], i, 0)) for b in blocks]
            + [pl.BlockSpec((3,) + b, lambda i, idx: (0, i, 0)) for b in blocks],
            out_specs=[pl.BlockSpec(b, lambda i, idx: (idx[1] * nb + i, 0)) for b in blocks]),
        out_shape=[_sds((2 * p.shape[1], p.shape[2]), F32) for p in partials],
        compiler_params=_params(("parallel",), 48),
    )(place_idx, *partials, *received)


def _share_pairs(arrays, send_sems, recv_sems):
    x, y, c, _, _ = _place()
    pairs = []
    for w, arr in enumerate(arrays):
        r2 = arr.shape[0] // 2
        mine = arr.at[pl.ds(pl.multiple_of(c * r2, 8), r2), :]
        theirs = arr.at[pl.ds(pl.multiple_of((1 - c) * r2, 8), r2), :]
        sems = (send_sems.at[w], recv_sems.at[w])
        pairs.append((_remote(mine, mine, *sems, (x, y, 1 - c)), _remote(theirs, theirs, *sems, (x, y, c))))
    return pairs


def _small_pack(red_mix, red_ffn, red_final, red_hg, g_conv):
    D = red_mix.shape[1]
    H = red_hg.shape[1]

    def body(mix_ref, ffn_ref, fin_ref, hg_ref, cv_ref, in_ref):
        in_ref[...] = jnp.zeros_like(in_ref)
        in_ref[0:1, :] = mix_ref[0:1, :]
        in_ref[1:2, :] = ffn_ref[0:1, :]
        in_ref[2:3, :] = fin_ref[0:1, :]
        gam = hg_ref[1:2, 0:HEAD_DIM]
        for h in range(1, H // HEAD_DIM):
            gam = gam + hg_ref[1:2, h * HEAD_DIM:(h + 1) * HEAD_DIM]
        in_ref[3:4, 0:HEAD_DIM] = gam
        in_ref[3:4, HEAD_DIM:2 * HEAD_DIM] = fin_ref[1:2, 0:HEAD_DIM]
        in_ref[4:5, 0:H] = hg_ref[0:1, :]
        in_ref[6:9, 0:H] = cv_ref[...]

    return _pallas_call(
        body, name="small_pack", pin=False,
        in_specs=[VMEM] * 5, out_specs=VMEM, out_shape=_sds((N_SMALL_ROWS, D), F32),
    )(red_mix, red_ffn, red_final, red_hg, g_conv)


def _small_pairs(arrays, send_sems, recv_sems):
    block, gathered = arrays
    x, y, c, _, _ = _place()
    me = 4 * x + 2 * y + c
    pairs = []
    for m in range(1, 8):
        px, py, pc = x ^ ((m >> 2) & 1), y ^ ((m >> 1) & 1), c ^ (m & 1)
        sems = (send_sems.at[m - 1], recv_sems.at[m - 1])
        pairs.append((_remote(block, gathered.at[me], *sems, (px, py, pc)),
                      _remote(block, gathered.at[4 * px + 2 * py + pc], *sems, (x, y, c))))
    return pairs


def _adamw_math(w, g, m, v):
    m = ADAM_B1 * m + (1.0 - ADAM_B1) * g
    v = ADAM_B2 * v + (1.0 - ADAM_B2) * jnp.square(g)
    m_hat = m / (1.0 - ADAM_B1 ** ADAM_STEP)
    v_hat = v / (1.0 - ADAM_B2 ** ADAM_STEP)
    delta = -ADAM_LR * (m_hat / (jnp.sqrt(v_hat) + ADAM_EPS) + ADAM_WD * w)
    return delta, m, v


def _adamw(name, gs, ws, ms, vs):
    n = len(gs)
    nb = 4

    def body(*refs):
        ins, outs = refs[:4 * n], refs[4 * n:]
        for j in range(n):
            g_ref, w_ref, m_ref, v_ref = ins[j], ins[n + j], ins[2 * n + j], ins[3 * n + j]
            go_ref, d_ref, mo_ref, vo_ref = outs[4 * j:4 * j + 4]
            g = g_ref[...]
            go_ref[...] = g
            d_ref[...], mo_ref[...], vo_ref[...] = _adamw_math(w_ref[...], g, m_ref[...], v_ref[...])

    blk = [pl.BlockSpec((g.shape[0] // nb, g.shape[1]), lambda i: (i, 0)) for g in gs]
    out = _pallas_call(
        body, name=name, grid=(nb,),
        in_specs=blk * 4, out_specs=[b for b in blk for _ in range(4)],
        out_shape=[_sds(g.shape, F32) for g in gs for _ in range(4)],
        compiler_params=_params(("parallel",), 56),
    )(*gs, *ws, *ms, *vs)
    return [list(out[4 * j:4 * j + 4]) for j in range(n)]


def _small_update(block, gathered, place_idx, ws, ms, vs):
    n = len(ws)
    H = ws[1].shape[1]

    def body(idx_ref, blk_ref, all_ref, *refs):
        w, m, v, outs, tot_ref = refs[:n], refs[n:2 * n], refs[2 * n:3 * n], refs[3 * n:-1], refs[-1]
        chip, me = idx_ref[0], idx_ref[1]
        tot = jnp.where(me == 0, blk_ref[...], all_ref[0])
        for d in range(1, 8):
            tot = tot + jnp.where(me == d, blk_ref[...], all_ref[d])
        tot_ref[...] = tot
        p0 = _lower_bound(w[1][...])
        dl0 = p0 * (1.0 - p0) * tot_ref[4:5, 0:H]
        conv = jnp.zeros((3, LANES), F32)
        for k in range(N_CHIPS):
            conv = jnp.where(chip == k, tot_ref[6:9, k * LANES:(k + 1) * LANES], conv)
        grads = [tot_ref[0:1, :], None, tot_ref[3:4, 0:HEAD_DIM], conv, tot_ref[1:2, :], tot_ref[2:3, :]]
        for p in range(n):
            g_ref, d_ref, mo_ref, vo_ref = outs[4 * p:4 * p + 4]
            if p == 1:
                for row, g in ((slice(0, 1), dl0), (slice(1, 2), -dl0)):
                    g_ref[row, :] = g
                    d_ref[row, :], mo_ref[row, :], vo_ref[row, :] = _adamw_math(
                        w[p][row, :], g, m[p][row, :], v[p][row, :])
            else:
                g_ref[...] = grads[p]
                d_ref[...], mo_ref[...], vo_ref[...] = _adamw_math(w[p][...], grads[p], m[p][...], v[p][...])
        outs[4 * n][...] = tot_ref[3:4, HEAD_DIM:2 * HEAD_DIM]

    full = lambda a: pl.BlockSpec(a.shape, lambda i, idx: (0,) * a.ndim)
    out_shape = [_sds(w.shape, F32) for w in ws for _ in range(4)] + [_sds((1, LANES), F32)]
    return _pallas_call(
        body, name="small_update",
        grid_spec=pltpu.PrefetchScalarGridSpec(
            num_scalar_prefetch=1, grid=(1,),
            in_specs=[full(block), full(gathered)] + [full(a) for a in ws + ms + vs],
            out_specs=[full(s) for s in out_shape],
            scratch_shapes=[pltpu.VMEM(block.shape, F32)]),
        out_shape=out_shape,
    )(place_idx, block, gathered, *ws, *ms, *vs)


def kernel(x, norm_mix_g, w_in, lower_bounds, hg_norm_g, conv_w, w_branch_a, w_branch_b, w_out, norm_ffn_g, w_ffn_gate, w_ffn_up, w_ffn_down, norm_final_g, loss_target, m_norm_mix_g, m_w_in, m_lower_bounds, m_hg_norm_g, m_conv_w, m_w_branch_a, m_w_branch_b, m_w_out, m_norm_ffn_g, m_w_ffn_gate, m_w_ffn_up, m_w_ffn_down, m_norm_final_g, v_norm_mix_g, v_w_in, v_lower_bounds, v_hg_norm_g, v_conv_w, v_w_branch_a, v_w_branch_b, v_w_out, v_norm_ffn_g, v_w_ffn_gate, v_w_ffn_up, v_w_ffn_down, v_norm_final_g):
    _, L, D = x.shape
    H = D // 2
    assert lower_bounds.shape == (2, H) and hg_norm_g.shape == (1, HEAD_DIM)
    assert conv_w.shape == (1, 3, LANES) and w_in.shape[2] * N_CHIPS == 11 * H
    x2d, target = x.reshape(L, D), loss_target.reshape(L, D)
    g_final = norm_final_g.reshape(1, D)
    chip = 2 * lax.axis_index("x") + lax.axis_index("y")
    core = lax.axis_index("c")

    tr = lambda w: jnp.transpose(w[0])
    big = [w_in[0], w_branch_a[0], w_branch_b[0], w_out[0], tr(w_ffn_gate), tr(w_ffn_up), w_ffn_down[0]]
    big_m = [m_w_in[0], m_w_branch_a[0], m_w_branch_b[0], m_w_out[0], tr(m_w_ffn_gate), tr(m_w_ffn_up),
             m_w_ffn_down[0]]
    big_v = [v_w_in[0], v_w_branch_a[0], v_w_branch_b[0], v_w_out[0], tr(v_w_ffn_gate), tr(v_w_ffn_up),
             v_w_ffn_down[0]]
    names = ["w_in", "w_branch_a", "w_branch_b", "w_out", "w_ffn_gate", "w_ffn_up", "w_ffn_down"]

    chip_idx = chip.reshape(1).astype(jnp.int32)
    def per_shape(fn, tag, js, *lists):
        groups = {}
        for pos, a in enumerate(lists[0]):
            groups.setdefault(a.shape, []).append(pos)
        results = [None] * len(js)
        for same in groups.values():
            out = fn(tag + names[js[same[0]]], *[[xs[p] for p in same] for xs in lists])
            for q, p in enumerate(same):
                results[p] = out[q]
        return results

    place_t = lambda name, ws: _cast_place_t(name, ws, chip_idx)
    placed = per_shape(place_t, "place_", [0, 1, 2], big[:3]) + list(_cast_place("place_rest", big[3:], chip_idx))
    conv_placed = lax.dynamic_update_slice(jnp.zeros((N_CHIPS, 3, LANES), F32), conv_w, (chip, 0, 0))
    x_i, y_i = lax.axis_index("x"), lax.axis_index("y")
    blocks = lambda *ks: jnp.stack(ks).astype(jnp.int32)
    near = lambda w, j: j < 2
    far = lambda w, j: w == 1 or j == 2
    near_sems, in_flight, _ = _gather_start("gather_start_near", [([placed[0]], set(), near)], chip_idx)
    w_in_buf = in_flight[0][0]
    h, proj = _fwd_proj_first(x2d, norm_mix_g, w_in_buf, blocks(chip), placed[-1])
    sems, in_flight, _ = _gather_start(
        "gather_start_rest", [([w_in_buf, conv_placed], {1}, far), (placed[1:4], set(), None),
                              (placed[4:], set(), None)], h)
    w_in_buf, conv_buf = in_flight[0]
    (w_in_buf,) = _gather_wait("gather_wait_in_near", [w_in_buf], set(), near_sems[0], h, near)
    (w_in_buf,) = _gather_forward("gather_fwd_in_near", [w_in_buf], (0, 1))
    proj = _fwd_proj_more("fwd_proj_near", h, w_in_buf, proj,
                          blocks(2 * (1 - x_i) + y_i, 2 * x_i + (1 - y_i)))
    w_in_buf, conv_all = _gather_wait("gather_wait_in_far", [w_in_buf, conv_buf], {1}, sems[0], proj, far)
    (w_int3,) = _gather_forward("gather_fwd_in_far", [w_in_buf], (2,))
    proj = _fwd_proj_more("fwd_proj_far", h, w_int3, proj, blocks(2 * (1 - x_i) + (1 - y_i)))
    w_int = w_int3.reshape(-1, D)
    conv_full = jnp.transpose(conv_all, (1, 0, 2)).reshape(3, H)
    og, o_pre, s_saved = _hgrn_fwd(proj, lower_bounds, hg_norm_g, H)
    landed = _gather_wait("gather_wait_mix", in_flight[1], set(), sems[1], og)
    fwd_sems, landed, token = _split_start("gather_fwd_mix_start", landed, 9, _forward_pairs)
    cb = _conv_fwd(proj, conv_full, H, token)
    wat3, wbt3, wout3 = _split_wait("gather_fwd_mix_wait", fwd_sems, landed, _forward_pairs, cb)
    wat, wbt, wout = wat3.reshape(D, H), wbt3.reshape(D, H), wout3.reshape(D, D)
    landed = _gather_wait("gather_wait_ffn", in_flight[2], set(), sems[2], cb)
    fwd_sems, landed, token = _split_start("gather_fwd_ffn_start", landed, 9, _forward_pairs)
    sig_a, sig_b, dm_dga, dm_dgb, merged, x1, h2 = _fwd_mix(og, cb, proj, x2d, wat, wbt, wout, norm_ffn_g,
                                                              H, token)
    wgt3, wut3, wd3 = _split_wait("gather_fwd_ffn_wait", fwd_sems, landed, _forward_pairs, h2)
    d_ff = N_CHIPS * wd3.shape[1]
    wgt, wut, wd = wgt3.reshape(d_ff, D), wut3.reshape(d_ff, D), wd3.reshape(d_ff, D)
    ffn_ds_da, ffn_ds_db, ffn_s = _fwd_ffn_up(h2, wgt, wut)
    dx2, dx2b, red_final = _fwd_down_loss(ffn_s, wd, x1, target, g_final)

    c_idx = core.reshape(1).astype(jnp.int32)
    place_idx = jnp.stack([chip, core]).astype(jnp.int32)

    def sibling_start(tag, grads):
        bufs = [lax.empty((N_CHIPS, g.shape[1] // 2, g.shape[2]), F32) for g in grads]
        return _split_start("rs_sibling_start_" + tag, list(grads) + bufs, len(grads), _sibling_pairs)

    def ici_start(tag, js, grads, from_sibling):
        partials = list(_rs_add("rs_add_" + tag, grads, from_sibling, c_idx))
        landings = [lax.empty((3,) + p.shape[1:], BF16) for p in partials]
        return _split_start("rs_ici_start_" + tag, partials + landings, 3 * len(js), _ici_pairs)

    def ici_start_behind(tag, js, started, after):
        n = len(js)
        arrays = _split_wait("rs_sibling_wait_" + tag, started[0], started[1], _sibling_pairs, after)
        return ici_start(tag, js, arrays[:n], arrays[n:])

    def sums(tag, started, after):
        partials, received = [], []
        for group, group_js, start in started:
            arrays = _split_wait("rs_ici_wait_" + group, start[0], start[1], _ici_pairs, after)
            partials += arrays[:len(group_js)]
            received += arrays[len(group_js):]
        return list(_rs_sum("rs_sum_" + tag, partials, received, place_idx))

    def adamw(tag, js, grads):
        return _adamw("adamw_" + tag, grads, *[[src[j] for j in js] for src in (big, big_m, big_v)])

    shards3 = lambda g: g.reshape(N_CHIPS, d_ff // N_CHIPS, D)
    da, db = _bwd_down(dx2b, wd, ffn_ds_da, ffn_ds_db)
    g_wd = shards3(_dw_rows2("dw_ffn_down", ffn_s, dx2b))
    g_wg = shards3(_dw_rows2("dw_ffn_gate", da, h2))
    g_wu = shards3(_dw_rows2("dw_ffn_up", db, h2))
    ffn_sibling = sibling_start("ffn", [g_wg, g_wu, g_wd])
    dx1, dx1b, red_ffn = _bwd_ffn_dh(da, db, wgt, wut, x1, dx2, norm_ffn_g, ffn_sibling[2])
    ffn_ici = ici_start_behind("ffn", [4, 5, 6], ffn_sibling, dx1b)
    dya, dyb, dproj, d_o, d_cb = _bwd_mix(dx1b, sig_a, sig_b, dm_dga, dm_dgb, wat, wbt, wout, H, ffn_ici[2])
    (g_wout,) = _dw_whole("dw_out", [(merged, dx1b)], True)
    g_wa, g_wb = _dw_whole("dw_branch", [(og, dya), (cb, dyb)], False)
    mix_sibling = sibling_start("mix", [g_wa, g_wb, g_wout])
    dproj, red_hg = _hgrn_bwd(proj, lower_bounds, hg_norm_g, o_pre, d_o, s_saved, H, mix_sibling[2], dproj)
    mix_ici = ici_start_behind("mix", [1, 2, 3], mix_sibling, red_hg)
    dproj, g_conv = _conv_bwd(proj, conv_full, d_cb, H, mix_ici[2], dproj)
    g_win, for_sibling = _dw_in(h, dproj, w_int3.shape[1], c_idx)
    in_sibling = _split_start("rs_sibling_start_in", [for_sibling, lax.empty(for_sibling.shape, BF16)], 1,
                              _sibling_whole_pairs)
    halves = sums("rest", [("mix", [1, 2, 3], mix_ici), ("ffn", [4, 5, 6], ffn_ici)], in_sibling[2])
    rest_share = _split_start("rs_share_start_rest", halves, len(halves), _share_pairs)
    from_sibling = _split_wait("rs_sibling_wait_in", in_sibling[0], in_sibling[1], _sibling_whole_pairs,
                               rest_share[2])[1]
    in_ici = ici_start("in", [0], [g_win], [from_sibling])
    grad_x, red_mix = _bwd_in(dproj, w_int, x2d, dx1, norm_mix_g, in_ici[2])
    in_share = _split_start("rs_share_start_in", sums("in", [("in", [0], in_ici)], grad_x), 1, _share_pairs)
    small_block = _small_pack(red_mix, red_ffn, red_final, red_hg, g_conv)
    small = _split_start("small_gather_start", [small_block, lax.empty((8,) + small_block.shape, F32)], 7,
                         _small_pairs)
    rest_grads = _split_wait("rs_share_wait_rest", rest_share[0], rest_share[1], _share_pairs, small[2])
    big_out = [None] + adamw("rest", [1, 2, 3, 4, 5, 6], rest_grads)
    in_grad = _split_wait("rs_share_wait_in", in_share[0], in_share[1], _share_pairs, big_out[6][0])
    big_out[0] = adamw("in", [0], in_grad)[0]
    small_block, small_all = _split_wait("small_gather_wait", small[0], small[1], _small_pairs, big_out[0][0])

    def smalls(mix, lb, hg, cw, ffn, fin):
        return [mix, lb, hg, cw[0], ffn, fin.reshape(1, D)]

    small_out = _small_update(
        small_block, small_all, jnp.stack([chip, 4 * x_i + 2 * y_i + core]).astype(jnp.int32),
        smalls(norm_mix_g, lower_bounds, hg_norm_g, conv_w, norm_ffn_g, norm_final_g),
        smalls(m_norm_mix_g, m_lower_bounds, m_hg_norm_g, m_conv_w, m_norm_ffn_g, m_norm_final_g),
        smalls(v_norm_mix_g, v_lower_bounds, v_hg_norm_g, v_conv_w, v_norm_ffn_g, v_norm_final_g))

    def outputs(i):
        big_i = [big_out[j][i] for j in range(7)]
        mix, lb, hg, cw, ffn, fin = [small_out[4 * p + i] for p in range(6)]
        return [mix, big_i[0][None], lb, hg, cw[None], big_i[1][None], big_i[2][None], big_i[3][None], ffn,
                big_i[4].T[None], big_i[5].T[None], big_i[6][None], fin.reshape(D)]

    outs = [small_out[24][0, 0], grad_x.reshape(1, L, D)]
    for i in range(4):
        outs += outputs(i)
    return tuple(outs)
```

```python
import jax
import jax.numpy as jnp
from jax import lax
from jax.experimental import pallas as pl
from jax.experimental.pallas import tpu as pltpu

F32 = jnp.float32
BF16 = jnp.bfloat16
EPS = 1e-6
CHUNK = 32
HEAD_DIM = 128
LANES = 128
N_CHIPS = 4
N_SMALL_ROWS = 16
DPROJ_BLOCKS = 12
DPROJ_BLOCK_OF = (0, 1, 2, 3, 8, 9, 10, 4, 5, 6, 7)

ADAM_LR = 0.001
ADAM_B1 = 0.9
ADAM_B2 = 0.999
ADAM_EPS = 1e-08
ADAM_WD = 0.01
ADAM_STEP = 10

MESH = pl.DeviceIdType.MESH
ANY = pl.BlockSpec(memory_space=pl.ANY)
VMEM = pl.BlockSpec(memory_space=pltpu.VMEM)
HBM = pl.BlockSpec(memory_space=pltpu.HBM)
SEM = pl.BlockSpec(memory_space=pltpu.SEMAPHORE)
EFFECT = pltpu.SideEffectType.DATAFLOW_SIDE_EFFECTING


def _sds(shape, dtype):
    return jax.ShapeDtypeStruct(shape, dtype)


def _pallas_call(body, pin=True, **kwargs):
    if not pin:
        return pl.pallas_call(body, **kwargs)
    in_hbm = lambda s: pltpu.HBM(s.shape, s.dtype) if isinstance(s, jax.ShapeDtypeStruct) else s
    kwargs["out_shape"] = jax.tree.map(in_hbm, kwargs["out_shape"])
    call = pl.pallas_call(body, **kwargs)

    def run(*args):
        return call(*[pltpu.with_memory_space_constraint(a, pltpu.HBM) if a.dtype in (F32, BF16) else a
                      for a in args])

    return run


def _params(semantics, vmem_mb):
    return pltpu.CompilerParams(dimension_semantics=semantics, vmem_limit_bytes=vmem_mb << 20)


def _nn(a, b):
    return lax.dot_general(a, b, (((1,), (0,)), ((), ())), preferred_element_type=F32)


def _nt(a, b):
    return lax.dot_general(a, b, (((1,), (1,)), ((), ())), preferred_element_type=F32)


def _tn(a, b):
    return lax.dot_general(a, b, (((0,), (0,)), ((), ())), preferred_element_type=F32)


def _sigmoid(x):
    return jax.nn.sigmoid(x)


def _rms_stats(x):
    r = lax.rsqrt(jnp.mean(x * x, axis=-1, keepdims=True) + EPS)
    return r, x * r


def _rms_bwd(dxh, xh, r):
    return r * (dxh - xh * jnp.mean(dxh * xh, axis=-1, keepdims=True))


def _fwd_proj_first(x, g_mix, w_int3, block, after):
    L, D = x.shape
    tn = w_int3.shape[1]
    tm = min(L, 1024)

    def body(blk_ref, x_ref, g_ref, w_ref, after_ref, h_ref, p_ref):
        _, xh = _rms_stats(x_ref[...])
        h = (xh * g_ref[...]).astype(BF16)
        h_ref[...] = h
        p_ref[...] = _nt(h, w_ref[...])

    return _pallas_call(
        body, name="fwd_proj_own",
        grid_spec=pltpu.PrefetchScalarGridSpec(
            num_scalar_prefetch=1, grid=(L // tm,),
            in_specs=[pl.BlockSpec((tm, D), lambda i, blk: (i, 0)),
                      pl.BlockSpec((1, D), lambda i, blk: (0, 0)),
                      pl.BlockSpec((None, tn, D), lambda i, blk: (blk[0], 0, 0)), ANY],
            out_specs=[pl.BlockSpec((tm, D), lambda i, blk: (i, 0)),
                       pl.BlockSpec((tm, tn), lambda i, blk: (i, blk[0]))]),
        out_shape=[_sds((L, D), BF16), _sds((L, N_CHIPS * tn), F32)],
        compiler_params=_params(("parallel",), 48),
    )(block, x, g_mix, w_int3, after)


def _fwd_proj_more(name, h, w_int3, proj, blocks):
    L, D = h.shape
    tn = w_int3.shape[1]
    tm = min(L, 1024)

    def body(blk_ref, h_ref, w_ref, proj_ref, p_ref):
        p_ref[...] = _nt(h_ref[...], w_ref[...])

    return _pallas_call(
        body, name=name,
        grid_spec=pltpu.PrefetchScalarGridSpec(
            num_scalar_prefetch=1, grid=(L // tm, blocks.shape[0]),
            in_specs=[pl.BlockSpec((tm, D), lambda i, j, blk: (i, 0)),
                      pl.BlockSpec((None, tn, D), lambda i, j, blk: (blk[j], 0, 0)), ANY],
            out_specs=pl.BlockSpec((tm, tn), lambda i, j, blk: (i, blk[j]))),
        out_shape=_sds(proj.shape, proj.dtype),
        input_output_aliases={3: 0},
        compiler_params=_params(("parallel", "arbitrary"), 48),
    )(blocks, h, w_int3, proj)


def _lower_bound(lbp):
    l0, l1 = lbp[0:1, :], lbp[1:2, :]
    m = jnp.maximum(l0, l1)
    e0, e1 = jnp.exp(l0 - m), jnp.exp(l1 - m)
    return e0 / (e0 + e1)


def _seg_scan(x, r32, forward):
    n = x.shape[0]
    s = 1
    while s < CHUNK:
        if forward:
            x = x + jnp.where(r32 >= s, pltpu.roll(x, s, 0), 0.0)
        else:
            x = x + jnp.where(r32 < CHUNK - s, pltpu.roll(x, n - s, 0), 0.0)
        s *= 2
    return x


def _bcast_row(x, row):
    n, w = x.shape
    nc = n // CHUNK
    x3 = x.reshape(nc, CHUNK, w)
    return jnp.broadcast_to(x3[:, row:row + 1, :], (nc, CHUNK, w)).reshape(n, w)


def _chunk_total(x):
    n, w = x.shape
    nc = n // CHUNK
    total = jnp.sum(x.reshape(nc, CHUNK, w), axis=1, keepdims=True)
    return jnp.broadcast_to(total, (nc, CHUNK, w)).reshape(n, w)


def _hgrn_prep(q_raw, f_raw, lb):
    r32 = lax.broadcasted_iota(jnp.int32, f_raw.shape, 0) & (CHUNK - 1)
    sig = _sigmoid(f_raw)
    f = lb + (1.0 - lb) * sig
    b = _seg_scan(jnp.log(f), r32, True)
    a = _bcast_row(b, CHUNK // 2 - 1)
    bl = _bcast_row(b, CHUNK - 1)
    sq = _sigmoid(q_raw)
    q = q_raw * sq * (HEAD_DIM ** -0.5)
    return dict(r32=r32, sig=sig, f=f, k=1.0 - f, b=b, a=a, bl=bl, sq=sq, q=q)


def _chunk_masks(n):
    ri = lax.broadcasted_iota(jnp.int32, (n, n), 0)
    ci = lax.broadcasted_iota(jnp.int32, (n, n), 1)
    same = (ri // CHUNK) == (ci // CHUNK)
    return same & (ci <= ri), same & (ri <= ci)


def _hgrn_fwd(proj, lower_bounds, gamma, H):
    L = proj.shape[0]
    nh = H // HEAD_DIM
    TL = min(L, 256)
    nc = TL // CHUNK

    def body(q_ref, f_ref, v_ref, g_ref, lbp_ref, gam_ref, og_ref, o_ref, s_ref, st_ref):
        @pl.when(pl.program_id(0) == 0)
        def _():
            st_ref[...] = jnp.zeros_like(st_ref)

        lb = _lower_bound(lbp_ref[...])
        gam = gam_ref[...]
        mask, _ = _chunk_masks(TL)
        rowc = lax.broadcasted_iota(jnp.int32, (TL, HEAD_DIM), 0) // CHUNK
        for h in range(nh):
            hs = slice(h * HEAD_DIM, (h + 1) * HEAD_DIM)
            p = _hgrn_prep(q_ref[:, hs], f_ref[:, hs], lb[:, hs])
            v = v_ref[:, hs]
            vb = v.astype(BF16)
            vt = v.T.astype(BF16)
            q_hat = (p["q"] * jnp.exp(p["b"] - p["a"])).astype(BF16)
            k_hat = (p["k"] * jnp.exp(p["a"] - p["b"])).astype(BF16)
            q_in = (p["q"] * jnp.exp(p["b"])).astype(BF16)
            k_out = (p["k"] * jnp.exp(p["bl"] - p["b"])).astype(BF16)
            dec = jnp.exp(p["bl"])
            att = jnp.where(mask, _nt(q_hat, k_hat), 0.0).astype(BF16)
            o_intra = _nn(att, vb)
            st = st_ref[h]
            for c in range(nc):
                rs = slice(c * CHUNK, (c + 1) * CHUNK)
                stb = st.astype(BF16)
                s_ref[c, h] = stb
                o_ref[rs, hs] = o_intra[rs] + _nt(q_in[rs], stb)
                k_c = jnp.where(rowc == c, k_out, jnp.zeros_like(k_out))
                st = st * dec[c * CHUNK:c * CHUNK + 1, :] + _nn(vt, k_c)
            st_ref[h] = st
            o = o_ref[:, hs]
            _, xh = _rms_stats(o)
            gr = g_ref[:, hs]
            og_ref[:, hs] = (xh * gam * (gr * _sigmoid(gr))).astype(BF16)

    col = lambda k: pl.BlockSpec((TL, H), lambda i, k=k: (i, k))
    return _pallas_call(
        body, name="hgrn_fwd", grid=(L // TL,),
        in_specs=[col(0), col(1), col(2), col(3),
                  pl.BlockSpec(lower_bounds.shape, lambda i: (0, 0)),
                  pl.BlockSpec(gamma.shape, lambda i: (0, 0))],
        out_specs=[pl.BlockSpec((TL, H), lambda i: (i, 0)),
                   pl.BlockSpec((TL, H), lambda i: (i, 0)),
                   pl.BlockSpec((nc, nh, HEAD_DIM, HEAD_DIM), lambda i: (i, 0, 0, 0))],
        out_shape=[_sds((L, H), BF16), _sds((L, H), F32),
                   _sds((L // CHUNK, nh, HEAD_DIM, HEAD_DIM), BF16)],
        scratch_shapes=[pltpu.VMEM((nh, HEAD_DIM, HEAD_DIM), F32)],
        compiler_params=_params(("arbitrary",), 48),
    )(proj, proj, proj, proj, lower_bounds, gamma)


def _hgrn_bwd(proj, lower_bounds, gamma, o_pre, d_out, s_saved, H, after, dproj):
    L = proj.shape[0]
    nh = H // HEAD_DIM
    TL = min(L, 256)
    nc = TL // CHUNK
    nt = L // TL

    def body(q_ref, f_ref, v_ref, g_ref, lbp_ref, gam_ref, o_ref, d_ref, s_ref, after_ref, dproj_ref,
             dp_ref, red_ref, dst_ref, dsall_ref, tmp_ref):
        @pl.when(pl.program_id(0) == 0)
        def _():
            dst_ref[...] = jnp.zeros_like(dst_ref)
            red_ref[...] = jnp.zeros_like(red_ref)

        lb = _lower_bound(lbp_ref[...])
        gam = gam_ref[...]
        mask, mask_t = _chunk_masks(TL)
        rowc = lax.broadcasted_iota(jnp.int32, (TL, HEAD_DIM), 0) // CHUNK
        for h in range(nh):
            hs = slice(h * HEAD_DIM, (h + 1) * HEAD_DIM)
            qr, gr, lbh = q_ref[:, hs], g_ref[:, hs], lb[:, hs]
            p = _hgrn_prep(qr, f_ref[:, hs], lbh)
            vb = v_ref[:, hs].astype(BF16)
            eba, eab = jnp.exp(p["b"] - p["a"]), jnp.exp(p["a"] - p["b"])
            eb, elb = jnp.exp(p["b"]), jnp.exp(p["bl"] - p["b"])
            dec = jnp.exp(p["bl"])
            q_hat, k_hat = p["q"] * eba, p["k"] * eab
            q_in, k_out = p["q"] * eb, p["k"] * elb
            q_hat_b, k_hat_b = q_hat.astype(BF16), k_hat.astype(BF16)
            q_in_b, k_out_b = q_in.astype(BF16), k_out.astype(BF16)

            o, dout = o_ref[:, hs], d_ref[:, hs]
            sg = _sigmoid(gr)
            r, xh = _rms_stats(o)
            dp_ref[3, :, hs] = (dout * (xh * gam) * (sg * (1.0 + gr * (1.0 - sg)))).astype(BF16)
            dn = dout * (gr * sg)
            red_ref[1:2, hs] += jnp.sum(dn * xh, axis=0, keepdims=True)
            do = _rms_bwd(dn * gam, xh, r)
            dob = do.astype(BF16)
            dot_b = do.T.astype(BF16)

            att_t = jnp.where(mask_t, _nt(k_hat_b, q_hat_b), 0.0).astype(BF16)
            dv_intra = _nn(att_t, dob)
            datt = jnp.where(mask, _nt(dob, vb), 0.0).astype(BF16)
            dqh = _nn(datt, k_hat_b)
            datt_t = jnp.where(mask_t, _nt(vb, dob), 0.0).astype(BF16)
            dkh = _nn(datt_t, q_hat_b)

            dst = dst_ref[h]
            for c in reversed(range(nc)):
                dsall_ref[c] = dst
                q_c = jnp.where(rowc == c, q_in_b, jnp.zeros_like(q_in_b))
                dst = dst * dec[c * CHUNK:c * CHUNK + 1, :] + _nn(dot_b, q_c)
            dst_ref[h] = dst
            for c in range(nc):
                rs = slice(c * CHUNK, (c + 1) * CHUNK)
                ds_c = dsall_ref[c]
                dsb = ds_c.astype(BF16)
                st_prev = s_ref[c, h]
                tmp_ref[0, rs, :] = _nt(k_out_b[rs], dsb)
                tmp_ref[1, rs, :] = _nn(vb[rs], dsb)
                tmp_ref[2, rs, :] = _nn(dob[rs], st_prev)
                ddec = jnp.sum(ds_c * st_prev.astype(F32), axis=0, keepdims=True)
                tmp_ref[3, rs, :] = jnp.broadcast_to(ddec * dec[c * CHUNK:c * CHUNK + 1, :],
                                                     (CHUNK, HEAD_DIM))
            dko, dqi = tmp_ref[1], tmp_ref[2]
            dq = dqh * eba + dqi * eb
            dk = dkh * eab + dko * elb
            tko = dko * k_out
            db = dqh * q_hat - dkh * k_hat + dqi * q_in - tko
            dlog = _seg_scan(db, p["r32"], False) + _chunk_total(tko) + tmp_ref[3]
            df = dlog / p["f"] - dk
            sig = p["sig"]
            red_ref[0:1, hs] += jnp.sum(df * (1.0 - sig), axis=0, keepdims=True)
            dp_ref[1, :, hs] = (df * (1.0 - lbh) * sig * (1.0 - sig)).astype(BF16)
            sq = p["sq"]
            dp_ref[0, :, hs] = (dq * (HEAD_DIM ** -0.5) * (sq * (1.0 + qr * (1.0 - sq)))).astype(BF16)
            dp_ref[2, :, hs] = (dv_intra + tmp_ref[0]).astype(BF16)

    col = lambda k: pl.BlockSpec((TL, H), lambda i, k=k: (nt - 1 - i, k))
    rev = pl.BlockSpec((TL, H), lambda i: (nt - 1 - i, 0))
    return _pallas_call(
        body, name="hgrn_bwd", grid=(nt,),
        in_specs=[col(0), col(1), col(2), col(3),
                  pl.BlockSpec(lower_bounds.shape, lambda i: (0, 0)),
                  pl.BlockSpec(gamma.shape, lambda i: (0, 0)),
                  rev, rev,
                  pl.BlockSpec((nc, nh, HEAD_DIM, HEAD_DIM), lambda i: (nt - 1 - i, 0, 0, 0)), ANY, ANY],
        out_specs=[pl.BlockSpec((4, TL, H), lambda i: (0, nt - 1 - i, 0)), pl.BlockSpec((8, H), lambda i: (0, 0))],
        out_shape=[_sds(dproj.shape, BF16), _sds((8, H), F32)],
        input_output_aliases={10: 0},
        scratch_shapes=[pltpu.VMEM((nh, HEAD_DIM, HEAD_DIM), F32),
                        pltpu.VMEM((nc, HEAD_DIM, HEAD_DIM), F32),
                        pltpu.VMEM((4, TL, HEAD_DIM), F32)],
        compiler_params=_params(("arbitrary",), 48),
    )(proj, proj, proj, proj, lower_bounds, gamma, o_pre, d_out, s_saved, after, dproj)


def _shift_down(u, s, row):
    return jnp.where(row >= s, pltpu.roll(u, s, 0), 0.0)


def _shift_up(u, s, row):
    n = u.shape[0]
    return jnp.where(row < n - s, pltpu.roll(u, n - s, 0), 0.0)


def _conv_specs(L, H):
    per = H // LANES
    return [pl.BlockSpec((L, LANES), lambda j, o=o: (0, o * per + j)) for o in (4, 5, 6)]


def _conv_fwd(proj, conv_w, H, after):
    L = proj.shape[0]

    def body(c_ref, b_ref, x_ref, w_ref, after_ref, o_ref):
        row = lax.broadcasted_iota(jnp.int32, (L, LANES), 0)
        u = c_ref[...] * x_ref[...]
        w = w_ref[...]
        y = w[0:1] * _shift_down(u, 2, row) + w[1:2] * _shift_down(u, 1, row) + w[2:3] * u
        o_ref[...] = (b_ref[...] * y).astype(BF16)

    return _pallas_call(
        body, name="conv_fwd", grid=(H // LANES,),
        in_specs=_conv_specs(L, H) + [pl.BlockSpec((3, LANES), lambda j: (0, j)), ANY],
        out_specs=pl.BlockSpec((L, LANES), lambda j: (0, j)),
        out_shape=_sds((L, H), BF16),
        compiler_params=_params(("parallel",), 48),
    )(proj, proj, proj, conv_w, after)


def _conv_bwd(proj, conv_w, dcb, H, after, dproj):
    L = proj.shape[0]

    def body(c_ref, b_ref, x_ref, w_ref, d_ref, after_ref, dproj_ref, dp_ref, dw_ref):
        row = lax.broadcasted_iota(jnp.int32, (L, LANES), 0)
        cg, xb = c_ref[...], x_ref[...]
        u = cg * xb
        u1, u2 = _shift_down(u, 1, row), _shift_down(u, 2, row)
        w = w_ref[...]
        y = w[0:1] * u2 + w[1:2] * u1 + w[2:3] * u
        d = d_ref[...]
        dp_ref[1] = (d * y).astype(BF16)
        dy = d * b_ref[...]
        du = w[2:3] * dy + w[1:2] * _shift_up(dy, 1, row) + w[0:1] * _shift_up(dy, 2, row)
        dw_ref[0:1, :] = jnp.sum(dy * u2, axis=0, keepdims=True)
        dw_ref[1:2, :] = jnp.sum(dy * u1, axis=0, keepdims=True)
        dw_ref[2:3, :] = jnp.sum(dy * u, axis=0, keepdims=True)
        dp_ref[0] = (du * xb).astype(BF16)
        dp_ref[2] = (du * cg).astype(BF16)
        dp_ref[3] = jnp.zeros((L, LANES), BF16)

    blk = pl.BlockSpec((L, LANES), lambda j: (0, j))
    return _pallas_call(
        body, name="conv_bwd", grid=(H // LANES,),
        in_specs=_conv_specs(L, H) + [pl.BlockSpec((3, LANES), lambda j: (0, j)), blk, ANY, ANY],
        out_specs=[pl.BlockSpec((4, L, LANES), lambda j: (2, 0, j)), pl.BlockSpec((3, LANES), lambda j: (0, j))],
        out_shape=[_sds(dproj.shape, BF16), _sds((3, H), F32)],
        input_output_aliases={6: 0},
        compiler_params=_params(("parallel",), 56),
    )(proj, proj, proj, conv_w, dcb, after, dproj)


def _gate_specs(tm, H):
    return [pl.BlockSpec((tm, H), lambda i, k=k: (i, k)) for k in (7, 8, 9, 10)]


def _fwd_mix(og, cb, proj, x, wat, wbt, wout, g_ffn, H, after):
    L, D = x.shape
    tm = min(L, 256)

    def body(o_ref, cb_ref, ga0, ga1, gb0, gb1, x_ref, wa_ref, wb_ref, wo_ref, g_ref, after_ref,
             sa_ref, sb_ref, ta_ref, tb_ref, m_ref, x1_ref, h2_ref):
        ya, yb = _nt(o_ref[...], wa_ref[...]), _nt(cb_ref[...], wb_ref[...])
        for k, (gar, gbr) in enumerate(((ga0, gb0), (ga1, gb1))):
            cs = slice(k * H, (k + 1) * H)
            sa, sb = _sigmoid(gar[...]), _sigmoid(gbr[...])
            ma, mb = sa * ya[:, cs], sb * yb[:, cs]
            m_ref[:, cs] = (ma + mb).astype(BF16)
            sa_ref[:, cs] = sa.astype(BF16)
            sb_ref[:, cs] = sb.astype(BF16)
            ta_ref[:, cs] = (ma * (1.0 - sa)).astype(BF16)
            tb_ref[:, cs] = (mb * (1.0 - sb)).astype(BF16)
        x1 = x_ref[...] + _nn(m_ref[...], wo_ref[...])
        x1_ref[...] = x1
        _, xh = _rms_stats(x1)
        h2_ref[...] = (xh * g_ref[...]).astype(BF16)

    row = lambda w: pl.BlockSpec((tm, w), lambda i: (i, 0))
    full = lambda a: pl.BlockSpec(a.shape, lambda i: (0,) * a.ndim)
    return _pallas_call(
        body, name="fwd_mix", grid=(L // tm,),
        in_specs=[row(H), row(H)] + _gate_specs(tm, H) + [row(D), full(wat), full(wbt), full(wout),
                                                           full(g_ffn), ANY],
        out_specs=[row(D)] * 7,
        out_shape=[_sds((L, D), BF16)] * 5 + [_sds((L, D), F32), _sds((L, D), BF16)],
        compiler_params=_params(("parallel",), 56),
    )(og, cb, proj, proj, proj, proj, x, wat, wbt, wout, g_ffn, after)


def _bwd_mix(dx1b, sig_a, sig_b, dm_dga, dm_dgb, wat, wbt, wout, H, after):
    L, D = dx1b.shape
    tm = min(L, 256)

    def body(dx_ref, sa_ref, sb_ref, ta_ref, tb_ref, wa_ref, wb_ref, wo_ref, after_ref,
             dya_ref, dyb_ref, dgate_ref, do_ref, dcb_ref):
        dm = _nt(dx_ref[...], wo_ref[...])
        dga = (dm * ta_ref[...].astype(F32)).astype(BF16)
        dgb = (dm * tb_ref[...].astype(F32)).astype(BF16)
        for q, part in enumerate((dga[:, 0:H], dga[:, H:D], dgb[:, 0:H], dgb[:, H:D])):
            dgate_ref[q] = part
        dya_ref[...] = (dm * sa_ref[...].astype(F32)).astype(BF16)
        dyb_ref[...] = (dm * sb_ref[...].astype(F32)).astype(BF16)
        do_ref[...] = _nn(dya_ref[...], wa_ref[...])
        dcb_ref[...] = _nn(dyb_ref[...], wb_ref[...])

    row = lambda w: pl.BlockSpec((tm, w), lambda i: (i, 0))
    full = lambda a: pl.BlockSpec(a.shape, lambda i: (0,) * a.ndim)
    return _pallas_call(
        body, name="bwd_mix", grid=(L // tm,),
        in_specs=[row(D)] * 5 + [full(wat), full(wbt), full(wout), ANY],
        out_specs=[row(D), row(D), pl.BlockSpec((4, tm, H), lambda i: (1, i, 0)), row(H), row(H)],
        out_shape=[_sds((L, D), BF16)] * 2 + [_sds((DPROJ_BLOCKS, L, H), BF16)] + [_sds((L, H), F32)] * 2,
        compiler_params=_params(("parallel",), 56),
    )(dx1b, sig_a, sig_b, dm_dga, dm_dgb, wat, wbt, wout, after)


def _fwd_ffn_up(h2, wgt, wut):
    L, D = h2.shape
    F = wgt.shape[0]
    tn = F // 2
    tm = min(L, 512)

    def body(h_ref, wg_ref, wu_ref, sa_ref, sb_ref, s_ref):
        h = h_ref[...]
        a, b = _nt(h, wg_ref[...]), _nt(h, wu_ref[...])
        sg = _sigmoid(a)
        silu = a * sg
        sa_ref[...] = (b * sg * (1.0 + a * (1.0 - sg))).astype(BF16)
        sb_ref[...] = silu.astype(BF16)
        s_ref[...] = (silu * b).astype(BF16)

    wspec = pl.BlockSpec((tn, D), lambda j, i: (j, 0))
    ospec = pl.BlockSpec((tm, tn), lambda j, i: (i, j))
    return _pallas_call(
        body, name="fwd_ffn_up", grid=(2, L // tm),
        in_specs=[pl.BlockSpec((tm, D), lambda j, i: (i, 0)), wspec, wspec],
        out_specs=[ospec] * 3,
        out_shape=[_sds((L, F), BF16)] * 3,
        compiler_params=_params(("parallel", "parallel"), 48),
    )(h2, wgt, wut)


def _fwd_down_loss(s, wd, x1, target, g_final):
    L, D = x1.shape
    F = wd.shape[0]
    tm = min(L, 512)

    def body(s_ref, wd_ref, x1_ref, t_ref, g_ref, dx_ref, dxb_ref, red_ref):
        @pl.when(pl.program_id(0) == 0)
        def _():
            red_ref[...] = jnp.zeros_like(red_ref)

        g = g_ref[...]
        r, xh = _rms_stats(x1_ref[...] + _nn(s_ref[...], wd_ref[...]))
        e = xh * g - t_ref[...]
        dy = e * (1.0 / D)
        dx = _rms_bwd(dy * g, xh, r)
        dx_ref[...] = dx
        dxb_ref[...] = dx.astype(BF16)
        red_ref[0:1, :] += jnp.sum(dy * xh, axis=0, keepdims=True)
        red_ref[1:2, :] += jnp.broadcast_to(0.5 * jnp.sum(e * e) * (1.0 / D), (1, D))

    row = pl.BlockSpec((tm, D), lambda i: (i, 0))
    return _pallas_call(
        body, name="fwd_down_loss", grid=(L // tm,),
        in_specs=[pl.BlockSpec((tm, F), lambda i: (i, 0)), pl.BlockSpec((F, D), lambda i: (0, 0)),
                  row, row, pl.BlockSpec((1, D), lambda i: (0, 0))],
        out_specs=[row, row, pl.BlockSpec((8, D), lambda i: (0, 0))],
        out_shape=[_sds((L, D), F32), _sds((L, D), BF16), _sds((8, D), F32)],
        compiler_params=_params(("arbitrary",), 56),
    )(s, wd, x1, target, g_final)


def _bwd_down(dx2b, wd, s_a, s_b):
    L, D = dx2b.shape
    F = wd.shape[0]
    tn = F // 2
    tm = min(L, 512)

    def body(dx_ref, wd_ref, sa_ref, sb_ref, da_ref, db_ref):
        ds = _nt(dx_ref[...], wd_ref[...])
        da_ref[...] = (ds * sa_ref[...].astype(F32)).astype(BF16)
        db_ref[...] = (ds * sb_ref[...].astype(F32)).astype(BF16)

    ospec = pl.BlockSpec((tm, tn), lambda j, i: (i, j))
    return _pallas_call(
        body, name="bwd_down", grid=(2, L // tm),
        in_specs=[pl.BlockSpec((tm, D), lambda j, i: (i, 0)),
                  pl.BlockSpec((tn, D), lambda j, i: (j, 0)), ospec, ospec],
        out_specs=[ospec] * 2,
        out_shape=[_sds((L, F), BF16)] * 2,
        compiler_params=_params(("parallel", "parallel"), 48),
    )(dx2b, wd, s_a, s_b)


def _bwd_ffn_dh(da, db, wgt, wut, x1, dx2, g_ffn, after):
    L, D = x1.shape
    F = wgt.shape[0]
    tm = min(L, 256)

    def body(da_ref, db_ref, wg_ref, wu_ref, x1_ref, dx2_ref, g_ref, after_ref, dx_ref, dxb_ref, red_ref):
        @pl.when(pl.program_id(0) == 0)
        def _():
            red_ref[...] = jnp.zeros_like(red_ref)

        dh = _nn(da_ref[...], wg_ref[...]) + _nn(db_ref[...], wu_ref[...])
        r, xh = _rms_stats(x1_ref[...])
        red_ref[0:1, :] += jnp.sum(dh * xh, axis=0, keepdims=True)
        dx = dx2_ref[...] + _rms_bwd(dh * g_ref[...], xh, r)
        dx_ref[...] = dx
        dxb_ref[...] = dx.astype(BF16)

    row = pl.BlockSpec((tm, D), lambda i: (i, 0))
    aspec = pl.BlockSpec((tm, F), lambda i: (i, 0))
    wspec = pl.BlockSpec((F, D), lambda i: (0, 0))
    return _pallas_call(
        body, name="bwd_ffn_dh", grid=(L // tm,),
        in_specs=[aspec, aspec, wspec, wspec, row, row, pl.BlockSpec((1, D), lambda i: (0, 0)), ANY],
        out_specs=[row, row, pl.BlockSpec((8, D), lambda i: (0, 0))],
        out_shape=[_sds((L, D), F32), _sds((L, D), BF16), _sds((8, D), F32)],
        compiler_params=_params(("arbitrary",), 56),
    )(da, db, wgt, wut, x1, dx2, g_ffn, after)


def _bwd_in(dproj, w_int, x, dx1, g_mix, after):
    L, D = x.shape
    N = w_int.shape[0]
    H = dproj.shape[2]
    tm = min(L, 256)
    assert N == len(DPROJ_BLOCK_OF) * H

    def body(blocks_ref, w_ref, x_ref, dx1_ref, g_ref, after_ref, dx_ref, red_ref, dp_ref):
        @pl.when(pl.program_id(0) == 0)
        def _():
            red_ref[...] = jnp.zeros_like(red_ref)

        for t, block in enumerate(DPROJ_BLOCK_OF):
            dp_ref[:, t * H:(t + 1) * H] = blocks_ref[block]
        dh = _nn(dp_ref[...], w_ref[...])
        r, xh = _rms_stats(x_ref[...])
        red_ref[0:1, :] += jnp.sum(dh * xh, axis=0, keepdims=True)
        dx_ref[...] = dx1_ref[...] + _rms_bwd(dh * g_ref[...], xh, r)

    row = pl.BlockSpec((tm, D), lambda i: (i, 0))
    return _pallas_call(
        body, name="bwd_in", grid=(L // tm,),
        in_specs=[pl.BlockSpec((DPROJ_BLOCKS, tm, H), lambda i: (0, i, 0)), pl.BlockSpec((N, D), lambda i: (0, 0)),
                  row, row, pl.BlockSpec((1, D), lambda i: (0, 0)), ANY],
        out_specs=[row, pl.BlockSpec((8, D), lambda i: (0, 0))],
        out_shape=[_sds((L, D), F32), _sds((8, D), F32)],
        scratch_shapes=[pltpu.VMEM((tm, N), BF16)],
        compiler_params=_params(("arbitrary",), 56),
    )(dproj, w_int, x, dx1, g_mix, after)


def _dw_in(h, dproj, n_cols, c_idx):
    L, D = h.shape
    H = dproj.shape[2]
    tk = min(L, TK_TOKENS)
    nk = L // tk
    r2 = D // 2
    first = [(j * n_cols) // H for j in range(N_CHIPS)]
    last = [((j + 1) * n_cols - 1) // H for j in range(N_CHIPS)]
    slots = max(b - a for a, b in zip(first, last)) + 1
    plan = []
    for j in range(N_CHIPS):
        lo, hi = j * n_cols, (j + 1) * n_cols
        segments = []
        for s in range(last[j] - first[j] + 1):
            a, b = max(lo, (first[j] + s) * H), min(hi, (first[j] + s + 1) * H)
            segments.append((s, a - (first[j] + s) * H, b - a, a - lo))
        plan.append(segments)

    def body(c_ref, *refs):
        h_ref, slot_refs = refs[0], refs[1:1 + slots]
        o_ref, sib_ref, b_ref = refs[1 + slots:]
        j, k = pl.program_id(0), pl.program_id(1)
        for jj in range(N_CHIPS):
            @pl.when(j == jj)
            def _(jj=jj):
                for s, start, width, at in plan[jj]:
                    b_ref[:, at:at + width] = slot_refs[s][:, start:start + width]

        part = _tn(h_ref[...], b_ref[...])

        @pl.when(k == 0)
        def _():
            o_ref[...] = part

        @pl.when(k > 0)
        def _():
            o_ref[...] += part

        @pl.when(k == nk - 1)
        def _():
            theirs = pl.ds(pl.multiple_of((1 - c_ref[0]) * r2, 8), r2)
            sib_ref[...] = o_ref[theirs, :].astype(BF16)

    def slot_spec(s):
        blocks = [DPROJ_BLOCK_OF[min(first[j] + s, last[j])] for j in range(N_CHIPS)]

        def index(j, k, c_ref):
            block = blocks[0]
            for jj in range(1, N_CHIPS):
                block = jnp.where(j == jj, blocks[jj], block)
            return (block, k, 0)

        return pl.BlockSpec((None, tk, H), index)

    return _pallas_call(
        body, name="dw_in",
        grid_spec=pltpu.PrefetchScalarGridSpec(
            num_scalar_prefetch=1, grid=(N_CHIPS, nk),
            in_specs=[pl.BlockSpec((tk, D), lambda j, k, c_ref: (k, 0))] + [slot_spec(s) for s in range(slots)],
            out_specs=[pl.BlockSpec((None, D, n_cols), lambda j, k, c_ref: (j, 0, 0)),
                       pl.BlockSpec((None, r2, n_cols), lambda j, k, c_ref: (j, 0, 0))],
            scratch_shapes=[pltpu.VMEM((tk, n_cols), BF16)]),
        out_shape=[_sds((N_CHIPS, D, n_cols), F32), _sds((N_CHIPS, r2, n_cols), BF16)],
        compiler_params=_params(("parallel", "arbitrary"), 56),
    )(c_idx, h, *([dproj] * slots))


def _mm_tn(name, a, b, a_spec, b_spec, o_block, n_out, n_k):
    def body(a_ref, b_ref, o_ref):
        part = _tn(a_ref[...], b_ref[...])

        @pl.when(pl.program_id(1) == 0)
        def _():
            o_ref[...] = part

        @pl.when(pl.program_id(1) > 0)
        def _():
            o_ref[...] += part

    return _pallas_call(
        body, name=name, grid=(n_out, n_k),
        in_specs=[a_spec, b_spec],
        out_specs=pl.BlockSpec((None,) + o_block, lambda j, k: (j, 0, 0)),
        out_shape=_sds((n_out,) + o_block, F32),
        compiler_params=_params(("parallel", "arbitrary"), 56),
    )(a, b)


TK_TOKENS = 2048


def _dw_whole(name, pairs, by_rows):
    n = len(pairs)
    L = pairs[0][0].shape[0]
    tk = min(L, TK_TOKENS)

    def body(*refs):
        for q in range(n):
            a_ref, b_ref, o_ref = refs[2 * q], refs[2 * q + 1], refs[2 * n + q]
            part = _tn(a_ref[...], b_ref[...])
            rows, cols = o_ref.shape[1], o_ref.shape[2]
            shards = [part[j * rows:(j + 1) * rows, :] if by_rows else part[:, j * cols:(j + 1) * cols]
                      for j in range(N_CHIPS)]

            @pl.when(pl.program_id(0) == 0)
            def _(shards=shards, o_ref=o_ref):
                for j, shard in enumerate(shards):
                    o_ref[j] = shard

            @pl.when(pl.program_id(0) > 0)
            def _(shards=shards, o_ref=o_ref):
                for j, shard in enumerate(shards):
                    o_ref[j] += shard

    in_specs, out_specs, out_shape, operands = [], [], [], []
    for a, b in pairs:
        M, N = a.shape[1], b.shape[1]
        shape = (N_CHIPS, M // N_CHIPS, N) if by_rows else (N_CHIPS, M, N // N_CHIPS)
        in_specs += [pl.BlockSpec((tk, M), lambda k: (k, 0)), pl.BlockSpec((tk, N), lambda k: (k, 0))]
        out_specs.append(pl.BlockSpec(shape, lambda k: (0, 0, 0)))
        out_shape.append(_sds(shape, F32))
        operands += [a, b]
    return _pallas_call(
        body, name=name, grid=(L // tk,), in_specs=in_specs, out_specs=out_specs, out_shape=out_shape,
        compiler_params=_params(("arbitrary",), 56),
    )(*operands)


def _dw_rows2(name, a, b):
    L, M = a.shape
    N = b.shape[1]
    tk = min(L, TK_TOKENS)
    return _mm_tn(name, a, b, pl.BlockSpec((tk, M // 2), lambda j, k: (k, j)),
                  pl.BlockSpec((tk, N), lambda j, k: (k, 0)), (M // 2, N), 2, L // tk)


def _place():
    x, y, c = lax.axis_index("x"), lax.axis_index("y"), lax.axis_index("c")
    chips = [(1 - x, y), (x, 1 - y), (1 - x, 1 - y)]
    return x, y, c, 2 * x + y, chips


def _remote(src, dst, send_sem, recv_sem, device):
    return pltpu.make_async_remote_copy(src_ref=src, dst_ref=dst, send_sem=send_sem,
                                        recv_sem=recv_sem, device_id=device, device_id_type=MESH)


def _half(ref, lead, c, r2):
    return ref.at[lead, pl.ds(pl.multiple_of(c * r2, 16), r2), :]


def _cast_place(name, ws, chip_idx):
    n = len(ws)

    def body(k_ref, *refs):
        for w_ref, o_ref in zip(refs[:n], refs[n:]):
            o_ref[...] = w_ref[...].astype(BF16)

    return _pallas_call(
        body, name=name,
        grid_spec=pltpu.PrefetchScalarGridSpec(
            num_scalar_prefetch=1, grid=(2,),
            in_specs=[pl.BlockSpec((w.shape[0] // 2, w.shape[1]), lambda i, k_ref: (i, 0)) for w in ws],
            out_specs=[pl.BlockSpec((None, w.shape[0] // 2, w.shape[1]), lambda i, k_ref: (k_ref[0], i, 0))
                       for w in ws]),
        out_shape=[_sds((N_CHIPS,) + w.shape, BF16) for w in ws],
        compiler_params=_params(("parallel",), 48),
    )(chip_idx, *ws)


def _cast_place_t(name, ws, chip_idx):
    n = len(ws)
    r, cols = ws[0].shape

    def body(k_ref, *refs):
        for w_ref, o_ref in zip(refs[:n], refs[n:]):
            o_ref[...] = w_ref[...].T.astype(BF16)

    return _pallas_call(
        body, name=name,
        grid_spec=pltpu.PrefetchScalarGridSpec(
            num_scalar_prefetch=1, grid=(cols // LANES,),
            in_specs=[pl.BlockSpec((r, LANES), lambda i, k_ref: (0, i))] * n,
            out_specs=[pl.BlockSpec((None, LANES, r), lambda i, k_ref: (k_ref[0], i, 0))] * n),
        out_shape=[_sds((N_CHIPS, cols, r), BF16)] * n,
        compiler_params=_params(("parallel",), 48),
    )(chip_idx, *ws)


def _gather_copies(bufs, whole, send_sems, recv_sems, select=None):
    x, y, c, k, chips = _place()
    pairs = []
    for w, buf in enumerate(bufs):
        for j, (cx, cy) in enumerate(chips):
            if select is not None and not select(w, j):
                continue
            if w in whole:
                mine, theirs = buf.at[k], buf.at[2 * cx + cy]
            else:
                r2 = buf.shape[1] // 2
                mine, theirs = _half(buf, k, c, r2), _half(buf, 2 * cx + cy, c, r2)
            sems = (send_sems.at[w * 3 + j], recv_sems.at[w * 3 + j])
            pairs.append((_remote(mine, mine, *sems, (cx, cy, c)), _remote(theirs, theirs, *sems, (x, y, c))))
    return pairs


def _gather_start(name, groups, after):
    flat = [b for bufs, _, _ in groups for b in bufs]
    nb, ng = len(flat), len(groups)

    def body(*refs):
        ins, sems, token = refs[:nb], refs[nb + 1:nb + 1 + 2 * ng], refs[-1]
        pos = 0
        for g, (bufs, whole, select) in enumerate(groups):
            for send, _ in _gather_copies(ins[pos:pos + len(bufs)], whole, sems[2 * g], sems[2 * g + 1], select):
                send.start()
            pos += len(bufs)
        token[...] = jnp.zeros_like(token)

    sem_shapes = []
    for bufs, _, _ in groups:
        sem_shapes += [pltpu.SemaphoreType.DMA((3 * len(bufs),))] * 2
    out = _pallas_call(
        body, name=name,
        in_specs=[HBM] * nb + [ANY], out_specs=tuple([SEM] * (2 * ng) + [HBM] * nb + [VMEM]),
        out_shape=tuple(sem_shapes + [pltpu.HBM(b.shape, b.dtype) for b in flat] + [_sds((8, LANES), F32)]),
        input_output_aliases={i: 2 * ng + i for i in range(nb)},
        compiler_params=pltpu.CompilerParams(has_side_effects=EFFECT),
    )(*flat, after)
    sems, thru, pos = [], [], 2 * ng
    for g, (bufs, _, _) in enumerate(groups):
        sems.append((out[2 * g], out[2 * g + 1]))
        thru.append(list(out[pos:pos + len(bufs)]))
        pos += len(bufs)
    return sems, thru, out[-1]


def _gather_wait(name, bufs, whole, sems, after, select=None):
    nb = len(bufs)

    def body(*refs):
        ins, send_sems, recv_sems = refs[:nb], refs[nb], refs[nb + 1]
        for send, arrival in _gather_copies(ins, whole, send_sems, recv_sems, select):
            send.wait_send()
            arrival.wait_recv()

    return _pallas_call(
        body, name=name,
        in_specs=[HBM] * nb + [SEM, SEM, ANY], out_specs=[HBM] * nb,
        out_shape=[pltpu.HBM(b.shape, b.dtype) for b in bufs],
        input_output_aliases={i: i for i in range(nb)},
        compiler_params=pltpu.CompilerParams(has_side_effects=EFFECT),
    )(*bufs, sems[0], sems[1], after)


def _gather_forward(name, bufs, sources=(0, 1, 2)):
    n = len(bufs)

    def body(*refs):
        outs = refs[n:2 * n]
        send_sems, recv_sems = refs[2 * n:]
        x, y, c, _, chips = _place()
        sends = []
        for w in range(n):
            r2 = outs[w].shape[1] // 2
            for j in sources:
                landed = _half(outs[w], 2 * chips[j][0] + chips[j][1], c, r2)
                sends.append(_remote(landed, landed, send_sems.at[w * 3 + j], recv_sems.at[w * 3 + j],
                                     (x, y, 1 - c)))
        for cp in sends:
            cp.start()
        for w in range(n):
            r2 = outs[w].shape[1] // 2
            for j in sources:
                got = _half(outs[w], 2 * chips[j][0] + chips[j][1], 1 - c, r2)
                _remote(got, got, send_sems.at[w * 3 + j], recv_sems.at[w * 3 + j], (x, y, c)).wait_recv()
        for cp in sends:
            cp.wait_send()

    return _pallas_call(
        body, name=name,
        in_specs=[ANY] * n, out_specs=[ANY] * n,
        out_shape=[_sds(b.shape, b.dtype) for b in bufs],
        input_output_aliases={i: i for i in range(n)},
        scratch_shapes=[pltpu.SemaphoreType.DMA((n * 3,)), pltpu.SemaphoreType.DMA((n * 3,))],
    )(*bufs)


def _rs_add(name, grads3, from_sibling, c_idx):
    n = len(grads3)

    def body(c_ref, *refs):
        for g_ref, s_ref, o_ref in zip(refs[:n], refs[n:2 * n], refs[2 * n:]):
            o_ref[...] = (g_ref[...] + s_ref[...].astype(F32)).astype(BF16)

    mine =[pl.BlockSpec((None,) + s.shape[1:], lambda k, c_ref: (k, c_ref[0], 0)) for s in from_sibling]
    whole = [pl.BlockSpec((None,) + s.shape[1:], lambda k, c_ref: (k, 0, 0)) for s in from_sibling]
    return _pallas_call(
        body, name=name,
        grid_spec=pltpu.PrefetchScalarGridSpec(num_scalar_prefetch=1, grid=(N_CHIPS,), in_specs=mine + whole,
                                               out_specs=whole),
        out_shape=[_sds(s.shape, BF16) for s in from_sibling],
        compiler_params=_params(("parallel",), 48),
    )(c_idx, *grads3, *from_sibling)


def _split_start(name, arrays, n_sems, pairs_fn):
    n = len(arrays)

    def body(*refs):
        for send, _ in pairs_fn(refs[:n], refs[n], refs[n + 1]):
            send.start()
        refs[-1][...] = jnp.zeros_like(refs[-1])

    out = _pallas_call(
        body, name=name,
        in_specs=[HBM] * n, out_specs=tuple([SEM, SEM] + [HBM] * n + [VMEM]),
        out_shape=tuple([pltpu.SemaphoreType.DMA((n_sems,))] * 2 + [pltpu.HBM(a.shape, a.dtype) for a in arrays]
                        + [_sds((8, LANES), F32)]),
        input_output_aliases={i: 2 + i for i in range(n)},
        compiler_params=pltpu.CompilerParams(has_side_effects=EFFECT),
    )(*arrays)
    return (out[0], out[1]), list(out[2:2 + n]), out[-1]


def _split_wait(name, sems, arrays, pairs_fn, after):
    n = len(arrays)

    def body(*refs):
        for send, arrival in pairs_fn(refs[:n], refs[n], refs[n + 1]):
            send.wait_send()
            arrival.wait_recv()

    return list(_pallas_call(
        body, name=name,
        in_specs=[HBM] * n + [SEM, SEM, ANY], out_specs=[HBM] * n,
        out_shape=[pltpu.HBM(a.shape, a.dtype) for a in arrays],
        input_output_aliases={i: i for i in range(n)},
        compiler_params=pltpu.CompilerParams(has_side_effects=EFFECT),
    )(*arrays, sems[0], sems[1], after))


def _forward_pairs(bufs, send_sems, recv_sems):
    x, y, c, _, chips = _place()
    pairs = []
    for w, buf in enumerate(bufs):
        r2 = buf.shape[1] // 2
        for j, (cx, cy) in enumerate(chips):
            landed, theirs = _half(buf, 2 * cx + cy, c, r2), _half(buf, 2 * cx + cy, 1 - c, r2)
            sems = (send_sems.at[w * 3 + j], recv_sems.at[w * 3 + j])
            pairs.append((_remote(landed, landed, *sems, (x, y, 1 - c)), _remote(theirs, theirs, *sems, (x, y, c))))
    return pairs


def _sibling_pairs(arrays, send_sems, recv_sems):
    x, y, c, _, _ = _place()
    n = len(arrays) // 2
    pairs = []
    for w in range(n):
        r2 = arrays[w].shape[1] // 2
        cp = _remote(_half(arrays[w], slice(None), 1 - c, r2), arrays[n + w], send_sems.at[w], recv_sems.at[w],
                     (x, y, 1 - c))
        pairs.append((cp, cp))
    return pairs


def _sibling_whole_pairs(arrays, send_sems, recv_sems):
    x, y, c, _, _ = _place()
    n = len(arrays) // 2
    pairs = []
    for w in range(n):
        cp = _remote(arrays[w], arrays[n + w], send_sems.at[w], recv_sems.at[w], (x, y, 1 - c))
        pairs.append((cp, cp))
    return pairs


def _ici_pairs(arrays, send_sems, recv_sems):
    x, y, c, _, chips = _place()
    n = len(arrays) // 2
    pairs = []
    for w in range(n):
        for j, (cx, cy) in enumerate(chips):
            cp = _remote(arrays[w].at[2 * cx + cy], arrays[n + w].at[j],
                         send_sems.at[w * 3 + j], recv_sems.at[w * 3 + j], (cx, cy, c))
            pairs.append((cp, cp))
    return pairs


def _rs_sum(name, partials, received, place_idx):
    n = len(partials)
    nb = 2
    blocks = [(p.shape[1] // nb, p.shape[2]) for p in partials]

    def body(idx_ref, *refs):
        for p_ref, r_ref, o_ref in zip(refs[:n], refs[n:2 * n], refs[2 * n:]):
            o_ref[...] = ((p_ref[...].astype(F32) + r_ref[0].astype(F32))
                          + (r_ref[1].astype(F32) + r_ref[2].astype(F32)))

    return _pallas_call(
        body, name=name,
        grid_spec=pltpu.PrefetchScalarGridSpec(
            num_scalar_prefetch=1, grid=(nb,),
            in_specs=[pl.BlockSpec((None,) + b, lambda i, idx: (idx[0], i, 0)) for b in blocks]
            + [pl.BlockSpec((3,) + b, lambda i, idx: (0, i, 0)) for b in blocks],
            out_specs=[pl.BlockSpec(b, lambda i, idx: (idx[1] * nb + i, 0)) for b in blocks]),
        out_shape=[_sds((2 * p.shape[1], p.shape[2]), F32) for p in partials],
        compiler_params=_params(("parallel",), 48),
    )(place_idx, *partials, *received)


def _share_pairs(arrays, send_sems, recv_sems):
    x, y, c, _, _ = _place()
    pairs = []
    for w, arr in enumerate(arrays):
        r2 = arr.shape[0] // 2
        mine = arr.at[pl.ds(pl.multiple_of(c * r2, 8), r2), :]
        theirs = arr.at[pl.ds(pl.multiple_of((1 - c) * r2, 8), r2), :]
        sems = (send_sems.at[w], recv_sems.at[w])
        pairs.append((_remote(mine, mine, *sems, (x, y, 1 - c)), _remote(theirs, theirs, *sems, (x, y, c))))
    return pairs


def _small_pack(red_mix, red_ffn, red_final, red_hg, g_conv):
    D = red_mix.shape[1]
    H = red_hg.shape[1]

    def body(mix_ref, ffn_ref, fin_ref, hg_ref, cv_ref, in_ref):
        in_ref[...] = jnp.zeros_like(in_ref)
        in_ref[0:1, :] = mix_ref[0:1, :]
        in_ref[1:2, :] = ffn_ref[0:1, :]
        in_ref[2:3, :] = fin_ref[0:1, :]
        gam = hg_ref[1:2, 0:HEAD_DIM]
        for h in range(1, H // HEAD_DIM):
            gam = gam + hg_ref[1:2, h * HEAD_DIM:(h + 1) * HEAD_DIM]
        in_ref[3:4, 0:HEAD_DIM] = gam
        in_ref[3:4, HEAD_DIM:2 * HEAD_DIM] = fin_ref[1:2, 0:HEAD_DIM]
        in_ref[4:5, 0:H] = hg_ref[0:1, :]
        in_ref[6:9, 0:H] = cv_ref[...]

    return _pallas_call(
        body, name="small_pack", pin=False,
        in_specs=[VMEM] * 5, out_specs=VMEM, out_shape=_sds((N_SMALL_ROWS, D), F32),
    )(red_mix, red_ffn, red_final, red_hg, g_conv)


def _small_pairs(arrays, send_sems, recv_sems):
    block, gathered = arrays
    x, y, c, _, _ = _place()
    me = 4 * x + 2 * y + c
    pairs = []
    for m in range(1, 8):
        px, py, pc = x ^ ((m >> 2) & 1), y ^ ((m >> 1) & 1), c ^ (m & 1)
        sems = (send_sems.at[m - 1], recv_sems.at[m - 1])
        pairs.append((_remote(block, gathered.at[me], *sems, (px, py, pc)),
                      _remote(block, gathered.at[4 * px + 2 * py + pc], *sems, (x, y, c))))
    return pairs


def _adamw_math(w, g, m, v):
    m = ADAM_B1 * m + (1.0 - ADAM_B1) * g
    v = ADAM_B2 * v + (1.0 - ADAM_B2) * jnp.square(g)
    m_hat = m / (1.0 - ADAM_B1 ** ADAM_STEP)
    v_hat = v / (1.0 - ADAM_B2 ** ADAM_STEP)
    delta = -ADAM_LR * (m_hat / (jnp.sqrt(v_hat) + ADAM_EPS) + ADAM_WD * w)
    return delta, m, v


def _adamw(name, gs, ws, ms, vs):
    n = len(gs)
    nb = 4

    def body(*refs):
        ins, outs = refs[:4 * n], refs[4 * n:]
        for j in range(n):
            g_ref, w_ref, m_ref, v_ref = ins[j], ins[n + j], ins[2 * n + j], ins[3 * n + j]
            go_ref, d_ref, mo_ref, vo_ref = outs[4 * j:4 * j + 4]
            g = g_ref[...]
            go_ref[...] = g
            d_ref[...], mo_ref[...], vo_ref[...] = _adamw_math(w_ref[...], g, m_ref[...], v_ref[...])

    blk = [pl.BlockSpec((g.shape[0] // nb, g.shape[1]), lambda i: (i, 0)) for g in gs]
    out = _pallas_call(
        body, name=name, grid=(nb,),
        in_specs=blk * 4, out_specs=[b for b in blk for _ in range(4)],
        out_shape=[_sds(g.shape, F32) for g in gs for _ in range(4)],
        compiler_params=_params(("parallel",), 56),
    )(*gs, *ws, *ms, *vs)
    return [list(out[4 * j:4 * j + 4]) for j in range(n)]


def _small_update(block, gathered, place_idx, ws, ms, vs):
    n = len(ws)
    H = ws[1].shape[1]

    def body(idx_ref, blk_ref, all_ref, *refs):
        w, m, v, outs, tot_ref = refs[:n], refs[n:2 * n], refs[2 * n:3 * n], refs[3 * n:-1], refs[-1]
        chip, me = idx_ref[0], idx_ref[1]
        tot = jnp.where(me == 0, blk_ref[...], all_ref[0])
        for d in range(1, 8):
            tot = tot + jnp.where(me == d, blk_ref[...], all_ref[d])
        tot_ref[...] = tot
        p0 = _lower_bound(w[1][...])
        dl0 = p0 * (1.0 - p0) * tot_ref[4:5, 0:H]
        conv = jnp.zeros((3, LANES), F32)
        for k in range(N_CHIPS):
            conv = jnp.where(chip == k, tot_ref[6:9, k * LANES:(k + 1) * LANES], conv)
        grads = [tot_ref[0:1, :], None, tot_ref[3:4, 0:HEAD_DIM], conv, tot_ref[1:2, :], tot_ref[2:3, :]]
        for p in range(n):
            g_ref, d_ref, mo_ref, vo_ref = outs[4 * p:4 * p + 4]
            if p == 1:
                for row, g in ((slice(0, 1), dl0), (slice(1, 2), -dl0)):
                    g_ref[row, :] = g
                    d_ref[row, :], mo_ref[row, :], vo_ref[row, :] = _adamw_math(
                        w[p][row, :], g, m[p][row, :], v[p][row, :])
            else:
                g_ref[...] = grads[p]
                d_ref[...], mo_ref[...], vo_ref[...] = _adamw_math(w[p][...], grads[p], m[p][...], v[p][...])
        outs[4 * n][...] = tot_ref[3:4, HEAD_DIM:2 * HEAD_DIM]

    full = lambda a: pl.BlockSpec(a.shape, lambda i, idx: (0,) * a.ndim)
    out_shape = [_sds(w.shape, F32) for w in ws for _ in range(4)] + [_sds((1, LANES), F32)]
    return _pallas_call(
        body, name="small_update",
        grid_spec=pltpu.PrefetchScalarGridSpec(
            num_scalar_prefetch=1, grid=(1,),
            in_specs=[full(block), full(gathered)] + [full(a) for a in ws + ms + vs],
            out_specs=[full(s) for s in out_shape],
            scratch_shapes=[pltpu.VMEM(block.shape, F32)]),
        out_shape=out_shape,
    )(place_idx, block, gathered, *ws, *ms, *vs)


def kernel(x, norm_mix_g, w_in, lower_bounds, hg_norm_g, conv_w, w_branch_a, w_branch_b, w_out, norm_ffn_g, w_ffn_gate, w_ffn_up, w_ffn_down, norm_final_g, loss_target, m_norm_mix_g, m_w_in, m_lower_bounds, m_hg_norm_g, m_conv_w, m_w_branch_a, m_w_branch_b, m_w_out, m_norm_ffn_g, m_w_ffn_gate, m_w_ffn_up, m_w_ffn_down, m_norm_final_g, v_norm_mix_g, v_w_in, v_lower_bounds, v_hg_norm_g, v_conv_w, v_w_branch_a, v_w_branch_b, v_w_out, v_norm_ffn_g, v_w_ffn_gate, v_w_ffn_up, v_w_ffn_down, v_norm_final_g):
    _, L, D = x.shape
    H = D // 2
    assert lower_bounds.shape == (2, H) and hg_norm_g.shape == (1, HEAD_DIM)
    assert conv_w.shape == (1, 3, LANES) and w_in.shape[2] * N_CHIPS == 11 * H
    x2d, target = x.reshape(L, D), loss_target.reshape(L, D)
    g_final = norm_final_g.reshape(1, D)
    chip = 2 * lax.axis_index("x") + lax.axis_index("y")
    core = lax.axis_index("c")

    tr = lambda w: jnp.transpose(w[0])
    big = [w_in[0], w_branch_a[0], w_branch_b[0], w_out[0], tr(w_ffn_gate), tr(w_ffn_up), w_ffn_down[0]]
    big_m = [m_w_in[0], m_w_branch_a[0], m_w_branch_b[0], m_w_out[0], tr(m_w_ffn_gate), tr(m_w_ffn_up),
             m_w_ffn_down[0]]
    big_v = [v_w_in[0], v_w_branch_a[0], v_w_branch_b[0], v_w_out[0], tr(v_w_ffn_gate), tr(v_w_ffn_up),
             v_w_ffn_down[0]]
    names = ["w_in", "w_branch_a", "w_branch_b", "w_out", "w_ffn_gate", "w_ffn_up", "w_ffn_down"]

    chip_idx = chip.reshape(1).astype(jnp.int32)
    def per_shape(fn, tag, js, *lists):
        groups = {}
        for pos, a in enumerate(lists[0]):
            groups.setdefault(a.shape, []).append(pos)
        results = [None] * len(js)
        for same in groups.values():
            out = fn(tag + names[js[same[0]]], *[[xs[p] for p in same] for xs in lists])
            for q, p in enumerate(same):
                results[p] = out[q]
        return results

    place_t = lambda name, ws: _cast_place_t(name, ws, chip_idx)
    placed = per_shape(place_t, "place_", [0, 1, 2], big[:3]) + list(_cast_place("place_rest", big[3:], chip_idx))
    conv_placed = lax.dynamic_update_slice(jnp.zeros((N_CHIPS, 3, LANES), F32), conv_w, (chip, 0, 0))
    x_i, y_i = lax.axis_index("x"), lax.axis_index("y")
    blocks = lambda *ks: jnp.stack(ks).astype(jnp.int32)
    near = lambda w, j: j < 2
    far = lambda w, j: w == 1 or j == 2
    near_sems, in_flight, _ = _gather_start("gather_start_near", [([placed[0]], set(), near)], chip_idx)
    w_in_buf = in_flight[0][0]
    h, proj = _fwd_proj_first(x2d, norm_mix_g, w_in_buf, blocks(chip), placed[-1])
    sems, in_flight, _ = _gather_start(
        "gather_start_rest", [([w_in_buf, conv_placed], {1}, far), (placed[1:4], set(), None),
                              (placed[4:], set(), None)], h)
    w_in_buf, conv_buf = in_flight[0]
    (w_in_buf,) = _gather_wait("gather_wait_in_near", [w_in_buf], set(), near_sems[0], h, near)
    (w_in_buf,) = _gather_forward("gather_fwd_in_near", [w_in_buf], (0, 1))
    proj = _fwd_proj_more("fwd_proj_near", h, w_in_buf, proj,
                          blocks(2 * (1 - x_i) + y_i, 2 * x_i + (1 - y_i)))
    w_in_buf, conv_all = _gather_wait("gather_wait_in_far", [w_in_buf, conv_buf], {1}, sems[0], proj, far)
    (w_int3,) = _gather_forward("gather_fwd_in_far", [w_in_buf], (2,))
    proj = _fwd_proj_more("fwd_proj_far", h, w_int3, proj, blocks(2 * (1 - x_i) + (1 - y_i)))
    w_int = w_int3.reshape(-1, D)
    conv_full = jnp.transpose(conv_all, (1, 0, 2)).reshape(3, H)
    og, o_pre, s_saved = _hgrn_fwd(proj, lower_bounds, hg_norm_g, H)
    landed = _gather_wait("gather_wait_mix", in_flight[1], set(), sems[1], og)
    fwd_sems, landed, token = _split_start("gather_fwd_mix_start", landed, 9, _forward_pairs)
    cb = _conv_fwd(proj, conv_full, H, token)
    wat3, wbt3, wout3 = _split_wait("gather_fwd_mix_wait", fwd_sems, landed, _forward_pairs, cb)
    wat, wbt, wout = wat3.reshape(D, H), wbt3.reshape(D, H), wout3.reshape(D, D)
    landed = _gather_wait("gather_wait_ffn", in_flight[2], set(), sems[2], cb)
    fwd_sems, landed, token = _split_start("gather_fwd_ffn_start", landed, 9, _forward_pairs)
    sig_a, sig_b, dm_dga, dm_dgb, merged, x1, h2 = _fwd_mix(og, cb, proj, x2d, wat, wbt, wout, norm_ffn_g,
                                                              H, token)
    wgt3, wut3, wd3 = _split_wait("gather_fwd_ffn_wait", fwd_sems, landed, _forward_pairs, h2)
    d_ff = N_CHIPS * wd3.shape[1]
    wgt, wut, wd = wgt3.reshape(d_ff, D), wut3.reshape(d_ff, D), wd3.reshape(d_ff, D)
    ffn_ds_da, ffn_ds_db, ffn_s = _fwd_ffn_up(h2, wgt, wut)
    dx2, dx2b, red_final = _fwd_down_loss(ffn_s, wd, x1, target, g_final)

    c_idx = core.reshape(1).astype(jnp.int32)
    place_idx = jnp.stack([chip, core]).astype(jnp.int32)

    def sibling_start(tag, grads):
        bufs = [lax.empty((N_CHIPS, g.shape[1] // 2, g.shape[2]), F32) for g in grads]
        return _split_start("rs_sibling_start_" + tag, list(grads) + bufs, len(grads), _sibling_pairs)

    def ici_start(tag, js, grads, from_sibling):
        partials = list(_rs_add("rs_add_" + tag, grads, from_sibling, c_idx))
        landings = [lax.empty((3,) + p.shape[1:], BF16) for p in partials]
        return _split_start("rs_ici_start_" + tag, partials + landings, 3 * len(js), _ici_pairs)

    def ici_start_behind(tag, js, started, after):
        n = len(js)
        arrays = _split_wait("rs_sibling_wait_" + tag, started[0], started[1], _sibling_pairs, after)
        return ici_start(tag, js, arrays[:n], arrays[n:])

    def sums(tag, started, after):
        partials, received = [], []
        for group, group_js, start in started:
            arrays = _split_wait("rs_ici_wait_" + group, start[0], start[1], _ici_pairs, after)
            partials += arrays[:len(group_js)]
            received += arrays[len(group_js):]
        return list(_rs_sum("rs_sum_" + tag, partials, received, place_idx))

    def adamw(tag, js, grads):
        return _adamw("adamw_" + tag, grads, *[[src[j] for j in js] for src in (big, big_m, big_v)])

    shards3 = lambda g: g.reshape(N_CHIPS, d_ff // N_CHIPS, D)
    da, db = _bwd_down(dx2b, wd, ffn_ds_da, ffn_ds_db)
    g_wd = shards3(_dw_rows2("dw_ffn_down", ffn_s, dx2b))
    g_wg = shards3(_dw_rows2("dw_ffn_gate", da, h2))
    g_wu = shards3(_dw_rows2("dw_ffn_up", db, h2))
    ffn_sibling = sibling_start("ffn", [g_wg, g_wu, g_wd])
    dx1, dx1b, red_ffn = _bwd_ffn_dh(da, db, wgt, wut, x1, dx2, norm_ffn_g, ffn_sibling[2])
    ffn_ici = ici_start_behind("ffn", [4, 5, 6], ffn_sibling, dx1b)
    dya, dyb, dproj, d_o, d_cb = _bwd_mix(dx1b, sig_a, sig_b, dm_dga, dm_dgb, wat, wbt, wout, H, ffn_ici[2])
    (g_wout,) = _dw_whole("dw_out", [(merged, dx1b)], True)
    g_wa, g_wb = _dw_whole("dw_branch", [(og, dya), (cb, dyb)], False)
    mix_sibling = sibling_start("mix", [g_wa, g_wb, g_wout])
    dproj, red_hg = _hgrn_bwd(proj, lower_bounds, hg_norm_g, o_pre, d_o, s_saved, H, mix_sibling[2], dproj)
    mix_ici = ici_start_behind("mix", [1, 2, 3], mix_sibling, red_hg)
    dproj, g_conv = _conv_bwd(proj, conv_full, d_cb, H, mix_ici[2], dproj)
    g_win, for_sibling = _dw_in(h, dproj, w_int3.shape[1], c_idx)
    in_sibling = _split_start("rs_sibling_start_in", [for_sibling, lax.empty(for_sibling.shape, BF16)], 1,
                              _sibling_whole_pairs)
    halves = sums("rest", [("mix", [1, 2, 3], mix_ici), ("ffn", [4, 5, 6], ffn_ici)], in_sibling[2])
    rest_share = _split_start("rs_share_start_rest", halves, len(halves), _share_pairs)
    from_sibling = _split_wait("rs_sibling_wait_in", in_sibling[0], in_sibling[1], _sibling_whole_pairs,
                               rest_share[2])[1]
    in_ici = ici_start("in", [0], [g_win], [from_sibling])
    grad_x, red_mix = _bwd_in(dproj, w_int, x2d, dx1, norm_mix_g, in_ici[2])
    in_share = _split_start("rs_share_start_in", sums("in", [("in", [0], in_ici)], grad_x), 1, _share_pairs)
    small_block = _small_pack(red_mix, red_ffn, red_final, red_hg, g_conv)
    small = _split_start("small_gather_start", [small_block, lax.empty((8,) + small_block.shape, F32)], 7,
                         _small_pairs)
    rest_grads = _split_wait("rs_share_wait_rest", rest_share[0], rest_share[1], _share_pairs, small[2])
    big_out = [None] + adamw("rest", [1, 2, 3, 4, 5, 6], rest_grads)
    in_grad = _split_wait("rs_share_wait_in", in_share[0], in_share[1], _share_pairs, big_out[6][0])
    big_out[0] = adamw("in", [0], in_grad)[0]
    small_block, small_all = _split_wait("small_gather_wait", small[0], small[1], _small_pairs, big_out[0][0])

    def smalls(mix, lb, hg, cw, ffn, fin):
        return [mix, lb, hg, cw[0], ffn, fin.reshape(1, D)]

    small_out = _small_update(
        small_block, small_all, jnp.stack([chip, 4 * x_i + 2 * y_i + core]).astype(jnp.int32),
        smalls(norm_mix_g, lower_bounds, hg_norm_g, conv_w, norm_ffn_g, norm_final_g),
        smalls(m_norm_mix_g, m_lower_bounds, m_hg_norm_g, m_conv_w, m_norm_ffn_g, m_norm_final_g),
        smalls(v_norm_mix_g, v_lower_bounds, v_hg_norm_g, v_conv_w, v_norm_ffn_g, v_norm_final_g))

    def outputs(i):
        big_i = [big_out[j][i] for j in range(7)]
        mix, lb, hg, cw, ffn, fin = [small_out[4 * p + i] for p in range(6)]
        return [mix, big_i[0][None], lb, hg, cw[None], big_i[1][None], big_i[2][None], big_i[3][None], ffn,
                big_i[4].T[None], big_i[5].T[None], big_i[6][None], fin.reshape(D)]

    outs = [small_out[24][0, 0], grad_x.reshape(1, L, D)]
    for i in range(4):
        outs += outputs(i)
    return tuple(outs)
```

```python
import jax
import jax.numpy as jnp
from jax import lax
from jax.experimental import pallas as pl
from jax.experimental.pallas import tpu as pltpu

F32 = jnp.float32
BF16 = jnp.bfloat16
EPS = 1e-6
CHUNK = 32
HEAD_DIM = 128
LANES = 128
N_CHIPS = 4
N_SMALL_ROWS = 16
DPROJ_BLOCKS = 12
DPROJ_BLOCK_OF = (0, 1, 2, 3, 8, 9, 10, 4, 5, 6, 7)

ADAM_LR = 0.001
ADAM_B1 = 0.9
ADAM_B2 = 0.999
ADAM_EPS = 1e-08
ADAM_WD = 0.01
ADAM_STEP = 10

MESH = pl.DeviceIdType.MESH
ANY = pl.BlockSpec(memory_space=pl.ANY)
VMEM = pl.BlockSpec(memory_space=pltpu.VMEM)
HBM = pl.BlockSpec(memory_space=pltpu.HBM)
SEM = pl.BlockSpec(memory_space=pltpu.SEMAPHORE)
EFFECT = pltpu.SideEffectType.DATAFLOW_SIDE_EFFECTING


def _sds(shape, dtype):
    return jax.ShapeDtypeStruct(shape, dtype)


def _pallas_call(body, pin=True, **kwargs):
    if not pin:
        return pl.pallas_call(body, **kwargs)
    in_hbm = lambda s: pltpu.HBM(s.shape, s.dtype) if isinstance(s, jax.ShapeDtypeStruct) else s
    kwargs["out_shape"] = jax.tree.map(in_hbm, kwargs["out_shape"])
    call = pl.pallas_call(body, **kwargs)

    def run(*args):
        return call(*[pltpu.with_memory_space_constraint(a, pltpu.HBM) if a.dtype in (F32, BF16) else a
                      for a in args])

    return run


def _params(semantics, vmem_mb):
    return pltpu.CompilerParams(dimension_semantics=semantics, vmem_limit_bytes=vmem_mb << 20)


def _nn(a, b):
    return lax.dot_general(a, b, (((1,), (0,)), ((), ())), preferred_element_type=F32)


def _nt(a, b):
    return lax.dot_general(a, b, (((1,), (1,)), ((), ())), preferred_element_type=F32)


def _tn(a, b):
    return lax.dot_general(a, b, (((0,), (0,)), ((), ())), preferred_element_type=F32)


def _sigmoid(x):
    return jax.nn.sigmoid(x)


def _rms_stats(x):
    r = lax.rsqrt(jnp.mean(x * x, axis=-1, keepdims=True) + EPS)
    return r, x * r


def _rms_bwd(dxh, xh, r):
    return r * (dxh - xh * jnp.mean(dxh * xh, axis=-1, keepdims=True))


def _fwd_proj_first(x, g_mix, w_int3, block, after):
    L, D = x.shape
    tn = w_int3.shape[1]
    tm = min(L, 1024)

    def body(blk_ref, x_ref, g_ref, w_ref, after_ref, h_ref, p_ref):
        _, xh = _rms_stats(x_ref[...])
        h = (xh * g_ref[...]).astype(BF16)
        h_ref[...] = h
        p_ref[...] = _nt(h, w_ref[...])

    return _pallas_call(
        body, name="fwd_proj_own",
        grid_spec=pltpu.PrefetchScalarGridSpec(
            num_scalar_prefetch=1, grid=(L // tm,),
            in_specs=[pl.BlockSpec((tm, D), lambda i, blk: (i, 0)),
                      pl.BlockSpec((1, D), lambda i, blk: (0, 0)),
                      pl.BlockSpec((None, tn, D), lambda i, blk: (blk[0], 0, 0)), ANY],
            out_specs=[pl.BlockSpec((tm, D), lambda i, blk: (i, 0)),
                       pl.BlockSpec((tm, tn), lambda i, blk: (i, blk[0]))]),
        out_shape=[_sds((L, D), BF16), _sds((L, N_CHIPS * tn), F32)],
        compiler_params=_params(("parallel",), 48),
    )(block, x, g_mix, w_int3, after)


def _fwd_proj_more(name, h, w_int3, proj, blocks):
    L, D = h.shape
    tn = w_int3.shape[1]
    tm = min(L, 1024)

    def body(blk_ref, h_ref, w_ref, proj_ref, p_ref):
        p_ref[...] = _nt(h_ref[...], w_ref[...])

    return _pallas_call(
        body, name=name,
        grid_spec=pltpu.PrefetchScalarGridSpec(
            num_scalar_prefetch=1, grid=(L // tm, blocks.shape[0]),
            in_specs=[pl.BlockSpec((tm, D), lambda i, j, blk: (i, 0)),
                      pl.BlockSpec((None, tn, D), lambda i, j, blk: (blk[j], 0, 0)), ANY],
            out_specs=pl.BlockSpec((tm, tn), lambda i, j, blk: (i, blk[j]))),
        out_shape=_sds(proj.shape, proj.dtype),
        input_output_aliases={3: 0},
        compiler_params=_params(("parallel", "arbitrary"), 48),
    )(blocks, h, w_int3, proj)


def _lower_bound(lbp):
    l0, l1 = lbp[0:1, :], lbp[1:2, :]
    m = jnp.maximum(l0, l1)
    e0, e1 = jnp.exp(l0 - m), jnp.exp(l1 - m)
    return e0 / (e0 + e1)


def _seg_scan(x, r32, forward):
    n = x.shape[0]
    s = 1
    while s < CHUNK:
        if forward:
            x = x + jnp.where(r32 >= s, pltpu.roll(x, s, 0), 0.0)
        else:
            x = x + jnp.where(r32 < CHUNK - s, pltpu.roll(x, n - s, 0), 0.0)
        s *= 2
    return x


def _bcast_row(x, row):
    n, w = x.shape
    nc = n // CHUNK
    x3 = x.reshape(nc, CHUNK, w)
    return jnp.broadcast_to(x3[:, row:row + 1, :], (nc, CHUNK, w)).reshape(n, w)


def _chunk_total(x):
    n, w = x.shape
    nc = n // CHUNK
    total = jnp.sum(x.reshape(nc, CHUNK, w), axis=1, keepdims=True)
    return jnp.broadcast_to(total, (nc, CHUNK, w)).reshape(n, w)


def _hgrn_prep(q_raw, f_raw, lb):
    r32 = lax.broadcasted_iota(jnp.int32, f_raw.shape, 0) & (CHUNK - 1)
    sig = _sigmoid(f_raw)
    f = lb + (1.0 - lb) * sig
    b = _seg_scan(jnp.log(f), r32, True)
    a = _bcast_row(b, CHUNK // 2 - 1)
    bl = _bcast_row(b, CHUNK - 1)
    sq = _sigmoid(q_raw)
    q = q_raw * sq * (HEAD_DIM ** -0.5)
    return dict(r32=r32, sig=sig, f=f, k=1.0 - f, b=b, a=a, bl=bl, sq=sq, q=q)


def _chunk_masks(n):
    ri = lax.broadcasted_iota(jnp.int32, (n, n), 0)
    ci = lax.broadcasted_iota(jnp.int32, (n, n), 1)
    same = (ri // CHUNK) == (ci // CHUNK)
    return same & (ci <= ri), same & (ri <= ci)


def _hgrn_fwd(proj, lower_bounds, gamma, H):
    L = proj.shape[0]
    nh = H // HEAD_DIM
    TL = min(L, 256)
    nc = TL // CHUNK

    def body(q_ref, f_ref, v_ref, g_ref, lbp_ref, gam_ref, og_ref, o_ref, s_ref, st_ref):
        @pl.when(pl.program_id(0) == 0)
        def _():
            st_ref[...] = jnp.zeros_like(st_ref)

        lb = _lower_bound(lbp_ref[...])
        gam = gam_ref[...]
        mask, _ = _chunk_masks(TL)
        rowc = lax.broadcasted_iota(jnp.int32, (TL, HEAD_DIM), 0) // CHUNK
        for h in range(nh):
            hs = slice(h * HEAD_DIM, (h + 1) * HEAD_DIM)
            p = _hgrn_prep(q_ref[:, hs], f_ref[:, hs], lb[:, hs])
            v = v_ref[:, hs]
            vb = v.astype(BF16)
            vt = v.T.astype(BF16)
            q_hat = (p["q"] * jnp.exp(p["b"] - p["a"])).astype(BF16)
            k_hat = (p["k"] * jnp.exp(p["a"] - p["b"])).astype(BF16)
            q_in = (p["q"] * jnp.exp(p["b"])).astype(BF16)
            k_out = (p["k"] * jnp.exp(p["bl"] - p["b"])).astype(BF16)
            dec = jnp.exp(p["bl"])
            att = jnp.where(mask, _nt(q_hat, k_hat), 0.0).astype(BF16)
            o_intra = _nn(att, vb)
            st = st_ref[h]
            for c in range(nc):
                rs = slice(c * CHUNK, (c + 1) * CHUNK)
                stb = st.astype(BF16)
                s_ref[c, h] = stb
                o_ref[rs, hs] = o_intra[rs] + _nt(q_in[rs], stb)
                k_c = jnp.where(rowc == c, k_out, jnp.zeros_like(k_out))
                st = st * dec[c * CHUNK:c * CHUNK + 1, :] + _nn(vt, k_c)
            st_ref[h] = st
            o = o_ref[:, hs]
            _, xh = _rms_stats(o)
            gr = g_ref[:, hs]
            og_ref[:, hs] = (xh * gam * (gr * _sigmoid(gr))).astype(BF16)

    col = lambda k: pl.BlockSpec((TL, H), lambda i, k=k: (i, k))
    return _pallas_call(
        body, name="hgrn_fwd", grid=(L // TL,),
        in_specs=[col(0), col(1), col(2), col(3),
                  pl.BlockSpec(lower_bounds.shape, lambda i: (0, 0)),
                  pl.BlockSpec(gamma.shape, lambda i: (0, 0))],
        out_specs=[pl.BlockSpec((TL, H), lambda i: (i, 0)),
                   pl.BlockSpec((TL, H), lambda i: (i, 0)),
                   pl.BlockSpec((nc, nh, HEAD_DIM, HEAD_DIM), lambda i: (i, 0, 0, 0))],
        out_shape=[_sds((L, H), BF16), _sds((L, H), F32),
                   _sds((L // CHUNK, nh, HEAD_DIM, HEAD_DIM), BF16)],
        scratch_shapes=[pltpu.VMEM((nh, HEAD_DIM, HEAD_DIM), F32)],
        compiler_params=_params(("arbitrary",), 48),
    )(proj, proj, proj, proj, lower_bounds, gamma)


def _hgrn_bwd(proj, lower_bounds, gamma, o_pre, d_out, s_saved, H, after, dproj):
    L = proj.shape[0]
    nh = H // HEAD_DIM
    TL = min(L, 256)
    nc = TL // CHUNK
    nt = L // TL

    def body(q_ref, f_ref, v_ref, g_ref, lbp_ref, gam_ref, o_ref, d_ref, s_ref, after_ref, dproj_ref,
             dp_ref, red_ref, dst_ref, dsall_ref, tmp_ref):
        @pl.when(pl.program_id(0) == 0)
        def _():
            dst_ref[...] = jnp.zeros_like(dst_ref)
            red_ref[...] = jnp.zeros_like(red_ref)

        lb = _lower_bound(lbp_ref[...])
        gam = gam_ref[...]
        mask, mask_t = _chunk_masks(TL)
        rowc = lax.broadcasted_iota(jnp.int32, (TL, HEAD_DIM), 0) // CHUNK
        for h in range(nh):
            hs = slice(h * HEAD_DIM, (h + 1) * HEAD_DIM)
            qr, gr, lbh = q_ref[:, hs], g_ref[:, hs], lb[:, hs]
            p = _hgrn_prep(qr, f_ref[:, hs], lbh)
            vb = v_ref[:, hs].astype(BF16)
            eba, eab = jnp.exp(p["b"] - p["a"]), jnp.exp(p["a"] - p["b"])
            eb, elb = jnp.exp(p["b"]), jnp.exp(p["bl"] - p["b"])
            dec = jnp.exp(p["bl"])
            q_hat, k_hat = p["q"] * eba, p["k"] * eab
            q_in, k_out = p["q"] * eb, p["k"] * elb
            q_hat_b, k_hat_b = q_hat.astype(BF16), k_hat.astype(BF16)
            q_in_b, k_out_b = q_in.astype(BF16), k_out.astype(BF16)

            o, dout = o_ref[:, hs], d_ref[:, hs]
            sg = _sigmoid(gr)
            r, xh = _rms_stats(o)
            dp_ref[3, :, hs] = (dout * (xh * gam) * (sg * (1.0 + gr * (1.0 - sg)))).astype(BF16)
            dn = dout * (gr * sg)
            red_ref[1:2, hs] += jnp.sum(dn * xh, axis=0, keepdims=True)
            do = _rms_bwd(dn * gam, xh, r)
            dob = do.astype(BF16)
            dot_b = do.T.astype(BF16)

            att_t = jnp.where(mask_t, _nt(k_hat_b, q_hat_b), 0.0).astype(BF16)
            dv_intra = _nn(att_t, dob)
            datt = jnp.where(mask, _nt(dob, vb), 0.0).astype(BF16)
            dqh = _nn(datt, k_hat_b)
            datt_t = jnp.where(mask_t, _nt(vb, dob), 0.0).astype(BF16)
            dkh = _nn(datt_t, q_hat_b)

            dst = dst_ref[h]
            for c in reversed(range(nc)):
                dsall_ref[c] = dst
                q_c = jnp.where(rowc == c, q_in_b, jnp.zeros_like(q_in_b))
                dst = dst * dec[c * CHUNK:c * CHUNK + 1, :] + _nn(dot_b, q_c)
            dst_ref[h] = dst
            for c in range(nc):
                rs = slice(c * CHUNK, (c + 1) * CHUNK)
                ds_c = dsall_ref[c]
                dsb = ds_c.astype(BF16)
                st_prev = s_ref[c, h]
                tmp_ref[0, rs, :] = _nt(k_out_b[rs], dsb)
                tmp_ref[1, rs, :] = _nn(vb[rs], dsb)
                tmp_ref[2, rs, :] = _nn(dob[rs], st_prev)
                ddec = jnp.sum(ds_c * st_prev.astype(F32), axis=0, keepdims=True)
                tmp_ref[3, rs, :] = jnp.broadcast_to(ddec * dec[c * CHUNK:c * CHUNK + 1, :],
                                                     (CHUNK, HEAD_DIM))
            dko, dqi = tmp_ref[1], tmp_ref[2]
            dq = dqh * eba + dqi * eb
            dk = dkh * eab + dko * elb
            tko = dko * k_out
            db = dqh * q_hat - dkh * k_hat + dqi * q_in - tko
            dlog = _seg_scan(db, p["r32"], False) + _chunk_total(tko) + tmp_ref[3]
            df = dlog / p["f"] - dk
            sig = p["sig"]
            red_ref[0:1, hs] += jnp.sum(df * (1.0 - sig), axis=0, keepdims=True)
            dp_ref[1, :, hs] = (df * (1.0 - lbh) * sig * (1.0 - sig)).astype(BF16)
            sq = p["sq"]
            dp_ref[0, :, hs] = (dq * (HEAD_DIM ** -0.5) * (sq * (1.0 + qr * (1.0 - sq)))).astype(BF16)
            dp_ref[2, :, hs] = (dv_intra + tmp_ref[0]).astype(BF16)

    col = lambda k: pl.BlockSpec((TL, H), lambda i, k=k: (nt - 1 - i, k))
    rev = pl.BlockSpec((TL, H), lambda i: (nt - 1 - i, 0))
    return _pallas_call(
        body, name="hgrn_bwd", grid=(nt,),
        in_specs=[col(0), col(1), col(2), col(3),
                  pl.BlockSpec(lower_bounds.shape, lambda i: (0, 0)),
                  pl.BlockSpec(gamma.shape, lambda i: (0, 0)),
                  rev, rev,
                  pl.BlockSpec((nc, nh, HEAD_DIM, HEAD_DIM), lambda i: (nt - 1 - i, 0, 0, 0)), ANY, ANY],
        out_specs=[pl.BlockSpec((4, TL, H), lambda i: (0, nt - 1 - i, 0)), pl.BlockSpec((8, H), lambda i: (0, 0))],
        out_shape=[_sds(dproj.shape, BF16), _sds((8, H), F32)],
        input_output_aliases={10: 0},
        scratch_shapes=[pltpu.VMEM((nh, HEAD_DIM, HEAD_DIM), F32),
                        pltpu.VMEM((nc, HEAD_DIM, HEAD_DIM), F32),
                        pltpu.VMEM((4, TL, HEAD_DIM), F32)],
        compiler_params=_params(("arbitrary",), 48),
    )(proj, proj, proj, proj, lower_bounds, gamma, o_pre, d_out, s_saved, after, dproj)


def _shift_down(u, s, row):
    return jnp.where(row >= s, pltpu.roll(u, s, 0), 0.0)


def _shift_up(u, s, row):
    n = u.shape[0]
    return jnp.where(row < n - s, pltpu.roll(u, n - s, 0), 0.0)


def _conv_specs(L, H):
    per = H // LANES
    return [pl.BlockSpec((L, LANES), lambda j, o=o: (0, o * per + j)) for o in (4, 5, 6)]


def _conv_fwd(proj, conv_w, H, after):
    L = proj.shape[0]

    def body(c_ref, b_ref, x_ref, w_ref, after_ref, o_ref):
        row = lax.broadcasted_iota(jnp.int32, (L, LANES), 0)
        u = c_ref[...] * x_ref[...]
        w = w_ref[...]
        y = w[0:1] * _shift_down(u, 2, row) + w[1:2] * _shift_down(u, 1, row) + w[2:3] * u
        o_ref[...] = (b_ref[...] * y).astype(BF16)

    return _pallas_call(
        body, name="conv_fwd", grid=(H // LANES,),
        in_specs=_conv_specs(L, H) + [pl.BlockSpec((3, LANES), lambda j: (0, j)), ANY],
        out_specs=pl.BlockSpec((L, LANES), lambda j: (0, j)),
        out_shape=_sds((L, H), BF16),
        compiler_params=_params(("parallel",), 48),
    )(proj, proj, proj, conv_w, after)


def _conv_bwd(proj, conv_w, dcb, H, after, dproj):
    L = proj.shape[0]

    def body(c_ref, b_ref, x_ref, w_ref, d_ref, after_ref, dproj_ref, dp_ref, dw_ref):
        row = lax.broadcasted_iota(jnp.int32, (L, LANES), 0)
        cg, xb = c_ref[...], x_ref[...]
        u = cg * xb
        u1, u2 = _shift_down(u, 1, row), _shift_down(u, 2, row)
        w = w_ref[...]
        y = w[0:1] * u2 + w[1:2] * u1 + w[2:3] * u
        d = d_ref[...]
        dp_ref[1] = (d * y).astype(BF16)
        dy = d * b_ref[...]
        du = w[2:3] * dy + w[1:2] * _shift_up(dy, 1, row) + w[0:1] * _shift_up(dy, 2, row)
        dw_ref[0:1, :] = jnp.sum(dy * u2, axis=0, keepdims=True)
        dw_ref[1:2, :] = jnp.sum(dy * u1, axis=0, keepdims=True)
        dw_ref[2:3, :] = jnp.sum(dy * u, axis=0, keepdims=True)
        dp_ref[0] = (du * xb).astype(BF16)
        dp_ref[2] = (du * cg).astype(BF16)
        dp_ref[3] = jnp.zeros((L, LANES), BF16)

    blk = pl.BlockSpec((L, LANES), lambda j: (0, j))
    return _pallas_call(
        body, name="conv_bwd", grid=(H // LANES,),
        in_specs=_conv_specs(L, H) + [pl.BlockSpec((3, LANES), lambda j: (0, j)), blk, ANY, ANY],
        out_specs=[pl.BlockSpec((4, L, LANES), lambda j: (2, 0, j)), pl.BlockSpec((3, LANES), lambda j: (0, j))],
        out_shape=[_sds(dproj.shape, BF16), _sds((3, H), F32)],
        input_output_aliases={6: 0},
        compiler_params=_params(("parallel",), 56),
    )(proj, proj, proj, conv_w, dcb, after, dproj)


def _gate_specs(tm, H):
    return [pl.BlockSpec((tm, H), lambda i, k=k: (i, k)) for k in (7, 8, 9, 10)]


def _fwd_mix(og, cb, proj, x, wat, wbt, wout, g_ffn, H, after):
    L, D = x.shape
    tm = min(L, 512)

    def body(o_ref, cb_ref, ga0, ga1, gb0, gb1, x_ref, wa_ref, wb_ref, wo_ref, g_ref, after_ref,
             sa_ref, sb_ref, ta_ref, tb_ref, m_ref, x1_ref, h2_ref):
        ya, yb = _nt(o_ref[...], wa_ref[...]), _nt(cb_ref[...], wb_ref[...])
        for k, (gar, gbr) in enumerate(((ga0, gb0), (ga1, gb1))):
            cs = slice(k * H, (k + 1) * H)
            sa, sb = _sigmoid(gar[...]), _sigmoid(gbr[...])
            ma, mb = sa * ya[:, cs], sb * yb[:, cs]
            m_ref[:, cs] = (ma + mb).astype(BF16)
            sa_ref[:, cs] = sa.astype(BF16)
            sb_ref[:, cs] = sb.astype(BF16)
            ta_ref[:, cs] = (ma * (1.0 - sa)).astype(BF16)
            tb_ref[:, cs] = (mb * (1.0 - sb)).astype(BF16)
        x1 = x_ref[...] + _nn(m_ref[...], wo_ref[...])
        x1_ref[...] = x1
        _, xh = _rms_stats(x1)
        h2_ref[...] = (xh * g_ref[...]).astype(BF16)

    row = lambda w: pl.BlockSpec((tm, w), lambda i: (i, 0))
    full = lambda a: pl.BlockSpec(a.shape, lambda i: (0,) * a.ndim)
    return _pallas_call(
        body, name="fwd_mix", grid=(L // tm,),
        in_specs=[row(H), row(H)] + _gate_specs(tm, H) + [row(D), full(wat), full(wbt), full(wout),
                                                           full(g_ffn), ANY],
        out_specs=[row(D)] * 7,
        out_shape=[_sds((L, D), BF16)] * 5 + [_sds((L, D), F32), _sds((L, D), BF16)],
        compiler_params=_params(("parallel",), 56),
    )(og, cb, proj, proj, proj, proj, x, wat, wbt, wout, g_ffn, after)


def _bwd_mix(dx1b, sig_a, sig_b, dm_dga, dm_dgb, wat, wbt, wout, H, after):
    L, D = dx1b.shape
    tm = min(L, 512)

    def body(dx_ref, sa_ref, sb_ref, ta_ref, tb_ref, wa_ref, wb_ref, wo_ref, after_ref,
             dya_ref, dyb_ref, dgate_ref, do_ref, dcb_ref):
        dm = _nt(dx_ref[...], wo_ref[...])
        dga = (dm * ta_ref[...].astype(F32)).astype(BF16)
        dgb = (dm * tb_ref[...].astype(F32)).astype(BF16)
        for q, part in enumerate((dga[:, 0:H], dga[:, H:D], dgb[:, 0:H], dgb[:, H:D])):
            dgate_ref[q] = part
        dya_ref[...] = (dm * sa_ref[...].astype(F32)).astype(BF16)
        dyb_ref[...] = (dm * sb_ref[...].astype(F32)).astype(BF16)
        do_ref[...] = _nn(dya_ref[...], wa_ref[...])
        dcb_ref[...] = _nn(dyb_ref[...], wb_ref[...])

    row = lambda w: pl.BlockSpec((tm, w), lambda i: (i, 0))
    full = lambda a: pl.BlockSpec(a.shape, lambda i: (0,) * a.ndim)
    return _pallas_call(
        body, name="bwd_mix", grid=(L // tm,),
        in_specs=[row(D)] * 5 + [full(wat), full(wbt), full(wout), ANY],
        out_specs=[row(D), row(D), pl.BlockSpec((4, tm, H), lambda i: (1, i, 0)), row(H), row(H)],
        out_shape=[_sds((L, D), BF16)] * 2 + [_sds((DPROJ_BLOCKS, L, H), BF16)] + [_sds((L, H), F32)] * 2,
        compiler_params=_params(("parallel",), 56),
    )(dx1b, sig_a, sig_b, dm_dga, dm_dgb, wat, wbt, wout, after)


def _fwd_ffn_up(h2, wgt, wut):
    L, D = h2.shape
    F = wgt.shape[0]
    tn = F // 2
    tm = min(L, 512)

    def body(h_ref, wg_ref, wu_ref, sa_ref, sb_ref, s_ref):
        h = h_ref[...]
        a, b = _nt(h, wg_ref[...]), _nt(h, wu_ref[...])
        sg = _sigmoid(a)
        silu = a * sg
        sa_ref[...] = (b * sg * (1.0 + a * (1.0 - sg))).astype(BF16)
        sb_ref[...] = silu.astype(BF16)
        s_ref[...] = (silu * b).astype(BF16)

    wspec = pl.BlockSpec((tn, D), lambda j, i: (j, 0))
    ospec = pl.BlockSpec((tm, tn), lambda j, i: (i, j))
    return _pallas_call(
        body, name="fwd_ffn_up", grid=(2, L // tm),
        in_specs=[pl.BlockSpec((tm, D), lambda j, i: (i, 0)), wspec, wspec],
        out_specs=[ospec] * 3,
        out_shape=[_sds((L, F), BF16)] * 3,
        compiler_params=_params(("parallel", "parallel"), 48),
    )(h2, wgt, wut)


def _fwd_down_loss(s, wd, x1, target, g_final):
    L, D = x1.shape
    F = wd.shape[0]
    tm = min(L, 512)

    def body(s_ref, wd_ref, x1_ref, t_ref, g_ref, dx_ref, dxb_ref, red_ref):
        @pl.when(pl.program_id(0) == 0)
        def _():
            red_ref[...] = jnp.zeros_like(red_ref)

        g = g_ref[...]
        r, xh = _rms_stats(x1_ref[...] + _nn(s_ref[...], wd_ref[...]))
        e = xh * g - t_ref[...]
        dy = e * (1.0 / D)
        dx = _rms_bwd(dy * g, xh, r)
        dx_ref[...] = dx
        dxb_ref[...] = dx.astype(BF16)
        red_ref[0:1, :] += jnp.sum(dy * xh, axis=0, keepdims=True)
        red_ref[1:2, :] += jnp.broadcast_to(0.5 * jnp.sum(e * e) * (1.0 / D), (1, D))

    row = pl.BlockSpec((tm, D), lambda i: (i, 0))
    return _pallas_call(
        body, name="fwd_down_loss", grid=(L // tm,),
        in_specs=[pl.BlockSpec((tm, F), lambda i: (i, 0)), pl.BlockSpec((F, D), lambda i: (0, 0)),
                  row, row, pl.BlockSpec((1, D), lambda i: (0, 0))],
        out_specs=[row, row, pl.BlockSpec((8, D), lambda i: (0, 0))],
        out_shape=[_sds((L, D), F32), _sds((L, D), BF16), _sds((8, D), F32)],
        compiler_params=_params(("arbitrary",), 56),
    )(s, wd, x1, target, g_final)


def _bwd_down(dx2b, wd, s_a, s_b):
    L, D = dx2b.shape
    F = wd.shape[0]
    tn = F // 2
    tm = min(L, 512)

    def body(dx_ref, wd_ref, sa_ref, sb_ref, da_ref, db_ref):
        ds = _nt(dx_ref[...], wd_ref[...])
        da_ref[...] = (ds * sa_ref[...].astype(F32)).astype(BF16)
        db_ref[...] = (ds * sb_ref[...].astype(F32)).astype(BF16)

    ospec = pl.BlockSpec((tm, tn), lambda j, i: (i, j))
    return _pallas_call(
        body, name="bwd_down", grid=(2, L // tm),
        in_specs=[pl.BlockSpec((tm, D), lambda j, i: (i, 0)),
                  pl.BlockSpec((tn, D), lambda j, i: (j, 0)), ospec, ospec],
        out_specs=[ospec] * 2,
        out_shape=[_sds((L, F), BF16)] * 2,
        compiler_params=_params(("parallel", "parallel"), 48),
    )(dx2b, wd, s_a, s_b)


def _bwd_ffn_dh(da, db, wgt, wut, x1, dx2, g_ffn, after):
    L, D = x1.shape
    F = wgt.shape[0]
    tm = min(L, 256)

    def body(da_ref, db_ref, wg_ref, wu_ref, x1_ref, dx2_ref, g_ref, after_ref, dx_ref, dxb_ref, red_ref):
        @pl.when(pl.program_id(0) == 0)
        def _():
            red_ref[...] = jnp.zeros_like(red_ref)

        dh = _nn(da_ref[...], wg_ref[...]) + _nn(db_ref[...], wu_ref[...])
        r, xh = _rms_stats(x1_ref[...])
        red_ref[0:1, :] += jnp.sum(dh * xh, axis=0, keepdims=True)
        dx = dx2_ref[...] + _rms_bwd(dh * g_ref[...], xh, r)
        dx_ref[...] = dx
        dxb_ref[...] = dx.astype(BF16)

    row = pl.BlockSpec((tm, D), lambda i: (i, 0))
    aspec = pl.BlockSpec((tm, F), lambda i: (i, 0))
    wspec = pl.BlockSpec((F, D), lambda i: (0, 0))
    return _pallas_call(
        body, name="bwd_ffn_dh", grid=(L // tm,),
        in_specs=[aspec, aspec, wspec, wspec, row, row, pl.BlockSpec((1, D), lambda i: (0, 0)), ANY],
        out_specs=[row, row, pl.BlockSpec((8, D), lambda i: (0, 0))],
        out_shape=[_sds((L, D), F32), _sds((L, D), BF16), _sds((8, D), F32)],
        compiler_params=_params(("arbitrary",), 56),
    )(da, db, wgt, wut, x1, dx2, g_ffn, after)


def _bwd_in(dproj, w_int, x, dx1, g_mix, after):
    L, D = x.shape
    N = w_int.shape[0]
    H = dproj.shape[2]
    tm = min(L, 256)
    assert N == len(DPROJ_BLOCK_OF) * H

    def body(blocks_ref, w_ref, x_ref, dx1_ref, g_ref, after_ref, dx_ref, red_ref, dp_ref):
        @pl.when(pl.program_id(0) == 0)
        def _():
            red_ref[...] = jnp.zeros_like(red_ref)

        for t, block in enumerate(DPROJ_BLOCK_OF):
            dp_ref[:, t * H:(t + 1) * H] = blocks_ref[block]
        dh = _nn(dp_ref[...], w_ref[...])
        r, xh = _rms_stats(x_ref[...])
        red_ref[0:1, :] += jnp.sum(dh * xh, axis=0, keepdims=True)
        dx_ref[...] = dx1_ref[...] + _rms_bwd(dh * g_ref[...], xh, r)

    row = pl.BlockSpec((tm, D), lambda i: (i, 0))
    return _pallas_call(
        body, name="bwd_in", grid=(L // tm,),
        in_specs=[pl.BlockSpec((DPROJ_BLOCKS, tm, H), lambda i: (0, i, 0)), pl.BlockSpec((N, D), lambda i: (0, 0)),
                  row, row, pl.BlockSpec((1, D), lambda i: (0, 0)), ANY],
        out_specs=[row, pl.BlockSpec((8, D), lambda i: (0, 0))],
        out_shape=[_sds((L, D), F32), _sds((8, D), F32)],
        scratch_shapes=[pltpu.VMEM((tm, N), BF16)],
        compiler_params=_params(("arbitrary",), 56),
    )(dproj, w_int, x, dx1, g_mix, after)


def _dw_in(h, dproj, n_cols, c_idx):
    L, D = h.shape
    H = dproj.shape[2]
    tk = min(L, TK_TOKENS)
    nk = L // tk
    r2 = D // 2
    first = [(j * n_cols) // H for j in range(N_CHIPS)]
    last = [((j + 1) * n_cols - 1) // H for j in range(N_CHIPS)]
    slots = max(b - a for a, b in zip(first, last)) + 1
    plan = []
    for j in range(N_CHIPS):
        lo, hi = j * n_cols, (j + 1) * n_cols
        segments = []
        for s in range(last[j] - first[j] + 1):
            a, b = max(lo, (first[j] + s) * H), min(hi, (first[j] + s + 1) * H)
            segments.append((s, a - (first[j] + s) * H, b - a, a - lo))
        plan.append(segments)

    def body(c_ref, *refs):
        h_ref, slot_refs = refs[0], refs[1:1 + slots]
        o_ref, sib_ref, b_ref = refs[1 + slots:]
        j, k = pl.program_id(0), pl.program_id(1)
        for jj in range(N_CHIPS):
            @pl.when(j == jj)
            def _(jj=jj):
                for s, start, width, at in plan[jj]:
                    b_ref[:, at:at + width] = slot_refs[s][:, start:start + width]

        part = _tn(h_ref[...], b_ref[...])

        @pl.when(k == 0)
        def _():
            o_ref[...] = part

        @pl.when(k > 0)
        def _():
            o_ref[...] += part

        @pl.when(k == nk - 1)
        def _():
            theirs = pl.ds(pl.multiple_of((1 - c_ref[0]) * r2, 8), r2)
            sib_ref[...] = o_ref[theirs, :].astype(BF16)

    def slot_spec(s):
        blocks = [DPROJ_BLOCK_OF[min(first[j] + s, last[j])] for j in range(N_CHIPS)]

        def index(j, k, c_ref):
            block = blocks[0]
            for jj in range(1, N_CHIPS):
                block = jnp.where(j == jj, blocks[jj], block)
            return (block, k, 0)

        return pl.BlockSpec((None, tk, H), index)

    return _pallas_call(
        body, name="dw_in",
        grid_spec=pltpu.PrefetchScalarGridSpec(
            num_scalar_prefetch=1, grid=(N_CHIPS, nk),
            in_specs=[pl.BlockSpec((tk, D), lambda j, k, c_ref: (k, 0))] + [slot_spec(s) for s in range(slots)],
            out_specs=[pl.BlockSpec((None, D, n_cols), lambda j, k, c_ref: (j, 0, 0)),
                       pl.BlockSpec((None, r2, n_cols), lambda j, k, c_ref: (j, 0, 0))],
            scratch_shapes=[pltpu.VMEM((tk, n_cols), BF16)]),
        out_shape=[_sds((N_CHIPS, D, n_cols), F32), _sds((N_CHIPS, r2, n_cols), BF16)],
        compiler_params=_params(("parallel", "arbitrary"), 56),
    )(c_idx, h, *([dproj] * slots))


def _mm_tn(name, a, b, a_spec, b_spec, o_block, n_out, n_k):
    def body(a_ref, b_ref, o_ref):
        part = _tn(a_ref[...], b_ref[...])

        @pl.when(pl.program_id(1) == 0)
        def _():
            o_ref[...] = part

        @pl.when(pl.program_id(1) > 0)
        def _():
            o_ref[...] += part

    return _pallas_call(
        body, name=name, grid=(n_out, n_k),
        in_specs=[a_spec, b_spec],
        out_specs=pl.BlockSpec((None,) + o_block, lambda j, k: (j, 0, 0)),
        out_shape=_sds((n_out,) + o_block, F32),
        compiler_params=_params(("parallel", "arbitrary"), 56),
    )(a, b)


TK_TOKENS = 2048


def _dw_whole(name, pairs, by_rows):
    n = len(pairs)
    L = pairs[0][0].shape[0]
    tk = min(L, TK_TOKENS)

    def body(*refs):
        for q in range(n):
            a_ref, b_ref, o_ref = refs[2 * q], refs[2 * q + 1], refs[2 * n + q]
            part = _tn(a_ref[...], b_ref[...])
            rows, cols = o_ref.shape[1], o_ref.shape[2]
            shards = [part[j * rows:(j + 1) * rows, :] if by_rows else part[:, j * cols:(j + 1) * cols]
                      for j in range(N_CHIPS)]

            @pl.when(pl.program_id(0) == 0)
            def _(shards=shards, o_ref=o_ref):
                for j, shard in enumerate(shards):
                    o_ref[j] = shard

            @pl.when(pl.program_id(0) > 0)
            def _(shards=shards, o_ref=o_ref):
                for j, shard in enumerate(shards):
                    o_ref[j] += shard

    in_specs, out_specs, out_shape, operands = [], [], [], []
    for a, b in pairs:
        M, N = a.shape[1], b.shape[1]
        shape = (N_CHIPS, M // N_CHIPS, N) if by_rows else (N_CHIPS, M, N // N_CHIPS)
        in_specs += [pl.BlockSpec((tk, M), lambda k: (k, 0)), pl.BlockSpec((tk, N), lambda k: (k, 0))]
        out_specs.append(pl.BlockSpec(shape, lambda k: (0, 0, 0)))
        out_shape.append(_sds(shape, F32))
        operands += [a, b]
    return _pallas_call(
        body, name=name, grid=(L // tk,), in_specs=in_specs, out_specs=out_specs, out_shape=out_shape,
        compiler_params=_params(("arbitrary",), 56),
    )(*operands)


def _dw_rows2(name, a, b):
    L, M = a.shape
    N = b.shape[1]
    tk = min(L, TK_TOKENS)
    return _mm_tn(name, a, b, pl.BlockSpec((tk, M // 2), lambda j, k: (k, j)),
                  pl.BlockSpec((tk, N), lambda j, k: (k, 0)), (M // 2, N), 2, L // tk)


def _place():
    x, y, c = lax.axis_index("x"), lax.axis_index("y"), lax.axis_index("c")
    chips = [(1 - x, y), (x, 1 - y), (1 - x, 1 - y)]
    return x, y, c, 2 * x + y, chips


def _remote(src, dst, send_sem, recv_sem, device):
    return pltpu.make_async_remote_copy(src_ref=src, dst_ref=dst, send_sem=send_sem,
                                        recv_sem=recv_sem, device_id=device, device_id_type=MESH)


def _half(ref, lead, c, r2):
    return ref.at[lead, pl.ds(pl.multiple_of(c * r2, 16), r2), :]


def _cast_place(name, ws, chip_idx):
    n = len(ws)

    def body(k_ref, *refs):
        for w_ref, o_ref in zip(refs[:n], refs[n:]):
            o_ref[...] = w_ref[...].astype(BF16)

    return _pallas_call(
        body, name=name,
        grid_spec=pltpu.PrefetchScalarGridSpec(
            num_scalar_prefetch=1, grid=(2,),
            in_specs=[pl.BlockSpec((w.shape[0] // 2, w.shape[1]), lambda i, k_ref: (i, 0)) for w in ws],
            out_specs=[pl.BlockSpec((None, w.shape[0] // 2, w.shape[1]), lambda i, k_ref: (k_ref[0], i, 0))
                       for w in ws]),
        out_shape=[_sds((N_CHIPS,) + w.shape, BF16) for w in ws],
        compiler_params=_params(("parallel",), 48),
    )(chip_idx, *ws)


def _cast_place_t(name, ws, chip_idx):
    n = len(ws)
    r, cols = ws[0].shape

    def body(k_ref, *refs):
        for w_ref, o_ref in zip(refs[:n], refs[n:]):
            o_ref[...] = w_ref[...].T.astype(BF16)

    return _pallas_call(
        body, name=name,
        grid_spec=pltpu.PrefetchScalarGridSpec(
            num_scalar_prefetch=1, grid=(cols // LANES,),
            in_specs=[pl.BlockSpec((r, LANES), lambda i, k_ref: (0, i))] * n,
            out_specs=[pl.BlockSpec((None, LANES, r), lambda i, k_ref: (k_ref[0], i, 0))] * n),
        out_shape=[_sds((N_CHIPS, cols, r), BF16)] * n,
        compiler_params=_params(("parallel",), 48),
    )(chip_idx, *ws)


def _gather_copies(bufs, whole, send_sems, recv_sems, select=None):
    x, y, c, k, chips = _place()
    pairs = []
    for w, buf in enumerate(bufs):
        for j, (cx, cy) in enumerate(chips):
            if select is not None and not select(w, j):
                continue
            if w in whole:
                mine, theirs = buf.at[k], buf.at[2 * cx + cy]
            else:
                r2 = buf.shape[1] // 2
                mine, theirs = _half(buf, k, c, r2), _half(buf, 2 * cx + cy, c, r2)
            sems = (send_sems.at[w * 3 + j], recv_sems.at[w * 3 + j])
            pairs.append((_remote(mine, mine, *sems, (cx, cy, c)), _remote(theirs, theirs, *sems, (x, y, c))))
    return pairs


def _gather_start(name, groups, after):
    flat = [b for bufs, _, _ in groups for b in bufs]
    nb, ng = len(flat), len(groups)

    def body(*refs):
        ins, sems, token = refs[:nb], refs[nb + 1:nb + 1 + 2 * ng], refs[-1]
        pos = 0
        for g, (bufs, whole, select) in enumerate(groups):
            for send, _ in _gather_copies(ins[pos:pos + len(bufs)], whole, sems[2 * g], sems[2 * g + 1], select):
                send.start()
            pos += len(bufs)
        token[...] = jnp.zeros_like(token)

    sem_shapes = []
    for bufs, _, _ in groups:
        sem_shapes += [pltpu.SemaphoreType.DMA((3 * len(bufs),))] * 2
    out = _pallas_call(
        body, name=name,
        in_specs=[HBM] * nb + [ANY], out_specs=tuple([SEM] * (2 * ng) + [HBM] * nb + [VMEM]),
        out_shape=tuple(sem_shapes + [pltpu.HBM(b.shape, b.dtype) for b in flat] + [_sds((8, LANES), F32)]),
        input_output_aliases={i: 2 * ng + i for i in range(nb)},
        compiler_params=pltpu.CompilerParams(has_side_effects=EFFECT),
    )(*flat, after)
    sems, thru, pos = [], [], 2 * ng
    for g, (bufs, _, _) in enumerate(groups):
        sems.append((out[2 * g], out[2 * g + 1]))
        thru.append(list(out[pos:pos + len(bufs)]))
        pos += len(bufs)
    return sems, thru, out[-1]


def _gather_wait(name, bufs, whole, sems, after, select=None):
    nb = len(bufs)

    def body(*refs):
        ins, send_sems, recv_sems = refs[:nb], refs[nb], refs[nb + 1]
        for send, arrival in _gather_copies(ins, whole, send_sems, recv_sems, select):
            send.wait_send()
            arrival.wait_recv()

    return _pallas_call(
        body, name=name,
        in_specs=[HBM] * nb + [SEM, SEM, ANY], out_specs=[HBM] * nb,
        out_shape=[pltpu.HBM(b.shape, b.dtype) for b in bufs],
        input_output_aliases={i: i for i in range(nb)},
        compiler_params=pltpu.CompilerParams(has_side_effects=EFFECT),
    )(*bufs, sems[0], sems[1], after)


def _gather_forward(name, bufs, sources=(0, 1, 2)):
    n = len(bufs)

    def body(*refs):
        outs = refs[n:2 * n]
        send_sems, recv_sems = refs[2 * n:]
        x, y, c, _, chips = _place()
        sends = []
        for w in range(n):
            r2 = outs[w].shape[1] // 2
            for j in sources:
                landed = _half(outs[w], 2 * chips[j][0] + chips[j][1], c, r2)
                sends.append(_remote(landed, landed, send_sems.at[w * 3 + j], recv_sems.at[w * 3 + j],
                                     (x, y, 1 - c)))
        for cp in sends:
            cp.start()
        for w in range(n):
            r2 = outs[w].shape[1] // 2
            for j in sources:
                got = _half(outs[w], 2 * chips[j][0] + chips[j][1], 1 - c, r2)
                _remote(got, got, send_sems.at[w * 3 + j], recv_sems.at[w * 3 + j], (x, y, c)).wait_recv()
        for cp in sends:
            cp.wait_send()

    return _pallas_call(
        body, name=name,
        in_specs=[ANY] * n, out_specs=[ANY] * n,
        out_shape=[_sds(b.shape, b.dtype) for b in bufs],
        input_output_aliases={i: i for i in range(n)},
        scratch_shapes=[pltpu.SemaphoreType.DMA((n * 3,)), pltpu.SemaphoreType.DMA((n * 3,))],
    )(*bufs)


def _rs_add(name, grads3, from_sibling, c_idx):
    n = len(grads3)

    def body(c_ref, *refs):
        for g_ref, s_ref, o_ref in zip(refs[:n], refs[n:2 * n], refs[2 * n:]):
            o_ref[...] = (g_ref[...] + s_ref[...].astype(F32)).astype(BF16)

    mine =[pl.BlockSpec((None,) + s.shape[1:], lambda k, c_ref: (k, c_ref[0], 0)) for s in from_sibling]
    whole = [pl.BlockSpec((None,) + s.shape[1:], lambda k, c_ref: (k, 0, 0)) for s in from_sibling]
    return _pallas_call(
        body, name=name,
        grid_spec=pltpu.PrefetchScalarGridSpec(num_scalar_prefetch=1, grid=(N_CHIPS,), in_specs=mine + whole,
                                               out_specs=whole),
        out_shape=[_sds(s.shape, BF16) for s in from_sibling],
        compiler_params=_params(("parallel",), 48),
    )(c_idx, *grads3, *from_sibling)


def _split_start(name, arrays, n_sems, pairs_fn):
    n = len(arrays)

    def body(*refs):
        for send, _ in pairs_fn(refs[:n], refs[n], refs[n + 1]):
            send.start()
        refs[-1][...] = jnp.zeros_like(refs[-1])

    out = _pallas_call(
        body, name=name,
        in_specs=[HBM] * n, out_specs=tuple([SEM, SEM] + [HBM] * n + [VMEM]),
        out_shape=tuple([pltpu.SemaphoreType.DMA((n_sems,))] * 2 + [pltpu.HBM(a.shape, a.dtype) for a in arrays]
                        + [_sds((8, LANES), F32)]),
        input_output_aliases={i: 2 + i for i in range(n)},
        compiler_params=pltpu.CompilerParams(has_side_effects=EFFECT),
    )(*arrays)
    return (out[0], out[1]), list(out[2:2 + n]), out[-1]


def _split_wait(name, sems, arrays, pairs_fn, after):
    n = len(arrays)

    def body(*refs):
        for send, arrival in pairs_fn(refs[:n], refs[n], refs[n + 1]):
            send.wait_send()
            arrival.wait_recv()

    return list(_pallas_call(
        body, name=name,
        in_specs=[HBM] * n + [SEM, SEM, ANY], out_specs=[HBM] * n,
        out_shape=[pltpu.HBM(a.shape, a.dtype) for a in arrays],
        input_output_aliases={i: i for i in range(n)},
        compiler_params=pltpu.CompilerParams(has_side_effects=EFFECT),
    )(*arrays, sems[0], sems[1], after))


def _forward_pairs(bufs, send_sems, recv_sems):
    x, y, c, _, chips = _place()
    pairs = []
    for w, buf in enumerate(bufs):
        r2 = buf.shape[1] // 2
        for j, (cx, cy) in enumerate(chips):
            landed, theirs = _half(buf, 2 * cx + cy, c, r2), _half(buf, 2 * cx + cy, 1 - c, r2)
            sems = (send_sems.at[w * 3 + j], recv_sems.at[w * 3 + j])
            pairs.append((_remote(landed, landed, *sems, (x, y, 1 - c)), _remote(theirs, theirs, *sems, (x, y, c))))
    return pairs


def _sibling_pairs(arrays, send_sems, recv_sems):
    x, y, c, _, _ = _place()
    n = len(arrays) // 2
    pairs = []
    for w in range(n):
        r2 = arrays[w].shape[1] // 2
        cp = _remote(_half(arrays[w], slice(None), 1 - c, r2), arrays[n + w], send_sems.at[w], recv_sems.at[w],
                     (x, y, 1 - c))
        pairs.append((cp, cp))
    return pairs


def _sibling_whole_pairs(arrays, send_sems, recv_sems):
    x, y, c, _, _ = _place()
    n = len(arrays) // 2
    pairs = []
    for w in range(n):
        cp = _remote(arrays[w], arrays[n + w], send_sems.at[w], recv_sems.at[w], (x, y, 1 - c))
        pairs.append((cp, cp))
    return pairs


def _ici_pairs(arrays, send_sems, recv_sems):
    x, y, c, _, chips = _place()
    n = len(arrays) // 2
    pairs = []
    for w in range(n):
        for j, (cx, cy) in enumerate(chips):
            cp = _remote(arrays[w].at[2 * cx + cy], arrays[n + w].at[j],
                         send_sems.at[w * 3 + j], recv_sems.at[w * 3 + j], (cx, cy, c))
            pairs.append((cp, cp))
    return pairs


def _rs_sum(name, partials, received, place_idx):
    n = len(partials)
    nb = 2
    blocks = [(p.shape[1] // nb, p.shape[2]) for p in partials]

    def body(idx_ref, *refs):
        for p_ref, r_ref, o_ref in zip(refs[:n], refs[n:2 * n], refs[2 * n:]):
            o_ref[...] = ((p_ref[...].astype(F32) + r_ref[0].astype(F32))
                          + (r_ref[1].astype(F32) + r_ref[2].astype(F32)))

    return _pallas_call(
        body, name=name,
        grid_spec=pltpu.PrefetchScalarGridSpec(
            num_scalar_prefetch=1, grid=(nb,),
            in_specs=[pl.BlockSpec((None,) + b, lambda i, idx: (idx[0], i, 0)) for b in blocks]
            + [pl.BlockSpec((3,) + b, lambda i, idx: (0, i, 0)) for b in blocks],
            out_specs=[pl.BlockSpec(b, lambda i, idx: (idx[1] * nb + i, 0)) for b in blocks]),
        out_shape=[_sds((2 * p.shape[1], p.shape[2]), F32) for p in partials],
        compiler_params=_params(("parallel",), 48),
    )(place_idx, *partials, *received)


def _share_pairs(arrays, send_sems, recv_sems, first=0):
    x, y, c, _, _ = _place()
    pairs = []
    for w, arr in enumerate(arrays):
        r2 = arr.shape[0] // 2
        mine = arr.at[pl.ds(pl.multiple_of(c * r2, 8), r2), :]
        theirs = arr.at[pl.ds(pl.multiple_of((1 - c) * r2, 8), r2), :]
        sems = (send_sems.at[first + w], recv_sems.at[first + w])
        pairs.append((_remote(mine, mine, *sems, (x, y, 1 - c)), _remote(theirs, theirs, *sems, (x, y, c))))
    return pairs


def _small_pack(red_mix, red_ffn, red_final, red_hg, g_conv):
    D = red_mix.shape[1]
    H = red_hg.shape[1]

    def body(mix_ref, ffn_ref, fin_ref, hg_ref, cv_ref, in_ref):
        in_ref[...] = jnp.zeros_like(in_ref)
        in_ref[0:1, :] = mix_ref[0:1, :]
        in_ref[1:2, :] = ffn_ref[0:1, :]
        in_ref[2:3, :] = fin_ref[0:1, :]
        gam = hg_ref[1:2, 0:HEAD_DIM]
        for h in range(1, H // HEAD_DIM):
            gam = gam + hg_ref[1:2, h * HEAD_DIM:(h + 1) * HEAD_DIM]
        in_ref[3:4, 0:HEAD_DIM] = gam
        in_ref[3:4, HEAD_DIM:2 * HEAD_DIM] = fin_ref[1:2, 0:HEAD_DIM]
        in_ref[4:5, 0:H] = hg_ref[0:1, :]
        in_ref[6:9, 0:H] = cv_ref[...]

    return _pallas_call(
        body, name="small_pack", pin=False,
        in_specs=[VMEM] * 5, out_specs=VMEM, out_shape=_sds((N_SMALL_ROWS, D), F32),
    )(red_mix, red_ffn, red_final, red_hg, g_conv)


def _small_pairs(arrays, send_sems, recv_sems):
    block, gathered = arrays
    x, y, c, _, _ = _place()
    me = 4 * x + 2 * y + c
    pairs = []
    for m in range(1, 8):
        px, py, pc = x ^ ((m >> 2) & 1), y ^ ((m >> 1) & 1), c ^ (m & 1)
        sems = (send_sems.at[m - 1], recv_sems.at[m - 1])
        pairs.append((_remote(block, gathered.at[me], *sems, (px, py, pc)),
                      _remote(block, gathered.at[4 * px + 2 * py + pc], *sems, (x, y, c))))
    return pairs


def _tail_pairs(arrays, send_sems, recv_sems):
    return _small_pairs(arrays[:2], send_sems, recv_sems) + _share_pairs(arrays[2:], send_sems, recv_sems, 7)


def _adamw_math(w, g, m, v):
    m = ADAM_B1 * m + (1.0 - ADAM_B1) * g
    v = ADAM_B2 * v + (1.0 - ADAM_B2) * jnp.square(g)
    m_hat = m / (1.0 - ADAM_B1 ** ADAM_STEP)
    v_hat = v / (1.0 - ADAM_B2 ** ADAM_STEP)
    delta = -ADAM_LR * (m_hat / (jnp.sqrt(v_hat) + ADAM_EPS) + ADAM_WD * w)
    return delta, m, v


def _adamw(name, gs, ws, ms, vs):
    n = len(gs)
    nb = 4

    def body(*refs):
        ins, outs = refs[:4 * n], refs[4 * n:]
        for j in range(n):
            g_ref, w_ref, m_ref, v_ref = ins[j], ins[n + j], ins[2 * n + j], ins[3 * n + j]
            go_ref, d_ref, mo_ref, vo_ref = outs[4 * j:4 * j + 4]
            g = g_ref[...]
            go_ref[...] = g
            d_ref[...], mo_ref[...], vo_ref[...] = _adamw_math(w_ref[...], g, m_ref[...], v_ref[...])

    blk = [pl.BlockSpec((g.shape[0] // nb, g.shape[1]), lambda i: (i, 0)) for g in gs]
    out = _pallas_call(
        body, name=name, grid=(nb,),
        in_specs=blk * 4, out_specs=[b for b in blk for _ in range(4)],
        out_shape=[_sds(g.shape, F32) for g in gs for _ in range(4)],
        compiler_params=_params(("parallel",), 56),
    )(*gs, *ws, *ms, *vs)
    return [list(out[4 * j:4 * j + 4]) for j in range(n)]


def _small_update(block, gathered, place_idx, ws, ms, vs):
    n = len(ws)
    H = ws[1].shape[1]

    def body(idx_ref, blk_ref, all_ref, *refs):
        w, m, v, outs, tot_ref = refs[:n], refs[n:2 * n], refs[2 * n:3 * n], refs[3 * n:-1], refs[-1]
        chip, me = idx_ref[0], idx_ref[1]
        tot = jnp.where(me == 0, blk_ref[...], all_ref[0])
        for d in range(1, 8):
            tot = tot + jnp.where(me == d, blk_ref[...], all_ref[d])
        tot_ref[...] = tot
        p0 = _lower_bound(w[1][...])
        dl0 = p0 * (1.0 - p0) * tot_ref[4:5, 0:H]
        conv = jnp.zeros((3, LANES), F32)
        for k in range(N_CHIPS):
            conv = jnp.where(chip == k, tot_ref[6:9, k * LANES:(k + 1) * LANES], conv)
        grads = [tot_ref[0:1, :], None, tot_ref[3:4, 0:HEAD_DIM], conv, tot_ref[1:2, :], tot_ref[2:3, :]]
        for p in range(n):
            g_ref, d_ref, mo_ref, vo_ref = outs[4 * p:4 * p + 4]
            if p == 1:
                for row, g in ((slice(0, 1), dl0), (slice(1, 2), -dl0)):
                    g_ref[row, :] = g
                    d_ref[row, :], mo_ref[row, :], vo_ref[row, :] = _adamw_math(
                        w[p][row, :], g, m[p][row, :], v[p][row, :])
            else:
                g_ref[...] = grads[p]
                d_ref[...], mo_ref[...], vo_ref[...] = _adamw_math(w[p][...], grads[p], m[p][...], v[p][...])
        outs[4 * n][...] = tot_ref[3:4, HEAD_DIM:2 * HEAD_DIM]

    full = lambda a: pl.BlockSpec(a.shape, lambda i, idx: (0,) * a.ndim)
    out_shape = [_sds(w.shape, F32) for w in ws for _ in range(4)] + [_sds((1, LANES), F32)]
    return _pallas_call(
        body, name="small_update",
        grid_spec=pltpu.PrefetchScalarGridSpec(
            num_scalar_prefetch=1, grid=(1,),
            in_specs=[full(block), full(gathered)] + [full(a) for a in ws + ms + vs],
            out_specs=[full(s) for s in out_shape],
            scratch_shapes=[pltpu.VMEM(block.shape, F32)]),
        out_shape=out_shape,
    )(place_idx, block, gathered, *ws, *ms, *vs)


def kernel(x, norm_mix_g, w_in, lower_bounds, hg_norm_g, conv_w, w_branch_a, w_branch_b, w_out, norm_ffn_g, w_ffn_gate, w_ffn_up, w_ffn_down, norm_final_g, loss_target, m_norm_mix_g, m_w_in, m_lower_bounds, m_hg_norm_g, m_conv_w, m_w_branch_a, m_w_branch_b, m_w_out, m_norm_ffn_g, m_w_ffn_gate, m_w_ffn_up, m_w_ffn_down, m_norm_final_g, v_norm_mix_g, v_w_in, v_lower_bounds, v_hg_norm_g, v_conv_w, v_w_branch_a, v_w_branch_b, v_w_out, v_norm_ffn_g, v_w_ffn_gate, v_w_ffn_up, v_w_ffn_down, v_norm_final_g):
    _, L, D = x.shape
    H = D // 2
    assert lower_bounds.shape == (2, H) and hg_norm_g.shape == (1, HEAD_DIM)
    assert conv_w.shape == (1, 3, LANES) and w_in.shape[2] * N_CHIPS == 11 * H
    x2d, target = x.reshape(L, D), loss_target.reshape(L, D)
    g_final = norm_final_g.reshape(1, D)
    chip = 2 * lax.axis_index("x") + lax.axis_index("y")
    core = lax.axis_index("c")

    tr = lambda w: jnp.transpose(w[0])
    big = [w_in[0], w_branch_a[0], w_branch_b[0], w_out[0], tr(w_ffn_gate), tr(w_ffn_up), w_ffn_down[0]]
    big_m = [m_w_in[0], m_w_branch_a[0], m_w_branch_b[0], m_w_out[0], tr(m_w_ffn_gate), tr(m_w_ffn_up),
             m_w_ffn_down[0]]
    big_v = [v_w_in[0], v_w_branch_a[0], v_w_branch_b[0], v_w_out[0], tr(v_w_ffn_gate), tr(v_w_ffn_up),
             v_w_ffn_down[0]]
    names = ["w_in", "w_branch_a", "w_branch_b", "w_out", "w_ffn_gate", "w_ffn_up", "w_ffn_down"]

    chip_idx = chip.reshape(1).astype(jnp.int32)
    def per_shape(fn, tag, js, *lists):
        groups = {}
        for pos, a in enumerate(lists[0]):
            groups.setdefault(a.shape, []).append(pos)
        results = [None] * len(js)
        for same in groups.values():
            out = fn(tag + names[js[same[0]]], *[[xs[p] for p in same] for xs in lists])
            for q, p in enumerate(same):
                results[p] = out[q]
        return results

    place_t = lambda name, ws: _cast_place_t(name, ws, chip_idx)
    placed = per_shape(place_t, "place_", [0, 1, 2], big[:3]) + list(_cast_place("place_rest", big[3:], chip_idx))
    conv_placed = lax.dynamic_update_slice(jnp.zeros((N_CHIPS, 3, LANES), F32), conv_w, (chip, 0, 0))
    x_i, y_i = lax.axis_index("x"), lax.axis_index("y")
    blocks = lambda *ks: jnp.stack(ks).astype(jnp.int32)
    near = lambda w, j: j < 2
    far = lambda w, j: w == 1 or j == 2
    near_sems, in_flight, _ = _gather_start("gather_start_near", [([placed[0]], set(), near)], chip_idx)
    w_in_buf = in_flight[0][0]
    h, proj = _fwd_proj_first(x2d, norm_mix_g, w_in_buf, blocks(chip), placed[-1])
    sems, in_flight, _ = _gather_start(
        "gather_start_rest", [([w_in_buf, conv_placed], {1}, far), (placed[1:4], set(), None),
                              (placed[4:], set(), None)], h)
    w_in_buf, conv_buf = in_flight[0]
    (w_in_buf,) = _gather_wait("gather_wait_in_near", [w_in_buf], set(), near_sems[0], h, near)
    (w_in_buf,) = _gather_forward("gather_fwd_in_near", [w_in_buf], (0, 1))
    proj = _fwd_proj_more("fwd_proj_near", h, w_in_buf, proj,
                          blocks(2 * (1 - x_i) + y_i, 2 * x_i + (1 - y_i)))
    w_in_buf, conv_all = _gather_wait("gather_wait_in_far", [w_in_buf, conv_buf], {1}, sems[0], proj, far)
    (w_int3,) = _gather_forward("gather_fwd_in_far", [w_in_buf], (2,))
    proj = _fwd_proj_more("fwd_proj_far", h, w_int3, proj, blocks(2 * (1 - x_i) + (1 - y_i)))
    w_int = w_int3.reshape(-1, D)
    conv_full = jnp.transpose(conv_all, (1, 0, 2)).reshape(3, H)
    og, o_pre, s_saved = _hgrn_fwd(proj, lower_bounds, hg_norm_g, H)
    landed = _gather_wait("gather_wait_mix", in_flight[1], set(), sems[1], og)
    fwd_sems, landed, token = _split_start("gather_fwd_mix_start", landed, 9, _forward_pairs)
    cb = _conv_fwd(proj, conv_full, H, token)
    wat3, wbt3, wout3 = _split_wait("gather_fwd_mix_wait", fwd_sems, landed, _forward_pairs, cb)
    wat, wbt, wout = wat3.reshape(D, H), wbt3.reshape(D, H), wout3.reshape(D, D)
    landed = _gather_wait("gather_wait_ffn", in_flight[2], set(), sems[2], cb)
    fwd_sems, landed, token = _split_start("gather_fwd_ffn_start", landed, 9, _forward_pairs)
    sig_a, sig_b, dm_dga, dm_dgb, merged, x1, h2 = _fwd_mix(og, cb, proj, x2d, wat, wbt, wout, norm_ffn_g,
                                                              H, token)
    wgt3, wut3, wd3 = _split_wait("gather_fwd_ffn_wait", fwd_sems, landed, _forward_pairs, h2)
    d_ff = N_CHIPS * wd3.shape[1]
    wgt, wut, wd = wgt3.reshape(d_ff, D), wut3.reshape(d_ff, D), wd3.reshape(d_ff, D)
    ffn_ds_da, ffn_ds_db, ffn_s = _fwd_ffn_up(h2, wgt, wut)
    dx2, dx2b, red_final = _fwd_down_loss(ffn_s, wd, x1, target, g_final)

    c_idx = core.reshape(1).astype(jnp.int32)
    place_idx = jnp.stack([chip, core]).astype(jnp.int32)

    def sibling_start(tag, grads):
        bufs = [lax.empty((N_CHIPS, g.shape[1] // 2, g.shape[2]), F32) for g in grads]
        return _split_start("rs_sibling_start_" + tag, list(grads) + bufs, len(grads), _sibling_pairs)

    def ici_start(tag, js, grads, from_sibling):
        partials = list(_rs_add("rs_add_" + tag, grads, from_sibling, c_idx))
        landings = [lax.empty((3,) + p.shape[1:], BF16) for p in partials]
        return _split_start("rs_ici_start_" + tag, partials + landings, 3 * len(js), _ici_pairs)

    def ici_start_behind(tag, js, started, after):
        n = len(js)
        arrays = _split_wait("rs_sibling_wait_" + tag, started[0], started[1], _sibling_pairs, after)
        return ici_start(tag, js, arrays[:n], arrays[n:])

    def sums(tag, started, after):
        partials, received = [], []
        for group, group_js, start in started:
            arrays = _split_wait("rs_ici_wait_" + group, start[0], start[1], _ici_pairs, after)
            partials += arrays[:len(group_js)]
            received += arrays[len(group_js):]
        return list(_rs_sum("rs_sum_" + tag, partials, received, place_idx))

    def adamw(tag, js, grads):
        return _adamw("adamw_" + tag, grads, *[[src[j] for j in js] for src in (big, big_m, big_v)])

    shards3 = lambda g: g.reshape(N_CHIPS, d_ff // N_CHIPS, D)
    da, db = _bwd_down(dx2b, wd, ffn_ds_da, ffn_ds_db)
    g_wd = shards3(_dw_rows2("dw_ffn_down", ffn_s, dx2b))
    g_wg = shards3(_dw_rows2("dw_ffn_gate", da, h2))
    g_wu = shards3(_dw_rows2("dw_ffn_up", db, h2))
    ffn_sibling = sibling_start("ffn", [g_wg, g_wu, g_wd])
    dx1, dx1b, red_ffn = _bwd_ffn_dh(da, db, wgt, wut, x1, dx2, norm_ffn_g, ffn_sibling[2])
    ffn_ici = ici_start_behind("ffn", [4, 5, 6], ffn_sibling, dx1b)
    dya, dyb, dproj, d_o, d_cb = _bwd_mix(dx1b, sig_a, sig_b, dm_dga, dm_dgb, wat, wbt, wout, H, ffn_ici[2])
    (g_wout,) = _dw_whole("dw_out", [(merged, dx1b)], True)
    g_wa, g_wb = _dw_whole("dw_branch", [(og, dya), (cb, dyb)], False)
    mix_sibling = sibling_start("mix", [g_wa, g_wb, g_wout])
    dproj, red_hg = _hgrn_bwd(proj, lower_bounds, hg_norm_g, o_pre, d_o, s_saved, H, mix_sibling[2], dproj)
    mix_ici = ici_start_behind("mix", [1, 2, 3], mix_sibling, red_hg)
    dproj, g_conv = _conv_bwd(proj, conv_full, d_cb, H, mix_ici[2], dproj)
    g_win, for_sibling = _dw_in(h, dproj, w_int3.shape[1], c_idx)
    in_sibling = _split_start("rs_sibling_start_in", [for_sibling, lax.empty(for_sibling.shape, BF16)], 1,
                              _sibling_whole_pairs)
    halves = sums("rest", [("mix", [1, 2, 3], mix_ici), ("ffn", [4, 5, 6], ffn_ici)], in_sibling[2])
    rest_share = _split_start("rs_share_start_rest", halves, len(halves), _share_pairs)
    from_sibling = _split_wait("rs_sibling_wait_in", in_sibling[0], in_sibling[1], _sibling_whole_pairs,
                               rest_share[2])[1]
    in_ici = ici_start("in", [0], [g_win], [from_sibling])
    grad_x, red_mix = _bwd_in(dproj, w_int, x2d, dx1, norm_mix_g, in_ici[2])
    in_half = sums("in", [("in", [0], in_ici)], grad_x)
    small_block = _small_pack(red_mix, red_ffn, red_final, red_hg, g_conv)
    tail = _split_start("tail_start", [small_block, lax.empty((8,) + small_block.shape, F32)] + in_half, 8,
                        _tail_pairs)
    rest_grads = _split_wait("rs_share_wait_rest", rest_share[0], rest_share[1], _share_pairs, tail[2])
    big_out = [None] + adamw("rest", [1, 2, 3, 4, 5, 6], rest_grads)
    small_block, small_all, in_grad = _split_wait("tail_wait", tail[0], tail[1], _tail_pairs, big_out[6][0])
    big_out[0] = adamw("in", [0], [in_grad])[0]

    def smalls(mix, lb, hg, cw, ffn, fin):
        return [mix, lb, hg, cw[0], ffn, fin.reshape(1, D)]

    small_out = _small_update(
        small_block, small_all, jnp.stack([chip, 4 * x_i + 2 * y_i + core]).astype(jnp.int32),
        smalls(norm_mix_g, lower_bounds, hg_norm_g, conv_w, norm_ffn_g, norm_final_g),
        smalls(m_norm_mix_g, m_lower_bounds, m_hg_norm_g, m_conv_w, m_norm_ffn_g, m_norm_final_g),
        smalls(v_norm_mix_g, v_lower_bounds, v_hg_norm_g, v_conv_w, v_norm_ffn_g, v_norm_final_g))

    def outputs(i):
        big_i = [big_out[j][i] for j in range(7)]
        mix, lb, hg, cw, ffn, fin = [small_out[4 * p + i] for p in range(6)]
        return [mix, big_i[0][None], lb, hg, cw[None], big_i[1][None], big_i[2][None], big_i[3][None], ffn,
                big_i[4].T[None], big_i[5].T[None], big_i[6][None], fin.reshape(D)]

    outs = [small_out[24][0, 0], grad_x.reshape(1, L, D)]
    for i in range(4):
        outs += outputs(i)
    return tuple(outs)
```

```python
import jax
import jax.numpy as jnp
from jax import lax
from jax.experimental import pallas as pl
from jax.experimental.pallas import tpu as pltpu

F32 = jnp.float32
BF16 = jnp.bfloat16
EPS = 1e-6
CHUNK = 32
HEAD_DIM = 128
LANES = 128
N_CHIPS = 4
N_SMALL_ROWS = 16
DPROJ_BLOCKS = 12
DPROJ_BLOCK_OF = (0, 1, 2, 3, 8, 9, 10, 4, 5, 6, 7)

ADAM_LR = 0.001
ADAM_B1 = 0.9
ADAM_B2 = 0.999
ADAM_EPS = 1e-08
ADAM_WD = 0.01
ADAM_STEP = 10

MESH = pl.DeviceIdType.MESH
ANY = pl.BlockSpec(memory_space=pl.ANY)
VMEM = pl.BlockSpec(memory_space=pltpu.VMEM)
HBM = pl.BlockSpec(memory_space=pltpu.HBM)
SEM = pl.BlockSpec(memory_space=pltpu.SEMAPHORE)
EFFECT = pltpu.SideEffectType.DATAFLOW_SIDE_EFFECTING


def _sds(shape, dtype):
    return jax.ShapeDtypeStruct(shape, dtype)


def _pallas_call(body, pin=True, **kwargs):
    if not pin:
        return pl.pallas_call(body, **kwargs)
    in_hbm = lambda s: pltpu.HBM(s.shape, s.dtype) if isinstance(s, jax.ShapeDtypeStruct) else s
    kwargs["out_shape"] = jax.tree.map(in_hbm, kwargs["out_shape"])
    call = pl.pallas_call(body, **kwargs)

    def run(*args):
        return call(*[pltpu.with_memory_space_constraint(a, pltpu.HBM) if a.dtype in (F32, BF16) else a
                      for a in args])

    return run


def _params(semantics, vmem_mb):
    return pltpu.CompilerParams(dimension_semantics=semantics, vmem_limit_bytes=vmem_mb << 20)


def _nn(a, b):
    return lax.dot_general(a, b, (((1,), (0,)), ((), ())), preferred_element_type=F32)


def _nt(a, b):
    return lax.dot_general(a, b, (((1,), (1,)), ((), ())), preferred_element_type=F32)


def _tn(a, b):
    return lax.dot_general(a, b, (((0,), (0,)), ((), ())), preferred_element_type=F32)


def _sigmoid(x):
    return jax.nn.sigmoid(x)


def _rms_stats(x):
    r = lax.rsqrt(jnp.mean(x * x, axis=-1, keepdims=True) + EPS)
    return r, x * r


def _rms_bwd(dxh, xh, r):
    return r * (dxh - xh * jnp.mean(dxh * xh, axis=-1, keepdims=True))


def _fwd_proj_first(x, g_mix, w_int3, block, after):
    L, D = x.shape
    tn = w_int3.shape[1]
    tm = min(L, 1024)

    def body(blk_ref, x_ref, g_ref, w_ref, after_ref, h_ref, p_ref):
        _, xh = _rms_stats(x_ref[...])
        h = (xh * g_ref[...]).astype(BF16)
        h_ref[...] = h
        p_ref[...] = _nt(h, w_ref[...])

    return _pallas_call(
        body, name="fwd_proj_own",
        grid_spec=pltpu.PrefetchScalarGridSpec(
            num_scalar_prefetch=1, grid=(L // tm,),
            in_specs=[pl.BlockSpec((tm, D), lambda i, blk: (i, 0)),
                      pl.BlockSpec((1, D), lambda i, blk: (0, 0)),
                      pl.BlockSpec((None, tn, D), lambda i, blk: (blk[0], 0, 0)), ANY],
            out_specs=[pl.BlockSpec((tm, D), lambda i, blk: (i, 0)),
                       pl.BlockSpec((tm, tn), lambda i, blk: (i, blk[0]))]),
        out_shape=[_sds((L, D), BF16), _sds((L, N_CHIPS * tn), F32)],
        compiler_params=_params(("parallel",), 48),
    )(block, x, g_mix, w_int3, after)


def _fwd_proj_more(name, h, w_int3, proj, blocks):
    L, D = h.shape
    tn = w_int3.shape[1]
    tm = min(L, 1024)

    def body(blk_ref, h_ref, w_ref, proj_ref, p_ref):
        p_ref[...] = _nt(h_ref[...], w_ref[...])

    return _pallas_call(
        body, name=name,
        grid_spec=pltpu.PrefetchScalarGridSpec(
            num_scalar_prefetch=1, grid=(L // tm, blocks.shape[0]),
            in_specs=[pl.BlockSpec((tm, D), lambda i, j, blk: (i, 0)),
                      pl.BlockSpec((None, tn, D), lambda i, j, blk: (blk[j], 0, 0)), ANY],
            out_specs=pl.BlockSpec((tm, tn), lambda i, j, blk: (i, blk[j]))),
        out_shape=_sds(proj.shape, proj.dtype),
        input_output_aliases={3: 0},
        compiler_params=_params(("parallel", "arbitrary"), 48),
    )(blocks, h, w_int3, proj)


def _lower_bound(lbp):
    l0, l1 = lbp[0:1, :], lbp[1:2, :]
    m = jnp.maximum(l0, l1)
    e0, e1 = jnp.exp(l0 - m), jnp.exp(l1 - m)
    return e0 / (e0 + e1)


def _seg_scan(x, r32, forward):
    n = x.shape[0]
    s = 1
    while s < CHUNK:
        if forward:
            x = x + jnp.where(r32 >= s, pltpu.roll(x, s, 0), 0.0)
        else:
            x = x + jnp.where(r32 < CHUNK - s, pltpu.roll(x, n - s, 0), 0.0)
        s *= 2
    return x


def _bcast_row(x, row):
    n, w = x.shape
    nc = n // CHUNK
    x3 = x.reshape(nc, CHUNK, w)
    return jnp.broadcast_to(x3[:, row:row + 1, :], (nc, CHUNK, w)).reshape(n, w)


def _chunk_total(x):
    n, w = x.shape
    nc = n // CHUNK
    total = jnp.sum(x.reshape(nc, CHUNK, w), axis=1, keepdims=True)
    return jnp.broadcast_to(total, (nc, CHUNK, w)).reshape(n, w)


def _hgrn_prep(q_raw, f_raw, lb):
    r32 = lax.broadcasted_iota(jnp.int32, f_raw.shape, 0) & (CHUNK - 1)
    sig = _sigmoid(f_raw)
    f = lb + (1.0 - lb) * sig
    b = _seg_scan(jnp.log(f), r32, True)
    a = _bcast_row(b, CHUNK // 2 - 1)
    bl = _bcast_row(b, CHUNK - 1)
    sq = _sigmoid(q_raw)
    q = q_raw * sq * (HEAD_DIM ** -0.5)
    return dict(r32=r32, sig=sig, f=f, k=1.0 - f, b=b, a=a, bl=bl, sq=sq, q=q)


def _chunk_masks(n):
    ri = lax.broadcasted_iota(jnp.int32, (n, n), 0)
    ci = lax.broadcasted_iota(jnp.int32, (n, n), 1)
    same = (ri // CHUNK) == (ci // CHUNK)
    return same & (ci <= ri), same & (ri <= ci)


def _hgrn_fwd(proj, lower_bounds, gamma, H):
    L = proj.shape[0]
    nh = H // HEAD_DIM
    TL = min(L, 256)
    nc = TL // CHUNK

    def body(q_ref, f_ref, v_ref, g_ref, lbp_ref, gam_ref, og_ref, o_ref, s_ref, st_ref):
        @pl.when(pl.program_id(0) == 0)
        def _():
            st_ref[...] = jnp.zeros_like(st_ref)

        lb = _lower_bound(lbp_ref[...])
        gam = gam_ref[...]
        mask, _ = _chunk_masks(TL)
        rowc = lax.broadcasted_iota(jnp.int32, (TL, HEAD_DIM), 0) // CHUNK
        for h in range(nh):
            hs = slice(h * HEAD_DIM, (h + 1) * HEAD_DIM)
            p = _hgrn_prep(q_ref[:, hs], f_ref[:, hs], lb[:, hs])
            v = v_ref[:, hs]
            vb = v.astype(BF16)
            vt = v.T.astype(BF16)
            q_hat = (p["q"] * jnp.exp(p["b"] - p["a"])).astype(BF16)
            k_hat = (p["k"] * jnp.exp(p["a"] - p["b"])).astype(BF16)
            q_in = (p["q"] * jnp.exp(p["b"])).astype(BF16)
            k_out = (p["k"] * jnp.exp(p["bl"] - p["b"])).astype(BF16)
            dec = jnp.exp(p["bl"])
            att = jnp.where(mask, _nt(q_hat, k_hat), 0.0).astype(BF16)
            o_intra = _nn(att, vb)
            st = st_ref[h]
            for c in range(nc):
                rs = slice(c * CHUNK, (c + 1) * CHUNK)
                stb = st.astype(BF16)
                s_ref[c, h] = stb
                o_ref[rs, hs] = o_intra[rs] + _nt(q_in[rs], stb)
                k_c = jnp.where(rowc == c, k_out, jnp.zeros_like(k_out))
                st = st * dec[c * CHUNK:c * CHUNK + 1, :] + _nn(vt, k_c)
            st_ref[h] = st
            o = o_ref[:, hs]
            _, xh = _rms_stats(o)
            gr = g_ref[:, hs]
            og_ref[:, hs] = (xh * gam * (gr * _sigmoid(gr))).astype(BF16)

    col = lambda k: pl.BlockSpec((TL, H), lambda i, k=k: (i, k))
    return _pallas_call(
        body, name="hgrn_fwd", grid=(L // TL,),
        in_specs=[col(0), col(1), col(2), col(3),
                  pl.BlockSpec(lower_bounds.shape, lambda i: (0, 0)),
                  pl.BlockSpec(gamma.shape, lambda i: (0, 0))],
        out_specs=[pl.BlockSpec((TL, H), lambda i: (i, 0)),
                   pl.BlockSpec((TL, H), lambda i: (i, 0)),
                   pl.BlockSpec((nc, nh, HEAD_DIM, HEAD_DIM), lambda i: (i, 0, 0, 0))],
        out_shape=[_sds((L, H), BF16), _sds((L, H), F32),
                   _sds((L // CHUNK, nh, HEAD_DIM, HEAD_DIM), BF16)],
        scratch_shapes=[pltpu.VMEM((nh, HEAD_DIM, HEAD_DIM), F32)],
        compiler_params=_params(("arbitrary",), 48),
    )(proj, proj, proj, proj, lower_bounds, gamma)


def _hgrn_bwd(proj, lower_bounds, gamma, o_pre, d_out, s_saved, H, after, dproj):
    L = proj.shape[0]
    nh = H // HEAD_DIM
    TL = min(L, 256)
    nc = TL // CHUNK
    nt = L // TL

    def body(q_ref, f_ref, v_ref, g_ref, lbp_ref, gam_ref, o_ref, d_ref, s_ref, after_ref, dproj_ref,
             dp_ref, red_ref, dst_ref, dsall_ref, tmp_ref):
        @pl.when(pl.program_id(0) == 0)
        def _():
            dst_ref[...] = jnp.zeros_like(dst_ref)
            red_ref[...] = jnp.zeros_like(red_ref)

        lb = _lower_bound(lbp_ref[...])
        gam = gam_ref[...]
        mask, mask_t = _chunk_masks(TL)
        rowc = lax.broadcasted_iota(jnp.int32, (TL, HEAD_DIM), 0) // CHUNK
        for h in range(nh):
            hs = slice(h * HEAD_DIM, (h + 1) * HEAD_DIM)
            qr, gr, lbh = q_ref[:, hs], g_ref[:, hs], lb[:, hs]
            p = _hgrn_prep(qr, f_ref[:, hs], lbh)
            vb = v_ref[:, hs].astype(BF16)
            eba, eab = jnp.exp(p["b"] - p["a"]), jnp.exp(p["a"] - p["b"])
            eb, elb = jnp.exp(p["b"]), jnp.exp(p["bl"] - p["b"])
            dec = jnp.exp(p["bl"])
            q_hat, k_hat = p["q"] * eba, p["k"] * eab
            q_in, k_out = p["q"] * eb, p["k"] * elb
            q_hat_b, k_hat_b = q_hat.astype(BF16), k_hat.astype(BF16)
            q_in_b, k_out_b = q_in.astype(BF16), k_out.astype(BF16)

            o, dout = o_ref[:, hs], d_ref[:, hs]
            sg = _sigmoid(gr)
            r, xh = _rms_stats(o)
            dp_ref[3, :, hs] = (dout * (xh * gam) * (sg * (1.0 + gr * (1.0 - sg)))).astype(BF16)
            dn = dout * (gr * sg)
            red_ref[1:2, hs] += jnp.sum(dn * xh, axis=0, keepdims=True)
            do = _rms_bwd(dn * gam, xh, r)
            dob = do.astype(BF16)
            dot_b = do.T.astype(BF16)

            att_t = jnp.where(mask_t, _nt(k_hat_b, q_hat_b), 0.0).astype(BF16)
            dv_intra = _nn(att_t, dob)
            datt = jnp.where(mask, _nt(dob, vb), 0.0).astype(BF16)
            dqh = _nn(datt, k_hat_b)
            datt_t = jnp.where(mask_t, _nt(vb, dob), 0.0).astype(BF16)
            dkh = _nn(datt_t, q_hat_b)

            dst = dst_ref[h]
            for c in reversed(range(nc)):
                dsall_ref[c] = dst
                q_c = jnp.where(rowc == c, q_in_b, jnp.zeros_like(q_in_b))
                dst = dst * dec[c * CHUNK:c * CHUNK + 1, :] + _nn(dot_b, q_c)
            dst_ref[h] = dst
            for c in range(nc):
                rs = slice(c * CHUNK, (c + 1) * CHUNK)
                ds_c = dsall_ref[c]
                dsb = ds_c.astype(BF16)
                st_prev = s_ref[c, h]
                tmp_ref[0, rs, :] = _nt(k_out_b[rs], dsb)
                tmp_ref[1, rs, :] = _nn(vb[rs], dsb)
                tmp_ref[2, rs, :] = _nn(dob[rs], st_prev)
                ddec = jnp.sum(ds_c * st_prev.astype(F32), axis=0, keepdims=True)
                tmp_ref[3, rs, :] = jnp.broadcast_to(ddec * dec[c * CHUNK:c * CHUNK + 1, :],
                                                     (CHUNK, HEAD_DIM))
            dko, dqi = tmp_ref[1], tmp_ref[2]
            dq = dqh * eba + dqi * eb
            dk = dkh * eab + dko * elb
            tko = dko * k_out
            db = dqh * q_hat - dkh * k_hat + dqi * q_in - tko
            dlog = _seg_scan(db, p["r32"], False) + _chunk_total(tko) + tmp_ref[3]
            df = dlog / p["f"] - dk
            sig = p["sig"]
            red_ref[0:1, hs] += jnp.sum(df * (1.0 - sig), axis=0, keepdims=True)
            dp_ref[1, :, hs] = (df * (1.0 - lbh) * sig * (1.0 - sig)).astype(BF16)
            sq = p["sq"]
            dp_ref[0, :, hs] = (dq * (HEAD_DIM ** -0.5) * (sq * (1.0 + qr * (1.0 - sq)))).astype(BF16)
            dp_ref[2, :, hs] = (dv_intra + tmp_ref[0]).astype(BF16)

    col = lambda k: pl.BlockSpec((TL, H), lambda i, k=k: (nt - 1 - i, k))
    rev = pl.BlockSpec((TL, H), lambda i: (nt - 1 - i, 0))
    return _pallas_call(
        body, name="hgrn_bwd", grid=(nt,),
        in_specs=[col(0), col(1), col(2), col(3),
                  pl.BlockSpec(lower_bounds.shape, lambda i: (0, 0)),
                  pl.BlockSpec(gamma.shape, lambda i: (0, 0)),
                  rev, rev,
                  pl.BlockSpec((nc, nh, HEAD_DIM, HEAD_DIM), lambda i: (nt - 1 - i, 0, 0, 0)), ANY, ANY],
        out_specs=[pl.BlockSpec((4, TL, H), lambda i: (0, nt - 1 - i, 0)), pl.BlockSpec((8, H), lambda i: (0, 0))],
        out_shape=[_sds(dproj.shape, BF16), _sds((8, H), F32)],
        input_output_aliases={10: 0},
        scratch_shapes=[pltpu.VMEM((nh, HEAD_DIM, HEAD_DIM), F32),
                        pltpu.VMEM((nc, HEAD_DIM, HEAD_DIM), F32),
                        pltpu.VMEM((4, TL, HEAD_DIM), F32)],
        compiler_params=_params(("arbitrary",), 48),
    )(proj, proj, proj, proj, lower_bounds, gamma, o_pre, d_out, s_saved, after, dproj)


def _shift_down(u, s, row):
    return jnp.where(row >= s, pltpu.roll(u, s, 0), 0.0)


def _shift_up(u, s, row):
    n = u.shape[0]
    return jnp.where(row < n - s, pltpu.roll(u, n - s, 0), 0.0)


def _conv_specs(L, H):
    per = H // LANES
    return [pl.BlockSpec((L, LANES), lambda j, o=o: (0, o * per + j)) for o in (4, 5, 6)]


def _conv_fwd(proj, conv_w, H, after):
    L = proj.shape[0]

    def body(c_ref, b_ref, x_ref, w_ref, after_ref, o_ref):
        row = lax.broadcasted_iota(jnp.int32, (L, LANES), 0)
        u = c_ref[...] * x_ref[...]
        w = w_ref[...]
        y = w[0:1] * _shift_down(u, 2, row) + w[1:2] * _shift_down(u, 1, row) + w[2:3] * u
        o_ref[...] = (b_ref[...] * y).astype(BF16)

    return _pallas_call(
        body, name="conv_fwd", grid=(H // LANES,),
        in_specs=_conv_specs(L, H) + [pl.BlockSpec((3, LANES), lambda j: (0, j)), ANY],
        out_specs=pl.BlockSpec((L, LANES), lambda j: (0, j)),
        out_shape=_sds((L, H), BF16),
        compiler_params=_params(("parallel",), 48),
    )(proj, proj, proj, conv_w, after)


def _conv_bwd(proj, conv_w, dcb, H, after, dproj):
    L = proj.shape[0]

    def body(c_ref, b_ref, x_ref, w_ref, d_ref, after_ref, dproj_ref, dp_ref, dw_ref):
        row = lax.broadcasted_iota(jnp.int32, (L, LANES), 0)
        cg, xb = c_ref[...], x_ref[...]
        u = cg * xb
        u1, u2 = _shift_down(u, 1, row), _shift_down(u, 2, row)
        w = w_ref[...]
        y = w[0:1] * u2 + w[1:2] * u1 + w[2:3] * u
        d = d_ref[...]
        dp_ref[1] = (d * y).astype(BF16)
        dy = d * b_ref[...]
        du = w[2:3] * dy + w[1:2] * _shift_up(dy, 1, row) + w[0:1] * _shift_up(dy, 2, row)
        dw_ref[0:1, :] = jnp.sum(dy * u2, axis=0, keepdims=True)
        dw_ref[1:2, :] = jnp.sum(dy * u1, axis=0, keepdims=True)
        dw_ref[2:3, :] = jnp.sum(dy * u, axis=0, keepdims=True)
        dp_ref[0] = (du * xb).astype(BF16)
        dp_ref[2] = (du * cg).astype(BF16)
        dp_ref[3] = jnp.zeros((L, LANES), BF16)

    blk = pl.BlockSpec((L, LANES), lambda j: (0, j))
    return _pallas_call(
        body, name="conv_bwd", grid=(H // LANES,),
        in_specs=_conv_specs(L, H) + [pl.BlockSpec((3, LANES), lambda j: (0, j)), blk, ANY, ANY],
        out_specs=[pl.BlockSpec((4, L, LANES), lambda j: (2, 0, j)), pl.BlockSpec((3, LANES), lambda j: (0, j))],
        out_shape=[_sds(dproj.shape, BF16), _sds((3, H), F32)],
        input_output_aliases={6: 0},
        compiler_params=_params(("parallel",), 56),
    )(proj, proj, proj, conv_w, dcb, after, dproj)


def _gate_specs(tm, H):
    return [pl.BlockSpec((tm, H), lambda i, k=k: (i, k)) for k in (7, 8, 9, 10)]


def _fwd_mix(og, cb, proj, x, wat, wbt, wout, g_ffn, H, after):
    L, D = x.shape
    tm = min(L, 512)

    def body(o_ref, cb_ref, ga0, ga1, gb0, gb1, x_ref, wa_ref, wb_ref, wo_ref, g_ref, after_ref,
             sa_ref, sb_ref, ta_ref, tb_ref, m_ref, x1_ref, h2_ref):
        ya, yb = _nt(o_ref[...], wa_ref[...]), _nt(cb_ref[...], wb_ref[...])
        for k, (gar, gbr) in enumerate(((ga0, gb0), (ga1, gb1))):
            cs = slice(k * H, (k + 1) * H)
            sa, sb = _sigmoid(gar[...]), _sigmoid(gbr[...])
            ma, mb = sa * ya[:, cs], sb * yb[:, cs]
            m_ref[:, cs] = (ma + mb).astype(BF16)
            sa_ref[:, cs] = sa.astype(BF16)
            sb_ref[:, cs] = sb.astype(BF16)
            ta_ref[:, cs] = (ma * (1.0 - sa)).astype(BF16)
            tb_ref[:, cs] = (mb * (1.0 - sb)).astype(BF16)
        x1 = x_ref[...] + _nn(m_ref[...], wo_ref[...])
        x1_ref[...] = x1
        _, xh = _rms_stats(x1)
        h2_ref[...] = (xh * g_ref[...]).astype(BF16)

    row = lambda w: pl.BlockSpec((tm, w), lambda i: (i, 0))
    full = lambda a: pl.BlockSpec(a.shape, lambda i: (0,) * a.ndim)
    return _pallas_call(
        body, name="fwd_mix", grid=(L // tm,),
        in_specs=[row(H), row(H)] + _gate_specs(tm, H) + [row(D), full(wat), full(wbt), full(wout),
                                                           full(g_ffn), ANY],
        out_specs=[row(D)] * 7,
        out_shape=[_sds((L, D), BF16)] * 5 + [_sds((L, D), F32), _sds((L, D), BF16)],
        compiler_params=_params(("parallel",), 56),
    )(og, cb, proj, proj, proj, proj, x, wat, wbt, wout, g_ffn, after)


def _bwd_mix(dx1b, sig_a, sig_b, dm_dga, dm_dgb, wat, wbt, wout, H, after):
    L, D = dx1b.shape
    tm = min(L, 512)

    def body(dx_ref, sa_ref, sb_ref, ta_ref, tb_ref, wa_ref, wb_ref, wo_ref, after_ref,
             dya_ref, dyb_ref, dgate_ref, do_ref, dcb_ref):
        dm = _nt(dx_ref[...], wo_ref[...])
        dga = (dm * ta_ref[...].astype(F32)).astype(BF16)
        dgb = (dm * tb_ref[...].astype(F32)).astype(BF16)
        for q, part in enumerate((dga[:, 0:H], dga[:, H:D], dgb[:, 0:H], dgb[:, H:D])):
            dgate_ref[q] = part
        dya_ref[...] = (dm * sa_ref[...].astype(F32)).astype(BF16)
        dyb_ref[...] = (dm * sb_ref[...].astype(F32)).astype(BF16)
        do_ref[...] = _nn(dya_ref[...], wa_ref[...])
        dcb_ref[...] = _nn(dyb_ref[...], wb_ref[...])

    row = lambda w: pl.BlockSpec((tm, w), lambda i: (i, 0))
    full = lambda a: pl.BlockSpec(a.shape, lambda i: (0,) * a.ndim)
    return _pallas_call(
        body, name="bwd_mix", grid=(L // tm,),
        in_specs=[row(D)] * 5 + [full(wat), full(wbt), full(wout), ANY],
        out_specs=[row(D), row(D), pl.BlockSpec((4, tm, H), lambda i: (1, i, 0)), row(H), row(H)],
        out_shape=[_sds((L, D), BF16)] * 2 + [_sds((DPROJ_BLOCKS, L, H), BF16)] + [_sds((L, H), F32)] * 2,
        compiler_params=_params(("parallel",), 56),
    )(dx1b, sig_a, sig_b, dm_dga, dm_dgb, wat, wbt, wout, after)


def _fwd_ffn_up(h2, wgt, wut):
    L, D = h2.shape
    F = wgt.shape[0]
    tn = F // 2
    tm = min(L, 512)

    def body(h_ref, wg_ref, wu_ref, sa_ref, sb_ref, s_ref):
        h = h_ref[...]
        a, b = _nt(h, wg_ref[...]), _nt(h, wu_ref[...])
        sg = _sigmoid(a)
        silu = a * sg
        sa_ref[...] = (b * sg * (1.0 + a * (1.0 - sg))).astype(BF16)
        sb_ref[...] = silu.astype(BF16)
        s_ref[...] = (silu * b).astype(BF16)

    wspec = pl.BlockSpec((tn, D), lambda j, i: (j, 0))
    ospec = pl.BlockSpec((tm, tn), lambda j, i: (i, j))
    return _pallas_call(
        body, name="fwd_ffn_up", grid=(2, L // tm),
        in_specs=[pl.BlockSpec((tm, D), lambda j, i: (i, 0)), wspec, wspec],
        out_specs=[ospec] * 3,
        out_shape=[_sds((L, F), BF16)] * 3,
        compiler_params=_params(("parallel", "parallel"), 48),
    )(h2, wgt, wut)


def _fwd_down_loss(s, wd, x1, target, g_final):
    L, D = x1.shape
    F = wd.shape[0]
    tm = min(L, 512)

    def body(s_ref, wd_ref, x1_ref, t_ref, g_ref, dx_ref, dxb_ref, red_ref):
        @pl.when(pl.program_id(0) == 0)
        def _():
            red_ref[...] = jnp.zeros_like(red_ref)

        g = g_ref[...]
        r, xh = _rms_stats(x1_ref[...] + _nn(s_ref[...], wd_ref[...]))
        e = xh * g - t_ref[...]
        dy = e * (1.0 / D)
        dx = _rms_bwd(dy * g, xh, r)
        dx_ref[...] = dx
        dxb_ref[...] = dx.astype(BF16)
        red_ref[0:1, :] += jnp.sum(dy * xh, axis=0, keepdims=True)
        red_ref[1:2, :] += jnp.broadcast_to(0.5 * jnp.sum(e * e) * (1.0 / D), (1, D))

    row = pl.BlockSpec((tm, D), lambda i: (i, 0))
    return _pallas_call(
        body, name="fwd_down_loss", grid=(L // tm,),
        in_specs=[pl.BlockSpec((tm, F), lambda i: (i, 0)), pl.BlockSpec((F, D), lambda i: (0, 0)),
                  row, row, pl.BlockSpec((1, D), lambda i: (0, 0))],
        out_specs=[row, row, pl.BlockSpec((8, D), lambda i: (0, 0))],
        out_shape=[_sds((L, D), F32), _sds((L, D), BF16), _sds((8, D), F32)],
        compiler_params=_params(("arbitrary",), 56),
    )(s, wd, x1, target, g_final)


def _bwd_down(dx2b, wd, s_a, s_b):
    L, D = dx2b.shape
    F = wd.shape[0]
    tn = F // 2
    tm = min(L, 512)

    def body(dx_ref, wd_ref, sa_ref, sb_ref, da_ref, db_ref):
        ds = _nt(dx_ref[...], wd_ref[...])
        da_ref[...] = (ds * sa_ref[...].astype(F32)).astype(BF16)
        db_ref[...] = (ds * sb_ref[...].astype(F32)).astype(BF16)

    ospec = pl.BlockSpec((tm, tn), lambda j, i: (i, j))
    return _pallas_call(
        body, name="bwd_down", grid=(2, L // tm),
        in_specs=[pl.BlockSpec((tm, D), lambda j, i: (i, 0)),
                  pl.BlockSpec((tn, D), lambda j, i: (j, 0)), ospec, ospec],
        out_specs=[ospec] * 2,
        out_shape=[_sds((L, F), BF16)] * 2,
        compiler_params=_params(("parallel", "parallel"), 48),
    )(dx2b, wd, s_a, s_b)


def _bwd_ffn_dh(da, db, wgt, wut, x1, dx2, g_ffn, after):
    L, D = x1.shape
    F = wgt.shape[0]
    tm = min(L, 256)

    def body(da_ref, db_ref, wg_ref, wu_ref, x1_ref, dx2_ref, g_ref, after_ref, dx_ref, dxb_ref, red_ref):
        @pl.when(pl.program_id(0) == 0)
        def _():
            red_ref[...] = jnp.zeros_like(red_ref)

        dh = _nn(da_ref[...], wg_ref[...]) + _nn(db_ref[...], wu_ref[...])
        r, xh = _rms_stats(x1_ref[...])
        red_ref[0:1, :] += jnp.sum(dh * xh, axis=0, keepdims=True)
        dx = dx2_ref[...] + _rms_bwd(dh * g_ref[...], xh, r)
        dx_ref[...] = dx
        dxb_ref[...] = dx.astype(BF16)

    row = pl.BlockSpec((tm, D), lambda i: (i, 0))
    aspec = pl.BlockSpec((tm, F), lambda i: (i, 0))
    wspec = pl.BlockSpec((F, D), lambda i: (0, 0))
    return _pallas_call(
        body, name="bwd_ffn_dh", grid=(L // tm,),
        in_specs=[aspec, aspec, wspec, wspec, row, row, pl.BlockSpec((1, D), lambda i: (0, 0)), ANY],
        out_specs=[row, row, pl.BlockSpec((8, D), lambda i: (0, 0))],
        out_shape=[_sds((L, D), F32), _sds((L, D), BF16), _sds((8, D), F32)],
        compiler_params=_params(("arbitrary",), 56),
    )(da, db, wgt, wut, x1, dx2, g_ffn, after)


def _bwd_in(dproj, w_int, x, dx1, g_mix, after):
    L, D = x.shape
    N = w_int.shape[0]
    H = dproj.shape[2]
    tm = min(L, 256)
    assert N == len(DPROJ_BLOCK_OF) * H

    def body(blocks_ref, w_ref, x_ref, dx1_ref, g_ref, after_ref, dx_ref, red_ref, dp_ref):
        @pl.when(pl.program_id(0) == 0)
        def _():
            red_ref[...] = jnp.zeros_like(red_ref)

        for t, block in enumerate(DPROJ_BLOCK_OF):
            dp_ref[:, t * H:(t + 1) * H] = blocks_ref[block]
        dh = _nn(dp_ref[...], w_ref[...])
        r, xh = _rms_stats(x_ref[...])
        red_ref[0:1, :] += jnp.sum(dh * xh, axis=0, keepdims=True)
        dx_ref[...] = dx1_ref[...] + _rms_bwd(dh * g_ref[...], xh, r)

    row = pl.BlockSpec((tm, D), lambda i: (i, 0))
    return _pallas_call(
        body, name="bwd_in", grid=(L // tm,),
        in_specs=[pl.BlockSpec((DPROJ_BLOCKS, tm, H), lambda i: (0, i, 0)), pl.BlockSpec((N, D), lambda i: (0, 0)),
                  row, row, pl.BlockSpec((1, D), lambda i: (0, 0)), ANY],
        out_specs=[row, pl.BlockSpec((8, D), lambda i: (0, 0))],
        out_shape=[_sds((L, D), F32), _sds((8, D), F32)],
        scratch_shapes=[pltpu.VMEM((tm, N), BF16)],
        compiler_params=_params(("arbitrary",), 56),
    )(dproj, w_int, x, dx1, g_mix, after)


def _dw_in(h, dproj, n_cols, c_idx):
    L, D = h.shape
    H = dproj.shape[2]
    tk = min(L, TK_TOKENS)
    nk = L // tk
    r2 = D // 2
    first = [(j * n_cols) // H for j in range(N_CHIPS)]
    last = [((j + 1) * n_cols - 1) // H for j in range(N_CHIPS)]
    slots = max(b - a for a, b in zip(first, last)) + 1
    plan = []
    for j in range(N_CHIPS):
        lo, hi = j * n_cols, (j + 1) * n_cols
        segments = []
        for s in range(last[j] - first[j] + 1):
            a, b = max(lo, (first[j] + s) * H), min(hi, (first[j] + s + 1) * H)
            segments.append((s, a - (first[j] + s) * H, b - a, a - lo))
        plan.append(segments)

    def body(c_ref, *refs):
        h_ref, slot_refs = refs[0], refs[1:1 + slots]
        o_ref, sib_ref, b_ref = refs[1 + slots:]
        j, k = pl.program_id(0), pl.program_id(1)
        for jj in range(N_CHIPS):
            @pl.when(j == jj)
            def _(jj=jj):
                for s, start, width, at in plan[jj]:
                    b_ref[:, at:at + width] = slot_refs[s][:, start:start + width]

        part = _tn(h_ref[...], b_ref[...])

        @pl.when(k == 0)
        def _():
            o_ref[...] = part

        @pl.when(k > 0)
        def _():
            o_ref[...] += part

        @pl.when(k == nk - 1)
        def _():
            theirs = pl.ds(pl.multiple_of((1 - c_ref[0]) * r2, 8), r2)
            sib_ref[...] = o_ref[theirs, :].astype(BF16)

    def slot_spec(s):
        blocks = [DPROJ_BLOCK_OF[min(first[j] + s, last[j])] for j in range(N_CHIPS)]

        def index(j, k, c_ref):
            block = blocks[0]
            for jj in range(1, N_CHIPS):
                block = jnp.where(j == jj, blocks[jj], block)
            return (block, k, 0)

        return pl.BlockSpec((None, tk, H), index)

    return _pallas_call(
        body, name="dw_in",
        grid_spec=pltpu.PrefetchScalarGridSpec(
            num_scalar_prefetch=1, grid=(N_CHIPS, nk),
            in_specs=[pl.BlockSpec((tk, D), lambda j, k, c_ref: (k, 0))] + [slot_spec(s) for s in range(slots)],
            out_specs=[pl.BlockSpec((None, D, n_cols), lambda j, k, c_ref: (j, 0, 0)),
                       pl.BlockSpec((None, r2, n_cols), lambda j, k, c_ref: (j, 0, 0))],
            scratch_shapes=[pltpu.VMEM((tk, n_cols), BF16)]),
        out_shape=[_sds((N_CHIPS, D, n_cols), F32), _sds((N_CHIPS, r2, n_cols), BF16)],
        compiler_params=_params(("parallel", "arbitrary"), 56),
    )(c_idx, h, *([dproj] * slots))


def _mm_tn(name, a, b, a_spec, b_spec, o_block, n_out, n_k):
    def body(a_ref, b_ref, o_ref):
        part = _tn(a_ref[...], b_ref[...])

        @pl.when(pl.program_id(1) == 0)
        def _():
            o_ref[...] = part

        @pl.when(pl.program_id(1) > 0)
        def _():
            o_ref[...] += part

    return _pallas_call(
        body, name=name, grid=(n_out, n_k),
        in_specs=[a_spec, b_spec],
        out_specs=pl.BlockSpec((None,) + o_block, lambda j, k: (j, 0, 0)),
        out_shape=_sds((n_out,) + o_block, F32),
        compiler_params=_params(("parallel", "arbitrary"), 56),
    )(a, b)


TK_TOKENS = 2048


def _dw_whole(name, pairs, by_rows):
    n = len(pairs)
    L = pairs[0][0].shape[0]
    tk = min(L, TK_TOKENS)

    def body(*refs):
        for q in range(n):
            a_ref, b_ref, o_ref = refs[2 * q], refs[2 * q + 1], refs[2 * n + q]
            part = _tn(a_ref[...], b_ref[...])
            rows, cols = o_ref.shape[1], o_ref.shape[2]
            shards = [part[j * rows:(j + 1) * rows, :] if by_rows else part[:, j * cols:(j + 1) * cols]
                      for j in range(N_CHIPS)]

            @pl.when(pl.program_id(0) == 0)
            def _(shards=shards, o_ref=o_ref):
                for j, shard in enumerate(shards):
                    o_ref[j] = shard

            @pl.when(pl.program_id(0) > 0)
            def _(shards=shards, o_ref=o_ref):
                for j, shard in enumerate(shards):
                    o_ref[j] += shard

    in_specs, out_specs, out_shape, operands = [], [], [], []
    for a, b in pairs:
        M, N = a.shape[1], b.shape[1]
        shape = (N_CHIPS, M // N_CHIPS, N) if by_rows else (N_CHIPS, M, N // N_CHIPS)
        in_specs += [pl.BlockSpec((tk, M), lambda k: (k, 0)), pl.BlockSpec((tk, N), lambda k: (k, 0))]
        out_specs.append(pl.BlockSpec(shape, lambda k: (0, 0, 0)))
        out_shape.append(_sds(shape, F32))
        operands += [a, b]
    return _pallas_call(
        body, name=name, grid=(L // tk,), in_specs=in_specs, out_specs=out_specs, out_shape=out_shape,
        compiler_params=_params(("arbitrary",), 56),
    )(*operands)


def _dw_rows2(name, a, b):
    L, M = a.shape
    N = b.shape[1]
    tk = min(L, TK_TOKENS)
    return _mm_tn(name, a, b, pl.BlockSpec((tk, M // 2), lambda j, k: (k, j)),
                  pl.BlockSpec((tk, N), lambda j, k: (k, 0)), (M // 2, N), 2, L // tk)


def _place():
    x, y, c = lax.axis_index("x"), lax.axis_index("y"), lax.axis_index("c")
    chips = [(1 - x, y), (x, 1 - y), (1 - x, 1 - y)]
    return x, y, c, 2 * x + y, chips


def _remote(src, dst, send_sem, recv_sem, device):
    return pltpu.make_async_remote_copy(src_ref=src, dst_ref=dst, send_sem=send_sem,
                                        recv_sem=recv_sem, device_id=device, device_id_type=MESH)


def _half(ref, lead, c, r2):
    return ref.at[lead, pl.ds(pl.multiple_of(c * r2, 16), r2), :]


def _cast_place(name, ws, chip_idx):
    n = len(ws)

    def body(k_ref, *refs):
        for w_ref, o_ref in zip(refs[:n], refs[n:]):
            o_ref[...] = w_ref[...].astype(BF16)

    return _pallas_call(
        body, name=name,
        grid_spec=pltpu.PrefetchScalarGridSpec(
            num_scalar_prefetch=1, grid=(2,),
            in_specs=[pl.BlockSpec((w.shape[0] // 2, w.shape[1]), lambda i, k_ref: (i, 0)) for w in ws],
            out_specs=[pl.BlockSpec((None, w.shape[0] // 2, w.shape[1]), lambda i, k_ref: (k_ref[0], i, 0))
                       for w in ws]),
        out_shape=[_sds((N_CHIPS,) + w.shape, BF16) for w in ws],
        compiler_params=_params(("parallel",), 48),
    )(chip_idx, *ws)


def _cast_place_t(name, ws, chip_idx):
    n = len(ws)
    r, cols = ws[0].shape

    def body(k_ref, *refs):
        for w_ref, o_ref in zip(refs[:n], refs[n:]):
            o_ref[...] = w_ref[...].T.astype(BF16)

    return _pallas_call(
        body, name=name,
        grid_spec=pltpu.PrefetchScalarGridSpec(
            num_scalar_prefetch=1, grid=(cols // LANES,),
            in_specs=[pl.BlockSpec((r, LANES), lambda i, k_ref: (0, i))] * n,
            out_specs=[pl.BlockSpec((None, LANES, r), lambda i, k_ref: (k_ref[0], i, 0))] * n),
        out_shape=[_sds((N_CHIPS, cols, r), BF16)] * n,
        compiler_params=_params(("parallel",), 48),
    )(chip_idx, *ws)


def _gather_copies(bufs, whole, send_sems, recv_sems, select=None):
    x, y, c, k, chips = _place()
    pairs = []
    for w, buf in enumerate(bufs):
        for j, (cx, cy) in enumerate(chips):
            if select is not None and not select(w, j):
                continue
            if w in whole:
                mine, theirs = buf.at[k], buf.at[2 * cx + cy]
            else:
                r2 = buf.shape[1] // 2
                mine, theirs = _half(buf, k, c, r2), _half(buf, 2 * cx + cy, c, r2)
            sems = (send_sems.at[w * 3 + j], recv_sems.at[w * 3 + j])
            pairs.append((_remote(mine, mine, *sems, (cx, cy, c)), _remote(theirs, theirs, *sems, (x, y, c))))
    return pairs


def _gather_start(name, groups, after):
    flat = [b for bufs, _, _ in groups for b in bufs]
    nb, ng = len(flat), len(groups)

    def body(*refs):
        ins, sems, token = refs[:nb], refs[nb + 1:nb + 1 + 2 * ng], refs[-1]
        pos = 0
        for g, (bufs, whole, select) in enumerate(groups):
            for send, _ in _gather_copies(ins[pos:pos + len(bufs)], whole, sems[2 * g], sems[2 * g + 1], select):
                send.start()
            pos += len(bufs)
        token[...] = jnp.zeros_like(token)

    sem_shapes = []
    for bufs, _, _ in groups:
        sem_shapes += [pltpu.SemaphoreType.DMA((3 * len(bufs),))] * 2
    out = _pallas_call(
        body, name=name,
        in_specs=[HBM] * nb + [ANY], out_specs=tuple([SEM] * (2 * ng) + [HBM] * nb + [VMEM]),
        out_shape=tuple(sem_shapes + [pltpu.HBM(b.shape, b.dtype) for b in flat] + [_sds((8, LANES), F32)]),
        input_output_aliases={i: 2 * ng + i for i in range(nb)},
        compiler_params=pltpu.CompilerParams(has_side_effects=EFFECT),
    )(*flat, after)
    sems, thru, pos = [], [], 2 * ng
    for g, (bufs, _, _) in enumerate(groups):
        sems.append((out[2 * g], out[2 * g + 1]))
        thru.append(list(out[pos:pos + len(bufs)]))
        pos += len(bufs)
    return sems, thru, out[-1]


def _gather_wait(name, bufs, whole, sems, after, select=None):
    nb = len(bufs)

    def body(*refs):
        ins, send_sems, recv_sems = refs[:nb], refs[nb], refs[nb + 1]
        for send, arrival in _gather_copies(ins, whole, send_sems, recv_sems, select):
            send.wait_send()
            arrival.wait_recv()

    return _pallas_call(
        body, name=name,
        in_specs=[HBM] * nb + [SEM, SEM, ANY], out_specs=[HBM] * nb,
        out_shape=[pltpu.HBM(b.shape, b.dtype) for b in bufs],
        input_output_aliases={i: i for i in range(nb)},
        compiler_params=pltpu.CompilerParams(has_side_effects=EFFECT),
    )(*bufs, sems[0], sems[1], after)


def _gather_forward(name, bufs, sources=(0, 1, 2)):
    n = len(bufs)

    def body(*refs):
        outs = refs[n:2 * n]
        send_sems, recv_sems = refs[2 * n:]
        x, y, c, _, chips = _place()
        sends = []
        for w in range(n):
            r2 = outs[w].shape[1] // 2
            for j in sources:
                landed = _half(outs[w], 2 * chips[j][0] + chips[j][1], c, r2)
                sends.append(_remote(landed, landed, send_sems.at[w * 3 + j], recv_sems.at[w * 3 + j],
                                     (x, y, 1 - c)))
        for cp in sends:
            cp.start()
        for w in range(n):
            r2 = outs[w].shape[1] // 2
            for j in sources:
                got = _half(outs[w], 2 * chips[j][0] + chips[j][1], 1 - c, r2)
                _remote(got, got, send_sems.at[w * 3 + j], recv_sems.at[w * 3 + j], (x, y, c)).wait_recv()
        for cp in sends:
            cp.wait_send()

    return _pallas_call(
        body, name=name,
        in_specs=[ANY] * n, out_specs=[ANY] * n,
        out_shape=[_sds(b.shape, b.dtype) for b in bufs],
        input_output_aliases={i: i for i in range(n)},
        scratch_shapes=[pltpu.SemaphoreType.DMA((n * 3,)), pltpu.SemaphoreType.DMA((n * 3,))],
    )(*bufs)


def _rs_add(name, grads3, from_sibling, c_idx):
    n = len(grads3)

    def body(c_ref, *refs):
        for g_ref, s_ref, o_ref in zip(refs[:n], refs[n:2 * n], refs[2 * n:]):
            o_ref[...] = (g_ref[...] + s_ref[...].astype(F32)).astype(BF16)

    mine =[pl.BlockSpec((None,) + s.shape[1:], lambda k, c_ref: (k, c_ref[0], 0)) for s in from_sibling]
    whole = [pl.BlockSpec((None,) + s.shape[1:], lambda k, c_ref: (k, 0, 0)) for s in from_sibling]
    return _pallas_call(
        body, name=name,
        grid_spec=pltpu.PrefetchScalarGridSpec(num_scalar_prefetch=1, grid=(N_CHIPS,), in_specs=mine + whole,
                                               out_specs=whole),
        out_shape=[_sds(s.shape, BF16) for s in from_sibling],
        compiler_params=_params(("parallel",), 48),
    )(c_idx, *grads3, *from_sibling)


def _split_start(name, arrays, n_sems, pairs_fn):
    n = len(arrays)

    def body(*refs):
        for send, _ in pairs_fn(refs[:n], refs[n], refs[n + 1]):
            send.start()
        refs[-1][...] = jnp.zeros_like(refs[-1])

    out = _pallas_call(
        body, name=name,
        in_specs=[HBM] * n, out_specs=tuple([SEM, SEM] + [HBM] * n + [VMEM]),
        out_shape=tuple([pltpu.SemaphoreType.DMA((n_sems,))] * 2 + [pltpu.HBM(a.shape, a.dtype) for a in arrays]
                        + [_sds((8, LANES), F32)]),
        input_output_aliases={i: 2 + i for i in range(n)},
        compiler_params=pltpu.CompilerParams(has_side_effects=EFFECT),
    )(*arrays)
    return (out[0], out[1]), list(out[2:2 + n]), out[-1]


def _split_wait(name, sems, arrays, pairs_fn, after):
    n = len(arrays)

    def body(*refs):
        for send, arrival in pairs_fn(refs[:n], refs[n], refs[n + 1]):
            send.wait_send()
            arrival.wait_recv()

    return list(_pallas_call(
        body, name=name,
        in_specs=[HBM] * n + [SEM, SEM, ANY], out_specs=[HBM] * n,
        out_shape=[pltpu.HBM(a.shape, a.dtype) for a in arrays],
        input_output_aliases={i: i for i in range(n)},
        compiler_params=pltpu.CompilerParams(has_side_effects=EFFECT),
    )(*arrays, sems[0], sems[1], after))


def _forward_pairs(bufs, send_sems, recv_sems):
    x, y, c, _, chips = _place()
    pairs = []
    for w, buf in enumerate(bufs):
        r2 = buf.shape[1] // 2
        for j, (cx, cy) in enumerate(chips):
            landed, theirs = _half(buf, 2 * cx + cy, c, r2), _half(buf, 2 * cx + cy, 1 - c, r2)
            sems = (send_sems.at[w * 3 + j], recv_sems.at[w * 3 + j])
            pairs.append((_remote(landed, landed, *sems, (x, y, 1 - c)), _remote(theirs, theirs, *sems, (x, y, c))))
    return pairs


def _sibling_pairs(arrays, send_sems, recv_sems):
    x, y, c, _, _ = _place()
    n = len(arrays) // 2
    pairs = []
    for w in range(n):
        r2 = arrays[w].shape[1] // 2
        cp = _remote(_half(arrays[w], slice(None), 1 - c, r2), arrays[n + w], send_sems.at[w], recv_sems.at[w],
                     (x, y, 1 - c))
        pairs.append((cp, cp))
    return pairs


def _sibling_whole_pairs(arrays, send_sems, recv_sems):
    x, y, c, _, _ = _place()
    n = len(arrays) // 2
    pairs = []
    for w in range(n):
        cp = _remote(arrays[w], arrays[n + w], send_sems.at[w], recv_sems.at[w], (x, y, 1 - c))
        pairs.append((cp, cp))
    return pairs


def _ici_pairs(arrays, send_sems, recv_sems):
    x, y, c, _, chips = _place()
    n = len(arrays) // 2
    pairs = []
    for w in range(n):
        for j, (cx, cy) in enumerate(chips):
            cp = _remote(arrays[w].at[2 * cx + cy], arrays[n + w].at[j],
                         send_sems.at[w * 3 + j], recv_sems.at[w * 3 + j], (cx, cy, c))
            pairs.append((cp, cp))
    return pairs


def _rs_sum(name, partials, received, place_idx):
    n = len(partials)
    nb = 2
    blocks = [(p.shape[1] // nb, p.shape[2]) for p in partials]

    def body(idx_ref, *refs):
        for p_ref, r_ref, o_ref in zip(refs[:n], refs[n:2 * n], refs[2 * n:]):
            o_ref[...] = ((p_ref[...].astype(F32) + r_ref[0].astype(F32))
                          + (r_ref[1].astype(F32) + r_ref[2].astype(F32)))

    return _pallas_call(
        body, name=name,
        grid_spec=pltpu.PrefetchScalarGridSpec(
            num_scalar_prefetch=1, grid=(nb,),
            in_specs=[pl.BlockSpec((None,) + b, lambda i, idx: (idx[0], i, 0)) for b in blocks]
            + [pl.BlockSpec((3,) + b, lambda i, idx: (0, i, 0)) for b in blocks],
            out_specs=[pl.BlockSpec(b, lambda i, idx: (idx[1] * nb + i, 0)) for b in blocks]),
        out_shape=[_sds((2 * p.shape[1], p.shape[2]), F32) for p in partials],
        compiler_params=_params(("parallel",), 48),
    )(place_idx, *partials, *received)


def _share_pairs(arrays, send_sems, recv_sems, first=0):
    x, y, c, _, _ = _place()
    pairs = []
    for w, arr in enumerate(arrays):
        r2 = arr.shape[0] // 2
        mine = arr.at[pl.ds(pl.multiple_of(c * r2, 8), r2), :]
        theirs = arr.at[pl.ds(pl.multiple_of((1 - c) * r2, 8), r2), :]
        sems = (send_sems.at[first + w], recv_sems.at[first + w])
        pairs.append((_remote(mine, mine, *sems, (x, y, 1 - c)), _remote(theirs, theirs, *sems, (x, y, c))))
    return pairs


def _small_pack(red_mix, red_ffn, red_final, red_hg, g_conv):
    D = red_mix.shape[1]
    H = red_hg.shape[1]

    def body(mix_ref, ffn_ref, fin_ref, hg_ref, cv_ref, in_ref):
        in_ref[...] = jnp.zeros_like(in_ref)
        in_ref[0:1, :] = mix_ref[0:1, :]
        in_ref[1:2, :] = ffn_ref[0:1, :]
        in_ref[2:3, :] = fin_ref[0:1, :]
        gam = hg_ref[1:2, 0:HEAD_DIM]
        for h in range(1, H // HEAD_DIM):
            gam = gam + hg_ref[1:2, h * HEAD_DIM:(h + 1) * HEAD_DIM]
        in_ref[3:4, 0:HEAD_DIM] = gam
        in_ref[3:4, HEAD_DIM:2 * HEAD_DIM] = fin_ref[1:2, 0:HEAD_DIM]
        in_ref[4:5, 0:H] = hg_ref[0:1, :]
        in_ref[6:9, 0:H] = cv_ref[...]

    return _pallas_call(
        body, name="small_pack", pin=False,
        in_specs=[VMEM] * 5, out_specs=VMEM, out_shape=_sds((N_SMALL_ROWS, D), F32),
    )(red_mix, red_ffn, red_final, red_hg, g_conv)


def _small_pairs(arrays, send_sems, recv_sems):
    block, gathered = arrays
    x, y, c, _, _ = _place()
    me = 4 * x + 2 * y + c
    pairs = []
    for m in range(1, 8):
        px, py, pc = x ^ ((m >> 2) & 1), y ^ ((m >> 1) & 1), c ^ (m & 1)
        sems = (send_sems.at[m - 1], recv_sems.at[m - 1])
        pairs.append((_remote(block, gathered.at[me], *sems, (px, py, pc)),
                      _remote(block, gathered.at[4 * px + 2 * py + pc], *sems, (x, y, c))))
    return pairs


def _share_pairs_behind_ici(arrays, send_sems, recv_sems):
    return _share_pairs(arrays, send_sems, recv_sems, 3)


def _ici_share_pairs(arrays, send_sems, recv_sems):
    return _ici_pairs(arrays[:2], send_sems, recv_sems) + _share_pairs_behind_ici(arrays[2:], send_sems, recv_sems)


def _tail_pairs(arrays, send_sems, recv_sems):
    return _small_pairs(arrays[:2], send_sems, recv_sems) + _share_pairs(arrays[2:], send_sems, recv_sems, 7)


def _adamw_math(w, g, m, v):
    m = ADAM_B1 * m + (1.0 - ADAM_B1) * g
    v = ADAM_B2 * v + (1.0 - ADAM_B2) * jnp.square(g)
    m_hat = m / (1.0 - ADAM_B1 ** ADAM_STEP)
    v_hat = v / (1.0 - ADAM_B2 ** ADAM_STEP)
    delta = -ADAM_LR * (m_hat / (jnp.sqrt(v_hat) + ADAM_EPS) + ADAM_WD * w)
    return delta, m, v


def _adamw(name, gs, ws, ms, vs):
    n = len(gs)
    nb = 4

    def body(*refs):
        ins, outs = refs[:4 * n], refs[4 * n:]
        for j in range(n):
            g_ref, w_ref, m_ref, v_ref = ins[j], ins[n + j], ins[2 * n + j], ins[3 * n + j]
            go_ref, d_ref, mo_ref, vo_ref = outs[4 * j:4 * j + 4]
            g = g_ref[...]
            go_ref[...] = g
            d_ref[...], mo_ref[...], vo_ref[...] = _adamw_math(w_ref[...], g, m_ref[...], v_ref[...])

    blk = [pl.BlockSpec((g.shape[0] // nb, g.shape[1]), lambda i: (i, 0)) for g in gs]
    out = _pallas_call(
        body, name=name, grid=(nb,),
        in_specs=blk * 4, out_specs=[b for b in blk for _ in range(4)],
        out_shape=[_sds(g.shape, F32) for g in gs for _ in range(4)],
        compiler_params=_params(("parallel",), 56),
    )(*gs, *ws, *ms, *vs)
    return [list(out[4 * j:4 * j + 4]) for j in range(n)]


def _small_update(block, gathered, place_idx, ws, ms, vs):
    n = len(ws)
    H = ws[1].shape[1]

    def body(idx_ref, blk_ref, all_ref, *refs):
        w, m, v, outs, tot_ref = refs[:n], refs[n:2 * n], refs[2 * n:3 * n], refs[3 * n:-1], refs[-1]
        chip, me = idx_ref[0], idx_ref[1]
        tot = jnp.where(me == 0, blk_ref[...], all_ref[0])
        for d in range(1, 8):
            tot = tot + jnp.where(me == d, blk_ref[...], all_ref[d])
        tot_ref[...] = tot
        p0 = _lower_bound(w[1][...])
        dl0 = p0 * (1.0 - p0) * tot_ref[4:5, 0:H]
        conv = jnp.zeros((3, LANES), F32)
        for k in range(N_CHIPS):
            conv = jnp.where(chip == k, tot_ref[6:9, k * LANES:(k + 1) * LANES], conv)
        grads = [tot_ref[0:1, :], None, tot_ref[3:4, 0:HEAD_DIM], conv, tot_ref[1:2, :], tot_ref[2:3, :]]
        for p in range(n):
            g_ref, d_ref, mo_ref, vo_ref = outs[4 * p:4 * p + 4]
            if p == 1:
                for row, g in ((slice(0, 1), dl0), (slice(1, 2), -dl0)):
                    g_ref[row, :] = g
                    d_ref[row, :], mo_ref[row, :], vo_ref[row, :] = _adamw_math(
                        w[p][row, :], g, m[p][row, :], v[p][row, :])
            else:
                g_ref[...] = grads[p]
                d_ref[...], mo_ref[...], vo_ref[...] = _adamw_math(w[p][...], grads[p], m[p][...], v[p][...])
        outs[4 * n][...] = tot_ref[3:4, HEAD_DIM:2 * HEAD_DIM]

    full = lambda a: pl.BlockSpec(a.shape, lambda i, idx: (0,) * a.ndim)
    out_shape = [_sds(w.shape, F32) for w in ws for _ in range(4)] + [_sds((1, LANES), F32)]
    return _pallas_call(
        body, name="small_update",
        grid_spec=pltpu.PrefetchScalarGridSpec(
            num_scalar_prefetch=1, grid=(1,),
            in_specs=[full(block), full(gathered)] + [full(a) for a in ws + ms + vs],
            out_specs=[full(s) for s in out_shape],
            scratch_shapes=[pltpu.VMEM(block.shape, F32)]),
        out_shape=out_shape,
    )(place_idx, block, gathered, *ws, *ms, *vs)


def kernel(x, norm_mix_g, w_in, lower_bounds, hg_norm_g, conv_w, w_branch_a, w_branch_b, w_out, norm_ffn_g, w_ffn_gate, w_ffn_up, w_ffn_down, norm_final_g, loss_target, m_norm_mix_g, m_w_in, m_lower_bounds, m_hg_norm_g, m_conv_w, m_w_branch_a, m_w_branch_b, m_w_out, m_norm_ffn_g, m_w_ffn_gate, m_w_ffn_up, m_w_ffn_down, m_norm_final_g, v_norm_mix_g, v_w_in, v_lower_bounds, v_hg_norm_g, v_conv_w, v_w_branch_a, v_w_branch_b, v_w_out, v_norm_ffn_g, v_w_ffn_gate, v_w_ffn_up, v_w_ffn_down, v_norm_final_g):
    _, L, D = x.shape
    H = D // 2
    assert lower_bounds.shape == (2, H) and hg_norm_g.shape == (1, HEAD_DIM)
    assert conv_w.shape == (1, 3, LANES) and w_in.shape[2] * N_CHIPS == 11 * H
    x2d, target = x.reshape(L, D), loss_target.reshape(L, D)
    g_final = norm_final_g.reshape(1, D)
    chip = 2 * lax.axis_index("x") + lax.axis_index("y")
    core = lax.axis_index("c")

    tr = lambda w: jnp.transpose(w[0])
    big = [w_in[0], w_branch_a[0], w_branch_b[0], w_out[0], tr(w_ffn_gate), tr(w_ffn_up), w_ffn_down[0]]
    big_m = [m_w_in[0], m_w_branch_a[0], m_w_branch_b[0], m_w_out[0], tr(m_w_ffn_gate), tr(m_w_ffn_up),
             m_w_ffn_down[0]]
    big_v = [v_w_in[0], v_w_branch_a[0], v_w_branch_b[0], v_w_out[0], tr(v_w_ffn_gate), tr(v_w_ffn_up),
             v_w_ffn_down[0]]
    names = ["w_in", "w_branch_a", "w_branch_b", "w_out", "w_ffn_gate", "w_ffn_up", "w_ffn_down"]

    chip_idx = chip.reshape(1).astype(jnp.int32)
    def per_shape(fn, tag, js, *lists):
        groups = {}
        for pos, a in enumerate(lists[0]):
            groups.setdefault(a.shape, []).append(pos)
        results = [None] * len(js)
        for same in groups.values():
            out = fn(tag + names[js[same[0]]], *[[xs[p] for p in same] for xs in lists])
            for q, p in enumerate(same):
                results[p] = out[q]
        return results

    place_t = lambda name, ws: _cast_place_t(name, ws, chip_idx)
    placed = per_shape(place_t, "place_", [0, 1, 2], big[:3]) + list(_cast_place("place_rest", big[3:], chip_idx))
    conv_placed = lax.dynamic_update_slice(jnp.zeros((N_CHIPS, 3, LANES), F32), conv_w, (chip, 0, 0))
    x_i, y_i = lax.axis_index("x"), lax.axis_index("y")
    blocks = lambda *ks: jnp.stack(ks).astype(jnp.int32)
    near = lambda w, j: j < 2
    far = lambda w, j: w == 1 or j == 2
    near_sems, in_flight, _ = _gather_start("gather_start_near", [([placed[0]], set(), near)], chip_idx)
    w_in_buf = in_flight[0][0]
    h, proj = _fwd_proj_first(x2d, norm_mix_g, w_in_buf, blocks(chip), placed[-1])
    sems, in_flight, _ = _gather_start(
        "gather_start_rest", [([w_in_buf, conv_placed], {1}, far), (placed[1:4], set(), None),
                              (placed[4:], set(), None)], h)
    w_in_buf, conv_buf = in_flight[0]
    (w_in_buf,) = _gather_wait("gather_wait_in_near", [w_in_buf], set(), near_sems[0], h, near)
    (w_in_buf,) = _gather_forward("gather_fwd_in_near", [w_in_buf], (0, 1))
    proj = _fwd_proj_more("fwd_proj_near", h, w_in_buf, proj,
                          blocks(2 * (1 - x_i) + y_i, 2 * x_i + (1 - y_i)))
    w_in_buf, conv_all = _gather_wait("gather_wait_in_far", [w_in_buf, conv_buf], {1}, sems[0], proj, far)
    (w_int3,) = _gather_forward("gather_fwd_in_far", [w_in_buf], (2,))
    proj = _fwd_proj_more("fwd_proj_far", h, w_int3, proj, blocks(2 * (1 - x_i) + (1 - y_i)))
    w_int = w_int3.reshape(-1, D)
    conv_full = jnp.transpose(conv_all, (1, 0, 2)).reshape(3, H)
    og, o_pre, s_saved = _hgrn_fwd(proj, lower_bounds, hg_norm_g, H)
    landed = _gather_wait("gather_wait_mix", in_flight[1], set(), sems[1], og)
    fwd_sems, landed, token = _split_start("gather_fwd_mix_start", landed, 9, _forward_pairs)
    cb = _conv_fwd(proj, conv_full, H, token)
    wat3, wbt3, wout3 = _split_wait("gather_fwd_mix_wait", fwd_sems, landed, _forward_pairs, cb)
    wat, wbt, wout = wat3.reshape(D, H), wbt3.reshape(D, H), wout3.reshape(D, D)
    landed = _gather_wait("gather_wait_ffn", in_flight[2], set(), sems[2], cb)
    fwd_sems, landed, token = _split_start("gather_fwd_ffn_start", landed, 9, _forward_pairs)
    sig_a, sig_b, dm_dga, dm_dgb, merged, x1, h2 = _fwd_mix(og, cb, proj, x2d, wat, wbt, wout, norm_ffn_g,
                                                              H, token)
    wgt3, wut3, wd3 = _split_wait("gather_fwd_ffn_wait", fwd_sems, landed, _forward_pairs, h2)
    d_ff = N_CHIPS * wd3.shape[1]
    wgt, wut, wd = wgt3.reshape(d_ff, D), wut3.reshape(d_ff, D), wd3.reshape(d_ff, D)
    ffn_ds_da, ffn_ds_db, ffn_s = _fwd_ffn_up(h2, wgt, wut)
    dx2, dx2b, red_final = _fwd_down_loss(ffn_s, wd, x1, target, g_final)

    c_idx = core.reshape(1).astype(jnp.int32)
    place_idx = jnp.stack([chip, core]).astype(jnp.int32)

    def sibling_start(tag, grads):
        bufs = [lax.empty((N_CHIPS, g.shape[1] // 2, g.shape[2]), F32) for g in grads]
        return _split_start("rs_sibling_start_" + tag, list(grads) + bufs, len(grads), _sibling_pairs)

    def ici_start(tag, js, grads, from_sibling):
        partials = list(_rs_add("rs_add_" + tag, grads, from_sibling, c_idx))
        landings = [lax.empty((3,) + p.shape[1:], BF16) for p in partials]
        return _split_start("rs_ici_start_" + tag, partials + landings, 3 * len(js), _ici_pairs)

    def ici_start_behind(tag, js, started, after):
        n = len(js)
        arrays = _split_wait("rs_sibling_wait_" + tag, started[0], started[1], _sibling_pairs, after)
        return ici_start(tag, js, arrays[:n], arrays[n:])

    def sums(tag, started, after):
        partials, received = [], []
        for group, group_js, start in started:
            arrays = _split_wait("rs_ici_wait_" + group, start[0], start[1], _ici_pairs, after)
            partials += arrays[:len(group_js)]
            received += arrays[len(group_js):]
        return list(_rs_sum("rs_sum_" + tag, partials, received, place_idx))

    def adamw(tag, js, grads):
        return _adamw("adamw_" + tag, grads, *[[src[j] for j in js] for src in (big, big_m, big_v)])

    shards3 = lambda g: g.reshape(N_CHIPS, d_ff // N_CHIPS, D)
    da, db = _bwd_down(dx2b, wd, ffn_ds_da, ffn_ds_db)
    g_wd = shards3(_dw_rows2("dw_ffn_down", ffn_s, dx2b))
    g_wg = shards3(_dw_rows2("dw_ffn_gate", da, h2))
    g_wu = shards3(_dw_rows2("dw_ffn_up", db, h2))
    ffn_sibling = sibling_start("ffn", [g_wg, g_wu, g_wd])
    dx1, dx1b, red_ffn = _bwd_ffn_dh(da, db, wgt, wut, x1, dx2, norm_ffn_g, ffn_sibling[2])
    ffn_ici = ici_start_behind("ffn", [4, 5, 6], ffn_sibling, dx1b)
    dya, dyb, dproj, d_o, d_cb = _bwd_mix(dx1b, sig_a, sig_b, dm_dga, dm_dgb, wat, wbt, wout, H, ffn_ici[2])
    (g_wout,) = _dw_whole("dw_out", [(merged, dx1b)], True)
    g_wa, g_wb = _dw_whole("dw_branch", [(og, dya), (cb, dyb)], False)
    mix_sibling = sibling_start("mix", [g_wa, g_wb, g_wout])
    dproj, red_hg = _hgrn_bwd(proj, lower_bounds, hg_norm_g, o_pre, d_o, s_saved, H, mix_sibling[2], dproj)
    mix_ici = ici_start_behind("mix", [1, 2, 3], mix_sibling, red_hg)
    dproj, g_conv = _conv_bwd(proj, conv_full, d_cb, H, mix_ici[2], dproj)
    g_win, for_sibling = _dw_in(h, dproj, w_int3.shape[1], c_idx)
    in_sibling = _split_start("rs_sibling_start_in", [for_sibling, lax.empty(for_sibling.shape, BF16)], 1,
                              _sibling_whole_pairs)
    halves = sums("rest", [("mix", [1, 2, 3], mix_ici), ("ffn", [4, 5, 6], ffn_ici)], in_sibling[2])
    from_sibling = _split_wait("rs_sibling_wait_in", in_sibling[0], in_sibling[1], _sibling_whole_pairs,
                               halves[0])[1]
    (in_partial,) = _rs_add("rs_add_in", [g_win], [from_sibling], c_idx)
    both = _split_start("rs_ici_start_in", [in_partial, lax.empty((3,) + in_partial.shape[1:], BF16)] + halves,
                        3 + len(halves), _ici_share_pairs)
    in_ici, rest_share = (both[0], both[1][:2], both[2]), (both[0], both[1][2:])
    grad_x, red_mix = _bwd_in(dproj, w_int, x2d, dx1, norm_mix_g, in_ici[2])
    in_half = sums("in", [("in", [0], in_ici)], grad_x)
    small_block = _small_pack(red_mix, red_ffn, red_final, red_hg, g_conv)
    tail = _split_start("tail_start", [small_block, lax.empty((8,) + small_block.shape, F32)] + in_half, 8,
                        _tail_pairs)
    rest_grads = _split_wait("rs_share_wait_rest", rest_share[0], rest_share[1], _share_pairs_behind_ici, tail[2])
    big_out = [None] + adamw("rest", [1, 2, 3, 4, 5, 6], rest_grads)
    small_block, small_all, in_grad = _split_wait("tail_wait", tail[0], tail[1], _tail_pairs, big_out[6][0])
    big_out[0] = adamw("in", [0], [in_grad])[0]

    def smalls(mix, lb, hg, cw, ffn, fin):
        return [mix, lb, hg, cw[0], ffn, fin.reshape(1, D)]

    small_out = _small_update(
        small_block, small_all, jnp.stack([chip, 4 * x_i + 2 * y_i + core]).astype(jnp.int32),
        smalls(norm_mix_g, lower_bounds, hg_norm_g, conv_w, norm_ffn_g, norm_final_g),
        smalls(m_norm_mix_g, m_lower_bounds, m_hg_norm_g, m_conv_w, m_norm_ffn_g, m_norm_final_g),
        smalls(v_norm_mix_g, v_lower_bounds, v_hg_norm_g, v_conv_w, v_norm_ffn_g, v_norm_final_g))

    def outputs(i):
        big_i = [big_out[j][i] for j in range(7)]
        mix, lb, hg, cw, ffn, fin = [small_out[4 * p + i] for p in range(6)]
        return [mix, big_i[0][None], lb, hg, cw[None], big_i[1][None], big_i[2][None], big_i[3][None], ffn,
                big_i[4].T[None], big_i[5].T[None], big_i[6][None], fin.reshape(D)]

    outs = [small_out[24][0, 0], grad_x.reshape(1, L, D)]
    for i in range(4):
        outs += outputs(i)
    return tuple(outs)
```

```python
import jax
import jax.numpy as jnp
from jax import lax
from jax.experimental import pallas as pl
from jax.experimental.pallas import tpu as pltpu

F32 = jnp.float32
BF16 = jnp.bfloat16
EPS = 1e-6
CHUNK = 32
HEAD_DIM = 128
LANES = 128
N_CHIPS = 4
N_SMALL_ROWS = 16
DPROJ_BLOCKS = 12
DPROJ_BLOCK_OF = (0, 1, 2, 3, 8, 9, 10, 4, 5, 6, 7)

ADAM_LR = 0.001
ADAM_B1 = 0.9
ADAM_B2 = 0.999
ADAM_EPS = 1e-08
ADAM_WD = 0.01
ADAM_STEP = 10

MESH = pl.DeviceIdType.MESH
ANY = pl.BlockSpec(memory_space=pl.ANY)
VMEM = pl.BlockSpec(memory_space=pltpu.VMEM)
HBM = pl.BlockSpec(memory_space=pltpu.HBM)
SEM = pl.BlockSpec(memory_space=pltpu.SEMAPHORE)
EFFECT = pltpu.SideEffectType.DATAFLOW_SIDE_EFFECTING


def _sds(shape, dtype):
    return jax.ShapeDtypeStruct(shape, dtype)


def _pallas_call(body, pin=True, **kwargs):
    if not pin:
        return pl.pallas_call(body, **kwargs)
    in_hbm = lambda s: pltpu.HBM(s.shape, s.dtype) if isinstance(s, jax.ShapeDtypeStruct) else s
    kwargs["out_shape"] = jax.tree.map(in_hbm, kwargs["out_shape"])
    call = pl.pallas_call(body, **kwargs)

    def run(*args):
        return call(*[pltpu.with_memory_space_constraint(a, pltpu.HBM) if a.dtype in (F32, BF16) else a
                      for a in args])

    return run


def _params(semantics, vmem_mb):
    return pltpu.CompilerParams(dimension_semantics=semantics, vmem_limit_bytes=vmem_mb << 20)


def _nn(a, b):
    return lax.dot_general(a, b, (((1,), (0,)), ((), ())), preferred_element_type=F32)


def _nt(a, b):
    return lax.dot_general(a, b, (((1,), (1,)), ((), ())), preferred_element_type=F32)


def _tn(a, b):
    return lax.dot_general(a, b, (((0,), (0,)), ((), ())), preferred_element_type=F32)


def _sigmoid(x):
    return jax.nn.sigmoid(x)


def _rms_stats(x):
    r = lax.rsqrt(jnp.mean(x * x, axis=-1, keepdims=True) + EPS)
    return r, x * r


def _rms_bwd(dxh, xh, r):
    return r * (dxh - xh * jnp.mean(dxh * xh, axis=-1, keepdims=True))


def _fwd_proj_first(x, g_mix, w_int3, block, after):
    L, D = x.shape
    tn = w_int3.shape[1]
    tm = min(L, 1024)

    def body(blk_ref, x_ref, g_ref, w_ref, after_ref, h_ref, p_ref):
        _, xh = _rms_stats(x_ref[...])
        h = (xh * g_ref[...]).astype(BF16)
        h_ref[...] = h
        p_ref[...] = _nt(h, w_ref[...])

    return _pallas_call(
        body, name="fwd_proj_own",
        grid_spec=pltpu.PrefetchScalarGridSpec(
            num_scalar_prefetch=1, grid=(L // tm,),
            in_specs=[pl.BlockSpec((tm, D), lambda i, blk: (i, 0)),
                      pl.BlockSpec((1, D), lambda i, blk: (0, 0)),
                      pl.BlockSpec((None, tn, D), lambda i, blk: (blk[0], 0, 0)), ANY],
            out_specs=[pl.BlockSpec((tm, D), lambda i, blk: (i, 0)),
                       pl.BlockSpec((tm, tn), lambda i, blk: (i, blk[0]))]),
        out_shape=[_sds((L, D), BF16), _sds((L, N_CHIPS * tn), F32)],
        compiler_params=_params(("parallel",), 48),
    )(block, x, g_mix, w_int3, after)


def _fwd_proj_more(name, h, w_int3, proj, blocks):
    L, D = h.shape
    tn = w_int3.shape[1]
    tm = min(L, 1024)

    def body(blk_ref, h_ref, w_ref, proj_ref, p_ref):
        p_ref[...] = _nt(h_ref[...], w_ref[...])

    return _pallas_call(
        body, name=name,
        grid_spec=pltpu.PrefetchScalarGridSpec(
            num_scalar_prefetch=1, grid=(L // tm, blocks.shape[0]),
            in_specs=[pl.BlockSpec((tm, D), lambda i, j, blk: (i, 0)),
                      pl.BlockSpec((None, tn, D), lambda i, j, blk: (blk[j], 0, 0)), ANY],
            out_specs=pl.BlockSpec((tm, tn), lambda i, j, blk: (i, blk[j]))),
        out_shape=_sds(proj.shape, proj.dtype),
        input_output_aliases={3: 0},
        compiler_params=_params(("parallel", "arbitrary"), 48),
    )(blocks, h, w_int3, proj)


def _lower_bound(lbp):
    l0, l1 = lbp[0:1, :], lbp[1:2, :]
    m = jnp.maximum(l0, l1)
    e0, e1 = jnp.exp(l0 - m), jnp.exp(l1 - m)
    return e0 / (e0 + e1)


def _seg_scan(x, r32, forward):
    n = x.shape[0]
    s = 1
    while s < CHUNK:
        if forward:
            x = x + jnp.where(r32 >= s, pltpu.roll(x, s, 0), 0.0)
        else:
            x = x + jnp.where(r32 < CHUNK - s, pltpu.roll(x, n - s, 0), 0.0)
        s *= 2
    return x


def _bcast_row(x, row):
    n, w = x.shape
    nc = n // CHUNK
    x3 = x.reshape(nc, CHUNK, w)
    return jnp.broadcast_to(x3[:, row:row + 1, :], (nc, CHUNK, w)).reshape(n, w)


def _chunk_total(x):
    n, w = x.shape
    nc = n // CHUNK
    total = jnp.sum(x.reshape(nc, CHUNK, w), axis=1, keepdims=True)
    return jnp.broadcast_to(total, (nc, CHUNK, w)).reshape(n, w)


def _hgrn_prep(q_raw, f_raw, lb):
    r32 = lax.broadcasted_iota(jnp.int32, f_raw.shape, 0) & (CHUNK - 1)
    sig = _sigmoid(f_raw)
    f = lb + (1.0 - lb) * sig
    b = _seg_scan(jnp.log(f), r32, True)
    a = _bcast_row(b, CHUNK // 2 - 1)
    bl = _bcast_row(b, CHUNK - 1)
    sq = _sigmoid(q_raw)
    q = q_raw * sq * (HEAD_DIM ** -0.5)
    return dict(r32=r32, sig=sig, f=f, k=1.0 - f, b=b, a=a, bl=bl, sq=sq, q=q)


def _chunk_masks(n):
    ri = lax.broadcasted_iota(jnp.int32, (n, n), 0)
    ci = lax.broadcasted_iota(jnp.int32, (n, n), 1)
    same = (ri // CHUNK) == (ci // CHUNK)
    return same & (ci <= ri), same & (ri <= ci)


def _hgrn_fwd(proj, lower_bounds, gamma, H):
    L = proj.shape[0]
    nh = H // HEAD_DIM
    TL = min(L, 256)
    nc = TL // CHUNK

    def body(q_ref, f_ref, v_ref, g_ref, lbp_ref, gam_ref, og_ref, o_ref, s_ref, st_ref):
        @pl.when(pl.program_id(0) == 0)
        def _():
            st_ref[...] = jnp.zeros_like(st_ref)

        lb = _lower_bound(lbp_ref[...])
        gam = gam_ref[...]
        mask, _ = _chunk_masks(TL)
        rowc = lax.broadcasted_iota(jnp.int32, (TL, HEAD_DIM), 0) // CHUNK
        for h in range(nh):
            hs = slice(h * HEAD_DIM, (h + 1) * HEAD_DIM)
            p = _hgrn_prep(q_ref[:, hs], f_ref[:, hs], lb[:, hs])
            v = v_ref[:, hs]
            vb = v.astype(BF16)
            vt = v.T.astype(BF16)
            q_hat = (p["q"] * jnp.exp(p["b"] - p["a"])).astype(BF16)
            k_hat = (p["k"] * jnp.exp(p["a"] - p["b"])).astype(BF16)
            q_in = (p["q"] * jnp.exp(p["b"])).astype(BF16)
            k_out = (p["k"] * jnp.exp(p["bl"] - p["b"])).astype(BF16)
            dec = jnp.exp(p["bl"])
            att = jnp.where(mask, _nt(q_hat, k_hat), 0.0).astype(BF16)
            o_intra = _nn(att, vb)
            st = st_ref[h]
            for c in range(nc):
                rs = slice(c * CHUNK, (c + 1) * CHUNK)
                stb = st.astype(BF16)
                s_ref[c, h] = stb
                o_ref[rs, hs] = o_intra[rs] + _nt(q_in[rs], stb)
                k_c = jnp.where(rowc == c, k_out, jnp.zeros_like(k_out))
                st = st * dec[c * CHUNK:c * CHUNK + 1, :] + _nn(vt, k_c)
            st_ref[h] = st
            o = o_ref[:, hs]
            _, xh = _rms_stats(o)
            gr = g_ref[:, hs]
            og_ref[:, hs] = (xh * gam * (gr * _sigmoid(gr))).astype(BF16)

    col = lambda k: pl.BlockSpec((TL, H), lambda i, k=k: (i, k))
    return _pallas_call(
        body, name="hgrn_fwd", grid=(L // TL,),
        in_specs=[col(0), col(1), col(2), col(3),
                  pl.BlockSpec(lower_bounds.shape, lambda i: (0, 0)),
                  pl.BlockSpec(gamma.shape, lambda i: (0, 0))],
        out_specs=[pl.BlockSpec((TL, H), lambda i: (i, 0)),
                   pl.BlockSpec((TL, H), lambda i: (i, 0)),
                   pl.BlockSpec((nc, nh, HEAD_DIM, HEAD_DIM), lambda i: (i, 0, 0, 0))],
        out_shape=[_sds((L, H), BF16), _sds((L, H), F32),
                   _sds((L // CHUNK, nh, HEAD_DIM, HEAD_DIM), BF16)],
        scratch_shapes=[pltpu.VMEM((nh, HEAD_DIM, HEAD_DIM), F32)],
        compiler_params=_params(("arbitrary",), 48),
    )(proj, proj, proj, proj, lower_bounds, gamma)


def _hgrn_bwd(proj, lower_bounds, gamma, o_pre, d_out, s_saved, H, after, dproj):
    L = proj.shape[0]
    nh = H // HEAD_DIM
    TL = min(L, 256)
    nc = TL // CHUNK
    nt = L // TL

    def body(q_ref, f_ref, v_ref, g_ref, lbp_ref, gam_ref, o_ref, d_ref, s_ref, after_ref, dproj_ref,
             dp_ref, red_ref, dst_ref, dsall_ref, tmp_ref):
        @pl.when(pl.program_id(0) == 0)
        def _():
            dst_ref[...] = jnp.zeros_like(dst_ref)
            red_ref[...] = jnp.zeros_like(red_ref)

        lb = _lower_bound(lbp_ref[...])
        gam = gam_ref[...]
        mask, mask_t = _chunk_masks(TL)
        rowc = lax.broadcasted_iota(jnp.int32, (TL, HEAD_DIM), 0) // CHUNK
        for h in range(nh):
            hs = slice(h * HEAD_DIM, (h + 1) * HEAD_DIM)
            qr, gr, lbh = q_ref[:, hs], g_ref[:, hs], lb[:, hs]
            p = _hgrn_prep(qr, f_ref[:, hs], lbh)
            vb = v_ref[:, hs].astype(BF16)
            eba, eab = jnp.exp(p["b"] - p["a"]), jnp.exp(p["a"] - p["b"])
            eb, elb = jnp.exp(p["b"]), jnp.exp(p["bl"] - p["b"])
            dec = jnp.exp(p["bl"])
            q_hat, k_hat = p["q"] * eba, p["k"] * eab
            q_in, k_out = p["q"] * eb, p["k"] * elb
            q_hat_b, k_hat_b = q_hat.astype(BF16), k_hat.astype(BF16)
            q_in_b, k_out_b = q_in.astype(BF16), k_out.astype(BF16)

            o, dout = o_ref[:, hs], d_ref[:, hs]
            sg = _sigmoid(gr)
            r, xh = _rms_stats(o)
            dp_ref[3, :, hs] = (dout * (xh * gam) * (sg * (1.0 + gr * (1.0 - sg)))).astype(BF16)
            dn = dout * (gr * sg)
            red_ref[1:2, hs] += jnp.sum(dn * xh, axis=0, keepdims=True)
            do = _rms_bwd(dn * gam, xh, r)
            dob = do.astype(BF16)
            dot_b = do.T.astype(BF16)

            att_t = jnp.where(mask_t, _nt(k_hat_b, q_hat_b), 0.0).astype(BF16)
            dv_intra = _nn(att_t, dob)
            datt = jnp.where(mask, _nt(dob, vb), 0.0).astype(BF16)
            dqh = _nn(datt, k_hat_b)
            datt_t = jnp.where(mask_t, _nt(vb, dob), 0.0).astype(BF16)
            dkh = _nn(datt_t, q_hat_b)

            dst = dst_ref[h]
            for c in reversed(range(nc)):
                dsall_ref[c] = dst
                q_c = jnp.where(rowc == c, q_in_b, jnp.zeros_like(q_in_b))
                dst = dst * dec[c * CHUNK:c * CHUNK + 1, :] + _nn(dot_b, q_c)
            dst_ref[h] = dst
            for c in range(nc):
                rs = slice(c * CHUNK, (c + 1) * CHUNK)
                ds_c = dsall_ref[c]
                dsb = ds_c.astype(BF16)
                st_prev = s_ref[c, h]
                tmp_ref[0, rs, :] = _nt(k_out_b[rs], dsb)
                tmp_ref[1, rs, :] = _nn(vb[rs], dsb)
                tmp_ref[2, rs, :] = _nn(dob[rs], st_prev)
                ddec = jnp.sum(ds_c * st_prev.astype(F32), axis=0, keepdims=True)
                tmp_ref[3, rs, :] = jnp.broadcast_to(ddec * dec[c * CHUNK:c * CHUNK + 1, :],
                                                     (CHUNK, HEAD_DIM))
            dko, dqi = tmp_ref[1], tmp_ref[2]
            dq = dqh * eba + dqi * eb
            dk = dkh * eab + dko * elb
            tko = dko * k_out
            db = dqh * q_hat - dkh * k_hat + dqi * q_in - tko
            dlog = _seg_scan(db, p["r32"], False) + _chunk_total(tko) + tmp_ref[3]
            df = dlog / p["f"] - dk
            sig = p["sig"]
            red_ref[0:1, hs] += jnp.sum(df * (1.0 - sig), axis=0, keepdims=True)
            dp_ref[1, :, hs] = (df * (1.0 - lbh) * sig * (1.0 - sig)).astype(BF16)
            sq = p["sq"]
            dp_ref[0, :, hs] = (dq * (HEAD_DIM ** -0.5) * (sq * (1.0 + qr * (1.0 - sq)))).astype(BF16)
            dp_ref[2, :, hs] = (dv_intra + tmp_ref[0]).astype(BF16)

    col = lambda k: pl.BlockSpec((TL, H), lambda i, k=k: (nt - 1 - i, k))
    rev = pl.BlockSpec((TL, H), lambda i: (nt - 1 - i, 0))
    return _pallas_call(
        body, name="hgrn_bwd", grid=(nt,),
        in_specs=[col(0), col(1), col(2), col(3),
                  pl.BlockSpec(lower_bounds.shape, lambda i: (0, 0)),
                  pl.BlockSpec(gamma.shape, lambda i: (0, 0)),
                  rev, rev,
                  pl.BlockSpec((nc, nh, HEAD_DIM, HEAD_DIM), lambda i: (nt - 1 - i, 0, 0, 0)), ANY, ANY],
        out_specs=[pl.BlockSpec((4, TL, H), lambda i: (0, nt - 1 - i, 0)), pl.BlockSpec((8, H), lambda i: (0, 0))],
        out_shape=[_sds(dproj.shape, BF16), _sds((8, H), F32)],
        input_output_aliases={10: 0},
        scratch_shapes=[pltpu.VMEM((nh, HEAD_DIM, HEAD_DIM), F32),
                        pltpu.VMEM((nc, HEAD_DIM, HEAD_DIM), F32),
                        pltpu.VMEM((4, TL, HEAD_DIM), F32)],
        compiler_params=_params(("arbitrary",), 48),
    )(proj, proj, proj, proj, lower_bounds, gamma, o_pre, d_out, s_saved, after, dproj)


def _shift_down(u, s, row):
    return jnp.where(row >= s, pltpu.roll(u, s, 0), 0.0)


def _shift_up(u, s, row):
    n = u.shape[0]
    return jnp.where(row < n - s, pltpu.roll(u, n - s, 0), 0.0)


def _conv_specs(L, H):
    per = H // LANES
    return [pl.BlockSpec((L, LANES), lambda j, o=o: (0, o * per + j)) for o in (4, 5, 6)]


def _conv_fwd(proj, conv_w, H, after):
    L = proj.shape[0]

    def body(c_ref, b_ref, x_ref, w_ref, after_ref, o_ref):
        row = lax.broadcasted_iota(jnp.int32, (L, LANES), 0)
        u = c_ref[...] * x_ref[...]
        w = w_ref[...]
        y = w[0:1] * _shift_down(u, 2, row) + w[1:2] * _shift_down(u, 1, row) + w[2:3] * u
        o_ref[...] = (b_ref[...] * y).astype(BF16)

    return _pallas_call(
        body, name="conv_fwd", grid=(H // LANES,),
        in_specs=_conv_specs(L, H) + [pl.BlockSpec((3, LANES), lambda j: (0, j)), ANY],
        out_specs=pl.BlockSpec((L, LANES), lambda j: (0, j)),
        out_shape=_sds((L, H), BF16),
        compiler_params=_params(("parallel",), 48),
    )(proj, proj, proj, conv_w, after)


def _conv_bwd(proj, conv_w, dcb, H, after, dproj):
    L = proj.shape[0]

    def body(c_ref, b_ref, x_ref, w_ref, d_ref, after_ref, dproj_ref, dp_ref, dw_ref):
        row = lax.broadcasted_iota(jnp.int32, (L, LANES), 0)
        cg, xb = c_ref[...], x_ref[...]
        u = cg * xb
        u1, u2 = _shift_down(u, 1, row), _shift_down(u, 2, row)
        w = w_ref[...]
        y = w[0:1] * u2 + w[1:2] * u1 + w[2:3] * u
        d = d_ref[...]
        dp_ref[1] = (d * y).astype(BF16)
        dy = d * b_ref[...]
        du = w[2:3] * dy + w[1:2] * _shift_up(dy, 1, row) + w[0:1] * _shift_up(dy, 2, row)
        dw_ref[0:1, :] = jnp.sum(dy * u2, axis=0, keepdims=True)
        dw_ref[1:2, :] = jnp.sum(dy * u1, axis=0, keepdims=True)
        dw_ref[2:3, :] = jnp.sum(dy * u, axis=0, keepdims=True)
        dp_ref[0] = (du * xb).astype(BF16)
        dp_ref[2] = (du * cg).astype(BF16)
        dp_ref[3] = jnp.zeros((L, LANES), BF16)

    blk = pl.BlockSpec((L, LANES), lambda j: (0, j))
    return _pallas_call(
        body, name="conv_bwd", grid=(H // LANES,),
        in_specs=_conv_specs(L, H) + [pl.BlockSpec((3, LANES), lambda j: (0, j)), blk, ANY, ANY],
        out_specs=[pl.BlockSpec((4, L, LANES), lambda j: (2, 0, j)), pl.BlockSpec((3, LANES), lambda j: (0, j))],
        out_shape=[_sds(dproj.shape, BF16), _sds((3, H), F32)],
        input_output_aliases={6: 0},
        compiler_params=_params(("parallel",), 56),
    )(proj, proj, proj, conv_w, dcb, after, dproj)


def _gate_specs(tm, H):
    return [pl.BlockSpec((tm, H), lambda i, k=k: (i, k)) for k in (7, 8, 9, 10)]


def _fwd_mix(og, cb, proj, x, wat, wbt, wout, g_ffn, H, after):
    L, D = x.shape
    tm = min(L, 512)

    def body(o_ref, cb_ref, ga0, ga1, gb0, gb1, x_ref, wa_ref, wb_ref, wo_ref, g_ref, after_ref,
             sa_ref, sb_ref, ta_ref, tb_ref, m_ref, x1_ref, h2_ref):
        ya, yb = _nt(o_ref[...], wa_ref[...]), _nt(cb_ref[...], wb_ref[...])
        for k, (gar, gbr) in enumerate(((ga0, gb0), (ga1, gb1))):
            cs = slice(k * H, (k + 1) * H)
            sa, sb = _sigmoid(gar[...]), _sigmoid(gbr[...])
            ma, mb = sa * ya[:, cs], sb * yb[:, cs]
            m_ref[:, cs] = (ma + mb).astype(BF16)
            sa_ref[:, cs] = sa.astype(BF16)
            sb_ref[:, cs] = sb.astype(BF16)
            ta_ref[:, cs] = (ma * (1.0 - sa)).astype(BF16)
            tb_ref[:, cs] = (mb * (1.0 - sb)).astype(BF16)
        x1 = x_ref[...] + _nn(m_ref[...], wo_ref[...])
        x1_ref[...] = x1
        _, xh = _rms_stats(x1)
        h2_ref[...] = (xh * g_ref[...]).astype(BF16)

    row = lambda w: pl.BlockSpec((tm, w), lambda i: (i, 0))
    full = lambda a: pl.BlockSpec(a.shape, lambda i: (0,) * a.ndim)
    return _pallas_call(
        body, name="fwd_mix", grid=(L // tm,),
        in_specs=[row(H), row(H)] + _gate_specs(tm, H) + [row(D), full(wat), full(wbt), full(wout),
                                                           full(g_ffn), ANY],
        out_specs=[row(D)] * 7,
        out_shape=[_sds((L, D), BF16)] * 5 + [_sds((L, D), F32), _sds((L, D), BF16)],
        compiler_params=_params(("parallel",), 56),
    )(og, cb, proj, proj, proj, proj, x, wat, wbt, wout, g_ffn, after)


def _bwd_mix(dx1b, sig_a, sig_b, dm_dga, dm_dgb, wat, wbt, wout, H, after):
    L, D = dx1b.shape
    tm = min(L, 512)

    def body(dx_ref, sa_ref, sb_ref, ta_ref, tb_ref, wa_ref, wb_ref, wo_ref, after_ref,
             dya_ref, dyb_ref, dgate_ref, do_ref, dcb_ref):
        dm = _nt(dx_ref[...], wo_ref[...])
        dga = (dm * ta_ref[...].astype(F32)).astype(BF16)
        dgb = (dm * tb_ref[...].astype(F32)).astype(BF16)
        for q, part in enumerate((dga[:, 0:H], dga[:, H:D], dgb[:, 0:H], dgb[:, H:D])):
            dgate_ref[q] = part
        dya_ref[...] = (dm * sa_ref[...].astype(F32)).astype(BF16)
        dyb_ref[...] = (dm * sb_ref[...].astype(F32)).astype(BF16)
        do_ref[...] = _nn(dya_ref[...], wa_ref[...])
        dcb_ref[...] = _nn(dyb_ref[...], wb_ref[...])

    row = lambda w: pl.BlockSpec((tm, w), lambda i: (i, 0))
    full = lambda a: pl.BlockSpec(a.shape, lambda i: (0,) * a.ndim)
    return _pallas_call(
        body, name="bwd_mix", grid=(L // tm,),
        in_specs=[row(D)] * 5 + [full(wat), full(wbt), full(wout), ANY],
        out_specs=[row(D), row(D), pl.BlockSpec((4, tm, H), lambda i: (1, i, 0)), row(H), row(H)],
        out_shape=[_sds((L, D), BF16)] * 2 + [_sds((DPROJ_BLOCKS, L, H), BF16)] + [_sds((L, H), F32)] * 2,
        compiler_params=_params(("parallel",), 56),
    )(dx1b, sig_a, sig_b, dm_dga, dm_dgb, wat, wbt, wout, after)


def _fwd_ffn_up(h2, wgt, wut):
    L, D = h2.shape
    F = wgt.shape[0]
    tn = F // 2
    tm = min(L, 512)

    def body(h_ref, wg_ref, wu_ref, sa_ref, sb_ref, s_ref):
        h = h_ref[...]
        a, b = _nt(h, wg_ref[...]), _nt(h, wu_ref[...])
        sg = _sigmoid(a)
        silu = a * sg
        sa_ref[...] = (b * sg * (1.0 + a * (1.0 - sg))).astype(BF16)
        sb_ref[...] = silu.astype(BF16)
        s_ref[...] = (silu * b).astype(BF16)

    wspec = pl.BlockSpec((tn, D), lambda j, i: (j, 0))
    ospec = pl.BlockSpec((tm, tn), lambda j, i: (i, j))
    return _pallas_call(
        body, name="fwd_ffn_up", grid=(2, L // tm),
        in_specs=[pl.BlockSpec((tm, D), lambda j, i: (i, 0)), wspec, wspec],
        out_specs=[ospec] * 3,
        out_shape=[_sds((L, F), BF16)] * 3,
        compiler_params=_params(("parallel", "parallel"), 48),
    )(h2, wgt, wut)


def _fwd_down_loss(s, wd, x1, target, g_final):
    L, D = x1.shape
    F = wd.shape[0]
    tm = min(L, 512)

    def body(s_ref, wd_ref, x1_ref, t_ref, g_ref, dx_ref, dxb_ref, red_ref):
        @pl.when(pl.program_id(0) == 0)
        def _():
            red_ref[...] = jnp.zeros_like(red_ref)

        g = g_ref[...]
        r, xh = _rms_stats(x1_ref[...] + _nn(s_ref[...], wd_ref[...]))
        e = xh * g - t_ref[...]
        dy = e * (1.0 / D)
        dx = _rms_bwd(dy * g, xh, r)
        dx_ref[...] = dx
        dxb_ref[...] = dx.astype(BF16)
        red_ref[0:1, :] += jnp.sum(dy * xh, axis=0, keepdims=True)
        red_ref[1:2, :] += jnp.broadcast_to(0.5 * jnp.sum(e * e) * (1.0 / D), (1, D))

    row = pl.BlockSpec((tm, D), lambda i: (i, 0))
    return _pallas_call(
        body, name="fwd_down_loss", grid=(L // tm,),
        in_specs=[pl.BlockSpec((tm, F), lambda i: (i, 0)), pl.BlockSpec((F, D), lambda i: (0, 0)),
                  row, row, pl.BlockSpec((1, D), lambda i: (0, 0))],
        out_specs=[row, row, pl.BlockSpec((8, D), lambda i: (0, 0))],
        out_shape=[_sds((L, D), F32), _sds((L, D), BF16), _sds((8, D), F32)],
        compiler_params=_params(("arbitrary",), 56),
    )(s, wd, x1, target, g_final)


def _bwd_down(dx2b, wd, s_a, s_b):
    L, D = dx2b.shape
    F = wd.shape[0]
    tn = F // 2
    tm = min(L, 512)

    def body(dx_ref, wd_ref, sa_ref, sb_ref, da_ref, db_ref):
        ds = _nt(dx_ref[...], wd_ref[...])
        da_ref[...] = (ds * sa_ref[...].astype(F32)).astype(BF16)
        db_ref[...] = (ds * sb_ref[...].astype(F32)).astype(BF16)

    ospec = pl.BlockSpec((tm, tn), lambda j, i: (i, j))
    return _pallas_call(
        body, name="bwd_down", grid=(2, L // tm),
        in_specs=[pl.BlockSpec((tm, D), lambda j, i: (i, 0)),
                  pl.BlockSpec((tn, D), lambda j, i: (j, 0)), ospec, ospec],
        out_specs=[ospec] * 2,
        out_shape=[_sds((L, F), BF16)] * 2,
        compiler_params=_params(("parallel", "parallel"), 48),
    )(dx2b, wd, s_a, s_b)


def _bwd_ffn_dh(da, db, wgt, wut, x1, dx2, g_ffn, after):
    L, D = x1.shape
    F = wgt.shape[0]
    tm = min(L, 256)

    def body(da_ref, db_ref, wg_ref, wu_ref, x1_ref, dx2_ref, g_ref, after_ref, dx_ref, dxb_ref, red_ref):
        @pl.when(pl.program_id(0) == 0)
        def _():
            red_ref[...] = jnp.zeros_like(red_ref)

        dh = _nn(da_ref[...], wg_ref[...]) + _nn(db_ref[...], wu_ref[...])
        r, xh = _rms_stats(x1_ref[...])
        red_ref[0:1, :] += jnp.sum(dh * xh, axis=0, keepdims=True)
        dx = dx2_ref[...] + _rms_bwd(dh * g_ref[...], xh, r)
        dx_ref[...] = dx
        dxb_ref[...] = dx.astype(BF16)

    row = pl.BlockSpec((tm, D), lambda i: (i, 0))
    aspec = pl.BlockSpec((tm, F), lambda i: (i, 0))
    wspec = pl.BlockSpec((F, D), lambda i: (0, 0))
    return _pallas_call(
        body, name="bwd_ffn_dh", grid=(L // tm,),
        in_specs=[aspec, aspec, wspec, wspec, row, row, pl.BlockSpec((1, D), lambda i: (0, 0)), ANY],
        out_specs=[row, row, pl.BlockSpec((8, D), lambda i: (0, 0))],
        out_shape=[_sds((L, D), F32), _sds((L, D), BF16), _sds((8, D), F32)],
        compiler_params=_params(("arbitrary",), 56),
    )(da, db, wgt, wut, x1, dx2, g_ffn, after)


def _bwd_in(dproj, w_int, x, dx1, g_mix, after):
    L, D = x.shape
    N = w_int.shape[0]
    H = dproj.shape[2]
    tm = min(L, 256)
    assert N == len(DPROJ_BLOCK_OF) * H

    def body(blocks_ref, w_ref, x_ref, dx1_ref, g_ref, after_ref, dx_ref, red_ref, dp_ref):
        @pl.when(pl.program_id(0) == 0)
        def _():
            red_ref[...] = jnp.zeros_like(red_ref)

        for t, block in enumerate(DPROJ_BLOCK_OF):
            dp_ref[:, t * H:(t + 1) * H] = blocks_ref[block]
        dh = _nn(dp_ref[...], w_ref[...])
        r, xh = _rms_stats(x_ref[...])
        red_ref[0:1, :] += jnp.sum(dh * xh, axis=0, keepdims=True)
        dx_ref[...] = dx1_ref[...] + _rms_bwd(dh * g_ref[...], xh, r)

    row = pl.BlockSpec((tm, D), lambda i: (i, 0))
    return _pallas_call(
        body, name="bwd_in", grid=(L // tm,),
        in_specs=[pl.BlockSpec((DPROJ_BLOCKS, tm, H), lambda i: (0, i, 0)), pl.BlockSpec((N, D), lambda i: (0, 0)),
                  row, row, pl.BlockSpec((1, D), lambda i: (0, 0)), ANY],
        out_specs=[row, pl.BlockSpec((8, D), lambda i: (0, 0))],
        out_shape=[_sds((L, D), F32), _sds((8, D), F32)],
        scratch_shapes=[pltpu.VMEM((tm, N), BF16)],
        compiler_params=_params(("arbitrary",), 56),
    )(dproj, w_int, x, dx1, g_mix, after)


def _dw_in(h, dproj, n_cols, c_idx):
    L, D = h.shape
    H = dproj.shape[2]
    tk = min(L, TK_TOKENS)
    nk = L // tk
    r2 = D // 2
    first = [(j * n_cols) // H for j in range(N_CHIPS)]
    last = [((j + 1) * n_cols - 1) // H for j in range(N_CHIPS)]
    slots = max(b - a for a, b in zip(first, last)) + 1
    plan = []
    for j in range(N_CHIPS):
        lo, hi = j * n_cols, (j + 1) * n_cols
        segments = []
        for s in range(last[j] - first[j] + 1):
            a, b = max(lo, (first[j] + s) * H), min(hi, (first[j] + s + 1) * H)
            segments.append((s, a - (first[j] + s) * H, b - a, a - lo))
        plan.append(segments)

    def body(c_ref, *refs):
        h_ref, slot_refs = refs[0], refs[1:1 + slots]
        o_ref, sib_ref, b_ref = refs[1 + slots:]
        j, k = pl.program_id(0), pl.program_id(1)
        for jj in range(N_CHIPS):
            @pl.when(j == jj)
            def _(jj=jj):
                for s, start, width, at in plan[jj]:
                    b_ref[:, at:at + width] = slot_refs[s][:, start:start + width]

        part = _tn(h_ref[...], b_ref[...])

        @pl.when(k == 0)
        def _():
            o_ref[...] = part

        @pl.when(k > 0)
        def _():
            o_ref[...] += part

        @pl.when(k == nk - 1)
        def _():
            theirs = pl.ds(pl.multiple_of((1 - c_ref[0]) * r2, 8), r2)
            sib_ref[...] = o_ref[theirs, :].astype(BF16)

    def slot_spec(s):
        blocks = [DPROJ_BLOCK_OF[min(first[j] + s, last[j])] for j in range(N_CHIPS)]

        def index(j, k, c_ref):
            block = blocks[0]
            for jj in range(1, N_CHIPS):
                block = jnp.where(j == jj, blocks[jj], block)
            return (block, k, 0)

        return pl.BlockSpec((None, tk, H), index)

    return _pallas_call(
        body, name="dw_in",
        grid_spec=pltpu.PrefetchScalarGridSpec(
            num_scalar_prefetch=1, grid=(N_CHIPS, nk),
            in_specs=[pl.BlockSpec((tk, D), lambda j, k, c_ref: (k, 0))] + [slot_spec(s) for s in range(slots)],
            out_specs=[pl.BlockSpec((None, D, n_cols), lambda j, k, c_ref: (j, 0, 0)),
                       pl.BlockSpec((None, r2, n_cols), lambda j, k, c_ref: (j, 0, 0))],
            scratch_shapes=[pltpu.VMEM((tk, n_cols), BF16)]),
        out_shape=[_sds((N_CHIPS, D, n_cols), F32), _sds((N_CHIPS, r2, n_cols), BF16)],
        compiler_params=_params(("parallel", "arbitrary"), 56),
    )(c_idx, h, *([dproj] * slots))


def _mm_tn(name, a, b, a_spec, b_spec, o_block, n_out, n_k):
    def body(a_ref, b_ref, o_ref):
        part = _tn(a_ref[...], b_ref[...])

        @pl.when(pl.program_id(1) == 0)
        def _():
            o_ref[...] = part

        @pl.when(pl.program_id(1) > 0)
        def _():
            o_ref[...] += part

    return _pallas_call(
        body, name=name, grid=(n_out, n_k),
        in_specs=[a_spec, b_spec],
        out_specs=pl.BlockSpec((None,) + o_block, lambda j, k: (j, 0, 0)),
        out_shape=_sds((n_out,) + o_block, F32),
        compiler_params=_params(("parallel", "arbitrary"), 56),
    )(a, b)


TK_TOKENS = 2048


def _dw_whole(name, pairs, by_rows):
    n = len(pairs)
    L = pairs[0][0].shape[0]
    tk = min(L, TK_TOKENS)

    def body(*refs):
        for q in range(n):
            a_ref, b_ref, o_ref = refs[2 * q], refs[2 * q + 1], refs[2 * n + q]
            part = _tn(a_ref[...], b_ref[...])
            rows, cols = o_ref.shape[1], o_ref.shape[2]
            shards = [part[j * rows:(j + 1) * rows, :] if by_rows else part[:, j * cols:(j + 1) * cols]
                      for j in range(N_CHIPS)]

            @pl.when(pl.program_id(0) == 0)
            def _(shards=shards, o_ref=o_ref):
                for j, shard in enumerate(shards):
                    o_ref[j] = shard

            @pl.when(pl.program_id(0) > 0)
            def _(shards=shards, o_ref=o_ref):
                for j, shard in enumerate(shards):
                    o_ref[j] += shard

    in_specs, out_specs, out_shape, operands = [], [], [], []
    for a, b in pairs:
        M, N = a.shape[1], b.shape[1]
        shape = (N_CHIPS, M // N_CHIPS, N) if by_rows else (N_CHIPS, M, N // N_CHIPS)
        in_specs += [pl.BlockSpec((tk, M), lambda k: (k, 0)), pl.BlockSpec((tk, N), lambda k: (k, 0))]
        out_specs.append(pl.BlockSpec(shape, lambda k: (0, 0, 0)))
        out_shape.append(_sds(shape, F32))
        operands += [a, b]
    return _pallas_call(
        body, name=name, grid=(L // tk,), in_specs=in_specs, out_specs=out_specs, out_shape=out_shape,
        compiler_params=_params(("arbitrary",), 56),
    )(*operands)


def _dw_rows2(name, a, b):
    L, M = a.shape
    N = b.shape[1]
    tk = min(L, TK_TOKENS)
    return _mm_tn(name, a, b, pl.BlockSpec((tk, M // 2), lambda j, k: (k, j)),
                  pl.BlockSpec((tk, N), lambda j, k: (k, 0)), (M // 2, N), 2, L // tk)


def _place():
    x, y, c = lax.axis_index("x"), lax.axis_index("y"), lax.axis_index("c")
    chips = [(1 - x, y), (x, 1 - y), (1 - x, 1 - y)]
    return x, y, c, 2 * x + y, chips


def _remote(src, dst, send_sem, recv_sem, device):
    return pltpu.make_async_remote_copy(src_ref=src, dst_ref=dst, send_sem=send_sem,
                                        recv_sem=recv_sem, device_id=device, device_id_type=MESH)


def _half(ref, lead, c, r2):
    return ref.at[lead, pl.ds(pl.multiple_of(c * r2, 16), r2), :]


def _cast_place(name, ws, chip_idx):
    n = len(ws)

    def body(k_ref, *refs):
        for w_ref, o_ref in zip(refs[:n], refs[n:]):
            o_ref[...] = w_ref[...].astype(BF16)

    return _pallas_call(
        body, name=name,
        grid_spec=pltpu.PrefetchScalarGridSpec(
            num_scalar_prefetch=1, grid=(2,),
            in_specs=[pl.BlockSpec((w.shape[0] // 2, w.shape[1]), lambda i, k_ref: (i, 0)) for w in ws],
            out_specs=[pl.BlockSpec((None, w.shape[0] // 2, w.shape[1]), lambda i, k_ref: (k_ref[0], i, 0))
                       for w in ws]),
        out_shape=[_sds((N_CHIPS,) + w.shape, BF16) for w in ws],
        compiler_params=_params(("parallel",), 48),
    )(chip_idx, *ws)


def _cast_place_t(name, ws, chip_idx):
    n = len(ws)
    r, cols = ws[0].shape

    def body(k_ref, *refs):
        for w_ref, o_ref in zip(refs[:n], refs[n:]):
            o_ref[...] = w_ref[...].T.astype(BF16)

    return _pallas_call(
        body, name=name,
        grid_spec=pltpu.PrefetchScalarGridSpec(
            num_scalar_prefetch=1, grid=(cols // LANES,),
            in_specs=[pl.BlockSpec((r, LANES), lambda i, k_ref: (0, i))] * n,
            out_specs=[pl.BlockSpec((None, LANES, r), lambda i, k_ref: (k_ref[0], i, 0))] * n),
        out_shape=[_sds((N_CHIPS, cols, r), BF16)] * n,
        compiler_params=_params(("parallel",), 48),
    )(chip_idx, *ws)


def _gather_copies(bufs, whole, send_sems, recv_sems, select=None):
    x, y, c, k, chips = _place()
    pairs = []
    for w, buf in enumerate(bufs):
        for j, (cx, cy) in enumerate(chips):
            if select is not None and not select(w, j):
                continue
            if w in whole:
                mine, theirs = buf.at[k], buf.at[2 * cx + cy]
            else:
                r2 = buf.shape[1] // 2
                mine, theirs = _half(buf, k, c, r2), _half(buf, 2 * cx + cy, c, r2)
            sems = (send_sems.at[w * 3 + j], recv_sems.at[w * 3 + j])
            pairs.append((_remote(mine, mine, *sems, (cx, cy, c)), _remote(theirs, theirs, *sems, (x, y, c))))
    return pairs


def _gather_start(name, groups, after):
    flat = [b for bufs, _, _ in groups for b in bufs]
    nb, ng = len(flat), len(groups)

    def body(*refs):
        ins, sems, token = refs[:nb], refs[nb + 1:nb + 1 + 2 * ng], refs[-1]
        pos = 0
        for g, (bufs, whole, select) in enumerate(groups):
            for send, _ in _gather_copies(ins[pos:pos + len(bufs)], whole, sems[2 * g], sems[2 * g + 1], select):
                send.start()
            pos += len(bufs)
        token[...] = jnp.zeros_like(token)

    sem_shapes = []
    for bufs, _, _ in groups:
        sem_shapes += [pltpu.SemaphoreType.DMA((3 * len(bufs),))] * 2
    out = _pallas_call(
        body, name=name,
        in_specs=[HBM] * nb + [ANY], out_specs=tuple([SEM] * (2 * ng) + [HBM] * nb + [VMEM]),
        out_shape=tuple(sem_shapes + [pltpu.HBM(b.shape, b.dtype) for b in flat] + [_sds((8, LANES), F32)]),
        input_output_aliases={i: 2 * ng + i for i in range(nb)},
        compiler_params=pltpu.CompilerParams(has_side_effects=EFFECT),
    )(*flat, after)
    sems, thru, pos = [], [], 2 * ng
    for g, (bufs, _, _) in enumerate(groups):
        sems.append((out[2 * g], out[2 * g + 1]))
        thru.append(list(out[pos:pos + len(bufs)]))
        pos += len(bufs)
    return sems, thru, out[-1]


def _gather_wait(name, bufs, whole, sems, after, select=None):
    nb = len(bufs)

    def body(*refs):
        ins, send_sems, recv_sems = refs[:nb], refs[nb], refs[nb + 1]
        for send, arrival in _gather_copies(ins, whole, send_sems, recv_sems, select):
            send.wait_send()
            arrival.wait_recv()

    return _pallas_call(
        body, name=name,
        in_specs=[HBM] * nb + [SEM, SEM, ANY], out_specs=[HBM] * nb,
        out_shape=[pltpu.HBM(b.shape, b.dtype) for b in bufs],
        input_output_aliases={i: i for i in range(nb)},
        compiler_params=pltpu.CompilerParams(has_side_effects=EFFECT),
    )(*bufs, sems[0], sems[1], after)


def _gather_forward(name, bufs, sources=(0, 1, 2)):
    n = len(bufs)

    def body(*refs):
        outs = refs[n:2 * n]
        send_sems, recv_sems = refs[2 * n:]
        x, y, c, _, chips = _place()
        sends = []
        for w in range(n):
            r2 = outs[w].shape[1] // 2
            for j in sources:
                landed = _half(outs[w], 2 * chips[j][0] + chips[j][1], c, r2)
                sends.append(_remote(landed, landed, send_sems.at[w * 3 + j], recv_sems.at[w * 3 + j],
                                     (x, y, 1 - c)))
        for cp in sends:
            cp.start()
        for w in range(n):
            r2 = outs[w].shape[1] // 2
            for j in sources:
                got = _half(outs[w], 2 * chips[j][0] + chips[j][1], 1 - c, r2)
                _remote(got, got, send_sems.at[w * 3 + j], recv_sems.at[w * 3 + j], (x, y, c)).wait_recv()
        for cp in sends:
            cp.wait_send()

    return _pallas_call(
        body, name=name,
        in_specs=[ANY] * n, out_specs=[ANY] * n,
        out_shape=[_sds(b.shape, b.dtype) for b in bufs],
        input_output_aliases={i: i for i in range(n)},
        scratch_shapes=[pltpu.SemaphoreType.DMA((n * 3,)), pltpu.SemaphoreType.DMA((n * 3,))],
    )(*bufs)


def _rs_add(name, grads3, from_sibling, c_idx):
    n = len(grads3)

    def body(c_ref, *refs):
        for g_ref, s_ref, o_ref in zip(refs[:n], refs[n:2 * n], refs[2 * n:]):
            o_ref[...] = (g_ref[...] + s_ref[...].astype(F32)).astype(BF16)

    mine =[pl.BlockSpec((None,) + s.shape[1:], lambda k, c_ref: (k, c_ref[0], 0)) for s in from_sibling]
    whole = [pl.BlockSpec((None,) + s.shape[1:], lambda k, c_ref: (k, 0, 0)) for s in from_sibling]
    return _pallas_call(
        body, name=name,
        grid_spec=pltpu.PrefetchScalarGridSpec(num_scalar_prefetch=1, grid=(N_CHIPS,), in_specs=mine + whole,
                                               out_specs=whole),
        out_shape=[_sds(s.shape, BF16) for s in from_sibling],
        compiler_params=_params(("parallel",), 48),
    )(c_idx, *grads3, *from_sibling)


def _split_start(name, arrays, n_sems, pairs_fn):
    n = len(arrays)

    def body(*refs):
        for send, _ in pairs_fn(refs[:n], refs[n], refs[n + 1]):
            send.start()
        refs[-1][...] = jnp.zeros_like(refs[-1])

    out = _pallas_call(
        body, name=name,
        in_specs=[HBM] * n, out_specs=tuple([SEM, SEM] + [HBM] * n + [VMEM]),
        out_shape=tuple([pltpu.SemaphoreType.DMA((n_sems,))] * 2 + [pltpu.HBM(a.shape, a.dtype) for a in arrays]
                        + [_sds((8, LANES), F32)]),
        input_output_aliases={i: 2 + i for i in range(n)},
        compiler_params=pltpu.CompilerParams(has_side_effects=EFFECT),
    )(*arrays)
    return (out[0], out[1]), list(out[2:2 + n]), out[-1]


def _split_wait(name, sems, arrays, pairs_fn, after):
    n = len(arrays)

    def body(*refs):
        for send, arrival in pairs_fn(refs[:n], refs[n], refs[n + 1]):
            send.wait_send()
            arrival.wait_recv()

    return list(_pallas_call(
        body, name=name,
        in_specs=[HBM] * n + [SEM, SEM, ANY], out_specs=[HBM] * n,
        out_shape=[pltpu.HBM(a.shape, a.dtype) for a in arrays],
        input_output_aliases={i: i for i in range(n)},
        compiler_params=pltpu.CompilerParams(has_side_effects=EFFECT),
    )(*arrays, sems[0], sems[1], after))


def _forward_pairs(bufs, send_sems, recv_sems):
    x, y, c, _, chips = _place()
    pairs = []
    for w, buf in enumerate(bufs):
        r2 = buf.shape[1] // 2
        for j, (cx, cy) in enumerate(chips):
            landed, theirs = _half(buf, 2 * cx + cy, c, r2), _half(buf, 2 * cx + cy, 1 - c, r2)
            sems = (send_sems.at[w * 3 + j], recv_sems.at[w * 3 + j])
            pairs.append((_remote(landed, landed, *sems, (x, y, 1 - c)), _remote(theirs, theirs, *sems, (x, y, c))))
    return pairs


def _sibling_pairs(arrays, send_sems, recv_sems):
    x, y, c, _, _ = _place()
    n = len(arrays) // 2
    pairs = []
    for w in range(n):
        r2 = arrays[w].shape[1] // 2
        cp = _remote(_half(arrays[w], slice(None), 1 - c, r2), arrays[n + w], send_sems.at[w], recv_sems.at[w],
                     (x, y, 1 - c))
        pairs.append((cp, cp))
    return pairs


def _sibling_whole_pairs(arrays, send_sems, recv_sems):
    x, y, c, _, _ = _place()
    n = len(arrays) // 2
    pairs = []
    for w in range(n):
        cp = _remote(arrays[w], arrays[n + w], send_sems.at[w], recv_sems.at[w], (x, y, 1 - c))
        pairs.append((cp, cp))
    return pairs


def _ici_pairs(arrays, send_sems, recv_sems, first=0):
    x, y, c, _, chips = _place()
    n = len(arrays) // 2
    pairs = []
    for w in range(n):
        for j, (cx, cy) in enumerate(chips):
            cp = _remote(arrays[w].at[2 * cx + cy], arrays[n + w].at[j],
                         send_sems.at[first + w * 3 + j], recv_sems.at[first + w * 3 + j], (cx, cy, c))
            pairs.append((cp, cp))
    return pairs


def _ici_pairs_behind_sibling(arrays, send_sems, recv_sems):
    return _ici_pairs(arrays, send_sems, recv_sems, 3)


def _sibling_ici_pairs(arrays, send_sems, recv_sems):
    return _sibling_pairs(arrays[:6], send_sems, recv_sems) + _ici_pairs_behind_sibling(arrays[6:], send_sems, recv_sems)


def _rs_sum(name, partials, received, place_idx):
    n = len(partials)
    nb = 2
    blocks = [(p.shape[1] // nb, p.shape[2]) for p in partials]

    def body(idx_ref, *refs):
        for p_ref, r_ref, o_ref in zip(refs[:n], refs[n:2 * n], refs[2 * n:]):
            o_ref[...] = ((p_ref[...].astype(F32) + r_ref[0].astype(F32))
                          + (r_ref[1].astype(F32) + r_ref[2].astype(F32)))

    return _pallas_call(
        body, name=name,
        grid_spec=pltpu.PrefetchScalarGridSpec(
            num_scalar_prefetch=1, grid=(nb,),
            in_specs=[pl.BlockSpec((None,) + b, lambda i, idx: (idx[0], i, 0)) for b in blocks]
            + [pl.BlockSpec((3,) + b, lambda i, idx: (0, i, 0)) for b in blocks],
            out_specs=[pl.BlockSpec(b, lambda i, idx: (idx[1] * nb + i, 0)) for b in blocks]),
        out_shape=[_sds((2 * p.shape[1], p.shape[2]), F32) for p in partials],
        compiler_params=_params(("parallel",), 48),
    )(place_idx, *partials, *received)


def _share_pairs(arrays, send_sems, recv_sems, first=0):
    x, y, c, _, _ = _place()
    pairs = []
    for w, arr in enumerate(arrays):
        r2 = arr.shape[0] // 2
        mine = arr.at[pl.ds(pl.multiple_of(c * r2, 8), r2), :]
        theirs = arr.at[pl.ds(pl.multiple_of((1 - c) * r2, 8), r2), :]
        sems = (send_sems.at[first + w], recv_sems.at[first + w])
        pairs.append((_remote(mine, mine, *sems, (x, y, 1 - c)), _remote(theirs, theirs, *sems, (x, y, c))))
    return pairs


def _small_pack(red_mix, red_ffn, red_final, red_hg, g_conv):
    D = red_mix.shape[1]
    H = red_hg.shape[1]

    def body(mix_ref, ffn_ref, fin_ref, hg_ref, cv_ref, in_ref):
        in_ref[...] = jnp.zeros_like(in_ref)
        in_ref[0:1, :] = mix_ref[0:1, :]
        in_ref[1:2, :] = ffn_ref[0:1, :]
        in_ref[2:3, :] = fin_ref[0:1, :]
        gam = hg_ref[1:2, 0:HEAD_DIM]
        for h in range(1, H // HEAD_DIM):
            gam = gam + hg_ref[1:2, h * HEAD_DIM:(h + 1) * HEAD_DIM]
        in_ref[3:4, 0:HEAD_DIM] = gam
        in_ref[3:4, HEAD_DIM:2 * HEAD_DIM] = fin_ref[1:2, 0:HEAD_DIM]
        in_ref[4:5, 0:H] = hg_ref[0:1, :]
        in_ref[6:9, 0:H] = cv_ref[...]

    return _pallas_call(
        body, name="small_pack", pin=False,
        in_specs=[VMEM] * 5, out_specs=VMEM, out_shape=_sds((N_SMALL_ROWS, D), F32),
    )(red_mix, red_ffn, red_final, red_hg, g_conv)


def _small_pairs(arrays, send_sems, recv_sems):
    block, gathered = arrays
    x, y, c, _, _ = _place()
    me = 4 * x + 2 * y + c
    pairs = []
    for m in range(1, 8):
        px, py, pc = x ^ ((m >> 2) & 1), y ^ ((m >> 1) & 1), c ^ (m & 1)
        sems = (send_sems.at[m - 1], recv_sems.at[m - 1])
        pairs.append((_remote(block, gathered.at[me], *sems, (px, py, pc)),
                      _remote(block, gathered.at[4 * px + 2 * py + pc], *sems, (x, y, c))))
    return pairs


def _share_pairs_behind_ici(arrays, send_sems, recv_sems):
    return _share_pairs(arrays, send_sems, recv_sems, 3)


def _ici_share_pairs(arrays, send_sems, recv_sems):
    return _ici_pairs(arrays[:2], send_sems, recv_sems) + _share_pairs_behind_ici(arrays[2:], send_sems, recv_sems)


def _tail_pairs(arrays, send_sems, recv_sems):
    return _small_pairs(arrays[:2], send_sems, recv_sems) + _share_pairs(arrays[2:], send_sems, recv_sems, 7)


def _adamw_math(w, g, m, v):
    m = ADAM_B1 * m + (1.0 - ADAM_B1) * g
    v = ADAM_B2 * v + (1.0 - ADAM_B2) * jnp.square(g)
    m_hat = m / (1.0 - ADAM_B1 ** ADAM_STEP)
    v_hat = v / (1.0 - ADAM_B2 ** ADAM_STEP)
    delta = -ADAM_LR * (m_hat / (jnp.sqrt(v_hat) + ADAM_EPS) + ADAM_WD * w)
    return delta, m, v


def _adamw(name, gs, ws, ms, vs):
    n = len(gs)
    nb = 4

    def body(*refs):
        ins, outs = refs[:4 * n], refs[4 * n:]
        for j in range(n):
            g_ref, w_ref, m_ref, v_ref = ins[j], ins[n + j], ins[2 * n + j], ins[3 * n + j]
            go_ref, d_ref, mo_ref, vo_ref = outs[4 * j:4 * j + 4]
            g = g_ref[...]
            go_ref[...] = g
            d_ref[...], mo_ref[...], vo_ref[...] = _adamw_math(w_ref[...], g, m_ref[...], v_ref[...])

    blk = [pl.BlockSpec((g.shape[0] // nb, g.shape[1]), lambda i: (i, 0)) for g in gs]
    out = _pallas_call(
        body, name=name, grid=(nb,),
        in_specs=blk * 4, out_specs=[b for b in blk for _ in range(4)],
        out_shape=[_sds(g.shape, F32) for g in gs for _ in range(4)],
        compiler_params=_params(("parallel",), 56),
    )(*gs, *ws, *ms, *vs)
    return [list(out[4 * j:4 * j + 4]) for j in range(n)]


def _small_update(block, gathered, place_idx, ws, ms, vs):
    n = len(ws)
    H = ws[1].shape[1]

    def body(idx_ref, blk_ref, all_ref, *refs):
        w, m, v, outs, tot_ref = refs[:n], refs[n:2 * n], refs[2 * n:3 * n], refs[3 * n:-1], refs[-1]
        chip, me = idx_ref[0], idx_ref[1]
        tot = jnp.where(me == 0, blk_ref[...], all_ref[0])
        for d in range(1, 8):
            tot = tot + jnp.where(me == d, blk_ref[...], all_ref[d])
        tot_ref[...] = tot
        p0 = _lower_bound(w[1][...])
        dl0 = p0 * (1.0 - p0) * tot_ref[4:5, 0:H]
        conv = jnp.zeros((3, LANES), F32)
        for k in range(N_CHIPS):
            conv = jnp.where(chip == k, tot_ref[6:9, k * LANES:(k + 1) * LANES], conv)
        grads = [tot_ref[0:1, :], None, tot_ref[3:4, 0:HEAD_DIM], conv, tot_ref[1:2, :], tot_ref[2:3, :]]
        for p in range(n):
            g_ref, d_ref, mo_ref, vo_ref = outs[4 * p:4 * p + 4]
            if p == 1:
                for row, g in ((slice(0, 1), dl0), (slice(1, 2), -dl0)):
                    g_ref[row, :] = g
                    d_ref[row, :], mo_ref[row, :], vo_ref[row, :] = _adamw_math(
                        w[p][row, :], g, m[p][row, :], v[p][row, :])
            else:
                g_ref[...] = grads[p]
                d_ref[...], mo_ref[...], vo_ref[...] = _adamw_math(w[p][...], grads[p], m[p][...], v[p][...])
        outs[4 * n][...] = tot_ref[3:4, HEAD_DIM:2 * HEAD_DIM]

    full = lambda a: pl.BlockSpec(a.shape, lambda i, idx: (0,) * a.ndim)
    out_shape = [_sds(w.shape, F32) for w in ws for _ in range(4)] + [_sds((1, LANES), F32)]
    return _pallas_call(
        body, name="small_update",
        grid_spec=pltpu.PrefetchScalarGridSpec(
            num_scalar_prefetch=1, grid=(1,),
            in_specs=[full(block), full(gathered)] + [full(a) for a in ws + ms + vs],
            out_specs=[full(s) for s in out_shape],
            scratch_shapes=[pltpu.VMEM(block.shape, F32)]),
        out_shape=out_shape,
    )(place_idx, block, gathered, *ws, *ms, *vs)


def kernel(x, norm_mix_g, w_in, lower_bounds, hg_norm_g, conv_w, w_branch_a, w_branch_b, w_out, norm_ffn_g, w_ffn_gate, w_ffn_up, w_ffn_down, norm_final_g, loss_target, m_norm_mix_g, m_w_in, m_lower_bounds, m_hg_norm_g, m_conv_w, m_w_branch_a, m_w_branch_b, m_w_out, m_norm_ffn_g, m_w_ffn_gate, m_w_ffn_up, m_w_ffn_down, m_norm_final_g, v_norm_mix_g, v_w_in, v_lower_bounds, v_hg_norm_g, v_conv_w, v_w_branch_a, v_w_branch_b, v_w_out, v_norm_ffn_g, v_w_ffn_gate, v_w_ffn_up, v_w_ffn_down, v_norm_final_g):
    _, L, D = x.shape
    H = D // 2
    assert lower_bounds.shape == (2, H) and hg_norm_g.shape == (1, HEAD_DIM)
    assert conv_w.shape == (1, 3, LANES) and w_in.shape[2] * N_CHIPS == 11 * H
    x2d, target = x.reshape(L, D), loss_target.reshape(L, D)
    g_final = norm_final_g.reshape(1, D)
    chip = 2 * lax.axis_index("x") + lax.axis_index("y")
    core = lax.axis_index("c")

    tr = lambda w: jnp.transpose(w[0])
    big = [w_in[0], w_branch_a[0], w_branch_b[0], w_out[0], tr(w_ffn_gate), tr(w_ffn_up), w_ffn_down[0]]
    big_m = [m_w_in[0], m_w_branch_a[0], m_w_branch_b[0], m_w_out[0], tr(m_w_ffn_gate), tr(m_w_ffn_up),
             m_w_ffn_down[0]]
    big_v = [v_w_in[0], v_w_branch_a[0], v_w_branch_b[0], v_w_out[0], tr(v_w_ffn_gate), tr(v_w_ffn_up),
             v_w_ffn_down[0]]
    names = ["w_in", "w_branch_a", "w_branch_b", "w_out", "w_ffn_gate", "w_ffn_up", "w_ffn_down"]

    chip_idx = chip.reshape(1).astype(jnp.int32)
    def per_shape(fn, tag, js, *lists):
        groups = {}
        for pos, a in enumerate(lists[0]):
            groups.setdefault(a.shape, []).append(pos)
        results = [None] * len(js)
        for same in groups.values():
            out = fn(tag + names[js[same[0]]], *[[xs[p] for p in same] for xs in lists])
            for q, p in enumerate(same):
                results[p] = out[q]
        return results

    place_t = lambda name, ws: _cast_place_t(name, ws, chip_idx)
    placed = per_shape(place_t, "place_", [0, 1, 2], big[:3]) + list(_cast_place("place_rest", big[3:], chip_idx))
    conv_placed = lax.dynamic_update_slice(jnp.zeros((N_CHIPS, 3, LANES), F32), conv_w, (chip, 0, 0))
    x_i, y_i = lax.axis_index("x"), lax.axis_index("y")
    blocks = lambda *ks: jnp.stack(ks).astype(jnp.int32)
    near = lambda w, j: j < 2
    far = lambda w, j: w == 1 or j == 2
    near_sems, in_flight, _ = _gather_start("gather_start_near", [([placed[0]], set(), near)], chip_idx)
    w_in_buf = in_flight[0][0]
    h, proj = _fwd_proj_first(x2d, norm_mix_g, w_in_buf, blocks(chip), placed[-1])
    sems, in_flight, _ = _gather_start(
        "gather_start_rest", [([w_in_buf, conv_placed], {1}, far), (placed[1:4], set(), None),
                              (placed[4:], set(), None)], h)
    w_in_buf, conv_buf = in_flight[0]
    (w_in_buf,) = _gather_wait("gather_wait_in_near", [w_in_buf], set(), near_sems[0], h, near)
    (w_in_buf,) = _gather_forward("gather_fwd_in_near", [w_in_buf], (0, 1))
    proj = _fwd_proj_more("fwd_proj_near", h, w_in_buf, proj,
                          blocks(2 * (1 - x_i) + y_i, 2 * x_i + (1 - y_i)))
    w_in_buf, conv_all = _gather_wait("gather_wait_in_far", [w_in_buf, conv_buf], {1}, sems[0], proj, far)
    (w_int3,) = _gather_forward("gather_fwd_in_far", [w_in_buf], (2,))
    proj = _fwd_proj_more("fwd_proj_far", h, w_int3, proj, blocks(2 * (1 - x_i) + (1 - y_i)))
    w_int = w_int3.reshape(-1, D)
    conv_full = jnp.transpose(conv_all, (1, 0, 2)).reshape(3, H)
    og, o_pre, s_saved = _hgrn_fwd(proj, lower_bounds, hg_norm_g, H)
    landed = _gather_wait("gather_wait_mix", in_flight[1], set(), sems[1], og)
    fwd_sems, landed, token = _split_start("gather_fwd_mix_start", landed, 9, _forward_pairs)
    cb = _conv_fwd(proj, conv_full, H, token)
    wat3, wbt3, wout3 = _split_wait("gather_fwd_mix_wait", fwd_sems, landed, _forward_pairs, cb)
    wat, wbt, wout = wat3.reshape(D, H), wbt3.reshape(D, H), wout3.reshape(D, D)
    landed = _gather_wait("gather_wait_ffn", in_flight[2], set(), sems[2], cb)
    fwd_sems, landed, token = _split_start("gather_fwd_ffn_start", landed, 9, _forward_pairs)
    sig_a, sig_b, dm_dga, dm_dgb, merged, x1, h2 = _fwd_mix(og, cb, proj, x2d, wat, wbt, wout, norm_ffn_g,
                                                              H, token)
    wgt3, wut3, wd3 = _split_wait("gather_fwd_ffn_wait", fwd_sems, landed, _forward_pairs, h2)
    d_ff = N_CHIPS * wd3.shape[1]
    wgt, wut, wd = wgt3.reshape(d_ff, D), wut3.reshape(d_ff, D), wd3.reshape(d_ff, D)
    ffn_ds_da, ffn_ds_db, ffn_s = _fwd_ffn_up(h2, wgt, wut)
    dx2, dx2b, red_final = _fwd_down_loss(ffn_s, wd, x1, target, g_final)

    c_idx = core.reshape(1).astype(jnp.int32)
    place_idx = jnp.stack([chip, core]).astype(jnp.int32)

    def sibling_start(tag, grads):
        bufs = [lax.empty((N_CHIPS, g.shape[1] // 2, g.shape[2]), F32) for g in grads]
        return _split_start("rs_sibling_start_" + tag, list(grads) + bufs, len(grads), _sibling_pairs)

    def ici_start(tag, js, grads, from_sibling):
        partials = list(_rs_add("rs_add_" + tag, grads, from_sibling, c_idx))
        landings = [lax.empty((3,) + p.shape[1:], BF16) for p in partials]
        return _split_start("rs_ici_start_" + tag, partials + landings, 3 * len(js), _ici_pairs)

    def ici_start_behind(tag, js, started, after):
        n = len(js)
        arrays = _split_wait("rs_sibling_wait_" + tag, started[0], started[1], _sibling_pairs, after)
        return ici_start(tag, js, arrays[:n], arrays[n:])

    def sums(tag, started, after):
        partials, received = [], []
        for group, group_js, start, pairs_fn in started:
            arrays = _split_wait("rs_ici_wait_" + group, start[0], start[1], pairs_fn, after)
            partials += arrays[:len(group_js)]
            received += arrays[len(group_js):]
        return list(_rs_sum("rs_sum_" + tag, partials, received, place_idx))

    def adamw(tag, js, grads):
        return _adamw("adamw_" + tag, grads, *[[src[j] for j in js] for src in (big, big_m, big_v)])

    shards3 = lambda g: g.reshape(N_CHIPS, d_ff // N_CHIPS, D)
    da, db = _bwd_down(dx2b, wd, ffn_ds_da, ffn_ds_db)
    g_wd = shards3(_dw_rows2("dw_ffn_down", ffn_s, dx2b))
    g_wg = shards3(_dw_rows2("dw_ffn_gate", da, h2))
    g_wu = shards3(_dw_rows2("dw_ffn_up", db, h2))
    ffn_sibling = sibling_start("ffn", [g_wg, g_wu, g_wd])
    dx1, dx1b, red_ffn = _bwd_ffn_dh(da, db, wgt, wut, x1, dx2, norm_ffn_g, ffn_sibling[2])
    ffn_arrays = _split_wait("rs_sibling_wait_ffn", ffn_sibling[0], ffn_sibling[1], _sibling_pairs, dx1b)
    ffn_partials = list(_rs_add("rs_add_ffn", ffn_arrays[:3], ffn_arrays[3:], c_idx))
    dya, dyb, dproj, d_o, d_cb = _bwd_mix(dx1b, sig_a, sig_b, dm_dga, dm_dgb, wat, wbt, wout, H, ffn_partials[0])
    (g_wout,) = _dw_whole("dw_out", [(merged, dx1b)], True)
    g_wa, g_wb = _dw_whole("dw_branch", [(og, dya), (cb, dyb)], False)
    mix_grads = [g_wa, g_wb, g_wout]
    both = _split_start("rs_sibling_start_mix",
                        mix_grads + [lax.empty((N_CHIPS, g.shape[1] // 2, g.shape[2]), F32) for g in mix_grads]
                        + ffn_partials + [lax.empty((3,) + p.shape[1:], BF16) for p in ffn_partials],
                        12, _sibling_ici_pairs)
    mix_sibling, ffn_ici = (both[0], both[1][:6], both[2]), (both[0], both[1][6:], both[2])
    dproj, red_hg = _hgrn_bwd(proj, lower_bounds, hg_norm_g, o_pre, d_o, s_saved, H, mix_sibling[2], dproj)
    mix_ici = ici_start_behind("mix", [1, 2, 3], mix_sibling, red_hg)
    dproj, g_conv = _conv_bwd(proj, conv_full, d_cb, H, mix_ici[2], dproj)
    g_win, for_sibling = _dw_in(h, dproj, w_int3.shape[1], c_idx)
    in_sibling = _split_start("rs_sibling_start_in", [for_sibling, lax.empty(for_sibling.shape, BF16)], 1,
                              _sibling_whole_pairs)
    halves = sums("rest", [("mix", [1, 2, 3], mix_ici, _ici_pairs),
                           ("ffn", [4, 5, 6], ffn_ici, _ici_pairs_behind_sibling)], in_sibling[2])
    from_sibling = _split_wait("rs_sibling_wait_in", in_sibling[0], in_sibling[1], _sibling_whole_pairs,
                               halves[0])[1]
    (in_partial,) = _rs_add("rs_add_in", [g_win], [from_sibling], c_idx)
    both = _split_start("rs_ici_start_in", [in_partial, lax.empty((3,) + in_partial.shape[1:], BF16)] + halves,
                        3 + len(halves), _ici_share_pairs)
    in_ici, rest_share = (both[0], both[1][:2], both[2]), (both[0], both[1][2:])
    grad_x, red_mix = _bwd_in(dproj, w_int, x2d, dx1, norm_mix_g, in_ici[2])
    in_half = sums("in", [("in", [0], in_ici, _ici_pairs)], grad_x)
    small_block = _small_pack(red_mix, red_ffn, red_final, red_hg, g_conv)
    tail = _split_start("tail_start", [small_block, lax.empty((8,) + small_block.shape, F32)] + in_half, 8,
                        _tail_pairs)
    rest_grads = _split_wait("rs_share_wait_rest", rest_share[0], rest_share[1], _share_pairs_behind_ici, tail[2])
    big_out = [None] + adamw("rest", [1, 2, 3, 4, 5, 6], rest_grads)
    small_block, small_all, in_grad = _split_wait("tail_wait", tail[0], tail[1], _tail_pairs, big_out[6][0])
    big_out[0] = adamw("in", [0], [in_grad])[0]

    def smalls(mix, lb, hg, cw, ffn, fin):
        return [mix, lb, hg, cw[0], ffn, fin.reshape(1, D)]

    small_out = _small_update(
        small_block, small_all, jnp.stack([chip, 4 * x_i + 2 * y_i + core]).astype(jnp.int32),
        smalls(norm_mix_g, lower_bounds, hg_norm_g, conv_w, norm_ffn_g, norm_final_g),
        smalls(m_norm_mix_g, m_lower_bounds, m_hg_norm_g, m_conv_w, m_norm_ffn_g, m_norm_final_g),
        smalls(v_norm_mix_g, v_lower_bounds, v_hg_norm_g, v_conv_w, v_norm_ffn_g, v_norm_final_g))

    def outputs(i):
        big_i = [big_out[j][i] for j in range(7)]
        mix, lb, hg, cw, ffn, fin = [small_out[4 * p + i] for p in range(6)]
        return [mix, big_i[0][None], lb, hg, cw[None], big_i[1][None], big_i[2][None], big_i[3][None], ffn,
                big_i[4].T[None], big_i[5].T[None], big_i[6][None], fin.reshape(D)]

    outs = [small_out[24][0, 0], grad_x.reshape(1, L, D)]
    for i in range(4):
        outs += outputs(i)
    return tuple(outs)
```
